```python
import math
import jax, jax.numpy as jnp
from jax import lax
import numpy as np

D_MODEL = 2048
BATCH = 8
SEQ = 8192
DEPTH = 1

CHUNK = 64
Q_BLOCK = 128

MLA_HEADS = 8
MLA_Q_LORA = 512
MLA_KV_LORA = 256
MLA_NOPE = 128
MLA_ROPE = 64
MLA_V = 128
ROPE_THETA = 10000.0

FOX_HEADS = 8
FOX_HEAD_DIM = 128

D_FF = 4 * D_MODEL

EPS = 1e-6

OFF_CQ = 0
OFF_CKV = OFF_CQ + MLA_Q_LORA
OFF_KR = OFF_CKV + MLA_KV_LORA
OFF_FQ = OFF_KR + MLA_ROPE
OFF_FK = OFF_FQ + FOX_HEADS * FOX_HEAD_DIM
OFF_FV = OFF_FK + FOX_HEADS * FOX_HEAD_DIM
OFF_FF = OFF_FV + FOX_HEADS * FOX_HEAD_DIM
OFF_G = OFF_FF + FOX_HEADS
D_IN = OFF_G + 2 * D_MODEL

kernel_name = "hybrid_mla_fox_gated_block"


def rmsnorm(x, g):
    xf = x.astype(jnp.float32)
    y = xf * lax.rsqrt(jnp.mean(xf * xf, axis=-1, keepdims=True) + EPS)
    return (y * g.astype(jnp.float32)).astype(x.dtype)


def rope_tables(seq_len):
    pos = jnp.arange(seq_len, dtype=jnp.float32)
    inv = 1.0 / (ROPE_THETA ** (jnp.arange(0, MLA_ROPE, 2, dtype=jnp.float32) / MLA_ROPE))
    ang = pos[:, None] * inv[None, :]
    return jnp.cos(ang), jnp.sin(ang)


def apply_rope(x, cos, sin):
    half = x.shape[-1] // 2
    x1, x2 = x[..., :half], x[..., half:]
    c = cos.astype(x.dtype)
    s = sin.astype(x.dtype)
    return jnp.concatenate([x1 * c - x2 * s, x1 * s + x2 * c], axis=-1)


def block_attention(q, k, v, scale, chunk_causal, cum=None):
    B, H, S, Dq = q.shape
    Dv = v.shape[-1]
    nb = S // Q_BLOCK
    qb = q.reshape(B, H, nb, Q_BLOCK, Dq).transpose(2, 0, 1, 3, 4)
    k_pos = jnp.arange(S)
    xs = (jnp.arange(nb), qb)
    if cum is not None:
        cb = cum.reshape(B, H, nb, Q_BLOCK).transpose(2, 0, 1, 3)
        xs = xs + (cb,)

    def one_block(args):
        i, q_blk = args[0], args[1]
        s = jnp.einsum('bhqd,bhkd->bhqk', q_blk, k,
                       preferred_element_type=jnp.float32) * scale
        q_pos = i * Q_BLOCK + jnp.arange(Q_BLOCK)
        if chunk_causal:
            mask = (k_pos // CHUNK)[None, :] <= (q_pos // CHUNK)[:, None]
        else:
            mask = k_pos[None, :] <= q_pos[:, None]
        if cum is not None:
            c_blk = args[2]
            s = s + c_blk[..., :, None] - cum[:, :, None, :]
        s = jnp.where(mask, s, -jnp.inf)
        p = jax.nn.softmax(s, axis=-1)
        return jnp.einsum('bhqk,bhkd->bhqd', p.astype(v.dtype), v)

    out = lax.map(one_block, xs)
    return out.transpose(1, 2, 0, 3, 4).reshape(B, H, S, Dv)


def mla_branch(proj, q_norm, w_uq, kv_norm, w_ukv, cos, sin):
    B, S, _ = proj.shape
    cq = rmsnorm(proj[..., OFF_CQ:OFF_CKV], q_norm)
    q = (cq @ w_uq).reshape(B, S, MLA_HEADS, MLA_NOPE + MLA_ROPE).transpose(0, 2, 1, 3)
    ckv = rmsnorm(proj[..., OFF_CKV:OFF_KR], kv_norm)
    kv = (ckv @ w_ukv).reshape(B, S, MLA_HEADS, MLA_NOPE + MLA_V).transpose(0, 2, 1, 3)
    k_nope, v = kv[..., :MLA_NOPE], kv[..., MLA_NOPE:]
    k_rope = apply_rope(proj[:, None, :, OFF_KR:OFF_FQ], cos, sin)
    q = jnp.concatenate([q[..., :MLA_NOPE], apply_rope(q[..., MLA_NOPE:], cos, sin)], axis=-1)
    k = jnp.concatenate(
        [k_nope, jnp.broadcast_to(k_rope, (B, MLA_HEADS, S, MLA_ROPE))], axis=-1)
    o = block_attention(q, k, v, 1.0 / math.sqrt(MLA_NOPE + MLA_ROPE), chunk_causal=True)
    return o.transpose(0, 2, 1, 3).reshape(B, S, MLA_HEADS * MLA_V)


def fox_branch(proj, f_bias):
    B, S, _ = proj.shape

    def heads(a):
        return a.reshape(B, S, FOX_HEADS, FOX_HEAD_DIM).transpose(0, 2, 1, 3)

    q = heads(proj[..., OFF_FQ:OFF_FK])
    k = heads(proj[..., OFF_FK:OFF_FV])
    v = heads(proj[..., OFF_FV:OFF_FF])
    logf = jax.nn.log_sigmoid((proj[..., OFF_FF:OFF_G] + f_bias).astype(jnp.float32))
    cum = jnp.cumsum(logf, axis=1).transpose(0, 2, 1)
    o = block_attention(q, k, v, 1.0 / math.sqrt(FOX_HEAD_DIM), chunk_causal=False, cum=cum)
    return o.transpose(0, 2, 1, 3).reshape(B, S, FOX_HEADS * FOX_HEAD_DIM)


def _fwd_setup_inputs(seed: int = 0) -> dict:
    key = jax.random.key(seed)
    ks = jax.random.split(key, 20)
    f32 = jnp.float32

    def w(k, shape, fan_in):
        return jax.random.normal(k, shape, f32) * (fan_in ** -0.5)

    def gain(k, shape):
        return 1.0 + 0.02 * jax.random.normal(k, shape, f32)

    return {
        "x": jax.random.normal(ks[0], (BATCH, SEQ, D_MODEL), f32),
        "attn_norm": gain(ks[1], (DEPTH, D_MODEL)),
        "w_in": w(ks[2], (DEPTH, D_MODEL, D_IN), D_MODEL),
        "fox_f_bias": jax.random.uniform(ks[3], (DEPTH, FOX_HEADS), f32, 1.0, 6.0),
        "q_norm": gain(ks[4], (DEPTH, MLA_Q_LORA)),
        "w_uq": w(ks[5], (DEPTH, MLA_Q_LORA, MLA_HEADS * (MLA_NOPE + MLA_ROPE)), MLA_Q_LORA),
        "kv_norm": gain(ks[6], (DEPTH, MLA_KV_LORA)),
        "w_ukv": w(ks[7], (DEPTH, MLA_KV_LORA, MLA_HEADS * (MLA_NOPE + MLA_V)), MLA_KV_LORA),
        "w_mla_branch": w(ks[8], (DEPTH, MLA_HEADS * MLA_V, D_MODEL), MLA_HEADS * MLA_V),
        "w_fox_branch": w(ks[9], (DEPTH, FOX_HEADS * FOX_HEAD_DIM, D_MODEL), FOX_HEADS * FOX_HEAD_DIM),
        "w_out": w(ks[10], (DEPTH, D_MODEL, D_MODEL), D_MODEL),
        "mlp_norm": gain(ks[11], (DEPTH, D_MODEL)),
        "w_up": w(ks[12], (DEPTH, D_MODEL, D_FF), D_MODEL),
        "w_down": w(ks[13], (DEPTH, D_FF, D_MODEL), D_FF),
        "final_norm": gain(ks[14], (D_MODEL,)),
    }


def _fwd_reference(x, attn_norm, w_in, fox_f_bias, q_norm, w_uq, kv_norm, w_ukv,
              w_mla_branch, w_fox_branch, w_out, mlp_norm, w_up, w_down, final_norm):
    S = x.shape[1]
    cos, sin = rope_tables(S)
    h = x
    for l in range(DEPTH):
        xn = rmsnorm(h, attn_norm[l])
        proj = xn @ w_in[l]
        y_mla = mla_branch(proj, q_norm[l], w_uq[l], kv_norm[l], w_ukv[l], cos, sin) @ w_mla_branch[l]
        y_fox = fox_branch(proj, fox_f_bias[l]) @ w_fox_branch[l]
        gates = jax.nn.sigmoid(proj[..., OFF_G:])
        g_mla, g_fox = gates[..., :D_MODEL], gates[..., D_MODEL:]
        h = h + (g_mla * y_mla + g_fox * y_fox) @ w_out[l]
        hn = rmsnorm(h, mlp_norm[l])
        u = jnp.square(jax.nn.relu(hn @ w_up[l]))
        h = h + u @ w_down[l]
    return rmsnorm(h, final_norm)


import jax as _jax
import jax.numpy as _jnp

TWIN_FORMAT = 'train_step'
FWD_PARAMS = ['x', 'attn_norm', 'w_in', 'fox_f_bias', 'q_norm', 'w_uq', 'kv_norm', 'w_ukv', 'w_mla_branch', 'w_fox_branch', 'w_out', 'mlp_norm', 'w_up', 'w_down', 'final_norm']
TWIN_WEIGHTS = ['attn_norm', 'w_in', 'fox_f_bias', 'q_norm', 'w_uq', 'kv_norm', 'w_ukv', 'w_mla_branch', 'w_fox_branch', 'w_out', 'mlp_norm', 'w_up', 'w_down', 'final_norm']
TWIN_DIFF_INPUT = 'x'
TWIN_INPUTS = ['x', 'attn_norm', 'w_in', 'fox_f_bias', 'q_norm', 'w_uq', 'kv_norm', 'w_ukv', 'w_mla_branch', 'w_fox_branch', 'w_out', 'mlp_norm', 'w_up', 'w_down', 'final_norm', 'loss_target', 'm_attn_norm', 'm_w_in', 'm_fox_f_bias', 'm_q_norm', 'm_w_uq', 'm_kv_norm', 'm_w_ukv', 'm_w_mla_branch', 'm_w_fox_branch', 'm_w_out', 'm_mlp_norm', 'm_w_up', 'm_w_down', 'm_final_norm', 'v_attn_norm', 'v_w_in', 'v_fox_f_bias', 'v_q_norm', 'v_w_uq', 'v_kv_norm', 'v_w_ukv', 'v_w_mla_branch', 'v_w_fox_branch', 'v_w_out', 'v_mlp_norm', 'v_w_up', 'v_w_down', 'v_final_norm']
TWIN_OUTPUTS = ['loss', 'grad_x', 'grad_attn_norm', 'grad_w_in', 'grad_fox_f_bias', 'grad_q_norm', 'grad_w_uq', 'grad_kv_norm', 'grad_w_ukv', 'grad_w_mla_branch', 'grad_w_fox_branch', 'grad_w_out', 'grad_mlp_norm', 'grad_w_up', 'grad_w_down', 'grad_final_norm', 'delta_attn_norm', 'delta_w_in', 'delta_fox_f_bias', 'delta_q_norm', 'delta_w_uq', 'delta_kv_norm', 'delta_w_ukv', 'delta_w_mla_branch', 'delta_w_fox_branch', 'delta_w_out', 'delta_mlp_norm', 'delta_w_up', 'delta_w_down', 'delta_final_norm', 'new_m_attn_norm', 'new_m_w_in', 'new_m_fox_f_bias', 'new_m_q_norm', 'new_m_w_uq', 'new_m_kv_norm', 'new_m_w_ukv', 'new_m_w_mla_branch', 'new_m_w_fox_branch', 'new_m_w_out', 'new_m_mlp_norm', 'new_m_w_up', 'new_m_w_down', 'new_m_final_norm', 'new_v_attn_norm', 'new_v_w_in', 'new_v_fox_f_bias', 'new_v_q_norm', 'new_v_w_uq', 'new_v_kv_norm', 'new_v_w_ukv', 'new_v_w_mla_branch', 'new_v_w_fox_branch', 'new_v_w_out', 'new_v_mlp_norm', 'new_v_w_up', 'new_v_w_down', 'new_v_final_norm']
TWIN_LEAF_KINDS = {'loss': 'loss', 'grad_x': 'grad_x', 'grad_attn_norm': 'grad_w', 'grad_w_in': 'grad_w', 'grad_fox_f_bias': 'grad_w', 'grad_q_norm': 'grad_w', 'grad_w_uq': 'grad_w', 'grad_kv_norm': 'grad_w', 'grad_w_ukv': 'grad_w', 'grad_w_mla_branch': 'grad_w', 'grad_w_fox_branch': 'grad_w', 'grad_w_out': 'grad_w', 'grad_mlp_norm': 'grad_w', 'grad_w_up': 'grad_w', 'grad_w_down': 'grad_w', 'grad_final_norm': 'grad_w', 'delta_attn_norm': 'delta_w', 'delta_w_in': 'delta_w', 'delta_fox_f_bias': 'delta_w', 'delta_q_norm': 'delta_w', 'delta_w_uq': 'delta_w', 'delta_kv_norm': 'delta_w', 'delta_w_ukv': 'delta_w', 'delta_w_mla_branch': 'delta_w', 'delta_w_fox_branch': 'delta_w', 'delta_w_out': 'delta_w', 'delta_mlp_norm': 'delta_w', 'delta_w_up': 'delta_w', 'delta_w_down': 'delta_w', 'delta_final_norm': 'delta_w', 'new_m_attn_norm': 'new_m', 'new_m_w_in': 'new_m', 'new_m_fox_f_bias': 'new_m', 'new_m_q_norm': 'new_m', 'new_m_w_uq': 'new_m', 'new_m_kv_norm': 'new_m', 'new_m_w_ukv': 'new_m', 'new_m_w_mla_branch': 'new_m', 'new_m_w_fox_branch': 'new_m', 'new_m_w_out': 'new_m', 'new_m_mlp_norm': 'new_m', 'new_m_w_up': 'new_m', 'new_m_w_down': 'new_m', 'new_m_final_norm': 'new_m', 'new_v_attn_norm': 'new_v', 'new_v_w_in': 'new_v', 'new_v_fox_f_bias': 'new_v', 'new_v_q_norm': 'new_v', 'new_v_w_uq': 'new_v', 'new_v_kv_norm': 'new_v', 'new_v_w_ukv': 'new_v', 'new_v_w_mla_branch': 'new_v', 'new_v_w_fox_branch': 'new_v', 'new_v_w_out': 'new_v', 'new_v_mlp_norm': 'new_v', 'new_v_w_up': 'new_v', 'new_v_w_down': 'new_v', 'new_v_final_norm': 'new_v'}


def _forward(args):
    return _fwd_reference(*[args[k] for k in FWD_PARAMS])


def _output_shape():
    def fwd():
        inp = _fwd_setup_inputs(0)
        return _fwd_reference(*[inp[k] for k in FWD_PARAMS])
    out = _jax.eval_shape(fwd)
    return out.shape, out.dtype

N_MICROBATCH = 1
ADAM_LR = 0.001
ADAM_B1 = 0.9
ADAM_B2 = 0.999
ADAM_EPS = 1e-08
ADAM_WD = 0.01
ADAM_STEP = 10
PER_EXAMPLE_BATCH_AXIS = {'x': 0, 'loss_target': 0}
SHARED_INPUTS = []
_WEIGHT_DTYPES = {'attn_norm': _jnp.float32, 'w_in': _jnp.float32, 'fox_f_bias': _jnp.float32, 'q_norm': _jnp.float32, 'w_uq': _jnp.float32, 'kv_norm': _jnp.float32, 'w_ukv': _jnp.float32, 'w_mla_branch': _jnp.float32, 'w_fox_branch': _jnp.float32, 'w_out': _jnp.float32, 'mlp_norm': _jnp.float32, 'w_up': _jnp.float32, 'w_down': _jnp.float32, 'final_norm': _jnp.float32}
MOMENT_SCALE = {'attn_norm': 4.265041e-02, 'w_in': 2.128120e-02, 'fox_f_bias': 1.610596e-01, 'q_norm': 2.167137e-02, 'w_uq': 1.237711e-02, 'kv_norm': 4.272762e-02, 'w_ukv': 1.433683e-02, 'w_mla_branch': 1.111386e-02, 'w_fox_branch': 2.460904e-02, 'w_out': 2.700102e-02, 'mlp_norm': 1.126747e-01, 'w_up': 5.587554e-02, 'w_down': 1.199873e-01, 'final_norm': 3.226518e+01}


def _to_microbatches(a, axis):
    t = _jnp.moveaxis(a, axis, 0)
    t = t.reshape((N_MICROBATCH, t.shape[0] // N_MICROBATCH) + t.shape[1:])
    return _jnp.moveaxis(t, 1, axis + 1)


def setup_inputs(seed: int = 0) -> dict:
    inp = _fwd_setup_inputs(seed)
    key = _jax.random.fold_in(_jax.random.key(seed), 7919)
    shape, _ = _output_shape()
    out = dict(inp)
    out["loss_target"] = _jax.random.normal(_jax.random.fold_in(key, 0), shape, _jnp.float32)
    for i, name in enumerate(TWIN_WEIGHTS):
        w = inp[name].astype(_jnp.float32)
        if MOMENT_SCALE is None:
            s = _jnp.sqrt(_jnp.mean(_jnp.square(w)) + 1e-30)
        else:
            s = MOMENT_SCALE[name]
        km, kv = _jax.random.split(_jax.random.fold_in(key, i + 1))
        out[name] = w
        out["m_" + name] = s * _jax.random.normal(km, w.shape, _jnp.float32)
        out["v_" + name] = (s * s) * _jax.random.uniform(kv, w.shape, _jnp.float32, 0.5, 1.5)
    if N_MICROBATCH > 1:
        for name, axis in PER_EXAMPLE_BATCH_AXIS.items():
            out[name] = _to_microbatches(out[name], axis)
    return {'x': out['x'], 'attn_norm': out['attn_norm'], 'w_in': out['w_in'], 'fox_f_bias': out['fox_f_bias'], 'q_norm': out['q_norm'], 'w_uq': out['w_uq'], 'kv_norm': out['kv_norm'], 'w_ukv': out['w_ukv'], 'w_mla_branch': out['w_mla_branch'], 'w_fox_branch': out['w_fox_branch'], 'w_out': out['w_out'], 'mlp_norm': out['mlp_norm'], 'w_up': out['w_up'], 'w_down': out['w_down'], 'final_norm': out['final_norm'], 'loss_target': out['loss_target'], 'm_attn_norm': out['m_attn_norm'], 'm_w_in': out['m_w_in'], 'm_fox_f_bias': out['m_fox_f_bias'], 'm_q_norm': out['m_q_norm'], 'm_w_uq': out['m_w_uq'], 'm_kv_norm': out['m_kv_norm'], 'm_w_ukv': out['m_w_ukv'], 'm_w_mla_branch': out['m_w_mla_branch'], 'm_w_fox_branch': out['m_w_fox_branch'], 'm_w_out': out['m_w_out'], 'm_mlp_norm': out['m_mlp_norm'], 'm_w_up': out['m_w_up'], 'm_w_down': out['m_w_down'], 'm_final_norm': out['m_final_norm'], 'v_attn_norm': out['v_attn_norm'], 'v_w_in': out['v_w_in'], 'v_fox_f_bias': out['v_fox_f_bias'], 'v_q_norm': out['v_q_norm'], 'v_w_uq': out['v_w_uq'], 'v_kv_norm': out['v_kv_norm'], 'v_w_ukv': out['v_w_ukv'], 'v_w_mla_branch': out['v_w_mla_branch'], 'v_w_fox_branch': out['v_w_fox_branch'], 'v_w_out': out['v_w_out'], 'v_mlp_norm': out['v_mlp_norm'], 'v_w_up': out['v_w_up'], 'v_w_down': out['v_w_down'], 'v_final_norm': out['v_final_norm']}


def _loss(weights, diff, rest, loss_target):
    with _jax.named_scope("forward"):
        args = {**rest, TWIN_DIFF_INPUT: diff, **{k: w.astype(_WEIGHT_DTYPES[k]) for k, w in weights.items()}}
        y = _forward(args)
    with _jax.named_scope("loss_head"):
        err = _jnp.square(y.astype(_jnp.float32) - loss_target)
        return 0.5 * _jnp.sum(_jnp.mean(err, axis=-1)) if err.ndim else 0.5 * err


def _adamw(w, g, m, v):
    m = ADAM_B1 * m + (1.0 - ADAM_B1) * g
    v = ADAM_B2 * v + (1.0 - ADAM_B2) * _jnp.square(g)
    m_hat = m / (1.0 - ADAM_B1 ** ADAM_STEP)
    v_hat = v / (1.0 - ADAM_B2 ** ADAM_STEP)
    delta = -ADAM_LR * (m_hat / (_jnp.sqrt(v_hat) + ADAM_EPS) + ADAM_WD * w)
    return delta, m, v


def reference(x, attn_norm, w_in, fox_f_bias, q_norm, w_uq, kv_norm, w_ukv, w_mla_branch, w_fox_branch, w_out, mlp_norm, w_up, w_down, final_norm, loss_target, m_attn_norm, m_w_in, m_fox_f_bias, m_q_norm, m_w_uq, m_kv_norm, m_w_ukv, m_w_mla_branch, m_w_fox_branch, m_w_out, m_mlp_norm, m_w_up, m_w_down, m_final_norm, v_attn_norm, v_w_in, v_fox_f_bias, v_q_norm, v_w_uq, v_kv_norm, v_w_ukv, v_w_mla_branch, v_w_fox_branch, v_w_out, v_mlp_norm, v_w_up, v_w_down, v_final_norm):
    given = dict(x=x, attn_norm=attn_norm, w_in=w_in, fox_f_bias=fox_f_bias, q_norm=q_norm, w_uq=w_uq, kv_norm=kv_norm, w_ukv=w_ukv, w_mla_branch=w_mla_branch, w_fox_branch=w_fox_branch, w_out=w_out, mlp_norm=mlp_norm, w_up=w_up, w_down=w_down, final_norm=final_norm, loss_target=loss_target, m_attn_norm=m_attn_norm, m_w_in=m_w_in, m_fox_f_bias=m_fox_f_bias, m_q_norm=m_q_norm, m_w_uq=m_w_uq, m_kv_norm=m_kv_norm, m_w_ukv=m_w_ukv, m_w_mla_branch=m_w_mla_branch, m_w_fox_branch=m_w_fox_branch, m_w_out=m_w_out, m_mlp_norm=m_mlp_norm, m_w_up=m_w_up, m_w_down=m_w_down, m_final_norm=m_final_norm, v_attn_norm=v_attn_norm, v_w_in=v_w_in, v_fox_f_bias=v_fox_f_bias, v_q_norm=v_q_norm, v_w_uq=v_w_uq, v_kv_norm=v_kv_norm, v_w_ukv=v_w_ukv, v_w_mla_branch=v_w_mla_branch, v_w_fox_branch=v_w_fox_branch, v_w_out=v_w_out, v_mlp_norm=v_mlp_norm, v_w_up=v_w_up, v_w_down=v_w_down, v_final_norm=v_final_norm)
    weights = {n: given[n] for n in TWIN_WEIGHTS}
    shared = {n: given[n] for n in SHARED_INPUTS}
    per_example = {n: given[n] for n in ['x']}
    grad_fn = _jax.value_and_grad(_loss, argnums=(0, 1))

    def one_microbatch(ex, loss_target):
        ex = dict(ex)
        diff = ex.pop(TWIN_DIFF_INPUT)
        return grad_fn(weights, diff, {**shared, **ex}, loss_target)

    if N_MICROBATCH == 1:
        loss, (grad_w, grad_x) = one_microbatch(per_example, given["loss_target"])
    else:
        def body(carry, xs):
            loss_sum, grad_sum = carry
            l_k, (gw_k, gx_k) = one_microbatch(xs[0], xs[1])
            with _jax.named_scope("update"):
                return (loss_sum + l_k, _jax.tree.map(_jnp.add, grad_sum, gw_k)), gx_k

        init = (_jnp.zeros((), _jnp.float32), _jax.tree.map(_jnp.zeros_like, weights))
        (loss, grad_w), grad_x = _jax.lax.scan(body, init, (per_example, given["loss_target"]))
    with _jax.named_scope("update"):
        delta_w, new_m, new_v = {}, {}, {}
        for n in TWIN_WEIGHTS:
            delta_w[n], new_m[n], new_v[n] = _adamw(weights[n], grad_w[n], given["m_" + n], given["v_" + n])
    return (loss, grad_x, *[grad_w[n] for n in TWIN_WEIGHTS], *[delta_w[n] for n in TWIN_WEIGHTS],
            *[new_m[n] for n in TWIN_WEIGHTS], *[new_v[n] for n in TWIN_WEIGHTS])
```

```python
import functools
import math

import jax
import jax.numpy as jnp
from jax import lax
from jax.experimental import pallas as pl
from jax.experimental.pallas import tpu as pltpu

CHUNK = 64
MLA_HEADS = 8
MLA_Q_LORA = 512
MLA_KV_LORA = 256
MLA_NOPE = 128
MLA_ROPE = 64
MLA_V = 128
ROPE_THETA = 10000.0
FOX_HEADS = 8
FOX_HEAD_DIM = 128
EPS = 1e-6

ADAM_LR = 0.001
ADAM_B1 = 0.9
ADAM_B2 = 0.999
ADAM_EPS = 1e-08
ADAM_WD = 0.01
ADAM_STEP = 10

LANE = 128
QPAD = 2 * LANE
N_CHIPS = 4
N_DEV = 8
VMEM_LIMIT = 48 * 1024 * 1024
ATT_T = 512
ROW_T = 256
PACK_ROWS = 256

BF16 = jnp.bfloat16
F32 = jnp.float32
MESH = pl.DeviceIdType.MESH


def _tile(dim, pref, align=LANE):
    if dim <= pref:
        return dim
    t = (pref // align) * align
    while t >= align:
        if dim % t == 0:
            return t
        t -= align
    return dim


def _params(sem=None):
    return pltpu.CompilerParams(dimension_semantics=sem, vmem_limit_bytes=VMEM_LIMIT)


def _matmul(a, b, mode, out_dtypes, name, *, tm=512, tn=512, tk=2048, extras=(), epilogue=None):
    if mode == "nn":
        (M, K), (K2, N) = a.shape, b.shape
    elif mode == "nt":
        (M, K), (N, K2) = a.shape, b.shape
    else:
        (K, M), (K2, N) = a.shape, b.shape
    assert K == K2, (name, a.shape, b.shape)
    tm, tn, tk = _tile(M, tm), _tile(N, tn), _tile(K, tk)
    nk = K // tk
    n_out = len(out_dtypes)
    n_ex = len(extras)

    def body(*refs):
        a_ref, b_ref = refs[0], refs[1]
        ex_refs = refs[2:2 + n_ex]
        o_refs = refs[2 + n_ex:2 + n_ex + n_out]
        acc_ref = refs[2 + n_ex + n_out]
        k = pl.program_id(2)
        if mode == "nn":
            dims = (((1,), (0,)), ((), ()))
        elif mode == "nt":
            dims = (((1,), (1,)), ((), ()))
        else:
            dims = (((0,), (0,)), ((), ()))
        part = lax.dot_general(a_ref[...], b_ref[...], dims, preferred_element_type=F32)

        @pl.when(k == 0)
        def _():
            acc_ref[...] = part

        @pl.when(k > 0)
        def _():
            acc_ref[...] += part

        @pl.when(k == nk - 1)
        def _():
            acc = acc_ref[...]
            if epilogue is None:
                outs = (acc,)
            else:
                outs = epilogue(acc, *[r[...] for r in ex_refs])
            for o_ref, o in zip(o_refs, outs):
                o_ref[...] = o.astype(o_ref.dtype)

    if mode == "nn":
        a_spec = pl.BlockSpec((tm, tk), lambda i, j, k: (i, k))
        b_spec = pl.BlockSpec((tk, tn), lambda i, j, k: (k, j))
    elif mode == "nt":
        a_spec = pl.BlockSpec((tm, tk), lambda i, j, k: (i, k))
        b_spec = pl.BlockSpec((tn, tk), lambda i, j, k: (j, k))
    else:
        a_spec = pl.BlockSpec((tk, tm), lambda i, j, k: (k, i))
        b_spec = pl.BlockSpec((tk, tn), lambda i, j, k: (k, j))
    mn_spec = pl.BlockSpec((tm, tn), lambda i, j, k: (i, j))
    outs = pl.pallas_call(
        body,
        name=name,
        grid=(M // tm, N // tn, nk),
        in_specs=[a_spec, b_spec] + [mn_spec] * n_ex,
        out_specs=[mn_spec] * n_out,
        out_shape=[jax.ShapeDtypeStruct((M, N), dt) for dt in out_dtypes],
        scratch_shapes=[pltpu.VMEM((tm, tn), F32)],
        compiler_params=_params(("parallel", "parallel", "arbitrary")),
    )(a, b, *extras)
    return outs[0] if n_out == 1 else outs


def _mm_tn(a, b, name):
    return _matmul(a, b, "tn", [F32], name, tm=1024, tn=1024, tk=512)


def _row_spec(ts, width, col=0):
    return pl.BlockSpec((ts, width), lambda i: (i, col))


def _full_spec(shape):
    return pl.BlockSpec(shape, lambda i: tuple(0 for _ in shape))


def _rms(x):
    return lax.rsqrt(jnp.mean(x * x, axis=-1, keepdims=True) + EPS)


def _rms_bwd(x, dy, g):
    r = _rms(x)
    xh = x * r
    gy = dy * g
    dx = r * (gy - xh * jnp.mean(xh * gy, axis=-1, keepdims=True))
    return dx, dy * xh


def _norm_fwd(x, g, name):
    S, D = x.shape
    ts = _tile(S, ROW_T, 8)

    def body(x_ref, g_ref, o_ref):
        xv = x_ref[...]
        o_ref[...] = ((xv * _rms(xv)) * g_ref[...]).astype(BF16)

    return pl.pallas_call(
        body, name=name, grid=(S // ts,),
        in_specs=[_row_spec(ts, D), _full_spec((1, D))],
        out_specs=_row_spec(ts, D),
        out_shape=jax.ShapeDtypeStruct((S, D), BF16),
        compiler_params=_params(("parallel",)),
    )(x, g)


def _norm_bwd(x, dy, g, dres, name):
    S, D = x.shape
    ts = _tile(S, ROW_T, 8)

    def body(x_ref, dy_ref, g_ref, dres_ref, dx_ref, dg_ref):
        dx, dg_rows = _rms_bwd(x_ref[...], dy_ref[...], g_ref[...])
        dx_ref[...] = dres_ref[...] + dx

        @pl.when(pl.program_id(0) == 0)
        def _():
            dg_ref[...] = jnp.zeros_like(dg_ref)

        dg_ref[...] += jnp.sum(dg_rows, axis=0, keepdims=True)

    return pl.pallas_call(
        body, name=name, grid=(S // ts,),
        in_specs=[_row_spec(ts, D), _row_spec(ts, D), _full_spec((1, D)), _row_spec(ts, D)],
        out_specs=[_row_spec(ts, D), _full_spec((1, D))],
        out_shape=[jax.ShapeDtypeStruct((S, D), F32), jax.ShapeDtypeStruct((1, D), F32)],
        compiler_params=_params(("arbitrary",)),
    )(x, dy, g, dres)


def _rope(x, c, sa, sb, sign):
    w = x.shape[-1]
    half = MLA_ROPE // 2
    fwd = pltpu.roll(x, w - half, 1)
    back = pltpu.roll(x, half, 1)
    if sign < 0:
        return x * c - fwd * sa - back * sb
    return x * c + fwd * sa + back * sb


def _split3(x):
    hi = x.astype(BF16)
    r1 = x - hi.astype(F32)
    mid = r1.astype(BF16)
    lo = (r1 - mid.astype(F32)).astype(BF16)
    return hi, mid, lo


def _prep_fwd(small, q_norm, kv_norm, bias_pad, kc, ksa, ksb, n_heads, name):
    S, W = small.shape
    QL, KVL = q_norm.shape[1], kv_norm.shape[1]
    assert W == QL + KVL + 2 * LANE
    ts = _tile(S, ROW_T, 8)
    tri = (lax.broadcasted_iota(jnp.int32, (ts, ts), 0) >= lax.broadcasted_iota(jnp.int32, (ts, ts), 1)).astype(BF16)

    def body(s_ref, qn_ref, kvn_ref, b_ref, kc_ref, ksa_ref, ksb_ref, tri_ref,
             cqn_ref, ckvn_ref, kr_ref, cum_ref, carry_ref):
        cq = s_ref[:, 0:QL]
        cqn_ref[...] = ((cq * _rms(cq)) * qn_ref[...]).astype(BF16)
        ckv = s_ref[:, QL:QL + KVL]
        ckvn_ref[...] = ((ckv * _rms(ckv)) * kvn_ref[...]).astype(BF16)
        kr = s_ref[:, QL + KVL:QL + KVL + LANE]
        kr_ref[...] = _rope(kr, kc_ref[...], ksa_ref[...], ksb_ref[...], 1).astype(BF16)
        z = s_ref[:, QL + KVL + LANE:W] + b_ref[...]
        logf = jnp.minimum(z, 0.0) - jnp.log1p(jnp.exp(-jnp.abs(z)))
        lane = lax.broadcasted_iota(jnp.int32, logf.shape, 1)
        logf = jnp.where(lane < n_heads, logf, 0.0)

        @pl.when(pl.program_id(0) == 0)
        def _():
            carry_ref[...] = jnp.zeros_like(carry_ref)

        t = tri_ref[...]
        cum = carry_ref[...]
        for part in _split3(logf):
            cum = cum + jnp.dot(t, part, preferred_element_type=F32)
        cum_ref[...] = cum
        carry_ref[...] = cum[ts - 1:ts, :]

    return pl.pallas_call(
        body, name=name, grid=(S // ts,),
        in_specs=[_row_spec(ts, W), _full_spec((1, QL)), _full_spec((1, KVL)), _full_spec((1, LANE)),
                  _row_spec(ts, LANE), _row_spec(ts, LANE), _row_spec(ts, LANE), _full_spec((ts, ts))],
        out_specs=[_row_spec(ts, QL), _row_spec(ts, KVL), _row_spec(ts, LANE), _row_spec(ts, LANE)],
        out_shape=[jax.ShapeDtypeStruct((S, QL), BF16), jax.ShapeDtypeStruct((S, KVL), BF16),
                   jax.ShapeDtypeStruct((S, LANE), BF16), jax.ShapeDtypeStruct((S, LANE), F32)],
        scratch_shapes=[pltpu.VMEM((1, LANE), F32)],
        compiler_params=_params(("arbitrary",)),
    )(small, q_norm, kv_norm, bias_pad, kc, ksa, ksb, tri)


def _prep_bwd(small, dcqn, dckvn, dkr_heads, dlogf, q_norm, kv_norm, bias_pad, kc, ksa, ksb, n_heads, name):
    S, W = small.shape
    QL, KVL = q_norm.shape[1], kv_norm.shape[1]
    ts = _tile(S, ROW_T, 8)

    def body(s_ref, dcq_ref, dckv_ref, dkr_ref, dlf_ref, qn_ref, kvn_ref, b_ref, kc_ref, ksa_ref, ksb_ref,
             ds_ref, gq_ref, gkv_ref, gb_ref):
        dcq, gq_rows = _rms_bwd(s_ref[:, 0:QL], dcq_ref[...], qn_ref[...])
        ds_ref[:, 0:QL] = dcq.astype(BF16)
        dckv, gkv_rows = _rms_bwd(s_ref[:, QL:QL + KVL], dckv_ref[...], kvn_ref[...])
        ds_ref[:, QL:QL + KVL] = dckv.astype(BF16)
        dkr = dkr_ref[:, 0:LANE]
        for h in range(1, n_heads):
            dkr = dkr + dkr_ref[:, h * LANE:(h + 1) * LANE]
        ds_ref[:, QL + KVL:QL + KVL + LANE] = _rope(dkr, kc_ref[...], ksa_ref[...], ksb_ref[...], -1).astype(BF16)
        z = s_ref[:, QL + KVL + LANE:W] + b_ref[...]
        dff = dlf_ref[...] * (1.0 / (1.0 + jnp.exp(z)))
        ds_ref[:, QL + KVL + LANE:W] = dff.astype(BF16)

        @pl.when(pl.program_id(0) == 0)
        def _():
            gq_ref[...] = jnp.zeros_like(gq_ref)
            gkv_ref[...] = jnp.zeros_like(gkv_ref)
            gb_ref[...] = jnp.zeros_like(gb_ref)

        gq_ref[...] += jnp.sum(gq_rows, axis=0, keepdims=True)
        gkv_ref[...] += jnp.sum(gkv_rows, axis=0, keepdims=True)
        gb_ref[...] += jnp.sum(dff, axis=0, keepdims=True)

    return pl.pallas_call(
        body, name=name, grid=(S // ts,),
        in_specs=[_row_spec(ts, W), _row_spec(ts, QL), _row_spec(ts, KVL), _row_spec(ts, n_heads * LANE),
                  _row_spec(ts, LANE), _full_spec((1, QL)), _full_spec((1, KVL)), _full_spec((1, LANE)),
                  _row_spec(ts, LANE), _row_spec(ts, LANE), _row_spec(ts, LANE)],
        out_specs=[_row_spec(ts, W), _full_spec((1, QL)), _full_spec((1, KVL)), _full_spec((1, LANE))],
        out_shape=[jax.ShapeDtypeStruct((S, W), BF16), jax.ShapeDtypeStruct((1, QL), F32),
                   jax.ShapeDtypeStruct((1, KVL), F32), jax.ShapeDtypeStruct((1, LANE), F32)],
        compiler_params=_params(("arbitrary",)),
    )(small, dcqn, dckvn, dkr_heads, dlogf, q_norm, kv_norm, bias_pad, kc, ksa, ksb)


def _rope_heads(x, c, sa, sb, sign, name):
    S, W = x.shape
    nh = W // QPAD
    ts = _tile(S, ROW_T, 8)

    def body(x_ref, c_ref, sa_ref, sb_ref, o_ref):
        o_ref[...] = _rope(x_ref[...], c_ref[...], sa_ref[...], sb_ref[...], sign).astype(BF16)

    tab = pl.BlockSpec((ts, QPAD), lambda i, h: (i, 0))
    blk = pl.BlockSpec((ts, QPAD), lambda i, h: (i, h))
    return pl.pallas_call(
        body, name=name, grid=(S // ts, nh),
        in_specs=[blk, tab, tab, tab], out_specs=blk,
        out_shape=jax.ShapeDtypeStruct((S, W), BF16),
        compiler_params=_params(("parallel", "parallel")),
    )(x, c, sa, sb)


def _sigmoid(z):
    return 1.0 / (1.0 + jnp.exp(-z))


def _gate_fwd(gpre, y_mla, y_fox, name):
    S, D = y_mla.shape
    ts = _tile(S, ROW_T, 8)

    def body(ga_ref, gb_ref, ya_ref, yb_ref, o_ref):
        o_ref[...] = (_sigmoid(ga_ref[...]) * ya_ref[...] + _sigmoid(gb_ref[...]) * yb_ref[...]).astype(BF16)

    return pl.pallas_call(
        body, name=name, grid=(S // ts,),
        in_specs=[_row_spec(ts, D, 0), _row_spec(ts, D, 1), _row_spec(ts, D), _row_spec(ts, D)],
        out_specs=_row_spec(ts, D),
        out_shape=jax.ShapeDtypeStruct((S, D), BF16),
        compiler_params=_params(("parallel",)),
    )(gpre, gpre, y_mla, y_fox)


def _gate_bwd(dmerged, gpre, y_mla, y_fox, name):
    S, D = y_mla.shape
    ts = _tile(S, ROW_T, 8)

    def body(dm_ref, ga_ref, gb_ref, ya_ref, yb_ref, dya_ref, dyb_ref, dga_ref, dgb_ref):
        dm = dm_ref[...]
        ga = _sigmoid(ga_ref[...])
        gb = _sigmoid(gb_ref[...])
        dya_ref[...] = (dm * ga).astype(BF16)
        dyb_ref[...] = (dm * gb).astype(BF16)
        dga_ref[...] = (dm * ya_ref[...] * (ga * (1.0 - ga))).astype(BF16)
        dgb_ref[...] = (dm * yb_ref[...] * (gb * (1.0 - gb))).astype(BF16)

    return pl.pallas_call(
        body, name=name, grid=(S // ts,),
        in_specs=[_row_spec(ts, D), _row_spec(ts, D, 0), _row_spec(ts, D, 1), _row_spec(ts, D), _row_spec(ts, D)],
        out_specs=[_row_spec(ts, D)] * 4,
        out_shape=[jax.ShapeDtypeStruct((S, D), BF16)] * 4,
        compiler_params=_params(("parallel",)),
    )(dmerged, gpre, gpre, y_mla, y_fox)


def _final(h, g, target, name):
    S, D = h.shape
    ts = _tile(S, ROW_T, 8)

    def body(h_ref, g_ref, t_ref, dh_ref, dg_ref, loss_ref):
        hv = h_ref[...]
        gv = g_ref[...]
        err = (hv * _rms(hv)) * gv - t_ref[...]
        dh, dg_rows = _rms_bwd(hv, err / D, gv)
        dh_ref[...] = dh

        @pl.when(pl.program_id(0) == 0)
        def _():
            dg_ref[...] = jnp.zeros_like(dg_ref)
            loss_ref[...] = jnp.zeros_like(loss_ref)

        dg_ref[...] += jnp.sum(dg_rows, axis=0, keepdims=True)
        row_loss = jnp.mean(err * err, axis=-1, keepdims=True)
        loss_ref[...] += 0.5 * jnp.sum(row_loss, axis=0, keepdims=True)

    return pl.pallas_call(
        body, name=name, grid=(S // ts,),
        in_specs=[_row_spec(ts, D), _full_spec((1, D)), _row_spec(ts, D)],
        out_specs=[_row_spec(ts, D), _full_spec((1, D)), _full_spec((1, LANE))],
        out_shape=[jax.ShapeDtypeStruct((S, D), F32), jax.ShapeDtypeStruct((1, D), F32),
                   jax.ShapeDtypeStruct((1, LANE), F32)],
        compiler_params=_params(("arbitrary",)),
    )(h, g, target)


def _suffix_sum_rows(x, name):
    R, S = x.shape
    tb = _tile(S, 512)
    nb = S // tb
    tri = (lax.broadcasted_iota(jnp.int32, (tb, tb), 0) >= lax.broadcasted_iota(jnp.int32, (tb, tb), 1)).astype(BF16)

    def body(x_ref, tri_ref, o_ref, carry_ref):
        @pl.when(pl.program_id(0) == 0)
        def _():
            carry_ref[...] = jnp.zeros_like(carry_ref)

        xv = x_ref[...]
        t = tri_ref[...]
        acc = jnp.broadcast_to(carry_ref[:, 0:1], xv.shape)
        for part in _split3(xv):
            acc = acc + jnp.dot(part, t, preferred_element_type=F32)
        o_ref[...] = acc
        carry_ref[...] = jnp.broadcast_to(acc[:, 0:1], carry_ref.shape)

    rev = pl.BlockSpec((R, tb), lambda i: (0, nb - 1 - i))
    return pl.pallas_call(
        body, name=name, grid=(nb,),
        in_specs=[rev, _full_spec((tb, tb))], out_specs=rev,
        out_shape=jax.ShapeDtypeStruct((R, S), F32),
        scratch_shapes=[pltpu.VMEM((R, LANE), F32)],
        compiler_params=_params(("arbitrary",)),
    )(x, tri)


def _pairs(nb, by_key):
    if by_key:
        pr = [(i, j) for j in range(nb) for i in range(j, nb)]
    else:
        pr = [(i, j) for i in range(nb) for j in range(i + 1)]
    return (jnp.asarray([p[0] for p in pr], jnp.int32), jnp.asarray([p[1] for p in pr], jnp.int32), len(pr))


def _diag_mask(t, chunk_causal):
    r = lax.broadcasted_iota(jnp.int32, (t, t), 0)
    c = lax.broadcasted_iota(jnp.int32, (t, t), 1)
    if chunk_causal:
        return (c // CHUNK) <= (r // CHUNK)
    return c <= r


class _Att:
    def __init__(self, S, n_heads, q, ks, v, scale, chunk_causal, cum_col=None, cum_row=None):
        self.S, self.H, self.q, self.ks, self.v = S, n_heads, q, ks, v
        self.scale, self.chunk_causal = scale, chunk_causal
        self.cum_col, self.cum_row = cum_col, cum_row
        self.T = _tile(S, ATT_T)
        self.nb = S // self.T
        self.dq = q[1]
        self.dv = v[1]
        self.has_bias = cum_col is not None

    def q_spec(self, op):
        arr, w, off, per_head = op
        return pl.BlockSpec((self.T, w), lambda h, p, it, jt: (it[p], off + (h if per_head else 0)))

    def k_spec(self, op):
        arr, w, off, per_head = op
        return pl.BlockSpec((self.T, w), lambda h, p, it, jt: (jt[p], off + (h if per_head else 0)))

    def col_q(self):
        return pl.BlockSpec((None, self.T, 1), lambda h, p, it, jt: (h, it[p], 0))

    def row_k(self):
        return pl.BlockSpec((None, 1, self.T), lambda h, p, it, jt: (h, 0, jt[p]))

    def scores(self, q, k_refs, cc_ref, cr_ref, masked):
        k = k_refs[0][...] if len(k_refs) == 1 else jnp.concatenate([r[...] for r in k_refs], axis=-1)
        s = lax.dot_general(q, k, (((1,), (1,)), ((), ())), preferred_element_type=F32) * self.scale
        if self.has_bias:
            s = s + cc_ref[...] - cr_ref[...]
        mask = _diag_mask(self.T, self.chunk_causal) if masked else None
        return s, k, mask


def _att_fwd(att, name):
    S, H, T = att.S, att.H, att.T
    it, jt, npairs = _pairs(att.nb, by_key=False)
    nk = len(att.ks)

    def body(it_ref, jt_ref, *refs):
        q_ref = refs[0]
        k_refs = refs[1:1 + nk]
        v_ref = refs[1 + nk]
        n = 2 + nk
        cc_ref = cr_ref = None
        if att.has_bias:
            cc_ref, cr_ref = refs[n], refs[n + 1]
            n += 2
        o_ref, lse_ref, m_ref, l_ref, acc_ref = refs[n:n + 5]
        p = pl.program_id(1)
        i, j = it_ref[p], jt_ref[p]

        @pl.when(j == 0)
        def _():
            m_ref[...] = jnp.full_like(m_ref, -jnp.inf)
            l_ref[...] = jnp.zeros_like(l_ref)
            acc_ref[...] = jnp.zeros_like(acc_ref)

        def step(masked):
            s, _, mask = att.scores(q_ref[...], k_refs, cc_ref, cr_ref, masked)
            if masked:
                s = jnp.where(mask, s, -jnp.inf)
            m_prev = m_ref[...]
            m_new = jnp.maximum(m_prev, jnp.max(s, axis=-1, keepdims=True))
            alpha = jnp.exp(m_prev - m_new)
            pr = jnp.exp(s - m_new)
            l_ref[...] = alpha * l_ref[...] + jnp.sum(pr, axis=-1, keepdims=True)
            acc_ref[...] = alpha * acc_ref[...] + jnp.dot(pr.astype(BF16), v_ref[...], preferred_element_type=F32)
            m_ref[...] = m_new

        @pl.when(j < i)
        def _():
            step(False)

        @pl.when(j == i)
        def _():
            step(True)
            l = l_ref[...]
            o_ref[...] = (acc_ref[...] / l).astype(o_ref.dtype)
            lse_ref[...] = m_ref[...] + jnp.log(l)

    in_specs = [att.q_spec(att.q)] + [att.k_spec(k) for k in att.ks] + [att.k_spec(att.v)]
    args = [att.q[0]] + [k[0] for k in att.ks] + [att.v[0]]
    if att.has_bias:
        in_specs += [att.col_q(), att.row_k()]
        args += [att.cum_col, att.cum_row]
    out_specs = [pl.BlockSpec((T, att.dv), lambda h, p, it, jt: (it[p], h)), att.col_q()]
    return pl.pallas_call(
        body, name=name,
        grid_spec=pltpu.PrefetchScalarGridSpec(
            num_scalar_prefetch=2, grid=(H, npairs), in_specs=in_specs, out_specs=out_specs,
            scratch_shapes=[pltpu.VMEM((T, 1), F32), pltpu.VMEM((T, 1), F32), pltpu.VMEM((T, att.dv), F32)]),
        out_shape=[jax.ShapeDtypeStruct((S, H * att.dv), BF16), jax.ShapeDtypeStruct((H, S, 1), F32)],
        compiler_params=_params(("parallel", "arbitrary")),
    )(it, jt, *args)


def _att_delta(do, o, n_heads, name):
    S = do.shape[0]
    w = do.shape[1] // n_heads
    ts = _tile(S, ATT_T, 8)

    def body(do_ref, o_ref, d_ref):
        d_ref[...] = jnp.sum(do_ref[...].astype(F32) * o_ref[...].astype(F32), axis=-1, keepdims=True)

    blk = pl.BlockSpec((ts, w), lambda i, h: (i, h))
    return pl.pallas_call(
        body, name=name, grid=(S // ts, n_heads),
        in_specs=[blk, blk], out_specs=pl.BlockSpec((None, ts, 1), lambda i, h: (h, i, 0)),
        out_shape=jax.ShapeDtypeStruct((n_heads, S, 1), F32),
        compiler_params=_params(("parallel", "parallel")),
    )(do, o)


def _att_probs(att, q, k_refs, v_ref, do_ref, lse_ref, dl_ref, cc_ref, cr_ref, masked):
    s, k, mask = att.scores(q, k_refs, cc_ref, cr_ref, masked)
    pr = jnp.exp(s - lse_ref[...])
    if masked:
        pr = jnp.where(mask, pr, 0.0)
    do = do_ref[...]
    dp = lax.dot_general(do, v_ref[...], (((1,), (1,)), ((), ())), preferred_element_type=F32)
    ds = pr * (dp - dl_ref[...])
    return pr, ds, k, do


def _att_bwd_q(att, do, lse, delta, out_dtype, name, with_rowsum=False):
    S, H, T = att.S, att.H, att.T
    it, jt, npairs = _pairs(att.nb, by_key=False)
    nk = len(att.ks)

    def body(it_ref, jt_ref, *refs):
        q_ref = refs[0]
        k_refs = refs[1:1 + nk]
        v_ref, do_ref, lse_ref, dl_ref = refs[1 + nk:5 + nk]
        n = 5 + nk
        cc_ref = cr_ref = None
        if att.has_bias:
            cc_ref, cr_ref = refs[n], refs[n + 1]
            n += 2
        dq_ref = refs[n]
        n += 1
        rs_ref = None
        if with_rowsum:
            rs_ref = refs[n]
            n += 1
        acc_ref = refs[n]
        rs_acc = refs[n + 1] if with_rowsum else None
        p = pl.program_id(1)
        i, j = it_ref[p], jt_ref[p]

        @pl.when(j == 0)
        def _():
            acc_ref[...] = jnp.zeros_like(acc_ref)
            if with_rowsum:
                rs_acc[...] = jnp.zeros_like(rs_acc)

        def step(masked):
            _, ds, k, _ = _att_probs(att, q_ref[...], k_refs, v_ref, do_ref, lse_ref, dl_ref, cc_ref, cr_ref, masked)
            acc_ref[...] += jnp.dot(ds.astype(BF16), k, preferred_element_type=F32)
            if with_rowsum:
                rs_acc[...] += jnp.sum(ds, axis=-1, keepdims=True)

        @pl.when(j < i)
        def _():
            step(False)

        @pl.when(j == i)
        def _():
            step(True)
            dq_ref[...] = (acc_ref[...] * att.scale).astype(dq_ref.dtype)
            if with_rowsum:
                rs_ref[...] = rs_acc[...]

    do_op = (do, att.dv, 0, True)
    in_specs = ([att.q_spec(att.q)] + [att.k_spec(k) for k in att.ks]
                + [att.k_spec(att.v), att.q_spec(do_op), att.col_q(), att.col_q()])
    args = [att.q[0]] + [k[0] for k in att.ks] + [att.v[0], do, lse, delta]
    if att.has_bias:
        in_specs += [att.col_q(), att.row_k()]
        args += [att.cum_col, att.cum_row]
    out_specs = [pl.BlockSpec((T, att.dq), lambda h, p, it, jt: (it[p], h))]
    out_shape = [jax.ShapeDtypeStruct((S, H * att.dq), out_dtype)]
    scratch = [pltpu.VMEM((T, att.dq), F32)]
    if with_rowsum:
        out_specs.append(att.col_q())
        out_shape.append(jax.ShapeDtypeStruct((H, S, 1), F32))
        scratch.append(pltpu.VMEM((T, 1), F32))
    outs = pl.pallas_call(
        body, name=name,
        grid_spec=pltpu.PrefetchScalarGridSpec(
            num_scalar_prefetch=2, grid=(H, npairs), in_specs=in_specs, out_specs=out_specs,
            scratch_shapes=scratch),
        out_shape=out_shape,
        compiler_params=_params(("parallel", "arbitrary")),
    )(it, jt, *args)
    return outs if with_rowsum else outs[0]


def _att_bwd_kv(att, do, lse, delta, dk_dtypes, name):
    S, H, T = att.S, att.H, att.T
    it, jt, npairs = _pairs(att.nb, by_key=True)
    nk = len(att.ks)
    last = att.nb - 1
    widths = [k[1] for k in att.ks]

    def body(it_ref, jt_ref, *refs):
        q_ref = refs[0]
        k_refs = refs[1:1 + nk]
        v_ref, do_ref, lse_ref, dl_ref = refs[1 + nk:5 + nk]
        n = 5 + nk
        cc_ref = cr_ref = None
        if att.has_bias:
            cc_ref, cr_ref = refs[n], refs[n + 1]
            n += 2
        dk_refs = refs[n:n + nk]
        dv_ref = refs[n + nk]
        n += nk + 1
        dc_ref = None
        if att.has_bias:
            dc_ref = refs[n]
            n += 1
        dk_acc, dv_acc = refs[n], refs[n + 1]
        dc_acc = refs[n + 2] if att.has_bias else None
        p = pl.program_id(1)
        i, j = it_ref[p], jt_ref[p]

        @pl.when(i == j)
        def _():
            dk_acc[...] = jnp.zeros_like(dk_acc)
            dv_acc[...] = jnp.zeros_like(dv_acc)
            if att.has_bias:
                dc_acc[...] = jnp.zeros_like(dc_acc)

        def step(masked):
            q = q_ref[...]
            pr, ds, _, do_v = _att_probs(att, q, k_refs, v_ref, do_ref, lse_ref, dl_ref, cc_ref, cr_ref, masked)
            tn = (((0,), (0,)), ((), ()))
            dv_acc[...] += lax.dot_general(pr.astype(BF16), do_v, tn, preferred_element_type=F32)
            dk_acc[...] += lax.dot_general(ds.astype(BF16), q, tn, preferred_element_type=F32)
            if att.has_bias:
                dc_acc[...] -= jnp.sum(ds, axis=0, keepdims=True)

        @pl.when(i > j)
        def _():
            step(False)

        @pl.when(i == j)
        def _():
            step(True)

        @pl.when(i == last)
        def _():
            dk = dk_acc[...] * att.scale
            off = 0
            for r, w in zip(dk_refs, widths):
                r[...] = dk[:, off:off + w].astype(r.dtype)
                off += w
            dv_ref[...] = dv_acc[...].astype(dv_ref.dtype)
            if att.has_bias:
                dc_ref[...] = dc_acc[...]

    do_op = (do, att.dv, 0, True)
    in_specs = ([att.q_spec(att.q)] + [att.k_spec(k) for k in att.ks]
                + [att.k_spec(att.v), att.q_spec(do_op), att.col_q(), att.col_q()])
    args = [att.q[0]] + [k[0] for k in att.ks] + [att.v[0], do, lse, delta]
    if att.has_bias:
        in_specs += [att.col_q(), att.row_k()]
        args += [att.cum_col, att.cum_row]
    out_specs = [pl.BlockSpec((T, w), lambda h, p, it, jt: (jt[p], h)) for w in widths]
    out_specs.append(pl.BlockSpec((T, att.dv), lambda h, p, it, jt: (jt[p], h)))
    out_shape = [jax.ShapeDtypeStruct((S, H * w), dt) for w, dt in zip(widths, dk_dtypes)]
    out_shape.append(jax.ShapeDtypeStruct((S, H * att.dv), BF16))
    scratch = [pltpu.VMEM((T, att.dq), F32), pltpu.VMEM((T, att.dv), F32)]
    if att.has_bias:
        out_specs.append(att.row_k())
        out_shape.append(jax.ShapeDtypeStruct((H, 1, S), F32))
        scratch.append(pltpu.VMEM((1, T), F32))
    return pl.pallas_call(
        body, name=name,
        grid_spec=pltpu.PrefetchScalarGridSpec(
            num_scalar_prefetch=2, grid=(H, npairs), in_specs=in_specs, out_specs=out_specs,
            scratch_shapes=scratch),
        out_shape=out_shape,
        compiler_params=_params(("parallel", "arbitrary")),
    )(it, jt, *args)


def _adamw(w, g, m, v, name):
    R, C = w.shape
    tr = _tile(R, 256, 8)
    c1 = 1.0 - ADAM_B1 ** ADAM_STEP
    c2 = 1.0 - ADAM_B2 ** ADAM_STEP

    def body(w_ref, g_ref, m_ref, v_ref, d_ref, nm_ref, nv_ref):
        gv = g_ref[...]
        nm = ADAM_B1 * m_ref[...] + (1.0 - ADAM_B1) * gv
        nv = ADAM_B2 * v_ref[...] + (1.0 - ADAM_B2) * (gv * gv)
        d_ref[...] = -ADAM_LR * ((nm / c1) / (jnp.sqrt(nv / c2) + ADAM_EPS) + ADAM_WD * w_ref[...])
        nm_ref[...] = nm
        nv_ref[...] = nv

    blk = pl.BlockSpec((tr, C), lambda i: (i, 0))
    return pl.pallas_call(
        body, name=name, grid=(R // tr,),
        in_specs=[blk] * 4, out_specs=[blk] * 3,
        out_shape=[jax.ShapeDtypeStruct((R, C), F32)] * 3,
        compiler_params=_params(("parallel",)),
    )(w, g, m, v)


def _place():
    return lax.axis_index("x"), lax.axis_index("y"), lax.axis_index("c")


def _other_chips(x, y):
    return [(1 - x, y), (x, 1 - y), (1 - x, 1 - y)]


def _all_gather_shards(wp, name):
    R, C = wp.shape

    def body(w_ref, out_ref, send_sems, recv_sems, local_sem):
        x, y, c = _place()
        me = 2 * x + y
        mine = pltpu.make_async_copy(w_ref, out_ref.at[me], local_sem)
        mine.start()
        sends = []
        for n, (px, py) in enumerate(_other_chips(x, y)):
            cp = pltpu.make_async_remote_copy(
                src_ref=w_ref, dst_ref=out_ref.at[me], send_sem=send_sems.at[n], recv_sem=recv_sems.at[n],
                device_id=(px, py, c), device_id_type=MESH)
            cp.start()
            sends.append(cp)
        for n, (px, py) in enumerate(_other_chips(x, y)):
            pltpu.make_async_remote_copy(
                src_ref=w_ref, dst_ref=out_ref.at[2 * px + py], send_sem=send_sems.at[n],
                recv_sem=recv_sems.at[n], device_id=(px, py, c), device_id_type=MESH).wait_recv()
        for cp in sends:
            cp.wait_send()
        mine.wait()

    return pl.pallas_call(
        body, name=name,
        in_specs=[pl.BlockSpec(memory_space=pl.ANY)],
        out_specs=pl.BlockSpec(memory_space=pl.ANY),
        out_shape=jax.ShapeDtypeStruct((N_CHIPS, R, C), wp.dtype),
        scratch_shapes=[pltpu.SemaphoreType.DMA((3,)), pltpu.SemaphoreType.DMA((3,)), pltpu.SemaphoreType.DMA],
    )(wp)


def _reduce_scatter(gp, vec, name):
    _, R, C = gp.shape
    VR, W = vec.shape
    tr = _tile(R, PACK_ROWS, 16)
    nchunk = R // tr

    def body(gp_ref, vec_ref, out_ref, recv_ref, part_ref, sib_ref, vall_ref, vout_ref,
             buf_ref, acc_ref, send_sems, recv_sems, sib_sems, vsend_sems, vrecv_sems):
        x, y, c = _place()
        me = 2 * x + y
        chips = _other_chips(x, y)

        vall_ref[4 * x + 2 * y + c] = vec_ref[...]
        vsends = []
        for r in range(1, N_DEV):
            dx, dy, dc = (r >> 2) & 1, (r >> 1) & 1, r & 1
            peer = (x ^ dx, y ^ dy, c ^ dc)
            cp = pltpu.make_async_remote_copy(
                src_ref=vec_ref, dst_ref=vall_ref.at[4 * x + 2 * y + c], send_sem=vsend_sems.at[r - 1],
                recv_sem=vrecv_sems.at[r - 1], device_id=peer, device_id_type=MESH)
            cp.start()
            vsends.append(cp)
        sends = []
        for n, (px, py) in enumerate(chips):
            cp = pltpu.make_async_remote_copy(
                src_ref=gp_ref.at[2 * px + py], dst_ref=recv_ref.at[n], send_sem=send_sems.at[n],
                recv_sem=recv_sems.at[n], device_id=(px, py, c), device_id_type=MESH)
            cp.start()
            sends.append(cp)
        for n, (px, py) in enumerate(chips):
            pltpu.make_async_remote_copy(
                src_ref=gp_ref.at[me], dst_ref=recv_ref.at[n], send_sem=send_sems.at[n],
                recv_sem=recv_sems.at[n], device_id=(px, py, c), device_id_type=MESH).wait_recv()

        def sum_four(i, carry):
            rows = pl.ds(pl.multiple_of(i * tr, tr), tr)
            pltpu.sync_copy(gp_ref.at[me, rows], buf_ref.at[0])
            for n in range(3):
                pltpu.sync_copy(recv_ref.at[n, rows], buf_ref.at[n + 1])
            acc = buf_ref[0].astype(F32)
            for n in range(3):
                acc = acc + buf_ref[n + 1].astype(F32)
            acc_ref[0] = acc
            pltpu.sync_copy(acc_ref.at[0], part_ref.at[rows])
            return carry

        lax.fori_loop(0, nchunk, sum_four, 0)

        swap = pltpu.make_async_remote_copy(
            src_ref=part_ref, dst_ref=sib_ref, send_sem=sib_sems.at[0], recv_sem=sib_sems.at[1],
            device_id=(x, y, 1 - c), device_id_type=MESH)
        swap.start()
        swap.wait()

        def sum_two(i, carry):
            rows = pl.ds(pl.multiple_of(i * tr, tr), tr)
            pltpu.sync_copy(part_ref.at[rows], acc_ref.at[0])
            pltpu.sync_copy(sib_ref.at[rows], acc_ref.at[1])
            acc_ref[0] = acc_ref[0] + acc_ref[1]
            pltpu.sync_copy(acc_ref.at[0], out_ref.at[rows])
            return carry

        lax.fori_loop(0, nchunk, sum_two, 0)

        for r in range(1, N_DEV):
            dx, dy, dc = (r >> 2) & 1, (r >> 1) & 1, r & 1
            peer = (x ^ dx, y ^ dy, c ^ dc)
            pltpu.make_async_remote_copy(
                src_ref=vec_ref, dst_ref=vall_ref.at[4 * peer[0] + 2 * peer[1] + peer[2]],
                send_sem=vsend_sems.at[r - 1], recv_sem=vrecv_sems.at[r - 1],
                device_id=peer, device_id_type=MESH).wait_recv()
        total = vall_ref[0]
        for d in range(1, N_DEV):
            total = total + vall_ref[d]
        vout_ref[...] = total
        for cp in sends + vsends:
            cp.wait_send()

    hbm = pl.BlockSpec(memory_space=pl.ANY)
    vmem = pl.BlockSpec(memory_space=pltpu.VMEM)
    outs = pl.pallas_call(
        body, name=name,
        in_specs=[hbm, vmem],
        out_specs=[hbm, hbm, hbm, hbm, vmem, vmem],
        out_shape=[jax.ShapeDtypeStruct((R, C), F32), jax.ShapeDtypeStruct((3, R, C), gp.dtype),
                   jax.ShapeDtypeStruct((R, C), F32), jax.ShapeDtypeStruct((R, C), F32),
                   jax.ShapeDtypeStruct((N_DEV, VR, W), F32), jax.ShapeDtypeStruct((VR, W), F32)],
        scratch_shapes=[pltpu.VMEM((4, tr, C), gp.dtype), pltpu.VMEM((2, tr, C), F32),
                        pltpu.SemaphoreType.DMA((3,)), pltpu.SemaphoreType.DMA((3,)), pltpu.SemaphoreType.DMA((2,)),
                        pltpu.SemaphoreType.DMA((N_DEV - 1,)), pltpu.SemaphoreType.DMA((N_DEV - 1,))],
        compiler_params=pltpu.CompilerParams(vmem_limit_bytes=VMEM_LIMIT),
    )(gp, vec)
    return outs[0], outs[5]


def _rope_tables(S):
    pos = jnp.arange(S, dtype=F32)
    inv = 1.0 / (ROPE_THETA ** (jnp.arange(0, MLA_ROPE, 2, dtype=F32) / MLA_ROPE))
    ang = pos[:, None] * inv[None, :]
    cos, sin = jnp.cos(ang), jnp.sin(ang)
    half = MLA_ROPE // 2
    z = jnp.zeros((S, half), F32)
    one = jnp.ones((S, LANE - MLA_ROPE), F32)
    zero = jnp.zeros((S, LANE - MLA_ROPE), F32)
    kc = jnp.concatenate([cos, cos, one], axis=1)
    ksa = jnp.concatenate([-sin, z, zero], axis=1)
    ksb = jnp.concatenate([z, sin, zero], axis=1)
    qc = jnp.concatenate([jnp.ones((S, MLA_NOPE), F32), kc], axis=1)
    qsa = jnp.concatenate([jnp.zeros((S, MLA_NOPE), F32), ksa], axis=1)
    qsb = jnp.concatenate([jnp.zeros((S, MLA_NOPE), F32), ksb], axis=1)
    return (kc, ksa, ksb), (qc, qsa, qsb)


def _pad_cols(a, width):
    return jnp.pad(a, ((0, 0), (0, width - a.shape[1])))


def kernel(x, attn_norm, w_in, fox_f_bias, q_norm, w_uq, kv_norm, w_ukv, w_mla_branch, w_fox_branch, w_out, mlp_norm, w_up, w_down, final_norm, loss_target, m_attn_norm, m_w_in, m_fox_f_bias, m_q_norm, m_w_uq, m_kv_norm, m_w_ukv, m_w_mla_branch, m_w_fox_branch, m_w_out, m_mlp_norm, m_w_up, m_w_down, m_final_norm, v_attn_norm, v_w_in, v_fox_f_bias, v_q_norm, v_w_uq, v_kv_norm, v_w_ukv, v_w_mla_branch, v_w_fox_branch, v_w_out, v_mlp_norm, v_w_up, v_w_down, v_final_norm):
    _, S, D = x.shape
    H, HF = MLA_HEADS, FOX_HEADS
    QL, KVL = MLA_Q_LORA, MLA_KV_LORA
    assert H == HF and H <= 8
    xs = x[0]
    target = loss_target[0]

    big = [("w_in", w_in, 1), ("w_uq", w_uq, 1), ("w_ukv", w_ukv, 1), ("w_mla_branch", w_mla_branch, 1),
           ("w_fox_branch", w_fox_branch, 1), ("w_out", w_out, 0), ("w_up", w_up, 1), ("w_down", w_down, 0)]
    C = D
    rows, offs, off = {}, {}, 0
    for nm, w, _ in big:
        assert w[0].size % C == 0, nm
        rows[nm] = w[0].size // C
        offs[nm] = off
        off += -(-rows[nm] // 16) * 16

    def pad16(a, axis):
        short = -a.shape[axis] % 16
        return jnp.pad(a, [(0, short if d == axis else 0) for d in range(a.ndim)])

    R = -(-off // PACK_ROWS) * PACK_ROWS
    wp = jnp.concatenate([pad16(w[0].astype(BF16).reshape(-1, C), 0) for _, w, _ in big]
                         + [jnp.zeros((R - off, C), BF16)], axis=0)
    gathered = _all_gather_shards(wp, "all_gather_weights")
    full = {}
    for nm, w, axis in big:
        parts = [gathered[k, offs[nm]:offs[nm] + rows[nm]].reshape(w[0].shape) for k in range(N_CHIPS)]
        full[nm] = jnp.concatenate(parts, axis=axis)

    o_ckv = QL
    o_kr = o_ckv + KVL
    o_fq = o_kr + MLA_ROPE
    o_ff = o_fq + 3 * HF * FOX_HEAD_DIM
    o_g = o_ff + HF
    wi = full["w_in"]
    assert wi.shape[1] == o_g + 2 * D
    WS = QL + KVL + 2 * LANE
    NQKV = 3 * HF * FOX_HEAD_DIM
    w_small = jnp.concatenate([wi[:, :o_kr], _pad_cols(wi[:, o_kr:o_fq], LANE), _pad_cols(wi[:, o_ff:o_g], LANE)], axis=1)
    w_qkv = wi[:, o_fq:o_ff]
    w_g = wi[:, o_g:]
    w_pack = jnp.concatenate([w_small, w_qkv, w_g], axis=1)
    dqk = MLA_NOPE + MLA_ROPE
    w_uq_p = jnp.pad(full["w_uq"].reshape(QL, H, dqk), ((0, 0), (0, 0), (0, QPAD - dqk))).reshape(QL, H * QPAD)
    ukv = full["w_ukv"].reshape(KVL, H, MLA_NOPE + MLA_V)
    w_ukv_p = jnp.concatenate([ukv[:, :, :MLA_NOPE].reshape(KVL, H * MLA_NOPE),
                               ukv[:, :, MLA_NOPE:].reshape(KVL, H * MLA_V)], axis=1)
    w_mb, w_fb, w_o, w_u, w_d = (full[n] for n in ("w_mla_branch", "w_fox_branch", "w_out", "w_up", "w_down"))

    (kc, ksa, ksb), (qc, qsa, qsb) = _rope_tables(S)
    bias_pad = _pad_cols(fox_f_bias, LANE)

    xn = _norm_fwd(xs, attn_norm, "attn_norm_fwd")
    small = _matmul(xn, w_small, "nn", [F32], "proj_small")
    qkv = _matmul(xn, w_qkv, "nn", [BF16], "proj_qkv")
    gpre = _matmul(xn, w_g, "nn", [F32], "proj_gates")
    cqn, ckvn, kr, cum = _prep_fwd(small, q_norm, kv_norm, bias_pad, kc, ksa, ksb, HF, "prep_fwd")
    q_raw = _matmul(cqn, w_uq_p, "nn", [F32], "mla_q_up")
    q_rot = _rope_heads(q_raw, qc, qsa, qsb, 1, "mla_q_rope")
    kv2 = _matmul(ckvn, w_ukv_p, "nn", [BF16], "mla_kv_up")

    mla = _Att(S, H, (q_rot, QPAD, 0, True), [(kv2, MLA_NOPE, 0, True), (kr, LANE, 0, False)],
               (kv2, MLA_V, H, True), 1.0 / math.sqrt(dqk), True)
    o_mla, lse_mla = _att_fwd(mla, "mla_att_fwd")

    cum_t = jnp.transpose(cum[:, :HF])
    cum_col, cum_row = cum_t[:, :, None], cum_t[:, None, :]
    fox = _Att(S, HF, (qkv, FOX_HEAD_DIM, 0, True), [(qkv, FOX_HEAD_DIM, HF, True)],
               (qkv, FOX_HEAD_DIM, 2 * HF, True), 1.0 / math.sqrt(FOX_HEAD_DIM), False, cum_col, cum_row)
    o_fox, lse_fox = _att_fwd(fox, "fox_att_fwd")

    y_mla = _matmul(o_mla, w_mb, "nn", [F32], "mla_branch")
    y_fox = _matmul(o_fox, w_fb, "nn", [F32], "fox_branch")
    merged = _gate_fwd(gpre, y_mla, y_fox, "gate_fwd")
    h1 = _matmul(merged, w_o, "nn", [F32], "out_proj", extras=(xs,), epilogue=lambda acc, r: (acc + r,))
    hn = _norm_fwd(h1, mlp_norm, "mlp_norm_fwd")

    def relu2(acc):
        a = jnp.maximum(acc, 0.0)
        return a * a, a

    u, a_pos = _matmul(hn, w_u, "nn", [BF16, BF16], "mlp_up", epilogue=relu2)
    h2 = _matmul(u, w_d, "nn", [F32], "mlp_down", extras=(h1,), epilogue=lambda acc, r: (acc + r,))
    dh2, g_final, loss_part = _final(h2, final_norm.reshape(1, D), target, "final_norm_loss")

    dh2_b = dh2.astype(BF16)
    da = _matmul(dh2_b, w_d, "nt", [BF16], "mlp_down_dx", extras=(a_pos,),
                 epilogue=lambda acc, a: (acc * (2.0 * a.astype(F32)),))
    g_w_down = _mm_tn(u, dh2_b, "mlp_down_dw")
    dhn = _matmul(da, w_u, "nt", [F32], "mlp_up_dx")
    g_w_up = _mm_tn(hn, da, "mlp_up_dw")
    dh1, g_mlp_norm = _norm_bwd(h1, dhn, mlp_norm, dh2, "mlp_norm_bwd")
    dh1_b = dh1.astype(BF16)
    dmerged = _matmul(dh1_b, w_o, "nt", [F32], "out_proj_dx")
    g_w_out = _mm_tn(merged, dh1_b, "out_proj_dw")
    dy_mla, dy_fox, dg_mla, dg_fox = _gate_bwd(dmerged, gpre, y_mla, y_fox, "gate_bwd")
    do_mla = _matmul(dy_mla, w_mb, "nt", [BF16], "mla_branch_dx")
    g_w_mb = _mm_tn(o_mla, dy_mla, "mla_branch_dw")
    do_fox = _matmul(dy_fox, w_fb, "nt", [BF16], "fox_branch_dx")
    g_w_fb = _mm_tn(o_fox, dy_fox, "fox_branch_dw")

    delta_mla = _att_delta(do_mla, o_mla, H, "mla_att_delta")
    dq_raw = _att_bwd_q(mla, do_mla, lse_mla, delta_mla, F32, "mla_att_bwd_q")
    dk_nope, dkr_heads, dv_mla = _att_bwd_kv(mla, do_mla, lse_mla, delta_mla, [BF16, F32], "mla_att_bwd_kv")
    delta_fox = _att_delta(do_fox, o_fox, HF, "fox_att_delta")
    dfq, ds_rowsum = _att_bwd_q(fox, do_fox, lse_fox, delta_fox, BF16, "fox_att_bwd_q", with_rowsum=True)
    dfk, dfv, dcum = _att_bwd_kv(fox, do_fox, lse_fox, delta_fox + ds_rowsum, [BF16], "fox_att_bwd_kv")

    dq_rot = _rope_heads(dq_raw, qc, qsa, qsb, -1, "mla_q_rope_bwd")
    dcqn = _matmul(dq_rot, w_uq_p, "nt", [F32], "mla_q_up_dx")
    g_w_uq_p = _mm_tn(cqn, dq_rot, "mla_q_up_dw")
    dkv2 = jnp.concatenate([dk_nope, dv_mla], axis=1)
    dckvn = _matmul(dkv2, w_ukv_p, "nt", [F32], "mla_kv_up_dx")
    g_w_ukv_p = _mm_tn(ckvn, dkv2, "mla_kv_up_dw")

    dcum_rows = jnp.pad(dcum[:, 0, :], ((0, 8 - HF), (0, 0)))
    dlogf_rows = _suffix_sum_rows(dcum_rows, "fox_forget_suffix_sum")
    dlogf = _pad_cols(jnp.transpose(dlogf_rows[:HF]), LANE)
    d_small, g_q_norm, g_kv_norm, g_bias = _prep_bwd(
        small, dcqn, dckvn, dkr_heads, dlogf, q_norm, kv_norm, bias_pad, kc, ksa, ksb, H, "prep_bwd")
    dproj = jnp.concatenate([d_small, dfq, dfk, dfv, dg_mla, dg_fox], axis=1)
    dxn = _matmul(dproj, w_pack, "nt", [F32], "proj_dx")
    g_w_pack = _mm_tn(xn, dproj, "proj_dw")
    grad_x, g_attn_norm = _norm_bwd(xs, dxn, attn_norm, dh1, "attn_norm_bwd")

    gs, gq, gg = g_w_pack[:, :WS], g_w_pack[:, WS:WS + NQKV], g_w_pack[:, WS + NQKV:]
    g_w_in = jnp.concatenate([gs[:, :o_kr], gs[:, o_kr:o_kr + MLA_ROPE], gq,
                              gs[:, o_kr + LANE:o_kr + LANE + HF], gg], axis=1)
    g_w_uq = g_w_uq_p.reshape(QL, H, QPAD)[:, :, :dqk].reshape(QL, H * dqk)
    g_w_ukv = jnp.concatenate([g_w_ukv_p[:, :H * MLA_NOPE].reshape(KVL, H, MLA_NOPE),
                               g_w_ukv_p[:, H * MLA_NOPE:].reshape(KVL, H, MLA_V)], axis=2).reshape(KVL, -1)
    g_full = {"w_in": g_w_in, "w_uq": g_w_uq, "w_ukv": g_w_ukv, "w_mla_branch": g_w_mb, "w_fox_branch": g_w_fb,
              "w_out": g_w_out, "w_up": g_w_up, "w_down": g_w_down}

    slabs = []
    for nm, w, axis in big:
        g = g_full[nm]
        if axis == 1:
            k_dim, n = g.shape[0], g.shape[1] // N_CHIPS
            g4 = jnp.transpose(g.reshape(k_dim, N_CHIPS, n), (1, 0, 2))
        else:
            g4 = g.reshape(N_CHIPS, g.shape[0] // N_CHIPS, g.shape[1])
        slabs.append(pad16(g4.reshape(N_CHIPS, -1, C).astype(BF16), 1))
    slabs.append(jnp.zeros((N_CHIPS, R - off, C), BF16))
    gp = jnp.concatenate(slabs, axis=1)
    vec_w = max(D, LANE)
    vec_rows = [g_attn_norm, g_mlp_norm, g_final, g_q_norm, g_kv_norm, g_bias, loss_part]
    vec = jnp.concatenate([_pad_cols(v, vec_w) for v in vec_rows] + [jnp.zeros((1, vec_w), F32)], axis=0)
    g_shards, vsum = _reduce_scatter(gp, vec, "reduce_scatter_grads")

    moments = {"attn_norm": (m_attn_norm, v_attn_norm), "w_in": (m_w_in, v_w_in), "fox_f_bias": (m_fox_f_bias, v_fox_f_bias),
               "q_norm": (m_q_norm, v_q_norm), "w_uq": (m_w_uq, v_w_uq), "kv_norm": (m_kv_norm, v_kv_norm),
               "w_ukv": (m_w_ukv, v_w_ukv), "w_mla_branch": (m_w_mla_branch, v_w_mla_branch),
               "w_fox_branch": (m_w_fox_branch, v_w_fox_branch), "w_out": (m_w_out, v_w_out),
               "mlp_norm": (m_mlp_norm, v_mlp_norm), "w_up": (m_w_up, v_w_up), "w_down": (m_w_down, v_w_down),
               "final_norm": (m_final_norm, v_final_norm)}
    weights = {"attn_norm": attn_norm, "w_in": w_in, "fox_f_bias": fox_f_bias, "q_norm": q_norm, "w_uq": w_uq,
               "kv_norm": kv_norm, "w_ukv": w_ukv, "w_mla_branch": w_mla_branch, "w_fox_branch": w_fox_branch,
               "w_out": w_out, "mlp_norm": mlp_norm, "w_up": w_up, "w_down": w_down, "final_norm": final_norm}
    grads, deltas, new_m, new_v = {}, {}, {}, {}
    for nm, w, _ in big:
        shp = w[0].shape
        g = g_shards[offs[nm]:offs[nm] + rows[nm]].reshape(shp)
        d, nm_, nv_ = _adamw(w[0], g, moments[nm][0][0], moments[nm][1][0], "adamw_" + nm)
        grads[nm], deltas[nm], new_m[nm], new_v[nm] = g[None], d[None], nm_[None], nv_[None]
    vec_names = ["attn_norm", "mlp_norm", "final_norm", "q_norm", "kv_norm", "fox_f_bias"]

    def vec_pack(arrs):
        return jnp.concatenate([_pad_cols(a.reshape(1, -1), vec_w) for a in arrs]
                               + [jnp.zeros((2, vec_w), F32)], axis=0)

    vd, vm, vv = _adamw(vec_pack([weights[n] for n in vec_names]), vsum,
                        vec_pack([moments[n][0] for n in vec_names]), vec_pack([moments[n][1] for n in vec_names]),
                        "adamw_vectors")
    for r, nm in enumerate(vec_names):
        shp = weights[nm].shape
        n = weights[nm].size
        grads[nm] = vsum[r, :n].reshape(shp)
        deltas[nm], new_m[nm], new_v[nm] = vd[r, :n].reshape(shp), vm[r, :n].reshape(shp), vv[r, :n].reshape(shp)
    loss = vsum[6, 0]

    order = ["attn_norm", "w_in", "fox_f_bias", "q_norm", "w_uq", "kv_norm", "w_ukv", "w_mla_branch", "w_fox_branch",
             "w_out", "mlp_norm", "w_up", "w_down", "final_norm"]
    return (loss, grad_x[None], *[grads[n] for n in order], *[deltas[n] for n in order],
            *[new_m[n] for n in order], *[new_v[n] for n in order])
```

```python
import functools
import math

import jax
import jax.numpy as jnp
from jax import lax
from jax.experimental import pallas as pl
from jax.experimental.pallas import tpu as pltpu

CHUNK = 64
MLA_HEADS = 8
MLA_Q_LORA = 512
MLA_KV_LORA = 256
MLA_NOPE = 128
MLA_ROPE = 64
MLA_V = 128
ROPE_THETA = 10000.0
FOX_HEADS = 8
FOX_HEAD_DIM = 128
EPS = 1e-6

ADAM_LR = 0.001
ADAM_B1 = 0.9
ADAM_B2 = 0.999
ADAM_EPS = 1e-08
ADAM_WD = 0.01
ADAM_STEP = 10

LANE = 128
QPAD = 2 * LANE
N_CHIPS = 4
N_DEV = 8
VMEM_LIMIT = 48 * 1024 * 1024
ATT_T = 512
ROW_T = 256
PACK_ROWS = 256

BF16 = jnp.bfloat16
F32 = jnp.float32
MESH = pl.DeviceIdType.MESH


def _tile(dim, pref, align=LANE):
    if dim <= pref:
        return dim
    t = (pref // align) * align
    while t >= align:
        if dim % t == 0:
            return t
        t -= align
    return dim


def _params(sem=None):
    return pltpu.CompilerParams(dimension_semantics=sem, vmem_limit_bytes=VMEM_LIMIT)


def _matmul(a, b, mode, out_dtypes, name, *, tm=1024, tn=512, tk=2048, extras=(), epilogue=None):
    if mode == "nn":
        (M, K), (K2, N) = a.shape, b.shape
    elif mode == "nt":
        (M, K), (N, K2) = a.shape, b.shape
    else:
        (K, M), (K2, N) = a.shape, b.shape
    assert K == K2, (name, a.shape, b.shape)
    tm, tn, tk = _tile(M, tm), _tile(N, tn), _tile(K, tk)
    nk = K // tk
    n_out = len(out_dtypes)
    n_ex = len(extras)

    def body(*refs):
        a_ref, b_ref = refs[0], refs[1]
        ex_refs = refs[2:2 + n_ex]
        o_refs = refs[2 + n_ex:2 + n_ex + n_out]
        acc_ref = refs[2 + n_ex + n_out]
        k = pl.program_id(2)
        if mode == "nn":
            dims = (((1,), (0,)), ((), ()))
        elif mode == "nt":
            dims = (((1,), (1,)), ((), ()))
        else:
            dims = (((0,), (0,)), ((), ()))
        part = lax.dot_general(a_ref[...], b_ref[...], dims, preferred_element_type=F32)

        @pl.when(k == 0)
        def _():
            acc_ref[...] = part

        @pl.when(k > 0)
        def _():
            acc_ref[...] += part

        @pl.when(k == nk - 1)
        def _():
            acc = acc_ref[...]
            if epilogue is None:
                outs = (acc,)
            else:
                outs = epilogue(acc, *[r[...] for r in ex_refs])
            for o_ref, o in zip(o_refs, outs):
                o_ref[...] = o.astype(o_ref.dtype)

    if mode == "nn":
        a_spec = pl.BlockSpec((tm, tk), lambda i, j, k: (i, k))
        b_spec = pl.BlockSpec((tk, tn), lambda i, j, k: (k, j))
    elif mode == "nt":
        a_spec = pl.BlockSpec((tm, tk), lambda i, j, k: (i, k))
        b_spec = pl.BlockSpec((tn, tk), lambda i, j, k: (j, k))
    else:
        a_spec = pl.BlockSpec((tk, tm), lambda i, j, k: (k, i))
        b_spec = pl.BlockSpec((tk, tn), lambda i, j, k: (k, j))
    mn_spec = pl.BlockSpec((tm, tn), lambda i, j, k: (i, j))
    outs = pl.pallas_call(
        body,
        name=name,
        grid=(M // tm, N // tn, nk),
        in_specs=[a_spec, b_spec] + [mn_spec] * n_ex,
        out_specs=[mn_spec] * n_out,
        out_shape=[jax.ShapeDtypeStruct((M, N), dt) for dt in out_dtypes],
        scratch_shapes=[pltpu.VMEM((tm, tn), F32)],
        compiler_params=_params(("parallel", "parallel", "arbitrary")),
    )(a, b, *extras)
    return outs[0] if n_out == 1 else outs


def _mm_tn(a, b, name):
    return _matmul(a, b, "tn", [F32], name, tm=1024, tn=1024, tk=2048)


def _row_spec(ts, width, col=0):
    return pl.BlockSpec((ts, width), lambda i: (i, col))


def _full_spec(shape):
    return pl.BlockSpec(shape, lambda i: tuple(0 for _ in shape))


def _rms(x):
    return lax.rsqrt(jnp.mean(x * x, axis=-1, keepdims=True) + EPS)


def _rms_bwd(x, dy, g):
    r = _rms(x)
    xh = x * r
    gy = dy * g
    dx = r * (gy - xh * jnp.mean(xh * gy, axis=-1, keepdims=True))
    return dx, dy * xh


def _norm_fwd(x, g, name):
    S, D = x.shape
    ts = _tile(S, ROW_T, 8)

    def body(x_ref, g_ref, o_ref):
        xv = x_ref[...]
        o_ref[...] = ((xv * _rms(xv)) * g_ref[...]).astype(BF16)

    return pl.pallas_call(
        body, name=name, grid=(S // ts,),
        in_specs=[_row_spec(ts, D), _full_spec((1, D))],
        out_specs=_row_spec(ts, D),
        out_shape=jax.ShapeDtypeStruct((S, D), BF16),
        compiler_params=_params(("parallel",)),
    )(x, g)


def _norm_bwd(x, dy, g, dres, name):
    S, D = x.shape
    ts = _tile(S, ROW_T, 8)

    def body(x_ref, dy_ref, g_ref, dres_ref, dx_ref, dg_ref):
        dx, dg_rows = _rms_bwd(x_ref[...], dy_ref[...], g_ref[...])
        dx_ref[...] = dres_ref[...] + dx

        @pl.when(pl.program_id(0) == 0)
        def _():
            dg_ref[...] = jnp.zeros_like(dg_ref)

        dg_ref[...] += jnp.sum(dg_rows, axis=0, keepdims=True)

    return pl.pallas_call(
        body, name=name, grid=(S // ts,),
        in_specs=[_row_spec(ts, D), _row_spec(ts, D), _full_spec((1, D)), _row_spec(ts, D)],
        out_specs=[_row_spec(ts, D), _full_spec((1, D))],
        out_shape=[jax.ShapeDtypeStruct((S, D), F32), jax.ShapeDtypeStruct((1, D), F32)],
        compiler_params=_params(("arbitrary",)),
    )(x, dy, g, dres)


def _rope(x, c, sa, sb, sign):
    w = x.shape[-1]
    half = MLA_ROPE // 2
    fwd = pltpu.roll(x, w - half, 1)
    back = pltpu.roll(x, half, 1)
    if sign < 0:
        return x * c - fwd * sa - back * sb
    return x * c + fwd * sa + back * sb


def _split3(x):
    hi = x.astype(BF16)
    r1 = x - hi.astype(F32)
    mid = r1.astype(BF16)
    lo = (r1 - mid.astype(F32)).astype(BF16)
    return hi, mid, lo


def _prep_fwd(small, q_norm, kv_norm, bias_pad, kc, ksa, ksb, n_heads, name):
    S, W = small.shape
    QL, KVL = q_norm.shape[1], kv_norm.shape[1]
    assert W == QL + KVL + 2 * LANE
    ts = _tile(S, ROW_T, 8)
    tri = (lax.broadcasted_iota(jnp.int32, (ts, ts), 0) >= lax.broadcasted_iota(jnp.int32, (ts, ts), 1)).astype(BF16)

    def body(s_ref, qn_ref, kvn_ref, b_ref, kc_ref, ksa_ref, ksb_ref, tri_ref,
             cqn_ref, ckvn_ref, kr_ref, cum_ref, carry_ref):
        cq = s_ref[:, 0:QL]
        cqn_ref[...] = ((cq * _rms(cq)) * qn_ref[...]).astype(BF16)
        ckv = s_ref[:, QL:QL + KVL]
        ckvn_ref[...] = ((ckv * _rms(ckv)) * kvn_ref[...]).astype(BF16)
        kr = s_ref[:, QL + KVL:QL + KVL + LANE]
        kr_ref[...] = _rope(kr, kc_ref[...], ksa_ref[...], ksb_ref[...], 1).astype(BF16)
        z = s_ref[:, QL + KVL + LANE:W] + b_ref[...]
        logf = jnp.minimum(z, 0.0) - jnp.log1p(jnp.exp(-jnp.abs(z)))
        lane = lax.broadcasted_iota(jnp.int32, logf.shape, 1)
        logf = jnp.where(lane < n_heads, logf, 0.0)

        @pl.when(pl.program_id(0) == 0)
        def _():
            carry_ref[...] = jnp.zeros_like(carry_ref)

        t = tri_ref[...]
        cum = carry_ref[...]
        for part in _split3(logf):
            cum = cum + jnp.dot(t, part, preferred_element_type=F32)
        cum_ref[...] = cum
        carry_ref[...] = cum[ts - 1:ts, :]

    return pl.pallas_call(
        body, name=name, grid=(S // ts,),
        in_specs=[_row_spec(ts, W), _full_spec((1, QL)), _full_spec((1, KVL)), _full_spec((1, LANE)),
                  _row_spec(ts, LANE), _row_spec(ts, LANE), _row_spec(ts, LANE), _full_spec((ts, ts))],
        out_specs=[_row_spec(ts, QL), _row_spec(ts, KVL), _row_spec(ts, LANE), _row_spec(ts, LANE)],
        out_shape=[jax.ShapeDtypeStruct((S, QL), BF16), jax.ShapeDtypeStruct((S, KVL), BF16),
                   jax.ShapeDtypeStruct((S, LANE), BF16), jax.ShapeDtypeStruct((S, LANE), F32)],
        scratch_shapes=[pltpu.VMEM((1, LANE), F32)],
        compiler_params=_params(("arbitrary",)),
    )(small, q_norm, kv_norm, bias_pad, kc, ksa, ksb, tri)


def _prep_bwd(small, dcqn, dckvn, dkr_heads, dlogf, q_norm, kv_norm, bias_pad, kc, ksa, ksb, n_heads, name):
    S, W = small.shape
    QL, KVL = q_norm.shape[1], kv_norm.shape[1]
    ts = _tile(S, ROW_T, 8)

    def body(s_ref, dcq_ref, dckv_ref, dkr_ref, dlf_ref, qn_ref, kvn_ref, b_ref, kc_ref, ksa_ref, ksb_ref,
             ds_ref, gq_ref, gkv_ref, gb_ref):
        dcq, gq_rows = _rms_bwd(s_ref[:, 0:QL], dcq_ref[...], qn_ref[...])
        ds_ref[:, 0:QL] = dcq.astype(BF16)
        dckv, gkv_rows = _rms_bwd(s_ref[:, QL:QL + KVL], dckv_ref[...], kvn_ref[...])
        ds_ref[:, QL:QL + KVL] = dckv.astype(BF16)
        dkr = dkr_ref[:, 0:LANE]
        for h in range(1, n_heads):
            dkr = dkr + dkr_ref[:, h * LANE:(h + 1) * LANE]
        ds_ref[:, QL + KVL:QL + KVL + LANE] = _rope(dkr, kc_ref[...], ksa_ref[...], ksb_ref[...], -1).astype(BF16)
        z = s_ref[:, QL + KVL + LANE:W] + b_ref[...]
        dff = dlf_ref[...] * (1.0 / (1.0 + jnp.exp(z)))
        ds_ref[:, QL + KVL + LANE:W] = dff.astype(BF16)

        @pl.when(pl.program_id(0) == 0)
        def _():
            gq_ref[...] = jnp.zeros_like(gq_ref)
            gkv_ref[...] = jnp.zeros_like(gkv_ref)
            gb_ref[...] = jnp.zeros_like(gb_ref)

        gq_ref[...] += jnp.sum(gq_rows, axis=0, keepdims=True)
        gkv_ref[...] += jnp.sum(gkv_rows, axis=0, keepdims=True)
        gb_ref[...] += jnp.sum(dff, axis=0, keepdims=True)

    return pl.pallas_call(
        body, name=name, grid=(S // ts,),
        in_specs=[_row_spec(ts, W), _row_spec(ts, QL), _row_spec(ts, KVL), _row_spec(ts, n_heads * LANE),
                  _row_spec(ts, LANE), _full_spec((1, QL)), _full_spec((1, KVL)), _full_spec((1, LANE)),
                  _row_spec(ts, LANE), _row_spec(ts, LANE), _row_spec(ts, LANE)],
        out_specs=[_row_spec(ts, W), _full_spec((1, QL)), _full_spec((1, KVL)), _full_spec((1, LANE))],
        out_shape=[jax.ShapeDtypeStruct((S, W), BF16), jax.ShapeDtypeStruct((1, QL), F32),
                   jax.ShapeDtypeStruct((1, KVL), F32), jax.ShapeDtypeStruct((1, LANE), F32)],
        compiler_params=_params(("arbitrary",)),
    )(small, dcqn, dckvn, dkr_heads, dlogf, q_norm, kv_norm, bias_pad, kc, ksa, ksb)


def _rope_heads(x, c, sa, sb, sign, name):
    S, W = x.shape
    nh = W // QPAD
    ts = _tile(S, ROW_T, 8)

    def body(x_ref, c_ref, sa_ref, sb_ref, o_ref):
        o_ref[...] = _rope(x_ref[...], c_ref[...], sa_ref[...], sb_ref[...], sign).astype(BF16)

    tab = pl.BlockSpec((ts, QPAD), lambda i, h: (i, 0))
    blk = pl.BlockSpec((ts, QPAD), lambda i, h: (i, h))
    return pl.pallas_call(
        body, name=name, grid=(S // ts, nh),
        in_specs=[blk, tab, tab, tab], out_specs=blk,
        out_shape=jax.ShapeDtypeStruct((S, W), BF16),
        compiler_params=_params(("parallel", "parallel")),
    )(x, c, sa, sb)


def _sigmoid(z):
    return 1.0 / (1.0 + jnp.exp(-z))


def _gate_fwd(gpre, y_mla, y_fox, name):
    S, D = y_mla.shape
    ts = _tile(S, ROW_T, 8)

    def body(ga_ref, gb_ref, ya_ref, yb_ref, o_ref):
        o_ref[...] = (_sigmoid(ga_ref[...]) * ya_ref[...] + _sigmoid(gb_ref[...]) * yb_ref[...]).astype(BF16)

    return pl.pallas_call(
        body, name=name, grid=(S // ts,),
        in_specs=[_row_spec(ts, D, 0), _row_spec(ts, D, 1), _row_spec(ts, D), _row_spec(ts, D)],
        out_specs=_row_spec(ts, D),
        out_shape=jax.ShapeDtypeStruct((S, D), BF16),
        compiler_params=_params(("parallel",)),
    )(gpre, gpre, y_mla, y_fox)


def _gate_bwd(dmerged, gpre, y_mla, y_fox, name):
    S, D = y_mla.shape
    ts = _tile(S, ROW_T, 8)

    def body(dm_ref, ga_ref, gb_ref, ya_ref, yb_ref, dya_ref, dyb_ref, dga_ref, dgb_ref):
        dm = dm_ref[...]
        ga = _sigmoid(ga_ref[...])
        gb = _sigmoid(gb_ref[...])
        dya_ref[...] = (dm * ga).astype(BF16)
        dyb_ref[...] = (dm * gb).astype(BF16)
        dga_ref[...] = (dm * ya_ref[...] * (ga * (1.0 - ga))).astype(BF16)
        dgb_ref[...] = (dm * yb_ref[...] * (gb * (1.0 - gb))).astype(BF16)

    return pl.pallas_call(
        body, name=name, grid=(S // ts,),
        in_specs=[_row_spec(ts, D), _row_spec(ts, D, 0), _row_spec(ts, D, 1), _row_spec(ts, D), _row_spec(ts, D)],
        out_specs=[_row_spec(ts, D)] * 4,
        out_shape=[jax.ShapeDtypeStruct((S, D), BF16)] * 4,
        compiler_params=_params(("parallel",)),
    )(dmerged, gpre, gpre, y_mla, y_fox)


def _final(h, g, target, name):
    S, D = h.shape
    ts = _tile(S, ROW_T, 8)

    def body(h_ref, g_ref, t_ref, dh_ref, dg_ref, loss_ref):
        hv = h_ref[...]
        gv = g_ref[...]
        err = (hv * _rms(hv)) * gv - t_ref[...]
        dh, dg_rows = _rms_bwd(hv, err / D, gv)
        dh_ref[...] = dh

        @pl.when(pl.program_id(0) == 0)
        def _():
            dg_ref[...] = jnp.zeros_like(dg_ref)
            loss_ref[...] = jnp.zeros_like(loss_ref)

        dg_ref[...] += jnp.sum(dg_rows, axis=0, keepdims=True)
        row_loss = jnp.mean(err * err, axis=-1, keepdims=True)
        loss_ref[...] += 0.5 * jnp.sum(row_loss, axis=0, keepdims=True)

    return pl.pallas_call(
        body, name=name, grid=(S // ts,),
        in_specs=[_row_spec(ts, D), _full_spec((1, D)), _row_spec(ts, D)],
        out_specs=[_row_spec(ts, D), _full_spec((1, D)), _full_spec((1, LANE))],
        out_shape=[jax.ShapeDtypeStruct((S, D), F32), jax.ShapeDtypeStruct((1, D), F32),
                   jax.ShapeDtypeStruct((1, LANE), F32)],
        compiler_params=_params(("arbitrary",)),
    )(h, g, target)


def _suffix_sum_rows(x, name):
    R, S = x.shape
    tb = _tile(S, 512)
    nb = S // tb
    tri = (lax.broadcasted_iota(jnp.int32, (tb, tb), 0) >= lax.broadcasted_iota(jnp.int32, (tb, tb), 1)).astype(BF16)

    def body(x_ref, tri_ref, o_ref, carry_ref):
        @pl.when(pl.program_id(0) == 0)
        def _():
            carry_ref[...] = jnp.zeros_like(carry_ref)

        xv = x_ref[...]
        t = tri_ref[...]
        acc = jnp.broadcast_to(carry_ref[:, 0:1], xv.shape)
        for part in _split3(xv):
            acc = acc + jnp.dot(part, t, preferred_element_type=F32)
        o_ref[...] = acc
        carry_ref[...] = jnp.broadcast_to(acc[:, 0:1], carry_ref.shape)

    rev = pl.BlockSpec((R, tb), lambda i: (0, nb - 1 - i))
    return pl.pallas_call(
        body, name=name, grid=(nb,),
        in_specs=[rev, _full_spec((tb, tb))], out_specs=rev,
        out_shape=jax.ShapeDtypeStruct((R, S), F32),
        scratch_shapes=[pltpu.VMEM((R, LANE), F32)],
        compiler_params=_params(("arbitrary",)),
    )(x, tri)


def _pairs(nb, by_key):
    if by_key:
        pr = [(i, j) for j in range(nb) for i in range(j, nb)]
    else:
        pr = [(i, j) for i in range(nb) for j in range(i + 1)]
    return (jnp.asarray([p[0] for p in pr], jnp.int32), jnp.asarray([p[1] for p in pr], jnp.int32), len(pr))


def _diag_mask(t, chunk_causal):
    r = lax.broadcasted_iota(jnp.int32, (t, t), 0)
    c = lax.broadcasted_iota(jnp.int32, (t, t), 1)
    if chunk_causal:
        return (c // CHUNK) <= (r // CHUNK)
    return c <= r


class _Att:
    def __init__(self, S, n_heads, q, ks, v, scale, chunk_causal, cum_col=None, cum_row=None):
        self.S, self.H, self.q, self.ks, self.v = S, n_heads, q, ks, v
        self.scale, self.chunk_causal = scale, chunk_causal
        self.cum_col, self.cum_row = cum_col, cum_row
        self.T = _tile(S, ATT_T)
        self.nb = S // self.T
        self.dq = q[1]
        self.dv = v[1]
        self.has_bias = cum_col is not None

    def q_spec(self, op):
        arr, w, off, per_head = op
        return pl.BlockSpec((self.T, w), lambda h, p, it, jt: (it[p], off + (h if per_head else 0)))

    def k_spec(self, op):
        arr, w, off, per_head = op
        return pl.BlockSpec((self.T, w), lambda h, p, it, jt: (jt[p], off + (h if per_head else 0)))

    def col_q(self):
        return pl.BlockSpec((None, self.T, 1), lambda h, p, it, jt: (h, it[p], 0))

    def row_k(self):
        return pl.BlockSpec((None, 1, self.T), lambda h, p, it, jt: (h, 0, jt[p]))

    def scores(self, q, k_refs, cc_ref, cr_ref, masked):
        k = k_refs[0][...] if len(k_refs) == 1 else jnp.concatenate([r[...] for r in k_refs], axis=-1)
        s = lax.dot_general(q, k, (((1,), (1,)), ((), ())), preferred_element_type=F32) * self.scale
        if self.has_bias:
            s = s + cc_ref[...] - cr_ref[...]
        mask = _diag_mask(self.T, self.chunk_causal) if masked else None
        return s, k, mask


def _att_fwd(att, name):
    S, H, T = att.S, att.H, att.T
    it, jt, npairs = _pairs(att.nb, by_key=False)
    nk = len(att.ks)

    def body(it_ref, jt_ref, *refs):
        q_ref = refs[0]
        k_refs = refs[1:1 + nk]
        v_ref = refs[1 + nk]
        n = 2 + nk
        cc_ref = cr_ref = None
        if att.has_bias:
            cc_ref, cr_ref = refs[n], refs[n + 1]
            n += 2
        o_ref, lse_ref, m_ref, l_ref, acc_ref = refs[n:n + 5]
        p = pl.program_id(1)
        i, j = it_ref[p], jt_ref[p]

        @pl.when(j == 0)
        def _():
            m_ref[...] = jnp.full_like(m_ref, -jnp.inf)
            l_ref[...] = jnp.zeros_like(l_ref)
            acc_ref[...] = jnp.zeros_like(acc_ref)

        def step(masked):
            s, _, mask = att.scores(q_ref[...], k_refs, cc_ref, cr_ref, masked)
            if masked:
                s = jnp.where(mask, s, -jnp.inf)
            m_prev = m_ref[...]
            m_new = jnp.maximum(m_prev, jnp.max(s, axis=-1, keepdims=True))
            alpha = jnp.exp(m_prev - m_new)
            pr = jnp.exp(s - m_new)
            l_ref[...] = alpha * l_ref[...] + jnp.sum(pr, axis=-1, keepdims=True)
            acc_ref[...] = alpha * acc_ref[...] + jnp.dot(pr.astype(BF16), v_ref[...], preferred_element_type=F32)
            m_ref[...] = m_new

        @pl.when(j < i)
        def _():
            step(False)

        @pl.when(j == i)
        def _():
            step(True)
            l = l_ref[...]
            o_ref[...] = (acc_ref[...] / l).astype(o_ref.dtype)
            lse_ref[...] = m_ref[...] + jnp.log(l)

    in_specs = [att.q_spec(att.q)] + [att.k_spec(k) for k in att.ks] + [att.k_spec(att.v)]
    args = [att.q[0]] + [k[0] for k in att.ks] + [att.v[0]]
    if att.has_bias:
        in_specs += [att.col_q(), att.row_k()]
        args += [att.cum_col, att.cum_row]
    out_specs = [pl.BlockSpec((T, att.dv), lambda h, p, it, jt: (it[p], h)), att.col_q()]
    return pl.pallas_call(
        body, name=name,
        grid_spec=pltpu.PrefetchScalarGridSpec(
            num_scalar_prefetch=2, grid=(H, npairs), in_specs=in_specs, out_specs=out_specs,
            scratch_shapes=[pltpu.VMEM((T, 1), F32), pltpu.VMEM((T, 1), F32), pltpu.VMEM((T, att.dv), F32)]),
        out_shape=[jax.ShapeDtypeStruct((S, H * att.dv), BF16), jax.ShapeDtypeStruct((H, S, 1), F32)],
        compiler_params=_params(("parallel", "arbitrary")),
    )(it, jt, *args)


def _att_delta(do, o, n_heads, name):
    S = do.shape[0]
    w = do.shape[1] // n_heads
    ts = _tile(S, ATT_T, 8)

    def body(do_ref, o_ref, d_ref):
        d_ref[...] = jnp.sum(do_ref[...].astype(F32) * o_ref[...].astype(F32), axis=-1, keepdims=True)

    blk = pl.BlockSpec((ts, w), lambda i, h: (i, h))
    return pl.pallas_call(
        body, name=name, grid=(S // ts, n_heads),
        in_specs=[blk, blk], out_specs=pl.BlockSpec((None, ts, 1), lambda i, h: (h, i, 0)),
        out_shape=jax.ShapeDtypeStruct((n_heads, S, 1), F32),
        compiler_params=_params(("parallel", "parallel")),
    )(do, o)


def _att_probs(att, q, k_refs, v_ref, do_ref, lse_ref, dl_ref, cc_ref, cr_ref, masked):
    s, k, mask = att.scores(q, k_refs, cc_ref, cr_ref, masked)
    pr = jnp.exp(s - lse_ref[...])
    if masked:
        pr = jnp.where(mask, pr, 0.0)
    do = do_ref[...]
    dp = lax.dot_general(do, v_ref[...], (((1,), (1,)), ((), ())), preferred_element_type=F32)
    ds = pr * (dp - dl_ref[...])
    return pr, ds, k, do


def _att_bwd_q(att, do, lse, delta, out_dtype, name, with_rowsum=False):
    S, H, T = att.S, att.H, att.T
    it, jt, npairs = _pairs(att.nb, by_key=False)
    nk = len(att.ks)

    def body(it_ref, jt_ref, *refs):
        q_ref = refs[0]
        k_refs = refs[1:1 + nk]
        v_ref, do_ref, lse_ref, dl_ref = refs[1 + nk:5 + nk]
        n = 5 + nk
        cc_ref = cr_ref = None
        if att.has_bias:
            cc_ref, cr_ref = refs[n], refs[n + 1]
            n += 2
        dq_ref = refs[n]
        n += 1
        rs_ref = None
        if with_rowsum:
            rs_ref = refs[n]
            n += 1
        acc_ref = refs[n]
        rs_acc = refs[n + 1] if with_rowsum else None
        p = pl.program_id(1)
        i, j = it_ref[p], jt_ref[p]

        @pl.when(j == 0)
        def _():
            acc_ref[...] = jnp.zeros_like(acc_ref)
            if with_rowsum:
                rs_acc[...] = jnp.zeros_like(rs_acc)

        def step(masked):
            _, ds, k, _ = _att_probs(att, q_ref[...], k_refs, v_ref, do_ref, lse_ref, dl_ref, cc_ref, cr_ref, masked)
            acc_ref[...] += jnp.dot(ds.astype(BF16), k, preferred_element_type=F32)
            if with_rowsum:
                rs_acc[...] += jnp.sum(ds, axis=-1, keepdims=True)

        @pl.when(j < i)
        def _():
            step(False)

        @pl.when(j == i)
        def _():
            step(True)
            dq_ref[...] = (acc_ref[...] * att.scale).astype(dq_ref.dtype)
            if with_rowsum:
                rs_ref[...] = rs_acc[...]

    do_op = (do, att.dv, 0, True)
    in_specs = ([att.q_spec(att.q)] + [att.k_spec(k) for k in att.ks]
                + [att.k_spec(att.v), att.q_spec(do_op), att.col_q(), att.col_q()])
    args = [att.q[0]] + [k[0] for k in att.ks] + [att.v[0], do, lse, delta]
    if att.has_bias:
        in_specs += [att.col_q(), att.row_k()]
        args += [att.cum_col, att.cum_row]
    out_specs = [pl.BlockSpec((T, att.dq), lambda h, p, it, jt: (it[p], h))]
    out_shape = [jax.ShapeDtypeStruct((S, H * att.dq), out_dtype)]
    scratch = [pltpu.VMEM((T, att.dq), F32)]
    if with_rowsum:
        out_specs.append(att.col_q())
        out_shape.append(jax.ShapeDtypeStruct((H, S, 1), F32))
        scratch.append(pltpu.VMEM((T, 1), F32))
    outs = pl.pallas_call(
        body, name=name,
        grid_spec=pltpu.PrefetchScalarGridSpec(
            num_scalar_prefetch=2, grid=(H, npairs), in_specs=in_specs, out_specs=out_specs,
            scratch_shapes=scratch),
        out_shape=out_shape,
        compiler_params=_params(("parallel", "arbitrary")),
    )(it, jt, *args)
    return outs if with_rowsum else outs[0]


def _att_bwd_kv(att, do, lse, delta, dk_dtypes, name):
    S, H, T = att.S, att.H, att.T
    it, jt, npairs = _pairs(att.nb, by_key=True)
    nk = len(att.ks)
    last = att.nb - 1
    widths = [k[1] for k in att.ks]

    def body(it_ref, jt_ref, *refs):
        q_ref = refs[0]
        k_refs = refs[1:1 + nk]
        v_ref, do_ref, lse_ref, dl_ref = refs[1 + nk:5 + nk]
        n = 5 + nk
        cc_ref = cr_ref = None
        if att.has_bias:
            cc_ref, cr_ref = refs[n], refs[n + 1]
            n += 2
        dk_refs = refs[n:n + nk]
        dv_ref = refs[n + nk]
        n += nk + 1
        dc_ref = None
        if att.has_bias:
            dc_ref = refs[n]
            n += 1
        dk_acc, dv_acc = refs[n], refs[n + 1]
        dc_acc = refs[n + 2] if att.has_bias else None
        p = pl.program_id(1)
        i, j = it_ref[p], jt_ref[p]

        @pl.when(i == j)
        def _():
            dk_acc[...] = jnp.zeros_like(dk_acc)
            dv_acc[...] = jnp.zeros_like(dv_acc)
            if att.has_bias:
                dc_acc[...] = jnp.zeros_like(dc_acc)

        def step(masked):
            q = q_ref[...]
            pr, ds, _, do_v = _att_probs(att, q, k_refs, v_ref, do_ref, lse_ref, dl_ref, cc_ref, cr_ref, masked)
            tn = (((0,), (0,)), ((), ()))
            dv_acc[...] += lax.dot_general(pr.astype(BF16), do_v, tn, preferred_element_type=F32)
            dk_acc[...] += lax.dot_general(ds.astype(BF16), q, tn, preferred_element_type=F32)
            if att.has_bias:
                dc_acc[...] -= jnp.sum(ds, axis=0, keepdims=True)

        @pl.when(i > j)
        def _():
            step(False)

        @pl.when(i == j)
        def _():
            step(True)

        @pl.when(i == last)
        def _():
            dk = dk_acc[...] * att.scale
            off = 0
            for r, w in zip(dk_refs, widths):
                r[...] = dk[:, off:off + w].astype(r.dtype)
                off += w
            dv_ref[...] = dv_acc[...].astype(dv_ref.dtype)
            if att.has_bias:
                dc_ref[...] = dc_acc[...]

    do_op = (do, att.dv, 0, True)
    in_specs = ([att.q_spec(att.q)] + [att.k_spec(k) for k in att.ks]
                + [att.k_spec(att.v), att.q_spec(do_op), att.col_q(), att.col_q()])
    args = [att.q[0]] + [k[0] for k in att.ks] + [att.v[0], do, lse, delta]
    if att.has_bias:
        in_specs += [att.col_q(), att.row_k()]
        args += [att.cum_col, att.cum_row]
    out_specs = [pl.BlockSpec((T, w), lambda h, p, it, jt: (jt[p], h)) for w in widths]
    out_specs.append(pl.BlockSpec((T, att.dv), lambda h, p, it, jt: (jt[p], h)))
    out_shape = [jax.ShapeDtypeStruct((S, H * w), dt) for w, dt in zip(widths, dk_dtypes)]
    out_shape.append(jax.ShapeDtypeStruct((S, H * att.dv), BF16))
    scratch = [pltpu.VMEM((T, att.dq), F32), pltpu.VMEM((T, att.dv), F32)]
    if att.has_bias:
        out_specs.append(att.row_k())
        out_shape.append(jax.ShapeDtypeStruct((H, 1, S), F32))
        scratch.append(pltpu.VMEM((1, T), F32))
    return pl.pallas_call(
        body, name=name,
        grid_spec=pltpu.PrefetchScalarGridSpec(
            num_scalar_prefetch=2, grid=(H, npairs), in_specs=in_specs, out_specs=out_specs,
            scratch_shapes=scratch),
        out_shape=out_shape,
        compiler_params=_params(("parallel", "arbitrary")),
    )(it, jt, *args)


LOG2E = 1.4426950408889634
QSUB = 256

_NT = (((1,), (1,)), ((), ()))
_TN = (((0,), (0,)), ((), ()))


class _AttT:
    def __init__(self, S, n_heads, q, ks, v, scale, chunk_causal, cum_rep=None):
        self.S, self.H, self.q, self.ks, self.v = S, n_heads, q, ks, v
        self.scale, self.chunk_causal, self.cum_rep = scale, chunk_causal, cum_rep
        self.T = _tile(S, ATT_T)
        self.qs = min(QSUB, self.T)
        self.nb = S // self.T
        self.dq, self.dv = q[1], v[1]
        self.has_bias = cum_rep is not None

    def q_spec(self, op):
        _, w, off, per_head = op
        return pl.BlockSpec((self.T, w), lambda h, p, it, jt: (it[p], off + (h if per_head else 0)))

    def k_spec(self, op):
        _, w, off, per_head = op
        return pl.BlockSpec((self.T, w), lambda h, p, it, jt: (jt[p], off + (h if per_head else 0)))

    def row_q(self):
        return pl.BlockSpec((None, 1, self.T), lambda h, p, it, jt: (h, 0, it[p]))

    def cum_k(self):
        return pl.BlockSpec((None, self.T, self.qs), lambda h, p, it, jt: (h, jt[p], 0))

    def sub_blocks(self, masked):
        return [(q0, min(self.T, q0 + self.qs) if masked else self.T) for q0 in range(0, self.T, self.qs)]

    def scores(self, k, q_sub, cum, q0, masked):
        s = lax.dot_general(k, q_sub, _NT, preferred_element_type=F32) * (self.scale * LOG2E)
        if self.has_bias:
            s = s - cum
        mask = None
        if masked:
            r = lax.broadcasted_iota(jnp.int32, s.shape, 0)
            c = lax.broadcasted_iota(jnp.int32, s.shape, 1) + q0
            mask = (r // CHUNK <= c // CHUNK) if self.chunk_causal else (r <= c)
        return s, mask


def _join(k_refs):
    return k_refs[0][...] if len(k_refs) == 1 else jnp.concatenate([r[...] for r in k_refs], axis=-1)


def _att_fwd_t(att, name, exact=False):
    S, H, T, qs = att.S, att.H, att.T, att.qs
    it, jt, npairs = _pairs(att.nb, by_key=False)
    nk = len(att.ks)

    def body(it_ref, jt_ref, *refs):
        q_ref = refs[0]
        k_refs = refs[1:1 + nk]
        v_ref = refs[1 + nk]
        n = 2 + nk
        cum_ref = None
        if att.has_bias:
            cum_ref = refs[n]
            n += 1
        o_ref = refs[n]
        n += 1
        ox_ref = None
        if exact:
            ox_ref = refs[n]
            n += 1
        lse_ref, m_ref, l_ref, acc_ref = refs[n:n + 4]
        lo_ref = refs[n + 4] if exact else None
        p = pl.program_id(1)
        i, j = it_ref[p], jt_ref[p]

        @pl.when(j == 0)
        def _():
            m_ref[...] = jnp.full_like(m_ref, -jnp.inf)
            l_ref[...] = jnp.zeros_like(l_ref)
            acc_ref[...] = jnp.zeros_like(acc_ref)
            if exact:
                lo_ref[...] = jnp.zeros_like(lo_ref)

        def step(masked):
            k = _join(k_refs)
            v = v_ref[...]
            for q0, nkeys in att.sub_blocks(masked):
                qsl = slice(q0, q0 + qs)
                cum = cum_ref[0:nkeys, :] if att.has_bias else None
                s, mask = att.scores(k[0:nkeys], q_ref[qsl, :], cum, q0, masked)
                if masked:
                    s = jnp.where(mask, s, -jnp.inf)
                m_prev = m_ref[:, qsl]
                m_new = jnp.maximum(m_prev, jnp.max(s, axis=0, keepdims=True))
                alpha = jnp.exp2(m_prev - m_new)
                pr = jnp.exp2(s - m_new)
                l_ref[:, qsl] = alpha * l_ref[:, qsl] + jnp.sum(pr, axis=0, keepdims=True)
                p_hi = pr.astype(BF16)
                acc_ref[:, qsl] = alpha * acc_ref[:, qsl] + lax.dot_general(
                    v[0:nkeys], p_hi, _TN, preferred_element_type=F32)
                if exact:
                    p_lo = (pr - p_hi.astype(F32)).astype(BF16)
                    lo_ref[:, qsl] = alpha * lo_ref[:, qsl] + lax.dot_general(
                        v[0:nkeys], p_lo, _TN, preferred_element_type=F32)
                m_ref[:, qsl] = m_new

        @pl.when(j < i)
        def _():
            step(False)

        @pl.when(j == i)
        def _():
            step(True)
            l = l_ref[...]
            inv = 1.0 / l
            o_ref[...] = jnp.transpose(acc_ref[...] * inv).astype(o_ref.dtype)
            if exact:
                ox_ref[...] = jnp.transpose((acc_ref[...] + lo_ref[...]) * inv)
            lse_ref[...] = m_ref[...] + jnp.log2(l)

    in_specs = [att.q_spec(att.q)] + [att.k_spec(k) for k in att.ks] + [att.k_spec(att.v)]
    args = [att.q[0]] + [k[0] for k in att.ks] + [att.v[0]]
    if att.has_bias:
        in_specs.append(att.cum_k())
        args.append(att.cum_rep)
    o_spec = pl.BlockSpec((T, att.dv), lambda h, p, it, jt: (it[p], h))
    out_specs = [o_spec]
    out_shape = [jax.ShapeDtypeStruct((S, H * att.dv), BF16)]
    scratch = [pltpu.VMEM((1, T), F32), pltpu.VMEM((1, T), F32), pltpu.VMEM((att.dv, T), F32)]
    if exact:
        out_specs.append(o_spec)
        out_shape.append(jax.ShapeDtypeStruct((S, H * att.dv), F32))
        scratch.append(pltpu.VMEM((att.dv, T), F32))
    out_specs.append(att.row_q())
    out_shape.append(jax.ShapeDtypeStruct((H, 1, S), F32))
    return pl.pallas_call(
        body, name=name,
        grid_spec=pltpu.PrefetchScalarGridSpec(
            num_scalar_prefetch=2, grid=(H, npairs), in_specs=in_specs, out_specs=out_specs,
            scratch_shapes=scratch),
        out_shape=out_shape,
        compiler_params=_params(("parallel", "arbitrary")),
    )(it, jt, *args)


def _att_delta_t(do, o, n_heads, name):
    S = do.shape[0]
    w = do.shape[1] // n_heads
    ts = _tile(S, ATT_T)
    ones = jnp.ones((8, w), BF16)

    def body(do_ref, o_ref, ones_ref, d_ref):
        prod = do_ref[...].astype(F32) * o_ref[...].astype(F32)
        acc = jnp.zeros((8, ts), F32)
        for part in _split3(prod):
            acc = acc + lax.dot_general(ones_ref[...], part, _NT, preferred_element_type=F32)
        d_ref[...] = acc[0:1, :]

    blk = pl.BlockSpec((ts, w), lambda i, h: (i, h))
    return pl.pallas_call(
        body, name=name, grid=(S // ts, n_heads),
        in_specs=[blk, blk, pl.BlockSpec((8, w), lambda i, h: (0, 0))],
        out_specs=pl.BlockSpec((None, 1, ts), lambda i, h: (h, 0, i)),
        out_shape=jax.ShapeDtypeStruct((n_heads, 1, S), F32),
        compiler_params=_params(("parallel", "parallel")),
    )(do, o, ones)


def _att_bwd_t(att, do, lse, delta, dq_dtype, dk_dtypes, name):
    S, H, T, qs = att.S, att.H, att.T, att.qs
    it, jt, npairs = _pairs(att.nb, by_key=True)
    nk = len(att.ks)
    last = att.nb - 1
    widths = [k[1] for k in att.ks]

    def body(it_ref, jt_ref, *refs):
        q_ref = refs[0]
        k_refs = refs[1:1 + nk]
        v_ref, do_ref, lse_ref, dl_ref = refs[1 + nk:5 + nk]
        n = 5 + nk
        cum_ref = None
        if att.has_bias:
            cum_ref = refs[n]
            n += 1
        dq_ref = refs[n]
        dk_refs = refs[n + 1:n + 1 + nk]
        dv_ref = refs[n + 1 + nk]
        n += nk + 2
        dc_ref = None
        if att.has_bias:
            dc_ref = refs[n]
            n += 1
        dq_acc, dk_acc, dv_acc = refs[n:n + 3]
        dc_acc = refs[n + 3] if att.has_bias else None
        p = pl.program_id(1)
        i, j = it_ref[p], jt_ref[p]

        @pl.when(p == 0)
        def _():
            dq_acc[...] = jnp.zeros_like(dq_acc)

        @pl.when(i == j)
        def _():
            dk_acc[...] = jnp.zeros_like(dk_acc)
            dv_acc[...] = jnp.zeros_like(dv_acc)
            if att.has_bias:
                dc_acc[...] = jnp.zeros_like(dc_acc)

        def step(masked):
            k = _join(k_refs)
            v = v_ref[...]
            for q0, nkeys in att.sub_blocks(masked):
                qsl = slice(q0, q0 + qs)
                ksl = slice(0, nkeys)
                q_sub = q_ref[qsl, :]
                do_sub = do_ref[qsl, :]
                cum = cum_ref[ksl, :] if att.has_bias else None
                s, mask = att.scores(k[ksl], q_sub, cum, q0, masked)
                pr = jnp.exp2(s - lse_ref[:, qsl])
                if masked:
                    pr = jnp.where(mask, pr, 0.0)
                dp = lax.dot_general(v[ksl], do_sub, _NT, preferred_element_type=F32)
                ds = pr * (dp - dl_ref[:, qsl])
                ds_b = ds.astype(BF16)
                dv_acc[ksl, :] += jnp.dot(pr.astype(BF16), do_sub, preferred_element_type=F32)
                dk_acc[ksl, :] += jnp.dot(ds_b, q_sub, preferred_element_type=F32)
                dq_acc[i, :, qsl] += lax.dot_general(k[ksl], ds_b, _TN, preferred_element_type=F32)
                if att.has_bias:
                    part = ds[:, 0:LANE] if qs >= LANE else ds
                    for c0 in range(LANE, qs, LANE):
                        part = part + ds[:, c0:c0 + LANE]
                    dc_acc[ksl, :] += part

        @pl.when(i > j)
        def _():
            step(False)

        @pl.when(i == j)
        def _():
            step(True)
            dq_ref[...] = jnp.transpose(dq_acc[i] * att.scale).astype(dq_ref.dtype)

        @pl.when(i == last)
        def _():
            dk = dk_acc[...] * att.scale
            off = 0
            for r, w in zip(dk_refs, widths):
                r[...] = dk[:, off:off + w].astype(r.dtype)
                off += w
            dv_ref[...] = dv_acc[...].astype(dv_ref.dtype)
            if att.has_bias:
                dc_ref[...] = -jnp.sum(dc_acc[...], axis=-1, keepdims=True)

    do_op = (do, att.dv, 0, True)
    in_specs = ([att.q_spec(att.q)] + [att.k_spec(k) for k in att.ks]
                + [att.k_spec(att.v), att.q_spec(do_op), att.row_q(), att.row_q()])
    args = [att.q[0]] + [k[0] for k in att.ks] + [att.v[0], do, lse, delta]
    if att.has_bias:
        in_specs.append(att.cum_k())
        args.append(att.cum_rep)
    out_specs = [pl.BlockSpec((T, att.dq), lambda h, p, it, jt: (jt[p], h))]
    out_shape = [jax.ShapeDtypeStruct((S, H * att.dq), dq_dtype)]
    out_specs += [pl.BlockSpec((T, w), lambda h, p, it, jt: (jt[p], h)) for w in widths]
    out_shape += [jax.ShapeDtypeStruct((S, H * w), dt) for w, dt in zip(widths, dk_dtypes)]
    out_specs.append(pl.BlockSpec((T, att.dv), lambda h, p, it, jt: (jt[p], h)))
    out_shape.append(jax.ShapeDtypeStruct((S, H * att.dv), BF16))
    scratch = [pltpu.VMEM((att.nb, att.dq, T), F32), pltpu.VMEM((T, att.dq), F32), pltpu.VMEM((T, att.dv), F32)]
    if att.has_bias:
        out_specs.append(pl.BlockSpec((None, T, 1), lambda h, p, it, jt: (h, jt[p], 0)))
        out_shape.append(jax.ShapeDtypeStruct((H, S, 1), F32))
        scratch.append(pltpu.VMEM((T, min(qs, LANE)), F32))
    return pl.pallas_call(
        body, name=name,
        grid_spec=pltpu.PrefetchScalarGridSpec(
            num_scalar_prefetch=2, grid=(H, npairs), in_specs=in_specs, out_specs=out_specs,
            scratch_shapes=scratch),
        out_shape=out_shape,
        compiler_params=_params(("parallel", "arbitrary")),
    )(it, jt, *args)


def _adamw(w, g, m, v, name):
    R, C = w.shape
    tr = _tile(R, 256, 8)
    c1 = 1.0 - ADAM_B1 ** ADAM_STEP
    c2 = 1.0 - ADAM_B2 ** ADAM_STEP

    def body(w_ref, g_ref, m_ref, v_ref, d_ref, nm_ref, nv_ref):
        gv = g_ref[...]
        nm = ADAM_B1 * m_ref[...] + (1.0 - ADAM_B1) * gv
        nv = ADAM_B2 * v_ref[...] + (1.0 - ADAM_B2) * (gv * gv)
        d_ref[...] = -ADAM_LR * ((nm / c1) / (jnp.sqrt(nv / c2) + ADAM_EPS) + ADAM_WD * w_ref[...])
        nm_ref[...] = nm
        nv_ref[...] = nv

    blk = pl.BlockSpec((tr, C), lambda i: (i, 0))
    return pl.pallas_call(
        body, name=name, grid=(R // tr,),
        in_specs=[blk] * 4, out_specs=[blk] * 3,
        out_shape=[jax.ShapeDtypeStruct((R, C), F32)] * 3,
        compiler_params=_params(("parallel",)),
    )(w, g, m, v)


def _place():
    return lax.axis_index("x"), lax.axis_index("y"), lax.axis_index("c")


def _other_chips(x, y):
    return [(1 - x, y), (x, 1 - y), (1 - x, 1 - y)]


def _all_gather_shards(wp, name):
    R, C = wp.shape

    def body(w_ref, out_ref, send_sems, recv_sems, local_sem):
        x, y, c = _place()
        me = 2 * x + y
        mine = pltpu.make_async_copy(w_ref, out_ref.at[me], local_sem)
        mine.start()
        sends = []
        for n, (px, py) in enumerate(_other_chips(x, y)):
            cp = pltpu.make_async_remote_copy(
                src_ref=w_ref, dst_ref=out_ref.at[me], send_sem=send_sems.at[n], recv_sem=recv_sems.at[n],
                device_id=(px, py, c), device_id_type=MESH)
            cp.start()
            sends.append(cp)
        for n, (px, py) in enumerate(_other_chips(x, y)):
            pltpu.make_async_remote_copy(
                src_ref=w_ref, dst_ref=out_ref.at[2 * px + py], send_sem=send_sems.at[n],
                recv_sem=recv_sems.at[n], device_id=(px, py, c), device_id_type=MESH).wait_recv()
        for cp in sends:
            cp.wait_send()
        mine.wait()

    return pl.pallas_call(
        body, name=name,
        in_specs=[pl.BlockSpec(memory_space=pl.ANY)],
        out_specs=pl.BlockSpec(memory_space=pl.ANY),
        out_shape=jax.ShapeDtypeStruct((N_CHIPS, R, C), wp.dtype),
        scratch_shapes=[pltpu.SemaphoreType.DMA((3,)), pltpu.SemaphoreType.DMA((3,)), pltpu.SemaphoreType.DMA],
    )(wp)


def _reduce_scatter(gp, vec, name):
    _, R, C = gp.shape
    VR, W = vec.shape
    tr = _tile(R, PACK_ROWS, 16)
    nchunk = R // tr

    def body(gp_ref, vec_ref, out_ref, recv_ref, part_ref, sib_ref, vall_ref, vout_ref,
             buf_ref, acc_ref, send_sems, recv_sems, sib_sems, vsend_sems, vrecv_sems):
        x, y, c = _place()
        me = 2 * x + y
        chips = _other_chips(x, y)

        vall_ref[4 * x + 2 * y + c] = vec_ref[...]
        vsends = []
        for r in range(1, N_DEV):
            dx, dy, dc = (r >> 2) & 1, (r >> 1) & 1, r & 1
            peer = (x ^ dx, y ^ dy, c ^ dc)
            cp = pltpu.make_async_remote_copy(
                src_ref=vec_ref, dst_ref=vall_ref.at[4 * x + 2 * y + c], send_sem=vsend_sems.at[r - 1],
                recv_sem=vrecv_sems.at[r - 1], device_id=peer, device_id_type=MESH)
            cp.start()
            vsends.append(cp)
        sends = []
        for n, (px, py) in enumerate(chips):
            cp = pltpu.make_async_remote_copy(
                src_ref=gp_ref.at[2 * px + py], dst_ref=recv_ref.at[n], send_sem=send_sems.at[n],
                recv_sem=recv_sems.at[n], device_id=(px, py, c), device_id_type=MESH)
            cp.start()
            sends.append(cp)
        for n, (px, py) in enumerate(chips):
            pltpu.make_async_remote_copy(
                src_ref=gp_ref.at[me], dst_ref=recv_ref.at[n], send_sem=send_sems.at[n],
                recv_sem=recv_sems.at[n], device_id=(px, py, c), device_id_type=MESH).wait_recv()

        def sum_four(i, carry):
            rows = pl.ds(pl.multiple_of(i * tr, tr), tr)
            pltpu.sync_copy(gp_ref.at[me, rows], buf_ref.at[0])
            for n in range(3):
                pltpu.sync_copy(recv_ref.at[n, rows], buf_ref.at[n + 1])
            acc = buf_ref[0].astype(F32)
            for n in range(3):
                acc = acc + buf_ref[n + 1].astype(F32)
            acc_ref[0] = acc
            pltpu.sync_copy(acc_ref.at[0], part_ref.at[rows])
            return carry

        lax.fori_loop(0, nchunk, sum_four, 0)

        swap = pltpu.make_async_remote_copy(
            src_ref=part_ref, dst_ref=sib_ref, send_sem=sib_sems.at[0], recv_sem=sib_sems.at[1],
            device_id=(x, y, 1 - c), device_id_type=MESH)
        swap.start()
        swap.wait()

        def sum_two(i, carry):
            rows = pl.ds(pl.multiple_of(i * tr, tr), tr)
            pltpu.sync_copy(part_ref.at[rows], acc_ref.at[0])
            pltpu.sync_copy(sib_ref.at[rows], acc_ref.at[1])
            acc_ref[0] = acc_ref[0] + acc_ref[1]
            pltpu.sync_copy(acc_ref.at[0], out_ref.at[rows])
            return carry

        lax.fori_loop(0, nchunk, sum_two, 0)

        for r in range(1, N_DEV):
            dx, dy, dc = (r >> 2) & 1, (r >> 1) & 1, r & 1
            peer = (x ^ dx, y ^ dy, c ^ dc)
            pltpu.make_async_remote_copy(
                src_ref=vec_ref, dst_ref=vall_ref.at[4 * peer[0] + 2 * peer[1] + peer[2]],
                send_sem=vsend_sems.at[r - 1], recv_sem=vrecv_sems.at[r - 1],
                device_id=peer, device_id_type=MESH).wait_recv()
        total = vall_ref[0]
        for d in range(1, N_DEV):
            total = total + vall_ref[d]
        vout_ref[...] = total
        for cp in sends + vsends:
            cp.wait_send()

    hbm = pl.BlockSpec(memory_space=pl.ANY)
    vmem = pl.BlockSpec(memory_space=pltpu.VMEM)
    outs = pl.pallas_call(
        body, name=name,
        in_specs=[hbm, vmem],
        out_specs=[hbm, hbm, hbm, hbm, vmem, vmem],
        out_shape=[jax.ShapeDtypeStruct((R, C), F32), jax.ShapeDtypeStruct((3, R, C), gp.dtype),
                   jax.ShapeDtypeStruct((R, C), F32), jax.ShapeDtypeStruct((R, C), F32),
                   jax.ShapeDtypeStruct((N_DEV, VR, W), F32), jax.ShapeDtypeStruct((VR, W), F32)],
        scratch_shapes=[pltpu.VMEM((4, tr, C), gp.dtype), pltpu.VMEM((2, tr, C), F32),
                        pltpu.SemaphoreType.DMA((3,)), pltpu.SemaphoreType.DMA((3,)), pltpu.SemaphoreType.DMA((2,)),
                        pltpu.SemaphoreType.DMA((N_DEV - 1,)), pltpu.SemaphoreType.DMA((N_DEV - 1,))],
        compiler_params=pltpu.CompilerParams(vmem_limit_bytes=VMEM_LIMIT),
    )(gp, vec)
    return outs[0], outs[5]


def _rope_tables(S):
    pos = jnp.arange(S, dtype=F32)
    inv = 1.0 / (ROPE_THETA ** (jnp.arange(0, MLA_ROPE, 2, dtype=F32) / MLA_ROPE))
    ang = pos[:, None] * inv[None, :]
    cos, sin = jnp.cos(ang), jnp.sin(ang)
    half = MLA_ROPE // 2
    z = jnp.zeros((S, half), F32)
    one = jnp.ones((S, LANE - MLA_ROPE), F32)
    zero = jnp.zeros((S, LANE - MLA_ROPE), F32)
    kc = jnp.concatenate([cos, cos, one], axis=1)
    ksa = jnp.concatenate([-sin, z, zero], axis=1)
    ksb = jnp.concatenate([z, sin, zero], axis=1)
    qc = jnp.concatenate([jnp.ones((S, MLA_NOPE), F32), kc], axis=1)
    qsa = jnp.concatenate([jnp.zeros((S, MLA_NOPE), F32), ksa], axis=1)
    qsb = jnp.concatenate([jnp.zeros((S, MLA_NOPE), F32), ksb], axis=1)
    return (kc, ksa, ksb), (qc, qsa, qsb)


def _pad_cols(a, width):
    return jnp.pad(a, ((0, 0), (0, width - a.shape[1])))


def kernel(x, attn_norm, w_in, fox_f_bias, q_norm, w_uq, kv_norm, w_ukv, w_mla_branch, w_fox_branch, w_out, mlp_norm, w_up, w_down, final_norm, loss_target, m_attn_norm, m_w_in, m_fox_f_bias, m_q_norm, m_w_uq, m_kv_norm, m_w_ukv, m_w_mla_branch, m_w_fox_branch, m_w_out, m_mlp_norm, m_w_up, m_w_down, m_final_norm, v_attn_norm, v_w_in, v_fox_f_bias, v_q_norm, v_w_uq, v_kv_norm, v_w_ukv, v_w_mla_branch, v_w_fox_branch, v_w_out, v_mlp_norm, v_w_up, v_w_down, v_final_norm):
    _, S, D = x.shape
    H, HF = MLA_HEADS, FOX_HEADS
    QL, KVL = MLA_Q_LORA, MLA_KV_LORA
    assert H == HF and H <= 8
    xs = x[0]
    target = loss_target[0]

    big = [("w_in", w_in, 1), ("w_uq", w_uq, 1), ("w_ukv", w_ukv, 1), ("w_mla_branch", w_mla_branch, 1),
           ("w_fox_branch", w_fox_branch, 1), ("w_out", w_out, 0), ("w_up", w_up, 1), ("w_down", w_down, 0)]
    C = D
    rows, offs, off = {}, {}, 0
    for nm, w, _ in big:
        assert w[0].size % C == 0, nm
        rows[nm] = w[0].size // C
        offs[nm] = off
        off += -(-rows[nm] // 16) * 16

    def pad16(a, axis):
        short = -a.shape[axis] % 16
        return jnp.pad(a, [(0, short if d == axis else 0) for d in range(a.ndim)])

    R = -(-off // PACK_ROWS) * PACK_ROWS
    wp = jnp.concatenate([pad16(w[0].astype(BF16).reshape(-1, C), 0) for _, w, _ in big]
                         + [jnp.zeros((R - off, C), BF16)], axis=0)
    gathered = _all_gather_shards(wp, "all_gather_weights")
    full = {}
    for nm, w, axis in big:
        parts = [gathered[k, offs[nm]:offs[nm] + rows[nm]].reshape(w[0].shape) for k in range(N_CHIPS)]
        full[nm] = jnp.concatenate(parts, axis=axis)

    o_ckv = QL
    o_kr = o_ckv + KVL
    o_fq = o_kr + MLA_ROPE
    o_ff = o_fq + 3 * HF * FOX_HEAD_DIM
    o_g = o_ff + HF
    wi = full["w_in"]
    assert wi.shape[1] == o_g + 2 * D
    WS = QL + KVL + 2 * LANE
    NQKV = 3 * HF * FOX_HEAD_DIM
    w_small = jnp.concatenate([wi[:, :o_kr], _pad_cols(wi[:, o_kr:o_fq], LANE), _pad_cols(wi[:, o_ff:o_g], LANE)], axis=1)
    w_qkv = wi[:, o_fq:o_ff]
    w_g = wi[:, o_g:]
    w_pack = jnp.concatenate([w_small, w_qkv, w_g], axis=1)
    dqk = MLA_NOPE + MLA_ROPE
    w_uq_p = jnp.pad(full["w_uq"].reshape(QL, H, dqk), ((0, 0), (0, 0), (0, QPAD - dqk))).reshape(QL, H * QPAD)
    ukv = full["w_ukv"].reshape(KVL, H, MLA_NOPE + MLA_V)
    w_ukv_p = jnp.concatenate([ukv[:, :, :MLA_NOPE].reshape(KVL, H * MLA_NOPE),
                               ukv[:, :, MLA_NOPE:].reshape(KVL, H * MLA_V)], axis=1)
    w_mb, w_fb, w_o, w_u, w_d = (full[n] for n in ("w_mla_branch", "w_fox_branch", "w_out", "w_up", "w_down"))

    (kc, ksa, ksb), (qc, qsa, qsb) = _rope_tables(S)
    bias_pad = _pad_cols(fox_f_bias, LANE)

    xn = _norm_fwd(xs, attn_norm, "attn_norm_fwd")
    small = _matmul(xn, w_small, "nn", [F32], "proj_small")
    qkv = _matmul(xn, w_qkv, "nn", [BF16], "proj_qkv")
    gpre = _matmul(xn, w_g, "nn", [F32], "proj_gates")
    cqn, ckvn, kr, cum = _prep_fwd(small, q_norm, kv_norm, bias_pad, kc, ksa, ksb, HF, "prep_fwd")
    q_raw = _matmul(cqn, w_uq_p, "nn", [F32], "mla_q_up")
    q_rot = _rope_heads(q_raw, qc, qsa, qsb, 1, "mla_q_rope")
    kv2 = _matmul(ckvn, w_ukv_p, "nn", [BF16], "mla_kv_up")

    mla = _AttT(S, H, (q_rot, QPAD, 0, True), [(kv2, MLA_NOPE, 0, True), (kr, LANE, 0, False)],
                (kv2, MLA_V, H, True), 1.0 / math.sqrt(dqk), True)
    o_mla, lse_mla = _att_fwd_t(mla, "mla_att_fwd")

    cum_t = jnp.transpose(cum[:, :HF]) * LOG2E
    cum_rep = jnp.broadcast_to(cum_t[:, :, None], (HF, S, min(QSUB, _tile(S, ATT_T))))
    fox = _AttT(S, HF, (qkv, FOX_HEAD_DIM, 0, True), [(qkv, FOX_HEAD_DIM, HF, True)],
                (qkv, FOX_HEAD_DIM, 2 * HF, True), 1.0 / math.sqrt(FOX_HEAD_DIM), False, cum_rep)
    o_fox, ox_fox, lse_fox = _att_fwd_t(fox, "fox_att_fwd", exact=True)

    y_mla = _matmul(o_mla, w_mb, "nn", [F32], "mla_branch")
    y_fox = _matmul(o_fox, w_fb, "nn", [F32], "fox_branch")
    merged = _gate_fwd(gpre, y_mla, y_fox, "gate_fwd")
    h1 = _matmul(merged, w_o, "nn", [F32], "out_proj", extras=(xs,), epilogue=lambda acc, r: (acc + r,))
    hn = _norm_fwd(h1, mlp_norm, "mlp_norm_fwd")

    def relu2(acc):
        a = jnp.maximum(acc, 0.0)
        return a * a, a

    u, a_pos = _matmul(hn, w_u, "nn", [BF16, BF16], "mlp_up", epilogue=relu2)
    h2 = _matmul(u, w_d, "nn", [F32], "mlp_down", extras=(h1,), epilogue=lambda acc, r: (acc + r,))
    dh2, g_final, loss_part = _final(h2, final_norm.reshape(1, D), target, "final_norm_loss")

    dh2_b = dh2.astype(BF16)
    da = _matmul(dh2_b, w_d, "nt", [BF16], "mlp_down_dx", extras=(a_pos,),
                 epilogue=lambda acc, a: (acc * (2.0 * a.astype(F32)),))
    g_w_down = _mm_tn(u, dh2_b, "mlp_down_dw")
    dhn = _matmul(da, w_u, "nt", [F32], "mlp_up_dx")
    g_w_up = _mm_tn(hn, da, "mlp_up_dw")
    dh1, g_mlp_norm = _norm_bwd(h1, dhn, mlp_norm, dh2, "mlp_norm_bwd")
    dh1_b = dh1.astype(BF16)
    dmerged = _matmul(dh1_b, w_o, "nt", [F32], "out_proj_dx")
    g_w_out = _mm_tn(merged, dh1_b, "out_proj_dw")
    dy_mla, dy_fox, dg_mla, dg_fox = _gate_bwd(dmerged, gpre, y_mla, y_fox, "gate_bwd")
    do_mla = _matmul(dy_mla, w_mb, "nt", [BF16], "mla_branch_dx")
    g_w_mb = _mm_tn(o_mla, dy_mla, "mla_branch_dw")
    do_fox = _matmul(dy_fox, w_fb, "nt", [BF16], "fox_branch_dx")
    g_w_fb = _mm_tn(o_fox, dy_fox, "fox_branch_dw")

    delta_mla = _att_delta_t(do_mla, o_mla, H, "mla_att_delta")
    dq_raw, dk_nope, dkr_heads, dv_mla = _att_bwd_t(mla, do_mla, lse_mla, delta_mla, F32, [BF16, F32], "mla_att_bwd")
    delta_fox = _att_delta_t(do_fox, ox_fox, HF, "fox_att_delta")
    dfq, dfk, dfv, dcum = _att_bwd_t(fox, do_fox, lse_fox, delta_fox, BF16, [BF16], "fox_att_bwd")

    dq_rot = _rope_heads(dq_raw, qc, qsa, qsb, -1, "mla_q_rope_bwd")
    dcqn = _matmul(dq_rot, w_uq_p, "nt", [F32], "mla_q_up_dx")
    g_w_uq_p = _mm_tn(cqn, dq_rot, "mla_q_up_dw")
    dkv2 = jnp.concatenate([dk_nope, dv_mla], axis=1)
    dckvn = _matmul(dkv2, w_ukv_p, "nt", [F32], "mla_kv_up_dx")
    g_w_ukv_p = _mm_tn(ckvn, dkv2, "mla_kv_up_dw")

    dcum_rows = jnp.pad(dcum[:, :, 0], ((0, 8 - HF), (0, 0)))
    dlogf_rows = _suffix_sum_rows(dcum_rows, "fox_forget_suffix_sum")
    dlogf = _pad_cols(jnp.transpose(dlogf_rows[:HF]), LANE)
    d_small, g_q_norm, g_kv_norm, g_bias = _prep_bwd(
        small, dcqn, dckvn, dkr_heads, dlogf, q_norm, kv_norm, bias_pad, kc, ksa, ksb, H, "prep_bwd")
    dproj = jnp.concatenate([d_small, dfq, dfk, dfv, dg_mla, dg_fox], axis=1)
    dxn = _matmul(dproj, w_pack, "nt", [F32], "proj_dx")
    g_w_pack = _mm_tn(xn, dproj, "proj_dw")
    grad_x, g_attn_norm = _norm_bwd(xs, dxn, attn_norm, dh1, "attn_norm_bwd")

    gs, gq, gg = g_w_pack[:, :WS], g_w_pack[:, WS:WS + NQKV], g_w_pack[:, WS + NQKV:]
    g_w_in = jnp.concatenate([gs[:, :o_kr], gs[:, o_kr:o_kr + MLA_ROPE], gq,
                              gs[:, o_kr + LANE:o_kr + LANE + HF], gg], axis=1)
    g_w_uq = g_w_uq_p.reshape(QL, H, QPAD)[:, :, :dqk].reshape(QL, H * dqk)
    g_w_ukv = jnp.concatenate([g_w_ukv_p[:, :H * MLA_NOPE].reshape(KVL, H, MLA_NOPE),
                               g_w_ukv_p[:, H * MLA_NOPE:].reshape(KVL, H, MLA_V)], axis=2).reshape(KVL, -1)
    g_full = {"w_in": g_w_in, "w_uq": g_w_uq, "w_ukv": g_w_ukv, "w_mla_branch": g_w_mb, "w_fox_branch": g_w_fb,
              "w_out": g_w_out, "w_up": g_w_up, "w_down": g_w_down}

    slabs = []
    for nm, w, axis in big:
        g = g_full[nm]
        if axis == 1:
            k_dim, n = g.shape[0], g.shape[1] // N_CHIPS
            g4 = jnp.transpose(g.reshape(k_dim, N_CHIPS, n), (1, 0, 2))
        else:
            g4 = g.reshape(N_CHIPS, g.shape[0] // N_CHIPS, g.shape[1])
        slabs.append(pad16(g4.reshape(N_CHIPS, -1, C).astype(BF16), 1))
    slabs.append(jnp.zeros((N_CHIPS, R - off, C), BF16))
    gp = jnp.concatenate(slabs, axis=1)
    vec_w = max(D, LANE)
    vec_rows = [g_attn_norm, g_mlp_norm, g_final, g_q_norm, g_kv_norm, g_bias, loss_part]
    vec = jnp.concatenate([_pad_cols(v, vec_w) for v in vec_rows] + [jnp.zeros((1, vec_w), F32)], axis=0)
    g_shards, vsum = _reduce_scatter(gp, vec, "reduce_scatter_grads")

    moments = {"attn_norm": (m_attn_norm, v_attn_norm), "w_in": (m_w_in, v_w_in), "fox_f_bias": (m_fox_f_bias, v_fox_f_bias),
               "q_norm": (m_q_norm, v_q_norm), "w_uq": (m_w_uq, v_w_uq), "kv_norm": (m_kv_norm, v_kv_norm),
               "w_ukv": (m_w_ukv, v_w_ukv), "w_mla_branch": (m_w_mla_branch, v_w_mla_branch),
               "w_fox_branch": (m_w_fox_branch, v_w_fox_branch), "w_out": (m_w_out, v_w_out),
               "mlp_norm": (m_mlp_norm, v_mlp_norm), "w_up": (m_w_up, v_w_up), "w_down": (m_w_down, v_w_down),
               "final_norm": (m_final_norm, v_final_norm)}
    weights = {"attn_norm": attn_norm, "w_in": w_in, "fox_f_bias": fox_f_bias, "q_norm": q_norm, "w_uq": w_uq,
               "kv_norm": kv_norm, "w_ukv": w_ukv, "w_mla_branch": w_mla_branch, "w_fox_branch": w_fox_branch,
               "w_out": w_out, "mlp_norm": mlp_norm, "w_up": w_up, "w_down": w_down, "final_norm": final_norm}
    grads, deltas, new_m, new_v = {}, {}, {}, {}
    for nm, w, _ in big:
        shp = w[0].shape
        g = g_shards[offs[nm]:offs[nm] + rows[nm]].reshape(shp)
        d, nm_, nv_ = _adamw(w[0], g, moments[nm][0][0], moments[nm][1][0], "adamw_" + nm)
        grads[nm], deltas[nm], new_m[nm], new_v[nm] = g[None], d[None], nm_[None], nv_[None]
    vec_names = ["attn_norm", "mlp_norm", "final_norm", "q_norm", "kv_norm", "fox_f_bias"]

    def vec_pack(arrs):
        return jnp.concatenate([_pad_cols(a.reshape(1, -1), vec_w) for a in arrs]
                               + [jnp.zeros((2, vec_w), F32)], axis=0)

    vd, vm, vv = _adamw(vec_pack([weights[n] for n in vec_names]), vsum,
                        vec_pack([moments[n][0] for n in vec_names]), vec_pack([moments[n][1] for n in vec_names]),
                        "adamw_vectors")
    for r, nm in enumerate(vec_names):
        shp = weights[nm].shape
        n = weights[nm].size
        grads[nm] = vsum[r, :n].reshape(shp)
        deltas[nm], new_m[nm], new_v[nm] = vd[r, :n].reshape(shp), vm[r, :n].reshape(shp), vv[r, :n].reshape(shp)
    loss = vsum[6, 0]

    order = ["attn_norm", "w_in", "fox_f_bias", "q_norm", "w_uq", "kv_norm", "w_ukv", "w_mla_branch", "w_fox_branch",
             "w_out", "mlp_norm", "w_up", "w_down", "final_norm"]
    return (loss, grad_x[None], *[grads[n] for n in order], *[deltas[n] for n in order],
            *[new_m[n] for n in order], *[new_v[n] for n in order])
```

```python
import functools
import math

import jax
import jax.numpy as jnp
from jax import lax
from jax.experimental import pallas as pl
from jax.experimental.pallas import tpu as pltpu

CHUNK = 64
MLA_HEADS = 8
MLA_Q_LORA = 512
MLA_KV_LORA = 256
MLA_NOPE = 128
MLA_ROPE = 64
MLA_V = 128
ROPE_THETA = 10000.0
FOX_HEADS = 8
FOX_HEAD_DIM = 128
EPS = 1e-6

ADAM_LR = 0.001
ADAM_B1 = 0.9
ADAM_B2 = 0.999
ADAM_EPS = 1e-08
ADAM_WD = 0.01
ADAM_STEP = 10

LANE = 128
QPAD = 2 * LANE
N_CHIPS = 4
N_DEV = 8
VMEM_LIMIT = 48 * 1024 * 1024
ATT_T = 1024
ROW_T = 256
PACK_ROWS = 256

BF16 = jnp.bfloat16
F32 = jnp.float32
MESH = pl.DeviceIdType.MESH


def _tile(dim, pref, align=LANE):
    if dim <= pref:
        return dim
    t = (pref // align) * align
    while t >= align:
        if dim % t == 0:
            return t
        t -= align
    return dim


def _params(sem=None):
    return pltpu.CompilerParams(dimension_semantics=sem, vmem_limit_bytes=VMEM_LIMIT)


def _matmul(a, b, mode, out_dtypes, name, *, tm=1024, tn=512, tk=2048, extras=(), epilogue=None):
    if mode == "nn":
        (M, K), (K2, N) = a.shape, b.shape
    elif mode == "nt":
        (M, K), (N, K2) = a.shape, b.shape
    else:
        (K, M), (K2, N) = a.shape, b.shape
    assert K == K2, (name, a.shape, b.shape)
    tm, tn, tk = _tile(M, tm), _tile(N, tn), _tile(K, tk)
    nk = K // tk
    n_out = len(out_dtypes)
    n_ex = len(extras)

    def body(*refs):
        a_ref, b_ref = refs[0], refs[1]
        ex_refs = refs[2:2 + n_ex]
        o_refs = refs[2 + n_ex:2 + n_ex + n_out]
        acc_ref = refs[2 + n_ex + n_out]
        k = pl.program_id(2)
        if mode == "nn":
            dims = (((1,), (0,)), ((), ()))
        elif mode == "nt":
            dims = (((1,), (1,)), ((), ()))
        else:
            dims = (((0,), (0,)), ((), ()))
        part = lax.dot_general(a_ref[...], b_ref[...], dims, preferred_element_type=F32)

        @pl.when(k == 0)
        def _():
            acc_ref[...] = part

        @pl.when(k > 0)
        def _():
            acc_ref[...] += part

        @pl.when(k == nk - 1)
        def _():
            acc = acc_ref[...]
            if epilogue is None:
                outs = (acc,)
            else:
                outs = epilogue(acc, *[r[...] for r in ex_refs])
            for o_ref, o in zip(o_refs, outs):
                o_ref[...] = o.astype(o_ref.dtype)

    if mode == "nn":
        a_spec = pl.BlockSpec((tm, tk), lambda i, j, k: (i, k))
        b_spec = pl.BlockSpec((tk, tn), lambda i, j, k: (k, j))
    elif mode == "nt":
        a_spec = pl.BlockSpec((tm, tk), lambda i, j, k: (i, k))
        b_spec = pl.BlockSpec((tn, tk), lambda i, j, k: (j, k))
    else:
        a_spec = pl.BlockSpec((tk, tm), lambda i, j, k: (k, i))
        b_spec = pl.BlockSpec((tk, tn), lambda i, j, k: (k, j))
    mn_spec = pl.BlockSpec((tm, tn), lambda i, j, k: (i, j))
    outs = pl.pallas_call(
        body,
        name=name,
        grid=(M // tm, N // tn, nk),
        in_specs=[a_spec, b_spec] + [mn_spec] * n_ex,
        out_specs=[mn_spec] * n_out,
        out_shape=[jax.ShapeDtypeStruct((M, N), dt) for dt in out_dtypes],
        scratch_shapes=[pltpu.VMEM((tm, tn), F32)],
        compiler_params=_params(("parallel", "parallel", "arbitrary")),
    )(a, b, *extras)
    return outs[0] if n_out == 1 else outs


def _mm_tn(a, b, name):
    return _matmul(a, b, "tn", [F32], name, tm=1024, tn=1024, tk=2048)


def _row_spec(ts, width, col=0):
    return pl.BlockSpec((ts, width), lambda i: (i, col))


def _full_spec(shape):
    return pl.BlockSpec(shape, lambda i: tuple(0 for _ in shape))


def _rms(x):
    return lax.rsqrt(jnp.mean(x * x, axis=-1, keepdims=True) + EPS)


def _rms_bwd(x, dy, g):
    r = _rms(x)
    xh = x * r
    gy = dy * g
    dx = r * (gy - xh * jnp.mean(xh * gy, axis=-1, keepdims=True))
    return dx, dy * xh


def _norm_fwd(x, g, name, order=None):
    S, D = x.shape
    ts = _tile(S, ROW_T, 8)

    def body(x_ref, g_ref, *rest):
        o_ref = rest[-1]
        xv = x_ref[...]
        o_ref[...] = ((xv * _rms(xv)) * g_ref[...]).astype(BF16)

    extra = [] if order is None else [order]
    return pl.pallas_call(
        body, name=name, grid=(S // ts,),
        in_specs=[_row_spec(ts, D), _full_spec((1, D))] + [pl.BlockSpec(memory_space=pl.ANY)] * len(extra),
        out_specs=_row_spec(ts, D),
        out_shape=jax.ShapeDtypeStruct((S, D), BF16),
        compiler_params=_params(("parallel",)),
    )(x, g, *extra)


def _norm_bwd(x, dy, g, dres, name):
    S, D = x.shape
    ts = _tile(S, ROW_T, 8)

    def body(x_ref, dy_ref, g_ref, dres_ref, dx_ref, dg_ref):
        dx, dg_rows = _rms_bwd(x_ref[...], dy_ref[...], g_ref[...])
        dx_ref[...] = dres_ref[...] + dx

        @pl.when(pl.program_id(0) == 0)
        def _():
            dg_ref[...] = jnp.zeros_like(dg_ref)

        dg_ref[...] += jnp.sum(dg_rows, axis=0, keepdims=True)

    return pl.pallas_call(
        body, name=name, grid=(S // ts,),
        in_specs=[_row_spec(ts, D), _row_spec(ts, D), _full_spec((1, D)), _row_spec(ts, D)],
        out_specs=[_row_spec(ts, D), _full_spec((1, D))],
        out_shape=[jax.ShapeDtypeStruct((S, D), F32), jax.ShapeDtypeStruct((1, D), F32)],
        compiler_params=_params(("arbitrary",)),
    )(x, dy, g, dres)


def _rope(x, c, sa, sb, sign):
    w = x.shape[-1]
    half = MLA_ROPE // 2
    fwd = pltpu.roll(x, w - half, 1)
    back = pltpu.roll(x, half, 1)
    if sign < 0:
        return x * c - fwd * sa - back * sb
    return x * c + fwd * sa + back * sb


def _split3(x):
    hi = x.astype(BF16)
    r1 = x - hi.astype(F32)
    mid = r1.astype(BF16)
    lo = (r1 - mid.astype(F32)).astype(BF16)
    return hi, mid, lo


def _prep_fwd(small, q_norm, kv_norm, bias_pad, kc, ksa, ksb, n_heads, name):
    S, W = small.shape
    QL, KVL = q_norm.shape[1], kv_norm.shape[1]
    assert W == QL + KVL + 2 * LANE
    ts = _tile(S, ROW_T, 8)
    tri = (lax.broadcasted_iota(jnp.int32, (ts, ts), 0) >= lax.broadcasted_iota(jnp.int32, (ts, ts), 1)).astype(BF16)

    def body(s_ref, qn_ref, kvn_ref, b_ref, kc_ref, ksa_ref, ksb_ref, tri_ref,
             cqn_ref, ckvn_ref, kr_ref, cum_ref, carry_ref):
        cq = s_ref[:, 0:QL]
        cqn_ref[...] = ((cq * _rms(cq)) * qn_ref[...]).astype(BF16)
        ckv = s_ref[:, QL:QL + KVL]
        ckvn_ref[...] = ((ckv * _rms(ckv)) * kvn_ref[...]).astype(BF16)
        kr = s_ref[:, QL + KVL:QL + KVL + LANE]
        kr_ref[...] = _rope(kr, kc_ref[...], ksa_ref[...], ksb_ref[...], 1).astype(BF16)
        z = s_ref[:, QL + KVL + LANE:W] + b_ref[...]
        logf = jnp.minimum(z, 0.0) - jnp.log1p(jnp.exp(-jnp.abs(z)))
        lane = lax.broadcasted_iota(jnp.int32, logf.shape, 1)
        logf = jnp.where(lane < n_heads, logf, 0.0)

        @pl.when(pl.program_id(0) == 0)
        def _():
            carry_ref[...] = jnp.zeros_like(carry_ref)

        t = tri_ref[...]
        cum = carry_ref[...]
        for part in _split3(logf):
            cum = cum + jnp.dot(t, part, preferred_element_type=F32)
        cum_ref[...] = cum
        carry_ref[...] = cum[ts - 1:ts, :]

    return pl.pallas_call(
        body, name=name, grid=(S // ts,),
        in_specs=[_row_spec(ts, W), _full_spec((1, QL)), _full_spec((1, KVL)), _full_spec((1, LANE)),
                  _row_spec(ts, LANE), _row_spec(ts, LANE), _row_spec(ts, LANE), _full_spec((ts, ts))],
        out_specs=[_row_spec(ts, QL), _row_spec(ts, KVL), _row_spec(ts, LANE), _row_spec(ts, LANE)],
        out_shape=[jax.ShapeDtypeStruct((S, QL), BF16), jax.ShapeDtypeStruct((S, KVL), BF16),
                   jax.ShapeDtypeStruct((S, LANE), BF16), jax.ShapeDtypeStruct((S, LANE), F32)],
        scratch_shapes=[pltpu.VMEM((1, LANE), F32)],
        compiler_params=_params(("arbitrary",)),
    )(small, q_norm, kv_norm, bias_pad, kc, ksa, ksb, tri)


def _prep_bwd(small, dcqn, dckvn, dkr_heads, dlogf, q_norm, kv_norm, bias_pad, kc, ksa, ksb, n_heads, name):
    S, W = small.shape
    QL, KVL = q_norm.shape[1], kv_norm.shape[1]
    ts = _tile(S, ROW_T, 8)

    def body(s_ref, dcq_ref, dckv_ref, dkr_ref, dlf_ref, qn_ref, kvn_ref, b_ref, kc_ref, ksa_ref, ksb_ref,
             ds_ref, gq_ref, gkv_ref, gb_ref):
        dcq, gq_rows = _rms_bwd(s_ref[:, 0:QL], dcq_ref[...], qn_ref[...])
        ds_ref[:, 0:QL] = dcq.astype(BF16)
        dckv, gkv_rows = _rms_bwd(s_ref[:, QL:QL + KVL], dckv_ref[...], kvn_ref[...])
        ds_ref[:, QL:QL + KVL] = dckv.astype(BF16)
        dkr = dkr_ref[:, 0:LANE]
        for h in range(1, n_heads):
            dkr = dkr + dkr_ref[:, h * LANE:(h + 1) * LANE]
        ds_ref[:, QL + KVL:QL + KVL + LANE] = _rope(dkr, kc_ref[...], ksa_ref[...], ksb_ref[...], -1).astype(BF16)
        z = s_ref[:, QL + KVL + LANE:W] + b_ref[...]
        dff = dlf_ref[...] * (1.0 / (1.0 + jnp.exp(z)))
        ds_ref[:, QL + KVL + LANE:W] = dff.astype(BF16)

        @pl.when(pl.program_id(0) == 0)
        def _():
            gq_ref[...] = jnp.zeros_like(gq_ref)
            gkv_ref[...] = jnp.zeros_like(gkv_ref)
            gb_ref[...] = jnp.zeros_like(gb_ref)

        gq_ref[...] += jnp.sum(gq_rows, axis=0, keepdims=True)
        gkv_ref[...] += jnp.sum(gkv_rows, axis=0, keepdims=True)
        gb_ref[...] += jnp.sum(dff, axis=0, keepdims=True)

    return pl.pallas_call(
        body, name=name, grid=(S // ts,),
        in_specs=[_row_spec(ts, W), _row_spec(ts, QL), _row_spec(ts, KVL), _row_spec(ts, n_heads * LANE),
                  _row_spec(ts, LANE), _full_spec((1, QL)), _full_spec((1, KVL)), _full_spec((1, LANE)),
                  _row_spec(ts, LANE), _row_spec(ts, LANE), _row_spec(ts, LANE)],
        out_specs=[_row_spec(ts, W), _full_spec((1, QL)), _full_spec((1, KVL)), _full_spec((1, LANE))],
        out_shape=[jax.ShapeDtypeStruct((S, W), BF16), jax.ShapeDtypeStruct((1, QL), F32),
                   jax.ShapeDtypeStruct((1, KVL), F32), jax.ShapeDtypeStruct((1, LANE), F32)],
        compiler_params=_params(("arbitrary",)),
    )(small, dcqn, dckvn, dkr_heads, dlogf, q_norm, kv_norm, bias_pad, kc, ksa, ksb)


def _rope_heads(x, c, sa, sb, sign, name):
    S, W = x.shape
    nh = W // QPAD
    ts = _tile(S, ROW_T, 8)

    def body(x_ref, c_ref, sa_ref, sb_ref, o_ref):
        o_ref[...] = _rope(x_ref[...], c_ref[...], sa_ref[...], sb_ref[...], sign).astype(BF16)

    tab = pl.BlockSpec((ts, QPAD), lambda i, h: (i, 0))
    blk = pl.BlockSpec((ts, QPAD), lambda i, h: (i, h))
    return pl.pallas_call(
        body, name=name, grid=(S // ts, nh),
        in_specs=[blk, tab, tab, tab], out_specs=blk,
        out_shape=jax.ShapeDtypeStruct((S, W), BF16),
        compiler_params=_params(("parallel", "parallel")),
    )(x, c, sa, sb)


def _sigmoid(z):
    return 1.0 / (1.0 + jnp.exp(-z))


def _gate_fwd(gpre, y_mla, y_fox, name):
    S, D = y_mla.shape
    ts = _tile(S, ROW_T, 8)

    def body(ga_ref, gb_ref, ya_ref, yb_ref, o_ref):
        o_ref[...] = (_sigmoid(ga_ref[...]) * ya_ref[...] + _sigmoid(gb_ref[...]) * yb_ref[...]).astype(BF16)

    return pl.pallas_call(
        body, name=name, grid=(S // ts,),
        in_specs=[_row_spec(ts, D, 0), _row_spec(ts, D, 1), _row_spec(ts, D), _row_spec(ts, D)],
        out_specs=_row_spec(ts, D),
        out_shape=jax.ShapeDtypeStruct((S, D), BF16),
        compiler_params=_params(("parallel",)),
    )(gpre, gpre, y_mla, y_fox)


def _gate_bwd(dmerged, gpre, y_mla, y_fox, name):
    S, D = y_mla.shape
    ts = _tile(S, ROW_T, 8)

    def body(dm_ref, ga_ref, gb_ref, ya_ref, yb_ref, dya_ref, dyb_ref, dga_ref, dgb_ref):
        dm = dm_ref[...]
        ga = _sigmoid(ga_ref[...])
        gb = _sigmoid(gb_ref[...])
        dya_ref[...] = (dm * ga).astype(BF16)
        dyb_ref[...] = (dm * gb).astype(BF16)
        dga_ref[...] = (dm * ya_ref[...] * (ga * (1.0 - ga))).astype(BF16)
        dgb_ref[...] = (dm * yb_ref[...] * (gb * (1.0 - gb))).astype(BF16)

    return pl.pallas_call(
        body, name=name, grid=(S // ts,),
        in_specs=[_row_spec(ts, D), _row_spec(ts, D, 0), _row_spec(ts, D, 1), _row_spec(ts, D), _row_spec(ts, D)],
        out_specs=[_row_spec(ts, D)] * 4,
        out_shape=[jax.ShapeDtypeStruct((S, D), BF16)] * 4,
        compiler_params=_params(("parallel",)),
    )(dmerged, gpre, gpre, y_mla, y_fox)


def _final(h, g, target, name):
    S, D = h.shape
    ts = _tile(S, ROW_T, 8)

    def body(h_ref, g_ref, t_ref, dh_ref, dg_ref, loss_ref):
        hv = h_ref[...]
        gv = g_ref[...]
        err = (hv * _rms(hv)) * gv - t_ref[...]
        dh, dg_rows = _rms_bwd(hv, err / D, gv)
        dh_ref[...] = dh

        @pl.when(pl.program_id(0) == 0)
        def _():
            dg_ref[...] = jnp.zeros_like(dg_ref)
            loss_ref[...] = jnp.zeros_like(loss_ref)

        dg_ref[...] += jnp.sum(dg_rows, axis=0, keepdims=True)
        row_loss = jnp.mean(err * err, axis=-1, keepdims=True)
        loss_ref[...] += 0.5 * jnp.sum(row_loss, axis=0, keepdims=True)

    return pl.pallas_call(
        body, name=name, grid=(S // ts,),
        in_specs=[_row_spec(ts, D), _full_spec((1, D)), _row_spec(ts, D)],
        out_specs=[_row_spec(ts, D), _full_spec((1, D)), _full_spec((1, LANE))],
        out_shape=[jax.ShapeDtypeStruct((S, D), F32), jax.ShapeDtypeStruct((1, D), F32),
                   jax.ShapeDtypeStruct((1, LANE), F32)],
        compiler_params=_params(("arbitrary",)),
    )(h, g, target)


def _suffix_sum_rows(x, name):
    R, S = x.shape
    tb = _tile(S, 512)
    nb = S // tb
    tri = (lax.broadcasted_iota(jnp.int32, (tb, tb), 0) >= lax.broadcasted_iota(jnp.int32, (tb, tb), 1)).astype(BF16)

    def body(x_ref, tri_ref, o_ref, carry_ref):
        @pl.when(pl.program_id(0) == 0)
        def _():
            carry_ref[...] = jnp.zeros_like(carry_ref)

        xv = x_ref[...]
        t = tri_ref[...]
        acc = jnp.broadcast_to(carry_ref[:, 0:1], xv.shape)
        for part in _split3(xv):
            acc = acc + jnp.dot(part, t, preferred_element_type=F32)
        o_ref[...] = acc
        carry_ref[...] = jnp.broadcast_to(acc[:, 0:1], carry_ref.shape)

    rev = pl.BlockSpec((R, tb), lambda i: (0, nb - 1 - i))
    return pl.pallas_call(
        body, name=name, grid=(nb,),
        in_specs=[rev, _full_spec((tb, tb))], out_specs=rev,
        out_shape=jax.ShapeDtypeStruct((R, S), F32),
        scratch_shapes=[pltpu.VMEM((R, LANE), F32)],
        compiler_params=_params(("arbitrary",)),
    )(x, tri)


def _pairs(nb, by_key):
    if by_key:
        pr = [(i, j) for j in range(nb) for i in range(j, nb)]
    else:
        pr = [(i, j) for i in range(nb) for j in range(i + 1)]
    return (jnp.asarray([p[0] for p in pr], jnp.int32), jnp.asarray([p[1] for p in pr], jnp.int32), len(pr))


def _diag_mask(t, chunk_causal):
    r = lax.broadcasted_iota(jnp.int32, (t, t), 0)
    c = lax.broadcasted_iota(jnp.int32, (t, t), 1)
    if chunk_causal:
        return (c // CHUNK) <= (r // CHUNK)
    return c <= r


class _Att:
    def __init__(self, S, n_heads, q, ks, v, scale, chunk_causal, cum_col=None, cum_row=None):
        self.S, self.H, self.q, self.ks, self.v = S, n_heads, q, ks, v
        self.scale, self.chunk_causal = scale, chunk_causal
        self.cum_col, self.cum_row = cum_col, cum_row
        self.T = _tile(S, ATT_T)
        self.nb = S // self.T
        self.dq = q[1]
        self.dv = v[1]
        self.has_bias = cum_col is not None

    def q_spec(self, op):
        arr, w, off, per_head = op
        return pl.BlockSpec((self.T, w), lambda h, p, it, jt: (it[p], off + (h if per_head else 0)))

    def k_spec(self, op):
        arr, w, off, per_head = op
        return pl.BlockSpec((self.T, w), lambda h, p, it, jt: (jt[p], off + (h if per_head else 0)))

    def col_q(self):
        return pl.BlockSpec((None, self.T, 1), lambda h, p, it, jt: (h, it[p], 0))

    def row_k(self):
        return pl.BlockSpec((None, 1, self.T), lambda h, p, it, jt: (h, 0, jt[p]))

    def scores(self, q, k_refs, cc_ref, cr_ref, masked):
        k = k_refs[0][...] if len(k_refs) == 1 else jnp.concatenate([r[...] for r in k_refs], axis=-1)
        s = lax.dot_general(q, k, (((1,), (1,)), ((), ())), preferred_element_type=F32) * self.scale
        if self.has_bias:
            s = s + cc_ref[...] - cr_ref[...]
        mask = _diag_mask(self.T, self.chunk_causal) if masked else None
        return s, k, mask


def _att_fwd(att, name):
    S, H, T = att.S, att.H, att.T
    it, jt, npairs = _pairs(att.nb, by_key=False)
    nk = len(att.ks)

    def body(it_ref, jt_ref, *refs):
        q_ref = refs[0]
        k_refs = refs[1:1 + nk]
        v_ref = refs[1 + nk]
        n = 2 + nk
        cc_ref = cr_ref = None
        if att.has_bias:
            cc_ref, cr_ref = refs[n], refs[n + 1]
            n += 2
        o_ref, lse_ref, m_ref, l_ref, acc_ref = refs[n:n + 5]
        p = pl.program_id(1)
        i, j = it_ref[p], jt_ref[p]

        @pl.when(j == 0)
        def _():
            m_ref[...] = jnp.full_like(m_ref, -jnp.inf)
            l_ref[...] = jnp.zeros_like(l_ref)
            acc_ref[...] = jnp.zeros_like(acc_ref)

        def step(masked):
            s, _, mask = att.scores(q_ref[...], k_refs, cc_ref, cr_ref, masked)
            if masked:
                s = jnp.where(mask, s, -jnp.inf)
            m_prev = m_ref[...]
            m_new = jnp.maximum(m_prev, jnp.max(s, axis=-1, keepdims=True))
            alpha = jnp.exp(m_prev - m_new)
            pr = jnp.exp(s - m_new)
            l_ref[...] = alpha * l_ref[...] + jnp.sum(pr, axis=-1, keepdims=True)
            acc_ref[...] = alpha * acc_ref[...] + jnp.dot(pr.astype(BF16), v_ref[...], preferred_element_type=F32)
            m_ref[...] = m_new

        @pl.when(j < i)
        def _():
            step(False)

        @pl.when(j == i)
        def _():
            step(True)
            l = l_ref[...]
            o_ref[...] = (acc_ref[...] / l).astype(o_ref.dtype)
            lse_ref[...] = m_ref[...] + jnp.log(l)

    in_specs = [att.q_spec(att.q)] + [att.k_spec(k) for k in att.ks] + [att.k_spec(att.v)]
    args = [att.q[0]] + [k[0] for k in att.ks] + [att.v[0]]
    if att.has_bias:
        in_specs += [att.col_q(), att.row_k()]
        args += [att.cum_col, att.cum_row]
    out_specs = [pl.BlockSpec((T, att.dv), lambda h, p, it, jt: (it[p], h)), att.col_q()]
    return pl.pallas_call(
        body, name=name,
        grid_spec=pltpu.PrefetchScalarGridSpec(
            num_scalar_prefetch=2, grid=(H, npairs), in_specs=in_specs, out_specs=out_specs,
            scratch_shapes=[pltpu.VMEM((T, 1), F32), pltpu.VMEM((T, 1), F32), pltpu.VMEM((T, att.dv), F32)]),
        out_shape=[jax.ShapeDtypeStruct((S, H * att.dv), BF16), jax.ShapeDtypeStruct((H, S, 1), F32)],
        compiler_params=_params(("parallel", "arbitrary")),
    )(it, jt, *args)


def _att_delta(do, o, n_heads, name):
    S = do.shape[0]
    w = do.shape[1] // n_heads
    ts = _tile(S, ATT_T, 8)

    def body(do_ref, o_ref, d_ref):
        d_ref[...] = jnp.sum(do_ref[...].astype(F32) * o_ref[...].astype(F32), axis=-1, keepdims=True)

    blk = pl.BlockSpec((ts, w), lambda i, h: (i, h))
    return pl.pallas_call(
        body, name=name, grid=(S // ts, n_heads),
        in_specs=[blk, blk], out_specs=pl.BlockSpec((None, ts, 1), lambda i, h: (h, i, 0)),
        out_shape=jax.ShapeDtypeStruct((n_heads, S, 1), F32),
        compiler_params=_params(("parallel", "parallel")),
    )(do, o)


def _att_probs(att, q, k_refs, v_ref, do_ref, lse_ref, dl_ref, cc_ref, cr_ref, masked):
    s, k, mask = att.scores(q, k_refs, cc_ref, cr_ref, masked)
    pr = jnp.exp(s - lse_ref[...])
    if masked:
        pr = jnp.where(mask, pr, 0.0)
    do = do_ref[...]
    dp = lax.dot_general(do, v_ref[...], (((1,), (1,)), ((), ())), preferred_element_type=F32)
    ds = pr * (dp - dl_ref[...])
    return pr, ds, k, do


def _att_bwd_q(att, do, lse, delta, out_dtype, name, with_rowsum=False):
    S, H, T = att.S, att.H, att.T
    it, jt, npairs = _pairs(att.nb, by_key=False)
    nk = len(att.ks)

    def body(it_ref, jt_ref, *refs):
        q_ref = refs[0]
        k_refs = refs[1:1 + nk]
        v_ref, do_ref, lse_ref, dl_ref = refs[1 + nk:5 + nk]
        n = 5 + nk
        cc_ref = cr_ref = None
        if att.has_bias:
            cc_ref, cr_ref = refs[n], refs[n + 1]
            n += 2
        dq_ref = refs[n]
        n += 1
        rs_ref = None
        if with_rowsum:
            rs_ref = refs[n]
            n += 1
        acc_ref = refs[n]
        rs_acc = refs[n + 1] if with_rowsum else None
        p = pl.program_id(1)
        i, j = it_ref[p], jt_ref[p]

        @pl.when(j == 0)
        def _():
            acc_ref[...] = jnp.zeros_like(acc_ref)
            if with_rowsum:
                rs_acc[...] = jnp.zeros_like(rs_acc)

        def step(masked):
            _, ds, k, _ = _att_probs(att, q_ref[...], k_refs, v_ref, do_ref, lse_ref, dl_ref, cc_ref, cr_ref, masked)
            acc_ref[...] += jnp.dot(ds.astype(BF16), k, preferred_element_type=F32)
            if with_rowsum:
                rs_acc[...] += jnp.sum(ds, axis=-1, keepdims=True)

        @pl.when(j < i)
        def _():
            step(False)

        @pl.when(j == i)
        def _():
            step(True)
            dq_ref[...] = (acc_ref[...] * att.scale).astype(dq_ref.dtype)
            if with_rowsum:
                rs_ref[...] = rs_acc[...]

    do_op = (do, att.dv, 0, True)
    in_specs = ([att.q_spec(att.q)] + [att.k_spec(k) for k in att.ks]
                + [att.k_spec(att.v), att.q_spec(do_op), att.col_q(), att.col_q()])
    args = [att.q[0]] + [k[0] for k in att.ks] + [att.v[0], do, lse, delta]
    if att.has_bias:
        in_specs += [att.col_q(), att.row_k()]
        args += [att.cum_col, att.cum_row]
    out_specs = [pl.BlockSpec((T, att.dq), lambda h, p, it, jt: (it[p], h))]
    out_shape = [jax.ShapeDtypeStruct((S, H * att.dq), out_dtype)]
    scratch = [pltpu.VMEM((T, att.dq), F32)]
    if with_rowsum:
        out_specs.append(att.col_q())
        out_shape.append(jax.ShapeDtypeStruct((H, S, 1), F32))
        scratch.append(pltpu.VMEM((T, 1), F32))
    outs = pl.pallas_call(
        body, name=name,
        grid_spec=pltpu.PrefetchScalarGridSpec(
            num_scalar_prefetch=2, grid=(H, npairs), in_specs=in_specs, out_specs=out_specs,
            scratch_shapes=scratch),
        out_shape=out_shape,
        compiler_params=_params(("parallel", "arbitrary")),
    )(it, jt, *args)
    return outs if with_rowsum else outs[0]


def _att_bwd_kv(att, do, lse, delta, dk_dtypes, name):
    S, H, T = att.S, att.H, att.T
    it, jt, npairs = _pairs(att.nb, by_key=True)
    nk = len(att.ks)
    last = att.nb - 1
    widths = [k[1] for k in att.ks]

    def body(it_ref, jt_ref, *refs):
        q_ref = refs[0]
        k_refs = refs[1:1 + nk]
        v_ref, do_ref, lse_ref, dl_ref = refs[1 + nk:5 + nk]
        n = 5 + nk
        cc_ref = cr_ref = None
        if att.has_bias:
            cc_ref, cr_ref = refs[n], refs[n + 1]
            n += 2
        dk_refs = refs[n:n + nk]
        dv_ref = refs[n + nk]
        n += nk + 1
        dc_ref = None
        if att.has_bias:
            dc_ref = refs[n]
            n += 1
        dk_acc, dv_acc = refs[n], refs[n + 1]
        dc_acc = refs[n + 2] if att.has_bias else None
        p = pl.program_id(1)
        i, j = it_ref[p], jt_ref[p]

        @pl.when(i == j)
        def _():
            dk_acc[...] = jnp.zeros_like(dk_acc)
            dv_acc[...] = jnp.zeros_like(dv_acc)
            if att.has_bias:
                dc_acc[...] = jnp.zeros_like(dc_acc)

        def step(masked):
            q = q_ref[...]
            pr, ds, _, do_v = _att_probs(att, q, k_refs, v_ref, do_ref, lse_ref, dl_ref, cc_ref, cr_ref, masked)
            tn = (((0,), (0,)), ((), ()))
            dv_acc[...] += lax.dot_general(pr.astype(BF16), do_v, tn, preferred_element_type=F32)
            dk_acc[...] += lax.dot_general(ds.astype(BF16), q, tn, preferred_element_type=F32)
            if att.has_bias:
                dc_acc[...] -= jnp.sum(ds, axis=0, keepdims=True)

        @pl.when(i > j)
        def _():
            step(False)

        @pl.when(i == j)
        def _():
            step(True)

        @pl.when(i == last)
        def _():
            dk = dk_acc[...] * att.scale
            off = 0
            for r, w in zip(dk_refs, widths):
                r[...] = dk[:, off:off + w].astype(r.dtype)
                off += w
            dv_ref[...] = dv_acc[...].astype(dv_ref.dtype)
            if att.has_bias:
                dc_ref[...] = dc_acc[...]

    do_op = (do, att.dv, 0, True)
    in_specs = ([att.q_spec(att.q)] + [att.k_spec(k) for k in att.ks]
                + [att.k_spec(att.v), att.q_spec(do_op), att.col_q(), att.col_q()])
    args = [att.q[0]] + [k[0] for k in att.ks] + [att.v[0], do, lse, delta]
    if att.has_bias:
        in_specs += [att.col_q(), att.row_k()]
        args += [att.cum_col, att.cum_row]
    out_specs = [pl.BlockSpec((T, w), lambda h, p, it, jt: (jt[p], h)) for w in widths]
    out_specs.append(pl.BlockSpec((T, att.dv), lambda h, p, it, jt: (jt[p], h)))
    out_shape = [jax.ShapeDtypeStruct((S, H * w), dt) for w, dt in zip(widths, dk_dtypes)]
    out_shape.append(jax.ShapeDtypeStruct((S, H * att.dv), BF16))
    scratch = [pltpu.VMEM((T, att.dq), F32), pltpu.VMEM((T, att.dv), F32)]
    if att.has_bias:
        out_specs.append(att.row_k())
        out_shape.append(jax.ShapeDtypeStruct((H, 1, S), F32))
        scratch.append(pltpu.VMEM((1, T), F32))
    return pl.pallas_call(
        body, name=name,
        grid_spec=pltpu.PrefetchScalarGridSpec(
            num_scalar_prefetch=2, grid=(H, npairs), in_specs=in_specs, out_specs=out_specs,
            scratch_shapes=scratch),
        out_shape=out_shape,
        compiler_params=_params(("parallel", "arbitrary")),
    )(it, jt, *args)


LOG2E = 1.4426950408889634
QSUB = 256

_NT = (((1,), (1,)), ((), ()))
_TN = (((0,), (0,)), ((), ()))


class _AttT:
    def __init__(self, S, n_heads, q, ks, v, scale, chunk_causal, cum_rep=None):
        self.S, self.H, self.q, self.ks, self.v = S, n_heads, q, ks, v
        self.scale, self.chunk_causal, self.cum_rep = scale, chunk_causal, cum_rep
        self.T = _tile(S, ATT_T)
        self.qs = min(QSUB, self.T)
        self.nb = S // self.T
        self.dq, self.dv = q[1], v[1]
        self.has_bias = cum_rep is not None

    def q_spec(self, op):
        _, w, off, per_head = op
        return pl.BlockSpec((self.T, w), lambda h, p, it, jt: (it[p], off + (h if per_head else 0)))

    def k_spec(self, op):
        _, w, off, per_head = op
        return pl.BlockSpec((self.T, w), lambda h, p, it, jt: (jt[p], off + (h if per_head else 0)))

    def row_q(self):
        return pl.BlockSpec((None, 1, self.T), lambda h, p, it, jt: (h, 0, it[p]))

    def cum_k(self):
        return pl.BlockSpec((None, self.T, self.qs), lambda h, p, it, jt: (h, jt[p], 0))

    def sub_blocks(self, masked):
        return [(q0, min(self.T, q0 + self.qs) if masked else self.T) for q0 in range(0, self.T, self.qs)]

    def scores(self, k, q_sub, cum, q0, masked):
        s = lax.dot_general(k, q_sub, _NT, preferred_element_type=F32) * (self.scale * LOG2E)
        if self.has_bias:
            s = s - cum
        mask = None
        if masked:
            r = lax.broadcasted_iota(jnp.int32, s.shape, 0)
            c = lax.broadcasted_iota(jnp.int32, s.shape, 1) + q0
            mask = (r // CHUNK <= c // CHUNK) if self.chunk_causal else (r <= c)
        return s, mask


def _join(k_refs):
    return k_refs[0][...] if len(k_refs) == 1 else jnp.concatenate([r[...] for r in k_refs], axis=-1)


def _att_fwd_t(att, name, exact=False):
    S, H, T, qs = att.S, att.H, att.T, att.qs
    it, jt, npairs = _pairs(att.nb, by_key=False)
    nk = len(att.ks)

    def body(it_ref, jt_ref, *refs):
        q_ref = refs[0]
        k_refs = refs[1:1 + nk]
        v_ref = refs[1 + nk]
        n = 2 + nk
        cum_ref = None
        if att.has_bias:
            cum_ref = refs[n]
            n += 1
        o_ref = refs[n]
        n += 1
        ox_ref = None
        if exact:
            ox_ref = refs[n]
            n += 1
        lse_ref, m_ref, l_ref, acc_ref = refs[n:n + 4]
        lo_ref = refs[n + 4] if exact else None
        p = pl.program_id(1)
        i, j = it_ref[p], jt_ref[p]

        @pl.when(j == 0)
        def _():
            m_ref[...] = jnp.full_like(m_ref, -jnp.inf)
            l_ref[...] = jnp.zeros_like(l_ref)
            acc_ref[...] = jnp.zeros_like(acc_ref)
            if exact:
                lo_ref[...] = jnp.zeros_like(lo_ref)

        def step(masked):
            k = _join(k_refs)
            v = v_ref[...]
            subs = att.sub_blocks(masked)

            def logits(idx):
                q0, nkeys = subs[idx]
                cum = cum_ref[0:nkeys, :] if att.has_bias else None
                return att.scores(k[0:nkeys], q_ref[q0:q0 + qs, :], cum, q0, masked)

            ahead = logits(0)
            for idx, (q0, nkeys) in enumerate(subs):
                qsl = slice(q0, q0 + qs)
                s, mask = ahead
                if idx + 1 < len(subs):
                    ahead = logits(idx + 1)
                if masked:
                    s = jnp.where(mask, s, -jnp.inf)
                m_prev = m_ref[:, qsl]
                m_new = jnp.maximum(m_prev, jnp.max(s, axis=0, keepdims=True))
                alpha = jnp.exp2(m_prev - m_new)
                pr = jnp.exp2(s - m_new)
                l_ref[:, qsl] = alpha * l_ref[:, qsl] + jnp.sum(pr, axis=0, keepdims=True)
                p_hi = pr.astype(BF16)
                acc_ref[:, qsl] = alpha * acc_ref[:, qsl] + lax.dot_general(
                    v[0:nkeys], p_hi, _TN, preferred_element_type=F32)
                if exact:
                    p_lo = (pr - p_hi.astype(F32)).astype(BF16)
                    lo_ref[:, qsl] = alpha * lo_ref[:, qsl] + lax.dot_general(
                        v[0:nkeys], p_lo, _TN, preferred_element_type=F32)
                m_ref[:, qsl] = m_new

        @pl.when(j < i)
        def _():
            step(False)

        @pl.when(j == i)
        def _():
            step(True)
            l = l_ref[...]
            inv = 1.0 / l
            o_ref[...] = jnp.transpose(acc_ref[...] * inv).astype(o_ref.dtype)
            if exact:
                ox_ref[...] = jnp.transpose((acc_ref[...] + lo_ref[...]) * inv)
            lse_ref[...] = m_ref[...] + jnp.log2(l)

    in_specs = [att.q_spec(att.q)] + [att.k_spec(k) for k in att.ks] + [att.k_spec(att.v)]
    args = [att.q[0]] + [k[0] for k in att.ks] + [att.v[0]]
    if att.has_bias:
        in_specs.append(att.cum_k())
        args.append(att.cum_rep)
    o_spec = pl.BlockSpec((T, att.dv), lambda h, p, it, jt: (it[p], h))
    out_specs = [o_spec]
    out_shape = [jax.ShapeDtypeStruct((S, H * att.dv), BF16)]
    scratch = [pltpu.VMEM((1, T), F32), pltpu.VMEM((1, T), F32), pltpu.VMEM((att.dv, T), F32)]
    if exact:
        out_specs.append(o_spec)
        out_shape.append(jax.ShapeDtypeStruct((S, H * att.dv), F32))
        scratch.append(pltpu.VMEM((att.dv, T), F32))
    out_specs.append(att.row_q())
    out_shape.append(jax.ShapeDtypeStruct((H, 1, S), F32))
    return pl.pallas_call(
        body, name=name,
        grid_spec=pltpu.PrefetchScalarGridSpec(
            num_scalar_prefetch=2, grid=(H, npairs), in_specs=in_specs, out_specs=out_specs,
            scratch_shapes=scratch),
        out_shape=out_shape,
        compiler_params=_params(("parallel", "arbitrary")),
    )(it, jt, *args)


def _att_delta_t(do, o, n_heads, name, order=None):
    S = do.shape[0]
    w = do.shape[1] // n_heads
    ts = _tile(S, ATT_T)
    ones = jnp.ones((8, w), BF16)
    extra = [] if order is None else [order]

    def body(do_ref, o_ref, ones_ref, *rest):
        d_ref = rest[-1]
        prod = do_ref[...].astype(F32) * o_ref[...].astype(F32)
        acc = jnp.zeros((8, ts), F32)
        for part in _split3(prod):
            acc = acc + lax.dot_general(ones_ref[...], part, _NT, preferred_element_type=F32)
        d_ref[...] = acc[0:1, :]

    blk = pl.BlockSpec((ts, w), lambda i, h: (i, h))
    return pl.pallas_call(
        body, name=name, grid=(S // ts, n_heads),
        in_specs=[blk, blk, pl.BlockSpec((8, w), lambda i, h: (0, 0))] + [pl.BlockSpec(memory_space=pl.ANY)] * len(extra),
        out_specs=pl.BlockSpec((None, 1, ts), lambda i, h: (h, 0, i)),
        out_shape=jax.ShapeDtypeStruct((n_heads, 1, S), F32),
        compiler_params=_params(("parallel", "parallel")),
    )(do, o, ones, *extra)


def _att_bwd_t(att, do, lse, delta, dq_dtype, dk_dtypes, name):
    S, H, T, qs = att.S, att.H, att.T, att.qs
    it, jt, npairs = _pairs(att.nb, by_key=True)
    nk = len(att.ks)
    last = att.nb - 1
    widths = [k[1] for k in att.ks]

    def body(it_ref, jt_ref, *refs):
        q_ref = refs[0]
        k_refs = refs[1:1 + nk]
        v_ref, do_ref, lse_ref, dl_ref = refs[1 + nk:5 + nk]
        n = 5 + nk
        cum_ref = None
        if att.has_bias:
            cum_ref = refs[n]
            n += 1
        dq_ref = refs[n]
        dk_refs = refs[n + 1:n + 1 + nk]
        dv_ref = refs[n + 1 + nk]
        n += nk + 2
        dc_ref = None
        if att.has_bias:
            dc_ref = refs[n]
            n += 1
        dq_acc, dk_acc, dv_acc = refs[n:n + 3]
        dc_acc = refs[n + 3] if att.has_bias else None
        p = pl.program_id(1)
        i, j = it_ref[p], jt_ref[p]

        @pl.when(p == 0)
        def _():
            dq_acc[...] = jnp.zeros_like(dq_acc)

        @pl.when(i == j)
        def _():
            dk_acc[...] = jnp.zeros_like(dk_acc)
            dv_acc[...] = jnp.zeros_like(dv_acc)
            if att.has_bias:
                dc_acc[...] = jnp.zeros_like(dc_acc)

        def step(masked):
            k = _join(k_refs)
            v = v_ref[...]
            subs = att.sub_blocks(masked)

            def logits(idx):
                q0, nkeys = subs[idx]
                cum = cum_ref[0:nkeys, :] if att.has_bias else None
                return att.scores(k[0:nkeys], q_ref[q0:q0 + qs, :], cum, q0, masked)

            ahead = logits(0)
            for idx, (q0, nkeys) in enumerate(subs):
                qsl = slice(q0, q0 + qs)
                ksl = slice(0, nkeys)
                q_sub = q_ref[qsl, :]
                do_sub = do_ref[qsl, :]
                s, mask = ahead
                if idx + 1 < len(subs):
                    ahead = logits(idx + 1)
                pr = jnp.exp2(s - lse_ref[:, qsl])
                if masked:
                    pr = jnp.where(mask, pr, 0.0)
                dp = lax.dot_general(v[ksl], do_sub, _NT, preferred_element_type=F32)
                ds = pr * (dp - dl_ref[:, qsl])
                ds_b = ds.astype(BF16)
                dv_acc[ksl, :] += jnp.dot(pr.astype(BF16), do_sub, preferred_element_type=F32)
                dk_acc[ksl, :] += jnp.dot(ds_b, q_sub, preferred_element_type=F32)
                dq_acc[i, :, qsl] += lax.dot_general(k[ksl], ds_b, _TN, preferred_element_type=F32)
                if att.has_bias:
                    part = ds[:, 0:LANE] if qs >= LANE else ds
                    for c0 in range(LANE, qs, LANE):
                        part = part + ds[:, c0:c0 + LANE]
                    dc_acc[ksl, :] += part

        @pl.when(i > j)
        def _():
            step(False)

        @pl.when(i == j)
        def _():
            step(True)
            dq_ref[...] = jnp.transpose(dq_acc[i] * att.scale).astype(dq_ref.dtype)

        @pl.when(i == last)
        def _():
            dk = dk_acc[...] * att.scale
            off = 0
            for r, w in zip(dk_refs, widths):
                r[...] = dk[:, off:off + w].astype(r.dtype)
                off += w
            dv_ref[...] = dv_acc[...].astype(dv_ref.dtype)
            if att.has_bias:
                dc_ref[...] = -jnp.sum(dc_acc[...], axis=-1, keepdims=True)

    do_op = (do, att.dv, 0, True)
    in_specs = ([att.q_spec(att.q)] + [att.k_spec(k) for k in att.ks]
                + [att.k_spec(att.v), att.q_spec(do_op), att.row_q(), att.row_q()])
    args = [att.q[0]] + [k[0] for k in att.ks] + [att.v[0], do, lse, delta]
    if att.has_bias:
        in_specs.append(att.cum_k())
        args.append(att.cum_rep)
    out_specs = [pl.BlockSpec((T, att.dq), lambda h, p, it, jt: (jt[p], h))]
    out_shape = [jax.ShapeDtypeStruct((S, H * att.dq), dq_dtype)]
    out_specs += [pl.BlockSpec((T, w), lambda h, p, it, jt: (jt[p], h)) for w in widths]
    out_shape += [jax.ShapeDtypeStruct((S, H * w), dt) for w, dt in zip(widths, dk_dtypes)]
    out_specs.append(pl.BlockSpec((T, att.dv), lambda h, p, it, jt: (jt[p], h)))
    out_shape.append(jax.ShapeDtypeStruct((S, H * att.dv), BF16))
    scratch = [pltpu.VMEM((att.nb, att.dq, T), F32), pltpu.VMEM((T, att.dq), F32), pltpu.VMEM((T, att.dv), F32)]
    if att.has_bias:
        out_specs.append(pl.BlockSpec((None, T, 1), lambda h, p, it, jt: (h, jt[p], 0)))
        out_shape.append(jax.ShapeDtypeStruct((H, S, 1), F32))
        scratch.append(pltpu.VMEM((T, min(qs, LANE)), F32))
    return pl.pallas_call(
        body, name=name,
        grid_spec=pltpu.PrefetchScalarGridSpec(
            num_scalar_prefetch=2, grid=(H, npairs), in_specs=in_specs, out_specs=out_specs,
            scratch_shapes=scratch),
        out_shape=out_shape,
        compiler_params=_params(("parallel", "arbitrary")),
    )(it, jt, *args)


def _adamw(w, g, m, v, name):
    R, C = w.shape
    tr = _tile(R, 256, 8)
    c1 = 1.0 - ADAM_B1 ** ADAM_STEP
    c2 = 1.0 - ADAM_B2 ** ADAM_STEP

    def body(w_ref, g_ref, m_ref, v_ref, d_ref, nm_ref, nv_ref):
        gv = g_ref[...]
        nm = ADAM_B1 * m_ref[...] + (1.0 - ADAM_B1) * gv
        nv = ADAM_B2 * v_ref[...] + (1.0 - ADAM_B2) * (gv * gv)
        d_ref[...] = -ADAM_LR * ((nm / c1) / (jnp.sqrt(nv / c2) + ADAM_EPS) + ADAM_WD * w_ref[...])
        nm_ref[...] = nm
        nv_ref[...] = nv

    blk = pl.BlockSpec((tr, C), lambda i: (i, 0))
    return pl.pallas_call(
        body, name=name, grid=(R // tr,),
        in_specs=[blk] * 4, out_specs=[blk] * 3,
        out_shape=[jax.ShapeDtypeStruct((R, C), F32)] * 3,
        compiler_params=_params(("parallel",)),
    )(w, g, m, v)


def _place():
    return lax.axis_index("x"), lax.axis_index("y"), lax.axis_index("c")


def _other_chips(x, y):
    return [(1 - x, y), (x, 1 - y), (1 - x, 1 - y)]


_HBM_SPEC = pl.BlockSpec(memory_space=pltpu.HBM)
_SEM_SPEC = pl.BlockSpec(memory_space=pltpu.SEMAPHORE)
_ANY_SPEC = pl.BlockSpec(memory_space=pl.ANY)
_EFFECT = pltpu.SideEffectType.DATAFLOW_SIDE_EFFECTING


def _chip_copies(src_ref, land_ref, sems, gather):
    x, y, c = _place()
    me = 2 * x + y
    out, back = [], []
    for n, (px, py) in enumerate(_other_chips(x, y)):
        src = src_ref if gather else src_ref.at[2 * px + py]
        out.append(pltpu.make_async_remote_copy(
            src_ref=src, dst_ref=land_ref.at[me] if gather else land_ref.at[n],
            send_sem=sems[n], recv_sem=sems[3 + n], device_id=(px, py, c), device_id_type=MESH))
        back.append(pltpu.make_async_remote_copy(
            src_ref=src, dst_ref=land_ref.at[2 * px + py] if gather else land_ref.at[n],
            send_sem=sems[n], recv_sem=sems[3 + n], device_id=(px, py, c), device_id_type=MESH))
    return out, back


def _xchg_start(src, land, gather, order, name):
    def body(src_ref, land_ref, order_ref, *outs):
        sems = outs[0:6]
        token = outs[8]
        out, _ = _chip_copies(src_ref, land_ref, sems, gather)
        for cp in out:
            cp.start()
        token[...] = jnp.zeros_like(token)

    outs = pl.pallas_call(
        body, name=name,
        out_shape=(pltpu.SemaphoreType.DMA(()),) * 6 + (
            pltpu.HBM(src.shape, src.dtype), pltpu.HBM(land.shape, land.dtype),
            jax.ShapeDtypeStruct((8, LANE), F32)),
        in_specs=(_HBM_SPEC, _HBM_SPEC, _ANY_SPEC),
        out_specs=(_SEM_SPEC,) * 6 + (_HBM_SPEC, _HBM_SPEC, pl.BlockSpec(memory_space=pltpu.VMEM)),
        input_output_aliases={0: 6, 1: 7},
        compiler_params=pltpu.CompilerParams(has_side_effects=_EFFECT),
    )(pltpu.with_memory_space_constraint(src, pltpu.HBM), pltpu.with_memory_space_constraint(land, pltpu.HBM), order)
    return outs[0:6], outs[6], outs[7], outs[8]


def _xchg_wait(started, gather, after, name):
    sems, src, land, _ = started

    def body(src_ref, land_ref, *rest):
        _, back = _chip_copies(src_ref, land_ref, rest[0:6], gather)
        for cp in back:
            cp.wait_send()
            cp.wait_recv()

    return pl.pallas_call(
        body, name=name,
        out_shape=(pltpu.HBM(src.shape, src.dtype), pltpu.HBM(land.shape, land.dtype)),
        in_specs=(_HBM_SPEC, _HBM_SPEC) + (_SEM_SPEC,) * 6 + (_ANY_SPEC,),
        out_specs=(_HBM_SPEC, _HBM_SPEC),
        input_output_aliases={0: 0, 1: 1},
        compiler_params=pltpu.CompilerParams(has_side_effects=_EFFECT),
    )(src, land, *sems, after)


def _reduce_scatter(gp, gp_b, recv_b, vec, name):
    _, R, C = gp.shape
    RB = gp_b.shape[1]
    VR, W = vec.shape
    tr = _tile(R, PACK_ROWS, 16)
    assert R % tr == 0 and RB % tr == 0
    nchunk = R // tr
    nchunk_b = RB // tr
    RT = R + RB

    def body(gp_ref, gpb_ref, recvb_ref, vec_ref, out_ref, recv_ref, part_ref, sib_ref, vall_ref, vout_ref,
             buf_ref, acc_ref, send_sems, recv_sems, sib_sems, vsend_sems, vrecv_sems):
        x, y, c = _place()
        me = 2 * x + y
        chips = _other_chips(x, y)

        vall_ref[4 * x + 2 * y + c] = vec_ref[...]
        vsends = []
        for r in range(1, N_DEV):
            dx, dy, dc = (r >> 2) & 1, (r >> 1) & 1, r & 1
            peer = (x ^ dx, y ^ dy, c ^ dc)
            cp = pltpu.make_async_remote_copy(
                src_ref=vec_ref, dst_ref=vall_ref.at[4 * x + 2 * y + c], send_sem=vsend_sems.at[r - 1],
                recv_sem=vrecv_sems.at[r - 1], device_id=peer, device_id_type=MESH)
            cp.start()
            vsends.append(cp)
        sends = []
        for n, (px, py) in enumerate(chips):
            cp = pltpu.make_async_remote_copy(
                src_ref=gp_ref.at[2 * px + py], dst_ref=recv_ref.at[n], send_sem=send_sems.at[n],
                recv_sem=recv_sems.at[n], device_id=(px, py, c), device_id_type=MESH)
            cp.start()
            sends.append(cp)

        def sum_four(own_ref, got_ref, base):
            def one(i, carry):
                rows = pl.ds(pl.multiple_of(i * tr, tr), tr)
                pltpu.sync_copy(own_ref.at[me, rows], buf_ref.at[0])
                for n in range(3):
                    pltpu.sync_copy(got_ref.at[n, rows], buf_ref.at[n + 1])
                acc = buf_ref[0].astype(F32)
                for n in range(3):
                    acc = acc + buf_ref[n + 1].astype(F32)
                acc_ref[0] = acc
                pltpu.sync_copy(acc_ref.at[0], part_ref.at[pl.ds(pl.multiple_of(base + i * tr, tr), tr)])
                return carry
            return one

        lax.fori_loop(0, nchunk_b, sum_four(gpb_ref, recvb_ref, R), 0)
        for n, (px, py) in enumerate(chips):
            pltpu.make_async_remote_copy(
                src_ref=gp_ref.at[me], dst_ref=recv_ref.at[n], send_sem=send_sems.at[n],
                recv_sem=recv_sems.at[n], device_id=(px, py, c), device_id_type=MESH).wait_recv()
        lax.fori_loop(0, nchunk, sum_four(gp_ref, recv_ref, 0), 0)

        swap = pltpu.make_async_remote_copy(
            src_ref=part_ref, dst_ref=sib_ref, send_sem=sib_sems.at[0], recv_sem=sib_sems.at[1],
            device_id=(x, y, 1 - c), device_id_type=MESH)
        swap.start()
        swap.wait()

        def sum_two(i, carry):
            rows = pl.ds(pl.multiple_of(i * tr, tr), tr)
            pltpu.sync_copy(part_ref.at[rows], acc_ref.at[0])
            pltpu.sync_copy(sib_ref.at[rows], acc_ref.at[1])
            acc_ref[0] = acc_ref[0] + acc_ref[1]
            pltpu.sync_copy(acc_ref.at[0], out_ref.at[rows])
            return carry

        lax.fori_loop(0, nchunk + nchunk_b, sum_two, 0)

        for r in range(1, N_DEV):
            dx, dy, dc = (r >> 2) & 1, (r >> 1) & 1, r & 1
            peer = (x ^ dx, y ^ dy, c ^ dc)
            pltpu.make_async_remote_copy(
                src_ref=vec_ref, dst_ref=vall_ref.at[4 * peer[0] + 2 * peer[1] + peer[2]],
                send_sem=vsend_sems.at[r - 1], recv_sem=vrecv_sems.at[r - 1],
                device_id=peer, device_id_type=MESH).wait_recv()
        total = vall_ref[0]
        for d in range(1, N_DEV):
            total = total + vall_ref[d]
        vout_ref[...] = total
        for cp in sends + vsends:
            cp.wait_send()

    hbm = pl.BlockSpec(memory_space=pl.ANY)
    vmem = pl.BlockSpec(memory_space=pltpu.VMEM)
    outs = pl.pallas_call(
        body, name=name,
        in_specs=[hbm, hbm, hbm, vmem],
        out_specs=[hbm, hbm, hbm, hbm, vmem, vmem],
        out_shape=[jax.ShapeDtypeStruct((RT, C), F32), jax.ShapeDtypeStruct((3, R, C), gp.dtype),
                   jax.ShapeDtypeStruct((RT, C), F32), jax.ShapeDtypeStruct((RT, C), F32),
                   jax.ShapeDtypeStruct((N_DEV, VR, W), F32), jax.ShapeDtypeStruct((VR, W), F32)],
        scratch_shapes=[pltpu.VMEM((4, tr, C), gp.dtype), pltpu.VMEM((2, tr, C), F32),
                        pltpu.SemaphoreType.DMA((3,)), pltpu.SemaphoreType.DMA((3,)), pltpu.SemaphoreType.DMA((2,)),
                        pltpu.SemaphoreType.DMA((N_DEV - 1,)), pltpu.SemaphoreType.DMA((N_DEV - 1,))],
        compiler_params=pltpu.CompilerParams(vmem_limit_bytes=VMEM_LIMIT),
    )(gp, gp_b, recv_b, vec)
    return outs[0], outs[5]


def _rope_tables(S):
    pos = jnp.arange(S, dtype=F32)
    inv = 1.0 / (ROPE_THETA ** (jnp.arange(0, MLA_ROPE, 2, dtype=F32) / MLA_ROPE))
    ang = pos[:, None] * inv[None, :]
    cos, sin = jnp.cos(ang), jnp.sin(ang)
    half = MLA_ROPE // 2
    z = jnp.zeros((S, half), F32)
    one = jnp.ones((S, LANE - MLA_ROPE), F32)
    zero = jnp.zeros((S, LANE - MLA_ROPE), F32)
    kc = jnp.concatenate([cos, cos, one], axis=1)
    ksa = jnp.concatenate([-sin, z, zero], axis=1)
    ksb = jnp.concatenate([z, sin, zero], axis=1)
    qc = jnp.concatenate([jnp.ones((S, MLA_NOPE), F32), kc], axis=1)
    qsa = jnp.concatenate([jnp.zeros((S, MLA_NOPE), F32), ksa], axis=1)
    qsb = jnp.concatenate([jnp.zeros((S, MLA_NOPE), F32), ksb], axis=1)
    return (kc, ksa, ksb), (qc, qsa, qsb)


def _pad_cols(a, width):
    return jnp.pad(a, ((0, 0), (0, width - a.shape[1])))


def kernel(x, attn_norm, w_in, fox_f_bias, q_norm, w_uq, kv_norm, w_ukv, w_mla_branch, w_fox_branch, w_out, mlp_norm, w_up, w_down, final_norm, loss_target, m_attn_norm, m_w_in, m_fox_f_bias, m_q_norm, m_w_uq, m_kv_norm, m_w_ukv, m_w_mla_branch, m_w_fox_branch, m_w_out, m_mlp_norm, m_w_up, m_w_down, m_final_norm, v_attn_norm, v_w_in, v_fox_f_bias, v_q_norm, v_w_uq, v_kv_norm, v_w_ukv, v_w_mla_branch, v_w_fox_branch, v_w_out, v_mlp_norm, v_w_up, v_w_down, v_final_norm):
    _, S, D = x.shape
    H, HF = MLA_HEADS, FOX_HEADS
    QL, KVL = MLA_Q_LORA, MLA_KV_LORA
    assert H == HF and H <= 8
    xs = x[0]
    target = loss_target[0]

    group_a = [("w_in", w_in, 1), ("w_uq", w_uq, 1), ("w_ukv", w_ukv, 1)]
    group_b = [("w_mla_branch", w_mla_branch, 1), ("w_fox_branch", w_fox_branch, 1), ("w_out", w_out, 0),
               ("w_up", w_up, 1), ("w_down", w_down, 0)]
    big = group_a + group_b
    C = D

    def pad16(a, axis):
        short = -a.shape[axis] % 16
        return jnp.pad(a, [(0, short if d == axis else 0) for d in range(a.ndim)])

    def layout(group):
        rows, offs, off = {}, {}, 0
        for nm, w, _ in group:
            assert w[0].size % C == 0, nm
            rows[nm] = w[0].size // C
            offs[nm] = off
            off += -(-rows[nm] // 16) * 16
        return rows, offs, off, -(-off // PACK_ROWS) * PACK_ROWS

    def pack_shards(group, lay):
        _, _, off, R = lay
        return jnp.concatenate([pad16(w[0].astype(BF16).reshape(-1, C), 0) for _, w, _ in group]
                               + [jnp.zeros((R - off, C), BF16)], axis=0)

    def unpack_full(group, lay, gathered, full):
        rows, offs, _, _ = lay
        for nm, w, axis in group:
            parts = [gathered[k, offs[nm]:offs[nm] + rows[nm]].reshape(w[0].shape) for k in range(N_CHIPS)]
            full[nm] = jnp.concatenate(parts, axis=axis)

    lay_a, lay_b = layout(group_a), layout(group_b)
    RA, RB = lay_a[3], lay_b[3]
    chip = 2 * lax.axis_index("x") + lax.axis_index("y")
    wp_a, wp_b = pack_shards(group_a, lay_a), pack_shards(group_b, lay_b)
    ag_a = _xchg_start(wp_a, lax.empty((N_CHIPS, RA, C), BF16), True, jnp.zeros((8, LANE), F32), "all_gather_start_a")
    ag_b = _xchg_start(wp_b, lax.empty((N_CHIPS, RB, C), BF16), True, ag_a[3], "all_gather_start_b")
    xn = _norm_fwd(xs, attn_norm, "attn_norm_fwd", order=ag_b[3])
    full = {}
    own_a, land_a = _xchg_wait(ag_a, True, xn, "all_gather_wait_a")
    unpack_full(group_a, lay_a, lax.dynamic_update_slice(land_a, own_a[None], (chip, 0, 0)), full)

    o_ckv = QL
    o_kr = o_ckv + KVL
    o_fq = o_kr + MLA_ROPE
    o_ff = o_fq + 3 * HF * FOX_HEAD_DIM
    o_g = o_ff + HF
    wi = full["w_in"]
    assert wi.shape[1] == o_g + 2 * D
    WS = QL + KVL + 2 * LANE
    NQKV = 3 * HF * FOX_HEAD_DIM
    w_small = jnp.concatenate([wi[:, :o_kr], _pad_cols(wi[:, o_kr:o_fq], LANE), _pad_cols(wi[:, o_ff:o_g], LANE)], axis=1)
    w_qkv = wi[:, o_fq:o_ff]
    w_g = wi[:, o_g:]
    w_pack = jnp.concatenate([w_small, w_qkv, w_g], axis=1)
    dqk = MLA_NOPE + MLA_ROPE
    w_uq_p = jnp.pad(full["w_uq"].reshape(QL, H, dqk), ((0, 0), (0, 0), (0, QPAD - dqk))).reshape(QL, H * QPAD)
    ukv = full["w_ukv"].reshape(KVL, H, MLA_NOPE + MLA_V)
    w_ukv_p = jnp.concatenate([ukv[:, :, :MLA_NOPE].reshape(KVL, H * MLA_NOPE),
                               ukv[:, :, MLA_NOPE:].reshape(KVL, H * MLA_V)], axis=1)

    (kc, ksa, ksb), (qc, qsa, qsb) = _rope_tables(S)
    bias_pad = _pad_cols(fox_f_bias, LANE)

    small = _matmul(xn, w_small, "nn", [F32], "proj_small")
    qkv = _matmul(xn, w_qkv, "nn", [BF16], "proj_qkv")
    gpre = _matmul(xn, w_g, "nn", [F32], "proj_gates")
    cqn, ckvn, kr, cum = _prep_fwd(small, q_norm, kv_norm, bias_pad, kc, ksa, ksb, HF, "prep_fwd")
    q_raw = _matmul(cqn, w_uq_p, "nn", [F32], "mla_q_up")
    q_rot = _rope_heads(q_raw, qc, qsa, qsb, 1, "mla_q_rope")
    kv2 = _matmul(ckvn, w_ukv_p, "nn", [BF16], "mla_kv_up")

    mla = _AttT(S, H, (q_rot, QPAD, 0, True), [(kv2, MLA_NOPE, 0, True), (kr, LANE, 0, False)],
                (kv2, MLA_V, H, True), 1.0 / math.sqrt(dqk), True)
    o_mla, lse_mla = _att_fwd_t(mla, "mla_att_fwd")

    cum_t = jnp.transpose(cum[:, :HF]) * LOG2E
    cum_rep = jnp.broadcast_to(cum_t[:, :, None], (HF, S, min(QSUB, _tile(S, ATT_T))))
    fox = _AttT(S, HF, (qkv, FOX_HEAD_DIM, 0, True), [(qkv, FOX_HEAD_DIM, HF, True)],
                (qkv, FOX_HEAD_DIM, 2 * HF, True), 1.0 / math.sqrt(FOX_HEAD_DIM), False, cum_rep)
    o_fox, ox_fox, lse_fox = _att_fwd_t(fox, "fox_att_fwd", exact=True)

    own_b, land_b = _xchg_wait(ag_b, True, lse_fox, "all_gather_wait_b")
    unpack_full(group_b, lay_b, lax.dynamic_update_slice(land_b, own_b[None], (chip, 0, 0)), full)
    w_mb, w_fb, w_o, w_u, w_d = (full[n] for n in ("w_mla_branch", "w_fox_branch", "w_out", "w_up", "w_down"))

    y_mla = _matmul(o_mla, w_mb, "nn", [F32], "mla_branch")
    y_fox = _matmul(o_fox, w_fb, "nn", [F32], "fox_branch")
    merged = _gate_fwd(gpre, y_mla, y_fox, "gate_fwd")
    h1 = _matmul(merged, w_o, "nn", [F32], "out_proj", extras=(xs,), epilogue=lambda acc, r: (acc + r,))
    hn = _norm_fwd(h1, mlp_norm, "mlp_norm_fwd")

    def relu2(acc):
        a = jnp.maximum(acc, 0.0)
        return a * a, a

    u, a_pos = _matmul(hn, w_u, "nn", [BF16, BF16], "mlp_up", epilogue=relu2)
    h2 = _matmul(u, w_d, "nn", [F32], "mlp_down", extras=(h1,), epilogue=lambda acc, r: (acc + r,))
    dh2, g_final, loss_part = _final(h2, final_norm.reshape(1, D), target, "final_norm_loss")

    dh2_b = dh2.astype(BF16)
    da = _matmul(dh2_b, w_d, "nt", [BF16], "mlp_down_dx", extras=(a_pos,),
                 epilogue=lambda acc, a: (acc * (2.0 * a.astype(F32)),))
    g_w_down = _mm_tn(u, dh2_b, "mlp_down_dw")
    dhn = _matmul(da, w_u, "nt", [F32], "mlp_up_dx")
    g_w_up = _mm_tn(hn, da, "mlp_up_dw")
    dh1, g_mlp_norm = _norm_bwd(h1, dhn, mlp_norm, dh2, "mlp_norm_bwd")
    dh1_b = dh1.astype(BF16)
    dmerged = _matmul(dh1_b, w_o, "nt", [F32], "out_proj_dx")
    g_w_out = _mm_tn(merged, dh1_b, "out_proj_dw")
    dy_mla, dy_fox, dg_mla, dg_fox = _gate_bwd(dmerged, gpre, y_mla, y_fox, "gate_bwd")
    do_mla = _matmul(dy_mla, w_mb, "nt", [BF16], "mla_branch_dx")
    g_w_mb = _mm_tn(o_mla, dy_mla, "mla_branch_dw")
    do_fox = _matmul(dy_fox, w_fb, "nt", [BF16], "fox_branch_dx")
    g_w_fb = _mm_tn(o_fox, dy_fox, "fox_branch_dw")

    def pack_grads(group, lay, g_full):
        _, _, off, R = lay
        slabs = []
        for nm, w, axis in group:
            g = g_full[nm]
            if axis == 1:
                k_dim, n = g.shape[0], g.shape[1] // N_CHIPS
                g4 = jnp.transpose(g.reshape(k_dim, N_CHIPS, n), (1, 0, 2))
            else:
                g4 = g.reshape(N_CHIPS, g.shape[0] // N_CHIPS, g.shape[1])
            slabs.append(pad16(g4.reshape(N_CHIPS, -1, C).astype(BF16), 1))
        slabs.append(jnp.zeros((N_CHIPS, R - off, C), BF16))
        return jnp.concatenate(slabs, axis=1)

    gp_b = pack_grads(group_b, lay_b, {"w_mla_branch": g_w_mb, "w_fox_branch": g_w_fb, "w_out": g_w_out,
                                       "w_up": g_w_up, "w_down": g_w_down})
    rs_b = _xchg_start(gp_b, lax.empty((3, RB, C), BF16), False, g_w_fb, "grad_scatter_start_b")

    delta_mla = _att_delta_t(do_mla, o_mla, H, "mla_att_delta", order=rs_b[3])
    dq_raw, dk_nope, dkr_heads, dv_mla = _att_bwd_t(mla, do_mla, lse_mla, delta_mla, F32, [BF16, F32], "mla_att_bwd")
    delta_fox = _att_delta_t(do_fox, ox_fox, HF, "fox_att_delta")
    dfq, dfk, dfv, dcum = _att_bwd_t(fox, do_fox, lse_fox, delta_fox, BF16, [BF16], "fox_att_bwd")

    dq_rot = _rope_heads(dq_raw, qc, qsa, qsb, -1, "mla_q_rope_bwd")
    dcqn = _matmul(dq_rot, w_uq_p, "nt", [F32], "mla_q_up_dx")
    g_w_uq_p = _mm_tn(cqn, dq_rot, "mla_q_up_dw")
    dkv2 = jnp.concatenate([dk_nope, dv_mla], axis=1)
    dckvn = _matmul(dkv2, w_ukv_p, "nt", [F32], "mla_kv_up_dx")
    g_w_ukv_p = _mm_tn(ckvn, dkv2, "mla_kv_up_dw")

    dcum_rows = jnp.pad(dcum[:, :, 0], ((0, 8 - HF), (0, 0)))
    dlogf_rows = _suffix_sum_rows(dcum_rows, "fox_forget_suffix_sum")
    dlogf = _pad_cols(jnp.transpose(dlogf_rows[:HF]), LANE)
    d_small, g_q_norm, g_kv_norm, g_bias = _prep_bwd(
        small, dcqn, dckvn, dkr_heads, dlogf, q_norm, kv_norm, bias_pad, kc, ksa, ksb, H, "prep_bwd")
    dproj = jnp.concatenate([d_small, dfq, dfk, dfv, dg_mla, dg_fox], axis=1)
    dxn = _matmul(dproj, w_pack, "nt", [F32], "proj_dx")
    g_w_pack = _mm_tn(xn, dproj, "proj_dw")
    grad_x, g_attn_norm = _norm_bwd(xs, dxn, attn_norm, dh1, "attn_norm_bwd")

    gs, gq, gg = g_w_pack[:, :WS], g_w_pack[:, WS:WS + NQKV], g_w_pack[:, WS + NQKV:]
    g_w_in = jnp.concatenate([gs[:, :o_kr], gs[:, o_kr:o_kr + MLA_ROPE], gq,
                              gs[:, o_kr + LANE:o_kr + LANE + HF], gg], axis=1)
    g_w_uq = g_w_uq_p.reshape(QL, H, QPAD)[:, :, :dqk].reshape(QL, H * dqk)
    g_w_ukv = jnp.concatenate([g_w_ukv_p[:, :H * MLA_NOPE].reshape(KVL, H, MLA_NOPE),
                               g_w_ukv_p[:, H * MLA_NOPE:].reshape(KVL, H, MLA_V)], axis=2).reshape(KVL, -1)

    gp_a = pack_grads(group_a, lay_a, {"w_in": g_w_in, "w_uq": g_w_uq, "w_ukv": g_w_ukv})
    gp_b_sent, recv_b = _xchg_wait(rs_b, False, grad_x, "grad_scatter_wait_b")
    vec_w = max(D, LANE)
    vec_rows = [g_attn_norm, g_mlp_norm, g_final, g_q_norm, g_kv_norm, g_bias, loss_part]
    vec = jnp.concatenate([_pad_cols(v, vec_w) for v in vec_rows] + [jnp.zeros((1, vec_w), F32)], axis=0)
    g_shards, vsum = _reduce_scatter(gp_a, gp_b_sent, recv_b, vec, "reduce_scatter_grads")

    moments = {"attn_norm": (m_attn_norm, v_attn_norm), "w_in": (m_w_in, v_w_in), "fox_f_bias": (m_fox_f_bias, v_fox_f_bias),
               "q_norm": (m_q_norm, v_q_norm), "w_uq": (m_w_uq, v_w_uq), "kv_norm": (m_kv_norm, v_kv_norm),
               "w_ukv": (m_w_ukv, v_w_ukv), "w_mla_branch": (m_w_mla_branch, v_w_mla_branch),
               "w_fox_branch": (m_w_fox_branch, v_w_fox_branch), "w_out": (m_w_out, v_w_out),
               "mlp_norm": (m_mlp_norm, v_mlp_norm), "w_up": (m_w_up, v_w_up), "w_down": (m_w_down, v_w_down),
               "final_norm": (m_final_norm, v_final_norm)}
    weights = {"attn_norm": attn_norm, "w_in": w_in, "fox_f_bias": fox_f_bias, "q_norm": q_norm, "w_uq": w_uq,
               "kv_norm": kv_norm, "w_ukv": w_ukv, "w_mla_branch": w_mla_branch, "w_fox_branch": w_fox_branch,
               "w_out": w_out, "mlp_norm": mlp_norm, "w_up": w_up, "w_down": w_down, "final_norm": final_norm}
    grads, deltas, new_m, new_v = {}, {}, {}, {}
    where = {nm: (0, lay_a) for nm, _, _ in group_a}
    where.update({nm: (RA, lay_b) for nm, _, _ in group_b})
    for nm, w, _ in big:
        shp = w[0].shape
        base, (rows, offs, _, _) = where[nm]
        g = g_shards[base + offs[nm]:base + offs[nm] + rows[nm]].reshape(shp)
        d, nm_, nv_ = _adamw(w[0], g, moments[nm][0][0], moments[nm][1][0], "adamw_" + nm)
        grads[nm], deltas[nm], new_m[nm], new_v[nm] = g[None], d[None], nm_[None], nv_[None]
    vec_names = ["attn_norm", "mlp_norm", "final_norm", "q_norm", "kv_norm", "fox_f_bias"]

    def vec_pack(arrs):
        return jnp.concatenate([_pad_cols(a.reshape(1, -1), vec_w) for a in arrs]
                               + [jnp.zeros((2, vec_w), F32)], axis=0)

    vd, vm, vv = _adamw(vec_pack([weights[n] for n in vec_names]), vsum,
                        vec_pack([moments[n][0] for n in vec_names]), vec_pack([moments[n][1] for n in vec_names]),
                        "adamw_vectors")
    for r, nm in enumerate(vec_names):
        shp = weights[nm].shape
        n = weights[nm].size
        grads[nm] = vsum[r, :n].reshape(shp)
        deltas[nm], new_m[nm], new_v[nm] = vd[r, :n].reshape(shp), vm[r, :n].reshape(shp), vv[r, :n].reshape(shp)
    loss = vsum[6, 0]

    order = ["attn_norm", "w_in", "fox_f_bias", "q_norm", "w_uq", "kv_norm", "w_ukv", "w_mla_branch", "w_fox_branch",
             "w_out", "mlp_norm", "w_up", "w_down", "final_norm"]
    return (loss, grad_x[None], *[grads[n] for n in order], *[deltas[n] for n in order],
            *[new_m[n] for n in order], *[new_v[n] for n in order])
```

```python
import functools
import math

import jax
import jax.numpy as jnp
from jax import lax
from jax.experimental import pallas as pl
from jax.experimental.pallas import tpu as pltpu

CHUNK = 64
MLA_HEADS = 8
MLA_Q_LORA = 512
MLA_KV_LORA = 256
MLA_NOPE = 128
MLA_ROPE = 64
MLA_V = 128
ROPE_THETA = 10000.0
FOX_HEADS = 8
FOX_HEAD_DIM = 128
EPS = 1e-6

ADAM_LR = 0.001
ADAM_B1 = 0.9
ADAM_B2 = 0.999
ADAM_EPS = 1e-08
ADAM_WD = 0.01
ADAM_STEP = 10

LANE = 128
QPAD = 2 * LANE
N_CHIPS = 4
N_DEV = 8
VMEM_LIMIT = 48 * 1024 * 1024
ATT_T = 1024
ROW_T = 256
PACK_ROWS = 256

BF16 = jnp.bfloat16
F32 = jnp.float32
MESH = pl.DeviceIdType.MESH


def _tile(dim, pref, align=LANE):
    if dim <= pref:
        return dim
    t = (pref // align) * align
    while t >= align:
        if dim % t == 0:
            return t
        t -= align
    return dim


def _params(sem=None):
    return pltpu.CompilerParams(dimension_semantics=sem, vmem_limit_bytes=VMEM_LIMIT)


def _matmul(a, b, mode, out_dtypes, name, *, tm=1024, tn=512, tk=2048, extras=(), row_extras=(), epilogue=None,
            order=None):
    if mode == "nn":
        (M, K), (K2, N) = a.shape, b.shape
    elif mode == "nt":
        (M, K), (N, K2) = a.shape, b.shape
    else:
        (K, M), (K2, N) = a.shape, b.shape
    assert K == K2, (name, a.shape, b.shape)
    tm, tn, tk = _tile(M, tm), _tile(N, tn), _tile(K, tk)
    nk = K // tk
    n_out = len(out_dtypes)
    n_ex = len(extras) + len(row_extras)
    assert all(r.shape == (M, tn) for r in row_extras), name

    n_ord = 0 if order is None else 1

    def body(*refs):
        a_ref, b_ref = refs[0], refs[1]
        ex_refs = refs[2:2 + n_ex]
        o_refs = refs[2 + n_ex + n_ord:2 + n_ex + n_ord + n_out]
        acc_ref = refs[2 + n_ex + n_ord + n_out]
        k = pl.program_id(2)
        if mode == "nn":
            dims = (((1,), (0,)), ((), ()))
        elif mode == "nt":
            dims = (((1,), (1,)), ((), ()))
        else:
            dims = (((0,), (0,)), ((), ()))
        part = lax.dot_general(a_ref[...], b_ref[...], dims, preferred_element_type=F32)

        @pl.when(k == 0)
        def _():
            acc_ref[...] = part

        @pl.when(k > 0)
        def _():
            acc_ref[...] += part

        @pl.when(k == nk - 1)
        def _():
            acc = acc_ref[...]
            if epilogue is None:
                outs = (acc,)
            else:
                outs = epilogue(acc, *[r[...] for r in ex_refs])
            for o_ref, o in zip(o_refs, outs):
                o_ref[...] = o.astype(o_ref.dtype)

    if mode == "nn":
        a_spec = pl.BlockSpec((tm, tk), lambda i, j, k: (i, k))
        b_spec = pl.BlockSpec((tk, tn), lambda i, j, k: (k, j))
    elif mode == "nt":
        a_spec = pl.BlockSpec((tm, tk), lambda i, j, k: (i, k))
        b_spec = pl.BlockSpec((tn, tk), lambda i, j, k: (j, k))
    else:
        a_spec = pl.BlockSpec((tk, tm), lambda i, j, k: (k, i))
        b_spec = pl.BlockSpec((tk, tn), lambda i, j, k: (k, j))
    mn_spec = pl.BlockSpec((tm, tn), lambda i, j, k: (i, j))
    row_spec = pl.BlockSpec((tm, tn), lambda i, j, k: (i, 0))
    outs = pl.pallas_call(
        body,
        name=name,
        grid=(M // tm, N // tn, nk),
        in_specs=([a_spec, b_spec] + [mn_spec] * len(extras) + [row_spec] * len(row_extras)
                  + [pl.BlockSpec(memory_space=pl.ANY)] * n_ord),
        out_specs=[mn_spec] * n_out,
        out_shape=[jax.ShapeDtypeStruct((M, N), dt) for dt in out_dtypes],
        scratch_shapes=[pltpu.VMEM((tm, tn), F32)],
        compiler_params=_params(("parallel", "parallel", "arbitrary")),
    )(a, b, *extras, *row_extras, *([] if order is None else [order]))
    return outs[0] if n_out == 1 else outs


def _mm_tn(a, b, name):
    return _matmul(a, b, "tn", [F32], name, tm=1024, tn=1024, tk=2048)


def _row_spec(ts, width, col=0):
    return pl.BlockSpec((ts, width), lambda i: (i, col))


def _full_spec(shape):
    return pl.BlockSpec(shape, lambda i: tuple(0 for _ in shape))


def _rms(x):
    return lax.rsqrt(jnp.mean(x * x, axis=-1, keepdims=True) + EPS)


def _rms_bwd(x, dy, g):
    r = _rms(x)
    xh = x * r
    gy = dy * g
    dx = r * (gy - xh * jnp.mean(xh * gy, axis=-1, keepdims=True))
    return dx, dy * xh


def _norm_fwd(x, g, name, order=None):
    S, D = x.shape
    ts = _tile(S, ROW_T, 8)

    def body(x_ref, g_ref, *rest):
        o_ref = rest[-1]
        xv = x_ref[...]
        o_ref[...] = ((xv * _rms(xv)) * g_ref[...]).astype(BF16)

    extra = [] if order is None else [order]
    return pl.pallas_call(
        body, name=name, grid=(S // ts,),
        in_specs=[_row_spec(ts, D), _full_spec((1, D))] + [pl.BlockSpec(memory_space=pl.ANY)] * len(extra),
        out_specs=_row_spec(ts, D),
        out_shape=jax.ShapeDtypeStruct((S, D), BF16),
        compiler_params=_params(("parallel",)),
    )(x, g, *extra)


def _norm_bwd(x, dy, g, dres, name):
    S, D = x.shape
    ts = _tile(S, ROW_T, 8)

    def body(x_ref, dy_ref, g_ref, dres_ref, dx_ref, dg_ref):
        dx, dg_rows = _rms_bwd(x_ref[...], dy_ref[...], g_ref[...])
        dx_ref[...] = dres_ref[...] + dx

        @pl.when(pl.program_id(0) == 0)
        def _():
            dg_ref[...] = jnp.zeros_like(dg_ref)

        dg_ref[...] += jnp.sum(dg_rows, axis=0, keepdims=True)

    return pl.pallas_call(
        body, name=name, grid=(S // ts,),
        in_specs=[_row_spec(ts, D), _row_spec(ts, D), _full_spec((1, D)), _row_spec(ts, D)],
        out_specs=[_row_spec(ts, D), _full_spec((1, D))],
        out_shape=[jax.ShapeDtypeStruct((S, D), F32), jax.ShapeDtypeStruct((1, D), F32)],
        compiler_params=_params(("arbitrary",)),
    )(x, dy, g, dres)


def _rope(x, c, sa, sb, sign):
    w = x.shape[-1]
    half = MLA_ROPE // 2
    fwd = pltpu.roll(x, w - half, 1)
    back = pltpu.roll(x, half, 1)
    if sign < 0:
        return x * c - fwd * sa - back * sb
    return x * c + fwd * sa + back * sb


def _split3(x):
    hi = x.astype(BF16)
    r1 = x - hi.astype(F32)
    mid = r1.astype(BF16)
    lo = (r1 - mid.astype(F32)).astype(BF16)
    return hi, mid, lo


def _prep_fwd(small, q_norm, kv_norm, bias_pad, kc, ksa, ksb, n_heads, name):
    S, W = small.shape
    QL, KVL = q_norm.shape[1], kv_norm.shape[1]
    assert W == QL + KVL + 2 * LANE
    ts = _tile(S, ROW_T, 8)
    tri = (lax.broadcasted_iota(jnp.int32, (ts, ts), 0) >= lax.broadcasted_iota(jnp.int32, (ts, ts), 1)).astype(BF16)

    def body(s_ref, qn_ref, kvn_ref, b_ref, kc_ref, ksa_ref, ksb_ref, tri_ref,
             cqn_ref, ckvn_ref, kr_ref, cum_ref, carry_ref):
        cq = s_ref[:, 0:QL]
        cqn_ref[...] = ((cq * _rms(cq)) * qn_ref[...]).astype(BF16)
        ckv = s_ref[:, QL:QL + KVL]
        ckvn_ref[...] = ((ckv * _rms(ckv)) * kvn_ref[...]).astype(BF16)
        kr = s_ref[:, QL + KVL:QL + KVL + LANE]
        kr_ref[...] = _rope(kr, kc_ref[...], ksa_ref[...], ksb_ref[...], 1).astype(BF16)
        z = s_ref[:, QL + KVL + LANE:W] + b_ref[...]
        logf = jnp.minimum(z, 0.0) - jnp.log1p(jnp.exp(-jnp.abs(z)))
        lane = lax.broadcasted_iota(jnp.int32, logf.shape, 1)
        logf = jnp.where(lane < n_heads, logf, 0.0)

        @pl.when(pl.program_id(0) == 0)
        def _():
            carry_ref[...] = jnp.zeros_like(carry_ref)

        t = tri_ref[...]
        cum = carry_ref[...]
        for part in _split3(logf):
            cum = cum + jnp.dot(t, part, preferred_element_type=F32)
        cum_ref[...] = cum
        carry_ref[...] = cum[ts - 1:ts, :]

    return pl.pallas_call(
        body, name=name, grid=(S // ts,),
        in_specs=[_row_spec(ts, W), _full_spec((1, QL)), _full_spec((1, KVL)), _full_spec((1, LANE)),
                  _row_spec(ts, LANE), _row_spec(ts, LANE), _row_spec(ts, LANE), _full_spec((ts, ts))],
        out_specs=[_row_spec(ts, QL), _row_spec(ts, KVL), _row_spec(ts, LANE), _row_spec(ts, LANE)],
        out_shape=[jax.ShapeDtypeStruct((S, QL), BF16), jax.ShapeDtypeStruct((S, KVL), BF16),
                   jax.ShapeDtypeStruct((S, LANE), BF16), jax.ShapeDtypeStruct((S, LANE), F32)],
        scratch_shapes=[pltpu.VMEM((1, LANE), F32)],
        compiler_params=_params(("arbitrary",)),
    )(small, q_norm, kv_norm, bias_pad, kc, ksa, ksb, tri)


def _prep_bwd(small, dcqn, dckvn, dkr_heads, dlogf, q_norm, kv_norm, bias_pad, kc, ksa, ksb, n_heads, name):
    S, W = small.shape
    QL, KVL = q_norm.shape[1], kv_norm.shape[1]
    ts = _tile(S, ROW_T, 8)

    def body(s_ref, dcq_ref, dckv_ref, dkr_ref, dlf_ref, qn_ref, kvn_ref, b_ref, kc_ref, ksa_ref, ksb_ref,
             ds_ref, gq_ref, gkv_ref, gb_ref):
        dcq, gq_rows = _rms_bwd(s_ref[:, 0:QL], dcq_ref[...], qn_ref[...])
        ds_ref[:, 0:QL] = dcq.astype(BF16)
        dckv, gkv_rows = _rms_bwd(s_ref[:, QL:QL + KVL], dckv_ref[...], kvn_ref[...])
        ds_ref[:, QL:QL + KVL] = dckv.astype(BF16)
        dkr = dkr_ref[:, 0:LANE]
        for h in range(1, n_heads):
            dkr = dkr + dkr_ref[:, h * LANE:(h + 1) * LANE]
        ds_ref[:, QL + KVL:QL + KVL + LANE] = _rope(dkr, kc_ref[...], ksa_ref[...], ksb_ref[...], -1).astype(BF16)
        z = s_ref[:, QL + KVL + LANE:W] + b_ref[...]
        dff = dlf_ref[...] * (1.0 / (1.0 + jnp.exp(z)))
        ds_ref[:, QL + KVL + LANE:W] = dff.astype(BF16)

        @pl.when(pl.program_id(0) == 0)
        def _():
            gq_ref[...] = jnp.zeros_like(gq_ref)
            gkv_ref[...] = jnp.zeros_like(gkv_ref)
            gb_ref[...] = jnp.zeros_like(gb_ref)

        gq_ref[...] += jnp.sum(gq_rows, axis=0, keepdims=True)
        gkv_ref[...] += jnp.sum(gkv_rows, axis=0, keepdims=True)
        gb_ref[...] += jnp.sum(dff, axis=0, keepdims=True)

    return pl.pallas_call(
        body, name=name, grid=(S // ts,),
        in_specs=[_row_spec(ts, W), _row_spec(ts, QL), _row_spec(ts, KVL), _row_spec(ts, n_heads * LANE),
                  _row_spec(ts, LANE), _full_spec((1, QL)), _full_spec((1, KVL)), _full_spec((1, LANE)),
                  _row_spec(ts, LANE), _row_spec(ts, LANE), _row_spec(ts, LANE)],
        out_specs=[_row_spec(ts, W), _full_spec((1, QL)), _full_spec((1, KVL)), _full_spec((1, LANE))],
        out_shape=[jax.ShapeDtypeStruct((S, W), BF16), jax.ShapeDtypeStruct((1, QL), F32),
                   jax.ShapeDtypeStruct((1, KVL), F32), jax.ShapeDtypeStruct((1, LANE), F32)],
        compiler_params=_params(("arbitrary",)),
    )(small, dcqn, dckvn, dkr_heads, dlogf, q_norm, kv_norm, bias_pad, kc, ksa, ksb)


def _rope_heads(x, c, sa, sb, sign, name):
    S, W = x.shape
    nh = W // QPAD
    ts = _tile(S, ROW_T, 8)

    def body(x_ref, c_ref, sa_ref, sb_ref, o_ref):
        o_ref[...] = _rope(x_ref[...], c_ref[...], sa_ref[...], sb_ref[...], sign).astype(BF16)

    tab = pl.BlockSpec((ts, QPAD), lambda i, h: (i, 0))
    blk = pl.BlockSpec((ts, QPAD), lambda i, h: (i, h))
    return pl.pallas_call(
        body, name=name, grid=(S // ts, nh),
        in_specs=[blk, tab, tab, tab], out_specs=blk,
        out_shape=jax.ShapeDtypeStruct((S, W), BF16),
        compiler_params=_params(("parallel", "parallel")),
    )(x, c, sa, sb)


def _sigmoid(z):
    return 1.0 / (1.0 + jnp.exp(-z))


def _gate_fwd(gpre, y_mla, y_fox, name):
    S, D = y_mla.shape
    ts = _tile(S, ROW_T, 8)

    def body(ga_ref, gb_ref, ya_ref, yb_ref, o_ref):
        o_ref[...] = (_sigmoid(ga_ref[...]) * ya_ref[...] + _sigmoid(gb_ref[...]) * yb_ref[...]).astype(BF16)

    return pl.pallas_call(
        body, name=name, grid=(S // ts,),
        in_specs=[_row_spec(ts, D, 0), _row_spec(ts, D, 1), _row_spec(ts, D), _row_spec(ts, D)],
        out_specs=_row_spec(ts, D),
        out_shape=jax.ShapeDtypeStruct((S, D), BF16),
        compiler_params=_params(("parallel",)),
    )(gpre, gpre, y_mla, y_fox)


def _gate_bwd(dmerged, gpre, y_mla, y_fox, name):
    S, D = y_mla.shape
    ts = _tile(S, ROW_T, 8)

    def body(dm_ref, ga_ref, gb_ref, ya_ref, yb_ref, dya_ref, dyb_ref, dga_ref, dgb_ref):
        dm = dm_ref[...]
        ga = _sigmoid(ga_ref[...])
        gb = _sigmoid(gb_ref[...])
        dya_ref[...] = (dm * ga).astype(BF16)
        dyb_ref[...] = (dm * gb).astype(BF16)
        dga_ref[...] = (dm * ya_ref[...] * (ga * (1.0 - ga))).astype(BF16)
        dgb_ref[...] = (dm * yb_ref[...] * (gb * (1.0 - gb))).astype(BF16)

    return pl.pallas_call(
        body, name=name, grid=(S // ts,),
        in_specs=[_row_spec(ts, D), _row_spec(ts, D, 0), _row_spec(ts, D, 1), _row_spec(ts, D), _row_spec(ts, D)],
        out_specs=[_row_spec(ts, D)] * 4,
        out_shape=[jax.ShapeDtypeStruct((S, D), BF16)] * 4,
        compiler_params=_params(("parallel",)),
    )(dmerged, gpre, gpre, y_mla, y_fox)


def _final(h, g, target, name):
    S, D = h.shape
    ts = _tile(S, ROW_T, 8)

    def body(h_ref, g_ref, t_ref, dh_ref, dg_ref, loss_ref):
        hv = h_ref[...]
        gv = g_ref[...]
        err = (hv * _rms(hv)) * gv - t_ref[...]
        dh, dg_rows = _rms_bwd(hv, err / D, gv)
        dh_ref[...] = dh

        @pl.when(pl.program_id(0) == 0)
        def _():
            dg_ref[...] = jnp.zeros_like(dg_ref)
            loss_ref[...] = jnp.zeros_like(loss_ref)

        dg_ref[...] += jnp.sum(dg_rows, axis=0, keepdims=True)
        row_loss = jnp.mean(err * err, axis=-1, keepdims=True)
        loss_ref[...] += 0.5 * jnp.sum(row_loss, axis=0, keepdims=True)

    return pl.pallas_call(
        body, name=name, grid=(S // ts,),
        in_specs=[_row_spec(ts, D), _full_spec((1, D)), _row_spec(ts, D)],
        out_specs=[_row_spec(ts, D), _full_spec((1, D)), _full_spec((1, LANE))],
        out_shape=[jax.ShapeDtypeStruct((S, D), F32), jax.ShapeDtypeStruct((1, D), F32),
                   jax.ShapeDtypeStruct((1, LANE), F32)],
        compiler_params=_params(("arbitrary",)),
    )(h, g, target)


def _suffix_sum_rows(x, name):
    R, S = x.shape
    tb = _tile(S, 512)
    nb = S // tb
    tri = (lax.broadcasted_iota(jnp.int32, (tb, tb), 0) >= lax.broadcasted_iota(jnp.int32, (tb, tb), 1)).astype(BF16)

    def body(x_ref, tri_ref, o_ref, carry_ref):
        @pl.when(pl.program_id(0) == 0)
        def _():
            carry_ref[...] = jnp.zeros_like(carry_ref)

        xv = x_ref[...]
        t = tri_ref[...]
        acc = jnp.broadcast_to(carry_ref[:, 0:1], xv.shape)
        for part in _split3(xv):
            acc = acc + jnp.dot(part, t, preferred_element_type=F32)
        o_ref[...] = acc
        carry_ref[...] = jnp.broadcast_to(acc[:, 0:1], carry_ref.shape)

    rev = pl.BlockSpec((R, tb), lambda i: (0, nb - 1 - i))
    return pl.pallas_call(
        body, name=name, grid=(nb,),
        in_specs=[rev, _full_spec((tb, tb))], out_specs=rev,
        out_shape=jax.ShapeDtypeStruct((R, S), F32),
        scratch_shapes=[pltpu.VMEM((R, LANE), F32)],
        compiler_params=_params(("arbitrary",)),
    )(x, tri)


def _pairs(nb, by_key):
    if by_key:
        pr = [(i, j) for j in range(nb) for i in range(j, nb)]
    else:
        pr = [(i, j) for i in range(nb) for j in range(i + 1)]
    return (jnp.asarray([p[0] for p in pr], jnp.int32), jnp.asarray([p[1] for p in pr], jnp.int32), len(pr))


def _diag_mask(t, chunk_causal):
    r = lax.broadcasted_iota(jnp.int32, (t, t), 0)
    c = lax.broadcasted_iota(jnp.int32, (t, t), 1)
    if chunk_causal:
        return (c // CHUNK) <= (r // CHUNK)
    return c <= r


class _Att:
    def __init__(self, S, n_heads, q, ks, v, scale, chunk_causal, cum_col=None, cum_row=None):
        self.S, self.H, self.q, self.ks, self.v = S, n_heads, q, ks, v
        self.scale, self.chunk_causal = scale, chunk_causal
        self.cum_col, self.cum_row = cum_col, cum_row
        self.T = _tile(S, ATT_T)
        self.nb = S // self.T
        self.dq = q[1]
        self.dv = v[1]
        self.has_bias = cum_col is not None

    def q_spec(self, op):
        arr, w, off, per_head = op
        return pl.BlockSpec((self.T, w), lambda h, p, it, jt: (it[p], off + (h if per_head else 0)))

    def k_spec(self, op):
        arr, w, off, per_head = op
        return pl.BlockSpec((self.T, w), lambda h, p, it, jt: (jt[p], off + (h if per_head else 0)))

    def col_q(self):
        return pl.BlockSpec((None, self.T, 1), lambda h, p, it, jt: (h, it[p], 0))

    def row_k(self):
        return pl.BlockSpec((None, 1, self.T), lambda h, p, it, jt: (h, 0, jt[p]))

    def scores(self, q, k_refs, cc_ref, cr_ref, masked):
        k = k_refs[0][...] if len(k_refs) == 1 else jnp.concatenate([r[...] for r in k_refs], axis=-1)
        s = lax.dot_general(q, k, (((1,), (1,)), ((), ())), preferred_element_type=F32) * self.scale
        if self.has_bias:
            s = s + cc_ref[...] - cr_ref[...]
        mask = _diag_mask(self.T, self.chunk_causal) if masked else None
        return s, k, mask


def _att_fwd(att, name):
    S, H, T = att.S, att.H, att.T
    it, jt, npairs = _pairs(att.nb, by_key=False)
    nk = len(att.ks)

    def body(it_ref, jt_ref, *refs):
        q_ref = refs[0]
        k_refs = refs[1:1 + nk]
        v_ref = refs[1 + nk]
        n = 2 + nk
        cc_ref = cr_ref = None
        if att.has_bias:
            cc_ref, cr_ref = refs[n], refs[n + 1]
            n += 2
        o_ref, lse_ref, m_ref, l_ref, acc_ref = refs[n:n + 5]
        p = pl.program_id(1)
        i, j = it_ref[p], jt_ref[p]

        @pl.when(j == 0)
        def _():
            m_ref[...] = jnp.full_like(m_ref, -jnp.inf)
            l_ref[...] = jnp.zeros_like(l_ref)
            acc_ref[...] = jnp.zeros_like(acc_ref)

        def step(masked):
            s, _, mask = att.scores(q_ref[...], k_refs, cc_ref, cr_ref, masked)
            if masked:
                s = jnp.where(mask, s, -jnp.inf)
            m_prev = m_ref[...]
            m_new = jnp.maximum(m_prev, jnp.max(s, axis=-1, keepdims=True))
            alpha = jnp.exp(m_prev - m_new)
            pr = jnp.exp(s - m_new)
            l_ref[...] = alpha * l_ref[...] + jnp.sum(pr, axis=-1, keepdims=True)
            acc_ref[...] = alpha * acc_ref[...] + jnp.dot(pr.astype(BF16), v_ref[...], preferred_element_type=F32)
            m_ref[...] = m_new

        @pl.when(j < i)
        def _():
            step(False)

        @pl.when(j == i)
        def _():
            step(True)
            l = l_ref[...]
            o_ref[...] = (acc_ref[...] / l).astype(o_ref.dtype)
            lse_ref[...] = m_ref[...] + jnp.log(l)

    in_specs = [att.q_spec(att.q)] + [att.k_spec(k) for k in att.ks] + [att.k_spec(att.v)]
    args = [att.q[0]] + [k[0] for k in att.ks] + [att.v[0]]
    if att.has_bias:
        in_specs += [att.col_q(), att.row_k()]
        args += [att.cum_col, att.cum_row]
    out_specs = [pl.BlockSpec((T, att.dv), lambda h, p, it, jt: (it[p], h)), att.col_q()]
    return pl.pallas_call(
        body, name=name,
        grid_spec=pltpu.PrefetchScalarGridSpec(
            num_scalar_prefetch=2, grid=(H, npairs), in_specs=in_specs, out_specs=out_specs,
            scratch_shapes=[pltpu.VMEM((T, 1), F32), pltpu.VMEM((T, 1), F32), pltpu.VMEM((T, att.dv), F32)]),
        out_shape=[jax.ShapeDtypeStruct((S, H * att.dv), BF16), jax.ShapeDtypeStruct((H, S, 1), F32)],
        compiler_params=_params(("parallel", "arbitrary")),
    )(it, jt, *args)


def _att_delta(do, o, n_heads, name):
    S = do.shape[0]
    w = do.shape[1] // n_heads
    ts = _tile(S, ATT_T, 8)

    def body(do_ref, o_ref, d_ref):
        d_ref[...] = jnp.sum(do_ref[...].astype(F32) * o_ref[...].astype(F32), axis=-1, keepdims=True)

    blk = pl.BlockSpec((ts, w), lambda i, h: (i, h))
    return pl.pallas_call(
        body, name=name, grid=(S // ts, n_heads),
        in_specs=[blk, blk], out_specs=pl.BlockSpec((None, ts, 1), lambda i, h: (h, i, 0)),
        out_shape=jax.ShapeDtypeStruct((n_heads, S, 1), F32),
        compiler_params=_params(("parallel", "parallel")),
    )(do, o)


def _att_probs(att, q, k_refs, v_ref, do_ref, lse_ref, dl_ref, cc_ref, cr_ref, masked):
    s, k, mask = att.scores(q, k_refs, cc_ref, cr_ref, masked)
    pr = jnp.exp(s - lse_ref[...])
    if masked:
        pr = jnp.where(mask, pr, 0.0)
    do = do_ref[...]
    dp = lax.dot_general(do, v_ref[...], (((1,), (1,)), ((), ())), preferred_element_type=F32)
    ds = pr * (dp - dl_ref[...])
    return pr, ds, k, do


def _att_bwd_q(att, do, lse, delta, out_dtype, name, with_rowsum=False):
    S, H, T = att.S, att.H, att.T
    it, jt, npairs = _pairs(att.nb, by_key=False)
    nk = len(att.ks)

    def body(it_ref, jt_ref, *refs):
        q_ref = refs[0]
        k_refs = refs[1:1 + nk]
        v_ref, do_ref, lse_ref, dl_ref = refs[1 + nk:5 + nk]
        n = 5 + nk
        cc_ref = cr_ref = None
        if att.has_bias:
            cc_ref, cr_ref = refs[n], refs[n + 1]
            n += 2
        dq_ref = refs[n]
        n += 1
        rs_ref = None
        if with_rowsum:
            rs_ref = refs[n]
            n += 1
        acc_ref = refs[n]
        rs_acc = refs[n + 1] if with_rowsum else None
        p = pl.program_id(1)
        i, j = it_ref[p], jt_ref[p]

        @pl.when(j == 0)
        def _():
            acc_ref[...] = jnp.zeros_like(acc_ref)
            if with_rowsum:
                rs_acc[...] = jnp.zeros_like(rs_acc)

        def step(masked):
            _, ds, k, _ = _att_probs(att, q_ref[...], k_refs, v_ref, do_ref, lse_ref, dl_ref, cc_ref, cr_ref, masked)
            acc_ref[...] += jnp.dot(ds.astype(BF16), k, preferred_element_type=F32)
            if with_rowsum:
                rs_acc[...] += jnp.sum(ds, axis=-1, keepdims=True)

        @pl.when(j < i)
        def _():
            step(False)

        @pl.when(j == i)
        def _():
            step(True)
            dq_ref[...] = (acc_ref[...] * att.scale).astype(dq_ref.dtype)
            if with_rowsum:
                rs_ref[...] = rs_acc[...]

    do_op = (do, att.dv, 0, True)
    in_specs = ([att.q_spec(att.q)] + [att.k_spec(k) for k in att.ks]
                + [att.k_spec(att.v), att.q_spec(do_op), att.col_q(), att.col_q()])
    args = [att.q[0]] + [k[0] for k in att.ks] + [att.v[0], do, lse, delta]
    if att.has_bias:
        in_specs += [att.col_q(), att.row_k()]
        args += [att.cum_col, att.cum_row]
    out_specs = [pl.BlockSpec((T, att.dq), lambda h, p, it, jt: (it[p], h))]
    out_shape = [jax.ShapeDtypeStruct((S, H * att.dq), out_dtype)]
    scratch = [pltpu.VMEM((T, att.dq), F32)]
    if with_rowsum:
        out_specs.append(att.col_q())
        out_shape.append(jax.ShapeDtypeStruct((H, S, 1), F32))
        scratch.append(pltpu.VMEM((T, 1), F32))
    outs = pl.pallas_call(
        body, name=name,
        grid_spec=pltpu.PrefetchScalarGridSpec(
            num_scalar_prefetch=2, grid=(H, npairs), in_specs=in_specs, out_specs=out_specs,
            scratch_shapes=scratch),
        out_shape=out_shape,
        compiler_params=_params(("parallel", "arbitrary")),
    )(it, jt, *args)
    return outs if with_rowsum else outs[0]


def _att_bwd_kv(att, do, lse, delta, dk_dtypes, name):
    S, H, T = att.S, att.H, att.T
    it, jt, npairs = _pairs(att.nb, by_key=True)
    nk = len(att.ks)
    last = att.nb - 1
    widths = [k[1] for k in att.ks]

    def body(it_ref, jt_ref, *refs):
        q_ref = refs[0]
        k_refs = refs[1:1 + nk]
        v_ref, do_ref, lse_ref, dl_ref = refs[1 + nk:5 + nk]
        n = 5 + nk
        cc_ref = cr_ref = None
        if att.has_bias:
            cc_ref, cr_ref = refs[n], refs[n + 1]
            n += 2
        dk_refs = refs[n:n + nk]
        dv_ref = refs[n + nk]
        n += nk + 1
        dc_ref = None
        if att.has_bias:
            dc_ref = refs[n]
            n += 1
        dk_acc, dv_acc = refs[n], refs[n + 1]
        dc_acc = refs[n + 2] if att.has_bias else None
        p = pl.program_id(1)
        i, j = it_ref[p], jt_ref[p]

        @pl.when(i == j)
        def _():
            dk_acc[...] = jnp.zeros_like(dk_acc)
            dv_acc[...] = jnp.zeros_like(dv_acc)
            if att.has_bias:
                dc_acc[...] = jnp.zeros_like(dc_acc)

        def step(masked):
            q = q_ref[...]
            pr, ds, _, do_v = _att_probs(att, q, k_refs, v_ref, do_ref, lse_ref, dl_ref, cc_ref, cr_ref, masked)
            tn = (((0,), (0,)), ((), ()))
            dv_acc[...] += lax.dot_general(pr.astype(BF16), do_v, tn, preferred_element_type=F32)
            dk_acc[...] += lax.dot_general(ds.astype(BF16), q, tn, preferred_element_type=F32)
            if att.has_bias:
                dc_acc[...] -= jnp.sum(ds, axis=0, keepdims=True)

        @pl.when(i > j)
        def _():
            step(False)

        @pl.when(i == j)
        def _():
            step(True)

        @pl.when(i == last)
        def _():
            dk = dk_acc[...] * att.scale
            off = 0
            for r, w in zip(dk_refs, widths):
                r[...] = dk[:, off:off + w].astype(r.dtype)
                off += w
            dv_ref[...] = dv_acc[...].astype(dv_ref.dtype)
            if att.has_bias:
                dc_ref[...] = dc_acc[...]

    do_op = (do, att.dv, 0, True)
    in_specs = ([att.q_spec(att.q)] + [att.k_spec(k) for k in att.ks]
                + [att.k_spec(att.v), att.q_spec(do_op), att.col_q(), att.col_q()])
    args = [att.q[0]] + [k[0] for k in att.ks] + [att.v[0], do, lse, delta]
    if att.has_bias:
        in_specs += [att.col_q(), att.row_k()]
        args += [att.cum_col, att.cum_row]
    out_specs = [pl.BlockSpec((T, w), lambda h, p, it, jt: (jt[p], h)) for w in widths]
    out_specs.append(pl.BlockSpec((T, att.dv), lambda h, p, it, jt: (jt[p], h)))
    out_shape = [jax.ShapeDtypeStruct((S, H * w), dt) for w, dt in zip(widths, dk_dtypes)]
    out_shape.append(jax.ShapeDtypeStruct((S, H * att.dv), BF16))
    scratch = [pltpu.VMEM((T, att.dq), F32), pltpu.VMEM((T, att.dv), F32)]
    if att.has_bias:
        out_specs.append(att.row_k())
        out_shape.append(jax.ShapeDtypeStruct((H, 1, S), F32))
        scratch.append(pltpu.VMEM((1, T), F32))
    return pl.pallas_call(
        body, name=name,
        grid_spec=pltpu.PrefetchScalarGridSpec(
            num_scalar_prefetch=2, grid=(H, npairs), in_specs=in_specs, out_specs=out_specs,
            scratch_shapes=scratch),
        out_shape=out_shape,
        compiler_params=_params(("parallel", "arbitrary")),
    )(it, jt, *args)


LOG2E = 1.4426950408889634
QSUB = 256

_NT = (((1,), (1,)), ((), ()))
_TN = (((0,), (0,)), ((), ()))


class _AttT:
    def __init__(self, S, n_heads, q, ks, v, scale, chunk_causal, cum_rep=None):
        self.S, self.H, self.q, self.ks, self.v = S, n_heads, q, ks, v
        self.scale, self.chunk_causal, self.cum_rep = scale, chunk_causal, cum_rep
        self.T = _tile(S, ATT_T)
        self.qs = min(QSUB, self.T)
        self.nb = S // self.T
        self.dq, self.dv = q[1], v[1]
        self.has_bias = cum_rep is not None

    def q_spec(self, op):
        _, w, off, per_head = op
        return pl.BlockSpec((self.T, w), lambda h, p, it, jt: (it[p], off + (h if per_head else 0)))

    def k_spec(self, op):
        _, w, off, per_head = op
        return pl.BlockSpec((self.T, w), lambda h, p, it, jt: (jt[p], off + (h if per_head else 0)))

    def row_q(self):
        return pl.BlockSpec((None, 1, self.T), lambda h, p, it, jt: (h, 0, it[p]))

    def cum_k(self):
        return pl.BlockSpec((None, self.T, self.qs), lambda h, p, it, jt: (h, jt[p], 0))

    def sub_blocks(self, masked):
        return [(q0, min(self.T, q0 + self.qs) if masked else self.T) for q0 in range(0, self.T, self.qs)]

    def scores(self, k, q_sub, cum, q0, masked):
        s = lax.dot_general(k, q_sub, _NT, preferred_element_type=F32) * (self.scale * LOG2E)
        if self.has_bias:
            s = s - cum
        mask = None
        if masked:
            r = lax.broadcasted_iota(jnp.int32, s.shape, 0)
            c = lax.broadcasted_iota(jnp.int32, s.shape, 1) + q0
            mask = (r // CHUNK <= c // CHUNK) if self.chunk_causal else (r <= c)
        return s, mask


def _join(k_refs):
    return k_refs[0][...] if len(k_refs) == 1 else jnp.concatenate([r[...] for r in k_refs], axis=-1)


def _att_fwd_t(att, name, exact=False):
    S, H, T, qs = att.S, att.H, att.T, att.qs
    it, jt, npairs = _pairs(att.nb, by_key=False)
    nk = len(att.ks)

    def body(it_ref, jt_ref, *refs):
        q_ref = refs[0]
        k_refs = refs[1:1 + nk]
        v_ref = refs[1 + nk]
        n = 2 + nk
        cum_ref = None
        if att.has_bias:
            cum_ref = refs[n]
            n += 1
        o_ref = refs[n]
        n += 1
        ox_ref = None
        if exact:
            ox_ref = refs[n]
            n += 1
        lse_ref, m_ref, l_ref, acc_ref = refs[n:n + 4]
        lo_ref = refs[n + 4] if exact else None
        p = pl.program_id(1)
        i, j = it_ref[p], jt_ref[p]

        @pl.when(j == 0)
        def _():
            m_ref[...] = jnp.full_like(m_ref, -jnp.inf)
            l_ref[...] = jnp.zeros_like(l_ref)
            acc_ref[...] = jnp.zeros_like(acc_ref)
            if exact:
                lo_ref[...] = jnp.zeros_like(lo_ref)

        def step(masked):
            k = _join(k_refs)
            v = v_ref[...]
            subs = att.sub_blocks(masked)

            def logits(idx):
                q0, nkeys = subs[idx]
                cum = cum_ref[0:nkeys, :] if att.has_bias else None
                return att.scores(k[0:nkeys], q_ref[q0:q0 + qs, :], cum, q0, masked)

            ahead = logits(0)
            for idx, (q0, nkeys) in enumerate(subs):
                qsl = slice(q0, q0 + qs)
                s, mask = ahead
                if idx + 1 < len(subs):
                    ahead = logits(idx + 1)
                if masked:
                    s = jnp.where(mask, s, -jnp.inf)
                m_prev = m_ref[:, qsl]
                m_new = jnp.maximum(m_prev, jnp.max(s, axis=0, keepdims=True))
                alpha = jnp.exp2(m_prev - m_new)
                pr = jnp.exp2(s - m_new)
                l_ref[:, qsl] = alpha * l_ref[:, qsl] + jnp.sum(pr, axis=0, keepdims=True)
                p_hi = pr.astype(BF16)
                acc_ref[:, qsl] = alpha * acc_ref[:, qsl] + lax.dot_general(
                    v[0:nkeys], p_hi, _TN, preferred_element_type=F32)
                if exact:
                    p_lo = (pr - p_hi.astype(F32)).astype(BF16)
                    lo_ref[:, qsl] = alpha * lo_ref[:, qsl] + lax.dot_general(
                        v[0:nkeys], p_lo, _TN, preferred_element_type=F32)
                m_ref[:, qsl] = m_new

        @pl.when(j < i)
        def _():
            step(False)

        @pl.when(j == i)
        def _():
            step(True)
            l = l_ref[...]
            inv = 1.0 / l
            o_ref[...] = jnp.transpose(acc_ref[...] * inv).astype(o_ref.dtype)
            if exact:
                ox_ref[...] = jnp.transpose((acc_ref[...] + lo_ref[...]) * inv)
            lse_ref[...] = m_ref[...] + jnp.log2(l)

    in_specs = [att.q_spec(att.q)] + [att.k_spec(k) for k in att.ks] + [att.k_spec(att.v)]
    args = [att.q[0]] + [k[0] for k in att.ks] + [att.v[0]]
    if att.has_bias:
        in_specs.append(att.cum_k())
        args.append(att.cum_rep)
    o_spec = pl.BlockSpec((T, att.dv), lambda h, p, it, jt: (it[p], h))
    out_specs = [o_spec]
    out_shape = [jax.ShapeDtypeStruct((S, H * att.dv), BF16)]
    scratch = [pltpu.VMEM((1, T), F32), pltpu.VMEM((1, T), F32), pltpu.VMEM((att.dv, T), F32)]
    if exact:
        out_specs.append(o_spec)
        out_shape.append(jax.ShapeDtypeStruct((S, H * att.dv), F32))
        scratch.append(pltpu.VMEM((att.dv, T), F32))
    out_specs.append(att.row_q())
    out_shape.append(jax.ShapeDtypeStruct((H, 1, S), F32))
    return pl.pallas_call(
        body, name=name,
        grid_spec=pltpu.PrefetchScalarGridSpec(
            num_scalar_prefetch=2, grid=(H, npairs), in_specs=in_specs, out_specs=out_specs,
            scratch_shapes=scratch),
        out_shape=out_shape,
        compiler_params=_params(("parallel", "arbitrary")),
    )(it, jt, *args)


def _att_delta_t(do, o, n_heads, name, order=None):
    S = do.shape[0]
    w = do.shape[1] // n_heads
    ts = _tile(S, ATT_T)
    ones = jnp.ones((8, w), BF16)
    extra = [] if order is None else [order]

    def body(do_ref, o_ref, ones_ref, *rest):
        d_ref = rest[-1]
        prod = do_ref[...].astype(F32) * o_ref[...].astype(F32)
        acc = jnp.zeros((8, ts), F32)
        for part in _split3(prod):
            acc = acc + lax.dot_general(ones_ref[...], part, _NT, preferred_element_type=F32)
        d_ref[...] = acc[0:1, :]

    blk = pl.BlockSpec((ts, w), lambda i, h: (i, h))
    return pl.pallas_call(
        body, name=name, grid=(S // ts, n_heads),
        in_specs=[blk, blk, pl.BlockSpec((8, w), lambda i, h: (0, 0))] + [pl.BlockSpec(memory_space=pl.ANY)] * len(extra),
        out_specs=pl.BlockSpec((None, 1, ts), lambda i, h: (h, 0, i)),
        out_shape=jax.ShapeDtypeStruct((n_heads, 1, S), F32),
        compiler_params=_params(("parallel", "parallel")),
    )(do, o, ones, *extra)


def _att_bwd_t(att, do, lse, delta, dq_dtype, dk_dtypes, name, dq_rope=None):
    S, H, T, qs = att.S, att.H, att.T, att.qs
    it, jt, npairs = _pairs(att.nb, by_key=True)
    nk = len(att.ks)
    last = att.nb - 1
    widths = [k[1] for k in att.ks]

    def body(it_ref, jt_ref, *refs):
        q_ref = refs[0]
        k_refs = refs[1:1 + nk]
        v_ref, do_ref, lse_ref, dl_ref = refs[1 + nk:5 + nk]
        n = 5 + nk
        cum_ref = None
        if att.has_bias:
            cum_ref = refs[n]
            n += 1
        rope_refs = None
        if dq_rope is not None:
            rope_refs = refs[n:n + 3]
            n += 3
        dq_ref = refs[n]
        dk_refs = refs[n + 1:n + 1 + nk]
        dv_ref = refs[n + 1 + nk]
        n += nk + 2
        dc_ref = None
        if att.has_bias:
            dc_ref = refs[n]
            n += 1
        dq_acc, dk_acc, dv_acc = refs[n:n + 3]
        dc_acc = refs[n + 3] if att.has_bias else None
        p = pl.program_id(1)
        i, j = it_ref[p], jt_ref[p]

        @pl.when(p == 0)
        def _():
            dq_acc[...] = jnp.zeros_like(dq_acc)

        @pl.when(i == j)
        def _():
            dk_acc[...] = jnp.zeros_like(dk_acc)
            dv_acc[...] = jnp.zeros_like(dv_acc)
            if att.has_bias:
                dc_acc[...] = jnp.zeros_like(dc_acc)

        def step(masked):
            k = _join(k_refs)
            v = v_ref[...]
            subs = att.sub_blocks(masked)

            def logits(idx):
                q0, nkeys = subs[idx]
                cum = cum_ref[0:nkeys, :] if att.has_bias else None
                return att.scores(k[0:nkeys], q_ref[q0:q0 + qs, :], cum, q0, masked)

            ahead = logits(0)
            for idx, (q0, nkeys) in enumerate(subs):
                qsl = slice(q0, q0 + qs)
                ksl = slice(0, nkeys)
                q_sub = q_ref[qsl, :]
                do_sub = do_ref[qsl, :]
                s, mask = ahead
                if idx + 1 < len(subs):
                    ahead = logits(idx + 1)
                pr = jnp.exp2(s - lse_ref[:, qsl])
                if masked:
                    pr = jnp.where(mask, pr, 0.0)
                dp = lax.dot_general(v[ksl], do_sub, _NT, preferred_element_type=F32)
                ds = pr * (dp - dl_ref[:, qsl])
                ds_b = ds.astype(BF16)
                dv_acc[ksl, :] += jnp.dot(pr.astype(BF16), do_sub, preferred_element_type=F32)
                dk_acc[ksl, :] += jnp.dot(ds_b, q_sub, preferred_element_type=F32)
                dq_acc[i, :, qsl] += lax.dot_general(k[ksl], ds_b, _TN, preferred_element_type=F32)
                if att.has_bias:
                    part = ds[:, 0:LANE] if qs >= LANE else ds
                    for c0 in range(LANE, qs, LANE):
                        part = part + ds[:, c0:c0 + LANE]
                    dc_acc[ksl, :] += part

        @pl.when(i > j)
        def _():
            step(False)

        @pl.when(i == j)
        def _():
            step(True)
            dq = jnp.transpose(dq_acc[i] * att.scale)
            if dq_rope is not None:
                dq = _rope(dq, rope_refs[0][...], rope_refs[1][...], rope_refs[2][...], -1)
            dq_ref[...] = dq.astype(dq_ref.dtype)

        @pl.when(i == last)
        def _():
            dk = dk_acc[...] * att.scale
            off = 0
            for r, w in zip(dk_refs, widths):
                r[...] = dk[:, off:off + w].astype(r.dtype)
                off += w
            dv_ref[...] = dv_acc[...].astype(dv_ref.dtype)
            if att.has_bias:
                dc_ref[...] = -jnp.sum(dc_acc[...], axis=-1, keepdims=True)

    do_op = (do, att.dv, 0, True)
    in_specs = ([att.q_spec(att.q)] + [att.k_spec(k) for k in att.ks]
                + [att.k_spec(att.v), att.q_spec(do_op), att.row_q(), att.row_q()])
    args = [att.q[0]] + [k[0] for k in att.ks] + [att.v[0], do, lse, delta]
    if att.has_bias:
        in_specs.append(att.cum_k())
        args.append(att.cum_rep)
    if dq_rope is not None:
        in_specs += [pl.BlockSpec((T, att.dq), lambda h, p, it, jt: (jt[p], 0))] * 3
        args += list(dq_rope)
    out_specs = [pl.BlockSpec((T, att.dq), lambda h, p, it, jt: (jt[p], h))]
    out_shape = [jax.ShapeDtypeStruct((S, H * att.dq), dq_dtype)]
    out_specs += [pl.BlockSpec((T, w), lambda h, p, it, jt: (jt[p], h)) for w in widths]
    out_shape += [jax.ShapeDtypeStruct((S, H * w), dt) for w, dt in zip(widths, dk_dtypes)]
    out_specs.append(pl.BlockSpec((T, att.dv), lambda h, p, it, jt: (jt[p], h)))
    out_shape.append(jax.ShapeDtypeStruct((S, H * att.dv), BF16))
    scratch = [pltpu.VMEM((att.nb, att.dq, T), F32), pltpu.VMEM((T, att.dq), F32), pltpu.VMEM((T, att.dv), F32)]
    if att.has_bias:
        out_specs.append(pl.BlockSpec((None, T, 1), lambda h, p, it, jt: (h, jt[p], 0)))
        out_shape.append(jax.ShapeDtypeStruct((H, S, 1), F32))
        scratch.append(pltpu.VMEM((T, min(qs, LANE)), F32))
    return pl.pallas_call(
        body, name=name,
        grid_spec=pltpu.PrefetchScalarGridSpec(
            num_scalar_prefetch=2, grid=(H, npairs), in_specs=in_specs, out_specs=out_specs,
            scratch_shapes=scratch),
        out_shape=out_shape,
        compiler_params=_params(("parallel", "arbitrary")),
    )(it, jt, *args)


def _adamw(w, g, m, v, name):
    R, C = w.shape
    tr = _tile(R, 256, 8)
    c1 = 1.0 - ADAM_B1 ** ADAM_STEP
    c2 = 1.0 - ADAM_B2 ** ADAM_STEP

    def body(w_ref, g_ref, m_ref, v_ref, d_ref, nm_ref, nv_ref):
        gv = g_ref[...]
        nm = ADAM_B1 * m_ref[...] + (1.0 - ADAM_B1) * gv
        nv = ADAM_B2 * v_ref[...] + (1.0 - ADAM_B2) * (gv * gv)
        d_ref[...] = -ADAM_LR * ((nm / c1) / (jnp.sqrt(nv / c2) + ADAM_EPS) + ADAM_WD * w_ref[...])
        nm_ref[...] = nm
        nv_ref[...] = nv

    blk = pl.BlockSpec((tr, C), lambda i: (i, 0))
    return pl.pallas_call(
        body, name=name, grid=(R // tr,),
        in_specs=[blk] * 4, out_specs=[blk] * 3,
        out_shape=[jax.ShapeDtypeStruct((R, C), F32)] * 3,
        compiler_params=_params(("parallel",)),
    )(w, g, m, v)


def _place():
    return lax.axis_index("x"), lax.axis_index("y"), lax.axis_index("c")


def _other_chips(x, y):
    return [(1 - x, y), (x, 1 - y), (1 - x, 1 - y)]


_HBM_SPEC = pl.BlockSpec(memory_space=pltpu.HBM)
_SEM_SPEC = pl.BlockSpec(memory_space=pltpu.SEMAPHORE)
_ANY_SPEC = pl.BlockSpec(memory_space=pl.ANY)
_EFFECT = pltpu.SideEffectType.DATAFLOW_SIDE_EFFECTING


def _chip_copies(src_ref, land_ref, sems, gather):
    x, y, c = _place()
    me = 2 * x + y
    out, back = [], []
    for n, (px, py) in enumerate(_other_chips(x, y)):
        src = src_ref if gather else src_ref.at[2 * px + py]
        out.append(pltpu.make_async_remote_copy(
            src_ref=src, dst_ref=land_ref.at[me] if gather else land_ref.at[n],
            send_sem=sems[n], recv_sem=sems[3 + n], device_id=(px, py, c), device_id_type=MESH))
        back.append(pltpu.make_async_remote_copy(
            src_ref=src, dst_ref=land_ref.at[2 * px + py] if gather else land_ref.at[n],
            send_sem=sems[n], recv_sem=sems[3 + n], device_id=(px, py, c), device_id_type=MESH))
    return out, back


def _xchg_start(src, land, gather, order, name):
    def body(src_ref, land_ref, order_ref, *outs):
        sems = outs[0:6]
        token = outs[8]
        out, _ = _chip_copies(src_ref, land_ref, sems, gather)
        for cp in out:
            cp.start()
        token[...] = jnp.zeros_like(token)

    outs = pl.pallas_call(
        body, name=name,
        out_shape=(pltpu.SemaphoreType.DMA(()),) * 6 + (
            pltpu.HBM(src.shape, src.dtype), pltpu.HBM(land.shape, land.dtype),
            jax.ShapeDtypeStruct((8, LANE), F32)),
        in_specs=(_HBM_SPEC, _HBM_SPEC, _ANY_SPEC),
        out_specs=(_SEM_SPEC,) * 6 + (_HBM_SPEC, _HBM_SPEC, pl.BlockSpec(memory_space=pltpu.VMEM)),
        input_output_aliases={0: 6, 1: 7},
        compiler_params=pltpu.CompilerParams(has_side_effects=_EFFECT),
    )(pltpu.with_memory_space_constraint(src, pltpu.HBM), pltpu.with_memory_space_constraint(land, pltpu.HBM), order)
    return outs[0:6], outs[6], outs[7], outs[8]


def _xchg_wait(started, gather, after, name):
    sems, src, land, _ = started

    def body(src_ref, land_ref, *rest):
        _, back = _chip_copies(src_ref, land_ref, rest[0:6], gather)
        for cp in back:
            cp.wait_send()
            cp.wait_recv()

    return pl.pallas_call(
        body, name=name,
        out_shape=(pltpu.HBM(src.shape, src.dtype), pltpu.HBM(land.shape, land.dtype)),
        in_specs=(_HBM_SPEC, _HBM_SPEC) + (_SEM_SPEC,) * 6 + (_ANY_SPEC,),
        out_specs=(_HBM_SPEC, _HBM_SPEC),
        input_output_aliases={0: 0, 1: 1},
        compiler_params=pltpu.CompilerParams(has_side_effects=_EFFECT),
    )(src, land, *sems, after)


def _reduce_scatter(gp, gp_b, recv_b, vec, name):
    _, R, C = gp.shape
    RB = gp_b.shape[1]
    VR, W = vec.shape
    tr = _tile(R, PACK_ROWS, 16)
    assert R % tr == 0 and RB % tr == 0
    nchunk = R // tr
    nchunk_b = RB // tr
    RT = R + RB

    def body(gp_ref, gpb_ref, recvb_ref, vec_ref, out_ref, recv_ref, part_ref, sib_ref, vall_ref, vout_ref,
             buf_ref, acc_ref, send_sems, recv_sems, sib_sems, vsend_sems, vrecv_sems):
        x, y, c = _place()
        me = 2 * x + y
        chips = _other_chips(x, y)

        vall_ref[4 * x + 2 * y + c] = vec_ref[...]
        vsends = []
        for r in range(1, N_DEV):
            dx, dy, dc = (r >> 2) & 1, (r >> 1) & 1, r & 1
            peer = (x ^ dx, y ^ dy, c ^ dc)
            cp = pltpu.make_async_remote_copy(
                src_ref=vec_ref, dst_ref=vall_ref.at[4 * x + 2 * y + c], send_sem=vsend_sems.at[r - 1],
                recv_sem=vrecv_sems.at[r - 1], device_id=peer, device_id_type=MESH)
            cp.start()
            vsends.append(cp)
        sends = []
        for n, (px, py) in enumerate(chips):
            cp = pltpu.make_async_remote_copy(
                src_ref=gp_ref.at[2 * px + py], dst_ref=recv_ref.at[n], send_sem=send_sems.at[n],
                recv_sem=recv_sems.at[n], device_id=(px, py, c), device_id_type=MESH)
            cp.start()
            sends.append(cp)

        def sum_four(own_ref, got_ref, base):
            def one(i, carry):
                rows = pl.ds(pl.multiple_of(i * tr, tr), tr)
                pltpu.sync_copy(own_ref.at[me, rows], buf_ref.at[0])
                for n in range(3):
                    pltpu.sync_copy(got_ref.at[n, rows], buf_ref.at[n + 1])
                acc = buf_ref[0].astype(F32)
                for n in range(3):
                    acc = acc + buf_ref[n + 1].astype(F32)
                acc_ref[0] = acc
                pltpu.sync_copy(acc_ref.at[0], part_ref.at[pl.ds(pl.multiple_of(base + i * tr, tr), tr)])
                return carry
            return one

        lax.fori_loop(0, nchunk_b, sum_four(gpb_ref, recvb_ref, R), 0)
        for n, (px, py) in enumerate(chips):
            pltpu.make_async_remote_copy(
                src_ref=gp_ref.at[me], dst_ref=recv_ref.at[n], send_sem=send_sems.at[n],
                recv_sem=recv_sems.at[n], device_id=(px, py, c), device_id_type=MESH).wait_recv()
        lax.fori_loop(0, nchunk, sum_four(gp_ref, recv_ref, 0), 0)

        swap = pltpu.make_async_remote_copy(
            src_ref=part_ref, dst_ref=sib_ref, send_sem=sib_sems.at[0], recv_sem=sib_sems.at[1],
            device_id=(x, y, 1 - c), device_id_type=MESH)
        swap.start()
        swap.wait()

        def sum_two(i, carry):
            rows = pl.ds(pl.multiple_of(i * tr, tr), tr)
            pltpu.sync_copy(part_ref.at[rows], acc_ref.at[0])
            pltpu.sync_copy(sib_ref.at[rows], acc_ref.at[1])
            acc_ref[0] = acc_ref[0] + acc_ref[1]
            pltpu.sync_copy(acc_ref.at[0], out_ref.at[rows])
            return carry

        lax.fori_loop(0, nchunk + nchunk_b, sum_two, 0)

        for r in range(1, N_DEV):
            dx, dy, dc = (r >> 2) & 1, (r >> 1) & 1, r & 1
            peer = (x ^ dx, y ^ dy, c ^ dc)
            pltpu.make_async_remote_copy(
                src_ref=vec_ref, dst_ref=vall_ref.at[4 * peer[0] + 2 * peer[1] + peer[2]],
                send_sem=vsend_sems.at[r - 1], recv_sem=vrecv_sems.at[r - 1],
                device_id=peer, device_id_type=MESH).wait_recv()
        total = vall_ref[0]
        for d in range(1, N_DEV):
            total = total + vall_ref[d]
        vout_ref[...] = total
        for cp in sends + vsends:
            cp.wait_send()

    hbm = pl.BlockSpec(memory_space=pl.ANY)
    vmem = pl.BlockSpec(memory_space=pltpu.VMEM)
    outs = pl.pallas_call(
        body, name=name,
        in_specs=[hbm, hbm, hbm, vmem],
        out_specs=[hbm, hbm, hbm, hbm, vmem, vmem],
        out_shape=[jax.ShapeDtypeStruct((RT, C), F32), jax.ShapeDtypeStruct((3, R, C), gp.dtype),
                   jax.ShapeDtypeStruct((RT, C), F32), jax.ShapeDtypeStruct((RT, C), F32),
                   jax.ShapeDtypeStruct((N_DEV, VR, W), F32), jax.ShapeDtypeStruct((VR, W), F32)],
        scratch_shapes=[pltpu.VMEM((4, tr, C), gp.dtype), pltpu.VMEM((2, tr, C), F32),
                        pltpu.SemaphoreType.DMA((3,)), pltpu.SemaphoreType.DMA((3,)), pltpu.SemaphoreType.DMA((2,)),
                        pltpu.SemaphoreType.DMA((N_DEV - 1,)), pltpu.SemaphoreType.DMA((N_DEV - 1,))],
        compiler_params=pltpu.CompilerParams(vmem_limit_bytes=VMEM_LIMIT),
    )(gp, gp_b, recv_b, vec)
    return outs[0], outs[5]


def _sib_copy(src_ref, land_ref, send_sem, recv_sem):
    x, y, c = _place()
    return pltpu.make_async_remote_copy(src_ref=src_ref, dst_ref=land_ref, send_sem=send_sem, recv_sem=recv_sem,
                                        device_id=(x, y, 1 - c), device_id_type=MESH)


def _sib_start(src, name):
    land = lax.empty(src.shape, src.dtype)

    def body(src_ref, land_ref, send_sem, recv_sem, src_thru, land_thru, token):
        _sib_copy(src_ref, land_ref, send_sem, recv_sem).start()
        token[...] = jnp.zeros_like(token)

    outs = pl.pallas_call(
        body, name=name,
        out_shape=(pltpu.SemaphoreType.DMA(()), pltpu.SemaphoreType.DMA(()),
                   pltpu.HBM(src.shape, src.dtype), pltpu.HBM(land.shape, land.dtype),
                   jax.ShapeDtypeStruct((8, LANE), F32)),
        in_specs=(_HBM_SPEC, _HBM_SPEC),
        out_specs=(_SEM_SPEC, _SEM_SPEC, _HBM_SPEC, _HBM_SPEC, pl.BlockSpec(memory_space=pltpu.VMEM)),
        input_output_aliases={0: 2, 1: 3},
        compiler_params=pltpu.CompilerParams(has_side_effects=_EFFECT),
    )(pltpu.with_memory_space_constraint(src, pltpu.HBM), pltpu.with_memory_space_constraint(land, pltpu.HBM))
    return outs


def _sib_wait(started, after, name):
    send_sem, recv_sem, src, land, _ = started

    def body(src_ref, land_ref, send_sem, recv_sem, after_ref, src_out, land_out):
        cp = _sib_copy(src_ref, land_ref, send_sem, recv_sem)
        cp.wait_send()
        cp.wait_recv()

    return pl.pallas_call(
        body, name=name,
        out_shape=(pltpu.HBM(src.shape, src.dtype), pltpu.HBM(land.shape, land.dtype)),
        in_specs=(_HBM_SPEC, _HBM_SPEC, _SEM_SPEC, _SEM_SPEC, _ANY_SPEC),
        out_specs=(_HBM_SPEC, _HBM_SPEC),
        input_output_aliases={0: 0, 1: 1},
        compiler_params=pltpu.CompilerParams(has_side_effects=_EFFECT),
    )(src, land, send_sem, recv_sem, after)


def _sum_slabs(gp, recv, chip, name):
    _, R, C = gp.shape
    tr = _tile(R, PACK_ROWS, 16)

    def body(chip_ref, own_ref, r0_ref, r1_ref, r2_ref, o_ref):
        acc = own_ref[...].astype(F32) + r0_ref[...].astype(F32)
        o_ref[...] = (acc + r1_ref[...].astype(F32)) + r2_ref[...].astype(F32)

    def got(n):
        return pl.BlockSpec((None, tr, C), lambda i, chip_ref: (n, i, 0))

    return pl.pallas_call(
        body, name=name,
        grid_spec=pltpu.PrefetchScalarGridSpec(
            num_scalar_prefetch=1, grid=(R // tr,),
            in_specs=[pl.BlockSpec((None, tr, C), lambda i, chip_ref: (chip_ref[0], i, 0)), got(0), got(1), got(2)],
            out_specs=pl.BlockSpec((tr, C), lambda i, chip_ref: (i, 0))),
        out_shape=jax.ShapeDtypeStruct((R, C), F32),
        compiler_params=_params(("parallel",)),
    )(jnp.reshape(chip, (1,)).astype(jnp.int32), gp, recv, recv, recv)


def _add2(a, b, name):
    R, C = a.shape
    tr = _tile(R, PACK_ROWS, 8)

    def body(a_ref, b_ref, o_ref):
        o_ref[...] = a_ref[...] + b_ref[...]

    blk = pl.BlockSpec((tr, C), lambda i: (i, 0))
    return pl.pallas_call(
        body, name=name, grid=(R // tr,), in_specs=[blk, blk], out_specs=blk,
        out_shape=jax.ShapeDtypeStruct((R, C), F32), compiler_params=_params(("parallel",)),
    )(a, b)


def _all_reduce_vec(vec, name):
    VR, W = vec.shape

    def body(vec_ref, vall_ref, vout_ref, vsend_sems, vrecv_sems):
        x, y, c = _place()
        vall_ref[4 * x + 2 * y + c] = vec_ref[...]
        sends = []
        peers = []
        for r in range(1, N_DEV):
            dx, dy, dc = (r >> 2) & 1, (r >> 1) & 1, r & 1
            peer = (x ^ dx, y ^ dy, c ^ dc)
            peers.append(peer)
            cp = pltpu.make_async_remote_copy(
                src_ref=vec_ref, dst_ref=vall_ref.at[4 * x + 2 * y + c], send_sem=vsend_sems.at[r - 1],
                recv_sem=vrecv_sems.at[r - 1], device_id=peer, device_id_type=MESH)
            cp.start()
            sends.append(cp)
        for r, peer in enumerate(peers):
            pltpu.make_async_remote_copy(
                src_ref=vec_ref, dst_ref=vall_ref.at[4 * peer[0] + 2 * peer[1] + peer[2]],
                send_sem=vsend_sems.at[r], recv_sem=vrecv_sems.at[r],
                device_id=peer, device_id_type=MESH).wait_recv()
        total = vall_ref[0]
        for d in range(1, N_DEV):
            total = total + vall_ref[d]
        vout_ref[...] = total
        for cp in sends:
            cp.wait_send()

    vmem = pl.BlockSpec(memory_space=pltpu.VMEM)
    outs = pl.pallas_call(
        body, name=name,
        in_specs=[vmem], out_specs=[vmem, vmem],
        out_shape=[jax.ShapeDtypeStruct((N_DEV, VR, W), F32), jax.ShapeDtypeStruct((VR, W), F32)],
        scratch_shapes=[pltpu.SemaphoreType.DMA((N_DEV - 1,)), pltpu.SemaphoreType.DMA((N_DEV - 1,))],
    )(vec)
    return outs[1]


def _rope_tables(S):
    pos = jnp.arange(S, dtype=F32)
    inv = 1.0 / (ROPE_THETA ** (jnp.arange(0, MLA_ROPE, 2, dtype=F32) / MLA_ROPE))
    ang = pos[:, None] * inv[None, :]
    cos, sin = jnp.cos(ang), jnp.sin(ang)
    half = MLA_ROPE // 2
    z = jnp.zeros((S, half), F32)
    one = jnp.ones((S, LANE - MLA_ROPE), F32)
    zero = jnp.zeros((S, LANE - MLA_ROPE), F32)
    kc = jnp.concatenate([cos, cos, one], axis=1)
    ksa = jnp.concatenate([-sin, z, zero], axis=1)
    ksb = jnp.concatenate([z, sin, zero], axis=1)
    qc = jnp.concatenate([jnp.ones((S, MLA_NOPE), F32), kc], axis=1)
    qsa = jnp.concatenate([jnp.zeros((S, MLA_NOPE), F32), ksa], axis=1)
    qsb = jnp.concatenate([jnp.zeros((S, MLA_NOPE), F32), ksb], axis=1)
    return (kc, ksa, ksb), (qc, qsa, qsb)


def _pad_cols(a, width):
    return jnp.pad(a, ((0, 0), (0, width - a.shape[1])))


def kernel(x, attn_norm, w_in, fox_f_bias, q_norm, w_uq, kv_norm, w_ukv, w_mla_branch, w_fox_branch, w_out, mlp_norm, w_up, w_down, final_norm, loss_target, m_attn_norm, m_w_in, m_fox_f_bias, m_q_norm, m_w_uq, m_kv_norm, m_w_ukv, m_w_mla_branch, m_w_fox_branch, m_w_out, m_mlp_norm, m_w_up, m_w_down, m_final_norm, v_attn_norm, v_w_in, v_fox_f_bias, v_q_norm, v_w_uq, v_kv_norm, v_w_ukv, v_w_mla_branch, v_w_fox_branch, v_w_out, v_mlp_norm, v_w_up, v_w_down, v_final_norm):
    _, S, D = x.shape
    H, HF = MLA_HEADS, FOX_HEADS
    QL, KVL = MLA_Q_LORA, MLA_KV_LORA
    assert H == HF and H <= 8
    xs = x[0]
    target = loss_target[0]

    group_a = [("w_in", w_in, 1), ("w_uq", w_uq, 1), ("w_ukv", w_ukv, 1)]
    group_b = [("w_mla_branch", w_mla_branch, 1), ("w_fox_branch", w_fox_branch, 1), ("w_out", w_out, 0),
               ("w_up", w_up, 1), ("w_down", w_down, 0)]
    big = group_a + group_b
    C = D

    def pad16(a, axis):
        short = -a.shape[axis] % 16
        return jnp.pad(a, [(0, short if d == axis else 0) for d in range(a.ndim)])

    def layout(group):
        rows, offs, off = {}, {}, 0
        for nm, w, _ in group:
            assert w[0].size % C == 0, nm
            rows[nm] = w[0].size // C
            offs[nm] = off
            off += -(-rows[nm] // 16) * 16
        return rows, offs, off, -(-off // PACK_ROWS) * PACK_ROWS

    def pack_shards(group, lay):
        _, _, off, R = lay
        return jnp.concatenate([pad16(w[0].astype(BF16).reshape(-1, C), 0) for _, w, _ in group]
                               + [jnp.zeros((R - off, C), BF16)], axis=0)

    def unpack_full(group, lay, gathered, full):
        rows, offs, _, _ = lay
        for nm, w, axis in group:
            parts = [gathered[k, offs[nm]:offs[nm] + rows[nm]].reshape(w[0].shape) for k in range(N_CHIPS)]
            full[nm] = jnp.concatenate(parts, axis=axis)

    lay_a, lay_b = layout(group_a), layout(group_b)
    RA, RB = lay_a[3], lay_b[3]
    chip = 2 * lax.axis_index("x") + lax.axis_index("y")
    wp_a, wp_b = pack_shards(group_a, lay_a), pack_shards(group_b, lay_b)
    ag_a = _xchg_start(wp_a, lax.empty((N_CHIPS, RA, C), BF16), True, jnp.zeros((8, LANE), F32), "all_gather_start_a")
    ag_b = _xchg_start(wp_b, lax.empty((N_CHIPS, RB, C), BF16), True, ag_a[3], "all_gather_start_b")
    xn = _norm_fwd(xs, attn_norm, "attn_norm_fwd", order=ag_b[3])
    full = {}
    own_a, land_a = _xchg_wait(ag_a, True, xn, "all_gather_wait_a")
    unpack_full(group_a, lay_a, lax.dynamic_update_slice(land_a, own_a[None], (chip, 0, 0)), full)

    o_ckv = QL
    o_kr = o_ckv + KVL
    o_fq = o_kr + MLA_ROPE
    o_ff = o_fq + 3 * HF * FOX_HEAD_DIM
    o_g = o_ff + HF
    wi = full["w_in"]
    assert wi.shape[1] == o_g + 2 * D
    WS = QL + KVL + 2 * LANE
    NQKV = 3 * HF * FOX_HEAD_DIM
    w_small = jnp.concatenate([wi[:, :o_kr], _pad_cols(wi[:, o_kr:o_fq], LANE), _pad_cols(wi[:, o_ff:o_g], LANE)], axis=1)
    w_qkv = wi[:, o_fq:o_ff]
    w_g = wi[:, o_g:]
    w_pack = jnp.concatenate([w_small, w_qkv, w_g], axis=1)
    dqk = MLA_NOPE + MLA_ROPE
    w_uq_p = jnp.pad(full["w_uq"].reshape(QL, H, dqk), ((0, 0), (0, 0), (0, QPAD - dqk))).reshape(QL, H * QPAD)
    ukv = full["w_ukv"].reshape(KVL, H, MLA_NOPE + MLA_V)
    w_ukv_p = jnp.concatenate([ukv[:, :, :MLA_NOPE].reshape(KVL, H * MLA_NOPE),
                               ukv[:, :, MLA_NOPE:].reshape(KVL, H * MLA_V)], axis=1)

    (kc, ksa, ksb), (qc, qsa, qsb) = _rope_tables(S)
    bias_pad = _pad_cols(fox_f_bias, LANE)

    small = _matmul(xn, w_small, "nn", [F32], "proj_small")
    qkv = _matmul(xn, w_qkv, "nn", [BF16], "proj_qkv")
    gpre = _matmul(xn, w_g, "nn", [F32], "proj_gates")
    cqn, ckvn, kr, cum = _prep_fwd(small, q_norm, kv_norm, bias_pad, kc, ksa, ksb, HF, "prep_fwd")
    q_rot = _matmul(cqn, w_uq_p, "nn", [BF16], "mla_q_up", tn=QPAD, row_extras=(qc, qsa, qsb),
                    epilogue=lambda acc, c, sa, sb: (_rope(acc, c, sa, sb, 1),))
    kv2 = _matmul(ckvn, w_ukv_p, "nn", [BF16], "mla_kv_up")

    mla = _AttT(S, H, (q_rot, QPAD, 0, True), [(kv2, MLA_NOPE, 0, True), (kr, LANE, 0, False)],
                (kv2, MLA_V, H, True), 1.0 / math.sqrt(dqk), True)
    o_mla, lse_mla = _att_fwd_t(mla, "mla_att_fwd")

    cum_t = jnp.transpose(cum[:, :HF]) * LOG2E
    cum_rep = jnp.broadcast_to(cum_t[:, :, None], (HF, S, min(QSUB, _tile(S, ATT_T))))
    fox = _AttT(S, HF, (qkv, FOX_HEAD_DIM, 0, True), [(qkv, FOX_HEAD_DIM, HF, True)],
                (qkv, FOX_HEAD_DIM, 2 * HF, True), 1.0 / math.sqrt(FOX_HEAD_DIM), False, cum_rep)
    o_fox, ox_fox, lse_fox = _att_fwd_t(fox, "fox_att_fwd", exact=True)

    own_b, land_b = _xchg_wait(ag_b, True, lse_fox, "all_gather_wait_b")
    unpack_full(group_b, lay_b, lax.dynamic_update_slice(land_b, own_b[None], (chip, 0, 0)), full)
    w_mb, w_fb, w_o, w_u, w_d = (full[n] for n in ("w_mla_branch", "w_fox_branch", "w_out", "w_up", "w_down"))

    y_mla = _matmul(o_mla, w_mb, "nn", [F32], "mla_branch")
    y_fox = _matmul(o_fox, w_fb, "nn", [F32], "fox_branch")
    merged = _gate_fwd(gpre, y_mla, y_fox, "gate_fwd")
    h1 = _matmul(merged, w_o, "nn", [F32], "out_proj", extras=(xs,), epilogue=lambda acc, r: (acc + r,))
    hn = _norm_fwd(h1, mlp_norm, "mlp_norm_fwd")

    def relu2(acc):
        a = jnp.maximum(acc, 0.0)
        return a * a, a

    u, a_pos = _matmul(hn, w_u, "nn", [BF16, BF16], "mlp_up", epilogue=relu2)
    h2 = _matmul(u, w_d, "nn", [F32], "mlp_down", extras=(h1,), epilogue=lambda acc, r: (acc + r,))
    dh2, g_final, loss_part = _final(h2, final_norm.reshape(1, D), target, "final_norm_loss")

    dh2_b = dh2.astype(BF16)
    da = _matmul(dh2_b, w_d, "nt", [BF16], "mlp_down_dx", extras=(a_pos,),
                 epilogue=lambda acc, a: (acc * (2.0 * a.astype(F32)),))
    g_w_down = _mm_tn(u, dh2_b, "mlp_down_dw")
    dhn = _matmul(da, w_u, "nt", [F32], "mlp_up_dx")
    g_w_up = _mm_tn(hn, da, "mlp_up_dw")
    dh1, g_mlp_norm = _norm_bwd(h1, dhn, mlp_norm, dh2, "mlp_norm_bwd")
    dh1_b = dh1.astype(BF16)
    dmerged = _matmul(dh1_b, w_o, "nt", [F32], "out_proj_dx")
    g_w_out = _mm_tn(merged, dh1_b, "out_proj_dw")
    dy_mla, dy_fox, dg_mla, dg_fox = _gate_bwd(dmerged, gpre, y_mla, y_fox, "gate_bwd")
    do_mla = _matmul(dy_mla, w_mb, "nt", [BF16], "mla_branch_dx")
    g_w_mb = _mm_tn(o_mla, dy_mla, "mla_branch_dw")
    do_fox = _matmul(dy_fox, w_fb, "nt", [BF16], "fox_branch_dx")
    g_w_fb = _mm_tn(o_fox, dy_fox, "fox_branch_dw")

    def pack_grads(group, lay, g_full):
        _, _, off, R = lay
        slabs = []
        for nm, w, axis in group:
            g = g_full[nm]
            if axis == 1:
                k_dim, n = g.shape[0], g.shape[1] // N_CHIPS
                g4 = jnp.transpose(g.reshape(k_dim, N_CHIPS, n), (1, 0, 2))
            else:
                g4 = g.reshape(N_CHIPS, g.shape[0] // N_CHIPS, g.shape[1])
            slabs.append(pad16(g4.reshape(N_CHIPS, -1, C).astype(BF16), 1))
        slabs.append(jnp.zeros((N_CHIPS, R - off, C), BF16))
        return jnp.concatenate(slabs, axis=1)

    gp_b = pack_grads(group_b, lay_b, {"w_mla_branch": g_w_mb, "w_fox_branch": g_w_fb, "w_out": g_w_out,
                                       "w_up": g_w_up, "w_down": g_w_down})
    rs_b = _xchg_start(gp_b, lax.empty((3, RB, C), BF16), False, g_w_fb, "grad_scatter_start_b")

    delta_mla = _att_delta_t(do_mla, o_mla, H, "mla_att_delta", order=rs_b[3])
    dq_rot, dk_nope, dkr_heads, dv_mla = _att_bwd_t(mla, do_mla, lse_mla, delta_mla, BF16, [BF16, F32],
                                                    "mla_att_bwd", dq_rope=(qc, qsa, qsb))
    delta_fox = _att_delta_t(do_fox, ox_fox, HF, "fox_att_delta")
    dfq, dfk, dfv, dcum = _att_bwd_t(fox, do_fox, lse_fox, delta_fox, BF16, [BF16], "fox_att_bwd")

    gp_b_sent, recv_b = _xchg_wait(rs_b, False, dfq, "grad_scatter_wait_b")
    swap_b = _sib_start(_sum_slabs(gp_b_sent, recv_b, chip, "grad_sum_b"), "grad_swap_start_b")

    dcqn = _matmul(dq_rot, w_uq_p, "nt", [F32], "mla_q_up_dx", order=swap_b[4])
    g_w_uq_p = _mm_tn(cqn, dq_rot, "mla_q_up_dw")
    dkv2 = jnp.concatenate([dk_nope, dv_mla], axis=1)
    dckvn = _matmul(dkv2, w_ukv_p, "nt", [F32], "mla_kv_up_dx")
    g_w_ukv_p = _mm_tn(ckvn, dkv2, "mla_kv_up_dw")

    dcum_rows = jnp.pad(dcum[:, :, 0], ((0, 8 - HF), (0, 0)))
    dlogf_rows = _suffix_sum_rows(dcum_rows, "fox_forget_suffix_sum")
    dlogf = _pad_cols(jnp.transpose(dlogf_rows[:HF]), LANE)
    d_small, g_q_norm, g_kv_norm, g_bias = _prep_bwd(
        small, dcqn, dckvn, dkr_heads, dlogf, q_norm, kv_norm, bias_pad, kc, ksa, ksb, H, "prep_bwd")
    dproj = jnp.concatenate([d_small, dfq, dfk, dfv, dg_mla, dg_fox], axis=1)
    g_w_pack = _mm_tn(xn, dproj, "proj_dw")

    gs, gq, gg = g_w_pack[:, :WS], g_w_pack[:, WS:WS + NQKV], g_w_pack[:, WS + NQKV:]
    g_w_in = jnp.concatenate([gs[:, :o_kr], gs[:, o_kr:o_kr + MLA_ROPE], gq,
                              gs[:, o_kr + LANE:o_kr + LANE + HF], gg], axis=1)
    g_w_uq = g_w_uq_p.reshape(QL, H, QPAD)[:, :, :dqk].reshape(QL, H * dqk)
    g_w_ukv = jnp.concatenate([g_w_ukv_p[:, :H * MLA_NOPE].reshape(KVL, H, MLA_NOPE),
                               g_w_ukv_p[:, H * MLA_NOPE:].reshape(KVL, H, MLA_V)], axis=2).reshape(KVL, -1)

    gp_a = pack_grads(group_a, lay_a, {"w_in": g_w_in, "w_uq": g_w_uq, "w_ukv": g_w_ukv})
    rs_a = _xchg_start(gp_a, lax.empty((3, RA, C), BF16), False, g_w_pack, "grad_scatter_start_a")
    dxn = _matmul(dproj, w_pack, "nt", [F32], "proj_dx", order=rs_a[3])
    grad_x, g_attn_norm = _norm_bwd(xs, dxn, attn_norm, dh1, "attn_norm_bwd")
    gp_a_sent, recv_a = _xchg_wait(rs_a, False, grad_x, "grad_scatter_wait_a")
    swap_a = _sib_start(_sum_slabs(gp_a_sent, recv_a, chip, "grad_sum_a"), "grad_swap_start_a")
    vec_w = max(D, LANE)
    vec_rows = [g_attn_norm, g_mlp_norm, g_final, g_q_norm, g_kv_norm, g_bias, loss_part]
    vec = jnp.concatenate([_pad_cols(v, vec_w) for v in vec_rows] + [jnp.zeros((1, vec_w), F32)], axis=0)
    vsum = _all_reduce_vec(vec, "all_reduce_vectors")
    part_b, sib_b = _sib_wait(swap_b, vsum, "grad_swap_wait_b")
    g_shards_b = _add2(part_b, sib_b, "grad_add_b")
    part_a, sib_a = _sib_wait(swap_a, g_shards_b, "grad_swap_wait_a")
    g_shards_a = _add2(part_a, sib_a, "grad_add_a")

    moments = {"attn_norm": (m_attn_norm, v_attn_norm), "w_in": (m_w_in, v_w_in), "fox_f_bias": (m_fox_f_bias, v_fox_f_bias),
               "q_norm": (m_q_norm, v_q_norm), "w_uq": (m_w_uq, v_w_uq), "kv_norm": (m_kv_norm, v_kv_norm),
               "w_ukv": (m_w_ukv, v_w_ukv), "w_mla_branch": (m_w_mla_branch, v_w_mla_branch),
               "w_fox_branch": (m_w_fox_branch, v_w_fox_branch), "w_out": (m_w_out, v_w_out),
               "mlp_norm": (m_mlp_norm, v_mlp_norm), "w_up": (m_w_up, v_w_up), "w_down": (m_w_down, v_w_down),
               "final_norm": (m_final_norm, v_final_norm)}
    weights = {"attn_norm": attn_norm, "w_in": w_in, "fox_f_bias": fox_f_bias, "q_norm": q_norm, "w_uq": w_uq,
               "kv_norm": kv_norm, "w_ukv": w_ukv, "w_mla_branch": w_mla_branch, "w_fox_branch": w_fox_branch,
               "w_out": w_out, "mlp_norm": mlp_norm, "w_up": w_up, "w_down": w_down, "final_norm": final_norm}
    grads, deltas, new_m, new_v = {}, {}, {}, {}
    where = {nm: (g_shards_a, lay_a) for nm, _, _ in group_a}
    where.update({nm: (g_shards_b, lay_b) for nm, _, _ in group_b})
    for nm, w, _ in big:
        shp = w[0].shape
        flat, (rows, offs, _, _) = where[nm]
        g = flat[offs[nm]:offs[nm] + rows[nm]].reshape(shp)
        d, nm_, nv_ = _adamw(w[0], g, moments[nm][0][0], moments[nm][1][0], "adamw_" + nm)
        grads[nm], deltas[nm], new_m[nm], new_v[nm] = g[None], d[None], nm_[None], nv_[None]
    vec_names = ["attn_norm", "mlp_norm", "final_norm", "q_norm", "kv_norm", "fox_f_bias"]

    def vec_pack(arrs):
        return jnp.concatenate([_pad_cols(a.reshape(1, -1), vec_w) for a in arrs]
                               + [jnp.zeros((2, vec_w), F32)], axis=0)

    vd, vm, vv = _adamw(vec_pack([weights[n] for n in vec_names]), vsum,
                        vec_pack([moments[n][0] for n in vec_names]), vec_pack([moments[n][1] for n in vec_names]),
                        "adamw_vectors")
    for r, nm in enumerate(vec_names):
        shp = weights[nm].shape
        n = weights[nm].size
        grads[nm] = vsum[r, :n].reshape(shp)
        deltas[nm], new_m[nm], new_v[nm] = vd[r, :n].reshape(shp), vm[r, :n].reshape(shp), vv[r, :n].reshape(shp)
    loss = vsum[6, 0]

    order = ["attn_norm", "w_in", "fox_f_bias", "q_norm", "w_uq", "kv_norm", "w_ukv", "w_mla_branch", "w_fox_branch",
             "w_out", "mlp_norm", "w_up", "w_down", "final_norm"]
    return (loss, grad_x[None], *[grads[n] for n in order], *[deltas[n] for n in order],
            *[new_m[n] for n in order], *[new_v[n] for n in order])
```

```python
import math

import jax
import jax.numpy as jnp
from jax import lax
from jax.experimental import pallas as pl
from jax.experimental.pallas import tpu as pltpu

CHUNK = 64
MLA_HEADS = 8
MLA_Q_LORA = 512
MLA_KV_LORA = 256
MLA_NOPE = 128
MLA_ROPE = 64
MLA_V = 128
ROPE_THETA = 10000.0
FOX_HEADS = 8
FOX_HEAD_DIM = 128
EPS = 1e-6

ADAM_LR = 0.001
ADAM_B1 = 0.9
ADAM_B2 = 0.999
ADAM_EPS = 1e-08
ADAM_WD = 0.01
ADAM_STEP = 10

LANE = 128
QPAD = 2 * LANE
N_CHIPS = 4
N_DEV = 8
VMEM_LIMIT = 48 * 1024 * 1024
ATT_T = 1024
QSUB = 256
ROW_T = 256
PACK_ROWS = 256
LOG2E = 1.4426950408889634

BF16 = jnp.bfloat16
F32 = jnp.float32
MESH = pl.DeviceIdType.MESH

_NT = (((1,), (1,)), ((), ()))
_TN = (((0,), (0,)), ((), ()))
_NN = (((1,), (0,)), ((), ()))


def _tile(dim, pref, align=LANE):
    if dim <= pref:
        return dim
    t = (pref // align) * align
    while t >= align:
        if dim % t == 0:
            return t
        t -= align
    return dim


def _params(sem=None):
    return pltpu.CompilerParams(dimension_semantics=sem, vmem_limit_bytes=VMEM_LIMIT)


_ANY_SPEC = pl.BlockSpec(memory_space=pl.ANY)


def _matmul(a, b, mode, out_dtypes, name, *, tm=1024, tn=512, tk=2048, extras=(), row_extras=(), epilogue=None,
            order=None):
    if mode == "nn":
        (M, K), (K2, N) = a.shape, b.shape
    elif mode == "nt":
        (M, K), (N, K2) = a.shape, b.shape
    else:
        (K, M), (K2, N) = a.shape, b.shape
    assert K == K2, (name, a.shape, b.shape)
    tm, tn, tk = _tile(M, tm), _tile(N, tn), _tile(K, tk)
    nk = K // tk
    n_out = len(out_dtypes)
    n_ex = len(extras) + len(row_extras)
    n_ord = 0 if order is None else 1
    assert all(r.shape == (M, tn) for r in row_extras), name
    dims = {"nn": _NN, "nt": _NT, "tn": _TN}[mode]

    def body(*refs):
        a_ref, b_ref = refs[0], refs[1]
        ex_refs = refs[2:2 + n_ex]
        o_refs = refs[2 + n_ex + n_ord:2 + n_ex + n_ord + n_out]
        acc_ref = refs[2 + n_ex + n_ord + n_out]
        k = pl.program_id(2)
        part = lax.dot_general(a_ref[...], b_ref[...], dims, preferred_element_type=F32)

        @pl.when(k == 0)
        def _():
            acc_ref[...] = part

        @pl.when(k > 0)
        def _():
            acc_ref[...] += part

        @pl.when(k == nk - 1)
        def _():
            acc = acc_ref[...]
            if epilogue is None:
                outs = (acc,)
            else:
                outs = epilogue(acc, *[r[...] for r in ex_refs])
            for o_ref, o in zip(o_refs, outs):
                o_ref[...] = o.astype(o_ref.dtype)

    if mode == "nn":
        a_spec = pl.BlockSpec((tm, tk), lambda i, j, k: (i, k))
        b_spec = pl.BlockSpec((tk, tn), lambda i, j, k: (k, j))
    elif mode == "nt":
        a_spec = pl.BlockSpec((tm, tk), lambda i, j, k: (i, k))
        b_spec = pl.BlockSpec((tn, tk), lambda i, j, k: (j, k))
    else:
        a_spec = pl.BlockSpec((tk, tm), lambda i, j, k: (k, i))
        b_spec = pl.BlockSpec((tk, tn), lambda i, j, k: (k, j))
    mn_spec = pl.BlockSpec((tm, tn), lambda i, j, k: (i, j))
    row_spec = pl.BlockSpec((tm, tn), lambda i, j, k: (i, 0))
    outs = pl.pallas_call(
        body,
        name=name,
        grid=(M // tm, N // tn, nk),
        in_specs=([a_spec, b_spec] + [mn_spec] * len(extras) + [row_spec] * len(row_extras) + [_ANY_SPEC] * n_ord),
        out_specs=[mn_spec] * n_out,
        out_shape=[jax.ShapeDtypeStruct((M, N), dt) for dt in out_dtypes],
        scratch_shapes=[pltpu.VMEM((tm, tn), F32)],
        compiler_params=_params(("parallel", "parallel", "arbitrary")),
    )(a, b, *extras, *row_extras, *([] if order is None else [order]))
    return outs[0] if n_out == 1 else outs


def _mm_tn(a, b, name):
    return _matmul(a, b, "tn", [F32], name, tm=1024, tn=1024, tk=2048)


def _row_spec(ts, width, col=0):
    return pl.BlockSpec((ts, width), lambda i: (i, col))


def _full_spec(shape):
    return pl.BlockSpec(shape, lambda i: tuple(0 for _ in shape))


def _rms(x):
    return lax.rsqrt(jnp.mean(x * x, axis=-1, keepdims=True) + EPS)


def _rms_bwd(x, dy, g):
    r = _rms(x)
    xh = x * r
    gy = dy * g
    dx = r * (gy - xh * jnp.mean(xh * gy, axis=-1, keepdims=True))
    return dx, dy * xh


def _norm_fwd(x, g, name, order=None):
    S, D = x.shape
    ts = _tile(S, ROW_T, 8)

    def body(x_ref, g_ref, *rest):
        o_ref = rest[-1]
        xv = x_ref[...]
        o_ref[...] = ((xv * _rms(xv)) * g_ref[...]).astype(BF16)

    extra = [] if order is None else [order]
    return pl.pallas_call(
        body, name=name, grid=(S // ts,),
        in_specs=[_row_spec(ts, D), _full_spec((1, D))] + [_ANY_SPEC] * len(extra),
        out_specs=_row_spec(ts, D),
        out_shape=jax.ShapeDtypeStruct((S, D), BF16),
        compiler_params=_params(("parallel",)),
    )(x, g, *extra)


def _norm_bwd(x, dy, g, dres, name):
    S, D = x.shape
    ts = _tile(S, ROW_T, 8)

    def body(x_ref, dy_ref, g_ref, dres_ref, dx_ref, dxb_ref, dg_ref):
        dx, dg_rows = _rms_bwd(x_ref[...], dy_ref[...], g_ref[...])
        dx = dres_ref[...] + dx
        dx_ref[...] = dx
        dxb_ref[...] = dx.astype(BF16)

        @pl.when(pl.program_id(0) == 0)
        def _():
            dg_ref[...] = jnp.zeros_like(dg_ref)

        dg_ref[...] += jnp.sum(dg_rows, axis=0, keepdims=True)

    return pl.pallas_call(
        body, name=name, grid=(S // ts,),
        in_specs=[_row_spec(ts, D), _row_spec(ts, D), _full_spec((1, D)), _row_spec(ts, D)],
        out_specs=[_row_spec(ts, D), _row_spec(ts, D), _full_spec((1, D))],
        out_shape=[jax.ShapeDtypeStruct((S, D), F32), jax.ShapeDtypeStruct((S, D), BF16),
                   jax.ShapeDtypeStruct((1, D), F32)],
        compiler_params=_params(("arbitrary",)),
    )(x, dy, g, dres)


def _rope(x, c, sa, sb, sign):
    w = x.shape[-1]
    half = MLA_ROPE // 2
    fwd = pltpu.roll(x, w - half, 1)
    back = pltpu.roll(x, half, 1)
    if sign < 0:
        return x * c - fwd * sa - back * sb
    return x * c + fwd * sa + back * sb


def _split3(x):
    hi = x.astype(BF16)
    r1 = x - hi.astype(F32)
    mid = r1.astype(BF16)
    lo = (r1 - mid.astype(F32)).astype(BF16)
    return hi, mid, lo


def _prep_fwd(small, q_norm, kv_norm, bias_pad, kc, ksa, ksb, n_heads, name):
    S, W = small.shape
    QL, KVL = q_norm.shape[1], kv_norm.shape[1]
    assert W == QL + KVL + 2 * LANE
    ts = _tile(S, ROW_T, 8)
    tri = (lax.broadcasted_iota(jnp.int32, (ts, ts), 0) >= lax.broadcasted_iota(jnp.int32, (ts, ts), 1)).astype(BF16)

    def body(s_ref, qn_ref, kvn_ref, b_ref, kc_ref, ksa_ref, ksb_ref, tri_ref,
             cqn_ref, ckvn_ref, kr_ref, cum_ref, carry_ref):
        cq = s_ref[:, 0:QL]
        cqn_ref[...] = ((cq * _rms(cq)) * qn_ref[...]).astype(BF16)
        ckv = s_ref[:, QL:QL + KVL]
        ckvn_ref[...] = ((ckv * _rms(ckv)) * kvn_ref[...]).astype(BF16)
        kr = s_ref[:, QL + KVL:QL + KVL + LANE]
        kr_ref[...] = _rope(kr, kc_ref[...], ksa_ref[...], ksb_ref[...], 1).astype(BF16)
        z = s_ref[:, QL + KVL + LANE:W] + b_ref[...]
        logf = jnp.minimum(z, 0.0) - jnp.log1p(jnp.exp(-jnp.abs(z)))
        lane = lax.broadcasted_iota(jnp.int32, logf.shape, 1)
        logf = jnp.where(lane < n_heads, logf, 0.0)

        @pl.when(pl.program_id(0) == 0)
        def _():
            carry_ref[...] = jnp.zeros_like(carry_ref)

        t = tri_ref[...]
        cum = carry_ref[...]
        for part in _split3(logf):
            cum = cum + jnp.dot(t, part, preferred_element_type=F32)
        cum_ref[...] = cum
        carry_ref[...] = cum[ts - 1:ts, :]

    return pl.pallas_call(
        body, name=name, grid=(S // ts,),
        in_specs=[_row_spec(ts, W), _full_spec((1, QL)), _full_spec((1, KVL)), _full_spec((1, LANE)),
                  _row_spec(ts, LANE), _row_spec(ts, LANE), _row_spec(ts, LANE), _full_spec((ts, ts))],
        out_specs=[_row_spec(ts, QL), _row_spec(ts, KVL), _row_spec(ts, LANE), _row_spec(ts, LANE)],
        out_shape=[jax.ShapeDtypeStruct((S, QL), BF16), jax.ShapeDtypeStruct((S, KVL), BF16),
                   jax.ShapeDtypeStruct((S, LANE), BF16), jax.ShapeDtypeStruct((S, LANE), F32)],
        scratch_shapes=[pltpu.VMEM((1, LANE), F32)],
        compiler_params=_params(("arbitrary",)),
    )(small, q_norm, kv_norm, bias_pad, kc, ksa, ksb, tri)


def _prep_bwd(small, dcqn, dckvn, dkr_heads, dlogf, q_norm, kv_norm, bias_pad, kc, ksa, ksb, n_heads, name):
    S, W = small.shape
    QL, KVL = q_norm.shape[1], kv_norm.shape[1]
    ts = _tile(S, ROW_T, 8)

    def body(s_ref, dcq_ref, dckv_ref, dkr_ref, dlf_ref, qn_ref, kvn_ref, b_ref, kc_ref, ksa_ref, ksb_ref,
             ds_ref, gq_ref, gkv_ref, gb_ref):
        dcq, gq_rows = _rms_bwd(s_ref[:, 0:QL], dcq_ref[...], qn_ref[...])
        ds_ref[:, 0:QL] = dcq.astype(BF16)
        dckv, gkv_rows = _rms_bwd(s_ref[:, QL:QL + KVL], dckv_ref[...], kvn_ref[...])
        ds_ref[:, QL:QL + KVL] = dckv.astype(BF16)
        dkr = dkr_ref[:, 0:LANE]
        for h in range(1, n_heads):
            dkr = dkr + dkr_ref[:, h * LANE:(h + 1) * LANE]
        ds_ref[:, QL + KVL:QL + KVL + LANE] = _rope(dkr, kc_ref[...], ksa_ref[...], ksb_ref[...], -1).astype(BF16)
        z = s_ref[:, QL + KVL + LANE:W] + b_ref[...]
        dff = dlf_ref[...] * (1.0 / (1.0 + jnp.exp(z)))
        ds_ref[:, QL + KVL + LANE:W] = dff.astype(BF16)

        @pl.when(pl.program_id(0) == 0)
        def _():
            gq_ref[...] = jnp.zeros_like(gq_ref)
            gkv_ref[...] = jnp.zeros_like(gkv_ref)
            gb_ref[...] = jnp.zeros_like(gb_ref)

        gq_ref[...] += jnp.sum(gq_rows, axis=0, keepdims=True)
        gkv_ref[...] += jnp.sum(gkv_rows, axis=0, keepdims=True)
        gb_ref[...] += jnp.sum(dff, axis=0, keepdims=True)

    return pl.pallas_call(
        body, name=name, grid=(S // ts,),
        in_specs=[_row_spec(ts, W), _row_spec(ts, QL), _row_spec(ts, KVL), _row_spec(ts, n_heads * LANE),
                  _row_spec(ts, LANE), _full_spec((1, QL)), _full_spec((1, KVL)), _full_spec((1, LANE)),
                  _row_spec(ts, LANE), _row_spec(ts, LANE), _row_spec(ts, LANE)],
        out_specs=[_row_spec(ts, W), _full_spec((1, QL)), _full_spec((1, KVL)), _full_spec((1, LANE))],
        out_shape=[jax.ShapeDtypeStruct((S, W), BF16), jax.ShapeDtypeStruct((1, QL), F32),
                   jax.ShapeDtypeStruct((1, KVL), F32), jax.ShapeDtypeStruct((1, LANE), F32)],
        compiler_params=_params(("arbitrary",)),
    )(small, dcqn, dckvn, dkr_heads, dlogf, q_norm, kv_norm, bias_pad, kc, ksa, ksb)


def _sigmoid(z):
    return 1.0 / (1.0 + jnp.exp(-z))


def _gate_fwd(gpre, y_mla, y_fox, name):
    S, D = y_mla.shape
    ts = _tile(S, ROW_T, 8)

    def body(ga_ref, gb_ref, ya_ref, yb_ref, o_ref):
        o_ref[...] = (_sigmoid(ga_ref[...]) * ya_ref[...] + _sigmoid(gb_ref[...]) * yb_ref[...]).astype(BF16)

    return pl.pallas_call(
        body, name=name, grid=(S // ts,),
        in_specs=[_row_spec(ts, D, 0), _row_spec(ts, D, 1), _row_spec(ts, D), _row_spec(ts, D)],
        out_specs=_row_spec(ts, D),
        out_shape=jax.ShapeDtypeStruct((S, D), BF16),
        compiler_params=_params(("parallel",)),
    )(gpre, gpre, y_mla, y_fox)


def _gate_bwd(dmerged, gpre, y_mla, y_fox, name):
    S, D = y_mla.shape
    ts = _tile(S, ROW_T, 8)

    def body(dm_ref, ga_ref, gb_ref, ya_ref, yb_ref, dya_ref, dyb_ref, dga_ref, dgb_ref):
        dm = dm_ref[...]
        ga = _sigmoid(ga_ref[...])
        gb = _sigmoid(gb_ref[...])
        dya_ref[...] = (dm * ga).astype(BF16)
        dyb_ref[...] = (dm * gb).astype(BF16)
        dga_ref[...] = (dm * ya_ref[...] * (ga * (1.0 - ga))).astype(BF16)
        dgb_ref[...] = (dm * yb_ref[...] * (gb * (1.0 - gb))).astype(BF16)

    return pl.pallas_call(
        body, name=name, grid=(S // ts,),
        in_specs=[_row_spec(ts, D), _row_spec(ts, D, 0), _row_spec(ts, D, 1), _row_spec(ts, D), _row_spec(ts, D)],
        out_specs=[_row_spec(ts, D)] * 4,
        out_shape=[jax.ShapeDtypeStruct((S, D), BF16)] * 4,
        compiler_params=_params(("parallel",)),
    )(dmerged, gpre, gpre, y_mla, y_fox)


def _final(h, g, target, name):
    S, D = h.shape
    ts = _tile(S, ROW_T, 8)

    def body(h_ref, g_ref, t_ref, dh_ref, dhb_ref, dg_ref, loss_ref):
        hv = h_ref[...]
        gv = g_ref[...]
        err = (hv * _rms(hv)) * gv - t_ref[...]
        dh, dg_rows = _rms_bwd(hv, err / D, gv)
        dh_ref[...] = dh
        dhb_ref[...] = dh.astype(BF16)

        @pl.when(pl.program_id(0) == 0)
        def _():
            dg_ref[...] = jnp.zeros_like(dg_ref)
            loss_ref[...] = jnp.zeros_like(loss_ref)

        dg_ref[...] += jnp.sum(dg_rows, axis=0, keepdims=True)
        row_loss = jnp.mean(err * err, axis=-1, keepdims=True)
        loss_ref[...] += 0.5 * jnp.sum(row_loss, axis=0, keepdims=True)

    return pl.pallas_call(
        body, name=name, grid=(S // ts,),
        in_specs=[_row_spec(ts, D), _full_spec((1, D)), _row_spec(ts, D)],
        out_specs=[_row_spec(ts, D), _row_spec(ts, D), _full_spec((1, D)), _full_spec((1, LANE))],
        out_shape=[jax.ShapeDtypeStruct((S, D), F32), jax.ShapeDtypeStruct((S, D), BF16),
                   jax.ShapeDtypeStruct((1, D), F32), jax.ShapeDtypeStruct((1, LANE), F32)],
        compiler_params=_params(("arbitrary",)),
    )(h, g, target)


def _suffix_sum_rows(x, name):
    R, S = x.shape
    tb = _tile(S, 512)
    nb = S // tb
    tri = (lax.broadcasted_iota(jnp.int32, (tb, tb), 0) >= lax.broadcasted_iota(jnp.int32, (tb, tb), 1)).astype(BF16)

    def body(x_ref, tri_ref, o_ref, carry_ref):
        @pl.when(pl.program_id(0) == 0)
        def _():
            carry_ref[...] = jnp.zeros_like(carry_ref)

        xv = x_ref[...]
        t = tri_ref[...]
        acc = jnp.broadcast_to(carry_ref[:, 0:1], xv.shape)
        for part in _split3(xv):
            acc = acc + jnp.dot(part, t, preferred_element_type=F32)
        o_ref[...] = acc
        carry_ref[...] = jnp.broadcast_to(acc[:, 0:1], carry_ref.shape)

    rev = pl.BlockSpec((R, tb), lambda i: (0, nb - 1 - i))
    return pl.pallas_call(
        body, name=name, grid=(nb,),
        in_specs=[rev, _full_spec((tb, tb))], out_specs=rev,
        out_shape=jax.ShapeDtypeStruct((R, S), F32),
        scratch_shapes=[pltpu.VMEM((R, LANE), F32)],
        compiler_params=_params(("arbitrary",)),
    )(x, tri)


def _pairs(nb, by_key):
    if by_key:
        pr = [(i, j) for j in range(nb) for i in range(j, nb)]
    else:
        pr = [(i, j) for i in range(nb) for j in range(i + 1)]
    return (jnp.asarray([p[0] for p in pr], jnp.int32), jnp.asarray([p[1] for p in pr], jnp.int32), len(pr))


class _AttT:
    def __init__(self, S, n_heads, q, ks, v, scale, chunk_causal, cum_rep=None):
        self.S, self.H, self.q, self.ks, self.v = S, n_heads, q, ks, v
        self.scale, self.chunk_causal, self.cum_rep = scale, chunk_causal, cum_rep
        self.T = _tile(S, ATT_T)
        self.qs = min(QSUB, self.T)
        self.nb = S // self.T
        self.dq, self.dv = q[1], v[1]
        self.has_bias = cum_rep is not None

    def q_spec(self, op):
        _, w, off, per_head = op
        return pl.BlockSpec((self.T, w), lambda h, p, it, jt: (it[p], off + (h if per_head else 0)))

    def k_spec(self, op):
        _, w, off, per_head = op
        return pl.BlockSpec((self.T, w), lambda h, p, it, jt: (jt[p], off + (h if per_head else 0)))

    def row_q(self):
        return pl.BlockSpec((None, 1, self.T), lambda h, p, it, jt: (h, 0, it[p]))

    def cum_k(self):
        return pl.BlockSpec((None, self.T, self.qs), lambda h, p, it, jt: (h, jt[p], 0))

    def sub_blocks(self, masked):
        return [(q0, min(self.T, q0 + self.qs) if masked else self.T) for q0 in range(0, self.T, self.qs)]

    def scores(self, k, q_sub, cum, q0, masked):
        s = lax.dot_general(k, q_sub, _NT, preferred_element_type=F32) * (self.scale * LOG2E)
        if self.has_bias:
            s = s - cum
        mask = None
        if masked:
            r = lax.broadcasted_iota(jnp.int32, s.shape, 0)
            c = lax.broadcasted_iota(jnp.int32, s.shape, 1) + q0
            mask = (r // CHUNK <= c // CHUNK) if self.chunk_causal else (r <= c)
        return s, mask


def _join(k_refs):
    return k_refs[0][...] if len(k_refs) == 1 else jnp.concatenate([r[...] for r in k_refs], axis=-1)


def _att_fwd_t(att, name, exact=False):
    S, H, T, qs = att.S, att.H, att.T, att.qs
    it, jt, npairs = _pairs(att.nb, by_key=False)
    nk = len(att.ks)

    def body(it_ref, jt_ref, *refs):
        q_ref = refs[0]
        k_refs = refs[1:1 + nk]
        v_ref = refs[1 + nk]
        n = 2 + nk
        cum_ref = None
        if att.has_bias:
            cum_ref = refs[n]
            n += 1
        o_ref = refs[n]
        n += 1
        ox_ref = None
        if exact:
            ox_ref = refs[n]
            n += 1
        lse_ref, m_ref, l_ref, acc_ref = refs[n:n + 4]
        lo_ref = refs[n + 4] if exact else None
        p = pl.program_id(1)
        i, j = it_ref[p], jt_ref[p]

        @pl.when(j == 0)
        def _():
            m_ref[...] = jnp.full_like(m_ref, -jnp.inf)
            l_ref[...] = jnp.zeros_like(l_ref)
            acc_ref[...] = jnp.zeros_like(acc_ref)
            if exact:
                lo_ref[...] = jnp.zeros_like(lo_ref)

        def step(masked):
            k = _join(k_refs)
            v = v_ref[...]
            subs = att.sub_blocks(masked)

            def logits(idx):
                q0, nkeys = subs[idx]
                cum = cum_ref[0:nkeys, :] if att.has_bias else None
                return att.scores(k[0:nkeys], q_ref[q0:q0 + qs, :], cum, q0, masked)

            ahead = logits(0)
            for idx, (q0, nkeys) in enumerate(subs):
                qsl = slice(q0, q0 + qs)
                s, mask = ahead
                if idx + 1 < len(subs):
                    ahead = logits(idx + 1)
                if masked:
                    s = jnp.where(mask, s, -jnp.inf)
                m_prev = m_ref[:, qsl]
                m_new = jnp.maximum(m_prev, jnp.max(s, axis=0, keepdims=True))
                alpha = jnp.exp2(m_prev - m_new)
                pr = jnp.exp2(s - m_new)
                l_ref[:, qsl] = alpha * l_ref[:, qsl] + jnp.sum(pr, axis=0, keepdims=True)
                p_hi = pr.astype(BF16)
                acc_ref[:, qsl] = alpha * acc_ref[:, qsl] + lax.dot_general(
                    v[0:nkeys], p_hi, _TN, preferred_element_type=F32)
                if exact:
                    p_lo = (pr - p_hi.astype(F32)).astype(BF16)
                    lo_ref[:, qsl] = alpha * lo_ref[:, qsl] + lax.dot_general(
                        v[0:nkeys], p_lo, _TN, preferred_element_type=F32)
                m_ref[:, qsl] = m_new

        @pl.when(j < i)
        def _():
            step(False)

        @pl.when(j == i)
        def _():
            step(True)
            l = l_ref[...]
            inv = 1.0 / l
            o_ref[...] = jnp.transpose(acc_ref[...] * inv).astype(o_ref.dtype)
            if exact:
                ox_ref[...] = jnp.transpose((acc_ref[...] + lo_ref[...]) * inv)
            lse_ref[...] = m_ref[...] + jnp.log2(l)

    in_specs = [att.q_spec(att.q)] + [att.k_spec(k) for k in att.ks] + [att.k_spec(att.v)]
    args = [att.q[0]] + [k[0] for k in att.ks] + [att.v[0]]
    if att.has_bias:
        in_specs.append(att.cum_k())
        args.append(att.cum_rep)
    o_spec = pl.BlockSpec((T, att.dv), lambda h, p, it, jt: (it[p], h))
    out_specs = [o_spec]
    out_shape = [jax.ShapeDtypeStruct((S, H * att.dv), BF16)]
    scratch = [pltpu.VMEM((1, T), F32), pltpu.VMEM((1, T), F32), pltpu.VMEM((att.dv, T), F32)]
    if exact:
        out_specs.append(o_spec)
        out_shape.append(jax.ShapeDtypeStruct((S, H * att.dv), F32))
        scratch.append(pltpu.VMEM((att.dv, T), F32))
    out_specs.append(att.row_q())
    out_shape.append(jax.ShapeDtypeStruct((H, 1, S), F32))
    return pl.pallas_call(
        body, name=name,
        grid_spec=pltpu.PrefetchScalarGridSpec(
            num_scalar_prefetch=2, grid=(H, npairs), in_specs=in_specs, out_specs=out_specs,
            scratch_shapes=scratch),
        out_shape=out_shape,
        compiler_params=_params(("parallel", "arbitrary")),
    )(it, jt, *args)


def _att_delta_t(do, o, n_heads, name, order=None):
    S = do.shape[0]
    w = do.shape[1] // n_heads
    ts = _tile(S, ATT_T)
    ones = jnp.ones((8, w), BF16)
    extra = [] if order is None else [order]

    def body(do_ref, o_ref, ones_ref, *rest):
        d_ref = rest[-1]
        prod = do_ref[...].astype(F32) * o_ref[...].astype(F32)
        acc = jnp.zeros((8, ts), F32)
        for part in _split3(prod):
            acc = acc + lax.dot_general(ones_ref[...], part, _NT, preferred_element_type=F32)
        d_ref[...] = acc[0:1, :]

    blk = pl.BlockSpec((ts, w), lambda i, h: (i, h))
    return pl.pallas_call(
        body, name=name, grid=(S // ts, n_heads),
        in_specs=[blk, blk, pl.BlockSpec((8, w), lambda i, h: (0, 0))] + [_ANY_SPEC] * len(extra),
        out_specs=pl.BlockSpec((None, 1, ts), lambda i, h: (h, 0, i)),
        out_shape=jax.ShapeDtypeStruct((n_heads, 1, S), F32),
        compiler_params=_params(("parallel", "parallel")),
    )(do, o, ones, *extra)


def _att_bwd_t(att, do, lse, delta, dq_dtype, dk_dtypes, name, dq_rope=None):
    S, H, T, qs = att.S, att.H, att.T, att.qs
    it, jt, npairs = _pairs(att.nb, by_key=True)
    nk = len(att.ks)
    last = att.nb - 1
    widths = [k[1] for k in att.ks]

    def body(it_ref, jt_ref, *refs):
        q_ref = refs[0]
        k_refs = refs[1:1 + nk]
        v_ref, do_ref, lse_ref, dl_ref = refs[1 + nk:5 + nk]
        n = 5 + nk
        cum_ref = None
        if att.has_bias:
            cum_ref = refs[n]
            n += 1
        rope_refs = None
        if dq_rope is not None:
            rope_refs = refs[n:n + 3]
            n += 3
        dq_ref = refs[n]
        dk_refs = refs[n + 1:n + 1 + nk]
        dv_ref = refs[n + 1 + nk]
        n += nk + 2
        dc_ref = None
        if att.has_bias:
            dc_ref = refs[n]
            n += 1
        dq_acc, dk_acc, dv_acc = refs[n:n + 3]
        dc_acc = refs[n + 3] if att.has_bias else None
        p = pl.program_id(1)
        i, j = it_ref[p], jt_ref[p]

        @pl.when(p == 0)
        def _():
            dq_acc[...] = jnp.zeros_like(dq_acc)

        @pl.when(i == j)
        def _():
            dk_acc[...] = jnp.zeros_like(dk_acc)
            dv_acc[...] = jnp.zeros_like(dv_acc)
            if att.has_bias:
                dc_acc[...] = jnp.zeros_like(dc_acc)

        def step(masked):
            k = _join(k_refs)
            v = v_ref[...]
            subs = att.sub_blocks(masked)

            def logits(idx):
                q0, nkeys = subs[idx]
                cum = cum_ref[0:nkeys, :] if att.has_bias else None
                return att.scores(k[0:nkeys], q_ref[q0:q0 + qs, :], cum, q0, masked)

            ahead = logits(0)
            for idx, (q0, nkeys) in enumerate(subs):
                qsl = slice(q0, q0 + qs)
                ksl = slice(0, nkeys)
                q_sub = q_ref[qsl, :]
                do_sub = do_ref[qsl, :]
                s, mask = ahead
                if idx + 1 < len(subs):
                    ahead = logits(idx + 1)
                pr = jnp.exp2(s - lse_ref[:, qsl])
                if masked:
                    pr = jnp.where(mask, pr, 0.0)
                dp = lax.dot_general(v[ksl], do_sub, _NT, preferred_element_type=F32)
                ds = pr * (dp - dl_ref[:, qsl])
                ds_b = ds.astype(BF16)
                dv_acc[ksl, :] += jnp.dot(pr.astype(BF16), do_sub, preferred_element_type=F32)
                dk_acc[ksl, :] += jnp.dot(ds_b, q_sub, preferred_element_type=F32)
                dq_acc[i, :, qsl] += lax.dot_general(k[ksl], ds_b, _TN, preferred_element_type=F32)
                if att.has_bias:
                    part = ds[:, 0:LANE] if qs >= LANE else ds
                    for c0 in range(LANE, qs, LANE):
                        part = part + ds[:, c0:c0 + LANE]
                    dc_acc[ksl, :] += part

        @pl.when(i > j)
        def _():
            step(False)

        @pl.when(i == j)
        def _():
            step(True)
            dq = jnp.transpose(dq_acc[i] * att.scale)
            if dq_rope is not None:
                dq = _rope(dq, rope_refs[0][...], rope_refs[1][...], rope_refs[2][...], -1)
            dq_ref[...] = dq.astype(dq_ref.dtype)

        @pl.when(i == last)
        def _():
            dk = dk_acc[...] * att.scale
            off = 0
            for r, w in zip(dk_refs, widths):
                r[...] = dk[:, off:off + w].astype(r.dtype)
                off += w
            dv_ref[...] = dv_acc[...].astype(dv_ref.dtype)
            if att.has_bias:
                dc_ref[...] = -jnp.sum(dc_acc[...], axis=-1, keepdims=True)

    do_op = (do, att.dv, 0, True)
    in_specs = ([att.q_spec(att.q)] + [att.k_spec(k) for k in att.ks]
                + [att.k_spec(att.v), att.q_spec(do_op), att.row_q(), att.row_q()])
    args = [att.q[0]] + [k[0] for k in att.ks] + [att.v[0], do, lse, delta]
    if att.has_bias:
        in_specs.append(att.cum_k())
        args.append(att.cum_rep)
    if dq_rope is not None:
        in_specs += [pl.BlockSpec((T, att.dq), lambda h, p, it, jt: (jt[p], 0))] * 3
        args += list(dq_rope)
    out_specs = [pl.BlockSpec((T, att.dq), lambda h, p, it, jt: (jt[p], h))]
    out_shape = [jax.ShapeDtypeStruct((S, H * att.dq), dq_dtype)]
    out_specs += [pl.BlockSpec((T, w), lambda h, p, it, jt: (jt[p], h)) for w in widths]
    out_shape += [jax.ShapeDtypeStruct((S, H * w), dt) for w, dt in zip(widths, dk_dtypes)]
    out_specs.append(pl.BlockSpec((T, att.dv), lambda h, p, it, jt: (jt[p], h)))
    out_shape.append(jax.ShapeDtypeStruct((S, H * att.dv), BF16))
    scratch = [pltpu.VMEM((att.nb, att.dq, T), F32), pltpu.VMEM((T, att.dq), F32), pltpu.VMEM((T, att.dv), F32)]
    if att.has_bias:
        out_specs.append(pl.BlockSpec((None, T, 1), lambda h, p, it, jt: (h, jt[p], 0)))
        out_shape.append(jax.ShapeDtypeStruct((H, S, 1), F32))
        scratch.append(pltpu.VMEM((T, min(qs, LANE)), F32))
    return pl.pallas_call(
        body, name=name,
        grid_spec=pltpu.PrefetchScalarGridSpec(
            num_scalar_prefetch=2, grid=(H, npairs), in_specs=in_specs, out_specs=out_specs,
            scratch_shapes=scratch),
        out_shape=out_shape,
        compiler_params=_params(("parallel", "arbitrary")),
    )(it, jt, *args)


def _adamw(w, g1, g2, m, v, name):
    _, K, N = w.shape
    assert g1.shape == (K, N) and g2.shape == (K, N), name
    tr = _tile(K, 256, 8)
    c1 = 1.0 - ADAM_B1 ** ADAM_STEP
    c2 = 1.0 - ADAM_B2 ** ADAM_STEP

    def body(w_ref, g1_ref, g2_ref, m_ref, v_ref, g_ref, d_ref, nm_ref, nv_ref):
        gv = g1_ref[...] + g2_ref[...]
        nm = ADAM_B1 * m_ref[...] + (1.0 - ADAM_B1) * gv
        nv = ADAM_B2 * v_ref[...] + (1.0 - ADAM_B2) * (gv * gv)
        g_ref[...] = gv
        d_ref[...] = -ADAM_LR * ((nm / c1) / (jnp.sqrt(nv / c2) + ADAM_EPS) + ADAM_WD * w_ref[...])
        nm_ref[...] = nm
        nv_ref[...] = nv

    blk = pl.BlockSpec((None, tr, N), lambda i: (0, i, 0))
    gblk = pl.BlockSpec((tr, N), lambda i: (i, 0))
    return pl.pallas_call(
        body, name=name, grid=(K // tr,),
        in_specs=[blk, gblk, gblk, blk, blk], out_specs=[blk] * 4,
        out_shape=[jax.ShapeDtypeStruct((1, K, N), F32)] * 4,
        compiler_params=_params(("parallel",)),
    )(w, g1, g2, m, v)


_HBM_SPEC = pl.BlockSpec(memory_space=pltpu.HBM)
_SEM_SPEC = pl.BlockSpec(memory_space=pltpu.SEMAPHORE)
_VMEM_SPEC = pl.BlockSpec(memory_space=pltpu.VMEM)
_EFFECT = pltpu.SideEffectType.DATAFLOW_SIDE_EFFECTING


def _place():
    return lax.axis_index("x"), lax.axis_index("y"), lax.axis_index("c")


def _other_chips(x, y):
    return [(1 - x, y), (x, 1 - y), (1 - x, 1 - y)]


def _all_gather_halves(wp, name):
    R, C = wp.shape
    half = R // 2
    assert half % 16 == 0

    def body(w_ref, out_ref, ici_send, ici_recv, d2d_send, d2d_recv, local_sem):
        x, y, c = _place()
        me = 2 * x + y
        chips = _other_chips(x, y)
        mine = pl.ds(pl.multiple_of(c * half, 16), half)
        theirs = pl.ds(pl.multiple_of((1 - c) * half, 16), half)
        local = pltpu.make_async_copy(w_ref, out_ref.at[me], local_sem)
        local.start()
        sends = []
        for n, (px, py) in enumerate(chips):
            cp = pltpu.make_async_remote_copy(
                src_ref=w_ref.at[mine], dst_ref=out_ref.at[me, mine], send_sem=ici_send.at[n],
                recv_sem=ici_recv.at[n], device_id=(px, py, c), device_id_type=MESH)
            cp.start()
            sends.append(cp)
        for n, (px, py) in enumerate(chips):
            slot = 2 * px + py
            pltpu.make_async_remote_copy(
                src_ref=w_ref.at[mine], dst_ref=out_ref.at[slot, mine], send_sem=ici_send.at[n],
                recv_sem=ici_recv.at[n], device_id=(px, py, c), device_id_type=MESH).wait_recv()
            cp = pltpu.make_async_remote_copy(
                src_ref=out_ref.at[slot, mine], dst_ref=out_ref.at[slot, mine], send_sem=d2d_send.at[n],
                recv_sem=d2d_recv.at[n], device_id=(x, y, 1 - c), device_id_type=MESH)
            cp.start()
            sends.append(cp)
        for n, (px, py) in enumerate(chips):
            slot = 2 * px + py
            pltpu.make_async_remote_copy(
                src_ref=out_ref.at[slot, theirs], dst_ref=out_ref.at[slot, theirs], send_sem=d2d_send.at[n],
                recv_sem=d2d_recv.at[n], device_id=(x, y, 1 - c), device_id_type=MESH).wait_recv()
        for cp in sends:
            cp.wait_send()
        local.wait()

    return pl.pallas_call(
        body, name=name,
        in_specs=[_ANY_SPEC], out_specs=_ANY_SPEC,
        out_shape=jax.ShapeDtypeStruct((N_CHIPS, R, C), wp.dtype),
        scratch_shapes=[pltpu.SemaphoreType.DMA((3,)), pltpu.SemaphoreType.DMA((3,)), pltpu.SemaphoreType.DMA((3,)),
                        pltpu.SemaphoreType.DMA((3,)), pltpu.SemaphoreType.DMA],
    )(wp)


def _chip_copies(src_ref, land_ref, sems, gather):
    x, y, c = _place()
    me = 2 * x + y
    out, back = [], []
    for n, (px, py) in enumerate(_other_chips(x, y)):
        src = src_ref if gather else src_ref.at[2 * px + py]
        out.append(pltpu.make_async_remote_copy(
            src_ref=src, dst_ref=land_ref.at[me] if gather else land_ref.at[n],
            send_sem=sems[n], recv_sem=sems[3 + n], device_id=(px, py, c), device_id_type=MESH))
        back.append(pltpu.make_async_remote_copy(
            src_ref=src, dst_ref=land_ref.at[2 * px + py] if gather else land_ref.at[n],
            send_sem=sems[n], recv_sem=sems[3 + n], device_id=(px, py, c), device_id_type=MESH))
    return out, back


def _xchg_start(src, land, gather, order, name):
    def body(src_ref, land_ref, order_ref, *outs):
        sems = outs[0:6]
        token = outs[8]
        out, _ = _chip_copies(src_ref, land_ref, sems, gather)
        for cp in out:
            cp.start()
        token[...] = jnp.zeros_like(token)

    outs = pl.pallas_call(
        body, name=name,
        out_shape=(pltpu.SemaphoreType.DMA(()),) * 6 + (
            pltpu.HBM(src.shape, src.dtype), pltpu.HBM(land.shape, land.dtype),
            jax.ShapeDtypeStruct((8, LANE), F32)),
        in_specs=(_HBM_SPEC, _HBM_SPEC, _ANY_SPEC),
        out_specs=(_SEM_SPEC,) * 6 + (_HBM_SPEC, _HBM_SPEC, _VMEM_SPEC),
        input_output_aliases={0: 6, 1: 7},
        compiler_params=pltpu.CompilerParams(has_side_effects=_EFFECT),
    )(pltpu.with_memory_space_constraint(src, pltpu.HBM), pltpu.with_memory_space_constraint(land, pltpu.HBM), order)
    return outs[0:6], outs[6], outs[7], outs[8]


def _xchg_wait(started, gather, after, name):
    sems, src, land, _ = started

    def body(src_ref, land_ref, *rest):
        _, back = _chip_copies(src_ref, land_ref, rest[0:6], gather)
        for cp in back:
            cp.wait_send()
            cp.wait_recv()

    return pl.pallas_call(
        body, name=name,
        out_shape=(pltpu.HBM(src.shape, src.dtype), pltpu.HBM(land.shape, land.dtype)),
        in_specs=(_HBM_SPEC, _HBM_SPEC) + (_SEM_SPEC,) * 6 + (_ANY_SPEC,),
        out_specs=(_HBM_SPEC, _HBM_SPEC),
        input_output_aliases={0: 0, 1: 1},
        compiler_params=pltpu.CompilerParams(has_side_effects=_EFFECT),
    )(src, land, *sems, after)


def _sib_copy(src_ref, land_ref, send_sem, recv_sem):
    x, y, c = _place()
    return pltpu.make_async_remote_copy(src_ref=src_ref, dst_ref=land_ref, send_sem=send_sem, recv_sem=recv_sem,
                                        device_id=(x, y, 1 - c), device_id_type=MESH)


def _sib_start(src, name):
    land = lax.empty(src.shape, src.dtype)

    def body(src_ref, land_ref, send_sem, recv_sem, src_thru, land_thru, token):
        _sib_copy(src_ref, land_ref, send_sem, recv_sem).start()
        token[...] = jnp.zeros_like(token)

    return pl.pallas_call(
        body, name=name,
        out_shape=(pltpu.SemaphoreType.DMA(()), pltpu.SemaphoreType.DMA(()),
                   pltpu.HBM(src.shape, src.dtype), pltpu.HBM(land.shape, land.dtype),
                   jax.ShapeDtypeStruct((8, LANE), F32)),
        in_specs=(_HBM_SPEC, _HBM_SPEC),
        out_specs=(_SEM_SPEC, _SEM_SPEC, _HBM_SPEC, _HBM_SPEC, _VMEM_SPEC),
        input_output_aliases={0: 2, 1: 3},
        compiler_params=pltpu.CompilerParams(has_side_effects=_EFFECT),
    )(pltpu.with_memory_space_constraint(src, pltpu.HBM), pltpu.with_memory_space_constraint(land, pltpu.HBM))


def _sib_wait(started, after, name):
    send_sem, recv_sem, src, land, _ = started

    def body(src_ref, land_ref, send_sem, recv_sem, after_ref, src_out, land_out):
        cp = _sib_copy(src_ref, land_ref, send_sem, recv_sem)
        cp.wait_send()
        cp.wait_recv()

    return pl.pallas_call(
        body, name=name,
        out_shape=(pltpu.HBM(src.shape, src.dtype), pltpu.HBM(land.shape, land.dtype)),
        in_specs=(_HBM_SPEC, _HBM_SPEC, _SEM_SPEC, _SEM_SPEC, _ANY_SPEC),
        out_specs=(_HBM_SPEC, _HBM_SPEC),
        input_output_aliases={0: 0, 1: 1},
        compiler_params=pltpu.CompilerParams(has_side_effects=_EFFECT),
    )(src, land, send_sem, recv_sem, after)


def _sum_slabs(gp, recv, chip, name):
    _, R, C = gp.shape
    tr = _tile(R, PACK_ROWS, 16)

    def body(chip_ref, own_ref, r0_ref, r1_ref, r2_ref, o_ref):
        acc = own_ref[...].astype(F32) + r0_ref[...].astype(F32)
        o_ref[...] = (acc + r1_ref[...].astype(F32)) + r2_ref[...].astype(F32)

    def got(n):
        return pl.BlockSpec((None, tr, C), lambda i, chip_ref: (n, i, 0))

    return pl.pallas_call(
        body, name=name,
        grid_spec=pltpu.PrefetchScalarGridSpec(
            num_scalar_prefetch=1, grid=(R // tr,),
            in_specs=[pl.BlockSpec((None, tr, C), lambda i, chip_ref: (chip_ref[0], i, 0)), got(0), got(1), got(2)],
            out_specs=pl.BlockSpec((tr, C), lambda i, chip_ref: (i, 0))),
        out_shape=jax.ShapeDtypeStruct((R, C), F32),
        compiler_params=_params(("parallel",)),
    )(jnp.reshape(chip, (1,)).astype(jnp.int32), gp, recv, recv, recv)


def _all_reduce_vec(vec, name):
    VR, W = vec.shape

    def body(vec_ref, vall_ref, vout_ref, vsend_sems, vrecv_sems):
        x, y, c = _place()
        vall_ref[4 * x + 2 * y + c] = vec_ref[...]
        sends = []
        peers = []
        for r in range(1, N_DEV):
            dx, dy, dc = (r >> 2) & 1, (r >> 1) & 1, r & 1
            peer = (x ^ dx, y ^ dy, c ^ dc)
            peers.append(peer)
            cp = pltpu.make_async_remote_copy(
                src_ref=vec_ref, dst_ref=vall_ref.at[4 * x + 2 * y + c], send_sem=vsend_sems.at[r - 1],
                recv_sem=vrecv_sems.at[r - 1], device_id=peer, device_id_type=MESH)
            cp.start()
            sends.append(cp)
        for r, peer in enumerate(peers):
            pltpu.make_async_remote_copy(
                src_ref=vec_ref, dst_ref=vall_ref.at[4 * peer[0] + 2 * peer[1] + peer[2]],
                send_sem=vsend_sems.at[r], recv_sem=vrecv_sems.at[r],
                device_id=peer, device_id_type=MESH).wait_recv()
        total = vall_ref[0]
        for d in range(1, N_DEV):
            total = total + vall_ref[d]
        vout_ref[...] = total
        for cp in sends:
            cp.wait_send()

    outs = pl.pallas_call(
        body, name=name,
        in_specs=[_VMEM_SPEC], out_specs=[_VMEM_SPEC, _VMEM_SPEC],
        out_shape=[jax.ShapeDtypeStruct((N_DEV, VR, W), F32), jax.ShapeDtypeStruct((VR, W), F32)],
        scratch_shapes=[pltpu.SemaphoreType.DMA((N_DEV - 1,)), pltpu.SemaphoreType.DMA((N_DEV - 1,))],
    )(vec)
    return outs[1]


class _Pack:
    def __init__(self, group, C):
        self.group, self.C = group, C
        self.rows, self.offs, off = {}, {}, 0
        for nm, (K, N), _ in group:
            assert N <= C, nm
            self.rows[nm] = K if 2 * N > C else -(-(K * N) // C)
            self.offs[nm] = off
            off += -(-self.rows[nm] // 16) * 16
        self.used = off
        self.R = -(-off // PACK_ROWS) * PACK_ROWS

    def _rows_of(self, a):
        K, N = a.shape
        if 2 * N > self.C:
            a = jnp.pad(a, ((0, 0), (0, self.C - N)))
        else:
            a = jnp.pad(a.reshape(-1), (0, -(K * N) % self.C)).reshape(-1, self.C)
        return jnp.pad(a, ((0, -a.shape[0] % 16), (0, 0)))

    def pack(self, shards):
        parts = [self._rows_of(shards[nm].astype(BF16)) for nm, _, _ in self.group]
        return jnp.concatenate(parts + [jnp.zeros((self.R - self.used, self.C), BF16)], axis=0)

    def part(self, flat, nm, shape):
        K, N = shape
        rows = flat[self.offs[nm]:self.offs[nm] + self.rows[nm]]
        return rows[:, :N] if 2 * N > self.C else rows.reshape(-1)[:K * N].reshape(K, N)

    def slabs(self, grads):
        out = []
        for k in range(N_CHIPS):
            cut = {}
            for nm, (K, N), axis in self.group:
                g = grads[nm]
                cut[nm] = g[:, k * N:(k + 1) * N] if axis == 1 else g[k * K:(k + 1) * K, :]
            out.append(self.pack(cut))
        return jnp.stack(out)

    def full(self, gathered):
        res = {}
        for nm, (K, N), axis in self.group:
            parts = [self.part(gathered[k], nm, (K, N)) for k in range(N_CHIPS)]
            res[nm] = jnp.concatenate(parts, axis=axis)
        return res


def _rope_tables(S):
    pos = jnp.arange(S, dtype=F32)
    inv = 1.0 / (ROPE_THETA ** (jnp.arange(0, MLA_ROPE, 2, dtype=F32) / MLA_ROPE))
    ang = pos[:, None] * inv[None, :]
    cos, sin = jnp.cos(ang), jnp.sin(ang)
    half = MLA_ROPE // 2
    z = jnp.zeros((S, half), F32)
    one = jnp.ones((S, LANE - MLA_ROPE), F32)
    zero = jnp.zeros((S, LANE - MLA_ROPE), F32)
    kc = jnp.concatenate([cos, cos, one], axis=1)
    ksa = jnp.concatenate([-sin, z, zero], axis=1)
    ksb = jnp.concatenate([z, sin, zero], axis=1)
    qc = jnp.concatenate([jnp.ones((S, MLA_NOPE), F32), kc], axis=1)
    qsa = jnp.concatenate([jnp.zeros((S, MLA_NOPE), F32), ksa], axis=1)
    qsb = jnp.concatenate([jnp.zeros((S, MLA_NOPE), F32), ksb], axis=1)
    return (kc, ksa, ksb), (qc, qsa, qsb)


def _pad_cols(a, width):
    return jnp.pad(a, ((0, 0), (0, width - a.shape[1])))


def kernel(x, attn_norm, w_in, fox_f_bias, q_norm, w_uq, kv_norm, w_ukv, w_mla_branch, w_fox_branch, w_out, mlp_norm, w_up, w_down, final_norm, loss_target, m_attn_norm, m_w_in, m_fox_f_bias, m_q_norm, m_w_uq, m_kv_norm, m_w_ukv, m_w_mla_branch, m_w_fox_branch, m_w_out, m_mlp_norm, m_w_up, m_w_down, m_final_norm, v_attn_norm, v_w_in, v_fox_f_bias, v_q_norm, v_w_uq, v_kv_norm, v_w_ukv, v_w_mla_branch, v_w_fox_branch, v_w_out, v_mlp_norm, v_w_up, v_w_down, v_final_norm):
    _, S, D = x.shape
    H, HF = MLA_HEADS, FOX_HEADS
    QL, KVL = MLA_Q_LORA, MLA_KV_LORA
    assert H == HF and H <= 8
    xs = x[0]
    target = loss_target[0]
    C = max(D, -(-w_in.shape[2] // LANE) * LANE)
    chip = 2 * lax.axis_index("x") + lax.axis_index("y")

    weights = {"attn_norm": attn_norm, "w_in": w_in, "fox_f_bias": fox_f_bias, "q_norm": q_norm, "w_uq": w_uq,
               "kv_norm": kv_norm, "w_ukv": w_ukv, "w_mla_branch": w_mla_branch, "w_fox_branch": w_fox_branch,
               "w_out": w_out, "mlp_norm": mlp_norm, "w_up": w_up, "w_down": w_down, "final_norm": final_norm}
    moments = {"attn_norm": (m_attn_norm, v_attn_norm), "w_in": (m_w_in, v_w_in), "fox_f_bias": (m_fox_f_bias, v_fox_f_bias),
               "q_norm": (m_q_norm, v_q_norm), "w_uq": (m_w_uq, v_w_uq), "kv_norm": (m_kv_norm, v_kv_norm),
               "w_ukv": (m_w_ukv, v_w_ukv), "w_mla_branch": (m_w_mla_branch, v_w_mla_branch),
               "w_fox_branch": (m_w_fox_branch, v_w_fox_branch), "w_out": (m_w_out, v_w_out),
               "mlp_norm": (m_mlp_norm, v_mlp_norm), "w_up": (m_w_up, v_w_up), "w_down": (m_w_down, v_w_down),
               "final_norm": (m_final_norm, v_final_norm)}

    def group(names_axes):
        return [(nm, weights[nm].shape[1:], axis) for nm, axis in names_axes]

    pack_a = _Pack(group([("w_in", 1), ("w_uq", 1), ("w_ukv", 1)]), C)
    pack_b = _Pack(group([("w_mla_branch", 1), ("w_fox_branch", 1), ("w_out", 0), ("w_up", 1), ("w_down", 0)]), C)
    RA, RB = pack_a.R, pack_b.R
    wp_a = pack_a.pack({nm: weights[nm][0] for nm, _, _ in pack_a.group})
    wp_b = pack_b.pack({nm: weights[nm][0] for nm, _, _ in pack_b.group})
    gathered_a = _all_gather_halves(wp_a, "all_gather_a")
    ag_b = _xchg_start(wp_b, lax.empty((N_CHIPS, RB, C), BF16), True, gathered_a, "all_gather_start_b")
    xn = _norm_fwd(xs, attn_norm, "attn_norm_fwd", order=ag_b[3])
    full = pack_a.full(gathered_a)

    o_ckv = QL
    o_kr = o_ckv + KVL
    o_fq = o_kr + MLA_ROPE
    o_ff = o_fq + 3 * HF * FOX_HEAD_DIM
    o_g = o_ff + HF
    wi = full["w_in"]
    assert wi.shape[1] == o_g + 2 * D
    WS = QL + KVL + 2 * LANE
    NQKV = 3 * HF * FOX_HEAD_DIM
    w_small = jnp.concatenate([wi[:, :o_kr], _pad_cols(wi[:, o_kr:o_fq], LANE), _pad_cols(wi[:, o_ff:o_g], LANE)], axis=1)
    w_qkv = wi[:, o_fq:o_ff]
    w_g = wi[:, o_g:]
    w_pack = jnp.concatenate([w_small, w_qkv, w_g], axis=1)
    dqk = MLA_NOPE + MLA_ROPE
    w_uq_p = jnp.pad(full["w_uq"].reshape(QL, H, dqk), ((0, 0), (0, 0), (0, QPAD - dqk))).reshape(QL, H * QPAD)
    ukv = full["w_ukv"].reshape(KVL, H, MLA_NOPE + MLA_V)
    w_ukv_p = jnp.concatenate([ukv[:, :, :MLA_NOPE].reshape(KVL, H * MLA_NOPE),
                               ukv[:, :, MLA_NOPE:].reshape(KVL, H * MLA_V)], axis=1)

    (kc, ksa, ksb), (qc, qsa, qsb) = _rope_tables(S)
    bias_pad = _pad_cols(fox_f_bias, LANE)

    small = _matmul(xn, w_small, "nn", [F32], "proj_small")
    qkv = _matmul(xn, w_qkv, "nn", [BF16], "proj_qkv")
    gpre = _matmul(xn, w_g, "nn", [F32], "proj_gates")
    cqn, ckvn, kr, cum = _prep_fwd(small, q_norm, kv_norm, bias_pad, kc, ksa, ksb, HF, "prep_fwd")
    q_rot = _matmul(cqn, w_uq_p, "nn", [BF16], "mla_q_up", tn=QPAD, row_extras=(qc, qsa, qsb),
                    epilogue=lambda acc, c, sa, sb: (_rope(acc, c, sa, sb, 1),))
    kv2 = _matmul(ckvn, w_ukv_p, "nn", [BF16], "mla_kv_up")

    mla = _AttT(S, H, (q_rot, QPAD, 0, True), [(kv2, MLA_NOPE, 0, True), (kr, LANE, 0, False)],
                (kv2, MLA_V, H, True), 1.0 / math.sqrt(dqk), True)
    o_mla, lse_mla = _att_fwd_t(mla, "mla_att_fwd")

    cum_t = jnp.transpose(cum[:, :HF]) * LOG2E
    cum_rep = jnp.broadcast_to(cum_t[:, :, None], (HF, S, min(QSUB, _tile(S, ATT_T))))
    fox = _AttT(S, HF, (qkv, FOX_HEAD_DIM, 0, True), [(qkv, FOX_HEAD_DIM, HF, True)],
                (qkv, FOX_HEAD_DIM, 2 * HF, True), 1.0 / math.sqrt(FOX_HEAD_DIM), False, cum_rep)
    o_fox, ox_fox, lse_fox = _att_fwd_t(fox, "fox_att_fwd", exact=True)

    own_b, land_b = _xchg_wait(ag_b, True, lse_fox, "all_gather_wait_b")
    full.update(pack_b.full(lax.dynamic_update_slice(land_b, own_b[None], (chip, 0, 0))))
    w_mb, w_fb, w_o, w_u, w_d = (full[n] for n in ("w_mla_branch", "w_fox_branch", "w_out", "w_up", "w_down"))

    y_mla = _matmul(o_mla, w_mb, "nn", [F32], "mla_branch")
    y_fox = _matmul(o_fox, w_fb, "nn", [F32], "fox_branch")
    merged = _gate_fwd(gpre, y_mla, y_fox, "gate_fwd")
    h1 = _matmul(merged, w_o, "nn", [F32], "out_proj", extras=(xs,), epilogue=lambda acc, r: (acc + r,))
    hn = _norm_fwd(h1, mlp_norm, "mlp_norm_fwd")

    def relu2(acc):
        a = jnp.maximum(acc, 0.0)
        return a * a, a

    u, a_pos = _matmul(hn, w_u, "nn", [BF16, BF16], "mlp_up", epilogue=relu2)
    h2 = _matmul(u, w_d, "nn", [F32], "mlp_down", tn=1024, extras=(h1,), epilogue=lambda acc, r: (acc + r,))
    dh2, dh2_b, g_final, loss_part = _final(h2, final_norm.reshape(1, D), target, "final_norm_loss")

    da = _matmul(dh2_b, w_d, "nt", [BF16], "mlp_down_dx", extras=(a_pos,),
                 epilogue=lambda acc, a: (acc * (2.0 * a.astype(F32)),))
    g_w_down = _mm_tn(u, dh2_b, "mlp_down_dw")
    dhn = _matmul(da, w_u, "nt", [F32], "mlp_up_dx", tn=1024)
    g_w_up = _mm_tn(hn, da, "mlp_up_dw")
    dh1, dh1_b, g_mlp_norm = _norm_bwd(h1, dhn, mlp_norm, dh2, "mlp_norm_bwd")
    dmerged = _matmul(dh1_b, w_o, "nt", [F32], "out_proj_dx")
    g_w_out = _mm_tn(merged, dh1_b, "out_proj_dw")
    dy_mla, dy_fox, dg_mla, dg_fox = _gate_bwd(dmerged, gpre, y_mla, y_fox, "gate_bwd")
    do_mla = _matmul(dy_mla, w_mb, "nt", [BF16], "mla_branch_dx")
    g_w_mb = _mm_tn(o_mla, dy_mla, "mla_branch_dw")
    do_fox = _matmul(dy_fox, w_fb, "nt", [BF16], "fox_branch_dx")
    g_w_fb = _mm_tn(o_fox, dy_fox, "fox_branch_dw")

    gp_b = pack_b.slabs({"w_mla_branch": g_w_mb, "w_fox_branch": g_w_fb, "w_out": g_w_out,
                         "w_up": g_w_up, "w_down": g_w_down})
    rs_b = _xchg_start(gp_b, lax.empty((3, RB, C), BF16), False, g_w_fb, "grad_scatter_start_b")

    delta_mla = _att_delta_t(do_mla, o_mla, H, "mla_att_delta", order=rs_b[3])
    dq_rot, dk_nope, dkr_heads, dv_mla = _att_bwd_t(mla, do_mla, lse_mla, delta_mla, BF16, [BF16, F32],
                                                    "mla_att_bwd", dq_rope=(qc, qsa, qsb))
    delta_fox = _att_delta_t(do_fox, ox_fox, HF, "fox_att_delta")
    dfq, dfk, dfv, dcum = _att_bwd_t(fox, do_fox, lse_fox, delta_fox, BF16, [BF16], "fox_att_bwd")

    gp_b_sent, recv_b = _xchg_wait(rs_b, False, dfq, "grad_scatter_wait_b")
    swap_b = _sib_start(_sum_slabs(gp_b_sent, recv_b, chip, "grad_sum_b"), "grad_swap_start_b")

    dcqn = _matmul(dq_rot, w_uq_p, "nt", [F32], "mla_q_up_dx", order=swap_b[4])
    g_w_uq_p = _mm_tn(cqn, dq_rot, "mla_q_up_dw")
    dkv2 = jnp.concatenate([dk_nope, dv_mla], axis=1)
    dckvn = _matmul(dkv2, w_ukv_p, "nt", [F32], "mla_kv_up_dx")
    g_w_ukv_p = _mm_tn(ckvn, dkv2, "mla_kv_up_dw")

    dcum_rows = jnp.pad(dcum[:, :, 0], ((0, 8 - HF), (0, 0)))
    dlogf_rows = _suffix_sum_rows(dcum_rows, "fox_forget_suffix_sum")
    dlogf = _pad_cols(jnp.transpose(dlogf_rows[:HF]), LANE)
    d_small, g_q_norm, g_kv_norm, g_bias = _prep_bwd(
        small, dcqn, dckvn, dkr_heads, dlogf, q_norm, kv_norm, bias_pad, kc, ksa, ksb, H, "prep_bwd")
    dproj = jnp.concatenate([d_small, dfq, dfk, dfv, dg_mla, dg_fox], axis=1)
    g_w_pack = _mm_tn(xn, dproj, "proj_dw")

    gs, gq, gg = g_w_pack[:, :WS], g_w_pack[:, WS:WS + NQKV], g_w_pack[:, WS + NQKV:]
    g_w_in = jnp.concatenate([gs[:, :o_kr], gs[:, o_kr:o_kr + MLA_ROPE], gq,
                              gs[:, o_kr + LANE:o_kr + LANE + HF], gg], axis=1)
    g_w_uq = g_w_uq_p.reshape(QL, H, QPAD)[:, :, :dqk].reshape(QL, H * dqk)
    g_w_ukv = jnp.concatenate([g_w_ukv_p[:, :H * MLA_NOPE].reshape(KVL, H, MLA_NOPE),
                               g_w_ukv_p[:, H * MLA_NOPE:].reshape(KVL, H, MLA_V)], axis=2).reshape(KVL, -1)

    gp_a = pack_a.slabs({"w_in": g_w_in, "w_uq": g_w_uq, "w_ukv": g_w_ukv})
    rs_a = _xchg_start(gp_a, lax.empty((3, RA, C), BF16), False, g_w_pack, "grad_scatter_start_a")
    dxn = _matmul(dproj, w_pack, "nt", [F32], "proj_dx", tn=1024, order=rs_a[3])
    grad_x, _, g_attn_norm = _norm_bwd(xs, dxn, attn_norm, dh1, "attn_norm_bwd")
    gp_a_sent, recv_a = _xchg_wait(rs_a, False, grad_x, "grad_scatter_wait_a")
    swap_a = _sib_start(_sum_slabs(gp_a_sent, recv_a, chip, "grad_sum_a"), "grad_swap_start_a")
    vec_w = max(D, LANE)
    vec_rows = [g_attn_norm, g_mlp_norm, g_final, g_q_norm, g_kv_norm, g_bias, loss_part]
    vec = jnp.concatenate([_pad_cols(v, vec_w) for v in vec_rows] + [jnp.zeros((1, vec_w), F32)], axis=0)
    vsum = _all_reduce_vec(vec, "all_reduce_vectors")
    part_b, sib_b = _sib_wait(swap_b, vsum, "grad_swap_wait_b")

    grads, deltas, new_m, new_v = {}, {}, {}, {}

    def update(pack, mine, theirs):
        for nm, shape, _ in pack.group:
            g, d, nm_, nv_ = _adamw(weights[nm], pack.part(mine, nm, shape), pack.part(theirs, nm, shape),
                                    moments[nm][0], moments[nm][1], "adamw_" + nm)
            grads[nm], deltas[nm], new_m[nm], new_v[nm] = g, d, nm_, nv_
        return g

    last_b = update(pack_b, part_b, sib_b)
    part_a, sib_a = _sib_wait(swap_a, last_b, "grad_swap_wait_a")
    update(pack_a, part_a, sib_a)

    vec_names = ["attn_norm", "mlp_norm", "final_norm", "q_norm", "kv_norm", "fox_f_bias"]

    def vec_pack(arrs):
        return jnp.concatenate([_pad_cols(a.reshape(1, -1), vec_w) for a in arrs]
                               + [jnp.zeros((2, vec_w), F32)], axis=0)[None]

    vg, vd, vm, vv = _adamw(vec_pack([weights[n] for n in vec_names]), vsum, jnp.zeros_like(vsum),
                            vec_pack([moments[n][0] for n in vec_names]), vec_pack([moments[n][1] for n in vec_names]),
                            "adamw_vectors")
    for r, nm in enumerate(vec_names):
        shp = weights[nm].shape
        n = weights[nm].size
        grads[nm] = vsum[r, :n].reshape(shp)
        deltas[nm], new_m[nm], new_v[nm] = (vd[0, r, :n].reshape(shp), vm[0, r, :n].reshape(shp),
                                            vv[0, r, :n].reshape(shp))
    loss = vsum[6, 0]

    order = ["attn_norm", "w_in", "fox_f_bias", "q_norm", "w_uq", "kv_norm", "w_ukv", "w_mla_branch", "w_fox_branch",
             "w_out", "mlp_norm", "w_up", "w_down", "final_norm"]
    return (loss, grad_x[None], *[grads[n] for n in order], *[deltas[n] for n in order],
            *[new_m[n] for n in order], *[new_v[n] for n in order])
```

```python
import math

import jax
import jax.numpy as jnp
from jax import lax
from jax.experimental import pallas as pl
from jax.experimental.pallas import tpu as pltpu

CHUNK = 64
MLA_HEADS = 8
MLA_Q_LORA = 512
MLA_KV_LORA = 256
MLA_NOPE = 128
MLA_ROPE = 64
MLA_V = 128
ROPE_THETA = 10000.0
FOX_HEADS = 8
FOX_HEAD_DIM = 128
EPS = 1e-6

ADAM_LR = 0.001
ADAM_B1 = 0.9
ADAM_B2 = 0.999
ADAM_EPS = 1e-08
ADAM_WD = 0.01
ADAM_STEP = 10

LANE = 128
QPAD = 2 * LANE
N_CHIPS = 4
N_DEV = 8
VMEM_LIMIT = 48 * 1024 * 1024
ATT_T = 1024
QSUB = 256
ROW_T = 256
PACK_ROWS = 256
LOG2E = 1.4426950408889634

BF16 = jnp.bfloat16
F32 = jnp.float32
MESH = pl.DeviceIdType.MESH

_NT = (((1,), (1,)), ((), ()))
_TN = (((0,), (0,)), ((), ()))
_NN = (((1,), (0,)), ((), ()))


def _tile(dim, pref, align=LANE):
    if dim <= pref:
        return dim
    t = (pref // align) * align
    while t >= align:
        if dim % t == 0:
            return t
        t -= align
    return dim


def _params(sem=None):
    return pltpu.CompilerParams(dimension_semantics=sem, vmem_limit_bytes=VMEM_LIMIT)


_ANY_SPEC = pl.BlockSpec(memory_space=pl.ANY)


def _matmul(a, b, mode, out_dtypes, name, *, tm=1024, tn=512, tk=2048, extras=(), row_extras=(), col_extras=(),
            epilogue=None, order=None):
    if mode == "nn":
        (M, K), (K2, N) = a.shape, b.shape
    elif mode == "nt":
        (M, K), (N, K2) = a.shape, b.shape
    else:
        (K, M), (K2, N) = a.shape, b.shape
    assert K == K2, (name, a.shape, b.shape)
    tm, tn, tk = _tile(M, tm), _tile(N, tn), _tile(K, tk)
    nk = K // tk
    n_out = len(out_dtypes)
    n_ex = len(extras) + len(row_extras) + len(col_extras)
    n_ord = 0 if order is None else 1
    assert all(r.shape == (M, tn) for r in row_extras), name
    dims = {"nn": _NN, "nt": _NT, "tn": _TN}[mode]

    def body(*refs):
        a_ref, b_ref = refs[0], refs[1]
        ex_refs = refs[2:2 + n_ex]
        o_refs = refs[2 + n_ex + n_ord:2 + n_ex + n_ord + n_out]
        acc_ref = refs[2 + n_ex + n_ord + n_out]
        k = pl.program_id(2)
        part = lax.dot_general(a_ref[...], b_ref[...], dims, preferred_element_type=F32)

        @pl.when(k == 0)
        def _():
            acc_ref[...] = part

        @pl.when(k > 0)
        def _():
            acc_ref[...] += part

        @pl.when(k == nk - 1)
        def _():
            acc = acc_ref[...]
            if epilogue is None:
                outs = (acc,)
            else:
                outs = epilogue(acc, *[r[...] for r in ex_refs])
            for o_ref, o in zip(o_refs, outs):
                o_ref[...] = o.astype(o_ref.dtype)

    if mode == "nn":
        a_spec = pl.BlockSpec((tm, tk), lambda i, j, k: (i, k))
        b_spec = pl.BlockSpec((tk, tn), lambda i, j, k: (k, j))
    elif mode == "nt":
        a_spec = pl.BlockSpec((tm, tk), lambda i, j, k: (i, k))
        b_spec = pl.BlockSpec((tn, tk), lambda i, j, k: (j, k))
    else:
        a_spec = pl.BlockSpec((tk, tm), lambda i, j, k: (k, i))
        b_spec = pl.BlockSpec((tk, tn), lambda i, j, k: (k, j))
    mn_spec = pl.BlockSpec((tm, tn), lambda i, j, k: (i, j))
    row_spec = pl.BlockSpec((tm, tn), lambda i, j, k: (i, 0))
    col_spec = pl.BlockSpec((1, tn), lambda i, j, k: (0, j))
    outs = pl.pallas_call(
        body,
        name=name,
        grid=(M // tm, N // tn, nk),
        in_specs=([a_spec, b_spec] + [mn_spec] * len(extras) + [row_spec] * len(row_extras)
                  + [col_spec] * len(col_extras) + [_ANY_SPEC] * n_ord),
        out_specs=[mn_spec] * n_out,
        out_shape=[jax.ShapeDtypeStruct((M, N), dt) for dt in out_dtypes],
        scratch_shapes=[pltpu.VMEM((tm, tn), F32)],
        compiler_params=_params(("parallel", "parallel", "arbitrary")),
    )(a, b, *extras, *row_extras, *col_extras, *([] if order is None else [order]))
    return outs[0] if n_out == 1 else outs


def _mm_tn(a, b, name):
    return _matmul(a, b, "tn", [F32], name, tm=1024, tn=1024, tk=2048)


def _row_spec(ts, width, col=0):
    return pl.BlockSpec((ts, width), lambda i: (i, col))


def _full_spec(shape):
    return pl.BlockSpec(shape, lambda i: tuple(0 for _ in shape))


def _rms(x):
    return lax.rsqrt(jnp.mean(x * x, axis=-1, keepdims=True) + EPS)


def _rms_bwd(x, dy, g):
    r = _rms(x)
    xh = x * r
    gy = dy * g
    dx = r * (gy - xh * jnp.mean(xh * gy, axis=-1, keepdims=True))
    return dx, dy * xh


def _norm_fwd(x, g, name, order=None):
    S, D = x.shape
    ts = _tile(S, ROW_T, 8)

    def body(x_ref, g_ref, *rest):
        o_ref = rest[-1]
        xv = x_ref[...]
        o_ref[...] = ((xv * _rms(xv)) * g_ref[...]).astype(BF16)

    extra = [] if order is None else [order]
    return pl.pallas_call(
        body, name=name, grid=(S // ts,),
        in_specs=[_row_spec(ts, D), _full_spec((1, D))] + [_ANY_SPEC] * len(extra),
        out_specs=_row_spec(ts, D),
        out_shape=jax.ShapeDtypeStruct((S, D), BF16),
        compiler_params=_params(("parallel",)),
    )(x, g, *extra)


def _norm_bwd(x, dy, g, dres, name):
    S, D = x.shape
    ts = _tile(S, ROW_T, 8)

    def body(x_ref, dy_ref, g_ref, dres_ref, dx_ref, dxb_ref, dg_ref):
        dx, dg_rows = _rms_bwd(x_ref[...], dy_ref[...], g_ref[...])
        dx = dres_ref[...] + dx
        dx_ref[...] = dx
        dxb_ref[...] = dx.astype(BF16)

        @pl.when(pl.program_id(0) == 0)
        def _():
            dg_ref[...] = jnp.zeros_like(dg_ref)

        dg_ref[...] += jnp.sum(dg_rows, axis=0, keepdims=True)

    return pl.pallas_call(
        body, name=name, grid=(S // ts,),
        in_specs=[_row_spec(ts, D), _row_spec(ts, D), _full_spec((1, D)), _row_spec(ts, D)],
        out_specs=[_row_spec(ts, D), _row_spec(ts, D), _full_spec((1, D))],
        out_shape=[jax.ShapeDtypeStruct((S, D), F32), jax.ShapeDtypeStruct((S, D), BF16),
                   jax.ShapeDtypeStruct((1, D), F32)],
        compiler_params=_params(("arbitrary",)),
    )(x, dy, g, dres)


def _rope(x, c, sa, sb, sign):
    w = x.shape[-1]
    half = MLA_ROPE // 2
    fwd = pltpu.roll(x, w - half, 1)
    back = pltpu.roll(x, half, 1)
    if sign < 0:
        return x * c - fwd * sa - back * sb
    return x * c + fwd * sa + back * sb


def _split3(x):
    hi = x.astype(BF16)
    r1 = x - hi.astype(F32)
    mid = r1.astype(BF16)
    lo = (r1 - mid.astype(F32)).astype(BF16)
    return hi, mid, lo


def _prep_fwd(small, q_norm, kv_norm, bias_pad, kc, ksa, ksb, n_heads, name):
    S, W = small.shape
    QL, KVL = q_norm.shape[1], kv_norm.shape[1]
    assert W == QL + KVL + 2 * LANE
    ts = _tile(S, ROW_T, 8)
    tri = (lax.broadcasted_iota(jnp.int32, (ts, ts), 0) >= lax.broadcasted_iota(jnp.int32, (ts, ts), 1)).astype(BF16)

    def body(s_ref, qn_ref, kvn_ref, b_ref, kc_ref, ksa_ref, ksb_ref, tri_ref,
             cqn_ref, ckvn_ref, kr_ref, cum_ref, carry_ref):
        cq = s_ref[:, 0:QL]
        cqn_ref[...] = ((cq * _rms(cq)) * qn_ref[...]).astype(BF16)
        ckv = s_ref[:, QL:QL + KVL]
        ckvn_ref[...] = ((ckv * _rms(ckv)) * kvn_ref[...]).astype(BF16)
        kr = s_ref[:, QL + KVL:QL + KVL + LANE]
        kr_ref[...] = _rope(kr, kc_ref[...], ksa_ref[...], ksb_ref[...], 1).astype(BF16)
        z = s_ref[:, QL + KVL + LANE:W] + b_ref[...]
        logf = jnp.minimum(z, 0.0) - jnp.log1p(jnp.exp(-jnp.abs(z)))
        lane = lax.broadcasted_iota(jnp.int32, logf.shape, 1)
        logf = jnp.where(lane < n_heads, logf, 0.0)

        @pl.when(pl.program_id(0) == 0)
        def _():
            carry_ref[...] = jnp.zeros_like(carry_ref)

        t = tri_ref[...]
        cum = carry_ref[...]
        for part in _split3(logf):
            cum = cum + jnp.dot(t, part, preferred_element_type=F32)
        cum_ref[...] = cum
        carry_ref[...] = cum[ts - 1:ts, :]

    return pl.pallas_call(
        body, name=name, grid=(S // ts,),
        in_specs=[_row_spec(ts, W), _full_spec((1, QL)), _full_spec((1, KVL)), _full_spec((1, LANE)),
                  _row_spec(ts, LANE), _row_spec(ts, LANE), _row_spec(ts, LANE), _full_spec((ts, ts))],
        out_specs=[_row_spec(ts, QL), _row_spec(ts, KVL), _row_spec(ts, LANE), _row_spec(ts, LANE)],
        out_shape=[jax.ShapeDtypeStruct((S, QL), BF16), jax.ShapeDtypeStruct((S, KVL), BF16),
                   jax.ShapeDtypeStruct((S, LANE), BF16), jax.ShapeDtypeStruct((S, LANE), F32)],
        scratch_shapes=[pltpu.VMEM((1, LANE), F32)],
        compiler_params=_params(("arbitrary",)),
    )(small, q_norm, kv_norm, bias_pad, kc, ksa, ksb, tri)


def _prep_bwd(small, dcqn, dckvn, dkr_heads, dlogf, q_norm, kv_norm, bias_pad, kc, ksa, ksb, n_heads, name):
    S, W = small.shape
    QL, KVL = q_norm.shape[1], kv_norm.shape[1]
    ts = _tile(S, ROW_T, 8)

    def body(s_ref, dcq_ref, dckv_ref, dkr_ref, dlf_ref, qn_ref, kvn_ref, b_ref, kc_ref, ksa_ref, ksb_ref,
             ds_ref, gq_ref, gkv_ref, gb_ref):
        dcq, gq_rows = _rms_bwd(s_ref[:, 0:QL], dcq_ref[...], qn_ref[...])
        ds_ref[:, 0:QL] = dcq.astype(BF16)
        dckv, gkv_rows = _rms_bwd(s_ref[:, QL:QL + KVL], dckv_ref[...], kvn_ref[...])
        ds_ref[:, QL:QL + KVL] = dckv.astype(BF16)
        dkr = dkr_ref[:, 0:LANE]
        for h in range(1, n_heads):
            dkr = dkr + dkr_ref[:, h * LANE:(h + 1) * LANE]
        ds_ref[:, QL + KVL:QL + KVL + LANE] = _rope(dkr, kc_ref[...], ksa_ref[...], ksb_ref[...], -1).astype(BF16)
        z = s_ref[:, QL + KVL + LANE:W] + b_ref[...]
        dff = dlf_ref[...] * (1.0 / (1.0 + jnp.exp(z)))
        ds_ref[:, QL + KVL + LANE:W] = dff.astype(BF16)

        @pl.when(pl.program_id(0) == 0)
        def _():
            gq_ref[...] = jnp.zeros_like(gq_ref)
            gkv_ref[...] = jnp.zeros_like(gkv_ref)
            gb_ref[...] = jnp.zeros_like(gb_ref)

        gq_ref[...] += jnp.sum(gq_rows, axis=0, keepdims=True)
        gkv_ref[...] += jnp.sum(gkv_rows, axis=0, keepdims=True)
        gb_ref[...] += jnp.sum(dff, axis=0, keepdims=True)

    return pl.pallas_call(
        body, name=name, grid=(S // ts,),
        in_specs=[_row_spec(ts, W), _row_spec(ts, QL), _row_spec(ts, KVL), _row_spec(ts, n_heads * LANE),
                  _row_spec(ts, LANE), _full_spec((1, QL)), _full_spec((1, KVL)), _full_spec((1, LANE)),
                  _row_spec(ts, LANE), _row_spec(ts, LANE), _row_spec(ts, LANE)],
        out_specs=[_row_spec(ts, W), _full_spec((1, QL)), _full_spec((1, KVL)), _full_spec((1, LANE))],
        out_shape=[jax.ShapeDtypeStruct((S, W), BF16), jax.ShapeDtypeStruct((1, QL), F32),
                   jax.ShapeDtypeStruct((1, KVL), F32), jax.ShapeDtypeStruct((1, LANE), F32)],
        compiler_params=_params(("arbitrary",)),
    )(small, dcqn, dckvn, dkr_heads, dlogf, q_norm, kv_norm, bias_pad, kc, ksa, ksb)


def _sigmoid(z):
    return 1.0 / (1.0 + jnp.exp(-z))


def _gate_fwd(gpre, y_mla, y_fox, name):
    S, D = y_mla.shape
    ts = _tile(S, ROW_T, 8)

    def body(ga_ref, gb_ref, ya_ref, yb_ref, o_ref):
        o_ref[...] = (_sigmoid(ga_ref[...]) * ya_ref[...] + _sigmoid(gb_ref[...]) * yb_ref[...]).astype(BF16)

    return pl.pallas_call(
        body, name=name, grid=(S // ts,),
        in_specs=[_row_spec(ts, D, 0), _row_spec(ts, D, 1), _row_spec(ts, D), _row_spec(ts, D)],
        out_specs=_row_spec(ts, D),
        out_shape=jax.ShapeDtypeStruct((S, D), BF16),
        compiler_params=_params(("parallel",)),
    )(gpre, gpre, y_mla, y_fox)


def _gate_bwd(dmerged, gpre, y_mla, y_fox, name):
    S, D = y_mla.shape
    ts = _tile(S, ROW_T, 8)

    def body(dm_ref, ga_ref, gb_ref, ya_ref, yb_ref, dya_ref, dyb_ref, dga_ref, dgb_ref):
        dm = dm_ref[...]
        ga = _sigmoid(ga_ref[...])
        gb = _sigmoid(gb_ref[...])
        dya_ref[...] = (dm * ga).astype(BF16)
        dyb_ref[...] = (dm * gb).astype(BF16)
        dga_ref[...] = (dm * ya_ref[...] * (ga * (1.0 - ga))).astype(BF16)
        dgb_ref[...] = (dm * yb_ref[...] * (gb * (1.0 - gb))).astype(BF16)

    return pl.pallas_call(
        body, name=name, grid=(S // ts,),
        in_specs=[_row_spec(ts, D), _row_spec(ts, D, 0), _row_spec(ts, D, 1), _row_spec(ts, D), _row_spec(ts, D)],
        out_specs=[_row_spec(ts, D)] * 4,
        out_shape=[jax.ShapeDtypeStruct((S, D), BF16)] * 4,
        compiler_params=_params(("parallel",)),
    )(dmerged, gpre, gpre, y_mla, y_fox)


def _final(h, g, target, name):
    S, D = h.shape
    ts = _tile(S, ROW_T, 8)

    def body(h_ref, g_ref, t_ref, dh_ref, dhb_ref, dg_ref, loss_ref):
        hv = h_ref[...]
        gv = g_ref[...]
        err = (hv * _rms(hv)) * gv - t_ref[...]
        dh, dg_rows = _rms_bwd(hv, err / D, gv)
        dh_ref[...] = dh
        dhb_ref[...] = dh.astype(BF16)

        @pl.when(pl.program_id(0) == 0)
        def _():
            dg_ref[...] = jnp.zeros_like(dg_ref)
            loss_ref[...] = jnp.zeros_like(loss_ref)

        dg_ref[...] += jnp.sum(dg_rows, axis=0, keepdims=True)
        row_loss = jnp.mean(err * err, axis=-1, keepdims=True)
        loss_ref[...] += 0.5 * jnp.sum(row_loss, axis=0, keepdims=True)

    return pl.pallas_call(
        body, name=name, grid=(S // ts,),
        in_specs=[_row_spec(ts, D), _full_spec((1, D)), _row_spec(ts, D)],
        out_specs=[_row_spec(ts, D), _row_spec(ts, D), _full_spec((1, D)), _full_spec((1, LANE))],
        out_shape=[jax.ShapeDtypeStruct((S, D), F32), jax.ShapeDtypeStruct((S, D), BF16),
                   jax.ShapeDtypeStruct((1, D), F32), jax.ShapeDtypeStruct((1, LANE), F32)],
        compiler_params=_params(("arbitrary",)),
    )(h, g, target)


def _suffix_sum_rows(x, name):
    R, S = x.shape
    tb = _tile(S, 512)
    nb = S // tb
    tri = (lax.broadcasted_iota(jnp.int32, (tb, tb), 0) >= lax.broadcasted_iota(jnp.int32, (tb, tb), 1)).astype(BF16)

    def body(x_ref, tri_ref, o_ref, carry_ref):
        @pl.when(pl.program_id(0) == 0)
        def _():
            carry_ref[...] = jnp.zeros_like(carry_ref)

        xv = x_ref[...]
        t = tri_ref[...]
        acc = jnp.broadcast_to(carry_ref[:, 0:1], xv.shape)
        for part in _split3(xv):
            acc = acc + jnp.dot(part, t, preferred_element_type=F32)
        o_ref[...] = acc
        carry_ref[...] = jnp.broadcast_to(acc[:, 0:1], carry_ref.shape)

    rev = pl.BlockSpec((R, tb), lambda i: (0, nb - 1 - i))
    return pl.pallas_call(
        body, name=name, grid=(nb,),
        in_specs=[rev, _full_spec((tb, tb))], out_specs=rev,
        out_shape=jax.ShapeDtypeStruct((R, S), F32),
        scratch_shapes=[pltpu.VMEM((R, LANE), F32)],
        compiler_params=_params(("arbitrary",)),
    )(x, tri)


def _pairs(nb, by_key):
    if by_key:
        pr = [(i, j) for j in range(nb) for i in range(j, nb)]
    else:
        pr = [(i, j) for i in range(nb) for j in range(i + 1)]
    return (jnp.asarray([p[0] for p in pr], jnp.int32), jnp.asarray([p[1] for p in pr], jnp.int32), len(pr))


class _AttT:
    def __init__(self, S, n_heads, q, ks, v, scale, chunk_causal, cum_rep=None):
        self.S, self.H, self.q, self.ks, self.v = S, n_heads, q, ks, v
        self.scale, self.chunk_causal, self.cum_rep = scale, chunk_causal, cum_rep
        self.T = _tile(S, ATT_T)
        self.qs = min(QSUB, self.T)
        self.nb = S // self.T
        self.dq, self.dv = q[1], v[1]
        self.has_bias = cum_rep is not None

    def q_spec(self, op):
        _, w, off, per_head = op
        return pl.BlockSpec((self.T, w), lambda h, p, it, jt: (it[p], off + (h if per_head else 0)))

    def k_spec(self, op):
        _, w, off, per_head = op
        return pl.BlockSpec((self.T, w), lambda h, p, it, jt: (jt[p], off + (h if per_head else 0)))

    def row_q(self):
        return pl.BlockSpec((None, 1, self.T), lambda h, p, it, jt: (h, 0, it[p]))

    def cum_k(self):
        return pl.BlockSpec((None, self.T, self.qs), lambda h, p, it, jt: (h, jt[p], 0))

    def sub_blocks(self, masked):
        return [(q0, min(self.T, q0 + self.qs) if masked else self.T) for q0 in range(0, self.T, self.qs)]

    def scores(self, k, q_sub, cum, q0, masked):
        s = lax.dot_general(k, q_sub, _NT, preferred_element_type=F32)
        if self.has_bias:
            s = s - cum
        mask = None
        if masked:
            r = lax.broadcasted_iota(jnp.int32, s.shape, 0)
            c = lax.broadcasted_iota(jnp.int32, s.shape, 1) + q0
            mask = (r // CHUNK <= c // CHUNK) if self.chunk_causal else (r <= c)
        return s, mask


def _join(k_refs):
    return k_refs[0][...] if len(k_refs) == 1 else jnp.concatenate([r[...] for r in k_refs], axis=-1)


def _att_fwd_t(att, name, exact=False):
    S, H, T, qs = att.S, att.H, att.T, att.qs
    it, jt, npairs = _pairs(att.nb, by_key=False)
    nk = len(att.ks)

    def body(it_ref, jt_ref, *refs):
        q_ref = refs[0]
        k_refs = refs[1:1 + nk]
        v_ref = refs[1 + nk]
        n = 2 + nk
        cum_ref = None
        if att.has_bias:
            cum_ref = refs[n]
            n += 1
        o_ref = refs[n]
        n += 1
        ox_ref = None
        if exact:
            ox_ref = refs[n]
            n += 1
        lse_ref, m_ref, l_ref, acc_ref = refs[n:n + 4]
        lo_ref = refs[n + 4] if exact else None
        p = pl.program_id(1)
        i, j = it_ref[p], jt_ref[p]

        @pl.when(j == 0)
        def _():
            m_ref[...] = jnp.full_like(m_ref, -jnp.inf)
            l_ref[...] = jnp.zeros_like(l_ref)
            acc_ref[...] = jnp.zeros_like(acc_ref)
            if exact:
                lo_ref[...] = jnp.zeros_like(lo_ref)

        def step(masked):
            k = _join(k_refs)
            v = v_ref[...]
            subs = att.sub_blocks(masked)

            def logits(idx):
                q0, nkeys = subs[idx]
                cum = cum_ref[0:nkeys, :] if att.has_bias else None
                return att.scores(k[0:nkeys], q_ref[q0:q0 + qs, :], cum, q0, masked)

            ahead = logits(0)
            for idx, (q0, nkeys) in enumerate(subs):
                qsl = slice(q0, q0 + qs)
                s, mask = ahead
                if idx + 1 < len(subs):
                    ahead = logits(idx + 1)
                if masked:
                    s = jnp.where(mask, s, -jnp.inf)
                m_prev = m_ref[:, qsl]
                m_new = jnp.maximum(m_prev, jnp.max(s, axis=0, keepdims=True))
                alpha = jnp.exp2(m_prev - m_new)
                pr = jnp.exp2(s - m_new)
                l_ref[:, qsl] = alpha * l_ref[:, qsl] + jnp.sum(pr, axis=0, keepdims=True)
                p_hi = pr.astype(BF16)
                acc_ref[:, qsl] = alpha * acc_ref[:, qsl] + lax.dot_general(
                    v[0:nkeys], p_hi, _TN, preferred_element_type=F32)
                if exact:
                    p_lo = (pr - p_hi.astype(F32)).astype(BF16)
                    lo_ref[:, qsl] = alpha * lo_ref[:, qsl] + lax.dot_general(
                        v[0:nkeys], p_lo, _TN, preferred_element_type=F32)
                m_ref[:, qsl] = m_new

        @pl.when(j < i)
        def _():
            step(False)

        @pl.when(j == i)
        def _():
            step(True)
            l = l_ref[...]
            inv = 1.0 / l
            o_ref[...] = jnp.transpose(acc_ref[...] * inv).astype(o_ref.dtype)
            if exact:
                ox_ref[...] = jnp.transpose((acc_ref[...] + lo_ref[...]) * inv)
            lse_ref[...] = m_ref[...] + jnp.log2(l)

    in_specs = [att.q_spec(att.q)] + [att.k_spec(k) for k in att.ks] + [att.k_spec(att.v)]
    args = [att.q[0]] + [k[0] for k in att.ks] + [att.v[0]]
    if att.has_bias:
        in_specs.append(att.cum_k())
        args.append(att.cum_rep)
    o_spec = pl.BlockSpec((T, att.dv), lambda h, p, it, jt: (it[p], h))
    out_specs = [o_spec]
    out_shape = [jax.ShapeDtypeStruct((S, H * att.dv), BF16)]
    scratch = [pltpu.VMEM((1, T), F32), pltpu.VMEM((1, T), F32), pltpu.VMEM((att.dv, T), F32)]
    if exact:
        out_specs.append(o_spec)
        out_shape.append(jax.ShapeDtypeStruct((S, H * att.dv), F32))
        scratch.append(pltpu.VMEM((att.dv, T), F32))
    out_specs.append(att.row_q())
    out_shape.append(jax.ShapeDtypeStruct((H, 1, S), F32))
    return pl.pallas_call(
        body, name=name,
        grid_spec=pltpu.PrefetchScalarGridSpec(
            num_scalar_prefetch=2, grid=(H, npairs), in_specs=in_specs, out_specs=out_specs,
            scratch_shapes=scratch),
        out_shape=out_shape,
        compiler_params=_params(("parallel", "arbitrary")),
    )(it, jt, *args)


def _att_delta_t(do, o, n_heads, name, order=None):
    S = do.shape[0]
    w = do.shape[1] // n_heads
    ts = _tile(S, ATT_T)
    ones = jnp.ones((8, w), BF16)
    extra = [] if order is None else [order]

    def body(do_ref, o_ref, ones_ref, *rest):
        d_ref = rest[-1]
        prod = do_ref[...].astype(F32) * o_ref[...].astype(F32)
        acc = jnp.zeros((8, ts), F32)
        for part in _split3(prod):
            acc = acc + lax.dot_general(ones_ref[...], part, _NT, preferred_element_type=F32)
        d_ref[...] = acc[0:1, :]

    blk = pl.BlockSpec((ts, w), lambda i, h: (i, h))
    return pl.pallas_call(
        body, name=name, grid=(S // ts, n_heads),
        in_specs=[blk, blk, pl.BlockSpec((8, w), lambda i, h: (0, 0))] + [_ANY_SPEC] * len(extra),
        out_specs=pl.BlockSpec((None, 1, ts), lambda i, h: (h, 0, i)),
        out_shape=jax.ShapeDtypeStruct((n_heads, 1, S), F32),
        compiler_params=_params(("parallel", "parallel")),
    )(do, o, ones, *extra)


def _att_bwd_t(att, do, lse, delta, dq_dtype, dk_dtypes, name, dq_rope=None):
    S, H, T, qs = att.S, att.H, att.T, att.qs
    it, jt, npairs = _pairs(att.nb, by_key=True)
    nk = len(att.ks)
    last = att.nb - 1
    widths = [k[1] for k in att.ks]

    def body(it_ref, jt_ref, *refs):
        q_ref = refs[0]
        k_refs = refs[1:1 + nk]
        v_ref, do_ref, lse_ref, dl_ref = refs[1 + nk:5 + nk]
        n = 5 + nk
        cum_ref = None
        if att.has_bias:
            cum_ref = refs[n]
            n += 1
        rope_refs = None
        if dq_rope is not None:
            rope_refs = refs[n:n + 3]
            n += 3
        dq_ref = refs[n]
        dk_refs = refs[n + 1:n + 1 + nk]
        dv_ref = refs[n + 1 + nk]
        n += nk + 2
        dc_ref = None
        if att.has_bias:
            dc_ref = refs[n]
            n += 1
        dq_acc, dk_acc, dv_acc = refs[n:n + 3]
        dc_acc = refs[n + 3] if att.has_bias else None
        p = pl.program_id(1)
        i, j = it_ref[p], jt_ref[p]

        @pl.when(p == 0)
        def _():
            dq_acc[...] = jnp.zeros_like(dq_acc)

        @pl.when(i == j)
        def _():
            dk_acc[...] = jnp.zeros_like(dk_acc)
            dv_acc[...] = jnp.zeros_like(dv_acc)
            if att.has_bias:
                dc_acc[...] = jnp.zeros_like(dc_acc)

        def step(masked):
            k = _join(k_refs)
            v = v_ref[...]
            subs = att.sub_blocks(masked)

            def logits(idx):
                q0, nkeys = subs[idx]
                cum = cum_ref[0:nkeys, :] if att.has_bias else None
                return att.scores(k[0:nkeys], q_ref[q0:q0 + qs, :], cum, q0, masked)

            ahead = logits(0)
            for idx, (q0, nkeys) in enumerate(subs):
                qsl = slice(q0, q0 + qs)
                ksl = slice(0, nkeys)
                q_sub = q_ref[qsl, :]
                do_sub = do_ref[qsl, :]
                s, mask = ahead
                if idx + 1 < len(subs):
                    ahead = logits(idx + 1)
                pr = jnp.exp2(s - lse_ref[:, qsl])
                if masked:
                    pr = jnp.where(mask, pr, 0.0)
                dp = lax.dot_general(v[ksl], do_sub, _NT, preferred_element_type=F32)
                ds = pr * (dp - dl_ref[:, qsl])
                ds_b = ds.astype(BF16)
                dv_acc[ksl, :] += jnp.dot(pr.astype(BF16), do_sub, preferred_element_type=F32)
                dk_acc[ksl, :] += jnp.dot(ds_b, q_sub, preferred_element_type=F32)
                dq_acc[i, :, qsl] += lax.dot_general(k[ksl], ds_b, _TN, preferred_element_type=F32)
                if att.has_bias:
                    part = ds[:, 0:LANE] if qs >= LANE else ds
                    for c0 in range(LANE, qs, LANE):
                        part = part + ds[:, c0:c0 + LANE]
                    dc_acc[ksl, :] += part

        @pl.when(i > j)
        def _():
            step(False)

        @pl.when(i == j)
        def _():
            step(True)
            dq = jnp.transpose(dq_acc[i] * att.scale)
            if dq_rope is not None:
                dq = _rope(dq, rope_refs[0][...], rope_refs[1][...], rope_refs[2][...], -1)
            dq_ref[...] = dq.astype(dq_ref.dtype)

        @pl.when(i == last)
        def _():
            dk = dk_acc[...] * (1.0 / LOG2E)
            off = 0
            for r, w in zip(dk_refs, widths):
                r[...] = dk[:, off:off + w].astype(r.dtype)
                off += w
            dv_ref[...] = dv_acc[...].astype(dv_ref.dtype)
            if att.has_bias:
                dc_ref[...] = -jnp.sum(dc_acc[...], axis=-1, keepdims=True)

    do_op = (do, att.dv, 0, True)
    in_specs = ([att.q_spec(att.q)] + [att.k_spec(k) for k in att.ks]
                + [att.k_spec(att.v), att.q_spec(do_op), att.row_q(), att.row_q()])
    args = [att.q[0]] + [k[0] for k in att.ks] + [att.v[0], do, lse, delta]
    if att.has_bias:
        in_specs.append(att.cum_k())
        args.append(att.cum_rep)
    if dq_rope is not None:
        in_specs += [pl.BlockSpec((T, att.dq), lambda h, p, it, jt: (jt[p], 0))] * 3
        args += list(dq_rope)
    out_specs = [pl.BlockSpec((T, att.dq), lambda h, p, it, jt: (jt[p], h))]
    out_shape = [jax.ShapeDtypeStruct((S, H * att.dq), dq_dtype)]
    out_specs += [pl.BlockSpec((T, w), lambda h, p, it, jt: (jt[p], h)) for w in widths]
    out_shape += [jax.ShapeDtypeStruct((S, H * w), dt) for w, dt in zip(widths, dk_dtypes)]
    out_specs.append(pl.BlockSpec((T, att.dv), lambda h, p, it, jt: (jt[p], h)))
    out_shape.append(jax.ShapeDtypeStruct((S, H * att.dv), BF16))
    scratch = [pltpu.VMEM((att.nb, att.dq, T), F32), pltpu.VMEM((T, att.dq), F32), pltpu.VMEM((T, att.dv), F32)]
    if att.has_bias:
        out_specs.append(pl.BlockSpec((None, T, 1), lambda h, p, it, jt: (h, jt[p], 0)))
        out_shape.append(jax.ShapeDtypeStruct((H, S, 1), F32))
        scratch.append(pltpu.VMEM((T, min(qs, LANE)), F32))
    return pl.pallas_call(
        body, name=name,
        grid_spec=pltpu.PrefetchScalarGridSpec(
            num_scalar_prefetch=2, grid=(H, npairs), in_specs=in_specs, out_specs=out_specs,
            scratch_shapes=scratch),
        out_shape=out_shape,
        compiler_params=_params(("parallel", "arbitrary")),
    )(it, jt, *args)


def _adamw(w, g1, g2, m, v, name):
    _, K, N = w.shape
    assert g1.shape == (K, N) and g2.shape == (K, N), name
    by_rows = K % 8 == 0
    tr = _tile(K, 256, 8) if by_rows else K
    tc = N if by_rows else _tile(N, LANE)
    c1 = 1.0 - ADAM_B1 ** ADAM_STEP
    c2 = 1.0 - ADAM_B2 ** ADAM_STEP

    def body(w_ref, g1_ref, g2_ref, m_ref, v_ref, g_ref, d_ref, nm_ref, nv_ref):
        gv = g1_ref[...] + g2_ref[...]
        nm = ADAM_B1 * m_ref[...] + (1.0 - ADAM_B1) * gv
        nv = ADAM_B2 * v_ref[...] + (1.0 - ADAM_B2) * (gv * gv)
        g_ref[...] = gv
        d_ref[...] = -ADAM_LR * ((nm / c1) / (jnp.sqrt(nv / c2) + ADAM_EPS) + ADAM_WD * w_ref[...])
        nm_ref[...] = nm
        nv_ref[...] = nv

    if by_rows:
        blk = pl.BlockSpec((None, tr, N), lambda i: (0, i, 0))
        gblk = pl.BlockSpec((tr, N), lambda i: (i, 0))
    else:
        blk = pl.BlockSpec((None, K, tc), lambda i: (0, 0, i))
        gblk = pl.BlockSpec((K, tc), lambda i: (0, i))
    return pl.pallas_call(
        body, name=name, grid=(K // tr if by_rows else N // tc,),
        in_specs=[blk, gblk, gblk, blk, blk], out_specs=[blk] * 4,
        out_shape=[jax.ShapeDtypeStruct((1, K, N), F32)] * 4,
        compiler_params=_params(("parallel",)),
    )(w, g1, g2, m, v)


_HBM_SPEC = pl.BlockSpec(memory_space=pltpu.HBM)
_SEM_SPEC = pl.BlockSpec(memory_space=pltpu.SEMAPHORE)
_VMEM_SPEC = pl.BlockSpec(memory_space=pltpu.VMEM)
_EFFECT = pltpu.SideEffectType.DATAFLOW_SIDE_EFFECTING


def _place():
    return lax.axis_index("x"), lax.axis_index("y"), lax.axis_index("c")


def _other_chips(x, y):
    return [(1 - x, y), (x, 1 - y), (1 - x, 1 - y)]


def _all_gather_halves(wp, name):
    R, C = wp.shape
    half = R // 2
    assert half % 16 == 0

    def body(w_ref, out_ref, ici_send, ici_recv, d2d_send, d2d_recv, local_sem):
        x, y, c = _place()
        me = 2 * x + y
        chips = _other_chips(x, y)
        mine = pl.ds(pl.multiple_of(c * half, 16), half)
        theirs = pl.ds(pl.multiple_of((1 - c) * half, 16), half)
        local = pltpu.make_async_copy(w_ref, out_ref.at[me], local_sem)
        local.start()
        sends = []
        for n, (px, py) in enumerate(chips):
            cp = pltpu.make_async_remote_copy(
                src_ref=w_ref.at[mine], dst_ref=out_ref.at[me, mine], send_sem=ici_send.at[n],
                recv_sem=ici_recv.at[n], device_id=(px, py, c), device_id_type=MESH)
            cp.start()
            sends.append(cp)
        for n, (px, py) in enumerate(chips):
            slot = 2 * px + py
            pltpu.make_async_remote_copy(
                src_ref=w_ref.at[mine], dst_ref=out_ref.at[slot, mine], send_sem=ici_send.at[n],
                recv_sem=ici_recv.at[n], device_id=(px, py, c), device_id_type=MESH).wait_recv()
            cp = pltpu.make_async_remote_copy(
                src_ref=out_ref.at[slot, mine], dst_ref=out_ref.at[slot, mine], send_sem=d2d_send.at[n],
                recv_sem=d2d_recv.at[n], device_id=(x, y, 1 - c), device_id_type=MESH)
            cp.start()
            sends.append(cp)
        for n, (px, py) in enumerate(chips):
            slot = 2 * px + py
            pltpu.make_async_remote_copy(
                src_ref=out_ref.at[slot, theirs], dst_ref=out_ref.at[slot, theirs], send_sem=d2d_send.at[n],
                recv_sem=d2d_recv.at[n], device_id=(x, y, 1 - c), device_id_type=MESH).wait_recv()
        for cp in sends:
            cp.wait_send()
        local.wait()

    return pl.pallas_call(
        body, name=name,
        in_specs=[_ANY_SPEC], out_specs=_ANY_SPEC,
        out_shape=jax.ShapeDtypeStruct((N_CHIPS, R, C), wp.dtype),
        scratch_shapes=[pltpu.SemaphoreType.DMA((3,)), pltpu.SemaphoreType.DMA((3,)), pltpu.SemaphoreType.DMA((3,)),
                        pltpu.SemaphoreType.DMA((3,)), pltpu.SemaphoreType.DMA],
    )(wp)


def _chip_copies(src_ref, land_ref, sems, gather):
    x, y, c = _place()
    me = 2 * x + y
    out, back = [], []
    for n, (px, py) in enumerate(_other_chips(x, y)):
        src = src_ref if gather else src_ref.at[2 * px + py]
        out.append(pltpu.make_async_remote_copy(
            src_ref=src, dst_ref=land_ref.at[me] if gather else land_ref.at[n],
            send_sem=sems[n], recv_sem=sems[3 + n], device_id=(px, py, c), device_id_type=MESH))
        back.append(pltpu.make_async_remote_copy(
            src_ref=src, dst_ref=land_ref.at[2 * px + py] if gather else land_ref.at[n],
            send_sem=sems[n], recv_sem=sems[3 + n], device_id=(px, py, c), device_id_type=MESH))
    return out, back


def _xchg_start(src, land, gather, order, name):
    def body(src_ref, land_ref, order_ref, *outs):
        sems = outs[0:6]
        token = outs[8]
        out, _ = _chip_copies(src_ref, land_ref, sems, gather)
        for cp in out:
            cp.start()
        token[...] = jnp.zeros_like(token)

    outs = pl.pallas_call(
        body, name=name,
        out_shape=(pltpu.SemaphoreType.DMA(()),) * 6 + (
            pltpu.HBM(src.shape, src.dtype), pltpu.HBM(land.shape, land.dtype),
            jax.ShapeDtypeStruct((8, LANE), F32)),
        in_specs=(_HBM_SPEC, _HBM_SPEC, _ANY_SPEC),
        out_specs=(_SEM_SPEC,) * 6 + (_HBM_SPEC, _HBM_SPEC, _VMEM_SPEC),
        input_output_aliases={0: 6, 1: 7},
        compiler_params=pltpu.CompilerParams(has_side_effects=_EFFECT),
    )(pltpu.with_memory_space_constraint(src, pltpu.HBM), pltpu.with_memory_space_constraint(land, pltpu.HBM), order)
    return outs[0:6], outs[6], outs[7], outs[8]


def _xchg_wait(started, gather, after, name):
    sems, src, land, _ = started

    def body(src_ref, land_ref, *rest):
        _, back = _chip_copies(src_ref, land_ref, rest[0:6], gather)
        for cp in back:
            cp.wait_send()
            cp.wait_recv()

    return pl.pallas_call(
        body, name=name,
        out_shape=(pltpu.HBM(src.shape, src.dtype), pltpu.HBM(land.shape, land.dtype)),
        in_specs=(_HBM_SPEC, _HBM_SPEC) + (_SEM_SPEC,) * 6 + (_ANY_SPEC,),
        out_specs=(_HBM_SPEC, _HBM_SPEC),
        input_output_aliases={0: 0, 1: 1},
        compiler_params=pltpu.CompilerParams(has_side_effects=_EFFECT),
    )(src, land, *sems, after)


def _sib_copy(src_ref, land_ref, send_sem, recv_sem):
    x, y, c = _place()
    return pltpu.make_async_remote_copy(src_ref=src_ref, dst_ref=land_ref, send_sem=send_sem, recv_sem=recv_sem,
                                        device_id=(x, y, 1 - c), device_id_type=MESH)


def _sib_start(src, name):
    land = lax.empty(src.shape, src.dtype)

    def body(src_ref, land_ref, send_sem, recv_sem, src_thru, land_thru, token):
        _sib_copy(src_ref, land_ref, send_sem, recv_sem).start()
        token[...] = jnp.zeros_like(token)

    return pl.pallas_call(
        body, name=name,
        out_shape=(pltpu.SemaphoreType.DMA(()), pltpu.SemaphoreType.DMA(()),
                   pltpu.HBM(src.shape, src.dtype), pltpu.HBM(land.shape, land.dtype),
                   jax.ShapeDtypeStruct((8, LANE), F32)),
        in_specs=(_HBM_SPEC, _HBM_SPEC),
        out_specs=(_SEM_SPEC, _SEM_SPEC, _HBM_SPEC, _HBM_SPEC, _VMEM_SPEC),
        input_output_aliases={0: 2, 1: 3},
        compiler_params=pltpu.CompilerParams(has_side_effects=_EFFECT),
    )(pltpu.with_memory_space_constraint(src, pltpu.HBM), pltpu.with_memory_space_constraint(land, pltpu.HBM))


def _sib_wait(started, after, name):
    send_sem, recv_sem, src, land, _ = started

    def body(src_ref, land_ref, send_sem, recv_sem, after_ref, src_out, land_out):
        cp = _sib_copy(src_ref, land_ref, send_sem, recv_sem)
        cp.wait_send()
        cp.wait_recv()

    return pl.pallas_call(
        body, name=name,
        out_shape=(pltpu.HBM(src.shape, src.dtype), pltpu.HBM(land.shape, land.dtype)),
        in_specs=(_HBM_SPEC, _HBM_SPEC, _SEM_SPEC, _SEM_SPEC, _ANY_SPEC),
        out_specs=(_HBM_SPEC, _HBM_SPEC),
        input_output_aliases={0: 0, 1: 1},
        compiler_params=pltpu.CompilerParams(has_side_effects=_EFFECT),
    )(src, land, send_sem, recv_sem, after)


def _sum_slabs(gp, recv, chip, name):
    _, R, C = gp.shape
    tr = _tile(R, PACK_ROWS, 16)

    def body(chip_ref, own_ref, r0_ref, r1_ref, r2_ref, o_ref):
        acc = own_ref[...].astype(F32) + r0_ref[...].astype(F32)
        o_ref[...] = (acc + r1_ref[...].astype(F32)) + r2_ref[...].astype(F32)

    def got(n):
        return pl.BlockSpec((None, tr, C), lambda i, chip_ref: (n, i, 0))

    return pl.pallas_call(
        body, name=name,
        grid_spec=pltpu.PrefetchScalarGridSpec(
            num_scalar_prefetch=1, grid=(R // tr,),
            in_specs=[pl.BlockSpec((None, tr, C), lambda i, chip_ref: (chip_ref[0], i, 0)), got(0), got(1), got(2)],
            out_specs=pl.BlockSpec((tr, C), lambda i, chip_ref: (i, 0))),
        out_shape=jax.ShapeDtypeStruct((R, C), F32),
        compiler_params=_params(("parallel",)),
    )(jnp.reshape(chip, (1,)).astype(jnp.int32), gp, recv, recv, recv)


def _all_reduce_vec(vec, name):
    VR, W = vec.shape

    def body(vec_ref, vall_ref, vout_ref, vsend_sems, vrecv_sems):
        x, y, c = _place()
        vall_ref[4 * x + 2 * y + c] = vec_ref[...]
        sends = []
        peers = []
        for r in range(1, N_DEV):
            dx, dy, dc = (r >> 2) & 1, (r >> 1) & 1, r & 1
            peer = (x ^ dx, y ^ dy, c ^ dc)
            peers.append(peer)
            cp = pltpu.make_async_remote_copy(
                src_ref=vec_ref, dst_ref=vall_ref.at[4 * x + 2 * y + c], send_sem=vsend_sems.at[r - 1],
                recv_sem=vrecv_sems.at[r - 1], device_id=peer, device_id_type=MESH)
            cp.start()
            sends.append(cp)
        for r, peer in enumerate(peers):
            pltpu.make_async_remote_copy(
                src_ref=vec_ref, dst_ref=vall_ref.at[4 * peer[0] + 2 * peer[1] + peer[2]],
                send_sem=vsend_sems.at[r], recv_sem=vrecv_sems.at[r],
                device_id=peer, device_id_type=MESH).wait_recv()
        total = vall_ref[0]
        for d in range(1, N_DEV):
            total = total + vall_ref[d]
        vout_ref[...] = total
        for cp in sends:
            cp.wait_send()

    outs = pl.pallas_call(
        body, name=name,
        in_specs=[_VMEM_SPEC], out_specs=[_VMEM_SPEC, _VMEM_SPEC],
        out_shape=[jax.ShapeDtypeStruct((N_DEV, VR, W), F32), jax.ShapeDtypeStruct((VR, W), F32)],
        scratch_shapes=[pltpu.SemaphoreType.DMA((N_DEV - 1,)), pltpu.SemaphoreType.DMA((N_DEV - 1,))],
    )(vec)
    return outs[1]


class _Pack:
    def __init__(self, group, C):
        self.group, self.C = group, C
        self.rows, self.offs, off = {}, {}, 0
        for nm, (K, N), _ in group:
            assert N <= C, nm
            self.rows[nm] = K if 2 * N > C else -(-(K * N) // C)
            self.offs[nm] = off
            off += -(-self.rows[nm] // 16) * 16
        self.used = off
        self.R = -(-off // PACK_ROWS) * PACK_ROWS

    def _rows_of(self, a):
        K, N = a.shape
        if 2 * N > self.C:
            a = jnp.pad(a, ((0, 0), (0, self.C - N)))
        else:
            a = jnp.pad(a.reshape(-1), (0, -(K * N) % self.C)).reshape(-1, self.C)
        return jnp.pad(a, ((0, -a.shape[0] % 16), (0, 0)))

    def pack(self, shards):
        parts = [self._rows_of(shards[nm].astype(BF16)) for nm, _, _ in self.group]
        return jnp.concatenate(parts + [jnp.zeros((self.R - self.used, self.C), BF16)], axis=0)

    def part(self, flat, nm, shape):
        K, N = shape
        rows = flat[self.offs[nm]:self.offs[nm] + self.rows[nm]]
        return rows[:, :N] if 2 * N > self.C else rows.reshape(-1)[:K * N].reshape(K, N)

    def slabs(self, grads):
        out = []
        for k in range(N_CHIPS):
            cut = {}
            for nm, (K, N), axis in self.group:
                g = grads[nm]
                cut[nm] = g[:, k * N:(k + 1) * N] if axis == 1 else g[k * K:(k + 1) * K, :]
            out.append(self.pack(cut))
        return jnp.stack(out)

    def full(self, gathered):
        res = {}
        for nm, (K, N), axis in self.group:
            parts = [self.part(gathered[k], nm, (K, N)) for k in range(N_CHIPS)]
            res[nm] = jnp.concatenate(parts, axis=axis)
        return res


def _rope_tables(S):
    pos = jnp.arange(S, dtype=F32)
    inv = 1.0 / (ROPE_THETA ** (jnp.arange(0, MLA_ROPE, 2, dtype=F32) / MLA_ROPE))
    ang = pos[:, None] * inv[None, :]
    cos, sin = jnp.cos(ang), jnp.sin(ang)
    half = MLA_ROPE // 2
    z = jnp.zeros((S, half), F32)
    one = jnp.ones((S, LANE - MLA_ROPE), F32)
    zero = jnp.zeros((S, LANE - MLA_ROPE), F32)
    kc = jnp.concatenate([cos, cos, one], axis=1)
    ksa = jnp.concatenate([-sin, z, zero], axis=1)
    ksb = jnp.concatenate([z, sin, zero], axis=1)
    qc = jnp.concatenate([jnp.ones((S, MLA_NOPE), F32), kc], axis=1)
    qsa = jnp.concatenate([jnp.zeros((S, MLA_NOPE), F32), ksa], axis=1)
    qsb = jnp.concatenate([jnp.zeros((S, MLA_NOPE), F32), ksb], axis=1)
    return (kc, ksa, ksb), (qc, qsa, qsb)


def _pad_cols(a, width):
    return jnp.pad(a, ((0, 0), (0, width - a.shape[1])))


def kernel(x, attn_norm, w_in, fox_f_bias, q_norm, w_uq, kv_norm, w_ukv, w_mla_branch, w_fox_branch, w_out, mlp_norm, w_up, w_down, final_norm, loss_target, m_attn_norm, m_w_in, m_fox_f_bias, m_q_norm, m_w_uq, m_kv_norm, m_w_ukv, m_w_mla_branch, m_w_fox_branch, m_w_out, m_mlp_norm, m_w_up, m_w_down, m_final_norm, v_attn_norm, v_w_in, v_fox_f_bias, v_q_norm, v_w_uq, v_kv_norm, v_w_ukv, v_w_mla_branch, v_w_fox_branch, v_w_out, v_mlp_norm, v_w_up, v_w_down, v_final_norm):
    _, S, D = x.shape
    H, HF = MLA_HEADS, FOX_HEADS
    QL, KVL = MLA_Q_LORA, MLA_KV_LORA
    assert H == HF and H <= 8
    xs = x[0]
    target = loss_target[0]
    C = D
    chip = 2 * lax.axis_index("x") + lax.axis_index("y")

    def flip(a):
        return jnp.transpose(a, (0, 2, 1))

    w_in, m_w_in, v_w_in = flip(w_in), flip(m_w_in), flip(v_w_in)
    weights = {"attn_norm": attn_norm, "w_in": w_in, "fox_f_bias": fox_f_bias, "q_norm": q_norm, "w_uq": w_uq,
               "kv_norm": kv_norm, "w_ukv": w_ukv, "w_mla_branch": w_mla_branch, "w_fox_branch": w_fox_branch,
               "w_out": w_out, "mlp_norm": mlp_norm, "w_up": w_up, "w_down": w_down, "final_norm": final_norm}
    moments = {"attn_norm": (m_attn_norm, v_attn_norm), "w_in": (m_w_in, v_w_in), "fox_f_bias": (m_fox_f_bias, v_fox_f_bias),
               "q_norm": (m_q_norm, v_q_norm), "w_uq": (m_w_uq, v_w_uq), "kv_norm": (m_kv_norm, v_kv_norm),
               "w_ukv": (m_w_ukv, v_w_ukv), "w_mla_branch": (m_w_mla_branch, v_w_mla_branch),
               "w_fox_branch": (m_w_fox_branch, v_w_fox_branch), "w_out": (m_w_out, v_w_out),
               "mlp_norm": (m_mlp_norm, v_mlp_norm), "w_up": (m_w_up, v_w_up), "w_down": (m_w_down, v_w_down),
               "final_norm": (m_final_norm, v_final_norm)}

    def group(names_axes):
        return [(nm, weights[nm].shape[1:], axis) for nm, axis in names_axes]

    pack_a = _Pack(group([("w_in", 0), ("w_uq", 1), ("w_ukv", 1)]), C)
    pack_b = _Pack(group([("w_mla_branch", 1), ("w_fox_branch", 1), ("w_out", 0), ("w_up", 1), ("w_down", 0)]), C)
    RA, RB = pack_a.R, pack_b.R
    wp_a = pack_a.pack({nm: weights[nm][0] for nm, _, _ in pack_a.group})
    wp_b = pack_b.pack({nm: weights[nm][0] for nm, _, _ in pack_b.group})
    gathered_a = _all_gather_halves(wp_a, "all_gather_a")
    ag_b = _xchg_start(wp_b, lax.empty((N_CHIPS, RB, C), BF16), True, gathered_a, "all_gather_start_b")
    xn = _norm_fwd(xs, attn_norm, "attn_norm_fwd", order=ag_b[3])
    full = pack_a.full(gathered_a)

    o_ckv = QL
    o_kr = o_ckv + KVL
    o_fq = o_kr + MLA_ROPE
    o_ff = o_fq + 3 * HF * FOX_HEAD_DIM
    o_g = o_ff + HF
    wi = full["w_in"]
    assert wi.shape[0] == o_g + 2 * D
    WS = QL + KVL + 2 * LANE
    NQKV = 3 * HF * FOX_HEAD_DIM

    def pad_rows(a, rows):
        return jnp.pad(a, ((0, rows - a.shape[0]), (0, 0)))

    w_small = jnp.concatenate([wi[:o_kr], pad_rows(wi[o_kr:o_fq], LANE), pad_rows(wi[o_ff:o_g], LANE)], axis=0)
    w_qkv = wi[o_fq:o_ff]
    w_g = wi[o_g:]
    w_pack = jnp.concatenate([w_small, w_qkv, w_g], axis=0)
    dqk = MLA_NOPE + MLA_ROPE
    w_uq_p = jnp.pad(full["w_uq"].reshape(QL, H, dqk), ((0, 0), (0, 0), (0, QPAD - dqk))).reshape(QL, H * QPAD)
    ukv = full["w_ukv"].reshape(KVL, H, MLA_NOPE + MLA_V)
    w_ukv_p = jnp.concatenate([ukv[:, :, :MLA_NOPE].reshape(KVL, H * MLA_NOPE),
                               ukv[:, :, MLA_NOPE:].reshape(KVL, H * MLA_V)], axis=1)

    (kc, ksa, ksb), (qc, qsa, qsb) = _rope_tables(S)
    bias_pad = _pad_cols(fox_f_bias, LANE)

    small = _matmul(xn, w_small, "nt", [F32], "proj_small")
    n_fq = HF * FOX_HEAD_DIM
    q_scale = jnp.concatenate([jnp.full((1, n_fq), LOG2E / math.sqrt(FOX_HEAD_DIM), F32),
                               jnp.ones((1, NQKV - n_fq), F32)], axis=1)
    qkv = _matmul(xn, w_qkv, "nt", [BF16], "proj_qkv", col_extras=(q_scale,), epilogue=lambda acc, cs: (acc * cs,))
    gpre = _matmul(xn, w_g, "nt", [F32], "proj_gates")
    cqn, ckvn, kr, cum = _prep_fwd(small, q_norm, kv_norm, bias_pad, kc, ksa, ksb, HF, "prep_fwd")
    c2_mla = LOG2E / math.sqrt(dqk)
    q_rot = _matmul(cqn, w_uq_p, "nn", [BF16], "mla_q_up", tn=QPAD, row_extras=(qc * c2_mla, qsa * c2_mla, qsb * c2_mla),
                    epilogue=lambda acc, c, sa, sb: (_rope(acc, c, sa, sb, 1),))
    kv2 = _matmul(ckvn, w_ukv_p, "nn", [BF16], "mla_kv_up")

    mla = _AttT(S, H, (q_rot, QPAD, 0, True), [(kv2, MLA_NOPE, 0, True), (kr, LANE, 0, False)],
                (kv2, MLA_V, H, True), 1.0 / math.sqrt(dqk), True)
    o_mla, lse_mla = _att_fwd_t(mla, "mla_att_fwd")

    cum_t = jnp.transpose(cum[:, :HF]) * LOG2E
    cum_rep = jnp.broadcast_to(cum_t[:, :, None], (HF, S, min(QSUB, _tile(S, ATT_T))))
    fox = _AttT(S, HF, (qkv, FOX_HEAD_DIM, 0, True), [(qkv, FOX_HEAD_DIM, HF, True)],
                (qkv, FOX_HEAD_DIM, 2 * HF, True), 1.0 / math.sqrt(FOX_HEAD_DIM), False, cum_rep)
    o_fox, ox_fox, lse_fox = _att_fwd_t(fox, "fox_att_fwd", exact=True)

    own_b, land_b = _xchg_wait(ag_b, True, lse_fox, "all_gather_wait_b")
    full.update(pack_b.full(lax.dynamic_update_slice(land_b, own_b[None], (chip, 0, 0))))
    w_mb, w_fb, w_o, w_u, w_d = (full[n] for n in ("w_mla_branch", "w_fox_branch", "w_out", "w_up", "w_down"))

    y_mla = _matmul(o_mla, w_mb, "nn", [F32], "mla_branch")
    y_fox = _matmul(o_fox, w_fb, "nn", [F32], "fox_branch")
    merged = _gate_fwd(gpre, y_mla, y_fox, "gate_fwd")
    h1 = _matmul(merged, w_o, "nn", [F32], "out_proj", extras=(xs,), epilogue=lambda acc, r: (acc + r,))
    hn = _norm_fwd(h1, mlp_norm, "mlp_norm_fwd")

    def relu2(acc):
        a = jnp.maximum(acc, 0.0)
        return a * a, a

    u, a_pos = _matmul(hn, w_u, "nn", [BF16, BF16], "mlp_up", epilogue=relu2)
    h2 = _matmul(u, w_d, "nn", [F32], "mlp_down", tn=1024, extras=(h1,), epilogue=lambda acc, r: (acc + r,))
    dh2, dh2_b, g_final, loss_part = _final(h2, final_norm.reshape(1, D), target, "final_norm_loss")

    da = _matmul(dh2_b, w_d, "nt", [BF16], "mlp_down_dx", extras=(a_pos,),
                 epilogue=lambda acc, a: (acc * (2.0 * a.astype(F32)),))
    g_w_down = _mm_tn(u, dh2_b, "mlp_down_dw")
    dhn = _matmul(da, w_u, "nt", [F32], "mlp_up_dx", tn=1024)
    g_w_up = _mm_tn(hn, da, "mlp_up_dw")
    dh1, dh1_b, g_mlp_norm = _norm_bwd(h1, dhn, mlp_norm, dh2, "mlp_norm_bwd")
    dmerged = _matmul(dh1_b, w_o, "nt", [F32], "out_proj_dx")
    g_w_out = _mm_tn(merged, dh1_b, "out_proj_dw")
    dy_mla, dy_fox, dg_mla, dg_fox = _gate_bwd(dmerged, gpre, y_mla, y_fox, "gate_bwd")
    do_mla = _matmul(dy_mla, w_mb, "nt", [BF16], "mla_branch_dx")
    g_w_mb = _mm_tn(o_mla, dy_mla, "mla_branch_dw")
    do_fox = _matmul(dy_fox, w_fb, "nt", [BF16], "fox_branch_dx")
    g_w_fb = _mm_tn(o_fox, dy_fox, "fox_branch_dw")

    gp_b = pack_b.slabs({"w_mla_branch": g_w_mb, "w_fox_branch": g_w_fb, "w_out": g_w_out,
                         "w_up": g_w_up, "w_down": g_w_down})
    rs_b = _xchg_start(gp_b, lax.empty((3, RB, C), BF16), False, g_w_fb, "grad_scatter_start_b")

    delta_mla = _att_delta_t(do_mla, o_mla, H, "mla_att_delta", order=rs_b[3])
    dq_rot, dk_nope, dkr_heads, dv_mla = _att_bwd_t(mla, do_mla, lse_mla, delta_mla, BF16, [BF16, F32],
                                                    "mla_att_bwd", dq_rope=(qc, qsa, qsb))
    delta_fox = _att_delta_t(do_fox, ox_fox, HF, "fox_att_delta")
    dfq, dfk, dfv, dcum = _att_bwd_t(fox, do_fox, lse_fox, delta_fox, BF16, [BF16], "fox_att_bwd")

    gp_b_sent, recv_b = _xchg_wait(rs_b, False, dfq, "grad_scatter_wait_b")
    swap_b = _sib_start(_sum_slabs(gp_b_sent, recv_b, chip, "grad_sum_b"), "grad_swap_start_b")

    dcqn = _matmul(dq_rot, w_uq_p, "nt", [F32], "mla_q_up_dx", order=swap_b[4])
    g_w_uq_p = _mm_tn(cqn, dq_rot, "mla_q_up_dw")
    dkv2 = jnp.concatenate([dk_nope, dv_mla], axis=1)
    dckvn = _matmul(dkv2, w_ukv_p, "nt", [F32], "mla_kv_up_dx")
    g_w_ukv_p = _mm_tn(ckvn, dkv2, "mla_kv_up_dw")

    dcum_rows = jnp.pad(dcum[:, :, 0], ((0, 8 - HF), (0, 0)))
    dlogf_rows = _suffix_sum_rows(dcum_rows, "fox_forget_suffix_sum")
    dlogf = _pad_cols(jnp.transpose(dlogf_rows[:HF]), LANE)
    d_small, g_q_norm, g_kv_norm, g_bias = _prep_bwd(
        small, dcqn, dckvn, dkr_heads, dlogf, q_norm, kv_norm, bias_pad, kc, ksa, ksb, H, "prep_bwd")
    dproj = jnp.concatenate([d_small, dfq, dfk, dfv, dg_mla, dg_fox], axis=1)
    g_w_pack = _mm_tn(dproj, xn, "proj_dw")

    gs, gq, gg = g_w_pack[:WS], g_w_pack[WS:WS + NQKV], g_w_pack[WS + NQKV:]
    g_w_in = jnp.concatenate([gs[:o_kr], gs[o_kr:o_kr + MLA_ROPE], gq,
                              gs[o_kr + LANE:o_kr + LANE + HF], gg], axis=0)
    g_w_uq = g_w_uq_p.reshape(QL, H, QPAD)[:, :, :dqk].reshape(QL, H * dqk)
    g_w_ukv = jnp.concatenate([g_w_ukv_p[:, :H * MLA_NOPE].reshape(KVL, H, MLA_NOPE),
                               g_w_ukv_p[:, H * MLA_NOPE:].reshape(KVL, H, MLA_V)], axis=2).reshape(KVL, -1)

    gp_a = pack_a.slabs({"w_in": g_w_in, "w_uq": g_w_uq, "w_ukv": g_w_ukv})
    rs_a = _xchg_start(gp_a, lax.empty((3, RA, C), BF16), False, g_w_pack, "grad_scatter_start_a")
    dxn = _matmul(dproj, w_pack, "nn", [F32], "proj_dx", tn=1024, order=rs_a[3])
    grad_x, _, g_attn_norm = _norm_bwd(xs, dxn, attn_norm, dh1, "attn_norm_bwd")
    gp_a_sent, recv_a = _xchg_wait(rs_a, False, grad_x, "grad_scatter_wait_a")
    swap_a = _sib_start(_sum_slabs(gp_a_sent, recv_a, chip, "grad_sum_a"), "grad_swap_start_a")
    vec_w = max(D, LANE)
    vec_rows = [g_attn_norm, g_mlp_norm, g_final, g_q_norm, g_kv_norm, g_bias, loss_part]
    vec = jnp.concatenate([_pad_cols(v, vec_w) for v in vec_rows] + [jnp.zeros((1, vec_w), F32)], axis=0)
    vsum = _all_reduce_vec(vec, "all_reduce_vectors")
    part_b, sib_b = _sib_wait(swap_b, vsum, "grad_swap_wait_b")

    grads, deltas, new_m, new_v = {}, {}, {}, {}

    def update(pack, mine, theirs):
        for nm, shape, _ in pack.group:
            g, d, nm_, nv_ = _adamw(weights[nm], pack.part(mine, nm, shape), pack.part(theirs, nm, shape),
                                    moments[nm][0], moments[nm][1], "adamw_" + nm)
            grads[nm], deltas[nm], new_m[nm], new_v[nm] = g, d, nm_, nv_
        return g

    last_b = update(pack_b, part_b, sib_b)
    part_a, sib_a = _sib_wait(swap_a, last_b, "grad_swap_wait_a")
    update(pack_a, part_a, sib_a)

    vec_names = ["attn_norm", "mlp_norm", "final_norm", "q_norm", "kv_norm", "fox_f_bias"]

    def vec_pack(arrs):
        return jnp.concatenate([_pad_cols(a.reshape(1, -1), vec_w) for a in arrs]
                               + [jnp.zeros((2, vec_w), F32)], axis=0)[None]

    vg, vd, vm, vv = _adamw(vec_pack([weights[n] for n in vec_names]), vsum, jnp.zeros_like(vsum),
                            vec_pack([moments[n][0] for n in vec_names]), vec_pack([moments[n][1] for n in vec_names]),
                            "adamw_vectors")
    for r, nm in enumerate(vec_names):
        shp = weights[nm].shape
        n = weights[nm].size
        grads[nm] = vsum[r, :n].reshape(shp)
        deltas[nm], new_m[nm], new_v[nm] = (vd[0, r, :n].reshape(shp), vm[0, r, :n].reshape(shp),
                                            vv[0, r, :n].reshape(shp))
    loss = vsum[6, 0]

    for res in (grads, deltas, new_m, new_v):
        res["w_in"] = flip(res["w_in"])
    order = ["attn_norm", "w_in", "fox_f_bias", "q_norm", "w_uq", "kv_norm", "w_ukv", "w_mla_branch", "w_fox_branch",
             "w_out", "mlp_norm", "w_up", "w_down", "final_norm"]
    return (loss, grad_x[None], *[grads[n] for n in order], *[deltas[n] for n in order],
            *[new_m[n] for n in order], *[new_v[n] for n in order])
```

```python
import math

import jax
import jax.numpy as jnp
from jax import lax
from jax.experimental import pallas as pl
from jax.experimental.pallas import tpu as pltpu

CHUNK = 64
MLA_HEADS = 8
MLA_Q_LORA = 512
MLA_KV_LORA = 256
MLA_NOPE = 128
MLA_ROPE = 64
MLA_V = 128
ROPE_THETA = 10000.0
FOX_HEADS = 8
FOX_HEAD_DIM = 128
EPS = 1e-6

ADAM_LR = 0.001
ADAM_B1 = 0.9
ADAM_B2 = 0.999
ADAM_EPS = 1e-08
ADAM_WD = 0.01
ADAM_STEP = 10

LANE = 128
QPAD = 2 * LANE
N_CHIPS = 4
N_DEV = 8
VMEM_LIMIT = 48 * 1024 * 1024
ATT_T = 1024
QSUB = 256
ROW_T = 256
PACK_ROWS = 256
LOG2E = 1.4426950408889634

BF16 = jnp.bfloat16
F32 = jnp.float32
MESH = pl.DeviceIdType.MESH

_NT = (((1,), (1,)), ((), ()))
_TN = (((0,), (0,)), ((), ()))
_NN = (((1,), (0,)), ((), ()))


def _tile(dim, pref, align=LANE):
    if dim <= pref:
        return dim
    t = (pref // align) * align
    while t >= align:
        if dim % t == 0:
            return t
        t -= align
    return dim


def _params(sem=None):
    return pltpu.CompilerParams(dimension_semantics=sem, vmem_limit_bytes=VMEM_LIMIT)


_ANY_SPEC = pl.BlockSpec(memory_space=pl.ANY)


def _matmul(a, b, mode, out_dtypes, name, *, tm=1024, tn=512, tk=2048, extras=(), row_extras=(), col_extras=(),
            epilogue=None, order=None):
    if mode == "nn":
        (M, K), (K2, N) = a.shape, b.shape
    elif mode == "nt":
        (M, K), (N, K2) = a.shape, b.shape
    else:
        (K, M), (K2, N) = a.shape, b.shape
    assert K == K2, (name, a.shape, b.shape)
    tm, tn, tk = _tile(M, tm), _tile(N, tn), _tile(K, tk)
    nk = K // tk
    extras = [e if isinstance(e, tuple) else (e, 0) for e in extras]
    n_out = len(out_dtypes)
    n_ex = len(extras) + len(row_extras) + len(col_extras)
    n_ord = 0 if order is None else 1
    assert all(r.shape == (M, tn) for r in row_extras), name
    dims = {"nn": _NN, "nt": _NT, "tn": _TN}[mode]

    def body(*refs):
        a_ref, b_ref = refs[0], refs[1]
        ex_refs = refs[2:2 + n_ex]
        o_refs = refs[2 + n_ex + n_ord:2 + n_ex + n_ord + n_out]
        acc_ref = refs[2 + n_ex + n_ord + n_out]
        k = pl.program_id(2)
        part = lax.dot_general(a_ref[...], b_ref[...], dims, preferred_element_type=F32)

        @pl.when(k == 0)
        def _():
            acc_ref[...] = part

        @pl.when(k > 0)
        def _():
            acc_ref[...] += part

        @pl.when(k == nk - 1)
        def _():
            acc = acc_ref[...]
            if epilogue is None:
                outs = (acc,)
            else:
                outs = epilogue(acc, *[r[...] for r in ex_refs])
            for o_ref, o in zip(o_refs, outs):
                o_ref[...] = o.astype(o_ref.dtype)

    if mode == "nn":
        a_spec = pl.BlockSpec((tm, tk), lambda i, j, k: (i, k))
        b_spec = pl.BlockSpec((tk, tn), lambda i, j, k: (k, j))
    elif mode == "nt":
        a_spec = pl.BlockSpec((tm, tk), lambda i, j, k: (i, k))
        b_spec = pl.BlockSpec((tn, tk), lambda i, j, k: (j, k))
    else:
        a_spec = pl.BlockSpec((tk, tm), lambda i, j, k: (k, i))
        b_spec = pl.BlockSpec((tk, tn), lambda i, j, k: (k, j))
    mn_spec = pl.BlockSpec((tm, tn), lambda i, j, k: (i, j))
    row_spec = pl.BlockSpec((tm, tn), lambda i, j, k: (i, 0))
    col_spec = pl.BlockSpec((1, tn), lambda i, j, k: (0, j))
    outs = pl.pallas_call(
        body,
        name=name,
        grid=(M // tm, N // tn, nk),
        in_specs=([a_spec, b_spec]
                  + [pl.BlockSpec((tm, tn), lambda i, j, k, g=g: (i, j + g * (N // tn))) for _, g in extras]
                  + [row_spec] * len(row_extras) + [col_spec] * len(col_extras) + [_ANY_SPEC] * n_ord),
        out_specs=[mn_spec] * n_out,
        out_shape=[jax.ShapeDtypeStruct((M, N), dt) for dt in out_dtypes],
        scratch_shapes=[pltpu.VMEM((tm, tn), F32)],
        compiler_params=_params(("parallel", "parallel", "arbitrary")),
    )(a, b, *[e for e, _ in extras], *row_extras, *col_extras, *([] if order is None else [order]))
    return outs[0] if n_out == 1 else outs


def _mm_tn(a, b, name):
    return _matmul(a, b, "tn", [F32], name, tm=1024, tn=1024, tk=2048)


def _row_spec(ts, width, col=0):
    return pl.BlockSpec((ts, width), lambda i: (i, col))


def _full_spec(shape):
    return pl.BlockSpec(shape, lambda i: tuple(0 for _ in shape))


def _rms(x):
    return lax.rsqrt(jnp.mean(x * x, axis=-1, keepdims=True) + EPS)


def _rms_bwd(x, dy, g):
    r = _rms(x)
    xh = x * r
    gy = dy * g
    dx = r * (gy - xh * jnp.mean(xh * gy, axis=-1, keepdims=True))
    return dx, dy * xh


def _norm_fwd(x, g, name, order=None):
    S, D = x.shape
    ts = _tile(S, ROW_T, 8)

    def body(x_ref, g_ref, *rest):
        o_ref = rest[-1]
        xv = x_ref[...]
        o_ref[...] = ((xv * _rms(xv)) * g_ref[...]).astype(BF16)

    extra = [] if order is None else [order]
    return pl.pallas_call(
        body, name=name, grid=(S // ts,),
        in_specs=[_row_spec(ts, D), _full_spec((1, D))] + [_ANY_SPEC] * len(extra),
        out_specs=_row_spec(ts, D),
        out_shape=jax.ShapeDtypeStruct((S, D), BF16),
        compiler_params=_params(("parallel",)),
    )(x, g, *extra)


def _norm_bwd(x, dy, g, dres, name):
    S, D = x.shape
    ts = _tile(S, ROW_T, 8)

    def body(x_ref, dy_ref, g_ref, dres_ref, dx_ref, dxb_ref, dg_ref):
        dx, dg_rows = _rms_bwd(x_ref[...], dy_ref[...], g_ref[...])
        dx = dres_ref[...] + dx
        dx_ref[...] = dx
        dxb_ref[...] = dx.astype(BF16)

        @pl.when(pl.program_id(0) == 0)
        def _():
            dg_ref[...] = jnp.zeros_like(dg_ref)

        dg_ref[...] += jnp.sum(dg_rows, axis=0, keepdims=True)

    return pl.pallas_call(
        body, name=name, grid=(S // ts,),
        in_specs=[_row_spec(ts, D), _row_spec(ts, D), _full_spec((1, D)), _row_spec(ts, D)],
        out_specs=[_row_spec(ts, D), _row_spec(ts, D), _full_spec((1, D))],
        out_shape=[jax.ShapeDtypeStruct((S, D), F32), jax.ShapeDtypeStruct((S, D), BF16),
                   jax.ShapeDtypeStruct((1, D), F32)],
        compiler_params=_params(("arbitrary",)),
    )(x, dy, g, dres)


def _rope(x, c, sa, sb, sign):
    w = x.shape[-1]
    half = MLA_ROPE // 2
    fwd = pltpu.roll(x, w - half, 1)
    back = pltpu.roll(x, half, 1)
    if sign < 0:
        return x * c - fwd * sa - back * sb
    return x * c + fwd * sa + back * sb


def _split3(x):
    hi = x.astype(BF16)
    r1 = x - hi.astype(F32)
    mid = r1.astype(BF16)
    lo = (r1 - mid.astype(F32)).astype(BF16)
    return hi, mid, lo


def _prep_fwd(small, q_norm, kv_norm, bias_pad, kc, ksa, ksb, n_heads, name):
    S, W = small.shape
    QL, KVL = q_norm.shape[1], kv_norm.shape[1]
    assert W == QL + KVL + 2 * LANE
    ts = _tile(S, ROW_T, 8)
    tri = (lax.broadcasted_iota(jnp.int32, (ts, ts), 0) >= lax.broadcasted_iota(jnp.int32, (ts, ts), 1)).astype(BF16)

    def body(s_ref, qn_ref, kvn_ref, b_ref, kc_ref, ksa_ref, ksb_ref, tri_ref,
             cqn_ref, ckvn_ref, kr_ref, cum_ref, carry_ref):
        cq = s_ref[:, 0:QL]
        cqn_ref[...] = ((cq * _rms(cq)) * qn_ref[...]).astype(BF16)
        ckv = s_ref[:, QL:QL + KVL]
        ckvn_ref[...] = ((ckv * _rms(ckv)) * kvn_ref[...]).astype(BF16)
        kr = s_ref[:, QL + KVL:QL + KVL + LANE]
        kr_ref[...] = _rope(kr, kc_ref[...], ksa_ref[...], ksb_ref[...], 1).astype(BF16)
        z = s_ref[:, QL + KVL + LANE:W] + b_ref[...]
        logf = jnp.minimum(z, 0.0) - jnp.log1p(jnp.exp(-jnp.abs(z)))
        lane = lax.broadcasted_iota(jnp.int32, logf.shape, 1)
        logf = jnp.where(lane < n_heads, logf, 0.0)

        @pl.when(pl.program_id(0) == 0)
        def _():
            carry_ref[...] = jnp.zeros_like(carry_ref)

        t = tri_ref[...]
        cum = carry_ref[...]
        for part in _split3(logf):
            cum = cum + jnp.dot(t, part, preferred_element_type=F32)
        cum_ref[...] = cum
        carry_ref[...] = cum[ts - 1:ts, :]

    return pl.pallas_call(
        body, name=name, grid=(S // ts,),
        in_specs=[_row_spec(ts, W), _full_spec((1, QL)), _full_spec((1, KVL)), _full_spec((1, LANE)),
                  _row_spec(ts, LANE), _row_spec(ts, LANE), _row_spec(ts, LANE), _full_spec((ts, ts))],
        out_specs=[_row_spec(ts, QL), _row_spec(ts, KVL), _row_spec(ts, LANE), _row_spec(ts, LANE)],
        out_shape=[jax.ShapeDtypeStruct((S, QL), BF16), jax.ShapeDtypeStruct((S, KVL), BF16),
                   jax.ShapeDtypeStruct((S, LANE), BF16), jax.ShapeDtypeStruct((S, LANE), F32)],
        scratch_shapes=[pltpu.VMEM((1, LANE), F32)],
        compiler_params=_params(("arbitrary",)),
    )(small, q_norm, kv_norm, bias_pad, kc, ksa, ksb, tri)


def _prep_bwd(small, dcqn, dckvn, dkr_heads, dlogf, q_norm, kv_norm, bias_pad, kc, ksa, ksb, n_heads, name):
    S, W = small.shape
    QL, KVL = q_norm.shape[1], kv_norm.shape[1]
    ts = _tile(S, ROW_T, 8)

    def body(s_ref, dcq_ref, dckv_ref, dkr_ref, dlf_ref, qn_ref, kvn_ref, b_ref, kc_ref, ksa_ref, ksb_ref,
             ds_ref, gq_ref, gkv_ref, gb_ref):
        dcq, gq_rows = _rms_bwd(s_ref[:, 0:QL], dcq_ref[...], qn_ref[...])
        ds_ref[:, 0:QL] = dcq.astype(BF16)
        dckv, gkv_rows = _rms_bwd(s_ref[:, QL:QL + KVL], dckv_ref[...], kvn_ref[...])
        ds_ref[:, QL:QL + KVL] = dckv.astype(BF16)
        dkr = dkr_ref[:, 0:LANE]
        for h in range(1, n_heads):
            dkr = dkr + dkr_ref[:, h * LANE:(h + 1) * LANE]
        ds_ref[:, QL + KVL:QL + KVL + LANE] = _rope(dkr, kc_ref[...], ksa_ref[...], ksb_ref[...], -1).astype(BF16)
        z = s_ref[:, QL + KVL + LANE:W] + b_ref[...]
        dff = dlf_ref[...] * (1.0 / (1.0 + jnp.exp(z)))
        ds_ref[:, QL + KVL + LANE:W] = dff.astype(BF16)

        @pl.when(pl.program_id(0) == 0)
        def _():
            gq_ref[...] = jnp.zeros_like(gq_ref)
            gkv_ref[...] = jnp.zeros_like(gkv_ref)
            gb_ref[...] = jnp.zeros_like(gb_ref)

        gq_ref[...] += jnp.sum(gq_rows, axis=0, keepdims=True)
        gkv_ref[...] += jnp.sum(gkv_rows, axis=0, keepdims=True)
        gb_ref[...] += jnp.sum(dff, axis=0, keepdims=True)

    return pl.pallas_call(
        body, name=name, grid=(S // ts,),
        in_specs=[_row_spec(ts, W), _row_spec(ts, QL), _row_spec(ts, KVL), _row_spec(ts, n_heads * LANE),
                  _row_spec(ts, LANE), _full_spec((1, QL)), _full_spec((1, KVL)), _full_spec((1, LANE)),
                  _row_spec(ts, LANE), _row_spec(ts, LANE), _row_spec(ts, LANE)],
        out_specs=[_row_spec(ts, W), _full_spec((1, QL)), _full_spec((1, KVL)), _full_spec((1, LANE))],
        out_shape=[jax.ShapeDtypeStruct((S, W), BF16), jax.ShapeDtypeStruct((1, QL), F32),
                   jax.ShapeDtypeStruct((1, KVL), F32), jax.ShapeDtypeStruct((1, LANE), F32)],
        compiler_params=_params(("arbitrary",)),
    )(small, dcqn, dckvn, dkr_heads, dlogf, q_norm, kv_norm, bias_pad, kc, ksa, ksb)


def _sigmoid(z):
    return 1.0 / (1.0 + jnp.exp(-z))


def _final(h, g, target, name):
    S, D = h.shape
    ts = _tile(S, ROW_T, 8)

    def body(h_ref, g_ref, t_ref, dh_ref, dhb_ref, dg_ref, loss_ref):
        hv = h_ref[...]
        gv = g_ref[...]
        err = (hv * _rms(hv)) * gv - t_ref[...]
        dh, dg_rows = _rms_bwd(hv, err / D, gv)
        dh_ref[...] = dh
        dhb_ref[...] = dh.astype(BF16)

        @pl.when(pl.program_id(0) == 0)
        def _():
            dg_ref[...] = jnp.zeros_like(dg_ref)
            loss_ref[...] = jnp.zeros_like(loss_ref)

        dg_ref[...] += jnp.sum(dg_rows, axis=0, keepdims=True)
        row_loss = jnp.mean(err * err, axis=-1, keepdims=True)
        loss_ref[...] += 0.5 * jnp.sum(row_loss, axis=0, keepdims=True)

    return pl.pallas_call(
        body, name=name, grid=(S // ts,),
        in_specs=[_row_spec(ts, D), _full_spec((1, D)), _row_spec(ts, D)],
        out_specs=[_row_spec(ts, D), _row_spec(ts, D), _full_spec((1, D)), _full_spec((1, LANE))],
        out_shape=[jax.ShapeDtypeStruct((S, D), F32), jax.ShapeDtypeStruct((S, D), BF16),
                   jax.ShapeDtypeStruct((1, D), F32), jax.ShapeDtypeStruct((1, LANE), F32)],
        compiler_params=_params(("arbitrary",)),
    )(h, g, target)


def _suffix_sum_rows(x, name):
    R, S = x.shape
    tb = _tile(S, 512)
    nb = S // tb
    tri = (lax.broadcasted_iota(jnp.int32, (tb, tb), 0) >= lax.broadcasted_iota(jnp.int32, (tb, tb), 1)).astype(BF16)

    def body(x_ref, tri_ref, o_ref, carry_ref):
        @pl.when(pl.program_id(0) == 0)
        def _():
            carry_ref[...] = jnp.zeros_like(carry_ref)

        xv = x_ref[...]
        t = tri_ref[...]
        acc = jnp.broadcast_to(carry_ref[:, 0:1], xv.shape)
        for part in _split3(xv):
            acc = acc + jnp.dot(part, t, preferred_element_type=F32)
        o_ref[...] = acc
        carry_ref[...] = jnp.broadcast_to(acc[:, 0:1], carry_ref.shape)

    rev = pl.BlockSpec((R, tb), lambda i: (0, nb - 1 - i))
    return pl.pallas_call(
        body, name=name, grid=(nb,),
        in_specs=[rev, _full_spec((tb, tb))], out_specs=rev,
        out_shape=jax.ShapeDtypeStruct((R, S), F32),
        scratch_shapes=[pltpu.VMEM((R, LANE), F32)],
        compiler_params=_params(("arbitrary",)),
    )(x, tri)


def _pairs(nb, by_key):
    if by_key:
        pr = [(i, j) for j in range(nb) for i in range(j, nb)]
    else:
        pr = [(i, j) for i in range(nb) for j in range(i + 1)]
    return (jnp.asarray([p[0] for p in pr], jnp.int32), jnp.asarray([p[1] for p in pr], jnp.int32), len(pr))


class _AttT:
    def __init__(self, S, n_heads, q, ks, v, scale, chunk_causal, cum_rep=None):
        self.S, self.H, self.q, self.ks, self.v = S, n_heads, q, ks, v
        self.scale, self.chunk_causal, self.cum_rep = scale, chunk_causal, cum_rep
        self.T = _tile(S, ATT_T)
        self.qs = min(QSUB, self.T)
        self.nb = S // self.T
        self.dq, self.dv = q[1], v[1]
        self.has_bias = cum_rep is not None

    def q_spec(self, op):
        _, w, off, per_head = op
        return pl.BlockSpec((self.T, w), lambda h, p, it, jt: (it[p], off + (h if per_head else 0)))

    def k_spec(self, op):
        _, w, off, per_head = op
        return pl.BlockSpec((self.T, w), lambda h, p, it, jt: (jt[p], off + (h if per_head else 0)))

    def row_q(self):
        return pl.BlockSpec((None, 1, self.T), lambda h, p, it, jt: (h, 0, it[p]))

    def cum_k(self):
        return pl.BlockSpec((None, self.T, self.qs), lambda h, p, it, jt: (h, jt[p], 0))

    def sub_blocks(self, masked):
        return [(q0, min(self.T, q0 + self.qs) if masked else self.T) for q0 in range(0, self.T, self.qs)]

    def scores(self, k, q_sub, cum, q0, masked):
        s = lax.dot_general(k, q_sub, _NT, preferred_element_type=F32)
        if self.has_bias:
            s = s - cum
        mask = None
        if masked:
            r = lax.broadcasted_iota(jnp.int32, s.shape, 0)
            c = lax.broadcasted_iota(jnp.int32, s.shape, 1) + q0
            mask = (r // CHUNK <= c // CHUNK) if self.chunk_causal else (r <= c)
        return s, mask


def _join(k_refs):
    return k_refs[0][...] if len(k_refs) == 1 else jnp.concatenate([r[...] for r in k_refs], axis=-1)


def _att_fwd_t(att, name, exact=False):
    S, H, T, qs = att.S, att.H, att.T, att.qs
    it, jt, npairs = _pairs(att.nb, by_key=False)
    nk = len(att.ks)

    def body(it_ref, jt_ref, *refs):
        q_ref = refs[0]
        k_refs = refs[1:1 + nk]
        v_ref = refs[1 + nk]
        n = 2 + nk
        cum_ref = None
        if att.has_bias:
            cum_ref = refs[n]
            n += 1
        o_ref = refs[n]
        n += 1
        ox_ref = None
        if exact:
            ox_ref = refs[n]
            n += 1
        lse_ref, m_ref, l_ref, acc_ref = refs[n:n + 4]
        lo_ref = refs[n + 4] if exact else None
        p = pl.program_id(1)
        i, j = it_ref[p], jt_ref[p]

        @pl.when(j == 0)
        def _():
            m_ref[...] = jnp.full_like(m_ref, -jnp.inf)
            l_ref[...] = jnp.zeros_like(l_ref)
            acc_ref[...] = jnp.zeros_like(acc_ref)
            if exact:
                lo_ref[...] = jnp.zeros_like(lo_ref)

        def step(masked):
            k = _join(k_refs)
            v = v_ref[...]
            subs = att.sub_blocks(masked)

            def logits(idx):
                q0, nkeys = subs[idx]
                cum = cum_ref[0:nkeys, :] if att.has_bias else None
                return att.scores(k[0:nkeys], q_ref[q0:q0 + qs, :], cum, q0, masked)

            ahead = logits(0)
            for idx, (q0, nkeys) in enumerate(subs):
                qsl = slice(q0, q0 + qs)
                s, mask = ahead
                if idx + 1 < len(subs):
                    ahead = logits(idx + 1)
                if masked:
                    s = jnp.where(mask, s, -jnp.inf)
                m_prev = m_ref[:, qsl]
                m_new = jnp.maximum(m_prev, jnp.max(s, axis=0, keepdims=True))
                alpha = jnp.exp2(m_prev - m_new)
                pr = jnp.exp2(s - m_new)
                l_ref[:, qsl] = alpha * l_ref[:, qsl] + jnp.sum(pr, axis=0, keepdims=True)
                p_hi = pr.astype(BF16)
                acc_ref[:, qsl] = alpha * acc_ref[:, qsl] + lax.dot_general(
                    v[0:nkeys], p_hi, _TN, preferred_element_type=F32)
                if exact:
                    p_lo = (pr - p_hi.astype(F32)).astype(BF16)
                    lo_ref[:, qsl] = alpha * lo_ref[:, qsl] + lax.dot_general(
                        v[0:nkeys], p_lo, _TN, preferred_element_type=F32)
                m_ref[:, qsl] = m_new

        @pl.when(j < i)
        def _():
            step(False)

        @pl.when(j == i)
        def _():
            step(True)
            l = l_ref[...]
            inv = 1.0 / l
            o_ref[...] = jnp.transpose(acc_ref[...] * inv).astype(o_ref.dtype)
            if exact:
                ox_ref[...] = jnp.transpose((acc_ref[...] + lo_ref[...]) * inv)
            lse_ref[...] = m_ref[...] + jnp.log2(l)

    in_specs = [att.q_spec(att.q)] + [att.k_spec(k) for k in att.ks] + [att.k_spec(att.v)]
    args = [att.q[0]] + [k[0] for k in att.ks] + [att.v[0]]
    if att.has_bias:
        in_specs.append(att.cum_k())
        args.append(att.cum_rep)
    o_spec = pl.BlockSpec((T, att.dv), lambda h, p, it, jt: (it[p], h))
    out_specs = [o_spec]
    out_shape = [jax.ShapeDtypeStruct((S, H * att.dv), BF16)]
    scratch = [pltpu.VMEM((1, T), F32), pltpu.VMEM((1, T), F32), pltpu.VMEM((att.dv, T), F32)]
    if exact:
        out_specs.append(o_spec)
        out_shape.append(jax.ShapeDtypeStruct((S, H * att.dv), F32))
        scratch.append(pltpu.VMEM((att.dv, T), F32))
    out_specs.append(att.row_q())
    out_shape.append(jax.ShapeDtypeStruct((H, 1, S), F32))
    return pl.pallas_call(
        body, name=name,
        grid_spec=pltpu.PrefetchScalarGridSpec(
            num_scalar_prefetch=2, grid=(H, npairs), in_specs=in_specs, out_specs=out_specs,
            scratch_shapes=scratch),
        out_shape=out_shape,
        compiler_params=_params(("parallel", "arbitrary")),
    )(it, jt, *args)


def _att_delta_t(do, o, n_heads, name, order=None):
    S = do.shape[0]
    w = do.shape[1] // n_heads
    ts = _tile(S, ATT_T)
    ones = jnp.ones((8, w), BF16)
    extra = [] if order is None else [order]

    def body(do_ref, o_ref, ones_ref, *rest):
        d_ref = rest[-1]
        prod = do_ref[...].astype(F32) * o_ref[...].astype(F32)
        acc = jnp.zeros((8, ts), F32)
        for part in _split3(prod):
            acc = acc + lax.dot_general(ones_ref[...], part, _NT, preferred_element_type=F32)
        d_ref[...] = acc[0:1, :]

    blk = pl.BlockSpec((ts, w), lambda i, h: (i, h))
    return pl.pallas_call(
        body, name=name, grid=(S // ts, n_heads),
        in_specs=[blk, blk, pl.BlockSpec((8, w), lambda i, h: (0, 0))] + [_ANY_SPEC] * len(extra),
        out_specs=pl.BlockSpec((None, 1, ts), lambda i, h: (h, 0, i)),
        out_shape=jax.ShapeDtypeStruct((n_heads, 1, S), F32),
        compiler_params=_params(("parallel", "parallel")),
    )(do, o, ones, *extra)


def _att_bwd_t(att, do, lse, delta, dq_dtype, dk_dtypes, name, dq_rope=None):
    S, H, T, qs = att.S, att.H, att.T, att.qs
    it, jt, npairs = _pairs(att.nb, by_key=True)
    nk = len(att.ks)
    last = att.nb - 1
    widths = [k[1] for k in att.ks]

    def body(it_ref, jt_ref, *refs):
        q_ref = refs[0]
        k_refs = refs[1:1 + nk]
        v_ref, do_ref, lse_ref, dl_ref = refs[1 + nk:5 + nk]
        n = 5 + nk
        cum_ref = None
        if att.has_bias:
            cum_ref = refs[n]
            n += 1
        rope_refs = None
        if dq_rope is not None:
            rope_refs = refs[n:n + 3]
            n += 3
        dq_ref = refs[n]
        dk_refs = refs[n + 1:n + 1 + nk]
        dv_ref = refs[n + 1 + nk]
        n += nk + 2
        dc_ref = None
        if att.has_bias:
            dc_ref = refs[n]
            n += 1
        dq_acc, dk_acc, dv_acc = refs[n:n + 3]
        dc_acc = refs[n + 3] if att.has_bias else None
        p = pl.program_id(1)
        i, j = it_ref[p], jt_ref[p]

        @pl.when(p == 0)
        def _():
            dq_acc[...] = jnp.zeros_like(dq_acc)

        @pl.when(i == j)
        def _():
            dk_acc[...] = jnp.zeros_like(dk_acc)
            dv_acc[...] = jnp.zeros_like(dv_acc)
            if att.has_bias:
                dc_acc[...] = jnp.zeros_like(dc_acc)

        def step(masked):
            k = _join(k_refs)
            v = v_ref[...]
            subs = att.sub_blocks(masked)

            def logits(idx):
                q0, nkeys = subs[idx]
                cum = cum_ref[0:nkeys, :] if att.has_bias else None
                return att.scores(k[0:nkeys], q_ref[q0:q0 + qs, :], cum, q0, masked)

            ahead = logits(0)
            for idx, (q0, nkeys) in enumerate(subs):
                qsl = slice(q0, q0 + qs)
                ksl = slice(0, nkeys)
                q_sub = q_ref[qsl, :]
                do_sub = do_ref[qsl, :]
                s, mask = ahead
                if idx + 1 < len(subs):
                    ahead = logits(idx + 1)
                pr = jnp.exp2(s - lse_ref[:, qsl])
                if masked:
                    pr = jnp.where(mask, pr, 0.0)
                dp = lax.dot_general(v[ksl], do_sub, _NT, preferred_element_type=F32)
                ds = pr * (dp - dl_ref[:, qsl])
                ds_b = ds.astype(BF16)
                dv_acc[ksl, :] += jnp.dot(pr.astype(BF16), do_sub, preferred_element_type=F32)
                dk_acc[ksl, :] += jnp.dot(ds_b, q_sub, preferred_element_type=F32)
                dq_acc[i, :, qsl] += lax.dot_general(k[ksl], ds_b, _TN, preferred_element_type=F32)
                if att.has_bias:
                    part = ds[:, 0:LANE] if qs >= LANE else ds
                    for c0 in range(LANE, qs, LANE):
                        part = part + ds[:, c0:c0 + LANE]
                    dc_acc[ksl, :] += part

        @pl.when(i > j)
        def _():
            step(False)

        @pl.when(i == j)
        def _():
            step(True)
            dq = jnp.transpose(dq_acc[i] * att.scale)
            if dq_rope is not None:
                dq = _rope(dq, rope_refs[0][...], rope_refs[1][...], rope_refs[2][...], -1)
            dq_ref[...] = dq.astype(dq_ref.dtype)

        @pl.when(i == last)
        def _():
            dk = dk_acc[...] * (1.0 / LOG2E)
            off = 0
            for r, w in zip(dk_refs, widths):
                r[...] = dk[:, off:off + w].astype(r.dtype)
                off += w
            dv_ref[...] = dv_acc[...].astype(dv_ref.dtype)
            if att.has_bias:
                dc_ref[...] = -jnp.sum(dc_acc[...], axis=-1, keepdims=True)

    do_op = (do, att.dv, 0, True)
    in_specs = ([att.q_spec(att.q)] + [att.k_spec(k) for k in att.ks]
                + [att.k_spec(att.v), att.q_spec(do_op), att.row_q(), att.row_q()])
    args = [att.q[0]] + [k[0] for k in att.ks] + [att.v[0], do, lse, delta]
    if att.has_bias:
        in_specs.append(att.cum_k())
        args.append(att.cum_rep)
    if dq_rope is not None:
        in_specs += [pl.BlockSpec((T, att.dq), lambda h, p, it, jt: (jt[p], 0))] * 3
        args += list(dq_rope)
    out_specs = [pl.BlockSpec((T, att.dq), lambda h, p, it, jt: (jt[p], h))]
    out_shape = [jax.ShapeDtypeStruct((S, H * att.dq), dq_dtype)]
    out_specs += [pl.BlockSpec((T, w), lambda h, p, it, jt: (jt[p], h)) for w in widths]
    out_shape += [jax.ShapeDtypeStruct((S, H * w), dt) for w, dt in zip(widths, dk_dtypes)]
    out_specs.append(pl.BlockSpec((T, att.dv), lambda h, p, it, jt: (jt[p], h)))
    out_shape.append(jax.ShapeDtypeStruct((S, H * att.dv), BF16))
    scratch = [pltpu.VMEM((att.nb, att.dq, T), F32), pltpu.VMEM((T, att.dq), F32), pltpu.VMEM((T, att.dv), F32)]
    if att.has_bias:
        out_specs.append(pl.BlockSpec((None, T, 1), lambda h, p, it, jt: (h, jt[p], 0)))
        out_shape.append(jax.ShapeDtypeStruct((H, S, 1), F32))
        scratch.append(pltpu.VMEM((T, min(qs, LANE)), F32))
    return pl.pallas_call(
        body, name=name,
        grid_spec=pltpu.PrefetchScalarGridSpec(
            num_scalar_prefetch=2, grid=(H, npairs), in_specs=in_specs, out_specs=out_specs,
            scratch_shapes=scratch),
        out_shape=out_shape,
        compiler_params=_params(("parallel", "arbitrary")),
    )(it, jt, *args)


def _adamw(w, g1, g2, m, v, name):
    _, K, N = w.shape
    assert g1.shape == (K, N) and g2.shape == (K, N), name
    by_rows = K % 8 == 0
    tr = _tile(K, 256, 8) if by_rows else K
    tc = N if by_rows else _tile(N, LANE)
    c1 = 1.0 - ADAM_B1 ** ADAM_STEP
    c2 = 1.0 - ADAM_B2 ** ADAM_STEP

    def body(w_ref, g1_ref, g2_ref, m_ref, v_ref, g_ref, d_ref, nm_ref, nv_ref):
        gv = g1_ref[...] + g2_ref[...]
        nm = ADAM_B1 * m_ref[...] + (1.0 - ADAM_B1) * gv
        nv = ADAM_B2 * v_ref[...] + (1.0 - ADAM_B2) * (gv * gv)
        g_ref[...] = gv
        d_ref[...] = -ADAM_LR * ((nm / c1) / (jnp.sqrt(nv / c2) + ADAM_EPS) + ADAM_WD * w_ref[...])
        nm_ref[...] = nm
        nv_ref[...] = nv

    if by_rows:
        blk = pl.BlockSpec((None, tr, N), lambda i: (0, i, 0))
        gblk = pl.BlockSpec((tr, N), lambda i: (i, 0))
    else:
        blk = pl.BlockSpec((None, K, tc), lambda i: (0, 0, i))
        gblk = pl.BlockSpec((K, tc), lambda i: (0, i))
    return pl.pallas_call(
        body, name=name, grid=(K // tr if by_rows else N // tc,),
        in_specs=[blk, gblk, gblk, blk, blk], out_specs=[blk] * 4,
        out_shape=[jax.ShapeDtypeStruct((1, K, N), F32)] * 4,
        compiler_params=_params(("parallel",)),
    )(w, g1, g2, m, v)


_HBM_SPEC = pl.BlockSpec(memory_space=pltpu.HBM)
_SEM_SPEC = pl.BlockSpec(memory_space=pltpu.SEMAPHORE)
_VMEM_SPEC = pl.BlockSpec(memory_space=pltpu.VMEM)
_EFFECT = pltpu.SideEffectType.DATAFLOW_SIDE_EFFECTING


def _place():
    return lax.axis_index("x"), lax.axis_index("y"), lax.axis_index("c")


def _other_chips(x, y):
    return [(1 - x, y), (x, 1 - y), (1 - x, 1 - y)]


def _all_gather_halves(wp, name):
    R, C = wp.shape
    half = R // 2
    assert half % 16 == 0

    def body(w_ref, out_ref, ici_send, ici_recv, d2d_send, d2d_recv, local_sem):
        x, y, c = _place()
        me = 2 * x + y
        chips = _other_chips(x, y)
        mine = pl.ds(pl.multiple_of(c * half, 16), half)
        theirs = pl.ds(pl.multiple_of((1 - c) * half, 16), half)
        local = pltpu.make_async_copy(w_ref, out_ref.at[me], local_sem)
        local.start()
        sends = []
        for n, (px, py) in enumerate(chips):
            cp = pltpu.make_async_remote_copy(
                src_ref=w_ref.at[mine], dst_ref=out_ref.at[me, mine], send_sem=ici_send.at[n],
                recv_sem=ici_recv.at[n], device_id=(px, py, c), device_id_type=MESH)
            cp.start()
            sends.append(cp)
        for n, (px, py) in enumerate(chips):
            slot = 2 * px + py
            pltpu.make_async_remote_copy(
                src_ref=w_ref.at[mine], dst_ref=out_ref.at[slot, mine], send_sem=ici_send.at[n],
                recv_sem=ici_recv.at[n], device_id=(px, py, c), device_id_type=MESH).wait_recv()
            cp = pltpu.make_async_remote_copy(
                src_ref=out_ref.at[slot, mine], dst_ref=out_ref.at[slot, mine], send_sem=d2d_send.at[n],
                recv_sem=d2d_recv.at[n], device_id=(x, y, 1 - c), device_id_type=MESH)
            cp.start()
            sends.append(cp)
        for n, (px, py) in enumerate(chips):
            slot = 2 * px + py
            pltpu.make_async_remote_copy(
                src_ref=out_ref.at[slot, theirs], dst_ref=out_ref.at[slot, theirs], send_sem=d2d_send.at[n],
                recv_sem=d2d_recv.at[n], device_id=(x, y, 1 - c), device_id_type=MESH).wait_recv()
        for cp in sends:
            cp.wait_send()
        local.wait()

    return pl.pallas_call(
        body, name=name,
        in_specs=[_ANY_SPEC], out_specs=_ANY_SPEC,
        out_shape=jax.ShapeDtypeStruct((N_CHIPS, R, C), wp.dtype),
        scratch_shapes=[pltpu.SemaphoreType.DMA((3,)), pltpu.SemaphoreType.DMA((3,)), pltpu.SemaphoreType.DMA((3,)),
                        pltpu.SemaphoreType.DMA((3,)), pltpu.SemaphoreType.DMA],
    )(wp)


def _chip_copies(src_ref, land_ref, sems, gather):
    x, y, c = _place()
    me = 2 * x + y
    out, back = [], []
    for n, (px, py) in enumerate(_other_chips(x, y)):
        src = src_ref if gather else src_ref.at[2 * px + py]
        out.append(pltpu.make_async_remote_copy(
            src_ref=src, dst_ref=land_ref.at[me] if gather else land_ref.at[n],
            send_sem=sems[n], recv_sem=sems[3 + n], device_id=(px, py, c), device_id_type=MESH))
        back.append(pltpu.make_async_remote_copy(
            src_ref=src, dst_ref=land_ref.at[2 * px + py] if gather else land_ref.at[n],
            send_sem=sems[n], recv_sem=sems[3 + n], device_id=(px, py, c), device_id_type=MESH))
    return out, back


def _xchg_start(src, land, gather, order, name):
    def body(src_ref, land_ref, order_ref, *outs):
        sems = outs[0:6]
        token = outs[8]
        out, _ = _chip_copies(src_ref, land_ref, sems, gather)
        for cp in out:
            cp.start()
        token[...] = jnp.zeros_like(token)

    outs = pl.pallas_call(
        body, name=name,
        out_shape=(pltpu.SemaphoreType.DMA(()),) * 6 + (
            pltpu.HBM(src.shape, src.dtype), pltpu.HBM(land.shape, land.dtype),
            jax.ShapeDtypeStruct((8, LANE), F32)),
        in_specs=(_HBM_SPEC, _HBM_SPEC, _ANY_SPEC),
        out_specs=(_SEM_SPEC,) * 6 + (_HBM_SPEC, _HBM_SPEC, _VMEM_SPEC),
        input_output_aliases={0: 6, 1: 7},
        compiler_params=pltpu.CompilerParams(has_side_effects=_EFFECT),
    )(pltpu.with_memory_space_constraint(src, pltpu.HBM), pltpu.with_memory_space_constraint(land, pltpu.HBM), order)
    return outs[0:6], outs[6], outs[7], outs[8]


def _xchg_wait(started, gather, after, name):
    sems, src, land, _ = started

    def body(src_ref, land_ref, *rest):
        _, back = _chip_copies(src_ref, land_ref, rest[0:6], gather)
        for cp in back:
            cp.wait_send()
            cp.wait_recv()

    return pl.pallas_call(
        body, name=name,
        out_shape=(pltpu.HBM(src.shape, src.dtype), pltpu.HBM(land.shape, land.dtype)),
        in_specs=(_HBM_SPEC, _HBM_SPEC) + (_SEM_SPEC,) * 6 + (_ANY_SPEC,),
        out_specs=(_HBM_SPEC, _HBM_SPEC),
        input_output_aliases={0: 0, 1: 1},
        compiler_params=pltpu.CompilerParams(has_side_effects=_EFFECT),
    )(src, land, *sems, after)


def _sib_copy(src_ref, land_ref, send_sem, recv_sem):
    x, y, c = _place()
    return pltpu.make_async_remote_copy(src_ref=src_ref, dst_ref=land_ref, send_sem=send_sem, recv_sem=recv_sem,
                                        device_id=(x, y, 1 - c), device_id_type=MESH)


def _sib_start(src, name):
    land = lax.empty(src.shape, src.dtype)

    def body(src_ref, land_ref, send_sem, recv_sem, src_thru, land_thru, token):
        _sib_copy(src_ref, land_ref, send_sem, recv_sem).start()
        token[...] = jnp.zeros_like(token)

    return pl.pallas_call(
        body, name=name,
        out_shape=(pltpu.SemaphoreType.DMA(()), pltpu.SemaphoreType.DMA(()),
                   pltpu.HBM(src.shape, src.dtype), pltpu.HBM(land.shape, land.dtype),
                   jax.ShapeDtypeStruct((8, LANE), F32)),
        in_specs=(_HBM_SPEC, _HBM_SPEC),
        out_specs=(_SEM_SPEC, _SEM_SPEC, _HBM_SPEC, _HBM_SPEC, _VMEM_SPEC),
        input_output_aliases={0: 2, 1: 3},
        compiler_params=pltpu.CompilerParams(has_side_effects=_EFFECT),
    )(pltpu.with_memory_space_constraint(src, pltpu.HBM), pltpu.with_memory_space_constraint(land, pltpu.HBM))


def _sib_wait(started, after, name):
    send_sem, recv_sem, src, land, _ = started

    def body(src_ref, land_ref, send_sem, recv_sem, after_ref, src_out, land_out):
        cp = _sib_copy(src_ref, land_ref, send_sem, recv_sem)
        cp.wait_send()
        cp.wait_recv()

    return pl.pallas_call(
        body, name=name,
        out_shape=(pltpu.HBM(src.shape, src.dtype), pltpu.HBM(land.shape, land.dtype)),
        in_specs=(_HBM_SPEC, _HBM_SPEC, _SEM_SPEC, _SEM_SPEC, _ANY_SPEC),
        out_specs=(_HBM_SPEC, _HBM_SPEC),
        input_output_aliases={0: 0, 1: 1},
        compiler_params=pltpu.CompilerParams(has_side_effects=_EFFECT),
    )(src, land, send_sem, recv_sem, after)


def _sum_slabs(gp, recv, chip, name):
    _, R, C = gp.shape
    tr = _tile(R, PACK_ROWS, 16)

    def body(chip_ref, own_ref, r0_ref, r1_ref, r2_ref, o_ref):
        acc = own_ref[...].astype(F32) + r0_ref[...].astype(F32)
        o_ref[...] = (acc + r1_ref[...].astype(F32)) + r2_ref[...].astype(F32)

    def got(n):
        return pl.BlockSpec((None, tr, C), lambda i, chip_ref: (n, i, 0))

    return pl.pallas_call(
        body, name=name,
        grid_spec=pltpu.PrefetchScalarGridSpec(
            num_scalar_prefetch=1, grid=(R // tr,),
            in_specs=[pl.BlockSpec((None, tr, C), lambda i, chip_ref: (chip_ref[0], i, 0)), got(0), got(1), got(2)],
            out_specs=pl.BlockSpec((tr, C), lambda i, chip_ref: (i, 0))),
        out_shape=jax.ShapeDtypeStruct((R, C), F32),
        compiler_params=_params(("parallel",)),
    )(jnp.reshape(chip, (1,)).astype(jnp.int32), gp, recv, recv, recv)


def _all_reduce_vec(vec, name):
    VR, W = vec.shape

    def body(vec_ref, vall_ref, vout_ref, vsend_sems, vrecv_sems):
        x, y, c = _place()
        vall_ref[4 * x + 2 * y + c] = vec_ref[...]
        sends = []
        peers = []
        for r in range(1, N_DEV):
            dx, dy, dc = (r >> 2) & 1, (r >> 1) & 1, r & 1
            peer = (x ^ dx, y ^ dy, c ^ dc)
            peers.append(peer)
            cp = pltpu.make_async_remote_copy(
                src_ref=vec_ref, dst_ref=vall_ref.at[4 * x + 2 * y + c], send_sem=vsend_sems.at[r - 1],
                recv_sem=vrecv_sems.at[r - 1], device_id=peer, device_id_type=MESH)
            cp.start()
            sends.append(cp)
        for r, peer in enumerate(peers):
            pltpu.make_async_remote_copy(
                src_ref=vec_ref, dst_ref=vall_ref.at[4 * peer[0] + 2 * peer[1] + peer[2]],
                send_sem=vsend_sems.at[r], recv_sem=vrecv_sems.at[r],
                device_id=peer, device_id_type=MESH).wait_recv()
        total = vall_ref[0]
        for d in range(1, N_DEV):
            total = total + vall_ref[d]
        vout_ref[...] = total
        for cp in sends:
            cp.wait_send()

    outs = pl.pallas_call(
        body, name=name,
        in_specs=[_VMEM_SPEC], out_specs=[_VMEM_SPEC, _VMEM_SPEC],
        out_shape=[jax.ShapeDtypeStruct((N_DEV, VR, W), F32), jax.ShapeDtypeStruct((VR, W), F32)],
        scratch_shapes=[pltpu.SemaphoreType.DMA((N_DEV - 1,)), pltpu.SemaphoreType.DMA((N_DEV - 1,))],
    )(vec)
    return outs[1]


class _Pack:
    def __init__(self, group, C):
        self.group, self.C = group, C
        self.rows, self.offs, off = {}, {}, 0
        for nm, (K, N), _ in group:
            assert N <= C, nm
            self.rows[nm] = K if 2 * N > C else -(-(K * N) // C)
            self.offs[nm] = off
            off += -(-self.rows[nm] // 16) * 16
        self.used = off
        self.R = -(-off // PACK_ROWS) * PACK_ROWS

    def _rows_of(self, a):
        K, N = a.shape
        if 2 * N > self.C:
            a = jnp.pad(a, ((0, 0), (0, self.C - N)))
        else:
            a = jnp.pad(a.reshape(-1), (0, -(K * N) % self.C)).reshape(-1, self.C)
        return jnp.pad(a, ((0, -a.shape[0] % 16), (0, 0)))

    def pack(self, shards):
        parts = [self._rows_of(shards[nm].astype(BF16)) for nm, _, _ in self.group]
        return jnp.concatenate(parts + [jnp.zeros((self.R - self.used, self.C), BF16)], axis=0)

    def part(self, flat, nm, shape):
        K, N = shape
        rows = flat[self.offs[nm]:self.offs[nm] + self.rows[nm]]
        return rows[:, :N] if 2 * N > self.C else rows.reshape(-1)[:K * N].reshape(K, N)

    def slabs(self, grads):
        out = []
        for k in range(N_CHIPS):
            cut = {}
            for nm, (K, N), axis in self.group:
                g = grads[nm]
                cut[nm] = g[:, k * N:(k + 1) * N] if axis == 1 else g[k * K:(k + 1) * K, :]
            out.append(self.pack(cut))
        return jnp.stack(out)

    def full(self, gathered):
        res = {}
        for nm, (K, N), axis in self.group:
            parts = [self.part(gathered[k], nm, (K, N)) for k in range(N_CHIPS)]
            res[nm] = jnp.concatenate(parts, axis=axis)
        return res


def _rope_tables(S):
    pos = jnp.arange(S, dtype=F32)
    inv = 1.0 / (ROPE_THETA ** (jnp.arange(0, MLA_ROPE, 2, dtype=F32) / MLA_ROPE))
    ang = pos[:, None] * inv[None, :]
    cos, sin = jnp.cos(ang), jnp.sin(ang)
    half = MLA_ROPE // 2
    z = jnp.zeros((S, half), F32)
    one = jnp.ones((S, LANE - MLA_ROPE), F32)
    zero = jnp.zeros((S, LANE - MLA_ROPE), F32)
    kc = jnp.concatenate([cos, cos, one], axis=1)
    ksa = jnp.concatenate([-sin, z, zero], axis=1)
    ksb = jnp.concatenate([z, sin, zero], axis=1)
    qc = jnp.concatenate([jnp.ones((S, MLA_NOPE), F32), kc], axis=1)
    qsa = jnp.concatenate([jnp.zeros((S, MLA_NOPE), F32), ksa], axis=1)
    qsb = jnp.concatenate([jnp.zeros((S, MLA_NOPE), F32), ksb], axis=1)
    return (kc, ksa, ksb), (qc, qsa, qsb)


def _pad_cols(a, width):
    return jnp.pad(a, ((0, 0), (0, width - a.shape[1])))


def kernel(x, attn_norm, w_in, fox_f_bias, q_norm, w_uq, kv_norm, w_ukv, w_mla_branch, w_fox_branch, w_out, mlp_norm, w_up, w_down, final_norm, loss_target, m_attn_norm, m_w_in, m_fox_f_bias, m_q_norm, m_w_uq, m_kv_norm, m_w_ukv, m_w_mla_branch, m_w_fox_branch, m_w_out, m_mlp_norm, m_w_up, m_w_down, m_final_norm, v_attn_norm, v_w_in, v_fox_f_bias, v_q_norm, v_w_uq, v_kv_norm, v_w_ukv, v_w_mla_branch, v_w_fox_branch, v_w_out, v_mlp_norm, v_w_up, v_w_down, v_final_norm):
    _, S, D = x.shape
    H, HF = MLA_HEADS, FOX_HEADS
    QL, KVL = MLA_Q_LORA, MLA_KV_LORA
    assert H == HF and H <= 8
    xs = x[0]
    target = loss_target[0]
    C = D
    chip = 2 * lax.axis_index("x") + lax.axis_index("y")

    def flip(a):
        return jnp.transpose(a, (0, 2, 1))

    w_in, m_w_in, v_w_in = flip(w_in), flip(m_w_in), flip(v_w_in)
    weights = {"attn_norm": attn_norm, "w_in": w_in, "fox_f_bias": fox_f_bias, "q_norm": q_norm, "w_uq": w_uq,
               "kv_norm": kv_norm, "w_ukv": w_ukv, "w_mla_branch": w_mla_branch, "w_fox_branch": w_fox_branch,
               "w_out": w_out, "mlp_norm": mlp_norm, "w_up": w_up, "w_down": w_down, "final_norm": final_norm}
    moments = {"attn_norm": (m_attn_norm, v_attn_norm), "w_in": (m_w_in, v_w_in), "fox_f_bias": (m_fox_f_bias, v_fox_f_bias),
               "q_norm": (m_q_norm, v_q_norm), "w_uq": (m_w_uq, v_w_uq), "kv_norm": (m_kv_norm, v_kv_norm),
               "w_ukv": (m_w_ukv, v_w_ukv), "w_mla_branch": (m_w_mla_branch, v_w_mla_branch),
               "w_fox_branch": (m_w_fox_branch, v_w_fox_branch), "w_out": (m_w_out, v_w_out),
               "mlp_norm": (m_mlp_norm, v_mlp_norm), "w_up": (m_w_up, v_w_up), "w_down": (m_w_down, v_w_down),
               "final_norm": (m_final_norm, v_final_norm)}

    def group(names_axes):
        return [(nm, weights[nm].shape[1:], axis) for nm, axis in names_axes]

    pack_a = _Pack(group([("w_in", 0), ("w_uq", 1), ("w_ukv", 1)]), C)
    pack_b = _Pack(group([("w_mla_branch", 1), ("w_fox_branch", 1), ("w_out", 0), ("w_up", 1), ("w_down", 0)]), C)
    RA, RB = pack_a.R, pack_b.R
    wp_a = pack_a.pack({nm: weights[nm][0] for nm, _, _ in pack_a.group})
    wp_b = pack_b.pack({nm: weights[nm][0] for nm, _, _ in pack_b.group})
    gathered_a = _all_gather_halves(wp_a, "all_gather_a")
    ag_b = _xchg_start(wp_b, lax.empty((N_CHIPS, RB, C), BF16), True, gathered_a, "all_gather_start_b")
    xn = _norm_fwd(xs, attn_norm, "attn_norm_fwd", order=ag_b[3])
    full = pack_a.full(gathered_a)

    o_ckv = QL
    o_kr = o_ckv + KVL
    o_fq = o_kr + MLA_ROPE
    o_ff = o_fq + 3 * HF * FOX_HEAD_DIM
    o_g = o_ff + HF
    wi = full["w_in"]
    assert wi.shape[0] == o_g + 2 * D
    WS = QL + KVL + 2 * LANE
    NQKV = 3 * HF * FOX_HEAD_DIM

    def pad_rows(a, rows):
        return jnp.pad(a, ((0, rows - a.shape[0]), (0, 0)))

    w_small = jnp.concatenate([wi[:o_kr], pad_rows(wi[o_kr:o_fq], LANE), pad_rows(wi[o_ff:o_g], LANE)], axis=0)
    w_qkv = wi[o_fq:o_ff]
    w_g = wi[o_g:]
    w_pack = jnp.concatenate([w_small, w_qkv, w_g], axis=0)
    dqk = MLA_NOPE + MLA_ROPE
    w_uq_p = jnp.pad(full["w_uq"].reshape(QL, H, dqk), ((0, 0), (0, 0), (0, QPAD - dqk))).reshape(QL, H * QPAD)
    ukv = full["w_ukv"].reshape(KVL, H, MLA_NOPE + MLA_V)
    w_ukv_p = jnp.concatenate([ukv[:, :, :MLA_NOPE].reshape(KVL, H * MLA_NOPE),
                               ukv[:, :, MLA_NOPE:].reshape(KVL, H * MLA_V)], axis=1)

    (kc, ksa, ksb), (qc, qsa, qsb) = _rope_tables(S)
    bias_pad = _pad_cols(fox_f_bias, LANE)

    small = _matmul(xn, w_small, "nt", [F32], "proj_small")
    n_fq = HF * FOX_HEAD_DIM
    q_scale = jnp.concatenate([jnp.full((1, n_fq), LOG2E / math.sqrt(FOX_HEAD_DIM), F32),
                               jnp.ones((1, NQKV - n_fq), F32)], axis=1)
    qkv = _matmul(xn, w_qkv, "nt", [BF16], "proj_qkv", col_extras=(q_scale,), epilogue=lambda acc, cs: (acc * cs,))
    gpre = _matmul(xn, w_g, "nt", [F32], "proj_gates")
    cqn, ckvn, kr, cum = _prep_fwd(small, q_norm, kv_norm, bias_pad, kc, ksa, ksb, HF, "prep_fwd")
    c2_mla = LOG2E / math.sqrt(dqk)
    q_rot = _matmul(cqn, w_uq_p, "nn", [BF16], "mla_q_up", tn=QPAD, row_extras=(qc * c2_mla, qsa * c2_mla, qsb * c2_mla),
                    epilogue=lambda acc, c, sa, sb: (_rope(acc, c, sa, sb, 1),))
    kv2 = _matmul(ckvn, w_ukv_p, "nn", [BF16], "mla_kv_up")

    mla = _AttT(S, H, (q_rot, QPAD, 0, True), [(kv2, MLA_NOPE, 0, True), (kr, LANE, 0, False)],
                (kv2, MLA_V, H, True), 1.0 / math.sqrt(dqk), True)
    o_mla, lse_mla = _att_fwd_t(mla, "mla_att_fwd")

    cum_t = jnp.transpose(cum[:, :HF]) * LOG2E
    cum_rep = jnp.broadcast_to(cum_t[:, :, None], (HF, S, min(QSUB, _tile(S, ATT_T))))
    fox = _AttT(S, HF, (qkv, FOX_HEAD_DIM, 0, True), [(qkv, FOX_HEAD_DIM, HF, True)],
                (qkv, FOX_HEAD_DIM, 2 * HF, True), 1.0 / math.sqrt(FOX_HEAD_DIM), False, cum_rep)
    o_fox, ox_fox, lse_fox = _att_fwd_t(fox, "fox_att_fwd", exact=True)

    own_b, land_b = _xchg_wait(ag_b, True, lse_fox, "all_gather_wait_b")
    full.update(pack_b.full(lax.dynamic_update_slice(land_b, own_b[None], (chip, 0, 0))))
    w_mb, w_fb, w_o, w_u, w_d = (full[n] for n in ("w_mla_branch", "w_fox_branch", "w_out", "w_up", "w_down"))

    y_mla = _matmul(o_mla, w_mb, "nn", [F32], "mla_branch")

    def gate_merge(acc, ga, gb, ya):
        return acc, _sigmoid(ga) * ya + _sigmoid(gb) * acc

    y_fox, merged = _matmul(o_fox, w_fb, "nn", [F32, BF16], "fox_branch_gates", extras=((gpre, 0), (gpre, 1), y_mla),
                            epilogue=gate_merge)
    h1 = _matmul(merged, w_o, "nn", [F32], "out_proj", extras=(xs,), epilogue=lambda acc, r: (acc + r,))
    hn = _norm_fwd(h1, mlp_norm, "mlp_norm_fwd")

    def relu2(acc):
        a = jnp.maximum(acc, 0.0)
        return a * a, a

    u, a_pos = _matmul(hn, w_u, "nn", [BF16, BF16], "mlp_up", epilogue=relu2)
    h2 = _matmul(u, w_d, "nn", [F32], "mlp_down", tn=1024, extras=(h1,), epilogue=lambda acc, r: (acc + r,))
    dh2, dh2_b, g_final, loss_part = _final(h2, final_norm.reshape(1, D), target, "final_norm_loss")

    da = _matmul(dh2_b, w_d, "nt", [BF16], "mlp_down_dx", extras=(a_pos,),
                 epilogue=lambda acc, a: (acc * (2.0 * a.astype(F32)),))
    g_w_down = _mm_tn(u, dh2_b, "mlp_down_dw")
    dhn = _matmul(da, w_u, "nt", [F32], "mlp_up_dx", tn=1024)
    g_w_up = _mm_tn(hn, da, "mlp_up_dw")
    dh1, dh1_b, g_mlp_norm = _norm_bwd(h1, dhn, mlp_norm, dh2, "mlp_norm_bwd")

    def gate_bwd(acc, ga, gb, ya, yb):
        ga, gb = _sigmoid(ga), _sigmoid(gb)
        return acc * ga, acc * gb, acc * ya * (ga * (1.0 - ga)), acc * yb * (gb * (1.0 - gb))

    dy_mla, dy_fox, dg_mla, dg_fox = _matmul(dh1_b, w_o, "nt", [BF16] * 4, "out_proj_dx_gates",
                                             extras=((gpre, 0), (gpre, 1), y_mla, y_fox), epilogue=gate_bwd)
    g_w_out = _mm_tn(merged, dh1_b, "out_proj_dw")
    do_mla = _matmul(dy_mla, w_mb, "nt", [BF16], "mla_branch_dx")
    g_w_mb = _mm_tn(o_mla, dy_mla, "mla_branch_dw")
    do_fox = _matmul(dy_fox, w_fb, "nt", [BF16], "fox_branch_dx")
    g_w_fb = _mm_tn(o_fox, dy_fox, "fox_branch_dw")

    gp_b = pack_b.slabs({"w_mla_branch": g_w_mb, "w_fox_branch": g_w_fb, "w_out": g_w_out,
                         "w_up": g_w_up, "w_down": g_w_down})
    rs_b = _xchg_start(gp_b, lax.empty((3, RB, C), BF16), False, g_w_fb, "grad_scatter_start_b")

    delta_mla = _att_delta_t(do_mla, o_mla, H, "mla_att_delta", order=rs_b[3])
    dq_rot, dk_nope, dkr_heads, dv_mla = _att_bwd_t(mla, do_mla, lse_mla, delta_mla, BF16, [BF16, F32],
                                                    "mla_att_bwd", dq_rope=(qc, qsa, qsb))
    delta_fox = _att_delta_t(do_fox, ox_fox, HF, "fox_att_delta")
    dfq, dfk, dfv, dcum = _att_bwd_t(fox, do_fox, lse_fox, delta_fox, BF16, [BF16], "fox_att_bwd")

    gp_b_sent, recv_b = _xchg_wait(rs_b, False, dfq, "grad_scatter_wait_b")
    swap_b = _sib_start(_sum_slabs(gp_b_sent, recv_b, chip, "grad_sum_b"), "grad_swap_start_b")

    dcqn = _matmul(dq_rot, w_uq_p, "nt", [F32], "mla_q_up_dx", order=swap_b[4])
    g_w_uq_p = _mm_tn(cqn, dq_rot, "mla_q_up_dw")
    dkv2 = jnp.concatenate([dk_nope, dv_mla], axis=1)
    dckvn = _matmul(dkv2, w_ukv_p, "nt", [F32], "mla_kv_up_dx")
    g_w_ukv_p = _mm_tn(ckvn, dkv2, "mla_kv_up_dw")

    dcum_rows = jnp.pad(dcum[:, :, 0], ((0, 8 - HF), (0, 0)))
    dlogf_rows = _suffix_sum_rows(dcum_rows, "fox_forget_suffix_sum")
    dlogf = _pad_cols(jnp.transpose(dlogf_rows[:HF]), LANE)
    d_small, g_q_norm, g_kv_norm, g_bias = _prep_bwd(
        small, dcqn, dckvn, dkr_heads, dlogf, q_norm, kv_norm, bias_pad, kc, ksa, ksb, H, "prep_bwd")
    dproj = jnp.concatenate([d_small, dfq, dfk, dfv, dg_mla, dg_fox], axis=1)
    g_w_pack = _mm_tn(dproj, xn, "proj_dw")

    gs, gq, gg = g_w_pack[:WS], g_w_pack[WS:WS + NQKV], g_w_pack[WS + NQKV:]
    g_w_in = jnp.concatenate([gs[:o_kr], gs[o_kr:o_kr + MLA_ROPE], gq,
                              gs[o_kr + LANE:o_kr + LANE + HF], gg], axis=0)
    g_w_uq = g_w_uq_p.reshape(QL, H, QPAD)[:, :, :dqk].reshape(QL, H * dqk)
    g_w_ukv = jnp.concatenate([g_w_ukv_p[:, :H * MLA_NOPE].reshape(KVL, H, MLA_NOPE),
                               g_w_ukv_p[:, H * MLA_NOPE:].reshape(KVL, H, MLA_V)], axis=2).reshape(KVL, -1)

    gp_a = pack_a.slabs({"w_in": g_w_in, "w_uq": g_w_uq, "w_ukv": g_w_ukv})
    rs_a = _xchg_start(gp_a, lax.empty((3, RA, C), BF16), False, g_w_pack, "grad_scatter_start_a")
    dxn = _matmul(dproj, w_pack, "nn", [F32], "proj_dx", tn=1024, order=rs_a[3])
    grad_x, _, g_attn_norm = _norm_bwd(xs, dxn, attn_norm, dh1, "attn_norm_bwd")
    gp_a_sent, recv_a = _xchg_wait(rs_a, False, grad_x, "grad_scatter_wait_a")
    swap_a = _sib_start(_sum_slabs(gp_a_sent, recv_a, chip, "grad_sum_a"), "grad_swap_start_a")
    vec_w = max(D, LANE)
    vec_rows = [g_attn_norm, g_mlp_norm, g_final, g_q_norm, g_kv_norm, g_bias, loss_part]
    vec = jnp.concatenate([_pad_cols(v, vec_w) for v in vec_rows] + [jnp.zeros((1, vec_w), F32)], axis=0)
    vsum = _all_reduce_vec(vec, "all_reduce_vectors")
    part_b, sib_b = _sib_wait(swap_b, vsum, "grad_swap_wait_b")

    grads, deltas, new_m, new_v = {}, {}, {}, {}

    def update(pack, mine, theirs):
        for nm, shape, _ in pack.group:
            g, d, nm_, nv_ = _adamw(weights[nm], pack.part(mine, nm, shape), pack.part(theirs, nm, shape),
                                    moments[nm][0], moments[nm][1], "adamw_" + nm)
            grads[nm], deltas[nm], new_m[nm], new_v[nm] = g, d, nm_, nv_
        return g

    last_b = update(pack_b, part_b, sib_b)
    part_a, sib_a = _sib_wait(swap_a, last_b, "grad_swap_wait_a")
    update(pack_a, part_a, sib_a)

    vec_names = ["attn_norm", "mlp_norm", "final_norm", "q_norm", "kv_norm", "fox_f_bias"]

    def vec_pack(arrs):
        return jnp.concatenate([_pad_cols(a.reshape(1, -1), vec_w) for a in arrs]
                               + [jnp.zeros((2, vec_w), F32)], axis=0)[None]

    vg, vd, vm, vv = _adamw(vec_pack([weights[n] for n in vec_names]), vsum, jnp.zeros_like(vsum),
                            vec_pack([moments[n][0] for n in vec_names]), vec_pack([moments[n][1] for n in vec_names]),
                            "adamw_vectors")
    for r, nm in enumerate(vec_names):
        shp = weights[nm].shape
        n = weights[nm].size
        grads[nm] = vsum[r, :n].reshape(shp)
        deltas[nm], new_m[nm], new_v[nm] = (vd[0, r, :n].reshape(shp), vm[0, r, :n].reshape(shp),
                                            vv[0, r, :n].reshape(shp))
    loss = vsum[6, 0]

    for res in (grads, deltas, new_m, new_v):
        res["w_in"] = flip(res["w_in"])
    order = ["attn_norm", "w_in", "fox_f_bias", "q_norm", "w_uq", "kv_norm", "w_ukv", "w_mla_branch", "w_fox_branch",
             "w_out", "mlp_norm", "w_up", "w_down", "final_norm"]
    return (loss, grad_x[None], *[grads[n] for n in order], *[deltas[n] for n in order],
            *[new_m[n] for n in order], *[new_v[n] for n in order])
```

```python
import math

import jax
import jax.numpy as jnp
from jax import lax
from jax.experimental import pallas as pl
from jax.experimental.pallas import tpu as pltpu

CHUNK = 64
MLA_HEADS = 8
MLA_Q_LORA = 512
MLA_KV_LORA = 256
MLA_NOPE = 128
MLA_ROPE = 64
MLA_V = 128
ROPE_THETA = 10000.0
FOX_HEADS = 8
FOX_HEAD_DIM = 128
EPS = 1e-6

ADAM_LR = 0.001
ADAM_B1 = 0.9
ADAM_B2 = 0.999
ADAM_EPS = 1e-08
ADAM_WD = 0.01
ADAM_STEP = 10

LANE = 128
QPAD = 2 * LANE
N_CHIPS = 4
N_DEV = 8
VMEM_LIMIT = 48 * 1024 * 1024
ATT_T = 1024
QSUB = 256
ROW_T = 256
PACK_ROWS = 256
LOG2E = 1.4426950408889634

BF16 = jnp.bfloat16
F32 = jnp.float32
MESH = pl.DeviceIdType.MESH

_NT = (((1,), (1,)), ((), ()))
_TN = (((0,), (0,)), ((), ()))
_NN = (((1,), (0,)), ((), ()))


def _tile(dim, pref, align=LANE):
    if dim <= pref:
        return dim
    t = (pref // align) * align
    while t >= align:
        if dim % t == 0:
            return t
        t -= align
    return dim


def _params(sem=None):
    return pltpu.CompilerParams(dimension_semantics=sem, vmem_limit_bytes=VMEM_LIMIT)


_ANY_SPEC = pl.BlockSpec(memory_space=pl.ANY)


def _matmul(a, b, mode, out_dtypes, name, *, tm=1024, tn=512, tk=2048, extras=(), row_extras=(), col_extras=(),
            epilogue=None, order=None, into=None):
    if mode == "nn":
        (M, K), (K2, N) = a.shape, b.shape
    elif mode == "nt":
        (M, K), (N, K2) = a.shape, b.shape
    else:
        (K, M), (K2, N) = a.shape, b.shape
    assert K == K2, (name, a.shape, b.shape)
    tm, tn, tk = _tile(M, tm), _tile(N, tn), _tile(K, tk)
    nk = K // tk
    extras = [e if isinstance(e, tuple) else (e, 0) for e in extras]
    n_out = len(out_dtypes)
    n_ex = len(extras) + len(row_extras) + len(col_extras)
    n_ord = 0 if order is None else 1
    assert all(r.shape == (M, tn) for r in row_extras), name
    dims = {"nn": _NN, "nt": _NT, "tn": _TN}[mode]

    def body(*refs):
        a_ref, b_ref = refs[0], refs[1]
        ex_refs = refs[2:2 + n_ex]
        o_refs = refs[2 + n_ex + n_ord:2 + n_ex + n_ord + n_out]
        acc_ref = refs[2 + n_ex + n_ord + n_out]
        k = pl.program_id(2)
        part = lax.dot_general(a_ref[...], b_ref[...], dims, preferred_element_type=F32)

        @pl.when(k == 0)
        def _():
            acc_ref[...] = part

        @pl.when(k > 0)
        def _():
            acc_ref[...] += part

        @pl.when(k == nk - 1)
        def _():
            acc = acc_ref[...]
            if epilogue is None:
                outs = (acc,)
            else:
                outs = epilogue(acc, *[r[...] for r in ex_refs])
            for o_ref, o in zip(o_refs, outs):
                o_ref[...] = o.astype(o_ref.dtype)

    if mode == "nn":
        a_spec = pl.BlockSpec((tm, tk), lambda i, j, k: (i, k))
        b_spec = pl.BlockSpec((tk, tn), lambda i, j, k: (k, j))
    elif mode == "nt":
        a_spec = pl.BlockSpec((tm, tk), lambda i, j, k: (i, k))
        b_spec = pl.BlockSpec((tn, tk), lambda i, j, k: (j, k))
    else:
        a_spec = pl.BlockSpec((tk, tm), lambda i, j, k: (k, i))
        b_spec = pl.BlockSpec((tk, tn), lambda i, j, k: (k, j))
    mn_spec = pl.BlockSpec((tm, tn), lambda i, j, k: (i, j))
    row_spec = pl.BlockSpec((tm, tn), lambda i, j, k: (i, 0))
    col_spec = pl.BlockSpec((1, tn), lambda i, j, k: (0, j))
    out_specs = [mn_spec] * n_out
    out_shape = [jax.ShapeDtypeStruct((M, N), dt) for dt in out_dtypes]
    aliases = {}
    if into is not None:
        buf, place = into
        assert n_out == 1 and n_ord == 1 and order is buf, name
        out_specs = [pl.BlockSpec((None, tm, tn), lambda i, j, k: place(i, j))]
        out_shape = [jax.ShapeDtypeStruct(buf.shape, buf.dtype)]
        aliases = {2 + n_ex: 0}
    outs = pl.pallas_call(
        body,
        name=name,
        grid=(M // tm, N // tn, nk),
        in_specs=([a_spec, b_spec]
                  + [pl.BlockSpec((tm, tn), lambda i, j, k, g=g: (i, j + g * (N // tn))) for _, g in extras]
                  + [row_spec] * len(row_extras) + [col_spec] * len(col_extras) + [_ANY_SPEC] * n_ord),
        out_specs=out_specs,
        out_shape=out_shape,
        scratch_shapes=[pltpu.VMEM((tm, tn), F32)],
        input_output_aliases=aliases,
        compiler_params=_params(("parallel", "parallel", "arbitrary")),
    )(a, b, *[e for e, _ in extras], *row_extras, *col_extras, *([] if order is None else [order]))
    return outs[0] if n_out == 1 else outs


def _mm_tn(a, b, name, tm=1024, tn=1024, into=None):
    return _matmul(a, b, "tn", [F32], name, tm=tm, tn=tn, tk=2048, into=into,
                   order=None if into is None else into[0])


def _row_spec(ts, width, col=0):
    return pl.BlockSpec((ts, width), lambda i: (i, col))


def _full_spec(shape):
    return pl.BlockSpec(shape, lambda i: tuple(0 for _ in shape))


def _rms(x):
    return lax.rsqrt(jnp.mean(x * x, axis=-1, keepdims=True) + EPS)


def _rms_bwd(x, dy, g):
    r = _rms(x)
    xh = x * r
    gy = dy * g
    dx = r * (gy - xh * jnp.mean(xh * gy, axis=-1, keepdims=True))
    return dx, dy * xh


def _norm_fwd(x, g, name, order=None):
    S, D = x.shape
    ts = _tile(S, ROW_T, 8)

    def body(x_ref, g_ref, *rest):
        o_ref = rest[-1]
        xv = x_ref[...]
        o_ref[...] = ((xv * _rms(xv)) * g_ref[...]).astype(BF16)

    extra = [] if order is None else [order]
    return pl.pallas_call(
        body, name=name, grid=(S // ts,),
        in_specs=[_row_spec(ts, D), _full_spec((1, D))] + [_ANY_SPEC] * len(extra),
        out_specs=_row_spec(ts, D),
        out_shape=jax.ShapeDtypeStruct((S, D), BF16),
        compiler_params=_params(("parallel",)),
    )(x, g, *extra)


def _norm_bwd(x, dy, g, dres, name):
    S, D = x.shape
    ts = _tile(S, ROW_T, 8)

    def body(x_ref, dy_ref, g_ref, dres_ref, dx_ref, dxb_ref, dg_ref):
        dx, dg_rows = _rms_bwd(x_ref[...], dy_ref[...], g_ref[...])
        dx = dres_ref[...] + dx
        dx_ref[...] = dx
        dxb_ref[...] = dx.astype(BF16)

        @pl.when(pl.program_id(0) == 0)
        def _():
            dg_ref[...] = jnp.zeros_like(dg_ref)

        dg_ref[...] += jnp.sum(dg_rows, axis=0, keepdims=True)

    return pl.pallas_call(
        body, name=name, grid=(S // ts,),
        in_specs=[_row_spec(ts, D), _row_spec(ts, D), _full_spec((1, D)), _row_spec(ts, D)],
        out_specs=[_row_spec(ts, D), _row_spec(ts, D), _full_spec((1, D))],
        out_shape=[jax.ShapeDtypeStruct((S, D), F32), jax.ShapeDtypeStruct((S, D), BF16),
                   jax.ShapeDtypeStruct((1, D), F32)],
        compiler_params=_params(("arbitrary",)),
    )(x, dy, g, dres)


def _rope(x, c, sa, sb, sign):
    w = x.shape[-1]
    half = MLA_ROPE // 2
    fwd = pltpu.roll(x, w - half, 1)
    back = pltpu.roll(x, half, 1)
    if sign < 0:
        return x * c - fwd * sa - back * sb
    return x * c + fwd * sa + back * sb


def _split3(x):
    hi = x.astype(BF16)
    r1 = x - hi.astype(F32)
    mid = r1.astype(BF16)
    lo = (r1 - mid.astype(F32)).astype(BF16)
    return hi, mid, lo


def _prep_fwd(small, q_norm, kv_norm, bias_pad, kc, ksa, ksb, n_heads, name):
    S, W = small.shape
    QL, KVL = q_norm.shape[1], kv_norm.shape[1]
    assert W == QL + KVL + 2 * LANE
    ts = _tile(S, ROW_T, 8)
    tri = (lax.broadcasted_iota(jnp.int32, (ts, ts), 0) >= lax.broadcasted_iota(jnp.int32, (ts, ts), 1)).astype(BF16)

    def body(s_ref, qn_ref, kvn_ref, b_ref, kc_ref, ksa_ref, ksb_ref, tri_ref,
             cqn_ref, ckvn_ref, kr_ref, cum_ref, carry_ref):
        cq = s_ref[:, 0:QL]
        cqn_ref[...] = ((cq * _rms(cq)) * qn_ref[...]).astype(BF16)
        ckv = s_ref[:, QL:QL + KVL]
        ckvn_ref[...] = ((ckv * _rms(ckv)) * kvn_ref[...]).astype(BF16)
        kr = s_ref[:, QL + KVL:QL + KVL + LANE]
        kr_ref[...] = _rope(kr, kc_ref[...], ksa_ref[...], ksb_ref[...], 1).astype(BF16)
        z = s_ref[:, QL + KVL + LANE:W] + b_ref[...]
        logf = jnp.minimum(z, 0.0) - jnp.log1p(jnp.exp(-jnp.abs(z)))
        lane = lax.broadcasted_iota(jnp.int32, logf.shape, 1)
        logf = jnp.where(lane < n_heads, logf, 0.0)

        @pl.when(pl.program_id(0) == 0)
        def _():
            carry_ref[...] = jnp.zeros_like(carry_ref)

        t = tri_ref[...]
        cum = carry_ref[...]
        for part in _split3(logf):
            cum = cum + jnp.dot(t, part, preferred_element_type=F32)
        cum_ref[...] = cum
        carry_ref[...] = cum[ts - 1:ts, :]

    return pl.pallas_call(
        body, name=name, grid=(S // ts,),
        in_specs=[_row_spec(ts, W), _full_spec((1, QL)), _full_spec((1, KVL)), _full_spec((1, LANE)),
                  _row_spec(ts, LANE), _row_spec(ts, LANE), _row_spec(ts, LANE), _full_spec((ts, ts))],
        out_specs=[_row_spec(ts, QL), _row_spec(ts, KVL), _row_spec(ts, LANE), _row_spec(ts, LANE)],
        out_shape=[jax.ShapeDtypeStruct((S, QL), BF16), jax.ShapeDtypeStruct((S, KVL), BF16),
                   jax.ShapeDtypeStruct((S, LANE), BF16), jax.ShapeDtypeStruct((S, LANE), F32)],
        scratch_shapes=[pltpu.VMEM((1, LANE), F32)],
        compiler_params=_params(("arbitrary",)),
    )(small, q_norm, kv_norm, bias_pad, kc, ksa, ksb, tri)


def _prep_bwd(small, dcqn, dckvn, dkr_heads, dlogf, q_norm, kv_norm, bias_pad, kc, ksa, ksb, n_heads, name):
    S, W = small.shape
    QL, KVL = q_norm.shape[1], kv_norm.shape[1]
    ts = _tile(S, ROW_T, 8)

    def body(s_ref, dcq_ref, dckv_ref, dkr_ref, dlf_ref, qn_ref, kvn_ref, b_ref, kc_ref, ksa_ref, ksb_ref,
             ds_ref, gq_ref, gkv_ref, gb_ref):
        dcq, gq_rows = _rms_bwd(s_ref[:, 0:QL], dcq_ref[...], qn_ref[...])
        ds_ref[:, 0:QL] = dcq.astype(BF16)
        dckv, gkv_rows = _rms_bwd(s_ref[:, QL:QL + KVL], dckv_ref[...], kvn_ref[...])
        ds_ref[:, QL:QL + KVL] = dckv.astype(BF16)
        dkr = dkr_ref[:, 0:LANE]
        for h in range(1, n_heads):
            dkr = dkr + dkr_ref[:, h * LANE:(h + 1) * LANE]
        ds_ref[:, QL + KVL:QL + KVL + LANE] = _rope(dkr, kc_ref[...], ksa_ref[...], ksb_ref[...], -1).astype(BF16)
        z = s_ref[:, QL + KVL + LANE:W] + b_ref[...]
        dff = dlf_ref[...] * (1.0 / (1.0 + jnp.exp(z)))
        ds_ref[:, QL + KVL + LANE:W] = dff.astype(BF16)

        @pl.when(pl.program_id(0) == 0)
        def _():
            gq_ref[...] = jnp.zeros_like(gq_ref)
            gkv_ref[...] = jnp.zeros_like(gkv_ref)
            gb_ref[...] = jnp.zeros_like(gb_ref)

        gq_ref[...] += jnp.sum(gq_rows, axis=0, keepdims=True)
        gkv_ref[...] += jnp.sum(gkv_rows, axis=0, keepdims=True)
        gb_ref[...] += jnp.sum(dff, axis=0, keepdims=True)

    return pl.pallas_call(
        body, name=name, grid=(S // ts,),
        in_specs=[_row_spec(ts, W), _row_spec(ts, QL), _row_spec(ts, KVL), _row_spec(ts, n_heads * LANE),
                  _row_spec(ts, LANE), _full_spec((1, QL)), _full_spec((1, KVL)), _full_spec((1, LANE)),
                  _row_spec(ts, LANE), _row_spec(ts, LANE), _row_spec(ts, LANE)],
        out_specs=[_row_spec(ts, W), _full_spec((1, QL)), _full_spec((1, KVL)), _full_spec((1, LANE))],
        out_shape=[jax.ShapeDtypeStruct((S, W), BF16), jax.ShapeDtypeStruct((1, QL), F32),
                   jax.ShapeDtypeStruct((1, KVL), F32), jax.ShapeDtypeStruct((1, LANE), F32)],
        compiler_params=_params(("arbitrary",)),
    )(small, dcqn, dckvn, dkr_heads, dlogf, q_norm, kv_norm, bias_pad, kc, ksa, ksb)


def _sigmoid(z):
    return 1.0 / (1.0 + jnp.exp(-z))


def _final(h, g, target, name):
    S, D = h.shape
    ts = _tile(S, ROW_T, 8)

    def body(h_ref, g_ref, t_ref, dh_ref, dhb_ref, dg_ref, loss_ref):
        hv = h_ref[...]
        gv = g_ref[...]
        err = (hv * _rms(hv)) * gv - t_ref[...]
        dh, dg_rows = _rms_bwd(hv, err / D, gv)
        dh_ref[...] = dh
        dhb_ref[...] = dh.astype(BF16)

        @pl.when(pl.program_id(0) == 0)
        def _():
            dg_ref[...] = jnp.zeros_like(dg_ref)
            loss_ref[...] = jnp.zeros_like(loss_ref)

        dg_ref[...] += jnp.sum(dg_rows, axis=0, keepdims=True)
        row_loss = jnp.mean(err * err, axis=-1, keepdims=True)
        loss_ref[...] += 0.5 * jnp.sum(row_loss, axis=0, keepdims=True)

    return pl.pallas_call(
        body, name=name, grid=(S // ts,),
        in_specs=[_row_spec(ts, D), _full_spec((1, D)), _row_spec(ts, D)],
        out_specs=[_row_spec(ts, D), _row_spec(ts, D), _full_spec((1, D)), _full_spec((1, LANE))],
        out_shape=[jax.ShapeDtypeStruct((S, D), F32), jax.ShapeDtypeStruct((S, D), BF16),
                   jax.ShapeDtypeStruct((1, D), F32), jax.ShapeDtypeStruct((1, LANE), F32)],
        compiler_params=_params(("arbitrary",)),
    )(h, g, target)


def _suffix_sum_rows(x, name):
    R, S = x.shape
    tb = _tile(S, 512)
    nb = S // tb
    tri = (lax.broadcasted_iota(jnp.int32, (tb, tb), 0) >= lax.broadcasted_iota(jnp.int32, (tb, tb), 1)).astype(BF16)

    def body(x_ref, tri_ref, o_ref, carry_ref):
        @pl.when(pl.program_id(0) == 0)
        def _():
            carry_ref[...] = jnp.zeros_like(carry_ref)

        xv = x_ref[...]
        t = tri_ref[...]
        acc = jnp.broadcast_to(carry_ref[:, 0:1], xv.shape)
        for part in _split3(xv):
            acc = acc + jnp.dot(part, t, preferred_element_type=F32)
        o_ref[...] = acc
        carry_ref[...] = jnp.broadcast_to(acc[:, 0:1], carry_ref.shape)

    rev = pl.BlockSpec((R, tb), lambda i: (0, nb - 1 - i))
    return pl.pallas_call(
        body, name=name, grid=(nb,),
        in_specs=[rev, _full_spec((tb, tb))], out_specs=rev,
        out_shape=jax.ShapeDtypeStruct((R, S), F32),
        scratch_shapes=[pltpu.VMEM((R, LANE), F32)],
        compiler_params=_params(("arbitrary",)),
    )(x, tri)


def _pairs(nb, by_key):
    if by_key:
        pr = [(i, j) for j in range(nb) for i in range(j, nb)]
    else:
        pr = [(i, j) for i in range(nb) for j in range(i + 1)]
    return (jnp.asarray([p[0] for p in pr], jnp.int32), jnp.asarray([p[1] for p in pr], jnp.int32), len(pr))


class _AttT:
    def __init__(self, S, n_heads, q, ks, v, scale, chunk_causal, cum_rep=None):
        self.S, self.H, self.q, self.ks, self.v = S, n_heads, q, ks, v
        self.scale, self.chunk_causal, self.cum_rep = scale, chunk_causal, cum_rep
        self.T = _tile(S, ATT_T)
        self.qs = min(QSUB, self.T)
        self.nb = S // self.T
        self.dq, self.dv = q[1], v[1]
        self.has_bias = cum_rep is not None

    def q_spec(self, op):
        _, w, off, per_head = op
        return pl.BlockSpec((self.T, w), lambda h, p, it, jt: (it[p], off + (h if per_head else 0)))

    def k_spec(self, op):
        _, w, off, per_head = op
        return pl.BlockSpec((self.T, w), lambda h, p, it, jt: (jt[p], off + (h if per_head else 0)))

    def row_q(self):
        return pl.BlockSpec((None, 1, self.T), lambda h, p, it, jt: (h, 0, it[p]))

    def cum_k(self):
        return pl.BlockSpec((None, self.T, self.qs), lambda h, p, it, jt: (h, jt[p], 0))

    def sub_blocks(self, masked):
        return [(q0, min(self.T, q0 + self.qs) if masked else self.T) for q0 in range(0, self.T, self.qs)]

    def scores(self, k, q_sub, cum, q0, masked):
        s = lax.dot_general(k, q_sub, _NT, preferred_element_type=F32)
        if self.has_bias:
            s = s - cum
        mask = None
        if masked:
            r = lax.broadcasted_iota(jnp.int32, s.shape, 0)
            c = lax.broadcasted_iota(jnp.int32, s.shape, 1) + q0
            mask = (r // CHUNK <= c // CHUNK) if self.chunk_causal else (r <= c)
        return s, mask


def _join(k_refs):
    return k_refs[0][...] if len(k_refs) == 1 else jnp.concatenate([r[...] for r in k_refs], axis=-1)


def _att_fwd_t(att, name, exact=False):
    S, H, T, qs = att.S, att.H, att.T, att.qs
    it, jt, npairs = _pairs(att.nb, by_key=False)
    nk = len(att.ks)

    def body(it_ref, jt_ref, *refs):
        q_ref = refs[0]
        k_refs = refs[1:1 + nk]
        v_ref = refs[1 + nk]
        n = 2 + nk
        cum_ref = None
        if att.has_bias:
            cum_ref = refs[n]
            n += 1
        o_ref = refs[n]
        n += 1
        ox_ref = None
        if exact:
            ox_ref = refs[n]
            n += 1
        lse_ref, m_ref, l_ref, acc_ref = refs[n:n + 4]
        lo_ref = refs[n + 4] if exact else None
        p = pl.program_id(1)
        i, j = it_ref[p], jt_ref[p]

        @pl.when(j == 0)
        def _():
            m_ref[...] = jnp.full_like(m_ref, -jnp.inf)
            l_ref[...] = jnp.zeros_like(l_ref)
            acc_ref[...] = jnp.zeros_like(acc_ref)
            if exact:
                lo_ref[...] = jnp.zeros_like(lo_ref)

        def step(masked):
            k = _join(k_refs)
            v = v_ref[...]
            subs = att.sub_blocks(masked)

            def logits(idx):
                q0, nkeys = subs[idx]
                cum = cum_ref[0:nkeys, :] if att.has_bias else None
                return att.scores(k[0:nkeys], q_ref[q0:q0 + qs, :], cum, q0, masked)

            ahead = logits(0)
            for idx, (q0, nkeys) in enumerate(subs):
                qsl = slice(q0, q0 + qs)
                s, mask = ahead
                if idx + 1 < len(subs):
                    ahead = logits(idx + 1)
                if masked:
                    s = jnp.where(mask, s, -jnp.inf)
                m_prev = m_ref[:, qsl]
                m_new = jnp.maximum(m_prev, jnp.max(s, axis=0, keepdims=True))
                alpha = jnp.exp2(m_prev - m_new)
                pr = jnp.exp2(s - m_new)
                l_ref[:, qsl] = alpha * l_ref[:, qsl] + jnp.sum(pr, axis=0, keepdims=True)
                p_hi = pr.astype(BF16)
                acc_ref[:, qsl] = alpha * acc_ref[:, qsl] + lax.dot_general(
                    v[0:nkeys], p_hi, _TN, preferred_element_type=F32)
                if exact:
                    p_lo = (pr - p_hi.astype(F32)).astype(BF16)
                    lo_ref[:, qsl] = alpha * lo_ref[:, qsl] + lax.dot_general(
                        v[0:nkeys], p_lo, _TN, preferred_element_type=F32)
                m_ref[:, qsl] = m_new

        @pl.when(j < i)
        def _():
            step(False)

        @pl.when(j == i)
        def _():
            step(True)
            l = l_ref[...]
            inv = 1.0 / l
            o_ref[...] = jnp.transpose(acc_ref[...] * inv).astype(o_ref.dtype)
            if exact:
                ox_ref[...] = jnp.transpose((acc_ref[...] + lo_ref[...]) * inv)
            lse_ref[...] = m_ref[...] + jnp.log2(l)

    in_specs = [att.q_spec(att.q)] + [att.k_spec(k) for k in att.ks] + [att.k_spec(att.v)]
    args = [att.q[0]] + [k[0] for k in att.ks] + [att.v[0]]
    if att.has_bias:
        in_specs.append(att.cum_k())
        args.append(att.cum_rep)
    o_spec = pl.BlockSpec((T, att.dv), lambda h, p, it, jt: (it[p], h))
    out_specs = [o_spec]
    out_shape = [jax.ShapeDtypeStruct((S, H * att.dv), BF16)]
    scratch = [pltpu.VMEM((1, T), F32), pltpu.VMEM((1, T), F32), pltpu.VMEM((att.dv, T), F32)]
    if exact:
        out_specs.append(o_spec)
        out_shape.append(jax.ShapeDtypeStruct((S, H * att.dv), F32))
        scratch.append(pltpu.VMEM((att.dv, T), F32))
    out_specs.append(att.row_q())
    out_shape.append(jax.ShapeDtypeStruct((H, 1, S), F32))
    return pl.pallas_call(
        body, name=name,
        grid_spec=pltpu.PrefetchScalarGridSpec(
            num_scalar_prefetch=2, grid=(H, npairs), in_specs=in_specs, out_specs=out_specs,
            scratch_shapes=scratch),
        out_shape=out_shape,
        compiler_params=_params(("parallel", "arbitrary")),
    )(it, jt, *args)


def _att_delta_t(do, o, n_heads, name, order=None):
    S = do.shape[0]
    w = do.shape[1] // n_heads
    ts = _tile(S, ATT_T)
    ones = jnp.ones((8, w), BF16)
    extra = [] if order is None else [order]

    def body(do_ref, o_ref, ones_ref, *rest):
        d_ref = rest[-1]
        prod = do_ref[...].astype(F32) * o_ref[...].astype(F32)
        acc = jnp.zeros((8, ts), F32)
        for part in _split3(prod):
            acc = acc + lax.dot_general(ones_ref[...], part, _NT, preferred_element_type=F32)
        d_ref[...] = acc[0:1, :]

    blk = pl.BlockSpec((ts, w), lambda i, h: (i, h))
    return pl.pallas_call(
        body, name=name, grid=(S // ts, n_heads),
        in_specs=[blk, blk, pl.BlockSpec((8, w), lambda i, h: (0, 0))] + [_ANY_SPEC] * len(extra),
        out_specs=pl.BlockSpec((None, 1, ts), lambda i, h: (h, 0, i)),
        out_shape=jax.ShapeDtypeStruct((n_heads, 1, S), F32),
        compiler_params=_params(("parallel", "parallel")),
    )(do, o, ones, *extra)


def _att_bwd_t(att, do, lse, delta, dq_dtype, dk_dtypes, name, dq_rope=None):
    S, H, T, qs = att.S, att.H, att.T, att.qs
    it, jt, npairs = _pairs(att.nb, by_key=True)
    nk = len(att.ks)
    last = att.nb - 1
    widths = [k[1] for k in att.ks]

    def body(it_ref, jt_ref, *refs):
        q_ref = refs[0]
        k_refs = refs[1:1 + nk]
        v_ref, do_ref, lse_ref, dl_ref = refs[1 + nk:5 + nk]
        n = 5 + nk
        cum_ref = None
        if att.has_bias:
            cum_ref = refs[n]
            n += 1
        rope_refs = None
        if dq_rope is not None:
            rope_refs = refs[n:n + 3]
            n += 3
        dq_ref = refs[n]
        dk_refs = refs[n + 1:n + 1 + nk]
        dv_ref = refs[n + 1 + nk]
        n += nk + 2
        dc_ref = None
        if att.has_bias:
            dc_ref = refs[n]
            n += 1
        dq_acc, dk_acc, dv_acc = refs[n:n + 3]
        dc_acc = refs[n + 3] if att.has_bias else None
        p = pl.program_id(1)
        i, j = it_ref[p], jt_ref[p]

        @pl.when(p == 0)
        def _():
            dq_acc[...] = jnp.zeros_like(dq_acc)

        @pl.when(i == j)
        def _():
            dk_acc[...] = jnp.zeros_like(dk_acc)
            dv_acc[...] = jnp.zeros_like(dv_acc)
            if att.has_bias:
                dc_acc[...] = jnp.zeros_like(dc_acc)

        def step(masked):
            k = _join(k_refs)
            v = v_ref[...]
            subs = att.sub_blocks(masked)

            def logits(idx):
                q0, nkeys = subs[idx]
                cum = cum_ref[0:nkeys, :] if att.has_bias else None
                return att.scores(k[0:nkeys], q_ref[q0:q0 + qs, :], cum, q0, masked)

            ahead = logits(0)
            for idx, (q0, nkeys) in enumerate(subs):
                qsl = slice(q0, q0 + qs)
                ksl = slice(0, nkeys)
                q_sub = q_ref[qsl, :]
                do_sub = do_ref[qsl, :]
                s, mask = ahead
                if idx + 1 < len(subs):
                    ahead = logits(idx + 1)
                pr = jnp.exp2(s - lse_ref[:, qsl])
                if masked:
                    pr = jnp.where(mask, pr, 0.0)
                dp = lax.dot_general(v[ksl], do_sub, _NT, preferred_element_type=F32)
                ds = pr * (dp - dl_ref[:, qsl])
                ds_b = ds.astype(BF16)
                dv_acc[ksl, :] += jnp.dot(pr.astype(BF16), do_sub, preferred_element_type=F32)
                dk_acc[ksl, :] += jnp.dot(ds_b, q_sub, preferred_element_type=F32)
                dq_acc[i, :, qsl] += lax.dot_general(k[ksl], ds_b, _TN, preferred_element_type=F32)
                if att.has_bias:
                    part = ds[:, 0:LANE] if qs >= LANE else ds
                    for c0 in range(LANE, qs, LANE):
                        part = part + ds[:, c0:c0 + LANE]
                    dc_acc[ksl, :] += part

        @pl.when(i > j)
        def _():
            step(False)

        @pl.when(i == j)
        def _():
            step(True)
            dq = jnp.transpose(dq_acc[i] * att.scale)
            if dq_rope is not None:
                dq = _rope(dq, rope_refs[0][...], rope_refs[1][...], rope_refs[2][...], -1)
            dq_ref[...] = dq.astype(dq_ref.dtype)

        @pl.when(i == last)
        def _():
            dk = dk_acc[...] * (1.0 / LOG2E)
            off = 0
            for r, w in zip(dk_refs, widths):
                r[...] = dk[:, off:off + w].astype(r.dtype)
                off += w
            dv_ref[...] = dv_acc[...].astype(dv_ref.dtype)
            if att.has_bias:
                dc_ref[...] = -jnp.sum(dc_acc[...], axis=-1, keepdims=True)

    do_op = (do, att.dv, 0, True)
    in_specs = ([att.q_spec(att.q)] + [att.k_spec(k) for k in att.ks]
                + [att.k_spec(att.v), att.q_spec(do_op), att.row_q(), att.row_q()])
    args = [att.q[0]] + [k[0] for k in att.ks] + [att.v[0], do, lse, delta]
    if att.has_bias:
        in_specs.append(att.cum_k())
        args.append(att.cum_rep)
    if dq_rope is not None:
        in_specs += [pl.BlockSpec((T, att.dq), lambda h, p, it, jt: (jt[p], 0))] * 3
        args += list(dq_rope)
    out_specs = [pl.BlockSpec((T, att.dq), lambda h, p, it, jt: (jt[p], h))]
    out_shape = [jax.ShapeDtypeStruct((S, H * att.dq), dq_dtype)]
    out_specs += [pl.BlockSpec((T, w), lambda h, p, it, jt: (jt[p], h)) for w in widths]
    out_shape += [jax.ShapeDtypeStruct((S, H * w), dt) for w, dt in zip(widths, dk_dtypes)]
    out_specs.append(pl.BlockSpec((T, att.dv), lambda h, p, it, jt: (jt[p], h)))
    out_shape.append(jax.ShapeDtypeStruct((S, H * att.dv), BF16))
    scratch = [pltpu.VMEM((att.nb, att.dq, T), F32), pltpu.VMEM((T, att.dq), F32), pltpu.VMEM((T, att.dv), F32)]
    if att.has_bias:
        out_specs.append(pl.BlockSpec((None, T, 1), lambda h, p, it, jt: (h, jt[p], 0)))
        out_shape.append(jax.ShapeDtypeStruct((H, S, 1), F32))
        scratch.append(pltpu.VMEM((T, min(qs, LANE)), F32))
    return pl.pallas_call(
        body, name=name,
        grid_spec=pltpu.PrefetchScalarGridSpec(
            num_scalar_prefetch=2, grid=(H, npairs), in_specs=in_specs, out_specs=out_specs,
            scratch_shapes=scratch),
        out_shape=out_shape,
        compiler_params=_params(("parallel", "arbitrary")),
    )(it, jt, *args)


def _adamw(w, g1, g2, m, v, name):
    _, K, N = w.shape
    assert g1.shape == (K, N) and g2.shape == (K, N), name
    by_rows = K % 8 == 0
    tr = _tile(K, 256, 8) if by_rows else K
    tc = N if by_rows else _tile(N, LANE)
    c1 = 1.0 - ADAM_B1 ** ADAM_STEP
    c2 = 1.0 - ADAM_B2 ** ADAM_STEP

    def body(w_ref, g1_ref, g2_ref, m_ref, v_ref, g_ref, d_ref, nm_ref, nv_ref):
        gv = g1_ref[...] + g2_ref[...]
        nm = ADAM_B1 * m_ref[...] + (1.0 - ADAM_B1) * gv
        nv = ADAM_B2 * v_ref[...] + (1.0 - ADAM_B2) * (gv * gv)
        g_ref[...] = gv
        d_ref[...] = -ADAM_LR * ((nm / c1) / (jnp.sqrt(nv / c2) + ADAM_EPS) + ADAM_WD * w_ref[...])
        nm_ref[...] = nm
        nv_ref[...] = nv

    if by_rows:
        blk = pl.BlockSpec((None, tr, N), lambda i: (0, i, 0))
        gblk = pl.BlockSpec((tr, N), lambda i: (i, 0))
    else:
        blk = pl.BlockSpec((None, K, tc), lambda i: (0, 0, i))
        gblk = pl.BlockSpec((K, tc), lambda i: (0, i))
    return pl.pallas_call(
        body, name=name, grid=(K // tr if by_rows else N // tc,),
        in_specs=[blk, gblk, gblk, blk, blk], out_specs=[blk] * 4,
        out_shape=[jax.ShapeDtypeStruct((1, K, N), F32)] * 4,
        compiler_params=_params(("parallel",)),
    )(w, g1, g2, m, v)


_HBM_SPEC = pl.BlockSpec(memory_space=pltpu.HBM)
_SEM_SPEC = pl.BlockSpec(memory_space=pltpu.SEMAPHORE)
_VMEM_SPEC = pl.BlockSpec(memory_space=pltpu.VMEM)
_EFFECT = pltpu.SideEffectType.DATAFLOW_SIDE_EFFECTING


def _place():
    return lax.axis_index("x"), lax.axis_index("y"), lax.axis_index("c")


def _other_chips(x, y):
    return [(1 - x, y), (x, 1 - y), (1 - x, 1 - y)]


def _all_gather_halves(wp, name):
    R, C = wp.shape
    half = R // 2
    assert half % 16 == 0

    def body(w_ref, out_ref, ici_send, ici_recv, d2d_send, d2d_recv, local_sem):
        x, y, c = _place()
        me = 2 * x + y
        chips = _other_chips(x, y)
        mine = pl.ds(pl.multiple_of(c * half, 16), half)
        theirs = pl.ds(pl.multiple_of((1 - c) * half, 16), half)
        local = pltpu.make_async_copy(w_ref, out_ref.at[me], local_sem)
        local.start()
        sends = []
        for n, (px, py) in enumerate(chips):
            cp = pltpu.make_async_remote_copy(
                src_ref=w_ref.at[mine], dst_ref=out_ref.at[me, mine], send_sem=ici_send.at[n],
                recv_sem=ici_recv.at[n], device_id=(px, py, c), device_id_type=MESH)
            cp.start()
            sends.append(cp)
        for n, (px, py) in enumerate(chips):
            slot = 2 * px + py
            pltpu.make_async_remote_copy(
                src_ref=w_ref.at[mine], dst_ref=out_ref.at[slot, mine], send_sem=ici_send.at[n],
                recv_sem=ici_recv.at[n], device_id=(px, py, c), device_id_type=MESH).wait_recv()
            cp = pltpu.make_async_remote_copy(
                src_ref=out_ref.at[slot, mine], dst_ref=out_ref.at[slot, mine], send_sem=d2d_send.at[n],
                recv_sem=d2d_recv.at[n], device_id=(x, y, 1 - c), device_id_type=MESH)
            cp.start()
            sends.append(cp)
        for n, (px, py) in enumerate(chips):
            slot = 2 * px + py
            pltpu.make_async_remote_copy(
                src_ref=out_ref.at[slot, theirs], dst_ref=out_ref.at[slot, theirs], send_sem=d2d_send.at[n],
                recv_sem=d2d_recv.at[n], device_id=(x, y, 1 - c), device_id_type=MESH).wait_recv()
        for cp in sends:
            cp.wait_send()
        local.wait()

    return pl.pallas_call(
        body, name=name,
        in_specs=[_ANY_SPEC], out_specs=_ANY_SPEC,
        out_shape=jax.ShapeDtypeStruct((N_CHIPS, R, C), wp.dtype),
        scratch_shapes=[pltpu.SemaphoreType.DMA((3,)), pltpu.SemaphoreType.DMA((3,)), pltpu.SemaphoreType.DMA((3,)),
                        pltpu.SemaphoreType.DMA((3,)), pltpu.SemaphoreType.DMA],
    )(wp)


def _chip_copies(src_ref, land_ref, sems, gather):
    x, y, c = _place()
    me = 2 * x + y
    out, back = [], []
    for n, (px, py) in enumerate(_other_chips(x, y)):
        src = src_ref if gather else src_ref.at[2 * px + py]
        out.append(pltpu.make_async_remote_copy(
            src_ref=src, dst_ref=land_ref.at[me] if gather else land_ref.at[n],
            send_sem=sems[n], recv_sem=sems[3 + n], device_id=(px, py, c), device_id_type=MESH))
        back.append(pltpu.make_async_remote_copy(
            src_ref=src, dst_ref=land_ref.at[2 * px + py] if gather else land_ref.at[n],
            send_sem=sems[n], recv_sem=sems[3 + n], device_id=(px, py, c), device_id_type=MESH))
    return out, back


def _xchg_start(src, land, gather, order, name):
    def body(src_ref, land_ref, order_ref, *outs):
        sems = outs[0:6]
        token = outs[8]
        out, _ = _chip_copies(src_ref, land_ref, sems, gather)
        for cp in out:
            cp.start()
        token[...] = jnp.zeros_like(token)

    outs = pl.pallas_call(
        body, name=name,
        out_shape=(pltpu.SemaphoreType.DMA(()),) * 6 + (
            pltpu.HBM(src.shape, src.dtype), pltpu.HBM(land.shape, land.dtype),
            jax.ShapeDtypeStruct((8, LANE), F32)),
        in_specs=(_HBM_SPEC, _HBM_SPEC, _ANY_SPEC),
        out_specs=(_SEM_SPEC,) * 6 + (_HBM_SPEC, _HBM_SPEC, _VMEM_SPEC),
        input_output_aliases={0: 6, 1: 7},
        compiler_params=pltpu.CompilerParams(has_side_effects=_EFFECT),
    )(pltpu.with_memory_space_constraint(src, pltpu.HBM), pltpu.with_memory_space_constraint(land, pltpu.HBM), order)
    return outs[0:6], outs[6], outs[7], outs[8]


def _xchg_wait(started, gather, after, name):
    sems, src, land, _ = started

    def body(src_ref, land_ref, *rest):
        _, back = _chip_copies(src_ref, land_ref, rest[0:6], gather)
        for cp in back:
            cp.wait_send()
            cp.wait_recv()

    return pl.pallas_call(
        body, name=name,
        out_shape=(pltpu.HBM(src.shape, src.dtype), pltpu.HBM(land.shape, land.dtype)),
        in_specs=(_HBM_SPEC, _HBM_SPEC) + (_SEM_SPEC,) * 6 + (_ANY_SPEC,),
        out_specs=(_HBM_SPEC, _HBM_SPEC),
        input_output_aliases={0: 0, 1: 1},
        compiler_params=pltpu.CompilerParams(has_side_effects=_EFFECT),
    )(src, land, *sems, after)


def _sib_copy(src_ref, land_ref, send_sem, recv_sem):
    x, y, c = _place()
    return pltpu.make_async_remote_copy(src_ref=src_ref, dst_ref=land_ref, send_sem=send_sem, recv_sem=recv_sem,
                                        device_id=(x, y, 1 - c), device_id_type=MESH)


def _sib_start(src, name):
    land = lax.empty(src.shape, src.dtype)

    def body(src_ref, land_ref, send_sem, recv_sem, src_thru, land_thru, token):
        _sib_copy(src_ref, land_ref, send_sem, recv_sem).start()
        token[...] = jnp.zeros_like(token)

    return pl.pallas_call(
        body, name=name,
        out_shape=(pltpu.SemaphoreType.DMA(()), pltpu.SemaphoreType.DMA(()),
                   pltpu.HBM(src.shape, src.dtype), pltpu.HBM(land.shape, land.dtype),
                   jax.ShapeDtypeStruct((8, LANE), F32)),
        in_specs=(_HBM_SPEC, _HBM_SPEC),
        out_specs=(_SEM_SPEC, _SEM_SPEC, _HBM_SPEC, _HBM_SPEC, _VMEM_SPEC),
        input_output_aliases={0: 2, 1: 3},
        compiler_params=pltpu.CompilerParams(has_side_effects=_EFFECT),
    )(pltpu.with_memory_space_constraint(src, pltpu.HBM), pltpu.with_memory_space_constraint(land, pltpu.HBM))


def _sib_wait(started, after, name):
    send_sem, recv_sem, src, land, _ = started

    def body(src_ref, land_ref, send_sem, recv_sem, after_ref, src_out, land_out):
        cp = _sib_copy(src_ref, land_ref, send_sem, recv_sem)
        cp.wait_send()
        cp.wait_recv()

    return pl.pallas_call(
        body, name=name,
        out_shape=(pltpu.HBM(src.shape, src.dtype), pltpu.HBM(land.shape, land.dtype)),
        in_specs=(_HBM_SPEC, _HBM_SPEC, _SEM_SPEC, _SEM_SPEC, _ANY_SPEC),
        out_specs=(_HBM_SPEC, _HBM_SPEC),
        input_output_aliases={0: 0, 1: 1},
        compiler_params=pltpu.CompilerParams(has_side_effects=_EFFECT),
    )(src, land, send_sem, recv_sem, after)


def _sum_slabs(gp, recv, chip, name):
    _, R, C = gp.shape
    tr = _tile(R, PACK_ROWS, 16)

    def body(chip_ref, own_ref, r0_ref, r1_ref, r2_ref, o_ref):
        acc = own_ref[...].astype(F32) + r0_ref[...].astype(F32)
        o_ref[...] = (acc + r1_ref[...].astype(F32)) + r2_ref[...].astype(F32)

    def got(n):
        return pl.BlockSpec((None, tr, C), lambda i, chip_ref: (n, i, 0))

    return pl.pallas_call(
        body, name=name,
        grid_spec=pltpu.PrefetchScalarGridSpec(
            num_scalar_prefetch=1, grid=(R // tr,),
            in_specs=[pl.BlockSpec((None, tr, C), lambda i, chip_ref: (chip_ref[0], i, 0)), got(0), got(1), got(2)],
            out_specs=pl.BlockSpec((tr, C), lambda i, chip_ref: (i, 0))),
        out_shape=jax.ShapeDtypeStruct((R, C), F32),
        compiler_params=_params(("parallel",)),
    )(jnp.reshape(chip, (1,)).astype(jnp.int32), gp, recv, recv, recv)


def _all_reduce_vec(vec, name):
    VR, W = vec.shape

    def body(vec_ref, vall_ref, vout_ref, vsend_sems, vrecv_sems):
        x, y, c = _place()
        vall_ref[4 * x + 2 * y + c] = vec_ref[...]
        sends = []
        peers = []
        for r in range(1, N_DEV):
            dx, dy, dc = (r >> 2) & 1, (r >> 1) & 1, r & 1
            peer = (x ^ dx, y ^ dy, c ^ dc)
            peers.append(peer)
            cp = pltpu.make_async_remote_copy(
                src_ref=vec_ref, dst_ref=vall_ref.at[4 * x + 2 * y + c], send_sem=vsend_sems.at[r - 1],
                recv_sem=vrecv_sems.at[r - 1], device_id=peer, device_id_type=MESH)
            cp.start()
            sends.append(cp)
        for r, peer in enumerate(peers):
            pltpu.make_async_remote_copy(
                src_ref=vec_ref, dst_ref=vall_ref.at[4 * peer[0] + 2 * peer[1] + peer[2]],
                send_sem=vsend_sems.at[r], recv_sem=vrecv_sems.at[r],
                device_id=peer, device_id_type=MESH).wait_recv()
        total = vall_ref[0]
        for d in range(1, N_DEV):
            total = total + vall_ref[d]
        vout_ref[...] = total
        for cp in sends:
            cp.wait_send()

    outs = pl.pallas_call(
        body, name=name,
        in_specs=[_VMEM_SPEC], out_specs=[_VMEM_SPEC, _VMEM_SPEC],
        out_shape=[jax.ShapeDtypeStruct((N_DEV, VR, W), F32), jax.ShapeDtypeStruct((VR, W), F32)],
        scratch_shapes=[pltpu.SemaphoreType.DMA((N_DEV - 1,)), pltpu.SemaphoreType.DMA((N_DEV - 1,))],
    )(vec)
    return outs[1]


class _Pack:
    def __init__(self, group, C):
        self.group, self.C = group, C
        self.rows, self.offs, off = {}, {}, 0
        for nm, (K, N), _ in group:
            assert N <= C, nm
            self.rows[nm] = K if 2 * N > C else -(-(K * N) // C)
            self.offs[nm] = off
            off += -(-self.rows[nm] // 16) * 16
        self.used = off
        self.R = -(-off // PACK_ROWS) * PACK_ROWS

    def _rows_of(self, a):
        K, N = a.shape
        if 2 * N > self.C:
            a = jnp.pad(a, ((0, 0), (0, self.C - N)))
        else:
            a = jnp.pad(a.reshape(-1), (0, -(K * N) % self.C)).reshape(-1, self.C)
        return jnp.pad(a, ((0, -a.shape[0] % 16), (0, 0)))

    def pack(self, shards):
        parts = [self._rows_of(shards[nm].astype(BF16)) for nm, _, _ in self.group]
        return jnp.concatenate(parts + [jnp.zeros((self.R - self.used, self.C), BF16)], axis=0)

    def part(self, flat, nm, shape):
        K, N = shape
        rows = flat[self.offs[nm]:self.offs[nm] + self.rows[nm]]
        return rows[:, :N] if 2 * N > self.C else rows.reshape(-1)[:K * N].reshape(K, N)

    def slab_rows(self, nm, g):
        (K, N), axis = next((shape, axis) for n, shape, axis in self.group if n == nm)
        cuts = [g[:, k * N:(k + 1) * N] if axis == 1 else g[k * K:(k + 1) * K, :] for k in range(N_CHIPS)]
        return jnp.stack([self._rows_of(c.astype(BF16)) for c in cuts])

    def slabs(self, grads):
        parts = [self.slab_rows(nm, grads[nm]) for nm, _, _ in self.group]
        return jnp.concatenate(parts + [jnp.zeros((N_CHIPS, self.R - self.used, self.C), BF16)], axis=1)

    def full(self, gathered):
        res = {}
        for nm, (K, N), axis in self.group:
            parts = [self.part(gathered[k], nm, (K, N)) for k in range(N_CHIPS)]
            res[nm] = jnp.concatenate(parts, axis=axis)
        return res


def _rope_tables(S):
    pos = jnp.arange(S, dtype=F32)
    inv = 1.0 / (ROPE_THETA ** (jnp.arange(0, MLA_ROPE, 2, dtype=F32) / MLA_ROPE))
    ang = pos[:, None] * inv[None, :]
    cos, sin = jnp.cos(ang), jnp.sin(ang)
    half = MLA_ROPE // 2
    z = jnp.zeros((S, half), F32)
    one = jnp.ones((S, LANE - MLA_ROPE), F32)
    zero = jnp.zeros((S, LANE - MLA_ROPE), F32)
    kc = jnp.concatenate([cos, cos, one], axis=1)
    ksa = jnp.concatenate([-sin, z, zero], axis=1)
    ksb = jnp.concatenate([z, sin, zero], axis=1)
    qc = jnp.concatenate([jnp.ones((S, MLA_NOPE), F32), kc], axis=1)
    qsa = jnp.concatenate([jnp.zeros((S, MLA_NOPE), F32), ksa], axis=1)
    qsb = jnp.concatenate([jnp.zeros((S, MLA_NOPE), F32), ksb], axis=1)
    return (kc, ksa, ksb), (qc, qsa, qsb)


def _pad_cols(a, width):
    return jnp.pad(a, ((0, 0), (0, width - a.shape[1])))


def kernel(x, attn_norm, w_in, fox_f_bias, q_norm, w_uq, kv_norm, w_ukv, w_mla_branch, w_fox_branch, w_out, mlp_norm, w_up, w_down, final_norm, loss_target, m_attn_norm, m_w_in, m_fox_f_bias, m_q_norm, m_w_uq, m_kv_norm, m_w_ukv, m_w_mla_branch, m_w_fox_branch, m_w_out, m_mlp_norm, m_w_up, m_w_down, m_final_norm, v_attn_norm, v_w_in, v_fox_f_bias, v_q_norm, v_w_uq, v_kv_norm, v_w_ukv, v_w_mla_branch, v_w_fox_branch, v_w_out, v_mlp_norm, v_w_up, v_w_down, v_final_norm):
    _, S, D = x.shape
    H, HF = MLA_HEADS, FOX_HEADS
    QL, KVL = MLA_Q_LORA, MLA_KV_LORA
    assert H == HF and H <= 8
    xs = x[0]
    target = loss_target[0]
    C = D
    chip = 2 * lax.axis_index("x") + lax.axis_index("y")

    def flip(a):
        return jnp.transpose(a, (0, 2, 1))

    w_in, m_w_in, v_w_in = flip(w_in), flip(m_w_in), flip(v_w_in)
    weights = {"attn_norm": attn_norm, "w_in": w_in, "fox_f_bias": fox_f_bias, "q_norm": q_norm, "w_uq": w_uq,
               "kv_norm": kv_norm, "w_ukv": w_ukv, "w_mla_branch": w_mla_branch, "w_fox_branch": w_fox_branch,
               "w_out": w_out, "mlp_norm": mlp_norm, "w_up": w_up, "w_down": w_down, "final_norm": final_norm}
    moments = {"attn_norm": (m_attn_norm, v_attn_norm), "w_in": (m_w_in, v_w_in), "fox_f_bias": (m_fox_f_bias, v_fox_f_bias),
               "q_norm": (m_q_norm, v_q_norm), "w_uq": (m_w_uq, v_w_uq), "kv_norm": (m_kv_norm, v_kv_norm),
               "w_ukv": (m_w_ukv, v_w_ukv), "w_mla_branch": (m_w_mla_branch, v_w_mla_branch),
               "w_fox_branch": (m_w_fox_branch, v_w_fox_branch), "w_out": (m_w_out, v_w_out),
               "mlp_norm": (m_mlp_norm, v_mlp_norm), "w_up": (m_w_up, v_w_up), "w_down": (m_w_down, v_w_down),
               "final_norm": (m_final_norm, v_final_norm)}

    def group(names_axes):
        return [(nm, weights[nm].shape[1:], axis) for nm, axis in names_axes]

    pack_a = _Pack(group([("w_in", 0), ("w_uq", 1), ("w_ukv", 1)]), C)
    pack_b = _Pack(group([("w_down", 0), ("w_up", 1), ("w_out", 0), ("w_mla_branch", 1), ("w_fox_branch", 1)]), C)
    RA, RB = pack_a.R, pack_b.R
    wp_a = pack_a.pack({nm: weights[nm][0] for nm, _, _ in pack_a.group})
    wp_b = pack_b.pack({nm: weights[nm][0] for nm, _, _ in pack_b.group})
    gathered_a = _all_gather_halves(wp_a, "all_gather_a")
    ag_b = _xchg_start(wp_b, lax.empty((N_CHIPS, RB, C), BF16), True, gathered_a, "all_gather_start_b")
    xn = _norm_fwd(xs, attn_norm, "attn_norm_fwd", order=ag_b[3])
    full = pack_a.full(gathered_a)

    o_ckv = QL
    o_kr = o_ckv + KVL
    o_fq = o_kr + MLA_ROPE
    o_ff = o_fq + 3 * HF * FOX_HEAD_DIM
    o_g = o_ff + HF
    wi = full["w_in"]
    assert wi.shape[0] == o_g + 2 * D
    WS = QL + KVL + 2 * LANE
    NQKV = 3 * HF * FOX_HEAD_DIM

    def pad_rows(a, rows):
        return jnp.pad(a, ((0, rows - a.shape[0]), (0, 0)))

    w_small = jnp.concatenate([wi[:o_kr], pad_rows(wi[o_kr:o_fq], LANE), pad_rows(wi[o_ff:o_g], LANE)], axis=0)
    w_qkv = wi[o_fq:o_ff]
    w_g = wi[o_g:]
    w_pack = jnp.concatenate([w_small, w_qkv, w_g], axis=0)
    dqk = MLA_NOPE + MLA_ROPE
    w_uq_p = jnp.pad(full["w_uq"].reshape(QL, H, dqk), ((0, 0), (0, 0), (0, QPAD - dqk))).reshape(QL, H * QPAD)
    ukv = full["w_ukv"].reshape(KVL, H, MLA_NOPE + MLA_V)
    w_ukv_p = jnp.concatenate([ukv[:, :, :MLA_NOPE].reshape(KVL, H * MLA_NOPE),
                               ukv[:, :, MLA_NOPE:].reshape(KVL, H * MLA_V)], axis=1)

    (kc, ksa, ksb), (qc, qsa, qsb) = _rope_tables(S)
    bias_pad = _pad_cols(fox_f_bias, LANE)

    small = _matmul(xn, w_small, "nt", [F32], "proj_small")
    n_fq = HF * FOX_HEAD_DIM
    q_scale = jnp.concatenate([jnp.full((1, n_fq), LOG2E / math.sqrt(FOX_HEAD_DIM), F32),
                               jnp.ones((1, NQKV - n_fq), F32)], axis=1)
    qkv = _matmul(xn, w_qkv, "nt", [BF16], "proj_qkv", col_extras=(q_scale,), epilogue=lambda acc, cs: (acc * cs,))
    gpre = _matmul(xn, w_g, "nt", [F32], "proj_gates")
    cqn, ckvn, kr, cum = _prep_fwd(small, q_norm, kv_norm, bias_pad, kc, ksa, ksb, HF, "prep_fwd")
    c2_mla = LOG2E / math.sqrt(dqk)
    q_rot = _matmul(cqn, w_uq_p, "nn", [BF16], "mla_q_up", tn=QPAD, row_extras=(qc * c2_mla, qsa * c2_mla, qsb * c2_mla),
                    epilogue=lambda acc, c, sa, sb: (_rope(acc, c, sa, sb, 1),))
    kv2 = _matmul(ckvn, w_ukv_p, "nn", [BF16], "mla_kv_up")

    mla = _AttT(S, H, (q_rot, QPAD, 0, True), [(kv2, MLA_NOPE, 0, True), (kr, LANE, 0, False)],
                (kv2, MLA_V, H, True), 1.0 / math.sqrt(dqk), True)
    o_mla, lse_mla = _att_fwd_t(mla, "mla_att_fwd")

    cum_t = jnp.transpose(cum[:, :HF]) * LOG2E
    cum_rep = jnp.broadcast_to(cum_t[:, :, None], (HF, S, min(QSUB, _tile(S, ATT_T))))
    fox = _AttT(S, HF, (qkv, FOX_HEAD_DIM, 0, True), [(qkv, FOX_HEAD_DIM, HF, True)],
                (qkv, FOX_HEAD_DIM, 2 * HF, True), 1.0 / math.sqrt(FOX_HEAD_DIM), False, cum_rep)
    o_fox, ox_fox, lse_fox = _att_fwd_t(fox, "fox_att_fwd", exact=True)

    own_b, land_b = _xchg_wait(ag_b, True, lse_fox, "all_gather_wait_b")
    full.update(pack_b.full(lax.dynamic_update_slice(land_b, own_b[None], (chip, 0, 0))))
    w_mb, w_fb, w_o, w_u, w_d = (full[n] for n in ("w_mla_branch", "w_fox_branch", "w_out", "w_up", "w_down"))

    y_mla = _matmul(o_mla, w_mb, "nn", [F32], "mla_branch")

    def gate_merge(acc, ga, gb, ya):
        return acc, _sigmoid(ga) * ya + _sigmoid(gb) * acc

    y_fox, merged = _matmul(o_fox, w_fb, "nn", [F32, BF16], "fox_branch_gates", extras=((gpre, 0), (gpre, 1), y_mla),
                            epilogue=gate_merge)
    h1 = _matmul(merged, w_o, "nn", [F32], "out_proj", extras=(xs,), epilogue=lambda acc, r: (acc + r,))
    hn = _norm_fwd(h1, mlp_norm, "mlp_norm_fwd")

    def relu2(acc):
        a = jnp.maximum(acc, 0.0)
        return a * a, a

    u, a_pos = _matmul(hn, w_u, "nn", [BF16, BF16], "mlp_up", epilogue=relu2)
    h2 = _matmul(u, w_d, "nn", [F32], "mlp_down", tn=1024, extras=(h1,), epilogue=lambda acc, r: (acc + r,))
    dh2, dh2_b, g_final, loss_part = _final(h2, final_norm.reshape(1, D), target, "final_norm_loss")

    gp_b = lax.empty((N_CHIPS, RB, C), BF16)
    by_glue = {}

    def grad_b(nm, a, b, name):
        nonlocal gp_b
        (K, N), axis = next((shape, axis) for n, shape, axis in pack_b.group if n == nm)
        off = pack_b.offs[nm]
        tm = min(1024, K) if axis == 0 else min(1024, a.shape[1])
        tn = min(1024, N) if axis == 1 else min(1024, b.shape[1])
        if not (N == C and tm % LANE == 0 and tn % LANE == 0 and K % tm == 0 and N % tn == 0 and off % tm == 0):
            by_glue[nm] = _mm_tn(a, b, name)
            return
        base = off // tm
        if axis == 0:
            per = K // tm
            place = lambda i, j: (i // per, base + i % per, j)
        else:
            per = N // tn
            place = lambda i, j: (j // per, base + i, j % per)
        gp_b = _mm_tn(a, b, name, tm=tm, tn=tn, into=(gp_b, place))

    da = _matmul(dh2_b, w_d, "nt", [BF16], "mlp_down_dx", extras=(a_pos,),
                 epilogue=lambda acc, a: (acc * (2.0 * a.astype(F32)),))
    grad_b("w_down", u, dh2_b, "mlp_down_dw")
    dhn = _matmul(da, w_u, "nt", [F32], "mlp_up_dx", tn=1024)
    grad_b("w_up", hn, da, "mlp_up_dw")
    dh1, dh1_b, g_mlp_norm = _norm_bwd(h1, dhn, mlp_norm, dh2, "mlp_norm_bwd")

    def gate_bwd(acc, ga, gb, ya, yb):
        ga, gb = _sigmoid(ga), _sigmoid(gb)
        return acc * ga, acc * gb, acc * ya * (ga * (1.0 - ga)), acc * yb * (gb * (1.0 - gb))

    dy_mla, dy_fox, dg_mla, dg_fox = _matmul(dh1_b, w_o, "nt", [BF16] * 4, "out_proj_dx_gates",
                                             extras=((gpre, 0), (gpre, 1), y_mla, y_fox), epilogue=gate_bwd)
    grad_b("w_out", merged, dh1_b, "out_proj_dw")
    do_mla = _matmul(dy_mla, w_mb, "nt", [BF16], "mla_branch_dx")
    grad_b("w_mla_branch", o_mla, dy_mla, "mla_branch_dw")
    do_fox = _matmul(dy_fox, w_fb, "nt", [BF16], "fox_branch_dx")
    grad_b("w_fox_branch", o_fox, dy_fox, "fox_branch_dw")
    for nm, g in by_glue.items():
        gp_b = lax.dynamic_update_slice(gp_b, pack_b.slab_rows(nm, g), (0, pack_b.offs[nm], 0))
    if RB > pack_b.used:
        gp_b = lax.dynamic_update_slice(gp_b, jnp.zeros((N_CHIPS, RB - pack_b.used, C), BF16), (0, pack_b.used, 0))

    rs_b = _xchg_start(gp_b, lax.empty((3, RB, C), BF16), False, do_fox, "grad_scatter_start_b")

    delta_mla = _att_delta_t(do_mla, o_mla, H, "mla_att_delta", order=rs_b[3])
    dq_rot, dk_nope, dkr_heads, dv_mla = _att_bwd_t(mla, do_mla, lse_mla, delta_mla, BF16, [BF16, F32],
                                                    "mla_att_bwd", dq_rope=(qc, qsa, qsb))
    delta_fox = _att_delta_t(do_fox, ox_fox, HF, "fox_att_delta")
    dfq, dfk, dfv, dcum = _att_bwd_t(fox, do_fox, lse_fox, delta_fox, BF16, [BF16], "fox_att_bwd")

    gp_b_sent, recv_b = _xchg_wait(rs_b, False, dfq, "grad_scatter_wait_b")
    swap_b = _sib_start(_sum_slabs(gp_b_sent, recv_b, chip, "grad_sum_b"), "grad_swap_start_b")

    dcqn = _matmul(dq_rot, w_uq_p, "nt", [F32], "mla_q_up_dx", order=swap_b[4])
    g_w_uq_p = _mm_tn(cqn, dq_rot, "mla_q_up_dw")
    dkv2 = jnp.concatenate([dk_nope, dv_mla], axis=1)
    dckvn = _matmul(dkv2, w_ukv_p, "nt", [F32], "mla_kv_up_dx")
    g_w_ukv_p = _mm_tn(ckvn, dkv2, "mla_kv_up_dw")

    dcum_rows = jnp.pad(dcum[:, :, 0], ((0, 8 - HF), (0, 0)))
    dlogf_rows = _suffix_sum_rows(dcum_rows, "fox_forget_suffix_sum")
    dlogf = _pad_cols(jnp.transpose(dlogf_rows[:HF]), LANE)
    d_small, g_q_norm, g_kv_norm, g_bias = _prep_bwd(
        small, dcqn, dckvn, dkr_heads, dlogf, q_norm, kv_norm, bias_pad, kc, ksa, ksb, H, "prep_bwd")
    dproj = jnp.concatenate([d_small, dfq, dfk, dfv, dg_mla, dg_fox], axis=1)
    g_w_pack = _mm_tn(dproj, xn, "proj_dw")

    gs, gq, gg = g_w_pack[:WS], g_w_pack[WS:WS + NQKV], g_w_pack[WS + NQKV:]
    g_w_in = jnp.concatenate([gs[:o_kr], gs[o_kr:o_kr + MLA_ROPE], gq,
                              gs[o_kr + LANE:o_kr + LANE + HF], gg], axis=0)
    g_w_uq = g_w_uq_p.reshape(QL, H, QPAD)[:, :, :dqk].reshape(QL, H * dqk)
    g_w_ukv = jnp.concatenate([g_w_ukv_p[:, :H * MLA_NOPE].reshape(KVL, H, MLA_NOPE),
                               g_w_ukv_p[:, H * MLA_NOPE:].reshape(KVL, H, MLA_V)], axis=2).reshape(KVL, -1)

    gp_a = pack_a.slabs({"w_in": g_w_in, "w_uq": g_w_uq, "w_ukv": g_w_ukv})
    rs_a = _xchg_start(gp_a, lax.empty((3, RA, C), BF16), False, g_w_pack, "grad_scatter_start_a")
    dxn = _matmul(dproj, w_pack, "nn", [F32], "proj_dx", tn=1024, order=rs_a[3])
    grad_x, _, g_attn_norm = _norm_bwd(xs, dxn, attn_norm, dh1, "attn_norm_bwd")
    gp_a_sent, recv_a = _xchg_wait(rs_a, False, grad_x, "grad_scatter_wait_a")
    swap_a = _sib_start(_sum_slabs(gp_a_sent, recv_a, chip, "grad_sum_a"), "grad_swap_start_a")
    vec_w = max(D, LANE)
    vec_rows = [g_attn_norm, g_mlp_norm, g_final, g_q_norm, g_kv_norm, g_bias, loss_part]
    vec = jnp.concatenate([_pad_cols(v, vec_w) for v in vec_rows] + [jnp.zeros((1, vec_w), F32)], axis=0)
    vsum = _all_reduce_vec(vec, "all_reduce_vectors")
    part_b, sib_b = _sib_wait(swap_b, vsum, "grad_swap_wait_b")

    grads, deltas, new_m, new_v = {}, {}, {}, {}

    def update(pack, mine, theirs):
        for nm, shape, _ in pack.group:
            g, d, nm_, nv_ = _adamw(weights[nm], pack.part(mine, nm, shape), pack.part(theirs, nm, shape),
                                    moments[nm][0], moments[nm][1], "adamw_" + nm)
            grads[nm], deltas[nm], new_m[nm], new_v[nm] = g, d, nm_, nv_
        return g

    last_b = update(pack_b, part_b, sib_b)
    part_a, sib_a = _sib_wait(swap_a, last_b, "grad_swap_wait_a")
    update(pack_a, part_a, sib_a)

    vec_names = ["attn_norm", "mlp_norm", "final_norm", "q_norm", "kv_norm", "fox_f_bias"]

    def vec_pack(arrs):
        return jnp.concatenate([_pad_cols(a.reshape(1, -1), vec_w) for a in arrs]
                               + [jnp.zeros((2, vec_w), F32)], axis=0)[None]

    vg, vd, vm, vv = _adamw(vec_pack([weights[n] for n in vec_names]), vsum, jnp.zeros_like(vsum),
                            vec_pack([moments[n][0] for n in vec_names]), vec_pack([moments[n][1] for n in vec_names]),
                            "adamw_vectors")
    for r, nm in enumerate(vec_names):
        shp = weights[nm].shape
        n = weights[nm].size
        grads[nm] = vsum[r, :n].reshape(shp)
        deltas[nm], new_m[nm], new_v[nm] = (vd[0, r, :n].reshape(shp), vm[0, r, :n].reshape(shp),
                                            vv[0, r, :n].reshape(shp))
    loss = vsum[6, 0]

    for res in (grads, deltas, new_m, new_v):
        res["w_in"] = flip(res["w_in"])
    order = ["attn_norm", "w_in", "fox_f_bias", "q_norm", "w_uq", "kv_norm", "w_ukv", "w_mla_branch", "w_fox_branch",
             "w_out", "mlp_norm", "w_up", "w_down", "final_norm"]
    return (loss, grad_x[None], *[grads[n] for n in order], *[deltas[n] for n in order],
            *[new_m[n] for n in order], *[new_v[n] for n in order])
```

```python
import math

import jax
import jax.numpy as jnp
from jax import lax
from jax.experimental import pallas as pl
from jax.experimental.pallas import tpu as pltpu

CHUNK = 64
MLA_HEADS = 8
MLA_Q_LORA = 512
MLA_KV_LORA = 256
MLA_NOPE = 128
MLA_ROPE = 64
MLA_V = 128
ROPE_THETA = 10000.0
FOX_HEADS = 8
FOX_HEAD_DIM = 128
EPS = 1e-6

ADAM_LR = 0.001
ADAM_B1 = 0.9
ADAM_B2 = 0.999
ADAM_EPS = 1e-08
ADAM_WD = 0.01
ADAM_STEP = 10

LANE = 128
QPAD = 2 * LANE
N_CHIPS = 4
N_DEV = 8
VMEM_LIMIT = 48 * 1024 * 1024
ATT_T = 1024
QSUB = 256
ROW_T = 256
PACK_ROWS = 256
LOG2E = 1.4426950408889634

BF16 = jnp.bfloat16
F32 = jnp.float32
MESH = pl.DeviceIdType.MESH

_NT = (((1,), (1,)), ((), ()))
_TN = (((0,), (0,)), ((), ()))
_NN = (((1,), (0,)), ((), ()))


def _tile(dim, pref, align=LANE):
    if dim <= pref:
        return dim
    t = (pref // align) * align
    while t >= align:
        if dim % t == 0:
            return t
        t -= align
    return dim


def _params(sem=None):
    return pltpu.CompilerParams(dimension_semantics=sem, vmem_limit_bytes=VMEM_LIMIT)


_ANY_SPEC = pl.BlockSpec(memory_space=pl.ANY)


def _matmul(a, b, mode, out_dtypes, name, *, tm=1024, tn=512, tk=2048, extras=(), row_extras=(), col_extras=(),
            epilogue=None, order=None, into=None, b_in=None):
    b_shape = b.shape if b_in is None else b_in[0]
    if mode == "nn":
        (M, K), (K2, N) = a.shape, b_shape
    elif mode == "nt":
        (M, K), (N, K2) = a.shape, b_shape
    else:
        (K, M), (K2, N) = a.shape, b_shape
    assert K == K2, (name, a.shape, b_shape)
    tm, tn, tk = _tile(M, tm), _tile(N, tn), _tile(K, tk)
    nk = K // tk
    extras = [e if isinstance(e, tuple) else (e, 0) for e in extras]
    n_out = len(out_dtypes)
    n_ex = len(extras) + len(row_extras) + len(col_extras)
    n_ord = 0 if order is None else 1
    assert all(r.shape == (M, tn) for r in row_extras), name
    dims = {"nn": _NN, "nt": _NT, "tn": _TN}[mode]

    def body(*refs):
        a_ref, b_ref = refs[0], refs[1]
        ex_refs = refs[2:2 + n_ex]
        o_refs = refs[2 + n_ex + n_ord:2 + n_ex + n_ord + n_out]
        acc_ref = refs[2 + n_ex + n_ord + n_out]
        k = pl.program_id(2)
        part = lax.dot_general(a_ref[...], b_ref[...], dims, preferred_element_type=F32)

        @pl.when(k == 0)
        def _():
            acc_ref[...] = part

        @pl.when(k > 0)
        def _():
            acc_ref[...] += part

        @pl.when(k == nk - 1)
        def _():
            acc = acc_ref[...]
            if epilogue is None:
                outs = (acc,)
            else:
                outs = epilogue(acc, *[r[...] for r in ex_refs])
            for o_ref, o in zip(o_refs, outs):
                o_ref[...] = o.astype(o_ref.dtype)

    if mode == "nn":
        a_spec = pl.BlockSpec((tm, tk), lambda i, j, k: (i, k))
        b_spec = pl.BlockSpec((tk, tn), lambda i, j, k: (k, j))
    elif mode == "nt":
        a_spec = pl.BlockSpec((tm, tk), lambda i, j, k: (i, k))
        b_spec = pl.BlockSpec((tn, tk), lambda i, j, k: (j, k))
    else:
        a_spec = pl.BlockSpec((tk, tm), lambda i, j, k: (k, i))
        b_spec = pl.BlockSpec((tk, tn), lambda i, j, k: (k, j))
    if b_in is not None:
        b_block = (None, tn, tk) if mode == "nt" else (None, tk, tn)
        b_spec = pl.BlockSpec(b_block, lambda i, j, k: b_in[1](j, k))
    mn_spec = pl.BlockSpec((tm, tn), lambda i, j, k: (i, j))
    row_spec = pl.BlockSpec((tm, tn), lambda i, j, k: (i, 0))
    col_spec = pl.BlockSpec((1, tn), lambda i, j, k: (0, j))
    out_specs = [mn_spec] * n_out
    out_shape = [jax.ShapeDtypeStruct((M, N), dt) for dt in out_dtypes]
    aliases = {}
    if into is not None:
        buf, place = into
        assert n_out == 1 and n_ord == 1 and order is buf, name
        out_specs = [pl.BlockSpec((None, tm, tn), lambda i, j, k: place(i, j))]
        out_shape = [jax.ShapeDtypeStruct(buf.shape, buf.dtype)]
        aliases = {2 + n_ex: 0}
    outs = pl.pallas_call(
        body,
        name=name,
        grid=(M // tm, N // tn, nk),
        in_specs=([a_spec, b_spec]
                  + [pl.BlockSpec((tm, tn), lambda i, j, k, g=g: (i, j + g * (N // tn))) for _, g in extras]
                  + [row_spec] * len(row_extras) + [col_spec] * len(col_extras) + [_ANY_SPEC] * n_ord),
        out_specs=out_specs,
        out_shape=out_shape,
        scratch_shapes=[pltpu.VMEM((tm, tn), F32)],
        input_output_aliases=aliases,
        compiler_params=_params(("parallel", "parallel", "arbitrary")),
    )(a, b, *[e for e, _ in extras], *row_extras, *col_extras, *([] if order is None else [order]))
    return outs[0] if n_out == 1 else outs


def _mm_tn(a, b, name, tm=1024, tn=1024, into=None):
    return _matmul(a, b, "tn", [F32], name, tm=tm, tn=tn, tk=2048, into=into,
                   order=None if into is None else into[0])


def _row_spec(ts, width, col=0):
    return pl.BlockSpec((ts, width), lambda i: (i, col))


def _full_spec(shape):
    return pl.BlockSpec(shape, lambda i: tuple(0 for _ in shape))


def _rms(x):
    return lax.rsqrt(jnp.mean(x * x, axis=-1, keepdims=True) + EPS)


def _rms_bwd(x, dy, g):
    r = _rms(x)
    xh = x * r
    gy = dy * g
    dx = r * (gy - xh * jnp.mean(xh * gy, axis=-1, keepdims=True))
    return dx, dy * xh


def _norm_fwd(x, g, name, order=None):
    S, D = x.shape
    ts = _tile(S, ROW_T, 8)

    def body(x_ref, g_ref, *rest):
        o_ref = rest[-1]
        xv = x_ref[...]
        o_ref[...] = ((xv * _rms(xv)) * g_ref[...]).astype(BF16)

    extra = [] if order is None else [order]
    return pl.pallas_call(
        body, name=name, grid=(S // ts,),
        in_specs=[_row_spec(ts, D), _full_spec((1, D))] + [_ANY_SPEC] * len(extra),
        out_specs=_row_spec(ts, D),
        out_shape=jax.ShapeDtypeStruct((S, D), BF16),
        compiler_params=_params(("parallel",)),
    )(x, g, *extra)


def _norm_bwd(x, dy, g, dres, name):
    S, D = x.shape
    ts = _tile(S, ROW_T, 8)

    def body(x_ref, dy_ref, g_ref, dres_ref, dx_ref, dxb_ref, dg_ref):
        dx, dg_rows = _rms_bwd(x_ref[...], dy_ref[...], g_ref[...])
        dx = dres_ref[...] + dx
        dx_ref[...] = dx
        dxb_ref[...] = dx.astype(BF16)

        @pl.when(pl.program_id(0) == 0)
        def _():
            dg_ref[...] = jnp.zeros_like(dg_ref)

        dg_ref[...] += jnp.sum(dg_rows, axis=0, keepdims=True)

    return pl.pallas_call(
        body, name=name, grid=(S // ts,),
        in_specs=[_row_spec(ts, D), _row_spec(ts, D), _full_spec((1, D)), _row_spec(ts, D)],
        out_specs=[_row_spec(ts, D), _row_spec(ts, D), _full_spec((1, D))],
        out_shape=[jax.ShapeDtypeStruct((S, D), F32), jax.ShapeDtypeStruct((S, D), BF16),
                   jax.ShapeDtypeStruct((1, D), F32)],
        compiler_params=_params(("arbitrary",)),
    )(x, dy, g, dres)


def _rope(x, c, sa, sb, sign):
    w = x.shape[-1]
    half = MLA_ROPE // 2
    fwd = pltpu.roll(x, w - half, 1)
    back = pltpu.roll(x, half, 1)
    if sign < 0:
        return x * c - fwd * sa - back * sb
    return x * c + fwd * sa + back * sb


def _split3(x):
    hi = x.astype(BF16)
    r1 = x - hi.astype(F32)
    mid = r1.astype(BF16)
    lo = (r1 - mid.astype(F32)).astype(BF16)
    return hi, mid, lo


def _prep_fwd(small, q_norm, kv_norm, bias_pad, kc, ksa, ksb, n_heads, name):
    S, W = small.shape
    QL, KVL = q_norm.shape[1], kv_norm.shape[1]
    assert W == QL + KVL + 2 * LANE
    ts = _tile(S, ROW_T, 8)
    tri = (lax.broadcasted_iota(jnp.int32, (ts, ts), 0) >= lax.broadcasted_iota(jnp.int32, (ts, ts), 1)).astype(BF16)

    def body(s_ref, qn_ref, kvn_ref, b_ref, kc_ref, ksa_ref, ksb_ref, tri_ref,
             cqn_ref, ckvn_ref, kr_ref, cum_ref, carry_ref):
        cq = s_ref[:, 0:QL]
        cqn_ref[...] = ((cq * _rms(cq)) * qn_ref[...]).astype(BF16)
        ckv = s_ref[:, QL:QL + KVL]
        ckvn_ref[...] = ((ckv * _rms(ckv)) * kvn_ref[...]).astype(BF16)
        kr = s_ref[:, QL + KVL:QL + KVL + LANE]
        kr_ref[...] = _rope(kr, kc_ref[...], ksa_ref[...], ksb_ref[...], 1).astype(BF16)
        z = s_ref[:, QL + KVL + LANE:W] + b_ref[...]
        logf = jnp.minimum(z, 0.0) - jnp.log1p(jnp.exp(-jnp.abs(z)))
        lane = lax.broadcasted_iota(jnp.int32, logf.shape, 1)
        logf = jnp.where(lane < n_heads, logf, 0.0)

        @pl.when(pl.program_id(0) == 0)
        def _():
            carry_ref[...] = jnp.zeros_like(carry_ref)

        t = tri_ref[...]
        cum = carry_ref[...]
        for part in _split3(logf):
            cum = cum + jnp.dot(t, part, preferred_element_type=F32)
        cum_ref[...] = cum
        carry_ref[...] = cum[ts - 1:ts, :]

    return pl.pallas_call(
        body, name=name, grid=(S // ts,),
        in_specs=[_row_spec(ts, W), _full_spec((1, QL)), _full_spec((1, KVL)), _full_spec((1, LANE)),
                  _row_spec(ts, LANE), _row_spec(ts, LANE), _row_spec(ts, LANE), _full_spec((ts, ts))],
        out_specs=[_row_spec(ts, QL), _row_spec(ts, KVL), _row_spec(ts, LANE), _row_spec(ts, LANE)],
        out_shape=[jax.ShapeDtypeStruct((S, QL), BF16), jax.ShapeDtypeStruct((S, KVL), BF16),
                   jax.ShapeDtypeStruct((S, LANE), BF16), jax.ShapeDtypeStruct((S, LANE), F32)],
        scratch_shapes=[pltpu.VMEM((1, LANE), F32)],
        compiler_params=_params(("arbitrary",)),
    )(small, q_norm, kv_norm, bias_pad, kc, ksa, ksb, tri)


def _prep_bwd(small, dcqn, dckvn, dkr_heads, dlogf, q_norm, kv_norm, bias_pad, kc, ksa, ksb, n_heads, name):
    S, W = small.shape
    QL, KVL = q_norm.shape[1], kv_norm.shape[1]
    ts = _tile(S, ROW_T, 8)

    def body(s_ref, dcq_ref, dckv_ref, dkr_ref, dlf_ref, qn_ref, kvn_ref, b_ref, kc_ref, ksa_ref, ksb_ref,
             ds_ref, gq_ref, gkv_ref, gb_ref):
        dcq, gq_rows = _rms_bwd(s_ref[:, 0:QL], dcq_ref[...], qn_ref[...])
        ds_ref[:, 0:QL] = dcq.astype(BF16)
        dckv, gkv_rows = _rms_bwd(s_ref[:, QL:QL + KVL], dckv_ref[...], kvn_ref[...])
        ds_ref[:, QL:QL + KVL] = dckv.astype(BF16)
        dkr = dkr_ref[:, 0:LANE]
        for h in range(1, n_heads):
            dkr = dkr + dkr_ref[:, h * LANE:(h + 1) * LANE]
        ds_ref[:, QL + KVL:QL + KVL + LANE] = _rope(dkr, kc_ref[...], ksa_ref[...], ksb_ref[...], -1).astype(BF16)
        z = s_ref[:, QL + KVL + LANE:W] + b_ref[...]
        dff = dlf_ref[...] * (1.0 / (1.0 + jnp.exp(z)))
        ds_ref[:, QL + KVL + LANE:W] = dff.astype(BF16)

        @pl.when(pl.program_id(0) == 0)
        def _():
            gq_ref[...] = jnp.zeros_like(gq_ref)
            gkv_ref[...] = jnp.zeros_like(gkv_ref)
            gb_ref[...] = jnp.zeros_like(gb_ref)

        gq_ref[...] += jnp.sum(gq_rows, axis=0, keepdims=True)
        gkv_ref[...] += jnp.sum(gkv_rows, axis=0, keepdims=True)
        gb_ref[...] += jnp.sum(dff, axis=0, keepdims=True)

    return pl.pallas_call(
        body, name=name, grid=(S // ts,),
        in_specs=[_row_spec(ts, W), _row_spec(ts, QL), _row_spec(ts, KVL), _row_spec(ts, n_heads * LANE),
                  _row_spec(ts, LANE), _full_spec((1, QL)), _full_spec((1, KVL)), _full_spec((1, LANE)),
                  _row_spec(ts, LANE), _row_spec(ts, LANE), _row_spec(ts, LANE)],
        out_specs=[_row_spec(ts, W), _full_spec((1, QL)), _full_spec((1, KVL)), _full_spec((1, LANE))],
        out_shape=[jax.ShapeDtypeStruct((S, W), BF16), jax.ShapeDtypeStruct((1, QL), F32),
                   jax.ShapeDtypeStruct((1, KVL), F32), jax.ShapeDtypeStruct((1, LANE), F32)],
        compiler_params=_params(("arbitrary",)),
    )(small, dcqn, dckvn, dkr_heads, dlogf, q_norm, kv_norm, bias_pad, kc, ksa, ksb)


def _sigmoid(z):
    return 1.0 / (1.0 + jnp.exp(-z))


def _final(h, g, target, name):
    S, D = h.shape
    ts = _tile(S, ROW_T, 8)

    def body(h_ref, g_ref, t_ref, dh_ref, dhb_ref, dg_ref, loss_ref):
        hv = h_ref[...]
        gv = g_ref[...]
        err = (hv * _rms(hv)) * gv - t_ref[...]
        dh, dg_rows = _rms_bwd(hv, err / D, gv)
        dh_ref[...] = dh
        dhb_ref[...] = dh.astype(BF16)

        @pl.when(pl.program_id(0) == 0)
        def _():
            dg_ref[...] = jnp.zeros_like(dg_ref)
            loss_ref[...] = jnp.zeros_like(loss_ref)

        dg_ref[...] += jnp.sum(dg_rows, axis=0, keepdims=True)
        row_loss = jnp.mean(err * err, axis=-1, keepdims=True)
        loss_ref[...] += 0.5 * jnp.sum(row_loss, axis=0, keepdims=True)

    return pl.pallas_call(
        body, name=name, grid=(S // ts,),
        in_specs=[_row_spec(ts, D), _full_spec((1, D)), _row_spec(ts, D)],
        out_specs=[_row_spec(ts, D), _row_spec(ts, D), _full_spec((1, D)), _full_spec((1, LANE))],
        out_shape=[jax.ShapeDtypeStruct((S, D), F32), jax.ShapeDtypeStruct((S, D), BF16),
                   jax.ShapeDtypeStruct((1, D), F32), jax.ShapeDtypeStruct((1, LANE), F32)],
        compiler_params=_params(("arbitrary",)),
    )(h, g, target)


def _suffix_sum_rows(x, name):
    R, S = x.shape
    tb = _tile(S, 512)
    nb = S // tb
    tri = (lax.broadcasted_iota(jnp.int32, (tb, tb), 0) >= lax.broadcasted_iota(jnp.int32, (tb, tb), 1)).astype(BF16)

    def body(x_ref, tri_ref, o_ref, carry_ref):
        @pl.when(pl.program_id(0) == 0)
        def _():
            carry_ref[...] = jnp.zeros_like(carry_ref)

        xv = x_ref[...]
        t = tri_ref[...]
        acc = jnp.broadcast_to(carry_ref[:, 0:1], xv.shape)
        for part in _split3(xv):
            acc = acc + jnp.dot(part, t, preferred_element_type=F32)
        o_ref[...] = acc
        carry_ref[...] = jnp.broadcast_to(acc[:, 0:1], carry_ref.shape)

    rev = pl.BlockSpec((R, tb), lambda i: (0, nb - 1 - i))
    return pl.pallas_call(
        body, name=name, grid=(nb,),
        in_specs=[rev, _full_spec((tb, tb))], out_specs=rev,
        out_shape=jax.ShapeDtypeStruct((R, S), F32),
        scratch_shapes=[pltpu.VMEM((R, LANE), F32)],
        compiler_params=_params(("arbitrary",)),
    )(x, tri)


def _pairs(nb, by_key):
    if by_key:
        pr = [(i, j) for j in range(nb) for i in range(j, nb)]
    else:
        pr = [(i, j) for i in range(nb) for j in range(i + 1)]
    return (jnp.asarray([p[0] for p in pr], jnp.int32), jnp.asarray([p[1] for p in pr], jnp.int32), len(pr))


class _AttT:
    def __init__(self, S, n_heads, q, ks, v, scale, chunk_causal, cum_rep=None):
        self.S, self.H, self.q, self.ks, self.v = S, n_heads, q, ks, v
        self.scale, self.chunk_causal, self.cum_rep = scale, chunk_causal, cum_rep
        self.T = _tile(S, ATT_T)
        self.qs = min(QSUB, self.T)
        self.nb = S // self.T
        self.dq, self.dv = q[1], v[1]
        self.has_bias = cum_rep is not None

    def q_spec(self, op):
        _, w, off, per_head = op
        return pl.BlockSpec((self.T, w), lambda h, p, it, jt: (it[p], off + (h if per_head else 0)))

    def k_spec(self, op):
        _, w, off, per_head = op
        return pl.BlockSpec((self.T, w), lambda h, p, it, jt: (jt[p], off + (h if per_head else 0)))

    def row_q(self):
        return pl.BlockSpec((None, 1, self.T), lambda h, p, it, jt: (h, 0, it[p]))

    def cum_k(self):
        return pl.BlockSpec((None, self.T, self.qs), lambda h, p, it, jt: (h, jt[p], 0))

    def sub_blocks(self, masked):
        return [(q0, min(self.T, q0 + self.qs) if masked else self.T) for q0 in range(0, self.T, self.qs)]

    def scores(self, k, q_sub, cum, q0, masked):
        s = lax.dot_general(k, q_sub, _NT, preferred_element_type=F32)
        if self.has_bias:
            s = s - cum
        mask = None
        if masked:
            r = lax.broadcasted_iota(jnp.int32, s.shape, 0)
            c = lax.broadcasted_iota(jnp.int32, s.shape, 1) + q0
            mask = (r // CHUNK <= c // CHUNK) if self.chunk_causal else (r <= c)
        return s, mask


def _join(k_refs):
    return k_refs[0][...] if len(k_refs) == 1 else jnp.concatenate([r[...] for r in k_refs], axis=-1)


def _att_fwd_t(att, name, exact=False):
    S, H, T, qs = att.S, att.H, att.T, att.qs
    it, jt, npairs = _pairs(att.nb, by_key=False)
    nk = len(att.ks)

    def body(it_ref, jt_ref, *refs):
        q_ref = refs[0]
        k_refs = refs[1:1 + nk]
        v_ref = refs[1 + nk]
        n = 2 + nk
        cum_ref = None
        if att.has_bias:
            cum_ref = refs[n]
            n += 1
        o_ref = refs[n]
        n += 1
        ox_ref = None
        if exact:
            ox_ref = refs[n]
            n += 1
        lse_ref, m_ref, l_ref, acc_ref = refs[n:n + 4]
        lo_ref = refs[n + 4] if exact else None
        p = pl.program_id(1)
        i, j = it_ref[p], jt_ref[p]

        @pl.when(j == 0)
        def _():
            m_ref[...] = jnp.full_like(m_ref, -jnp.inf)
            l_ref[...] = jnp.zeros_like(l_ref)
            acc_ref[...] = jnp.zeros_like(acc_ref)
            if exact:
                lo_ref[...] = jnp.zeros_like(lo_ref)

        def step(masked):
            k = _join(k_refs)
            v = v_ref[...]
            subs = att.sub_blocks(masked)

            def logits(idx):
                q0, nkeys = subs[idx]
                cum = cum_ref[0:nkeys, :] if att.has_bias else None
                return att.scores(k[0:nkeys], q_ref[q0:q0 + qs, :], cum, q0, masked)

            ahead = logits(0)
            for idx, (q0, nkeys) in enumerate(subs):
                qsl = slice(q0, q0 + qs)
                s, mask = ahead
                if idx + 1 < len(subs):
                    ahead = logits(idx + 1)
                if masked:
                    s = jnp.where(mask, s, -jnp.inf)
                m_prev = m_ref[:, qsl]
                m_new = jnp.maximum(m_prev, jnp.max(s, axis=0, keepdims=True))
                alpha = jnp.exp2(m_prev - m_new)
                pr = jnp.exp2(s - m_new)
                l_ref[:, qsl] = alpha * l_ref[:, qsl] + jnp.sum(pr, axis=0, keepdims=True)
                p_hi = pr.astype(BF16)
                acc_ref[:, qsl] = alpha * acc_ref[:, qsl] + lax.dot_general(
                    v[0:nkeys], p_hi, _TN, preferred_element_type=F32)
                if exact:
                    p_lo = (pr - p_hi.astype(F32)).astype(BF16)
                    lo_ref[:, qsl] = alpha * lo_ref[:, qsl] + lax.dot_general(
                        v[0:nkeys], p_lo, _TN, preferred_element_type=F32)
                m_ref[:, qsl] = m_new

        @pl.when(j < i)
        def _():
            step(False)

        @pl.when(j == i)
        def _():
            step(True)
            l = l_ref[...]
            inv = 1.0 / l
            o_ref[...] = jnp.transpose(acc_ref[...] * inv).astype(o_ref.dtype)
            if exact:
                ox_ref[...] = jnp.transpose((acc_ref[...] + lo_ref[...]) * inv)
            lse_ref[...] = m_ref[...] + jnp.log2(l)

    in_specs = [att.q_spec(att.q)] + [att.k_spec(k) for k in att.ks] + [att.k_spec(att.v)]
    args = [att.q[0]] + [k[0] for k in att.ks] + [att.v[0]]
    if att.has_bias:
        in_specs.append(att.cum_k())
        args.append(att.cum_rep)
    o_spec = pl.BlockSpec((T, att.dv), lambda h, p, it, jt: (it[p], h))
    out_specs = [o_spec]
    out_shape = [jax.ShapeDtypeStruct((S, H * att.dv), BF16)]
    scratch = [pltpu.VMEM((1, T), F32), pltpu.VMEM((1, T), F32), pltpu.VMEM((att.dv, T), F32)]
    if exact:
        out_specs.append(o_spec)
        out_shape.append(jax.ShapeDtypeStruct((S, H * att.dv), F32))
        scratch.append(pltpu.VMEM((att.dv, T), F32))
    out_specs.append(att.row_q())
    out_shape.append(jax.ShapeDtypeStruct((H, 1, S), F32))
    return pl.pallas_call(
        body, name=name,
        grid_spec=pltpu.PrefetchScalarGridSpec(
            num_scalar_prefetch=2, grid=(H, npairs), in_specs=in_specs, out_specs=out_specs,
            scratch_shapes=scratch),
        out_shape=out_shape,
        compiler_params=_params(("parallel", "arbitrary")),
    )(it, jt, *args)


def _att_delta_t(do, o, n_heads, name, order=None):
    S = do.shape[0]
    w = do.shape[1] // n_heads
    ts = _tile(S, ATT_T)
    ones = jnp.ones((8, w), BF16)
    extra = [] if order is None else [order]

    def body(do_ref, o_ref, ones_ref, *rest):
        d_ref = rest[-1]
        prod = do_ref[...].astype(F32) * o_ref[...].astype(F32)
        acc = jnp.zeros((8, ts), F32)
        for part in _split3(prod):
            acc = acc + lax.dot_general(ones_ref[...], part, _NT, preferred_element_type=F32)
        d_ref[...] = acc[0:1, :]

    blk = pl.BlockSpec((ts, w), lambda i, h: (i, h))
    return pl.pallas_call(
        body, name=name, grid=(S // ts, n_heads),
        in_specs=[blk, blk, pl.BlockSpec((8, w), lambda i, h: (0, 0))] + [_ANY_SPEC] * len(extra),
        out_specs=pl.BlockSpec((None, 1, ts), lambda i, h: (h, 0, i)),
        out_shape=jax.ShapeDtypeStruct((n_heads, 1, S), F32),
        compiler_params=_params(("parallel", "parallel")),
    )(do, o, ones, *extra)


def _att_bwd_t(att, do, lse, delta, dq_dtype, dk_dtypes, name, dq_rope=None):
    S, H, T, qs = att.S, att.H, att.T, att.qs
    it, jt, npairs = _pairs(att.nb, by_key=True)
    nk = len(att.ks)
    last = att.nb - 1
    widths = [k[1] for k in att.ks]

    def body(it_ref, jt_ref, *refs):
        q_ref = refs[0]
        k_refs = refs[1:1 + nk]
        v_ref, do_ref, lse_ref, dl_ref = refs[1 + nk:5 + nk]
        n = 5 + nk
        cum_ref = None
        if att.has_bias:
            cum_ref = refs[n]
            n += 1
        rope_refs = None
        if dq_rope is not None:
            rope_refs = refs[n:n + 3]
            n += 3
        dq_ref = refs[n]
        dk_refs = refs[n + 1:n + 1 + nk]
        dv_ref = refs[n + 1 + nk]
        n += nk + 2
        dc_ref = None
        if att.has_bias:
            dc_ref = refs[n]
            n += 1
        dq_acc, dk_acc, dv_acc = refs[n:n + 3]
        dc_acc = refs[n + 3] if att.has_bias else None
        p = pl.program_id(1)
        i, j = it_ref[p], jt_ref[p]

        @pl.when(p == 0)
        def _():
            dq_acc[...] = jnp.zeros_like(dq_acc)

        @pl.when(i == j)
        def _():
            dk_acc[...] = jnp.zeros_like(dk_acc)
            dv_acc[...] = jnp.zeros_like(dv_acc)
            if att.has_bias:
                dc_acc[...] = jnp.zeros_like(dc_acc)

        def step(masked):
            k = _join(k_refs)
            v = v_ref[...]
            subs = att.sub_blocks(masked)

            def logits(idx):
                q0, nkeys = subs[idx]
                cum = cum_ref[0:nkeys, :] if att.has_bias else None
                return att.scores(k[0:nkeys], q_ref[q0:q0 + qs, :], cum, q0, masked)

            ahead = logits(0)
            for idx, (q0, nkeys) in enumerate(subs):
                qsl = slice(q0, q0 + qs)
                ksl = slice(0, nkeys)
                q_sub = q_ref[qsl, :]
                do_sub = do_ref[qsl, :]
                s, mask = ahead
                if idx + 1 < len(subs):
                    ahead = logits(idx + 1)
                pr = jnp.exp2(s - lse_ref[:, qsl])
                if masked:
                    pr = jnp.where(mask, pr, 0.0)
                dp = lax.dot_general(v[ksl], do_sub, _NT, preferred_element_type=F32)
                ds = pr * (dp - dl_ref[:, qsl])
                ds_b = ds.astype(BF16)
                dv_acc[ksl, :] += jnp.dot(pr.astype(BF16), do_sub, preferred_element_type=F32)
                dk_acc[ksl, :] += jnp.dot(ds_b, q_sub, preferred_element_type=F32)
                dq_acc[i, :, qsl] += lax.dot_general(k[ksl], ds_b, _TN, preferred_element_type=F32)
                if att.has_bias:
                    part = ds[:, 0:LANE] if qs >= LANE else ds
                    for c0 in range(LANE, qs, LANE):
                        part = part + ds[:, c0:c0 + LANE]
                    dc_acc[ksl, :] += part

        @pl.when(i > j)
        def _():
            step(False)

        @pl.when(i == j)
        def _():
            step(True)
            dq = jnp.transpose(dq_acc[i] * att.scale)
            if dq_rope is not None:
                dq = _rope(dq, rope_refs[0][...], rope_refs[1][...], rope_refs[2][...], -1)
            dq_ref[...] = dq.astype(dq_ref.dtype)

        @pl.when(i == last)
        def _():
            dk = dk_acc[...] * (1.0 / LOG2E)
            off = 0
            for r, w in zip(dk_refs, widths):
                r[...] = dk[:, off:off + w].astype(r.dtype)
                off += w
            dv_ref[...] = dv_acc[...].astype(dv_ref.dtype)
            if att.has_bias:
                dc_ref[...] = -jnp.sum(dc_acc[...], axis=-1, keepdims=True)

    do_op = (do, att.dv, 0, True)
    in_specs = ([att.q_spec(att.q)] + [att.k_spec(k) for k in att.ks]
                + [att.k_spec(att.v), att.q_spec(do_op), att.row_q(), att.row_q()])
    args = [att.q[0]] + [k[0] for k in att.ks] + [att.v[0], do, lse, delta]
    if att.has_bias:
        in_specs.append(att.cum_k())
        args.append(att.cum_rep)
    if dq_rope is not None:
        in_specs += [pl.BlockSpec((T, att.dq), lambda h, p, it, jt: (jt[p], 0))] * 3
        args += list(dq_rope)
    out_specs = [pl.BlockSpec((T, att.dq), lambda h, p, it, jt: (jt[p], h))]
    out_shape = [jax.ShapeDtypeStruct((S, H * att.dq), dq_dtype)]
    out_specs += [pl.BlockSpec((T, w), lambda h, p, it, jt: (jt[p], h)) for w in widths]
    out_shape += [jax.ShapeDtypeStruct((S, H * w), dt) for w, dt in zip(widths, dk_dtypes)]
    out_specs.append(pl.BlockSpec((T, att.dv), lambda h, p, it, jt: (jt[p], h)))
    out_shape.append(jax.ShapeDtypeStruct((S, H * att.dv), BF16))
    scratch = [pltpu.VMEM((att.nb, att.dq, T), F32), pltpu.VMEM((T, att.dq), F32), pltpu.VMEM((T, att.dv), F32)]
    if att.has_bias:
        out_specs.append(pl.BlockSpec((None, T, 1), lambda h, p, it, jt: (h, jt[p], 0)))
        out_shape.append(jax.ShapeDtypeStruct((H, S, 1), F32))
        scratch.append(pltpu.VMEM((T, min(qs, LANE)), F32))
    return pl.pallas_call(
        body, name=name,
        grid_spec=pltpu.PrefetchScalarGridSpec(
            num_scalar_prefetch=2, grid=(H, npairs), in_specs=in_specs, out_specs=out_specs,
            scratch_shapes=scratch),
        out_shape=out_shape,
        compiler_params=_params(("parallel", "arbitrary")),
    )(it, jt, *args)


def _adamw(w, g1, g2, m, v, name, g_row=None):
    _, K, N = w.shape
    by_rows = K % 8 == 0
    tr = _tile(K, 256, 8) if by_rows else K
    if g_row is None:
        assert g1.shape == (K, N) and g2.shape == (K, N), name
        g_row = 0
    assert by_rows and g_row % tr == 0 or g_row == 0, name
    g_blk = g_row // tr
    tc = N if by_rows else _tile(N, LANE)
    c1 = 1.0 - ADAM_B1 ** ADAM_STEP
    c2 = 1.0 - ADAM_B2 ** ADAM_STEP

    def body(w_ref, g1_ref, g2_ref, m_ref, v_ref, g_ref, d_ref, nm_ref, nv_ref):
        gv = g1_ref[...] + g2_ref[...]
        nm = ADAM_B1 * m_ref[...] + (1.0 - ADAM_B1) * gv
        nv = ADAM_B2 * v_ref[...] + (1.0 - ADAM_B2) * (gv * gv)
        g_ref[...] = gv
        d_ref[...] = -ADAM_LR * ((nm / c1) / (jnp.sqrt(nv / c2) + ADAM_EPS) + ADAM_WD * w_ref[...])
        nm_ref[...] = nm
        nv_ref[...] = nv

    if by_rows:
        blk = pl.BlockSpec((None, tr, N), lambda i: (0, i, 0))
        gblk = pl.BlockSpec((tr, N), lambda i: (g_blk + i, 0))
    else:
        blk = pl.BlockSpec((None, K, tc), lambda i: (0, 0, i))
        gblk = pl.BlockSpec((K, tc), lambda i: (0, i))
    return pl.pallas_call(
        body, name=name, grid=(K // tr if by_rows else N // tc,),
        in_specs=[blk, gblk, gblk, blk, blk], out_specs=[blk] * 4,
        out_shape=[jax.ShapeDtypeStruct((1, K, N), F32)] * 4,
        compiler_params=_params(("parallel",)),
    )(w, g1, g2, m, v)


_HBM_SPEC = pl.BlockSpec(memory_space=pltpu.HBM)
_SEM_SPEC = pl.BlockSpec(memory_space=pltpu.SEMAPHORE)
_VMEM_SPEC = pl.BlockSpec(memory_space=pltpu.VMEM)
_EFFECT = pltpu.SideEffectType.DATAFLOW_SIDE_EFFECTING


def _place():
    return lax.axis_index("x"), lax.axis_index("y"), lax.axis_index("c")


def _other_chips(x, y):
    return [(1 - x, y), (x, 1 - y), (1 - x, 1 - y)]


def _all_gather_halves(wp, name):
    R, C = wp.shape
    half = R // 2
    assert half % 16 == 0

    def body(w_ref, out_ref, ici_send, ici_recv, d2d_send, d2d_recv, local_sem):
        x, y, c = _place()
        me = 2 * x + y
        chips = _other_chips(x, y)
        mine = pl.ds(pl.multiple_of(c * half, 16), half)
        theirs = pl.ds(pl.multiple_of((1 - c) * half, 16), half)
        local = pltpu.make_async_copy(w_ref, out_ref.at[me], local_sem)
        local.start()
        sends = []
        for n, (px, py) in enumerate(chips):
            cp = pltpu.make_async_remote_copy(
                src_ref=w_ref.at[mine], dst_ref=out_ref.at[me, mine], send_sem=ici_send.at[n],
                recv_sem=ici_recv.at[n], device_id=(px, py, c), device_id_type=MESH)
            cp.start()
            sends.append(cp)
        for n, (px, py) in enumerate(chips):
            slot = 2 * px + py
            pltpu.make_async_remote_copy(
                src_ref=w_ref.at[mine], dst_ref=out_ref.at[slot, mine], send_sem=ici_send.at[n],
                recv_sem=ici_recv.at[n], device_id=(px, py, c), device_id_type=MESH).wait_recv()
            cp = pltpu.make_async_remote_copy(
                src_ref=out_ref.at[slot, mine], dst_ref=out_ref.at[slot, mine], send_sem=d2d_send.at[n],
                recv_sem=d2d_recv.at[n], device_id=(x, y, 1 - c), device_id_type=MESH)
            cp.start()
            sends.append(cp)
        for n, (px, py) in enumerate(chips):
            slot = 2 * px + py
            pltpu.make_async_remote_copy(
                src_ref=out_ref.at[slot, theirs], dst_ref=out_ref.at[slot, theirs], send_sem=d2d_send.at[n],
                recv_sem=d2d_recv.at[n], device_id=(x, y, 1 - c), device_id_type=MESH).wait_recv()
        for cp in sends:
            cp.wait_send()
        local.wait()

    return pl.pallas_call(
        body, name=name,
        in_specs=[_ANY_SPEC], out_specs=_ANY_SPEC,
        out_shape=jax.ShapeDtypeStruct((N_CHIPS, R, C), wp.dtype),
        scratch_shapes=[pltpu.SemaphoreType.DMA((3,)), pltpu.SemaphoreType.DMA((3,)), pltpu.SemaphoreType.DMA((3,)),
                        pltpu.SemaphoreType.DMA((3,)), pltpu.SemaphoreType.DMA],
    )(wp)


def _chip_copies(src_ref, land_ref, sems, gather):
    x, y, c = _place()
    me = 2 * x + y
    out, back = [], []
    for n, (px, py) in enumerate(_other_chips(x, y)):
        src = src_ref if gather else src_ref.at[2 * px + py]
        out.append(pltpu.make_async_remote_copy(
            src_ref=src, dst_ref=land_ref.at[me] if gather else land_ref.at[n],
            send_sem=sems[n], recv_sem=sems[3 + n], device_id=(px, py, c), device_id_type=MESH))
        back.append(pltpu.make_async_remote_copy(
            src_ref=src, dst_ref=land_ref.at[2 * px + py] if gather else land_ref.at[n],
            send_sem=sems[n], recv_sem=sems[3 + n], device_id=(px, py, c), device_id_type=MESH))
    return out, back


def _xchg_start(src, land, gather, order, name):
    def body(src_ref, land_ref, order_ref, *outs):
        sems = outs[0:6]
        token = outs[8]
        out, _ = _chip_copies(src_ref, land_ref, sems, gather)
        for cp in out:
            cp.start()
        token[...] = jnp.zeros_like(token)

    outs = pl.pallas_call(
        body, name=name,
        out_shape=(pltpu.SemaphoreType.DMA(()),) * 6 + (
            pltpu.HBM(src.shape, src.dtype), pltpu.HBM(land.shape, land.dtype),
            jax.ShapeDtypeStruct((8, LANE), F32)),
        in_specs=(_HBM_SPEC, _HBM_SPEC, _ANY_SPEC),
        out_specs=(_SEM_SPEC,) * 6 + (_HBM_SPEC, _HBM_SPEC, _VMEM_SPEC),
        input_output_aliases={0: 6, 1: 7},
        compiler_params=pltpu.CompilerParams(has_side_effects=_EFFECT),
    )(pltpu.with_memory_space_constraint(src, pltpu.HBM), pltpu.with_memory_space_constraint(land, pltpu.HBM), order)
    return outs[0:6], outs[6], outs[7], outs[8]


def _xchg_wait(started, gather, after, name):
    sems, src, land, _ = started

    def body(src_ref, land_ref, *rest):
        _, back = _chip_copies(src_ref, land_ref, rest[0:6], gather)
        for cp in back:
            cp.wait_send()
            cp.wait_recv()

    return pl.pallas_call(
        body, name=name,
        out_shape=(pltpu.HBM(src.shape, src.dtype), pltpu.HBM(land.shape, land.dtype)),
        in_specs=(_HBM_SPEC, _HBM_SPEC) + (_SEM_SPEC,) * 6 + (_ANY_SPEC,),
        out_specs=(_HBM_SPEC, _HBM_SPEC),
        input_output_aliases={0: 0, 1: 1},
        compiler_params=pltpu.CompilerParams(has_side_effects=_EFFECT),
    )(src, land, *sems, after)


def _sib_copy(src_ref, land_ref, send_sem, recv_sem):
    x, y, c = _place()
    return pltpu.make_async_remote_copy(src_ref=src_ref, dst_ref=land_ref, send_sem=send_sem, recv_sem=recv_sem,
                                        device_id=(x, y, 1 - c), device_id_type=MESH)


def _sib_start(src, name):
    land = lax.empty(src.shape, src.dtype)

    def body(src_ref, land_ref, send_sem, recv_sem, src_thru, land_thru, token):
        _sib_copy(src_ref, land_ref, send_sem, recv_sem).start()
        token[...] = jnp.zeros_like(token)

    return pl.pallas_call(
        body, name=name,
        out_shape=(pltpu.SemaphoreType.DMA(()), pltpu.SemaphoreType.DMA(()),
                   pltpu.HBM(src.shape, src.dtype), pltpu.HBM(land.shape, land.dtype),
                   jax.ShapeDtypeStruct((8, LANE), F32)),
        in_specs=(_HBM_SPEC, _HBM_SPEC),
        out_specs=(_SEM_SPEC, _SEM_SPEC, _HBM_SPEC, _HBM_SPEC, _VMEM_SPEC),
        input_output_aliases={0: 2, 1: 3},
        compiler_params=pltpu.CompilerParams(has_side_effects=_EFFECT),
    )(pltpu.with_memory_space_constraint(src, pltpu.HBM), pltpu.with_memory_space_constraint(land, pltpu.HBM))


def _sib_wait(started, after, name):
    send_sem, recv_sem, src, land, _ = started

    def body(src_ref, land_ref, send_sem, recv_sem, after_ref, src_out, land_out):
        cp = _sib_copy(src_ref, land_ref, send_sem, recv_sem)
        cp.wait_send()
        cp.wait_recv()

    return pl.pallas_call(
        body, name=name,
        out_shape=(pltpu.HBM(src.shape, src.dtype), pltpu.HBM(land.shape, land.dtype)),
        in_specs=(_HBM_SPEC, _HBM_SPEC, _SEM_SPEC, _SEM_SPEC, _ANY_SPEC),
        out_specs=(_HBM_SPEC, _HBM_SPEC),
        input_output_aliases={0: 0, 1: 1},
        compiler_params=pltpu.CompilerParams(has_side_effects=_EFFECT),
    )(src, land, send_sem, recv_sem, after)


def _sum_slabs(gp, recv, chip, name):
    _, R, C = gp.shape
    tr = _tile(R, PACK_ROWS, 16)

    def body(chip_ref, own_ref, r0_ref, r1_ref, r2_ref, o_ref):
        acc = own_ref[...].astype(F32) + r0_ref[...].astype(F32)
        o_ref[...] = (acc + r1_ref[...].astype(F32)) + r2_ref[...].astype(F32)

    def got(n):
        return pl.BlockSpec((None, tr, C), lambda i, chip_ref: (n, i, 0))

    return pl.pallas_call(
        body, name=name,
        grid_spec=pltpu.PrefetchScalarGridSpec(
            num_scalar_prefetch=1, grid=(R // tr,),
            in_specs=[pl.BlockSpec((None, tr, C), lambda i, chip_ref: (chip_ref[0], i, 0)), got(0), got(1), got(2)],
            out_specs=pl.BlockSpec((tr, C), lambda i, chip_ref: (i, 0))),
        out_shape=jax.ShapeDtypeStruct((R, C), F32),
        compiler_params=_params(("parallel",)),
    )(jnp.reshape(chip, (1,)).astype(jnp.int32), gp, recv, recv, recv)


def _all_reduce_vec(vec, name):
    VR, W = vec.shape

    def body(vec_ref, vall_ref, vout_ref, vsend_sems, vrecv_sems):
        x, y, c = _place()
        vall_ref[4 * x + 2 * y + c] = vec_ref[...]
        sends = []
        peers = []
        for r in range(1, N_DEV):
            dx, dy, dc = (r >> 2) & 1, (r >> 1) & 1, r & 1
            peer = (x ^ dx, y ^ dy, c ^ dc)
            peers.append(peer)
            cp = pltpu.make_async_remote_copy(
                src_ref=vec_ref, dst_ref=vall_ref.at[4 * x + 2 * y + c], send_sem=vsend_sems.at[r - 1],
                recv_sem=vrecv_sems.at[r - 1], device_id=peer, device_id_type=MESH)
            cp.start()
            sends.append(cp)
        for r, peer in enumerate(peers):
            pltpu.make_async_remote_copy(
                src_ref=vec_ref, dst_ref=vall_ref.at[4 * peer[0] + 2 * peer[1] + peer[2]],
                send_sem=vsend_sems.at[r], recv_sem=vrecv_sems.at[r],
                device_id=peer, device_id_type=MESH).wait_recv()
        total = vall_ref[0]
        for d in range(1, N_DEV):
            total = total + vall_ref[d]
        vout_ref[...] = total
        for cp in sends:
            cp.wait_send()

    outs = pl.pallas_call(
        body, name=name,
        in_specs=[_VMEM_SPEC], out_specs=[_VMEM_SPEC, _VMEM_SPEC],
        out_shape=[jax.ShapeDtypeStruct((N_DEV, VR, W), F32), jax.ShapeDtypeStruct((VR, W), F32)],
        scratch_shapes=[pltpu.SemaphoreType.DMA((N_DEV - 1,)), pltpu.SemaphoreType.DMA((N_DEV - 1,))],
    )(vec)
    return outs[1]


class _Pack:
    def __init__(self, group, C):
        self.group, self.C = group, C
        self.rows, self.offs, off = {}, {}, 0
        for nm, (K, N), _ in group:
            assert N <= C, nm
            self.rows[nm] = K if 2 * N > C else -(-(K * N) // C)
            self.offs[nm] = off
            off += -(-self.rows[nm] // 16) * 16
        self.used = off
        self.R = -(-off // PACK_ROWS) * PACK_ROWS

    def _rows_of(self, a):
        K, N = a.shape
        if 2 * N > self.C:
            a = jnp.pad(a, ((0, 0), (0, self.C - N)))
        else:
            a = jnp.pad(a.reshape(-1), (0, -(K * N) % self.C)).reshape(-1, self.C)
        return jnp.pad(a, ((0, -a.shape[0] % 16), (0, 0)))

    def pack(self, shards):
        parts = [self._rows_of(shards[nm].astype(BF16)) for nm, _, _ in self.group]
        return jnp.concatenate(parts + [jnp.zeros((self.R - self.used, self.C), BF16)], axis=0)

    def part(self, flat, nm, shape):
        K, N = shape
        rows = flat[self.offs[nm]:self.offs[nm] + self.rows[nm]]
        return rows[:, :N] if 2 * N > self.C else rows.reshape(-1)[:K * N].reshape(K, N)

    def slab_rows(self, nm, g):
        (K, N), axis = next((shape, axis) for n, shape, axis in self.group if n == nm)
        cuts = [g[:, k * N:(k + 1) * N] if axis == 1 else g[k * K:(k + 1) * K, :] for k in range(N_CHIPS)]
        return jnp.stack([self._rows_of(c.astype(BF16)) for c in cuts])

    def slabs(self, grads):
        parts = [self.slab_rows(nm, grads[nm]) for nm, _, _ in self.group]
        return jnp.concatenate(parts + [jnp.zeros((N_CHIPS, self.R - self.used, self.C), BF16)], axis=1)

    def full(self, gathered):
        res = {}
        for nm, (K, N), axis in self.group:
            parts = [self.part(gathered[k], nm, (K, N)) for k in range(N_CHIPS)]
            res[nm] = jnp.concatenate(parts, axis=axis)
        return res


def _rope_tables(S):
    pos = jnp.arange(S, dtype=F32)
    inv = 1.0 / (ROPE_THETA ** (jnp.arange(0, MLA_ROPE, 2, dtype=F32) / MLA_ROPE))
    ang = pos[:, None] * inv[None, :]
    cos, sin = jnp.cos(ang), jnp.sin(ang)
    half = MLA_ROPE // 2
    z = jnp.zeros((S, half), F32)
    one = jnp.ones((S, LANE - MLA_ROPE), F32)
    zero = jnp.zeros((S, LANE - MLA_ROPE), F32)
    kc = jnp.concatenate([cos, cos, one], axis=1)
    ksa = jnp.concatenate([-sin, z, zero], axis=1)
    ksb = jnp.concatenate([z, sin, zero], axis=1)
    qc = jnp.concatenate([jnp.ones((S, MLA_NOPE), F32), kc], axis=1)
    qsa = jnp.concatenate([jnp.zeros((S, MLA_NOPE), F32), ksa], axis=1)
    qsb = jnp.concatenate([jnp.zeros((S, MLA_NOPE), F32), ksb], axis=1)
    return (kc, ksa, ksb), (qc, qsa, qsb)


def _pad_cols(a, width):
    return jnp.pad(a, ((0, 0), (0, width - a.shape[1])))


def kernel(x, attn_norm, w_in, fox_f_bias, q_norm, w_uq, kv_norm, w_ukv, w_mla_branch, w_fox_branch, w_out, mlp_norm, w_up, w_down, final_norm, loss_target, m_attn_norm, m_w_in, m_fox_f_bias, m_q_norm, m_w_uq, m_kv_norm, m_w_ukv, m_w_mla_branch, m_w_fox_branch, m_w_out, m_mlp_norm, m_w_up, m_w_down, m_final_norm, v_attn_norm, v_w_in, v_fox_f_bias, v_q_norm, v_w_uq, v_kv_norm, v_w_ukv, v_w_mla_branch, v_w_fox_branch, v_w_out, v_mlp_norm, v_w_up, v_w_down, v_final_norm):
    _, S, D = x.shape
    H, HF = MLA_HEADS, FOX_HEADS
    QL, KVL = MLA_Q_LORA, MLA_KV_LORA
    assert H == HF and H <= 8
    xs = x[0]
    target = loss_target[0]
    C = D
    chip = 2 * lax.axis_index("x") + lax.axis_index("y")

    def flip(a):
        return jnp.transpose(a, (0, 2, 1))

    w_in, m_w_in, v_w_in = flip(w_in), flip(m_w_in), flip(v_w_in)
    weights = {"attn_norm": attn_norm, "w_in": w_in, "fox_f_bias": fox_f_bias, "q_norm": q_norm, "w_uq": w_uq,
               "kv_norm": kv_norm, "w_ukv": w_ukv, "w_mla_branch": w_mla_branch, "w_fox_branch": w_fox_branch,
               "w_out": w_out, "mlp_norm": mlp_norm, "w_up": w_up, "w_down": w_down, "final_norm": final_norm}
    moments = {"attn_norm": (m_attn_norm, v_attn_norm), "w_in": (m_w_in, v_w_in), "fox_f_bias": (m_fox_f_bias, v_fox_f_bias),
               "q_norm": (m_q_norm, v_q_norm), "w_uq": (m_w_uq, v_w_uq), "kv_norm": (m_kv_norm, v_kv_norm),
               "w_ukv": (m_w_ukv, v_w_ukv), "w_mla_branch": (m_w_mla_branch, v_w_mla_branch),
               "w_fox_branch": (m_w_fox_branch, v_w_fox_branch), "w_out": (m_w_out, v_w_out),
               "mlp_norm": (m_mlp_norm, v_mlp_norm), "w_up": (m_w_up, v_w_up), "w_down": (m_w_down, v_w_down),
               "final_norm": (m_final_norm, v_final_norm)}

    def group(names_axes):
        return [(nm, weights[nm].shape[1:], axis) for nm, axis in names_axes]

    pack_a = _Pack(group([("w_in", 0), ("w_uq", 1), ("w_ukv", 1)]), C)
    pack_b = _Pack(group([("w_down", 0), ("w_up", 1), ("w_out", 0), ("w_mla_branch", 1), ("w_fox_branch", 1)]), C)
    RA, RB = pack_a.R, pack_b.R
    wp_a = pack_a.pack({nm: weights[nm][0] for nm, _, _ in pack_a.group})
    wp_b = pack_b.pack({nm: weights[nm][0] for nm, _, _ in pack_b.group})
    gathered_a = _all_gather_halves(wp_a, "all_gather_a")
    ag_b = _xchg_start(wp_b, lax.empty((N_CHIPS, RB, C), BF16), True, gathered_a, "all_gather_start_b")
    xn = _norm_fwd(xs, attn_norm, "attn_norm_fwd", order=ag_b[3])
    full = pack_a.full(gathered_a)

    o_ckv = QL
    o_kr = o_ckv + KVL
    o_fq = o_kr + MLA_ROPE
    o_ff = o_fq + 3 * HF * FOX_HEAD_DIM
    o_g = o_ff + HF
    wi = full["w_in"]
    assert wi.shape[0] == o_g + 2 * D
    WS = QL + KVL + 2 * LANE
    NQKV = 3 * HF * FOX_HEAD_DIM

    def pad_rows(a, rows):
        return jnp.pad(a, ((0, rows - a.shape[0]), (0, 0)))

    w_small = jnp.concatenate([wi[:o_kr], pad_rows(wi[o_kr:o_fq], LANE), pad_rows(wi[o_ff:o_g], LANE)], axis=0)
    w_qkv = wi[o_fq:o_ff]
    w_g = wi[o_g:]
    w_pack = jnp.concatenate([w_small, w_qkv, w_g], axis=0)
    dqk = MLA_NOPE + MLA_ROPE
    w_uq_p = jnp.pad(full["w_uq"].reshape(QL, H, dqk), ((0, 0), (0, 0), (0, QPAD - dqk))).reshape(QL, H * QPAD)
    ukv = full["w_ukv"].reshape(KVL, H, MLA_NOPE + MLA_V)
    w_ukv_p = jnp.concatenate([ukv[:, :, :MLA_NOPE].reshape(KVL, H * MLA_NOPE),
                               ukv[:, :, MLA_NOPE:].reshape(KVL, H * MLA_V)], axis=1)

    (kc, ksa, ksb), (qc, qsa, qsb) = _rope_tables(S)
    bias_pad = _pad_cols(fox_f_bias, LANE)

    small = _matmul(xn, w_small, "nt", [F32], "proj_small")
    n_fq = HF * FOX_HEAD_DIM
    q_scale = jnp.concatenate([jnp.full((1, n_fq), LOG2E / math.sqrt(FOX_HEAD_DIM), F32),
                               jnp.ones((1, NQKV - n_fq), F32)], axis=1)
    qkv = _matmul(xn, w_qkv, "nt", [BF16], "proj_qkv", col_extras=(q_scale,), epilogue=lambda acc, cs: (acc * cs,))
    gpre = _matmul(xn, w_g, "nt", [F32], "proj_gates")
    cqn, ckvn, kr, cum = _prep_fwd(small, q_norm, kv_norm, bias_pad, kc, ksa, ksb, HF, "prep_fwd")
    c2_mla = LOG2E / math.sqrt(dqk)
    q_rot = _matmul(cqn, w_uq_p, "nn", [BF16], "mla_q_up", tn=QPAD, row_extras=(qc * c2_mla, qsa * c2_mla, qsb * c2_mla),
                    epilogue=lambda acc, c, sa, sb: (_rope(acc, c, sa, sb, 1),))
    kv2 = _matmul(ckvn, w_ukv_p, "nn", [BF16], "mla_kv_up")

    mla = _AttT(S, H, (q_rot, QPAD, 0, True), [(kv2, MLA_NOPE, 0, True), (kr, LANE, 0, False)],
                (kv2, MLA_V, H, True), 1.0 / math.sqrt(dqk), True)
    o_mla, lse_mla = _att_fwd_t(mla, "mla_att_fwd")

    cum_t = jnp.transpose(cum[:, :HF]) * LOG2E
    cum_rep = jnp.broadcast_to(cum_t[:, :, None], (HF, S, min(QSUB, _tile(S, ATT_T))))
    fox = _AttT(S, HF, (qkv, FOX_HEAD_DIM, 0, True), [(qkv, FOX_HEAD_DIM, HF, True)],
                (qkv, FOX_HEAD_DIM, 2 * HF, True), 1.0 / math.sqrt(FOX_HEAD_DIM), False, cum_rep)
    o_fox, ox_fox, lse_fox = _att_fwd_t(fox, "fox_att_fwd", exact=True)

    own_b, land_b = _xchg_wait(ag_b, True, lse_fox, "all_gather_wait_b")
    gathered_b = lax.dynamic_update_slice(land_b, own_b[None], (chip, 0, 0))
    full.update(pack_b.full(gathered_b))
    w_mb, w_fb, w_o = (full[n] for n in ("w_mla_branch", "w_fox_branch", "w_out"))

    def b_of(nm, mode, tn, tk):
        (K, N), axis = next((shape, axis) for n, shape, axis in pack_b.group if n == nm)
        off = pack_b.offs[nm]
        shape = (N_CHIPS * K, N) if axis == 0 else (K, N_CHIPS * N)
        t_r, t_c = (tk, tn) if mode == "nn" else (tn, tk)
        t_r, t_c = _tile(shape[0], t_r), _tile(shape[1], t_c)
        if not (N == C and K % t_r == 0 and N % t_c == 0 and off % t_r == 0):
            return full[nm], None
        base = off // t_r
        if axis == 0:
            per = K // t_r
            place = lambda rb, cb: (rb // per, base + rb % per, cb)
        else:
            per = N // t_c
            place = lambda rb, cb: (cb // per, base + rb, cb % per)
        return gathered_b, (shape, (lambda j, k: place(k, j)) if mode == "nn" else (lambda j, k: place(j, k)))

    y_mla = _matmul(o_mla, w_mb, "nn", [F32], "mla_branch")

    def gate_merge(acc, ga, gb, ya):
        return acc, _sigmoid(ga) * ya + _sigmoid(gb) * acc

    y_fox, merged = _matmul(o_fox, w_fb, "nn", [F32, BF16], "fox_branch_gates", extras=((gpre, 0), (gpre, 1), y_mla),
                            epilogue=gate_merge)
    h1 = _matmul(merged, w_o, "nn", [F32], "out_proj", extras=(xs,), epilogue=lambda acc, r: (acc + r,))
    hn = _norm_fwd(h1, mlp_norm, "mlp_norm_fwd")

    def relu2(acc):
        a = jnp.maximum(acc, 0.0)
        return a * a, a

    w_u, w_u_in = b_of("w_up", "nn", 512, 2048)
    u, a_pos = _matmul(hn, w_u, "nn", [BF16, BF16], "mlp_up", epilogue=relu2, b_in=w_u_in)
    w_d, w_d_in = b_of("w_down", "nn", 1024, 2048)
    h2 = _matmul(u, w_d, "nn", [F32], "mlp_down", tn=1024, extras=(h1,), epilogue=lambda acc, r: (acc + r,),
                 b_in=w_d_in)
    dh2, dh2_b, g_final, loss_part = _final(h2, final_norm.reshape(1, D), target, "final_norm_loss")

    gp_b = lax.empty((N_CHIPS, RB, C), BF16)
    by_glue = {}

    def grad_b(nm, a, b, name):
        nonlocal gp_b
        (K, N), axis = next((shape, axis) for n, shape, axis in pack_b.group if n == nm)
        off = pack_b.offs[nm]
        tm = min(1024, K) if axis == 0 else min(1024, a.shape[1])
        tn = min(1024, N) if axis == 1 else min(1024, b.shape[1])
        if not (N == C and tm % LANE == 0 and tn % LANE == 0 and K % tm == 0 and N % tn == 0 and off % tm == 0):
            by_glue[nm] = _mm_tn(a, b, name)
            return
        base = off // tm
        if axis == 0:
            per = K // tm
            place = lambda i, j: (i // per, base + i % per, j)
        else:
            per = N // tn
            place = lambda i, j: (j // per, base + i, j % per)
        gp_b = _mm_tn(a, b, name, tm=tm, tn=tn, into=(gp_b, place))

    w_d, w_d_in = b_of("w_down", "nt", 512, 2048)
    da = _matmul(dh2_b, w_d, "nt", [BF16], "mlp_down_dx", extras=(a_pos,),
                 epilogue=lambda acc, a: (acc * (2.0 * a.astype(F32)),), b_in=w_d_in)
    grad_b("w_down", u, dh2_b, "mlp_down_dw")
    w_u, w_u_in = b_of("w_up", "nt", 1024, 2048)
    dhn = _matmul(da, w_u, "nt", [F32], "mlp_up_dx", tn=1024, b_in=w_u_in)
    grad_b("w_up", hn, da, "mlp_up_dw")
    dh1, dh1_b, g_mlp_norm = _norm_bwd(h1, dhn, mlp_norm, dh2, "mlp_norm_bwd")

    def gate_bwd(acc, ga, gb, ya, yb):
        ga, gb = _sigmoid(ga), _sigmoid(gb)
        return acc * ga, acc * gb, acc * ya * (ga * (1.0 - ga)), acc * yb * (gb * (1.0 - gb))

    dy_mla, dy_fox, dg_mla, dg_fox = _matmul(dh1_b, w_o, "nt", [BF16] * 4, "out_proj_dx_gates",
                                             extras=((gpre, 0), (gpre, 1), y_mla, y_fox), epilogue=gate_bwd)
    grad_b("w_out", merged, dh1_b, "out_proj_dw")
    do_mla = _matmul(dy_mla, w_mb, "nt", [BF16], "mla_branch_dx")
    grad_b("w_mla_branch", o_mla, dy_mla, "mla_branch_dw")
    do_fox = _matmul(dy_fox, w_fb, "nt", [BF16], "fox_branch_dx")
    grad_b("w_fox_branch", o_fox, dy_fox, "fox_branch_dw")
    for nm, g in by_glue.items():
        gp_b = lax.dynamic_update_slice(gp_b, pack_b.slab_rows(nm, g), (0, pack_b.offs[nm], 0))
    if RB > pack_b.used:
        gp_b = lax.dynamic_update_slice(gp_b, jnp.zeros((N_CHIPS, RB - pack_b.used, C), BF16), (0, pack_b.used, 0))

    rs_b = _xchg_start(gp_b, lax.empty((3, RB, C), BF16), False, do_fox, "grad_scatter_start_b")

    delta_mla = _att_delta_t(do_mla, o_mla, H, "mla_att_delta", order=rs_b[3])
    dq_rot, dk_nope, dkr_heads, dv_mla = _att_bwd_t(mla, do_mla, lse_mla, delta_mla, BF16, [BF16, F32],
                                                    "mla_att_bwd", dq_rope=(qc, qsa, qsb))
    delta_fox = _att_delta_t(do_fox, ox_fox, HF, "fox_att_delta")
    dfq, dfk, dfv, dcum = _att_bwd_t(fox, do_fox, lse_fox, delta_fox, BF16, [BF16], "fox_att_bwd")

    gp_b_sent, recv_b = _xchg_wait(rs_b, False, dfq, "grad_scatter_wait_b")
    swap_b = _sib_start(_sum_slabs(gp_b_sent, recv_b, chip, "grad_sum_b"), "grad_swap_start_b")

    dcqn = _matmul(dq_rot, w_uq_p, "nt", [F32], "mla_q_up_dx", order=swap_b[4])
    g_w_uq_p = _mm_tn(cqn, dq_rot, "mla_q_up_dw")
    dkv2 = jnp.concatenate([dk_nope, dv_mla], axis=1)
    dckvn = _matmul(dkv2, w_ukv_p, "nt", [F32], "mla_kv_up_dx")
    g_w_ukv_p = _mm_tn(ckvn, dkv2, "mla_kv_up_dw")

    dcum_rows = jnp.pad(dcum[:, :, 0], ((0, 8 - HF), (0, 0)))
    dlogf_rows = _suffix_sum_rows(dcum_rows, "fox_forget_suffix_sum")
    dlogf = _pad_cols(jnp.transpose(dlogf_rows[:HF]), LANE)
    d_small, g_q_norm, g_kv_norm, g_bias = _prep_bwd(
        small, dcqn, dckvn, dkr_heads, dlogf, q_norm, kv_norm, bias_pad, kc, ksa, ksb, H, "prep_bwd")
    dproj = jnp.concatenate([d_small, dfq, dfk, dfv, dg_mla, dg_fox], axis=1)
    g_w_pack = _matmul(dproj, xn, "tn", [BF16], "proj_dw", tm=1024, tn=1024)

    gs, gq, gg = g_w_pack[:WS], g_w_pack[WS:WS + NQKV], g_w_pack[WS + NQKV:]
    g_w_in = jnp.concatenate([gs[:o_kr], gs[o_kr:o_kr + MLA_ROPE], gq,
                              gs[o_kr + LANE:o_kr + LANE + HF], gg], axis=0)
    g_w_uq = g_w_uq_p.reshape(QL, H, QPAD)[:, :, :dqk].reshape(QL, H * dqk)
    g_w_ukv = jnp.concatenate([g_w_ukv_p[:, :H * MLA_NOPE].reshape(KVL, H, MLA_NOPE),
                               g_w_ukv_p[:, H * MLA_NOPE:].reshape(KVL, H, MLA_V)], axis=2).reshape(KVL, -1)

    gp_a = pack_a.slabs({"w_in": g_w_in, "w_uq": g_w_uq, "w_ukv": g_w_ukv})
    rs_a = _xchg_start(gp_a, lax.empty((3, RA, C), BF16), False, g_w_pack, "grad_scatter_start_a")
    dxn = _matmul(dproj, w_pack, "nn", [F32], "proj_dx", tn=1024, order=rs_a[3])
    grad_x, _, g_attn_norm = _norm_bwd(xs, dxn, attn_norm, dh1, "attn_norm_bwd")
    gp_a_sent, recv_a = _xchg_wait(rs_a, False, grad_x, "grad_scatter_wait_a")
    swap_a = _sib_start(_sum_slabs(gp_a_sent, recv_a, chip, "grad_sum_a"), "grad_swap_start_a")
    vec_w = max(D, LANE)
    vec_rows = [g_attn_norm, g_mlp_norm, g_final, g_q_norm, g_kv_norm, g_bias, loss_part]
    vec = jnp.concatenate([_pad_cols(v, vec_w) for v in vec_rows] + [jnp.zeros((1, vec_w), F32)], axis=0)
    vsum = _all_reduce_vec(vec, "all_reduce_vectors")
    part_b, sib_b = _sib_wait(swap_b, vsum, "grad_swap_wait_b")

    grads, deltas, new_m, new_v = {}, {}, {}, {}

    def update(pack, mine, theirs):
        for nm, shape, _ in pack.group:
            K, N = shape
            if N == pack.C and K % 8 == 0 and pack.offs[nm] % _tile(K, 256, 8) == 0:
                g, d, nm_, nv_ = _adamw(weights[nm], mine, theirs, moments[nm][0], moments[nm][1], "adamw_" + nm,
                                        g_row=pack.offs[nm])
            else:
                g, d, nm_, nv_ = _adamw(weights[nm], pack.part(mine, nm, shape), pack.part(theirs, nm, shape),
                                        moments[nm][0], moments[nm][1], "adamw_" + nm)
            grads[nm], deltas[nm], new_m[nm], new_v[nm] = g, d, nm_, nv_
        return g

    last_b = update(pack_b, part_b, sib_b)
    part_a, sib_a = _sib_wait(swap_a, last_b, "grad_swap_wait_a")
    update(pack_a, part_a, sib_a)

    vec_names = ["attn_norm", "mlp_norm", "final_norm", "q_norm", "kv_norm", "fox_f_bias"]

    def vec_pack(arrs):
        return jnp.concatenate([_pad_cols(a.reshape(1, -1), vec_w) for a in arrs]
                               + [jnp.zeros((2, vec_w), F32)], axis=0)[None]

    vg, vd, vm, vv = _adamw(vec_pack([weights[n] for n in vec_names]), vsum, jnp.zeros_like(vsum),
                            vec_pack([moments[n][0] for n in vec_names]), vec_pack([moments[n][1] for n in vec_names]),
                            "adamw_vectors")
    for r, nm in enumerate(vec_names):
        shp = weights[nm].shape
        n = weights[nm].size
        grads[nm] = vsum[r, :n].reshape(shp)
        deltas[nm], new_m[nm], new_v[nm] = (vd[0, r, :n].reshape(shp), vm[0, r, :n].reshape(shp),
                                            vv[0, r, :n].reshape(shp))
    loss = vsum[6, 0]

    for res in (grads, deltas, new_m, new_v):
        res["w_in"] = flip(res["w_in"])
    order = ["attn_norm", "w_in", "fox_f_bias", "q_norm", "w_uq", "kv_norm", "w_ukv", "w_mla_branch", "w_fox_branch",
             "w_out", "mlp_norm", "w_up", "w_down", "final_norm"]
    return (loss, grad_x[None], *[grads[n] for n in order], *[deltas[n] for n in order],
            *[new_m[n] for n in order], *[new_v[n] for n in order])
```

```python
import math

import jax
import jax.numpy as jnp
from jax import lax
from jax.experimental import pallas as pl
from jax.experimental.pallas import tpu as pltpu

CHUNK = 64
MLA_HEADS = 8
MLA_Q_LORA = 512
MLA_KV_LORA = 256
MLA_NOPE = 128
MLA_ROPE = 64
MLA_V = 128
ROPE_THETA = 10000.0
FOX_HEADS = 8
FOX_HEAD_DIM = 128
EPS = 1e-6

ADAM_LR = 0.001
ADAM_B1 = 0.9
ADAM_B2 = 0.999
ADAM_EPS = 1e-08
ADAM_WD = 0.01
ADAM_STEP = 10

LANE = 128
QPAD = 2 * LANE
N_CHIPS = 4
N_DEV = 8
VMEM_LIMIT = 48 * 1024 * 1024
ATT_T = 1024
QSUB = 256
ROW_T = 256
PACK_ROWS = 256
LOG2E = 1.4426950408889634

BF16 = jnp.bfloat16
F32 = jnp.float32
MESH = pl.DeviceIdType.MESH

_NT = (((1,), (1,)), ((), ()))
_TN = (((0,), (0,)), ((), ()))
_NN = (((1,), (0,)), ((), ()))


def _tile(dim, pref, align=LANE):
    if dim <= pref:
        return dim
    t = (pref // align) * align
    while t >= align:
        if dim % t == 0:
            return t
        t -= align
    return dim


def _params(sem=None):
    return pltpu.CompilerParams(dimension_semantics=sem, vmem_limit_bytes=VMEM_LIMIT)


_ANY_SPEC = pl.BlockSpec(memory_space=pl.ANY)


def _matmul(a, b, mode, out_dtypes, name, *, tm=1024, tn=512, tk=2048, extras=(), row_extras=(), col_extras=(),
            epilogue=None, order=None, into=None, b_in=None):
    b_shape = b.shape if b_in is None else b_in[0]
    if mode == "nn":
        (M, K), (K2, N) = a.shape, b_shape
    elif mode == "nt":
        (M, K), (N, K2) = a.shape, b_shape
    else:
        (K, M), (K2, N) = a.shape, b_shape
    assert K == K2, (name, a.shape, b_shape)
    tm, tn, tk = _tile(M, tm), _tile(N, tn), _tile(K, tk)
    nk = K // tk
    extras = [e if isinstance(e, tuple) else (e, 0) for e in extras]
    n_out = len(out_dtypes)
    n_ex = len(extras) + len(row_extras) + len(col_extras)
    n_ord = 0 if order is None else 1
    assert all(r.shape == (M, tn) for r in row_extras), name
    dims = {"nn": _NN, "nt": _NT, "tn": _TN}[mode]

    def body(*refs):
        a_ref, b_ref = refs[0], refs[1]
        ex_refs = refs[2:2 + n_ex]
        o_refs = refs[2 + n_ex + n_ord:2 + n_ex + n_ord + n_out]
        acc_ref = refs[2 + n_ex + n_ord + n_out]
        k = pl.program_id(2)
        part = lax.dot_general(a_ref[...], b_ref[...], dims, preferred_element_type=F32)

        @pl.when(k == 0)
        def _():
            acc_ref[...] = part

        @pl.when(k > 0)
        def _():
            acc_ref[...] += part

        @pl.when(k == nk - 1)
        def _():
            acc = acc_ref[...]
            if epilogue is None:
                outs = (acc,)
            else:
                outs = epilogue(acc, *[r[...] for r in ex_refs])
            for o_ref, o in zip(o_refs, outs):
                o_ref[...] = o.astype(o_ref.dtype)

    if mode == "nn":
        a_spec = pl.BlockSpec((tm, tk), lambda i, j, k: (i, k))
        b_spec = pl.BlockSpec((tk, tn), lambda i, j, k: (k, j))
    elif mode == "nt":
        a_spec = pl.BlockSpec((tm, tk), lambda i, j, k: (i, k))
        b_spec = pl.BlockSpec((tn, tk), lambda i, j, k: (j, k))
    else:
        a_spec = pl.BlockSpec((tk, tm), lambda i, j, k: (k, i))
        b_spec = pl.BlockSpec((tk, tn), lambda i, j, k: (k, j))
    if b_in is not None:
        b_block = (None, tn, tk) if mode == "nt" else (None, tk, tn)
        b_spec = pl.BlockSpec(b_block, lambda i, j, k: b_in[1](j, k))
    mn_spec = pl.BlockSpec((tm, tn), lambda i, j, k: (i, j))
    row_spec = pl.BlockSpec((tm, tn), lambda i, j, k: (i, 0))
    col_spec = pl.BlockSpec((1, tn), lambda i, j, k: (0, j))
    out_specs = [mn_spec] * n_out
    out_shape = [jax.ShapeDtypeStruct((M, N), dt) for dt in out_dtypes]
    aliases = {}
    if into is not None:
        buf, place = into
        assert n_out == 1 and n_ord == 1 and order is buf, name
        out_specs = [pl.BlockSpec((None, tm, tn), lambda i, j, k: place(i, j))]
        out_shape = [jax.ShapeDtypeStruct(buf.shape, buf.dtype)]
        aliases = {2 + n_ex: 0}
    outs = pl.pallas_call(
        body,
        name=name,
        grid=(M // tm, N // tn, nk),
        in_specs=([a_spec, b_spec]
                  + [pl.BlockSpec((tm, tn), lambda i, j, k, g=g: (i, j + g * (N // tn))) for _, g in extras]
                  + [row_spec] * len(row_extras) + [col_spec] * len(col_extras) + [_ANY_SPEC] * n_ord),
        out_specs=out_specs,
        out_shape=out_shape,
        scratch_shapes=[pltpu.VMEM((tm, tn), F32)],
        input_output_aliases=aliases,
        compiler_params=_params(("parallel", "parallel", "arbitrary")),
    )(a, b, *[e for e, _ in extras], *row_extras, *col_extras, *([] if order is None else [order]))
    return outs[0] if n_out == 1 else outs


def _matmul_parts(parts, b, mode, out_dtype, name, *, tm=1024, tn=1024, tk=1024, order=None):
    assert mode in ("nn", "tn")
    if mode == "nn":
        M, (K, N) = parts[0].shape[0], b.shape
        widths = [p.shape[1] for p in parts]
    else:
        K, N = b.shape
        widths = [p.shape[1] for p in parts]
        M = sum(widths)
    common = math.gcd(*widths)
    tm, tn, tk = _tile(M if mode == "nn" else common, tm), _tile(N, tn), _tile(common if mode == "nn" else K, tk)
    t_part = tk if mode == "nn" else tm
    assert sum(widths) == (K if mode == "nn" else M), name
    if any(w % t_part for w in widths):
        parts, widths = [jnp.concatenate(parts, axis=1)], [sum(widths)]
    lo =[sum(widths[:p]) // t_part for p in range(len(parts))]
    cnt = [w // t_part for w in widths]
    nk = K // tk
    n_parts = len(parts)
    n_ord = 0 if order is None else 1
    dims = _NN if mode == "nn" else _TN

    def body(*refs):
        a_refs = refs[0:n_parts]
        b_ref = refs[n_parts]
        o_ref, acc_ref = refs[n_parts + 1 + n_ord], refs[n_parts + 2 + n_ord]
        i, k = pl.program_id(0), pl.program_id(2)
        sel = k if mode == "nn" else i
        for p in range(n_parts):
            @pl.when((sel >= lo[p]) & (sel < lo[p] + cnt[p]))
            def _(p=p):
                part = lax.dot_general(a_refs[p][...], b_ref[...], dims, preferred_element_type=F32)

                @pl.when(k == 0)
                def _():
                    acc_ref[...] = part

                @pl.when(k > 0)
                def _():
                    acc_ref[...] += part

        @pl.when(k == nk - 1)
        def _():
            o_ref[...] = acc_ref[...].astype(o_ref.dtype)

    def a_spec(p):
        if mode == "nn":
            return pl.BlockSpec((tm, tk), lambda i, j, k: (i, jnp.clip(k - lo[p], 0, cnt[p] - 1)))
        return pl.BlockSpec((tk, tm), lambda i, j, k: (
            jnp.where((i >= lo[p]) & (i < lo[p] + cnt[p]), k, 0), jnp.clip(i - lo[p], 0, cnt[p] - 1)))

    return pl.pallas_call(
        body, name=name, grid=(M // tm, N // tn, nk),
        in_specs=[a_spec(p) for p in range(n_parts)] + [pl.BlockSpec((tk, tn), lambda i, j, k: (k, j))]
        + [_ANY_SPEC] * n_ord,
        out_specs=pl.BlockSpec((tm, tn), lambda i, j, k: (i, j)),
        out_shape=jax.ShapeDtypeStruct((M, N), out_dtype),
        scratch_shapes=[pltpu.VMEM((tm, tn), F32)],
        compiler_params=_params(("parallel", "parallel", "arbitrary")),
    )(*parts, b, *([] if order is None else [order]))


def _mm_tn(a, b, name, tm=1024, tn=1024, into=None):
    return _matmul(a, b, "tn", [F32], name, tm=tm, tn=tn, tk=2048, into=into,
                   order=None if into is None else into[0])


def _row_spec(ts, width, col=0):
    return pl.BlockSpec((ts, width), lambda i: (i, col))


def _full_spec(shape):
    return pl.BlockSpec(shape, lambda i: tuple(0 for _ in shape))


def _rms(x):
    return lax.rsqrt(jnp.mean(x * x, axis=-1, keepdims=True) + EPS)


def _rms_bwd(x, dy, g):
    r = _rms(x)
    xh = x * r
    gy = dy * g
    dx = r * (gy - xh * jnp.mean(xh * gy, axis=-1, keepdims=True))
    return dx, dy * xh


def _norm_fwd(x, g, name, order=None):
    S, D = x.shape
    ts = _tile(S, ROW_T, 8)

    def body(x_ref, g_ref, *rest):
        o_ref = rest[-1]
        xv = x_ref[...]
        o_ref[...] = ((xv * _rms(xv)) * g_ref[...]).astype(BF16)

    extra = [] if order is None else [order]
    return pl.pallas_call(
        body, name=name, grid=(S // ts,),
        in_specs=[_row_spec(ts, D), _full_spec((1, D))] + [_ANY_SPEC] * len(extra),
        out_specs=_row_spec(ts, D),
        out_shape=jax.ShapeDtypeStruct((S, D), BF16),
        compiler_params=_params(("parallel",)),
    )(x, g, *extra)


def _norm_bwd(x, dy, g, dres, name):
    S, D = x.shape
    ts = _tile(S, ROW_T, 8)

    def body(x_ref, dy_ref, g_ref, dres_ref, dx_ref, dxb_ref, dg_ref):
        dx, dg_rows = _rms_bwd(x_ref[...], dy_ref[...], g_ref[...])
        dx = dres_ref[...] + dx
        dx_ref[...] = dx
        dxb_ref[...] = dx.astype(BF16)

        @pl.when(pl.program_id(0) == 0)
        def _():
            dg_ref[...] = jnp.zeros_like(dg_ref)

        dg_ref[...] += jnp.sum(dg_rows, axis=0, keepdims=True)

    return pl.pallas_call(
        body, name=name, grid=(S // ts,),
        in_specs=[_row_spec(ts, D), _row_spec(ts, D), _full_spec((1, D)), _row_spec(ts, D)],
        out_specs=[_row_spec(ts, D), _row_spec(ts, D), _full_spec((1, D))],
        out_shape=[jax.ShapeDtypeStruct((S, D), F32), jax.ShapeDtypeStruct((S, D), BF16),
                   jax.ShapeDtypeStruct((1, D), F32)],
        compiler_params=_params(("arbitrary",)),
    )(x, dy, g, dres)


def _rope(x, c, sa, sb, sign):
    w = x.shape[-1]
    half = MLA_ROPE // 2
    fwd = pltpu.roll(x, w - half, 1)
    back = pltpu.roll(x, half, 1)
    if sign < 0:
        return x * c - fwd * sa - back * sb
    return x * c + fwd * sa + back * sb


def _split3(x):
    hi = x.astype(BF16)
    r1 = x - hi.astype(F32)
    mid = r1.astype(BF16)
    lo = (r1 - mid.astype(F32)).astype(BF16)
    return hi, mid, lo


def _prep_fwd(small, q_norm, kv_norm, bias_pad, kc, ksa, ksb, n_heads, name):
    S, W = small.shape
    QL, KVL = q_norm.shape[1], kv_norm.shape[1]
    assert W == QL + KVL + 2 * LANE
    ts = _tile(S, ROW_T, 8)
    tri = (lax.broadcasted_iota(jnp.int32, (ts, ts), 0) >= lax.broadcasted_iota(jnp.int32, (ts, ts), 1)).astype(BF16)

    def body(s_ref, qn_ref, kvn_ref, b_ref, kc_ref, ksa_ref, ksb_ref, tri_ref,
             cqn_ref, ckvn_ref, kr_ref, cum_ref, carry_ref):
        cq = s_ref[:, 0:QL]
        cqn_ref[...] = ((cq * _rms(cq)) * qn_ref[...]).astype(BF16)
        ckv = s_ref[:, QL:QL + KVL]
        ckvn_ref[...] = ((ckv * _rms(ckv)) * kvn_ref[...]).astype(BF16)
        kr = s_ref[:, QL + KVL:QL + KVL + LANE]
        kr_ref[...] = _rope(kr, kc_ref[...], ksa_ref[...], ksb_ref[...], 1).astype(BF16)
        z = s_ref[:, QL + KVL + LANE:W] + b_ref[...]
        logf = jnp.minimum(z, 0.0) - jnp.log1p(jnp.exp(-jnp.abs(z)))
        lane = lax.broadcasted_iota(jnp.int32, logf.shape, 1)
        logf = jnp.where(lane < n_heads, logf, 0.0)

        @pl.when(pl.program_id(0) == 0)
        def _():
            carry_ref[...] = jnp.zeros_like(carry_ref)

        t = tri_ref[...]
        cum = carry_ref[...]
        for part in _split3(logf):
            cum = cum + jnp.dot(t, part, preferred_element_type=F32)
        cum_ref[...] = cum
        carry_ref[...] = cum[ts - 1:ts, :]

    return pl.pallas_call(
        body, name=name, grid=(S // ts,),
        in_specs=[_row_spec(ts, W), _full_spec((1, QL)), _full_spec((1, KVL)), _full_spec((1, LANE)),
                  _row_spec(ts, LANE), _row_spec(ts, LANE), _row_spec(ts, LANE), _full_spec((ts, ts))],
        out_specs=[_row_spec(ts, QL), _row_spec(ts, KVL), _row_spec(ts, LANE), _row_spec(ts, LANE)],
        out_shape=[jax.ShapeDtypeStruct((S, QL), BF16), jax.ShapeDtypeStruct((S, KVL), BF16),
                   jax.ShapeDtypeStruct((S, LANE), BF16), jax.ShapeDtypeStruct((S, LANE), F32)],
        scratch_shapes=[pltpu.VMEM((1, LANE), F32)],
        compiler_params=_params(("arbitrary",)),
    )(small, q_norm, kv_norm, bias_pad, kc, ksa, ksb, tri)


def _prep_bwd(small, dcqn, dckvn, dkr_heads, dlogf, q_norm, kv_norm, bias_pad, kc, ksa, ksb, n_heads, name):
    S, W = small.shape
    QL, KVL = q_norm.shape[1], kv_norm.shape[1]
    ts = _tile(S, ROW_T, 8)

    def body(s_ref, dcq_ref, dckv_ref, dkr_ref, dlf_ref, qn_ref, kvn_ref, b_ref, kc_ref, ksa_ref, ksb_ref,
             ds_ref, gq_ref, gkv_ref, gb_ref):
        dcq, gq_rows = _rms_bwd(s_ref[:, 0:QL], dcq_ref[...], qn_ref[...])
        ds_ref[:, 0:QL] = dcq.astype(BF16)
        dckv, gkv_rows = _rms_bwd(s_ref[:, QL:QL + KVL], dckv_ref[...], kvn_ref[...])
        ds_ref[:, QL:QL + KVL] = dckv.astype(BF16)
        dkr = dkr_ref[:, 0:LANE]
        for h in range(1, n_heads):
            dkr = dkr + dkr_ref[:, h * LANE:(h + 1) * LANE]
        ds_ref[:, QL + KVL:QL + KVL + LANE] = _rope(dkr, kc_ref[...], ksa_ref[...], ksb_ref[...], -1).astype(BF16)
        z = s_ref[:, QL + KVL + LANE:W] + b_ref[...]
        dff = dlf_ref[...] * (1.0 / (1.0 + jnp.exp(z)))
        ds_ref[:, QL + KVL + LANE:W] = dff.astype(BF16)

        @pl.when(pl.program_id(0) == 0)
        def _():
            gq_ref[...] = jnp.zeros_like(gq_ref)
            gkv_ref[...] = jnp.zeros_like(gkv_ref)
            gb_ref[...] = jnp.zeros_like(gb_ref)

        gq_ref[...] += jnp.sum(gq_rows, axis=0, keepdims=True)
        gkv_ref[...] += jnp.sum(gkv_rows, axis=0, keepdims=True)
        gb_ref[...] += jnp.sum(dff, axis=0, keepdims=True)

    return pl.pallas_call(
        body, name=name, grid=(S // ts,),
        in_specs=[_row_spec(ts, W), _row_spec(ts, QL), _row_spec(ts, KVL), _row_spec(ts, n_heads * LANE),
                  _row_spec(ts, LANE), _full_spec((1, QL)), _full_spec((1, KVL)), _full_spec((1, LANE)),
                  _row_spec(ts, LANE), _row_spec(ts, LANE), _row_spec(ts, LANE)],
        out_specs=[_row_spec(ts, W), _full_spec((1, QL)), _full_spec((1, KVL)), _full_spec((1, LANE))],
        out_shape=[jax.ShapeDtypeStruct((S, W), BF16), jax.ShapeDtypeStruct((1, QL), F32),
                   jax.ShapeDtypeStruct((1, KVL), F32), jax.ShapeDtypeStruct((1, LANE), F32)],
        compiler_params=_params(("arbitrary",)),
    )(small, dcqn, dckvn, dkr_heads, dlogf, q_norm, kv_norm, bias_pad, kc, ksa, ksb)


def _sigmoid(z):
    return 1.0 / (1.0 + jnp.exp(-z))


def _final(h, g, target, name):
    S, D = h.shape
    ts = _tile(S, ROW_T, 8)

    def body(h_ref, g_ref, t_ref, dh_ref, dhb_ref, dg_ref, loss_ref):
        hv = h_ref[...]
        gv = g_ref[...]
        err = (hv * _rms(hv)) * gv - t_ref[...]
        dh, dg_rows = _rms_bwd(hv, err / D, gv)
        dh_ref[...] = dh
        dhb_ref[...] = dh.astype(BF16)

        @pl.when(pl.program_id(0) == 0)
        def _():
            dg_ref[...] = jnp.zeros_like(dg_ref)
            loss_ref[...] = jnp.zeros_like(loss_ref)

        dg_ref[...] += jnp.sum(dg_rows, axis=0, keepdims=True)
        row_loss = jnp.mean(err * err, axis=-1, keepdims=True)
        loss_ref[...] += 0.5 * jnp.sum(row_loss, axis=0, keepdims=True)

    return pl.pallas_call(
        body, name=name, grid=(S // ts,),
        in_specs=[_row_spec(ts, D), _full_spec((1, D)), _row_spec(ts, D)],
        out_specs=[_row_spec(ts, D), _row_spec(ts, D), _full_spec((1, D)), _full_spec((1, LANE))],
        out_shape=[jax.ShapeDtypeStruct((S, D), F32), jax.ShapeDtypeStruct((S, D), BF16),
                   jax.ShapeDtypeStruct((1, D), F32), jax.ShapeDtypeStruct((1, LANE), F32)],
        compiler_params=_params(("arbitrary",)),
    )(h, g, target)


def _suffix_sum_rows(x, name):
    R, S = x.shape
    tb = _tile(S, 512)
    nb = S // tb
    tri = (lax.broadcasted_iota(jnp.int32, (tb, tb), 0) >= lax.broadcasted_iota(jnp.int32, (tb, tb), 1)).astype(BF16)

    def body(x_ref, tri_ref, o_ref, carry_ref):
        @pl.when(pl.program_id(0) == 0)
        def _():
            carry_ref[...] = jnp.zeros_like(carry_ref)

        xv = x_ref[...]
        t = tri_ref[...]
        acc = jnp.broadcast_to(carry_ref[:, 0:1], xv.shape)
        for part in _split3(xv):
            acc = acc + jnp.dot(part, t, preferred_element_type=F32)
        o_ref[...] = acc
        carry_ref[...] = jnp.broadcast_to(acc[:, 0:1], carry_ref.shape)

    rev = pl.BlockSpec((R, tb), lambda i: (0, nb - 1 - i))
    return pl.pallas_call(
        body, name=name, grid=(nb,),
        in_specs=[rev, _full_spec((tb, tb))], out_specs=rev,
        out_shape=jax.ShapeDtypeStruct((R, S), F32),
        scratch_shapes=[pltpu.VMEM((R, LANE), F32)],
        compiler_params=_params(("arbitrary",)),
    )(x, tri)


def _pairs(nb, by_key):
    if by_key:
        pr = [(i, j) for j in range(nb) for i in range(j, nb)]
    else:
        pr = [(i, j) for i in range(nb) for j in range(i + 1)]
    return (jnp.asarray([p[0] for p in pr], jnp.int32), jnp.asarray([p[1] for p in pr], jnp.int32), len(pr))


class _AttT:
    def __init__(self, S, n_heads, q, ks, v, scale, chunk_causal, cum_rep=None):
        self.S, self.H, self.q, self.ks, self.v = S, n_heads, q, ks, v
        self.scale, self.chunk_causal, self.cum_rep = scale, chunk_causal, cum_rep
        self.T = _tile(S, ATT_T)
        self.qs = min(QSUB, self.T)
        self.nb = S // self.T
        self.dq, self.dv = q[1], v[1]
        self.has_bias = cum_rep is not None

    def q_spec(self, op):
        _, w, off, per_head = op
        return pl.BlockSpec((self.T, w), lambda h, p, it, jt: (it[p], off + (h if per_head else 0)))

    def k_spec(self, op):
        _, w, off, per_head = op
        return pl.BlockSpec((self.T, w), lambda h, p, it, jt: (jt[p], off + (h if per_head else 0)))

    def row_q(self):
        return pl.BlockSpec((None, 1, self.T), lambda h, p, it, jt: (h, 0, it[p]))

    def cum_k(self):
        return pl.BlockSpec((None, self.T, self.qs), lambda h, p, it, jt: (h, jt[p], 0))

    def sub_blocks(self, masked):
        return [(q0, min(self.T, q0 + self.qs) if masked else self.T) for q0 in range(0, self.T, self.qs)]

    def scores(self, k, q_sub, cum, q0, masked):
        s = lax.dot_general(k, q_sub, _NT, preferred_element_type=F32)
        if self.has_bias:
            s = s - cum
        mask = None
        if masked:
            r = lax.broadcasted_iota(jnp.int32, s.shape, 0)
            c = lax.broadcasted_iota(jnp.int32, s.shape, 1) + q0
            mask = (r // CHUNK <= c // CHUNK) if self.chunk_causal else (r <= c)
        return s, mask


def _join(k_refs):
    return k_refs[0][...] if len(k_refs) == 1 else jnp.concatenate([r[...] for r in k_refs], axis=-1)


def _att_fwd_t(att, name, exact=False):
    S, H, T, qs = att.S, att.H, att.T, att.qs
    it, jt, npairs = _pairs(att.nb, by_key=False)
    nk = len(att.ks)

    def body(it_ref, jt_ref, *refs):
        q_ref = refs[0]
        k_refs = refs[1:1 + nk]
        v_ref = refs[1 + nk]
        n = 2 + nk
        cum_ref = None
        if att.has_bias:
            cum_ref = refs[n]
            n += 1
        o_ref = refs[n]
        n += 1
        ox_ref = None
        if exact:
            ox_ref = refs[n]
            n += 1
        lse_ref, m_ref, l_ref, acc_ref = refs[n:n + 4]
        lo_ref = refs[n + 4] if exact else None
        p = pl.program_id(1)
        i, j = it_ref[p], jt_ref[p]

        @pl.when(j == 0)
        def _():
            m_ref[...] = jnp.full_like(m_ref, -jnp.inf)
            l_ref[...] = jnp.zeros_like(l_ref)
            acc_ref[...] = jnp.zeros_like(acc_ref)
            if exact:
                lo_ref[...] = jnp.zeros_like(lo_ref)

        def step(masked):
            k = _join(k_refs)
            v = v_ref[...]
            subs = att.sub_blocks(masked)

            def logits(idx):
                q0, nkeys = subs[idx]
                cum = cum_ref[0:nkeys, :] if att.has_bias else None
                return att.scores(k[0:nkeys], q_ref[q0:q0 + qs, :], cum, q0, masked)

            ahead = logits(0)
            for idx, (q0, nkeys) in enumerate(subs):
                qsl = slice(q0, q0 + qs)
                s, mask = ahead
                if idx + 1 < len(subs):
                    ahead = logits(idx + 1)
                if masked:
                    s = jnp.where(mask, s, -jnp.inf)
                m_prev = m_ref[:, qsl]
                m_new = jnp.maximum(m_prev, jnp.max(s, axis=0, keepdims=True))
                alpha = jnp.exp2(m_prev - m_new)
                pr = jnp.exp2(s - m_new)
                l_ref[:, qsl] = alpha * l_ref[:, qsl] + jnp.sum(pr, axis=0, keepdims=True)
                p_hi = pr.astype(BF16)
                acc_ref[:, qsl] = alpha * acc_ref[:, qsl] + lax.dot_general(
                    v[0:nkeys], p_hi, _TN, preferred_element_type=F32)
                if exact:
                    p_lo = (pr - p_hi.astype(F32)).astype(BF16)
                    lo_ref[:, qsl] = alpha * lo_ref[:, qsl] + lax.dot_general(
                        v[0:nkeys], p_lo, _TN, preferred_element_type=F32)
                m_ref[:, qsl] = m_new

        @pl.when(j < i)
        def _():
            step(False)

        @pl.when(j == i)
        def _():
            step(True)
            l = l_ref[...]
            inv = 1.0 / l
            o_ref[...] = jnp.transpose(acc_ref[...] * inv).astype(o_ref.dtype)
            if exact:
                ox_ref[...] = jnp.transpose((acc_ref[...] + lo_ref[...]) * inv)
            lse_ref[...] = m_ref[...] + jnp.log2(l)

    in_specs = [att.q_spec(att.q)] + [att.k_spec(k) for k in att.ks] + [att.k_spec(att.v)]
    args = [att.q[0]] + [k[0] for k in att.ks] + [att.v[0]]
    if att.has_bias:
        in_specs.append(att.cum_k())
        args.append(att.cum_rep)
    o_spec = pl.BlockSpec((T, att.dv), lambda h, p, it, jt: (it[p], h))
    out_specs = [o_spec]
    out_shape = [jax.ShapeDtypeStruct((S, H * att.dv), BF16)]
    scratch = [pltpu.VMEM((1, T), F32), pltpu.VMEM((1, T), F32), pltpu.VMEM((att.dv, T), F32)]
    if exact:
        out_specs.append(o_spec)
        out_shape.append(jax.ShapeDtypeStruct((S, H * att.dv), F32))
        scratch.append(pltpu.VMEM((att.dv, T), F32))
    out_specs.append(att.row_q())
    out_shape.append(jax.ShapeDtypeStruct((H, 1, S), F32))
    return pl.pallas_call(
        body, name=name,
        grid_spec=pltpu.PrefetchScalarGridSpec(
            num_scalar_prefetch=2, grid=(H, npairs), in_specs=in_specs, out_specs=out_specs,
            scratch_shapes=scratch),
        out_shape=out_shape,
        compiler_params=_params(("parallel", "arbitrary")),
    )(it, jt, *args)


def _att_delta_t(do, o, n_heads, name, order=None):
    S = do.shape[0]
    w = do.shape[1] // n_heads
    ts = _tile(S, ATT_T)
    ones = jnp.ones((8, w), BF16)
    extra = [] if order is None else [order]

    def body(do_ref, o_ref, ones_ref, *rest):
        d_ref = rest[-1]
        prod = do_ref[...].astype(F32) * o_ref[...].astype(F32)
        acc = jnp.zeros((8, ts), F32)
        for part in _split3(prod):
            acc = acc + lax.dot_general(ones_ref[...], part, _NT, preferred_element_type=F32)
        d_ref[...] = acc[0:1, :]

    blk = pl.BlockSpec((ts, w), lambda i, h: (i, h))
    return pl.pallas_call(
        body, name=name, grid=(S // ts, n_heads),
        in_specs=[blk, blk, pl.BlockSpec((8, w), lambda i, h: (0, 0))] + [_ANY_SPEC] * len(extra),
        out_specs=pl.BlockSpec((None, 1, ts), lambda i, h: (h, 0, i)),
        out_shape=jax.ShapeDtypeStruct((n_heads, 1, S), F32),
        compiler_params=_params(("parallel", "parallel")),
    )(do, o, ones, *extra)


def _att_bwd_t(att, do, lse, delta, dq_dtype, dk_dtypes, name, dq_rope=None):
    S, H, T, qs = att.S, att.H, att.T, att.qs
    it, jt, npairs = _pairs(att.nb, by_key=True)
    nk = len(att.ks)
    last = att.nb - 1
    widths = [k[1] for k in att.ks]

    def body(it_ref, jt_ref, *refs):
        q_ref = refs[0]
        k_refs = refs[1:1 + nk]
        v_ref, do_ref, lse_ref, dl_ref = refs[1 + nk:5 + nk]
        n = 5 + nk
        cum_ref = None
        if att.has_bias:
            cum_ref = refs[n]
            n += 1
        rope_refs = None
        if dq_rope is not None:
            rope_refs = refs[n:n + 3]
            n += 3
        dq_ref = refs[n]
        dk_refs = refs[n + 1:n + 1 + nk]
        dv_ref = refs[n + 1 + nk]
        n += nk + 2
        dc_ref = None
        if att.has_bias:
            dc_ref = refs[n]
            n += 1
        dq_acc, dk_acc, dv_acc = refs[n:n + 3]
        dc_acc = refs[n + 3] if att.has_bias else None
        p = pl.program_id(1)
        i, j = it_ref[p], jt_ref[p]

        @pl.when(p == 0)
        def _():
            dq_acc[...] = jnp.zeros_like(dq_acc)

        @pl.when(i == j)
        def _():
            dk_acc[...] = jnp.zeros_like(dk_acc)
            dv_acc[...] = jnp.zeros_like(dv_acc)
            if att.has_bias:
                dc_acc[...] = jnp.zeros_like(dc_acc)

        def step(masked):
            k = _join(k_refs)
            v = v_ref[...]
            subs = att.sub_blocks(masked)

            def logits(idx):
                q0, nkeys = subs[idx]
                cum = cum_ref[0:nkeys, :] if att.has_bias else None
                return att.scores(k[0:nkeys], q_ref[q0:q0 + qs, :], cum, q0, masked)

            ahead = logits(0)
            for idx, (q0, nkeys) in enumerate(subs):
                qsl = slice(q0, q0 + qs)
                ksl = slice(0, nkeys)
                q_sub = q_ref[qsl, :]
                do_sub = do_ref[qsl, :]
                s, mask = ahead
                if idx + 1 < len(subs):
                    ahead = logits(idx + 1)
                pr = jnp.exp2(s - lse_ref[:, qsl])
                if masked:
                    pr = jnp.where(mask, pr, 0.0)
                dp = lax.dot_general(v[ksl], do_sub, _NT, preferred_element_type=F32)
                ds = pr * (dp - dl_ref[:, qsl])
                ds_b = ds.astype(BF16)
                dv_acc[ksl, :] += jnp.dot(pr.astype(BF16), do_sub, preferred_element_type=F32)
                dk_acc[ksl, :] += jnp.dot(ds_b, q_sub, preferred_element_type=F32)
                dq_acc[i, :, qsl] += lax.dot_general(k[ksl], ds_b, _TN, preferred_element_type=F32)
                if att.has_bias:
                    part = ds[:, 0:LANE] if qs >= LANE else ds
                    for c0 in range(LANE, qs, LANE):
                        part = part + ds[:, c0:c0 + LANE]
                    dc_acc[ksl, :] += part

        @pl.when(i > j)
        def _():
            step(False)

        @pl.when(i == j)
        def _():
            step(True)
            dq = jnp.transpose(dq_acc[i] * att.scale)
            if dq_rope is not None:
                dq = _rope(dq, rope_refs[0][...], rope_refs[1][...], rope_refs[2][...], -1)
            dq_ref[...] = dq.astype(dq_ref.dtype)

        @pl.when(i == last)
        def _():
            dk = dk_acc[...] * (1.0 / LOG2E)
            off = 0
            for r, w in zip(dk_refs, widths):
                r[...] = dk[:, off:off + w].astype(r.dtype)
                off += w
            dv_ref[...] = dv_acc[...].astype(dv_ref.dtype)
            if att.has_bias:
                dc_ref[...] = -jnp.sum(dc_acc[...], axis=-1, keepdims=True)

    do_op = (do, att.dv, 0, True)
    in_specs = ([att.q_spec(att.q)] + [att.k_spec(k) for k in att.ks]
                + [att.k_spec(att.v), att.q_spec(do_op), att.row_q(), att.row_q()])
    args = [att.q[0]] + [k[0] for k in att.ks] + [att.v[0], do, lse, delta]
    if att.has_bias:
        in_specs.append(att.cum_k())
        args.append(att.cum_rep)
    if dq_rope is not None:
        in_specs += [pl.BlockSpec((T, att.dq), lambda h, p, it, jt: (jt[p], 0))] * 3
        args += list(dq_rope)
    out_specs = [pl.BlockSpec((T, att.dq), lambda h, p, it, jt: (jt[p], h))]
    out_shape = [jax.ShapeDtypeStruct((S, H * att.dq), dq_dtype)]
    out_specs += [pl.BlockSpec((T, w), lambda h, p, it, jt: (jt[p], h)) for w in widths]
    out_shape += [jax.ShapeDtypeStruct((S, H * w), dt) for w, dt in zip(widths, dk_dtypes)]
    out_specs.append(pl.BlockSpec((T, att.dv), lambda h, p, it, jt: (jt[p], h)))
    out_shape.append(jax.ShapeDtypeStruct((S, H * att.dv), BF16))
    scratch = [pltpu.VMEM((att.nb, att.dq, T), F32), pltpu.VMEM((T, att.dq), F32), pltpu.VMEM((T, att.dv), F32)]
    if att.has_bias:
        out_specs.append(pl.BlockSpec((None, T, 1), lambda h, p, it, jt: (h, jt[p], 0)))
        out_shape.append(jax.ShapeDtypeStruct((H, S, 1), F32))
        scratch.append(pltpu.VMEM((T, min(qs, LANE)), F32))
    return pl.pallas_call(
        body, name=name,
        grid_spec=pltpu.PrefetchScalarGridSpec(
            num_scalar_prefetch=2, grid=(H, npairs), in_specs=in_specs, out_specs=out_specs,
            scratch_shapes=scratch),
        out_shape=out_shape,
        compiler_params=_params(("parallel", "arbitrary")),
    )(it, jt, *args)


def _adamw(w, g1, g2, m, v, name, g_row=None):
    _, K, N = w.shape
    by_rows = K % 8 == 0
    tr = _tile(K, 256, 8) if by_rows else K
    if g_row is None:
        assert g1.shape == (K, N) and g2.shape == (K, N), name
        g_row = 0
    assert by_rows and g_row % tr == 0 or g_row == 0, name
    g_blk = g_row // tr
    tc = N if by_rows else _tile(N, LANE)
    c1 = 1.0 - ADAM_B1 ** ADAM_STEP
    c2 = 1.0 - ADAM_B2 ** ADAM_STEP

    def body(w_ref, g1_ref, g2_ref, m_ref, v_ref, g_ref, d_ref, nm_ref, nv_ref):
        gv = g1_ref[...] + g2_ref[...]
        nm = ADAM_B1 * m_ref[...] + (1.0 - ADAM_B1) * gv
        nv = ADAM_B2 * v_ref[...] + (1.0 - ADAM_B2) * (gv * gv)
        g_ref[...] = gv
        d_ref[...] = -ADAM_LR * ((nm / c1) / (jnp.sqrt(nv / c2) + ADAM_EPS) + ADAM_WD * w_ref[...])
        nm_ref[...] = nm
        nv_ref[...] = nv

    if by_rows:
        blk = pl.BlockSpec((None, tr, N), lambda i: (0, i, 0))
        gblk = pl.BlockSpec((tr, N), lambda i: (g_blk + i, 0))
    else:
        blk = pl.BlockSpec((None, K, tc), lambda i: (0, 0, i))
        gblk = pl.BlockSpec((K, tc), lambda i: (0, i))
    return pl.pallas_call(
        body, name=name, grid=(K // tr if by_rows else N // tc,),
        in_specs=[blk, gblk, gblk, blk, blk], out_specs=[blk] * 4,
        out_shape=[jax.ShapeDtypeStruct((1, K, N), F32)] * 4,
        compiler_params=_params(("parallel",)),
    )(w, g1, g2, m, v)


_HBM_SPEC = pl.BlockSpec(memory_space=pltpu.HBM)
_SEM_SPEC = pl.BlockSpec(memory_space=pltpu.SEMAPHORE)
_VMEM_SPEC = pl.BlockSpec(memory_space=pltpu.VMEM)
_EFFECT = pltpu.SideEffectType.DATAFLOW_SIDE_EFFECTING


def _place():
    return lax.axis_index("x"), lax.axis_index("y"), lax.axis_index("c")


def _other_chips(x, y):
    return [(1 - x, y), (x, 1 - y), (1 - x, 1 - y)]


def _all_gather_halves(wp, name):
    R, C = wp.shape
    half = R // 2
    assert half % 16 == 0

    def body(w_ref, out_ref, ici_send, ici_recv, d2d_send, d2d_recv, local_sem):
        x, y, c = _place()
        me = 2 * x + y
        chips = _other_chips(x, y)
        mine = pl.ds(pl.multiple_of(c * half, 16), half)
        theirs = pl.ds(pl.multiple_of((1 - c) * half, 16), half)
        local = pltpu.make_async_copy(w_ref, out_ref.at[me], local_sem)
        local.start()
        sends = []
        for n, (px, py) in enumerate(chips):
            cp = pltpu.make_async_remote_copy(
                src_ref=w_ref.at[mine], dst_ref=out_ref.at[me, mine], send_sem=ici_send.at[n],
                recv_sem=ici_recv.at[n], device_id=(px, py, c), device_id_type=MESH)
            cp.start()
            sends.append(cp)
        for n, (px, py) in enumerate(chips):
            slot = 2 * px + py
            pltpu.make_async_remote_copy(
                src_ref=w_ref.at[mine], dst_ref=out_ref.at[slot, mine], send_sem=ici_send.at[n],
                recv_sem=ici_recv.at[n], device_id=(px, py, c), device_id_type=MESH).wait_recv()
            cp = pltpu.make_async_remote_copy(
                src_ref=out_ref.at[slot, mine], dst_ref=out_ref.at[slot, mine], send_sem=d2d_send.at[n],
                recv_sem=d2d_recv.at[n], device_id=(x, y, 1 - c), device_id_type=MESH)
            cp.start()
            sends.append(cp)
        for n, (px, py) in enumerate(chips):
            slot = 2 * px + py
            pltpu.make_async_remote_copy(
                src_ref=out_ref.at[slot, theirs], dst_ref=out_ref.at[slot, theirs], send_sem=d2d_send.at[n],
                recv_sem=d2d_recv.at[n], device_id=(x, y, 1 - c), device_id_type=MESH).wait_recv()
        for cp in sends:
            cp.wait_send()
        local.wait()

    return pl.pallas_call(
        body, name=name,
        in_specs=[_ANY_SPEC], out_specs=_ANY_SPEC,
        out_shape=jax.ShapeDtypeStruct((N_CHIPS, R, C), wp.dtype),
        scratch_shapes=[pltpu.SemaphoreType.DMA((3,)), pltpu.SemaphoreType.DMA((3,)), pltpu.SemaphoreType.DMA((3,)),
                        pltpu.SemaphoreType.DMA((3,)), pltpu.SemaphoreType.DMA],
    )(wp)


def _chip_copies(src_ref, land_ref, sems, gather):
    x, y, c = _place()
    me = 2 * x + y
    out, back = [], []
    for n, (px, py) in enumerate(_other_chips(x, y)):
        src = src_ref if gather else src_ref.at[2 * px + py]
        out.append(pltpu.make_async_remote_copy(
            src_ref=src, dst_ref=land_ref.at[me] if gather else land_ref.at[n],
            send_sem=sems[n], recv_sem=sems[3 + n], device_id=(px, py, c), device_id_type=MESH))
        back.append(pltpu.make_async_remote_copy(
            src_ref=src, dst_ref=land_ref.at[2 * px + py] if gather else land_ref.at[n],
            send_sem=sems[n], recv_sem=sems[3 + n], device_id=(px, py, c), device_id_type=MESH))
    return out, back


def _xchg_start(src, land, gather, order, name):
    def body(src_ref, land_ref, order_ref, *outs):
        sems = outs[0:6]
        token = outs[8]
        out, _ = _chip_copies(src_ref, land_ref, sems, gather)
        for cp in out:
            cp.start()
        token[...] = jnp.zeros_like(token)

    outs = pl.pallas_call(
        body, name=name,
        out_shape=(pltpu.SemaphoreType.DMA(()),) * 6 + (
            pltpu.HBM(src.shape, src.dtype), pltpu.HBM(land.shape, land.dtype),
            jax.ShapeDtypeStruct((8, LANE), F32)),
        in_specs=(_HBM_SPEC, _HBM_SPEC, _ANY_SPEC),
        out_specs=(_SEM_SPEC,) * 6 + (_HBM_SPEC, _HBM_SPEC, _VMEM_SPEC),
        input_output_aliases={0: 6, 1: 7},
        compiler_params=pltpu.CompilerParams(has_side_effects=_EFFECT),
    )(pltpu.with_memory_space_constraint(src, pltpu.HBM), pltpu.with_memory_space_constraint(land, pltpu.HBM), order)
    return outs[0:6], outs[6], outs[7], outs[8]


def _xchg_wait(started, gather, after, name):
    sems, src, land, _ = started

    def body(src_ref, land_ref, *rest):
        _, back = _chip_copies(src_ref, land_ref, rest[0:6], gather)
        for cp in back:
            cp.wait_send()
            cp.wait_recv()

    return pl.pallas_call(
        body, name=name,
        out_shape=(pltpu.HBM(src.shape, src.dtype), pltpu.HBM(land.shape, land.dtype)),
        in_specs=(_HBM_SPEC, _HBM_SPEC) + (_SEM_SPEC,) * 6 + (_ANY_SPEC,),
        out_specs=(_HBM_SPEC, _HBM_SPEC),
        input_output_aliases={0: 0, 1: 1},
        compiler_params=pltpu.CompilerParams(has_side_effects=_EFFECT),
    )(src, land, *sems, after)


def _sib_copy(src_ref, land_ref, send_sem, recv_sem):
    x, y, c = _place()
    return pltpu.make_async_remote_copy(src_ref=src_ref, dst_ref=land_ref, send_sem=send_sem, recv_sem=recv_sem,
                                        device_id=(x, y, 1 - c), device_id_type=MESH)


def _sib_start(src, name):
    land = lax.empty(src.shape, src.dtype)

    def body(src_ref, land_ref, send_sem, recv_sem, src_thru, land_thru, token):
        _sib_copy(src_ref, land_ref, send_sem, recv_sem).start()
        token[...] = jnp.zeros_like(token)

    return pl.pallas_call(
        body, name=name,
        out_shape=(pltpu.SemaphoreType.DMA(()), pltpu.SemaphoreType.DMA(()),
                   pltpu.HBM(src.shape, src.dtype), pltpu.HBM(land.shape, land.dtype),
                   jax.ShapeDtypeStruct((8, LANE), F32)),
        in_specs=(_HBM_SPEC, _HBM_SPEC),
        out_specs=(_SEM_SPEC, _SEM_SPEC, _HBM_SPEC, _HBM_SPEC, _VMEM_SPEC),
        input_output_aliases={0: 2, 1: 3},
        compiler_params=pltpu.CompilerParams(has_side_effects=_EFFECT),
    )(pltpu.with_memory_space_constraint(src, pltpu.HBM), pltpu.with_memory_space_constraint(land, pltpu.HBM))


def _sib_wait(started, after, name):
    send_sem, recv_sem, src, land, _ = started

    def body(src_ref, land_ref, send_sem, recv_sem, after_ref, src_out, land_out):
        cp = _sib_copy(src_ref, land_ref, send_sem, recv_sem)
        cp.wait_send()
        cp.wait_recv()

    return pl.pallas_call(
        body, name=name,
        out_shape=(pltpu.HBM(src.shape, src.dtype), pltpu.HBM(land.shape, land.dtype)),
        in_specs=(_HBM_SPEC, _HBM_SPEC, _SEM_SPEC, _SEM_SPEC, _ANY_SPEC),
        out_specs=(_HBM_SPEC, _HBM_SPEC),
        input_output_aliases={0: 0, 1: 1},
        compiler_params=pltpu.CompilerParams(has_side_effects=_EFFECT),
    )(src, land, send_sem, recv_sem, after)


def _sum_slabs(gp, recv, chip, name):
    _, R, C = gp.shape
    tr = _tile(R, PACK_ROWS, 16)

    def body(chip_ref, own_ref, r0_ref, r1_ref, r2_ref, o_ref):
        acc = own_ref[...].astype(F32) + r0_ref[...].astype(F32)
        o_ref[...] = (acc + r1_ref[...].astype(F32)) + r2_ref[...].astype(F32)

    def got(n):
        return pl.BlockSpec((None, tr, C), lambda i, chip_ref: (n, i, 0))

    return pl.pallas_call(
        body, name=name,
        grid_spec=pltpu.PrefetchScalarGridSpec(
            num_scalar_prefetch=1, grid=(R // tr,),
            in_specs=[pl.BlockSpec((None, tr, C), lambda i, chip_ref: (chip_ref[0], i, 0)), got(0), got(1), got(2)],
            out_specs=pl.BlockSpec((tr, C), lambda i, chip_ref: (i, 0))),
        out_shape=jax.ShapeDtypeStruct((R, C), F32),
        compiler_params=_params(("parallel",)),
    )(jnp.reshape(chip, (1,)).astype(jnp.int32), gp, recv, recv, recv)


def _all_reduce_vec(vec, name):
    VR, W = vec.shape

    def body(vec_ref, vall_ref, vout_ref, vsend_sems, vrecv_sems):
        x, y, c = _place()
        vall_ref[4 * x + 2 * y + c] = vec_ref[...]
        sends = []
        peers = []
        for r in range(1, N_DEV):
            dx, dy, dc = (r >> 2) & 1, (r >> 1) & 1, r & 1
            peer = (x ^ dx, y ^ dy, c ^ dc)
            peers.append(peer)
            cp = pltpu.make_async_remote_copy(
                src_ref=vec_ref, dst_ref=vall_ref.at[4 * x + 2 * y + c], send_sem=vsend_sems.at[r - 1],
                recv_sem=vrecv_sems.at[r - 1], device_id=peer, device_id_type=MESH)
            cp.start()
            sends.append(cp)
        for r, peer in enumerate(peers):
            pltpu.make_async_remote_copy(
                src_ref=vec_ref, dst_ref=vall_ref.at[4 * peer[0] + 2 * peer[1] + peer[2]],
                send_sem=vsend_sems.at[r], recv_sem=vrecv_sems.at[r],
                device_id=peer, device_id_type=MESH).wait_recv()
        total = vall_ref[0]
        for d in range(1, N_DEV):
            total = total + vall_ref[d]
        vout_ref[...] = total
        for cp in sends:
            cp.wait_send()

    outs = pl.pallas_call(
        body, name=name,
        in_specs=[_VMEM_SPEC], out_specs=[_VMEM_SPEC, _VMEM_SPEC],
        out_shape=[jax.ShapeDtypeStruct((N_DEV, VR, W), F32), jax.ShapeDtypeStruct((VR, W), F32)],
        scratch_shapes=[pltpu.SemaphoreType.DMA((N_DEV - 1,)), pltpu.SemaphoreType.DMA((N_DEV - 1,))],
    )(vec)
    return outs[1]


class _Pack:
    def __init__(self, group, C):
        self.group, self.C = group, C
        self.rows, self.offs, off = {}, {}, 0
        for nm, (K, N), _ in group:
            assert N <= C, nm
            self.rows[nm] = K if 2 * N > C else -(-(K * N) // C)
            self.offs[nm] = off
            off += -(-self.rows[nm] // 16) * 16
        self.used = off
        self.R = -(-off // PACK_ROWS) * PACK_ROWS

    def _rows_of(self, a):
        K, N = a.shape
        if 2 * N > self.C:
            a = jnp.pad(a, ((0, 0), (0, self.C - N)))
        else:
            a = jnp.pad(a.reshape(-1), (0, -(K * N) % self.C)).reshape(-1, self.C)
        return jnp.pad(a, ((0, -a.shape[0] % 16), (0, 0)))

    def pack(self, shards):
        parts = [self._rows_of(shards[nm].astype(BF16)) for nm, _, _ in self.group]
        return jnp.concatenate(parts + [jnp.zeros((self.R - self.used, self.C), BF16)], axis=0)

    def _shard_of(self, rows, shape):
        K, N = shape
        return rows[:, :N] if 2 * N > self.C else rows.reshape(-1)[:K * N].reshape(K, N)

    def part(self, flat, nm, shape):
        return self._shard_of(flat[self.offs[nm]:self.offs[nm] + self.rows[nm]], shape)

    def slab_rows(self, nm, g):
        (K, N), axis = next((shape, axis) for n, shape, axis in self.group if n == nm)
        cuts = [g[:, k * N:(k + 1) * N] if axis == 1 else g[k * K:(k + 1) * K, :] for k in range(N_CHIPS)]
        return jnp.stack([self._rows_of(c.astype(BF16)) for c in cuts])

    def slabs(self, grads):
        parts = [self.slab_rows(nm, grads[nm]) for nm, _, _ in self.group]
        return jnp.concatenate(parts + [jnp.zeros((N_CHIPS, self.R - self.used, self.C), BF16)], axis=1)

    def full(self, gathered, names=None):
        res = {}
        for nm, (K, N), axis in self.group:
            if names is None or nm in names:
                rows = gathered[:, self.offs[nm]:self.offs[nm] + self.rows[nm]]
                res[nm] = jnp.concatenate([self._shard_of(rows[k], (K, N)) for k in range(N_CHIPS)], axis=axis)
        return res


def _rope_tables(S):
    pos = jnp.arange(S, dtype=F32)
    inv = 1.0 / (ROPE_THETA ** (jnp.arange(0, MLA_ROPE, 2, dtype=F32) / MLA_ROPE))
    ang = pos[:, None] * inv[None, :]
    cos, sin = jnp.cos(ang), jnp.sin(ang)
    half = MLA_ROPE // 2
    z = jnp.zeros((S, half), F32)
    one = jnp.ones((S, LANE - MLA_ROPE), F32)
    zero = jnp.zeros((S, LANE - MLA_ROPE), F32)
    kc = jnp.concatenate([cos, cos, one], axis=1)
    ksa = jnp.concatenate([-sin, z, zero], axis=1)
    ksb = jnp.concatenate([z, sin, zero], axis=1)
    qc = jnp.concatenate([jnp.ones((S, MLA_NOPE), F32), kc], axis=1)
    qsa = jnp.concatenate([jnp.zeros((S, MLA_NOPE), F32), ksa], axis=1)
    qsb = jnp.concatenate([jnp.zeros((S, MLA_NOPE), F32), ksb], axis=1)
    return (kc, ksa, ksb), (qc, qsa, qsb)


def _pad_cols(a, width):
    return jnp.pad(a, ((0, 0), (0, width - a.shape[1])))


def kernel(x, attn_norm, w_in, fox_f_bias, q_norm, w_uq, kv_norm, w_ukv, w_mla_branch, w_fox_branch, w_out, mlp_norm, w_up, w_down, final_norm, loss_target, m_attn_norm, m_w_in, m_fox_f_bias, m_q_norm, m_w_uq, m_kv_norm, m_w_ukv, m_w_mla_branch, m_w_fox_branch, m_w_out, m_mlp_norm, m_w_up, m_w_down, m_final_norm, v_attn_norm, v_w_in, v_fox_f_bias, v_q_norm, v_w_uq, v_kv_norm, v_w_ukv, v_w_mla_branch, v_w_fox_branch, v_w_out, v_mlp_norm, v_w_up, v_w_down, v_final_norm):
    _, S, D = x.shape
    H, HF = MLA_HEADS, FOX_HEADS
    QL, KVL = MLA_Q_LORA, MLA_KV_LORA
    assert H == HF and H <= 8
    xs = x[0]
    target = loss_target[0]
    C = D
    chip = 2 * lax.axis_index("x") + lax.axis_index("y")

    def flip(a):
        return jnp.transpose(a, (0, 2, 1))

    w_in, m_w_in, v_w_in = flip(w_in), flip(m_w_in), flip(v_w_in)
    weights = {"attn_norm": attn_norm, "w_in": w_in, "fox_f_bias": fox_f_bias, "q_norm": q_norm, "w_uq": w_uq,
               "kv_norm": kv_norm, "w_ukv": w_ukv, "w_mla_branch": w_mla_branch, "w_fox_branch": w_fox_branch,
               "w_out": w_out, "mlp_norm": mlp_norm, "w_up": w_up, "w_down": w_down, "final_norm": final_norm}
    moments = {"attn_norm": (m_attn_norm, v_attn_norm), "w_in": (m_w_in, v_w_in), "fox_f_bias": (m_fox_f_bias, v_fox_f_bias),
               "q_norm": (m_q_norm, v_q_norm), "w_uq": (m_w_uq, v_w_uq), "kv_norm": (m_kv_norm, v_kv_norm),
               "w_ukv": (m_w_ukv, v_w_ukv), "w_mla_branch": (m_w_mla_branch, v_w_mla_branch),
               "w_fox_branch": (m_w_fox_branch, v_w_fox_branch), "w_out": (m_w_out, v_w_out),
               "mlp_norm": (m_mlp_norm, v_mlp_norm), "w_up": (m_w_up, v_w_up), "w_down": (m_w_down, v_w_down),
               "final_norm": (m_final_norm, v_final_norm)}

    def group(names_axes):
        return [(nm, weights[nm].shape[1:], axis) for nm, axis in names_axes]

    pack_a = _Pack(group([("w_in", 0), ("w_uq", 1), ("w_ukv", 1)]), C)
    pack_b = _Pack(group([("w_down", 0), ("w_up", 1), ("w_out", 0), ("w_mla_branch", 1), ("w_fox_branch", 1)]), C)
    RA, RB = pack_a.R, pack_b.R
    wp_a = pack_a.pack({nm: weights[nm][0] for nm, _, _ in pack_a.group})
    wp_b = pack_b.pack({nm: weights[nm][0] for nm, _, _ in pack_b.group})
    gathered_a = _all_gather_halves(wp_a, "all_gather_a")
    ag_b = _xchg_start(wp_b, lax.empty((N_CHIPS, RB, C), BF16), True, gathered_a, "all_gather_start_b")
    xn = _norm_fwd(xs, attn_norm, "attn_norm_fwd", order=ag_b[3])
    full = pack_a.full(gathered_a)

    o_ckv = QL
    o_kr = o_ckv + KVL
    o_fq = o_kr + MLA_ROPE
    o_ff = o_fq + 3 * HF * FOX_HEAD_DIM
    o_g = o_ff + HF
    wi = full["w_in"]
    assert wi.shape[0] == o_g + 2 * D
    WS = QL + KVL + 2 * LANE
    NQKV = 3 * HF * FOX_HEAD_DIM

    def pad_rows(a, rows):
        return jnp.pad(a, ((0, rows - a.shape[0]), (0, 0)))

    w_small = jnp.concatenate([wi[:o_kr], pad_rows(wi[o_kr:o_fq], LANE), pad_rows(wi[o_ff:o_g], LANE)], axis=0)
    w_qkv = wi[o_fq:o_ff]
    w_g = wi[o_g:]
    w_pack = jnp.concatenate([w_small, w_qkv, w_g], axis=0)
    dqk = MLA_NOPE + MLA_ROPE
    w_uq_p = jnp.pad(full["w_uq"].reshape(QL, H, dqk), ((0, 0), (0, 0), (0, QPAD - dqk))).reshape(QL, H * QPAD)
    ukv = full["w_ukv"].reshape(KVL, H, MLA_NOPE + MLA_V)
    w_ukv_p = jnp.concatenate([ukv[:, :, :MLA_NOPE].reshape(KVL, H * MLA_NOPE),
                               ukv[:, :, MLA_NOPE:].reshape(KVL, H * MLA_V)], axis=1)

    (kc, ksa, ksb), (qc, qsa, qsb) = _rope_tables(S)
    bias_pad = _pad_cols(fox_f_bias, LANE)

    small = _matmul(xn, w_small, "nt", [F32], "proj_small")
    n_fq = HF * FOX_HEAD_DIM
    q_scale = jnp.concatenate([jnp.full((1, n_fq), LOG2E / math.sqrt(FOX_HEAD_DIM), F32),
                               jnp.ones((1, NQKV - n_fq), F32)], axis=1)
    qkv = _matmul(xn, w_qkv, "nt", [BF16], "proj_qkv", col_extras=(q_scale,), epilogue=lambda acc, cs: (acc * cs,))
    gpre = _matmul(xn, w_g, "nt", [F32], "proj_gates")
    cqn, ckvn, kr, cum = _prep_fwd(small, q_norm, kv_norm, bias_pad, kc, ksa, ksb, HF, "prep_fwd")
    c2_mla = LOG2E / math.sqrt(dqk)
    q_rot = _matmul(cqn, w_uq_p, "nn", [BF16], "mla_q_up", tn=QPAD, row_extras=(qc * c2_mla, qsa * c2_mla, qsb * c2_mla),
                    epilogue=lambda acc, c, sa, sb: (_rope(acc, c, sa, sb, 1),))
    kv2 = _matmul(ckvn, w_ukv_p, "nn", [BF16], "mla_kv_up")

    mla = _AttT(S, H, (q_rot, QPAD, 0, True), [(kv2, MLA_NOPE, 0, True), (kr, LANE, 0, False)],
                (kv2, MLA_V, H, True), 1.0 / math.sqrt(dqk), True)
    o_mla, lse_mla = _att_fwd_t(mla, "mla_att_fwd")

    cum_t = jnp.transpose(cum[:, :HF]) * LOG2E
    cum_rep = jnp.broadcast_to(cum_t[:, :, None], (HF, S, min(QSUB, _tile(S, ATT_T))))
    fox = _AttT(S, HF, (qkv, FOX_HEAD_DIM, 0, True), [(qkv, FOX_HEAD_DIM, HF, True)],
                (qkv, FOX_HEAD_DIM, 2 * HF, True), 1.0 / math.sqrt(FOX_HEAD_DIM), False, cum_rep)
    o_fox, ox_fox, lse_fox = _att_fwd_t(fox, "fox_att_fwd", exact=True)

    own_b, land_b = _xchg_wait(ag_b, True, lse_fox, "all_gather_wait_b")
    gathered_b = lax.dynamic_update_slice(land_b, own_b[None], (chip, 0, 0))
    full.update(pack_b.full(gathered_b, ("w_mla_branch", "w_fox_branch", "w_out")))
    w_mb, w_fb, w_o = (full[n] for n in ("w_mla_branch", "w_fox_branch", "w_out"))

    def b_of(nm, mode, tn, tk):
        (K, N), axis = next((shape, axis) for n, shape, axis in pack_b.group if n == nm)
        off = pack_b.offs[nm]
        shape = (N_CHIPS * K, N) if axis == 0 else (K, N_CHIPS * N)
        t_r, t_c = (tk, tn) if mode == "nn" else (tn, tk)
        t_r, t_c = _tile(shape[0], t_r), _tile(shape[1], t_c)
        if not (N == C and K % t_r == 0 and N % t_c == 0 and off % t_r == 0):
            return pack_b.full(gathered_b, (nm,))[nm], None
        base = off // t_r
        if axis == 0:
            per = K // t_r
            place = lambda rb, cb: (rb // per, base + rb % per, cb)
        else:
            per = N // t_c
            place = lambda rb, cb: (cb // per, base + rb, cb % per)
        return gathered_b, (shape, (lambda j, k: place(k, j)) if mode == "nn" else (lambda j, k: place(j, k)))

    y_mla = _matmul(o_mla, w_mb, "nn", [F32], "mla_branch")

    def gate_merge(acc, ga, gb, ya):
        return acc, _sigmoid(ga) * ya + _sigmoid(gb) * acc

    y_fox, merged = _matmul(o_fox, w_fb, "nn", [F32, BF16], "fox_branch_gates", extras=((gpre, 0), (gpre, 1), y_mla),
                            epilogue=gate_merge)
    h1 = _matmul(merged, w_o, "nn", [F32], "out_proj", extras=(xs,), epilogue=lambda acc, r: (acc + r,))
    hn = _norm_fwd(h1, mlp_norm, "mlp_norm_fwd")

    def relu2(acc):
        a = jnp.maximum(acc, 0.0)
        return a * a, a

    w_u, w_u_in = b_of("w_up", "nn", 512, 2048)
    u, a_pos = _matmul(hn, w_u, "nn", [BF16, BF16], "mlp_up", epilogue=relu2, b_in=w_u_in)
    w_d, w_d_in = b_of("w_down", "nn", 1024, 2048)
    h2 = _matmul(u, w_d, "nn", [F32], "mlp_down", tn=1024, extras=(h1,), epilogue=lambda acc, r: (acc + r,),
                 b_in=w_d_in)
    dh2, dh2_b, g_final, loss_part = _final(h2, final_norm.reshape(1, D), target, "final_norm_loss")

    gp_b = lax.empty((N_CHIPS, RB, C), BF16)
    by_glue = {}

    def grad_b(nm, a, b, name):
        nonlocal gp_b
        (K, N), axis = next((shape, axis) for n, shape, axis in pack_b.group if n == nm)
        off = pack_b.offs[nm]
        tm = min(1024, K) if axis == 0 else min(1024, a.shape[1])
        tn = min(1024, N) if axis == 1 else min(1024, b.shape[1])
        if not (N == C and tm % LANE == 0 and tn % LANE == 0 and K % tm == 0 and N % tn == 0 and off % tm == 0):
            by_glue[nm] = _mm_tn(a, b, name)
            return
        base = off // tm
        if axis == 0:
            per = K // tm
            place = lambda i, j: (i // per, base + i % per, j)
        else:
            per = N // tn
            place = lambda i, j: (j // per, base + i, j % per)
        gp_b = _mm_tn(a, b, name, tm=tm, tn=tn, into=(gp_b, place))

    w_d, w_d_in = b_of("w_down", "nt", 512, 2048)
    da = _matmul(dh2_b, w_d, "nt", [BF16], "mlp_down_dx", extras=(a_pos,),
                 epilogue=lambda acc, a: (acc * (2.0 * a.astype(F32)),), b_in=w_d_in)
    grad_b("w_down", u, dh2_b, "mlp_down_dw")
    w_u, w_u_in = b_of("w_up", "nt", 1024, 2048)
    dhn = _matmul(da, w_u, "nt", [F32], "mlp_up_dx", tn=1024, b_in=w_u_in)
    grad_b("w_up", hn, da, "mlp_up_dw")
    dh1, dh1_b, g_mlp_norm = _norm_bwd(h1, dhn, mlp_norm, dh2, "mlp_norm_bwd")

    def gate_bwd(acc, ga, gb, ya, yb):
        ga, gb = _sigmoid(ga), _sigmoid(gb)
        return acc * ga, acc * gb, acc * ya * (ga * (1.0 - ga)), acc * yb * (gb * (1.0 - gb))

    dy_mla, dy_fox, dg_mla, dg_fox = _matmul(dh1_b, w_o, "nt", [BF16] * 4, "out_proj_dx_gates",
                                             extras=((gpre, 0), (gpre, 1), y_mla, y_fox), epilogue=gate_bwd)
    grad_b("w_out", merged, dh1_b, "out_proj_dw")
    do_mla = _matmul(dy_mla, w_mb, "nt", [BF16], "mla_branch_dx")
    grad_b("w_mla_branch", o_mla, dy_mla, "mla_branch_dw")
    do_fox = _matmul(dy_fox, w_fb, "nt", [BF16], "fox_branch_dx")
    grad_b("w_fox_branch", o_fox, dy_fox, "fox_branch_dw")
    for nm, g in by_glue.items():
        gp_b = lax.dynamic_update_slice(gp_b, pack_b.slab_rows(nm, g), (0, pack_b.offs[nm], 0))
    if RB > pack_b.used:
        gp_b = lax.dynamic_update_slice(gp_b, jnp.zeros((N_CHIPS, RB - pack_b.used, C), BF16), (0, pack_b.used, 0))

    rs_b = _xchg_start(gp_b, lax.empty((3, RB, C), BF16), False, do_fox, "grad_scatter_start_b")

    delta_mla = _att_delta_t(do_mla, o_mla, H, "mla_att_delta", order=rs_b[3])
    dq_rot, dk_nope, dkr_heads, dv_mla = _att_bwd_t(mla, do_mla, lse_mla, delta_mla, BF16, [BF16, F32],
                                                    "mla_att_bwd", dq_rope=(qc, qsa, qsb))
    delta_fox = _att_delta_t(do_fox, ox_fox, HF, "fox_att_delta")
    dfq, dfk, dfv, dcum = _att_bwd_t(fox, do_fox, lse_fox, delta_fox, BF16, [BF16], "fox_att_bwd")

    gp_b_sent, recv_b = _xchg_wait(rs_b, False, dfq, "grad_scatter_wait_b")
    swap_b = _sib_start(_sum_slabs(gp_b_sent, recv_b, chip, "grad_sum_b"), "grad_swap_start_b")

    dcqn = _matmul(dq_rot, w_uq_p, "nt", [F32], "mla_q_up_dx", order=swap_b[4])
    g_w_uq_p = _mm_tn(cqn, dq_rot, "mla_q_up_dw")
    dkv2 = jnp.concatenate([dk_nope, dv_mla], axis=1)
    dckvn = _matmul(dkv2, w_ukv_p, "nt", [F32], "mla_kv_up_dx")
    g_w_ukv_p = _mm_tn(ckvn, dkv2, "mla_kv_up_dw")

    dcum_rows = jnp.pad(dcum[:, :, 0], ((0, 8 - HF), (0, 0)))
    dlogf_rows = _suffix_sum_rows(dcum_rows, "fox_forget_suffix_sum")
    dlogf = _pad_cols(jnp.transpose(dlogf_rows[:HF]), LANE)
    d_small, g_q_norm, g_kv_norm, g_bias = _prep_bwd(
        small, dcqn, dckvn, dkr_heads, dlogf, q_norm, kv_norm, bias_pad, kc, ksa, ksb, H, "prep_bwd")
    dproj = [d_small, dfq, dfk, dfv, dg_mla, dg_fox]
    g_w_pack = _matmul_parts(dproj, xn, "tn", BF16, "proj_dw")

    gs, gq, gg = g_w_pack[:WS], g_w_pack[WS:WS + NQKV], g_w_pack[WS + NQKV:]
    g_w_in = jnp.concatenate([gs[:o_kr], gs[o_kr:o_kr + MLA_ROPE], gq,
                              gs[o_kr + LANE:o_kr + LANE + HF], gg], axis=0)
    g_w_uq = g_w_uq_p.reshape(QL, H, QPAD)[:, :, :dqk].reshape(QL, H * dqk)
    g_w_ukv = jnp.concatenate([g_w_ukv_p[:, :H * MLA_NOPE].reshape(KVL, H, MLA_NOPE),
                               g_w_ukv_p[:, H * MLA_NOPE:].reshape(KVL, H, MLA_V)], axis=2).reshape(KVL, -1)

    gp_a = pack_a.slabs({"w_in": g_w_in, "w_uq": g_w_uq, "w_ukv": g_w_ukv})
    rs_a = _xchg_start(gp_a, lax.empty((3, RA, C), BF16), False, g_w_pack, "grad_scatter_start_a")
    dxn = _matmul_parts(dproj, w_pack, "nn", F32, "proj_dx", order=rs_a[3])
    grad_x, _, g_attn_norm = _norm_bwd(xs, dxn, attn_norm, dh1, "attn_norm_bwd")
    gp_a_sent, recv_a = _xchg_wait(rs_a, False, grad_x, "grad_scatter_wait_a")
    swap_a = _sib_start(_sum_slabs(gp_a_sent, recv_a, chip, "grad_sum_a"), "grad_swap_start_a")
    vec_w = max(D, LANE)
    vec_rows = [g_attn_norm, g_mlp_norm, g_final, g_q_norm, g_kv_norm, g_bias, loss_part]
    vec = jnp.concatenate([_pad_cols(v, vec_w) for v in vec_rows] + [jnp.zeros((1, vec_w), F32)], axis=0)
    vsum = _all_reduce_vec(vec, "all_reduce_vectors")
    part_b, sib_b = _sib_wait(swap_b, vsum, "grad_swap_wait_b")

    grads, deltas, new_m, new_v = {}, {}, {}, {}

    def update(pack, mine, theirs):
        for nm, shape, _ in pack.group:
            K, N = shape
            if N == pack.C and K % 8 == 0 and pack.offs[nm] % _tile(K, 256, 8) == 0:
                g, d, nm_, nv_ = _adamw(weights[nm], mine, theirs, moments[nm][0], moments[nm][1], "adamw_" + nm,
                                        g_row=pack.offs[nm])
            else:
                g, d, nm_, nv_ = _adamw(weights[nm], pack.part(mine, nm, shape), pack.part(theirs, nm, shape),
                                        moments[nm][0], moments[nm][1], "adamw_" + nm)
            grads[nm], deltas[nm], new_m[nm], new_v[nm] = g, d, nm_, nv_
        return g

    last_b = update(pack_b, part_b, sib_b)
    part_a, sib_a = _sib_wait(swap_a, last_b, "grad_swap_wait_a")
    update(pack_a, part_a, sib_a)

    vec_names = ["attn_norm", "mlp_norm", "final_norm", "q_norm", "kv_norm", "fox_f_bias"]

    def vec_pack(arrs):
        return jnp.concatenate([_pad_cols(a.reshape(1, -1), vec_w) for a in arrs]
                               + [jnp.zeros((2, vec_w), F32)], axis=0)[None]

    vg, vd, vm, vv = _adamw(vec_pack([weights[n] for n in vec_names]), vsum, jnp.zeros_like(vsum),
                            vec_pack([moments[n][0] for n in vec_names]), vec_pack([moments[n][1] for n in vec_names]),
                            "adamw_vectors")
    for r, nm in enumerate(vec_names):
        shp = weights[nm].shape
        n = weights[nm].size
        grads[nm] = vsum[r, :n].reshape(shp)
        deltas[nm], new_m[nm], new_v[nm] = (vd[0, r, :n].reshape(shp), vm[0, r, :n].reshape(shp),
                                            vv[0, r, :n].reshape(shp))
    loss = vsum[6, 0]

    for res in (grads, deltas, new_m, new_v):
        res["w_in"] = flip(res["w_in"])
    order = ["attn_norm", "w_in", "fox_f_bias", "q_norm", "w_uq", "kv_norm", "w_ukv", "w_mla_branch", "w_fox_branch",
             "w_out", "mlp_norm", "w_up", "w_down", "final_norm"]
    return (loss, grad_x[None], *[grads[n] for n in order], *[deltas[n] for n in order],
            *[new_m[n] for n in order], *[new_v[n] for n in order])
```

```python
import math

import jax
import jax.numpy as jnp
from jax import lax
from jax.experimental import pallas as pl
from jax.experimental.pallas import tpu as pltpu

CHUNK = 64
MLA_HEADS = 8
MLA_Q_LORA = 512
MLA_KV_LORA = 256
MLA_NOPE = 128
MLA_ROPE = 64
MLA_V = 128
ROPE_THETA = 10000.0
FOX_HEADS = 8
FOX_HEAD_DIM = 128
EPS = 1e-6

ADAM_LR = 0.001
ADAM_B1 = 0.9
ADAM_B2 = 0.999
ADAM_EPS = 1e-08
ADAM_WD = 0.01
ADAM_STEP = 10

LANE = 128
QPAD = 2 * LANE
N_CHIPS = 4
N_DEV = 8
VMEM_LIMIT = 48 * 1024 * 1024
ATT_T = 1024
QSUB = 256
ROW_T = 256
PACK_ROWS = 256
LOG2E = 1.4426950408889634

BF16 = jnp.bfloat16
F32 = jnp.float32
MESH = pl.DeviceIdType.MESH

_NT = (((1,), (1,)), ((), ()))
_TN = (((0,), (0,)), ((), ()))
_NN = (((1,), (0,)), ((), ()))


def _tile(dim, pref, align=LANE):
    if dim <= pref:
        return dim
    t = (pref // align) * align
    while t >= align:
        if dim % t == 0:
            return t
        t -= align
    return dim


def _params(sem=None):
    return pltpu.CompilerParams(dimension_semantics=sem, vmem_limit_bytes=VMEM_LIMIT)


_ANY_SPEC = pl.BlockSpec(memory_space=pl.ANY)


def _matmul(a, b, mode, out_dtypes, name, *, tm=1024, tn=512, tk=2048, extras=(), row_extras=(), col_extras=(),
            epilogue=None, order=None, into=None, b_in=None):
    b_shape = b.shape if b_in is None else b_in[0]
    if mode == "nn":
        (M, K), (K2, N) = a.shape, b_shape
    elif mode == "nt":
        (M, K), (N, K2) = a.shape, b_shape
    else:
        (K, M), (K2, N) = a.shape, b_shape
    assert K == K2, (name, a.shape, b_shape)
    tm, tn, tk = _tile(M, tm), _tile(N, tn), _tile(K, tk)
    nk = K // tk
    extras = [e if isinstance(e, tuple) else (e, 0) for e in extras]
    n_out = len(out_dtypes)
    n_ex = len(extras) + len(row_extras) + len(col_extras)
    n_ord = 0 if order is None else 1
    assert all(r.shape == (M, tn) for r in row_extras), name
    dims = {"nn": _NN, "nt": _NT, "tn": _TN}[mode]

    def body(*refs):
        a_ref, b_ref = refs[0], refs[1]
        ex_refs = refs[2:2 + n_ex]
        o_refs = refs[2 + n_ex + n_ord:2 + n_ex + n_ord + n_out]
        acc_ref = refs[2 + n_ex + n_ord + n_out]
        k = pl.program_id(2)
        part = lax.dot_general(a_ref[...], b_ref[...], dims, preferred_element_type=F32)

        @pl.when(k == 0)
        def _():
            acc_ref[...] = part

        @pl.when(k > 0)
        def _():
            acc_ref[...] += part

        @pl.when(k == nk - 1)
        def _():
            acc = acc_ref[...]
            if epilogue is None:
                outs = (acc,)
            else:
                outs = epilogue(acc, *[r[...] for r in ex_refs])
            for o_ref, o in zip(o_refs, outs):
                o_ref[...] = o.astype(o_ref.dtype)

    if mode == "nn":
        a_spec = pl.BlockSpec((tm, tk), lambda i, j, k: (i, k))
        b_spec = pl.BlockSpec((tk, tn), lambda i, j, k: (k, j))
    elif mode == "nt":
        a_spec = pl.BlockSpec((tm, tk), lambda i, j, k: (i, k))
        b_spec = pl.BlockSpec((tn, tk), lambda i, j, k: (j, k))
    else:
        a_spec = pl.BlockSpec((tk, tm), lambda i, j, k: (k, i))
        b_spec = pl.BlockSpec((tk, tn), lambda i, j, k: (k, j))
    if b_in is not None:
        b_block = (None, tn, tk) if mode == "nt" else (None, tk, tn)
        b_spec = pl.BlockSpec(b_block, lambda i, j, k: b_in[1](j, k))
    mn_spec = pl.BlockSpec((tm, tn), lambda i, j, k: (i, j))
    row_spec = pl.BlockSpec((tm, tn), lambda i, j, k: (i, 0))
    col_spec = pl.BlockSpec((1, tn), lambda i, j, k: (0, j))
    out_specs = [mn_spec] * n_out
    out_shape = [jax.ShapeDtypeStruct((M, N), dt) for dt in out_dtypes]
    aliases = {}
    if into is not None:
        buf, place = into
        assert n_out == 1 and n_ord == 1 and order is buf, name
        out_specs = [pl.BlockSpec((None, tm, tn), lambda i, j, k: place(i, j))]
        out_shape = [jax.ShapeDtypeStruct(buf.shape, buf.dtype)]
        aliases = {2 + n_ex: 0}
    outs = pl.pallas_call(
        body,
        name=name,
        grid=(M // tm, N // tn, nk),
        in_specs=([a_spec, b_spec]
                  + [pl.BlockSpec((tm, tn), lambda i, j, k, g=g: (i, j + g * (N // tn))) for _, g in extras]
                  + [row_spec] * len(row_extras) + [col_spec] * len(col_extras) + [_ANY_SPEC] * n_ord),
        out_specs=out_specs,
        out_shape=out_shape,
        scratch_shapes=[pltpu.VMEM((tm, tn), F32)],
        input_output_aliases=aliases,
        compiler_params=_params(("parallel", "parallel", "arbitrary")),
    )(a, b, *[e for e, _ in extras], *row_extras, *col_extras, *([] if order is None else [order]))
    return outs[0] if n_out == 1 else outs


def _matmul_parts(parts, b, mode, out_dtype, name, *, tm=1024, tn=1024, tk=1024, order=None):
    assert mode in ("nn", "tn")
    if mode == "nn":
        M, (K, N) = parts[0].shape[0], b.shape
        widths = [p.shape[1] for p in parts]
    else:
        K, N = b.shape
        widths = [p.shape[1] for p in parts]
        M = sum(widths)
    common = math.gcd(*widths)
    tm, tn, tk = _tile(M if mode == "nn" else common, tm), _tile(N, tn), _tile(common if mode == "nn" else K, tk)
    t_part = tk if mode == "nn" else tm
    assert sum(widths) == (K if mode == "nn" else M), name
    if any(w % t_part for w in widths):
        parts, widths = [jnp.concatenate(parts, axis=1)], [sum(widths)]
    lo =[sum(widths[:p]) // t_part for p in range(len(parts))]
    cnt = [w // t_part for w in widths]
    nk = K // tk
    n_parts = len(parts)
    n_ord = 0 if order is None else 1
    dims = _NN if mode == "nn" else _TN

    def body(*refs):
        a_refs = refs[0:n_parts]
        b_ref = refs[n_parts]
        o_ref, acc_ref = refs[n_parts + 1 + n_ord], refs[n_parts + 2 + n_ord]
        i, k = pl.program_id(0), pl.program_id(2)
        sel = k if mode == "nn" else i
        for p in range(n_parts):
            @pl.when((sel >= lo[p]) & (sel < lo[p] + cnt[p]))
            def _(p=p):
                part = lax.dot_general(a_refs[p][...], b_ref[...], dims, preferred_element_type=F32)

                @pl.when(k == 0)
                def _():
                    acc_ref[...] = part

                @pl.when(k > 0)
                def _():
                    acc_ref[...] += part

        @pl.when(k == nk - 1)
        def _():
            o_ref[...] = acc_ref[...].astype(o_ref.dtype)

    def a_spec(p):
        if mode == "nn":
            return pl.BlockSpec((tm, tk), lambda i, j, k: (i, jnp.clip(k - lo[p], 0, cnt[p] - 1)))
        return pl.BlockSpec((tk, tm), lambda i, j, k: (
            jnp.where((i >= lo[p]) & (i < lo[p] + cnt[p]), k, 0), jnp.clip(i - lo[p], 0, cnt[p] - 1)))

    return pl.pallas_call(
        body, name=name, grid=(M // tm, N // tn, nk),
        in_specs=[a_spec(p) for p in range(n_parts)] + [pl.BlockSpec((tk, tn), lambda i, j, k: (k, j))]
        + [_ANY_SPEC] * n_ord,
        out_specs=pl.BlockSpec((tm, tn), lambda i, j, k: (i, j)),
        out_shape=jax.ShapeDtypeStruct((M, N), out_dtype),
        scratch_shapes=[pltpu.VMEM((tm, tn), F32)],
        compiler_params=_params(("parallel", "parallel", "arbitrary")),
    )(*parts, b, *([] if order is None else [order]))


def _mm_tn(a, b, name, tm=1024, tn=1024, into=None):
    return _matmul(a, b, "tn", [F32], name, tm=tm, tn=tn, tk=2048, into=into,
                   order=None if into is None else into[0])


def _row_spec(ts, width, col=0):
    return pl.BlockSpec((ts, width), lambda i: (i, col))


def _full_spec(shape):
    return pl.BlockSpec(shape, lambda i: tuple(0 for _ in shape))


def _rms(x):
    return lax.rsqrt(jnp.mean(x * x, axis=-1, keepdims=True) + EPS)


def _rms_bwd(x, dy, g):
    r = _rms(x)
    xh = x * r
    gy = dy * g
    dx = r * (gy - xh * jnp.mean(xh * gy, axis=-1, keepdims=True))
    return dx, dy * xh


def _norm_fwd(x, g, name, order=None):
    S, D = x.shape
    ts = _tile(S, ROW_T, 8)

    def body(x_ref, g_ref, *rest):
        o_ref = rest[-1]
        xv = x_ref[...]
        o_ref[...] = ((xv * _rms(xv)) * g_ref[...]).astype(BF16)

    extra = [] if order is None else [order]
    return pl.pallas_call(
        body, name=name, grid=(S // ts,),
        in_specs=[_row_spec(ts, D), _full_spec((1, D))] + [_ANY_SPEC] * len(extra),
        out_specs=_row_spec(ts, D),
        out_shape=jax.ShapeDtypeStruct((S, D), BF16),
        compiler_params=_params(("parallel",)),
    )(x, g, *extra)


def _norm_bwd(x, dy, g, dres, name):
    S, D = x.shape
    ts = _tile(S, ROW_T, 8)

    def body(x_ref, dy_ref, g_ref, dres_ref, dx_ref, dxb_ref, dg_ref):
        dx, dg_rows = _rms_bwd(x_ref[...], dy_ref[...], g_ref[...])
        dx = dres_ref[...] + dx
        dx_ref[...] = dx
        dxb_ref[...] = dx.astype(BF16)

        @pl.when(pl.program_id(0) == 0)
        def _():
            dg_ref[...] = jnp.zeros_like(dg_ref)

        dg_ref[...] += jnp.sum(dg_rows, axis=0, keepdims=True)

    return pl.pallas_call(
        body, name=name, grid=(S // ts,),
        in_specs=[_row_spec(ts, D), _row_spec(ts, D), _full_spec((1, D)), _row_spec(ts, D)],
        out_specs=[_row_spec(ts, D), _row_spec(ts, D), _full_spec((1, D))],
        out_shape=[jax.ShapeDtypeStruct((S, D), F32), jax.ShapeDtypeStruct((S, D), BF16),
                   jax.ShapeDtypeStruct((1, D), F32)],
        compiler_params=_params(("arbitrary",)),
    )(x, dy, g, dres)


def _rope(x, c, sa, sb, sign):
    w = x.shape[-1]
    half = MLA_ROPE // 2
    fwd = pltpu.roll(x, w - half, 1)
    back = pltpu.roll(x, half, 1)
    if sign < 0:
        return x * c - fwd * sa - back * sb
    return x * c + fwd * sa + back * sb


def _split3(x):
    hi = x.astype(BF16)
    r1 = x - hi.astype(F32)
    mid = r1.astype(BF16)
    lo = (r1 - mid.astype(F32)).astype(BF16)
    return hi, mid, lo


def _prep_fwd(small, q_norm, kv_norm, bias_pad, kc, ksa, ksb, n_heads, name):
    S, W = small.shape
    QL, KVL = q_norm.shape[1], kv_norm.shape[1]
    assert W == QL + KVL + 2 * LANE
    ts = _tile(S, ROW_T, 8)
    tri = (lax.broadcasted_iota(jnp.int32, (ts, ts), 0) >= lax.broadcasted_iota(jnp.int32, (ts, ts), 1)).astype(BF16)

    def body(s_ref, qn_ref, kvn_ref, b_ref, kc_ref, ksa_ref, ksb_ref, tri_ref,
             cqn_ref, ckvn_ref, kr_ref, cum_ref, carry_ref):
        cq = s_ref[:, 0:QL]
        cqn_ref[...] = ((cq * _rms(cq)) * qn_ref[...]).astype(BF16)
        ckv = s_ref[:, QL:QL + KVL]
        ckvn_ref[...] = ((ckv * _rms(ckv)) * kvn_ref[...]).astype(BF16)
        kr = s_ref[:, QL + KVL:QL + KVL + LANE]
        kr_ref[...] = _rope(kr, kc_ref[...], ksa_ref[...], ksb_ref[...], 1).astype(BF16)
        z = s_ref[:, QL + KVL + LANE:W] + b_ref[...]
        logf = jnp.minimum(z, 0.0) - jnp.log1p(jnp.exp(-jnp.abs(z)))
        lane = lax.broadcasted_iota(jnp.int32, logf.shape, 1)
        logf = jnp.where(lane < n_heads, logf, 0.0)

        @pl.when(pl.program_id(0) == 0)
        def _():
            carry_ref[...] = jnp.zeros_like(carry_ref)

        t = tri_ref[...]
        cum = carry_ref[...]
        for part in _split3(logf):
            cum = cum + jnp.dot(t, part, preferred_element_type=F32)
        cum_ref[...] = cum
        carry_ref[...] = cum[ts - 1:ts, :]

    return pl.pallas_call(
        body, name=name, grid=(S // ts,),
        in_specs=[_row_spec(ts, W), _full_spec((1, QL)), _full_spec((1, KVL)), _full_spec((1, LANE)),
                  _row_spec(ts, LANE), _row_spec(ts, LANE), _row_spec(ts, LANE), _full_spec((ts, ts))],
        out_specs=[_row_spec(ts, QL), _row_spec(ts, KVL), _row_spec(ts, LANE), _row_spec(ts, LANE)],
        out_shape=[jax.ShapeDtypeStruct((S, QL), BF16), jax.ShapeDtypeStruct((S, KVL), BF16),
                   jax.ShapeDtypeStruct((S, LANE), BF16), jax.ShapeDtypeStruct((S, LANE), F32)],
        scratch_shapes=[pltpu.VMEM((1, LANE), F32)],
        compiler_params=_params(("arbitrary",)),
    )(small, q_norm, kv_norm, bias_pad, kc, ksa, ksb, tri)


def _prep_bwd(small, dcqn, dckvn, dkr_heads, dlogf, q_norm, kv_norm, bias_pad, kc, ksa, ksb, n_heads, name):
    S, W = small.shape
    QL, KVL = q_norm.shape[1], kv_norm.shape[1]
    ts = _tile(S, ROW_T, 8)

    def body(s_ref, dcq_ref, dckv_ref, dkr_ref, dlf_ref, qn_ref, kvn_ref, b_ref, kc_ref, ksa_ref, ksb_ref,
             ds_ref, gq_ref, gkv_ref, gb_ref):
        dcq, gq_rows = _rms_bwd(s_ref[:, 0:QL], dcq_ref[...], qn_ref[...])
        ds_ref[:, 0:QL] = dcq.astype(BF16)
        dckv, gkv_rows = _rms_bwd(s_ref[:, QL:QL + KVL], dckv_ref[...], kvn_ref[...])
        ds_ref[:, QL:QL + KVL] = dckv.astype(BF16)
        dkr = dkr_ref[:, 0:LANE]
        for h in range(1, n_heads):
            dkr = dkr + dkr_ref[:, h * LANE:(h + 1) * LANE]
        ds_ref[:, QL + KVL:QL + KVL + LANE] = _rope(dkr, kc_ref[...], ksa_ref[...], ksb_ref[...], -1).astype(BF16)
        z = s_ref[:, QL + KVL + LANE:W] + b_ref[...]
        dff = dlf_ref[...] * (1.0 / (1.0 + jnp.exp(z)))
        ds_ref[:, QL + KVL + LANE:W] = dff.astype(BF16)

        @pl.when(pl.program_id(0) == 0)
        def _():
            gq_ref[...] = jnp.zeros_like(gq_ref)
            gkv_ref[...] = jnp.zeros_like(gkv_ref)
            gb_ref[...] = jnp.zeros_like(gb_ref)

        gq_ref[...] += jnp.sum(gq_rows, axis=0, keepdims=True)
        gkv_ref[...] += jnp.sum(gkv_rows, axis=0, keepdims=True)
        gb_ref[...] += jnp.sum(dff, axis=0, keepdims=True)

    return pl.pallas_call(
        body, name=name, grid=(S // ts,),
        in_specs=[_row_spec(ts, W), _row_spec(ts, QL), _row_spec(ts, KVL), _row_spec(ts, n_heads * LANE),
                  _row_spec(ts, LANE), _full_spec((1, QL)), _full_spec((1, KVL)), _full_spec((1, LANE)),
                  _row_spec(ts, LANE), _row_spec(ts, LANE), _row_spec(ts, LANE)],
        out_specs=[_row_spec(ts, W), _full_spec((1, QL)), _full_spec((1, KVL)), _full_spec((1, LANE))],
        out_shape=[jax.ShapeDtypeStruct((S, W), BF16), jax.ShapeDtypeStruct((1, QL), F32),
                   jax.ShapeDtypeStruct((1, KVL), F32), jax.ShapeDtypeStruct((1, LANE), F32)],
        compiler_params=_params(("arbitrary",)),
    )(small, dcqn, dckvn, dkr_heads, dlogf, q_norm, kv_norm, bias_pad, kc, ksa, ksb)


def _sigmoid(z):
    return 1.0 / (1.0 + jnp.exp(-z))


def _final(h, g, target, name):
    S, D = h.shape
    ts = _tile(S, ROW_T, 8)

    def body(h_ref, g_ref, t_ref, dh_ref, dhb_ref, dg_ref, loss_ref):
        hv = h_ref[...]
        gv = g_ref[...]
        err = (hv * _rms(hv)) * gv - t_ref[...]
        dh, dg_rows = _rms_bwd(hv, err / D, gv)
        dh_ref[...] = dh
        dhb_ref[...] = dh.astype(BF16)

        @pl.when(pl.program_id(0) == 0)
        def _():
            dg_ref[...] = jnp.zeros_like(dg_ref)
            loss_ref[...] = jnp.zeros_like(loss_ref)

        dg_ref[...] += jnp.sum(dg_rows, axis=0, keepdims=True)
        row_loss = jnp.mean(err * err, axis=-1, keepdims=True)
        loss_ref[...] += 0.5 * jnp.sum(row_loss, axis=0, keepdims=True)

    return pl.pallas_call(
        body, name=name, grid=(S // ts,),
        in_specs=[_row_spec(ts, D), _full_spec((1, D)), _row_spec(ts, D)],
        out_specs=[_row_spec(ts, D), _row_spec(ts, D), _full_spec((1, D)), _full_spec((1, LANE))],
        out_shape=[jax.ShapeDtypeStruct((S, D), F32), jax.ShapeDtypeStruct((S, D), BF16),
                   jax.ShapeDtypeStruct((1, D), F32), jax.ShapeDtypeStruct((1, LANE), F32)],
        compiler_params=_params(("arbitrary",)),
    )(h, g, target)


def _suffix_sum_rows(x, name):
    R, S = x.shape
    tb = _tile(S, 512)
    nb = S // tb
    tri = (lax.broadcasted_iota(jnp.int32, (tb, tb), 0) >= lax.broadcasted_iota(jnp.int32, (tb, tb), 1)).astype(BF16)

    def body(x_ref, tri_ref, o_ref, carry_ref):
        @pl.when(pl.program_id(0) == 0)
        def _():
            carry_ref[...] = jnp.zeros_like(carry_ref)

        xv = x_ref[...]
        t = tri_ref[...]
        acc = jnp.broadcast_to(carry_ref[:, 0:1], xv.shape)
        for part in _split3(xv):
            acc = acc + jnp.dot(part, t, preferred_element_type=F32)
        o_ref[...] = acc
        carry_ref[...] = jnp.broadcast_to(acc[:, 0:1], carry_ref.shape)

    rev = pl.BlockSpec((R, tb), lambda i: (0, nb - 1 - i))
    return pl.pallas_call(
        body, name=name, grid=(nb,),
        in_specs=[rev, _full_spec((tb, tb))], out_specs=rev,
        out_shape=jax.ShapeDtypeStruct((R, S), F32),
        scratch_shapes=[pltpu.VMEM((R, LANE), F32)],
        compiler_params=_params(("arbitrary",)),
    )(x, tri)


def _pairs(nb, by_key):
    if by_key:
        pr = [(i, j) for j in range(nb) for i in range(j, nb)]
    else:
        pr = [(i, j) for i in range(nb) for j in range(i + 1)]
    return (jnp.asarray([p[0] for p in pr], jnp.int32), jnp.asarray([p[1] for p in pr], jnp.int32), len(pr))


class _AttT:
    def __init__(self, S, n_heads, q, ks, v, scale, chunk_causal, cum_rep=None):
        self.S, self.H, self.q, self.ks, self.v = S, n_heads, q, ks, v
        self.scale, self.chunk_causal, self.cum_rep = scale, chunk_causal, cum_rep
        self.T = _tile(S, ATT_T)
        self.qs = min(QSUB, self.T)
        self.nb = S // self.T
        self.dq, self.dv = q[1], v[1]
        self.has_bias = cum_rep is not None

    def q_spec(self, op):
        _, w, off, per_head = op
        return pl.BlockSpec((self.T, w), lambda h, p, it, jt: (it[p], off + (h if per_head else 0)))

    def k_spec(self, op):
        _, w, off, per_head = op
        return pl.BlockSpec((self.T, w), lambda h, p, it, jt: (jt[p], off + (h if per_head else 0)))

    def row_q(self):
        return pl.BlockSpec((None, 1, self.T), lambda h, p, it, jt: (h, 0, it[p]))

    def cum_k(self):
        return pl.BlockSpec((None, self.T, self.qs), lambda h, p, it, jt: (h, jt[p], 0))

    def sub_blocks(self, masked):
        return [(q0, min(self.T, q0 + self.qs) if masked else self.T) for q0 in range(0, self.T, self.qs)]

    def scores(self, k, q_sub, cum, q0, masked):
        s = lax.dot_general(k, q_sub, _NT, preferred_element_type=F32)
        if self.has_bias:
            s = s - cum
        mask = None
        if masked:
            r = lax.broadcasted_iota(jnp.int32, s.shape, 0)
            c = lax.broadcasted_iota(jnp.int32, s.shape, 1) + q0
            mask = (r // CHUNK <= c // CHUNK) if self.chunk_causal else (r <= c)
        return s, mask


def _join(k_refs):
    return k_refs[0][...] if len(k_refs) == 1 else jnp.concatenate([r[...] for r in k_refs], axis=-1)


def _att_fwd_t(att, name, exact=False):
    S, H, T, qs = att.S, att.H, att.T, att.qs
    it, jt, npairs = _pairs(att.nb, by_key=False)
    nk = len(att.ks)

    def body(it_ref, jt_ref, *refs):
        q_ref = refs[0]
        k_refs = refs[1:1 + nk]
        v_ref = refs[1 + nk]
        n = 2 + nk
        cum_ref = None
        if att.has_bias:
            cum_ref = refs[n]
            n += 1
        o_ref = refs[n]
        n += 1
        ox_ref = None
        if exact:
            ox_ref = refs[n]
            n += 1
        lse_ref, m_ref, l_ref, acc_ref = refs[n:n + 4]
        lo_ref = refs[n + 4] if exact else None
        p = pl.program_id(1)
        i, j = it_ref[p], jt_ref[p]

        @pl.when(j == 0)
        def _():
            m_ref[...] = jnp.full_like(m_ref, -jnp.inf)
            l_ref[...] = jnp.zeros_like(l_ref)
            acc_ref[...] = jnp.zeros_like(acc_ref)
            if exact:
                lo_ref[...] = jnp.zeros_like(lo_ref)

        def step(masked):
            k = _join(k_refs)
            v = v_ref[...]
            subs = att.sub_blocks(masked)

            def logits(idx):
                q0, nkeys = subs[idx]
                cum = cum_ref[0:nkeys, :] if att.has_bias else None
                return att.scores(k[0:nkeys], q_ref[q0:q0 + qs, :], cum, q0, masked)

            ahead = logits(0)
            for idx, (q0, nkeys) in enumerate(subs):
                qsl = slice(q0, q0 + qs)
                s, mask = ahead
                if idx + 1 < len(subs):
                    ahead = logits(idx + 1)
                if masked:
                    s = jnp.where(mask, s, -jnp.inf)
                m_prev = m_ref[:, qsl]
                m_new = jnp.maximum(m_prev, jnp.max(s, axis=0, keepdims=True))
                alpha = jnp.exp2(m_prev - m_new)
                pr = jnp.exp2(s - m_new)
                l_ref[:, qsl] = alpha * l_ref[:, qsl] + jnp.sum(pr, axis=0, keepdims=True)
                p_hi = pr.astype(BF16)
                acc_ref[:, qsl] = alpha * acc_ref[:, qsl] + lax.dot_general(
                    v[0:nkeys], p_hi, _TN, preferred_element_type=F32)
                if exact:
                    p_lo = (pr - p_hi.astype(F32)).astype(BF16)
                    lo_ref[:, qsl] = alpha * lo_ref[:, qsl] + lax.dot_general(
                        v[0:nkeys], p_lo, _TN, preferred_element_type=F32)
                m_ref[:, qsl] = m_new

        @pl.when(j < i)
        def _():
            step(False)

        @pl.when(j == i)
        def _():
            step(True)
            l = l_ref[...]
            inv = 1.0 / l
            o_ref[...] = jnp.transpose(acc_ref[...] * inv).astype(o_ref.dtype)
            if exact:
                ox_ref[...] = jnp.transpose((acc_ref[...] + lo_ref[...]) * inv)
            lse_ref[...] = m_ref[...] + jnp.log2(l)

    in_specs = [att.q_spec(att.q)] + [att.k_spec(k) for k in att.ks] + [att.k_spec(att.v)]
    args = [att.q[0]] + [k[0] for k in att.ks] + [att.v[0]]
    if att.has_bias:
        in_specs.append(att.cum_k())
        args.append(att.cum_rep)
    o_spec = pl.BlockSpec((T, att.dv), lambda h, p, it, jt: (it[p], h))
    out_specs = [o_spec]
    out_shape = [jax.ShapeDtypeStruct((S, H * att.dv), BF16)]
    scratch = [pltpu.VMEM((1, T), F32), pltpu.VMEM((1, T), F32), pltpu.VMEM((att.dv, T), F32)]
    if exact:
        out_specs.append(o_spec)
        out_shape.append(jax.ShapeDtypeStruct((S, H * att.dv), F32))
        scratch.append(pltpu.VMEM((att.dv, T), F32))
    out_specs.append(att.row_q())
    out_shape.append(jax.ShapeDtypeStruct((H, 1, S), F32))
    return pl.pallas_call(
        body, name=name,
        grid_spec=pltpu.PrefetchScalarGridSpec(
            num_scalar_prefetch=2, grid=(H, npairs), in_specs=in_specs, out_specs=out_specs,
            scratch_shapes=scratch),
        out_shape=out_shape,
        compiler_params=_params(("parallel", "arbitrary")),
    )(it, jt, *args)


def _att_delta_t(do, o, n_heads, name, order=None):
    S = do.shape[0]
    w = do.shape[1] // n_heads
    ts = _tile(S, ATT_T)
    ones = jnp.ones((8, w), BF16)
    extra = [] if order is None else [order]

    def body(do_ref, o_ref, ones_ref, *rest):
        d_ref = rest[-1]
        prod = do_ref[...].astype(F32) * o_ref[...].astype(F32)
        acc = jnp.zeros((8, ts), F32)
        for part in _split3(prod):
            acc = acc + lax.dot_general(ones_ref[...], part, _NT, preferred_element_type=F32)
        d_ref[...] = acc[0:1, :]

    blk = pl.BlockSpec((ts, w), lambda i, h: (i, h))
    return pl.pallas_call(
        body, name=name, grid=(S // ts, n_heads),
        in_specs=[blk, blk, pl.BlockSpec((8, w), lambda i, h: (0, 0))] + [_ANY_SPEC] * len(extra),
        out_specs=pl.BlockSpec((None, 1, ts), lambda i, h: (h, 0, i)),
        out_shape=jax.ShapeDtypeStruct((n_heads, 1, S), F32),
        compiler_params=_params(("parallel", "parallel")),
    )(do, o, ones, *extra)


def _att_bwd_t(att, do, lse, delta, dq_dtype, dk_dtypes, name, dq_rope=None):
    S, H, T, qs = att.S, att.H, att.T, att.qs
    it, jt, npairs = _pairs(att.nb, by_key=True)
    nk = len(att.ks)
    last = att.nb - 1
    widths = [k[1] for k in att.ks]

    def body(it_ref, jt_ref, *refs):
        q_ref = refs[0]
        k_refs = refs[1:1 + nk]
        v_ref, do_ref, lse_ref, dl_ref = refs[1 + nk:5 + nk]
        n = 5 + nk
        cum_ref = None
        if att.has_bias:
            cum_ref = refs[n]
            n += 1
        rope_refs = None
        if dq_rope is not None:
            rope_refs = refs[n:n + 3]
            n += 3
        dq_ref = refs[n]
        dk_refs = refs[n + 1:n + 1 + nk]
        dv_ref = refs[n + 1 + nk]
        n += nk + 2
        dc_ref = None
        if att.has_bias:
            dc_ref = refs[n]
            n += 1
        dq_acc, dk_acc, dv_acc = refs[n:n + 3]
        dc_acc = refs[n + 3] if att.has_bias else None
        p = pl.program_id(1)
        i, j = it_ref[p], jt_ref[p]

        @pl.when(p == 0)
        def _():
            dq_acc[...] = jnp.zeros_like(dq_acc)

        @pl.when(i == j)
        def _():
            dk_acc[...] = jnp.zeros_like(dk_acc)
            dv_acc[...] = jnp.zeros_like(dv_acc)
            if att.has_bias:
                dc_acc[...] = jnp.zeros_like(dc_acc)

        def step(masked):
            k = _join(k_refs)
            v = v_ref[...]
            subs = att.sub_blocks(masked)

            def logits(idx):
                q0, nkeys = subs[idx]
                cum = cum_ref[0:nkeys, :] if att.has_bias else None
                return att.scores(k[0:nkeys], q_ref[q0:q0 + qs, :], cum, q0, masked)

            ahead = logits(0)
            for idx, (q0, nkeys) in enumerate(subs):
                qsl = slice(q0, q0 + qs)
                ksl = slice(0, nkeys)
                q_sub = q_ref[qsl, :]
                do_sub = do_ref[qsl, :]
                s, mask = ahead
                if idx + 1 < len(subs):
                    ahead = logits(idx + 1)
                pr = jnp.exp2(s - lse_ref[:, qsl])
                if masked:
                    pr = jnp.where(mask, pr, 0.0)
                dp = lax.dot_general(v[ksl], do_sub, _NT, preferred_element_type=F32)
                ds = pr * (dp - dl_ref[:, qsl])
                ds_b = ds.astype(BF16)
                dv_acc[ksl, :] += jnp.dot(pr.astype(BF16), do_sub, preferred_element_type=F32)
                dk_acc[ksl, :] += jnp.dot(ds_b, q_sub, preferred_element_type=F32)
                dq_acc[i, :, qsl] += lax.dot_general(k[ksl], ds_b, _TN, preferred_element_type=F32)
                if att.has_bias:
                    part = ds[:, 0:LANE] if qs >= LANE else ds
                    for c0 in range(LANE, qs, LANE):
                        part = part + ds[:, c0:c0 + LANE]
                    dc_acc[ksl, :] += part

        @pl.when(i > j)
        def _():
            step(False)

        @pl.when(i == j)
        def _():
            step(True)
            dq = jnp.transpose(dq_acc[i] * att.scale)
            if dq_rope is not None:
                dq = _rope(dq, rope_refs[0][...], rope_refs[1][...], rope_refs[2][...], -1)
            dq_ref[...] = dq.astype(dq_ref.dtype)

        @pl.when(i == last)
        def _():
            dk = dk_acc[...] * (1.0 / LOG2E)
            off = 0
            for r, w in zip(dk_refs, widths):
                r[...] = dk[:, off:off + w].astype(r.dtype)
                off += w
            dv_ref[...] = dv_acc[...].astype(dv_ref.dtype)
            if att.has_bias:
                dc_ref[...] = -jnp.sum(dc_acc[...], axis=-1, keepdims=True)

    do_op = (do, att.dv, 0, True)
    in_specs = ([att.q_spec(att.q)] + [att.k_spec(k) for k in att.ks]
                + [att.k_spec(att.v), att.q_spec(do_op), att.row_q(), att.row_q()])
    args = [att.q[0]] + [k[0] for k in att.ks] + [att.v[0], do, lse, delta]
    if att.has_bias:
        in_specs.append(att.cum_k())
        args.append(att.cum_rep)
    if dq_rope is not None:
        in_specs += [pl.BlockSpec((T, att.dq), lambda h, p, it, jt: (jt[p], 0))] * 3
        args += list(dq_rope)
    out_specs = [pl.BlockSpec((T, att.dq), lambda h, p, it, jt: (jt[p], h))]
    out_shape = [jax.ShapeDtypeStruct((S, H * att.dq), dq_dtype)]
    out_specs += [pl.BlockSpec((T, w), lambda h, p, it, jt: (jt[p], h)) for w in widths]
    out_shape += [jax.ShapeDtypeStruct((S, H * w), dt) for w, dt in zip(widths, dk_dtypes)]
    out_specs.append(pl.BlockSpec((T, att.dv), lambda h, p, it, jt: (jt[p], h)))
    out_shape.append(jax.ShapeDtypeStruct((S, H * att.dv), BF16))
    scratch = [pltpu.VMEM((att.nb, att.dq, T), F32), pltpu.VMEM((T, att.dq), F32), pltpu.VMEM((T, att.dv), F32)]
    if att.has_bias:
        out_specs.append(pl.BlockSpec((None, T, 1), lambda h, p, it, jt: (h, jt[p], 0)))
        out_shape.append(jax.ShapeDtypeStruct((H, S, 1), F32))
        scratch.append(pltpu.VMEM((T, min(qs, LANE)), F32))
    return pl.pallas_call(
        body, name=name,
        grid_spec=pltpu.PrefetchScalarGridSpec(
            num_scalar_prefetch=2, grid=(H, npairs), in_specs=in_specs, out_specs=out_specs,
            scratch_shapes=scratch),
        out_shape=out_shape,
        compiler_params=_params(("parallel", "arbitrary")),
    )(it, jt, *args)


def _adamw(w, g1, g2, m, v, name, g_row=None):
    _, K, N = w.shape
    by_rows = K % 8 == 0
    tr = _tile(K, 256, 8) if by_rows else K
    if g_row is None:
        assert g1.shape == (K, N) and g2.shape == (K, N), name
        g_row = 0
    assert by_rows and g_row % tr == 0 or g_row == 0, name
    g_blk = g_row // tr
    tc = N if by_rows else _tile(N, LANE)
    c1 = 1.0 - ADAM_B1 ** ADAM_STEP
    c2 = 1.0 - ADAM_B2 ** ADAM_STEP

    def body(w_ref, g1_ref, g2_ref, m_ref, v_ref, g_ref, d_ref, nm_ref, nv_ref):
        gv = g1_ref[...] + g2_ref[...]
        nm = ADAM_B1 * m_ref[...] + (1.0 - ADAM_B1) * gv
        nv = ADAM_B2 * v_ref[...] + (1.0 - ADAM_B2) * (gv * gv)
        g_ref[...] = gv
        d_ref[...] = -ADAM_LR * ((nm / c1) / (jnp.sqrt(nv / c2) + ADAM_EPS) + ADAM_WD * w_ref[...])
        nm_ref[...] = nm
        nv_ref[...] = nv

    if by_rows:
        blk = pl.BlockSpec((None, tr, N), lambda i: (0, i, 0))
        gblk = pl.BlockSpec((tr, N), lambda i: (g_blk + i, 0))
    else:
        blk = pl.BlockSpec((None, K, tc), lambda i: (0, 0, i))
        gblk = pl.BlockSpec((K, tc), lambda i: (0, i))
    return pl.pallas_call(
        body, name=name, grid=(K // tr if by_rows else N // tc,),
        in_specs=[blk, gblk, gblk, blk, blk], out_specs=[blk] * 4,
        out_shape=[jax.ShapeDtypeStruct((1, K, N), F32)] * 4,
        compiler_params=_params(("parallel",)),
    )(w, g1, g2, m, v)


_HBM_SPEC = pl.BlockSpec(memory_space=pltpu.HBM)
_SEM_SPEC = pl.BlockSpec(memory_space=pltpu.SEMAPHORE)
_VMEM_SPEC = pl.BlockSpec(memory_space=pltpu.VMEM)
_EFFECT = pltpu.SideEffectType.DATAFLOW_SIDE_EFFECTING


def _place():
    return lax.axis_index("x"), lax.axis_index("y"), lax.axis_index("c")


def _other_chips(x, y):
    return [(1 - x, y), (x, 1 - y), (1 - x, 1 - y)]


def _all_gather_halves(wp, name):
    R, C = wp.shape
    half = R // 2
    assert half % 16 == 0

    def body(w_ref, out_ref, ici_send, ici_recv, d2d_send, d2d_recv, local_sem):
        x, y, c = _place()
        me = 2 * x + y
        chips = _other_chips(x, y)
        mine = pl.ds(pl.multiple_of(c * half, 16), half)
        theirs = pl.ds(pl.multiple_of((1 - c) * half, 16), half)
        local = pltpu.make_async_copy(w_ref, out_ref.at[me], local_sem)
        local.start()
        sends = []
        for n, (px, py) in enumerate(chips):
            cp = pltpu.make_async_remote_copy(
                src_ref=w_ref.at[mine], dst_ref=out_ref.at[me, mine], send_sem=ici_send.at[n],
                recv_sem=ici_recv.at[n], device_id=(px, py, c), device_id_type=MESH)
            cp.start()
            sends.append(cp)
        for n, (px, py) in enumerate(chips):
            slot = 2 * px + py
            pltpu.make_async_remote_copy(
                src_ref=w_ref.at[mine], dst_ref=out_ref.at[slot, mine], send_sem=ici_send.at[n],
                recv_sem=ici_recv.at[n], device_id=(px, py, c), device_id_type=MESH).wait_recv()
            cp = pltpu.make_async_remote_copy(
                src_ref=out_ref.at[slot, mine], dst_ref=out_ref.at[slot, mine], send_sem=d2d_send.at[n],
                recv_sem=d2d_recv.at[n], device_id=(x, y, 1 - c), device_id_type=MESH)
            cp.start()
            sends.append(cp)
        for n, (px, py) in enumerate(chips):
            slot = 2 * px + py
            pltpu.make_async_remote_copy(
                src_ref=out_ref.at[slot, theirs], dst_ref=out_ref.at[slot, theirs], send_sem=d2d_send.at[n],
                recv_sem=d2d_recv.at[n], device_id=(x, y, 1 - c), device_id_type=MESH).wait_recv()
        for cp in sends:
            cp.wait_send()
        local.wait()

    return pl.pallas_call(
        body, name=name,
        in_specs=[_ANY_SPEC], out_specs=_ANY_SPEC,
        out_shape=jax.ShapeDtypeStruct((N_CHIPS, R, C), wp.dtype),
        scratch_shapes=[pltpu.SemaphoreType.DMA((3,)), pltpu.SemaphoreType.DMA((3,)), pltpu.SemaphoreType.DMA((3,)),
                        pltpu.SemaphoreType.DMA((3,)), pltpu.SemaphoreType.DMA],
    )(wp)


def _chip_copies(src_ref, land_ref, sems, gather):
    x, y, c = _place()
    me = 2 * x + y
    out, back = [], []
    for n, (px, py) in enumerate(_other_chips(x, y)):
        src = src_ref if gather else src_ref.at[2 * px + py]
        out.append(pltpu.make_async_remote_copy(
            src_ref=src, dst_ref=land_ref.at[me] if gather else land_ref.at[n],
            send_sem=sems[n], recv_sem=sems[3 + n], device_id=(px, py, c), device_id_type=MESH))
        back.append(pltpu.make_async_remote_copy(
            src_ref=src, dst_ref=land_ref.at[2 * px + py] if gather else land_ref.at[n],
            send_sem=sems[n], recv_sem=sems[3 + n], device_id=(px, py, c), device_id_type=MESH))
    return out, back


def _xchg_start(src, land, gather, order, name):
    def body(src_ref, land_ref, order_ref, *outs):
        sems = outs[0:6]
        token = outs[8]
        out, _ = _chip_copies(src_ref, land_ref, sems, gather)
        for cp in out:
            cp.start()
        token[...] = jnp.zeros_like(token)

    outs = pl.pallas_call(
        body, name=name,
        out_shape=(pltpu.SemaphoreType.DMA(()),) * 6 + (
            pltpu.HBM(src.shape, src.dtype), pltpu.HBM(land.shape, land.dtype),
            jax.ShapeDtypeStruct((8, LANE), F32)),
        in_specs=(_HBM_SPEC, _HBM_SPEC, _ANY_SPEC),
        out_specs=(_SEM_SPEC,) * 6 + (_HBM_SPEC, _HBM_SPEC, _VMEM_SPEC),
        input_output_aliases={0: 6, 1: 7},
        compiler_params=pltpu.CompilerParams(has_side_effects=_EFFECT),
    )(pltpu.with_memory_space_constraint(src, pltpu.HBM), pltpu.with_memory_space_constraint(land, pltpu.HBM), order)
    return outs[0:6], outs[6], outs[7], outs[8]


def _xchg_wait(started, gather, after, name):
    sems, src, land, _ = started

    def body(src_ref, land_ref, *rest):
        _, back = _chip_copies(src_ref, land_ref, rest[0:6], gather)
        for cp in back:
            cp.wait_send()
            cp.wait_recv()

    return pl.pallas_call(
        body, name=name,
        out_shape=(pltpu.HBM(src.shape, src.dtype), pltpu.HBM(land.shape, land.dtype)),
        in_specs=(_HBM_SPEC, _HBM_SPEC) + (_SEM_SPEC,) * 6 + (_ANY_SPEC,),
        out_specs=(_HBM_SPEC, _HBM_SPEC),
        input_output_aliases={0: 0, 1: 1},
        compiler_params=pltpu.CompilerParams(has_side_effects=_EFFECT),
    )(src, land, *sems, after)


def _sib_copy(src_ref, land_ref, send_sem, recv_sem):
    x, y, c = _place()
    return pltpu.make_async_remote_copy(src_ref=src_ref, dst_ref=land_ref, send_sem=send_sem, recv_sem=recv_sem,
                                        device_id=(x, y, 1 - c), device_id_type=MESH)


def _sib_start(src, name):
    land = lax.empty(src.shape, src.dtype)

    def body(src_ref, land_ref, send_sem, recv_sem, src_thru, land_thru, token):
        _sib_copy(src_ref, land_ref, send_sem, recv_sem).start()
        token[...] = jnp.zeros_like(token)

    return pl.pallas_call(
        body, name=name,
        out_shape=(pltpu.SemaphoreType.DMA(()), pltpu.SemaphoreType.DMA(()),
                   pltpu.HBM(src.shape, src.dtype), pltpu.HBM(land.shape, land.dtype),
                   jax.ShapeDtypeStruct((8, LANE), F32)),
        in_specs=(_HBM_SPEC, _HBM_SPEC),
        out_specs=(_SEM_SPEC, _SEM_SPEC, _HBM_SPEC, _HBM_SPEC, _VMEM_SPEC),
        input_output_aliases={0: 2, 1: 3},
        compiler_params=pltpu.CompilerParams(has_side_effects=_EFFECT),
    )(pltpu.with_memory_space_constraint(src, pltpu.HBM), pltpu.with_memory_space_constraint(land, pltpu.HBM))


def _sib_wait(started, after, name):
    send_sem, recv_sem, src, land, _ = started

    def body(src_ref, land_ref, send_sem, recv_sem, after_ref, src_out, land_out):
        cp = _sib_copy(src_ref, land_ref, send_sem, recv_sem)
        cp.wait_send()
        cp.wait_recv()

    return pl.pallas_call(
        body, name=name,
        out_shape=(pltpu.HBM(src.shape, src.dtype), pltpu.HBM(land.shape, land.dtype)),
        in_specs=(_HBM_SPEC, _HBM_SPEC, _SEM_SPEC, _SEM_SPEC, _ANY_SPEC),
        out_specs=(_HBM_SPEC, _HBM_SPEC),
        input_output_aliases={0: 0, 1: 1},
        compiler_params=pltpu.CompilerParams(has_side_effects=_EFFECT),
    )(src, land, send_sem, recv_sem, after)


def _sum_slabs(gp, recv, chip, name):
    _, R, C = gp.shape
    tr = _tile(R, PACK_ROWS, 16)

    def body(chip_ref, own_ref, r0_ref, r1_ref, r2_ref, o_ref):
        acc = own_ref[...].astype(F32) + r0_ref[...].astype(F32)
        o_ref[...] = (acc + r1_ref[...].astype(F32)) + r2_ref[...].astype(F32)

    def got(n):
        return pl.BlockSpec((None, tr, C), lambda i, chip_ref: (n, i, 0))

    return pl.pallas_call(
        body, name=name,
        grid_spec=pltpu.PrefetchScalarGridSpec(
            num_scalar_prefetch=1, grid=(R // tr,),
            in_specs=[pl.BlockSpec((None, tr, C), lambda i, chip_ref: (chip_ref[0], i, 0)), got(0), got(1), got(2)],
            out_specs=pl.BlockSpec((tr, C), lambda i, chip_ref: (i, 0))),
        out_shape=jax.ShapeDtypeStruct((R, C), F32),
        compiler_params=_params(("parallel",)),
    )(jnp.reshape(chip, (1,)).astype(jnp.int32), gp, recv, recv, recv)


def _all_reduce_vec(vec, name):
    VR, W = vec.shape

    def body(vec_ref, vall_ref, vout_ref, vsend_sems, vrecv_sems):
        x, y, c = _place()
        vall_ref[4 * x + 2 * y + c] = vec_ref[...]
        sends = []
        peers = []
        for r in range(1, N_DEV):
            dx, dy, dc = (r >> 2) & 1, (r >> 1) & 1, r & 1
            peer = (x ^ dx, y ^ dy, c ^ dc)
            peers.append(peer)
            cp = pltpu.make_async_remote_copy(
                src_ref=vec_ref, dst_ref=vall_ref.at[4 * x + 2 * y + c], send_sem=vsend_sems.at[r - 1],
                recv_sem=vrecv_sems.at[r - 1], device_id=peer, device_id_type=MESH)
            cp.start()
            sends.append(cp)
        for r, peer in enumerate(peers):
            pltpu.make_async_remote_copy(
                src_ref=vec_ref, dst_ref=vall_ref.at[4 * peer[0] + 2 * peer[1] + peer[2]],
                send_sem=vsend_sems.at[r], recv_sem=vrecv_sems.at[r],
                device_id=peer, device_id_type=MESH).wait_recv()
        total = vall_ref[0]
        for d in range(1, N_DEV):
            total = total + vall_ref[d]
        vout_ref[...] = total
        for cp in sends:
            cp.wait_send()

    outs = pl.pallas_call(
        body, name=name,
        in_specs=[_VMEM_SPEC], out_specs=[_VMEM_SPEC, _VMEM_SPEC],
        out_shape=[jax.ShapeDtypeStruct((N_DEV, VR, W), F32), jax.ShapeDtypeStruct((VR, W), F32)],
        scratch_shapes=[pltpu.SemaphoreType.DMA((N_DEV - 1,)), pltpu.SemaphoreType.DMA((N_DEV - 1,))],
    )(vec)
    return outs[1]


class _Pack:
    def __init__(self, group, C):
        self.group, self.C = group, C
        self.rows, self.offs, off = {}, {}, 0
        for nm, (K, N), _ in group:
            assert N <= C, nm
            self.rows[nm] = K if 2 * N > C else -(-(K * N) // C)
            self.offs[nm] = off
            off += -(-self.rows[nm] // 16) * 16
        self.used = off
        self.R = -(-off // PACK_ROWS) * PACK_ROWS

    def _rows_of(self, a):
        K, N = a.shape
        if 2 * N > self.C:
            a = jnp.pad(a, ((0, 0), (0, self.C - N)))
        else:
            a = jnp.pad(a.reshape(-1), (0, -(K * N) % self.C)).reshape(-1, self.C)
        return jnp.pad(a, ((0, -a.shape[0] % 16), (0, 0)))

    def pack(self, shards):
        parts = [self._rows_of(shards[nm].astype(BF16)) for nm, _, _ in self.group]
        return jnp.concatenate(parts + [jnp.zeros((self.R - self.used, self.C), BF16)], axis=0)

    def _shard_of(self, rows, shape):
        K, N = shape
        return rows[:, :N] if 2 * N > self.C else rows.reshape(-1)[:K * N].reshape(K, N)

    def part(self, flat, nm, shape):
        return self._shard_of(flat[self.offs[nm]:self.offs[nm] + self.rows[nm]], shape)

    def slab_rows(self, nm, g):
        (K, N), axis = next((shape, axis) for n, shape, axis in self.group if n == nm)
        cuts = [g[:, k * N:(k + 1) * N] if axis == 1 else g[k * K:(k + 1) * K, :] for k in range(N_CHIPS)]
        return jnp.stack([self._rows_of(c.astype(BF16)) for c in cuts])

    def slabs(self, grads):
        parts = [self.slab_rows(nm, grads[nm]) for nm, _, _ in self.group]
        return jnp.concatenate(parts + [jnp.zeros((N_CHIPS, self.R - self.used, self.C), BF16)], axis=1)

    def full(self, gathered, names=None):
        res = {}
        for nm, (K, N), axis in self.group:
            if names is None or nm in names:
                rows = gathered[:, self.offs[nm]:self.offs[nm] + self.rows[nm]]
                res[nm] = jnp.concatenate([self._shard_of(rows[k], (K, N)) for k in range(N_CHIPS)], axis=axis)
        return res


def _rope_tables(S):
    pos = jnp.arange(S, dtype=F32)
    inv = 1.0 / (ROPE_THETA ** (jnp.arange(0, MLA_ROPE, 2, dtype=F32) / MLA_ROPE))
    ang = pos[:, None] * inv[None, :]
    cos, sin = jnp.cos(ang), jnp.sin(ang)
    half = MLA_ROPE // 2
    z = jnp.zeros((S, half), F32)
    one = jnp.ones((S, LANE - MLA_ROPE), F32)
    zero = jnp.zeros((S, LANE - MLA_ROPE), F32)
    kc = jnp.concatenate([cos, cos, one], axis=1)
    ksa = jnp.concatenate([-sin, z, zero], axis=1)
    ksb = jnp.concatenate([z, sin, zero], axis=1)
    qc = jnp.concatenate([jnp.ones((S, MLA_NOPE), F32), kc], axis=1)
    qsa = jnp.concatenate([jnp.zeros((S, MLA_NOPE), F32), ksa], axis=1)
    qsb = jnp.concatenate([jnp.zeros((S, MLA_NOPE), F32), ksb], axis=1)
    return (kc, ksa, ksb), (qc, qsa, qsb)


def _pad_cols(a, width):
    return jnp.pad(a, ((0, 0), (0, width - a.shape[1])))


def kernel(x, attn_norm, w_in, fox_f_bias, q_norm, w_uq, kv_norm, w_ukv, w_mla_branch, w_fox_branch, w_out, mlp_norm, w_up, w_down, final_norm, loss_target, m_attn_norm, m_w_in, m_fox_f_bias, m_q_norm, m_w_uq, m_kv_norm, m_w_ukv, m_w_mla_branch, m_w_fox_branch, m_w_out, m_mlp_norm, m_w_up, m_w_down, m_final_norm, v_attn_norm, v_w_in, v_fox_f_bias, v_q_norm, v_w_uq, v_kv_norm, v_w_ukv, v_w_mla_branch, v_w_fox_branch, v_w_out, v_mlp_norm, v_w_up, v_w_down, v_final_norm):
    _, S, D = x.shape
    H, HF = MLA_HEADS, FOX_HEADS
    QL, KVL = MLA_Q_LORA, MLA_KV_LORA
    assert H == HF and H <= 8
    xs = x[0]
    target = loss_target[0]
    C = D
    chip = 2 * lax.axis_index("x") + lax.axis_index("y")

    def flip(a):
        return jnp.transpose(a, (0, 2, 1))

    w_in, m_w_in, v_w_in = flip(w_in), flip(m_w_in), flip(v_w_in)
    weights = {"attn_norm": attn_norm, "w_in": w_in, "fox_f_bias": fox_f_bias, "q_norm": q_norm, "w_uq": w_uq,
               "kv_norm": kv_norm, "w_ukv": w_ukv, "w_mla_branch": w_mla_branch, "w_fox_branch": w_fox_branch,
               "w_out": w_out, "mlp_norm": mlp_norm, "w_up": w_up, "w_down": w_down, "final_norm": final_norm}
    moments = {"attn_norm": (m_attn_norm, v_attn_norm), "w_in": (m_w_in, v_w_in), "fox_f_bias": (m_fox_f_bias, v_fox_f_bias),
               "q_norm": (m_q_norm, v_q_norm), "w_uq": (m_w_uq, v_w_uq), "kv_norm": (m_kv_norm, v_kv_norm),
               "w_ukv": (m_w_ukv, v_w_ukv), "w_mla_branch": (m_w_mla_branch, v_w_mla_branch),
               "w_fox_branch": (m_w_fox_branch, v_w_fox_branch), "w_out": (m_w_out, v_w_out),
               "mlp_norm": (m_mlp_norm, v_mlp_norm), "w_up": (m_w_up, v_w_up), "w_down": (m_w_down, v_w_down),
               "final_norm": (m_final_norm, v_final_norm)}

    def group(names_axes):
        return [(nm, weights[nm].shape[1:], axis) for nm, axis in names_axes]

    pack_a = _Pack(group([("w_in", 0), ("w_uq", 1), ("w_ukv", 1)]), C)
    pack_b = _Pack(group([("w_down", 0), ("w_up", 1), ("w_out", 0), ("w_mla_branch", 1), ("w_fox_branch", 1)]), C)
    RA, RB = pack_a.R, pack_b.R
    wp_a = pack_a.pack({nm: weights[nm][0] for nm, _, _ in pack_a.group})
    wp_b = pack_b.pack({nm: weights[nm][0] for nm, _, _ in pack_b.group})
    n_in = w_in.shape[1]
    rows_in = -(-n_in // 16) * 16
    assert pack_a.offs["w_in"] == 0 and all((k * n_in) % 16 + n_in <= rows_in for k in range(N_CHIPS))
    shifted = lax.dynamic_update_slice(jnp.zeros((rows_in, C), BF16), wp_a[:n_in], ((chip * n_in) % 16, 0))
    wp_a = jnp.concatenate([shifted, wp_a[rows_in:]], axis=0)
    gathered_a = _all_gather_halves(wp_a, "all_gather_a")
    ag_b = _xchg_start(wp_b, lax.empty((N_CHIPS, RB, C), BF16), True, gathered_a, "all_gather_start_b")
    xn = _norm_fwd(xs, attn_norm, "attn_norm_fwd", order=ag_b[3])
    full = pack_a.full(gathered_a, ("w_uq", "w_ukv"))
    tile0 = [(k * n_in) // 16 * 16 for k in range(N_CHIPS)]
    total = tile0[-1] + rows_in
    full["w_in"] = sum(jnp.pad(gathered_a[k, :rows_in], ((tile0[k], total - tile0[k] - rows_in), (0, 0)))
                       for k in range(N_CHIPS))

    o_ckv = QL
    o_kr = o_ckv + KVL
    o_fq = o_kr + MLA_ROPE
    o_ff = o_fq + 3 * HF * FOX_HEAD_DIM
    o_g = o_ff + HF
    wi = full["w_in"]
    assert N_CHIPS * n_in == o_g + 2 * D and wi.shape[0] >= o_g + 2 * D
    WS = QL + KVL + 2 * LANE
    NQKV = 3 * HF * FOX_HEAD_DIM

    def pad_rows(a, rows):
        return jnp.pad(a, ((0, rows - a.shape[0]), (0, 0)))

    w_small = jnp.concatenate([wi[:o_kr], pad_rows(wi[o_kr:o_fq], LANE), pad_rows(wi[o_ff:o_g], LANE)], axis=0)
    w_qkv = wi[o_fq:o_ff]
    w_g = wi[o_g:o_g + 2 * D]
    w_pack = jnp.concatenate([w_small, w_qkv, w_g], axis=0)
    dqk = MLA_NOPE + MLA_ROPE
    w_uq_p = jnp.pad(full["w_uq"].reshape(QL, H, dqk), ((0, 0), (0, 0), (0, QPAD - dqk))).reshape(QL, H * QPAD)
    ukv = full["w_ukv"].reshape(KVL, H, MLA_NOPE + MLA_V)
    w_ukv_p = jnp.concatenate([ukv[:, :, :MLA_NOPE].reshape(KVL, H * MLA_NOPE),
                               ukv[:, :, MLA_NOPE:].reshape(KVL, H * MLA_V)], axis=1)

    (kc, ksa, ksb), (qc, qsa, qsb) = _rope_tables(S)
    bias_pad = _pad_cols(fox_f_bias, LANE)

    small = _matmul(xn, w_small, "nt", [F32], "proj_small")
    n_fq = HF * FOX_HEAD_DIM
    q_scale = jnp.concatenate([jnp.full((1, n_fq), LOG2E / math.sqrt(FOX_HEAD_DIM), F32),
                               jnp.ones((1, NQKV - n_fq), F32)], axis=1)
    qkv = _matmul(xn, w_qkv, "nt", [BF16], "proj_qkv", col_extras=(q_scale,), epilogue=lambda acc, cs: (acc * cs,))
    gpre = _matmul(xn, w_g, "nt", [F32], "proj_gates")
    cqn, ckvn, kr, cum = _prep_fwd(small, q_norm, kv_norm, bias_pad, kc, ksa, ksb, HF, "prep_fwd")
    c2_mla = LOG2E / math.sqrt(dqk)
    q_rot = _matmul(cqn, w_uq_p, "nn", [BF16], "mla_q_up", tn=QPAD, row_extras=(qc * c2_mla, qsa * c2_mla, qsb * c2_mla),
                    epilogue=lambda acc, c, sa, sb: (_rope(acc, c, sa, sb, 1),))
    kv2 = _matmul(ckvn, w_ukv_p, "nn", [BF16], "mla_kv_up")

    mla = _AttT(S, H, (q_rot, QPAD, 0, True), [(kv2, MLA_NOPE, 0, True), (kr, LANE, 0, False)],
                (kv2, MLA_V, H, True), 1.0 / math.sqrt(dqk), True)
    o_mla, lse_mla = _att_fwd_t(mla, "mla_att_fwd")

    cum_t = jnp.transpose(cum[:, :HF]) * LOG2E
    cum_rep = jnp.broadcast_to(cum_t[:, :, None], (HF, S, min(QSUB, _tile(S, ATT_T))))
    fox = _AttT(S, HF, (qkv, FOX_HEAD_DIM, 0, True), [(qkv, FOX_HEAD_DIM, HF, True)],
                (qkv, FOX_HEAD_DIM, 2 * HF, True), 1.0 / math.sqrt(FOX_HEAD_DIM), False, cum_rep)
    o_fox, ox_fox, lse_fox = _att_fwd_t(fox, "fox_att_fwd", exact=True)

    own_b, land_b = _xchg_wait(ag_b, True, lse_fox, "all_gather_wait_b")
    gathered_b = lax.dynamic_update_slice(land_b, own_b[None], (chip, 0, 0))
    full.update(pack_b.full(gathered_b, ("w_mla_branch", "w_fox_branch", "w_out")))
    w_mb, w_fb, w_o = (full[n] for n in ("w_mla_branch", "w_fox_branch", "w_out"))

    def b_of(nm, mode, tn, tk):
        (K, N), axis = next((shape, axis) for n, shape, axis in pack_b.group if n == nm)
        off = pack_b.offs[nm]
        shape = (N_CHIPS * K, N) if axis == 0 else (K, N_CHIPS * N)
        t_r, t_c = (tk, tn) if mode == "nn" else (tn, tk)
        t_r, t_c = _tile(shape[0], t_r), _tile(shape[1], t_c)
        if not (N == C and K % t_r == 0 and N % t_c == 0 and off % t_r == 0):
            return pack_b.full(gathered_b, (nm,))[nm], None
        base = off // t_r
        if axis == 0:
            per = K // t_r
            place = lambda rb, cb: (rb // per, base + rb % per, cb)
        else:
            per = N // t_c
            place = lambda rb, cb: (cb // per, base + rb, cb % per)
        return gathered_b, (shape, (lambda j, k: place(k, j)) if mode == "nn" else (lambda j, k: place(j, k)))

    y_mla = _matmul(o_mla, w_mb, "nn", [F32], "mla_branch")

    def gate_merge(acc, ga, gb, ya):
        return acc, _sigmoid(ga) * ya + _sigmoid(gb) * acc

    y_fox, merged = _matmul(o_fox, w_fb, "nn", [F32, BF16], "fox_branch_gates", extras=((gpre, 0), (gpre, 1), y_mla),
                            epilogue=gate_merge)
    h1 = _matmul(merged, w_o, "nn", [F32], "out_proj", extras=(xs,), epilogue=lambda acc, r: (acc + r,))
    hn = _norm_fwd(h1, mlp_norm, "mlp_norm_fwd")

    def relu2(acc):
        a = jnp.maximum(acc, 0.0)
        return a * a, a

    w_u, w_u_in = b_of("w_up", "nn", 512, 2048)
    u, a_pos = _matmul(hn, w_u, "nn", [BF16, BF16], "mlp_up", epilogue=relu2, b_in=w_u_in)
    w_d, w_d_in = b_of("w_down", "nn", 1024, 2048)
    h2 = _matmul(u, w_d, "nn", [F32], "mlp_down", tn=1024, extras=(h1,), epilogue=lambda acc, r: (acc + r,),
                 b_in=w_d_in)
    dh2, dh2_b, g_final, loss_part = _final(h2, final_norm.reshape(1, D), target, "final_norm_loss")

    gp_b = lax.empty((N_CHIPS, RB, C), BF16)
    by_glue = {}

    def grad_b(nm, a, b, name):
        nonlocal gp_b
        (K, N), axis = next((shape, axis) for n, shape, axis in pack_b.group if n == nm)
        off = pack_b.offs[nm]
        tm = min(1024, K) if axis == 0 else min(1024, a.shape[1])
        tn = min(1024, N) if axis == 1 else min(1024, b.shape[1])
        if not (N == C and tm % LANE == 0 and tn % LANE == 0 and K % tm == 0 and N % tn == 0 and off % tm == 0):
            by_glue[nm] = _mm_tn(a, b, name)
            return
        base = off // tm
        if axis == 0:
            per = K // tm
            place = lambda i, j: (i // per, base + i % per, j)
        else:
            per = N // tn
            place = lambda i, j: (j // per, base + i, j % per)
        gp_b = _mm_tn(a, b, name, tm=tm, tn=tn, into=(gp_b, place))

    w_d, w_d_in = b_of("w_down", "nt", 512, 2048)
    da = _matmul(dh2_b, w_d, "nt", [BF16], "mlp_down_dx", extras=(a_pos,),
                 epilogue=lambda acc, a: (acc * (2.0 * a.astype(F32)),), b_in=w_d_in)
    grad_b("w_down", u, dh2_b, "mlp_down_dw")
    w_u, w_u_in = b_of("w_up", "nt", 1024, 2048)
    dhn = _matmul(da, w_u, "nt", [F32], "mlp_up_dx", tn=1024, b_in=w_u_in)
    grad_b("w_up", hn, da, "mlp_up_dw")
    dh1, dh1_b, g_mlp_norm = _norm_bwd(h1, dhn, mlp_norm, dh2, "mlp_norm_bwd")

    def gate_bwd(acc, ga, gb, ya, yb):
        ga, gb = _sigmoid(ga), _sigmoid(gb)
        return acc * ga, acc * gb, acc * ya * (ga * (1.0 - ga)), acc * yb * (gb * (1.0 - gb))

    dy_mla, dy_fox, dg_mla, dg_fox = _matmul(dh1_b, w_o, "nt", [BF16] * 4, "out_proj_dx_gates",
                                             extras=((gpre, 0), (gpre, 1), y_mla, y_fox), epilogue=gate_bwd)
    grad_b("w_out", merged, dh1_b, "out_proj_dw")
    do_mla = _matmul(dy_mla, w_mb, "nt", [BF16], "mla_branch_dx")
    grad_b("w_mla_branch", o_mla, dy_mla, "mla_branch_dw")
    do_fox = _matmul(dy_fox, w_fb, "nt", [BF16], "fox_branch_dx")
    grad_b("w_fox_branch", o_fox, dy_fox, "fox_branch_dw")
    for nm, g in by_glue.items():
        gp_b = lax.dynamic_update_slice(gp_b, pack_b.slab_rows(nm, g), (0, pack_b.offs[nm], 0))
    if RB > pack_b.used:
        gp_b = lax.dynamic_update_slice(gp_b, jnp.zeros((N_CHIPS, RB - pack_b.used, C), BF16), (0, pack_b.used, 0))

    rs_b = _xchg_start(gp_b, lax.empty((3, RB, C), BF16), False, do_fox, "grad_scatter_start_b")

    delta_mla = _att_delta_t(do_mla, o_mla, H, "mla_att_delta", order=rs_b[3])
    dq_rot, dk_nope, dkr_heads, dv_mla = _att_bwd_t(mla, do_mla, lse_mla, delta_mla, BF16, [BF16, F32],
                                                    "mla_att_bwd", dq_rope=(qc, qsa, qsb))
    delta_fox = _att_delta_t(do_fox, ox_fox, HF, "fox_att_delta")
    dfq, dfk, dfv, dcum = _att_bwd_t(fox, do_fox, lse_fox, delta_fox, BF16, [BF16], "fox_att_bwd")

    gp_b_sent, recv_b = _xchg_wait(rs_b, False, dfq, "grad_scatter_wait_b")
    swap_b = _sib_start(_sum_slabs(gp_b_sent, recv_b, chip, "grad_sum_b"), "grad_swap_start_b")

    dcqn = _matmul(dq_rot, w_uq_p, "nt", [F32], "mla_q_up_dx", order=swap_b[4])
    g_w_uq_p = _mm_tn(cqn, dq_rot, "mla_q_up_dw")
    dkv2 = jnp.concatenate([dk_nope, dv_mla], axis=1)
    dckvn = _matmul(dkv2, w_ukv_p, "nt", [F32], "mla_kv_up_dx")
    g_w_ukv_p = _mm_tn(ckvn, dkv2, "mla_kv_up_dw")

    dcum_rows = jnp.pad(dcum[:, :, 0], ((0, 8 - HF), (0, 0)))
    dlogf_rows = _suffix_sum_rows(dcum_rows, "fox_forget_suffix_sum")
    dlogf = _pad_cols(jnp.transpose(dlogf_rows[:HF]), LANE)
    d_small, g_q_norm, g_kv_norm, g_bias = _prep_bwd(
        small, dcqn, dckvn, dkr_heads, dlogf, q_norm, kv_norm, bias_pad, kc, ksa, ksb, H, "prep_bwd")
    dproj = [d_small, dfq, dfk, dfv, dg_mla, dg_fox]
    gs, gfq, gfk, gfv, gg_mla, gg_fox = [
        _matmul(part, xn, "tn", [BF16], "proj_dw_" + tag, tm=1024, tn=1024, tk=2048)
        for part, tag in zip(dproj, ("small", "fq", "fk", "fv", "g_mla", "g_fox"))]

    g_w_in = jnp.concatenate([gs[:o_kr], gs[o_kr:o_kr + MLA_ROPE], gfq, gfk, gfv,
                              gs[o_kr + LANE:o_kr + LANE + HF], gg_mla, gg_fox], axis=0)
    g_w_uq = g_w_uq_p.reshape(QL, H, QPAD)[:, :, :dqk].reshape(QL, H * dqk)
    g_w_ukv = jnp.concatenate([g_w_ukv_p[:, :H * MLA_NOPE].reshape(KVL, H, MLA_NOPE),
                               g_w_ukv_p[:, H * MLA_NOPE:].reshape(KVL, H, MLA_V)], axis=2).reshape(KVL, -1)

    gp_a = pack_a.slabs({"w_in": g_w_in, "w_uq": g_w_uq, "w_ukv": g_w_ukv})
    rs_a = _xchg_start(gp_a, lax.empty((3, RA, C), BF16), False, gg_fox, "grad_scatter_start_a")
    dxn = _matmul_parts(dproj, w_pack, "nn", F32, "proj_dx", order=rs_a[3])
    grad_x, _, g_attn_norm = _norm_bwd(xs, dxn, attn_norm, dh1, "attn_norm_bwd")
    gp_a_sent, recv_a = _xchg_wait(rs_a, False, grad_x, "grad_scatter_wait_a")
    swap_a = _sib_start(_sum_slabs(gp_a_sent, recv_a, chip, "grad_sum_a"), "grad_swap_start_a")
    vec_w = max(D, LANE)
    vec_rows = [g_attn_norm, g_mlp_norm, g_final, g_q_norm, g_kv_norm, g_bias, loss_part]
    vec = jnp.concatenate([_pad_cols(v, vec_w) for v in vec_rows] + [jnp.zeros((1, vec_w), F32)], axis=0)
    vsum = _all_reduce_vec(vec, "all_reduce_vectors")
    part_b, sib_b = _sib_wait(swap_b, vsum, "grad_swap_wait_b")

    grads, deltas, new_m, new_v = {}, {}, {}, {}

    def update(pack, mine, theirs):
        for nm, shape, _ in pack.group:
            K, N = shape
            if N == pack.C and K % 8 == 0 and pack.offs[nm] % _tile(K, 256, 8) == 0:
                g, d, nm_, nv_ = _adamw(weights[nm], mine, theirs, moments[nm][0], moments[nm][1], "adamw_" + nm,
                                        g_row=pack.offs[nm])
            else:
                g, d, nm_, nv_ = _adamw(weights[nm], pack.part(mine, nm, shape), pack.part(theirs, nm, shape),
                                        moments[nm][0], moments[nm][1], "adamw_" + nm)
            grads[nm], deltas[nm], new_m[nm], new_v[nm] = g, d, nm_, nv_
        return g

    last_b = update(pack_b, part_b, sib_b)
    part_a, sib_a = _sib_wait(swap_a, last_b, "grad_swap_wait_a")
    update(pack_a, part_a, sib_a)

    vec_names = ["attn_norm", "mlp_norm", "final_norm", "q_norm", "kv_norm", "fox_f_bias"]

    def vec_pack(arrs):
        return jnp.concatenate([_pad_cols(a.reshape(1, -1), vec_w) for a in arrs]
                               + [jnp.zeros((2, vec_w), F32)], axis=0)[None]

    vg, vd, vm, vv = _adamw(vec_pack([weights[n] for n in vec_names]), vsum, jnp.zeros_like(vsum),
                            vec_pack([moments[n][0] for n in vec_names]), vec_pack([moments[n][1] for n in vec_names]),
                            "adamw_vectors")
    for r, nm in enumerate(vec_names):
        shp = weights[nm].shape
        n = weights[nm].size
        grads[nm] = vsum[r, :n].reshape(shp)
        deltas[nm], new_m[nm], new_v[nm] = (vd[0, r, :n].reshape(shp), vm[0, r, :n].reshape(shp),
                                            vv[0, r, :n].reshape(shp))
    loss = vsum[6, 0]

    for res in (grads, deltas, new_m, new_v):
        res["w_in"] = flip(res["w_in"])
    order = ["attn_norm", "w_in", "fox_f_bias", "q_norm", "w_uq", "kv_norm", "w_ukv", "w_mla_branch", "w_fox_branch",
             "w_out", "mlp_norm", "w_up", "w_down", "final_norm"]
    return (loss, grad_x[None], *[grads[n] for n in order], *[deltas[n] for n in order],
            *[new_m[n] for n in order], *[new_v[n] for n in order])
```

```python
import math

import jax
import jax.numpy as jnp
from jax import lax
from jax.experimental import pallas as pl
from jax.experimental.pallas import tpu as pltpu

CHUNK = 64
MLA_HEADS = 8
MLA_Q_LORA = 512
MLA_KV_LORA = 256
MLA_NOPE = 128
MLA_ROPE = 64
MLA_V = 128
ROPE_THETA = 10000.0
FOX_HEADS = 8
FOX_HEAD_DIM = 128
EPS = 1e-6

ADAM_LR = 0.001
ADAM_B1 = 0.9
ADAM_B2 = 0.999
ADAM_EPS = 1e-08
ADAM_WD = 0.01
ADAM_STEP = 10

LANE = 128
QPAD = 2 * LANE
N_CHIPS = 4
N_DEV = 8
VMEM_LIMIT = 48 * 1024 * 1024
ATT_T = 2048
QSUB = 256
ROW_T = 256
PACK_ROWS = 256
LOG2E = 1.4426950408889634

BF16 = jnp.bfloat16
F32 = jnp.float32
MESH = pl.DeviceIdType.MESH

_NT = (((1,), (1,)), ((), ()))
_TN = (((0,), (0,)), ((), ()))
_NN = (((1,), (0,)), ((), ()))


def _tile(dim, pref, align=LANE):
    if dim <= pref:
        return dim
    t = (pref // align) * align
    while t >= align:
        if dim % t == 0:
            return t
        t -= align
    return dim


def _params(sem=None):
    return pltpu.CompilerParams(dimension_semantics=sem, vmem_limit_bytes=VMEM_LIMIT)


_ANY_SPEC = pl.BlockSpec(memory_space=pl.ANY)


def _matmul(a, b, mode, out_dtypes, name, *, tm=1024, tn=512, tk=2048, extras=(), row_extras=(), col_extras=(),
            epilogue=None, order=None, into=None, b_in=None):
    b_shape = b.shape if b_in is None else b_in[0]
    if mode == "nn":
        (M, K), (K2, N) = a.shape, b_shape
    elif mode == "nt":
        (M, K), (N, K2) = a.shape, b_shape
    else:
        (K, M), (K2, N) = a.shape, b_shape
    assert K == K2, (name, a.shape, b_shape)
    tm, tn, tk = _tile(M, tm), _tile(N, tn), _tile(K, tk)
    nk = K // tk
    extras = [e if isinstance(e, tuple) else (e, 0) for e in extras]
    n_out = len(out_dtypes)
    n_ex = len(extras) + len(row_extras) + len(col_extras)
    n_ord = 0 if order is None else 1
    assert all(r.shape == (M, tn) for r in row_extras), name
    dims = {"nn": _NN, "nt": _NT, "tn": _TN}[mode]

    def body(*refs):
        a_ref, b_ref = refs[0], refs[1]
        ex_refs = refs[2:2 + n_ex]
        o_refs = refs[2 + n_ex + n_ord:2 + n_ex + n_ord + n_out]
        acc_ref = refs[2 + n_ex + n_ord + n_out]
        k = pl.program_id(2)
        part = lax.dot_general(a_ref[...], b_ref[...], dims, preferred_element_type=F32)

        @pl.when(k == 0)
        def _():
            acc_ref[...] = part

        @pl.when(k > 0)
        def _():
            acc_ref[...] += part

        @pl.when(k == nk - 1)
        def _():
            acc = acc_ref[...]
            if epilogue is None:
                outs = (acc,)
            else:
                outs = epilogue(acc, *[r[...] for r in ex_refs])
            for o_ref, o in zip(o_refs, outs):
                o_ref[...] = o.astype(o_ref.dtype)

    if mode == "nn":
        a_spec = pl.BlockSpec((tm, tk), lambda i, j, k: (i, k))
        b_spec = pl.BlockSpec((tk, tn), lambda i, j, k: (k, j))
    elif mode == "nt":
        a_spec = pl.BlockSpec((tm, tk), lambda i, j, k: (i, k))
        b_spec = pl.BlockSpec((tn, tk), lambda i, j, k: (j, k))
    else:
        a_spec = pl.BlockSpec((tk, tm), lambda i, j, k: (k, i))
        b_spec = pl.BlockSpec((tk, tn), lambda i, j, k: (k, j))
    if b_in is not None:
        b_block = (None, tn, tk) if mode == "nt" else (None, tk, tn)
        b_spec = pl.BlockSpec(b_block, lambda i, j, k: b_in[1](j, k))
    mn_spec = pl.BlockSpec((tm, tn), lambda i, j, k: (i, j))
    row_spec = pl.BlockSpec((tm, tn), lambda i, j, k: (i, 0))
    col_spec = pl.BlockSpec((1, tn), lambda i, j, k: (0, j))
    out_specs = [mn_spec] * n_out
    out_shape = [jax.ShapeDtypeStruct((M, N), dt) for dt in out_dtypes]
    aliases = {}
    if into is not None:
        buf, place = into
        assert n_out == 1 and n_ord == 1 and order is buf, name
        out_specs = [pl.BlockSpec((None, tm, tn), lambda i, j, k: place(i, j))]
        out_shape = [jax.ShapeDtypeStruct(buf.shape, buf.dtype)]
        aliases = {2 + n_ex: 0}
    outs = pl.pallas_call(
        body,
        name=name,
        grid=(M // tm, N // tn, nk),
        in_specs=([a_spec, b_spec]
                  + [pl.BlockSpec((tm, tn), lambda i, j, k, g=g: (i, j + g * (N // tn))) for _, g in extras]
                  + [row_spec] * len(row_extras) + [col_spec] * len(col_extras) + [_ANY_SPEC] * n_ord),
        out_specs=out_specs,
        out_shape=out_shape,
        scratch_shapes=[pltpu.VMEM((tm, tn), F32)],
        input_output_aliases=aliases,
        compiler_params=_params(("parallel", "parallel", "arbitrary")),
    )(a, b, *[e for e, _ in extras], *row_extras, *col_extras, *([] if order is None else [order]))
    return outs[0] if n_out == 1 else outs


def _matmul_parts(parts, b, mode, out_dtype, name, *, tm=1024, tn=1024, tk=1024, order=None):
    assert mode in ("nn", "tn")
    if mode == "nn":
        M, (K, N) = parts[0].shape[0], b.shape
        widths = [p.shape[1] for p in parts]
    else:
        K, N = b.shape
        widths = [p.shape[1] for p in parts]
        M = sum(widths)
    common = math.gcd(*widths)
    tm, tn, tk = _tile(M if mode == "nn" else common, tm), _tile(N, tn), _tile(common if mode == "nn" else K, tk)
    t_part = tk if mode == "nn" else tm
    assert sum(widths) == (K if mode == "nn" else M), name
    if any(w % t_part for w in widths):
        parts, widths = [jnp.concatenate(parts, axis=1)], [sum(widths)]
    lo =[sum(widths[:p]) // t_part for p in range(len(parts))]
    cnt = [w // t_part for w in widths]
    nk = K // tk
    n_parts = len(parts)
    n_ord = 0 if order is None else 1
    dims = _NN if mode == "nn" else _TN

    def body(*refs):
        a_refs = refs[0:n_parts]
        b_ref = refs[n_parts]
        o_ref, acc_ref = refs[n_parts + 1 + n_ord], refs[n_parts + 2 + n_ord]
        i, k = pl.program_id(0), pl.program_id(2)
        sel = k if mode == "nn" else i
        for p in range(n_parts):
            @pl.when((sel >= lo[p]) & (sel < lo[p] + cnt[p]))
            def _(p=p):
                part = lax.dot_general(a_refs[p][...], b_ref[...], dims, preferred_element_type=F32)

                @pl.when(k == 0)
                def _():
                    acc_ref[...] = part

                @pl.when(k > 0)
                def _():
                    acc_ref[...] += part

        @pl.when(k == nk - 1)
        def _():
            o_ref[...] = acc_ref[...].astype(o_ref.dtype)

    def a_spec(p):
        if mode == "nn":
            return pl.BlockSpec((tm, tk), lambda i, j, k: (i, jnp.clip(k - lo[p], 0, cnt[p] - 1)))
        return pl.BlockSpec((tk, tm), lambda i, j, k: (
            jnp.where((i >= lo[p]) & (i < lo[p] + cnt[p]), k, 0), jnp.clip(i - lo[p], 0, cnt[p] - 1)))

    return pl.pallas_call(
        body, name=name, grid=(M // tm, N // tn, nk),
        in_specs=[a_spec(p) for p in range(n_parts)] + [pl.BlockSpec((tk, tn), lambda i, j, k: (k, j))]
        + [_ANY_SPEC] * n_ord,
        out_specs=pl.BlockSpec((tm, tn), lambda i, j, k: (i, j)),
        out_shape=jax.ShapeDtypeStruct((M, N), out_dtype),
        scratch_shapes=[pltpu.VMEM((tm, tn), F32)],
        compiler_params=_params(("parallel", "parallel", "arbitrary")),
    )(*parts, b, *([] if order is None else [order]))


def _mm_tn(a, b, name, tm=1024, tn=1024, into=None):
    return _matmul(a, b, "tn", [F32], name, tm=tm, tn=tn, tk=2048, into=into,
                   order=None if into is None else into[0])


def _row_spec(ts, width, col=0):
    return pl.BlockSpec((ts, width), lambda i: (i, col))


def _full_spec(shape):
    return pl.BlockSpec(shape, lambda i: tuple(0 for _ in shape))


def _rms(x):
    return lax.rsqrt(jnp.mean(x * x, axis=-1, keepdims=True) + EPS)


def _rms_bwd(x, dy, g):
    r = _rms(x)
    xh = x * r
    gy = dy * g
    dx = r * (gy - xh * jnp.mean(xh * gy, axis=-1, keepdims=True))
    return dx, dy * xh


def _norm_fwd(x, g, name, order=None):
    S, D = x.shape
    ts = _tile(S, ROW_T, 8)

    def body(x_ref, g_ref, *rest):
        o_ref = rest[-1]
        xv = x_ref[...]
        o_ref[...] = ((xv * _rms(xv)) * g_ref[...]).astype(BF16)

    extra = [] if order is None else [order]
    return pl.pallas_call(
        body, name=name, grid=(S // ts,),
        in_specs=[_row_spec(ts, D), _full_spec((1, D))] + [_ANY_SPEC] * len(extra),
        out_specs=_row_spec(ts, D),
        out_shape=jax.ShapeDtypeStruct((S, D), BF16),
        compiler_params=_params(("parallel",)),
    )(x, g, *extra)


def _norm_bwd(x, dy, g, dres, name):
    S, D = x.shape
    ts = _tile(S, ROW_T, 8)

    def body(x_ref, dy_ref, g_ref, dres_ref, dx_ref, dxb_ref, dg_ref):
        dx, dg_rows = _rms_bwd(x_ref[...], dy_ref[...], g_ref[...])
        dx = dres_ref[...] + dx
        dx_ref[...] = dx
        dxb_ref[...] = dx.astype(BF16)

        @pl.when(pl.program_id(0) == 0)
        def _():
            dg_ref[...] = jnp.zeros_like(dg_ref)

        dg_ref[...] += jnp.sum(dg_rows, axis=0, keepdims=True)

    return pl.pallas_call(
        body, name=name, grid=(S // ts,),
        in_specs=[_row_spec(ts, D), _row_spec(ts, D), _full_spec((1, D)), _row_spec(ts, D)],
        out_specs=[_row_spec(ts, D), _row_spec(ts, D), _full_spec((1, D))],
        out_shape=[jax.ShapeDtypeStruct((S, D), F32), jax.ShapeDtypeStruct((S, D), BF16),
                   jax.ShapeDtypeStruct((1, D), F32)],
        compiler_params=_params(("arbitrary",)),
    )(x, dy, g, dres)


def _rope(x, c, sa, sb, sign):
    w = x.shape[-1]
    half = MLA_ROPE // 2
    fwd = pltpu.roll(x, w - half, 1)
    back = pltpu.roll(x, half, 1)
    if sign < 0:
        return x * c - fwd * sa - back * sb
    return x * c + fwd * sa + back * sb


def _split3(x):
    hi = x.astype(BF16)
    r1 = x - hi.astype(F32)
    mid = r1.astype(BF16)
    lo = (r1 - mid.astype(F32)).astype(BF16)
    return hi, mid, lo


def _prep_fwd(small, q_norm, kv_norm, bias_pad, kc, ksa, ksb, n_heads, name):
    S, W = small.shape
    QL, KVL = q_norm.shape[1], kv_norm.shape[1]
    assert W == QL + KVL + 2 * LANE
    ts = _tile(S, ROW_T, 8)
    tri = (lax.broadcasted_iota(jnp.int32, (ts, ts), 0) >= lax.broadcasted_iota(jnp.int32, (ts, ts), 1)).astype(BF16)

    def body(s_ref, qn_ref, kvn_ref, b_ref, kc_ref, ksa_ref, ksb_ref, tri_ref,
             cqn_ref, ckvn_ref, kr_ref, cum_ref, carry_ref):
        cq = s_ref[:, 0:QL]
        cqn_ref[...] = ((cq * _rms(cq)) * qn_ref[...]).astype(BF16)
        ckv = s_ref[:, QL:QL + KVL]
        ckvn_ref[...] = ((ckv * _rms(ckv)) * kvn_ref[...]).astype(BF16)
        kr = s_ref[:, QL + KVL:QL + KVL + LANE]
        kr_ref[...] = _rope(kr, kc_ref[...], ksa_ref[...], ksb_ref[...], 1).astype(BF16)
        z = s_ref[:, QL + KVL + LANE:W] + b_ref[...]
        logf = jnp.minimum(z, 0.0) - jnp.log1p(jnp.exp(-jnp.abs(z)))
        lane = lax.broadcasted_iota(jnp.int32, logf.shape, 1)
        logf = jnp.where(lane < n_heads, logf, 0.0)

        @pl.when(pl.program_id(0) == 0)
        def _():
            carry_ref[...] = jnp.zeros_like(carry_ref)

        t = tri_ref[...]
        cum = carry_ref[...]
        for part in _split3(logf):
            cum = cum + jnp.dot(t, part, preferred_element_type=F32)
        cum_ref[...] = cum
        carry_ref[...] = cum[ts - 1:ts, :]

    return pl.pallas_call(
        body, name=name, grid=(S // ts,),
        in_specs=[_row_spec(ts, W), _full_spec((1, QL)), _full_spec((1, KVL)), _full_spec((1, LANE)),
                  _row_spec(ts, LANE), _row_spec(ts, LANE), _row_spec(ts, LANE), _full_spec((ts, ts))],
        out_specs=[_row_spec(ts, QL), _row_spec(ts, KVL), _row_spec(ts, LANE), _row_spec(ts, LANE)],
        out_shape=[jax.ShapeDtypeStruct((S, QL), BF16), jax.ShapeDtypeStruct((S, KVL), BF16),
                   jax.ShapeDtypeStruct((S, LANE), BF16), jax.ShapeDtypeStruct((S, LANE), F32)],
        scratch_shapes=[pltpu.VMEM((1, LANE), F32)],
        compiler_params=_params(("arbitrary",)),
    )(small, q_norm, kv_norm, bias_pad, kc, ksa, ksb, tri)


def _prep_bwd(small, dcqn, dckvn, dkr_heads, dlogf, q_norm, kv_norm, bias_pad, kc, ksa, ksb, n_heads, name):
    S, W = small.shape
    QL, KVL = q_norm.shape[1], kv_norm.shape[1]
    ts = _tile(S, ROW_T, 8)

    def body(s_ref, dcq_ref, dckv_ref, dkr_ref, dlf_ref, qn_ref, kvn_ref, b_ref, kc_ref, ksa_ref, ksb_ref,
             ds_ref, gq_ref, gkv_ref, gb_ref):
        dcq, gq_rows = _rms_bwd(s_ref[:, 0:QL], dcq_ref[...], qn_ref[...])
        ds_ref[:, 0:QL] = dcq.astype(BF16)
        dckv, gkv_rows = _rms_bwd(s_ref[:, QL:QL + KVL], dckv_ref[...], kvn_ref[...])
        ds_ref[:, QL:QL + KVL] = dckv.astype(BF16)
        dkr = dkr_ref[:, 0:LANE]
        for h in range(1, n_heads):
            dkr = dkr + dkr_ref[:, h * LANE:(h + 1) * LANE]
        ds_ref[:, QL + KVL:QL + KVL + LANE] = _rope(dkr, kc_ref[...], ksa_ref[...], ksb_ref[...], -1).astype(BF16)
        z = s_ref[:, QL + KVL + LANE:W] + b_ref[...]
        dff = dlf_ref[...] * (1.0 / (1.0 + jnp.exp(z)))
        ds_ref[:, QL + KVL + LANE:W] = dff.astype(BF16)

        @pl.when(pl.program_id(0) == 0)
        def _():
            gq_ref[...] = jnp.zeros_like(gq_ref)
            gkv_ref[...] = jnp.zeros_like(gkv_ref)
            gb_ref[...] = jnp.zeros_like(gb_ref)

        gq_ref[...] += jnp.sum(gq_rows, axis=0, keepdims=True)
        gkv_ref[...] += jnp.sum(gkv_rows, axis=0, keepdims=True)
        gb_ref[...] += jnp.sum(dff, axis=0, keepdims=True)

    return pl.pallas_call(
        body, name=name, grid=(S // ts,),
        in_specs=[_row_spec(ts, W), _row_spec(ts, QL), _row_spec(ts, KVL), _row_spec(ts, n_heads * LANE),
                  _row_spec(ts, LANE), _full_spec((1, QL)), _full_spec((1, KVL)), _full_spec((1, LANE)),
                  _row_spec(ts, LANE), _row_spec(ts, LANE), _row_spec(ts, LANE)],
        out_specs=[_row_spec(ts, W), _full_spec((1, QL)), _full_spec((1, KVL)), _full_spec((1, LANE))],
        out_shape=[jax.ShapeDtypeStruct((S, W), BF16), jax.ShapeDtypeStruct((1, QL), F32),
                   jax.ShapeDtypeStruct((1, KVL), F32), jax.ShapeDtypeStruct((1, LANE), F32)],
        compiler_params=_params(("arbitrary",)),
    )(small, dcqn, dckvn, dkr_heads, dlogf, q_norm, kv_norm, bias_pad, kc, ksa, ksb)


def _sigmoid(z):
    return 1.0 / (1.0 + jnp.exp(-z))


def _final(h, g, target, name):
    S, D = h.shape
    ts = _tile(S, ROW_T, 8)

    def body(h_ref, g_ref, t_ref, dh_ref, dhb_ref, dg_ref, loss_ref):
        hv = h_ref[...]
        gv = g_ref[...]
        err = (hv * _rms(hv)) * gv - t_ref[...]
        dh, dg_rows = _rms_bwd(hv, err / D, gv)
        dh_ref[...] = dh
        dhb_ref[...] = dh.astype(BF16)

        @pl.when(pl.program_id(0) == 0)
        def _():
            dg_ref[...] = jnp.zeros_like(dg_ref)
            loss_ref[...] = jnp.zeros_like(loss_ref)

        dg_ref[...] += jnp.sum(dg_rows, axis=0, keepdims=True)
        row_loss = jnp.mean(err * err, axis=-1, keepdims=True)
        loss_ref[...] += 0.5 * jnp.sum(row_loss, axis=0, keepdims=True)

    return pl.pallas_call(
        body, name=name, grid=(S // ts,),
        in_specs=[_row_spec(ts, D), _full_spec((1, D)), _row_spec(ts, D)],
        out_specs=[_row_spec(ts, D), _row_spec(ts, D), _full_spec((1, D)), _full_spec((1, LANE))],
        out_shape=[jax.ShapeDtypeStruct((S, D), F32), jax.ShapeDtypeStruct((S, D), BF16),
                   jax.ShapeDtypeStruct((1, D), F32), jax.ShapeDtypeStruct((1, LANE), F32)],
        compiler_params=_params(("arbitrary",)),
    )(h, g, target)


def _suffix_sum_rows(x, name):
    R, S = x.shape
    tb = _tile(S, 512)
    nb = S // tb
    tri = (lax.broadcasted_iota(jnp.int32, (tb, tb), 0) >= lax.broadcasted_iota(jnp.int32, (tb, tb), 1)).astype(BF16)

    def body(x_ref, tri_ref, o_ref, carry_ref):
        @pl.when(pl.program_id(0) == 0)
        def _():
            carry_ref[...] = jnp.zeros_like(carry_ref)

        xv = x_ref[...]
        t = tri_ref[...]
        acc = jnp.broadcast_to(carry_ref[:, 0:1], xv.shape)
        for part in _split3(xv):
            acc = acc + jnp.dot(part, t, preferred_element_type=F32)
        o_ref[...] = acc
        carry_ref[...] = jnp.broadcast_to(acc[:, 0:1], carry_ref.shape)

    rev = pl.BlockSpec((R, tb), lambda i: (0, nb - 1 - i))
    return pl.pallas_call(
        body, name=name, grid=(nb,),
        in_specs=[rev, _full_spec((tb, tb))], out_specs=rev,
        out_shape=jax.ShapeDtypeStruct((R, S), F32),
        scratch_shapes=[pltpu.VMEM((R, LANE), F32)],
        compiler_params=_params(("arbitrary",)),
    )(x, tri)


def _pairs(nb, by_key):
    if by_key:
        pr = [(i, j) for j in range(nb) for i in range(j, nb)]
    else:
        pr = [(i, j) for i in range(nb) for j in range(i + 1)]
    return (jnp.asarray([p[0] for p in pr], jnp.int32), jnp.asarray([p[1] for p in pr], jnp.int32), len(pr))


class _AttT:
    def __init__(self, S, n_heads, q, ks, v, scale, chunk_causal, cum_rep=None):
        self.S, self.H, self.q, self.ks, self.v = S, n_heads, q, ks, v
        self.scale, self.chunk_causal, self.cum_rep = scale, chunk_causal, cum_rep
        self.T = _tile(S, ATT_T)
        self.qs = min(QSUB, self.T)
        self.nb = S // self.T
        self.dq, self.dv = q[1], v[1]
        self.has_bias = cum_rep is not None

    def q_spec(self, op):
        _, w, off, per_head = op
        return pl.BlockSpec((self.T, w), lambda h, p, it, jt: (it[p], off + (h if per_head else 0)))

    def k_spec(self, op):
        _, w, off, per_head = op
        return pl.BlockSpec((self.T, w), lambda h, p, it, jt: (jt[p], off + (h if per_head else 0)))

    def row_q(self):
        return pl.BlockSpec((None, 1, self.T), lambda h, p, it, jt: (h, 0, it[p]))

    def cum_k(self):
        return pl.BlockSpec((None, self.T, self.qs), lambda h, p, it, jt: (h, jt[p], 0))

    def sub_blocks(self, masked):
        return [(q0, min(self.T, q0 + self.qs) if masked else self.T) for q0 in range(0, self.T, self.qs)]

    def scores(self, k, q_sub, cum, q0, masked):
        s = lax.dot_general(k, q_sub, _NT, preferred_element_type=F32)
        if self.has_bias:
            s = s - cum
        mask = None
        if masked:
            r = lax.broadcasted_iota(jnp.int32, s.shape, 0)
            c = lax.broadcasted_iota(jnp.int32, s.shape, 1) + q0
            mask = (r // CHUNK <= c // CHUNK) if self.chunk_causal else (r <= c)
        return s, mask


def _join(k_refs):
    return k_refs[0][...] if len(k_refs) == 1 else jnp.concatenate([r[...] for r in k_refs], axis=-1)


def _att_fwd_t(att, name, exact=False):
    S, H, T, qs = att.S, att.H, att.T, att.qs
    it, jt, npairs = _pairs(att.nb, by_key=False)
    nk = len(att.ks)

    def body(it_ref, jt_ref, *refs):
        q_ref = refs[0]
        k_refs = refs[1:1 + nk]
        v_ref = refs[1 + nk]
        n = 2 + nk
        cum_ref = None
        if att.has_bias:
            cum_ref = refs[n]
            n += 1
        o_ref = refs[n]
        n += 1
        ox_ref = None
        if exact:
            ox_ref = refs[n]
            n += 1
        lse_ref, m_ref, l_ref, acc_ref = refs[n:n + 4]
        lo_ref = refs[n + 4] if exact else None
        p = pl.program_id(1)
        i, j = it_ref[p], jt_ref[p]

        @pl.when(j == 0)
        def _():
            m_ref[...] = jnp.full_like(m_ref, -jnp.inf)
            l_ref[...] = jnp.zeros_like(l_ref)
            acc_ref[...] = jnp.zeros_like(acc_ref)
            if exact:
                lo_ref[...] = jnp.zeros_like(lo_ref)

        def step(masked):
            k = _join(k_refs)
            v = v_ref[...]
            subs = att.sub_blocks(masked)

            def logits(idx):
                q0, nkeys = subs[idx]
                cum = cum_ref[0:nkeys, :] if att.has_bias else None
                return att.scores(k[0:nkeys], q_ref[q0:q0 + qs, :], cum, q0, masked)

            ahead = logits(0)
            for idx, (q0, nkeys) in enumerate(subs):
                qsl = slice(q0, q0 + qs)
                s, mask = ahead
                if idx + 1 < len(subs):
                    ahead = logits(idx + 1)
                if masked:
                    s = jnp.where(mask, s, -jnp.inf)
                m_prev = m_ref[:, qsl]
                m_new = jnp.maximum(m_prev, jnp.max(s, axis=0, keepdims=True))
                alpha = jnp.exp2(m_prev - m_new)
                pr = jnp.exp2(s - m_new)
                l_ref[:, qsl] = alpha * l_ref[:, qsl] + jnp.sum(pr, axis=0, keepdims=True)
                p_hi = pr.astype(BF16)
                acc_ref[:, qsl] = alpha * acc_ref[:, qsl] + lax.dot_general(
                    v[0:nkeys], p_hi, _TN, preferred_element_type=F32)
                if exact:
                    p_lo = (pr - p_hi.astype(F32)).astype(BF16)
                    lo_ref[:, qsl] = alpha * lo_ref[:, qsl] + lax.dot_general(
                        v[0:nkeys], p_lo, _TN, preferred_element_type=F32)
                m_ref[:, qsl] = m_new

        @pl.when(j < i)
        def _():
            step(False)

        @pl.when(j == i)
        def _():
            step(True)
            l = l_ref[...]
            inv = 1.0 / l
            o_ref[...] = jnp.transpose(acc_ref[...] * inv).astype(o_ref.dtype)
            if exact:
                ox_ref[...] = jnp.transpose((acc_ref[...] + lo_ref[...]) * inv)
            lse_ref[...] = m_ref[...] + jnp.log2(l)

    in_specs = [att.q_spec(att.q)] + [att.k_spec(k) for k in att.ks] + [att.k_spec(att.v)]
    args = [att.q[0]] + [k[0] for k in att.ks] + [att.v[0]]
    if att.has_bias:
        in_specs.append(att.cum_k())
        args.append(att.cum_rep)
    o_spec = pl.BlockSpec((T, att.dv), lambda h, p, it, jt: (it[p], h))
    out_specs = [o_spec]
    out_shape = [jax.ShapeDtypeStruct((S, H * att.dv), BF16)]
    scratch = [pltpu.VMEM((1, T), F32), pltpu.VMEM((1, T), F32), pltpu.VMEM((att.dv, T), F32)]
    if exact:
        out_specs.append(o_spec)
        out_shape.append(jax.ShapeDtypeStruct((S, H * att.dv), F32))
        scratch.append(pltpu.VMEM((att.dv, T), F32))
    out_specs.append(att.row_q())
    out_shape.append(jax.ShapeDtypeStruct((H, 1, S), F32))
    return pl.pallas_call(
        body, name=name,
        grid_spec=pltpu.PrefetchScalarGridSpec(
            num_scalar_prefetch=2, grid=(H, npairs), in_specs=in_specs, out_specs=out_specs,
            scratch_shapes=scratch),
        out_shape=out_shape,
        compiler_params=_params(("parallel", "arbitrary")),
    )(it, jt, *args)


def _att_delta_t(do, o, n_heads, name, order=None):
    S = do.shape[0]
    w = do.shape[1] // n_heads
    ts = _tile(S, ATT_T)
    ones = jnp.ones((8, w), BF16)
    extra = [] if order is None else [order]

    def body(do_ref, o_ref, ones_ref, *rest):
        d_ref = rest[-1]
        prod = do_ref[...].astype(F32) * o_ref[...].astype(F32)
        acc = jnp.zeros((8, ts), F32)
        for part in _split3(prod):
            acc = acc + lax.dot_general(ones_ref[...], part, _NT, preferred_element_type=F32)
        d_ref[...] = acc[0:1, :]

    blk = pl.BlockSpec((ts, w), lambda i, h: (i, h))
    return pl.pallas_call(
        body, name=name, grid=(S // ts, n_heads),
        in_specs=[blk, blk, pl.BlockSpec((8, w), lambda i, h: (0, 0))] + [_ANY_SPEC] * len(extra),
        out_specs=pl.BlockSpec((None, 1, ts), lambda i, h: (h, 0, i)),
        out_shape=jax.ShapeDtypeStruct((n_heads, 1, S), F32),
        compiler_params=_params(("parallel", "parallel")),
    )(do, o, ones, *extra)


def _att_bwd_t(att, do, lse, delta, dq_dtype, dk_dtypes, name, dq_rope=None):
    S, H, T, qs = att.S, att.H, att.T, att.qs
    it, jt, npairs = _pairs(att.nb, by_key=True)
    nk = len(att.ks)
    last = att.nb - 1
    widths = [k[1] for k in att.ks]

    def body(it_ref, jt_ref, *refs):
        q_ref = refs[0]
        k_refs = refs[1:1 + nk]
        v_ref, do_ref, lse_ref, dl_ref = refs[1 + nk:5 + nk]
        n = 5 + nk
        cum_ref = None
        if att.has_bias:
            cum_ref = refs[n]
            n += 1
        rope_refs = None
        if dq_rope is not None:
            rope_refs = refs[n:n + 3]
            n += 3
        dq_ref = refs[n]
        dk_refs = refs[n + 1:n + 1 + nk]
        dv_ref = refs[n + 1 + nk]
        n += nk + 2
        dc_ref = None
        if att.has_bias:
            dc_ref = refs[n]
            n += 1
        dq_acc, dk_acc, dv_acc = refs[n:n + 3]
        dc_acc = refs[n + 3] if att.has_bias else None
        p = pl.program_id(1)
        i, j = it_ref[p], jt_ref[p]

        @pl.when(p == 0)
        def _():
            dq_acc[...] = jnp.zeros_like(dq_acc)

        @pl.when(i == j)
        def _():
            dk_acc[...] = jnp.zeros_like(dk_acc)
            dv_acc[...] = jnp.zeros_like(dv_acc)
            if att.has_bias:
                dc_acc[...] = jnp.zeros_like(dc_acc)

        def step(masked):
            k = _join(k_refs)
            v = v_ref[...]
            subs = att.sub_blocks(masked)

            def logits(idx):
                q0, nkeys = subs[idx]
                cum = cum_ref[0:nkeys, :] if att.has_bias else None
                return att.scores(k[0:nkeys], q_ref[q0:q0 + qs, :], cum, q0, masked)

            ahead = logits(0)
            for idx, (q0, nkeys) in enumerate(subs):
                qsl = slice(q0, q0 + qs)
                ksl = slice(0, nkeys)
                q_sub = q_ref[qsl, :]
                do_sub = do_ref[qsl, :]
                s, mask = ahead
                if idx + 1 < len(subs):
                    ahead = logits(idx + 1)
                pr = jnp.exp2(s - lse_ref[:, qsl])
                if masked:
                    pr = jnp.where(mask, pr, 0.0)
                dp = lax.dot_general(v[ksl], do_sub, _NT, preferred_element_type=F32)
                ds = pr * (dp - dl_ref[:, qsl])
                ds_b = ds.astype(BF16)
                dv_acc[ksl, :] += jnp.dot(pr.astype(BF16), do_sub, preferred_element_type=F32)
                dk_acc[ksl, :] += jnp.dot(ds_b, q_sub, preferred_element_type=F32)
                dq_acc[i, :, qsl] += lax.dot_general(k[ksl], ds_b, _TN, preferred_element_type=F32)
                if att.has_bias:
                    part = ds[:, 0:LANE] if qs >= LANE else ds
                    for c0 in range(LANE, qs, LANE):
                        part = part + ds[:, c0:c0 + LANE]
                    dc_acc[ksl, :] += part

        @pl.when(i > j)
        def _():
            step(False)

        @pl.when(i == j)
        def _():
            step(True)
            dq = jnp.transpose(dq_acc[i] * att.scale)
            if dq_rope is not None:
                dq = _rope(dq, rope_refs[0][...], rope_refs[1][...], rope_refs[2][...], -1)
            dq_ref[...] = dq.astype(dq_ref.dtype)

        @pl.when(i == last)
        def _():
            dk = dk_acc[...] * (1.0 / LOG2E)
            off = 0
            for r, w in zip(dk_refs, widths):
                r[...] = dk[:, off:off + w].astype(r.dtype)
                off += w
            dv_ref[...] = dv_acc[...].astype(dv_ref.dtype)
            if att.has_bias:
                dc_ref[...] = -jnp.sum(dc_acc[...], axis=-1, keepdims=True)

    do_op = (do, att.dv, 0, True)
    in_specs = ([att.q_spec(att.q)] + [att.k_spec(k) for k in att.ks]
                + [att.k_spec(att.v), att.q_spec(do_op), att.row_q(), att.row_q()])
    args = [att.q[0]] + [k[0] for k in att.ks] + [att.v[0], do, lse, delta]
    if att.has_bias:
        in_specs.append(att.cum_k())
        args.append(att.cum_rep)
    if dq_rope is not None:
        in_specs += [pl.BlockSpec((T, att.dq), lambda h, p, it, jt: (jt[p], 0))] * 3
        args += list(dq_rope)
    out_specs = [pl.BlockSpec((T, att.dq), lambda h, p, it, jt: (jt[p], h))]
    out_shape = [jax.ShapeDtypeStruct((S, H * att.dq), dq_dtype)]
    out_specs += [pl.BlockSpec((T, w), lambda h, p, it, jt: (jt[p], h)) for w in widths]
    out_shape += [jax.ShapeDtypeStruct((S, H * w), dt) for w, dt in zip(widths, dk_dtypes)]
    out_specs.append(pl.BlockSpec((T, att.dv), lambda h, p, it, jt: (jt[p], h)))
    out_shape.append(jax.ShapeDtypeStruct((S, H * att.dv), BF16))
    scratch = [pltpu.VMEM((att.nb, att.dq, T), F32), pltpu.VMEM((T, att.dq), F32), pltpu.VMEM((T, att.dv), F32)]
    if att.has_bias:
        out_specs.append(pl.BlockSpec((None, T, 1), lambda h, p, it, jt: (h, jt[p], 0)))
        out_shape.append(jax.ShapeDtypeStruct((H, S, 1), F32))
        scratch.append(pltpu.VMEM((T, min(qs, LANE)), F32))
    return pl.pallas_call(
        body, name=name,
        grid_spec=pltpu.PrefetchScalarGridSpec(
            num_scalar_prefetch=2, grid=(H, npairs), in_specs=in_specs, out_specs=out_specs,
            scratch_shapes=scratch),
        out_shape=out_shape,
        compiler_params=_params(("parallel", "arbitrary")),
    )(it, jt, *args)


def _adamw(w, g1, g2, m, v, name, g_row=None):
    _, K, N = w.shape
    by_rows = K % 8 == 0
    tr = _tile(K, 256, 8) if by_rows else K
    if g_row is None:
        assert g1.shape == (K, N) and g2.shape == (K, N), name
        g_row = 0
    assert by_rows and g_row % tr == 0 or g_row == 0, name
    g_blk = g_row // tr
    tc = N if by_rows else _tile(N, LANE)
    c1 = 1.0 - ADAM_B1 ** ADAM_STEP
    c2 = 1.0 - ADAM_B2 ** ADAM_STEP

    def body(w_ref, g1_ref, g2_ref, m_ref, v_ref, g_ref, d_ref, nm_ref, nv_ref):
        gv = g1_ref[...] + g2_ref[...]
        nm = ADAM_B1 * m_ref[...] + (1.0 - ADAM_B1) * gv
        nv = ADAM_B2 * v_ref[...] + (1.0 - ADAM_B2) * (gv * gv)
        g_ref[...] = gv
        d_ref[...] = -ADAM_LR * ((nm / c1) / (jnp.sqrt(nv / c2) + ADAM_EPS) + ADAM_WD * w_ref[...])
        nm_ref[...] = nm
        nv_ref[...] = nv

    if by_rows:
        blk = pl.BlockSpec((None, tr, N), lambda i: (0, i, 0))
        gblk = pl.BlockSpec((tr, N), lambda i: (g_blk + i, 0))
    else:
        blk = pl.BlockSpec((None, K, tc), lambda i: (0, 0, i))
        gblk = pl.BlockSpec((K, tc), lambda i: (0, i))
    return pl.pallas_call(
        body, name=name, grid=(K // tr if by_rows else N // tc,),
        in_specs=[blk, gblk, gblk, blk, blk], out_specs=[blk] * 4,
        out_shape=[jax.ShapeDtypeStruct((1, K, N), F32)] * 4,
        compiler_params=_params(("parallel",)),
    )(w, g1, g2, m, v)


_HBM_SPEC = pl.BlockSpec(memory_space=pltpu.HBM)
_SEM_SPEC = pl.BlockSpec(memory_space=pltpu.SEMAPHORE)
_VMEM_SPEC = pl.BlockSpec(memory_space=pltpu.VMEM)
_EFFECT = pltpu.SideEffectType.DATAFLOW_SIDE_EFFECTING


def _place():
    return lax.axis_index("x"), lax.axis_index("y"), lax.axis_index("c")


def _other_chips(x, y):
    return [(1 - x, y), (x, 1 - y), (1 - x, 1 - y)]


def _all_gather_halves(wp, name):
    R, C = wp.shape
    half = R // 2
    assert half % 16 == 0

    def body(w_ref, out_ref, ici_send, ici_recv, d2d_send, d2d_recv, local_sem):
        x, y, c = _place()
        me = 2 * x + y
        chips = _other_chips(x, y)
        mine = pl.ds(pl.multiple_of(c * half, 16), half)
        theirs = pl.ds(pl.multiple_of((1 - c) * half, 16), half)
        local = pltpu.make_async_copy(w_ref, out_ref.at[me], local_sem)
        local.start()
        sends = []
        for n, (px, py) in enumerate(chips):
            cp = pltpu.make_async_remote_copy(
                src_ref=w_ref.at[mine], dst_ref=out_ref.at[me, mine], send_sem=ici_send.at[n],
                recv_sem=ici_recv.at[n], device_id=(px, py, c), device_id_type=MESH)
            cp.start()
            sends.append(cp)
        for n, (px, py) in enumerate(chips):
            slot = 2 * px + py
            pltpu.make_async_remote_copy(
                src_ref=w_ref.at[mine], dst_ref=out_ref.at[slot, mine], send_sem=ici_send.at[n],
                recv_sem=ici_recv.at[n], device_id=(px, py, c), device_id_type=MESH).wait_recv()
            cp = pltpu.make_async_remote_copy(
                src_ref=out_ref.at[slot, mine], dst_ref=out_ref.at[slot, mine], send_sem=d2d_send.at[n],
                recv_sem=d2d_recv.at[n], device_id=(x, y, 1 - c), device_id_type=MESH)
            cp.start()
            sends.append(cp)
        for n, (px, py) in enumerate(chips):
            slot = 2 * px + py
            pltpu.make_async_remote_copy(
                src_ref=out_ref.at[slot, theirs], dst_ref=out_ref.at[slot, theirs], send_sem=d2d_send.at[n],
                recv_sem=d2d_recv.at[n], device_id=(x, y, 1 - c), device_id_type=MESH).wait_recv()
        for cp in sends:
            cp.wait_send()
        local.wait()

    return pl.pallas_call(
        body, name=name,
        in_specs=[_ANY_SPEC], out_specs=_ANY_SPEC,
        out_shape=jax.ShapeDtypeStruct((N_CHIPS, R, C), wp.dtype),
        scratch_shapes=[pltpu.SemaphoreType.DMA((3,)), pltpu.SemaphoreType.DMA((3,)), pltpu.SemaphoreType.DMA((3,)),
                        pltpu.SemaphoreType.DMA((3,)), pltpu.SemaphoreType.DMA],
    )(wp)


def _chip_copies(src_ref, land_ref, sems, gather):
    x, y, c = _place()
    me = 2 * x + y
    out, back = [], []
    for n, (px, py) in enumerate(_other_chips(x, y)):
        src = src_ref if gather else src_ref.at[2 * px + py]
        out.append(pltpu.make_async_remote_copy(
            src_ref=src, dst_ref=land_ref.at[me] if gather else land_ref.at[n],
            send_sem=sems[n], recv_sem=sems[3 + n], device_id=(px, py, c), device_id_type=MESH))
        back.append(pltpu.make_async_remote_copy(
            src_ref=src, dst_ref=land_ref.at[2 * px + py] if gather else land_ref.at[n],
            send_sem=sems[n], recv_sem=sems[3 + n], device_id=(px, py, c), device_id_type=MESH))
    return out, back


def _xchg_start(src, land, gather, order, name):
    def body(src_ref, land_ref, order_ref, *outs):
        sems = outs[0:6]
        token = outs[8]
        out, _ = _chip_copies(src_ref, land_ref, sems, gather)
        for cp in out:
            cp.start()
        token[...] = jnp.zeros_like(token)

    outs = pl.pallas_call(
        body, name=name,
        out_shape=(pltpu.SemaphoreType.DMA(()),) * 6 + (
            pltpu.HBM(src.shape, src.dtype), pltpu.HBM(land.shape, land.dtype),
            jax.ShapeDtypeStruct((8, LANE), F32)),
        in_specs=(_HBM_SPEC, _HBM_SPEC, _ANY_SPEC),
        out_specs=(_SEM_SPEC,) * 6 + (_HBM_SPEC, _HBM_SPEC, _VMEM_SPEC),
        input_output_aliases={0: 6, 1: 7},
        compiler_params=pltpu.CompilerParams(has_side_effects=_EFFECT),
    )(pltpu.with_memory_space_constraint(src, pltpu.HBM), pltpu.with_memory_space_constraint(land, pltpu.HBM), order)
    return outs[0:6], outs[6], outs[7], outs[8]


def _xchg_wait(started, gather, after, name):
    sems, src, land, _ = started

    def body(src_ref, land_ref, *rest):
        _, back = _chip_copies(src_ref, land_ref, rest[0:6], gather)
        for cp in back:
            cp.wait_send()
            cp.wait_recv()

    return pl.pallas_call(
        body, name=name,
        out_shape=(pltpu.HBM(src.shape, src.dtype), pltpu.HBM(land.shape, land.dtype)),
        in_specs=(_HBM_SPEC, _HBM_SPEC) + (_SEM_SPEC,) * 6 + (_ANY_SPEC,),
        out_specs=(_HBM_SPEC, _HBM_SPEC),
        input_output_aliases={0: 0, 1: 1},
        compiler_params=pltpu.CompilerParams(has_side_effects=_EFFECT),
    )(src, land, *sems, after)


def _sib_copy(src_ref, land_ref, send_sem, recv_sem):
    x, y, c = _place()
    return pltpu.make_async_remote_copy(src_ref=src_ref, dst_ref=land_ref, send_sem=send_sem, recv_sem=recv_sem,
                                        device_id=(x, y, 1 - c), device_id_type=MESH)


def _sib_start(src, name):
    land = lax.empty(src.shape, src.dtype)

    def body(src_ref, land_ref, send_sem, recv_sem, src_thru, land_thru, token):
        _sib_copy(src_ref, land_ref, send_sem, recv_sem).start()
        token[...] = jnp.zeros_like(token)

    return pl.pallas_call(
        body, name=name,
        out_shape=(pltpu.SemaphoreType.DMA(()), pltpu.SemaphoreType.DMA(()),
                   pltpu.HBM(src.shape, src.dtype), pltpu.HBM(land.shape, land.dtype),
                   jax.ShapeDtypeStruct((8, LANE), F32)),
        in_specs=(_HBM_SPEC, _HBM_SPEC),
        out_specs=(_SEM_SPEC, _SEM_SPEC, _HBM_SPEC, _HBM_SPEC, _VMEM_SPEC),
        input_output_aliases={0: 2, 1: 3},
        compiler_params=pltpu.CompilerParams(has_side_effects=_EFFECT),
    )(pltpu.with_memory_space_constraint(src, pltpu.HBM), pltpu.with_memory_space_constraint(land, pltpu.HBM))


def _sib_wait(started, after, name):
    send_sem, recv_sem, src, land, _ = started

    def body(src_ref, land_ref, send_sem, recv_sem, after_ref, src_out, land_out):
        cp = _sib_copy(src_ref, land_ref, send_sem, recv_sem)
        cp.wait_send()
        cp.wait_recv()

    return pl.pallas_call(
        body, name=name,
        out_shape=(pltpu.HBM(src.shape, src.dtype), pltpu.HBM(land.shape, land.dtype)),
        in_specs=(_HBM_SPEC, _HBM_SPEC, _SEM_SPEC, _SEM_SPEC, _ANY_SPEC),
        out_specs=(_HBM_SPEC, _HBM_SPEC),
        input_output_aliases={0: 0, 1: 1},
        compiler_params=pltpu.CompilerParams(has_side_effects=_EFFECT),
    )(src, land, send_sem, recv_sem, after)


def _sum_slabs(gp, recv, chip, name):
    _, R, C = gp.shape
    tr = _tile(R, PACK_ROWS, 16)

    def body(chip_ref, own_ref, r0_ref, r1_ref, r2_ref, o_ref):
        acc = own_ref[...].astype(F32) + r0_ref[...].astype(F32)
        o_ref[...] = (acc + r1_ref[...].astype(F32)) + r2_ref[...].astype(F32)

    def got(n):
        return pl.BlockSpec((None, tr, C), lambda i, chip_ref: (n, i, 0))

    return pl.pallas_call(
        body, name=name,
        grid_spec=pltpu.PrefetchScalarGridSpec(
            num_scalar_prefetch=1, grid=(R // tr,),
            in_specs=[pl.BlockSpec((None, tr, C), lambda i, chip_ref: (chip_ref[0], i, 0)), got(0), got(1), got(2)],
            out_specs=pl.BlockSpec((tr, C), lambda i, chip_ref: (i, 0))),
        out_shape=jax.ShapeDtypeStruct((R, C), F32),
        compiler_params=_params(("parallel",)),
    )(jnp.reshape(chip, (1,)).astype(jnp.int32), gp, recv, recv, recv)


def _all_reduce_vec(vec, name):
    VR, W = vec.shape

    def body(vec_ref, vall_ref, vout_ref, vsend_sems, vrecv_sems):
        x, y, c = _place()
        vall_ref[4 * x + 2 * y + c] = vec_ref[...]
        sends = []
        peers = []
        for r in range(1, N_DEV):
            dx, dy, dc = (r >> 2) & 1, (r >> 1) & 1, r & 1
            peer = (x ^ dx, y ^ dy, c ^ dc)
            peers.append(peer)
            cp = pltpu.make_async_remote_copy(
                src_ref=vec_ref, dst_ref=vall_ref.at[4 * x + 2 * y + c], send_sem=vsend_sems.at[r - 1],
                recv_sem=vrecv_sems.at[r - 1], device_id=peer, device_id_type=MESH)
            cp.start()
            sends.append(cp)
        for r, peer in enumerate(peers):
            pltpu.make_async_remote_copy(
                src_ref=vec_ref, dst_ref=vall_ref.at[4 * peer[0] + 2 * peer[1] + peer[2]],
                send_sem=vsend_sems.at[r], recv_sem=vrecv_sems.at[r],
                device_id=peer, device_id_type=MESH).wait_recv()
        total = vall_ref[0]
        for d in range(1, N_DEV):
            total = total + vall_ref[d]
        vout_ref[...] = total
        for cp in sends:
            cp.wait_send()

    outs = pl.pallas_call(
        body, name=name,
        in_specs=[_VMEM_SPEC], out_specs=[_VMEM_SPEC, _VMEM_SPEC],
        out_shape=[jax.ShapeDtypeStruct((N_DEV, VR, W), F32), jax.ShapeDtypeStruct((VR, W), F32)],
        scratch_shapes=[pltpu.SemaphoreType.DMA((N_DEV - 1,)), pltpu.SemaphoreType.DMA((N_DEV - 1,))],
    )(vec)
    return outs[1]


class _Pack:
    def __init__(self, group, C):
        self.group, self.C = group, C
        self.rows, self.offs, off = {}, {}, 0
        for nm, (K, N), _ in group:
            assert N <= C, nm
            self.rows[nm] = K if 2 * N > C else -(-(K * N) // C)
            self.offs[nm] = off
            off += -(-self.rows[nm] // 16) * 16
        self.used = off
        self.R = -(-off // PACK_ROWS) * PACK_ROWS

    def _rows_of(self, a):
        K, N = a.shape
        if 2 * N > self.C:
            a = jnp.pad(a, ((0, 0), (0, self.C - N)))
        else:
            a = jnp.pad(a.reshape(-1), (0, -(K * N) % self.C)).reshape(-1, self.C)
        return jnp.pad(a, ((0, -a.shape[0] % 16), (0, 0)))

    def pack(self, shards):
        parts = [self._rows_of(shards[nm].astype(BF16)) for nm, _, _ in self.group]
        return jnp.concatenate(parts + [jnp.zeros((self.R - self.used, self.C), BF16)], axis=0)

    def _shard_of(self, rows, shape):
        K, N = shape
        return rows[:, :N] if 2 * N > self.C else rows.reshape(-1)[:K * N].reshape(K, N)

    def part(self, flat, nm, shape):
        return self._shard_of(flat[self.offs[nm]:self.offs[nm] + self.rows[nm]], shape)

    def slab_rows(self, nm, g):
        (K, N), axis = next((shape, axis) for n, shape, axis in self.group if n == nm)
        cuts = [g[:, k * N:(k + 1) * N] if axis == 1 else g[k * K:(k + 1) * K, :] for k in range(N_CHIPS)]
        return jnp.stack([self._rows_of(c.astype(BF16)) for c in cuts])

    def slabs(self, grads):
        parts = [self.slab_rows(nm, grads[nm]) for nm, _, _ in self.group]
        return jnp.concatenate(parts + [jnp.zeros((N_CHIPS, self.R - self.used, self.C), BF16)], axis=1)

    def full(self, gathered, names=None):
        res = {}
        for nm, (K, N), axis in self.group:
            if names is None or nm in names:
                rows = gathered[:, self.offs[nm]:self.offs[nm] + self.rows[nm]]
                res[nm] = jnp.concatenate([self._shard_of(rows[k], (K, N)) for k in range(N_CHIPS)], axis=axis)
        return res


def _rope_tables(S):
    pos = jnp.arange(S, dtype=F32)
    inv = 1.0 / (ROPE_THETA ** (jnp.arange(0, MLA_ROPE, 2, dtype=F32) / MLA_ROPE))
    ang = pos[:, None] * inv[None, :]
    cos, sin = jnp.cos(ang), jnp.sin(ang)
    half = MLA_ROPE // 2
    z = jnp.zeros((S, half), F32)
    one = jnp.ones((S, LANE - MLA_ROPE), F32)
    zero = jnp.zeros((S, LANE - MLA_ROPE), F32)
    kc = jnp.concatenate([cos, cos, one], axis=1)
    ksa = jnp.concatenate([-sin, z, zero], axis=1)
    ksb = jnp.concatenate([z, sin, zero], axis=1)
    qc = jnp.concatenate([jnp.ones((S, MLA_NOPE), F32), kc], axis=1)
    qsa = jnp.concatenate([jnp.zeros((S, MLA_NOPE), F32), ksa], axis=1)
    qsb = jnp.concatenate([jnp.zeros((S, MLA_NOPE), F32), ksb], axis=1)
    return (kc, ksa, ksb), (qc, qsa, qsb)


def _pad_cols(a, width):
    return jnp.pad(a, ((0, 0), (0, width - a.shape[1])))


def kernel(x, attn_norm, w_in, fox_f_bias, q_norm, w_uq, kv_norm, w_ukv, w_mla_branch, w_fox_branch, w_out, mlp_norm, w_up, w_down, final_norm, loss_target, m_attn_norm, m_w_in, m_fox_f_bias, m_q_norm, m_w_uq, m_kv_norm, m_w_ukv, m_w_mla_branch, m_w_fox_branch, m_w_out, m_mlp_norm, m_w_up, m_w_down, m_final_norm, v_attn_norm, v_w_in, v_fox_f_bias, v_q_norm, v_w_uq, v_kv_norm, v_w_ukv, v_w_mla_branch, v_w_fox_branch, v_w_out, v_mlp_norm, v_w_up, v_w_down, v_final_norm):
    _, S, D = x.shape
    H, HF = MLA_HEADS, FOX_HEADS
    QL, KVL = MLA_Q_LORA, MLA_KV_LORA
    assert H == HF and H <= 8
    xs = x[0]
    target = loss_target[0]
    C = D
    chip = 2 * lax.axis_index("x") + lax.axis_index("y")

    def flip(a):
        return jnp.transpose(a, (0, 2, 1))

    w_in, m_w_in, v_w_in = flip(w_in), flip(m_w_in), flip(v_w_in)
    weights = {"attn_norm": attn_norm, "w_in": w_in, "fox_f_bias": fox_f_bias, "q_norm": q_norm, "w_uq": w_uq,
               "kv_norm": kv_norm, "w_ukv": w_ukv, "w_mla_branch": w_mla_branch, "w_fox_branch": w_fox_branch,
               "w_out": w_out, "mlp_norm": mlp_norm, "w_up": w_up, "w_down": w_down, "final_norm": final_norm}
    moments = {"attn_norm": (m_attn_norm, v_attn_norm), "w_in": (m_w_in, v_w_in), "fox_f_bias": (m_fox_f_bias, v_fox_f_bias),
               "q_norm": (m_q_norm, v_q_norm), "w_uq": (m_w_uq, v_w_uq), "kv_norm": (m_kv_norm, v_kv_norm),
               "w_ukv": (m_w_ukv, v_w_ukv), "w_mla_branch": (m_w_mla_branch, v_w_mla_branch),
               "w_fox_branch": (m_w_fox_branch, v_w_fox_branch), "w_out": (m_w_out, v_w_out),
               "mlp_norm": (m_mlp_norm, v_mlp_norm), "w_up": (m_w_up, v_w_up), "w_down": (m_w_down, v_w_down),
               "final_norm": (m_final_norm, v_final_norm)}

    def group(names_axes):
        return [(nm, weights[nm].shape[1:], axis) for nm, axis in names_axes]

    pack_a = _Pack(group([("w_in", 0), ("w_uq", 1), ("w_ukv", 1)]), C)
    pack_b = _Pack(group([("w_down", 0), ("w_up", 1), ("w_out", 0), ("w_mla_branch", 1), ("w_fox_branch", 1)]), C)
    RA, RB = pack_a.R, pack_b.R
    wp_a = pack_a.pack({nm: weights[nm][0] for nm, _, _ in pack_a.group})
    wp_b = pack_b.pack({nm: weights[nm][0] for nm, _, _ in pack_b.group})
    n_in = w_in.shape[1]
    rows_in = -(-n_in // 16) * 16
    assert pack_a.offs["w_in"] == 0 and all((k * n_in) % 16 + n_in <= rows_in for k in range(N_CHIPS))
    shifted = lax.dynamic_update_slice(jnp.zeros((rows_in, C), BF16), wp_a[:n_in], ((chip * n_in) % 16, 0))
    wp_a = jnp.concatenate([shifted, wp_a[rows_in:]], axis=0)
    gathered_a = _all_gather_halves(wp_a, "all_gather_a")
    ag_b = _xchg_start(wp_b, lax.empty((N_CHIPS, RB, C), BF16), True, gathered_a, "all_gather_start_b")
    xn = _norm_fwd(xs, attn_norm, "attn_norm_fwd", order=ag_b[3])
    full = pack_a.full(gathered_a, ("w_uq", "w_ukv"))
    tile0 = [(k * n_in) // 16 * 16 for k in range(N_CHIPS)]
    total = tile0[-1] + rows_in
    full["w_in"] = sum(jnp.pad(gathered_a[k, :rows_in], ((tile0[k], total - tile0[k] - rows_in), (0, 0)))
                       for k in range(N_CHIPS))

    o_ckv = QL
    o_kr = o_ckv + KVL
    o_fq = o_kr + MLA_ROPE
    o_ff = o_fq + 3 * HF * FOX_HEAD_DIM
    o_g = o_ff + HF
    wi = full["w_in"]
    assert N_CHIPS * n_in == o_g + 2 * D and wi.shape[0] >= o_g + 2 * D
    WS = QL + KVL + 2 * LANE
    NQKV = 3 * HF * FOX_HEAD_DIM

    def pad_rows(a, rows):
        return jnp.pad(a, ((0, rows - a.shape[0]), (0, 0)))

    w_small = jnp.concatenate([wi[:o_kr], pad_rows(wi[o_kr:o_fq], LANE), pad_rows(wi[o_ff:o_g], LANE)], axis=0)
    w_qkv = wi[o_fq:o_ff]
    w_g = wi[o_g:o_g + 2 * D]
    w_pack = jnp.concatenate([w_small, w_qkv, w_g], axis=0)
    dqk = MLA_NOPE + MLA_ROPE
    w_uq_p = jnp.pad(full["w_uq"].reshape(QL, H, dqk), ((0, 0), (0, 0), (0, QPAD - dqk))).reshape(QL, H * QPAD)
    ukv = full["w_ukv"].reshape(KVL, H, MLA_NOPE + MLA_V)
    w_ukv_p = jnp.concatenate([ukv[:, :, :MLA_NOPE].reshape(KVL, H * MLA_NOPE),
                               ukv[:, :, MLA_NOPE:].reshape(KVL, H * MLA_V)], axis=1)

    (kc, ksa, ksb), (qc, qsa, qsb) = _rope_tables(S)
    bias_pad = _pad_cols(fox_f_bias, LANE)

    small = _matmul(xn, w_small, "nt", [F32], "proj_small")
    n_fq = HF * FOX_HEAD_DIM
    q_scale = jnp.concatenate([jnp.full((1, n_fq), LOG2E / math.sqrt(FOX_HEAD_DIM), F32),
                               jnp.ones((1, NQKV - n_fq), F32)], axis=1)
    qkv = _matmul(xn, w_qkv, "nt", [BF16], "proj_qkv", col_extras=(q_scale,), epilogue=lambda acc, cs: (acc * cs,))
    gpre = _matmul(xn, w_g, "nt", [F32], "proj_gates")
    cqn, ckvn, kr, cum = _prep_fwd(small, q_norm, kv_norm, bias_pad, kc, ksa, ksb, HF, "prep_fwd")
    c2_mla = LOG2E / math.sqrt(dqk)
    q_rot = _matmul(cqn, w_uq_p, "nn", [BF16], "mla_q_up", tn=QPAD, row_extras=(qc * c2_mla, qsa * c2_mla, qsb * c2_mla),
                    epilogue=lambda acc, c, sa, sb: (_rope(acc, c, sa, sb, 1),))
    kv2 = _matmul(ckvn, w_ukv_p, "nn", [BF16], "mla_kv_up")

    mla = _AttT(S, H, (q_rot, QPAD, 0, True), [(kv2, MLA_NOPE, 0, True), (kr, LANE, 0, False)],
                (kv2, MLA_V, H, True), 1.0 / math.sqrt(dqk), True)
    o_mla, lse_mla = _att_fwd_t(mla, "mla_att_fwd")

    cum_t = jnp.transpose(cum[:, :HF]) * LOG2E
    cum_rep = jnp.broadcast_to(cum_t[:, :, None], (HF, S, min(QSUB, _tile(S, ATT_T))))
    fox = _AttT(S, HF, (qkv, FOX_HEAD_DIM, 0, True), [(qkv, FOX_HEAD_DIM, HF, True)],
                (qkv, FOX_HEAD_DIM, 2 * HF, True), 1.0 / math.sqrt(FOX_HEAD_DIM), False, cum_rep)
    o_fox, ox_fox, lse_fox = _att_fwd_t(fox, "fox_att_fwd", exact=True)

    own_b, land_b = _xchg_wait(ag_b, True, lse_fox, "all_gather_wait_b")
    gathered_b = lax.dynamic_update_slice(land_b, own_b[None], (chip, 0, 0))
    full.update(pack_b.full(gathered_b, ("w_mla_branch", "w_fox_branch", "w_out")))
    w_mb, w_fb, w_o = (full[n] for n in ("w_mla_branch", "w_fox_branch", "w_out"))

    def b_of(nm, mode, tn, tk):
        (K, N), axis = next((shape, axis) for n, shape, axis in pack_b.group if n == nm)
        off = pack_b.offs[nm]
        shape = (N_CHIPS * K, N) if axis == 0 else (K, N_CHIPS * N)
        t_r, t_c = (tk, tn) if mode == "nn" else (tn, tk)
        t_r, t_c = _tile(shape[0], t_r), _tile(shape[1], t_c)
        if not (N == C and K % t_r == 0 and N % t_c == 0 and off % t_r == 0):
            return pack_b.full(gathered_b, (nm,))[nm], None
        base = off // t_r
        if axis == 0:
            per = K // t_r
            place = lambda rb, cb: (rb // per, base + rb % per, cb)
        else:
            per = N // t_c
            place = lambda rb, cb: (cb // per, base + rb, cb % per)
        return gathered_b, (shape, (lambda j, k: place(k, j)) if mode == "nn" else (lambda j, k: place(j, k)))

    y_mla = _matmul(o_mla, w_mb, "nn", [F32], "mla_branch")

    def gate_merge(acc, ga, gb, ya):
        return acc, _sigmoid(ga) * ya + _sigmoid(gb) * acc

    y_fox, merged = _matmul(o_fox, w_fb, "nn", [F32, BF16], "fox_branch_gates", extras=((gpre, 0), (gpre, 1), y_mla),
                            epilogue=gate_merge)
    h1 = _matmul(merged, w_o, "nn", [F32], "out_proj", extras=(xs,), epilogue=lambda acc, r: (acc + r,))
    hn = _norm_fwd(h1, mlp_norm, "mlp_norm_fwd")

    def relu2(acc):
        a = jnp.maximum(acc, 0.0)
        return a * a, a

    w_u, w_u_in = b_of("w_up", "nn", 512, 2048)
    u, a_pos = _matmul(hn, w_u, "nn", [BF16, BF16], "mlp_up", epilogue=relu2, b_in=w_u_in)
    w_d, w_d_in = b_of("w_down", "nn", 1024, 2048)
    h2 = _matmul(u, w_d, "nn", [F32], "mlp_down", tn=1024, extras=(h1,), epilogue=lambda acc, r: (acc + r,),
                 b_in=w_d_in)
    dh2, dh2_b, g_final, loss_part = _final(h2, final_norm.reshape(1, D), target, "final_norm_loss")

    gp_b = lax.empty((N_CHIPS, RB, C), BF16)
    by_glue = {}

    def grad_b(nm, a, b, name):
        nonlocal gp_b
        (K, N), axis = next((shape, axis) for n, shape, axis in pack_b.group if n == nm)
        off = pack_b.offs[nm]
        tm = min(1024, K) if axis == 0 else min(1024, a.shape[1])
        tn = min(1024, N) if axis == 1 else min(1024, b.shape[1])
        if not (N == C and tm % LANE == 0 and tn % LANE == 0 and K % tm == 0 and N % tn == 0 and off % tm == 0):
            by_glue[nm] = _mm_tn(a, b, name)
            return
        base = off // tm
        if axis == 0:
            per = K // tm
            place = lambda i, j: (i // per, base + i % per, j)
        else:
            per = N // tn
            place = lambda i, j: (j // per, base + i, j % per)
        gp_b = _mm_tn(a, b, name, tm=tm, tn=tn, into=(gp_b, place))

    w_d, w_d_in = b_of("w_down", "nt", 512, 2048)
    da = _matmul(dh2_b, w_d, "nt", [BF16], "mlp_down_dx", extras=(a_pos,),
                 epilogue=lambda acc, a: (acc * (2.0 * a.astype(F32)),), b_in=w_d_in)
    grad_b("w_down", u, dh2_b, "mlp_down_dw")
    w_u, w_u_in = b_of("w_up", "nt", 1024, 2048)
    dhn = _matmul(da, w_u, "nt", [F32], "mlp_up_dx", tn=1024, b_in=w_u_in)
    grad_b("w_up", hn, da, "mlp_up_dw")
    dh1, dh1_b, g_mlp_norm = _norm_bwd(h1, dhn, mlp_norm, dh2, "mlp_norm_bwd")

    def gate_bwd(acc, ga, gb, ya, yb):
        ga, gb = _sigmoid(ga), _sigmoid(gb)
        return acc * ga, acc * gb, acc * ya * (ga * (1.0 - ga)), acc * yb * (gb * (1.0 - gb))

    dy_mla, dy_fox, dg_mla, dg_fox = _matmul(dh1_b, w_o, "nt", [BF16] * 4, "out_proj_dx_gates",
                                             extras=((gpre, 0), (gpre, 1), y_mla, y_fox), epilogue=gate_bwd)
    grad_b("w_out", merged, dh1_b, "out_proj_dw")
    do_mla = _matmul(dy_mla, w_mb, "nt", [BF16], "mla_branch_dx")
    grad_b("w_mla_branch", o_mla, dy_mla, "mla_branch_dw")
    do_fox = _matmul(dy_fox, w_fb, "nt", [BF16], "fox_branch_dx")
    grad_b("w_fox_branch", o_fox, dy_fox, "fox_branch_dw")
    for nm, g in by_glue.items():
        gp_b = lax.dynamic_update_slice(gp_b, pack_b.slab_rows(nm, g), (0, pack_b.offs[nm], 0))
    if RB > pack_b.used:
        gp_b = lax.dynamic_update_slice(gp_b, jnp.zeros((N_CHIPS, RB - pack_b.used, C), BF16), (0, pack_b.used, 0))

    rs_b = _xchg_start(gp_b, lax.empty((3, RB, C), BF16), False, do_fox, "grad_scatter_start_b")

    delta_mla = _att_delta_t(do_mla, o_mla, H, "mla_att_delta", order=rs_b[3])
    dq_rot, dk_nope, dkr_heads, dv_mla = _att_bwd_t(mla, do_mla, lse_mla, delta_mla, BF16, [BF16, F32],
                                                    "mla_att_bwd", dq_rope=(qc, qsa, qsb))
    delta_fox = _att_delta_t(do_fox, ox_fox, HF, "fox_att_delta")
    dfq, dfk, dfv, dcum = _att_bwd_t(fox, do_fox, lse_fox, delta_fox, BF16, [BF16], "fox_att_bwd")

    gp_b_sent, recv_b = _xchg_wait(rs_b, False, dfq, "grad_scatter_wait_b")
    swap_b = _sib_start(_sum_slabs(gp_b_sent, recv_b, chip, "grad_sum_b"), "grad_swap_start_b")

    dcqn = _matmul(dq_rot, w_uq_p, "nt", [F32], "mla_q_up_dx", order=swap_b[4])
    g_w_uq_p = _mm_tn(cqn, dq_rot, "mla_q_up_dw")
    dkv2 = jnp.concatenate([dk_nope, dv_mla], axis=1)
    dckvn = _matmul(dkv2, w_ukv_p, "nt", [F32], "mla_kv_up_dx")
    g_w_ukv_p = _mm_tn(ckvn, dkv2, "mla_kv_up_dw")

    dcum_rows = jnp.pad(dcum[:, :, 0], ((0, 8 - HF), (0, 0)))
    dlogf_rows = _suffix_sum_rows(dcum_rows, "fox_forget_suffix_sum")
    dlogf = _pad_cols(jnp.transpose(dlogf_rows[:HF]), LANE)
    d_small, g_q_norm, g_kv_norm, g_bias = _prep_bwd(
        small, dcqn, dckvn, dkr_heads, dlogf, q_norm, kv_norm, bias_pad, kc, ksa, ksb, H, "prep_bwd")
    dproj = [d_small, dfq, dfk, dfv, dg_mla, dg_fox]
    gs, gfq, gfk, gfv, gg_mla, gg_fox = [
        _matmul(part, xn, "tn", [BF16], "proj_dw_" + tag, tm=1024, tn=1024, tk=2048)
        for part, tag in zip(dproj, ("small", "fq", "fk", "fv", "g_mla", "g_fox"))]

    g_w_in = jnp.concatenate([gs[:o_kr], gs[o_kr:o_kr + MLA_ROPE], gfq, gfk, gfv,
                              gs[o_kr + LANE:o_kr + LANE + HF], gg_mla, gg_fox], axis=0)
    g_w_uq = g_w_uq_p.reshape(QL, H, QPAD)[:, :, :dqk].reshape(QL, H * dqk)
    g_w_ukv = jnp.concatenate([g_w_ukv_p[:, :H * MLA_NOPE].reshape(KVL, H, MLA_NOPE),
                               g_w_ukv_p[:, H * MLA_NOPE:].reshape(KVL, H, MLA_V)], axis=2).reshape(KVL, -1)

    gp_a = pack_a.slabs({"w_in": g_w_in, "w_uq": g_w_uq, "w_ukv": g_w_ukv})
    rs_a = _xchg_start(gp_a, lax.empty((3, RA, C), BF16), False, gg_fox, "grad_scatter_start_a")
    dxn = _matmul_parts(dproj, w_pack, "nn", F32, "proj_dx", order=rs_a[3])
    grad_x, _, g_attn_norm = _norm_bwd(xs, dxn, attn_norm, dh1, "attn_norm_bwd")
    gp_a_sent, recv_a = _xchg_wait(rs_a, False, grad_x, "grad_scatter_wait_a")
    swap_a = _sib_start(_sum_slabs(gp_a_sent, recv_a, chip, "grad_sum_a"), "grad_swap_start_a")
    vec_w = max(D, LANE)
    vec_rows = [g_attn_norm, g_mlp_norm, g_final, g_q_norm, g_kv_norm, g_bias, loss_part]
    vec = jnp.concatenate([_pad_cols(v, vec_w) for v in vec_rows] + [jnp.zeros((1, vec_w), F32)], axis=0)
    vsum = _all_reduce_vec(vec, "all_reduce_vectors")
    part_b, sib_b = _sib_wait(swap_b, vsum, "grad_swap_wait_b")

    grads, deltas, new_m, new_v = {}, {}, {}, {}

    def update(pack, mine, theirs):
        for nm, shape, _ in pack.group:
            K, N = shape
            if N == pack.C and K % 8 == 0 and pack.offs[nm] % _tile(K, 256, 8) == 0:
                g, d, nm_, nv_ = _adamw(weights[nm], mine, theirs, moments[nm][0], moments[nm][1], "adamw_" + nm,
                                        g_row=pack.offs[nm])
            else:
                g, d, nm_, nv_ = _adamw(weights[nm], pack.part(mine, nm, shape), pack.part(theirs, nm, shape),
                                        moments[nm][0], moments[nm][1], "adamw_" + nm)
            grads[nm], deltas[nm], new_m[nm], new_v[nm] = g, d, nm_, nv_
        return g

    last_b = update(pack_b, part_b, sib_b)
    part_a, sib_a = _sib_wait(swap_a, last_b, "grad_swap_wait_a")
    update(pack_a, part_a, sib_a)

    vec_names = ["attn_norm", "mlp_norm", "final_norm", "q_norm", "kv_norm", "fox_f_bias"]

    def vec_pack(arrs):
        return jnp.concatenate([_pad_cols(a.reshape(1, -1), vec_w) for a in arrs]
                               + [jnp.zeros((2, vec_w), F32)], axis=0)[None]

    vg, vd, vm, vv = _adamw(vec_pack([weights[n] for n in vec_names]), vsum, jnp.zeros_like(vsum),
                            vec_pack([moments[n][0] for n in vec_names]), vec_pack([moments[n][1] for n in vec_names]),
                            "adamw_vectors")
    for r, nm in enumerate(vec_names):
        shp = weights[nm].shape
        n = weights[nm].size
        grads[nm] = vsum[r, :n].reshape(shp)
        deltas[nm], new_m[nm], new_v[nm] = (vd[0, r, :n].reshape(shp), vm[0, r, :n].reshape(shp),
                                            vv[0, r, :n].reshape(shp))
    loss = vsum[6, 0]

    for res in (grads, deltas, new_m, new_v):
        res["w_in"] = flip(res["w_in"])
    order = ["attn_norm", "w_in", "fox_f_bias", "q_norm", "w_uq", "kv_norm", "w_ukv", "w_mla_branch", "w_fox_branch",
             "w_out", "mlp_norm", "w_up", "w_down", "final_norm"]
    return (loss, grad_x[None], *[grads[n] for n in order], *[deltas[n] for n in order],
            *[new_m[n] for n in order], *[new_v[n] for n in order])
```

```python
import math

import jax
import jax.numpy as jnp
from jax import lax
from jax.experimental import pallas as pl
from jax.experimental.pallas import tpu as pltpu

CHUNK = 64
MLA_HEADS = 8
MLA_Q_LORA = 512
MLA_KV_LORA = 256
MLA_NOPE = 128
MLA_ROPE = 64
MLA_V = 128
ROPE_THETA = 10000.0
FOX_HEADS = 8
FOX_HEAD_DIM = 128
EPS = 1e-6

ADAM_LR = 0.001
ADAM_B1 = 0.9
ADAM_B2 = 0.999
ADAM_EPS = 1e-08
ADAM_WD = 0.01
ADAM_STEP = 10

LANE = 128
QPAD = 2 * LANE
N_CHIPS = 4
N_DEV = 8
VMEM_LIMIT = 48 * 1024 * 1024
ATT_T = 2048
ATT_T_FWD = 4096
QSUB = 256
ROW_T = 256
PACK_ROWS = 256
LOG2E = 1.4426950408889634

BF16 = jnp.bfloat16
F32 = jnp.float32
MESH = pl.DeviceIdType.MESH

_NT = (((1,), (1,)), ((), ()))
_TN = (((0,), (0,)), ((), ()))
_NN = (((1,), (0,)), ((), ()))


def _tile(dim, pref, align=LANE):
    if dim <= pref:
        return dim
    t = (pref // align) * align
    while t >= align:
        if dim % t == 0:
            return t
        t -= align
    return dim


def _params(sem=None):
    return pltpu.CompilerParams(dimension_semantics=sem, vmem_limit_bytes=VMEM_LIMIT)


_ANY_SPEC = pl.BlockSpec(memory_space=pl.ANY)


def _matmul(a, b, mode, out_dtypes, name, *, tm=1024, tn=512, tk=2048, extras=(), row_extras=(), col_extras=(),
            epilogue=None, order=None, into=None, b_in=None):
    b_shape = b.shape if b_in is None else b_in[0]
    if mode == "nn":
        (M, K), (K2, N) = a.shape, b_shape
    elif mode == "nt":
        (M, K), (N, K2) = a.shape, b_shape
    else:
        (K, M), (K2, N) = a.shape, b_shape
    assert K == K2, (name, a.shape, b_shape)
    tm, tn, tk = _tile(M, tm), _tile(N, tn), _tile(K, tk)
    nk = K // tk
    extras = [e if isinstance(e, tuple) else (e, 0) for e in extras]
    n_out = len(out_dtypes)
    n_ex = len(extras) + len(row_extras) + len(col_extras)
    n_ord = 0 if order is None else 1
    assert all(r.shape == (M, tn) for r in row_extras), name
    dims = {"nn": _NN, "nt": _NT, "tn": _TN}[mode]

    def body(*refs):
        a_ref, b_ref = refs[0], refs[1]
        ex_refs = refs[2:2 + n_ex]
        o_refs = refs[2 + n_ex + n_ord:2 + n_ex + n_ord + n_out]
        acc_ref = refs[2 + n_ex + n_ord + n_out]
        k = pl.program_id(2)
        part = lax.dot_general(a_ref[...], b_ref[...], dims, preferred_element_type=F32)

        @pl.when(k == 0)
        def _():
            acc_ref[...] = part

        @pl.when(k > 0)
        def _():
            acc_ref[...] += part

        @pl.when(k == nk - 1)
        def _():
            acc = acc_ref[...]
            if epilogue is None:
                outs = (acc,)
            else:
                outs = epilogue(acc, *[r[...] for r in ex_refs])
            for o_ref, o in zip(o_refs, outs):
                o_ref[...] = o.astype(o_ref.dtype)

    if mode == "nn":
        a_spec = pl.BlockSpec((tm, tk), lambda i, j, k: (i, k))
        b_spec = pl.BlockSpec((tk, tn), lambda i, j, k: (k, j))
    elif mode == "nt":
        a_spec = pl.BlockSpec((tm, tk), lambda i, j, k: (i, k))
        b_spec = pl.BlockSpec((tn, tk), lambda i, j, k: (j, k))
    else:
        a_spec = pl.BlockSpec((tk, tm), lambda i, j, k: (k, i))
        b_spec = pl.BlockSpec((tk, tn), lambda i, j, k: (k, j))
    if b_in is not None:
        b_block = (None, tn, tk) if mode == "nt" else (None, tk, tn)
        b_spec = pl.BlockSpec(b_block, lambda i, j, k: b_in[1](j, k))
    mn_spec = pl.BlockSpec((tm, tn), lambda i, j, k: (i, j))
    row_spec = pl.BlockSpec((tm, tn), lambda i, j, k: (i, 0))
    col_spec = pl.BlockSpec((1, tn), lambda i, j, k: (0, j))
    out_specs = [mn_spec] * n_out
    out_shape = [jax.ShapeDtypeStruct((M, N), dt) for dt in out_dtypes]
    aliases = {}
    if into is not None:
        buf, place = into
        assert n_out == 1 and n_ord == 1 and order is buf, name
        out_specs = [pl.BlockSpec((None, tm, tn), lambda i, j, k: place(i, j))]
        out_shape = [jax.ShapeDtypeStruct(buf.shape, buf.dtype)]
        aliases = {2 + n_ex: 0}
    outs = pl.pallas_call(
        body,
        name=name,
        grid=(M // tm, N // tn, nk),
        in_specs=([a_spec, b_spec]
                  + [pl.BlockSpec((tm, tn), lambda i, j, k, g=g: (i, j + g * (N // tn))) for _, g in extras]
                  + [row_spec] * len(row_extras) + [col_spec] * len(col_extras) + [_ANY_SPEC] * n_ord),
        out_specs=out_specs,
        out_shape=out_shape,
        scratch_shapes=[pltpu.VMEM((tm, tn), F32)],
        input_output_aliases=aliases,
        compiler_params=_params(("parallel", "parallel", "arbitrary")),
    )(a, b, *[e for e, _ in extras], *row_extras, *col_extras, *([] if order is None else [order]))
    return outs[0] if n_out == 1 else outs


def _matmul_parts(parts, b, mode, out_dtype, name, *, tm=1024, tn=1024, tk=1024, order=None):
    assert mode in ("nn", "tn")
    if mode == "nn":
        M, (K, N) = parts[0].shape[0], b.shape
        widths = [p.shape[1] for p in parts]
    else:
        K, N = b.shape
        widths = [p.shape[1] for p in parts]
        M = sum(widths)
    common = math.gcd(*widths)
    tm, tn, tk = _tile(M if mode == "nn" else common, tm), _tile(N, tn), _tile(common if mode == "nn" else K, tk)
    t_part = tk if mode == "nn" else tm
    assert sum(widths) == (K if mode == "nn" else M), name
    if any(w % t_part for w in widths):
        parts, widths = [jnp.concatenate(parts, axis=1)], [sum(widths)]
    lo =[sum(widths[:p]) // t_part for p in range(len(parts))]
    cnt = [w // t_part for w in widths]
    nk = K // tk
    n_parts = len(parts)
    n_ord = 0 if order is None else 1
    dims = _NN if mode == "nn" else _TN

    def body(*refs):
        a_refs = refs[0:n_parts]
        b_ref = refs[n_parts]
        o_ref, acc_ref = refs[n_parts + 1 + n_ord], refs[n_parts + 2 + n_ord]
        i, k = pl.program_id(0), pl.program_id(2)
        sel = k if mode == "nn" else i
        for p in range(n_parts):
            @pl.when((sel >= lo[p]) & (sel < lo[p] + cnt[p]))
            def _(p=p):
                part = lax.dot_general(a_refs[p][...], b_ref[...], dims, preferred_element_type=F32)

                @pl.when(k == 0)
                def _():
                    acc_ref[...] = part

                @pl.when(k > 0)
                def _():
                    acc_ref[...] += part

        @pl.when(k == nk - 1)
        def _():
            o_ref[...] = acc_ref[...].astype(o_ref.dtype)

    def a_spec(p):
        if mode == "nn":
            return pl.BlockSpec((tm, tk), lambda i, j, k: (i, jnp.clip(k - lo[p], 0, cnt[p] - 1)))
        return pl.BlockSpec((tk, tm), lambda i, j, k: (
            jnp.where((i >= lo[p]) & (i < lo[p] + cnt[p]), k, 0), jnp.clip(i - lo[p], 0, cnt[p] - 1)))

    return pl.pallas_call(
        body, name=name, grid=(M // tm, N // tn, nk),
        in_specs=[a_spec(p) for p in range(n_parts)] + [pl.BlockSpec((tk, tn), lambda i, j, k: (k, j))]
        + [_ANY_SPEC] * n_ord,
        out_specs=pl.BlockSpec((tm, tn), lambda i, j, k: (i, j)),
        out_shape=jax.ShapeDtypeStruct((M, N), out_dtype),
        scratch_shapes=[pltpu.VMEM((tm, tn), F32)],
        compiler_params=_params(("parallel", "parallel", "arbitrary")),
    )(*parts, b, *([] if order is None else [order]))


def _mm_tn(a, b, name, tm=1024, tn=1024, into=None):
    return _matmul(a, b, "tn", [F32], name, tm=tm, tn=tn, tk=2048, into=into,
                   order=None if into is None else into[0])


def _row_spec(ts, width, col=0):
    return pl.BlockSpec((ts, width), lambda i: (i, col))


def _full_spec(shape):
    return pl.BlockSpec(shape, lambda i: tuple(0 for _ in shape))


def _rms(x):
    return lax.rsqrt(jnp.mean(x * x, axis=-1, keepdims=True) + EPS)


def _rms_bwd(x, dy, g):
    r = _rms(x)
    xh = x * r
    gy = dy * g
    dx = r * (gy - xh * jnp.mean(xh * gy, axis=-1, keepdims=True))
    return dx, dy * xh


def _norm_fwd(x, g, name, order=None):
    S, D = x.shape
    ts = _tile(S, ROW_T, 8)

    def body(x_ref, g_ref, *rest):
        o_ref = rest[-1]
        xv = x_ref[...]
        o_ref[...] = ((xv * _rms(xv)) * g_ref[...]).astype(BF16)

    extra = [] if order is None else [order]
    return pl.pallas_call(
        body, name=name, grid=(S // ts,),
        in_specs=[_row_spec(ts, D), _full_spec((1, D))] + [_ANY_SPEC] * len(extra),
        out_specs=_row_spec(ts, D),
        out_shape=jax.ShapeDtypeStruct((S, D), BF16),
        compiler_params=_params(("parallel",)),
    )(x, g, *extra)


def _norm_bwd(x, dy, g, dres, name):
    S, D = x.shape
    ts = _tile(S, ROW_T, 8)

    def body(x_ref, dy_ref, g_ref, dres_ref, dx_ref, dxb_ref, dg_ref):
        dx, dg_rows = _rms_bwd(x_ref[...], dy_ref[...], g_ref[...])
        dx = dres_ref[...] + dx
        dx_ref[...] = dx
        dxb_ref[...] = dx.astype(BF16)

        @pl.when(pl.program_id(0) == 0)
        def _():
            dg_ref[...] = jnp.zeros_like(dg_ref)

        dg_ref[...] += jnp.sum(dg_rows, axis=0, keepdims=True)

    return pl.pallas_call(
        body, name=name, grid=(S // ts,),
        in_specs=[_row_spec(ts, D), _row_spec(ts, D), _full_spec((1, D)), _row_spec(ts, D)],
        out_specs=[_row_spec(ts, D), _row_spec(ts, D), _full_spec((1, D))],
        out_shape=[jax.ShapeDtypeStruct((S, D), F32), jax.ShapeDtypeStruct((S, D), BF16),
                   jax.ShapeDtypeStruct((1, D), F32)],
        compiler_params=_params(("arbitrary",)),
    )(x, dy, g, dres)


def _rope(x, c, sa, sb, sign):
    w = x.shape[-1]
    half = MLA_ROPE // 2
    fwd = pltpu.roll(x, w - half, 1)
    back = pltpu.roll(x, half, 1)
    if sign < 0:
        return x * c - fwd * sa - back * sb
    return x * c + fwd * sa + back * sb


def _split3(x):
    hi = x.astype(BF16)
    r1 = x - hi.astype(F32)
    mid = r1.astype(BF16)
    lo = (r1 - mid.astype(F32)).astype(BF16)
    return hi, mid, lo


def _prep_fwd(small, q_norm, kv_norm, bias_pad, kc, ksa, ksb, n_heads, name):
    S, W = small.shape
    QL, KVL = q_norm.shape[1], kv_norm.shape[1]
    assert W == QL + KVL + 2 * LANE
    ts = _tile(S, ROW_T, 8)
    tri = (lax.broadcasted_iota(jnp.int32, (ts, ts), 0) >= lax.broadcasted_iota(jnp.int32, (ts, ts), 1)).astype(BF16)

    def body(s_ref, qn_ref, kvn_ref, b_ref, kc_ref, ksa_ref, ksb_ref, tri_ref,
             cqn_ref, ckvn_ref, kr_ref, cum_ref, carry_ref):
        cq = s_ref[:, 0:QL]
        cqn_ref[...] = ((cq * _rms(cq)) * qn_ref[...]).astype(BF16)
        ckv = s_ref[:, QL:QL + KVL]
        ckvn_ref[...] = ((ckv * _rms(ckv)) * kvn_ref[...]).astype(BF16)
        kr = s_ref[:, QL + KVL:QL + KVL + LANE]
        kr_ref[...] = _rope(kr, kc_ref[...], ksa_ref[...], ksb_ref[...], 1).astype(BF16)
        z = s_ref[:, QL + KVL + LANE:W] + b_ref[...]
        logf = jnp.minimum(z, 0.0) - jnp.log1p(jnp.exp(-jnp.abs(z)))
        lane = lax.broadcasted_iota(jnp.int32, logf.shape, 1)
        logf = jnp.where(lane < n_heads, logf, 0.0)

        @pl.when(pl.program_id(0) == 0)
        def _():
            carry_ref[...] = jnp.zeros_like(carry_ref)

        t = tri_ref[...]
        cum = carry_ref[...]
        for part in _split3(logf):
            cum = cum + jnp.dot(t, part, preferred_element_type=F32)
        cum_ref[...] = cum
        carry_ref[...] = cum[ts - 1:ts, :]

    return pl.pallas_call(
        body, name=name, grid=(S // ts,),
        in_specs=[_row_spec(ts, W), _full_spec((1, QL)), _full_spec((1, KVL)), _full_spec((1, LANE)),
                  _row_spec(ts, LANE), _row_spec(ts, LANE), _row_spec(ts, LANE), _full_spec((ts, ts))],
        out_specs=[_row_spec(ts, QL), _row_spec(ts, KVL), _row_spec(ts, LANE), _row_spec(ts, LANE)],
        out_shape=[jax.ShapeDtypeStruct((S, QL), BF16), jax.ShapeDtypeStruct((S, KVL), BF16),
                   jax.ShapeDtypeStruct((S, LANE), BF16), jax.ShapeDtypeStruct((S, LANE), F32)],
        scratch_shapes=[pltpu.VMEM((1, LANE), F32)],
        compiler_params=_params(("arbitrary",)),
    )(small, q_norm, kv_norm, bias_pad, kc, ksa, ksb, tri)


def _prep_bwd(small, dcqn, dckvn, dkr_heads, dlogf, q_norm, kv_norm, bias_pad, kc, ksa, ksb, n_heads, name):
    S, W = small.shape
    QL, KVL = q_norm.shape[1], kv_norm.shape[1]
    ts = _tile(S, ROW_T, 8)

    def body(s_ref, dcq_ref, dckv_ref, dkr_ref, dlf_ref, qn_ref, kvn_ref, b_ref, kc_ref, ksa_ref, ksb_ref,
             ds_ref, gq_ref, gkv_ref, gb_ref):
        dcq, gq_rows = _rms_bwd(s_ref[:, 0:QL], dcq_ref[...], qn_ref[...])
        ds_ref[:, 0:QL] = dcq.astype(BF16)
        dckv, gkv_rows = _rms_bwd(s_ref[:, QL:QL + KVL], dckv_ref[...], kvn_ref[...])
        ds_ref[:, QL:QL + KVL] = dckv.astype(BF16)
        dkr = dkr_ref[:, 0:LANE]
        for h in range(1, n_heads):
            dkr = dkr + dkr_ref[:, h * LANE:(h + 1) * LANE]
        ds_ref[:, QL + KVL:QL + KVL + LANE] = _rope(dkr, kc_ref[...], ksa_ref[...], ksb_ref[...], -1).astype(BF16)
        z = s_ref[:, QL + KVL + LANE:W] + b_ref[...]
        dff = dlf_ref[...] * (1.0 / (1.0 + jnp.exp(z)))
        ds_ref[:, QL + KVL + LANE:W] = dff.astype(BF16)

        @pl.when(pl.program_id(0) == 0)
        def _():
            gq_ref[...] = jnp.zeros_like(gq_ref)
            gkv_ref[...] = jnp.zeros_like(gkv_ref)
            gb_ref[...] = jnp.zeros_like(gb_ref)

        gq_ref[...] += jnp.sum(gq_rows, axis=0, keepdims=True)
        gkv_ref[...] += jnp.sum(gkv_rows, axis=0, keepdims=True)
        gb_ref[...] += jnp.sum(dff, axis=0, keepdims=True)

    return pl.pallas_call(
        body, name=name, grid=(S // ts,),
        in_specs=[_row_spec(ts, W), _row_spec(ts, QL), _row_spec(ts, KVL), _row_spec(ts, n_heads * LANE),
                  _row_spec(ts, LANE), _full_spec((1, QL)), _full_spec((1, KVL)), _full_spec((1, LANE)),
                  _row_spec(ts, LANE), _row_spec(ts, LANE), _row_spec(ts, LANE)],
        out_specs=[_row_spec(ts, W), _full_spec((1, QL)), _full_spec((1, KVL)), _full_spec((1, LANE))],
        out_shape=[jax.ShapeDtypeStruct((S, W), BF16), jax.ShapeDtypeStruct((1, QL), F32),
                   jax.ShapeDtypeStruct((1, KVL), F32), jax.ShapeDtypeStruct((1, LANE), F32)],
        compiler_params=_params(("arbitrary",)),
    )(small, dcqn, dckvn, dkr_heads, dlogf, q_norm, kv_norm, bias_pad, kc, ksa, ksb)


def _sigmoid(z):
    return 1.0 / (1.0 + jnp.exp(-z))


def _final(h, g, target, name):
    S, D = h.shape
    ts = _tile(S, ROW_T, 8)

    def body(h_ref, g_ref, t_ref, dh_ref, dhb_ref, dg_ref, loss_ref):
        hv = h_ref[...]
        gv = g_ref[...]
        err = (hv * _rms(hv)) * gv - t_ref[...]
        dh, dg_rows = _rms_bwd(hv, err / D, gv)
        dh_ref[...] = dh
        dhb_ref[...] = dh.astype(BF16)

        @pl.when(pl.program_id(0) == 0)
        def _():
            dg_ref[...] = jnp.zeros_like(dg_ref)
            loss_ref[...] = jnp.zeros_like(loss_ref)

        dg_ref[...] += jnp.sum(dg_rows, axis=0, keepdims=True)
        row_loss = jnp.mean(err * err, axis=-1, keepdims=True)
        loss_ref[...] += 0.5 * jnp.sum(row_loss, axis=0, keepdims=True)

    return pl.pallas_call(
        body, name=name, grid=(S // ts,),
        in_specs=[_row_spec(ts, D), _full_spec((1, D)), _row_spec(ts, D)],
        out_specs=[_row_spec(ts, D), _row_spec(ts, D), _full_spec((1, D)), _full_spec((1, LANE))],
        out_shape=[jax.ShapeDtypeStruct((S, D), F32), jax.ShapeDtypeStruct((S, D), BF16),
                   jax.ShapeDtypeStruct((1, D), F32), jax.ShapeDtypeStruct((1, LANE), F32)],
        compiler_params=_params(("arbitrary",)),
    )(h, g, target)


def _suffix_sum_rows(x, name):
    R, S = x.shape
    tb = _tile(S, 512)
    nb = S // tb
    tri = (lax.broadcasted_iota(jnp.int32, (tb, tb), 0) >= lax.broadcasted_iota(jnp.int32, (tb, tb), 1)).astype(BF16)

    def body(x_ref, tri_ref, o_ref, carry_ref):
        @pl.when(pl.program_id(0) == 0)
        def _():
            carry_ref[...] = jnp.zeros_like(carry_ref)

        xv = x_ref[...]
        t = tri_ref[...]
        acc = jnp.broadcast_to(carry_ref[:, 0:1], xv.shape)
        for part in _split3(xv):
            acc = acc + jnp.dot(part, t, preferred_element_type=F32)
        o_ref[...] = acc
        carry_ref[...] = jnp.broadcast_to(acc[:, 0:1], carry_ref.shape)

    rev = pl.BlockSpec((R, tb), lambda i: (0, nb - 1 - i))
    return pl.pallas_call(
        body, name=name, grid=(nb,),
        in_specs=[rev, _full_spec((tb, tb))], out_specs=rev,
        out_shape=jax.ShapeDtypeStruct((R, S), F32),
        scratch_shapes=[pltpu.VMEM((R, LANE), F32)],
        compiler_params=_params(("arbitrary",)),
    )(x, tri)


def _pairs(nb, by_key):
    if by_key:
        pr = [(i, j) for j in range(nb) for i in range(j, nb)]
    else:
        pr = [(i, j) for i in range(nb) for j in range(i + 1)]
    return (jnp.asarray([p[0] for p in pr], jnp.int32), jnp.asarray([p[1] for p in pr], jnp.int32), len(pr))


class _AttT:
    def __init__(self, S, n_heads, q, ks, v, scale, chunk_causal, cum_rep=None, block=None):
        self.S, self.H, self.q, self.ks, self.v = S, n_heads, q, ks, v
        self.scale, self.chunk_causal, self.cum_rep = scale, chunk_causal, cum_rep
        self.T = _tile(S, block or ATT_T)
        self.qs = min(QSUB, self.T)
        self.nb = S // self.T
        self.dq, self.dv = q[1], v[1]
        self.has_bias = cum_rep is not None

    def q_spec(self, op):
        _, w, off, per_head = op
        return pl.BlockSpec((self.T, w), lambda h, p, it, jt: (it[p], off + (h if per_head else 0)))

    def k_spec(self, op):
        _, w, off, per_head = op
        return pl.BlockSpec((self.T, w), lambda h, p, it, jt: (jt[p], off + (h if per_head else 0)))

    def row_q(self):
        return pl.BlockSpec((None, 1, self.T), lambda h, p, it, jt: (h, 0, it[p]))

    def cum_k(self):
        return pl.BlockSpec((None, self.T, self.qs), lambda h, p, it, jt: (h, jt[p], 0))

    def sub_blocks(self, masked):
        return [(q0, min(self.T, q0 + self.qs) if masked else self.T) for q0 in range(0, self.T, self.qs)]

    def scores(self, k, q_sub, cum, q0, masked):
        s = lax.dot_general(k, q_sub, _NT, preferred_element_type=F32)
        if self.has_bias:
            s = s - cum
        mask = None
        if masked:
            r = lax.broadcasted_iota(jnp.int32, s.shape, 0)
            c = lax.broadcasted_iota(jnp.int32, s.shape, 1) + q0
            mask = (r // CHUNK <= c // CHUNK) if self.chunk_causal else (r <= c)
        return s, mask


def _join(k_refs):
    return k_refs[0][...] if len(k_refs) == 1 else jnp.concatenate([r[...] for r in k_refs], axis=-1)


def _att_fwd_t(att, name, exact=False):
    S, H, T, qs = att.S, att.H, att.T, att.qs
    it, jt, npairs = _pairs(att.nb, by_key=False)
    nk = len(att.ks)

    def body(it_ref, jt_ref, *refs):
        q_ref = refs[0]
        k_refs = refs[1:1 + nk]
        v_ref = refs[1 + nk]
        n = 2 + nk
        cum_ref = None
        if att.has_bias:
            cum_ref = refs[n]
            n += 1
        o_ref = refs[n]
        n += 1
        ox_ref = None
        if exact:
            ox_ref = refs[n]
            n += 1
        lse_ref, m_ref, l_ref, acc_ref = refs[n:n + 4]
        lo_ref = refs[n + 4] if exact else None
        p = pl.program_id(1)
        i, j = it_ref[p], jt_ref[p]

        @pl.when(j == 0)
        def _():
            m_ref[...] = jnp.full_like(m_ref, -jnp.inf)
            l_ref[...] = jnp.zeros_like(l_ref)
            acc_ref[...] = jnp.zeros_like(acc_ref)
            if exact:
                lo_ref[...] = jnp.zeros_like(lo_ref)

        def step(masked):
            k = _join(k_refs)
            v = v_ref[...]
            subs = att.sub_blocks(masked)

            def logits(idx):
                q0, nkeys = subs[idx]
                cum = cum_ref[0:nkeys, :] if att.has_bias else None
                return att.scores(k[0:nkeys], q_ref[q0:q0 + qs, :], cum, q0, masked)

            ahead = logits(0)
            for idx, (q0, nkeys) in enumerate(subs):
                qsl = slice(q0, q0 + qs)
                s, mask = ahead
                if idx + 1 < len(subs):
                    ahead = logits(idx + 1)
                if masked:
                    s = jnp.where(mask, s, -jnp.inf)
                m_prev = m_ref[:, qsl]
                m_new = jnp.maximum(m_prev, jnp.max(s, axis=0, keepdims=True))
                alpha = jnp.exp2(m_prev - m_new)
                pr = jnp.exp2(s - m_new)
                l_ref[:, qsl] = alpha * l_ref[:, qsl] + jnp.sum(pr, axis=0, keepdims=True)
                p_hi = pr.astype(BF16)
                acc_ref[:, qsl] = alpha * acc_ref[:, qsl] + lax.dot_general(
                    v[0:nkeys], p_hi, _TN, preferred_element_type=F32)
                if exact:
                    p_lo = (pr - p_hi.astype(F32)).astype(BF16)
                    lo_ref[:, qsl] = alpha * lo_ref[:, qsl] + lax.dot_general(
                        v[0:nkeys], p_lo, _TN, preferred_element_type=F32)
                m_ref[:, qsl] = m_new

        @pl.when(j < i)
        def _():
            step(False)

        @pl.when(j == i)
        def _():
            step(True)
            l = l_ref[...]
            inv = 1.0 / l
            o_ref[...] = jnp.transpose(acc_ref[...] * inv).astype(o_ref.dtype)
            if exact:
                ox_ref[...] = jnp.transpose((acc_ref[...] + lo_ref[...]) * inv)
            lse_ref[...] = m_ref[...] + jnp.log2(l)

    in_specs = [att.q_spec(att.q)] + [att.k_spec(k) for k in att.ks] + [att.k_spec(att.v)]
    args = [att.q[0]] + [k[0] for k in att.ks] + [att.v[0]]
    if att.has_bias:
        in_specs.append(att.cum_k())
        args.append(att.cum_rep)
    o_spec = pl.BlockSpec((T, att.dv), lambda h, p, it, jt: (it[p], h))
    out_specs = [o_spec]
    out_shape = [jax.ShapeDtypeStruct((S, H * att.dv), BF16)]
    scratch = [pltpu.VMEM((1, T), F32), pltpu.VMEM((1, T), F32), pltpu.VMEM((att.dv, T), F32)]
    if exact:
        out_specs.append(o_spec)
        out_shape.append(jax.ShapeDtypeStruct((S, H * att.dv), F32))
        scratch.append(pltpu.VMEM((att.dv, T), F32))
    out_specs.append(att.row_q())
    out_shape.append(jax.ShapeDtypeStruct((H, 1, S), F32))
    return pl.pallas_call(
        body, name=name,
        grid_spec=pltpu.PrefetchScalarGridSpec(
            num_scalar_prefetch=2, grid=(H, npairs), in_specs=in_specs, out_specs=out_specs,
            scratch_shapes=scratch),
        out_shape=out_shape,
        compiler_params=_params(("parallel", "arbitrary")),
    )(it, jt, *args)


def _att_delta_t(do, o, n_heads, name, order=None):
    S = do.shape[0]
    w = do.shape[1] // n_heads
    ts = _tile(S, ATT_T)
    ones = jnp.ones((8, w), BF16)
    extra = [] if order is None else [order]

    def body(do_ref, o_ref, ones_ref, *rest):
        d_ref = rest[-1]
        prod = do_ref[...].astype(F32) * o_ref[...].astype(F32)
        acc = jnp.zeros((8, ts), F32)
        for part in _split3(prod):
            acc = acc + lax.dot_general(ones_ref[...], part, _NT, preferred_element_type=F32)
        d_ref[...] = acc[0:1, :]

    blk = pl.BlockSpec((ts, w), lambda i, h: (i, h))
    return pl.pallas_call(
        body, name=name, grid=(S // ts, n_heads),
        in_specs=[blk, blk, pl.BlockSpec((8, w), lambda i, h: (0, 0))] + [_ANY_SPEC] * len(extra),
        out_specs=pl.BlockSpec((None, 1, ts), lambda i, h: (h, 0, i)),
        out_shape=jax.ShapeDtypeStruct((n_heads, 1, S), F32),
        compiler_params=_params(("parallel", "parallel")),
    )(do, o, ones, *extra)


def _att_bwd_t(att, do, lse, delta, dq_dtype, dk_dtypes, name, dq_rope=None):
    S, H, T, qs = att.S, att.H, att.T, att.qs
    it, jt, npairs = _pairs(att.nb, by_key=True)
    nk = len(att.ks)
    last = att.nb - 1
    widths = [k[1] for k in att.ks]

    def body(it_ref, jt_ref, *refs):
        q_ref = refs[0]
        k_refs = refs[1:1 + nk]
        v_ref, do_ref, lse_ref, dl_ref = refs[1 + nk:5 + nk]
        n = 5 + nk
        cum_ref = None
        if att.has_bias:
            cum_ref = refs[n]
            n += 1
        rope_refs = None
        if dq_rope is not None:
            rope_refs = refs[n:n + 3]
            n += 3
        dq_ref = refs[n]
        dk_refs = refs[n + 1:n + 1 + nk]
        dv_ref = refs[n + 1 + nk]
        n += nk + 2
        dc_ref = None
        if att.has_bias:
            dc_ref = refs[n]
            n += 1
        dq_acc, dk_acc, dv_acc = refs[n:n + 3]
        dc_acc = refs[n + 3] if att.has_bias else None
        p = pl.program_id(1)
        i, j = it_ref[p], jt_ref[p]

        @pl.when(p == 0)
        def _():
            dq_acc[...] = jnp.zeros_like(dq_acc)

        @pl.when(i == j)
        def _():
            dk_acc[...] = jnp.zeros_like(dk_acc)
            dv_acc[...] = jnp.zeros_like(dv_acc)
            if att.has_bias:
                dc_acc[...] = jnp.zeros_like(dc_acc)

        def step(masked):
            k = _join(k_refs)
            v = v_ref[...]
            subs = att.sub_blocks(masked)

            def logits(idx):
                q0, nkeys = subs[idx]
                cum = cum_ref[0:nkeys, :] if att.has_bias else None
                return att.scores(k[0:nkeys], q_ref[q0:q0 + qs, :], cum, q0, masked)

            ahead = logits(0)
            for idx, (q0, nkeys) in enumerate(subs):
                qsl = slice(q0, q0 + qs)
                ksl = slice(0, nkeys)
                q_sub = q_ref[qsl, :]
                do_sub = do_ref[qsl, :]
                s, mask = ahead
                if idx + 1 < len(subs):
                    ahead = logits(idx + 1)
                pr = jnp.exp2(s - lse_ref[:, qsl])
                if masked:
                    pr = jnp.where(mask, pr, 0.0)
                dp = lax.dot_general(v[ksl], do_sub, _NT, preferred_element_type=F32)
                ds = pr * (dp - dl_ref[:, qsl])
                ds_b = ds.astype(BF16)
                dv_acc[ksl, :] += jnp.dot(pr.astype(BF16), do_sub, preferred_element_type=F32)
                dk_acc[ksl, :] += jnp.dot(ds_b, q_sub, preferred_element_type=F32)
                dq_acc[i, :, qsl] += lax.dot_general(k[ksl], ds_b, _TN, preferred_element_type=F32)
                if att.has_bias:
                    part = ds[:, 0:LANE] if qs >= LANE else ds
                    for c0 in range(LANE, qs, LANE):
                        part = part + ds[:, c0:c0 + LANE]
                    dc_acc[ksl, :] += part

        @pl.when(i > j)
        def _():
            step(False)

        @pl.when(i == j)
        def _():
            step(True)
            dq = jnp.transpose(dq_acc[i] * att.scale)
            if dq_rope is not None:
                dq = _rope(dq, rope_refs[0][...], rope_refs[1][...], rope_refs[2][...], -1)
            dq_ref[...] = dq.astype(dq_ref.dtype)

        @pl.when(i == last)
        def _():
            dk = dk_acc[...] * (1.0 / LOG2E)
            off = 0
            for r, w in zip(dk_refs, widths):
                r[...] = dk[:, off:off + w].astype(r.dtype)
                off += w
            dv_ref[...] = dv_acc[...].astype(dv_ref.dtype)
            if att.has_bias:
                dc_ref[...] = -jnp.sum(dc_acc[...], axis=-1, keepdims=True)

    do_op = (do, att.dv, 0, True)
    in_specs = ([att.q_spec(att.q)] + [att.k_spec(k) for k in att.ks]
                + [att.k_spec(att.v), att.q_spec(do_op), att.row_q(), att.row_q()])
    args = [att.q[0]] + [k[0] for k in att.ks] + [att.v[0], do, lse, delta]
    if att.has_bias:
        in_specs.append(att.cum_k())
        args.append(att.cum_rep)
    if dq_rope is not None:
        in_specs += [pl.BlockSpec((T, att.dq), lambda h, p, it, jt: (jt[p], 0))] * 3
        args += list(dq_rope)
    out_specs = [pl.BlockSpec((T, att.dq), lambda h, p, it, jt: (jt[p], h))]
    out_shape = [jax.ShapeDtypeStruct((S, H * att.dq), dq_dtype)]
    out_specs += [pl.BlockSpec((T, w), lambda h, p, it, jt: (jt[p], h)) for w in widths]
    out_shape += [jax.ShapeDtypeStruct((S, H * w), dt) for w, dt in zip(widths, dk_dtypes)]
    out_specs.append(pl.BlockSpec((T, att.dv), lambda h, p, it, jt: (jt[p], h)))
    out_shape.append(jax.ShapeDtypeStruct((S, H * att.dv), BF16))
    scratch = [pltpu.VMEM((att.nb, att.dq, T), F32), pltpu.VMEM((T, att.dq), F32), pltpu.VMEM((T, att.dv), F32)]
    if att.has_bias:
        out_specs.append(pl.BlockSpec((None, T, 1), lambda h, p, it, jt: (h, jt[p], 0)))
        out_shape.append(jax.ShapeDtypeStruct((H, S, 1), F32))
        scratch.append(pltpu.VMEM((T, min(qs, LANE)), F32))
    return pl.pallas_call(
        body, name=name,
        grid_spec=pltpu.PrefetchScalarGridSpec(
            num_scalar_prefetch=2, grid=(H, npairs), in_specs=in_specs, out_specs=out_specs,
            scratch_shapes=scratch),
        out_shape=out_shape,
        compiler_params=_params(("parallel", "arbitrary")),
    )(it, jt, *args)


def _adamw(w, g1, g2, m, v, name, g_row=None):
    _, K, N = w.shape
    by_rows = K % 8 == 0
    tr = _tile(K, 256, 8) if by_rows else K
    if g_row is None:
        assert g1.shape == (K, N) and g2.shape == (K, N), name
        g_row = 0
    assert by_rows and g_row % tr == 0 or g_row == 0, name
    g_blk = g_row // tr
    tc = N if by_rows else _tile(N, LANE)
    c1 = 1.0 - ADAM_B1 ** ADAM_STEP
    c2 = 1.0 - ADAM_B2 ** ADAM_STEP

    def body(w_ref, g1_ref, g2_ref, m_ref, v_ref, g_ref, d_ref, nm_ref, nv_ref):
        gv = g1_ref[...] + g2_ref[...]
        nm = ADAM_B1 * m_ref[...] + (1.0 - ADAM_B1) * gv
        nv = ADAM_B2 * v_ref[...] + (1.0 - ADAM_B2) * (gv * gv)
        g_ref[...] = gv
        d_ref[...] = -ADAM_LR * ((nm / c1) / (jnp.sqrt(nv / c2) + ADAM_EPS) + ADAM_WD * w_ref[...])
        nm_ref[...] = nm
        nv_ref[...] = nv

    if by_rows:
        blk = pl.BlockSpec((None, tr, N), lambda i: (0, i, 0))
        gblk = pl.BlockSpec((tr, N), lambda i: (g_blk + i, 0))
    else:
        blk = pl.BlockSpec((None, K, tc), lambda i: (0, 0, i))
        gblk = pl.BlockSpec((K, tc), lambda i: (0, i))
    return pl.pallas_call(
        body, name=name, grid=(K // tr if by_rows else N // tc,),
        in_specs=[blk, gblk, gblk, blk, blk], out_specs=[blk] * 4,
        out_shape=[jax.ShapeDtypeStruct((1, K, N), F32)] * 4,
        compiler_params=_params(("parallel",)),
    )(w, g1, g2, m, v)


_HBM_SPEC = pl.BlockSpec(memory_space=pltpu.HBM)
_SEM_SPEC = pl.BlockSpec(memory_space=pltpu.SEMAPHORE)
_VMEM_SPEC = pl.BlockSpec(memory_space=pltpu.VMEM)
_EFFECT = pltpu.SideEffectType.DATAFLOW_SIDE_EFFECTING


def _place():
    return lax.axis_index("x"), lax.axis_index("y"), lax.axis_index("c")


def _other_chips(x, y):
    return [(1 - x, y), (x, 1 - y), (1 - x, 1 - y)]


def _all_gather_halves(wp, name):
    R, C = wp.shape
    half = R // 2
    assert half % 16 == 0

    def body(w_ref, out_ref, ici_send, ici_recv, d2d_send, d2d_recv, local_sem):
        x, y, c = _place()
        me = 2 * x + y
        chips = _other_chips(x, y)
        mine = pl.ds(pl.multiple_of(c * half, 16), half)
        theirs = pl.ds(pl.multiple_of((1 - c) * half, 16), half)
        local = pltpu.make_async_copy(w_ref, out_ref.at[me], local_sem)
        local.start()
        sends = []
        for n, (px, py) in enumerate(chips):
            cp = pltpu.make_async_remote_copy(
                src_ref=w_ref.at[mine], dst_ref=out_ref.at[me, mine], send_sem=ici_send.at[n],
                recv_sem=ici_recv.at[n], device_id=(px, py, c), device_id_type=MESH)
            cp.start()
            sends.append(cp)
        for n, (px, py) in enumerate(chips):
            slot = 2 * px + py
            pltpu.make_async_remote_copy(
                src_ref=w_ref.at[mine], dst_ref=out_ref.at[slot, mine], send_sem=ici_send.at[n],
                recv_sem=ici_recv.at[n], device_id=(px, py, c), device_id_type=MESH).wait_recv()
            cp = pltpu.make_async_remote_copy(
                src_ref=out_ref.at[slot, mine], dst_ref=out_ref.at[slot, mine], send_sem=d2d_send.at[n],
                recv_sem=d2d_recv.at[n], device_id=(x, y, 1 - c), device_id_type=MESH)
            cp.start()
            sends.append(cp)
        for n, (px, py) in enumerate(chips):
            slot = 2 * px + py
            pltpu.make_async_remote_copy(
                src_ref=out_ref.at[slot, theirs], dst_ref=out_ref.at[slot, theirs], send_sem=d2d_send.at[n],
                recv_sem=d2d_recv.at[n], device_id=(x, y, 1 - c), device_id_type=MESH).wait_recv()
        for cp in sends:
            cp.wait_send()
        local.wait()

    return pl.pallas_call(
        body, name=name,
        in_specs=[_ANY_SPEC], out_specs=_ANY_SPEC,
        out_shape=jax.ShapeDtypeStruct((N_CHIPS, R, C), wp.dtype),
        scratch_shapes=[pltpu.SemaphoreType.DMA((3,)), pltpu.SemaphoreType.DMA((3,)), pltpu.SemaphoreType.DMA((3,)),
                        pltpu.SemaphoreType.DMA((3,)), pltpu.SemaphoreType.DMA],
    )(wp)


def _chip_copies(src_ref, land_ref, sems, gather):
    x, y, c = _place()
    me = 2 * x + y
    out, back = [], []
    for n, (px, py) in enumerate(_other_chips(x, y)):
        src = src_ref if gather else src_ref.at[2 * px + py]
        out.append(pltpu.make_async_remote_copy(
            src_ref=src, dst_ref=land_ref.at[me] if gather else land_ref.at[n],
            send_sem=sems[n], recv_sem=sems[3 + n], device_id=(px, py, c), device_id_type=MESH))
        back.append(pltpu.make_async_remote_copy(
            src_ref=src, dst_ref=land_ref.at[2 * px + py] if gather else land_ref.at[n],
            send_sem=sems[n], recv_sem=sems[3 + n], device_id=(px, py, c), device_id_type=MESH))
    return out, back


def _xchg_start(src, land, gather, order, name):
    def body(src_ref, land_ref, order_ref, *outs):
        sems = outs[0:6]
        token = outs[8]
        out, _ = _chip_copies(src_ref, land_ref, sems, gather)
        for cp in out:
            cp.start()
        token[...] = jnp.zeros_like(token)

    outs = pl.pallas_call(
        body, name=name,
        out_shape=(pltpu.SemaphoreType.DMA(()),) * 6 + (
            pltpu.HBM(src.shape, src.dtype), pltpu.HBM(land.shape, land.dtype),
            jax.ShapeDtypeStruct((8, LANE), F32)),
        in_specs=(_HBM_SPEC, _HBM_SPEC, _ANY_SPEC),
        out_specs=(_SEM_SPEC,) * 6 + (_HBM_SPEC, _HBM_SPEC, _VMEM_SPEC),
        input_output_aliases={0: 6, 1: 7},
        compiler_params=pltpu.CompilerParams(has_side_effects=_EFFECT),
    )(pltpu.with_memory_space_constraint(src, pltpu.HBM), pltpu.with_memory_space_constraint(land, pltpu.HBM), order)
    return outs[0:6], outs[6], outs[7], outs[8]


def _xchg_wait(started, gather, after, name):
    sems, src, land, _ = started

    def body(src_ref, land_ref, *rest):
        _, back = _chip_copies(src_ref, land_ref, rest[0:6], gather)
        for cp in back:
            cp.wait_send()
            cp.wait_recv()

    return pl.pallas_call(
        body, name=name,
        out_shape=(pltpu.HBM(src.shape, src.dtype), pltpu.HBM(land.shape, land.dtype)),
        in_specs=(_HBM_SPEC, _HBM_SPEC) + (_SEM_SPEC,) * 6 + (_ANY_SPEC,),
        out_specs=(_HBM_SPEC, _HBM_SPEC),
        input_output_aliases={0: 0, 1: 1},
        compiler_params=pltpu.CompilerParams(has_side_effects=_EFFECT),
    )(src, land, *sems, after)


def _sib_copy(src_ref, land_ref, send_sem, recv_sem):
    x, y, c = _place()
    return pltpu.make_async_remote_copy(src_ref=src_ref, dst_ref=land_ref, send_sem=send_sem, recv_sem=recv_sem,
                                        device_id=(x, y, 1 - c), device_id_type=MESH)


def _sib_start(src, name):
    land = lax.empty(src.shape, src.dtype)

    def body(src_ref, land_ref, send_sem, recv_sem, src_thru, land_thru, token):
        _sib_copy(src_ref, land_ref, send_sem, recv_sem).start()
        token[...] = jnp.zeros_like(token)

    return pl.pallas_call(
        body, name=name,
        out_shape=(pltpu.SemaphoreType.DMA(()), pltpu.SemaphoreType.DMA(()),
                   pltpu.HBM(src.shape, src.dtype), pltpu.HBM(land.shape, land.dtype),
                   jax.ShapeDtypeStruct((8, LANE), F32)),
        in_specs=(_HBM_SPEC, _HBM_SPEC),
        out_specs=(_SEM_SPEC, _SEM_SPEC, _HBM_SPEC, _HBM_SPEC, _VMEM_SPEC),
        input_output_aliases={0: 2, 1: 3},
        compiler_params=pltpu.CompilerParams(has_side_effects=_EFFECT),
    )(pltpu.with_memory_space_constraint(src, pltpu.HBM), pltpu.with_memory_space_constraint(land, pltpu.HBM))


def _sib_wait(started, after, name):
    send_sem, recv_sem, src, land, _ = started

    def body(src_ref, land_ref, send_sem, recv_sem, after_ref, src_out, land_out):
        cp = _sib_copy(src_ref, land_ref, send_sem, recv_sem)
        cp.wait_send()
        cp.wait_recv()

    return pl.pallas_call(
        body, name=name,
        out_shape=(pltpu.HBM(src.shape, src.dtype), pltpu.HBM(land.shape, land.dtype)),
        in_specs=(_HBM_SPEC, _HBM_SPEC, _SEM_SPEC, _SEM_SPEC, _ANY_SPEC),
        out_specs=(_HBM_SPEC, _HBM_SPEC),
        input_output_aliases={0: 0, 1: 1},
        compiler_params=pltpu.CompilerParams(has_side_effects=_EFFECT),
    )(src, land, send_sem, recv_sem, after)


def _sum_slabs(gp, recv, chip, name):
    _, R, C = gp.shape
    tr = _tile(R, PACK_ROWS, 16)

    def body(chip_ref, own_ref, r0_ref, r1_ref, r2_ref, o_ref):
        acc = own_ref[...].astype(F32) + r0_ref[...].astype(F32)
        o_ref[...] = (acc + r1_ref[...].astype(F32)) + r2_ref[...].astype(F32)

    def got(n):
        return pl.BlockSpec((None, tr, C), lambda i, chip_ref: (n, i, 0))

    return pl.pallas_call(
        body, name=name,
        grid_spec=pltpu.PrefetchScalarGridSpec(
            num_scalar_prefetch=1, grid=(R // tr,),
            in_specs=[pl.BlockSpec((None, tr, C), lambda i, chip_ref: (chip_ref[0], i, 0)), got(0), got(1), got(2)],
            out_specs=pl.BlockSpec((tr, C), lambda i, chip_ref: (i, 0))),
        out_shape=jax.ShapeDtypeStruct((R, C), F32),
        compiler_params=_params(("parallel",)),
    )(jnp.reshape(chip, (1,)).astype(jnp.int32), gp, recv, recv, recv)


def _all_reduce_vec(vec, name):
    VR, W = vec.shape

    def body(vec_ref, vall_ref, vout_ref, vsend_sems, vrecv_sems):
        x, y, c = _place()
        vall_ref[4 * x + 2 * y + c] = vec_ref[...]
        sends = []
        peers = []
        for r in range(1, N_DEV):
            dx, dy, dc = (r >> 2) & 1, (r >> 1) & 1, r & 1
            peer = (x ^ dx, y ^ dy, c ^ dc)
            peers.append(peer)
            cp = pltpu.make_async_remote_copy(
                src_ref=vec_ref, dst_ref=vall_ref.at[4 * x + 2 * y + c], send_sem=vsend_sems.at[r - 1],
                recv_sem=vrecv_sems.at[r - 1], device_id=peer, device_id_type=MESH)
            cp.start()
            sends.append(cp)
        for r, peer in enumerate(peers):
            pltpu.make_async_remote_copy(
                src_ref=vec_ref, dst_ref=vall_ref.at[4 * peer[0] + 2 * peer[1] + peer[2]],
                send_sem=vsend_sems.at[r], recv_sem=vrecv_sems.at[r],
                device_id=peer, device_id_type=MESH).wait_recv()
        total = vall_ref[0]
        for d in range(1, N_DEV):
            total = total + vall_ref[d]
        vout_ref[...] = total
        for cp in sends:
            cp.wait_send()

    outs = pl.pallas_call(
        body, name=name,
        in_specs=[_VMEM_SPEC], out_specs=[_VMEM_SPEC, _VMEM_SPEC],
        out_shape=[jax.ShapeDtypeStruct((N_DEV, VR, W), F32), jax.ShapeDtypeStruct((VR, W), F32)],
        scratch_shapes=[pltpu.SemaphoreType.DMA((N_DEV - 1,)), pltpu.SemaphoreType.DMA((N_DEV - 1,))],
    )(vec)
    return outs[1]


class _Pack:
    def __init__(self, group, C):
        self.group, self.C = group, C
        self.rows, self.offs, off = {}, {}, 0
        for nm, (K, N), _ in group:
            assert N <= C, nm
            self.rows[nm] = K if 2 * N > C else -(-(K * N) // C)
            self.offs[nm] = off
            off += -(-self.rows[nm] // 16) * 16
        self.used = off
        self.R = -(-off // PACK_ROWS) * PACK_ROWS

    def _rows_of(self, a):
        K, N = a.shape
        if 2 * N > self.C:
            a = jnp.pad(a, ((0, 0), (0, self.C - N)))
        else:
            a = jnp.pad(a.reshape(-1), (0, -(K * N) % self.C)).reshape(-1, self.C)
        return jnp.pad(a, ((0, -a.shape[0] % 16), (0, 0)))

    def pack(self, shards):
        parts = [self._rows_of(shards[nm].astype(BF16)) for nm, _, _ in self.group]
        return jnp.concatenate(parts + [jnp.zeros((self.R - self.used, self.C), BF16)], axis=0)

    def _shard_of(self, rows, shape):
        K, N = shape
        return rows[:, :N] if 2 * N > self.C else rows.reshape(-1)[:K * N].reshape(K, N)

    def part(self, flat, nm, shape):
        return self._shard_of(flat[self.offs[nm]:self.offs[nm] + self.rows[nm]], shape)

    def slab_rows(self, nm, g):
        (K, N), axis = next((shape, axis) for n, shape, axis in self.group if n == nm)
        cuts = [g[:, k * N:(k + 1) * N] if axis == 1 else g[k * K:(k + 1) * K, :] for k in range(N_CHIPS)]
        return jnp.stack([self._rows_of(c.astype(BF16)) for c in cuts])

    def slabs(self, grads):
        parts = [self.slab_rows(nm, grads[nm]) for nm, _, _ in self.group]
        return jnp.concatenate(parts + [jnp.zeros((N_CHIPS, self.R - self.used, self.C), BF16)], axis=1)

    def full(self, gathered, names=None):
        res = {}
        for nm, (K, N), axis in self.group:
            if names is None or nm in names:
                rows = gathered[:, self.offs[nm]:self.offs[nm] + self.rows[nm]]
                res[nm] = jnp.concatenate([self._shard_of(rows[k], (K, N)) for k in range(N_CHIPS)], axis=axis)
        return res


def _rope_tables(S):
    pos = jnp.arange(S, dtype=F32)
    inv = 1.0 / (ROPE_THETA ** (jnp.arange(0, MLA_ROPE, 2, dtype=F32) / MLA_ROPE))
    ang = pos[:, None] * inv[None, :]
    cos, sin = jnp.cos(ang), jnp.sin(ang)
    half = MLA_ROPE // 2
    z = jnp.zeros((S, half), F32)
    one = jnp.ones((S, LANE - MLA_ROPE), F32)
    zero = jnp.zeros((S, LANE - MLA_ROPE), F32)
    kc = jnp.concatenate([cos, cos, one], axis=1)
    ksa = jnp.concatenate([-sin, z, zero], axis=1)
    ksb = jnp.concatenate([z, sin, zero], axis=1)
    qc = jnp.concatenate([jnp.ones((S, MLA_NOPE), F32), kc], axis=1)
    qsa = jnp.concatenate([jnp.zeros((S, MLA_NOPE), F32), ksa], axis=1)
    qsb = jnp.concatenate([jnp.zeros((S, MLA_NOPE), F32), ksb], axis=1)
    return (kc, ksa, ksb), (qc, qsa, qsb)


def _pad_cols(a, width):
    return jnp.pad(a, ((0, 0), (0, width - a.shape[1])))


def kernel(x, attn_norm, w_in, fox_f_bias, q_norm, w_uq, kv_norm, w_ukv, w_mla_branch, w_fox_branch, w_out, mlp_norm, w_up, w_down, final_norm, loss_target, m_attn_norm, m_w_in, m_fox_f_bias, m_q_norm, m_w_uq, m_kv_norm, m_w_ukv, m_w_mla_branch, m_w_fox_branch, m_w_out, m_mlp_norm, m_w_up, m_w_down, m_final_norm, v_attn_norm, v_w_in, v_fox_f_bias, v_q_norm, v_w_uq, v_kv_norm, v_w_ukv, v_w_mla_branch, v_w_fox_branch, v_w_out, v_mlp_norm, v_w_up, v_w_down, v_final_norm):
    _, S, D = x.shape
    H, HF = MLA_HEADS, FOX_HEADS
    QL, KVL = MLA_Q_LORA, MLA_KV_LORA
    assert H == HF and H <= 8
    xs = x[0]
    target = loss_target[0]
    C = D
    chip = 2 * lax.axis_index("x") + lax.axis_index("y")

    def flip(a):
        return jnp.transpose(a, (0, 2, 1))

    w_in, m_w_in, v_w_in = flip(w_in), flip(m_w_in), flip(v_w_in)
    weights = {"attn_norm": attn_norm, "w_in": w_in, "fox_f_bias": fox_f_bias, "q_norm": q_norm, "w_uq": w_uq,
               "kv_norm": kv_norm, "w_ukv": w_ukv, "w_mla_branch": w_mla_branch, "w_fox_branch": w_fox_branch,
               "w_out": w_out, "mlp_norm": mlp_norm, "w_up": w_up, "w_down": w_down, "final_norm": final_norm}
    moments = {"attn_norm": (m_attn_norm, v_attn_norm), "w_in": (m_w_in, v_w_in), "fox_f_bias": (m_fox_f_bias, v_fox_f_bias),
               "q_norm": (m_q_norm, v_q_norm), "w_uq": (m_w_uq, v_w_uq), "kv_norm": (m_kv_norm, v_kv_norm),
               "w_ukv": (m_w_ukv, v_w_ukv), "w_mla_branch": (m_w_mla_branch, v_w_mla_branch),
               "w_fox_branch": (m_w_fox_branch, v_w_fox_branch), "w_out": (m_w_out, v_w_out),
               "mlp_norm": (m_mlp_norm, v_mlp_norm), "w_up": (m_w_up, v_w_up), "w_down": (m_w_down, v_w_down),
               "final_norm": (m_final_norm, v_final_norm)}

    def group(names_axes):
        return [(nm, weights[nm].shape[1:], axis) for nm, axis in names_axes]

    pack_a = _Pack(group([("w_in", 0), ("w_uq", 1), ("w_ukv", 1)]), C)
    pack_b = _Pack(group([("w_down", 0), ("w_up", 1), ("w_out", 0), ("w_mla_branch", 1), ("w_fox_branch", 1)]), C)
    RA, RB = pack_a.R, pack_b.R
    wp_a = pack_a.pack({nm: weights[nm][0] for nm, _, _ in pack_a.group})
    wp_b = pack_b.pack({nm: weights[nm][0] for nm, _, _ in pack_b.group})
    n_in = w_in.shape[1]
    rows_in = -(-n_in // 16) * 16
    assert pack_a.offs["w_in"] == 0 and all((k * n_in) % 16 + n_in <= rows_in for k in range(N_CHIPS))
    shifted = lax.dynamic_update_slice(jnp.zeros((rows_in, C), BF16), wp_a[:n_in], ((chip * n_in) % 16, 0))
    wp_a = jnp.concatenate([shifted, wp_a[rows_in:]], axis=0)
    gathered_a = _all_gather_halves(wp_a, "all_gather_a")
    ag_b = _xchg_start(wp_b, lax.empty((N_CHIPS, RB, C), BF16), True, gathered_a, "all_gather_start_b")
    xn = _norm_fwd(xs, attn_norm, "attn_norm_fwd", order=ag_b[3])
    full = pack_a.full(gathered_a, ("w_uq", "w_ukv"))
    tile0 = [(k * n_in) // 16 * 16 for k in range(N_CHIPS)]
    total = tile0[-1] + rows_in
    full["w_in"] = sum(jnp.pad(gathered_a[k, :rows_in], ((tile0[k], total - tile0[k] - rows_in), (0, 0)))
                       for k in range(N_CHIPS))

    o_ckv = QL
    o_kr = o_ckv + KVL
    o_fq = o_kr + MLA_ROPE
    o_ff = o_fq + 3 * HF * FOX_HEAD_DIM
    o_g = o_ff + HF
    wi = full["w_in"]
    assert N_CHIPS * n_in == o_g + 2 * D and wi.shape[0] >= o_g + 2 * D
    WS = QL + KVL + 2 * LANE
    NQKV = 3 * HF * FOX_HEAD_DIM

    def pad_rows(a, rows):
        return jnp.pad(a, ((0, rows - a.shape[0]), (0, 0)))

    w_small = jnp.concatenate([wi[:o_kr], pad_rows(wi[o_kr:o_fq], LANE), pad_rows(wi[o_ff:o_g], LANE)], axis=0)
    w_qkv = wi[o_fq:o_ff]
    w_g = wi[o_g:o_g + 2 * D]
    w_pack = jnp.concatenate([w_small, w_qkv, w_g], axis=0)
    dqk = MLA_NOPE + MLA_ROPE
    w_uq_p = jnp.pad(full["w_uq"].reshape(QL, H, dqk), ((0, 0), (0, 0), (0, QPAD - dqk))).reshape(QL, H * QPAD)
    ukv = full["w_ukv"].reshape(KVL, H, MLA_NOPE + MLA_V)
    w_ukv_p = jnp.concatenate([ukv[:, :, :MLA_NOPE].reshape(KVL, H * MLA_NOPE),
                               ukv[:, :, MLA_NOPE:].reshape(KVL, H * MLA_V)], axis=1)

    (kc, ksa, ksb), (qc, qsa, qsb) = _rope_tables(S)
    bias_pad = _pad_cols(fox_f_bias, LANE)

    small = _matmul(xn, w_small, "nt", [F32], "proj_small")
    n_fq = HF * FOX_HEAD_DIM
    q_scale = jnp.concatenate([jnp.full((1, n_fq), LOG2E / math.sqrt(FOX_HEAD_DIM), F32),
                               jnp.ones((1, NQKV - n_fq), F32)], axis=1)
    qkv = _matmul(xn, w_qkv, "nt", [BF16], "proj_qkv", col_extras=(q_scale,), epilogue=lambda acc, cs: (acc * cs,))
    gpre = _matmul(xn, w_g, "nt", [F32], "proj_gates")
    cqn, ckvn, kr, cum = _prep_fwd(small, q_norm, kv_norm, bias_pad, kc, ksa, ksb, HF, "prep_fwd")
    c2_mla = LOG2E / math.sqrt(dqk)
    q_rot = _matmul(cqn, w_uq_p, "nn", [BF16], "mla_q_up", tn=QPAD, row_extras=(qc * c2_mla, qsa * c2_mla, qsb * c2_mla),
                    epilogue=lambda acc, c, sa, sb: (_rope(acc, c, sa, sb, 1),))
    kv2 = _matmul(ckvn, w_ukv_p, "nn", [BF16], "mla_kv_up")

    def mla_att(block):
        return _AttT(S, H, (q_rot, QPAD, 0, True), [(kv2, MLA_NOPE, 0, True), (kr, LANE, 0, False)],
                     (kv2, MLA_V, H, True), 1.0 / math.sqrt(dqk), True, block=block)

    mla = mla_att(ATT_T)
    o_mla, lse_mla = _att_fwd_t(mla_att(ATT_T_FWD), "mla_att_fwd")

    cum_t = jnp.transpose(cum[:, :HF]) * LOG2E
    cum_rep = jnp.broadcast_to(cum_t[:, :, None], (HF, S, min(QSUB, _tile(S, ATT_T))))

    def fox_att(block):
        return _AttT(S, HF, (qkv, FOX_HEAD_DIM, 0, True), [(qkv, FOX_HEAD_DIM, HF, True)],
                     (qkv, FOX_HEAD_DIM, 2 * HF, True), 1.0 / math.sqrt(FOX_HEAD_DIM), False, cum_rep, block=block)

    fox = fox_att(ATT_T)
    o_fox, ox_fox, lse_fox = _att_fwd_t(fox_att(ATT_T_FWD), "fox_att_fwd", exact=True)

    own_b, land_b = _xchg_wait(ag_b, True, lse_fox, "all_gather_wait_b")
    gathered_b = lax.dynamic_update_slice(land_b, own_b[None], (chip, 0, 0))
    full.update(pack_b.full(gathered_b, ("w_mla_branch", "w_fox_branch", "w_out")))
    w_mb, w_fb, w_o = (full[n] for n in ("w_mla_branch", "w_fox_branch", "w_out"))

    def b_of(nm, mode, tn, tk):
        (K, N), axis = next((shape, axis) for n, shape, axis in pack_b.group if n == nm)
        off = pack_b.offs[nm]
        shape = (N_CHIPS * K, N) if axis == 0 else (K, N_CHIPS * N)
        t_r, t_c = (tk, tn) if mode == "nn" else (tn, tk)
        t_r, t_c = _tile(shape[0], t_r), _tile(shape[1], t_c)
        if not (N == C and K % t_r == 0 and N % t_c == 0 and off % t_r == 0):
            return pack_b.full(gathered_b, (nm,))[nm], None
        base = off // t_r
        if axis == 0:
            per = K // t_r
            place = lambda rb, cb: (rb // per, base + rb % per, cb)
        else:
            per = N // t_c
            place = lambda rb, cb: (cb // per, base + rb, cb % per)
        return gathered_b, (shape, (lambda j, k: place(k, j)) if mode == "nn" else (lambda j, k: place(j, k)))

    y_mla = _matmul(o_mla, w_mb, "nn", [F32], "mla_branch")

    def gate_merge(acc, ga, gb, ya):
        return acc, _sigmoid(ga) * ya + _sigmoid(gb) * acc

    y_fox, merged = _matmul(o_fox, w_fb, "nn", [F32, BF16], "fox_branch_gates", extras=((gpre, 0), (gpre, 1), y_mla),
                            epilogue=gate_merge)
    h1 = _matmul(merged, w_o, "nn", [F32], "out_proj", extras=(xs,), epilogue=lambda acc, r: (acc + r,))
    hn = _norm_fwd(h1, mlp_norm, "mlp_norm_fwd")

    def relu2(acc):
        a = jnp.maximum(acc, 0.0)
        return a * a, a

    w_u, w_u_in = b_of("w_up", "nn", 512, 2048)
    u, a_pos = _matmul(hn, w_u, "nn", [BF16, BF16], "mlp_up", epilogue=relu2, b_in=w_u_in)
    w_d, w_d_in = b_of("w_down", "nn", 1024, 2048)
    h2 = _matmul(u, w_d, "nn", [F32], "mlp_down", tn=1024, extras=(h1,), epilogue=lambda acc, r: (acc + r,),
                 b_in=w_d_in)
    dh2, dh2_b, g_final, loss_part = _final(h2, final_norm.reshape(1, D), target, "final_norm_loss")

    gp_b = lax.empty((N_CHIPS, RB, C), BF16)
    by_glue = {}

    def grad_b(nm, a, b, name):
        nonlocal gp_b
        (K, N), axis = next((shape, axis) for n, shape, axis in pack_b.group if n == nm)
        off = pack_b.offs[nm]
        tm = min(1024, K) if axis == 0 else min(1024, a.shape[1])
        tn = min(1024, N) if axis == 1 else min(1024, b.shape[1])
        if not (N == C and tm % LANE == 0 and tn % LANE == 0 and K % tm == 0 and N % tn == 0 and off % tm == 0):
            by_glue[nm] = _mm_tn(a, b, name)
            return
        base = off // tm
        if axis == 0:
            per = K // tm
            place = lambda i, j: (i // per, base + i % per, j)
        else:
            per = N // tn
            place = lambda i, j: (j // per, base + i, j % per)
        gp_b = _mm_tn(a, b, name, tm=tm, tn=tn, into=(gp_b, place))

    w_d, w_d_in = b_of("w_down", "nt", 512, 2048)
    da = _matmul(dh2_b, w_d, "nt", [BF16], "mlp_down_dx", extras=(a_pos,),
                 epilogue=lambda acc, a: (acc * (2.0 * a.astype(F32)),), b_in=w_d_in)
    grad_b("w_down", u, dh2_b, "mlp_down_dw")
    w_u, w_u_in = b_of("w_up", "nt", 1024, 2048)
    dhn = _matmul(da, w_u, "nt", [F32], "mlp_up_dx", tn=1024, b_in=w_u_in)
    grad_b("w_up", hn, da, "mlp_up_dw")
    dh1, dh1_b, g_mlp_norm = _norm_bwd(h1, dhn, mlp_norm, dh2, "mlp_norm_bwd")

    def gate_bwd(acc, ga, gb, ya, yb):
        ga, gb = _sigmoid(ga), _sigmoid(gb)
        return acc * ga, acc * gb, acc * ya * (ga * (1.0 - ga)), acc * yb * (gb * (1.0 - gb))

    dy_mla, dy_fox, dg_mla, dg_fox = _matmul(dh1_b, w_o, "nt", [BF16] * 4, "out_proj_dx_gates",
                                             extras=((gpre, 0), (gpre, 1), y_mla, y_fox), epilogue=gate_bwd)
    grad_b("w_out", merged, dh1_b, "out_proj_dw")
    do_mla = _matmul(dy_mla, w_mb, "nt", [BF16], "mla_branch_dx")
    grad_b("w_mla_branch", o_mla, dy_mla, "mla_branch_dw")
    do_fox = _matmul(dy_fox, w_fb, "nt", [BF16], "fox_branch_dx")
    grad_b("w_fox_branch", o_fox, dy_fox, "fox_branch_dw")
    for nm, g in by_glue.items():
        gp_b = lax.dynamic_update_slice(gp_b, pack_b.slab_rows(nm, g), (0, pack_b.offs[nm], 0))
    if RB > pack_b.used:
        gp_b = lax.dynamic_update_slice(gp_b, jnp.zeros((N_CHIPS, RB - pack_b.used, C), BF16), (0, pack_b.used, 0))

    rs_b = _xchg_start(gp_b, lax.empty((3, RB, C), BF16), False, do_fox, "grad_scatter_start_b")

    delta_mla = _att_delta_t(do_mla, o_mla, H, "mla_att_delta", order=rs_b[3])
    dq_rot, dk_nope, dkr_heads, dv_mla = _att_bwd_t(mla, do_mla, lse_mla, delta_mla, BF16, [BF16, F32],
                                                    "mla_att_bwd", dq_rope=(qc, qsa, qsb))
    delta_fox = _att_delta_t(do_fox, ox_fox, HF, "fox_att_delta")
    dfq, dfk, dfv, dcum = _att_bwd_t(fox, do_fox, lse_fox, delta_fox, BF16, [BF16], "fox_att_bwd")

    gp_b_sent, recv_b = _xchg_wait(rs_b, False, dfq, "grad_scatter_wait_b")
    swap_b = _sib_start(_sum_slabs(gp_b_sent, recv_b, chip, "grad_sum_b"), "grad_swap_start_b")

    dcqn = _matmul(dq_rot, w_uq_p, "nt", [F32], "mla_q_up_dx", order=swap_b[4])
    g_w_uq_p = _mm_tn(cqn, dq_rot, "mla_q_up_dw")
    dkv2 = jnp.concatenate([dk_nope, dv_mla], axis=1)
    dckvn = _matmul(dkv2, w_ukv_p, "nt", [F32], "mla_kv_up_dx")
    g_w_ukv_p = _mm_tn(ckvn, dkv2, "mla_kv_up_dw")

    dcum_rows = jnp.pad(dcum[:, :, 0], ((0, 8 - HF), (0, 0)))
    dlogf_rows = _suffix_sum_rows(dcum_rows, "fox_forget_suffix_sum")
    dlogf = _pad_cols(jnp.transpose(dlogf_rows[:HF]), LANE)
    d_small, g_q_norm, g_kv_norm, g_bias = _prep_bwd(
        small, dcqn, dckvn, dkr_heads, dlogf, q_norm, kv_norm, bias_pad, kc, ksa, ksb, H, "prep_bwd")
    dproj = [d_small, dfq, dfk, dfv, dg_mla, dg_fox]
    gs, gfq, gfk, gfv, gg_mla, gg_fox = [
        _matmul(part, xn, "tn", [BF16], "proj_dw_" + tag, tm=1024, tn=1024, tk=2048)
        for part, tag in zip(dproj, ("small", "fq", "fk", "fv", "g_mla", "g_fox"))]

    g_w_in = jnp.concatenate([gs[:o_kr], gs[o_kr:o_kr + MLA_ROPE], gfq, gfk, gfv,
                              gs[o_kr + LANE:o_kr + LANE + HF], gg_mla, gg_fox], axis=0)
    g_w_uq = g_w_uq_p.reshape(QL, H, QPAD)[:, :, :dqk].reshape(QL, H * dqk)
    g_w_ukv = jnp.concatenate([g_w_ukv_p[:, :H * MLA_NOPE].reshape(KVL, H, MLA_NOPE),
                               g_w_ukv_p[:, H * MLA_NOPE:].reshape(KVL, H, MLA_V)], axis=2).reshape(KVL, -1)

    gp_a = pack_a.slabs({"w_in": g_w_in, "w_uq": g_w_uq, "w_ukv": g_w_ukv})
    rs_a = _xchg_start(gp_a, lax.empty((3, RA, C), BF16), False, gg_fox, "grad_scatter_start_a")
    dxn = _matmul_parts(dproj, w_pack, "nn", F32, "proj_dx", order=rs_a[3])
    grad_x, _, g_attn_norm = _norm_bwd(xs, dxn, attn_norm, dh1, "attn_norm_bwd")
    gp_a_sent, recv_a = _xchg_wait(rs_a, False, grad_x, "grad_scatter_wait_a")
    swap_a = _sib_start(_sum_slabs(gp_a_sent, recv_a, chip, "grad_sum_a"), "grad_swap_start_a")
    vec_w = max(D, LANE)
    vec_rows = [g_attn_norm, g_mlp_norm, g_final, g_q_norm, g_kv_norm, g_bias, loss_part]
    vec = jnp.concatenate([_pad_cols(v, vec_w) for v in vec_rows] + [jnp.zeros((1, vec_w), F32)], axis=0)
    vsum = _all_reduce_vec(vec, "all_reduce_vectors")
    part_b, sib_b = _sib_wait(swap_b, vsum, "grad_swap_wait_b")

    grads, deltas, new_m, new_v = {}, {}, {}, {}

    def update(pack, mine, theirs):
        for nm, shape, _ in pack.group:
            K, N = shape
            if N == pack.C and K % 8 == 0 and pack.offs[nm] % _tile(K, 256, 8) == 0:
                g, d, nm_, nv_ = _adamw(weights[nm], mine, theirs, moments[nm][0], moments[nm][1], "adamw_" + nm,
                                        g_row=pack.offs[nm])
            else:
                g, d, nm_, nv_ = _adamw(weights[nm], pack.part(mine, nm, shape), pack.part(theirs, nm, shape),
                                        moments[nm][0], moments[nm][1], "adamw_" + nm)
            grads[nm], deltas[nm], new_m[nm], new_v[nm] = g, d, nm_, nv_
        return g

    last_b = update(pack_b, part_b, sib_b)
    part_a, sib_a = _sib_wait(swap_a, last_b, "grad_swap_wait_a")
    update(pack_a, part_a, sib_a)

    vec_names = ["attn_norm", "mlp_norm", "final_norm", "q_norm", "kv_norm", "fox_f_bias"]

    def vec_pack(arrs):
        return jnp.concatenate([_pad_cols(a.reshape(1, -1), vec_w) for a in arrs]
                               + [jnp.zeros((2, vec_w), F32)], axis=0)[None]

    vg, vd, vm, vv = _adamw(vec_pack([weights[n] for n in vec_names]), vsum, jnp.zeros_like(vsum),
                            vec_pack([moments[n][0] for n in vec_names]), vec_pack([moments[n][1] for n in vec_names]),
                            "adamw_vectors")
    for r, nm in enumerate(vec_names):
        shp = weights[nm].shape
        n = weights[nm].size
        grads[nm] = vsum[r, :n].reshape(shp)
        deltas[nm], new_m[nm], new_v[nm] = (vd[0, r, :n].reshape(shp), vm[0, r, :n].reshape(shp),
                                            vv[0, r, :n].reshape(shp))
    loss = vsum[6, 0]

    for res in (grads, deltas, new_m, new_v):
        res["w_in"] = flip(res["w_in"])
    order = ["attn_norm", "w_in", "fox_f_bias", "q_norm", "w_uq", "kv_norm", "w_ukv", "w_mla_branch", "w_fox_branch",
             "w_out", "mlp_norm", "w_up", "w_down", "final_norm"]
    return (loss, grad_x[None], *[grads[n] for n in order], *[deltas[n] for n in order],
            *[new_m[n] for n in order], *[new_v[n] for n in order])
```

```python
import math

import jax
import jax.numpy as jnp
from jax import lax
from jax.experimental import pallas as pl
from jax.experimental.pallas import tpu as pltpu

CHUNK = 64
MLA_HEADS = 8
MLA_Q_LORA = 512
MLA_KV_LORA = 256
MLA_NOPE = 128
MLA_ROPE = 64
MLA_V = 128
ROPE_THETA = 10000.0
FOX_HEADS = 8
FOX_HEAD_DIM = 128
EPS = 1e-6

ADAM_LR = 0.001
ADAM_B1 = 0.9
ADAM_B2 = 0.999
ADAM_EPS = 1e-08
ADAM_WD = 0.01
ADAM_STEP = 10

LANE = 128
QPAD = 2 * LANE
N_CHIPS = 4
N_DEV = 8
VMEM_LIMIT = 48 * 1024 * 1024
ATT_T = 2048
QSUB = 256
ROW_T = 256
PACK_ROWS = 256
LOG2E = 1.4426950408889634

BF16 = jnp.bfloat16
F32 = jnp.float32
MESH = pl.DeviceIdType.MESH

_NT = (((1,), (1,)), ((), ()))
_TN = (((0,), (0,)), ((), ()))
_NN = (((1,), (0,)), ((), ()))


def _tile(dim, pref, align=LANE):
    if dim <= pref:
        return dim
    t = (pref // align) * align
    while t >= align:
        if dim % t == 0:
            return t
        t -= align
    return dim


def _params(sem=None):
    return pltpu.CompilerParams(dimension_semantics=sem, vmem_limit_bytes=VMEM_LIMIT)


_ANY_SPEC = pl.BlockSpec(memory_space=pl.ANY)


def _matmul(a, b, mode, out_dtypes, name, *, tm=1024, tn=1024, tk=2048, extras=(), row_extras=(), col_extras=(),
            epilogue=None, order=None, into=None, b_in=None):
    b_shape = b.shape if b_in is None else b_in[0]
    if mode == "nn":
        (M, K), (K2, N) = a.shape, b_shape
    elif mode == "nt":
        (M, K), (N, K2) = a.shape, b_shape
    else:
        (K, M), (K2, N) = a.shape, b_shape
    assert K == K2, (name, a.shape, b_shape)
    tm, tn, tk = _tile(M, tm), _tile(N, tn), _tile(K, tk)
    nk = K // tk
    extras = [e if isinstance(e, tuple) else (e, 0) for e in extras]
    n_out = len(out_dtypes)
    n_ex = len(extras) + len(row_extras) + len(col_extras)
    n_ord = 0 if order is None else 1
    assert all(r.shape == (M, tn) for r in row_extras), name
    dims = {"nn": _NN, "nt": _NT, "tn": _TN}[mode]

    def body(*refs):
        a_ref, b_ref = refs[0], refs[1]
        ex_refs = refs[2:2 + n_ex]
        o_refs = refs[2 + n_ex + n_ord:2 + n_ex + n_ord + n_out]
        acc_ref = refs[2 + n_ex + n_ord + n_out]
        k = pl.program_id(2)
        part = lax.dot_general(a_ref[...], b_ref[...], dims, preferred_element_type=F32)

        @pl.when(k == 0)
        def _():
            acc_ref[...] = part

        @pl.when(k > 0)
        def _():
            acc_ref[...] += part

        @pl.when(k == nk - 1)
        def _():
            acc = acc_ref[...]
            if epilogue is None:
                outs = (acc,)
            else:
                outs = epilogue(acc, *[r[...] for r in ex_refs])
            for o_ref, o in zip(o_refs, outs):
                o_ref[...] = o.astype(o_ref.dtype)

    if mode == "nn":
        a_spec = pl.BlockSpec((tm, tk), lambda i, j, k: (i, k))
        b_spec = pl.BlockSpec((tk, tn), lambda i, j, k: (k, j))
    elif mode == "nt":
        a_spec = pl.BlockSpec((tm, tk), lambda i, j, k: (i, k))
        b_spec = pl.BlockSpec((tn, tk), lambda i, j, k: (j, k))
    else:
        a_spec = pl.BlockSpec((tk, tm), lambda i, j, k: (k, i))
        b_spec = pl.BlockSpec((tk, tn), lambda i, j, k: (k, j))
    if b_in is not None:
        b_block = (None, tn, tk) if mode == "nt" else (None, tk, tn)
        b_spec = pl.BlockSpec(b_block, lambda i, j, k: b_in[1](j, k))
    mn_spec = pl.BlockSpec((tm, tn), lambda i, j, k: (i, j))
    row_spec = pl.BlockSpec((tm, tn), lambda i, j, k: (i, 0))
    col_spec = pl.BlockSpec((1, tn), lambda i, j, k: (0, j))
    out_specs = [mn_spec] * n_out
    out_shape = [jax.ShapeDtypeStruct((M, N), dt) for dt in out_dtypes]
    aliases = {}
    if into is not None:
        buf, place = into
        assert n_out == 1 and n_ord == 1 and order is buf, name
        out_specs = [pl.BlockSpec((None, tm, tn), lambda i, j, k: place(i, j))]
        out_shape = [jax.ShapeDtypeStruct(buf.shape, buf.dtype)]
        aliases = {2 + n_ex: 0}
    outs = pl.pallas_call(
        body,
        name=name,
        grid=(M // tm, N // tn, nk),
        in_specs=([a_spec, b_spec]
                  + [pl.BlockSpec((tm, tn), lambda i, j, k, g=g: (i, j + g * (N // tn))) for _, g in extras]
                  + [row_spec] * len(row_extras) + [col_spec] * len(col_extras) + [_ANY_SPEC] * n_ord),
        out_specs=out_specs,
        out_shape=out_shape,
        scratch_shapes=[pltpu.VMEM((tm, tn), F32)],
        input_output_aliases=aliases,
        compiler_params=_params(("parallel", "parallel", "arbitrary")),
    )(a, b, *[e for e, _ in extras], *row_extras, *col_extras, *([] if order is None else [order]))
    return outs[0] if n_out == 1 else outs


def _matmul_parts(parts, b, mode, out_dtype, name, *, tm=1024, tn=1024, tk=1024, order=None):
    assert mode in ("nn", "tn")
    if mode == "nn":
        M, (K, N) = parts[0].shape[0], b.shape
        widths = [p.shape[1] for p in parts]
    else:
        K, N = b.shape
        widths = [p.shape[1] for p in parts]
        M = sum(widths)
    common = math.gcd(*widths)
    tm, tn, tk = _tile(M if mode == "nn" else common, tm), _tile(N, tn), _tile(common if mode == "nn" else K, tk)
    t_part = tk if mode == "nn" else tm
    assert sum(widths) == (K if mode == "nn" else M), name
    if any(w % t_part for w in widths):
        parts, widths = [jnp.concatenate(parts, axis=1)], [sum(widths)]
    lo =[sum(widths[:p]) // t_part for p in range(len(parts))]
    cnt = [w // t_part for w in widths]
    nk = K // tk
    n_parts = len(parts)
    n_ord = 0 if order is None else 1
    dims = _NN if mode == "nn" else _TN

    def body(*refs):
        a_refs = refs[0:n_parts]
        b_ref = refs[n_parts]
        o_ref, acc_ref = refs[n_parts + 1 + n_ord], refs[n_parts + 2 + n_ord]
        i, k = pl.program_id(0), pl.program_id(2)
        sel = k if mode == "nn" else i
        for p in range(n_parts):
            @pl.when((sel >= lo[p]) & (sel < lo[p] + cnt[p]))
            def _(p=p):
                part = lax.dot_general(a_refs[p][...], b_ref[...], dims, preferred_element_type=F32)

                @pl.when(k == 0)
                def _():
                    acc_ref[...] = part

                @pl.when(k > 0)
                def _():
                    acc_ref[...] += part

        @pl.when(k == nk - 1)
        def _():
            o_ref[...] = acc_ref[...].astype(o_ref.dtype)

    def a_spec(p):
        if mode == "nn":
            return pl.BlockSpec((tm, tk), lambda i, j, k: (i, jnp.clip(k - lo[p], 0, cnt[p] - 1)))
        return pl.BlockSpec((tk, tm), lambda i, j, k: (
            jnp.where((i >= lo[p]) & (i < lo[p] + cnt[p]), k, 0), jnp.clip(i - lo[p], 0, cnt[p] - 1)))

    return pl.pallas_call(
        body, name=name, grid=(M // tm, N // tn, nk),
        in_specs=[a_spec(p) for p in range(n_parts)] + [pl.BlockSpec((tk, tn), lambda i, j, k: (k, j))]
        + [_ANY_SPEC] * n_ord,
        out_specs=pl.BlockSpec((tm, tn), lambda i, j, k: (i, j)),
        out_shape=jax.ShapeDtypeStruct((M, N), out_dtype),
        scratch_shapes=[pltpu.VMEM((tm, tn), F32)],
        compiler_params=_params(("parallel", "parallel", "arbitrary")),
    )(*parts, b, *([] if order is None else [order]))


def _mm_tn(a, b, name, tm=1024, tn=1024, into=None):
    return _matmul(a, b, "tn", [F32], name, tm=tm, tn=tn, tk=2048, into=into,
                   order=None if into is None else into[0])


def _row_spec(ts, width, col=0):
    return pl.BlockSpec((ts, width), lambda i: (i, col))


def _full_spec(shape):
    return pl.BlockSpec(shape, lambda i: tuple(0 for _ in shape))


def _rms(x):
    return lax.rsqrt(jnp.mean(x * x, axis=-1, keepdims=True) + EPS)


def _rms_bwd(x, dy, g):
    r = _rms(x)
    xh = x * r
    gy = dy * g
    dx = r * (gy - xh * jnp.mean(xh * gy, axis=-1, keepdims=True))
    return dx, dy * xh


def _norm_fwd(x, g, name, order=None):
    S, D = x.shape
    ts = _tile(S, ROW_T, 8)

    def body(x_ref, g_ref, *rest):
        o_ref = rest[-1]
        xv = x_ref[...]
        o_ref[...] = ((xv * _rms(xv)) * g_ref[...]).astype(BF16)

    extra = [] if order is None else [order]
    return pl.pallas_call(
        body, name=name, grid=(S // ts,),
        in_specs=[_row_spec(ts, D), _full_spec((1, D))] + [_ANY_SPEC] * len(extra),
        out_specs=_row_spec(ts, D),
        out_shape=jax.ShapeDtypeStruct((S, D), BF16),
        compiler_params=_params(("parallel",)),
    )(x, g, *extra)


def _norm_bwd(x, dy, g, dres, name):
    S, D = x.shape
    ts = _tile(S, ROW_T, 8)

    def body(x_ref, dy_ref, g_ref, dres_ref, dx_ref, dxb_ref, dg_ref):
        dx, dg_rows = _rms_bwd(x_ref[...], dy_ref[...], g_ref[...])
        dx = dres_ref[...] + dx
        dx_ref[...] = dx
        dxb_ref[...] = dx.astype(BF16)

        @pl.when(pl.program_id(0) == 0)
        def _():
            dg_ref[...] = jnp.zeros_like(dg_ref)

        dg_ref[...] += jnp.sum(dg_rows, axis=0, keepdims=True)

    return pl.pallas_call(
        body, name=name, grid=(S // ts,),
        in_specs=[_row_spec(ts, D), _row_spec(ts, D), _full_spec((1, D)), _row_spec(ts, D)],
        out_specs=[_row_spec(ts, D), _row_spec(ts, D), _full_spec((1, D))],
        out_shape=[jax.ShapeDtypeStruct((S, D), F32), jax.ShapeDtypeStruct((S, D), BF16),
                   jax.ShapeDtypeStruct((1, D), F32)],
        compiler_params=_params(("arbitrary",)),
    )(x, dy, g, dres)


def _rope(x, c, sa, sb, sign):
    w = x.shape[-1]
    half = MLA_ROPE // 2
    fwd = pltpu.roll(x, w - half, 1)
    back = pltpu.roll(x, half, 1)
    if sign < 0:
        return x * c - fwd * sa - back * sb
    return x * c + fwd * sa + back * sb


def _split3(x):
    hi = x.astype(BF16)
    r1 = x - hi.astype(F32)
    mid = r1.astype(BF16)
    lo = (r1 - mid.astype(F32)).astype(BF16)
    return hi, mid, lo


def _prep_fwd(small, q_norm, kv_norm, bias_pad, kc, ksa, ksb, n_heads, name):
    S, W = small.shape
    QL, KVL = q_norm.shape[1], kv_norm.shape[1]
    assert W == QL + KVL + 2 * LANE
    ts = _tile(S, ROW_T, 8)
    tri = (lax.broadcasted_iota(jnp.int32, (ts, ts), 0) >= lax.broadcasted_iota(jnp.int32, (ts, ts), 1)).astype(BF16)

    def body(s_ref, qn_ref, kvn_ref, b_ref, kc_ref, ksa_ref, ksb_ref, tri_ref,
             cqn_ref, ckvn_ref, kr_ref, cum_ref, carry_ref):
        cq = s_ref[:, 0:QL]
        cqn_ref[...] = ((cq * _rms(cq)) * qn_ref[...]).astype(BF16)
        ckv = s_ref[:, QL:QL + KVL]
        ckvn_ref[...] = ((ckv * _rms(ckv)) * kvn_ref[...]).astype(BF16)
        kr = s_ref[:, QL + KVL:QL + KVL + LANE]
        kr_ref[...] = _rope(kr, kc_ref[...], ksa_ref[...], ksb_ref[...], 1).astype(BF16)
        z = s_ref[:, QL + KVL + LANE:W] + b_ref[...]
        logf = jnp.minimum(z, 0.0) - jnp.log1p(jnp.exp(-jnp.abs(z)))
        lane = lax.broadcasted_iota(jnp.int32, logf.shape, 1)
        logf = jnp.where(lane < n_heads, logf, 0.0)

        @pl.when(pl.program_id(0) == 0)
        def _():
            carry_ref[...] = jnp.zeros_like(carry_ref)

        t = tri_ref[...]
        cum = carry_ref[...]
        for part in _split3(logf):
            cum = cum + jnp.dot(t, part, preferred_element_type=F32)
        cum_ref[...] = cum
        carry_ref[...] = cum[ts - 1:ts, :]

    return pl.pallas_call(
        body, name=name, grid=(S // ts,),
        in_specs=[_row_spec(ts, W), _full_spec((1, QL)), _full_spec((1, KVL)), _full_spec((1, LANE)),
                  _row_spec(ts, LANE), _row_spec(ts, LANE), _row_spec(ts, LANE), _full_spec((ts, ts))],
        out_specs=[_row_spec(ts, QL), _row_spec(ts, KVL), _row_spec(ts, LANE), _row_spec(ts, LANE)],
        out_shape=[jax.ShapeDtypeStruct((S, QL), BF16), jax.ShapeDtypeStruct((S, KVL), BF16),
                   jax.ShapeDtypeStruct((S, LANE), BF16), jax.ShapeDtypeStruct((S, LANE), F32)],
        scratch_shapes=[pltpu.VMEM((1, LANE), F32)],
        compiler_params=_params(("arbitrary",)),
    )(small, q_norm, kv_norm, bias_pad, kc, ksa, ksb, tri)


def _prep_bwd(small, dcqn, dckvn, dkr_heads, dlogf, q_norm, kv_norm, bias_pad, kc, ksa, ksb, n_heads, name):
    S, W = small.shape
    QL, KVL = q_norm.shape[1], kv_norm.shape[1]
    ts = _tile(S, ROW_T, 8)

    def body(s_ref, dcq_ref, dckv_ref, dkr_ref, dlf_ref, qn_ref, kvn_ref, b_ref, kc_ref, ksa_ref, ksb_ref,
             ds_ref, gq_ref, gkv_ref, gb_ref):
        dcq, gq_rows = _rms_bwd(s_ref[:, 0:QL], dcq_ref[...], qn_ref[...])
        ds_ref[:, 0:QL] = dcq.astype(BF16)
        dckv, gkv_rows = _rms_bwd(s_ref[:, QL:QL + KVL], dckv_ref[...], kvn_ref[...])
        ds_ref[:, QL:QL + KVL] = dckv.astype(BF16)
        dkr = dkr_ref[:, 0:LANE]
        for h in range(1, n_heads):
            dkr = dkr + dkr_ref[:, h * LANE:(h + 1) * LANE]
        ds_ref[:, QL + KVL:QL + KVL + LANE] = _rope(dkr, kc_ref[...], ksa_ref[...], ksb_ref[...], -1).astype(BF16)
        z = s_ref[:, QL + KVL + LANE:W] + b_ref[...]
        dff = dlf_ref[...] * (1.0 / (1.0 + jnp.exp(z)))
        ds_ref[:, QL + KVL + LANE:W] = dff.astype(BF16)

        @pl.when(pl.program_id(0) == 0)
        def _():
            gq_ref[...] = jnp.zeros_like(gq_ref)
            gkv_ref[...] = jnp.zeros_like(gkv_ref)
            gb_ref[...] = jnp.zeros_like(gb_ref)

        gq_ref[...] += jnp.sum(gq_rows, axis=0, keepdims=True)
        gkv_ref[...] += jnp.sum(gkv_rows, axis=0, keepdims=True)
        gb_ref[...] += jnp.sum(dff, axis=0, keepdims=True)

    return pl.pallas_call(
        body, name=name, grid=(S // ts,),
        in_specs=[_row_spec(ts, W), _row_spec(ts, QL), _row_spec(ts, KVL), _row_spec(ts, n_heads * LANE),
                  _row_spec(ts, LANE), _full_spec((1, QL)), _full_spec((1, KVL)), _full_spec((1, LANE)),
                  _row_spec(ts, LANE), _row_spec(ts, LANE), _row_spec(ts, LANE)],
        out_specs=[_row_spec(ts, W), _full_spec((1, QL)), _full_spec((1, KVL)), _full_spec((1, LANE))],
        out_shape=[jax.ShapeDtypeStruct((S, W), BF16), jax.ShapeDtypeStruct((1, QL), F32),
                   jax.ShapeDtypeStruct((1, KVL), F32), jax.ShapeDtypeStruct((1, LANE), F32)],
        compiler_params=_params(("arbitrary",)),
    )(small, dcqn, dckvn, dkr_heads, dlogf, q_norm, kv_norm, bias_pad, kc, ksa, ksb)


def _sigmoid(z):
    return 1.0 / (1.0 + jnp.exp(-z))


def _final(h, g, target, name):
    S, D = h.shape
    ts = _tile(S, ROW_T, 8)

    def body(h_ref, g_ref, t_ref, dh_ref, dhb_ref, dg_ref, loss_ref):
        hv = h_ref[...]
        gv = g_ref[...]
        err = (hv * _rms(hv)) * gv - t_ref[...]
        dh, dg_rows = _rms_bwd(hv, err / D, gv)
        dh_ref[...] = dh
        dhb_ref[...] = dh.astype(BF16)

        @pl.when(pl.program_id(0) == 0)
        def _():
            dg_ref[...] = jnp.zeros_like(dg_ref)
            loss_ref[...] = jnp.zeros_like(loss_ref)

        dg_ref[...] += jnp.sum(dg_rows, axis=0, keepdims=True)
        row_loss = jnp.mean(err * err, axis=-1, keepdims=True)
        loss_ref[...] += 0.5 * jnp.sum(row_loss, axis=0, keepdims=True)

    return pl.pallas_call(
        body, name=name, grid=(S // ts,),
        in_specs=[_row_spec(ts, D), _full_spec((1, D)), _row_spec(ts, D)],
        out_specs=[_row_spec(ts, D), _row_spec(ts, D), _full_spec((1, D)), _full_spec((1, LANE))],
        out_shape=[jax.ShapeDtypeStruct((S, D), F32), jax.ShapeDtypeStruct((S, D), BF16),
                   jax.ShapeDtypeStruct((1, D), F32), jax.ShapeDtypeStruct((1, LANE), F32)],
        compiler_params=_params(("arbitrary",)),
    )(h, g, target)


def _suffix_sum_rows(x, name):
    R, S = x.shape
    tb = _tile(S, 512)
    nb = S // tb
    tri = (lax.broadcasted_iota(jnp.int32, (tb, tb), 0) >= lax.broadcasted_iota(jnp.int32, (tb, tb), 1)).astype(BF16)

    def body(x_ref, tri_ref, o_ref, carry_ref):
        @pl.when(pl.program_id(0) == 0)
        def _():
            carry_ref[...] = jnp.zeros_like(carry_ref)

        xv = x_ref[...]
        t = tri_ref[...]
        acc = jnp.broadcast_to(carry_ref[:, 0:1], xv.shape)
        for part in _split3(xv):
            acc = acc + jnp.dot(part, t, preferred_element_type=F32)
        o_ref[...] = acc
        carry_ref[...] = jnp.broadcast_to(acc[:, 0:1], carry_ref.shape)

    rev = pl.BlockSpec((R, tb), lambda i: (0, nb - 1 - i))
    return pl.pallas_call(
        body, name=name, grid=(nb,),
        in_specs=[rev, _full_spec((tb, tb))], out_specs=rev,
        out_shape=jax.ShapeDtypeStruct((R, S), F32),
        scratch_shapes=[pltpu.VMEM((R, LANE), F32)],
        compiler_params=_params(("arbitrary",)),
    )(x, tri)


def _pairs(nb, by_key):
    if by_key:
        pr = [(i, j) for j in range(nb) for i in range(j, nb)]
    else:
        pr = [(i, j) for i in range(nb) for j in range(i + 1)]
    return (jnp.asarray([p[0] for p in pr], jnp.int32), jnp.asarray([p[1] for p in pr], jnp.int32), len(pr))


class _AttT:
    def __init__(self, S, n_heads, q, ks, v, scale, chunk_causal, cum_rep=None):
        self.S, self.H, self.q, self.ks, self.v = S, n_heads, q, ks, v
        self.scale, self.chunk_causal, self.cum_rep = scale, chunk_causal, cum_rep
        self.T = _tile(S, ATT_T)
        self.qs = min(QSUB, self.T)
        self.nb = S // self.T
        self.dq, self.dv = q[1], v[1]
        self.has_bias = cum_rep is not None

    def q_spec(self, op):
        _, w, off, per_head = op
        return pl.BlockSpec((self.T, w), lambda h, p, it, jt: (it[p], off + (h if per_head else 0)))

    def k_spec(self, op):
        _, w, off, per_head = op
        return pl.BlockSpec((self.T, w), lambda h, p, it, jt: (jt[p], off + (h if per_head else 0)))

    def row_q(self):
        return pl.BlockSpec((None, 1, self.T), lambda h, p, it, jt: (h, 0, it[p]))

    def cum_k(self):
        return pl.BlockSpec((None, self.T, self.qs), lambda h, p, it, jt: (h, jt[p], 0))

    def sub_blocks(self, masked):
        return [(q0, min(self.T, q0 + self.qs) if masked else self.T) for q0 in range(0, self.T, self.qs)]

    def scores(self, k, q_sub, cum, q0, masked):
        s = lax.dot_general(k, q_sub, _NT, preferred_element_type=F32)
        if self.has_bias:
            s = s - cum
        mask = None
        if masked:
            r = lax.broadcasted_iota(jnp.int32, s.shape, 0)
            c = lax.broadcasted_iota(jnp.int32, s.shape, 1) + q0
            mask = (r // CHUNK <= c // CHUNK) if self.chunk_causal else (r <= c)
        return s, mask


def _join(k_refs):
    return k_refs[0][...] if len(k_refs) == 1 else jnp.concatenate([r[...] for r in k_refs], axis=-1)


def _att_fwd_t(att, name, exact=False):
    S, H, T, qs = att.S, att.H, att.T, att.qs
    it, jt, npairs = _pairs(att.nb, by_key=False)
    nk = len(att.ks)

    def body(it_ref, jt_ref, *refs):
        q_ref = refs[0]
        k_refs = refs[1:1 + nk]
        v_ref = refs[1 + nk]
        n = 2 + nk
        cum_ref = None
        if att.has_bias:
            cum_ref = refs[n]
            n += 1
        o_ref = refs[n]
        n += 1
        ox_ref = None
        if exact:
            ox_ref = refs[n]
            n += 1
        lse_ref, m_ref, l_ref, acc_ref = refs[n:n + 4]
        lo_ref = refs[n + 4] if exact else None
        p = pl.program_id(1)
        i, j = it_ref[p], jt_ref[p]

        @pl.when(j == 0)
        def _():
            m_ref[...] = jnp.full_like(m_ref, -jnp.inf)
            l_ref[...] = jnp.zeros_like(l_ref)
            acc_ref[...] = jnp.zeros_like(acc_ref)
            if exact:
                lo_ref[...] = jnp.zeros_like(lo_ref)

        def step(masked):
            k = _join(k_refs)
            v = v_ref[...]
            subs = att.sub_blocks(masked)

            def logits(idx):
                q0, nkeys = subs[idx]
                cum = cum_ref[0:nkeys, :] if att.has_bias else None
                return att.scores(k[0:nkeys], q_ref[q0:q0 + qs, :], cum, q0, masked)

            ahead = logits(0)
            for idx, (q0, nkeys) in enumerate(subs):
                qsl = slice(q0, q0 + qs)
                s, mask = ahead
                if idx + 1 < len(subs):
                    ahead = logits(idx + 1)
                if masked:
                    s = jnp.where(mask, s, -jnp.inf)
                m_prev = m_ref[:, qsl]
                m_new = jnp.maximum(m_prev, jnp.max(s, axis=0, keepdims=True))
                alpha = jnp.exp2(m_prev - m_new)
                pr = jnp.exp2(s - m_new)
                l_ref[:, qsl] = alpha * l_ref[:, qsl] + jnp.sum(pr, axis=0, keepdims=True)
                p_hi = pr.astype(BF16)
                acc_ref[:, qsl] = alpha * acc_ref[:, qsl] + lax.dot_general(
                    v[0:nkeys], p_hi, _TN, preferred_element_type=F32)
                if exact:
                    p_lo = (pr - p_hi.astype(F32)).astype(BF16)
                    lo_ref[:, qsl] = alpha * lo_ref[:, qsl] + lax.dot_general(
                        v[0:nkeys], p_lo, _TN, preferred_element_type=F32)
                m_ref[:, qsl] = m_new

        @pl.when(j < i)
        def _():
            step(False)

        @pl.when(j == i)
        def _():
            step(True)
            l = l_ref[...]
            inv = 1.0 / l
            o_ref[...] = jnp.transpose(acc_ref[...] * inv).astype(o_ref.dtype)
            if exact:
                ox_ref[...] = jnp.transpose((acc_ref[...] + lo_ref[...]) * inv)
            lse_ref[...] = m_ref[...] + jnp.log2(l)

    in_specs = [att.q_spec(att.q)] + [att.k_spec(k) for k in att.ks] + [att.k_spec(att.v)]
    args = [att.q[0]] + [k[0] for k in att.ks] + [att.v[0]]
    if att.has_bias:
        in_specs.append(att.cum_k())
        args.append(att.cum_rep)
    o_spec = pl.BlockSpec((T, att.dv), lambda h, p, it, jt: (it[p], h))
    out_specs = [o_spec]
    out_shape = [jax.ShapeDtypeStruct((S, H * att.dv), BF16)]
    scratch = [pltpu.VMEM((1, T), F32), pltpu.VMEM((1, T), F32), pltpu.VMEM((att.dv, T), F32)]
    if exact:
        out_specs.append(o_spec)
        out_shape.append(jax.ShapeDtypeStruct((S, H * att.dv), F32))
        scratch.append(pltpu.VMEM((att.dv, T), F32))
    out_specs.append(att.row_q())
    out_shape.append(jax.ShapeDtypeStruct((H, 1, S), F32))
    return pl.pallas_call(
        body, name=name,
        grid_spec=pltpu.PrefetchScalarGridSpec(
            num_scalar_prefetch=2, grid=(H, npairs), in_specs=in_specs, out_specs=out_specs,
            scratch_shapes=scratch),
        out_shape=out_shape,
        compiler_params=_params(("parallel", "arbitrary")),
    )(it, jt, *args)


def _att_delta_t(do, o, n_heads, name, order=None):
    S = do.shape[0]
    w = do.shape[1] // n_heads
    ts = _tile(S, ATT_T)
    ones = jnp.ones((8, w), BF16)
    extra = [] if order is None else [order]

    def body(do_ref, o_ref, ones_ref, *rest):
        d_ref = rest[-1]
        prod = do_ref[...].astype(F32) * o_ref[...].astype(F32)
        acc = jnp.zeros((8, ts), F32)
        for part in _split3(prod):
            acc = acc + lax.dot_general(ones_ref[...], part, _NT, preferred_element_type=F32)
        d_ref[...] = acc[0:1, :]

    blk = pl.BlockSpec((ts, w), lambda i, h: (i, h))
    return pl.pallas_call(
        body, name=name, grid=(S // ts, n_heads),
        in_specs=[blk, blk, pl.BlockSpec((8, w), lambda i, h: (0, 0))] + [_ANY_SPEC] * len(extra),
        out_specs=pl.BlockSpec((None, 1, ts), lambda i, h: (h, 0, i)),
        out_shape=jax.ShapeDtypeStruct((n_heads, 1, S), F32),
        compiler_params=_params(("parallel", "parallel")),
    )(do, o, ones, *extra)


def _att_bwd_t(att, do, lse, delta, dq_dtype, dk_dtypes, name, dq_rope=None):
    S, H, T, qs = att.S, att.H, att.T, att.qs
    it, jt, npairs = _pairs(att.nb, by_key=True)
    nk = len(att.ks)
    last = att.nb - 1
    widths = [k[1] for k in att.ks]

    def body(it_ref, jt_ref, *refs):
        q_ref = refs[0]
        k_refs = refs[1:1 + nk]
        v_ref, do_ref, lse_ref, dl_ref = refs[1 + nk:5 + nk]
        n = 5 + nk
        cum_ref = None
        if att.has_bias:
            cum_ref = refs[n]
            n += 1
        rope_refs = None
        if dq_rope is not None:
            rope_refs = refs[n:n + 3]
            n += 3
        dq_ref = refs[n]
        dk_refs = refs[n + 1:n + 1 + nk]
        dv_ref = refs[n + 1 + nk]
        n += nk + 2
        dc_ref = None
        if att.has_bias:
            dc_ref = refs[n]
            n += 1
        dq_acc, dk_acc, dv_acc = refs[n:n + 3]
        dc_acc = refs[n + 3] if att.has_bias else None
        p = pl.program_id(1)
        i, j = it_ref[p], jt_ref[p]

        @pl.when(p == 0)
        def _():
            dq_acc[...] = jnp.zeros_like(dq_acc)

        @pl.when(i == j)
        def _():
            dk_acc[...] = jnp.zeros_like(dk_acc)
            dv_acc[...] = jnp.zeros_like(dv_acc)
            if att.has_bias:
                dc_acc[...] = jnp.zeros_like(dc_acc)

        def step(masked):
            k = _join(k_refs)
            v = v_ref[...]
            subs = att.sub_blocks(masked)

            def logits(idx):
                q0, nkeys = subs[idx]
                cum = cum_ref[0:nkeys, :] if att.has_bias else None
                return att.scores(k[0:nkeys], q_ref[q0:q0 + qs, :], cum, q0, masked)

            ahead = logits(0)
            for idx, (q0, nkeys) in enumerate(subs):
                qsl = slice(q0, q0 + qs)
                ksl = slice(0, nkeys)
                q_sub = q_ref[qsl, :]
                do_sub = do_ref[qsl, :]
                s, mask = ahead
                if idx + 1 < len(subs):
                    ahead = logits(idx + 1)
                pr = jnp.exp2(s - lse_ref[:, qsl])
                if masked:
                    pr = jnp.where(mask, pr, 0.0)
                dp = lax.dot_general(v[ksl], do_sub, _NT, preferred_element_type=F32)
                ds = pr * (dp - dl_ref[:, qsl])
                ds_b = ds.astype(BF16)
                dv_acc[ksl, :] += jnp.dot(pr.astype(BF16), do_sub, preferred_element_type=F32)
                dk_acc[ksl, :] += jnp.dot(ds_b, q_sub, preferred_element_type=F32)
                dq_acc[i, :, qsl] += lax.dot_general(k[ksl], ds_b, _TN, preferred_element_type=F32)
                if att.has_bias:
                    part = ds[:, 0:LANE] if qs >= LANE else ds
                    for c0 in range(LANE, qs, LANE):
                        part = part + ds[:, c0:c0 + LANE]
                    dc_acc[ksl, :] += part

        @pl.when(i > j)
        def _():
            step(False)

        @pl.when(i == j)
        def _():
            step(True)
            dq = jnp.transpose(dq_acc[i] * att.scale)
            if dq_rope is not None:
                dq = _rope(dq, rope_refs[0][...], rope_refs[1][...], rope_refs[2][...], -1)
            dq_ref[...] = dq.astype(dq_ref.dtype)

        @pl.when(i == last)
        def _():
            dk = dk_acc[...] * (1.0 / LOG2E)
            off = 0
            for r, w in zip(dk_refs, widths):
                r[...] = dk[:, off:off + w].astype(r.dtype)
                off += w
            dv_ref[...] = dv_acc[...].astype(dv_ref.dtype)
            if att.has_bias:
                dc_ref[...] = -jnp.sum(dc_acc[...], axis=-1, keepdims=True)

    do_op = (do, att.dv, 0, True)
    in_specs = ([att.q_spec(att.q)] + [att.k_spec(k) for k in att.ks]
                + [att.k_spec(att.v), att.q_spec(do_op), att.row_q(), att.row_q()])
    args = [att.q[0]] + [k[0] for k in att.ks] + [att.v[0], do, lse, delta]
    if att.has_bias:
        in_specs.append(att.cum_k())
        args.append(att.cum_rep)
    if dq_rope is not None:
        in_specs += [pl.BlockSpec((T, att.dq), lambda h, p, it, jt: (jt[p], 0))] * 3
        args += list(dq_rope)
    out_specs = [pl.BlockSpec((T, att.dq), lambda h, p, it, jt: (jt[p], h))]
    out_shape = [jax.ShapeDtypeStruct((S, H * att.dq), dq_dtype)]
    out_specs += [pl.BlockSpec((T, w), lambda h, p, it, jt: (jt[p], h)) for w in widths]
    out_shape += [jax.ShapeDtypeStruct((S, H * w), dt) for w, dt in zip(widths, dk_dtypes)]
    out_specs.append(pl.BlockSpec((T, att.dv), lambda h, p, it, jt: (jt[p], h)))
    out_shape.append(jax.ShapeDtypeStruct((S, H * att.dv), BF16))
    scratch = [pltpu.VMEM((att.nb, att.dq, T), F32), pltpu.VMEM((T, att.dq), F32), pltpu.VMEM((T, att.dv), F32)]
    if att.has_bias:
        out_specs.append(pl.BlockSpec((None, T, 1), lambda h, p, it, jt: (h, jt[p], 0)))
        out_shape.append(jax.ShapeDtypeStruct((H, S, 1), F32))
        scratch.append(pltpu.VMEM((T, min(qs, LANE)), F32))
    return pl.pallas_call(
        body, name=name,
        grid_spec=pltpu.PrefetchScalarGridSpec(
            num_scalar_prefetch=2, grid=(H, npairs), in_specs=in_specs, out_specs=out_specs,
            scratch_shapes=scratch),
        out_shape=out_shape,
        compiler_params=_params(("parallel", "arbitrary")),
    )(it, jt, *args)


def _adamw(w, g1, g2, m, v, name, g_row=None):
    _, K, N = w.shape
    by_rows = K % 8 == 0
    tr = _tile(K, 256, 8) if by_rows else K
    if g_row is None:
        assert g1.shape == (K, N) and g2.shape == (K, N), name
        g_row = 0
    assert by_rows and g_row % tr == 0 or g_row == 0, name
    g_blk = g_row // tr
    tc = N if by_rows else _tile(N, LANE)
    c1 = 1.0 - ADAM_B1 ** ADAM_STEP
    c2 = 1.0 - ADAM_B2 ** ADAM_STEP

    def body(w_ref, g1_ref, g2_ref, m_ref, v_ref, g_ref, d_ref, nm_ref, nv_ref):
        gv = g1_ref[...] + g2_ref[...]
        nm = ADAM_B1 * m_ref[...] + (1.0 - ADAM_B1) * gv
        nv = ADAM_B2 * v_ref[...] + (1.0 - ADAM_B2) * (gv * gv)
        g_ref[...] = gv
        d_ref[...] = -ADAM_LR * ((nm / c1) / (jnp.sqrt(nv / c2) + ADAM_EPS) + ADAM_WD * w_ref[...])
        nm_ref[...] = nm
        nv_ref[...] = nv

    if by_rows:
        blk = pl.BlockSpec((None, tr, N), lambda i: (0, i, 0))
        gblk = pl.BlockSpec((tr, N), lambda i: (g_blk + i, 0))
    else:
        blk = pl.BlockSpec((None, K, tc), lambda i: (0, 0, i))
        gblk = pl.BlockSpec((K, tc), lambda i: (0, i))
    return pl.pallas_call(
        body, name=name, grid=(K // tr if by_rows else N // tc,),
        in_specs=[blk, gblk, gblk, blk, blk], out_specs=[blk] * 4,
        out_shape=[jax.ShapeDtypeStruct((1, K, N), F32)] * 4,
        compiler_params=_params(("parallel",)),
    )(w, g1, g2, m, v)


_HBM_SPEC = pl.BlockSpec(memory_space=pltpu.HBM)
_SEM_SPEC = pl.BlockSpec(memory_space=pltpu.SEMAPHORE)
_VMEM_SPEC = pl.BlockSpec(memory_space=pltpu.VMEM)
_EFFECT = pltpu.SideEffectType.DATAFLOW_SIDE_EFFECTING


def _place():
    return lax.axis_index("x"), lax.axis_index("y"), lax.axis_index("c")


def _other_chips(x, y):
    return [(1 - x, y), (x, 1 - y), (1 - x, 1 - y)]


def _all_gather_halves(wp, name):
    R, C = wp.shape
    half = R // 2
    assert half % 16 == 0

    def body(w_ref, out_ref, ici_send, ici_recv, d2d_send, d2d_recv, local_sem):
        x, y, c = _place()
        me = 2 * x + y
        chips = _other_chips(x, y)
        mine = pl.ds(pl.multiple_of(c * half, 16), half)
        theirs = pl.ds(pl.multiple_of((1 - c) * half, 16), half)
        local = pltpu.make_async_copy(w_ref, out_ref.at[me], local_sem)
        local.start()
        sends = []
        for n, (px, py) in enumerate(chips):
            cp = pltpu.make_async_remote_copy(
                src_ref=w_ref.at[mine], dst_ref=out_ref.at[me, mine], send_sem=ici_send.at[n],
                recv_sem=ici_recv.at[n], device_id=(px, py, c), device_id_type=MESH)
            cp.start()
            sends.append(cp)
        for n, (px, py) in enumerate(chips):
            slot = 2 * px + py
            pltpu.make_async_remote_copy(
                src_ref=w_ref.at[mine], dst_ref=out_ref.at[slot, mine], send_sem=ici_send.at[n],
                recv_sem=ici_recv.at[n], device_id=(px, py, c), device_id_type=MESH).wait_recv()
            cp = pltpu.make_async_remote_copy(
                src_ref=out_ref.at[slot, mine], dst_ref=out_ref.at[slot, mine], send_sem=d2d_send.at[n],
                recv_sem=d2d_recv.at[n], device_id=(x, y, 1 - c), device_id_type=MESH)
            cp.start()
            sends.append(cp)
        for n, (px, py) in enumerate(chips):
            slot = 2 * px + py
            pltpu.make_async_remote_copy(
                src_ref=out_ref.at[slot, theirs], dst_ref=out_ref.at[slot, theirs], send_sem=d2d_send.at[n],
                recv_sem=d2d_recv.at[n], device_id=(x, y, 1 - c), device_id_type=MESH).wait_recv()
        for cp in sends:
            cp.wait_send()
        local.wait()

    return pl.pallas_call(
        body, name=name,
        in_specs=[_ANY_SPEC], out_specs=_ANY_SPEC,
        out_shape=jax.ShapeDtypeStruct((N_CHIPS, R, C), wp.dtype),
        scratch_shapes=[pltpu.SemaphoreType.DMA((3,)), pltpu.SemaphoreType.DMA((3,)), pltpu.SemaphoreType.DMA((3,)),
                        pltpu.SemaphoreType.DMA((3,)), pltpu.SemaphoreType.DMA],
    )(wp)


def _chip_copies(src_ref, land_ref, sems, gather):
    x, y, c = _place()
    me = 2 * x + y
    out, back = [], []
    for n, (px, py) in enumerate(_other_chips(x, y)):
        src = src_ref if gather else src_ref.at[2 * px + py]
        out.append(pltpu.make_async_remote_copy(
            src_ref=src, dst_ref=land_ref.at[me] if gather else land_ref.at[n],
            send_sem=sems[n], recv_sem=sems[3 + n], device_id=(px, py, c), device_id_type=MESH))
        back.append(pltpu.make_async_remote_copy(
            src_ref=src, dst_ref=land_ref.at[2 * px + py] if gather else land_ref.at[n],
            send_sem=sems[n], recv_sem=sems[3 + n], device_id=(px, py, c), device_id_type=MESH))
    return out, back


def _xchg_start(src, land, gather, order, name):
    def body(src_ref, land_ref, order_ref, *outs):
        sems = outs[0:6]
        token = outs[8]
        out, _ = _chip_copies(src_ref, land_ref, sems, gather)
        for cp in out:
            cp.start()
        token[...] = jnp.zeros_like(token)

    outs = pl.pallas_call(
        body, name=name,
        out_shape=(pltpu.SemaphoreType.DMA(()),) * 6 + (
            pltpu.HBM(src.shape, src.dtype), pltpu.HBM(land.shape, land.dtype),
            jax.ShapeDtypeStruct((8, LANE), F32)),
        in_specs=(_HBM_SPEC, _HBM_SPEC, _ANY_SPEC),
        out_specs=(_SEM_SPEC,) * 6 + (_HBM_SPEC, _HBM_SPEC, _VMEM_SPEC),
        input_output_aliases={0: 6, 1: 7},
        compiler_params=pltpu.CompilerParams(has_side_effects=_EFFECT),
    )(pltpu.with_memory_space_constraint(src, pltpu.HBM), pltpu.with_memory_space_constraint(land, pltpu.HBM), order)
    return outs[0:6], outs[6], outs[7], outs[8]


def _xchg_wait(started, gather, after, name):
    sems, src, land, _ = started

    def body(src_ref, land_ref, *rest):
        _, back = _chip_copies(src_ref, land_ref, rest[0:6], gather)
        for cp in back:
            cp.wait_send()
            cp.wait_recv()

    return pl.pallas_call(
        body, name=name,
        out_shape=(pltpu.HBM(src.shape, src.dtype), pltpu.HBM(land.shape, land.dtype)),
        in_specs=(_HBM_SPEC, _HBM_SPEC) + (_SEM_SPEC,) * 6 + (_ANY_SPEC,),
        out_specs=(_HBM_SPEC, _HBM_SPEC),
        input_output_aliases={0: 0, 1: 1},
        compiler_params=pltpu.CompilerParams(has_side_effects=_EFFECT),
    )(src, land, *sems, after)


def _sib_copy(src_ref, land_ref, send_sem, recv_sem):
    x, y, c = _place()
    return pltpu.make_async_remote_copy(src_ref=src_ref, dst_ref=land_ref, send_sem=send_sem, recv_sem=recv_sem,
                                        device_id=(x, y, 1 - c), device_id_type=MESH)


def _sib_start(src, name):
    land = lax.empty(src.shape, src.dtype)

    def body(src_ref, land_ref, send_sem, recv_sem, src_thru, land_thru, token):
        _sib_copy(src_ref, land_ref, send_sem, recv_sem).start()
        token[...] = jnp.zeros_like(token)

    return pl.pallas_call(
        body, name=name,
        out_shape=(pltpu.SemaphoreType.DMA(()), pltpu.SemaphoreType.DMA(()),
                   pltpu.HBM(src.shape, src.dtype), pltpu.HBM(land.shape, land.dtype),
                   jax.ShapeDtypeStruct((8, LANE), F32)),
        in_specs=(_HBM_SPEC, _HBM_SPEC),
        out_specs=(_SEM_SPEC, _SEM_SPEC, _HBM_SPEC, _HBM_SPEC, _VMEM_SPEC),
        input_output_aliases={0: 2, 1: 3},
        compiler_params=pltpu.CompilerParams(has_side_effects=_EFFECT),
    )(pltpu.with_memory_space_constraint(src, pltpu.HBM), pltpu.with_memory_space_constraint(land, pltpu.HBM))


def _sib_wait(started, after, name):
    send_sem, recv_sem, src, land, _ = started

    def body(src_ref, land_ref, send_sem, recv_sem, after_ref, src_out, land_out):
        cp = _sib_copy(src_ref, land_ref, send_sem, recv_sem)
        cp.wait_send()
        cp.wait_recv()

    return pl.pallas_call(
        body, name=name,
        out_shape=(pltpu.HBM(src.shape, src.dtype), pltpu.HBM(land.shape, land.dtype)),
        in_specs=(_HBM_SPEC, _HBM_SPEC, _SEM_SPEC, _SEM_SPEC, _ANY_SPEC),
        out_specs=(_HBM_SPEC, _HBM_SPEC),
        input_output_aliases={0: 0, 1: 1},
        compiler_params=pltpu.CompilerParams(has_side_effects=_EFFECT),
    )(src, land, send_sem, recv_sem, after)


def _sum_slabs(gp, recv, chip, name):
    _, R, C = gp.shape
    tr = _tile(R, PACK_ROWS, 16)

    def body(chip_ref, own_ref, r0_ref, r1_ref, r2_ref, o_ref):
        acc = own_ref[...].astype(F32) + r0_ref[...].astype(F32)
        o_ref[...] = (acc + r1_ref[...].astype(F32)) + r2_ref[...].astype(F32)

    def got(n):
        return pl.BlockSpec((None, tr, C), lambda i, chip_ref: (n, i, 0))

    return pl.pallas_call(
        body, name=name,
        grid_spec=pltpu.PrefetchScalarGridSpec(
            num_scalar_prefetch=1, grid=(R // tr,),
            in_specs=[pl.BlockSpec((None, tr, C), lambda i, chip_ref: (chip_ref[0], i, 0)), got(0), got(1), got(2)],
            out_specs=pl.BlockSpec((tr, C), lambda i, chip_ref: (i, 0))),
        out_shape=jax.ShapeDtypeStruct((R, C), F32),
        compiler_params=_params(("parallel",)),
    )(jnp.reshape(chip, (1,)).astype(jnp.int32), gp, recv, recv, recv)


def _all_reduce_vec(vec, name):
    VR, W = vec.shape

    def body(vec_ref, vall_ref, vout_ref, vsend_sems, vrecv_sems):
        x, y, c = _place()
        vall_ref[4 * x + 2 * y + c] = vec_ref[...]
        sends = []
        peers = []
        for r in range(1, N_DEV):
            dx, dy, dc = (r >> 2) & 1, (r >> 1) & 1, r & 1
            peer = (x ^ dx, y ^ dy, c ^ dc)
            peers.append(peer)
            cp = pltpu.make_async_remote_copy(
                src_ref=vec_ref, dst_ref=vall_ref.at[4 * x + 2 * y + c], send_sem=vsend_sems.at[r - 1],
                recv_sem=vrecv_sems.at[r - 1], device_id=peer, device_id_type=MESH)
            cp.start()
            sends.append(cp)
        for r, peer in enumerate(peers):
            pltpu.make_async_remote_copy(
                src_ref=vec_ref, dst_ref=vall_ref.at[4 * peer[0] + 2 * peer[1] + peer[2]],
                send_sem=vsend_sems.at[r], recv_sem=vrecv_sems.at[r],
                device_id=peer, device_id_type=MESH).wait_recv()
        total = vall_ref[0]
        for d in range(1, N_DEV):
            total = total + vall_ref[d]
        vout_ref[...] = total
        for cp in sends:
            cp.wait_send()

    outs = pl.pallas_call(
        body, name=name,
        in_specs=[_VMEM_SPEC], out_specs=[_VMEM_SPEC, _VMEM_SPEC],
        out_shape=[jax.ShapeDtypeStruct((N_DEV, VR, W), F32), jax.ShapeDtypeStruct((VR, W), F32)],
        scratch_shapes=[pltpu.SemaphoreType.DMA((N_DEV - 1,)), pltpu.SemaphoreType.DMA((N_DEV - 1,))],
    )(vec)
    return outs[1]


class _Pack:
    def __init__(self, group, C):
        self.group, self.C = group, C
        self.rows, self.offs, off = {}, {}, 0
        for nm, (K, N), _ in group:
            assert N <= C, nm
            self.rows[nm] = K if 2 * N > C else -(-(K * N) // C)
            self.offs[nm] = off
            off += -(-self.rows[nm] // 16) * 16
        self.used = off
        self.R = -(-off // PACK_ROWS) * PACK_ROWS

    def _rows_of(self, a):
        K, N = a.shape
        if 2 * N > self.C:
            a = jnp.pad(a, ((0, 0), (0, self.C - N)))
        else:
            a = jnp.pad(a.reshape(-1), (0, -(K * N) % self.C)).reshape(-1, self.C)
        return jnp.pad(a, ((0, -a.shape[0] % 16), (0, 0)))

    def pack(self, shards):
        parts = [self._rows_of(shards[nm].astype(BF16)) for nm, _, _ in self.group]
        return jnp.concatenate(parts + [jnp.zeros((self.R - self.used, self.C), BF16)], axis=0)

    def _shard_of(self, rows, shape):
        K, N = shape
        return rows[:, :N] if 2 * N > self.C else rows.reshape(-1)[:K * N].reshape(K, N)

    def part(self, flat, nm, shape):
        return self._shard_of(flat[self.offs[nm]:self.offs[nm] + self.rows[nm]], shape)

    def slab_rows(self, nm, g):
        (K, N), axis = next((shape, axis) for n, shape, axis in self.group if n == nm)
        cuts = [g[:, k * N:(k + 1) * N] if axis == 1 else g[k * K:(k + 1) * K, :] for k in range(N_CHIPS)]
        return jnp.stack([self._rows_of(c.astype(BF16)) for c in cuts])

    def slabs(self, grads):
        parts = [self.slab_rows(nm, grads[nm]) for nm, _, _ in self.group]
        return jnp.concatenate(parts + [jnp.zeros((N_CHIPS, self.R - self.used, self.C), BF16)], axis=1)

    def full(self, gathered, names=None):
        res = {}
        for nm, (K, N), axis in self.group:
            if names is None or nm in names:
                rows = gathered[:, self.offs[nm]:self.offs[nm] + self.rows[nm]]
                res[nm] = jnp.concatenate([self._shard_of(rows[k], (K, N)) for k in range(N_CHIPS)], axis=axis)
        return res


def _rope_tables(S):
    pos = jnp.arange(S, dtype=F32)
    inv = 1.0 / (ROPE_THETA ** (jnp.arange(0, MLA_ROPE, 2, dtype=F32) / MLA_ROPE))
    ang = pos[:, None] * inv[None, :]
    cos, sin = jnp.cos(ang), jnp.sin(ang)
    half = MLA_ROPE // 2
    z = jnp.zeros((S, half), F32)
    one = jnp.ones((S, LANE - MLA_ROPE), F32)
    zero = jnp.zeros((S, LANE - MLA_ROPE), F32)
    kc = jnp.concatenate([cos, cos, one], axis=1)
    ksa = jnp.concatenate([-sin, z, zero], axis=1)
    ksb = jnp.concatenate([z, sin, zero], axis=1)
    qc = jnp.concatenate([jnp.ones((S, MLA_NOPE), F32), kc], axis=1)
    qsa = jnp.concatenate([jnp.zeros((S, MLA_NOPE), F32), ksa], axis=1)
    qsb = jnp.concatenate([jnp.zeros((S, MLA_NOPE), F32), ksb], axis=1)
    return (kc, ksa, ksb), (qc, qsa, qsb)


def _pad_cols(a, width):
    return jnp.pad(a, ((0, 0), (0, width - a.shape[1])))


def kernel(x, attn_norm, w_in, fox_f_bias, q_norm, w_uq, kv_norm, w_ukv, w_mla_branch, w_fox_branch, w_out, mlp_norm, w_up, w_down, final_norm, loss_target, m_attn_norm, m_w_in, m_fox_f_bias, m_q_norm, m_w_uq, m_kv_norm, m_w_ukv, m_w_mla_branch, m_w_fox_branch, m_w_out, m_mlp_norm, m_w_up, m_w_down, m_final_norm, v_attn_norm, v_w_in, v_fox_f_bias, v_q_norm, v_w_uq, v_kv_norm, v_w_ukv, v_w_mla_branch, v_w_fox_branch, v_w_out, v_mlp_norm, v_w_up, v_w_down, v_final_norm):
    _, S, D = x.shape
    H, HF = MLA_HEADS, FOX_HEADS
    QL, KVL = MLA_Q_LORA, MLA_KV_LORA
    assert H == HF and H <= 8
    xs = x[0]
    target = loss_target[0]
    C = D
    chip = 2 * lax.axis_index("x") + lax.axis_index("y")

    def flip(a):
        return jnp.transpose(a, (0, 2, 1))

    w_in, m_w_in, v_w_in = flip(w_in), flip(m_w_in), flip(v_w_in)
    weights = {"attn_norm": attn_norm, "w_in": w_in, "fox_f_bias": fox_f_bias, "q_norm": q_norm, "w_uq": w_uq,
               "kv_norm": kv_norm, "w_ukv": w_ukv, "w_mla_branch": w_mla_branch, "w_fox_branch": w_fox_branch,
               "w_out": w_out, "mlp_norm": mlp_norm, "w_up": w_up, "w_down": w_down, "final_norm": final_norm}
    moments = {"attn_norm": (m_attn_norm, v_attn_norm), "w_in": (m_w_in, v_w_in), "fox_f_bias": (m_fox_f_bias, v_fox_f_bias),
               "q_norm": (m_q_norm, v_q_norm), "w_uq": (m_w_uq, v_w_uq), "kv_norm": (m_kv_norm, v_kv_norm),
               "w_ukv": (m_w_ukv, v_w_ukv), "w_mla_branch": (m_w_mla_branch, v_w_mla_branch),
               "w_fox_branch": (m_w_fox_branch, v_w_fox_branch), "w_out": (m_w_out, v_w_out),
               "mlp_norm": (m_mlp_norm, v_mlp_norm), "w_up": (m_w_up, v_w_up), "w_down": (m_w_down, v_w_down),
               "final_norm": (m_final_norm, v_final_norm)}

    def group(names_axes):
        return [(nm, weights[nm].shape[1:], axis) for nm, axis in names_axes]

    pack_a = _Pack(group([("w_in", 0), ("w_uq", 1), ("w_ukv", 1)]), C)
    pack_b = _Pack(group([("w_down", 0), ("w_up", 1), ("w_out", 0), ("w_mla_branch", 1), ("w_fox_branch", 1)]), C)
    RA, RB = pack_a.R, pack_b.R
    wp_a = pack_a.pack({nm: weights[nm][0] for nm, _, _ in pack_a.group})
    wp_b = pack_b.pack({nm: weights[nm][0] for nm, _, _ in pack_b.group})
    n_in = w_in.shape[1]
    rows_in = -(-n_in // 16) * 16
    assert pack_a.offs["w_in"] == 0 and all((k * n_in) % 16 + n_in <= rows_in for k in range(N_CHIPS))
    shifted = lax.dynamic_update_slice(jnp.zeros((rows_in, C), BF16), wp_a[:n_in], ((chip * n_in) % 16, 0))
    wp_a = jnp.concatenate([shifted, wp_a[rows_in:]], axis=0)
    gathered_a = _all_gather_halves(wp_a, "all_gather_a")
    ag_b = _xchg_start(wp_b, lax.empty((N_CHIPS, RB, C), BF16), True, gathered_a, "all_gather_start_b")
    xn = _norm_fwd(xs, attn_norm, "attn_norm_fwd", order=ag_b[3])
    full = pack_a.full(gathered_a, ("w_uq", "w_ukv"))
    tile0 = [(k * n_in) // 16 * 16 for k in range(N_CHIPS)]
    total = tile0[-1] + rows_in
    full["w_in"] = sum(jnp.pad(gathered_a[k, :rows_in], ((tile0[k], total - tile0[k] - rows_in), (0, 0)))
                       for k in range(N_CHIPS))

    o_ckv = QL
    o_kr = o_ckv + KVL
    o_fq = o_kr + MLA_ROPE
    o_ff = o_fq + 3 * HF * FOX_HEAD_DIM
    o_g = o_ff + HF
    wi = full["w_in"]
    assert N_CHIPS * n_in == o_g + 2 * D and wi.shape[0] >= o_g + 2 * D
    WS = QL + KVL + 2 * LANE
    NQKV = 3 * HF * FOX_HEAD_DIM

    def pad_rows(a, rows):
        return jnp.pad(a, ((0, rows - a.shape[0]), (0, 0)))

    w_small = jnp.concatenate([wi[:o_kr], pad_rows(wi[o_kr:o_fq], LANE), pad_rows(wi[o_ff:o_g], LANE)], axis=0)
    w_qkv = wi[o_fq:o_ff]
    w_g = wi[o_g:o_g + 2 * D]
    w_pack = jnp.concatenate([w_small, w_qkv, w_g], axis=0)
    dqk = MLA_NOPE + MLA_ROPE
    w_uq_p = jnp.pad(full["w_uq"].reshape(QL, H, dqk), ((0, 0), (0, 0), (0, QPAD - dqk))).reshape(QL, H * QPAD)
    ukv = full["w_ukv"].reshape(KVL, H, MLA_NOPE + MLA_V)
    w_ukv_p = jnp.concatenate([ukv[:, :, :MLA_NOPE].reshape(KVL, H * MLA_NOPE),
                               ukv[:, :, MLA_NOPE:].reshape(KVL, H * MLA_V)], axis=1)

    (kc, ksa, ksb), (qc, qsa, qsb) = _rope_tables(S)
    bias_pad = _pad_cols(fox_f_bias, LANE)

    small = _matmul(xn, w_small, "nt", [F32], "proj_small")
    n_fq = HF * FOX_HEAD_DIM
    q_scale = jnp.concatenate([jnp.full((1, n_fq), LOG2E / math.sqrt(FOX_HEAD_DIM), F32),
                               jnp.ones((1, NQKV - n_fq), F32)], axis=1)
    qkv = _matmul(xn, w_qkv, "nt", [BF16], "proj_qkv", col_extras=(q_scale,), epilogue=lambda acc, cs: (acc * cs,))
    gpre = _matmul(xn, w_g, "nt", [F32], "proj_gates")
    cqn, ckvn, kr, cum = _prep_fwd(small, q_norm, kv_norm, bias_pad, kc, ksa, ksb, HF, "prep_fwd")
    c2_mla = LOG2E / math.sqrt(dqk)
    q_rot = _matmul(cqn, w_uq_p, "nn", [BF16], "mla_q_up", tn=QPAD, row_extras=(qc * c2_mla, qsa * c2_mla, qsb * c2_mla),
                    epilogue=lambda acc, c, sa, sb: (_rope(acc, c, sa, sb, 1),))
    kv2 = _matmul(ckvn, w_ukv_p, "nn", [BF16], "mla_kv_up")

    mla = _AttT(S, H, (q_rot, QPAD, 0, True), [(kv2, MLA_NOPE, 0, True), (kr, LANE, 0, False)],
                (kv2, MLA_V, H, True), 1.0 / math.sqrt(dqk), True)
    o_mla, lse_mla = _att_fwd_t(mla, "mla_att_fwd")

    cum_t = jnp.transpose(cum[:, :HF]) * LOG2E
    cum_rep = jnp.broadcast_to(cum_t[:, :, None], (HF, S, min(QSUB, _tile(S, ATT_T))))
    fox = _AttT(S, HF, (qkv, FOX_HEAD_DIM, 0, True), [(qkv, FOX_HEAD_DIM, HF, True)],
                (qkv, FOX_HEAD_DIM, 2 * HF, True), 1.0 / math.sqrt(FOX_HEAD_DIM), False, cum_rep)
    o_fox, ox_fox, lse_fox = _att_fwd_t(fox, "fox_att_fwd", exact=True)

    own_b, land_b = _xchg_wait(ag_b, True, lse_fox, "all_gather_wait_b")
    gathered_b = lax.dynamic_update_slice(land_b, own_b[None], (chip, 0, 0))
    full.update(pack_b.full(gathered_b, ("w_mla_branch", "w_fox_branch", "w_out")))
    w_mb, w_fb, w_o = (full[n] for n in ("w_mla_branch", "w_fox_branch", "w_out"))

    def b_of(nm, mode, tn, tk):
        (K, N), axis = next((shape, axis) for n, shape, axis in pack_b.group if n == nm)
        off = pack_b.offs[nm]
        shape = (N_CHIPS * K, N) if axis == 0 else (K, N_CHIPS * N)
        t_r, t_c = (tk, tn) if mode == "nn" else (tn, tk)
        t_r, t_c = _tile(shape[0], t_r), _tile(shape[1], t_c)
        if not (N == C and K % t_r == 0 and N % t_c == 0 and off % t_r == 0):
            return pack_b.full(gathered_b, (nm,))[nm], None
        base = off // t_r
        if axis == 0:
            per = K // t_r
            place = lambda rb, cb: (rb // per, base + rb % per, cb)
        else:
            per = N // t_c
            place = lambda rb, cb: (cb // per, base + rb, cb % per)
        return gathered_b, (shape, (lambda j, k: place(k, j)) if mode == "nn" else (lambda j, k: place(j, k)))

    y_mla = _matmul(o_mla, w_mb, "nn", [F32], "mla_branch")

    def gate_merge(acc, ga, gb, ya):
        return acc, _sigmoid(ga) * ya + _sigmoid(gb) * acc

    y_fox, merged = _matmul(o_fox, w_fb, "nn", [F32, BF16], "fox_branch_gates", tn=512,
                            extras=((gpre, 0), (gpre, 1), y_mla), epilogue=gate_merge)
    h1 = _matmul(merged, w_o, "nn", [F32], "out_proj", extras=(xs,), epilogue=lambda acc, r: (acc + r,))
    hn = _norm_fwd(h1, mlp_norm, "mlp_norm_fwd")

    def relu2(acc):
        a = jnp.maximum(acc, 0.0)
        return a * a, a

    w_u, w_u_in = b_of("w_up", "nn", 1024, 2048)
    u, a_pos = _matmul(hn, w_u, "nn", [BF16, BF16], "mlp_up", epilogue=relu2, b_in=w_u_in)
    w_d, w_d_in = b_of("w_down", "nn", 1024, 2048)
    h2 = _matmul(u, w_d, "nn", [F32], "mlp_down", tn=1024, extras=(h1,), epilogue=lambda acc, r: (acc + r,),
                 b_in=w_d_in)
    dh2, dh2_b, g_final, loss_part = _final(h2, final_norm.reshape(1, D), target, "final_norm_loss")

    gp_b = lax.empty((N_CHIPS, RB, C), BF16)
    by_glue = {}

    def grad_b(nm, a, b, name):
        nonlocal gp_b
        (K, N), axis = next((shape, axis) for n, shape, axis in pack_b.group if n == nm)
        off = pack_b.offs[nm]
        tm = min(1024, K) if axis == 0 else min(1024, a.shape[1])
        tn = min(1024, N) if axis == 1 else min(1024, b.shape[1])
        if not (N == C and tm % LANE == 0 and tn % LANE == 0 and K % tm == 0 and N % tn == 0 and off % tm == 0):
            by_glue[nm] = _mm_tn(a, b, name)
            return
        base = off // tm
        if axis == 0:
            per = K // tm
            place = lambda i, j: (i // per, base + i % per, j)
        else:
            per = N // tn
            place = lambda i, j: (j // per, base + i, j % per)
        gp_b = _mm_tn(a, b, name, tm=tm, tn=tn, into=(gp_b, place))

    w_d, w_d_in = b_of("w_down", "nt", 1024, 2048)
    da = _matmul(dh2_b, w_d, "nt", [BF16], "mlp_down_dx", extras=(a_pos,),
                 epilogue=lambda acc, a: (acc * (2.0 * a.astype(F32)),), b_in=w_d_in)
    grad_b("w_down", u, dh2_b, "mlp_down_dw")
    w_u, w_u_in = b_of("w_up", "nt", 1024, 2048)
    dhn = _matmul(da, w_u, "nt", [F32], "mlp_up_dx", tn=1024, b_in=w_u_in)
    grad_b("w_up", hn, da, "mlp_up_dw")
    dh1, dh1_b, g_mlp_norm = _norm_bwd(h1, dhn, mlp_norm, dh2, "mlp_norm_bwd")

    def gate_bwd(acc, ga, gb, ya, yb):
        ga, gb = _sigmoid(ga), _sigmoid(gb)
        return acc * ga, acc * gb, acc * ya * (ga * (1.0 - ga)), acc * yb * (gb * (1.0 - gb))

    dy_mla, dy_fox, dg_mla, dg_fox = _matmul(dh1_b, w_o, "nt", [BF16] * 4, "out_proj_dx_gates", tn=512,
                                             extras=((gpre, 0), (gpre, 1), y_mla, y_fox), epilogue=gate_bwd)
    grad_b("w_out", merged, dh1_b, "out_proj_dw")
    do_mla = _matmul(dy_mla, w_mb, "nt", [BF16], "mla_branch_dx")
    grad_b("w_mla_branch", o_mla, dy_mla, "mla_branch_dw")
    do_fox = _matmul(dy_fox, w_fb, "nt", [BF16], "fox_branch_dx")
    grad_b("w_fox_branch", o_fox, dy_fox, "fox_branch_dw")
    for nm, g in by_glue.items():
        gp_b = lax.dynamic_update_slice(gp_b, pack_b.slab_rows(nm, g), (0, pack_b.offs[nm], 0))
    if RB > pack_b.used:
        gp_b = lax.dynamic_update_slice(gp_b, jnp.zeros((N_CHIPS, RB - pack_b.used, C), BF16), (0, pack_b.used, 0))

    rs_b = _xchg_start(gp_b, lax.empty((3, RB, C), BF16), False, do_fox, "grad_scatter_start_b")

    delta_mla = _att_delta_t(do_mla, o_mla, H, "mla_att_delta", order=rs_b[3])
    dq_rot, dk_nope, dkr_heads, dv_mla = _att_bwd_t(mla, do_mla, lse_mla, delta_mla, BF16, [BF16, F32],
                                                    "mla_att_bwd", dq_rope=(qc, qsa, qsb))
    delta_fox = _att_delta_t(do_fox, ox_fox, HF, "fox_att_delta")
    dfq, dfk, dfv, dcum = _att_bwd_t(fox, do_fox, lse_fox, delta_fox, BF16, [BF16], "fox_att_bwd")

    gp_b_sent, recv_b = _xchg_wait(rs_b, False, dfq, "grad_scatter_wait_b")
    swap_b = _sib_start(_sum_slabs(gp_b_sent, recv_b, chip, "grad_sum_b"), "grad_swap_start_b")

    dcqn = _matmul(dq_rot, w_uq_p, "nt", [F32], "mla_q_up_dx", order=swap_b[4])
    g_w_uq_p = _mm_tn(cqn, dq_rot, "mla_q_up_dw")
    dkv2 = jnp.concatenate([dk_nope, dv_mla], axis=1)
    dckvn = _matmul(dkv2, w_ukv_p, "nt", [F32], "mla_kv_up_dx")
    g_w_ukv_p = _mm_tn(ckvn, dkv2, "mla_kv_up_dw")

    dcum_rows = jnp.pad(dcum[:, :, 0], ((0, 8 - HF), (0, 0)))
    dlogf_rows = _suffix_sum_rows(dcum_rows, "fox_forget_suffix_sum")
    dlogf = _pad_cols(jnp.transpose(dlogf_rows[:HF]), LANE)
    d_small, g_q_norm, g_kv_norm, g_bias = _prep_bwd(
        small, dcqn, dckvn, dkr_heads, dlogf, q_norm, kv_norm, bias_pad, kc, ksa, ksb, H, "prep_bwd")
    dproj = [d_small, dfq, dfk, dfv, dg_mla, dg_fox]
    gs, gfq, gfk, gfv, gg_mla, gg_fox = [
        _matmul(part, xn, "tn", [BF16], "proj_dw_" + tag, tm=1024, tn=1024, tk=2048)
        for part, tag in zip(dproj, ("small", "fq", "fk", "fv", "g_mla", "g_fox"))]

    g_w_in = jnp.concatenate([gs[:o_kr], gs[o_kr:o_kr + MLA_ROPE], gfq, gfk, gfv,
                              gs[o_kr + LANE:o_kr + LANE + HF], gg_mla, gg_fox], axis=0)
    g_w_uq = g_w_uq_p.reshape(QL, H, QPAD)[:, :, :dqk].reshape(QL, H * dqk)
    g_w_ukv = jnp.concatenate([g_w_ukv_p[:, :H * MLA_NOPE].reshape(KVL, H, MLA_NOPE),
                               g_w_ukv_p[:, H * MLA_NOPE:].reshape(KVL, H, MLA_V)], axis=2).reshape(KVL, -1)

    gp_a = pack_a.slabs({"w_in": g_w_in, "w_uq": g_w_uq, "w_ukv": g_w_ukv})
    rs_a = _xchg_start(gp_a, lax.empty((3, RA, C), BF16), False, gg_fox, "grad_scatter_start_a")
    dxn = _matmul_parts(dproj, w_pack, "nn", F32, "proj_dx", order=rs_a[3])
    grad_x, _, g_attn_norm = _norm_bwd(xs, dxn, attn_norm, dh1, "attn_norm_bwd")
    gp_a_sent, recv_a = _xchg_wait(rs_a, False, grad_x, "grad_scatter_wait_a")
    swap_a = _sib_start(_sum_slabs(gp_a_sent, recv_a, chip, "grad_sum_a"), "grad_swap_start_a")
    vec_w = max(D, LANE)
    vec_rows = [g_attn_norm, g_mlp_norm, g_final, g_q_norm, g_kv_norm, g_bias, loss_part]
    vec = jnp.concatenate([_pad_cols(v, vec_w) for v in vec_rows] + [jnp.zeros((1, vec_w), F32)], axis=0)
    vsum = _all_reduce_vec(vec, "all_reduce_vectors")
    part_b, sib_b = _sib_wait(swap_b, vsum, "grad_swap_wait_b")

    grads, deltas, new_m, new_v = {}, {}, {}, {}

    def update(pack, mine, theirs):
        for nm, shape, _ in pack.group:
            K, N = shape
            if N == pack.C and K % 8 == 0 and pack.offs[nm] % _tile(K, 256, 8) == 0:
                g, d, nm_, nv_ = _adamw(weights[nm], mine, theirs, moments[nm][0], moments[nm][1], "adamw_" + nm,
                                        g_row=pack.offs[nm])
            else:
                g, d, nm_, nv_ = _adamw(weights[nm], pack.part(mine, nm, shape), pack.part(theirs, nm, shape),
                                        moments[nm][0], moments[nm][1], "adamw_" + nm)
            grads[nm], deltas[nm], new_m[nm], new_v[nm] = g, d, nm_, nv_
        return g

    last_b = update(pack_b, part_b, sib_b)
    part_a, sib_a = _sib_wait(swap_a, last_b, "grad_swap_wait_a")
    update(pack_a, part_a, sib_a)

    vec_names = ["attn_norm", "mlp_norm", "final_norm", "q_norm", "kv_norm", "fox_f_bias"]

    def vec_pack(arrs):
        return jnp.concatenate([_pad_cols(a.reshape(1, -1), vec_w) for a in arrs]
                               + [jnp.zeros((2, vec_w), F32)], axis=0)[None]

    vg, vd, vm, vv = _adamw(vec_pack([weights[n] for n in vec_names]), vsum, jnp.zeros_like(vsum),
                            vec_pack([moments[n][0] for n in vec_names]), vec_pack([moments[n][1] for n in vec_names]),
                            "adamw_vectors")
    for r, nm in enumerate(vec_names):
        shp = weights[nm].shape
        n = weights[nm].size
        grads[nm] = vsum[r, :n].reshape(shp)
        deltas[nm], new_m[nm], new_v[nm] = (vd[0, r, :n].reshape(shp), vm[0, r, :n].reshape(shp),
                                            vv[0, r, :n].reshape(shp))
    loss = vsum[6, 0]

    for res in (grads, deltas, new_m, new_v):
        res["w_in"] = flip(res["w_in"])
    order = ["attn_norm", "w_in", "fox_f_bias", "q_norm", "w_uq", "kv_norm", "w_ukv", "w_mla_branch", "w_fox_branch",
             "w_out", "mlp_norm", "w_up", "w_down", "final_norm"]
    return (loss, grad_x[None], *[grads[n] for n in order], *[deltas[n] for n in order],
            *[new_m[n] for n in order], *[new_v[n] for n in order])
```

```python
import math

import jax
import jax.numpy as jnp
from jax import lax
from jax.experimental import pallas as pl
from jax.experimental.pallas import tpu as pltpu

CHUNK = 64
MLA_HEADS = 8
MLA_Q_LORA = 512
MLA_KV_LORA = 256
MLA_NOPE = 128
MLA_ROPE = 64
MLA_V = 128
ROPE_THETA = 10000.0
FOX_HEADS = 8
FOX_HEAD_DIM = 128
EPS = 1e-6

ADAM_LR = 0.001
ADAM_B1 = 0.9
ADAM_B2 = 0.999
ADAM_EPS = 1e-08
ADAM_WD = 0.01
ADAM_STEP = 10

LANE = 128
QPAD = 2 * LANE
N_CHIPS = 4
N_DEV = 8
VMEM_LIMIT = 48 * 1024 * 1024
ATT_T = 2048
QSUB = 256
ROW_T = 256
PACK_ROWS = 256
LOG2E = 1.4426950408889634

BF16 = jnp.bfloat16
F32 = jnp.float32
MESH = pl.DeviceIdType.MESH

_NT = (((1,), (1,)), ((), ()))
_TN = (((0,), (0,)), ((), ()))
_NN = (((1,), (0,)), ((), ()))


def _tile(dim, pref, align=LANE):
    if dim <= pref:
        return dim
    t = (pref // align) * align
    while t >= align:
        if dim % t == 0:
            return t
        t -= align
    return dim


def _params(sem=None):
    return pltpu.CompilerParams(dimension_semantics=sem, vmem_limit_bytes=VMEM_LIMIT)


_ANY_SPEC = pl.BlockSpec(memory_space=pl.ANY)


def _matmul(a, b, mode, out_dtypes, name, *, tm=1024, tn=1024, tk=2048, extras=(), row_extras=(), col_extras=(),
            epilogue=None, order=None, into=None, b_in=None):
    b_shape = b.shape if b_in is None else b_in[0]
    if mode == "nn":
        (M, K), (K2, N) = a.shape, b_shape
    elif mode == "nt":
        (M, K), (N, K2) = a.shape, b_shape
    else:
        (K, M), (K2, N) = a.shape, b_shape
    assert K == K2, (name, a.shape, b_shape)
    tm, tn, tk = _tile(M, tm), _tile(N, tn), _tile(K, tk)
    nk = K // tk
    extras = [e if isinstance(e, tuple) else (e, 0) for e in extras]
    n_out = len(out_dtypes)
    n_ex = len(extras) + len(row_extras) + len(col_extras)
    n_ord = 0 if order is None else 1
    assert all(r.shape == (M, tn) for r in row_extras), name
    dims = {"nn": _NN, "nt": _NT, "tn": _TN}[mode]

    def body(*refs):
        a_ref, b_ref = refs[0], refs[1]
        ex_refs = refs[2:2 + n_ex]
        o_refs = refs[2 + n_ex + n_ord:2 + n_ex + n_ord + n_out]
        acc_ref = refs[2 + n_ex + n_ord + n_out]
        k = pl.program_id(2)
        part = lax.dot_general(a_ref[...], b_ref[...], dims, preferred_element_type=F32)

        @pl.when(k == 0)
        def _():
            acc_ref[...] = part

        @pl.when(k > 0)
        def _():
            acc_ref[...] += part

        @pl.when(k == nk - 1)
        def _():
            acc = acc_ref[...]
            if epilogue is None:
                outs = (acc,)
            else:
                outs = epilogue(acc, *[r[...] for r in ex_refs])
            for o_ref, o in zip(o_refs, outs):
                o_ref[...] = o.astype(o_ref.dtype)

    if mode == "nn":
        a_spec = pl.BlockSpec((tm, tk), lambda i, j, k: (i, k))
        b_spec = pl.BlockSpec((tk, tn), lambda i, j, k: (k, j))
    elif mode == "nt":
        a_spec = pl.BlockSpec((tm, tk), lambda i, j, k: (i, k))
        b_spec = pl.BlockSpec((tn, tk), lambda i, j, k: (j, k))
    else:
        a_spec = pl.BlockSpec((tk, tm), lambda i, j, k: (k, i))
        b_spec = pl.BlockSpec((tk, tn), lambda i, j, k: (k, j))
    if b_in is not None:
        b_block = (None, tn, tk) if mode == "nt" else (None, tk, tn)
        b_spec = pl.BlockSpec(b_block, lambda i, j, k: b_in[1](j, k))
    mn_spec = pl.BlockSpec((tm, tn), lambda i, j, k: (i, j))
    row_spec = pl.BlockSpec((tm, tn), lambda i, j, k: (i, 0))
    col_spec = pl.BlockSpec((1, tn), lambda i, j, k: (0, j))
    out_specs = [mn_spec] * n_out
    out_shape = [jax.ShapeDtypeStruct((M, N), dt) for dt in out_dtypes]
    aliases = {}
    if into is not None:
        buf, place = into
        assert n_out == 1 and n_ord == 1 and order is buf, name
        out_specs = [pl.BlockSpec((None, tm, tn), lambda i, j, k: place(i, j))]
        out_shape = [jax.ShapeDtypeStruct(buf.shape, buf.dtype)]
        aliases = {2 + n_ex: 0}
    outs = pl.pallas_call(
        body,
        name=name,
        grid=(M // tm, N // tn, nk),
        in_specs=([a_spec, b_spec]
                  + [pl.BlockSpec((tm, tn), lambda i, j, k, g=g: (i, j + g * (N // tn))) for _, g in extras]
                  + [row_spec] * len(row_extras) + [col_spec] * len(col_extras) + [_ANY_SPEC] * n_ord),
        out_specs=out_specs,
        out_shape=out_shape,
        scratch_shapes=[pltpu.VMEM((tm, tn), F32)],
        input_output_aliases=aliases,
        compiler_params=_params(("parallel", "parallel", "arbitrary")),
    )(a, b, *[e for e, _ in extras], *row_extras, *col_extras, *([] if order is None else [order]))
    return outs[0] if n_out == 1 else outs


def _matmul_parts(parts, b, mode, out_dtype, name, *, tm=1024, tn=1024, tk=1024, order=None):
    assert mode in ("nn", "tn")
    if mode == "nn":
        M, (K, N) = parts[0].shape[0], b.shape
        widths = [p.shape[1] for p in parts]
    else:
        K, N = b.shape
        widths = [p.shape[1] for p in parts]
        M = sum(widths)
    common = math.gcd(*widths)
    tm, tn, tk = _tile(M if mode == "nn" else common, tm), _tile(N, tn), _tile(common if mode == "nn" else K, tk)
    t_part = tk if mode == "nn" else tm
    assert sum(widths) == (K if mode == "nn" else M), name
    if any(w % t_part for w in widths):
        parts, widths = [jnp.concatenate(parts, axis=1)], [sum(widths)]
    lo =[sum(widths[:p]) // t_part for p in range(len(parts))]
    cnt = [w // t_part for w in widths]
    nk = K // tk
    n_parts = len(parts)
    n_ord = 0 if order is None else 1
    dims = _NN if mode == "nn" else _TN

    def body(*refs):
        a_refs = refs[0:n_parts]
        b_ref = refs[n_parts]
        o_ref, acc_ref = refs[n_parts + 1 + n_ord], refs[n_parts + 2 + n_ord]
        i, k = pl.program_id(0), pl.program_id(2)
        sel = k if mode == "nn" else i
        for p in range(n_parts):
            @pl.when((sel >= lo[p]) & (sel < lo[p] + cnt[p]))
            def _(p=p):
                part = lax.dot_general(a_refs[p][...], b_ref[...], dims, preferred_element_type=F32)

                @pl.when(k == 0)
                def _():
                    acc_ref[...] = part

                @pl.when(k > 0)
                def _():
                    acc_ref[...] += part

        @pl.when(k == nk - 1)
        def _():
            o_ref[...] = acc_ref[...].astype(o_ref.dtype)

    def a_spec(p):
        if mode == "nn":
            return pl.BlockSpec((tm, tk), lambda i, j, k: (i, jnp.clip(k - lo[p], 0, cnt[p] - 1)))
        return pl.BlockSpec((tk, tm), lambda i, j, k: (
            jnp.where((i >= lo[p]) & (i < lo[p] + cnt[p]), k, 0), jnp.clip(i - lo[p], 0, cnt[p] - 1)))

    return pl.pallas_call(
        body, name=name, grid=(M // tm, N // tn, nk),
        in_specs=[a_spec(p) for p in range(n_parts)] + [pl.BlockSpec((tk, tn), lambda i, j, k: (k, j))]
        + [_ANY_SPEC] * n_ord,
        out_specs=pl.BlockSpec((tm, tn), lambda i, j, k: (i, j)),
        out_shape=jax.ShapeDtypeStruct((M, N), out_dtype),
        scratch_shapes=[pltpu.VMEM((tm, tn), F32)],
        compiler_params=_params(("parallel", "parallel", "arbitrary")),
    )(*parts, b, *([] if order is None else [order]))


def _mm_tn(a, b, name, tm=1024, tn=1024, into=None):
    return _matmul(a, b, "tn", [F32], name, tm=tm, tn=tn, tk=2048, into=into,
                   order=None if into is None else into[0])


def _row_spec(ts, width, col=0):
    return pl.BlockSpec((ts, width), lambda i: (i, col))


def _full_spec(shape):
    return pl.BlockSpec(shape, lambda i: tuple(0 for _ in shape))


def _rms(x):
    return lax.rsqrt(jnp.mean(x * x, axis=-1, keepdims=True) + EPS)


def _rms_bwd(x, dy, g):
    r = _rms(x)
    xh = x * r
    gy = dy * g
    dx = r * (gy - xh * jnp.mean(xh * gy, axis=-1, keepdims=True))
    return dx, dy * xh


def _norm_fwd(x, g, name, order=None):
    S, D = x.shape
    ts = _tile(S, ROW_T, 8)

    def body(x_ref, g_ref, *rest):
        o_ref = rest[-1]
        xv = x_ref[...]
        o_ref[...] = ((xv * _rms(xv)) * g_ref[...]).astype(BF16)

    extra = [] if order is None else [order]
    return pl.pallas_call(
        body, name=name, grid=(S // ts,),
        in_specs=[_row_spec(ts, D), _full_spec((1, D))] + [_ANY_SPEC] * len(extra),
        out_specs=_row_spec(ts, D),
        out_shape=jax.ShapeDtypeStruct((S, D), BF16),
        compiler_params=_params(("parallel",)),
    )(x, g, *extra)


def _norm_bwd(x, dy, g, dres, name):
    S, D = x.shape
    ts = _tile(S, ROW_T, 8)

    def body(x_ref, dy_ref, g_ref, dres_ref, dx_ref, dxb_ref, dg_ref):
        dx, dg_rows = _rms_bwd(x_ref[...], dy_ref[...], g_ref[...])
        dx = dres_ref[...] + dx
        dx_ref[...] = dx
        dxb_ref[...] = dx.astype(BF16)

        @pl.when(pl.program_id(0) == 0)
        def _():
            dg_ref[...] = jnp.zeros_like(dg_ref)

        dg_ref[...] += jnp.sum(dg_rows, axis=0, keepdims=True)

    return pl.pallas_call(
        body, name=name, grid=(S // ts,),
        in_specs=[_row_spec(ts, D), _row_spec(ts, D), _full_spec((1, D)), _row_spec(ts, D)],
        out_specs=[_row_spec(ts, D), _row_spec(ts, D), _full_spec((1, D))],
        out_shape=[jax.ShapeDtypeStruct((S, D), F32), jax.ShapeDtypeStruct((S, D), BF16),
                   jax.ShapeDtypeStruct((1, D), F32)],
        compiler_params=_params(("arbitrary",)),
    )(x, dy, g, dres)


def _rope(x, c, sa, sb, sign):
    w = x.shape[-1]
    half = MLA_ROPE // 2
    fwd = pltpu.roll(x, w - half, 1)
    back = pltpu.roll(x, half, 1)
    if sign < 0:
        return x * c - fwd * sa - back * sb
    return x * c + fwd * sa + back * sb


def _split3(x):
    hi = x.astype(BF16)
    r1 = x - hi.astype(F32)
    mid = r1.astype(BF16)
    lo = (r1 - mid.astype(F32)).astype(BF16)
    return hi, mid, lo


def _prep_fwd(small, q_norm, kv_norm, bias_pad, kc, ksa, ksb, n_heads, name):
    S, W = small.shape
    QL, KVL = q_norm.shape[1], kv_norm.shape[1]
    assert W == QL + KVL + 2 * LANE
    ts = _tile(S, ROW_T, 8)
    tri = (lax.broadcasted_iota(jnp.int32, (ts, ts), 0) >= lax.broadcasted_iota(jnp.int32, (ts, ts), 1)).astype(BF16)

    def body(s_ref, qn_ref, kvn_ref, b_ref, kc_ref, ksa_ref, ksb_ref, tri_ref,
             cqn_ref, ckvn_ref, kr_ref, cum_ref, carry_ref):
        cq = s_ref[:, 0:QL]
        cqn_ref[...] = ((cq * _rms(cq)) * qn_ref[...]).astype(BF16)
        ckv = s_ref[:, QL:QL + KVL]
        ckvn_ref[...] = ((ckv * _rms(ckv)) * kvn_ref[...]).astype(BF16)
        kr = s_ref[:, QL + KVL:QL + KVL + LANE]
        kr_ref[...] = _rope(kr, kc_ref[...], ksa_ref[...], ksb_ref[...], 1).astype(BF16)
        z = s_ref[:, QL + KVL + LANE:W] + b_ref[...]
        logf = jnp.minimum(z, 0.0) - jnp.log1p(jnp.exp(-jnp.abs(z)))
        lane = lax.broadcasted_iota(jnp.int32, logf.shape, 1)
        logf = jnp.where(lane < n_heads, logf, 0.0)

        @pl.when(pl.program_id(0) == 0)
        def _():
            carry_ref[...] = jnp.zeros_like(carry_ref)

        t = tri_ref[...]
        cum = carry_ref[...]
        for part in _split3(logf):
            cum = cum + jnp.dot(t, part, preferred_element_type=F32)
        cum_ref[...] = cum
        carry_ref[...] = cum[ts - 1:ts, :]

    return pl.pallas_call(
        body, name=name, grid=(S // ts,),
        in_specs=[_row_spec(ts, W), _full_spec((1, QL)), _full_spec((1, KVL)), _full_spec((1, LANE)),
                  _row_spec(ts, LANE), _row_spec(ts, LANE), _row_spec(ts, LANE), _full_spec((ts, ts))],
        out_specs=[_row_spec(ts, QL), _row_spec(ts, KVL), _row_spec(ts, LANE), _row_spec(ts, LANE)],
        out_shape=[jax.ShapeDtypeStruct((S, QL), BF16), jax.ShapeDtypeStruct((S, KVL), BF16),
                   jax.ShapeDtypeStruct((S, LANE), BF16), jax.ShapeDtypeStruct((S, LANE), F32)],
        scratch_shapes=[pltpu.VMEM((1, LANE), F32)],
        compiler_params=_params(("arbitrary",)),
    )(small, q_norm, kv_norm, bias_pad, kc, ksa, ksb, tri)


def _prep_bwd(small, dcqn, dckvn, dkr_heads, dlogf, q_norm, kv_norm, bias_pad, kc, ksa, ksb, n_heads, name):
    S, W = small.shape
    QL, KVL = q_norm.shape[1], kv_norm.shape[1]
    ts = _tile(S, ROW_T, 8)

    def body(s_ref, dcq_ref, dckv_ref, dkr_ref, dlf_ref, qn_ref, kvn_ref, b_ref, kc_ref, ksa_ref, ksb_ref,
             ds_ref, gq_ref, gkv_ref, gb_ref):
        dcq, gq_rows = _rms_bwd(s_ref[:, 0:QL], dcq_ref[...], qn_ref[...])
        ds_ref[:, 0:QL] = dcq.astype(BF16)
        dckv, gkv_rows = _rms_bwd(s_ref[:, QL:QL + KVL], dckv_ref[...], kvn_ref[...])
        ds_ref[:, QL:QL + KVL] = dckv.astype(BF16)
        dkr = dkr_ref[:, 0:LANE]
        for h in range(1, n_heads):
            dkr = dkr + dkr_ref[:, h * LANE:(h + 1) * LANE]
        ds_ref[:, QL + KVL:QL + KVL + LANE] = _rope(dkr, kc_ref[...], ksa_ref[...], ksb_ref[...], -1).astype(BF16)
        z = s_ref[:, QL + KVL + LANE:W] + b_ref[...]
        dff = dlf_ref[...] * (1.0 / (1.0 + jnp.exp(z)))
        ds_ref[:, QL + KVL + LANE:W] = dff.astype(BF16)

        @pl.when(pl.program_id(0) == 0)
        def _():
            gq_ref[...] = jnp.zeros_like(gq_ref)
            gkv_ref[...] = jnp.zeros_like(gkv_ref)
            gb_ref[...] = jnp.zeros_like(gb_ref)

        gq_ref[...] += jnp.sum(gq_rows, axis=0, keepdims=True)
        gkv_ref[...] += jnp.sum(gkv_rows, axis=0, keepdims=True)
        gb_ref[...] += jnp.sum(dff, axis=0, keepdims=True)

    return pl.pallas_call(
        body, name=name, grid=(S // ts,),
        in_specs=[_row_spec(ts, W), _row_spec(ts, QL), _row_spec(ts, KVL), _row_spec(ts, n_heads * LANE),
                  _row_spec(ts, LANE), _full_spec((1, QL)), _full_spec((1, KVL)), _full_spec((1, LANE)),
                  _row_spec(ts, LANE), _row_spec(ts, LANE), _row_spec(ts, LANE)],
        out_specs=[_row_spec(ts, W), _full_spec((1, QL)), _full_spec((1, KVL)), _full_spec((1, LANE))],
        out_shape=[jax.ShapeDtypeStruct((S, W), BF16), jax.ShapeDtypeStruct((1, QL), F32),
                   jax.ShapeDtypeStruct((1, KVL), F32), jax.ShapeDtypeStruct((1, LANE), F32)],
        compiler_params=_params(("arbitrary",)),
    )(small, dcqn, dckvn, dkr_heads, dlogf, q_norm, kv_norm, bias_pad, kc, ksa, ksb)


def _sigmoid(z):
    return 1.0 / (1.0 + jnp.exp(-z))


def _final(h, g, target, name):
    S, D = h.shape
    ts = _tile(S, ROW_T, 8)

    def body(h_ref, g_ref, t_ref, dh_ref, dhb_ref, dg_ref, loss_ref):
        hv = h_ref[...]
        gv = g_ref[...]
        err = (hv * _rms(hv)) * gv - t_ref[...]
        dh, dg_rows = _rms_bwd(hv, err / D, gv)
        dh_ref[...] = dh
        dhb_ref[...] = dh.astype(BF16)

        @pl.when(pl.program_id(0) == 0)
        def _():
            dg_ref[...] = jnp.zeros_like(dg_ref)
            loss_ref[...] = jnp.zeros_like(loss_ref)

        dg_ref[...] += jnp.sum(dg_rows, axis=0, keepdims=True)
        row_loss = jnp.mean(err * err, axis=-1, keepdims=True)
        loss_ref[...] += 0.5 * jnp.sum(row_loss, axis=0, keepdims=True)

    return pl.pallas_call(
        body, name=name, grid=(S // ts,),
        in_specs=[_row_spec(ts, D), _full_spec((1, D)), _row_spec(ts, D)],
        out_specs=[_row_spec(ts, D), _row_spec(ts, D), _full_spec((1, D)), _full_spec((1, LANE))],
        out_shape=[jax.ShapeDtypeStruct((S, D), F32), jax.ShapeDtypeStruct((S, D), BF16),
                   jax.ShapeDtypeStruct((1, D), F32), jax.ShapeDtypeStruct((1, LANE), F32)],
        compiler_params=_params(("arbitrary",)),
    )(h, g, target)


def _suffix_sum_rows(x, name):
    R, S = x.shape
    tb = _tile(S, 512)
    nb = S // tb
    tri = (lax.broadcasted_iota(jnp.int32, (tb, tb), 0) >= lax.broadcasted_iota(jnp.int32, (tb, tb), 1)).astype(BF16)

    def body(x_ref, tri_ref, o_ref, carry_ref):
        @pl.when(pl.program_id(0) == 0)
        def _():
            carry_ref[...] = jnp.zeros_like(carry_ref)

        xv = x_ref[...]
        t = tri_ref[...]
        acc = jnp.broadcast_to(carry_ref[:, 0:1], xv.shape)
        for part in _split3(xv):
            acc = acc + jnp.dot(part, t, preferred_element_type=F32)
        o_ref[...] = acc
        carry_ref[...] = jnp.broadcast_to(acc[:, 0:1], carry_ref.shape)

    rev = pl.BlockSpec((R, tb), lambda i: (0, nb - 1 - i))
    return pl.pallas_call(
        body, name=name, grid=(nb,),
        in_specs=[rev, _full_spec((tb, tb))], out_specs=rev,
        out_shape=jax.ShapeDtypeStruct((R, S), F32),
        scratch_shapes=[pltpu.VMEM((R, LANE), F32)],
        compiler_params=_params(("arbitrary",)),
    )(x, tri)


def _pairs(nb, by_key):
    if by_key:
        pr = [(i, j) for j in range(nb) for i in range(j, nb)]
    else:
        pr = [(i, j) for i in range(nb) for j in range(i + 1)]
    return (jnp.asarray([p[0] for p in pr], jnp.int32), jnp.asarray([p[1] for p in pr], jnp.int32), len(pr))


class _AttT:
    def __init__(self, S, n_heads, q, ks, v, scale, chunk_causal, cum_rep=None):
        self.S, self.H, self.q, self.ks, self.v = S, n_heads, q, ks, v
        self.scale, self.chunk_causal, self.cum_rep = scale, chunk_causal, cum_rep
        self.T = _tile(S, ATT_T)
        self.qs = min(QSUB, self.T)
        self.nb = S // self.T
        self.dq, self.dv = q[1], v[1]
        self.has_bias = cum_rep is not None

    def q_spec(self, op):
        _, w, off, per_head = op
        return pl.BlockSpec((self.T, w), lambda h, p, it, jt: (it[p], off + (h if per_head else 0)))

    def k_spec(self, op):
        _, w, off, per_head = op
        return pl.BlockSpec((self.T, w), lambda h, p, it, jt: (jt[p], off + (h if per_head else 0)))

    def row_q(self):
        return pl.BlockSpec((None, 1, self.T), lambda h, p, it, jt: (h, 0, it[p]))

    def cum_k(self):
        return pl.BlockSpec((None, self.T, self.qs), lambda h, p, it, jt: (h, jt[p], 0))

    def sub_blocks(self, masked):
        return [(q0, min(self.T, q0 + self.qs) if masked else self.T) for q0 in range(0, self.T, self.qs)]

    def scores(self, k, q_sub, cum, q0, masked):
        s = lax.dot_general(k, q_sub, _NT, preferred_element_type=F32)
        if self.has_bias:
            s = s - cum
        mask = None
        if masked:
            r = lax.broadcasted_iota(jnp.int32, s.shape, 0)
            c = lax.broadcasted_iota(jnp.int32, s.shape, 1) + q0
            mask = (r // CHUNK <= c // CHUNK) if self.chunk_causal else (r <= c)
        return s, mask


def _join(k_refs):
    return k_refs[0][...] if len(k_refs) == 1 else jnp.concatenate([r[...] for r in k_refs], axis=-1)


def _att_fwd_t(att, name, exact=False):
    S, H, T, qs = att.S, att.H, att.T, att.qs
    it, jt, npairs = _pairs(att.nb, by_key=False)
    nk = len(att.ks)

    def body(it_ref, jt_ref, *refs):
        q_ref = refs[0]
        k_refs = refs[1:1 + nk]
        v_ref = refs[1 + nk]
        n = 2 + nk
        cum_ref = None
        if att.has_bias:
            cum_ref = refs[n]
            n += 1
        o_ref = refs[n]
        n += 1
        ox_ref = None
        if exact:
            ox_ref = refs[n]
            n += 1
        lse_ref, m_ref, l_ref, acc_ref = refs[n:n + 4]
        lo_ref = refs[n + 4] if exact else None
        p = pl.program_id(1)
        i, j = it_ref[p], jt_ref[p]

        @pl.when(j == 0)
        def _():
            m_ref[...] = jnp.full_like(m_ref, -jnp.inf)
            l_ref[...] = jnp.zeros_like(l_ref)
            acc_ref[...] = jnp.zeros_like(acc_ref)
            if exact:
                lo_ref[...] = jnp.zeros_like(lo_ref)

        def step(masked):
            k = _join(k_refs)
            v = v_ref[...]
            subs = att.sub_blocks(masked)

            def logits(idx):
                q0, nkeys = subs[idx]
                cum = cum_ref[0:nkeys, :] if att.has_bias else None
                return att.scores(k[0:nkeys], q_ref[q0:q0 + qs, :], cum, q0, masked)

            ahead = logits(0)
            for idx, (q0, nkeys) in enumerate(subs):
                qsl = slice(q0, q0 + qs)
                s, mask = ahead
                if idx + 1 < len(subs):
                    ahead = logits(idx + 1)
                if masked:
                    s = jnp.where(mask, s, -jnp.inf)
                m_prev = m_ref[:, qsl]
                m_new = jnp.maximum(m_prev, jnp.max(s, axis=0, keepdims=True))
                alpha = jnp.exp2(m_prev - m_new)
                pr = jnp.exp2(s - m_new)
                l_ref[:, qsl] = alpha * l_ref[:, qsl] + jnp.sum(pr, axis=0, keepdims=True)
                p_hi = pr.astype(BF16)
                acc_ref[:, qsl] = alpha * acc_ref[:, qsl] + lax.dot_general(
                    v[0:nkeys], p_hi, _TN, preferred_element_type=F32)
                if exact:
                    p_lo = (pr - p_hi.astype(F32)).astype(BF16)
                    lo_ref[:, qsl] = alpha * lo_ref[:, qsl] + lax.dot_general(
                        v[0:nkeys], p_lo, _TN, preferred_element_type=F32)
                m_ref[:, qsl] = m_new

        @pl.when(j < i)
        def _():
            step(False)

        @pl.when(j == i)
        def _():
            step(True)
            l = l_ref[...]
            inv = 1.0 / l
            o_ref[...] = jnp.transpose(acc_ref[...] * inv).astype(o_ref.dtype)
            if exact:
                ox_ref[...] = jnp.transpose((acc_ref[...] + lo_ref[...]) * inv)
            lse_ref[...] = m_ref[...] + jnp.log2(l)

    in_specs = [att.q_spec(att.q)] + [att.k_spec(k) for k in att.ks] + [att.k_spec(att.v)]
    args = [att.q[0]] + [k[0] for k in att.ks] + [att.v[0]]
    if att.has_bias:
        in_specs.append(att.cum_k())
        args.append(att.cum_rep)
    o_spec = pl.BlockSpec((T, att.dv), lambda h, p, it, jt: (it[p], h))
    out_specs = [o_spec]
    out_shape = [jax.ShapeDtypeStruct((S, H * att.dv), BF16)]
    scratch = [pltpu.VMEM((1, T), F32), pltpu.VMEM((1, T), F32), pltpu.VMEM((att.dv, T), F32)]
    if exact:
        out_specs.append(o_spec)
        out_shape.append(jax.ShapeDtypeStruct((S, H * att.dv), F32))
        scratch.append(pltpu.VMEM((att.dv, T), F32))
    out_specs.append(att.row_q())
    out_shape.append(jax.ShapeDtypeStruct((H, 1, S), F32))
    return pl.pallas_call(
        body, name=name,
        grid_spec=pltpu.PrefetchScalarGridSpec(
            num_scalar_prefetch=2, grid=(H, npairs), in_specs=in_specs, out_specs=out_specs,
            scratch_shapes=scratch),
        out_shape=out_shape,
        compiler_params=_params(("parallel", "arbitrary")),
    )(it, jt, *args)


def _att_delta_t(do, o, n_heads, name, order=None):
    S = do.shape[0]
    w = do.shape[1] // n_heads
    ts = _tile(S, ATT_T)
    ones = jnp.ones((8, w), BF16)
    extra = [] if order is None else [order]

    def body(do_ref, o_ref, ones_ref, *rest):
        d_ref = rest[-1]
        prod = do_ref[...].astype(F32) * o_ref[...].astype(F32)
        acc = jnp.zeros((8, ts), F32)
        for part in _split3(prod):
            acc = acc + lax.dot_general(ones_ref[...], part, _NT, preferred_element_type=F32)
        d_ref[...] = acc[0:1, :]

    blk = pl.BlockSpec((ts, w), lambda i, h: (i, h))
    return pl.pallas_call(
        body, name=name, grid=(S // ts, n_heads),
        in_specs=[blk, blk, pl.BlockSpec((8, w), lambda i, h: (0, 0))] + [_ANY_SPEC] * len(extra),
        out_specs=pl.BlockSpec((None, 1, ts), lambda i, h: (h, 0, i)),
        out_shape=jax.ShapeDtypeStruct((n_heads, 1, S), F32),
        compiler_params=_params(("parallel", "parallel")),
    )(do, o, ones, *extra)


def _att_bwd_t(att, do, lse, delta, dq_dtype, dk_dtypes, name, dq_rope=None):
    S, H, T, qs = att.S, att.H, att.T, att.qs
    it, jt, npairs = _pairs(att.nb, by_key=True)
    nk = len(att.ks)
    last = att.nb - 1
    widths = [k[1] for k in att.ks]

    def body(it_ref, jt_ref, *refs):
        q_ref = refs[0]
        k_refs = refs[1:1 + nk]
        v_ref, do_ref, lse_ref, dl_ref = refs[1 + nk:5 + nk]
        n = 5 + nk
        cum_ref = None
        if att.has_bias:
            cum_ref = refs[n]
            n += 1
        rope_refs = None
        if dq_rope is not None:
            rope_refs = refs[n:n + 3]
            n += 3
        dq_ref = refs[n]
        dk_refs = refs[n + 1:n + 1 + nk]
        dv_ref = refs[n + 1 + nk]
        n += nk + 2
        dc_ref = None
        if att.has_bias:
            dc_ref = refs[n]
            n += 1
        dq_acc, dk_acc, dv_acc = refs[n:n + 3]
        dc_acc = refs[n + 3] if att.has_bias else None
        p = pl.program_id(1)
        i, j = it_ref[p], jt_ref[p]

        @pl.when(p == 0)
        def _():
            dq_acc[...] = jnp.zeros_like(dq_acc)

        @pl.when(i == j)
        def _():
            dk_acc[...] = jnp.zeros_like(dk_acc)
            dv_acc[...] = jnp.zeros_like(dv_acc)
            if att.has_bias:
                dc_acc[...] = jnp.zeros_like(dc_acc)

        def step(masked):
            k = _join(k_refs)
            v = v_ref[...]
            subs = att.sub_blocks(masked)

            def logits(idx):
                q0, nkeys = subs[idx]
                cum = cum_ref[0:nkeys, :] if att.has_bias else None
                return att.scores(k[0:nkeys], q_ref[q0:q0 + qs, :], cum, q0, masked)

            ahead = logits(0)
            for idx, (q0, nkeys) in enumerate(subs):
                qsl = slice(q0, q0 + qs)
                ksl = slice(0, nkeys)
                q_sub = q_ref[qsl, :]
                do_sub = do_ref[qsl, :]
                s, mask = ahead
                if idx + 1 < len(subs):
                    ahead = logits(idx + 1)
                pr = jnp.exp2(s - lse_ref[:, qsl])
                if masked:
                    pr = jnp.where(mask, pr, 0.0)
                dp = lax.dot_general(v[ksl], do_sub, _NT, preferred_element_type=F32)
                ds = pr * (dp - dl_ref[:, qsl])
                ds_b = ds.astype(BF16)
                dv_acc[ksl, :] += jnp.dot(pr.astype(BF16), do_sub, preferred_element_type=F32)
                dk_acc[ksl, :] += jnp.dot(ds_b, q_sub, preferred_element_type=F32)
                dq_acc[i, :, qsl] += lax.dot_general(k[ksl], ds_b, _TN, preferred_element_type=F32)
                if att.has_bias:
                    part = ds[:, 0:LANE] if qs >= LANE else ds
                    for c0 in range(LANE, qs, LANE):
                        part = part + ds[:, c0:c0 + LANE]
                    dc_acc[ksl, :] += part

        @pl.when(i > j)
        def _():
            step(False)

        @pl.when(i == j)
        def _():
            step(True)
            dq = jnp.transpose(dq_acc[i] * att.scale)
            if dq_rope is not None:
                dq = _rope(dq, rope_refs[0][...], rope_refs[1][...], rope_refs[2][...], -1)
            dq_ref[...] = dq.astype(dq_ref.dtype)

        @pl.when(i == last)
        def _():
            dk = dk_acc[...] * (1.0 / LOG2E)
            off = 0
            for r, w in zip(dk_refs, widths):
                r[...] = dk[:, off:off + w].astype(r.dtype)
                off += w
            dv_ref[...] = dv_acc[...].astype(dv_ref.dtype)
            if att.has_bias:
                dc_ref[...] = -jnp.sum(dc_acc[...], axis=-1, keepdims=True)

    do_op = (do, att.dv, 0, True)
    in_specs = ([att.q_spec(att.q)] + [att.k_spec(k) for k in att.ks]
                + [att.k_spec(att.v), att.q_spec(do_op), att.row_q(), att.row_q()])
    args = [att.q[0]] + [k[0] for k in att.ks] + [att.v[0], do, lse, delta]
    if att.has_bias:
        in_specs.append(att.cum_k())
        args.append(att.cum_rep)
    if dq_rope is not None:
        in_specs += [pl.BlockSpec((T, att.dq), lambda h, p, it, jt: (jt[p], 0))] * 3
        args += list(dq_rope)
    out_specs = [pl.BlockSpec((T, att.dq), lambda h, p, it, jt: (jt[p], h))]
    out_shape = [jax.ShapeDtypeStruct((S, H * att.dq), dq_dtype)]
    out_specs += [pl.BlockSpec((T, w), lambda h, p, it, jt: (jt[p], h)) for w in widths]
    out_shape += [jax.ShapeDtypeStruct((S, H * w), dt) for w, dt in zip(widths, dk_dtypes)]
    out_specs.append(pl.BlockSpec((T, att.dv), lambda h, p, it, jt: (jt[p], h)))
    out_shape.append(jax.ShapeDtypeStruct((S, H * att.dv), BF16))
    scratch = [pltpu.VMEM((att.nb, att.dq, T), F32), pltpu.VMEM((T, att.dq), F32), pltpu.VMEM((T, att.dv), F32)]
    if att.has_bias:
        out_specs.append(pl.BlockSpec((None, T, 1), lambda h, p, it, jt: (h, jt[p], 0)))
        out_shape.append(jax.ShapeDtypeStruct((H, S, 1), F32))
        scratch.append(pltpu.VMEM((T, min(qs, LANE)), F32))
    return pl.pallas_call(
        body, name=name,
        grid_spec=pltpu.PrefetchScalarGridSpec(
            num_scalar_prefetch=2, grid=(H, npairs), in_specs=in_specs, out_specs=out_specs,
            scratch_shapes=scratch),
        out_shape=out_shape,
        compiler_params=_params(("parallel", "arbitrary")),
    )(it, jt, *args)


def _adamw(w, g1, g2, m, v, name, g_row=None):
    _, K, N = w.shape
    by_rows = K % 8 == 0
    tr = _tile(K, 256, 8) if by_rows else K
    if g_row is None:
        assert g1.shape == (K, N) and g2.shape == (K, N), name
        g_row = 0
    assert by_rows and g_row % tr == 0 or g_row == 0, name
    g_blk = g_row // tr
    tc = N if by_rows else _tile(N, LANE)
    c1 = 1.0 - ADAM_B1 ** ADAM_STEP
    c2 = 1.0 - ADAM_B2 ** ADAM_STEP

    def body(w_ref, g1_ref, g2_ref, m_ref, v_ref, g_ref, d_ref, nm_ref, nv_ref):
        gv = g1_ref[...] + g2_ref[...]
        nm = ADAM_B1 * m_ref[...] + (1.0 - ADAM_B1) * gv
        nv = ADAM_B2 * v_ref[...] + (1.0 - ADAM_B2) * (gv * gv)
        g_ref[...] = gv
        d_ref[...] = -ADAM_LR * ((nm / c1) / (jnp.sqrt(nv / c2) + ADAM_EPS) + ADAM_WD * w_ref[...])
        nm_ref[...] = nm
        nv_ref[...] = nv

    if by_rows:
        blk = pl.BlockSpec((None, tr, N), lambda i: (0, i, 0))
        gblk = pl.BlockSpec((tr, N), lambda i: (g_blk + i, 0))
    else:
        blk = pl.BlockSpec((None, K, tc), lambda i: (0, 0, i))
        gblk = pl.BlockSpec((K, tc), lambda i: (0, i))
    return pl.pallas_call(
        body, name=name, grid=(K // tr if by_rows else N // tc,),
        in_specs=[blk, gblk, gblk, blk, blk], out_specs=[blk] * 4,
        out_shape=[jax.ShapeDtypeStruct((1, K, N), F32)] * 4,
        compiler_params=_params(("parallel",)),
    )(w, g1, g2, m, v)


_HBM_SPEC = pl.BlockSpec(memory_space=pltpu.HBM)
_SEM_SPEC = pl.BlockSpec(memory_space=pltpu.SEMAPHORE)
_VMEM_SPEC = pl.BlockSpec(memory_space=pltpu.VMEM)
_EFFECT = pltpu.SideEffectType.DATAFLOW_SIDE_EFFECTING


def _place():
    return lax.axis_index("x"), lax.axis_index("y"), lax.axis_index("c")


def _other_chips(x, y):
    return [(1 - x, y), (x, 1 - y), (1 - x, 1 - y)]


def _chip_copies(src_ref, land_ref, sems, gather):
    x, y, c = _place()
    me = 2 * x + y
    out, back = [], []
    if gather == "half":
        half = src_ref.shape[0] // 2
        mine = pl.ds(pl.multiple_of(c * half, 16), half)
    for n, (px, py) in enumerate(_other_chips(x, y)):
        if gather == "half":
            src, there, here = src_ref.at[mine], land_ref.at[me, mine], land_ref.at[2 * px + py, mine]
        elif gather:
            src, there, here = src_ref, land_ref.at[me], land_ref.at[2 * px + py]
        else:
            src, there, here = src_ref.at[2 * px + py], land_ref.at[n], land_ref.at[n]
        out.append(pltpu.make_async_remote_copy(
            src_ref=src, dst_ref=there, send_sem=sems[n], recv_sem=sems[3 + n],
            device_id=(px, py, c), device_id_type=MESH))
        back.append(pltpu.make_async_remote_copy(
            src_ref=src, dst_ref=here, send_sem=sems[n], recv_sem=sems[3 + n],
            device_id=(px, py, c), device_id_type=MESH))
    return out, back


def _xchg_start(src, land, gather, order, name):
    def body(src_ref, land_ref, order_ref, *outs):
        sems = outs[0:6]
        token = outs[8]
        out, _ = _chip_copies(src_ref, land_ref, sems, gather)
        for cp in out:
            cp.start()
        token[...] = jnp.zeros_like(token)

    outs = pl.pallas_call(
        body, name=name,
        out_shape=(pltpu.SemaphoreType.DMA(()),) * 6 + (
            pltpu.HBM(src.shape, src.dtype), pltpu.HBM(land.shape, land.dtype),
            jax.ShapeDtypeStruct((8, LANE), F32)),
        in_specs=(_HBM_SPEC, _HBM_SPEC, _ANY_SPEC),
        out_specs=(_SEM_SPEC,) * 6 + (_HBM_SPEC, _HBM_SPEC, _VMEM_SPEC),
        input_output_aliases={0: 6, 1: 7},
        compiler_params=pltpu.CompilerParams(has_side_effects=_EFFECT),
    )(pltpu.with_memory_space_constraint(src, pltpu.HBM), pltpu.with_memory_space_constraint(land, pltpu.HBM), order)
    return outs[0:6], outs[6], outs[7], outs[8]


def _xchg_wait(started, gather, after, name):
    sems, src, land, _ = started
    after = after if isinstance(after, tuple) else (after,)

    def body(src_ref, land_ref, *rest):
        _, back = _chip_copies(src_ref, land_ref, rest[0:6], gather)
        for cp in back:
            cp.wait_send()
            cp.wait_recv()

    return pl.pallas_call(
        body, name=name,
        out_shape=(pltpu.HBM(src.shape, src.dtype), pltpu.HBM(land.shape, land.dtype)),
        in_specs=(_HBM_SPEC, _HBM_SPEC) + (_SEM_SPEC,) * 6 + (_ANY_SPEC,) * len(after),
        out_specs=(_HBM_SPEC, _HBM_SPEC),
        input_output_aliases={0: 0, 1: 1},
        compiler_params=pltpu.CompilerParams(has_side_effects=_EFFECT),
    )(src, land, *sems, *after)


def _forward_halves(land, name):
    _, R, C = land.shape
    half = R // 2
    assert half % 16 == 0

    def body(land_ref, out_ref, send_sems, recv_sems):
        x, y, c = _place()
        mine = pl.ds(pl.multiple_of(c * half, 16), half)
        theirs = pl.ds(pl.multiple_of((1 - c) * half, 16), half)
        sends = []
        for n, (px, py) in enumerate(_other_chips(x, y)):
            cp = pltpu.make_async_remote_copy(
                src_ref=land_ref.at[2 * px + py, mine], dst_ref=out_ref.at[2 * px + py, mine],
                send_sem=send_sems.at[n], recv_sem=recv_sems.at[n], device_id=(x, y, 1 - c), device_id_type=MESH)
            cp.start()
            sends.append(cp)
        for n, (px, py) in enumerate(_other_chips(x, y)):
            pltpu.make_async_remote_copy(
                src_ref=land_ref.at[2 * px + py, theirs], dst_ref=out_ref.at[2 * px + py, theirs],
                send_sem=send_sems.at[n], recv_sem=recv_sems.at[n], device_id=(x, y, 1 - c),
                device_id_type=MESH).wait_recv()
        for cp in sends:
            cp.wait_send()

    return pl.pallas_call(
        body, name=name,
        in_specs=[_ANY_SPEC], out_specs=_ANY_SPEC,
        out_shape=jax.ShapeDtypeStruct(land.shape, land.dtype),
        input_output_aliases={0: 0},
        scratch_shapes=[pltpu.SemaphoreType.DMA((3,)), pltpu.SemaphoreType.DMA((3,))],
    )(land)


def _sib_copy(src_ref, land_ref, send_sem, recv_sem):
    x, y, c = _place()
    return pltpu.make_async_remote_copy(src_ref=src_ref, dst_ref=land_ref, send_sem=send_sem, recv_sem=recv_sem,
                                        device_id=(x, y, 1 - c), device_id_type=MESH)


def _sib_start(src, name):
    land = lax.empty(src.shape, src.dtype)

    def body(src_ref, land_ref, send_sem, recv_sem, src_thru, land_thru, token):
        _sib_copy(src_ref, land_ref, send_sem, recv_sem).start()
        token[...] = jnp.zeros_like(token)

    return pl.pallas_call(
        body, name=name,
        out_shape=(pltpu.SemaphoreType.DMA(()), pltpu.SemaphoreType.DMA(()),
                   pltpu.HBM(src.shape, src.dtype), pltpu.HBM(land.shape, land.dtype),
                   jax.ShapeDtypeStruct((8, LANE), F32)),
        in_specs=(_HBM_SPEC, _HBM_SPEC),
        out_specs=(_SEM_SPEC, _SEM_SPEC, _HBM_SPEC, _HBM_SPEC, _VMEM_SPEC),
        input_output_aliases={0: 2, 1: 3},
        compiler_params=pltpu.CompilerParams(has_side_effects=_EFFECT),
    )(pltpu.with_memory_space_constraint(src, pltpu.HBM), pltpu.with_memory_space_constraint(land, pltpu.HBM))


def _sib_wait(started, after, name):
    send_sem, recv_sem, src, land, _ = started

    def body(src_ref, land_ref, send_sem, recv_sem, after_ref, src_out, land_out):
        cp = _sib_copy(src_ref, land_ref, send_sem, recv_sem)
        cp.wait_send()
        cp.wait_recv()

    return pl.pallas_call(
        body, name=name,
        out_shape=(pltpu.HBM(src.shape, src.dtype), pltpu.HBM(land.shape, land.dtype)),
        in_specs=(_HBM_SPEC, _HBM_SPEC, _SEM_SPEC, _SEM_SPEC, _ANY_SPEC),
        out_specs=(_HBM_SPEC, _HBM_SPEC),
        input_output_aliases={0: 0, 1: 1},
        compiler_params=pltpu.CompilerParams(has_side_effects=_EFFECT),
    )(src, land, send_sem, recv_sem, after)


def _sum_slabs(gp, recv, chip, name):
    _, R, C = gp.shape
    tr = _tile(R, PACK_ROWS, 16)

    def body(chip_ref, own_ref, r0_ref, r1_ref, r2_ref, o_ref):
        acc = own_ref[...].astype(F32) + r0_ref[...].astype(F32)
        o_ref[...] = (acc + r1_ref[...].astype(F32)) + r2_ref[...].astype(F32)

    def got(n):
        return pl.BlockSpec((None, tr, C), lambda i, chip_ref: (n, i, 0))

    return pl.pallas_call(
        body, name=name,
        grid_spec=pltpu.PrefetchScalarGridSpec(
            num_scalar_prefetch=1, grid=(R // tr,),
            in_specs=[pl.BlockSpec((None, tr, C), lambda i, chip_ref: (chip_ref[0], i, 0)), got(0), got(1), got(2)],
            out_specs=pl.BlockSpec((tr, C), lambda i, chip_ref: (i, 0))),
        out_shape=jax.ShapeDtypeStruct((R, C), F32),
        compiler_params=_params(("parallel",)),
    )(jnp.reshape(chip, (1,)).astype(jnp.int32), gp, recv, recv, recv)


def _all_reduce_vec(vec, name):
    VR, W = vec.shape

    def body(vec_ref, vall_ref, vout_ref, vsend_sems, vrecv_sems):
        x, y, c = _place()
        vall_ref[4 * x + 2 * y + c] = vec_ref[...]
        sends = []
        peers = []
        for r in range(1, N_DEV):
            dx, dy, dc = (r >> 2) & 1, (r >> 1) & 1, r & 1
            peer = (x ^ dx, y ^ dy, c ^ dc)
            peers.append(peer)
            cp = pltpu.make_async_remote_copy(
                src_ref=vec_ref, dst_ref=vall_ref.at[4 * x + 2 * y + c], send_sem=vsend_sems.at[r - 1],
                recv_sem=vrecv_sems.at[r - 1], device_id=peer, device_id_type=MESH)
            cp.start()
            sends.append(cp)
        for r, peer in enumerate(peers):
            pltpu.make_async_remote_copy(
                src_ref=vec_ref, dst_ref=vall_ref.at[4 * peer[0] + 2 * peer[1] + peer[2]],
                send_sem=vsend_sems.at[r], recv_sem=vrecv_sems.at[r],
                device_id=peer, device_id_type=MESH).wait_recv()
        total = vall_ref[0]
        for d in range(1, N_DEV):
            total = total + vall_ref[d]
        vout_ref[...] = total
        for cp in sends:
            cp.wait_send()

    outs = pl.pallas_call(
        body, name=name,
        in_specs=[_VMEM_SPEC], out_specs=[_VMEM_SPEC, _VMEM_SPEC],
        out_shape=[jax.ShapeDtypeStruct((N_DEV, VR, W), F32), jax.ShapeDtypeStruct((VR, W), F32)],
        scratch_shapes=[pltpu.SemaphoreType.DMA((N_DEV - 1,)), pltpu.SemaphoreType.DMA((N_DEV - 1,))],
    )(vec)
    return outs[1]


class _Pack:
    def __init__(self, group, C):
        self.group, self.C = group, C
        self.rows, self.offs, off = {}, {}, 0
        for nm, (K, N), _ in group:
            assert N <= C, nm
            self.rows[nm] = K if 2 * N > C else -(-(K * N) // C)
            self.offs[nm] = off
            off += -(-self.rows[nm] // 16) * 16
        self.used = off
        self.R = -(-off // PACK_ROWS) * PACK_ROWS

    def _rows_of(self, a):
        K, N = a.shape
        if 2 * N > self.C:
            a = jnp.pad(a, ((0, 0), (0, self.C - N)))
        else:
            a = jnp.pad(a.reshape(-1), (0, -(K * N) % self.C)).reshape(-1, self.C)
        return jnp.pad(a, ((0, -a.shape[0] % 16), (0, 0)))

    def pack(self, shards):
        parts = [self._rows_of(shards[nm].astype(BF16)) for nm, _, _ in self.group]
        return jnp.concatenate(parts + [jnp.zeros((self.R - self.used, self.C), BF16)], axis=0)

    def _shard_of(self, rows, shape):
        K, N = shape
        return rows[:, :N] if 2 * N > self.C else rows.reshape(-1)[:K * N].reshape(K, N)

    def part(self, flat, nm, shape):
        return self._shard_of(flat[self.offs[nm]:self.offs[nm] + self.rows[nm]], shape)

    def slab_rows(self, nm, g):
        (K, N), axis = next((shape, axis) for n, shape, axis in self.group if n == nm)
        cuts = [g[:, k * N:(k + 1) * N] if axis == 1 else g[k * K:(k + 1) * K, :] for k in range(N_CHIPS)]
        return jnp.stack([self._rows_of(c.astype(BF16)) for c in cuts])

    def slabs(self, grads):
        parts = [self.slab_rows(nm, grads[nm]) for nm, _, _ in self.group]
        return jnp.concatenate(parts + [jnp.zeros((N_CHIPS, self.R - self.used, self.C), BF16)], axis=1)

    def full(self, gathered, names=None):
        res = {}
        for nm, (K, N), axis in self.group:
            if names is None or nm in names:
                rows = gathered[:, self.offs[nm]:self.offs[nm] + self.rows[nm]]
                res[nm] = jnp.concatenate([self._shard_of(rows[k], (K, N)) for k in range(N_CHIPS)], axis=axis)
        return res


def _rope_tables(S):
    pos = jnp.arange(S, dtype=F32)
    inv = 1.0 / (ROPE_THETA ** (jnp.arange(0, MLA_ROPE, 2, dtype=F32) / MLA_ROPE))
    ang = pos[:, None] * inv[None, :]
    cos, sin = jnp.cos(ang), jnp.sin(ang)
    half = MLA_ROPE // 2
    z = jnp.zeros((S, half), F32)
    one = jnp.ones((S, LANE - MLA_ROPE), F32)
    zero = jnp.zeros((S, LANE - MLA_ROPE), F32)
    kc = jnp.concatenate([cos, cos, one], axis=1)
    ksa = jnp.concatenate([-sin, z, zero], axis=1)
    ksb = jnp.concatenate([z, sin, zero], axis=1)
    qc = jnp.concatenate([jnp.ones((S, MLA_NOPE), F32), kc], axis=1)
    qsa = jnp.concatenate([jnp.zeros((S, MLA_NOPE), F32), ksa], axis=1)
    qsb = jnp.concatenate([jnp.zeros((S, MLA_NOPE), F32), ksb], axis=1)
    return (kc, ksa, ksb), (qc, qsa, qsb)


def _pad_cols(a, width):
    return jnp.pad(a, ((0, 0), (0, width - a.shape[1])))


def kernel(x, attn_norm, w_in, fox_f_bias, q_norm, w_uq, kv_norm, w_ukv, w_mla_branch, w_fox_branch, w_out, mlp_norm, w_up, w_down, final_norm, loss_target, m_attn_norm, m_w_in, m_fox_f_bias, m_q_norm, m_w_uq, m_kv_norm, m_w_ukv, m_w_mla_branch, m_w_fox_branch, m_w_out, m_mlp_norm, m_w_up, m_w_down, m_final_norm, v_attn_norm, v_w_in, v_fox_f_bias, v_q_norm, v_w_uq, v_kv_norm, v_w_ukv, v_w_mla_branch, v_w_fox_branch, v_w_out, v_mlp_norm, v_w_up, v_w_down, v_final_norm):
    _, S, D = x.shape
    H, HF = MLA_HEADS, FOX_HEADS
    QL, KVL = MLA_Q_LORA, MLA_KV_LORA
    assert H == HF and H <= 8
    xs = x[0]
    target = loss_target[0]
    C = D
    chip = 2 * lax.axis_index("x") + lax.axis_index("y")

    def flip(a):
        return jnp.transpose(a, (0, 2, 1))

    w_in, m_w_in, v_w_in = flip(w_in), flip(m_w_in), flip(v_w_in)
    weights = {"attn_norm": attn_norm, "w_in": w_in, "fox_f_bias": fox_f_bias, "q_norm": q_norm, "w_uq": w_uq,
               "kv_norm": kv_norm, "w_ukv": w_ukv, "w_mla_branch": w_mla_branch, "w_fox_branch": w_fox_branch,
               "w_out": w_out, "mlp_norm": mlp_norm, "w_up": w_up, "w_down": w_down, "final_norm": final_norm}
    moments = {"attn_norm": (m_attn_norm, v_attn_norm), "w_in": (m_w_in, v_w_in), "fox_f_bias": (m_fox_f_bias, v_fox_f_bias),
               "q_norm": (m_q_norm, v_q_norm), "w_uq": (m_w_uq, v_w_uq), "kv_norm": (m_kv_norm, v_kv_norm),
               "w_ukv": (m_w_ukv, v_w_ukv), "w_mla_branch": (m_w_mla_branch, v_w_mla_branch),
               "w_fox_branch": (m_w_fox_branch, v_w_fox_branch), "w_out": (m_w_out, v_w_out),
               "mlp_norm": (m_mlp_norm, v_mlp_norm), "w_up": (m_w_up, v_w_up), "w_down": (m_w_down, v_w_down),
               "final_norm": (m_final_norm, v_final_norm)}

    def group(names_axes):
        return [(nm, weights[nm].shape[1:], axis) for nm, axis in names_axes]

    pack_a = _Pack(group([("w_in", 0), ("w_uq", 1), ("w_ukv", 1)]), C)
    pack_b = _Pack(group([("w_down", 0), ("w_up", 1), ("w_out", 0), ("w_mla_branch", 1), ("w_fox_branch", 1)]), C)
    RA, RB = pack_a.R, pack_b.R
    wp_a = pack_a.pack({nm: weights[nm][0] for nm, _, _ in pack_a.group})
    wp_b = pack_b.pack({nm: weights[nm][0] for nm, _, _ in pack_b.group})
    n_in = w_in.shape[1]
    rows_in = -(-n_in // 16) * 16
    assert pack_a.offs["w_in"] == 0 and all((k * n_in) % 16 + n_in <= rows_in for k in range(N_CHIPS))
    shifted = lax.dynamic_update_slice(jnp.zeros((rows_in, C), BF16), wp_a[:n_in], ((chip * n_in) % 16, 0))
    wp_a = jnp.concatenate([shifted, wp_a[rows_in:]], axis=0)
    ag_a = _xchg_start(wp_a, lax.empty((N_CHIPS, RA, C), BF16), "half", jnp.zeros((8, LANE), F32), "all_gather_start_a")
    xn = _norm_fwd(xs, attn_norm, "attn_norm_fwd", order=ag_a[3])
    own_a, land_a = _xchg_wait(ag_a, "half", (xn, wp_b), "all_gather_wait_a")
    land_a = _forward_halves(land_a, "all_gather_forward_a")
    gathered_a = lax.dynamic_update_slice(land_a, own_a[None], (chip, 0, 0))
    ag_b = _xchg_start(wp_b, lax.empty((N_CHIPS, RB, C), BF16), True, gathered_a, "all_gather_start_b")
    full = pack_a.full(gathered_a, ("w_uq", "w_ukv"))
    tile0 = [(k * n_in) // 16 * 16 for k in range(N_CHIPS)]
    total = tile0[-1] + rows_in
    full["w_in"] = sum(jnp.pad(gathered_a[k, :rows_in], ((tile0[k], total - tile0[k] - rows_in), (0, 0)))
                       for k in range(N_CHIPS))

    o_ckv = QL
    o_kr = o_ckv + KVL
    o_fq = o_kr + MLA_ROPE
    o_ff = o_fq + 3 * HF * FOX_HEAD_DIM
    o_g = o_ff + HF
    wi = full["w_in"]
    assert N_CHIPS * n_in == o_g + 2 * D and wi.shape[0] >= o_g + 2 * D
    WS = QL + KVL + 2 * LANE
    NQKV = 3 * HF * FOX_HEAD_DIM

    def pad_rows(a, rows):
        return jnp.pad(a, ((0, rows - a.shape[0]), (0, 0)))

    w_small = jnp.concatenate([wi[:o_kr], pad_rows(wi[o_kr:o_fq], LANE), pad_rows(wi[o_ff:o_g], LANE)], axis=0)
    w_qkv = wi[o_fq:o_ff]
    w_g = wi[o_g:o_g + 2 * D]
    w_pack = jnp.concatenate([w_small, w_qkv, w_g], axis=0)
    dqk = MLA_NOPE + MLA_ROPE
    w_uq_p = jnp.pad(full["w_uq"].reshape(QL, H, dqk), ((0, 0), (0, 0), (0, QPAD - dqk))).reshape(QL, H * QPAD)
    ukv = full["w_ukv"].reshape(KVL, H, MLA_NOPE + MLA_V)
    w_ukv_p = jnp.concatenate([ukv[:, :, :MLA_NOPE].reshape(KVL, H * MLA_NOPE),
                               ukv[:, :, MLA_NOPE:].reshape(KVL, H * MLA_V)], axis=1)

    (kc, ksa, ksb), (qc, qsa, qsb) = _rope_tables(S)
    bias_pad = _pad_cols(fox_f_bias, LANE)

    small = _matmul(xn, w_small, "nt", [F32], "proj_small")
    n_fq = HF * FOX_HEAD_DIM
    q_scale = jnp.concatenate([jnp.full((1, n_fq), LOG2E / math.sqrt(FOX_HEAD_DIM), F32),
                               jnp.ones((1, NQKV - n_fq), F32)], axis=1)
    qkv = _matmul(xn, w_qkv, "nt", [BF16], "proj_qkv", col_extras=(q_scale,), epilogue=lambda acc, cs: (acc * cs,))
    gpre = _matmul(xn, w_g, "nt", [F32], "proj_gates")
    cqn, ckvn, kr, cum = _prep_fwd(small, q_norm, kv_norm, bias_pad, kc, ksa, ksb, HF, "prep_fwd")
    c2_mla = LOG2E / math.sqrt(dqk)
    q_rot = _matmul(cqn, w_uq_p, "nn", [BF16], "mla_q_up", tn=QPAD, row_extras=(qc * c2_mla, qsa * c2_mla, qsb * c2_mla),
                    epilogue=lambda acc, c, sa, sb: (_rope(acc, c, sa, sb, 1),))
    kv2 = _matmul(ckvn, w_ukv_p, "nn", [BF16], "mla_kv_up")

    mla = _AttT(S, H, (q_rot, QPAD, 0, True), [(kv2, MLA_NOPE, 0, True), (kr, LANE, 0, False)],
                (kv2, MLA_V, H, True), 1.0 / math.sqrt(dqk), True)
    o_mla, lse_mla = _att_fwd_t(mla, "mla_att_fwd")

    cum_t = jnp.transpose(cum[:, :HF]) * LOG2E
    cum_rep = jnp.broadcast_to(cum_t[:, :, None], (HF, S, min(QSUB, _tile(S, ATT_T))))
    fox = _AttT(S, HF, (qkv, FOX_HEAD_DIM, 0, True), [(qkv, FOX_HEAD_DIM, HF, True)],
                (qkv, FOX_HEAD_DIM, 2 * HF, True), 1.0 / math.sqrt(FOX_HEAD_DIM), False, cum_rep)
    o_fox, ox_fox, lse_fox = _att_fwd_t(fox, "fox_att_fwd", exact=True)

    own_b, land_b = _xchg_wait(ag_b, True, lse_fox, "all_gather_wait_b")
    gathered_b = lax.dynamic_update_slice(land_b, own_b[None], (chip, 0, 0))
    full.update(pack_b.full(gathered_b, ("w_mla_branch", "w_fox_branch", "w_out")))
    w_mb, w_fb, w_o = (full[n] for n in ("w_mla_branch", "w_fox_branch", "w_out"))

    def b_of(nm, mode, tn, tk):
        (K, N), axis = next((shape, axis) for n, shape, axis in pack_b.group if n == nm)
        off = pack_b.offs[nm]
        shape = (N_CHIPS * K, N) if axis == 0 else (K, N_CHIPS * N)
        t_r, t_c = (tk, tn) if mode == "nn" else (tn, tk)
        t_r, t_c = _tile(shape[0], t_r), _tile(shape[1], t_c)
        if not (N == C and K % t_r == 0 and N % t_c == 0 and off % t_r == 0):
            return pack_b.full(gathered_b, (nm,))[nm], None
        base = off // t_r
        if axis == 0:
            per = K // t_r
            place = lambda rb, cb: (rb // per, base + rb % per, cb)
        else:
            per = N // t_c
            place = lambda rb, cb: (cb // per, base + rb, cb % per)
        return gathered_b, (shape, (lambda j, k: place(k, j)) if mode == "nn" else (lambda j, k: place(j, k)))

    y_mla = _matmul(o_mla, w_mb, "nn", [F32], "mla_branch")

    def gate_merge(acc, ga, gb, ya):
        return acc, _sigmoid(ga) * ya + _sigmoid(gb) * acc

    y_fox, merged = _matmul(o_fox, w_fb, "nn", [F32, BF16], "fox_branch_gates", tn=512,
                            extras=((gpre, 0), (gpre, 1), y_mla), epilogue=gate_merge)
    h1 = _matmul(merged, w_o, "nn", [F32], "out_proj", extras=(xs,), epilogue=lambda acc, r: (acc + r,))
    hn = _norm_fwd(h1, mlp_norm, "mlp_norm_fwd")

    def relu2(acc):
        a = jnp.maximum(acc, 0.0)
        return a * a, a

    w_u, w_u_in = b_of("w_up", "nn", 1024, 2048)
    u, a_pos = _matmul(hn, w_u, "nn", [BF16, BF16], "mlp_up", epilogue=relu2, b_in=w_u_in)
    w_d, w_d_in = b_of("w_down", "nn", 1024, 2048)
    h2 = _matmul(u, w_d, "nn", [F32], "mlp_down", tn=1024, extras=(h1,), epilogue=lambda acc, r: (acc + r,),
                 b_in=w_d_in)
    dh2, dh2_b, g_final, loss_part = _final(h2, final_norm.reshape(1, D), target, "final_norm_loss")

    gp_b = lax.empty((N_CHIPS, RB, C), BF16)
    by_glue = {}

    def grad_b(nm, a, b, name):
        nonlocal gp_b
        (K, N), axis = next((shape, axis) for n, shape, axis in pack_b.group if n == nm)
        off = pack_b.offs[nm]
        tm = min(1024, K) if axis == 0 else min(1024, a.shape[1])
        tn = min(1024, N) if axis == 1 else min(1024, b.shape[1])
        if not (N == C and tm % LANE == 0 and tn % LANE == 0 and K % tm == 0 and N % tn == 0 and off % tm == 0):
            by_glue[nm] = _mm_tn(a, b, name)
            return
        base = off // tm
        if axis == 0:
            per = K // tm
            place = lambda i, j: (i // per, base + i % per, j)
        else:
            per = N // tn
            place = lambda i, j: (j // per, base + i, j % per)
        gp_b = _mm_tn(a, b, name, tm=tm, tn=tn, into=(gp_b, place))

    w_d, w_d_in = b_of("w_down", "nt", 1024, 2048)
    da = _matmul(dh2_b, w_d, "nt", [BF16], "mlp_down_dx", extras=(a_pos,),
                 epilogue=lambda acc, a: (acc * (2.0 * a.astype(F32)),), b_in=w_d_in)
    grad_b("w_down", u, dh2_b, "mlp_down_dw")
    w_u, w_u_in = b_of("w_up", "nt", 1024, 2048)
    dhn = _matmul(da, w_u, "nt", [F32], "mlp_up_dx", tn=1024, b_in=w_u_in)
    grad_b("w_up", hn, da, "mlp_up_dw")
    dh1, dh1_b, g_mlp_norm = _norm_bwd(h1, dhn, mlp_norm, dh2, "mlp_norm_bwd")

    def gate_bwd(acc, ga, gb, ya, yb):
        ga, gb = _sigmoid(ga), _sigmoid(gb)
        return acc * ga, acc * gb, acc * ya * (ga * (1.0 - ga)), acc * yb * (gb * (1.0 - gb))

    dy_mla, dy_fox, dg_mla, dg_fox = _matmul(dh1_b, w_o, "nt", [BF16] * 4, "out_proj_dx_gates", tn=512,
                                             extras=((gpre, 0), (gpre, 1), y_mla, y_fox), epilogue=gate_bwd)
    grad_b("w_out", merged, dh1_b, "out_proj_dw")
    do_mla = _matmul(dy_mla, w_mb, "nt", [BF16], "mla_branch_dx")
    grad_b("w_mla_branch", o_mla, dy_mla, "mla_branch_dw")
    do_fox = _matmul(dy_fox, w_fb, "nt", [BF16], "fox_branch_dx")
    grad_b("w_fox_branch", o_fox, dy_fox, "fox_branch_dw")
    for nm, g in by_glue.items():
        gp_b = lax.dynamic_update_slice(gp_b, pack_b.slab_rows(nm, g), (0, pack_b.offs[nm], 0))
    if RB > pack_b.used:
        gp_b = lax.dynamic_update_slice(gp_b, jnp.zeros((N_CHIPS, RB - pack_b.used, C), BF16), (0, pack_b.used, 0))

    rs_b = _xchg_start(gp_b, lax.empty((3, RB, C), BF16), False, do_fox, "grad_scatter_start_b")

    delta_mla = _att_delta_t(do_mla, o_mla, H, "mla_att_delta", order=rs_b[3])
    dq_rot, dk_nope, dkr_heads, dv_mla = _att_bwd_t(mla, do_mla, lse_mla, delta_mla, BF16, [BF16, F32],
                                                    "mla_att_bwd", dq_rope=(qc, qsa, qsb))
    delta_fox = _att_delta_t(do_fox, ox_fox, HF, "fox_att_delta")
    dfq, dfk, dfv, dcum = _att_bwd_t(fox, do_fox, lse_fox, delta_fox, BF16, [BF16], "fox_att_bwd")

    gp_b_sent, recv_b = _xchg_wait(rs_b, False, dfq, "grad_scatter_wait_b")
    swap_b = _sib_start(_sum_slabs(gp_b_sent, recv_b, chip, "grad_sum_b"), "grad_swap_start_b")

    dcqn = _matmul(dq_rot, w_uq_p, "nt", [F32], "mla_q_up_dx", order=swap_b[4])
    g_w_uq_p = _mm_tn(cqn, dq_rot, "mla_q_up_dw")
    dkv2 = jnp.concatenate([dk_nope, dv_mla], axis=1)
    dckvn = _matmul(dkv2, w_ukv_p, "nt", [F32], "mla_kv_up_dx")
    g_w_ukv_p = _mm_tn(ckvn, dkv2, "mla_kv_up_dw")

    dcum_rows = jnp.pad(dcum[:, :, 0], ((0, 8 - HF), (0, 0)))
    dlogf_rows = _suffix_sum_rows(dcum_rows, "fox_forget_suffix_sum")
    dlogf = _pad_cols(jnp.transpose(dlogf_rows[:HF]), LANE)
    d_small, g_q_norm, g_kv_norm, g_bias = _prep_bwd(
        small, dcqn, dckvn, dkr_heads, dlogf, q_norm, kv_norm, bias_pad, kc, ksa, ksb, H, "prep_bwd")
    dproj = [d_small, dfq, dfk, dfv, dg_mla, dg_fox]
    gs, gfq, gfk, gfv, gg_mla, gg_fox = [
        _matmul(part, xn, "tn", [BF16], "proj_dw_" + tag, tm=1024, tn=1024, tk=2048)
        for part, tag in zip(dproj, ("small", "fq", "fk", "fv", "g_mla", "g_fox"))]

    g_w_in = jnp.concatenate([gs[:o_kr], gs[o_kr:o_kr + MLA_ROPE], gfq, gfk, gfv,
                              gs[o_kr + LANE:o_kr + LANE + HF], gg_mla, gg_fox], axis=0)
    g_w_uq = g_w_uq_p.reshape(QL, H, QPAD)[:, :, :dqk].reshape(QL, H * dqk)
    g_w_ukv = jnp.concatenate([g_w_ukv_p[:, :H * MLA_NOPE].reshape(KVL, H, MLA_NOPE),
                               g_w_ukv_p[:, H * MLA_NOPE:].reshape(KVL, H, MLA_V)], axis=2).reshape(KVL, -1)

    gp_a = pack_a.slabs({"w_in": g_w_in, "w_uq": g_w_uq, "w_ukv": g_w_ukv})
    rs_a = _xchg_start(gp_a, lax.empty((3, RA, C), BF16), False, gg_fox, "grad_scatter_start_a")
    dxn = _matmul_parts(dproj, w_pack, "nn", F32, "proj_dx", order=rs_a[3])
    grad_x, _, g_attn_norm = _norm_bwd(xs, dxn, attn_norm, dh1, "attn_norm_bwd")
    gp_a_sent, recv_a = _xchg_wait(rs_a, False, grad_x, "grad_scatter_wait_a")
    swap_a = _sib_start(_sum_slabs(gp_a_sent, recv_a, chip, "grad_sum_a"), "grad_swap_start_a")
    vec_w = max(D, LANE)
    vec_rows = [g_attn_norm, g_mlp_norm, g_final, g_q_norm, g_kv_norm, g_bias, loss_part]
    vec = jnp.concatenate([_pad_cols(v, vec_w) for v in vec_rows] + [jnp.zeros((1, vec_w), F32)], axis=0)
    vsum = _all_reduce_vec(vec, "all_reduce_vectors")
    part_b, sib_b = _sib_wait(swap_b, vsum, "grad_swap_wait_b")

    grads, deltas, new_m, new_v = {}, {}, {}, {}

    def update(pack, mine, theirs):
        for nm, shape, _ in pack.group:
            K, N = shape
            if N == pack.C and K % 8 == 0 and pack.offs[nm] % _tile(K, 256, 8) == 0:
                g, d, nm_, nv_ = _adamw(weights[nm], mine, theirs, moments[nm][0], moments[nm][1], "adamw_" + nm,
                                        g_row=pack.offs[nm])
            else:
                g, d, nm_, nv_ = _adamw(weights[nm], pack.part(mine, nm, shape), pack.part(theirs, nm, shape),
                                        moments[nm][0], moments[nm][1], "adamw_" + nm)
            grads[nm], deltas[nm], new_m[nm], new_v[nm] = g, d, nm_, nv_
        return g

    last_b = update(pack_b, part_b, sib_b)
    part_a, sib_a = _sib_wait(swap_a, last_b, "grad_swap_wait_a")
    update(pack_a, part_a, sib_a)

    vec_names = ["attn_norm", "mlp_norm", "final_norm", "q_norm", "kv_norm", "fox_f_bias"]

    def vec_pack(arrs):
        return jnp.concatenate([_pad_cols(a.reshape(1, -1), vec_w) for a in arrs]
                               + [jnp.zeros((2, vec_w), F32)], axis=0)[None]

    vg, vd, vm, vv = _adamw(vec_pack([weights[n] for n in vec_names]), vsum, jnp.zeros_like(vsum),
                            vec_pack([moments[n][0] for n in vec_names]), vec_pack([moments[n][1] for n in vec_names]),
                            "adamw_vectors")
    for r, nm in enumerate(vec_names):
        shp = weights[nm].shape
        n = weights[nm].size
        grads[nm] = vsum[r, :n].reshape(shp)
        deltas[nm], new_m[nm], new_v[nm] = (vd[0, r, :n].reshape(shp), vm[0, r, :n].reshape(shp),
                                            vv[0, r, :n].reshape(shp))
    loss = vsum[6, 0]

    for res in (grads, deltas, new_m, new_v):
        res["w_in"] = flip(res["w_in"])
    order = ["attn_norm", "w_in", "fox_f_bias", "q_norm", "w_uq", "kv_norm", "w_ukv", "w_mla_branch", "w_fox_branch",
             "w_out", "mlp_norm", "w_up", "w_down", "final_norm"]
    return (loss, grad_x[None], *[grads[n] for n in order], *[deltas[n] for n in order],
            *[new_m[n] for n in order], *[new_v[n] for n in order])
```

```python
import math

import jax
import jax.numpy as jnp
from jax import lax
from jax.experimental import pallas as pl
from jax.experimental.pallas import tpu as pltpu

CHUNK = 64
MLA_HEADS = 8
MLA_Q_LORA = 512
MLA_KV_LORA = 256
MLA_NOPE = 128
MLA_ROPE = 64
MLA_V = 128
ROPE_THETA = 10000.0
FOX_HEADS = 8
FOX_HEAD_DIM = 128
EPS = 1e-6

ADAM_LR = 0.001
ADAM_B1 = 0.9
ADAM_B2 = 0.999
ADAM_EPS = 1e-08
ADAM_WD = 0.01
ADAM_STEP = 10

LANE = 128
QPAD = 2 * LANE
N_CHIPS = 4
N_DEV = 8
VMEM_LIMIT = 48 * 1024 * 1024
ATT_T = 2048
QSUB = 256
ROW_T = 256
PACK_ROWS = 256
LOG2E = 1.4426950408889634

BF16 = jnp.bfloat16
F32 = jnp.float32
MESH = pl.DeviceIdType.MESH

_NT = (((1,), (1,)), ((), ()))
_TN = (((0,), (0,)), ((), ()))
_NN = (((1,), (0,)), ((), ()))


def _tile(dim, pref, align=LANE):
    if dim <= pref:
        return dim
    t = (pref // align) * align
    while t >= align:
        if dim % t == 0:
            return t
        t -= align
    return dim


def _params(sem=None):
    return pltpu.CompilerParams(dimension_semantics=sem, vmem_limit_bytes=VMEM_LIMIT)


_ANY_SPEC = pl.BlockSpec(memory_space=pl.ANY)


def _matmul(a, b, mode, out_dtypes, name, *, tm=1024, tn=1024, tk=2048, extras=(), row_extras=(), col_extras=(),
            epilogue=None, order=None, into=None, b_in=None):
    b_shape = b.shape if b_in is None else b_in[0]
    if mode == "nn":
        (M, K), (K2, N) = a.shape, b_shape
    elif mode == "nt":
        (M, K), (N, K2) = a.shape, b_shape
    else:
        (K, M), (K2, N) = a.shape, b_shape
    assert K == K2, (name, a.shape, b_shape)
    tm, tn, tk = _tile(M, tm), _tile(N, tn), _tile(K, tk)
    nk = K // tk
    extras = [e if isinstance(e, tuple) else (e, 0) for e in extras]
    n_out = len(out_dtypes)
    n_ex = len(extras) + len(row_extras) + len(col_extras)
    n_ord = 0 if order is None else 1
    assert all(r.shape == (M, tn) for r in row_extras), name
    dims = {"nn": _NN, "nt": _NT, "tn": _TN}[mode]

    def body(*refs):
        a_ref, b_ref = refs[0], refs[1]
        ex_refs = refs[2:2 + n_ex]
        o_refs = refs[2 + n_ex + n_ord:2 + n_ex + n_ord + n_out]
        acc_ref = refs[2 + n_ex + n_ord + n_out]
        k = pl.program_id(2)
        part = lax.dot_general(a_ref[...], b_ref[...], dims, preferred_element_type=F32)

        @pl.when(k == 0)
        def _():
            acc_ref[...] = part

        @pl.when(k > 0)
        def _():
            acc_ref[...] += part

        @pl.when(k == nk - 1)
        def _():
            acc = acc_ref[...]
            if epilogue is None:
                outs = (acc,)
            else:
                outs = epilogue(acc, *[r[...] for r in ex_refs])
            for o_ref, o in zip(o_refs, outs):
                o_ref[...] = o.astype(o_ref.dtype)

    if mode == "nn":
        a_spec = pl.BlockSpec((tm, tk), lambda i, j, k: (i, k))
        b_spec = pl.BlockSpec((tk, tn), lambda i, j, k: (k, j))
    elif mode == "nt":
        a_spec = pl.BlockSpec((tm, tk), lambda i, j, k: (i, k))
        b_spec = pl.BlockSpec((tn, tk), lambda i, j, k: (j, k))
    else:
        a_spec = pl.BlockSpec((tk, tm), lambda i, j, k: (k, i))
        b_spec = pl.BlockSpec((tk, tn), lambda i, j, k: (k, j))
    if b_in is not None:
        b_block = (None, tn, tk) if mode == "nt" else (None, tk, tn)
        b_spec = pl.BlockSpec(b_block, lambda i, j, k: b_in[1](j, k))
    mn_spec = pl.BlockSpec((tm, tn), lambda i, j, k: (i, j))
    row_spec = pl.BlockSpec((tm, tn), lambda i, j, k: (i, 0))
    col_spec = pl.BlockSpec((1, tn), lambda i, j, k: (0, j))
    out_specs = [mn_spec] * n_out
    out_shape = [jax.ShapeDtypeStruct((M, N), dt) for dt in out_dtypes]
    aliases = {}
    if into is not None:
        buf, place = into
        assert n_out == 1 and n_ord == 1 and order is buf, name
        out_specs = [pl.BlockSpec((None, tm, tn), lambda i, j, k: place(i, j))]
        out_shape = [jax.ShapeDtypeStruct(buf.shape, buf.dtype)]
        aliases = {2 + n_ex: 0}
    outs = pl.pallas_call(
        body,
        name=name,
        grid=(M // tm, N // tn, nk),
        in_specs=([a_spec, b_spec]
                  + [pl.BlockSpec((tm, tn), lambda i, j, k, g=g: (i, j + g * (N // tn))) for _, g in extras]
                  + [row_spec] * len(row_extras) + [col_spec] * len(col_extras) + [_ANY_SPEC] * n_ord),
        out_specs=out_specs,
        out_shape=out_shape,
        scratch_shapes=[pltpu.VMEM((tm, tn), F32)],
        input_output_aliases=aliases,
        compiler_params=_params(("parallel", "parallel", "arbitrary")),
    )(a, b, *[e for e, _ in extras], *row_extras, *col_extras, *([] if order is None else [order]))
    return outs[0] if n_out == 1 else outs


def _matmul_parts(parts, b, mode, out_dtype, name, *, tm=1024, tn=1024, tk=1024, order=None):
    assert mode in ("nn", "tn")
    if mode == "nn":
        M, (K, N) = parts[0].shape[0], b.shape
        widths = [p.shape[1] for p in parts]
    else:
        K, N = b.shape
        widths = [p.shape[1] for p in parts]
        M = sum(widths)
    common = math.gcd(*widths)
    tm, tn, tk = _tile(M if mode == "nn" else common, tm), _tile(N, tn), _tile(common if mode == "nn" else K, tk)
    t_part = tk if mode == "nn" else tm
    assert sum(widths) == (K if mode == "nn" else M), name
    if any(w % t_part for w in widths):
        parts, widths = [jnp.concatenate(parts, axis=1)], [sum(widths)]
    lo =[sum(widths[:p]) // t_part for p in range(len(parts))]
    cnt = [w // t_part for w in widths]
    nk = K // tk
    n_parts = len(parts)
    n_ord = 0 if order is None else 1
    dims = _NN if mode == "nn" else _TN

    def body(*refs):
        a_refs = refs[0:n_parts]
        b_ref = refs[n_parts]
        o_ref, acc_ref = refs[n_parts + 1 + n_ord], refs[n_parts + 2 + n_ord]
        i, k = pl.program_id(0), pl.program_id(2)
        sel = k if mode == "nn" else i
        for p in range(n_parts):
            @pl.when((sel >= lo[p]) & (sel < lo[p] + cnt[p]))
            def _(p=p):
                part = lax.dot_general(a_refs[p][...], b_ref[...], dims, preferred_element_type=F32)

                @pl.when(k == 0)
                def _():
                    acc_ref[...] = part

                @pl.when(k > 0)
                def _():
                    acc_ref[...] += part

        @pl.when(k == nk - 1)
        def _():
            o_ref[...] = acc_ref[...].astype(o_ref.dtype)

    def a_spec(p):
        if mode == "nn":
            return pl.BlockSpec((tm, tk), lambda i, j, k: (i, jnp.clip(k - lo[p], 0, cnt[p] - 1)))
        return pl.BlockSpec((tk, tm), lambda i, j, k: (
            jnp.where((i >= lo[p]) & (i < lo[p] + cnt[p]), k, 0), jnp.clip(i - lo[p], 0, cnt[p] - 1)))

    return pl.pallas_call(
        body, name=name, grid=(M // tm, N // tn, nk),
        in_specs=[a_spec(p) for p in range(n_parts)] + [pl.BlockSpec((tk, tn), lambda i, j, k: (k, j))]
        + [_ANY_SPEC] * n_ord,
        out_specs=pl.BlockSpec((tm, tn), lambda i, j, k: (i, j)),
        out_shape=jax.ShapeDtypeStruct((M, N), out_dtype),
        scratch_shapes=[pltpu.VMEM((tm, tn), F32)],
        compiler_params=_params(("parallel", "parallel", "arbitrary")),
    )(*parts, b, *([] if order is None else [order]))


def _mm_tn(a, b, name, tm=1024, tn=1024, into=None):
    return _matmul(a, b, "tn", [F32], name, tm=tm, tn=tn, tk=2048, into=into,
                   order=None if into is None else into[0])


def _row_spec(ts, width, col=0):
    return pl.BlockSpec((ts, width), lambda i: (i, col))


def _full_spec(shape):
    return pl.BlockSpec(shape, lambda i: tuple(0 for _ in shape))


def _rms(x):
    return lax.rsqrt(jnp.mean(x * x, axis=-1, keepdims=True) + EPS)


def _rms_bwd(x, dy, g):
    r = _rms(x)
    xh = x * r
    gy = dy * g
    dx = r * (gy - xh * jnp.mean(xh * gy, axis=-1, keepdims=True))
    return dx, dy * xh


def _norm_fwd(x, g, name, order=None):
    S, D = x.shape
    ts = _tile(S, ROW_T, 8)

    def body(x_ref, g_ref, *rest):
        o_ref = rest[-1]
        xv = x_ref[...]
        o_ref[...] = ((xv * _rms(xv)) * g_ref[...]).astype(BF16)

    extra = [] if order is None else [order]
    return pl.pallas_call(
        body, name=name, grid=(S // ts,),
        in_specs=[_row_spec(ts, D), _full_spec((1, D))] + [_ANY_SPEC] * len(extra),
        out_specs=_row_spec(ts, D),
        out_shape=jax.ShapeDtypeStruct((S, D), BF16),
        compiler_params=_params(("parallel",)),
    )(x, g, *extra)


def _norm_bwd(x, dy, g, dres, name):
    S, D = x.shape
    ts = _tile(S, ROW_T, 8)

    def body(x_ref, dy_ref, g_ref, dres_ref, dx_ref, dxb_ref, dg_ref):
        dx, dg_rows = _rms_bwd(x_ref[...], dy_ref[...], g_ref[...])
        dx = dres_ref[...] + dx
        dx_ref[...] = dx
        dxb_ref[...] = dx.astype(BF16)

        @pl.when(pl.program_id(0) == 0)
        def _():
            dg_ref[...] = jnp.zeros_like(dg_ref)

        dg_ref[...] += jnp.sum(dg_rows, axis=0, keepdims=True)

    return pl.pallas_call(
        body, name=name, grid=(S // ts,),
        in_specs=[_row_spec(ts, D), _row_spec(ts, D), _full_spec((1, D)), _row_spec(ts, D)],
        out_specs=[_row_spec(ts, D), _row_spec(ts, D), _full_spec((1, D))],
        out_shape=[jax.ShapeDtypeStruct((S, D), F32), jax.ShapeDtypeStruct((S, D), BF16),
                   jax.ShapeDtypeStruct((1, D), F32)],
        compiler_params=_params(("arbitrary",)),
    )(x, dy, g, dres)


def _rope(x, c, sa, sb, sign):
    w = x.shape[-1]
    half = MLA_ROPE // 2
    fwd = pltpu.roll(x, w - half, 1)
    back = pltpu.roll(x, half, 1)
    if sign < 0:
        return x * c - fwd * sa - back * sb
    return x * c + fwd * sa + back * sb


def _split3(x):
    hi = x.astype(BF16)
    r1 = x - hi.astype(F32)
    mid = r1.astype(BF16)
    lo = (r1 - mid.astype(F32)).astype(BF16)
    return hi, mid, lo


def _prep_fwd(small, q_norm, kv_norm, bias_pad, kc, ksa, ksb, n_heads, name):
    S, W = small.shape
    QL, KVL = q_norm.shape[1], kv_norm.shape[1]
    assert W == QL + KVL + 2 * LANE
    ts = _tile(S, ROW_T, 8)
    tri = (lax.broadcasted_iota(jnp.int32, (ts, ts), 0) >= lax.broadcasted_iota(jnp.int32, (ts, ts), 1)).astype(BF16)

    def body(s_ref, qn_ref, kvn_ref, b_ref, kc_ref, ksa_ref, ksb_ref, tri_ref,
             cqn_ref, ckvn_ref, kr_ref, cum_ref, carry_ref):
        cq = s_ref[:, 0:QL]
        cqn_ref[...] = ((cq * _rms(cq)) * qn_ref[...]).astype(BF16)
        ckv = s_ref[:, QL:QL + KVL]
        ckvn_ref[...] = ((ckv * _rms(ckv)) * kvn_ref[...]).astype(BF16)
        kr = s_ref[:, QL + KVL:QL + KVL + LANE]
        kr_ref[...] = _rope(kr, kc_ref[...], ksa_ref[...], ksb_ref[...], 1).astype(BF16)
        z = s_ref[:, QL + KVL + LANE:W] + b_ref[...]
        logf = jnp.minimum(z, 0.0) - jnp.log1p(jnp.exp(-jnp.abs(z)))
        lane = lax.broadcasted_iota(jnp.int32, logf.shape, 1)
        logf = jnp.where(lane < n_heads, logf, 0.0)

        @pl.when(pl.program_id(0) == 0)
        def _():
            carry_ref[...] = jnp.zeros_like(carry_ref)

        t = tri_ref[...]
        cum = carry_ref[...]
        for part in _split3(logf):
            cum = cum + jnp.dot(t, part, preferred_element_type=F32)
        cum_ref[...] = cum
        carry_ref[...] = cum[ts - 1:ts, :]

    return pl.pallas_call(
        body, name=name, grid=(S // ts,),
        in_specs=[_row_spec(ts, W), _full_spec((1, QL)), _full_spec((1, KVL)), _full_spec((1, LANE)),
                  _row_spec(ts, LANE), _row_spec(ts, LANE), _row_spec(ts, LANE), _full_spec((ts, ts))],
        out_specs=[_row_spec(ts, QL), _row_spec(ts, KVL), _row_spec(ts, LANE), _row_spec(ts, LANE)],
        out_shape=[jax.ShapeDtypeStruct((S, QL), BF16), jax.ShapeDtypeStruct((S, KVL), BF16),
                   jax.ShapeDtypeStruct((S, LANE), BF16), jax.ShapeDtypeStruct((S, LANE), F32)],
        scratch_shapes=[pltpu.VMEM((1, LANE), F32)],
        compiler_params=_params(("arbitrary",)),
    )(small, q_norm, kv_norm, bias_pad, kc, ksa, ksb, tri)


def _prep_bwd(small, dcqn, dckvn, dkr_heads, dlogf, q_norm, kv_norm, bias_pad, kc, ksa, ksb, n_heads, name):
    S, W = small.shape
    QL, KVL = q_norm.shape[1], kv_norm.shape[1]
    ts = _tile(S, ROW_T, 8)

    def body(s_ref, dcq_ref, dckv_ref, dkr_ref, dlf_ref, qn_ref, kvn_ref, b_ref, kc_ref, ksa_ref, ksb_ref,
             ds_ref, gq_ref, gkv_ref, gb_ref):
        dcq, gq_rows = _rms_bwd(s_ref[:, 0:QL], dcq_ref[...], qn_ref[...])
        ds_ref[:, 0:QL] = dcq.astype(BF16)
        dckv, gkv_rows = _rms_bwd(s_ref[:, QL:QL + KVL], dckv_ref[...], kvn_ref[...])
        ds_ref[:, QL:QL + KVL] = dckv.astype(BF16)
        dkr = dkr_ref[:, 0:LANE]
        for h in range(1, n_heads):
            dkr = dkr + dkr_ref[:, h * LANE:(h + 1) * LANE]
        ds_ref[:, QL + KVL:QL + KVL + LANE] = _rope(dkr, kc_ref[...], ksa_ref[...], ksb_ref[...], -1).astype(BF16)
        z = s_ref[:, QL + KVL + LANE:W] + b_ref[...]
        dff = dlf_ref[...] * (1.0 / (1.0 + jnp.exp(z)))
        ds_ref[:, QL + KVL + LANE:W] = dff.astype(BF16)

        @pl.when(pl.program_id(0) == 0)
        def _():
            gq_ref[...] = jnp.zeros_like(gq_ref)
            gkv_ref[...] = jnp.zeros_like(gkv_ref)
            gb_ref[...] = jnp.zeros_like(gb_ref)

        gq_ref[...] += jnp.sum(gq_rows, axis=0, keepdims=True)
        gkv_ref[...] += jnp.sum(gkv_rows, axis=0, keepdims=True)
        gb_ref[...] += jnp.sum(dff, axis=0, keepdims=True)

    return pl.pallas_call(
        body, name=name, grid=(S // ts,),
        in_specs=[_row_spec(ts, W), _row_spec(ts, QL), _row_spec(ts, KVL), _row_spec(ts, n_heads * LANE),
                  _row_spec(ts, LANE), _full_spec((1, QL)), _full_spec((1, KVL)), _full_spec((1, LANE)),
                  _row_spec(ts, LANE), _row_spec(ts, LANE), _row_spec(ts, LANE)],
        out_specs=[_row_spec(ts, W), _full_spec((1, QL)), _full_spec((1, KVL)), _full_spec((1, LANE))],
        out_shape=[jax.ShapeDtypeStruct((S, W), BF16), jax.ShapeDtypeStruct((1, QL), F32),
                   jax.ShapeDtypeStruct((1, KVL), F32), jax.ShapeDtypeStruct((1, LANE), F32)],
        compiler_params=_params(("arbitrary",)),
    )(small, dcqn, dckvn, dkr_heads, dlogf, q_norm, kv_norm, bias_pad, kc, ksa, ksb)


def _sigmoid(z):
    return 1.0 / (1.0 + jnp.exp(-z))


def _final(h, g, target, name):
    S, D = h.shape
    ts = _tile(S, ROW_T, 8)

    def body(h_ref, g_ref, t_ref, dh_ref, dhb_ref, dg_ref, loss_ref):
        hv = h_ref[...]
        gv = g_ref[...]
        err = (hv * _rms(hv)) * gv - t_ref[...]
        dh, dg_rows = _rms_bwd(hv, err / D, gv)
        dh_ref[...] = dh
        dhb_ref[...] = dh.astype(BF16)

        @pl.when(pl.program_id(0) == 0)
        def _():
            dg_ref[...] = jnp.zeros_like(dg_ref)
            loss_ref[...] = jnp.zeros_like(loss_ref)

        dg_ref[...] += jnp.sum(dg_rows, axis=0, keepdims=True)
        row_loss = jnp.mean(err * err, axis=-1, keepdims=True)
        loss_ref[...] += 0.5 * jnp.sum(row_loss, axis=0, keepdims=True)

    return pl.pallas_call(
        body, name=name, grid=(S // ts,),
        in_specs=[_row_spec(ts, D), _full_spec((1, D)), _row_spec(ts, D)],
        out_specs=[_row_spec(ts, D), _row_spec(ts, D), _full_spec((1, D)), _full_spec((1, LANE))],
        out_shape=[jax.ShapeDtypeStruct((S, D), F32), jax.ShapeDtypeStruct((S, D), BF16),
                   jax.ShapeDtypeStruct((1, D), F32), jax.ShapeDtypeStruct((1, LANE), F32)],
        compiler_params=_params(("arbitrary",)),
    )(h, g, target)


def _suffix_sum_rows(x, name):
    R, S = x.shape
    tb = _tile(S, 512)
    nb = S // tb
    tri = (lax.broadcasted_iota(jnp.int32, (tb, tb), 0) >= lax.broadcasted_iota(jnp.int32, (tb, tb), 1)).astype(BF16)

    def body(x_ref, tri_ref, o_ref, carry_ref):
        @pl.when(pl.program_id(0) == 0)
        def _():
            carry_ref[...] = jnp.zeros_like(carry_ref)

        xv = x_ref[...]
        t = tri_ref[...]
        acc = jnp.broadcast_to(carry_ref[:, 0:1], xv.shape)
        for part in _split3(xv):
            acc = acc + jnp.dot(part, t, preferred_element_type=F32)
        o_ref[...] = acc
        carry_ref[...] = jnp.broadcast_to(acc[:, 0:1], carry_ref.shape)

    rev = pl.BlockSpec((R, tb), lambda i: (0, nb - 1 - i))
    return pl.pallas_call(
        body, name=name, grid=(nb,),
        in_specs=[rev, _full_spec((tb, tb))], out_specs=rev,
        out_shape=jax.ShapeDtypeStruct((R, S), F32),
        scratch_shapes=[pltpu.VMEM((R, LANE), F32)],
        compiler_params=_params(("arbitrary",)),
    )(x, tri)


def _pairs(nb, by_key):
    if by_key:
        pr = [(i, j) for j in range(nb) for i in range(j, nb)]
    else:
        pr = [(i, j) for i in range(nb) for j in range(i + 1)]
    return (jnp.asarray([p[0] for p in pr], jnp.int32), jnp.asarray([p[1] for p in pr], jnp.int32), len(pr))


class _AttT:
    def __init__(self, S, n_heads, q, ks, v, scale, chunk_causal, cum_rep=None):
        self.S, self.H, self.q, self.ks, self.v = S, n_heads, q, ks, v
        self.scale, self.chunk_causal, self.cum_rep = scale, chunk_causal, cum_rep
        self.T = _tile(S, ATT_T)
        self.qs = min(QSUB, self.T)
        self.nb = S // self.T
        self.dq, self.dv = q[1], v[1]
        self.has_bias = cum_rep is not None

    def q_spec(self, op):
        _, w, off, per_head = op
        return pl.BlockSpec((self.T, w), lambda h, p, it, jt: (it[p], off + (h if per_head else 0)))

    def k_spec(self, op):
        _, w, off, per_head = op
        return pl.BlockSpec((self.T, w), lambda h, p, it, jt: (jt[p], off + (h if per_head else 0)))

    def row_q(self):
        return pl.BlockSpec((None, 1, self.T), lambda h, p, it, jt: (h, 0, it[p]))

    def cum_k(self):
        return pl.BlockSpec((None, self.T, self.qs), lambda h, p, it, jt: (h, jt[p], 0))

    def sub_blocks(self, masked):
        return [(q0, min(self.T, q0 + self.qs) if masked else self.T) for q0 in range(0, self.T, self.qs)]

    def scores(self, k, q_sub, cum, q0, masked):
        s = lax.dot_general(k, q_sub, _NT, preferred_element_type=F32)
        if self.has_bias:
            s = s - cum
        mask = None
        if masked:
            r = lax.broadcasted_iota(jnp.int32, s.shape, 0)
            c = lax.broadcasted_iota(jnp.int32, s.shape, 1) + q0
            mask = (r // CHUNK <= c // CHUNK) if self.chunk_causal else (r <= c)
        return s, mask


def _join(k_refs):
    return k_refs[0][...] if len(k_refs) == 1 else jnp.concatenate([r[...] for r in k_refs], axis=-1)


def _att_fwd_t(att, name, exact=False):
    S, H, T, qs = att.S, att.H, att.T, att.qs
    it, jt, npairs = _pairs(att.nb, by_key=False)
    nk = len(att.ks)

    def body(it_ref, jt_ref, *refs):
        q_ref = refs[0]
        k_refs = refs[1:1 + nk]
        v_ref = refs[1 + nk]
        n = 2 + nk
        cum_ref = None
        if att.has_bias:
            cum_ref = refs[n]
            n += 1
        o_ref = refs[n]
        n += 1
        ox_ref = None
        if exact:
            ox_ref = refs[n]
            n += 1
        lse_ref, m_ref, l_ref, acc_ref = refs[n:n + 4]
        lo_ref = refs[n + 4] if exact else None
        p = pl.program_id(1)
        i, j = it_ref[p], jt_ref[p]

        @pl.when(j == 0)
        def _():
            m_ref[...] = jnp.full_like(m_ref, -jnp.inf)
            l_ref[...] = jnp.zeros_like(l_ref)
            acc_ref[...] = jnp.zeros_like(acc_ref)
            if exact:
                lo_ref[...] = jnp.zeros_like(lo_ref)

        def step(masked):
            k = _join(k_refs)
            v = v_ref[...]
            subs = att.sub_blocks(masked)

            def logits(idx):
                q0, nkeys = subs[idx]
                cum = cum_ref[0:nkeys, :] if att.has_bias else None
                return att.scores(k[0:nkeys], q_ref[q0:q0 + qs, :], cum, q0, masked)

            ahead = logits(0)
            for idx, (q0, nkeys) in enumerate(subs):
                qsl = slice(q0, q0 + qs)
                s, mask = ahead
                if idx + 1 < len(subs):
                    ahead = logits(idx + 1)
                if masked:
                    s = jnp.where(mask, s, -jnp.inf)
                m_prev = m_ref[:, qsl]
                m_new = jnp.maximum(m_prev, jnp.max(s, axis=0, keepdims=True))
                alpha = jnp.exp2(m_prev - m_new)
                pr = jnp.exp2(s - m_new)
                l_ref[:, qsl] = alpha * l_ref[:, qsl] + jnp.sum(pr, axis=0, keepdims=True)
                p_hi = pr.astype(BF16)
                acc_ref[:, qsl] = alpha * acc_ref[:, qsl] + lax.dot_general(
                    v[0:nkeys], p_hi, _TN, preferred_element_type=F32)
                if exact:
                    p_lo = (pr - p_hi.astype(F32)).astype(BF16)
                    lo_ref[:, qsl] = alpha * lo_ref[:, qsl] + lax.dot_general(
                        v[0:nkeys], p_lo, _TN, preferred_element_type=F32)
                m_ref[:, qsl] = m_new

        @pl.when(j < i)
        def _():
            step(False)

        @pl.when(j == i)
        def _():
            step(True)
            l = l_ref[...]
            inv = 1.0 / l
            o_ref[...] = jnp.transpose(acc_ref[...] * inv).astype(o_ref.dtype)
            if exact:
                ox_ref[...] = jnp.transpose((acc_ref[...] + lo_ref[...]) * inv)
            lse_ref[...] = m_ref[...] + jnp.log2(l)

    in_specs = [att.q_spec(att.q)] + [att.k_spec(k) for k in att.ks] + [att.k_spec(att.v)]
    args = [att.q[0]] + [k[0] for k in att.ks] + [att.v[0]]
    if att.has_bias:
        in_specs.append(att.cum_k())
        args.append(att.cum_rep)
    o_spec = pl.BlockSpec((T, att.dv), lambda h, p, it, jt: (it[p], h))
    out_specs = [o_spec]
    out_shape = [jax.ShapeDtypeStruct((S, H * att.dv), BF16)]
    scratch = [pltpu.VMEM((1, T), F32), pltpu.VMEM((1, T), F32), pltpu.VMEM((att.dv, T), F32)]
    if exact:
        out_specs.append(o_spec)
        out_shape.append(jax.ShapeDtypeStruct((S, H * att.dv), F32))
        scratch.append(pltpu.VMEM((att.dv, T), F32))
    out_specs.append(att.row_q())
    out_shape.append(jax.ShapeDtypeStruct((H, 1, S), F32))
    return pl.pallas_call(
        body, name=name,
        grid_spec=pltpu.PrefetchScalarGridSpec(
            num_scalar_prefetch=2, grid=(H, npairs), in_specs=in_specs, out_specs=out_specs,
            scratch_shapes=scratch),
        out_shape=out_shape,
        compiler_params=_params(("parallel", "arbitrary")),
    )(it, jt, *args)


def _att_delta_t(do, o, n_heads, name, order=None):
    S = do.shape[0]
    w = do.shape[1] // n_heads
    ts = _tile(S, ATT_T)
    ones = jnp.ones((8, w), BF16)
    extra = [] if order is None else [order]

    def body(do_ref, o_ref, ones_ref, *rest):
        d_ref = rest[-1]
        prod = do_ref[...].astype(F32) * o_ref[...].astype(F32)
        acc = jnp.zeros((8, ts), F32)
        for part in _split3(prod):
            acc = acc + lax.dot_general(ones_ref[...], part, _NT, preferred_element_type=F32)
        d_ref[...] = acc[0:1, :]

    blk = pl.BlockSpec((ts, w), lambda i, h: (i, h))
    return pl.pallas_call(
        body, name=name, grid=(S // ts, n_heads),
        in_specs=[blk, blk, pl.BlockSpec((8, w), lambda i, h: (0, 0))] + [_ANY_SPEC] * len(extra),
        out_specs=pl.BlockSpec((None, 1, ts), lambda i, h: (h, 0, i)),
        out_shape=jax.ShapeDtypeStruct((n_heads, 1, S), F32),
        compiler_params=_params(("parallel", "parallel")),
    )(do, o, ones, *extra)


def _att_bwd_t(att, do, lse, delta, dq_dtype, dk_dtypes, name, dq_rope=None):
    S, H, T, qs = att.S, att.H, att.T, att.qs
    it, jt, npairs = _pairs(att.nb, by_key=True)
    nk = len(att.ks)
    last = att.nb - 1
    widths = [k[1] for k in att.ks]

    def body(it_ref, jt_ref, *refs):
        q_ref = refs[0]
        k_refs = refs[1:1 + nk]
        v_ref, do_ref, lse_ref, dl_ref = refs[1 + nk:5 + nk]
        n = 5 + nk
        cum_ref = None
        if att.has_bias:
            cum_ref = refs[n]
            n += 1
        rope_refs = None
        if dq_rope is not None:
            rope_refs = refs[n:n + 3]
            n += 3
        dq_ref = refs[n]
        dk_refs = refs[n + 1:n + 1 + nk]
        dv_ref = refs[n + 1 + nk]
        n += nk + 2
        dc_ref = None
        if att.has_bias:
            dc_ref = refs[n]
            n += 1
        dq_acc, dk_acc, dv_acc = refs[n:n + 3]
        dc_acc = refs[n + 3] if att.has_bias else None
        p = pl.program_id(1)
        i, j = it_ref[p], jt_ref[p]

        @pl.when(p == 0)
        def _():
            dq_acc[...] = jnp.zeros_like(dq_acc)

        @pl.when(i == j)
        def _():
            dk_acc[...] = jnp.zeros_like(dk_acc)
            dv_acc[...] = jnp.zeros_like(dv_acc)
            if att.has_bias:
                dc_acc[...] = jnp.zeros_like(dc_acc)

        def step(masked):
            k = _join(k_refs)
            v = v_ref[...]
            subs = att.sub_blocks(masked)

            def logits(idx):
                q0, nkeys = subs[idx]
                cum = cum_ref[0:nkeys, :] if att.has_bias else None
                return att.scores(k[0:nkeys], q_ref[q0:q0 + qs, :], cum, q0, masked)

            ahead = logits(0)
            for idx, (q0, nkeys) in enumerate(subs):
                qsl = slice(q0, q0 + qs)
                ksl = slice(0, nkeys)
                q_sub = q_ref[qsl, :]
                do_sub = do_ref[qsl, :]
                s, mask = ahead
                if idx + 1 < len(subs):
                    ahead = logits(idx + 1)
                pr = jnp.exp2(s - lse_ref[:, qsl])
                if masked:
                    pr = jnp.where(mask, pr, 0.0)
                dp = lax.dot_general(v[ksl], do_sub, _NT, preferred_element_type=F32)
                ds = pr * (dp - dl_ref[:, qsl])
                ds_b = ds.astype(BF16)
                dv_acc[ksl, :] += jnp.dot(pr.astype(BF16), do_sub, preferred_element_type=F32)
                dk_acc[ksl, :] += jnp.dot(ds_b, q_sub, preferred_element_type=F32)
                dq_acc[i, :, qsl] += lax.dot_general(k[ksl], ds_b, _TN, preferred_element_type=F32)
                if att.has_bias:
                    part = ds[:, 0:LANE] if qs >= LANE else ds
                    for c0 in range(LANE, qs, LANE):
                        part = part + ds[:, c0:c0 + LANE]
                    dc_acc[ksl, :] += part

        @pl.when(i > j)
        def _():
            step(False)

        @pl.when(i == j)
        def _():
            step(True)
            dq = jnp.transpose(dq_acc[i] * att.scale)
            if dq_rope is not None:
                dq = _rope(dq, rope_refs[0][...], rope_refs[1][...], rope_refs[2][...], -1)
            dq_ref[...] = dq.astype(dq_ref.dtype)

        @pl.when(i == last)
        def _():
            dk = dk_acc[...] * (1.0 / LOG2E)
            off = 0
            for r, w in zip(dk_refs, widths):
                r[...] = dk[:, off:off + w].astype(r.dtype)
                off += w
            dv_ref[...] = dv_acc[...].astype(dv_ref.dtype)
            if att.has_bias:
                dc_ref[...] = -jnp.sum(dc_acc[...], axis=-1, keepdims=True)

    do_op = (do, att.dv, 0, True)
    in_specs = ([att.q_spec(att.q)] + [att.k_spec(k) for k in att.ks]
                + [att.k_spec(att.v), att.q_spec(do_op), att.row_q(), att.row_q()])
    args = [att.q[0]] + [k[0] for k in att.ks] + [att.v[0], do, lse, delta]
    if att.has_bias:
        in_specs.append(att.cum_k())
        args.append(att.cum_rep)
    if dq_rope is not None:
        in_specs += [pl.BlockSpec((T, att.dq), lambda h, p, it, jt: (jt[p], 0))] * 3
        args += list(dq_rope)
    out_specs = [pl.BlockSpec((T, att.dq), lambda h, p, it, jt: (jt[p], h))]
    out_shape = [jax.ShapeDtypeStruct((S, H * att.dq), dq_dtype)]
    out_specs += [pl.BlockSpec((T, w), lambda h, p, it, jt: (jt[p], h)) for w in widths]
    out_shape += [jax.ShapeDtypeStruct((S, H * w), dt) for w, dt in zip(widths, dk_dtypes)]
    out_specs.append(pl.BlockSpec((T, att.dv), lambda h, p, it, jt: (jt[p], h)))
    out_shape.append(jax.ShapeDtypeStruct((S, H * att.dv), BF16))
    scratch = [pltpu.VMEM((att.nb, att.dq, T), F32), pltpu.VMEM((T, att.dq), F32), pltpu.VMEM((T, att.dv), F32)]
    if att.has_bias:
        out_specs.append(pl.BlockSpec((None, T, 1), lambda h, p, it, jt: (h, jt[p], 0)))
        out_shape.append(jax.ShapeDtypeStruct((H, S, 1), F32))
        scratch.append(pltpu.VMEM((T, min(qs, LANE)), F32))
    return pl.pallas_call(
        body, name=name,
        grid_spec=pltpu.PrefetchScalarGridSpec(
            num_scalar_prefetch=2, grid=(H, npairs), in_specs=in_specs, out_specs=out_specs,
            scratch_shapes=scratch),
        out_shape=out_shape,
        compiler_params=_params(("parallel", "arbitrary")),
    )(it, jt, *args)


def _adamw(w, g1, g2, m, v, name, g_row=None):
    _, K, N = w.shape
    by_rows = K % 8 == 0
    tr = _tile(K, 256, 8) if by_rows else K
    if g_row is None:
        assert g1.shape == (K, N) and g2.shape == (K, N), name
        g_row = 0
    assert by_rows and g_row % tr == 0 or g_row == 0, name
    g_blk = g_row // tr
    tc = N if by_rows else _tile(N, LANE)
    c1 = 1.0 - ADAM_B1 ** ADAM_STEP
    c2 = 1.0 - ADAM_B2 ** ADAM_STEP

    def body(w_ref, g1_ref, g2_ref, m_ref, v_ref, g_ref, d_ref, nm_ref, nv_ref):
        gv = g1_ref[...] + g2_ref[...]
        nm = ADAM_B1 * m_ref[...] + (1.0 - ADAM_B1) * gv
        nv = ADAM_B2 * v_ref[...] + (1.0 - ADAM_B2) * (gv * gv)
        g_ref[...] = gv
        d_ref[...] = -ADAM_LR * ((nm / c1) / (jnp.sqrt(nv / c2) + ADAM_EPS) + ADAM_WD * w_ref[...])
        nm_ref[...] = nm
        nv_ref[...] = nv

    if by_rows:
        blk = pl.BlockSpec((None, tr, N), lambda i: (0, i, 0))
        gblk = pl.BlockSpec((tr, N), lambda i: (g_blk + i, 0))
    else:
        blk = pl.BlockSpec((None, K, tc), lambda i: (0, 0, i))
        gblk = pl.BlockSpec((K, tc), lambda i: (0, i))
    return pl.pallas_call(
        body, name=name, grid=(K // tr if by_rows else N // tc,),
        in_specs=[blk, gblk, gblk, blk, blk], out_specs=[blk] * 4,
        out_shape=[jax.ShapeDtypeStruct((1, K, N), F32)] * 4,
        compiler_params=_params(("parallel",)),
    )(w, g1, g2, m, v)


_HBM_SPEC = pl.BlockSpec(memory_space=pltpu.HBM)
_SEM_SPEC = pl.BlockSpec(memory_space=pltpu.SEMAPHORE)
_VMEM_SPEC = pl.BlockSpec(memory_space=pltpu.VMEM)
_EFFECT = pltpu.SideEffectType.DATAFLOW_SIDE_EFFECTING


def _place():
    return lax.axis_index("x"), lax.axis_index("y"), lax.axis_index("c")


def _other_chips(x, y):
    return [(1 - x, y), (x, 1 - y), (1 - x, 1 - y)]


def _chip_copies(src_ref, land_ref, sems, gather):
    x, y, c = _place()
    me = 2 * x + y
    out, back = [], []
    if gather == "half":
        half = src_ref.shape[0] // 2
        mine = pl.ds(pl.multiple_of(c * half, 16), half)
    for n, (px, py) in enumerate(_other_chips(x, y)):
        if gather == "half":
            src, there, here = src_ref.at[mine], land_ref.at[me, mine], land_ref.at[2 * px + py, mine]
        elif gather:
            src, there, here = src_ref, land_ref.at[me], land_ref.at[2 * px + py]
        else:
            src, there, here = src_ref.at[2 * px + py], land_ref.at[n], land_ref.at[n]
        out.append(pltpu.make_async_remote_copy(
            src_ref=src, dst_ref=there, send_sem=sems[n], recv_sem=sems[3 + n],
            device_id=(px, py, c), device_id_type=MESH))
        back.append(pltpu.make_async_remote_copy(
            src_ref=src, dst_ref=here, send_sem=sems[n], recv_sem=sems[3 + n],
            device_id=(px, py, c), device_id_type=MESH))
    return out, back


def _xchg_start(src, land, gather, order, name):
    def body(src_ref, land_ref, order_ref, *outs):
        sems = outs[0:6]
        token = outs[8]
        out, _ = _chip_copies(src_ref, land_ref, sems, gather)
        for cp in out:
            cp.start()
        token[...] = jnp.zeros_like(token)

    outs = pl.pallas_call(
        body, name=name,
        out_shape=(pltpu.SemaphoreType.DMA(()),) * 6 + (
            pltpu.HBM(src.shape, src.dtype), pltpu.HBM(land.shape, land.dtype),
            jax.ShapeDtypeStruct((8, LANE), F32)),
        in_specs=(_HBM_SPEC, _HBM_SPEC, _ANY_SPEC),
        out_specs=(_SEM_SPEC,) * 6 + (_HBM_SPEC, _HBM_SPEC, _VMEM_SPEC),
        input_output_aliases={0: 6, 1: 7},
        compiler_params=pltpu.CompilerParams(has_side_effects=_EFFECT),
    )(pltpu.with_memory_space_constraint(src, pltpu.HBM), pltpu.with_memory_space_constraint(land, pltpu.HBM), order)
    return outs[0:6], outs[6], outs[7], outs[8]


def _xchg_wait(started, gather, after, name):
    sems, src, land, _ = started
    after = after if isinstance(after, tuple) else (after,)

    def body(src_ref, land_ref, *rest):
        _, back = _chip_copies(src_ref, land_ref, rest[0:6], gather)
        for cp in back:
            cp.wait_send()
            cp.wait_recv()

    return pl.pallas_call(
        body, name=name,
        out_shape=(pltpu.HBM(src.shape, src.dtype), pltpu.HBM(land.shape, land.dtype)),
        in_specs=(_HBM_SPEC, _HBM_SPEC) + (_SEM_SPEC,) * 6 + (_ANY_SPEC,) * len(after),
        out_specs=(_HBM_SPEC, _HBM_SPEC),
        input_output_aliases={0: 0, 1: 1},
        compiler_params=pltpu.CompilerParams(has_side_effects=_EFFECT),
    )(src, land, *sems, *after)


def _forward_halves(land, name):
    _, R, C = land.shape
    half = R // 2
    assert half % 16 == 0

    def body(land_ref, out_ref, send_sems, recv_sems):
        x, y, c = _place()
        mine = pl.ds(pl.multiple_of(c * half, 16), half)
        theirs = pl.ds(pl.multiple_of((1 - c) * half, 16), half)
        sends = []
        for n, (px, py) in enumerate(_other_chips(x, y)):
            cp = pltpu.make_async_remote_copy(
                src_ref=land_ref.at[2 * px + py, mine], dst_ref=out_ref.at[2 * px + py, mine],
                send_sem=send_sems.at[n], recv_sem=recv_sems.at[n], device_id=(x, y, 1 - c), device_id_type=MESH)
            cp.start()
            sends.append(cp)
        for n, (px, py) in enumerate(_other_chips(x, y)):
            pltpu.make_async_remote_copy(
                src_ref=land_ref.at[2 * px + py, theirs], dst_ref=out_ref.at[2 * px + py, theirs],
                send_sem=send_sems.at[n], recv_sem=recv_sems.at[n], device_id=(x, y, 1 - c),
                device_id_type=MESH).wait_recv()
        for cp in sends:
            cp.wait_send()

    return pl.pallas_call(
        body, name=name,
        in_specs=[_ANY_SPEC], out_specs=_ANY_SPEC,
        out_shape=jax.ShapeDtypeStruct(land.shape, land.dtype),
        input_output_aliases={0: 0},
        scratch_shapes=[pltpu.SemaphoreType.DMA((3,)), pltpu.SemaphoreType.DMA((3,))],
    )(land)


def _sib_copy(src_ref, land_ref, send_sem, recv_sem):
    x, y, c = _place()
    return pltpu.make_async_remote_copy(src_ref=src_ref, dst_ref=land_ref, send_sem=send_sem, recv_sem=recv_sem,
                                        device_id=(x, y, 1 - c), device_id_type=MESH)


def _sib_start(src, name):
    land = lax.empty(src.shape, src.dtype)

    def body(src_ref, land_ref, send_sem, recv_sem, src_thru, land_thru, token):
        _sib_copy(src_ref, land_ref, send_sem, recv_sem).start()
        token[...] = jnp.zeros_like(token)

    return pl.pallas_call(
        body, name=name,
        out_shape=(pltpu.SemaphoreType.DMA(()), pltpu.SemaphoreType.DMA(()),
                   pltpu.HBM(src.shape, src.dtype), pltpu.HBM(land.shape, land.dtype),
                   jax.ShapeDtypeStruct((8, LANE), F32)),
        in_specs=(_HBM_SPEC, _HBM_SPEC),
        out_specs=(_SEM_SPEC, _SEM_SPEC, _HBM_SPEC, _HBM_SPEC, _VMEM_SPEC),
        input_output_aliases={0: 2, 1: 3},
        compiler_params=pltpu.CompilerParams(has_side_effects=_EFFECT),
    )(pltpu.with_memory_space_constraint(src, pltpu.HBM), pltpu.with_memory_space_constraint(land, pltpu.HBM))


def _sib_wait(started, after, name):
    send_sem, recv_sem, src, land, _ = started

    def body(src_ref, land_ref, send_sem, recv_sem, after_ref, src_out, land_out):
        cp = _sib_copy(src_ref, land_ref, send_sem, recv_sem)
        cp.wait_send()
        cp.wait_recv()

    return pl.pallas_call(
        body, name=name,
        out_shape=(pltpu.HBM(src.shape, src.dtype), pltpu.HBM(land.shape, land.dtype)),
        in_specs=(_HBM_SPEC, _HBM_SPEC, _SEM_SPEC, _SEM_SPEC, _ANY_SPEC),
        out_specs=(_HBM_SPEC, _HBM_SPEC),
        input_output_aliases={0: 0, 1: 1},
        compiler_params=pltpu.CompilerParams(has_side_effects=_EFFECT),
    )(src, land, send_sem, recv_sem, after)


def _sum_slabs(gp, recv, chip, name):
    _, R, C = gp.shape
    tr = _tile(R, PACK_ROWS, 16)

    def body(chip_ref, own_ref, r0_ref, r1_ref, r2_ref, o_ref):
        acc = own_ref[...].astype(F32) + r0_ref[...].astype(F32)
        o_ref[...] = (acc + r1_ref[...].astype(F32)) + r2_ref[...].astype(F32)

    def got(n):
        return pl.BlockSpec((None, tr, C), lambda i, chip_ref: (n, i, 0))

    return pl.pallas_call(
        body, name=name,
        grid_spec=pltpu.PrefetchScalarGridSpec(
            num_scalar_prefetch=1, grid=(R // tr,),
            in_specs=[pl.BlockSpec((None, tr, C), lambda i, chip_ref: (chip_ref[0], i, 0)), got(0), got(1), got(2)],
            out_specs=pl.BlockSpec((tr, C), lambda i, chip_ref: (i, 0))),
        out_shape=jax.ShapeDtypeStruct((R, C), F32),
        compiler_params=_params(("parallel",)),
    )(jnp.reshape(chip, (1,)).astype(jnp.int32), gp, recv, recv, recv)


def _all_reduce_vec(vec, name):
    VR, W = vec.shape

    def body(vec_ref, vall_ref, vout_ref, vsend_sems, vrecv_sems):
        x, y, c = _place()
        vall_ref[4 * x + 2 * y + c] = vec_ref[...]
        sends = []
        peers = []
        for r in range(1, N_DEV):
            dx, dy, dc = (r >> 2) & 1, (r >> 1) & 1, r & 1
            peer = (x ^ dx, y ^ dy, c ^ dc)
            peers.append(peer)
            cp = pltpu.make_async_remote_copy(
                src_ref=vec_ref, dst_ref=vall_ref.at[4 * x + 2 * y + c], send_sem=vsend_sems.at[r - 1],
                recv_sem=vrecv_sems.at[r - 1], device_id=peer, device_id_type=MESH)
            cp.start()
            sends.append(cp)
        for r, peer in enumerate(peers):
            pltpu.make_async_remote_copy(
                src_ref=vec_ref, dst_ref=vall_ref.at[4 * peer[0] + 2 * peer[1] + peer[2]],
                send_sem=vsend_sems.at[r], recv_sem=vrecv_sems.at[r],
                device_id=peer, device_id_type=MESH).wait_recv()
        total = vall_ref[0]
        for d in range(1, N_DEV):
            total = total + vall_ref[d]
        vout_ref[...] = total
        for cp in sends:
            cp.wait_send()

    outs = pl.pallas_call(
        body, name=name,
        in_specs=[_VMEM_SPEC], out_specs=[_VMEM_SPEC, _VMEM_SPEC],
        out_shape=[jax.ShapeDtypeStruct((N_DEV, VR, W), F32), jax.ShapeDtypeStruct((VR, W), F32)],
        scratch_shapes=[pltpu.SemaphoreType.DMA((N_DEV - 1,)), pltpu.SemaphoreType.DMA((N_DEV - 1,))],
    )(vec)
    return outs[1]


class _Pack:
    def __init__(self, group, C):
        self.group, self.C = group, C
        self.rows, self.offs, off = {}, {}, 0
        for nm, (K, N), _ in group:
            assert N <= C, nm
            self.rows[nm] = K if 2 * N > C else -(-(K * N) // C)
            self.offs[nm] = off
            off += -(-self.rows[nm] // 16) * 16
        self.used = off
        self.R = -(-off // PACK_ROWS) * PACK_ROWS

    def _rows_of(self, a):
        K, N = a.shape
        if 2 * N > self.C:
            a = jnp.pad(a, ((0, 0), (0, self.C - N)))
        else:
            a = jnp.pad(a.reshape(-1), (0, -(K * N) % self.C)).reshape(-1, self.C)
        return jnp.pad(a, ((0, -a.shape[0] % 16), (0, 0)))

    def pack(self, shards):
        parts = [self._rows_of(shards[nm].astype(BF16)) for nm, _, _ in self.group]
        return jnp.concatenate(parts + [jnp.zeros((self.R - self.used, self.C), BF16)], axis=0)

    def _shard_of(self, rows, shape):
        K, N = shape
        return rows[:, :N] if 2 * N > self.C else rows.reshape(-1)[:K * N].reshape(K, N)

    def part(self, flat, nm, shape):
        return self._shard_of(flat[self.offs[nm]:self.offs[nm] + self.rows[nm]], shape)

    def slab_rows(self, nm, g):
        (K, N), axis = next((shape, axis) for n, shape, axis in self.group if n == nm)
        cuts = [g[:, k * N:(k + 1) * N] if axis == 1 else g[k * K:(k + 1) * K, :] for k in range(N_CHIPS)]
        return jnp.stack([self._rows_of(c.astype(BF16)) for c in cuts])

    def slabs(self, grads):
        parts = [self.slab_rows(nm, grads[nm]) for nm, _, _ in self.group]
        return jnp.concatenate(parts + [jnp.zeros((N_CHIPS, self.R - self.used, self.C), BF16)], axis=1)

    def full(self, gathered, names=None):
        res = {}
        for nm, (K, N), axis in self.group:
            if names is None or nm in names:
                rows = gathered[:, self.offs[nm]:self.offs[nm] + self.rows[nm]]
                res[nm] = jnp.concatenate([self._shard_of(rows[k], (K, N)) for k in range(N_CHIPS)], axis=axis)
        return res


def _rope_tables(S):
    pos = jnp.arange(S, dtype=F32)
    inv = 1.0 / (ROPE_THETA ** (jnp.arange(0, MLA_ROPE, 2, dtype=F32) / MLA_ROPE))
    ang = pos[:, None] * inv[None, :]
    cos, sin = jnp.cos(ang), jnp.sin(ang)
    half = MLA_ROPE // 2
    z = jnp.zeros((S, half), F32)
    one = jnp.ones((S, LANE - MLA_ROPE), F32)
    zero = jnp.zeros((S, LANE - MLA_ROPE), F32)
    kc = jnp.concatenate([cos, cos, one], axis=1)
    ksa = jnp.concatenate([-sin, z, zero], axis=1)
    ksb = jnp.concatenate([z, sin, zero], axis=1)
    qc = jnp.concatenate([jnp.ones((S, MLA_NOPE), F32), kc], axis=1)
    qsa = jnp.concatenate([jnp.zeros((S, MLA_NOPE), F32), ksa], axis=1)
    qsb = jnp.concatenate([jnp.zeros((S, MLA_NOPE), F32), ksb], axis=1)
    return (kc, ksa, ksb), (qc, qsa, qsb)


def _pad_cols(a, width):
    return jnp.pad(a, ((0, 0), (0, width - a.shape[1])))


def kernel(x, attn_norm, w_in, fox_f_bias, q_norm, w_uq, kv_norm, w_ukv, w_mla_branch, w_fox_branch, w_out, mlp_norm, w_up, w_down, final_norm, loss_target, m_attn_norm, m_w_in, m_fox_f_bias, m_q_norm, m_w_uq, m_kv_norm, m_w_ukv, m_w_mla_branch, m_w_fox_branch, m_w_out, m_mlp_norm, m_w_up, m_w_down, m_final_norm, v_attn_norm, v_w_in, v_fox_f_bias, v_q_norm, v_w_uq, v_kv_norm, v_w_ukv, v_w_mla_branch, v_w_fox_branch, v_w_out, v_mlp_norm, v_w_up, v_w_down, v_final_norm):
    _, S, D = x.shape
    H, HF = MLA_HEADS, FOX_HEADS
    QL, KVL = MLA_Q_LORA, MLA_KV_LORA
    assert H == HF and H <= 8
    xs = x[0]
    target = loss_target[0]
    C = D
    chip = 2 * lax.axis_index("x") + lax.axis_index("y")

    def flip(a):
        return jnp.transpose(a, (0, 2, 1))

    w_in, m_w_in, v_w_in = flip(w_in), flip(m_w_in), flip(v_w_in)
    weights = {"attn_norm": attn_norm, "w_in": w_in, "fox_f_bias": fox_f_bias, "q_norm": q_norm, "w_uq": w_uq,
               "kv_norm": kv_norm, "w_ukv": w_ukv, "w_mla_branch": w_mla_branch, "w_fox_branch": w_fox_branch,
               "w_out": w_out, "mlp_norm": mlp_norm, "w_up": w_up, "w_down": w_down, "final_norm": final_norm}
    moments = {"attn_norm": (m_attn_norm, v_attn_norm), "w_in": (m_w_in, v_w_in), "fox_f_bias": (m_fox_f_bias, v_fox_f_bias),
               "q_norm": (m_q_norm, v_q_norm), "w_uq": (m_w_uq, v_w_uq), "kv_norm": (m_kv_norm, v_kv_norm),
               "w_ukv": (m_w_ukv, v_w_ukv), "w_mla_branch": (m_w_mla_branch, v_w_mla_branch),
               "w_fox_branch": (m_w_fox_branch, v_w_fox_branch), "w_out": (m_w_out, v_w_out),
               "mlp_norm": (m_mlp_norm, v_mlp_norm), "w_up": (m_w_up, v_w_up), "w_down": (m_w_down, v_w_down),
               "final_norm": (m_final_norm, v_final_norm)}

    def group(names_axes):
        return [(nm, weights[nm].shape[1:], axis) for nm, axis in names_axes]

    pack_a = _Pack(group([("w_in", 0), ("w_uq", 1), ("w_ukv", 1)]), C)
    pack_b = _Pack(group([("w_down", 0), ("w_up", 1), ("w_out", 0), ("w_mla_branch", 1), ("w_fox_branch", 1)]), C)
    RA, RB = pack_a.R, pack_b.R
    wp_a = pack_a.pack({nm: weights[nm][0] for nm, _, _ in pack_a.group})
    wp_b = pack_b.pack({nm: weights[nm][0] for nm, _, _ in pack_b.group})
    n_in = w_in.shape[1]
    rows_in = -(-n_in // 16) * 16
    assert pack_a.offs["w_in"] == 0 and all((k * n_in) % 16 + n_in <= rows_in for k in range(N_CHIPS))
    shifted = lax.dynamic_update_slice(jnp.zeros((rows_in, C), BF16), wp_a[:n_in], ((chip * n_in) % 16, 0))
    wp_a = jnp.concatenate([shifted, wp_a[rows_in:]], axis=0)
    ag_a = _xchg_start(wp_a, lax.empty((N_CHIPS, RA, C), BF16), "half", jnp.zeros((8, LANE), F32), "all_gather_start_a")
    xn = _norm_fwd(xs, attn_norm, "attn_norm_fwd", order=ag_a[3])
    own_a, land_a = _xchg_wait(ag_a, "half", (xn, wp_b), "all_gather_wait_a")
    land_a = _forward_halves(land_a, "all_gather_forward_a")
    gathered_a = lax.dynamic_update_slice(land_a, own_a[None], (chip, 0, 0))
    ag_b = _xchg_start(wp_b, lax.empty((N_CHIPS, RB, C), BF16), True, gathered_a, "all_gather_start_b")
    full = pack_a.full(gathered_a, ("w_uq", "w_ukv"))
    tile0 = [(k * n_in) // 16 * 16 for k in range(N_CHIPS)]
    total = tile0[-1] + rows_in
    full["w_in"] = sum(jnp.pad(gathered_a[k, :rows_in], ((tile0[k], total - tile0[k] - rows_in), (0, 0)))
                       for k in range(N_CHIPS))

    o_ckv = QL
    o_kr = o_ckv + KVL
    o_fq = o_kr + MLA_ROPE
    o_ff = o_fq + 3 * HF * FOX_HEAD_DIM
    o_g = o_ff + HF
    wi = full["w_in"]
    assert N_CHIPS * n_in == o_g + 2 * D and wi.shape[0] >= o_g + 2 * D
    WS = QL + KVL + 2 * LANE
    NQKV = 3 * HF * FOX_HEAD_DIM

    def pad_rows(a, rows):
        return jnp.pad(a, ((0, rows - a.shape[0]), (0, 0)))

    w_small = jnp.concatenate([wi[:o_kr], pad_rows(wi[o_kr:o_fq], LANE), pad_rows(wi[o_ff:o_g], LANE)], axis=0)
    w_qkv = wi[o_fq:o_ff]
    w_g = wi[o_g:o_g + 2 * D]
    w_pack = jnp.concatenate([w_small, w_qkv, w_g], axis=0)
    dqk = MLA_NOPE + MLA_ROPE
    w_uq_p = jnp.pad(full["w_uq"].reshape(QL, H, dqk), ((0, 0), (0, 0), (0, QPAD - dqk))).reshape(QL, H * QPAD)
    ukv = full["w_ukv"].reshape(KVL, H, MLA_NOPE + MLA_V)
    w_ukv_p = jnp.concatenate([ukv[:, :, :MLA_NOPE].reshape(KVL, H * MLA_NOPE),
                               ukv[:, :, MLA_NOPE:].reshape(KVL, H * MLA_V)], axis=1)

    (kc, ksa, ksb), (qc, qsa, qsb) = _rope_tables(S)
    bias_pad = _pad_cols(fox_f_bias, LANE)

    small = _matmul(xn, w_small, "nt", [F32], "proj_small")
    n_fq = HF * FOX_HEAD_DIM
    q_scale = jnp.concatenate([jnp.full((1, n_fq), LOG2E / math.sqrt(FOX_HEAD_DIM), F32),
                               jnp.ones((1, NQKV - n_fq), F32)], axis=1)
    qkv = _matmul(xn, w_qkv, "nt", [BF16], "proj_qkv", col_extras=(q_scale,), epilogue=lambda acc, cs: (acc * cs,))
    gpre = _matmul(xn, w_g, "nt", [F32], "proj_gates")
    cqn, ckvn, kr, cum = _prep_fwd(small, q_norm, kv_norm, bias_pad, kc, ksa, ksb, HF, "prep_fwd")
    c2_mla = LOG2E / math.sqrt(dqk)
    q_rot = _matmul(cqn, w_uq_p, "nn", [BF16], "mla_q_up", tn=QPAD, row_extras=(qc * c2_mla, qsa * c2_mla, qsb * c2_mla),
                    epilogue=lambda acc, c, sa, sb: (_rope(acc, c, sa, sb, 1),))
    kv2 = _matmul(ckvn, w_ukv_p, "nn", [BF16], "mla_kv_up")

    mla = _AttT(S, H, (q_rot, QPAD, 0, True), [(kv2, MLA_NOPE, 0, True), (kr, LANE, 0, False)],
                (kv2, MLA_V, H, True), 1.0 / math.sqrt(dqk), True)
    o_mla, lse_mla = _att_fwd_t(mla, "mla_att_fwd")

    cum_t = jnp.transpose(cum[:, :HF]) * LOG2E
    cum_rep = jnp.broadcast_to(cum_t[:, :, None], (HF, S, min(QSUB, _tile(S, ATT_T))))
    fox = _AttT(S, HF, (qkv, FOX_HEAD_DIM, 0, True), [(qkv, FOX_HEAD_DIM, HF, True)],
                (qkv, FOX_HEAD_DIM, 2 * HF, True), 1.0 / math.sqrt(FOX_HEAD_DIM), False, cum_rep)
    o_fox, ox_fox, lse_fox = _att_fwd_t(fox, "fox_att_fwd", exact=True)

    own_b, land_b = _xchg_wait(ag_b, True, (lse_fox, lse_mla, gpre), "all_gather_wait_b")
    gathered_b = lax.dynamic_update_slice(land_b, own_b[None], (chip, 0, 0))
    full.update(pack_b.full(gathered_b, ("w_mla_branch", "w_fox_branch", "w_out")))
    w_mb, w_fb, w_o = (full[n] for n in ("w_mla_branch", "w_fox_branch", "w_out"))

    def b_of(nm, mode, tn, tk):
        (K, N), axis = next((shape, axis) for n, shape, axis in pack_b.group if n == nm)
        off = pack_b.offs[nm]
        shape = (N_CHIPS * K, N) if axis == 0 else (K, N_CHIPS * N)
        t_r, t_c = (tk, tn) if mode == "nn" else (tn, tk)
        t_r, t_c = _tile(shape[0], t_r), _tile(shape[1], t_c)
        if not (N == C and K % t_r == 0 and N % t_c == 0 and off % t_r == 0):
            return pack_b.full(gathered_b, (nm,))[nm], None
        base = off // t_r
        if axis == 0:
            per = K // t_r
            place = lambda rb, cb: (rb // per, base + rb % per, cb)
        else:
            per = N // t_c
            place = lambda rb, cb: (cb // per, base + rb, cb % per)
        return gathered_b, (shape, (lambda j, k: place(k, j)) if mode == "nn" else (lambda j, k: place(j, k)))

    y_mla = _matmul(o_mla, w_mb, "nn", [F32], "mla_branch")

    def gate_merge(acc, ga, gb, ya):
        return acc, _sigmoid(ga) * ya + _sigmoid(gb) * acc

    y_fox, merged = _matmul(o_fox, w_fb, "nn", [F32, BF16], "fox_branch_gates", tn=512,
                            extras=((gpre, 0), (gpre, 1), y_mla), epilogue=gate_merge)
    h1 = _matmul(merged, w_o, "nn", [F32], "out_proj", extras=(xs,), epilogue=lambda acc, r: (acc + r,))
    hn = _norm_fwd(h1, mlp_norm, "mlp_norm_fwd")

    def relu2(acc):
        a = jnp.maximum(acc, 0.0)
        return a * a, a

    w_u, w_u_in = b_of("w_up", "nn", 1024, 2048)
    u, a_pos = _matmul(hn, w_u, "nn", [BF16, BF16], "mlp_up", epilogue=relu2, b_in=w_u_in)
    w_d, w_d_in = b_of("w_down", "nn", 1024, 2048)
    h2 = _matmul(u, w_d, "nn", [F32], "mlp_down", tn=1024, extras=(h1,), epilogue=lambda acc, r: (acc + r,),
                 b_in=w_d_in)
    dh2, dh2_b, g_final, loss_part = _final(h2, final_norm.reshape(1, D), target, "final_norm_loss")

    gp_b = lax.empty((N_CHIPS, RB, C), BF16)
    by_glue = {}

    def grad_b(nm, a, b, name):
        nonlocal gp_b
        (K, N), axis = next((shape, axis) for n, shape, axis in pack_b.group if n == nm)
        off = pack_b.offs[nm]
        tm = min(1024, K) if axis == 0 else min(1024, a.shape[1])
        tn = min(1024, N) if axis == 1 else min(1024, b.shape[1])
        if not (N == C and tm % LANE == 0 and tn % LANE == 0 and K % tm == 0 and N % tn == 0 and off % tm == 0):
            by_glue[nm] = _mm_tn(a, b, name)
            return
        base = off // tm
        if axis == 0:
            per = K // tm
            place = lambda i, j: (i // per, base + i % per, j)
        else:
            per = N // tn
            place = lambda i, j: (j // per, base + i, j % per)
        gp_b = _mm_tn(a, b, name, tm=tm, tn=tn, into=(gp_b, place))

    w_d, w_d_in = b_of("w_down", "nt", 1024, 2048)
    da = _matmul(dh2_b, w_d, "nt", [BF16], "mlp_down_dx", extras=(a_pos,),
                 epilogue=lambda acc, a: (acc * (2.0 * a.astype(F32)),), b_in=w_d_in)
    grad_b("w_down", u, dh2_b, "mlp_down_dw")
    w_u, w_u_in = b_of("w_up", "nt", 1024, 2048)
    dhn = _matmul(da, w_u, "nt", [F32], "mlp_up_dx", tn=1024, b_in=w_u_in)
    grad_b("w_up", hn, da, "mlp_up_dw")
    dh1, dh1_b, g_mlp_norm = _norm_bwd(h1, dhn, mlp_norm, dh2, "mlp_norm_bwd")

    def gate_bwd(acc, ga, gb, ya, yb):
        ga, gb = _sigmoid(ga), _sigmoid(gb)
        return acc * ga, acc * gb, acc * ya * (ga * (1.0 - ga)), acc * yb * (gb * (1.0 - gb))

    dy_mla, dy_fox, dg_mla, dg_fox = _matmul(dh1_b, w_o, "nt", [BF16] * 4, "out_proj_dx_gates", tn=512,
                                             extras=((gpre, 0), (gpre, 1), y_mla, y_fox), epilogue=gate_bwd)
    grad_b("w_out", merged, dh1_b, "out_proj_dw")
    do_mla = _matmul(dy_mla, w_mb, "nt", [BF16], "mla_branch_dx")
    grad_b("w_mla_branch", o_mla, dy_mla, "mla_branch_dw")
    do_fox = _matmul(dy_fox, w_fb, "nt", [BF16], "fox_branch_dx")
    grad_b("w_fox_branch", o_fox, dy_fox, "fox_branch_dw")
    for nm, g in by_glue.items():
        gp_b = lax.dynamic_update_slice(gp_b, pack_b.slab_rows(nm, g), (0, pack_b.offs[nm], 0))
    if RB > pack_b.used:
        gp_b = lax.dynamic_update_slice(gp_b, jnp.zeros((N_CHIPS, RB - pack_b.used, C), BF16), (0, pack_b.used, 0))

    rs_b = _xchg_start(gp_b, lax.empty((3, RB, C), BF16), False, do_fox, "grad_scatter_start_b")

    delta_mla = _att_delta_t(do_mla, o_mla, H, "mla_att_delta", order=rs_b[3])
    dq_rot, dk_nope, dkr_heads, dv_mla = _att_bwd_t(mla, do_mla, lse_mla, delta_mla, BF16, [BF16, F32],
                                                    "mla_att_bwd", dq_rope=(qc, qsa, qsb))
    delta_fox = _att_delta_t(do_fox, ox_fox, HF, "fox_att_delta")
    dfq, dfk, dfv, dcum = _att_bwd_t(fox, do_fox, lse_fox, delta_fox, BF16, [BF16], "fox_att_bwd")

    gp_b_sent, recv_b = _xchg_wait(rs_b, False, (dfq, dq_rot), "grad_scatter_wait_b")
    swap_b = _sib_start(_sum_slabs(gp_b_sent, recv_b, chip, "grad_sum_b"), "grad_swap_start_b")

    dcqn = _matmul(dq_rot, w_uq_p, "nt", [F32], "mla_q_up_dx", order=swap_b[4])
    g_w_uq_p = _mm_tn(cqn, dq_rot, "mla_q_up_dw")
    dkv2 = jnp.concatenate([dk_nope, dv_mla], axis=1)
    dckvn = _matmul(dkv2, w_ukv_p, "nt", [F32], "mla_kv_up_dx")
    g_w_ukv_p = _mm_tn(ckvn, dkv2, "mla_kv_up_dw")

    dcum_rows = jnp.pad(dcum[:, :, 0], ((0, 8 - HF), (0, 0)))
    dlogf_rows = _suffix_sum_rows(dcum_rows, "fox_forget_suffix_sum")
    dlogf = _pad_cols(jnp.transpose(dlogf_rows[:HF]), LANE)
    d_small, g_q_norm, g_kv_norm, g_bias = _prep_bwd(
        small, dcqn, dckvn, dkr_heads, dlogf, q_norm, kv_norm, bias_pad, kc, ksa, ksb, H, "prep_bwd")
    dproj = [d_small, dfq, dfk, dfv, dg_mla, dg_fox]
    gs, gfq, gfk, gfv, gg_mla, gg_fox = [
        _matmul(part, xn, "tn", [BF16], "proj_dw_" + tag, tm=1024, tn=1024, tk=2048)
        for part, tag in zip(dproj, ("small", "fq", "fk", "fv", "g_mla", "g_fox"))]

    g_w_in = jnp.concatenate([gs[:o_kr], gs[o_kr:o_kr + MLA_ROPE], gfq, gfk, gfv,
                              gs[o_kr + LANE:o_kr + LANE + HF], gg_mla, gg_fox], axis=0)
    g_w_uq = g_w_uq_p.reshape(QL, H, QPAD)[:, :, :dqk].reshape(QL, H * dqk)
    g_w_ukv = jnp.concatenate([g_w_ukv_p[:, :H * MLA_NOPE].reshape(KVL, H, MLA_NOPE),
                               g_w_ukv_p[:, H * MLA_NOPE:].reshape(KVL, H, MLA_V)], axis=2).reshape(KVL, -1)

    gp_a = pack_a.slabs({"w_in": g_w_in, "w_uq": g_w_uq, "w_ukv": g_w_ukv})
    rs_a = _xchg_start(gp_a, lax.empty((3, RA, C), BF16), False, gg_fox, "grad_scatter_start_a")
    dxn = _matmul_parts(dproj, w_pack, "nn", F32, "proj_dx", order=rs_a[3])
    grad_x, _, g_attn_norm = _norm_bwd(xs, dxn, attn_norm, dh1, "attn_norm_bwd")
    gp_a_sent, recv_a = _xchg_wait(rs_a, False, grad_x, "grad_scatter_wait_a")
    swap_a = _sib_start(_sum_slabs(gp_a_sent, recv_a, chip, "grad_sum_a"), "grad_swap_start_a")
    vec_w = max(D, LANE)
    vec_rows = [g_attn_norm, g_mlp_norm, g_final, g_q_norm, g_kv_norm, g_bias, loss_part]
    vec = jnp.concatenate([_pad_cols(v, vec_w) for v in vec_rows] + [jnp.zeros((1, vec_w), F32)], axis=0)
    vsum = _all_reduce_vec(vec, "all_reduce_vectors")
    part_b, sib_b = _sib_wait(swap_b, vsum, "grad_swap_wait_b")

    grads, deltas, new_m, new_v = {}, {}, {}, {}

    def update(pack, mine, theirs):
        for nm, shape, _ in pack.group:
            K, N = shape
            if N == pack.C and K % 8 == 0 and pack.offs[nm] % _tile(K, 256, 8) == 0:
                g, d, nm_, nv_ = _adamw(weights[nm], mine, theirs, moments[nm][0], moments[nm][1], "adamw_" + nm,
                                        g_row=pack.offs[nm])
            else:
                g, d, nm_, nv_ = _adamw(weights[nm], pack.part(mine, nm, shape), pack.part(theirs, nm, shape),
                                        moments[nm][0], moments[nm][1], "adamw_" + nm)
            grads[nm], deltas[nm], new_m[nm], new_v[nm] = g, d, nm_, nv_
        return g

    last_b = update(pack_b, part_b, sib_b)
    part_a, sib_a = _sib_wait(swap_a, last_b, "grad_swap_wait_a")
    update(pack_a, part_a, sib_a)

    vec_names = ["attn_norm", "mlp_norm", "final_norm", "q_norm", "kv_norm", "fox_f_bias"]

    def vec_pack(arrs):
        return jnp.concatenate([_pad_cols(a.reshape(1, -1), vec_w) for a in arrs]
                               + [jnp.zeros((2, vec_w), F32)], axis=0)[None]

    vg, vd, vm, vv = _adamw(vec_pack([weights[n] for n in vec_names]), vsum, jnp.zeros_like(vsum),
                            vec_pack([moments[n][0] for n in vec_names]), vec_pack([moments[n][1] for n in vec_names]),
                            "adamw_vectors")
    for r, nm in enumerate(vec_names):
        shp = weights[nm].shape
        n = weights[nm].size
        grads[nm] = vsum[r, :n].reshape(shp)
        deltas[nm], new_m[nm], new_v[nm] = (vd[0, r, :n].reshape(shp), vm[0, r, :n].reshape(shp),
                                            vv[0, r, :n].reshape(shp))
    loss = vsum[6, 0]

    for res in (grads, deltas, new_m, new_v):
        res["w_in"] = flip(res["w_in"])
    order = ["attn_norm", "w_in", "fox_f_bias", "q_norm", "w_uq", "kv_norm", "w_ukv", "w_mla_branch", "w_fox_branch",
             "w_out", "mlp_norm", "w_up", "w_down", "final_norm"]
    return (loss, grad_x[None], *[grads[n] for n in order], *[deltas[n] for n in order],
            *[new_m[n] for n in order], *[new_v[n] for n in order])
```

```python
import math

import jax
import jax.numpy as jnp
from jax import lax
from jax.experimental import pallas as pl
from jax.experimental.pallas import tpu as pltpu

CHUNK = 64
MLA_HEADS = 8
MLA_Q_LORA = 512
MLA_KV_LORA = 256
MLA_NOPE = 128
MLA_ROPE = 64
MLA_V = 128
ROPE_THETA = 10000.0
FOX_HEADS = 8
FOX_HEAD_DIM = 128
EPS = 1e-6

ADAM_LR = 0.001
ADAM_B1 = 0.9
ADAM_B2 = 0.999
ADAM_EPS = 1e-08
ADAM_WD = 0.01
ADAM_STEP = 10

LANE = 128
QPAD = 2 * LANE
N_CHIPS = 4
N_DEV = 8
VMEM_LIMIT = 48 * 1024 * 1024
ATT_T = 2048
QSUB = 256
ROW_T = 256
PACK_ROWS = 256
LOG2E = 1.4426950408889634

BF16 = jnp.bfloat16
F32 = jnp.float32
MESH = pl.DeviceIdType.MESH

_NT = (((1,), (1,)), ((), ()))
_TN = (((0,), (0,)), ((), ()))
_NN = (((1,), (0,)), ((), ()))


def _tile(dim, pref, align=LANE):
    if dim <= pref:
        return dim
    t = (pref // align) * align
    while t >= align:
        if dim % t == 0:
            return t
        t -= align
    return dim


def _params(sem=None):
    return pltpu.CompilerParams(dimension_semantics=sem, vmem_limit_bytes=VMEM_LIMIT)


_ANY_SPEC = pl.BlockSpec(memory_space=pl.ANY)


def _matmul(a, b, mode, out_dtypes, name, *, tm=1024, tn=1024, tk=2048, extras=(), row_extras=(), col_extras=(),
            epilogue=None, order=None, into=None, b_in=None):
    b_shape = b.shape if b_in is None else b_in[0]
    if mode == "nn":
        (M, K), (K2, N) = a.shape, b_shape
    elif mode == "nt":
        (M, K), (N, K2) = a.shape, b_shape
    else:
        (K, M), (K2, N) = a.shape, b_shape
    assert K == K2, (name, a.shape, b_shape)
    tm, tn, tk = _tile(M, tm), _tile(N, tn), _tile(K, tk)
    nk = K // tk
    extras = [e if isinstance(e, tuple) else (e, 0) for e in extras]
    n_out = len(out_dtypes)
    n_ex = len(extras) + len(row_extras) + len(col_extras)
    n_ord = 0 if order is None else 1
    assert all(r.shape == (M, tn) for r in row_extras), name
    dims = {"nn": _NN, "nt": _NT, "tn": _TN}[mode]

    def body(*refs):
        a_ref, b_ref = refs[0], refs[1]
        ex_refs = refs[2:2 + n_ex]
        o_refs = refs[2 + n_ex + n_ord:2 + n_ex + n_ord + n_out]
        acc_ref = refs[2 + n_ex + n_ord + n_out]
        k = pl.program_id(2)
        part = lax.dot_general(a_ref[...], b_ref[...], dims, preferred_element_type=F32)

        @pl.when(k == 0)
        def _():
            acc_ref[...] = part

        @pl.when(k > 0)
        def _():
            acc_ref[...] += part

        @pl.when(k == nk - 1)
        def _():
            acc = acc_ref[...]
            if epilogue is None:
                outs = (acc,)
            else:
                outs = epilogue(acc, *[r[...] for r in ex_refs])
            for o_ref, o in zip(o_refs, outs):
                o_ref[...] = o.astype(o_ref.dtype)

    if mode == "nn":
        a_spec = pl.BlockSpec((tm, tk), lambda i, j, k: (i, k))
        b_spec = pl.BlockSpec((tk, tn), lambda i, j, k: (k, j))
    elif mode == "nt":
        a_spec = pl.BlockSpec((tm, tk), lambda i, j, k: (i, k))
        b_spec = pl.BlockSpec((tn, tk), lambda i, j, k: (j, k))
    else:
        a_spec = pl.BlockSpec((tk, tm), lambda i, j, k: (k, i))
        b_spec = pl.BlockSpec((tk, tn), lambda i, j, k: (k, j))
    if b_in is not None:
        b_block = (None, tn, tk) if mode == "nt" else (None, tk, tn)
        b_spec = pl.BlockSpec(b_block, lambda i, j, k: b_in[1](j, k))
    mn_spec = pl.BlockSpec((tm, tn), lambda i, j, k: (i, j))
    row_spec = pl.BlockSpec((tm, tn), lambda i, j, k: (i, 0))
    col_spec = pl.BlockSpec((1, tn), lambda i, j, k: (0, j))
    out_specs = [mn_spec] * n_out
    out_shape = [jax.ShapeDtypeStruct((M, N), dt) for dt in out_dtypes]
    aliases = {}
    if into is not None:
        buf, place = into
        assert n_out == 1 and n_ord == 1 and order is buf, name
        out_specs = [pl.BlockSpec((None, tm, tn), lambda i, j, k: place(i, j))]
        out_shape = [jax.ShapeDtypeStruct(buf.shape, buf.dtype)]
        aliases = {2 + n_ex: 0}
    outs = pl.pallas_call(
        body,
        name=name,
        grid=(M // tm, N // tn, nk),
        in_specs=([a_spec, b_spec]
                  + [pl.BlockSpec((tm, tn), lambda i, j, k, g=g: (i, j + g * (N // tn))) for _, g in extras]
                  + [row_spec] * len(row_extras) + [col_spec] * len(col_extras) + [_ANY_SPEC] * n_ord),
        out_specs=out_specs,
        out_shape=out_shape,
        scratch_shapes=[pltpu.VMEM((tm, tn), F32)],
        input_output_aliases=aliases,
        compiler_params=_params(("parallel", "parallel", "arbitrary")),
    )(a, b, *[e for e, _ in extras], *row_extras, *col_extras, *([] if order is None else [order]))
    return outs[0] if n_out == 1 else outs


def _matmul_parts(parts, b, mode, out_dtype, name, *, tm=1024, tn=1024, tk=1024, order=None):
    assert mode in ("nn", "tn")
    if mode == "nn":
        M, (K, N) = parts[0].shape[0], b.shape
        widths = [p.shape[1] for p in parts]
    else:
        K, N = b.shape
        widths = [p.shape[1] for p in parts]
        M = sum(widths)
    common = math.gcd(*widths)
    tm, tn, tk = _tile(M if mode == "nn" else common, tm), _tile(N, tn), _tile(common if mode == "nn" else K, tk)
    t_part = tk if mode == "nn" else tm
    assert sum(widths) == (K if mode == "nn" else M), name
    if any(w % t_part for w in widths):
        parts, widths = [jnp.concatenate(parts, axis=1)], [sum(widths)]
    lo =[sum(widths[:p]) // t_part for p in range(len(parts))]
    cnt = [w // t_part for w in widths]
    nk = K // tk
    n_parts = len(parts)
    n_ord = 0 if order is None else 1
    dims = _NN if mode == "nn" else _TN

    def body(*refs):
        a_refs = refs[0:n_parts]
        b_ref = refs[n_parts]
        o_ref, acc_ref = refs[n_parts + 1 + n_ord], refs[n_parts + 2 + n_ord]
        i, k = pl.program_id(0), pl.program_id(2)
        sel = k if mode == "nn" else i
        for p in range(n_parts):
            @pl.when((sel >= lo[p]) & (sel < lo[p] + cnt[p]))
            def _(p=p):
                part = lax.dot_general(a_refs[p][...], b_ref[...], dims, preferred_element_type=F32)

                @pl.when(k == 0)
                def _():
                    acc_ref[...] = part

                @pl.when(k > 0)
                def _():
                    acc_ref[...] += part

        @pl.when(k == nk - 1)
        def _():
            o_ref[...] = acc_ref[...].astype(o_ref.dtype)

    def a_spec(p):
        if mode == "nn":
            return pl.BlockSpec((tm, tk), lambda i, j, k: (i, jnp.clip(k - lo[p], 0, cnt[p] - 1)))
        return pl.BlockSpec((tk, tm), lambda i, j, k: (
            jnp.where((i >= lo[p]) & (i < lo[p] + cnt[p]), k, 0), jnp.clip(i - lo[p], 0, cnt[p] - 1)))

    return pl.pallas_call(
        body, name=name, grid=(M // tm, N // tn, nk),
        in_specs=[a_spec(p) for p in range(n_parts)] + [pl.BlockSpec((tk, tn), lambda i, j, k: (k, j))]
        + [_ANY_SPEC] * n_ord,
        out_specs=pl.BlockSpec((tm, tn), lambda i, j, k: (i, j)),
        out_shape=jax.ShapeDtypeStruct((M, N), out_dtype),
        scratch_shapes=[pltpu.VMEM((tm, tn), F32)],
        compiler_params=_params(("parallel", "parallel", "arbitrary")),
    )(*parts, b, *([] if order is None else [order]))


def _mm_tn(a, b, name, tm=1024, tn=1024, into=None):
    return _matmul(a, b, "tn", [F32], name, tm=tm, tn=tn, tk=2048, into=into,
                   order=None if into is None else into[0])


def _row_spec(ts, width, col=0):
    return pl.BlockSpec((ts, width), lambda i: (i, col))


def _full_spec(shape):
    return pl.BlockSpec(shape, lambda i: tuple(0 for _ in shape))


def _rms(x):
    return lax.rsqrt(jnp.mean(x * x, axis=-1, keepdims=True) + EPS)


def _rms_bwd(x, dy, g):
    r = _rms(x)
    xh = x * r
    gy = dy * g
    dx = r * (gy - xh * jnp.mean(xh * gy, axis=-1, keepdims=True))
    return dx, dy * xh


def _norm_fwd(x, g, name, order=None):
    S, D = x.shape
    ts = _tile(S, ROW_T, 8)

    def body(x_ref, g_ref, *rest):
        o_ref = rest[-1]
        xv = x_ref[...]
        o_ref[...] = ((xv * _rms(xv)) * g_ref[...]).astype(BF16)

    extra = [] if order is None else [order]
    return pl.pallas_call(
        body, name=name, grid=(S // ts,),
        in_specs=[_row_spec(ts, D), _full_spec((1, D))] + [_ANY_SPEC] * len(extra),
        out_specs=_row_spec(ts, D),
        out_shape=jax.ShapeDtypeStruct((S, D), BF16),
        compiler_params=_params(("parallel",)),
    )(x, g, *extra)


def _norm_bwd(x, dy, g, dres, name, with_bf16=True):
    S, D = x.shape
    ts = _tile(S, ROW_T, 8)

    def body(x_ref, dy_ref, g_ref, dres_ref, dx_ref, *rest):
        dg_ref = rest[-1]
        dx, dg_rows = _rms_bwd(x_ref[...], dy_ref[...], g_ref[...])
        dx = dres_ref[...] + dx
        dx_ref[...] = dx
        if with_bf16:
            rest[0][...] = dx.astype(BF16)

        @pl.when(pl.program_id(0) == 0)
        def _():
            dg_ref[...] = jnp.zeros_like(dg_ref)

        dg_ref[...] += jnp.sum(dg_rows, axis=0, keepdims=True)

    return pl.pallas_call(
        body, name=name, grid=(S // ts,),
        in_specs=[_row_spec(ts, D), _row_spec(ts, D), _full_spec((1, D)), _row_spec(ts, D)],
        out_specs=[_row_spec(ts, D)] * (2 if with_bf16 else 1) + [_full_spec((1, D))],
        out_shape=([jax.ShapeDtypeStruct((S, D), F32)] + [jax.ShapeDtypeStruct((S, D), BF16)] * with_bf16
                   + [jax.ShapeDtypeStruct((1, D), F32)]),
        compiler_params=_params(("arbitrary",)),
    )(x, dy, g, dres)


def _rope(x, c, sa, sb, sign):
    w = x.shape[-1]
    half = MLA_ROPE // 2
    fwd = pltpu.roll(x, w - half, 1)
    back = pltpu.roll(x, half, 1)
    if sign < 0:
        return x * c - fwd * sa - back * sb
    return x * c + fwd * sa + back * sb


def _split3(x):
    hi = x.astype(BF16)
    r1 = x - hi.astype(F32)
    mid = r1.astype(BF16)
    lo = (r1 - mid.astype(F32)).astype(BF16)
    return hi, mid, lo


def _prep_fwd(small, q_norm, kv_norm, bias_pad, kc, ksa, ksb, n_heads, name):
    S, W = small.shape
    QL, KVL = q_norm.shape[1], kv_norm.shape[1]
    assert W == QL + KVL + 2 * LANE
    ts = _tile(S, ROW_T, 8)
    tri = (lax.broadcasted_iota(jnp.int32, (ts, ts), 0) >= lax.broadcasted_iota(jnp.int32, (ts, ts), 1)).astype(BF16)

    def body(s_ref, qn_ref, kvn_ref, b_ref, kc_ref, ksa_ref, ksb_ref, tri_ref,
             cqn_ref, ckvn_ref, kr_ref, cum_ref, carry_ref):
        cq = s_ref[:, 0:QL]
        cqn_ref[...] = ((cq * _rms(cq)) * qn_ref[...]).astype(BF16)
        ckv = s_ref[:, QL:QL + KVL]
        ckvn_ref[...] = ((ckv * _rms(ckv)) * kvn_ref[...]).astype(BF16)
        kr = s_ref[:, QL + KVL:QL + KVL + LANE]
        kr_ref[...] = _rope(kr, kc_ref[...], ksa_ref[...], ksb_ref[...], 1).astype(BF16)
        z = s_ref[:, QL + KVL + LANE:W] + b_ref[...]
        logf = jnp.minimum(z, 0.0) - jnp.log1p(jnp.exp(-jnp.abs(z)))
        lane = lax.broadcasted_iota(jnp.int32, logf.shape, 1)
        logf = jnp.where(lane < n_heads, logf, 0.0)

        @pl.when(pl.program_id(0) == 0)
        def _():
            carry_ref[...] = jnp.zeros_like(carry_ref)

        t = tri_ref[...]
        cum = carry_ref[...]
        for part in _split3(logf):
            cum = cum + jnp.dot(t, part, preferred_element_type=F32)
        cum_ref[...] = cum
        carry_ref[...] = cum[ts - 1:ts, :]

    return pl.pallas_call(
        body, name=name, grid=(S // ts,),
        in_specs=[_row_spec(ts, W), _full_spec((1, QL)), _full_spec((1, KVL)), _full_spec((1, LANE)),
                  _row_spec(ts, LANE), _row_spec(ts, LANE), _row_spec(ts, LANE), _full_spec((ts, ts))],
        out_specs=[_row_spec(ts, QL), _row_spec(ts, KVL), _row_spec(ts, LANE), _row_spec(ts, LANE)],
        out_shape=[jax.ShapeDtypeStruct((S, QL), BF16), jax.ShapeDtypeStruct((S, KVL), BF16),
                   jax.ShapeDtypeStruct((S, LANE), BF16), jax.ShapeDtypeStruct((S, LANE), F32)],
        scratch_shapes=[pltpu.VMEM((1, LANE), F32)],
        compiler_params=_params(("arbitrary",)),
    )(small, q_norm, kv_norm, bias_pad, kc, ksa, ksb, tri)


def _prep_bwd(small, dcqn, dckvn, dkr_heads, dlogf, q_norm, kv_norm, bias_pad, kc, ksa, ksb, n_heads, name):
    S, W = small.shape
    QL, KVL = q_norm.shape[1], kv_norm.shape[1]
    ts = _tile(S, ROW_T, 8)

    def body(s_ref, dcq_ref, dckv_ref, dkr_ref, dlf_ref, qn_ref, kvn_ref, b_ref, kc_ref, ksa_ref, ksb_ref,
             ds_ref, gq_ref, gkv_ref, gb_ref):
        dcq, gq_rows = _rms_bwd(s_ref[:, 0:QL], dcq_ref[...], qn_ref[...])
        ds_ref[:, 0:QL] = dcq.astype(BF16)
        dckv, gkv_rows = _rms_bwd(s_ref[:, QL:QL + KVL], dckv_ref[...], kvn_ref[...])
        ds_ref[:, QL:QL + KVL] = dckv.astype(BF16)
        dkr = dkr_ref[:, 0:LANE]
        for h in range(1, n_heads):
            dkr = dkr + dkr_ref[:, h * LANE:(h + 1) * LANE]
        ds_ref[:, QL + KVL:QL + KVL + LANE] = _rope(dkr, kc_ref[...], ksa_ref[...], ksb_ref[...], -1).astype(BF16)
        z = s_ref[:, QL + KVL + LANE:W] + b_ref[...]
        dff = dlf_ref[...] * (1.0 / (1.0 + jnp.exp(z)))
        ds_ref[:, QL + KVL + LANE:W] = dff.astype(BF16)

        @pl.when(pl.program_id(0) == 0)
        def _():
            gq_ref[...] = jnp.zeros_like(gq_ref)
            gkv_ref[...] = jnp.zeros_like(gkv_ref)
            gb_ref[...] = jnp.zeros_like(gb_ref)

        gq_ref[...] += jnp.sum(gq_rows, axis=0, keepdims=True)
        gkv_ref[...] += jnp.sum(gkv_rows, axis=0, keepdims=True)
        gb_ref[...] += jnp.sum(dff, axis=0, keepdims=True)

    return pl.pallas_call(
        body, name=name, grid=(S // ts,),
        in_specs=[_row_spec(ts, W), _row_spec(ts, QL), _row_spec(ts, KVL), _row_spec(ts, n_heads * LANE),
                  _row_spec(ts, LANE), _full_spec((1, QL)), _full_spec((1, KVL)), _full_spec((1, LANE)),
                  _row_spec(ts, LANE), _row_spec(ts, LANE), _row_spec(ts, LANE)],
        out_specs=[_row_spec(ts, W), _full_spec((1, QL)), _full_spec((1, KVL)), _full_spec((1, LANE))],
        out_shape=[jax.ShapeDtypeStruct((S, W), BF16), jax.ShapeDtypeStruct((1, QL), F32),
                   jax.ShapeDtypeStruct((1, KVL), F32), jax.ShapeDtypeStruct((1, LANE), F32)],
        compiler_params=_params(("arbitrary",)),
    )(small, dcqn, dckvn, dkr_heads, dlogf, q_norm, kv_norm, bias_pad, kc, ksa, ksb)


def _sigmoid(z):
    return 1.0 / (1.0 + jnp.exp(-z))


def _final(h, g, target, name):
    S, D = h.shape
    ts = _tile(S, ROW_T, 8)

    def body(h_ref, g_ref, t_ref, dh_ref, dhb_ref, dg_ref, loss_ref):
        hv = h_ref[...]
        gv = g_ref[...]
        err = (hv * _rms(hv)) * gv - t_ref[...]
        dh, dg_rows = _rms_bwd(hv, err / D, gv)
        dh_ref[...] = dh
        dhb_ref[...] = dh.astype(BF16)

        @pl.when(pl.program_id(0) == 0)
        def _():
            dg_ref[...] = jnp.zeros_like(dg_ref)
            loss_ref[...] = jnp.zeros_like(loss_ref)

        dg_ref[...] += jnp.sum(dg_rows, axis=0, keepdims=True)
        row_loss = jnp.mean(err * err, axis=-1, keepdims=True)
        loss_ref[...] += 0.5 * jnp.sum(row_loss, axis=0, keepdims=True)

    return pl.pallas_call(
        body, name=name, grid=(S // ts,),
        in_specs=[_row_spec(ts, D), _full_spec((1, D)), _row_spec(ts, D)],
        out_specs=[_row_spec(ts, D), _row_spec(ts, D), _full_spec((1, D)), _full_spec((1, LANE))],
        out_shape=[jax.ShapeDtypeStruct((S, D), F32), jax.ShapeDtypeStruct((S, D), BF16),
                   jax.ShapeDtypeStruct((1, D), F32), jax.ShapeDtypeStruct((1, LANE), F32)],
        compiler_params=_params(("arbitrary",)),
    )(h, g, target)


def _suffix_sum_rows(x, name):
    R, S = x.shape
    tb = _tile(S, 512)
    nb = S // tb
    tri = (lax.broadcasted_iota(jnp.int32, (tb, tb), 0) >= lax.broadcasted_iota(jnp.int32, (tb, tb), 1)).astype(BF16)

    def body(x_ref, tri_ref, o_ref, carry_ref):
        @pl.when(pl.program_id(0) == 0)
        def _():
            carry_ref[...] = jnp.zeros_like(carry_ref)

        xv = x_ref[...]
        t = tri_ref[...]
        acc = jnp.broadcast_to(carry_ref[:, 0:1], xv.shape)
        for part in _split3(xv):
            acc = acc + jnp.dot(part, t, preferred_element_type=F32)
        o_ref[...] = acc
        carry_ref[...] = jnp.broadcast_to(acc[:, 0:1], carry_ref.shape)

    rev = pl.BlockSpec((R, tb), lambda i: (0, nb - 1 - i))
    return pl.pallas_call(
        body, name=name, grid=(nb,),
        in_specs=[rev, _full_spec((tb, tb))], out_specs=rev,
        out_shape=jax.ShapeDtypeStruct((R, S), F32),
        scratch_shapes=[pltpu.VMEM((R, LANE), F32)],
        compiler_params=_params(("arbitrary",)),
    )(x, tri)


def _pairs(nb, by_key):
    if by_key:
        pr = [(i, j) for j in range(nb) for i in range(j, nb)]
    else:
        pr = [(i, j) for i in range(nb) for j in range(i + 1)]
    return (jnp.asarray([p[0] for p in pr], jnp.int32), jnp.asarray([p[1] for p in pr], jnp.int32), len(pr))


class _AttT:
    def __init__(self, S, n_heads, q, ks, v, scale, chunk_causal, cum_rep=None):
        self.S, self.H, self.q, self.ks, self.v = S, n_heads, q, ks, v
        self.scale, self.chunk_causal, self.cum_rep = scale, chunk_causal, cum_rep
        self.T = _tile(S, ATT_T)
        self.qs = min(QSUB, self.T)
        self.nb = S // self.T
        self.dq, self.dv = q[1], v[1]
        self.has_bias = cum_rep is not None

    def q_spec(self, op):
        _, w, off, per_head = op
        return pl.BlockSpec((self.T, w), lambda h, p, it, jt: (it[p], off + (h if per_head else 0)))

    def k_spec(self, op):
        _, w, off, per_head = op
        return pl.BlockSpec((self.T, w), lambda h, p, it, jt: (jt[p], off + (h if per_head else 0)))

    def row_q(self):
        return pl.BlockSpec((None, 1, self.T), lambda h, p, it, jt: (h, 0, it[p]))

    def cum_k(self):
        return pl.BlockSpec((None, self.T, self.qs), lambda h, p, it, jt: (h, jt[p], 0))

    def sub_blocks(self, masked):
        return [(q0, min(self.T, q0 + self.qs) if masked else self.T) for q0 in range(0, self.T, self.qs)]

    def scores(self, k, q_sub, cum, q0, masked):
        s = lax.dot_general(k, q_sub, _NT, preferred_element_type=F32)
        if self.has_bias:
            s = s - cum
        mask = None
        if masked:
            r = lax.broadcasted_iota(jnp.int32, s.shape, 0)
            c = lax.broadcasted_iota(jnp.int32, s.shape, 1) + q0
            mask = (r // CHUNK <= c // CHUNK) if self.chunk_causal else (r <= c)
        return s, mask


def _join(k_refs):
    return k_refs[0][...] if len(k_refs) == 1 else jnp.concatenate([r[...] for r in k_refs], axis=-1)


def _att_fwd_t(att, name, exact=False):
    S, H, T, qs = att.S, att.H, att.T, att.qs
    it, jt, npairs = _pairs(att.nb, by_key=False)
    nk = len(att.ks)

    def body(it_ref, jt_ref, *refs):
        q_ref = refs[0]
        k_refs = refs[1:1 + nk]
        v_ref = refs[1 + nk]
        n = 2 + nk
        cum_ref = None
        if att.has_bias:
            cum_ref = refs[n]
            n += 1
        o_ref = refs[n]
        n += 1
        ox_ref = None
        if exact:
            ox_ref = refs[n]
            n += 1
        lse_ref, m_ref, l_ref, acc_ref = refs[n:n + 4]
        lo_ref = refs[n + 4] if exact else None
        p = pl.program_id(1)
        i, j = it_ref[p], jt_ref[p]

        @pl.when(j == 0)
        def _():
            m_ref[...] = jnp.full_like(m_ref, -jnp.inf)
            l_ref[...] = jnp.zeros_like(l_ref)
            acc_ref[...] = jnp.zeros_like(acc_ref)
            if exact:
                lo_ref[...] = jnp.zeros_like(lo_ref)

        def step(masked):
            k = _join(k_refs)
            v = v_ref[...]
            subs = att.sub_blocks(masked)

            def logits(idx):
                q0, nkeys = subs[idx]
                cum = cum_ref[0:nkeys, :] if att.has_bias else None
                return att.scores(k[0:nkeys], q_ref[q0:q0 + qs, :], cum, q0, masked)

            ahead = logits(0)
            for idx, (q0, nkeys) in enumerate(subs):
                qsl = slice(q0, q0 + qs)
                s, mask = ahead
                if idx + 1 < len(subs):
                    ahead = logits(idx + 1)
                if masked:
                    s = jnp.where(mask, s, -jnp.inf)
                m_prev = m_ref[:, qsl]
                m_new = jnp.maximum(m_prev, jnp.max(s, axis=0, keepdims=True))
                alpha = jnp.exp2(m_prev - m_new)
                pr = jnp.exp2(s - m_new)
                l_ref[:, qsl] = alpha * l_ref[:, qsl] + jnp.sum(pr, axis=0, keepdims=True)
                p_hi = pr.astype(BF16)
                acc_ref[:, qsl] = alpha * acc_ref[:, qsl] + lax.dot_general(
                    v[0:nkeys], p_hi, _TN, preferred_element_type=F32)
                if exact:
                    p_lo = (pr - p_hi.astype(F32)).astype(BF16)
                    lo_ref[:, qsl] = alpha * lo_ref[:, qsl] + lax.dot_general(
                        v[0:nkeys], p_lo, _TN, preferred_element_type=F32)
                m_ref[:, qsl] = m_new

        @pl.when(j < i)
        def _():
            step(False)

        @pl.when(j == i)
        def _():
            step(True)
            l = l_ref[...]
            inv = 1.0 / l
            o_ref[...] = jnp.transpose(acc_ref[...] * inv).astype(o_ref.dtype)
            if exact:
                ox_ref[...] = jnp.transpose((acc_ref[...] + lo_ref[...]) * inv)
            lse_ref[...] = m_ref[...] + jnp.log2(l)

    in_specs = [att.q_spec(att.q)] + [att.k_spec(k) for k in att.ks] + [att.k_spec(att.v)]
    args = [att.q[0]] + [k[0] for k in att.ks] + [att.v[0]]
    if att.has_bias:
        in_specs.append(att.cum_k())
        args.append(att.cum_rep)
    o_spec = pl.BlockSpec((T, att.dv), lambda h, p, it, jt: (it[p], h))
    out_specs = [o_spec]
    out_shape = [jax.ShapeDtypeStruct((S, H * att.dv), BF16)]
    scratch = [pltpu.VMEM((1, T), F32), pltpu.VMEM((1, T), F32), pltpu.VMEM((att.dv, T), F32)]
    if exact:
        out_specs.append(o_spec)
        out_shape.append(jax.ShapeDtypeStruct((S, H * att.dv), F32))
        scratch.append(pltpu.VMEM((att.dv, T), F32))
    out_specs.append(att.row_q())
    out_shape.append(jax.ShapeDtypeStruct((H, 1, S), F32))
    return pl.pallas_call(
        body, name=name,
        grid_spec=pltpu.PrefetchScalarGridSpec(
            num_scalar_prefetch=2, grid=(H, npairs), in_specs=in_specs, out_specs=out_specs,
            scratch_shapes=scratch),
        out_shape=out_shape,
        compiler_params=_params(("parallel", "arbitrary")),
    )(it, jt, *args)


def _att_bwd_t(att, do, lse, o, dq_dtype, dk_dtypes, name, dq_rope=None, order=None):
    S, H, T, qs = att.S, att.H, att.T, att.qs
    it, jt, npairs = _pairs(att.nb, by_key=True)
    nk = len(att.ks)
    last = att.nb - 1
    widths = [k[1] for k in att.ks]

    def body(it_ref, jt_ref, *refs):
        q_ref = refs[0]
        k_refs = refs[1:1 + nk]
        v_ref, do_ref, lse_ref, o_ref = refs[1 + nk:5 + nk]
        n = 5 + nk
        cum_ref = None
        if att.has_bias:
            cum_ref = refs[n]
            n += 1
        rope_refs = None
        if dq_rope is not None:
            rope_refs = refs[n:n + 3]
            n += 3
        if order is not None:
            n += 1
        dl_acc = refs[-1]
        dq_ref = refs[n]
        dk_refs = refs[n + 1:n + 1 + nk]
        dv_ref = refs[n + 1 + nk]
        n += nk + 2
        dc_ref = None
        if att.has_bias:
            dc_ref = refs[n]
            n += 1
        dq_acc, dk_acc, dv_acc = refs[n:n + 3]
        dc_acc = refs[n + 3] if att.has_bias else None
        p = pl.program_id(1)
        i, j = it_ref[p], jt_ref[p]

        @pl.when(p == 0)
        def _():
            dq_acc[...] = jnp.zeros_like(dq_acc)

        @pl.when(i == j)
        def _():
            dk_acc[...] = jnp.zeros_like(dk_acc)
            dv_acc[...] = jnp.zeros_like(dv_acc)
            if att.has_bias:
                dc_acc[...] = jnp.zeros_like(dc_acc)

        @pl.when(j == 0)
        def _():
            prod = do_ref[...].astype(F32) * o_ref[...].astype(F32)
            ones = jnp.ones((8, att.dv), BF16)
            rows = jnp.zeros((8, T), F32)
            for part in _split3(prod):
                rows = rows + lax.dot_general(ones, part, _NT, preferred_element_type=F32)
            dl_acc[i] = rows[0:1, :]

        def step(masked):
            k = _join(k_refs)
            v = v_ref[...]
            dl = dl_acc[i]
            subs = att.sub_blocks(masked)

            def logits(idx):
                q0, nkeys = subs[idx]
                cum = cum_ref[0:nkeys, :] if att.has_bias else None
                return att.scores(k[0:nkeys], q_ref[q0:q0 + qs, :], cum, q0, masked)

            ahead = logits(0)
            for idx, (q0, nkeys) in enumerate(subs):
                qsl = slice(q0, q0 + qs)
                ksl = slice(0, nkeys)
                q_sub = q_ref[qsl, :]
                do_sub = do_ref[qsl, :]
                s, mask = ahead
                if idx + 1 < len(subs):
                    ahead = logits(idx + 1)
                pr = jnp.exp2(s - lse_ref[:, qsl])
                if masked:
                    pr = jnp.where(mask, pr, 0.0)
                dp = lax.dot_general(v[ksl], do_sub, _NT, preferred_element_type=F32)
                ds = pr * (dp - dl[:, qsl])
                ds_b = ds.astype(BF16)
                dv_acc[ksl, :] += jnp.dot(pr.astype(BF16), do_sub, preferred_element_type=F32)
                dk_acc[ksl, :] += jnp.dot(ds_b, q_sub, preferred_element_type=F32)
                dq_acc[i, :, qsl] += lax.dot_general(k[ksl], ds_b, _TN, preferred_element_type=F32)
                if att.has_bias:
                    part = ds[:, 0:LANE] if qs >= LANE else ds
                    for c0 in range(LANE, qs, LANE):
                        part = part + ds[:, c0:c0 + LANE]
                    dc_acc[ksl, :] += part

        @pl.when(i > j)
        def _():
            step(False)

        @pl.when(i == j)
        def _():
            step(True)
            dq = jnp.transpose(dq_acc[i] * att.scale)
            if dq_rope is not None:
                dq = _rope(dq, rope_refs[0][...], rope_refs[1][...], rope_refs[2][...], -1)
            dq_ref[...] = dq.astype(dq_ref.dtype)

        @pl.when(i == last)
        def _():
            dk = dk_acc[...] * (1.0 / LOG2E)
            off = 0
            for r, w in zip(dk_refs, widths):
                r[...] = dk[:, off:off + w].astype(r.dtype)
                off += w
            dv_ref[...] = dv_acc[...].astype(dv_ref.dtype)
            if att.has_bias:
                dc_ref[...] = -jnp.sum(dc_acc[...], axis=-1, keepdims=True)

    do_op = (do, att.dv, 0, True)
    o_spec = pl.BlockSpec((T, att.dv), lambda h, p, it, jt: (jnp.where(jt[p] == 0, it[p], last), h))
    in_specs = ([att.q_spec(att.q)] + [att.k_spec(k) for k in att.ks]
                + [att.k_spec(att.v), att.q_spec(do_op), att.row_q(), o_spec])
    args = [att.q[0]] + [k[0] for k in att.ks] + [att.v[0], do, lse, o]
    if att.has_bias:
        in_specs.append(att.cum_k())
        args.append(att.cum_rep)
    if dq_rope is not None:
        in_specs += [pl.BlockSpec((T, att.dq), lambda h, p, it, jt: (jt[p], 0))] * 3
        args += list(dq_rope)
    if order is not None:
        in_specs.append(_ANY_SPEC)
        args.append(order)
    out_specs = [pl.BlockSpec((T, att.dq), lambda h, p, it, jt: (jt[p], h))]
    out_shape = [jax.ShapeDtypeStruct((S, H * att.dq), dq_dtype)]
    out_specs += [pl.BlockSpec((T, w), lambda h, p, it, jt: (jt[p], h)) for w in widths]
    out_shape += [jax.ShapeDtypeStruct((S, H * w), dt) for w, dt in zip(widths, dk_dtypes)]
    out_specs.append(pl.BlockSpec((T, att.dv), lambda h, p, it, jt: (jt[p], h)))
    out_shape.append(jax.ShapeDtypeStruct((S, H * att.dv), BF16))
    scratch = [pltpu.VMEM((att.nb, att.dq, T), F32), pltpu.VMEM((T, att.dq), F32), pltpu.VMEM((T, att.dv), F32)]
    if att.has_bias:
        out_specs.append(pl.BlockSpec((None, T, 1), lambda h, p, it, jt: (h, jt[p], 0)))
        out_shape.append(jax.ShapeDtypeStruct((H, S, 1), F32))
        scratch.append(pltpu.VMEM((T, min(qs, LANE)), F32))
    scratch.append(pltpu.VMEM((att.nb, 1, T), F32))
    return pl.pallas_call(
        body, name=name,
        grid_spec=pltpu.PrefetchScalarGridSpec(
            num_scalar_prefetch=2, grid=(H, npairs), in_specs=in_specs, out_specs=out_specs,
            scratch_shapes=scratch),
        out_shape=out_shape,
        compiler_params=_params(("parallel", "arbitrary")),
    )(it, jt, *args)


def _adamw(w, g1, g2, m, v, name, g_row=None):
    _, K, N = w.shape
    by_rows = K % 8 == 0
    tr = _tile(K, 256, 8) if by_rows else K
    if g_row is None:
        assert g1.shape == (K, N) and g2.shape == (K, N), name
        g_row = 0
    assert by_rows and g_row % tr == 0 or g_row == 0, name
    g_blk = g_row // tr
    tc = N if by_rows else _tile(N, LANE)
    c1 = 1.0 - ADAM_B1 ** ADAM_STEP
    c2 = 1.0 - ADAM_B2 ** ADAM_STEP

    def body(w_ref, g1_ref, g2_ref, m_ref, v_ref, g_ref, d_ref, nm_ref, nv_ref):
        gv = g1_ref[...] + g2_ref[...]
        nm = ADAM_B1 * m_ref[...] + (1.0 - ADAM_B1) * gv
        nv = ADAM_B2 * v_ref[...] + (1.0 - ADAM_B2) * (gv * gv)
        g_ref[...] = gv
        d_ref[...] = -ADAM_LR * ((nm / c1) / (jnp.sqrt(nv / c2) + ADAM_EPS) + ADAM_WD * w_ref[...])
        nm_ref[...] = nm
        nv_ref[...] = nv

    if by_rows:
        blk = pl.BlockSpec((None, tr, N), lambda i: (0, i, 0))
        gblk = pl.BlockSpec((tr, N), lambda i: (g_blk + i, 0))
    else:
        blk = pl.BlockSpec((None, K, tc), lambda i: (0, 0, i))
        gblk = pl.BlockSpec((K, tc), lambda i: (0, i))
    return pl.pallas_call(
        body, name=name, grid=(K // tr if by_rows else N // tc,),
        in_specs=[blk, gblk, gblk, blk, blk], out_specs=[blk] * 4,
        out_shape=[jax.ShapeDtypeStruct((1, K, N), F32)] * 4,
        compiler_params=_params(("parallel",)),
    )(w, g1, g2, m, v)


_HBM_SPEC = pl.BlockSpec(memory_space=pltpu.HBM)
_SEM_SPEC = pl.BlockSpec(memory_space=pltpu.SEMAPHORE)
_VMEM_SPEC = pl.BlockSpec(memory_space=pltpu.VMEM)
_EFFECT = pltpu.SideEffectType.DATAFLOW_SIDE_EFFECTING


def _place():
    return lax.axis_index("x"), lax.axis_index("y"), lax.axis_index("c")


def _other_chips(x, y):
    return [(1 - x, y), (x, 1 - y), (1 - x, 1 - y)]


def _chip_copies(src_ref, land_ref, sems, gather):
    x, y, c = _place()
    me = 2 * x + y
    out, back = [], []
    if gather == "half":
        half = src_ref.shape[0] // 2
        mine = pl.ds(pl.multiple_of(c * half, 16), half)
    for n, (px, py) in enumerate(_other_chips(x, y)):
        if gather == "half":
            src, there, here = src_ref.at[mine], land_ref.at[me, mine], land_ref.at[2 * px + py, mine]
        elif gather:
            src, there, here = src_ref, land_ref.at[me], land_ref.at[2 * px + py]
        else:
            src, there, here = src_ref.at[2 * px + py], land_ref.at[n], land_ref.at[n]
        out.append(pltpu.make_async_remote_copy(
            src_ref=src, dst_ref=there, send_sem=sems[n], recv_sem=sems[3 + n],
            device_id=(px, py, c), device_id_type=MESH))
        back.append(pltpu.make_async_remote_copy(
            src_ref=src, dst_ref=here, send_sem=sems[n], recv_sem=sems[3 + n],
            device_id=(px, py, c), device_id_type=MESH))
    return out, back


def _xchg_start(src, land, gather, order, name):
    def body(src_ref, land_ref, order_ref, *outs):
        sems = outs[0:6]
        token = outs[8]
        out, _ = _chip_copies(src_ref, land_ref, sems, gather)
        for cp in out:
            cp.start()
        token[...] = jnp.zeros_like(token)

    outs = pl.pallas_call(
        body, name=name,
        out_shape=(pltpu.SemaphoreType.DMA(()),) * 6 + (
            pltpu.HBM(src.shape, src.dtype), pltpu.HBM(land.shape, land.dtype),
            jax.ShapeDtypeStruct((8, LANE), F32)),
        in_specs=(_HBM_SPEC, _HBM_SPEC, _ANY_SPEC),
        out_specs=(_SEM_SPEC,) * 6 + (_HBM_SPEC, _HBM_SPEC, _VMEM_SPEC),
        input_output_aliases={0: 6, 1: 7},
        compiler_params=pltpu.CompilerParams(has_side_effects=_EFFECT),
    )(pltpu.with_memory_space_constraint(src, pltpu.HBM), pltpu.with_memory_space_constraint(land, pltpu.HBM), order)
    return outs[0:6], outs[6], outs[7], outs[8]


def _xchg_wait(started, gather, after, name):
    sems, src, land, _ = started
    after = after if isinstance(after, tuple) else (after,)

    def body(src_ref, land_ref, *rest):
        _, back = _chip_copies(src_ref, land_ref, rest[0:6], gather)
        for cp in back:
            cp.wait_send()
            cp.wait_recv()

    return pl.pallas_call(
        body, name=name,
        out_shape=(pltpu.HBM(src.shape, src.dtype), pltpu.HBM(land.shape, land.dtype)),
        in_specs=(_HBM_SPEC, _HBM_SPEC) + (_SEM_SPEC,) * 6 + (_ANY_SPEC,) * len(after),
        out_specs=(_HBM_SPEC, _HBM_SPEC),
        input_output_aliases={0: 0, 1: 1},
        compiler_params=pltpu.CompilerParams(has_side_effects=_EFFECT),
    )(src, land, *sems, *after)


def _forward_halves(land, name):
    _, R, C = land.shape
    half = R // 2
    assert half % 16 == 0

    def body(land_ref, out_ref, send_sems, recv_sems):
        x, y, c = _place()
        mine = pl.ds(pl.multiple_of(c * half, 16), half)
        theirs = pl.ds(pl.multiple_of((1 - c) * half, 16), half)
        sends = []
        for n, (px, py) in enumerate(_other_chips(x, y)):
            cp = pltpu.make_async_remote_copy(
                src_ref=land_ref.at[2 * px + py, mine], dst_ref=out_ref.at[2 * px + py, mine],
                send_sem=send_sems.at[n], recv_sem=recv_sems.at[n], device_id=(x, y, 1 - c), device_id_type=MESH)
            cp.start()
            sends.append(cp)
        for n, (px, py) in enumerate(_other_chips(x, y)):
            pltpu.make_async_remote_copy(
                src_ref=land_ref.at[2 * px + py, theirs], dst_ref=out_ref.at[2 * px + py, theirs],
                send_sem=send_sems.at[n], recv_sem=recv_sems.at[n], device_id=(x, y, 1 - c),
                device_id_type=MESH).wait_recv()
        for cp in sends:
            cp.wait_send()

    return pl.pallas_call(
        body, name=name,
        in_specs=[_ANY_SPEC], out_specs=_ANY_SPEC,
        out_shape=jax.ShapeDtypeStruct(land.shape, land.dtype),
        input_output_aliases={0: 0},
        scratch_shapes=[pltpu.SemaphoreType.DMA((3,)), pltpu.SemaphoreType.DMA((3,))],
    )(land)


def _sib_copy(src_ref, land_ref, send_sem, recv_sem):
    x, y, c = _place()
    return pltpu.make_async_remote_copy(src_ref=src_ref, dst_ref=land_ref, send_sem=send_sem, recv_sem=recv_sem,
                                        device_id=(x, y, 1 - c), device_id_type=MESH)


def _sib_start(src, name):
    land = lax.empty(src.shape, src.dtype)

    def body(src_ref, land_ref, send_sem, recv_sem, src_thru, land_thru, token):
        _sib_copy(src_ref, land_ref, send_sem, recv_sem).start()
        token[...] = jnp.zeros_like(token)

    return pl.pallas_call(
        body, name=name,
        out_shape=(pltpu.SemaphoreType.DMA(()), pltpu.SemaphoreType.DMA(()),
                   pltpu.HBM(src.shape, src.dtype), pltpu.HBM(land.shape, land.dtype),
                   jax.ShapeDtypeStruct((8, LANE), F32)),
        in_specs=(_HBM_SPEC, _HBM_SPEC),
        out_specs=(_SEM_SPEC, _SEM_SPEC, _HBM_SPEC, _HBM_SPEC, _VMEM_SPEC),
        input_output_aliases={0: 2, 1: 3},
        compiler_params=pltpu.CompilerParams(has_side_effects=_EFFECT),
    )(pltpu.with_memory_space_constraint(src, pltpu.HBM), pltpu.with_memory_space_constraint(land, pltpu.HBM))


def _sib_wait(started, after, name):
    send_sem, recv_sem, src, land, _ = started

    def body(src_ref, land_ref, send_sem, recv_sem, after_ref, src_out, land_out):
        cp = _sib_copy(src_ref, land_ref, send_sem, recv_sem)
        cp.wait_send()
        cp.wait_recv()

    return pl.pallas_call(
        body, name=name,
        out_shape=(pltpu.HBM(src.shape, src.dtype), pltpu.HBM(land.shape, land.dtype)),
        in_specs=(_HBM_SPEC, _HBM_SPEC, _SEM_SPEC, _SEM_SPEC, _ANY_SPEC),
        out_specs=(_HBM_SPEC, _HBM_SPEC),
        input_output_aliases={0: 0, 1: 1},
        compiler_params=pltpu.CompilerParams(has_side_effects=_EFFECT),
    )(src, land, send_sem, recv_sem, after)


def _sum_slabs(gp, recv, chip, name):
    _, R, C = gp.shape
    tr = _tile(R, PACK_ROWS, 16)

    def body(chip_ref, own_ref, r0_ref, r1_ref, r2_ref, o_ref):
        acc = own_ref[...].astype(F32) + r0_ref[...].astype(F32)
        o_ref[...] = (acc + r1_ref[...].astype(F32)) + r2_ref[...].astype(F32)

    def got(n):
        return pl.BlockSpec((None, tr, C), lambda i, chip_ref: (n, i, 0))

    return pl.pallas_call(
        body, name=name,
        grid_spec=pltpu.PrefetchScalarGridSpec(
            num_scalar_prefetch=1, grid=(R // tr,),
            in_specs=[pl.BlockSpec((None, tr, C), lambda i, chip_ref: (chip_ref[0], i, 0)), got(0), got(1), got(2)],
            out_specs=pl.BlockSpec((tr, C), lambda i, chip_ref: (i, 0))),
        out_shape=jax.ShapeDtypeStruct((R, C), F32),
        compiler_params=_params(("parallel",)),
    )(jnp.reshape(chip, (1,)).astype(jnp.int32), gp, recv, recv, recv)


def _all_reduce_vec(vec, name):
    VR, W = vec.shape

    def body(vec_ref, vall_ref, vout_ref, vsend_sems, vrecv_sems):
        x, y, c = _place()
        vall_ref[4 * x + 2 * y + c] = vec_ref[...]
        sends = []
        peers = []
        for r in range(1, N_DEV):
            dx, dy, dc = (r >> 2) & 1, (r >> 1) & 1, r & 1
            peer = (x ^ dx, y ^ dy, c ^ dc)
            peers.append(peer)
            cp = pltpu.make_async_remote_copy(
                src_ref=vec_ref, dst_ref=vall_ref.at[4 * x + 2 * y + c], send_sem=vsend_sems.at[r - 1],
                recv_sem=vrecv_sems.at[r - 1], device_id=peer, device_id_type=MESH)
            cp.start()
            sends.append(cp)
        for r, peer in enumerate(peers):
            pltpu.make_async_remote_copy(
                src_ref=vec_ref, dst_ref=vall_ref.at[4 * peer[0] + 2 * peer[1] + peer[2]],
                send_sem=vsend_sems.at[r], recv_sem=vrecv_sems.at[r],
                device_id=peer, device_id_type=MESH).wait_recv()
        total = vall_ref[0]
        for d in range(1, N_DEV):
            total = total + vall_ref[d]
        vout_ref[...] = total
        for cp in sends:
            cp.wait_send()

    outs = pl.pallas_call(
        body, name=name,
        in_specs=[_VMEM_SPEC], out_specs=[_VMEM_SPEC, _VMEM_SPEC],
        out_shape=[jax.ShapeDtypeStruct((N_DEV, VR, W), F32), jax.ShapeDtypeStruct((VR, W), F32)],
        scratch_shapes=[pltpu.SemaphoreType.DMA((N_DEV - 1,)), pltpu.SemaphoreType.DMA((N_DEV - 1,))],
    )(vec)
    return outs[1]


class _Pack:
    def __init__(self, group, C):
        self.group, self.C = group, C
        self.rows, self.offs, off = {}, {}, 0
        for nm, (K, N), _ in group:
            assert N <= C, nm
            self.rows[nm] = K if 2 * N > C else -(-(K * N) // C)
            self.offs[nm] = off
            off += -(-self.rows[nm] // 16) * 16
        self.used = off
        self.R = -(-off // PACK_ROWS) * PACK_ROWS

    def _rows_of(self, a):
        K, N = a.shape
        if 2 * N > self.C:
            a = jnp.pad(a, ((0, 0), (0, self.C - N)))
        else:
            a = jnp.pad(a.reshape(-1), (0, -(K * N) % self.C)).reshape(-1, self.C)
        return jnp.pad(a, ((0, -a.shape[0] % 16), (0, 0)))

    def pack(self, shards):
        parts = [self._rows_of(shards[nm].astype(BF16)) for nm, _, _ in self.group]
        return jnp.concatenate(parts + [jnp.zeros((self.R - self.used, self.C), BF16)], axis=0)

    def _shard_of(self, rows, shape):
        K, N = shape
        return rows[:, :N] if 2 * N > self.C else rows.reshape(-1)[:K * N].reshape(K, N)

    def part(self, flat, nm, shape):
        return self._shard_of(flat[self.offs[nm]:self.offs[nm] + self.rows[nm]], shape)

    def slab_rows(self, nm, g):
        (K, N), axis = next((shape, axis) for n, shape, axis in self.group if n == nm)
        cuts = [g[:, k * N:(k + 1) * N] if axis == 1 else g[k * K:(k + 1) * K, :] for k in range(N_CHIPS)]
        return jnp.stack([self._rows_of(c.astype(BF16)) for c in cuts])

    def slabs(self, grads):
        parts = [self.slab_rows(nm, grads[nm]) for nm, _, _ in self.group]
        return jnp.concatenate(parts + [jnp.zeros((N_CHIPS, self.R - self.used, self.C), BF16)], axis=1)

    def full(self, gathered, names=None):
        res = {}
        for nm, (K, N), axis in self.group:
            if names is None or nm in names:
                rows = gathered[:, self.offs[nm]:self.offs[nm] + self.rows[nm]]
                res[nm] = jnp.concatenate([self._shard_of(rows[k], (K, N)) for k in range(N_CHIPS)], axis=axis)
        return res


def _rope_tables(S):
    pos = jnp.arange(S, dtype=F32)
    inv = 1.0 / (ROPE_THETA ** (jnp.arange(0, MLA_ROPE, 2, dtype=F32) / MLA_ROPE))
    ang = pos[:, None] * inv[None, :]
    cos, sin = jnp.cos(ang), jnp.sin(ang)
    half = MLA_ROPE // 2
    z = jnp.zeros((S, half), F32)
    one = jnp.ones((S, LANE - MLA_ROPE), F32)
    zero = jnp.zeros((S, LANE - MLA_ROPE), F32)
    kc = jnp.concatenate([cos, cos, one], axis=1)
    ksa = jnp.concatenate([-sin, z, zero], axis=1)
    ksb = jnp.concatenate([z, sin, zero], axis=1)
    qc = jnp.concatenate([jnp.ones((S, MLA_NOPE), F32), kc], axis=1)
    qsa = jnp.concatenate([jnp.zeros((S, MLA_NOPE), F32), ksa], axis=1)
    qsb = jnp.concatenate([jnp.zeros((S, MLA_NOPE), F32), ksb], axis=1)
    return (kc, ksa, ksb), (qc, qsa, qsb)


def _pad_cols(a, width):
    return jnp.pad(a, ((0, 0), (0, width - a.shape[1])))


def kernel(x, attn_norm, w_in, fox_f_bias, q_norm, w_uq, kv_norm, w_ukv, w_mla_branch, w_fox_branch, w_out, mlp_norm, w_up, w_down, final_norm, loss_target, m_attn_norm, m_w_in, m_fox_f_bias, m_q_norm, m_w_uq, m_kv_norm, m_w_ukv, m_w_mla_branch, m_w_fox_branch, m_w_out, m_mlp_norm, m_w_up, m_w_down, m_final_norm, v_attn_norm, v_w_in, v_fox_f_bias, v_q_norm, v_w_uq, v_kv_norm, v_w_ukv, v_w_mla_branch, v_w_fox_branch, v_w_out, v_mlp_norm, v_w_up, v_w_down, v_final_norm):
    _, S, D = x.shape
    H, HF = MLA_HEADS, FOX_HEADS
    QL, KVL = MLA_Q_LORA, MLA_KV_LORA
    assert H == HF and H <= 8
    xs = x[0]
    target = loss_target[0]
    C = D
    chip = 2 * lax.axis_index("x") + lax.axis_index("y")

    def flip(a):
        return jnp.transpose(a, (0, 2, 1))

    w_in, m_w_in, v_w_in = flip(w_in), flip(m_w_in), flip(v_w_in)
    weights = {"attn_norm": attn_norm, "w_in": w_in, "fox_f_bias": fox_f_bias, "q_norm": q_norm, "w_uq": w_uq,
               "kv_norm": kv_norm, "w_ukv": w_ukv, "w_mla_branch": w_mla_branch, "w_fox_branch": w_fox_branch,
               "w_out": w_out, "mlp_norm": mlp_norm, "w_up": w_up, "w_down": w_down, "final_norm": final_norm}
    moments = {"attn_norm": (m_attn_norm, v_attn_norm), "w_in": (m_w_in, v_w_in), "fox_f_bias": (m_fox_f_bias, v_fox_f_bias),
               "q_norm": (m_q_norm, v_q_norm), "w_uq": (m_w_uq, v_w_uq), "kv_norm": (m_kv_norm, v_kv_norm),
               "w_ukv": (m_w_ukv, v_w_ukv), "w_mla_branch": (m_w_mla_branch, v_w_mla_branch),
               "w_fox_branch": (m_w_fox_branch, v_w_fox_branch), "w_out": (m_w_out, v_w_out),
               "mlp_norm": (m_mlp_norm, v_mlp_norm), "w_up": (m_w_up, v_w_up), "w_down": (m_w_down, v_w_down),
               "final_norm": (m_final_norm, v_final_norm)}

    def group(names_axes):
        return [(nm, weights[nm].shape[1:], axis) for nm, axis in names_axes]

    pack_a = _Pack(group([("w_in", 0), ("w_uq", 1), ("w_ukv", 1)]), C)
    pack_b = _Pack(group([("w_down", 0), ("w_up", 1), ("w_out", 0), ("w_mla_branch", 1), ("w_fox_branch", 1)]), C)
    RA, RB = pack_a.R, pack_b.R
    wp_a = pack_a.pack({nm: weights[nm][0] for nm, _, _ in pack_a.group})
    wp_b = pack_b.pack({nm: weights[nm][0] for nm, _, _ in pack_b.group})
    n_in = w_in.shape[1]
    rows_in = -(-n_in // 16) * 16
    assert pack_a.offs["w_in"] == 0 and all((k * n_in) % 16 + n_in <= rows_in for k in range(N_CHIPS))
    shifted = lax.dynamic_update_slice(jnp.zeros((rows_in, C), BF16), wp_a[:n_in], ((chip * n_in) % 16, 0))
    wp_a = jnp.concatenate([shifted, wp_a[rows_in:]], axis=0)
    ag_a = _xchg_start(wp_a, lax.empty((N_CHIPS, RA, C), BF16), "half", jnp.zeros((8, LANE), F32), "all_gather_start_a")
    xn = _norm_fwd(xs, attn_norm, "attn_norm_fwd", order=ag_a[3])
    own_a, land_a = _xchg_wait(ag_a, "half", (xn, wp_b), "all_gather_wait_a")
    land_a = _forward_halves(land_a, "all_gather_forward_a")
    gathered_a = lax.dynamic_update_slice(land_a, own_a[None], (chip, 0, 0))
    ag_b = _xchg_start(wp_b, lax.empty((N_CHIPS, RB, C), BF16), True, gathered_a, "all_gather_start_b")
    full = pack_a.full(gathered_a, ("w_uq", "w_ukv"))
    tile0 = [(k * n_in) // 16 * 16 for k in range(N_CHIPS)]
    total = tile0[-1] + rows_in
    full["w_in"] = sum(jnp.pad(gathered_a[k, :rows_in], ((tile0[k], total - tile0[k] - rows_in), (0, 0)))
                       for k in range(N_CHIPS))

    o_ckv = QL
    o_kr = o_ckv + KVL
    o_fq = o_kr + MLA_ROPE
    o_ff = o_fq + 3 * HF * FOX_HEAD_DIM
    o_g = o_ff + HF
    wi = full["w_in"]
    assert N_CHIPS * n_in == o_g + 2 * D and wi.shape[0] >= o_g + 2 * D
    WS = QL + KVL + 2 * LANE
    NQKV = 3 * HF * FOX_HEAD_DIM

    def pad_rows(a, rows):
        return jnp.pad(a, ((0, rows - a.shape[0]), (0, 0)))

    w_small = jnp.concatenate([wi[:o_kr], pad_rows(wi[o_kr:o_fq], LANE), pad_rows(wi[o_ff:o_g], LANE)], axis=0)
    w_qkv = wi[o_fq:o_ff]
    w_g = wi[o_g:o_g + 2 * D]
    w_pack = jnp.concatenate([w_small, w_qkv, w_g], axis=0)
    dqk = MLA_NOPE + MLA_ROPE
    w_uq_p = jnp.pad(full["w_uq"].reshape(QL, H, dqk), ((0, 0), (0, 0), (0, QPAD - dqk))).reshape(QL, H * QPAD)
    ukv = full["w_ukv"].reshape(KVL, H, MLA_NOPE + MLA_V)
    w_ukv_p = jnp.concatenate([ukv[:, :, :MLA_NOPE].reshape(KVL, H * MLA_NOPE),
                               ukv[:, :, MLA_NOPE:].reshape(KVL, H * MLA_V)], axis=1)

    (kc, ksa, ksb), (qc, qsa, qsb) = _rope_tables(S)
    bias_pad = _pad_cols(fox_f_bias, LANE)

    small = _matmul(xn, w_small, "nt", [F32], "proj_small")
    n_fq = HF * FOX_HEAD_DIM
    q_scale = jnp.concatenate([jnp.full((1, n_fq), LOG2E / math.sqrt(FOX_HEAD_DIM), F32),
                               jnp.ones((1, NQKV - n_fq), F32)], axis=1)
    qkv = _matmul(xn, w_qkv, "nt", [BF16], "proj_qkv", col_extras=(q_scale,), epilogue=lambda acc, cs: (acc * cs,))
    gpre = _matmul(xn, w_g, "nt", [F32], "proj_gates")
    cqn, ckvn, kr, cum = _prep_fwd(small, q_norm, kv_norm, bias_pad, kc, ksa, ksb, HF, "prep_fwd")
    c2_mla = LOG2E / math.sqrt(dqk)
    q_rot = _matmul(cqn, w_uq_p, "nn", [BF16], "mla_q_up", tn=QPAD, row_extras=(qc * c2_mla, qsa * c2_mla, qsb * c2_mla),
                    epilogue=lambda acc, c, sa, sb: (_rope(acc, c, sa, sb, 1),))
    kv2 = _matmul(ckvn, w_ukv_p, "nn", [BF16], "mla_kv_up")

    mla = _AttT(S, H, (q_rot, QPAD, 0, True), [(kv2, MLA_NOPE, 0, True), (kr, LANE, 0, False)],
                (kv2, MLA_V, H, True), 1.0 / math.sqrt(dqk), True)
    o_mla, lse_mla = _att_fwd_t(mla, "mla_att_fwd")

    cum_t = jnp.transpose(cum[:, :HF]) * LOG2E
    cum_rep = jnp.broadcast_to(cum_t[:, :, None], (HF, S, min(QSUB, _tile(S, ATT_T))))
    fox = _AttT(S, HF, (qkv, FOX_HEAD_DIM, 0, True), [(qkv, FOX_HEAD_DIM, HF, True)],
                (qkv, FOX_HEAD_DIM, 2 * HF, True), 1.0 / math.sqrt(FOX_HEAD_DIM), False, cum_rep)
    o_fox, ox_fox, lse_fox = _att_fwd_t(fox, "fox_att_fwd", exact=True)

    own_b, land_b = _xchg_wait(ag_b, True, (lse_fox, lse_mla, gpre), "all_gather_wait_b")
    gathered_b = lax.dynamic_update_slice(land_b, own_b[None], (chip, 0, 0))
    full.update(pack_b.full(gathered_b, ("w_mla_branch", "w_fox_branch", "w_out")))
    w_mb, w_fb, w_o = (full[n] for n in ("w_mla_branch", "w_fox_branch", "w_out"))

    def b_of(nm, mode, tn, tk):
        (K, N), axis = next((shape, axis) for n, shape, axis in pack_b.group if n == nm)
        off = pack_b.offs[nm]
        shape = (N_CHIPS * K, N) if axis == 0 else (K, N_CHIPS * N)
        t_r, t_c = (tk, tn) if mode == "nn" else (tn, tk)
        t_r, t_c = _tile(shape[0], t_r), _tile(shape[1], t_c)
        if not (N == C and K % t_r == 0 and N % t_c == 0 and off % t_r == 0):
            return pack_b.full(gathered_b, (nm,))[nm], None
        base = off // t_r
        if axis == 0:
            per = K // t_r
            place = lambda rb, cb: (rb // per, base + rb % per, cb)
        else:
            per = N // t_c
            place = lambda rb, cb: (cb // per, base + rb, cb % per)
        return gathered_b, (shape, (lambda j, k: place(k, j)) if mode == "nn" else (lambda j, k: place(j, k)))

    y_mla = _matmul(o_mla, w_mb, "nn", [F32], "mla_branch")

    def gate_merge(acc, ga, gb, ya):
        return acc, _sigmoid(ga) * ya + _sigmoid(gb) * acc

    y_fox, merged = _matmul(o_fox, w_fb, "nn", [F32, BF16], "fox_branch_gates", tn=512,
                            extras=((gpre, 0), (gpre, 1), y_mla), epilogue=gate_merge)
    h1 = _matmul(merged, w_o, "nn", [F32], "out_proj", extras=(xs,), epilogue=lambda acc, r: (acc + r,))
    hn = _norm_fwd(h1, mlp_norm, "mlp_norm_fwd")

    def relu2(acc):
        a = jnp.maximum(acc, 0.0)
        return a * a, a

    w_u, w_u_in = b_of("w_up", "nn", 1024, 2048)
    u, a_pos = _matmul(hn, w_u, "nn", [BF16, BF16], "mlp_up", epilogue=relu2, b_in=w_u_in)
    w_d, w_d_in = b_of("w_down", "nn", 1024, 2048)
    h2 = _matmul(u, w_d, "nn", [F32], "mlp_down", tn=1024, extras=(h1,), epilogue=lambda acc, r: (acc + r,),
                 b_in=w_d_in)
    dh2, dh2_b, g_final, loss_part = _final(h2, final_norm.reshape(1, D), target, "final_norm_loss")

    gp_b = lax.empty((N_CHIPS, RB, C), BF16)
    by_glue = {}

    def grad_b(nm, a, b, name):
        nonlocal gp_b
        (K, N), axis = next((shape, axis) for n, shape, axis in pack_b.group if n == nm)
        off = pack_b.offs[nm]
        tm = min(1024, K) if axis == 0 else min(1024, a.shape[1])
        tn = min(1024, N) if axis == 1 else min(1024, b.shape[1])
        if not (N == C and tm % LANE == 0 and tn % LANE == 0 and K % tm == 0 and N % tn == 0 and off % tm == 0):
            by_glue[nm] = _mm_tn(a, b, name)
            return
        base = off // tm
        if axis == 0:
            per = K // tm
            place = lambda i, j: (i // per, base + i % per, j)
        else:
            per = N // tn
            place = lambda i, j: (j // per, base + i, j % per)
        gp_b = _mm_tn(a, b, name, tm=tm, tn=tn, into=(gp_b, place))

    w_d, w_d_in = b_of("w_down", "nt", 1024, 2048)
    da = _matmul(dh2_b, w_d, "nt", [BF16], "mlp_down_dx", extras=(a_pos,),
                 epilogue=lambda acc, a: (acc * (2.0 * a.astype(F32)),), b_in=w_d_in)
    grad_b("w_down", u, dh2_b, "mlp_down_dw")
    w_u, w_u_in = b_of("w_up", "nt", 1024, 2048)
    dhn = _matmul(da, w_u, "nt", [F32], "mlp_up_dx", tn=1024, b_in=w_u_in)
    grad_b("w_up", hn, da, "mlp_up_dw")
    dh1, dh1_b, g_mlp_norm = _norm_bwd(h1, dhn, mlp_norm, dh2, "mlp_norm_bwd")

    def gate_bwd(acc, ga, gb, ya, yb):
        ga, gb = _sigmoid(ga), _sigmoid(gb)
        return acc * ga, acc * gb, acc * ya * (ga * (1.0 - ga)), acc * yb * (gb * (1.0 - gb))

    dy_mla, dy_fox, dg_mla, dg_fox = _matmul(dh1_b, w_o, "nt", [BF16] * 4, "out_proj_dx_gates", tn=512,
                                             extras=((gpre, 0), (gpre, 1), y_mla, y_fox), epilogue=gate_bwd)
    grad_b("w_out", merged, dh1_b, "out_proj_dw")
    do_mla = _matmul(dy_mla, w_mb, "nt", [BF16], "mla_branch_dx")
    grad_b("w_mla_branch", o_mla, dy_mla, "mla_branch_dw")
    do_fox = _matmul(dy_fox, w_fb, "nt", [BF16], "fox_branch_dx")
    grad_b("w_fox_branch", o_fox, dy_fox, "fox_branch_dw")
    for nm, g in by_glue.items():
        gp_b = lax.dynamic_update_slice(gp_b, pack_b.slab_rows(nm, g), (0, pack_b.offs[nm], 0))
    if RB > pack_b.used:
        gp_b = lax.dynamic_update_slice(gp_b, jnp.zeros((N_CHIPS, RB - pack_b.used, C), BF16), (0, pack_b.used, 0))

    rs_b = _xchg_start(gp_b, lax.empty((3, RB, C), BF16), False, do_fox, "grad_scatter_start_b")

    dq_rot, dk_nope, dkr_heads, dv_mla = _att_bwd_t(mla, do_mla, lse_mla, o_mla, BF16, [BF16, F32],
                                                    "mla_att_bwd", dq_rope=(qc, qsa, qsb), order=rs_b[3])
    dfq, dfk, dfv, dcum = _att_bwd_t(fox, do_fox, lse_fox, ox_fox, BF16, [BF16], "fox_att_bwd")

    gp_b_sent, recv_b = _xchg_wait(rs_b, False, (dfq, dq_rot), "grad_scatter_wait_b")
    swap_b = _sib_start(_sum_slabs(gp_b_sent, recv_b, chip, "grad_sum_b"), "grad_swap_start_b")

    dcqn = _matmul(dq_rot, w_uq_p, "nt", [F32], "mla_q_up_dx", order=swap_b[4])
    g_w_uq_p = _mm_tn(cqn, dq_rot, "mla_q_up_dw")
    dkv2 = jnp.concatenate([dk_nope, dv_mla], axis=1)
    dckvn = _matmul(dkv2, w_ukv_p, "nt", [F32], "mla_kv_up_dx")
    g_w_ukv_p = _mm_tn(ckvn, dkv2, "mla_kv_up_dw")

    dcum_rows = jnp.pad(dcum[:, :, 0], ((0, 8 - HF), (0, 0)))
    dlogf_rows = _suffix_sum_rows(dcum_rows, "fox_forget_suffix_sum")
    dlogf = _pad_cols(jnp.transpose(dlogf_rows[:HF]), LANE)
    d_small, g_q_norm, g_kv_norm, g_bias = _prep_bwd(
        small, dcqn, dckvn, dkr_heads, dlogf, q_norm, kv_norm, bias_pad, kc, ksa, ksb, H, "prep_bwd")
    dproj = [d_small, dfq, dfk, dfv, dg_mla, dg_fox]
    gs, gfq, gfk, gfv, gg_mla, gg_fox = [
        _matmul(part, xn, "tn", [BF16], "proj_dw_" + tag, tm=1024, tn=1024, tk=2048)
        for part, tag in zip(dproj, ("small", "fq", "fk", "fv", "g_mla", "g_fox"))]

    g_w_in = jnp.concatenate([gs[:o_kr], gs[o_kr:o_kr + MLA_ROPE], gfq, gfk, gfv,
                              gs[o_kr + LANE:o_kr + LANE + HF], gg_mla, gg_fox], axis=0)
    g_w_uq = g_w_uq_p.reshape(QL, H, QPAD)[:, :, :dqk].reshape(QL, H * dqk)
    g_w_ukv = jnp.concatenate([g_w_ukv_p[:, :H * MLA_NOPE].reshape(KVL, H, MLA_NOPE),
                               g_w_ukv_p[:, H * MLA_NOPE:].reshape(KVL, H, MLA_V)], axis=2).reshape(KVL, -1)

    gp_a = pack_a.slabs({"w_in": g_w_in, "w_uq": g_w_uq, "w_ukv": g_w_ukv})
    rs_a = _xchg_start(gp_a, lax.empty((3, RA, C), BF16), False, gg_fox, "grad_scatter_start_a")
    dxn = _matmul_parts(dproj, w_pack, "nn", F32, "proj_dx", order=rs_a[3])
    grad_x, g_attn_norm = _norm_bwd(xs, dxn, attn_norm, dh1, "attn_norm_bwd", with_bf16=False)
    gp_a_sent, recv_a = _xchg_wait(rs_a, False, grad_x, "grad_scatter_wait_a")
    swap_a = _sib_start(_sum_slabs(gp_a_sent, recv_a, chip, "grad_sum_a"), "grad_swap_start_a")
    vec_w = max(D, LANE)
    vec_rows = [g_attn_norm, g_mlp_norm, g_final, g_q_norm, g_kv_norm, g_bias, loss_part]
    vec = jnp.concatenate([_pad_cols(v, vec_w) for v in vec_rows] + [jnp.zeros((1, vec_w), F32)], axis=0)
    vsum = _all_reduce_vec(vec, "all_reduce_vectors")
    part_b, sib_b = _sib_wait(swap_b, vsum, "grad_swap_wait_b")

    grads, deltas, new_m, new_v = {}, {}, {}, {}

    def update(pack, mine, theirs):
        for nm, shape, _ in pack.group:
            K, N = shape
            if N == pack.C and K % 8 == 0 and pack.offs[nm] % _tile(K, 256, 8) == 0:
                g, d, nm_, nv_ = _adamw(weights[nm], mine, theirs, moments[nm][0], moments[nm][1], "adamw_" + nm,
                                        g_row=pack.offs[nm])
            else:
                g, d, nm_, nv_ = _adamw(weights[nm], pack.part(mine, nm, shape), pack.part(theirs, nm, shape),
                                        moments[nm][0], moments[nm][1], "adamw_" + nm)
            grads[nm], deltas[nm], new_m[nm], new_v[nm] = g, d, nm_, nv_
        return g

    last_b = update(pack_b, part_b, sib_b)
    part_a, sib_a = _sib_wait(swap_a, last_b, "grad_swap_wait_a")
    update(pack_a, part_a, sib_a)

    vec_names = ["attn_norm", "mlp_norm", "final_norm", "q_norm", "kv_norm", "fox_f_bias"]

    def vec_pack(arrs):
        return jnp.concatenate([_pad_cols(a.reshape(1, -1), vec_w) for a in arrs]
                               + [jnp.zeros((2, vec_w), F32)], axis=0)[None]

    vg, vd, vm, vv = _adamw(vec_pack([weights[n] for n in vec_names]), vsum, jnp.zeros_like(vsum),
                            vec_pack([moments[n][0] for n in vec_names]), vec_pack([moments[n][1] for n in vec_names]),
                            "adamw_vectors")
    for r, nm in enumerate(vec_names):
        shp = weights[nm].shape
        n = weights[nm].size
        grads[nm] = vsum[r, :n].reshape(shp)
        deltas[nm], new_m[nm], new_v[nm] = (vd[0, r, :n].reshape(shp), vm[0, r, :n].reshape(shp),
                                            vv[0, r, :n].reshape(shp))
    loss = vsum[6, 0]

    for res in (grads, deltas, new_m, new_v):
        res["w_in"] = flip(res["w_in"])
    order = ["attn_norm", "w_in", "fox_f_bias", "q_norm", "w_uq", "kv_norm", "w_ukv", "w_mla_branch", "w_fox_branch",
             "w_out", "mlp_norm", "w_up", "w_down", "final_norm"]
    return (loss, grad_x[None], *[grads[n] for n in order], *[deltas[n] for n in order],
            *[new_m[n] for n in order], *[new_v[n] for n in order])
```

```python
import math

import jax
import jax.numpy as jnp
from jax import lax
from jax.experimental import pallas as pl
from jax.experimental.pallas import tpu as pltpu

CHUNK = 64
MLA_HEADS = 8
MLA_Q_LORA = 512
MLA_KV_LORA = 256
MLA_NOPE = 128
MLA_ROPE = 64
MLA_V = 128
ROPE_THETA = 10000.0
FOX_HEADS = 8
FOX_HEAD_DIM = 128
EPS = 1e-6

ADAM_LR = 0.001
ADAM_B1 = 0.9
ADAM_B2 = 0.999
ADAM_EPS = 1e-08
ADAM_WD = 0.01
ADAM_STEP = 10

LANE = 128
QPAD = 2 * LANE
N_CHIPS = 4
N_DEV = 8
VMEM_LIMIT = 48 * 1024 * 1024
ATT_T = 2048
QSUB = 256
ROW_T = 256
PACK_ROWS = 256
LOG2E = 1.4426950408889634

BF16 = jnp.bfloat16
F32 = jnp.float32
MESH = pl.DeviceIdType.MESH

_NT = (((1,), (1,)), ((), ()))
_TN = (((0,), (0,)), ((), ()))
_NN = (((1,), (0,)), ((), ()))


def _tile(dim, pref, align=LANE):
    if dim <= pref:
        return dim
    t = (pref // align) * align
    while t >= align:
        if dim % t == 0:
            return t
        t -= align
    return dim


def _params(sem=None):
    return pltpu.CompilerParams(dimension_semantics=sem, vmem_limit_bytes=VMEM_LIMIT)


_ANY_SPEC = pl.BlockSpec(memory_space=pl.ANY)


def _matmul(a, b, mode, out_dtypes, name, *, tm=1024, tn=1024, tk=2048, extras=(), row_extras=(), col_extras=(),
            epilogue=None, order=None, into=None, b_in=None):
    b_shape = b.shape if b_in is None else b_in[0]
    if mode == "nn":
        (M, K), (K2, N) = a.shape, b_shape
    elif mode == "nt":
        (M, K), (N, K2) = a.shape, b_shape
    else:
        (K, M), (K2, N) = a.shape, b_shape
    assert K == K2, (name, a.shape, b_shape)
    tm, tn, tk = _tile(M, tm), _tile(N, tn), _tile(K, tk)
    nk = K // tk
    extras = [e if isinstance(e, tuple) else (e, 0) for e in extras]
    n_out = len(out_dtypes)
    n_ex = len(extras) + len(row_extras) + len(col_extras)
    n_ord = 0 if order is None else 1
    assert all(r.shape == (M, tn) for r in row_extras), name
    dims = {"nn": _NN, "nt": _NT, "tn": _TN}[mode]

    def body(*refs):
        a_ref, b_ref = refs[0], refs[1]
        ex_refs = refs[2:2 + n_ex]
        o_refs = refs[2 + n_ex + n_ord:2 + n_ex + n_ord + n_out]
        acc_ref = refs[2 + n_ex + n_ord + n_out]
        k = pl.program_id(2)
        part = lax.dot_general(a_ref[...], b_ref[...], dims, preferred_element_type=F32)

        @pl.when(k == 0)
        def _():
            acc_ref[...] = part

        @pl.when(k > 0)
        def _():
            acc_ref[...] += part

        @pl.when(k == nk - 1)
        def _():
            acc = acc_ref[...]
            if epilogue is None:
                outs = (acc,)
            else:
                outs = epilogue(acc, *[r[...] for r in ex_refs])
            for o_ref, o in zip(o_refs, outs):
                o_ref[...] = o.astype(o_ref.dtype)

    if mode == "nn":
        a_spec = pl.BlockSpec((tm, tk), lambda i, j, k: (i, k))
        b_spec = pl.BlockSpec((tk, tn), lambda i, j, k: (k, j))
    elif mode == "nt":
        a_spec = pl.BlockSpec((tm, tk), lambda i, j, k: (i, k))
        b_spec = pl.BlockSpec((tn, tk), lambda i, j, k: (j, k))
    else:
        a_spec = pl.BlockSpec((tk, tm), lambda i, j, k: (k, i))
        b_spec = pl.BlockSpec((tk, tn), lambda i, j, k: (k, j))
    if b_in is not None:
        b_block = (None, tn, tk) if mode == "nt" else (None, tk, tn)
        b_spec = pl.BlockSpec(b_block, lambda i, j, k: b_in[1](j, k))
    mn_spec = pl.BlockSpec((tm, tn), lambda i, j, k: (i, j))
    row_spec = pl.BlockSpec((tm, tn), lambda i, j, k: (i, 0))
    col_spec = pl.BlockSpec((1, tn), lambda i, j, k: (0, j))
    out_specs = [mn_spec] * n_out
    out_shape = [jax.ShapeDtypeStruct((M, N), dt) for dt in out_dtypes]
    aliases = {}
    if into is not None:
        buf, place = into
        assert n_out == 1 and n_ord == 1 and order is buf, name
        out_specs = [pl.BlockSpec((None, tm, tn), lambda i, j, k: place(i, j))]
        out_shape = [jax.ShapeDtypeStruct(buf.shape, buf.dtype)]
        aliases = {2 + n_ex: 0}
    outs = pl.pallas_call(
        body,
        name=name,
        grid=(M // tm, N // tn, nk),
        in_specs=([a_spec, b_spec]
                  + [pl.BlockSpec((tm, tn), lambda i, j, k, g=g: (i, j + g * (N // tn))) for _, g in extras]
                  + [row_spec] * len(row_extras) + [col_spec] * len(col_extras) + [_ANY_SPEC] * n_ord),
        out_specs=out_specs,
        out_shape=out_shape,
        scratch_shapes=[pltpu.VMEM((tm, tn), F32)],
        input_output_aliases=aliases,
        compiler_params=_params(("parallel", "parallel", "arbitrary")),
    )(a, b, *[e for e, _ in extras], *row_extras, *col_extras, *([] if order is None else [order]))
    return outs[0] if n_out == 1 else outs


def _matmul_parts(parts, b, mode, out_dtype, name, *, tm=1024, tn=1024, tk=1024, order=None):
    assert mode in ("nn", "tn")
    if mode == "nn":
        M, (K, N) = parts[0].shape[0], b.shape
        widths = [p.shape[1] for p in parts]
    else:
        K, N = b.shape
        widths = [p.shape[1] for p in parts]
        M = sum(widths)
    common = math.gcd(*widths)
    tm, tn, tk = _tile(M if mode == "nn" else common, tm), _tile(N, tn), _tile(common if mode == "nn" else K, tk)
    t_part = tk if mode == "nn" else tm
    assert sum(widths) == (K if mode == "nn" else M), name
    if any(w % t_part for w in widths):
        parts, widths = [jnp.concatenate(parts, axis=1)], [sum(widths)]
    lo =[sum(widths[:p]) // t_part for p in range(len(parts))]
    cnt = [w // t_part for w in widths]
    nk = K // tk
    n_parts = len(parts)
    n_ord = 0 if order is None else 1
    dims = _NN if mode == "nn" else _TN

    def body(*refs):
        a_refs = refs[0:n_parts]
        b_ref = refs[n_parts]
        o_ref, acc_ref = refs[n_parts + 1 + n_ord], refs[n_parts + 2 + n_ord]
        i, k = pl.program_id(0), pl.program_id(2)
        sel = k if mode == "nn" else i
        for p in range(n_parts):
            @pl.when((sel >= lo[p]) & (sel < lo[p] + cnt[p]))
            def _(p=p):
                part = lax.dot_general(a_refs[p][...], b_ref[...], dims, preferred_element_type=F32)

                @pl.when(k == 0)
                def _():
                    acc_ref[...] = part

                @pl.when(k > 0)
                def _():
                    acc_ref[...] += part

        @pl.when(k == nk - 1)
        def _():
            o_ref[...] = acc_ref[...].astype(o_ref.dtype)

    def a_spec(p):
        if mode == "nn":
            return pl.BlockSpec((tm, tk), lambda i, j, k: (i, jnp.clip(k - lo[p], 0, cnt[p] - 1)))
        return pl.BlockSpec((tk, tm), lambda i, j, k: (
            jnp.where((i >= lo[p]) & (i < lo[p] + cnt[p]), k, 0), jnp.clip(i - lo[p], 0, cnt[p] - 1)))

    return pl.pallas_call(
        body, name=name, grid=(M // tm, N // tn, nk),
        in_specs=[a_spec(p) for p in range(n_parts)] + [pl.BlockSpec((tk, tn), lambda i, j, k: (k, j))]
        + [_ANY_SPEC] * n_ord,
        out_specs=pl.BlockSpec((tm, tn), lambda i, j, k: (i, j)),
        out_shape=jax.ShapeDtypeStruct((M, N), out_dtype),
        scratch_shapes=[pltpu.VMEM((tm, tn), F32)],
        compiler_params=_params(("parallel", "parallel", "arbitrary")),
    )(*parts, b, *([] if order is None else [order]))


def _mm_tn(a, b, name, tm=1024, tn=1024, into=None):
    return _matmul(a, b, "tn", [F32], name, tm=tm, tn=tn, tk=2048, into=into,
                   order=None if into is None else into[0])


def _row_spec(ts, width, col=0):
    return pl.BlockSpec((ts, width), lambda i: (i, col))


def _full_spec(shape):
    return pl.BlockSpec(shape, lambda i: tuple(0 for _ in shape))


def _rms(x):
    return lax.rsqrt(jnp.mean(x * x, axis=-1, keepdims=True) + EPS)


def _rms_bwd(x, dy, g):
    r = _rms(x)
    xh = x * r
    gy = dy * g
    dx = r * (gy - xh * jnp.mean(xh * gy, axis=-1, keepdims=True))
    return dx, dy * xh


def _norm_fwd(x, g, name, order=None):
    S, D = x.shape
    ts = _tile(S, ROW_T, 8)

    def body(x_ref, g_ref, *rest):
        o_ref = rest[-1]
        xv = x_ref[...]
        o_ref[...] = ((xv * _rms(xv)) * g_ref[...]).astype(BF16)

    extra = [] if order is None else [order]
    return pl.pallas_call(
        body, name=name, grid=(S // ts,),
        in_specs=[_row_spec(ts, D), _full_spec((1, D))] + [_ANY_SPEC] * len(extra),
        out_specs=_row_spec(ts, D),
        out_shape=jax.ShapeDtypeStruct((S, D), BF16),
        compiler_params=_params(("parallel",)),
    )(x, g, *extra)


def _norm_bwd(x, dy, g, dres, name, with_bf16=True):
    S, D = x.shape
    ts = _tile(S, ROW_T, 8)

    def body(x_ref, dy_ref, g_ref, dres_ref, dx_ref, *rest):
        dg_ref = rest[-1]
        dx, dg_rows = _rms_bwd(x_ref[...], dy_ref[...], g_ref[...])
        dx = dres_ref[...] + dx
        dx_ref[...] = dx
        if with_bf16:
            rest[0][...] = dx.astype(BF16)

        @pl.when(pl.program_id(0) == 0)
        def _():
            dg_ref[...] = jnp.zeros_like(dg_ref)

        dg_ref[...] += jnp.sum(dg_rows, axis=0, keepdims=True)

    return pl.pallas_call(
        body, name=name, grid=(S // ts,),
        in_specs=[_row_spec(ts, D), _row_spec(ts, D), _full_spec((1, D)), _row_spec(ts, D)],
        out_specs=[_row_spec(ts, D)] * (2 if with_bf16 else 1) + [_full_spec((1, D))],
        out_shape=([jax.ShapeDtypeStruct((S, D), F32)] + [jax.ShapeDtypeStruct((S, D), BF16)] * with_bf16
                   + [jax.ShapeDtypeStruct((1, D), F32)]),
        compiler_params=_params(("arbitrary",)),
    )(x, dy, g, dres)


def _rope(x, c, sa, sb, sign):
    w = x.shape[-1]
    half = MLA_ROPE // 2
    fwd = pltpu.roll(x, w - half, 1)
    back = pltpu.roll(x, half, 1)
    if sign < 0:
        return x * c - fwd * sa - back * sb
    return x * c + fwd * sa + back * sb


def _split3(x):
    hi = x.astype(BF16)
    r1 = x - hi.astype(F32)
    mid = r1.astype(BF16)
    lo = (r1 - mid.astype(F32)).astype(BF16)
    return hi, mid, lo


def _prep_fwd(small, q_norm, kv_norm, bias_pad, kc, ksa, ksb, n_heads, name):
    S, W = small.shape
    QL, KVL = q_norm.shape[1], kv_norm.shape[1]
    assert W == QL + KVL + 2 * LANE
    ts = _tile(S, ROW_T, 8)
    tri = (lax.broadcasted_iota(jnp.int32, (ts, ts), 0) >= lax.broadcasted_iota(jnp.int32, (ts, ts), 1)).astype(BF16)

    def body(s_ref, qn_ref, kvn_ref, b_ref, kc_ref, ksa_ref, ksb_ref, tri_ref,
             cqn_ref, ckvn_ref, kr_ref, cum_ref, carry_ref):
        cq = s_ref[:, 0:QL]
        cqn_ref[...] = ((cq * _rms(cq)) * qn_ref[...]).astype(BF16)
        ckv = s_ref[:, QL:QL + KVL]
        ckvn_ref[...] = ((ckv * _rms(ckv)) * kvn_ref[...]).astype(BF16)
        kr = s_ref[:, QL + KVL:QL + KVL + LANE]
        kr_ref[...] = _rope(kr, kc_ref[...], ksa_ref[...], ksb_ref[...], 1).astype(BF16)
        z = s_ref[:, QL + KVL + LANE:W] + b_ref[...]
        logf = jnp.minimum(z, 0.0) - jnp.log1p(jnp.exp(-jnp.abs(z)))
        lane = lax.broadcasted_iota(jnp.int32, logf.shape, 1)
        logf = jnp.where(lane < n_heads, logf, 0.0)

        @pl.when(pl.program_id(0) == 0)
        def _():
            carry_ref[...] = jnp.zeros_like(carry_ref)

        t = tri_ref[...]
        cum = carry_ref[...]
        for part in _split3(logf):
            cum = cum + jnp.dot(t, part, preferred_element_type=F32)
        cum_ref[...] = cum
        carry_ref[...] = cum[ts - 1:ts, :]

    return pl.pallas_call(
        body, name=name, grid=(S // ts,),
        in_specs=[_row_spec(ts, W), _full_spec((1, QL)), _full_spec((1, KVL)), _full_spec((1, LANE)),
                  _row_spec(ts, LANE), _row_spec(ts, LANE), _row_spec(ts, LANE), _full_spec((ts, ts))],
        out_specs=[_row_spec(ts, QL), _row_spec(ts, KVL), _row_spec(ts, LANE), _row_spec(ts, LANE)],
        out_shape=[jax.ShapeDtypeStruct((S, QL), BF16), jax.ShapeDtypeStruct((S, KVL), BF16),
                   jax.ShapeDtypeStruct((S, LANE), BF16), jax.ShapeDtypeStruct((S, LANE), F32)],
        scratch_shapes=[pltpu.VMEM((1, LANE), F32)],
        compiler_params=_params(("arbitrary",)),
    )(small, q_norm, kv_norm, bias_pad, kc, ksa, ksb, tri)


def _prep_bwd(small, dcqn, dckvn, dkr_heads, dlogf, q_norm, kv_norm, bias_pad, kc, ksa, ksb, n_heads, name):
    S, W = small.shape
    QL, KVL = q_norm.shape[1], kv_norm.shape[1]
    ts = _tile(S, ROW_T, 8)

    def body(s_ref, dcq_ref, dckv_ref, dkr_ref, dlf_ref, qn_ref, kvn_ref, b_ref, kc_ref, ksa_ref, ksb_ref,
             ds_ref, gq_ref, gkv_ref, gb_ref):
        dcq, gq_rows = _rms_bwd(s_ref[:, 0:QL], dcq_ref[...], qn_ref[...])
        ds_ref[:, 0:QL] = dcq.astype(BF16)
        dckv, gkv_rows = _rms_bwd(s_ref[:, QL:QL + KVL], dckv_ref[...], kvn_ref[...])
        ds_ref[:, QL:QL + KVL] = dckv.astype(BF16)
        dkr = dkr_ref[:, 0:LANE]
        for h in range(1, n_heads):
            dkr = dkr + dkr_ref[:, h * LANE:(h + 1) * LANE]
        ds_ref[:, QL + KVL:QL + KVL + LANE] = _rope(dkr, kc_ref[...], ksa_ref[...], ksb_ref[...], -1).astype(BF16)
        z = s_ref[:, QL + KVL + LANE:W] + b_ref[...]
        dff = dlf_ref[...] * (1.0 / (1.0 + jnp.exp(z)))
        ds_ref[:, QL + KVL + LANE:W] = dff.astype(BF16)

        @pl.when(pl.program_id(0) == 0)
        def _():
            gq_ref[...] = jnp.zeros_like(gq_ref)
            gkv_ref[...] = jnp.zeros_like(gkv_ref)
            gb_ref[...] = jnp.zeros_like(gb_ref)

        gq_ref[...] += jnp.sum(gq_rows, axis=0, keepdims=True)
        gkv_ref[...] += jnp.sum(gkv_rows, axis=0, keepdims=True)
        gb_ref[...] += jnp.sum(dff, axis=0, keepdims=True)

    return pl.pallas_call(
        body, name=name, grid=(S // ts,),
        in_specs=[_row_spec(ts, W), _row_spec(ts, QL), _row_spec(ts, KVL), _row_spec(ts, n_heads * LANE),
                  _row_spec(ts, LANE), _full_spec((1, QL)), _full_spec((1, KVL)), _full_spec((1, LANE)),
                  _row_spec(ts, LANE), _row_spec(ts, LANE), _row_spec(ts, LANE)],
        out_specs=[_row_spec(ts, W), _full_spec((1, QL)), _full_spec((1, KVL)), _full_spec((1, LANE))],
        out_shape=[jax.ShapeDtypeStruct((S, W), BF16), jax.ShapeDtypeStruct((1, QL), F32),
                   jax.ShapeDtypeStruct((1, KVL), F32), jax.ShapeDtypeStruct((1, LANE), F32)],
        compiler_params=_params(("arbitrary",)),
    )(small, dcqn, dckvn, dkr_heads, dlogf, q_norm, kv_norm, bias_pad, kc, ksa, ksb)


def _sigmoid(z):
    return 1.0 / (1.0 + jnp.exp(-z))


def _final(h, g, target, name):
    S, D = h.shape
    ts = _tile(S, ROW_T, 8)

    def body(h_ref, g_ref, t_ref, dh_ref, dhb_ref, dg_ref, loss_ref):
        hv = h_ref[...]
        gv = g_ref[...]
        err = (hv * _rms(hv)) * gv - t_ref[...]
        dh, dg_rows = _rms_bwd(hv, err / D, gv)
        dh_ref[...] = dh
        dhb_ref[...] = dh.astype(BF16)

        @pl.when(pl.program_id(0) == 0)
        def _():
            dg_ref[...] = jnp.zeros_like(dg_ref)
            loss_ref[...] = jnp.zeros_like(loss_ref)

        dg_ref[...] += jnp.sum(dg_rows, axis=0, keepdims=True)
        row_loss = jnp.mean(err * err, axis=-1, keepdims=True)
        loss_ref[...] += 0.5 * jnp.sum(row_loss, axis=0, keepdims=True)

    return pl.pallas_call(
        body, name=name, grid=(S // ts,),
        in_specs=[_row_spec(ts, D), _full_spec((1, D)), _row_spec(ts, D)],
        out_specs=[_row_spec(ts, D), _row_spec(ts, D), _full_spec((1, D)), _full_spec((1, LANE))],
        out_shape=[jax.ShapeDtypeStruct((S, D), F32), jax.ShapeDtypeStruct((S, D), BF16),
                   jax.ShapeDtypeStruct((1, D), F32), jax.ShapeDtypeStruct((1, LANE), F32)],
        compiler_params=_params(("arbitrary",)),
    )(h, g, target)


def _suffix_sum_rows(x, name):
    R, S = x.shape
    tb = _tile(S, 512)
    nb = S // tb
    tri = (lax.broadcasted_iota(jnp.int32, (tb, tb), 0) >= lax.broadcasted_iota(jnp.int32, (tb, tb), 1)).astype(BF16)

    def body(x_ref, tri_ref, o_ref, carry_ref):
        @pl.when(pl.program_id(0) == 0)
        def _():
            carry_ref[...] = jnp.zeros_like(carry_ref)

        xv = x_ref[...]
        t = tri_ref[...]
        acc = jnp.broadcast_to(carry_ref[:, 0:1], xv.shape)
        for part in _split3(xv):
            acc = acc + jnp.dot(part, t, preferred_element_type=F32)
        o_ref[...] = acc
        carry_ref[...] = jnp.broadcast_to(acc[:, 0:1], carry_ref.shape)

    rev = pl.BlockSpec((R, tb), lambda i: (0, nb - 1 - i))
    return pl.pallas_call(
        body, name=name, grid=(nb,),
        in_specs=[rev, _full_spec((tb, tb))], out_specs=rev,
        out_shape=jax.ShapeDtypeStruct((R, S), F32),
        scratch_shapes=[pltpu.VMEM((R, LANE), F32)],
        compiler_params=_params(("arbitrary",)),
    )(x, tri)


def _pairs(nb, by_key):
    if by_key:
        pr = [(i, j) for j in range(nb) for i in range(j, nb)]
    else:
        pr = [(i, j) for i in range(nb) for j in range(i + 1)]
    return (jnp.asarray([p[0] for p in pr], jnp.int32), jnp.asarray([p[1] for p in pr], jnp.int32), len(pr))


class _AttT:
    def __init__(self, S, n_heads, q, ks, v, scale, chunk_causal, cum_rep=None):
        self.S, self.H, self.q, self.ks, self.v = S, n_heads, q, ks, v
        self.scale, self.chunk_causal, self.cum_rep = scale, chunk_causal, cum_rep
        self.T = _tile(S, ATT_T)
        self.qs = min(QSUB, self.T)
        self.nb = S // self.T
        self.dq, self.dv = q[1], v[1]
        self.has_bias = cum_rep is not None

    def q_spec(self, op):
        _, w, off, per_head = op
        return pl.BlockSpec((self.T, w), lambda h, p, it, jt: (it[p], off + (h if per_head else 0)))

    def k_spec(self, op):
        _, w, off, per_head = op
        return pl.BlockSpec((self.T, w), lambda h, p, it, jt: (jt[p], off + (h if per_head else 0)))

    def row_q(self):
        return pl.BlockSpec((None, 1, self.T), lambda h, p, it, jt: (h, 0, it[p]))

    def cum_k(self):
        return pl.BlockSpec((None, self.T, self.qs), lambda h, p, it, jt: (h, jt[p], 0))

    def sub_blocks(self, masked):
        return [(q0, min(self.T, q0 + self.qs) if masked else self.T) for q0 in range(0, self.T, self.qs)]

    def scores(self, k, q_sub, cum, q0, masked):
        s = lax.dot_general(k, q_sub, _NT, preferred_element_type=F32)
        if self.has_bias:
            s = s - cum
        mask = None
        if masked:
            r = lax.broadcasted_iota(jnp.int32, s.shape, 0)
            c = lax.broadcasted_iota(jnp.int32, s.shape, 1) + q0
            mask = (r // CHUNK <= c // CHUNK) if self.chunk_causal else (r <= c)
        return s, mask


def _join(k_refs):
    return k_refs[0][...] if len(k_refs) == 1 else jnp.concatenate([r[...] for r in k_refs], axis=-1)


def _att_fwd_t(att, name, exact=False):
    S, H, T, qs = att.S, att.H, att.T, att.qs
    it, jt, npairs = _pairs(att.nb, by_key=False)
    nk = len(att.ks)

    def body(it_ref, jt_ref, *refs):
        q_ref = refs[0]
        k_refs = refs[1:1 + nk]
        v_ref = refs[1 + nk]
        n = 2 + nk
        cum_ref = None
        if att.has_bias:
            cum_ref = refs[n]
            n += 1
        o_ref = refs[n]
        n += 1
        ox_ref = None
        if exact:
            ox_ref = refs[n]
            n += 1
        lse_ref, m_ref, l_ref, acc_ref = refs[n:n + 4]
        lo_ref = refs[n + 4] if exact else None
        p = pl.program_id(1)
        i, j = it_ref[p], jt_ref[p]

        @pl.when(j == 0)
        def _():
            m_ref[...] = jnp.full_like(m_ref, -jnp.inf)
            l_ref[...] = jnp.zeros_like(l_ref)
            acc_ref[...] = jnp.zeros_like(acc_ref)
            if exact:
                lo_ref[...] = jnp.zeros_like(lo_ref)

        def step(masked):
            k = _join(k_refs)
            v = v_ref[...]
            subs = att.sub_blocks(masked)

            def logits(idx):
                q0, nkeys = subs[idx]
                cum = cum_ref[0:nkeys, :] if att.has_bias else None
                return att.scores(k[0:nkeys], q_ref[q0:q0 + qs, :], cum, q0, masked)

            ahead = logits(0)
            for idx, (q0, nkeys) in enumerate(subs):
                qsl = slice(q0, q0 + qs)
                s, mask = ahead
                if idx + 1 < len(subs):
                    ahead = logits(idx + 1)
                if masked:
                    s = jnp.where(mask, s, -jnp.inf)
                m_prev = m_ref[:, qsl]
                m_new = jnp.maximum(m_prev, jnp.max(s, axis=0, keepdims=True))
                alpha = jnp.exp2(m_prev - m_new)
                pr = jnp.exp2(s - m_new)
                l_ref[:, qsl] = alpha * l_ref[:, qsl] + jnp.sum(pr, axis=0, keepdims=True)
                p_hi = pr.astype(BF16)
                acc_ref[:, qsl] = alpha * acc_ref[:, qsl] + lax.dot_general(
                    v[0:nkeys], p_hi, _TN, preferred_element_type=F32)
                if exact:
                    p_lo = (pr - p_hi.astype(F32)).astype(BF16)
                    lo_ref[:, qsl] = alpha * lo_ref[:, qsl] + lax.dot_general(
                        v[0:nkeys], p_lo, _TN, preferred_element_type=F32)
                m_ref[:, qsl] = m_new

        @pl.when(j < i)
        def _():
            step(False)

        @pl.when(j == i)
        def _():
            step(True)
            l = l_ref[...]
            inv = 1.0 / l
            o_ref[...] = jnp.transpose(acc_ref[...] * inv).astype(o_ref.dtype)
            if exact:
                ox_ref[...] = jnp.transpose((acc_ref[...] + lo_ref[...]) * inv)
            lse_ref[...] = m_ref[...] + jnp.log2(l)

    in_specs = [att.q_spec(att.q)] + [att.k_spec(k) for k in att.ks] + [att.k_spec(att.v)]
    args = [att.q[0]] + [k[0] for k in att.ks] + [att.v[0]]
    if att.has_bias:
        in_specs.append(att.cum_k())
        args.append(att.cum_rep)
    o_spec = pl.BlockSpec((T, att.dv), lambda h, p, it, jt: (it[p], h))
    out_specs = [o_spec]
    out_shape = [jax.ShapeDtypeStruct((S, H * att.dv), BF16)]
    scratch = [pltpu.VMEM((1, T), F32), pltpu.VMEM((1, T), F32), pltpu.VMEM((att.dv, T), F32)]
    if exact:
        out_specs.append(o_spec)
        out_shape.append(jax.ShapeDtypeStruct((S, H * att.dv), F32))
        scratch.append(pltpu.VMEM((att.dv, T), F32))
    out_specs.append(att.row_q())
    out_shape.append(jax.ShapeDtypeStruct((H, 1, S), F32))
    return pl.pallas_call(
        body, name=name,
        grid_spec=pltpu.PrefetchScalarGridSpec(
            num_scalar_prefetch=2, grid=(H, npairs), in_specs=in_specs, out_specs=out_specs,
            scratch_shapes=scratch),
        out_shape=out_shape,
        compiler_params=_params(("parallel", "arbitrary")),
    )(it, jt, *args)


def _att_bwd_t(att, do, lse, o, dq_dtype, dk_dtypes, name, dq_rope=None, order=None):
    S, H, T, qs = att.S, att.H, att.T, att.qs
    it, jt, npairs = _pairs(att.nb, by_key=True)
    nk = len(att.ks)
    last = att.nb - 1
    widths = [k[1] for k in att.ks]

    def body(it_ref, jt_ref, *refs):
        q_ref = refs[0]
        k_refs = refs[1:1 + nk]
        v_ref, do_ref, lse_ref, o_ref = refs[1 + nk:5 + nk]
        n = 5 + nk
        cum_ref = None
        if att.has_bias:
            cum_ref = refs[n]
            n += 1
        rope_refs = None
        if dq_rope is not None:
            rope_refs = refs[n:n + 3]
            n += 3
        if order is not None:
            n += 1
        dl_acc = refs[-1]
        dq_ref = refs[n]
        dk_refs = refs[n + 1:n + 1 + nk]
        dv_ref = refs[n + 1 + nk]
        n += nk + 2
        dc_ref = None
        if att.has_bias:
            dc_ref = refs[n]
            n += 1
        dq_acc, dk_acc, dv_acc = refs[n:n + 3]
        dc_acc = refs[n + 3] if att.has_bias else None
        p = pl.program_id(1)
        i, j = it_ref[p], jt_ref[p]

        @pl.when(p == 0)
        def _():
            dq_acc[...] = jnp.zeros_like(dq_acc)

        @pl.when(i == j)
        def _():
            dk_acc[...] = jnp.zeros_like(dk_acc)
            dv_acc[...] = jnp.zeros_like(dv_acc)
            if att.has_bias:
                dc_acc[...] = jnp.zeros_like(dc_acc)

        @pl.when(j == 0)
        def _():
            prod = do_ref[...].astype(F32) * o_ref[...].astype(F32)
            ones = jnp.ones((8, att.dv), BF16)
            rows = jnp.zeros((8, T), F32)
            for part in _split3(prod):
                rows = rows + lax.dot_general(ones, part, _NT, preferred_element_type=F32)
            dl_acc[i] = rows[0:1, :]

        def step(masked):
            k = _join(k_refs)
            v = v_ref[...]
            dl = dl_acc[i]
            subs = att.sub_blocks(masked)

            def logits(idx):
                q0, nkeys = subs[idx]
                cum = cum_ref[0:nkeys, :] if att.has_bias else None
                return att.scores(k[0:nkeys], q_ref[q0:q0 + qs, :], cum, q0, masked)

            ahead = logits(0)
            for idx, (q0, nkeys) in enumerate(subs):
                qsl = slice(q0, q0 + qs)
                ksl = slice(0, nkeys)
                q_sub = q_ref[qsl, :]
                do_sub = do_ref[qsl, :]
                s, mask = ahead
                if idx + 1 < len(subs):
                    ahead = logits(idx + 1)
                pr = jnp.exp2(s - lse_ref[:, qsl])
                if masked:
                    pr = jnp.where(mask, pr, 0.0)
                dp = lax.dot_general(v[ksl], do_sub, _NT, preferred_element_type=F32)
                ds = pr * (dp - dl[:, qsl])
                ds_b = ds.astype(BF16)
                dv_acc[ksl, :] += jnp.dot(pr.astype(BF16), do_sub, preferred_element_type=F32)
                dk_acc[ksl, :] += jnp.dot(ds_b, q_sub, preferred_element_type=F32)
                dq_acc[i, :, qsl] += lax.dot_general(k[ksl], ds_b, _TN, preferred_element_type=F32)
                if att.has_bias:
                    part = ds[:, 0:LANE] if qs >= LANE else ds
                    for c0 in range(LANE, qs, LANE):
                        part = part + ds[:, c0:c0 + LANE]
                    dc_acc[ksl, :] += part

        @pl.when(i > j)
        def _():
            step(False)

        @pl.when(i == j)
        def _():
            step(True)
            dq = jnp.transpose(dq_acc[i] * att.scale)
            if dq_rope is not None:
                dq = _rope(dq, rope_refs[0][...], rope_refs[1][...], rope_refs[2][...], -1)
            dq_ref[...] = dq.astype(dq_ref.dtype)

        @pl.when(i == last)
        def _():
            dk = dk_acc[...] * (1.0 / LOG2E)
            off = 0
            for r, w in zip(dk_refs, widths):
                r[...] = dk[:, off:off + w].astype(r.dtype)
                off += w
            dv_ref[...] = dv_acc[...].astype(dv_ref.dtype)
            if att.has_bias:
                dc_ref[...] = -jnp.sum(dc_acc[...], axis=-1, keepdims=True)

    do_op = (do, att.dv, 0, True)
    o_spec = pl.BlockSpec((T, att.dv), lambda h, p, it, jt: (jnp.where(jt[p] == 0, it[p], last), h))
    in_specs = ([att.q_spec(att.q)] + [att.k_spec(k) for k in att.ks]
                + [att.k_spec(att.v), att.q_spec(do_op), att.row_q(), o_spec])
    args = [att.q[0]] + [k[0] for k in att.ks] + [att.v[0], do, lse, o]
    if att.has_bias:
        in_specs.append(att.cum_k())
        args.append(att.cum_rep)
    if dq_rope is not None:
        in_specs += [pl.BlockSpec((T, att.dq), lambda h, p, it, jt: (jt[p], 0))] * 3
        args += list(dq_rope)
    if order is not None:
        in_specs.append(_ANY_SPEC)
        args.append(order)
    out_specs = [pl.BlockSpec((T, att.dq), lambda h, p, it, jt: (jt[p], h))]
    out_shape = [jax.ShapeDtypeStruct((S, H * att.dq), dq_dtype)]
    out_specs += [pl.BlockSpec((T, w), lambda h, p, it, jt: (jt[p], h)) for w in widths]
    out_shape += [jax.ShapeDtypeStruct((S, H * w), dt) for w, dt in zip(widths, dk_dtypes)]
    out_specs.append(pl.BlockSpec((T, att.dv), lambda h, p, it, jt: (jt[p], h)))
    out_shape.append(jax.ShapeDtypeStruct((S, H * att.dv), BF16))
    scratch = [pltpu.VMEM((att.nb, att.dq, T), F32), pltpu.VMEM((T, att.dq), F32), pltpu.VMEM((T, att.dv), F32)]
    if att.has_bias:
        out_specs.append(pl.BlockSpec((None, T, 1), lambda h, p, it, jt: (h, jt[p], 0)))
        out_shape.append(jax.ShapeDtypeStruct((H, S, 1), F32))
        scratch.append(pltpu.VMEM((T, min(qs, LANE)), F32))
    scratch.append(pltpu.VMEM((att.nb, 1, T), F32))
    return pl.pallas_call(
        body, name=name,
        grid_spec=pltpu.PrefetchScalarGridSpec(
            num_scalar_prefetch=2, grid=(H, npairs), in_specs=in_specs, out_specs=out_specs,
            scratch_shapes=scratch),
        out_shape=out_shape,
        compiler_params=_params(("parallel", "arbitrary")),
    )(it, jt, *args)


def _adamw(w, g1, g2, m, v, name, g_row=None):
    _, K, N = w.shape
    by_rows = K % 8 == 0
    tr = _tile(K, 256, 8) if by_rows else K
    if g_row is None:
        assert g1.shape == (K, N) and g2.shape == (K, N), name
        g_row = 0
    assert by_rows and g_row % tr == 0 or g_row == 0, name
    g_blk = g_row // tr
    tc = N if by_rows else _tile(N, LANE)
    c1 = 1.0 - ADAM_B1 ** ADAM_STEP
    c2 = 1.0 - ADAM_B2 ** ADAM_STEP

    def body(w_ref, g1_ref, g2_ref, m_ref, v_ref, g_ref, d_ref, nm_ref, nv_ref):
        gv = g1_ref[...] + g2_ref[...]
        nm = ADAM_B1 * m_ref[...] + (1.0 - ADAM_B1) * gv
        nv = ADAM_B2 * v_ref[...] + (1.0 - ADAM_B2) * (gv * gv)
        g_ref[...] = gv
        d_ref[...] = -ADAM_LR * ((nm / c1) / (jnp.sqrt(nv / c2) + ADAM_EPS) + ADAM_WD * w_ref[...])
        nm_ref[...] = nm
        nv_ref[...] = nv

    if by_rows:
        blk = pl.BlockSpec((None, tr, N), lambda i: (0, i, 0))
        gblk = pl.BlockSpec((tr, N), lambda i: (g_blk + i, 0))
    else:
        blk = pl.BlockSpec((None, K, tc), lambda i: (0, 0, i))
        gblk = pl.BlockSpec((K, tc), lambda i: (0, i))
    return pl.pallas_call(
        body, name=name, grid=(K // tr if by_rows else N // tc,),
        in_specs=[blk, gblk, gblk, blk, blk], out_specs=[blk] * 4,
        out_shape=[jax.ShapeDtypeStruct((1, K, N), F32)] * 4,
        compiler_params=_params(("parallel",)),
    )(w, g1, g2, m, v)


_HBM_SPEC = pl.BlockSpec(memory_space=pltpu.HBM)
_SEM_SPEC = pl.BlockSpec(memory_space=pltpu.SEMAPHORE)
_VMEM_SPEC = pl.BlockSpec(memory_space=pltpu.VMEM)
_EFFECT = pltpu.SideEffectType.DATAFLOW_SIDE_EFFECTING


def _place():
    return lax.axis_index("x"), lax.axis_index("y"), lax.axis_index("c")


def _other_chips(x, y):
    return [(1 - x, y), (x, 1 - y), (1 - x, 1 - y)]


def _chip_copies(src_ref, land_ref, sems, gather):
    x, y, c = _place()
    me = 2 * x + y
    out, back = [], []
    if gather == "half":
        half = src_ref.shape[0] // 2
        mine = pl.ds(pl.multiple_of(c * half, 16), half)
    for n, (px, py) in enumerate(_other_chips(x, y)):
        if gather == "half":
            src, there, here = src_ref.at[mine], land_ref.at[me, mine], land_ref.at[2 * px + py, mine]
        elif gather:
            src, there, here = src_ref, land_ref.at[me], land_ref.at[2 * px + py]
        else:
            src, there, here = src_ref.at[2 * px + py], land_ref.at[n], land_ref.at[n]
        out.append(pltpu.make_async_remote_copy(
            src_ref=src, dst_ref=there, send_sem=sems[n], recv_sem=sems[3 + n],
            device_id=(px, py, c), device_id_type=MESH))
        back.append(pltpu.make_async_remote_copy(
            src_ref=src, dst_ref=here, send_sem=sems[n], recv_sem=sems[3 + n],
            device_id=(px, py, c), device_id_type=MESH))
    return out, back


def _xchg_start(src, land, gather, order, name):
    def body(src_ref, land_ref, order_ref, *outs):
        sems = outs[0:6]
        token = outs[8]
        out, _ = _chip_copies(src_ref, land_ref, sems, gather)
        for cp in out:
            cp.start()
        token[...] = jnp.zeros_like(token)

    outs = pl.pallas_call(
        body, name=name,
        out_shape=(pltpu.SemaphoreType.DMA(()),) * 6 + (
            pltpu.HBM(src.shape, src.dtype), pltpu.HBM(land.shape, land.dtype),
            jax.ShapeDtypeStruct((8, LANE), F32)),
        in_specs=(_HBM_SPEC, _HBM_SPEC, _ANY_SPEC),
        out_specs=(_SEM_SPEC,) * 6 + (_HBM_SPEC, _HBM_SPEC, _VMEM_SPEC),
        input_output_aliases={0: 6, 1: 7},
        compiler_params=pltpu.CompilerParams(has_side_effects=_EFFECT),
    )(pltpu.with_memory_space_constraint(src, pltpu.HBM), pltpu.with_memory_space_constraint(land, pltpu.HBM), order)
    return outs[0:6], outs[6], outs[7], outs[8]


def _xchg_wait(started, gather, after, name):
    sems, src, land, _ = started
    after = after if isinstance(after, tuple) else (after,)

    def body(src_ref, land_ref, *rest):
        _, back = _chip_copies(src_ref, land_ref, rest[0:6], gather)
        for cp in back:
            cp.wait_send()
            cp.wait_recv()

    return pl.pallas_call(
        body, name=name,
        out_shape=(pltpu.HBM(src.shape, src.dtype), pltpu.HBM(land.shape, land.dtype)),
        in_specs=(_HBM_SPEC, _HBM_SPEC) + (_SEM_SPEC,) * 6 + (_ANY_SPEC,) * len(after),
        out_specs=(_HBM_SPEC, _HBM_SPEC),
        input_output_aliases={0: 0, 1: 1},
        compiler_params=pltpu.CompilerParams(has_side_effects=_EFFECT),
    )(src, land, *sems, *after)


def _forward_halves(land, name):
    _, R, C = land.shape
    half = R // 2
    assert half % 16 == 0

    def body(land_ref, out_ref, send_sems, recv_sems):
        x, y, c = _place()
        mine = pl.ds(pl.multiple_of(c * half, 16), half)
        theirs = pl.ds(pl.multiple_of((1 - c) * half, 16), half)
        sends = []
        for n, (px, py) in enumerate(_other_chips(x, y)):
            cp = pltpu.make_async_remote_copy(
                src_ref=land_ref.at[2 * px + py, mine], dst_ref=out_ref.at[2 * px + py, mine],
                send_sem=send_sems.at[n], recv_sem=recv_sems.at[n], device_id=(x, y, 1 - c), device_id_type=MESH)
            cp.start()
            sends.append(cp)
        for n, (px, py) in enumerate(_other_chips(x, y)):
            pltpu.make_async_remote_copy(
                src_ref=land_ref.at[2 * px + py, theirs], dst_ref=out_ref.at[2 * px + py, theirs],
                send_sem=send_sems.at[n], recv_sem=recv_sems.at[n], device_id=(x, y, 1 - c),
                device_id_type=MESH).wait_recv()
        for cp in sends:
            cp.wait_send()

    return pl.pallas_call(
        body, name=name,
        in_specs=[_ANY_SPEC], out_specs=_ANY_SPEC,
        out_shape=jax.ShapeDtypeStruct(land.shape, land.dtype),
        input_output_aliases={0: 0},
        scratch_shapes=[pltpu.SemaphoreType.DMA((3,)), pltpu.SemaphoreType.DMA((3,))],
    )(land)


def _sib_copy(src_ref, land_ref, send_sem, recv_sem):
    x, y, c = _place()
    return pltpu.make_async_remote_copy(src_ref=src_ref, dst_ref=land_ref, send_sem=send_sem, recv_sem=recv_sem,
                                        device_id=(x, y, 1 - c), device_id_type=MESH)


def _sib_start(src, name):
    land = lax.empty(src.shape, src.dtype)

    def body(src_ref, land_ref, send_sem, recv_sem, src_thru, land_thru, token):
        _sib_copy(src_ref, land_ref, send_sem, recv_sem).start()
        token[...] = jnp.zeros_like(token)

    return pl.pallas_call(
        body, name=name,
        out_shape=(pltpu.SemaphoreType.DMA(()), pltpu.SemaphoreType.DMA(()),
                   pltpu.HBM(src.shape, src.dtype), pltpu.HBM(land.shape, land.dtype),
                   jax.ShapeDtypeStruct((8, LANE), F32)),
        in_specs=(_HBM_SPEC, _HBM_SPEC),
        out_specs=(_SEM_SPEC, _SEM_SPEC, _HBM_SPEC, _HBM_SPEC, _VMEM_SPEC),
        input_output_aliases={0: 2, 1: 3},
        compiler_params=pltpu.CompilerParams(has_side_effects=_EFFECT),
    )(pltpu.with_memory_space_constraint(src, pltpu.HBM), pltpu.with_memory_space_constraint(land, pltpu.HBM))


def _sib_wait(started, after, name):
    send_sem, recv_sem, src, land, _ = started

    def body(src_ref, land_ref, send_sem, recv_sem, after_ref, src_out, land_out):
        cp = _sib_copy(src_ref, land_ref, send_sem, recv_sem)
        cp.wait_send()
        cp.wait_recv()

    return pl.pallas_call(
        body, name=name,
        out_shape=(pltpu.HBM(src.shape, src.dtype), pltpu.HBM(land.shape, land.dtype)),
        in_specs=(_HBM_SPEC, _HBM_SPEC, _SEM_SPEC, _SEM_SPEC, _ANY_SPEC),
        out_specs=(_HBM_SPEC, _HBM_SPEC),
        input_output_aliases={0: 0, 1: 1},
        compiler_params=pltpu.CompilerParams(has_side_effects=_EFFECT),
    )(src, land, send_sem, recv_sem, after)


def _sum_slabs(gp, recv, chip, name):
    _, R, C = gp.shape
    tr = _tile(R, PACK_ROWS, 16)

    def body(chip_ref, own_ref, r0_ref, r1_ref, r2_ref, o_ref):
        acc = own_ref[...].astype(F32) + r0_ref[...].astype(F32)
        o_ref[...] = (acc + r1_ref[...].astype(F32)) + r2_ref[...].astype(F32)

    def got(n):
        return pl.BlockSpec((None, tr, C), lambda i, chip_ref: (n, i, 0))

    return pl.pallas_call(
        body, name=name,
        grid_spec=pltpu.PrefetchScalarGridSpec(
            num_scalar_prefetch=1, grid=(R // tr,),
            in_specs=[pl.BlockSpec((None, tr, C), lambda i, chip_ref: (chip_ref[0], i, 0)), got(0), got(1), got(2)],
            out_specs=pl.BlockSpec((tr, C), lambda i, chip_ref: (i, 0))),
        out_shape=jax.ShapeDtypeStruct((R, C), F32),
        compiler_params=_params(("parallel",)),
    )(jnp.reshape(chip, (1,)).astype(jnp.int32), gp, recv, recv, recv)


def _all_reduce_vec(vec, name):
    VR, W = vec.shape

    def body(vec_ref, vall_ref, vout_ref, vsend_sems, vrecv_sems):
        x, y, c = _place()
        vall_ref[4 * x + 2 * y + c] = vec_ref[...]
        sends = []
        peers = []
        for r in range(1, N_DEV):
            dx, dy, dc = (r >> 2) & 1, (r >> 1) & 1, r & 1
            peer = (x ^ dx, y ^ dy, c ^ dc)
            peers.append(peer)
            cp = pltpu.make_async_remote_copy(
                src_ref=vec_ref, dst_ref=vall_ref.at[4 * x + 2 * y + c], send_sem=vsend_sems.at[r - 1],
                recv_sem=vrecv_sems.at[r - 1], device_id=peer, device_id_type=MESH)
            cp.start()
            sends.append(cp)
        for r, peer in enumerate(peers):
            pltpu.make_async_remote_copy(
                src_ref=vec_ref, dst_ref=vall_ref.at[4 * peer[0] + 2 * peer[1] + peer[2]],
                send_sem=vsend_sems.at[r], recv_sem=vrecv_sems.at[r],
                device_id=peer, device_id_type=MESH).wait_recv()
        total = vall_ref[0]
        for d in range(1, N_DEV):
            total = total + vall_ref[d]
        vout_ref[...] = total
        for cp in sends:
            cp.wait_send()

    outs = pl.pallas_call(
        body, name=name,
        in_specs=[_VMEM_SPEC], out_specs=[_VMEM_SPEC, _VMEM_SPEC],
        out_shape=[jax.ShapeDtypeStruct((N_DEV, VR, W), F32), jax.ShapeDtypeStruct((VR, W), F32)],
        scratch_shapes=[pltpu.SemaphoreType.DMA((N_DEV - 1,)), pltpu.SemaphoreType.DMA((N_DEV - 1,))],
    )(vec)
    return outs[1]


class _Pack:
    def __init__(self, group, C):
        self.group, self.C = group, C
        self.rows, self.offs, off = {}, {}, 0
        for nm, (K, N), _ in group:
            assert N <= C, nm
            self.rows[nm] = K if 2 * N > C else -(-(K * N) // C)
            self.offs[nm] = off
            off += -(-self.rows[nm] // 16) * 16
        self.used = off
        self.R = -(-off // PACK_ROWS) * PACK_ROWS

    def _rows_of(self, a):
        K, N = a.shape
        if 2 * N > self.C:
            a = jnp.pad(a, ((0, 0), (0, self.C - N)))
        else:
            a = jnp.pad(a.reshape(-1), (0, -(K * N) % self.C)).reshape(-1, self.C)
        return jnp.pad(a, ((0, -a.shape[0] % 16), (0, 0)))

    def pack(self, shards):
        parts = [self._rows_of(shards[nm].astype(BF16)) for nm, _, _ in self.group]
        return jnp.concatenate(parts + [jnp.zeros((self.R - self.used, self.C), BF16)], axis=0)

    def _shard_of(self, rows, shape):
        K, N = shape
        return rows[:, :N] if 2 * N > self.C else rows.reshape(-1)[:K * N].reshape(K, N)

    def part(self, flat, nm, shape):
        return self._shard_of(flat[self.offs[nm]:self.offs[nm] + self.rows[nm]], shape)

    def slab_rows(self, nm, g):
        (K, N), axis = next((shape, axis) for n, shape, axis in self.group if n == nm)
        cuts = [g[:, k * N:(k + 1) * N] if axis == 1 else g[k * K:(k + 1) * K, :] for k in range(N_CHIPS)]
        return jnp.stack([self._rows_of(c.astype(BF16)) for c in cuts])

    def slabs(self, grads):
        parts = [self.slab_rows(nm, grads[nm]) for nm, _, _ in self.group]
        return jnp.concatenate(parts + [jnp.zeros((N_CHIPS, self.R - self.used, self.C), BF16)], axis=1)

    def full(self, gathered, names=None):
        res = {}
        for nm, (K, N), axis in self.group:
            if names is None or nm in names:
                rows = gathered[:, self.offs[nm]:self.offs[nm] + self.rows[nm]]
                res[nm] = jnp.concatenate([self._shard_of(rows[k], (K, N)) for k in range(N_CHIPS)], axis=axis)
        return res


def _rope_tables(S):
    pos = jnp.arange(S, dtype=F32)
    inv = 1.0 / (ROPE_THETA ** (jnp.arange(0, MLA_ROPE, 2, dtype=F32) / MLA_ROPE))
    ang = pos[:, None] * inv[None, :]
    cos, sin = jnp.cos(ang), jnp.sin(ang)
    half = MLA_ROPE // 2
    z = jnp.zeros((S, half), F32)
    one = jnp.ones((S, LANE - MLA_ROPE), F32)
    zero = jnp.zeros((S, LANE - MLA_ROPE), F32)
    kc = jnp.concatenate([cos, cos, one], axis=1)
    ksa = jnp.concatenate([-sin, z, zero], axis=1)
    ksb = jnp.concatenate([z, sin, zero], axis=1)
    qc = jnp.concatenate([jnp.ones((S, MLA_NOPE), F32), kc], axis=1)
    qsa = jnp.concatenate([jnp.zeros((S, MLA_NOPE), F32), ksa], axis=1)
    qsb = jnp.concatenate([jnp.zeros((S, MLA_NOPE), F32), ksb], axis=1)
    return (kc, ksa, ksb), (qc, qsa, qsb)


def _pad_cols(a, width):
    return jnp.pad(a, ((0, 0), (0, width - a.shape[1])))


def kernel(x, attn_norm, w_in, fox_f_bias, q_norm, w_uq, kv_norm, w_ukv, w_mla_branch, w_fox_branch, w_out, mlp_norm, w_up, w_down, final_norm, loss_target, m_attn_norm, m_w_in, m_fox_f_bias, m_q_norm, m_w_uq, m_kv_norm, m_w_ukv, m_w_mla_branch, m_w_fox_branch, m_w_out, m_mlp_norm, m_w_up, m_w_down, m_final_norm, v_attn_norm, v_w_in, v_fox_f_bias, v_q_norm, v_w_uq, v_kv_norm, v_w_ukv, v_w_mla_branch, v_w_fox_branch, v_w_out, v_mlp_norm, v_w_up, v_w_down, v_final_norm):
    _, S, D = x.shape
    H, HF = MLA_HEADS, FOX_HEADS
    QL, KVL = MLA_Q_LORA, MLA_KV_LORA
    assert H == HF and H <= 8
    xs = x[0]
    target = loss_target[0]
    C = D
    chip = 2 * lax.axis_index("x") + lax.axis_index("y")

    def flip(a):
        return jnp.transpose(a, (0, 2, 1))

    w_in, m_w_in, v_w_in = flip(w_in), flip(m_w_in), flip(v_w_in)
    weights = {"attn_norm": attn_norm, "w_in": w_in, "fox_f_bias": fox_f_bias, "q_norm": q_norm, "w_uq": w_uq,
               "kv_norm": kv_norm, "w_ukv": w_ukv, "w_mla_branch": w_mla_branch, "w_fox_branch": w_fox_branch,
               "w_out": w_out, "mlp_norm": mlp_norm, "w_up": w_up, "w_down": w_down, "final_norm": final_norm}
    moments = {"attn_norm": (m_attn_norm, v_attn_norm), "w_in": (m_w_in, v_w_in), "fox_f_bias": (m_fox_f_bias, v_fox_f_bias),
               "q_norm": (m_q_norm, v_q_norm), "w_uq": (m_w_uq, v_w_uq), "kv_norm": (m_kv_norm, v_kv_norm),
               "w_ukv": (m_w_ukv, v_w_ukv), "w_mla_branch": (m_w_mla_branch, v_w_mla_branch),
               "w_fox_branch": (m_w_fox_branch, v_w_fox_branch), "w_out": (m_w_out, v_w_out),
               "mlp_norm": (m_mlp_norm, v_mlp_norm), "w_up": (m_w_up, v_w_up), "w_down": (m_w_down, v_w_down),
               "final_norm": (m_final_norm, v_final_norm)}

    def group(names_axes):
        return [(nm, weights[nm].shape[1:], axis) for nm, axis in names_axes]

    pack_a = _Pack(group([("w_in", 0), ("w_uq", 1), ("w_ukv", 1)]), C)
    pack_b = _Pack(group([("w_down", 0), ("w_up", 1), ("w_out", 0), ("w_mla_branch", 1), ("w_fox_branch", 1)]), C)
    RA, RB = pack_a.R, pack_b.R
    wp_a = pack_a.pack({nm: weights[nm][0] for nm, _, _ in pack_a.group})
    wp_b = pack_b.pack({nm: weights[nm][0] for nm, _, _ in pack_b.group})
    n_in = w_in.shape[1]
    rows_in = -(-n_in // 16) * 16
    assert pack_a.offs["w_in"] == 0 and all((k * n_in) % 16 + n_in <= rows_in for k in range(N_CHIPS))
    shifted = lax.dynamic_update_slice(jnp.zeros((rows_in, C), BF16), wp_a[:n_in], ((chip * n_in) % 16, 0))
    wp_a = jnp.concatenate([shifted, wp_a[rows_in:]], axis=0)
    ag_a = _xchg_start(wp_a, lax.empty((N_CHIPS, RA, C), BF16), "half", jnp.zeros((8, LANE), F32), "all_gather_start_a")
    xn = _norm_fwd(xs, attn_norm, "attn_norm_fwd", order=ag_a[3])
    own_a, land_a = _xchg_wait(ag_a, "half", (xn, wp_b), "all_gather_wait_a")
    land_a = _forward_halves(land_a, "all_gather_forward_a")
    gathered_a = lax.dynamic_update_slice(land_a, own_a[None], (chip, 0, 0))
    ag_b = _xchg_start(wp_b, lax.empty((N_CHIPS, RB, C), BF16), True, gathered_a, "all_gather_start_b")
    full = pack_a.full(gathered_a, ("w_uq", "w_ukv"))
    tile0 = [(k * n_in) // 16 * 16 for k in range(N_CHIPS)]
    total = tile0[-1] + rows_in
    full["w_in"] = sum(jnp.pad(gathered_a[k, :rows_in], ((tile0[k], total - tile0[k] - rows_in), (0, 0)))
                       for k in range(N_CHIPS))

    o_ckv = QL
    o_kr = o_ckv + KVL
    o_fq = o_kr + MLA_ROPE
    o_ff = o_fq + 3 * HF * FOX_HEAD_DIM
    o_g = o_ff + HF
    wi = full["w_in"]
    assert N_CHIPS * n_in == o_g + 2 * D and wi.shape[0] >= o_g + 2 * D
    WS = QL + KVL + 2 * LANE
    NQKV = 3 * HF * FOX_HEAD_DIM

    def pad_rows(a, rows):
        return jnp.pad(a, ((0, rows - a.shape[0]), (0, 0)))

    w_small = jnp.concatenate([wi[:o_kr], pad_rows(wi[o_kr:o_fq], LANE), pad_rows(wi[o_ff:o_g], LANE)], axis=0)
    w_qkv = wi[o_fq:o_ff]
    w_g = wi[o_g:o_g + 2 * D]
    w_pack = jnp.concatenate([w_small, w_qkv, w_g], axis=0)
    dqk = MLA_NOPE + MLA_ROPE
    w_uq_p = jnp.pad(full["w_uq"].reshape(QL, H, dqk), ((0, 0), (0, 0), (0, QPAD - dqk))).reshape(QL, H * QPAD)
    ukv = full["w_ukv"].reshape(KVL, H, MLA_NOPE + MLA_V)
    w_ukv_p = jnp.concatenate([ukv[:, :, :MLA_NOPE].reshape(KVL, H * MLA_NOPE),
                               ukv[:, :, MLA_NOPE:].reshape(KVL, H * MLA_V)], axis=1)

    (kc, ksa, ksb), (qc, qsa, qsb) = _rope_tables(S)
    bias_pad = _pad_cols(fox_f_bias, LANE)

    small = _matmul(xn, w_small, "nt", [F32], "proj_small")
    n_fq = HF * FOX_HEAD_DIM
    q_scale = jnp.concatenate([jnp.full((1, n_fq), LOG2E / math.sqrt(FOX_HEAD_DIM), F32),
                               jnp.ones((1, NQKV - n_fq), F32)], axis=1)
    qkv = _matmul(xn, w_qkv, "nt", [BF16], "proj_qkv", col_extras=(q_scale,), epilogue=lambda acc, cs: (acc * cs,))
    gpre = _matmul(xn, w_g, "nt", [F32], "proj_gates")
    cqn, ckvn, kr, cum = _prep_fwd(small, q_norm, kv_norm, bias_pad, kc, ksa, ksb, HF, "prep_fwd")
    c2_mla = LOG2E / math.sqrt(dqk)
    q_rot = _matmul(cqn, w_uq_p, "nn", [BF16], "mla_q_up", tn=QPAD, row_extras=(qc * c2_mla, qsa * c2_mla, qsb * c2_mla),
                    epilogue=lambda acc, c, sa, sb: (_rope(acc, c, sa, sb, 1),))
    kv2 = _matmul(ckvn, w_ukv_p, "nn", [BF16], "mla_kv_up")

    mla = _AttT(S, H, (q_rot, QPAD, 0, True), [(kv2, MLA_NOPE, 0, True), (kr, LANE, 0, False)],
                (kv2, MLA_V, H, True), 1.0 / math.sqrt(dqk), True)
    o_mla, lse_mla = _att_fwd_t(mla, "mla_att_fwd")

    cum_t = jnp.transpose(cum[:, :HF]) * LOG2E
    cum_rep = jnp.broadcast_to(cum_t[:, :, None], (HF, S, min(QSUB, _tile(S, ATT_T))))
    fox = _AttT(S, HF, (qkv, FOX_HEAD_DIM, 0, True), [(qkv, FOX_HEAD_DIM, HF, True)],
                (qkv, FOX_HEAD_DIM, 2 * HF, True), 1.0 / math.sqrt(FOX_HEAD_DIM), False, cum_rep)
    o_fox, ox_fox, lse_fox = _att_fwd_t(fox, "fox_att_fwd", exact=True)

    own_b, land_b = _xchg_wait(ag_b, True, (lse_fox, lse_mla, gpre), "all_gather_wait_b")
    gathered_b = lax.dynamic_update_slice(land_b, own_b[None], (chip, 0, 0))
    full.update(pack_b.full(gathered_b, ("w_mla_branch", "w_fox_branch", "w_out")))
    w_mb, w_fb, w_o = (full[n] for n in ("w_mla_branch", "w_fox_branch", "w_out"))

    def b_of(nm, mode, tn, tk):
        (K, N), axis = next((shape, axis) for n, shape, axis in pack_b.group if n == nm)
        off = pack_b.offs[nm]
        shape = (N_CHIPS * K, N) if axis == 0 else (K, N_CHIPS * N)
        t_r, t_c = (tk, tn) if mode == "nn" else (tn, tk)
        t_r, t_c = _tile(shape[0], t_r), _tile(shape[1], t_c)
        if not (N == C and K % t_r == 0 and N % t_c == 0 and off % t_r == 0):
            return pack_b.full(gathered_b, (nm,))[nm], None
        base = off // t_r
        if axis == 0:
            per = K // t_r
            place = lambda rb, cb: (rb // per, base + rb % per, cb)
        else:
            per = N // t_c
            place = lambda rb, cb: (cb // per, base + rb, cb % per)
        return gathered_b, (shape, (lambda j, k: place(k, j)) if mode == "nn" else (lambda j, k: place(j, k)))

    y_mla = _matmul(o_mla, w_mb, "nn", [F32], "mla_branch")

    def gate_merge(acc, ga, gb, ya):
        return acc, _sigmoid(ga) * ya + _sigmoid(gb) * acc

    y_fox, merged = _matmul(o_fox, w_fb, "nn", [F32, BF16], "fox_branch_gates", tn=512,
                            extras=((gpre, 0), (gpre, 1), y_mla), epilogue=gate_merge)
    h1 = _matmul(merged, w_o, "nn", [F32], "out_proj", extras=(xs,), epilogue=lambda acc, r: (acc + r,))
    hn = _norm_fwd(h1, mlp_norm, "mlp_norm_fwd")

    def relu2(acc):
        a = jnp.maximum(acc, 0.0)
        return a * a, a

    w_u, w_u_in = b_of("w_up", "nn", 1024, 2048)
    u, a_pos = _matmul(hn, w_u, "nn", [BF16, BF16], "mlp_up", epilogue=relu2, b_in=w_u_in)
    w_d, w_d_in = b_of("w_down", "nn", 1024, 2048)
    h2 = _matmul(u, w_d, "nn", [F32], "mlp_down", tn=1024, extras=(h1,), epilogue=lambda acc, r: (acc + r,),
                 b_in=w_d_in)
    dh2, dh2_b, g_final, loss_part = _final(h2, final_norm.reshape(1, D), target, "final_norm_loss")

    gp_b = lax.empty((N_CHIPS, RB, C), BF16)
    by_glue = {}

    def grad_b(nm, a, b, name):
        nonlocal gp_b
        (K, N), axis = next((shape, axis) for n, shape, axis in pack_b.group if n == nm)
        off = pack_b.offs[nm]
        tm = min(1024, K) if axis == 0 else min(1024, a.shape[1])
        tn = min(1024, N) if axis == 1 else min(1024, b.shape[1])
        if not (N == C and tm % LANE == 0 and tn % LANE == 0 and K % tm == 0 and N % tn == 0 and off % tm == 0):
            by_glue[nm] = _mm_tn(a, b, name)
            return
        base = off // tm
        if axis == 0:
            per = K // tm
            place = lambda i, j: (i // per, base + i % per, j)
        else:
            per = N // tn
            place = lambda i, j: (j // per, base + i, j % per)
        gp_b = _mm_tn(a, b, name, tm=tm, tn=tn, into=(gp_b, place))

    w_d, w_d_in = b_of("w_down", "nt", 1024, 2048)
    da = _matmul(dh2_b, w_d, "nt", [BF16], "mlp_down_dx", extras=(a_pos,),
                 epilogue=lambda acc, a: (acc * (2.0 * a.astype(F32)),), b_in=w_d_in)
    grad_b("w_down", u, dh2_b, "mlp_down_dw")
    w_u, w_u_in = b_of("w_up", "nt", 1024, 2048)
    dhn = _matmul(da, w_u, "nt", [F32], "mlp_up_dx", tn=1024, b_in=w_u_in)
    grad_b("w_up", hn, da, "mlp_up_dw")
    dh1, dh1_b, g_mlp_norm = _norm_bwd(h1, dhn, mlp_norm, dh2, "mlp_norm_bwd")

    def gate_bwd(acc, ga, gb, ya, yb):
        ga, gb = _sigmoid(ga), _sigmoid(gb)
        return acc * ga, acc * gb, acc * ya * (ga * (1.0 - ga)), acc * yb * (gb * (1.0 - gb))

    dy_mla, dy_fox, dg_mla, dg_fox = _matmul(dh1_b, w_o, "nt", [BF16] * 4, "out_proj_dx_gates", tn=512,
                                             extras=((gpre, 0), (gpre, 1), y_mla, y_fox), epilogue=gate_bwd)
    grad_b("w_out", merged, dh1_b, "out_proj_dw")
    do_mla = _matmul(dy_mla, w_mb, "nt", [BF16], "mla_branch_dx")
    grad_b("w_mla_branch", o_mla, dy_mla, "mla_branch_dw")
    do_fox = _matmul(dy_fox, w_fb, "nt", [BF16], "fox_branch_dx")
    grad_b("w_fox_branch", o_fox, dy_fox, "fox_branch_dw")
    for nm, g in by_glue.items():
        gp_b = lax.dynamic_update_slice(gp_b, pack_b.slab_rows(nm, g), (0, pack_b.offs[nm], 0))
    if RB > pack_b.used:
        gp_b = lax.dynamic_update_slice(gp_b, jnp.zeros((N_CHIPS, RB - pack_b.used, C), BF16), (0, pack_b.used, 0))

    rs_b = _xchg_start(gp_b, lax.empty((3, RB, C), BF16), False, do_fox, "grad_scatter_start_b")

    dq_rot, dk_nope, dkr_heads, dv_mla = _att_bwd_t(mla, do_mla, lse_mla, o_mla, BF16, [BF16, F32],
                                                    "mla_att_bwd", dq_rope=(qc, qsa, qsb), order=rs_b[3])
    dfq, dfk, dfv, dcum = _att_bwd_t(fox, do_fox, lse_fox, ox_fox, BF16, [BF16], "fox_att_bwd")

    gp_b_sent, recv_b = _xchg_wait(rs_b, False, (dfq, dq_rot), "grad_scatter_wait_b")
    swap_b = _sib_start(_sum_slabs(gp_b_sent, recv_b, chip, "grad_sum_b"), "grad_swap_start_b")

    dcqn = _matmul(dq_rot, w_uq_p, "nt", [F32], "mla_q_up_dx", order=swap_b[4])
    g_w_uq_p = _mm_tn(cqn, dq_rot, "mla_q_up_dw")
    dkv2 = jnp.concatenate([dk_nope, dv_mla], axis=1)
    dckvn = _matmul(dkv2, w_ukv_p, "nt", [F32], "mla_kv_up_dx")
    g_w_ukv_p = _mm_tn(ckvn, dkv2, "mla_kv_up_dw")

    dcum_rows = jnp.pad(dcum[:, :, 0], ((0, 8 - HF), (0, 0)))
    dlogf_rows = _suffix_sum_rows(dcum_rows, "fox_forget_suffix_sum")
    dlogf = _pad_cols(jnp.transpose(dlogf_rows[:HF]), LANE)
    d_small, g_q_norm, g_kv_norm, g_bias = _prep_bwd(
        small, dcqn, dckvn, dkr_heads, dlogf, q_norm, kv_norm, bias_pad, kc, ksa, ksb, H, "prep_bwd")
    dproj = [d_small, dfq, dfk, dfv, dg_mla, dg_fox]
    gs, gfq, gfk, gfv, gg_mla, gg_fox = [
        _matmul(part, xn, "tn", [BF16], "proj_dw_" + tag, tm=1024, tn=1024, tk=2048)
        for part, tag in zip(dproj, ("small", "fq", "fk", "fv", "g_mla", "g_fox"))]

    g_w_in = jnp.concatenate([gs[:o_kr], gs[o_kr:o_kr + MLA_ROPE], gfq, gfk, gfv,
                              gs[o_kr + LANE:o_kr + LANE + HF], gg_mla, gg_fox], axis=0)
    g_w_uq = g_w_uq_p.reshape(QL, H, QPAD)[:, :, :dqk].reshape(QL, H * dqk)
    g_w_ukv = jnp.concatenate([g_w_ukv_p[:, :H * MLA_NOPE].reshape(KVL, H, MLA_NOPE),
                               g_w_ukv_p[:, H * MLA_NOPE:].reshape(KVL, H, MLA_V)], axis=2).reshape(KVL, -1)

    gp_a = pack_a.slabs({"w_in": g_w_in, "w_uq": g_w_uq, "w_ukv": g_w_ukv})
    rs_a = _xchg_start(gp_a, lax.empty((3, RA, C), BF16), False, gg_fox, "grad_scatter_start_a")
    dxn = _matmul_parts(dproj, w_pack, "nn", F32, "proj_dx", tm=512, tn=2048, order=rs_a[3])
    grad_x, g_attn_norm = _norm_bwd(xs, dxn, attn_norm, dh1, "attn_norm_bwd", with_bf16=False)
    gp_a_sent, recv_a = _xchg_wait(rs_a, False, grad_x, "grad_scatter_wait_a")
    swap_a = _sib_start(_sum_slabs(gp_a_sent, recv_a, chip, "grad_sum_a"), "grad_swap_start_a")
    vec_w = max(D, LANE)
    vec_rows = [g_attn_norm, g_mlp_norm, g_final, g_q_norm, g_kv_norm, g_bias, loss_part]
    vec = jnp.concatenate([_pad_cols(v, vec_w) for v in vec_rows] + [jnp.zeros((1, vec_w), F32)], axis=0)
    vsum = _all_reduce_vec(vec, "all_reduce_vectors")
    part_b, sib_b = _sib_wait(swap_b, vsum, "grad_swap_wait_b")

    grads, deltas, new_m, new_v = {}, {}, {}, {}

    def update(pack, mine, theirs):
        for nm, shape, _ in pack.group:
            K, N = shape
            if N == pack.C and K % 8 == 0 and pack.offs[nm] % _tile(K, 256, 8) == 0:
                g, d, nm_, nv_ = _adamw(weights[nm], mine, theirs, moments[nm][0], moments[nm][1], "adamw_" + nm,
                                        g_row=pack.offs[nm])
            else:
                g, d, nm_, nv_ = _adamw(weights[nm], pack.part(mine, nm, shape), pack.part(theirs, nm, shape),
                                        moments[nm][0], moments[nm][1], "adamw_" + nm)
            grads[nm], deltas[nm], new_m[nm], new_v[nm] = g, d, nm_, nv_
        return g

    last_b = update(pack_b, part_b, sib_b)
    part_a, sib_a = _sib_wait(swap_a, last_b, "grad_swap_wait_a")
    update(pack_a, part_a, sib_a)

    vec_names = ["attn_norm", "mlp_norm", "final_norm", "q_norm", "kv_norm", "fox_f_bias"]

    def vec_pack(arrs):
        return jnp.concatenate([_pad_cols(a.reshape(1, -1), vec_w) for a in arrs]
                               + [jnp.zeros((2, vec_w), F32)], axis=0)[None]

    vg, vd, vm, vv = _adamw(vec_pack([weights[n] for n in vec_names]), vsum, jnp.zeros_like(vsum),
                            vec_pack([moments[n][0] for n in vec_names]), vec_pack([moments[n][1] for n in vec_names]),
                            "adamw_vectors")
    for r, nm in enumerate(vec_names):
        shp = weights[nm].shape
        n = weights[nm].size
        grads[nm] = vsum[r, :n].reshape(shp)
        deltas[nm], new_m[nm], new_v[nm] = (vd[0, r, :n].reshape(shp), vm[0, r, :n].reshape(shp),
                                            vv[0, r, :n].reshape(shp))
    loss = vsum[6, 0]

    for res in (grads, deltas, new_m, new_v):
        res["w_in"] = flip(res["w_in"])
    order = ["attn_norm", "w_in", "fox_f_bias", "q_norm", "w_uq", "kv_norm", "w_ukv", "w_mla_branch", "w_fox_branch",
             "w_out", "mlp_norm", "w_up", "w_down", "final_norm"]
    return (loss, grad_x[None], *[grads[n] for n in order], *[deltas[n] for n in order],
            *[new_m[n] for n in order], *[new_v[n] for n in order])
```

```python
import math

import jax
import jax.numpy as jnp
from jax import lax
from jax.experimental import pallas as pl
from jax.experimental.pallas import tpu as pltpu

CHUNK = 64
MLA_HEADS = 8
MLA_Q_LORA = 512
MLA_KV_LORA = 256
MLA_NOPE = 128
MLA_ROPE = 64
MLA_V = 128
ROPE_THETA = 10000.0
FOX_HEADS = 8
FOX_HEAD_DIM = 128
EPS = 1e-6

ADAM_LR = 0.001
ADAM_B1 = 0.9
ADAM_B2 = 0.999
ADAM_EPS = 1e-08
ADAM_WD = 0.01
ADAM_STEP = 10

LANE = 128
QPAD = 2 * LANE
N_CHIPS = 4
N_DEV = 8
VMEM_LIMIT = 48 * 1024 * 1024
ATT_T = 2048
QSUB = 256
ROW_T = 256
PACK_ROWS = 256
LOG2E = 1.4426950408889634

BF16 = jnp.bfloat16
F32 = jnp.float32
MESH = pl.DeviceIdType.MESH

_NT = (((1,), (1,)), ((), ()))
_TN = (((0,), (0,)), ((), ()))
_NN = (((1,), (0,)), ((), ()))


def _tile(dim, pref, align=LANE):
    if dim <= pref:
        return dim
    t = (pref // align) * align
    while t >= align:
        if dim % t == 0:
            return t
        t -= align
    return dim


def _params(sem=None):
    return pltpu.CompilerParams(dimension_semantics=sem, vmem_limit_bytes=VMEM_LIMIT)


_ANY_SPEC = pl.BlockSpec(memory_space=pl.ANY)


def _matmul(a, b, mode, out_dtypes, name, *, tm=1024, tn=1024, tk=2048, extras=(), row_extras=(), col_extras=(),
            epilogue=None, order=None, into=None, b_in=None):
    b_shape = b.shape if b_in is None else b_in[0]
    if mode == "nn":
        (M, K), (K2, N) = a.shape, b_shape
    elif mode == "nt":
        (M, K), (N, K2) = a.shape, b_shape
    else:
        (K, M), (K2, N) = a.shape, b_shape
    assert K == K2, (name, a.shape, b_shape)
    tm, tn, tk = _tile(M, tm), _tile(N, tn), _tile(K, tk)
    nk = K // tk
    extras = [e if isinstance(e, tuple) else (e, 0) for e in extras]
    n_out = len(out_dtypes)
    n_ex = len(extras) + len(row_extras) + len(col_extras)
    n_ord = 0 if order is None else 1
    assert all(r.shape == (M, tn) for r in row_extras), name
    dims = {"nn": _NN, "nt": _NT, "tn": _TN}[mode]

    def body(*refs):
        a_ref, b_ref = refs[0], refs[1]
        ex_refs = refs[2:2 + n_ex]
        o_refs = refs[2 + n_ex + n_ord:2 + n_ex + n_ord + n_out]
        acc_ref = refs[2 + n_ex + n_ord + n_out]
        k = pl.program_id(2)
        part = lax.dot_general(a_ref[...], b_ref[...], dims, preferred_element_type=F32)

        @pl.when(k == 0)
        def _():
            acc_ref[...] = part

        @pl.when(k > 0)
        def _():
            acc_ref[...] += part

        @pl.when(k == nk - 1)
        def _():
            acc = acc_ref[...]
            if epilogue is None:
                outs = (acc,)
            else:
                outs = epilogue(acc, *[r[...] for r in ex_refs])
            for o_ref, o in zip(o_refs, outs):
                o_ref[...] = o.astype(o_ref.dtype)

    if mode == "nn":
        a_spec = pl.BlockSpec((tm, tk), lambda i, j, k: (i, k))
        b_spec = pl.BlockSpec((tk, tn), lambda i, j, k: (k, j))
    elif mode == "nt":
        a_spec = pl.BlockSpec((tm, tk), lambda i, j, k: (i, k))
        b_spec = pl.BlockSpec((tn, tk), lambda i, j, k: (j, k))
    else:
        a_spec = pl.BlockSpec((tk, tm), lambda i, j, k: (k, i))
        b_spec = pl.BlockSpec((tk, tn), lambda i, j, k: (k, j))
    if b_in is not None:
        b_block = (None, tn, tk) if mode == "nt" else (None, tk, tn)
        b_spec = pl.BlockSpec(b_block, lambda i, j, k: b_in[1](j, k))
    mn_spec = pl.BlockSpec((tm, tn), lambda i, j, k: (i, j))
    row_spec = pl.BlockSpec((tm, tn), lambda i, j, k: (i, 0))
    col_spec = pl.BlockSpec((1, tn), lambda i, j, k: (0, j))
    out_specs = [mn_spec] * n_out
    out_shape = [jax.ShapeDtypeStruct((M, N), dt) for dt in out_dtypes]
    aliases = {}
    if into is not None:
        buf, place = into
        assert n_out == 1 and n_ord == 1 and order is buf, name
        out_specs = [pl.BlockSpec((None, tm, tn), lambda i, j, k: place(i, j))]
        out_shape = [jax.ShapeDtypeStruct(buf.shape, buf.dtype)]
        aliases = {2 + n_ex: 0}
    outs = pl.pallas_call(
        body,
        name=name,
        grid=(M // tm, N // tn, nk),
        in_specs=([a_spec, b_spec]
                  + [pl.BlockSpec((tm, tn), lambda i, j, k, g=g: (i, j + g * (N // tn))) for _, g in extras]
                  + [row_spec] * len(row_extras) + [col_spec] * len(col_extras) + [_ANY_SPEC] * n_ord),
        out_specs=out_specs,
        out_shape=out_shape,
        scratch_shapes=[pltpu.VMEM((tm, tn), F32)],
        input_output_aliases=aliases,
        compiler_params=_params(("parallel", "parallel", "arbitrary")),
    )(a, b, *[e for e, _ in extras], *row_extras, *col_extras, *([] if order is None else [order]))
    return outs[0] if n_out == 1 else outs


def _matmul_parts(parts, b, mode, out_dtype, name, *, tm=1024, tn=1024, tk=1024, order=None):
    assert mode in ("nn", "tn")
    if mode == "nn":
        M, (K, N) = parts[0].shape[0], b.shape
        widths = [p.shape[1] for p in parts]
    else:
        K, N = b.shape
        widths = [p.shape[1] for p in parts]
        M = sum(widths)
    common = math.gcd(*widths)
    tm, tn, tk = _tile(M if mode == "nn" else common, tm), _tile(N, tn), _tile(common if mode == "nn" else K, tk)
    t_part = tk if mode == "nn" else tm
    assert sum(widths) == (K if mode == "nn" else M), name
    if any(w % t_part for w in widths):
        parts, widths = [jnp.concatenate(parts, axis=1)], [sum(widths)]
    lo =[sum(widths[:p]) // t_part for p in range(len(parts))]
    cnt = [w // t_part for w in widths]
    nk = K // tk
    n_parts = len(parts)
    n_ord = 0 if order is None else 1
    dims = _NN if mode == "nn" else _TN

    def body(*refs):
        a_refs = refs[0:n_parts]
        b_ref = refs[n_parts]
        o_ref, acc_ref = refs[n_parts + 1 + n_ord], refs[n_parts + 2 + n_ord]
        i, k = pl.program_id(0), pl.program_id(2)
        sel = k if mode == "nn" else i
        for p in range(n_parts):
            @pl.when((sel >= lo[p]) & (sel < lo[p] + cnt[p]))
            def _(p=p):
                part = lax.dot_general(a_refs[p][...], b_ref[...], dims, preferred_element_type=F32)

                @pl.when(k == 0)
                def _():
                    acc_ref[...] = part

                @pl.when(k > 0)
                def _():
                    acc_ref[...] += part

        @pl.when(k == nk - 1)
        def _():
            o_ref[...] = acc_ref[...].astype(o_ref.dtype)

    def a_spec(p):
        if mode == "nn":
            return pl.BlockSpec((tm, tk), lambda i, j, k: (i, jnp.clip(k - lo[p], 0, cnt[p] - 1)))
        return pl.BlockSpec((tk, tm), lambda i, j, k: (
            jnp.where((i >= lo[p]) & (i < lo[p] + cnt[p]), k, 0), jnp.clip(i - lo[p], 0, cnt[p] - 1)))

    return pl.pallas_call(
        body, name=name, grid=(M // tm, N // tn, nk),
        in_specs=[a_spec(p) for p in range(n_parts)] + [pl.BlockSpec((tk, tn), lambda i, j, k: (k, j))]
        + [_ANY_SPEC] * n_ord,
        out_specs=pl.BlockSpec((tm, tn), lambda i, j, k: (i, j)),
        out_shape=jax.ShapeDtypeStruct((M, N), out_dtype),
        scratch_shapes=[pltpu.VMEM((tm, tn), F32)],
        compiler_params=_params(("parallel", "parallel", "arbitrary")),
    )(*parts, b, *([] if order is None else [order]))


def _mm_tn(a, b, name, tm=1024, tn=1024, into=None):
    return _matmul(a, b, "tn", [F32], name, tm=tm, tn=tn, tk=2048, into=into,
                   order=None if into is None else into[0])


def _row_spec(ts, width, col=0):
    return pl.BlockSpec((ts, width), lambda i: (i, col))


def _full_spec(shape):
    return pl.BlockSpec(shape, lambda i: tuple(0 for _ in shape))


def _rms(x):
    return lax.rsqrt(jnp.mean(x * x, axis=-1, keepdims=True) + EPS)


def _rms_bwd(x, dy, g):
    r = _rms(x)
    xh = x * r
    gy = dy * g
    dx = r * (gy - xh * jnp.mean(xh * gy, axis=-1, keepdims=True))
    return dx, dy * xh


def _norm_fwd(x, g, name, order=None):
    S, D = x.shape
    ts = _tile(S, ROW_T, 8)

    def body(x_ref, g_ref, *rest):
        o_ref = rest[-1]
        xv = x_ref[...]
        o_ref[...] = ((xv * _rms(xv)) * g_ref[...]).astype(BF16)

    extra = [] if order is None else [order]
    return pl.pallas_call(
        body, name=name, grid=(S // ts,),
        in_specs=[_row_spec(ts, D), _full_spec((1, D))] + [_ANY_SPEC] * len(extra),
        out_specs=_row_spec(ts, D),
        out_shape=jax.ShapeDtypeStruct((S, D), BF16),
        compiler_params=_params(("parallel",)),
    )(x, g, *extra)


def _norm_bwd(x, dy, g, dres, name, with_bf16=True):
    S, D = x.shape
    ts = _tile(S, ROW_T, 8)

    def body(x_ref, dy_ref, g_ref, dres_ref, dx_ref, *rest):
        dg_ref = rest[-1]
        dx, dg_rows = _rms_bwd(x_ref[...], dy_ref[...], g_ref[...])
        dx = dres_ref[...] + dx
        dx_ref[...] = dx
        if with_bf16:
            rest[0][...] = dx.astype(BF16)

        @pl.when(pl.program_id(0) == 0)
        def _():
            dg_ref[...] = jnp.zeros_like(dg_ref)

        dg_ref[...] += jnp.sum(dg_rows, axis=0, keepdims=True)

    return pl.pallas_call(
        body, name=name, grid=(S // ts,),
        in_specs=[_row_spec(ts, D), _row_spec(ts, D), _full_spec((1, D)), _row_spec(ts, D)],
        out_specs=[_row_spec(ts, D)] * (2 if with_bf16 else 1) + [_full_spec((1, D))],
        out_shape=([jax.ShapeDtypeStruct((S, D), F32)] + [jax.ShapeDtypeStruct((S, D), BF16)] * with_bf16
                   + [jax.ShapeDtypeStruct((1, D), F32)]),
        compiler_params=_params(("arbitrary",)),
    )(x, dy, g, dres)


def _rope(x, c, sa, sb, sign):
    w = x.shape[-1]
    half = MLA_ROPE // 2
    fwd = pltpu.roll(x, w - half, 1)
    back = pltpu.roll(x, half, 1)
    if sign < 0:
        return x * c - fwd * sa - back * sb
    return x * c + fwd * sa + back * sb


def _split3(x):
    hi = x.astype(BF16)
    r1 = x - hi.astype(F32)
    mid = r1.astype(BF16)
    lo = (r1 - mid.astype(F32)).astype(BF16)
    return hi, mid, lo


def _prep_fwd(small, q_norm, kv_norm, bias_pad, kc, ksa, ksb, n_heads, name):
    S, W = small.shape
    QL, KVL = q_norm.shape[1], kv_norm.shape[1]
    assert W == QL + KVL + 2 * LANE
    ts = _tile(S, ROW_T, 8)
    tri = (lax.broadcasted_iota(jnp.int32, (ts, ts), 0) >= lax.broadcasted_iota(jnp.int32, (ts, ts), 1)).astype(BF16)

    def body(s_ref, qn_ref, kvn_ref, b_ref, kc_ref, ksa_ref, ksb_ref, tri_ref,
             cqn_ref, ckvn_ref, kr_ref, cum_ref, carry_ref):
        cq = s_ref[:, 0:QL]
        cqn_ref[...] = ((cq * _rms(cq)) * qn_ref[...]).astype(BF16)
        ckv = s_ref[:, QL:QL + KVL]
        ckvn_ref[...] = ((ckv * _rms(ckv)) * kvn_ref[...]).astype(BF16)
        kr = s_ref[:, QL + KVL:QL + KVL + LANE]
        kr_ref[...] = _rope(kr, kc_ref[...], ksa_ref[...], ksb_ref[...], 1).astype(BF16)
        z = s_ref[:, QL + KVL + LANE:W] + b_ref[...]
        logf = jnp.minimum(z, 0.0) - jnp.log1p(jnp.exp(-jnp.abs(z)))
        lane = lax.broadcasted_iota(jnp.int32, logf.shape, 1)
        logf = jnp.where(lane < n_heads, logf, 0.0)

        @pl.when(pl.program_id(0) == 0)
        def _():
            carry_ref[...] = jnp.zeros_like(carry_ref)

        t = tri_ref[...]
        cum = carry_ref[...]
        for part in _split3(logf):
            cum = cum + jnp.dot(t, part, preferred_element_type=F32)
        cum_ref[...] = cum
        carry_ref[...] = cum[ts - 1:ts, :]

    return pl.pallas_call(
        body, name=name, grid=(S // ts,),
        in_specs=[_row_spec(ts, W), _full_spec((1, QL)), _full_spec((1, KVL)), _full_spec((1, LANE)),
                  _row_spec(ts, LANE), _row_spec(ts, LANE), _row_spec(ts, LANE), _full_spec((ts, ts))],
        out_specs=[_row_spec(ts, QL), _row_spec(ts, KVL), _row_spec(ts, LANE), _row_spec(ts, LANE)],
        out_shape=[jax.ShapeDtypeStruct((S, QL), BF16), jax.ShapeDtypeStruct((S, KVL), BF16),
                   jax.ShapeDtypeStruct((S, LANE), BF16), jax.ShapeDtypeStruct((S, LANE), F32)],
        scratch_shapes=[pltpu.VMEM((1, LANE), F32)],
        compiler_params=_params(("arbitrary",)),
    )(small, q_norm, kv_norm, bias_pad, kc, ksa, ksb, tri)


def _prep_bwd(small, dcqn, dckvn, dkr_heads, dlogf, q_norm, kv_norm, bias_pad, kc, ksa, ksb, n_heads, name):
    S, W = small.shape
    QL, KVL = q_norm.shape[1], kv_norm.shape[1]
    ts = _tile(S, ROW_T, 8)

    def body(s_ref, dcq_ref, dckv_ref, dkr_ref, dlf_ref, qn_ref, kvn_ref, b_ref, kc_ref, ksa_ref, ksb_ref,
             ds_ref, gq_ref, gkv_ref, gb_ref):
        dcq, gq_rows = _rms_bwd(s_ref[:, 0:QL], dcq_ref[...], qn_ref[...])
        ds_ref[:, 0:QL] = dcq.astype(BF16)
        dckv, gkv_rows = _rms_bwd(s_ref[:, QL:QL + KVL], dckv_ref[...], kvn_ref[...])
        ds_ref[:, QL:QL + KVL] = dckv.astype(BF16)
        dkr = dkr_ref[:, 0:LANE]
        for h in range(1, n_heads):
            dkr = dkr + dkr_ref[:, h * LANE:(h + 1) * LANE]
        ds_ref[:, QL + KVL:QL + KVL + LANE] = _rope(dkr, kc_ref[...], ksa_ref[...], ksb_ref[...], -1).astype(BF16)
        z = s_ref[:, QL + KVL + LANE:W] + b_ref[...]
        dff = dlf_ref[...] * (1.0 / (1.0 + jnp.exp(z)))
        ds_ref[:, QL + KVL + LANE:W] = dff.astype(BF16)

        @pl.when(pl.program_id(0) == 0)
        def _():
            gq_ref[...] = jnp.zeros_like(gq_ref)
            gkv_ref[...] = jnp.zeros_like(gkv_ref)
            gb_ref[...] = jnp.zeros_like(gb_ref)

        gq_ref[...] += jnp.sum(gq_rows, axis=0, keepdims=True)
        gkv_ref[...] += jnp.sum(gkv_rows, axis=0, keepdims=True)
        gb_ref[...] += jnp.sum(dff, axis=0, keepdims=True)

    return pl.pallas_call(
        body, name=name, grid=(S // ts,),
        in_specs=[_row_spec(ts, W), _row_spec(ts, QL), _row_spec(ts, KVL), _row_spec(ts, n_heads * LANE),
                  _row_spec(ts, LANE), _full_spec((1, QL)), _full_spec((1, KVL)), _full_spec((1, LANE)),
                  _row_spec(ts, LANE), _row_spec(ts, LANE), _row_spec(ts, LANE)],
        out_specs=[_row_spec(ts, W), _full_spec((1, QL)), _full_spec((1, KVL)), _full_spec((1, LANE))],
        out_shape=[jax.ShapeDtypeStruct((S, W), BF16), jax.ShapeDtypeStruct((1, QL), F32),
                   jax.ShapeDtypeStruct((1, KVL), F32), jax.ShapeDtypeStruct((1, LANE), F32)],
        compiler_params=_params(("arbitrary",)),
    )(small, dcqn, dckvn, dkr_heads, dlogf, q_norm, kv_norm, bias_pad, kc, ksa, ksb)


def _sigmoid(z):
    return 1.0 / (1.0 + jnp.exp(-z))


def _final(h, g, target, name):
    S, D = h.shape
    ts = _tile(S, ROW_T, 8)

    def body(h_ref, g_ref, t_ref, dh_ref, dhb_ref, dg_ref, loss_ref):
        hv = h_ref[...]
        gv = g_ref[...]
        err = (hv * _rms(hv)) * gv - t_ref[...]
        dh, dg_rows = _rms_bwd(hv, err / D, gv)
        dh_ref[...] = dh
        dhb_ref[...] = dh.astype(BF16)

        @pl.when(pl.program_id(0) == 0)
        def _():
            dg_ref[...] = jnp.zeros_like(dg_ref)
            loss_ref[...] = jnp.zeros_like(loss_ref)

        dg_ref[...] += jnp.sum(dg_rows, axis=0, keepdims=True)
        row_loss = jnp.mean(err * err, axis=-1, keepdims=True)
        loss_ref[...] += 0.5 * jnp.sum(row_loss, axis=0, keepdims=True)

    return pl.pallas_call(
        body, name=name, grid=(S // ts,),
        in_specs=[_row_spec(ts, D), _full_spec((1, D)), _row_spec(ts, D)],
        out_specs=[_row_spec(ts, D), _row_spec(ts, D), _full_spec((1, D)), _full_spec((1, LANE))],
        out_shape=[jax.ShapeDtypeStruct((S, D), F32), jax.ShapeDtypeStruct((S, D), BF16),
                   jax.ShapeDtypeStruct((1, D), F32), jax.ShapeDtypeStruct((1, LANE), F32)],
        compiler_params=_params(("arbitrary",)),
    )(h, g, target)


def _suffix_sum_rows(x, name):
    R, S = x.shape
    tb = _tile(S, 512)
    nb = S // tb
    tri = (lax.broadcasted_iota(jnp.int32, (tb, tb), 0) >= lax.broadcasted_iota(jnp.int32, (tb, tb), 1)).astype(BF16)

    def body(x_ref, tri_ref, o_ref, carry_ref):
        @pl.when(pl.program_id(0) == 0)
        def _():
            carry_ref[...] = jnp.zeros_like(carry_ref)

        xv = x_ref[...]
        t = tri_ref[...]
        acc = jnp.broadcast_to(carry_ref[:, 0:1], xv.shape)
        for part in _split3(xv):
            acc = acc + jnp.dot(part, t, preferred_element_type=F32)
        o_ref[...] = acc
        carry_ref[...] = jnp.broadcast_to(acc[:, 0:1], carry_ref.shape)

    rev = pl.BlockSpec((R, tb), lambda i: (0, nb - 1 - i))
    return pl.pallas_call(
        body, name=name, grid=(nb,),
        in_specs=[rev, _full_spec((tb, tb))], out_specs=rev,
        out_shape=jax.ShapeDtypeStruct((R, S), F32),
        scratch_shapes=[pltpu.VMEM((R, LANE), F32)],
        compiler_params=_params(("arbitrary",)),
    )(x, tri)


def _pairs(nb, by_key):
    if by_key:
        pr = [(i, j) for j in range(nb) for i in range(j, nb)]
    else:
        pr = [(i, j) for i in range(nb) for j in range(i + 1)]
    return (jnp.asarray([p[0] for p in pr], jnp.int32), jnp.asarray([p[1] for p in pr], jnp.int32), len(pr))


class _AttT:
    def __init__(self, S, n_heads, q, ks, v, scale, chunk_causal, cum_rep=None, qsub=None):
        self.S, self.H, self.q, self.ks, self.v = S, n_heads, q, ks, v
        self.scale, self.chunk_causal, self.cum_rep = scale, chunk_causal, cum_rep
        self.T = _tile(S, ATT_T)
        self.qs = min(qsub or QSUB, self.T)
        self.nb = S // self.T
        self.dq, self.dv = q[1], v[1]
        self.has_bias = cum_rep is not None

    def q_spec(self, op):
        _, w, off, per_head = op
        return pl.BlockSpec((self.T, w), lambda h, p, it, jt: (it[p], off + (h if per_head else 0)))

    def k_spec(self, op):
        _, w, off, per_head = op
        return pl.BlockSpec((self.T, w), lambda h, p, it, jt: (jt[p], off + (h if per_head else 0)))

    def row_q(self):
        return pl.BlockSpec((None, 1, self.T), lambda h, p, it, jt: (h, 0, it[p]))

    def cum_k(self):
        return pl.BlockSpec((None, self.T, self.qs), lambda h, p, it, jt: (h, jt[p], 0))

    def sub_blocks(self, masked):
        return [(q0, min(self.T, q0 + self.qs) if masked else self.T) for q0 in range(0, self.T, self.qs)]

    def scores(self, k, q_sub, cum, q0, masked):
        s = lax.dot_general(k, q_sub, _NT, preferred_element_type=F32)
        if self.has_bias:
            s = s - cum
        mask = None
        if masked:
            r = lax.broadcasted_iota(jnp.int32, s.shape, 0)
            c = lax.broadcasted_iota(jnp.int32, s.shape, 1) + q0
            mask = (r // CHUNK <= c // CHUNK) if self.chunk_causal else (r <= c)
        return s, mask


def _join(k_refs):
    return k_refs[0][...] if len(k_refs) == 1 else jnp.concatenate([r[...] for r in k_refs], axis=-1)


def _att_fwd_t(att, name, exact=False):
    S, H, T, qs = att.S, att.H, att.T, att.qs
    it, jt, npairs = _pairs(att.nb, by_key=False)
    nk = len(att.ks)

    def body(it_ref, jt_ref, *refs):
        q_ref = refs[0]
        k_refs = refs[1:1 + nk]
        v_ref = refs[1 + nk]
        n = 2 + nk
        cum_ref = None
        if att.has_bias:
            cum_ref = refs[n]
            n += 1
        o_ref = refs[n]
        n += 1
        ox_ref = None
        if exact:
            ox_ref = refs[n]
            n += 1
        lse_ref, m_ref, l_ref, acc_ref = refs[n:n + 4]
        lo_ref = refs[n + 4] if exact else None
        p = pl.program_id(1)
        i, j = it_ref[p], jt_ref[p]

        @pl.when(j == 0)
        def _():
            m_ref[...] = jnp.full_like(m_ref, -jnp.inf)
            l_ref[...] = jnp.zeros_like(l_ref)
            acc_ref[...] = jnp.zeros_like(acc_ref)
            if exact:
                lo_ref[...] = jnp.zeros_like(lo_ref)

        def step(masked):
            k = _join(k_refs)
            v = v_ref[...]
            subs = att.sub_blocks(masked)

            def logits(idx):
                q0, nkeys = subs[idx]
                cum = cum_ref[0:nkeys, :] if att.has_bias else None
                return att.scores(k[0:nkeys], q_ref[q0:q0 + qs, :], cum, q0, masked)

            ahead = logits(0)
            for idx, (q0, nkeys) in enumerate(subs):
                qsl = slice(q0, q0 + qs)
                s, mask = ahead
                if idx + 1 < len(subs):
                    ahead = logits(idx + 1)
                if masked:
                    s = jnp.where(mask, s, -jnp.inf)
                m_prev = m_ref[:, qsl]
                m_new = jnp.maximum(m_prev, jnp.max(s, axis=0, keepdims=True))
                alpha = jnp.exp2(m_prev - m_new)
                pr = jnp.exp2(s - m_new)
                l_ref[:, qsl] = alpha * l_ref[:, qsl] + jnp.sum(pr, axis=0, keepdims=True)
                p_hi = pr.astype(BF16)
                acc_ref[:, qsl] = alpha * acc_ref[:, qsl] + lax.dot_general(
                    v[0:nkeys], p_hi, _TN, preferred_element_type=F32)
                if exact:
                    p_lo = (pr - p_hi.astype(F32)).astype(BF16)
                    lo_ref[:, qsl] = alpha * lo_ref[:, qsl] + lax.dot_general(
                        v[0:nkeys], p_lo, _TN, preferred_element_type=F32)
                m_ref[:, qsl] = m_new

        @pl.when(j < i)
        def _():
            step(False)

        @pl.when(j == i)
        def _():
            step(True)
            l = l_ref[...]
            inv = 1.0 / l
            o_ref[...] = jnp.transpose(acc_ref[...] * inv).astype(o_ref.dtype)
            if exact:
                ox_ref[...] = jnp.transpose((acc_ref[...] + lo_ref[...]) * inv)
            lse_ref[...] = m_ref[...] + jnp.log2(l)

    in_specs = [att.q_spec(att.q)] + [att.k_spec(k) for k in att.ks] + [att.k_spec(att.v)]
    args = [att.q[0]] + [k[0] for k in att.ks] + [att.v[0]]
    if att.has_bias:
        in_specs.append(att.cum_k())
        args.append(att.cum_rep)
    o_spec = pl.BlockSpec((T, att.dv), lambda h, p, it, jt: (it[p], h))
    out_specs = [o_spec]
    out_shape = [jax.ShapeDtypeStruct((S, H * att.dv), BF16)]
    scratch = [pltpu.VMEM((1, T), F32), pltpu.VMEM((1, T), F32), pltpu.VMEM((att.dv, T), F32)]
    if exact:
        out_specs.append(o_spec)
        out_shape.append(jax.ShapeDtypeStruct((S, H * att.dv), F32))
        scratch.append(pltpu.VMEM((att.dv, T), F32))
    out_specs.append(att.row_q())
    out_shape.append(jax.ShapeDtypeStruct((H, 1, S), F32))
    return pl.pallas_call(
        body, name=name,
        grid_spec=pltpu.PrefetchScalarGridSpec(
            num_scalar_prefetch=2, grid=(H, npairs), in_specs=in_specs, out_specs=out_specs,
            scratch_shapes=scratch),
        out_shape=out_shape,
        compiler_params=_params(("parallel", "arbitrary")),
    )(it, jt, *args)


def _att_bwd_t(att, do, lse, o, dq_dtype, dk_dtypes, name, dq_rope=None, order=None):
    S, H, T, qs = att.S, att.H, att.T, att.qs
    it, jt, npairs = _pairs(att.nb, by_key=True)
    nk = len(att.ks)
    last = att.nb - 1
    widths = [k[1] for k in att.ks]

    def body(it_ref, jt_ref, *refs):
        q_ref = refs[0]
        k_refs = refs[1:1 + nk]
        v_ref, do_ref, lse_ref, o_ref = refs[1 + nk:5 + nk]
        n = 5 + nk
        cum_ref = None
        if att.has_bias:
            cum_ref = refs[n]
            n += 1
        rope_refs = None
        if dq_rope is not None:
            rope_refs = refs[n:n + 3]
            n += 3
        if order is not None:
            n += 1
        dl_acc = refs[-1]
        dq_ref = refs[n]
        dk_refs = refs[n + 1:n + 1 + nk]
        dv_ref = refs[n + 1 + nk]
        n += nk + 2
        dc_ref = None
        if att.has_bias:
            dc_ref = refs[n]
            n += 1
        dq_acc, dk_acc, dv_acc = refs[n:n + 3]
        dc_acc = refs[n + 3] if att.has_bias else None
        p = pl.program_id(1)
        i, j = it_ref[p], jt_ref[p]

        @pl.when(p == 0)
        def _():
            dq_acc[...] = jnp.zeros_like(dq_acc)

        @pl.when(i == j)
        def _():
            dk_acc[...] = jnp.zeros_like(dk_acc)
            dv_acc[...] = jnp.zeros_like(dv_acc)
            if att.has_bias:
                dc_acc[...] = jnp.zeros_like(dc_acc)

        @pl.when(j == 0)
        def _():
            prod = do_ref[...].astype(F32) * o_ref[...].astype(F32)
            ones = jnp.ones((8, att.dv), BF16)
            rows = jnp.zeros((8, T), F32)
            for part in _split3(prod):
                rows = rows + lax.dot_general(ones, part, _NT, preferred_element_type=F32)
            dl_acc[i] = rows[0:1, :]

        def step(masked):
            k = _join(k_refs)
            v = v_ref[...]
            dl = dl_acc[i]
            subs = att.sub_blocks(masked)

            def logits(idx):
                q0, nkeys = subs[idx]
                cum = cum_ref[0:nkeys, :] if att.has_bias else None
                return att.scores(k[0:nkeys], q_ref[q0:q0 + qs, :], cum, q0, masked)

            ahead = logits(0)
            for idx, (q0, nkeys) in enumerate(subs):
                qsl = slice(q0, q0 + qs)
                ksl = slice(0, nkeys)
                q_sub = q_ref[qsl, :]
                do_sub = do_ref[qsl, :]
                s, mask = ahead
                if idx + 1 < len(subs):
                    ahead = logits(idx + 1)
                pr = jnp.exp2(s - lse_ref[:, qsl])
                if masked:
                    pr = jnp.where(mask, pr, 0.0)
                dp = lax.dot_general(v[ksl], do_sub, _NT, preferred_element_type=F32)
                ds = pr * (dp - dl[:, qsl])
                ds_b = ds.astype(BF16)
                dv_acc[ksl, :] += jnp.dot(pr.astype(BF16), do_sub, preferred_element_type=F32)
                dk_acc[ksl, :] += jnp.dot(ds_b, q_sub, preferred_element_type=F32)
                dq_acc[i, :, qsl] += lax.dot_general(k[ksl], ds_b, _TN, preferred_element_type=F32)
                if att.has_bias:
                    part = ds[:, 0:LANE] if qs >= LANE else ds
                    for c0 in range(LANE, qs, LANE):
                        part = part + ds[:, c0:c0 + LANE]
                    dc_acc[ksl, :] += part

        @pl.when(i > j)
        def _():
            step(False)

        @pl.when(i == j)
        def _():
            step(True)
            dq = jnp.transpose(dq_acc[i] * att.scale)
            if dq_rope is not None:
                dq = _rope(dq, rope_refs[0][...], rope_refs[1][...], rope_refs[2][...], -1)
            dq_ref[...] = dq.astype(dq_ref.dtype)

        @pl.when(i == last)
        def _():
            dk = dk_acc[...] * (1.0 / LOG2E)
            off = 0
            for r, w in zip(dk_refs, widths):
                r[...] = dk[:, off:off + w].astype(r.dtype)
                off += w
            dv_ref[...] = dv_acc[...].astype(dv_ref.dtype)
            if att.has_bias:
                dc_ref[...] = -jnp.sum(dc_acc[...], axis=-1, keepdims=True)

    do_op = (do, att.dv, 0, True)
    o_spec = pl.BlockSpec((T, att.dv), lambda h, p, it, jt: (jnp.where(jt[p] == 0, it[p], last), h))
    in_specs = ([att.q_spec(att.q)] + [att.k_spec(k) for k in att.ks]
                + [att.k_spec(att.v), att.q_spec(do_op), att.row_q(), o_spec])
    args = [att.q[0]] + [k[0] for k in att.ks] + [att.v[0], do, lse, o]
    if att.has_bias:
        in_specs.append(att.cum_k())
        args.append(att.cum_rep)
    if dq_rope is not None:
        in_specs += [pl.BlockSpec((T, att.dq), lambda h, p, it, jt: (jt[p], 0))] * 3
        args += list(dq_rope)
    if order is not None:
        in_specs.append(_ANY_SPEC)
        args.append(order)
    out_specs = [pl.BlockSpec((T, att.dq), lambda h, p, it, jt: (jt[p], h))]
    out_shape = [jax.ShapeDtypeStruct((S, H * att.dq), dq_dtype)]
    out_specs += [pl.BlockSpec((T, w), lambda h, p, it, jt: (jt[p], h)) for w in widths]
    out_shape += [jax.ShapeDtypeStruct((S, H * w), dt) for w, dt in zip(widths, dk_dtypes)]
    out_specs.append(pl.BlockSpec((T, att.dv), lambda h, p, it, jt: (jt[p], h)))
    out_shape.append(jax.ShapeDtypeStruct((S, H * att.dv), BF16))
    scratch = [pltpu.VMEM((att.nb, att.dq, T), F32), pltpu.VMEM((T, att.dq), F32), pltpu.VMEM((T, att.dv), F32)]
    if att.has_bias:
        out_specs.append(pl.BlockSpec((None, T, 1), lambda h, p, it, jt: (h, jt[p], 0)))
        out_shape.append(jax.ShapeDtypeStruct((H, S, 1), F32))
        scratch.append(pltpu.VMEM((T, min(qs, LANE)), F32))
    scratch.append(pltpu.VMEM((att.nb, 1, T), F32))
    return pl.pallas_call(
        body, name=name,
        grid_spec=pltpu.PrefetchScalarGridSpec(
            num_scalar_prefetch=2, grid=(H, npairs), in_specs=in_specs, out_specs=out_specs,
            scratch_shapes=scratch),
        out_shape=out_shape,
        compiler_params=_params(("parallel", "arbitrary")),
    )(it, jt, *args)


def _adamw(w, g1, g2, m, v, name, g_row=None):
    _, K, N = w.shape
    by_rows = K % 8 == 0
    tr = _tile(K, 256, 8) if by_rows else K
    if g_row is None:
        assert g1.shape == (K, N) and g2.shape == (K, N), name
        g_row = 0
    assert by_rows and g_row % tr == 0 or g_row == 0, name
    g_blk = g_row // tr
    tc = N if by_rows else _tile(N, LANE)
    c1 = 1.0 - ADAM_B1 ** ADAM_STEP
    c2 = 1.0 - ADAM_B2 ** ADAM_STEP

    def body(w_ref, g1_ref, g2_ref, m_ref, v_ref, g_ref, d_ref, nm_ref, nv_ref):
        gv = g1_ref[...] + g2_ref[...]
        nm = ADAM_B1 * m_ref[...] + (1.0 - ADAM_B1) * gv
        nv = ADAM_B2 * v_ref[...] + (1.0 - ADAM_B2) * (gv * gv)
        g_ref[...] = gv
        d_ref[...] = -ADAM_LR * ((nm / c1) / (jnp.sqrt(nv / c2) + ADAM_EPS) + ADAM_WD * w_ref[...])
        nm_ref[...] = nm
        nv_ref[...] = nv

    if by_rows:
        blk = pl.BlockSpec((None, tr, N), lambda i: (0, i, 0))
        gblk = pl.BlockSpec((tr, N), lambda i: (g_blk + i, 0))
    else:
        blk = pl.BlockSpec((None, K, tc), lambda i: (0, 0, i))
        gblk = pl.BlockSpec((K, tc), lambda i: (0, i))
    return pl.pallas_call(
        body, name=name, grid=(K // tr if by_rows else N // tc,),
        in_specs=[blk, gblk, gblk, blk, blk], out_specs=[blk] * 4,
        out_shape=[jax.ShapeDtypeStruct((1, K, N), F32)] * 4,
        compiler_params=_params(("parallel",)),
    )(w, g1, g2, m, v)


_HBM_SPEC = pl.BlockSpec(memory_space=pltpu.HBM)
_SEM_SPEC = pl.BlockSpec(memory_space=pltpu.SEMAPHORE)
_VMEM_SPEC = pl.BlockSpec(memory_space=pltpu.VMEM)
_EFFECT = pltpu.SideEffectType.DATAFLOW_SIDE_EFFECTING


def _place():
    return lax.axis_index("x"), lax.axis_index("y"), lax.axis_index("c")


def _other_chips(x, y):
    return [(1 - x, y), (x, 1 - y), (1 - x, 1 - y)]


def _chip_copies(src_ref, land_ref, sems, gather):
    x, y, c = _place()
    me = 2 * x + y
    out, back = [], []
    if gather == "half":
        half = src_ref.shape[0] // 2
        mine = pl.ds(pl.multiple_of(c * half, 16), half)
    for n, (px, py) in enumerate(_other_chips(x, y)):
        if gather == "half":
            src, there, here = src_ref.at[mine], land_ref.at[me, mine], land_ref.at[2 * px + py, mine]
        elif gather:
            src, there, here = src_ref, land_ref.at[me], land_ref.at[2 * px + py]
        else:
            src, there, here = src_ref.at[2 * px + py], land_ref.at[n], land_ref.at[n]
        out.append(pltpu.make_async_remote_copy(
            src_ref=src, dst_ref=there, send_sem=sems[n], recv_sem=sems[3 + n],
            device_id=(px, py, c), device_id_type=MESH))
        back.append(pltpu.make_async_remote_copy(
            src_ref=src, dst_ref=here, send_sem=sems[n], recv_sem=sems[3 + n],
            device_id=(px, py, c), device_id_type=MESH))
    return out, back


def _xchg_start(src, land, gather, order, name):
    def body(src_ref, land_ref, order_ref, *outs):
        sems = outs[0:6]
        token = outs[8]
        out, _ = _chip_copies(src_ref, land_ref, sems, gather)
        for cp in out:
            cp.start()
        token[...] = jnp.zeros_like(token)

    outs = pl.pallas_call(
        body, name=name,
        out_shape=(pltpu.SemaphoreType.DMA(()),) * 6 + (
            pltpu.HBM(src.shape, src.dtype), pltpu.HBM(land.shape, land.dtype),
            jax.ShapeDtypeStruct((8, LANE), F32)),
        in_specs=(_HBM_SPEC, _HBM_SPEC, _ANY_SPEC),
        out_specs=(_SEM_SPEC,) * 6 + (_HBM_SPEC, _HBM_SPEC, _VMEM_SPEC),
        input_output_aliases={0: 6, 1: 7},
        compiler_params=pltpu.CompilerParams(has_side_effects=_EFFECT),
    )(pltpu.with_memory_space_constraint(src, pltpu.HBM), pltpu.with_memory_space_constraint(land, pltpu.HBM), order)
    return outs[0:6], outs[6], outs[7], outs[8]


def _xchg_wait(started, gather, after, name):
    sems, src, land, _ = started
    after = after if isinstance(after, tuple) else (after,)

    def body(src_ref, land_ref, *rest):
        _, back = _chip_copies(src_ref, land_ref, rest[0:6], gather)
        for cp in back:
            cp.wait_send()
            cp.wait_recv()

    return pl.pallas_call(
        body, name=name,
        out_shape=(pltpu.HBM(src.shape, src.dtype), pltpu.HBM(land.shape, land.dtype)),
        in_specs=(_HBM_SPEC, _HBM_SPEC) + (_SEM_SPEC,) * 6 + (_ANY_SPEC,) * len(after),
        out_specs=(_HBM_SPEC, _HBM_SPEC),
        input_output_aliases={0: 0, 1: 1},
        compiler_params=pltpu.CompilerParams(has_side_effects=_EFFECT),
    )(src, land, *sems, *after)


def _forward_halves(land, name):
    _, R, C = land.shape
    half = R // 2
    assert half % 16 == 0

    def body(land_ref, out_ref, send_sems, recv_sems):
        x, y, c = _place()
        mine = pl.ds(pl.multiple_of(c * half, 16), half)
        theirs = pl.ds(pl.multiple_of((1 - c) * half, 16), half)
        sends = []
        for n, (px, py) in enumerate(_other_chips(x, y)):
            cp = pltpu.make_async_remote_copy(
                src_ref=land_ref.at[2 * px + py, mine], dst_ref=out_ref.at[2 * px + py, mine],
                send_sem=send_sems.at[n], recv_sem=recv_sems.at[n], device_id=(x, y, 1 - c), device_id_type=MESH)
            cp.start()
            sends.append(cp)
        for n, (px, py) in enumerate(_other_chips(x, y)):
            pltpu.make_async_remote_copy(
                src_ref=land_ref.at[2 * px + py, theirs], dst_ref=out_ref.at[2 * px + py, theirs],
                send_sem=send_sems.at[n], recv_sem=recv_sems.at[n], device_id=(x, y, 1 - c),
                device_id_type=MESH).wait_recv()
        for cp in sends:
            cp.wait_send()

    return pl.pallas_call(
        body, name=name,
        in_specs=[_ANY_SPEC], out_specs=_ANY_SPEC,
        out_shape=jax.ShapeDtypeStruct(land.shape, land.dtype),
        input_output_aliases={0: 0},
        scratch_shapes=[pltpu.SemaphoreType.DMA((3,)), pltpu.SemaphoreType.DMA((3,))],
    )(land)


def _sib_copy(src_ref, land_ref, send_sem, recv_sem):
    x, y, c = _place()
    return pltpu.make_async_remote_copy(src_ref=src_ref, dst_ref=land_ref, send_sem=send_sem, recv_sem=recv_sem,
                                        device_id=(x, y, 1 - c), device_id_type=MESH)


def _sib_start(src, name):
    land = lax.empty(src.shape, src.dtype)

    def body(src_ref, land_ref, send_sem, recv_sem, src_thru, land_thru, token):
        _sib_copy(src_ref, land_ref, send_sem, recv_sem).start()
        token[...] = jnp.zeros_like(token)

    return pl.pallas_call(
        body, name=name,
        out_shape=(pltpu.SemaphoreType.DMA(()), pltpu.SemaphoreType.DMA(()),
                   pltpu.HBM(src.shape, src.dtype), pltpu.HBM(land.shape, land.dtype),
                   jax.ShapeDtypeStruct((8, LANE), F32)),
        in_specs=(_HBM_SPEC, _HBM_SPEC),
        out_specs=(_SEM_SPEC, _SEM_SPEC, _HBM_SPEC, _HBM_SPEC, _VMEM_SPEC),
        input_output_aliases={0: 2, 1: 3},
        compiler_params=pltpu.CompilerParams(has_side_effects=_EFFECT),
    )(pltpu.with_memory_space_constraint(src, pltpu.HBM), pltpu.with_memory_space_constraint(land, pltpu.HBM))


def _sib_wait(started, after, name):
    send_sem, recv_sem, src, land, _ = started

    def body(src_ref, land_ref, send_sem, recv_sem, after_ref, src_out, land_out):
        cp = _sib_copy(src_ref, land_ref, send_sem, recv_sem)
        cp.wait_send()
        cp.wait_recv()

    return pl.pallas_call(
        body, name=name,
        out_shape=(pltpu.HBM(src.shape, src.dtype), pltpu.HBM(land.shape, land.dtype)),
        in_specs=(_HBM_SPEC, _HBM_SPEC, _SEM_SPEC, _SEM_SPEC, _ANY_SPEC),
        out_specs=(_HBM_SPEC, _HBM_SPEC),
        input_output_aliases={0: 0, 1: 1},
        compiler_params=pltpu.CompilerParams(has_side_effects=_EFFECT),
    )(src, land, send_sem, recv_sem, after)


def _sum_slabs(gp, recv, chip, name):
    _, R, C = gp.shape
    tr = _tile(R, PACK_ROWS, 16)

    def body(chip_ref, own_ref, r0_ref, r1_ref, r2_ref, o_ref):
        acc = own_ref[...].astype(F32) + r0_ref[...].astype(F32)
        o_ref[...] = (acc + r1_ref[...].astype(F32)) + r2_ref[...].astype(F32)

    def got(n):
        return pl.BlockSpec((None, tr, C), lambda i, chip_ref: (n, i, 0))

    return pl.pallas_call(
        body, name=name,
        grid_spec=pltpu.PrefetchScalarGridSpec(
            num_scalar_prefetch=1, grid=(R // tr,),
            in_specs=[pl.BlockSpec((None, tr, C), lambda i, chip_ref: (chip_ref[0], i, 0)), got(0), got(1), got(2)],
            out_specs=pl.BlockSpec((tr, C), lambda i, chip_ref: (i, 0))),
        out_shape=jax.ShapeDtypeStruct((R, C), F32),
        compiler_params=_params(("parallel",)),
    )(jnp.reshape(chip, (1,)).astype(jnp.int32), gp, recv, recv, recv)


def _all_reduce_vec(vec, name):
    VR, W = vec.shape

    def body(vec_ref, vall_ref, vout_ref, vsend_sems, vrecv_sems):
        x, y, c = _place()
        vall_ref[4 * x + 2 * y + c] = vec_ref[...]
        sends = []
        peers = []
        for r in range(1, N_DEV):
            dx, dy, dc = (r >> 2) & 1, (r >> 1) & 1, r & 1
            peer = (x ^ dx, y ^ dy, c ^ dc)
            peers.append(peer)
            cp = pltpu.make_async_remote_copy(
                src_ref=vec_ref, dst_ref=vall_ref.at[4 * x + 2 * y + c], send_sem=vsend_sems.at[r - 1],
                recv_sem=vrecv_sems.at[r - 1], device_id=peer, device_id_type=MESH)
            cp.start()
            sends.append(cp)
        for r, peer in enumerate(peers):
            pltpu.make_async_remote_copy(
                src_ref=vec_ref, dst_ref=vall_ref.at[4 * peer[0] + 2 * peer[1] + peer[2]],
                send_sem=vsend_sems.at[r], recv_sem=vrecv_sems.at[r],
                device_id=peer, device_id_type=MESH).wait_recv()
        total = vall_ref[0]
        for d in range(1, N_DEV):
            total = total + vall_ref[d]
        vout_ref[...] = total
        for cp in sends:
            cp.wait_send()

    outs = pl.pallas_call(
        body, name=name,
        in_specs=[_VMEM_SPEC], out_specs=[_VMEM_SPEC, _VMEM_SPEC],
        out_shape=[jax.ShapeDtypeStruct((N_DEV, VR, W), F32), jax.ShapeDtypeStruct((VR, W), F32)],
        scratch_shapes=[pltpu.SemaphoreType.DMA((N_DEV - 1,)), pltpu.SemaphoreType.DMA((N_DEV - 1,))],
    )(vec)
    return outs[1]


class _Pack:
    def __init__(self, group, C):
        self.group, self.C = group, C
        self.rows, self.offs, off = {}, {}, 0
        for nm, (K, N), _ in group:
            assert N <= C, nm
            self.rows[nm] = K if 2 * N > C else -(-(K * N) // C)
            self.offs[nm] = off
            off += -(-self.rows[nm] // 16) * 16
        self.used = off
        self.R = -(-off // PACK_ROWS) * PACK_ROWS

    def _rows_of(self, a):
        K, N = a.shape
        if 2 * N > self.C:
            a = jnp.pad(a, ((0, 0), (0, self.C - N)))
        else:
            a = jnp.pad(a.reshape(-1), (0, -(K * N) % self.C)).reshape(-1, self.C)
        return jnp.pad(a, ((0, -a.shape[0] % 16), (0, 0)))

    def pack(self, shards):
        parts = [self._rows_of(shards[nm].astype(BF16)) for nm, _, _ in self.group]
        return jnp.concatenate(parts + [jnp.zeros((self.R - self.used, self.C), BF16)], axis=0)

    def _shard_of(self, rows, shape):
        K, N = shape
        return rows[:, :N] if 2 * N > self.C else rows.reshape(-1)[:K * N].reshape(K, N)

    def part(self, flat, nm, shape):
        return self._shard_of(flat[self.offs[nm]:self.offs[nm] + self.rows[nm]], shape)

    def slab_rows(self, nm, g):
        (K, N), axis = next((shape, axis) for n, shape, axis in self.group if n == nm)
        cuts = [g[:, k * N:(k + 1) * N] if axis == 1 else g[k * K:(k + 1) * K, :] for k in range(N_CHIPS)]
        return jnp.stack([self._rows_of(c.astype(BF16)) for c in cuts])

    def slabs(self, grads):
        parts = [self.slab_rows(nm, grads[nm]) for nm, _, _ in self.group]
        return jnp.concatenate(parts + [jnp.zeros((N_CHIPS, self.R - self.used, self.C), BF16)], axis=1)

    def full(self, gathered, names=None):
        res = {}
        for nm, (K, N), axis in self.group:
            if names is None or nm in names:
                rows = gathered[:, self.offs[nm]:self.offs[nm] + self.rows[nm]]
                res[nm] = jnp.concatenate([self._shard_of(rows[k], (K, N)) for k in range(N_CHIPS)], axis=axis)
        return res


def _rope_tables(S):
    pos = jnp.arange(S, dtype=F32)
    inv = 1.0 / (ROPE_THETA ** (jnp.arange(0, MLA_ROPE, 2, dtype=F32) / MLA_ROPE))
    ang = pos[:, None] * inv[None, :]
    cos, sin = jnp.cos(ang), jnp.sin(ang)
    half = MLA_ROPE // 2
    z = jnp.zeros((S, half), F32)
    one = jnp.ones((S, LANE - MLA_ROPE), F32)
    zero = jnp.zeros((S, LANE - MLA_ROPE), F32)
    kc = jnp.concatenate([cos, cos, one], axis=1)
    ksa = jnp.concatenate([-sin, z, zero], axis=1)
    ksb = jnp.concatenate([z, sin, zero], axis=1)
    qc = jnp.concatenate([jnp.ones((S, MLA_NOPE), F32), kc], axis=1)
    qsa = jnp.concatenate([jnp.zeros((S, MLA_NOPE), F32), ksa], axis=1)
    qsb = jnp.concatenate([jnp.zeros((S, MLA_NOPE), F32), ksb], axis=1)
    return (kc, ksa, ksb), (qc, qsa, qsb)


def _pad_cols(a, width):
    return jnp.pad(a, ((0, 0), (0, width - a.shape[1])))


def kernel(x, attn_norm, w_in, fox_f_bias, q_norm, w_uq, kv_norm, w_ukv, w_mla_branch, w_fox_branch, w_out, mlp_norm, w_up, w_down, final_norm, loss_target, m_attn_norm, m_w_in, m_fox_f_bias, m_q_norm, m_w_uq, m_kv_norm, m_w_ukv, m_w_mla_branch, m_w_fox_branch, m_w_out, m_mlp_norm, m_w_up, m_w_down, m_final_norm, v_attn_norm, v_w_in, v_fox_f_bias, v_q_norm, v_w_uq, v_kv_norm, v_w_ukv, v_w_mla_branch, v_w_fox_branch, v_w_out, v_mlp_norm, v_w_up, v_w_down, v_final_norm):
    _, S, D = x.shape
    H, HF = MLA_HEADS, FOX_HEADS
    QL, KVL = MLA_Q_LORA, MLA_KV_LORA
    assert H == HF and H <= 8
    xs = x[0]
    target = loss_target[0]
    C = D
    chip = 2 * lax.axis_index("x") + lax.axis_index("y")

    def flip(a):
        return jnp.transpose(a, (0, 2, 1))

    w_in, m_w_in, v_w_in = flip(w_in), flip(m_w_in), flip(v_w_in)
    weights = {"attn_norm": attn_norm, "w_in": w_in, "fox_f_bias": fox_f_bias, "q_norm": q_norm, "w_uq": w_uq,
               "kv_norm": kv_norm, "w_ukv": w_ukv, "w_mla_branch": w_mla_branch, "w_fox_branch": w_fox_branch,
               "w_out": w_out, "mlp_norm": mlp_norm, "w_up": w_up, "w_down": w_down, "final_norm": final_norm}
    moments = {"attn_norm": (m_attn_norm, v_attn_norm), "w_in": (m_w_in, v_w_in), "fox_f_bias": (m_fox_f_bias, v_fox_f_bias),
               "q_norm": (m_q_norm, v_q_norm), "w_uq": (m_w_uq, v_w_uq), "kv_norm": (m_kv_norm, v_kv_norm),
               "w_ukv": (m_w_ukv, v_w_ukv), "w_mla_branch": (m_w_mla_branch, v_w_mla_branch),
               "w_fox_branch": (m_w_fox_branch, v_w_fox_branch), "w_out": (m_w_out, v_w_out),
               "mlp_norm": (m_mlp_norm, v_mlp_norm), "w_up": (m_w_up, v_w_up), "w_down": (m_w_down, v_w_down),
               "final_norm": (m_final_norm, v_final_norm)}

    def group(names_axes):
        return [(nm, weights[nm].shape[1:], axis) for nm, axis in names_axes]

    pack_a = _Pack(group([("w_in", 0), ("w_uq", 1), ("w_ukv", 1)]), C)
    pack_b = _Pack(group([("w_down", 0), ("w_up", 1), ("w_out", 0), ("w_mla_branch", 1), ("w_fox_branch", 1)]), C)
    RA, RB = pack_a.R, pack_b.R
    wp_a = pack_a.pack({nm: weights[nm][0] for nm, _, _ in pack_a.group})
    wp_b = pack_b.pack({nm: weights[nm][0] for nm, _, _ in pack_b.group})
    n_in = w_in.shape[1]
    rows_in = -(-n_in // 16) * 16
    assert pack_a.offs["w_in"] == 0 and all((k * n_in) % 16 + n_in <= rows_in for k in range(N_CHIPS))
    shifted = lax.dynamic_update_slice(jnp.zeros((rows_in, C), BF16), wp_a[:n_in], ((chip * n_in) % 16, 0))
    wp_a = jnp.concatenate([shifted, wp_a[rows_in:]], axis=0)
    ag_a = _xchg_start(wp_a, lax.empty((N_CHIPS, RA, C), BF16), "half", jnp.zeros((8, LANE), F32), "all_gather_start_a")
    xn = _norm_fwd(xs, attn_norm, "attn_norm_fwd", order=ag_a[3])
    own_a, land_a = _xchg_wait(ag_a, "half", (xn, wp_b), "all_gather_wait_a")
    land_a = _forward_halves(land_a, "all_gather_forward_a")
    gathered_a = lax.dynamic_update_slice(land_a, own_a[None], (chip, 0, 0))
    ag_b = _xchg_start(wp_b, lax.empty((N_CHIPS, RB, C), BF16), True, gathered_a, "all_gather_start_b")
    full = pack_a.full(gathered_a, ("w_uq", "w_ukv"))
    tile0 = [(k * n_in) // 16 * 16 for k in range(N_CHIPS)]
    total = tile0[-1] + rows_in
    full["w_in"] = sum(jnp.pad(gathered_a[k, :rows_in], ((tile0[k], total - tile0[k] - rows_in), (0, 0)))
                       for k in range(N_CHIPS))

    o_ckv = QL
    o_kr = o_ckv + KVL
    o_fq = o_kr + MLA_ROPE
    o_ff = o_fq + 3 * HF * FOX_HEAD_DIM
    o_g = o_ff + HF
    wi = full["w_in"]
    assert N_CHIPS * n_in == o_g + 2 * D and wi.shape[0] >= o_g + 2 * D
    WS = QL + KVL + 2 * LANE
    NQKV = 3 * HF * FOX_HEAD_DIM

    def pad_rows(a, rows):
        return jnp.pad(a, ((0, rows - a.shape[0]), (0, 0)))

    w_small = jnp.concatenate([wi[:o_kr], pad_rows(wi[o_kr:o_fq], LANE), pad_rows(wi[o_ff:o_g], LANE)], axis=0)
    w_qkv = wi[o_fq:o_ff]
    w_g = wi[o_g:o_g + 2 * D]
    w_pack = jnp.concatenate([w_small, w_qkv, w_g], axis=0)
    dqk = MLA_NOPE + MLA_ROPE
    w_uq_p = jnp.pad(full["w_uq"].reshape(QL, H, dqk), ((0, 0), (0, 0), (0, QPAD - dqk))).reshape(QL, H * QPAD)
    ukv = full["w_ukv"].reshape(KVL, H, MLA_NOPE + MLA_V)
    w_ukv_p = jnp.concatenate([ukv[:, :, :MLA_NOPE].reshape(KVL, H * MLA_NOPE),
                               ukv[:, :, MLA_NOPE:].reshape(KVL, H * MLA_V)], axis=1)

    (kc, ksa, ksb), (qc, qsa, qsb) = _rope_tables(S)
    bias_pad = _pad_cols(fox_f_bias, LANE)

    small = _matmul(xn, w_small, "nt", [F32], "proj_small")
    n_fq = HF * FOX_HEAD_DIM
    q_scale = jnp.concatenate([jnp.full((1, n_fq), LOG2E / math.sqrt(FOX_HEAD_DIM), F32),
                               jnp.ones((1, NQKV - n_fq), F32)], axis=1)
    qkv = _matmul(xn, w_qkv, "nt", [BF16], "proj_qkv", col_extras=(q_scale,), epilogue=lambda acc, cs: (acc * cs,))
    gpre = _matmul(xn, w_g, "nt", [F32], "proj_gates")
    cqn, ckvn, kr, cum = _prep_fwd(small, q_norm, kv_norm, bias_pad, kc, ksa, ksb, HF, "prep_fwd")
    c2_mla = LOG2E / math.sqrt(dqk)
    q_rot = _matmul(cqn, w_uq_p, "nn", [BF16], "mla_q_up", tn=QPAD, row_extras=(qc * c2_mla, qsa * c2_mla, qsb * c2_mla),
                    epilogue=lambda acc, c, sa, sb: (_rope(acc, c, sa, sb, 1),))
    kv2 = _matmul(ckvn, w_ukv_p, "nn", [BF16], "mla_kv_up")

    def mla_att(qsub):
        return _AttT(S, H, (q_rot, QPAD, 0, True), [(kv2, MLA_NOPE, 0, True), (kr, LANE, 0, False)],
                     (kv2, MLA_V, H, True), 1.0 / math.sqrt(dqk), True, qsub=qsub)

    mla = mla_att(QSUB)
    o_mla, lse_mla = _att_fwd_t(mla_att(2 * QSUB), "mla_att_fwd")

    cum_t = jnp.transpose(cum[:, :HF]) * LOG2E
    cum_rep = jnp.broadcast_to(cum_t[:, :, None], (HF, S, min(QSUB, _tile(S, ATT_T))))
    fox = _AttT(S, HF, (qkv, FOX_HEAD_DIM, 0, True), [(qkv, FOX_HEAD_DIM, HF, True)],
                (qkv, FOX_HEAD_DIM, 2 * HF, True), 1.0 / math.sqrt(FOX_HEAD_DIM), False, cum_rep)
    o_fox, ox_fox, lse_fox = _att_fwd_t(fox, "fox_att_fwd", exact=True)

    own_b, land_b = _xchg_wait(ag_b, True, (lse_fox, lse_mla, gpre), "all_gather_wait_b")
    gathered_b = lax.dynamic_update_slice(land_b, own_b[None], (chip, 0, 0))
    full.update(pack_b.full(gathered_b, ("w_mla_branch", "w_fox_branch", "w_out")))
    w_mb, w_fb, w_o = (full[n] for n in ("w_mla_branch", "w_fox_branch", "w_out"))

    def b_of(nm, mode, tn, tk):
        (K, N), axis = next((shape, axis) for n, shape, axis in pack_b.group if n == nm)
        off = pack_b.offs[nm]
        shape = (N_CHIPS * K, N) if axis == 0 else (K, N_CHIPS * N)
        t_r, t_c = (tk, tn) if mode == "nn" else (tn, tk)
        t_r, t_c = _tile(shape[0], t_r), _tile(shape[1], t_c)
        if not (N == C and K % t_r == 0 and N % t_c == 0 and off % t_r == 0):
            return pack_b.full(gathered_b, (nm,))[nm], None
        base = off // t_r
        if axis == 0:
            per = K // t_r
            place = lambda rb, cb: (rb // per, base + rb % per, cb)
        else:
            per = N // t_c
            place = lambda rb, cb: (cb // per, base + rb, cb % per)
        return gathered_b, (shape, (lambda j, k: place(k, j)) if mode == "nn" else (lambda j, k: place(j, k)))

    y_mla = _matmul(o_mla, w_mb, "nn", [F32], "mla_branch")

    def gate_merge(acc, ga, gb, ya):
        return acc, _sigmoid(ga) * ya + _sigmoid(gb) * acc

    y_fox, merged = _matmul(o_fox, w_fb, "nn", [F32, BF16], "fox_branch_gates", tn=512,
                            extras=((gpre, 0), (gpre, 1), y_mla), epilogue=gate_merge)
    h1 = _matmul(merged, w_o, "nn", [F32], "out_proj", extras=(xs,), epilogue=lambda acc, r: (acc + r,))
    hn = _norm_fwd(h1, mlp_norm, "mlp_norm_fwd")

    def relu2(acc):
        a = jnp.maximum(acc, 0.0)
        return a * a, a

    w_u, w_u_in = b_of("w_up", "nn", 1024, 2048)
    u, a_pos = _matmul(hn, w_u, "nn", [BF16, BF16], "mlp_up", epilogue=relu2, b_in=w_u_in)
    w_d, w_d_in = b_of("w_down", "nn", 1024, 2048)
    h2 = _matmul(u, w_d, "nn", [F32], "mlp_down", tn=1024, extras=(h1,), epilogue=lambda acc, r: (acc + r,),
                 b_in=w_d_in)
    dh2, dh2_b, g_final, loss_part = _final(h2, final_norm.reshape(1, D), target, "final_norm_loss")

    gp_b = lax.empty((N_CHIPS, RB, C), BF16)
    by_glue = {}

    def grad_b(nm, a, b, name):
        nonlocal gp_b
        (K, N), axis = next((shape, axis) for n, shape, axis in pack_b.group if n == nm)
        off = pack_b.offs[nm]
        tm = min(1024, K) if axis == 0 else min(1024, a.shape[1])
        tn = min(1024, N) if axis == 1 else min(1024, b.shape[1])
        if not (N == C and tm % LANE == 0 and tn % LANE == 0 and K % tm == 0 and N % tn == 0 and off % tm == 0):
            by_glue[nm] = _mm_tn(a, b, name)
            return
        base = off // tm
        if axis == 0:
            per = K // tm
            place = lambda i, j: (i // per, base + i % per, j)
        else:
            per = N // tn
            place = lambda i, j: (j // per, base + i, j % per)
        gp_b = _mm_tn(a, b, name, tm=tm, tn=tn, into=(gp_b, place))

    w_d, w_d_in = b_of("w_down", "nt", 1024, 2048)
    da = _matmul(dh2_b, w_d, "nt", [BF16], "mlp_down_dx", extras=(a_pos,),
                 epilogue=lambda acc, a: (acc * (2.0 * a.astype(F32)),), b_in=w_d_in)
    grad_b("w_down", u, dh2_b, "mlp_down_dw")
    w_u, w_u_in = b_of("w_up", "nt", 1024, 2048)
    dhn = _matmul(da, w_u, "nt", [F32], "mlp_up_dx", tn=1024, b_in=w_u_in)
    grad_b("w_up", hn, da, "mlp_up_dw")
    dh1, dh1_b, g_mlp_norm = _norm_bwd(h1, dhn, mlp_norm, dh2, "mlp_norm_bwd")

    def gate_bwd(acc, ga, gb, ya, yb):
        ga, gb = _sigmoid(ga), _sigmoid(gb)
        return acc * ga, acc * gb, acc * ya * (ga * (1.0 - ga)), acc * yb * (gb * (1.0 - gb))

    dy_mla, dy_fox, dg_mla, dg_fox = _matmul(dh1_b, w_o, "nt", [BF16] * 4, "out_proj_dx_gates", tn=512,
                                             extras=((gpre, 0), (gpre, 1), y_mla, y_fox), epilogue=gate_bwd)
    grad_b("w_out", merged, dh1_b, "out_proj_dw")
    do_mla = _matmul(dy_mla, w_mb, "nt", [BF16], "mla_branch_dx")
    grad_b("w_mla_branch", o_mla, dy_mla, "mla_branch_dw")
    do_fox = _matmul(dy_fox, w_fb, "nt", [BF16], "fox_branch_dx")
    grad_b("w_fox_branch", o_fox, dy_fox, "fox_branch_dw")
    for nm, g in by_glue.items():
        gp_b = lax.dynamic_update_slice(gp_b, pack_b.slab_rows(nm, g), (0, pack_b.offs[nm], 0))
    if RB > pack_b.used:
        gp_b = lax.dynamic_update_slice(gp_b, jnp.zeros((N_CHIPS, RB - pack_b.used, C), BF16), (0, pack_b.used, 0))

    rs_b = _xchg_start(gp_b, lax.empty((3, RB, C), BF16), False, do_fox, "grad_scatter_start_b")

    dq_rot, dk_nope, dkr_heads, dv_mla = _att_bwd_t(mla, do_mla, lse_mla, o_mla, BF16, [BF16, F32],
                                                    "mla_att_bwd", dq_rope=(qc, qsa, qsb), order=rs_b[3])
    dfq, dfk, dfv, dcum = _att_bwd_t(fox, do_fox, lse_fox, ox_fox, BF16, [BF16], "fox_att_bwd")

    gp_b_sent, recv_b = _xchg_wait(rs_b, False, (dfq, dq_rot), "grad_scatter_wait_b")
    swap_b = _sib_start(_sum_slabs(gp_b_sent, recv_b, chip, "grad_sum_b"), "grad_swap_start_b")

    dcqn = _matmul(dq_rot, w_uq_p, "nt", [F32], "mla_q_up_dx", order=swap_b[4])
    g_w_uq_p = _mm_tn(cqn, dq_rot, "mla_q_up_dw")
    dkv2 = jnp.concatenate([dk_nope, dv_mla], axis=1)
    dckvn = _matmul(dkv2, w_ukv_p, "nt", [F32], "mla_kv_up_dx")
    g_w_ukv_p = _mm_tn(ckvn, dkv2, "mla_kv_up_dw")

    dcum_rows = jnp.pad(dcum[:, :, 0], ((0, 8 - HF), (0, 0)))
    dlogf_rows = _suffix_sum_rows(dcum_rows, "fox_forget_suffix_sum")
    dlogf = _pad_cols(jnp.transpose(dlogf_rows[:HF]), LANE)
    d_small, g_q_norm, g_kv_norm, g_bias = _prep_bwd(
        small, dcqn, dckvn, dkr_heads, dlogf, q_norm, kv_norm, bias_pad, kc, ksa, ksb, H, "prep_bwd")
    dproj = [d_small, dfq, dfk, dfv, dg_mla, dg_fox]
    gs, gfq, gfk, gfv, gg_mla, gg_fox = [
        _matmul(part, xn, "tn", [BF16], "proj_dw_" + tag, tm=1024, tn=1024, tk=2048)
        for part, tag in zip(dproj, ("small", "fq", "fk", "fv", "g_mla", "g_fox"))]

    g_w_in = jnp.concatenate([gs[:o_kr], gs[o_kr:o_kr + MLA_ROPE], gfq, gfk, gfv,
                              gs[o_kr + LANE:o_kr + LANE + HF], gg_mla, gg_fox], axis=0)
    g_w_uq = g_w_uq_p.reshape(QL, H, QPAD)[:, :, :dqk].reshape(QL, H * dqk)
    g_w_ukv = jnp.concatenate([g_w_ukv_p[:, :H * MLA_NOPE].reshape(KVL, H, MLA_NOPE),
                               g_w_ukv_p[:, H * MLA_NOPE:].reshape(KVL, H, MLA_V)], axis=2).reshape(KVL, -1)

    gp_a = pack_a.slabs({"w_in": g_w_in, "w_uq": g_w_uq, "w_ukv": g_w_ukv})
    rs_a = _xchg_start(gp_a, lax.empty((3, RA, C), BF16), False, gg_fox, "grad_scatter_start_a")
    dxn = _matmul_parts(dproj, w_pack, "nn", F32, "proj_dx", tm=512, tn=2048, order=rs_a[3])
    grad_x, g_attn_norm = _norm_bwd(xs, dxn, attn_norm, dh1, "attn_norm_bwd", with_bf16=False)
    gp_a_sent, recv_a = _xchg_wait(rs_a, False, grad_x, "grad_scatter_wait_a")
    swap_a = _sib_start(_sum_slabs(gp_a_sent, recv_a, chip, "grad_sum_a"), "grad_swap_start_a")
    vec_w = max(D, LANE)
    vec_rows = [g_attn_norm, g_mlp_norm, g_final, g_q_norm, g_kv_norm, g_bias, loss_part]
    vec = jnp.concatenate([_pad_cols(v, vec_w) for v in vec_rows] + [jnp.zeros((1, vec_w), F32)], axis=0)
    vsum = _all_reduce_vec(vec, "all_reduce_vectors")
    part_b, sib_b = _sib_wait(swap_b, vsum, "grad_swap_wait_b")

    grads, deltas, new_m, new_v = {}, {}, {}, {}

    def update(pack, mine, theirs):
        for nm, shape, _ in pack.group:
            K, N = shape
            if N == pack.C and K % 8 == 0 and pack.offs[nm] % _tile(K, 256, 8) == 0:
                g, d, nm_, nv_ = _adamw(weights[nm], mine, theirs, moments[nm][0], moments[nm][1], "adamw_" + nm,
                                        g_row=pack.offs[nm])
            else:
                g, d, nm_, nv_ = _adamw(weights[nm], pack.part(mine, nm, shape), pack.part(theirs, nm, shape),
                                        moments[nm][0], moments[nm][1], "adamw_" + nm)
            grads[nm], deltas[nm], new_m[nm], new_v[nm] = g, d, nm_, nv_
        return g

    last_b = update(pack_b, part_b, sib_b)
    part_a, sib_a = _sib_wait(swap_a, last_b, "grad_swap_wait_a")
    update(pack_a, part_a, sib_a)

    vec_names = ["attn_norm", "mlp_norm", "final_norm", "q_norm", "kv_norm", "fox_f_bias"]

    def vec_pack(arrs):
        return jnp.concatenate([_pad_cols(a.reshape(1, -1), vec_w) for a in arrs]
                               + [jnp.zeros((2, vec_w), F32)], axis=0)[None]

    vg, vd, vm, vv = _adamw(vec_pack([weights[n] for n in vec_names]), vsum, jnp.zeros_like(vsum),
                            vec_pack([moments[n][0] for n in vec_names]), vec_pack([moments[n][1] for n in vec_names]),
                            "adamw_vectors")
    for r, nm in enumerate(vec_names):
        shp = weights[nm].shape
        n = weights[nm].size
        grads[nm] = vsum[r, :n].reshape(shp)
        deltas[nm], new_m[nm], new_v[nm] = (vd[0, r, :n].reshape(shp), vm[0, r, :n].reshape(shp),
                                            vv[0, r, :n].reshape(shp))
    loss = vsum[6, 0]

    for res in (grads, deltas, new_m, new_v):
        res["w_in"] = flip(res["w_in"])
    order = ["attn_norm", "w_in", "fox_f_bias", "q_norm", "w_uq", "kv_norm", "w_ukv", "w_mla_branch", "w_fox_branch",
             "w_out", "mlp_norm", "w_up", "w_down", "final_norm"]
    return (loss, grad_x[None], *[grads[n] for n in order], *[deltas[n] for n in order],
            *[new_m[n] for n in order], *[new_v[n] for n in order])
```

```python
import math

import jax
import jax.numpy as jnp
from jax import lax
from jax.experimental import pallas as pl
from jax.experimental.pallas import tpu as pltpu

CHUNK = 64
MLA_HEADS = 8
MLA_Q_LORA = 512
MLA_KV_LORA = 256
MLA_NOPE = 128
MLA_ROPE = 64
MLA_V = 128
ROPE_THETA = 10000.0
FOX_HEADS = 8
FOX_HEAD_DIM = 128
EPS = 1e-6

ADAM_LR = 0.001
ADAM_B1 = 0.9
ADAM_B2 = 0.999
ADAM_EPS = 1e-08
ADAM_WD = 0.01
ADAM_STEP = 10

LANE = 128
QPAD = 2 * LANE
N_CHIPS = 4
N_DEV = 8
VMEM_LIMIT = 48 * 1024 * 1024
ATT_T = 2048
QSUB = 256
ROW_T = 256
PACK_ROWS = 256
LOG2E = 1.4426950408889634

BF16 = jnp.bfloat16
F32 = jnp.float32
MESH = pl.DeviceIdType.MESH

_NT = (((1,), (1,)), ((), ()))
_TN = (((0,), (0,)), ((), ()))
_NN = (((1,), (0,)), ((), ()))


def _tile(dim, pref, align=LANE):
    if dim <= pref:
        return dim
    t = (pref // align) * align
    while t >= align:
        if dim % t == 0:
            return t
        t -= align
    return dim


def _params(sem=None):
    return pltpu.CompilerParams(dimension_semantics=sem, vmem_limit_bytes=VMEM_LIMIT)


_ANY_SPEC = pl.BlockSpec(memory_space=pl.ANY)


def _matmul(a, b, mode, out_dtypes, name, *, tm=1024, tn=1024, tk=2048, extras=(), row_extras=(), col_extras=(),
            epilogue=None, order=None, into=None, b_in=None):
    b_shape = b.shape if b_in is None else b_in[0]
    if mode == "nn":
        (M, K), (K2, N) = a.shape, b_shape
    elif mode == "nt":
        (M, K), (N, K2) = a.shape, b_shape
    else:
        (K, M), (K2, N) = a.shape, b_shape
    assert K == K2, (name, a.shape, b_shape)
    tm, tn, tk = _tile(M, tm), _tile(N, tn), _tile(K, tk)
    nk = K // tk
    extras = [e if isinstance(e, tuple) else (e, 0) for e in extras]
    n_out = len(out_dtypes)
    n_ex = len(extras) + len(row_extras) + len(col_extras)
    n_ord = 0 if order is None else 1
    assert all(r.shape == (M, tn) for r in row_extras), name
    dims = {"nn": _NN, "nt": _NT, "tn": _TN}[mode]

    def body(*refs):
        a_ref, b_ref = refs[0], refs[1]
        ex_refs = refs[2:2 + n_ex]
        o_refs = refs[2 + n_ex + n_ord:2 + n_ex + n_ord + n_out]
        acc_ref = refs[2 + n_ex + n_ord + n_out]
        k = pl.program_id(2)
        part = lax.dot_general(a_ref[...], b_ref[...], dims, preferred_element_type=F32)

        @pl.when(k == 0)
        def _():
            acc_ref[...] = part

        @pl.when(k > 0)
        def _():
            acc_ref[...] += part

        @pl.when(k == nk - 1)
        def _():
            acc = acc_ref[...]
            if epilogue is None:
                outs = (acc,)
            else:
                outs = epilogue(acc, *[r[...] for r in ex_refs])
            for o_ref, o in zip(o_refs, outs):
                o_ref[...] = o.astype(o_ref.dtype)

    if mode == "nn":
        a_spec = pl.BlockSpec((tm, tk), lambda i, j, k: (i, k))
        b_spec = pl.BlockSpec((tk, tn), lambda i, j, k: (k, j))
    elif mode == "nt":
        a_spec = pl.BlockSpec((tm, tk), lambda i, j, k: (i, k))
        b_spec = pl.BlockSpec((tn, tk), lambda i, j, k: (j, k))
    else:
        a_spec = pl.BlockSpec((tk, tm), lambda i, j, k: (k, i))
        b_spec = pl.BlockSpec((tk, tn), lambda i, j, k: (k, j))
    if b_in is not None:
        b_block = (None, tn, tk) if mode == "nt" else (None, tk, tn)
        b_spec = pl.BlockSpec(b_block, lambda i, j, k: b_in[1](j, k))
    mn_spec = pl.BlockSpec((tm, tn), lambda i, j, k: (i, j))
    row_spec = pl.BlockSpec((tm, tn), lambda i, j, k: (i, 0))
    col_spec = pl.BlockSpec((1, tn), lambda i, j, k: (0, j))
    out_specs = [mn_spec] * n_out
    out_shape = [jax.ShapeDtypeStruct((M, N), dt) for dt in out_dtypes]
    aliases = {}
    if into is not None:
        buf, place = into
        assert n_out == 1 and n_ord == 1 and order is buf, name
        out_specs = [pl.BlockSpec((None, tm, tn), lambda i, j, k: place(i, j))]
        out_shape = [jax.ShapeDtypeStruct(buf.shape, buf.dtype)]
        aliases = {2 + n_ex: 0}
    outs = pl.pallas_call(
        body,
        name=name,
        grid=(M // tm, N // tn, nk),
        in_specs=([a_spec, b_spec]
                  + [pl.BlockSpec((tm, tn), lambda i, j, k, g=g: (i, j + g * (N // tn))) for _, g in extras]
                  + [row_spec] * len(row_extras) + [col_spec] * len(col_extras) + [_ANY_SPEC] * n_ord),
        out_specs=out_specs,
        out_shape=out_shape,
        scratch_shapes=[pltpu.VMEM((tm, tn), F32)],
        input_output_aliases=aliases,
        compiler_params=_params(("parallel", "parallel", "arbitrary")),
    )(a, b, *[e for e, _ in extras], *row_extras, *col_extras, *([] if order is None else [order]))
    return outs[0] if n_out == 1 else outs


def _matmul_row_parts(parts, b, out_dtype, name, *, tm=512, tk=2048, order=None):
    M, (K, N) = parts[0].shape[0], b.shape
    widths = [p.shape[1] for p in parts]
    assert sum(widths) == K, name
    tm, tk = _tile(M, tm), _tile(K, tk)
    nk = K // tk
    steps, at = [], 0
    for p, w in enumerate(widths):
        off = 0
        while off < w:
            k, room = divmod(at, tk)
            take = min(w - off, tk - room)
            if room == 0:
                steps.append([])
            steps[k].append((p, off, take, room))
            off += take
            at += take
    assert len(steps) == nk and all(t % LANE == 0 and o % LANE == 0 for s in steps for _, o, t, _ in s), name
    n_parts = len(parts)
    n_ord = 0 if order is None else 1

    def body(*refs):
        a_refs = refs[0:n_parts]
        b_ref = refs[n_parts]
        o_ref, acc_ref = refs[n_parts + 1 + n_ord], refs[n_parts + 2 + n_ord]
        k = pl.program_id(1)
        for kk, pieces in enumerate(steps):
            @pl.when(k == kk)
            def _(kk=kk, pieces=pieces):
                part = None
                for p, off, take, room in pieces:
                    d = jnp.dot(a_refs[p][:, off:off + take], b_ref[room:room + take, :], preferred_element_type=F32)
                    part = d if part is None else part + d
                if kk == 0:
                    acc_ref[...] = part
                else:
                    acc_ref[...] += part

        @pl.when(k == nk - 1)
        def _():
            o_ref[...] = acc_ref[...].astype(o_ref.dtype)

    return pl.pallas_call(
        body, name=name, grid=(M // tm, nk),
        in_specs=[pl.BlockSpec((tm, w), lambda i, k: (i, 0)) for w in widths]
        + [pl.BlockSpec((tk, N), lambda i, k: (k, 0))] + [_ANY_SPEC] * n_ord,
        out_specs=pl.BlockSpec((tm, N), lambda i, k: (i, 0)),
        out_shape=jax.ShapeDtypeStruct((M, N), out_dtype),
        scratch_shapes=[pltpu.VMEM((tm, N), F32)],
        compiler_params=_params(("parallel", "arbitrary")),
    )(*parts, b, *([] if order is None else [order]))


def _mm_tn(a, b, name, tm=1024, tn=1024, into=None):
    return _matmul(a, b, "tn", [F32], name, tm=tm, tn=tn, tk=2048, into=into,
                   order=None if into is None else into[0])


def _row_spec(ts, width, col=0):
    return pl.BlockSpec((ts, width), lambda i: (i, col))


def _full_spec(shape):
    return pl.BlockSpec(shape, lambda i: tuple(0 for _ in shape))


def _rms(x):
    return lax.rsqrt(jnp.mean(x * x, axis=-1, keepdims=True) + EPS)


def _rms_bwd(x, dy, g):
    r = _rms(x)
    xh = x * r
    gy = dy * g
    dx = r * (gy - xh * jnp.mean(xh * gy, axis=-1, keepdims=True))
    return dx, dy * xh


def _norm_fwd(x, g, name, order=None):
    S, D = x.shape
    ts = _tile(S, ROW_T, 8)

    def body(x_ref, g_ref, *rest):
        o_ref = rest[-1]
        xv = x_ref[...]
        o_ref[...] = ((xv * _rms(xv)) * g_ref[...]).astype(BF16)

    extra = [] if order is None else [order]
    return pl.pallas_call(
        body, name=name, grid=(S // ts,),
        in_specs=[_row_spec(ts, D), _full_spec((1, D))] + [_ANY_SPEC] * len(extra),
        out_specs=_row_spec(ts, D),
        out_shape=jax.ShapeDtypeStruct((S, D), BF16),
        compiler_params=_params(("parallel",)),
    )(x, g, *extra)


def _norm_bwd(x, dy, g, dres, name, with_bf16=True):
    S, D = x.shape
    ts = _tile(S, ROW_T, 8)

    def body(x_ref, dy_ref, g_ref, dres_ref, dx_ref, *rest):
        dg_ref = rest[-1]
        dx, dg_rows = _rms_bwd(x_ref[...], dy_ref[...], g_ref[...])
        dx = dres_ref[...] + dx
        dx_ref[...] = dx
        if with_bf16:
            rest[0][...] = dx.astype(BF16)

        @pl.when(pl.program_id(0) == 0)
        def _():
            dg_ref[...] = jnp.zeros_like(dg_ref)

        dg_ref[...] += jnp.sum(dg_rows, axis=0, keepdims=True)

    return pl.pallas_call(
        body, name=name, grid=(S // ts,),
        in_specs=[_row_spec(ts, D), _row_spec(ts, D), _full_spec((1, D)), _row_spec(ts, D)],
        out_specs=[_row_spec(ts, D)] * (2 if with_bf16 else 1) + [_full_spec((1, D))],
        out_shape=([jax.ShapeDtypeStruct((S, D), F32)] + [jax.ShapeDtypeStruct((S, D), BF16)] * with_bf16
                   + [jax.ShapeDtypeStruct((1, D), F32)]),
        compiler_params=_params(("arbitrary",)),
    )(x, dy, g, dres)


def _rope(x, c, sa, sb, sign):
    w = x.shape[-1]
    half = MLA_ROPE // 2
    fwd = pltpu.roll(x, w - half, 1)
    back = pltpu.roll(x, half, 1)
    if sign < 0:
        return x * c - fwd * sa - back * sb
    return x * c + fwd * sa + back * sb


def _split3(x):
    hi = x.astype(BF16)
    r1 = x - hi.astype(F32)
    mid = r1.astype(BF16)
    lo = (r1 - mid.astype(F32)).astype(BF16)
    return hi, mid, lo


def _prep_fwd(small, q_norm, kv_norm, bias_pad, kc, ksa, ksb, n_heads, name):
    S, W = small.shape
    QL, KVL = q_norm.shape[1], kv_norm.shape[1]
    assert W == QL + KVL + 2 * LANE
    ts = _tile(S, ROW_T, 8)
    tri = (lax.broadcasted_iota(jnp.int32, (ts, ts), 0) >= lax.broadcasted_iota(jnp.int32, (ts, ts), 1)).astype(BF16)

    def body(s_ref, qn_ref, kvn_ref, b_ref, kc_ref, ksa_ref, ksb_ref, tri_ref,
             cqn_ref, ckvn_ref, kr_ref, cum_ref, carry_ref):
        cq = s_ref[:, 0:QL]
        cqn_ref[...] = ((cq * _rms(cq)) * qn_ref[...]).astype(BF16)
        ckv = s_ref[:, QL:QL + KVL]
        ckvn_ref[...] = ((ckv * _rms(ckv)) * kvn_ref[...]).astype(BF16)
        kr = s_ref[:, QL + KVL:QL + KVL + LANE]
        kr_ref[...] = _rope(kr, kc_ref[...], ksa_ref[...], ksb_ref[...], 1).astype(BF16)
        z = s_ref[:, QL + KVL + LANE:W] + b_ref[...]
        logf = jnp.minimum(z, 0.0) - jnp.log1p(jnp.exp(-jnp.abs(z)))
        lane = lax.broadcasted_iota(jnp.int32, logf.shape, 1)
        logf = jnp.where(lane < n_heads, logf, 0.0)

        @pl.when(pl.program_id(0) == 0)
        def _():
            carry_ref[...] = jnp.zeros_like(carry_ref)

        t = tri_ref[...]
        cum = carry_ref[...]
        for part in _split3(logf):
            cum = cum + jnp.dot(t, part, preferred_element_type=F32)
        cum_ref[...] = cum
        carry_ref[...] = cum[ts - 1:ts, :]

    return pl.pallas_call(
        body, name=name, grid=(S // ts,),
        in_specs=[_row_spec(ts, W), _full_spec((1, QL)), _full_spec((1, KVL)), _full_spec((1, LANE)),
                  _row_spec(ts, LANE), _row_spec(ts, LANE), _row_spec(ts, LANE), _full_spec((ts, ts))],
        out_specs=[_row_spec(ts, QL), _row_spec(ts, KVL), _row_spec(ts, LANE), _row_spec(ts, LANE)],
        out_shape=[jax.ShapeDtypeStruct((S, QL), BF16), jax.ShapeDtypeStruct((S, KVL), BF16),
                   jax.ShapeDtypeStruct((S, LANE), BF16), jax.ShapeDtypeStruct((S, LANE), F32)],
        scratch_shapes=[pltpu.VMEM((1, LANE), F32)],
        compiler_params=_params(("arbitrary",)),
    )(small, q_norm, kv_norm, bias_pad, kc, ksa, ksb, tri)


def _prep_bwd(small, dcqn, dckvn, dkr_heads, dlogf, q_norm, kv_norm, bias_pad, kc, ksa, ksb, n_heads, name):
    S, W = small.shape
    QL, KVL = q_norm.shape[1], kv_norm.shape[1]
    ts = _tile(S, ROW_T, 8)

    def body(s_ref, dcq_ref, dckv_ref, dkr_ref, dlf_ref, qn_ref, kvn_ref, b_ref, kc_ref, ksa_ref, ksb_ref,
             ds_ref, gq_ref, gkv_ref, gb_ref):
        dcq, gq_rows = _rms_bwd(s_ref[:, 0:QL], dcq_ref[...], qn_ref[...])
        ds_ref[:, 0:QL] = dcq.astype(BF16)
        dckv, gkv_rows = _rms_bwd(s_ref[:, QL:QL + KVL], dckv_ref[...], kvn_ref[...])
        ds_ref[:, QL:QL + KVL] = dckv.astype(BF16)
        dkr = dkr_ref[:, 0:LANE]
        for h in range(1, n_heads):
            dkr = dkr + dkr_ref[:, h * LANE:(h + 1) * LANE]
        ds_ref[:, QL + KVL:QL + KVL + LANE] = _rope(dkr, kc_ref[...], ksa_ref[...], ksb_ref[...], -1).astype(BF16)
        z = s_ref[:, QL + KVL + LANE:W] + b_ref[...]
        dff = dlf_ref[...] * (1.0 / (1.0 + jnp.exp(z)))
        ds_ref[:, QL + KVL + LANE:W] = dff.astype(BF16)

        @pl.when(pl.program_id(0) == 0)
        def _():
            gq_ref[...] = jnp.zeros_like(gq_ref)
            gkv_ref[...] = jnp.zeros_like(gkv_ref)
            gb_ref[...] = jnp.zeros_like(gb_ref)

        gq_ref[...] += jnp.sum(gq_rows, axis=0, keepdims=True)
        gkv_ref[...] += jnp.sum(gkv_rows, axis=0, keepdims=True)
        gb_ref[...] += jnp.sum(dff, axis=0, keepdims=True)

    return pl.pallas_call(
        body, name=name, grid=(S // ts,),
        in_specs=[_row_spec(ts, W), _row_spec(ts, QL), _row_spec(ts, KVL), _row_spec(ts, n_heads * LANE),
                  _row_spec(ts, LANE), _full_spec((1, QL)), _full_spec((1, KVL)), _full_spec((1, LANE)),
                  _row_spec(ts, LANE), _row_spec(ts, LANE), _row_spec(ts, LANE)],
        out_specs=[_row_spec(ts, W), _full_spec((1, QL)), _full_spec((1, KVL)), _full_spec((1, LANE))],
        out_shape=[jax.ShapeDtypeStruct((S, W), BF16), jax.ShapeDtypeStruct((1, QL), F32),
                   jax.ShapeDtypeStruct((1, KVL), F32), jax.ShapeDtypeStruct((1, LANE), F32)],
        compiler_params=_params(("arbitrary",)),
    )(small, dcqn, dckvn, dkr_heads, dlogf, q_norm, kv_norm, bias_pad, kc, ksa, ksb)


def _sigmoid(z):
    return 1.0 / (1.0 + jnp.exp(-z))


def _final(h, g, target, name):
    S, D = h.shape
    ts = _tile(S, ROW_T, 8)

    def body(h_ref, g_ref, t_ref, dh_ref, dhb_ref, dg_ref, loss_ref):
        hv = h_ref[...]
        gv = g_ref[...]
        err = (hv * _rms(hv)) * gv - t_ref[...]
        dh, dg_rows = _rms_bwd(hv, err / D, gv)
        dh_ref[...] = dh
        dhb_ref[...] = dh.astype(BF16)

        @pl.when(pl.program_id(0) == 0)
        def _():
            dg_ref[...] = jnp.zeros_like(dg_ref)
            loss_ref[...] = jnp.zeros_like(loss_ref)

        dg_ref[...] += jnp.sum(dg_rows, axis=0, keepdims=True)
        row_loss = jnp.mean(err * err, axis=-1, keepdims=True)
        loss_ref[...] += 0.5 * jnp.sum(row_loss, axis=0, keepdims=True)

    return pl.pallas_call(
        body, name=name, grid=(S // ts,),
        in_specs=[_row_spec(ts, D), _full_spec((1, D)), _row_spec(ts, D)],
        out_specs=[_row_spec(ts, D), _row_spec(ts, D), _full_spec((1, D)), _full_spec((1, LANE))],
        out_shape=[jax.ShapeDtypeStruct((S, D), F32), jax.ShapeDtypeStruct((S, D), BF16),
                   jax.ShapeDtypeStruct((1, D), F32), jax.ShapeDtypeStruct((1, LANE), F32)],
        compiler_params=_params(("arbitrary",)),
    )(h, g, target)


def _suffix_sum_rows(x, name):
    R, S = x.shape
    tb = _tile(S, 512)
    nb = S // tb
    tri = (lax.broadcasted_iota(jnp.int32, (tb, tb), 0) >= lax.broadcasted_iota(jnp.int32, (tb, tb), 1)).astype(BF16)

    def body(x_ref, tri_ref, o_ref, carry_ref):
        @pl.when(pl.program_id(0) == 0)
        def _():
            carry_ref[...] = jnp.zeros_like(carry_ref)

        xv = x_ref[...]
        t = tri_ref[...]
        acc = jnp.broadcast_to(carry_ref[:, 0:1], xv.shape)
        for part in _split3(xv):
            acc = acc + jnp.dot(part, t, preferred_element_type=F32)
        o_ref[...] = acc
        carry_ref[...] = jnp.broadcast_to(acc[:, 0:1], carry_ref.shape)

    rev = pl.BlockSpec((R, tb), lambda i: (0, nb - 1 - i))
    return pl.pallas_call(
        body, name=name, grid=(nb,),
        in_specs=[rev, _full_spec((tb, tb))], out_specs=rev,
        out_shape=jax.ShapeDtypeStruct((R, S), F32),
        scratch_shapes=[pltpu.VMEM((R, LANE), F32)],
        compiler_params=_params(("arbitrary",)),
    )(x, tri)


def _pairs(nb, by_key):
    if by_key:
        pr = [(i, j) for j in range(nb) for i in range(j, nb)]
    else:
        pr = [(i, j) for i in range(nb) for j in range(i + 1)]
    return (jnp.asarray([p[0] for p in pr], jnp.int32), jnp.asarray([p[1] for p in pr], jnp.int32), len(pr))


class _AttT:
    def __init__(self, S, n_heads, q, ks, v, scale, chunk_causal, cum_rep=None, qsub=None):
        self.S, self.H, self.q, self.ks, self.v = S, n_heads, q, ks, v
        self.scale, self.chunk_causal, self.cum_rep = scale, chunk_causal, cum_rep
        self.T = _tile(S, ATT_T)
        self.qs = min(qsub or QSUB, self.T)
        self.nb = S // self.T
        self.dq, self.dv = q[1], v[1]
        self.has_bias = cum_rep is not None

    def q_spec(self, op):
        _, w, off, per_head = op
        return pl.BlockSpec((self.T, w), lambda h, p, it, jt: (it[p], off + (h if per_head else 0)))

    def k_spec(self, op):
        _, w, off, per_head = op
        return pl.BlockSpec((self.T, w), lambda h, p, it, jt: (jt[p], off + (h if per_head else 0)))

    def row_q(self):
        return pl.BlockSpec((None, 1, self.T), lambda h, p, it, jt: (h, 0, it[p]))

    def cum_k(self):
        return pl.BlockSpec((None, self.T, self.qs), lambda h, p, it, jt: (h, jt[p], 0))

    def sub_blocks(self, masked):
        return [(q0, min(self.T, q0 + self.qs) if masked else self.T) for q0 in range(0, self.T, self.qs)]

    def scores(self, k, q_sub, cum, q0, masked):
        s = lax.dot_general(k, q_sub, _NT, preferred_element_type=F32)
        if self.has_bias:
            s = s - cum
        mask = None
        if masked:
            r = lax.broadcasted_iota(jnp.int32, s.shape, 0)
            c = lax.broadcasted_iota(jnp.int32, s.shape, 1) + q0
            mask = (r // CHUNK <= c // CHUNK) if self.chunk_causal else (r <= c)
        return s, mask


def _join(k_refs):
    return k_refs[0][...] if len(k_refs) == 1 else jnp.concatenate([r[...] for r in k_refs], axis=-1)


def _att_fwd_t(att, name, exact=False):
    S, H, T, qs = att.S, att.H, att.T, att.qs
    it, jt, npairs = _pairs(att.nb, by_key=False)
    nk = len(att.ks)

    def body(it_ref, jt_ref, *refs):
        q_ref = refs[0]
        k_refs = refs[1:1 + nk]
        v_ref = refs[1 + nk]
        n = 2 + nk
        cum_ref = None
        if att.has_bias:
            cum_ref = refs[n]
            n += 1
        o_ref = refs[n]
        n += 1
        ox_ref = None
        if exact:
            ox_ref = refs[n]
            n += 1
        lse_ref, m_ref, l_ref, acc_ref = refs[n:n + 4]
        lo_ref = refs[n + 4] if exact else None
        p = pl.program_id(1)
        i, j = it_ref[p], jt_ref[p]

        @pl.when(j == 0)
        def _():
            m_ref[...] = jnp.full_like(m_ref, -jnp.inf)
            l_ref[...] = jnp.zeros_like(l_ref)
            acc_ref[...] = jnp.zeros_like(acc_ref)
            if exact:
                lo_ref[...] = jnp.zeros_like(lo_ref)

        def step(masked):
            k = _join(k_refs)
            v = v_ref[...]
            subs = att.sub_blocks(masked)

            def logits(idx):
                q0, nkeys = subs[idx]
                cum = cum_ref[0:nkeys, :] if att.has_bias else None
                return att.scores(k[0:nkeys], q_ref[q0:q0 + qs, :], cum, q0, masked)

            ahead = logits(0)
            for idx, (q0, nkeys) in enumerate(subs):
                qsl = slice(q0, q0 + qs)
                s, mask = ahead
                if idx + 1 < len(subs):
                    ahead = logits(idx + 1)
                if masked:
                    s = jnp.where(mask, s, -jnp.inf)
                m_prev = m_ref[:, qsl]
                m_new = jnp.maximum(m_prev, jnp.max(s, axis=0, keepdims=True))
                alpha = jnp.exp2(m_prev - m_new)
                pr = jnp.exp2(s - m_new)
                l_ref[:, qsl] = alpha * l_ref[:, qsl] + jnp.sum(pr, axis=0, keepdims=True)
                p_hi = pr.astype(BF16)
                acc_ref[:, qsl] = alpha * acc_ref[:, qsl] + lax.dot_general(
                    v[0:nkeys], p_hi, _TN, preferred_element_type=F32)
                if exact:
                    p_lo = (pr - p_hi.astype(F32)).astype(BF16)
                    lo_ref[:, qsl] = alpha * lo_ref[:, qsl] + lax.dot_general(
                        v[0:nkeys], p_lo, _TN, preferred_element_type=F32)
                m_ref[:, qsl] = m_new

        @pl.when(j < i)
        def _():
            step(False)

        @pl.when(j == i)
        def _():
            step(True)
            l = l_ref[...]
            inv = 1.0 / l
            o_ref[...] = jnp.transpose(acc_ref[...] * inv).astype(o_ref.dtype)
            if exact:
                ox_ref[...] = jnp.transpose((acc_ref[...] + lo_ref[...]) * inv)
            lse_ref[...] = m_ref[...] + jnp.log2(l)

    in_specs = [att.q_spec(att.q)] + [att.k_spec(k) for k in att.ks] + [att.k_spec(att.v)]
    args = [att.q[0]] + [k[0] for k in att.ks] + [att.v[0]]
    if att.has_bias:
        in_specs.append(att.cum_k())
        args.append(att.cum_rep)
    o_spec = pl.BlockSpec((T, att.dv), lambda h, p, it, jt: (it[p], h))
    out_specs = [o_spec]
    out_shape = [jax.ShapeDtypeStruct((S, H * att.dv), BF16)]
    scratch = [pltpu.VMEM((1, T), F32), pltpu.VMEM((1, T), F32), pltpu.VMEM((att.dv, T), F32)]
    if exact:
        out_specs.append(o_spec)
        out_shape.append(jax.ShapeDtypeStruct((S, H * att.dv), F32))
        scratch.append(pltpu.VMEM((att.dv, T), F32))
    out_specs.append(att.row_q())
    out_shape.append(jax.ShapeDtypeStruct((H, 1, S), F32))
    return pl.pallas_call(
        body, name=name,
        grid_spec=pltpu.PrefetchScalarGridSpec(
            num_scalar_prefetch=2, grid=(H, npairs), in_specs=in_specs, out_specs=out_specs,
            scratch_shapes=scratch),
        out_shape=out_shape,
        compiler_params=_params(("parallel", "arbitrary")),
    )(it, jt, *args)


def _att_bwd_t(att, do, lse, o, dq_dtype, dk_dtypes, name, dq_rope=None, order=None):
    S, H, T, qs = att.S, att.H, att.T, att.qs
    it, jt, npairs = _pairs(att.nb, by_key=True)
    nk = len(att.ks)
    last = att.nb - 1
    widths = [k[1] for k in att.ks]

    def body(it_ref, jt_ref, *refs):
        q_ref = refs[0]
        k_refs = refs[1:1 + nk]
        v_ref, do_ref, lse_ref, o_ref = refs[1 + nk:5 + nk]
        n = 5 + nk
        cum_ref = None
        if att.has_bias:
            cum_ref = refs[n]
            n += 1
        rope_refs = None
        if dq_rope is not None:
            rope_refs = refs[n:n + 3]
            n += 3
        if order is not None:
            n += 1
        dl_acc = refs[-1]
        dq_ref = refs[n]
        dk_refs = refs[n + 1:n + 1 + nk]
        dv_ref = refs[n + 1 + nk]
        n += nk + 2
        dc_ref = None
        if att.has_bias:
            dc_ref = refs[n]
            n += 1
        dq_acc, dk_acc, dv_acc = refs[n:n + 3]
        dc_acc = refs[n + 3] if att.has_bias else None
        p = pl.program_id(1)
        i, j = it_ref[p], jt_ref[p]

        @pl.when(p == 0)
        def _():
            dq_acc[...] = jnp.zeros_like(dq_acc)

        @pl.when(i == j)
        def _():
            dk_acc[...] = jnp.zeros_like(dk_acc)
            dv_acc[...] = jnp.zeros_like(dv_acc)
            if att.has_bias:
                dc_acc[...] = jnp.zeros_like(dc_acc)

        @pl.when(j == 0)
        def _():
            prod = do_ref[...].astype(F32) * o_ref[...].astype(F32)
            ones = jnp.ones((8, att.dv), BF16)
            rows = jnp.zeros((8, T), F32)
            for part in _split3(prod):
                rows = rows + lax.dot_general(ones, part, _NT, preferred_element_type=F32)
            dl_acc[i] = rows[0:1, :]

        def step(masked):
            k = _join(k_refs)
            v = v_ref[...]
            dl = dl_acc[i]
            subs = att.sub_blocks(masked)

            def logits(idx):
                q0, nkeys = subs[idx]
                cum = cum_ref[0:nkeys, :] if att.has_bias else None
                return att.scores(k[0:nkeys], q_ref[q0:q0 + qs, :], cum, q0, masked)

            ahead = logits(0)
            for idx, (q0, nkeys) in enumerate(subs):
                qsl = slice(q0, q0 + qs)
                ksl = slice(0, nkeys)
                q_sub = q_ref[qsl, :]
                do_sub = do_ref[qsl, :]
                s, mask = ahead
                if idx + 1 < len(subs):
                    ahead = logits(idx + 1)
                pr = jnp.exp2(s - lse_ref[:, qsl])
                if masked:
                    pr = jnp.where(mask, pr, 0.0)
                dp = lax.dot_general(v[ksl], do_sub, _NT, preferred_element_type=F32)
                ds = pr * (dp - dl[:, qsl])
                ds_b = ds.astype(BF16)
                dv_acc[ksl, :] += jnp.dot(pr.astype(BF16), do_sub, preferred_element_type=F32)
                dk_acc[ksl, :] += jnp.dot(ds_b, q_sub, preferred_element_type=F32)
                dq_acc[i, :, qsl] += lax.dot_general(k[ksl], ds_b, _TN, preferred_element_type=F32)
                if att.has_bias:
                    part = ds[:, 0:LANE] if qs >= LANE else ds
                    for c0 in range(LANE, qs, LANE):
                        part = part + ds[:, c0:c0 + LANE]
                    dc_acc[ksl, :] += part

        @pl.when(i > j)
        def _():
            step(False)

        @pl.when(i == j)
        def _():
            step(True)
            dq = jnp.transpose(dq_acc[i] * att.scale)
            if dq_rope is not None:
                dq = _rope(dq, rope_refs[0][...], rope_refs[1][...], rope_refs[2][...], -1)
            dq_ref[...] = dq.astype(dq_ref.dtype)

        @pl.when(i == last)
        def _():
            dk = dk_acc[...] * (1.0 / LOG2E)
            off = 0
            for r, w in zip(dk_refs, widths):
                r[...] = dk[:, off:off + w].astype(r.dtype)
                off += w
            dv_ref[...] = dv_acc[...].astype(dv_ref.dtype)
            if att.has_bias:
                dc_ref[...] = -jnp.sum(dc_acc[...], axis=-1, keepdims=True)

    do_op = (do, att.dv, 0, True)
    o_spec = pl.BlockSpec((T, att.dv), lambda h, p, it, jt: (jnp.where(jt[p] == 0, it[p], last), h))
    in_specs = ([att.q_spec(att.q)] + [att.k_spec(k) for k in att.ks]
                + [att.k_spec(att.v), att.q_spec(do_op), att.row_q(), o_spec])
    args = [att.q[0]] + [k[0] for k in att.ks] + [att.v[0], do, lse, o]
    if att.has_bias:
        in_specs.append(att.cum_k())
        args.append(att.cum_rep)
    if dq_rope is not None:
        in_specs += [pl.BlockSpec((T, att.dq), lambda h, p, it, jt: (jt[p], 0))] * 3
        args += list(dq_rope)
    if order is not None:
        in_specs.append(_ANY_SPEC)
        args.append(order)
    out_specs = [pl.BlockSpec((T, att.dq), lambda h, p, it, jt: (jt[p], h))]
    out_shape = [jax.ShapeDtypeStruct((S, H * att.dq), dq_dtype)]
    out_specs += [pl.BlockSpec((T, w), lambda h, p, it, jt: (jt[p], h)) for w in widths]
    out_shape += [jax.ShapeDtypeStruct((S, H * w), dt) for w, dt in zip(widths, dk_dtypes)]
    out_specs.append(pl.BlockSpec((T, att.dv), lambda h, p, it, jt: (jt[p], h)))
    out_shape.append(jax.ShapeDtypeStruct((S, H * att.dv), BF16))
    scratch = [pltpu.VMEM((att.nb, att.dq, T), F32), pltpu.VMEM((T, att.dq), F32), pltpu.VMEM((T, att.dv), F32)]
    if att.has_bias:
        out_specs.append(pl.BlockSpec((None, T, 1), lambda h, p, it, jt: (h, jt[p], 0)))
        out_shape.append(jax.ShapeDtypeStruct((H, S, 1), F32))
        scratch.append(pltpu.VMEM((T, min(qs, LANE)), F32))
    scratch.append(pltpu.VMEM((att.nb, 1, T), F32))
    return pl.pallas_call(
        body, name=name,
        grid_spec=pltpu.PrefetchScalarGridSpec(
            num_scalar_prefetch=2, grid=(H, npairs), in_specs=in_specs, out_specs=out_specs,
            scratch_shapes=scratch),
        out_shape=out_shape,
        compiler_params=_params(("parallel", "arbitrary")),
    )(it, jt, *args)


def _adamw(w, g1, g2, m, v, name, g_row=None):
    _, K, N = w.shape
    by_rows = K % 8 == 0
    tr = _tile(K, 256, 8) if by_rows else K
    if g_row is None:
        assert g1.shape == (K, N) and g2.shape == (K, N), name
        g_row = 0
    assert by_rows and g_row % tr == 0 or g_row == 0, name
    g_blk = g_row // tr
    tc = N if by_rows else _tile(N, LANE)
    c1 = 1.0 - ADAM_B1 ** ADAM_STEP
    c2 = 1.0 - ADAM_B2 ** ADAM_STEP

    def body(w_ref, g1_ref, g2_ref, m_ref, v_ref, g_ref, d_ref, nm_ref, nv_ref):
        gv = g1_ref[...] + g2_ref[...]
        nm = ADAM_B1 * m_ref[...] + (1.0 - ADAM_B1) * gv
        nv = ADAM_B2 * v_ref[...] + (1.0 - ADAM_B2) * (gv * gv)
        g_ref[...] = gv
        d_ref[...] = -ADAM_LR * ((nm / c1) / (jnp.sqrt(nv / c2) + ADAM_EPS) + ADAM_WD * w_ref[...])
        nm_ref[...] = nm
        nv_ref[...] = nv

    if by_rows:
        blk = pl.BlockSpec((None, tr, N), lambda i: (0, i, 0))
        gblk = pl.BlockSpec((tr, N), lambda i: (g_blk + i, 0))
    else:
        blk = pl.BlockSpec((None, K, tc), lambda i: (0, 0, i))
        gblk = pl.BlockSpec((K, tc), lambda i: (0, i))
    return pl.pallas_call(
        body, name=name, grid=(K // tr if by_rows else N // tc,),
        in_specs=[blk, gblk, gblk, blk, blk], out_specs=[blk] * 4,
        out_shape=[jax.ShapeDtypeStruct((1, K, N), F32)] * 4,
        compiler_params=_params(("parallel",)),
    )(w, g1, g2, m, v)


_HBM_SPEC = pl.BlockSpec(memory_space=pltpu.HBM)
_SEM_SPEC = pl.BlockSpec(memory_space=pltpu.SEMAPHORE)
_VMEM_SPEC = pl.BlockSpec(memory_space=pltpu.VMEM)
_EFFECT = pltpu.SideEffectType.DATAFLOW_SIDE_EFFECTING


def _place():
    return lax.axis_index("x"), lax.axis_index("y"), lax.axis_index("c")


def _other_chips(x, y):
    return [(1 - x, y), (x, 1 - y), (1 - x, 1 - y)]


def _chip_copies(src_ref, land_ref, sems, gather):
    x, y, c = _place()
    me = 2 * x + y
    out, back = [], []
    if gather == "half":
        half = src_ref.shape[0] // 2
        mine = pl.ds(pl.multiple_of(c * half, 16), half)
    for n, (px, py) in enumerate(_other_chips(x, y)):
        if gather == "half":
            src, there, here = src_ref.at[mine], land_ref.at[me, mine], land_ref.at[2 * px + py, mine]
        elif gather:
            src, there, here = src_ref, land_ref.at[me], land_ref.at[2 * px + py]
        else:
            src, there, here = src_ref.at[2 * px + py], land_ref.at[n], land_ref.at[n]
        out.append(pltpu.make_async_remote_copy(
            src_ref=src, dst_ref=there, send_sem=sems[n], recv_sem=sems[3 + n],
            device_id=(px, py, c), device_id_type=MESH))
        back.append(pltpu.make_async_remote_copy(
            src_ref=src, dst_ref=here, send_sem=sems[n], recv_sem=sems[3 + n],
            device_id=(px, py, c), device_id_type=MESH))
    return out, back


def _xchg_start(src, land, gather, order, name):
    def body(src_ref, land_ref, order_ref, *outs):
        sems = outs[0:6]
        token = outs[8]
        out, _ = _chip_copies(src_ref, land_ref, sems, gather)
        for cp in out:
            cp.start()
        token[...] = jnp.zeros_like(token)

    outs = pl.pallas_call(
        body, name=name,
        out_shape=(pltpu.SemaphoreType.DMA(()),) * 6 + (
            pltpu.HBM(src.shape, src.dtype), pltpu.HBM(land.shape, land.dtype),
            jax.ShapeDtypeStruct((8, LANE), F32)),
        in_specs=(_HBM_SPEC, _HBM_SPEC, _ANY_SPEC),
        out_specs=(_SEM_SPEC,) * 6 + (_HBM_SPEC, _HBM_SPEC, _VMEM_SPEC),
        input_output_aliases={0: 6, 1: 7},
        compiler_params=pltpu.CompilerParams(has_side_effects=_EFFECT),
    )(pltpu.with_memory_space_constraint(src, pltpu.HBM), pltpu.with_memory_space_constraint(land, pltpu.HBM), order)
    return outs[0:6], outs[6], outs[7], outs[8]


def _xchg_wait(started, gather, after, name):
    sems, src, land, _ = started
    after = after if isinstance(after, tuple) else (after,)

    def body(src_ref, land_ref, *rest):
        _, back = _chip_copies(src_ref, land_ref, rest[0:6], gather)
        for cp in back:
            cp.wait_send()
            cp.wait_recv()

    return pl.pallas_call(
        body, name=name,
        out_shape=(pltpu.HBM(src.shape, src.dtype), pltpu.HBM(land.shape, land.dtype)),
        in_specs=(_HBM_SPEC, _HBM_SPEC) + (_SEM_SPEC,) * 6 + (_ANY_SPEC,) * len(after),
        out_specs=(_HBM_SPEC, _HBM_SPEC),
        input_output_aliases={0: 0, 1: 1},
        compiler_params=pltpu.CompilerParams(has_side_effects=_EFFECT),
    )(src, land, *sems, *after)


def _forward_halves(land, name):
    _, R, C = land.shape
    half = R // 2
    assert half % 16 == 0

    def body(land_ref, out_ref, send_sems, recv_sems):
        x, y, c = _place()
        mine = pl.ds(pl.multiple_of(c * half, 16), half)
        theirs = pl.ds(pl.multiple_of((1 - c) * half, 16), half)
        sends = []
        for n, (px, py) in enumerate(_other_chips(x, y)):
            cp = pltpu.make_async_remote_copy(
                src_ref=land_ref.at[2 * px + py, mine], dst_ref=out_ref.at[2 * px + py, mine],
                send_sem=send_sems.at[n], recv_sem=recv_sems.at[n], device_id=(x, y, 1 - c), device_id_type=MESH)
            cp.start()
            sends.append(cp)
        for n, (px, py) in enumerate(_other_chips(x, y)):
            pltpu.make_async_remote_copy(
                src_ref=land_ref.at[2 * px + py, theirs], dst_ref=out_ref.at[2 * px + py, theirs],
                send_sem=send_sems.at[n], recv_sem=recv_sems.at[n], device_id=(x, y, 1 - c),
                device_id_type=MESH).wait_recv()
        for cp in sends:
            cp.wait_send()

    return pl.pallas_call(
        body, name=name,
        in_specs=[_ANY_SPEC], out_specs=_ANY_SPEC,
        out_shape=jax.ShapeDtypeStruct(land.shape, land.dtype),
        input_output_aliases={0: 0},
        scratch_shapes=[pltpu.SemaphoreType.DMA((3,)), pltpu.SemaphoreType.DMA((3,))],
    )(land)


def _sib_copy(src_ref, land_ref, send_sem, recv_sem):
    x, y, c = _place()
    return pltpu.make_async_remote_copy(src_ref=src_ref, dst_ref=land_ref, send_sem=send_sem, recv_sem=recv_sem,
                                        device_id=(x, y, 1 - c), device_id_type=MESH)


def _sib_start(src, name):
    land = lax.empty(src.shape, src.dtype)

    def body(src_ref, land_ref, send_sem, recv_sem, src_thru, land_thru, token):
        _sib_copy(src_ref, land_ref, send_sem, recv_sem).start()
        token[...] = jnp.zeros_like(token)

    return pl.pallas_call(
        body, name=name,
        out_shape=(pltpu.SemaphoreType.DMA(()), pltpu.SemaphoreType.DMA(()),
                   pltpu.HBM(src.shape, src.dtype), pltpu.HBM(land.shape, land.dtype),
                   jax.ShapeDtypeStruct((8, LANE), F32)),
        in_specs=(_HBM_SPEC, _HBM_SPEC),
        out_specs=(_SEM_SPEC, _SEM_SPEC, _HBM_SPEC, _HBM_SPEC, _VMEM_SPEC),
        input_output_aliases={0: 2, 1: 3},
        compiler_params=pltpu.CompilerParams(has_side_effects=_EFFECT),
    )(pltpu.with_memory_space_constraint(src, pltpu.HBM), pltpu.with_memory_space_constraint(land, pltpu.HBM))


def _sib_wait(started, after, name):
    send_sem, recv_sem, src, land, _ = started

    def body(src_ref, land_ref, send_sem, recv_sem, after_ref, src_out, land_out):
        cp = _sib_copy(src_ref, land_ref, send_sem, recv_sem)
        cp.wait_send()
        cp.wait_recv()

    return pl.pallas_call(
        body, name=name,
        out_shape=(pltpu.HBM(src.shape, src.dtype), pltpu.HBM(land.shape, land.dtype)),
        in_specs=(_HBM_SPEC, _HBM_SPEC, _SEM_SPEC, _SEM_SPEC, _ANY_SPEC),
        out_specs=(_HBM_SPEC, _HBM_SPEC),
        input_output_aliases={0: 0, 1: 1},
        compiler_params=pltpu.CompilerParams(has_side_effects=_EFFECT),
    )(src, land, send_sem, recv_sem, after)


def _sum_slabs(gp, recv, chip, name):
    _, R, C = gp.shape
    tr = _tile(R, PACK_ROWS, 16)

    def body(chip_ref, own_ref, r0_ref, r1_ref, r2_ref, o_ref):
        acc = own_ref[...].astype(F32) + r0_ref[...].astype(F32)
        o_ref[...] = (acc + r1_ref[...].astype(F32)) + r2_ref[...].astype(F32)

    def got(n):
        return pl.BlockSpec((None, tr, C), lambda i, chip_ref: (n, i, 0))

    return pl.pallas_call(
        body, name=name,
        grid_spec=pltpu.PrefetchScalarGridSpec(
            num_scalar_prefetch=1, grid=(R // tr,),
            in_specs=[pl.BlockSpec((None, tr, C), lambda i, chip_ref: (chip_ref[0], i, 0)), got(0), got(1), got(2)],
            out_specs=pl.BlockSpec((tr, C), lambda i, chip_ref: (i, 0))),
        out_shape=jax.ShapeDtypeStruct((R, C), F32),
        compiler_params=_params(("parallel",)),
    )(jnp.reshape(chip, (1,)).astype(jnp.int32), gp, recv, recv, recv)


def _all_reduce_vec(vec, name):
    VR, W = vec.shape

    def body(vec_ref, vall_ref, vout_ref, vsend_sems, vrecv_sems):
        x, y, c = _place()
        vall_ref[4 * x + 2 * y + c] = vec_ref[...]
        sends = []
        peers = []
        for r in range(1, N_DEV):
            dx, dy, dc = (r >> 2) & 1, (r >> 1) & 1, r & 1
            peer = (x ^ dx, y ^ dy, c ^ dc)
            peers.append(peer)
            cp = pltpu.make_async_remote_copy(
                src_ref=vec_ref, dst_ref=vall_ref.at[4 * x + 2 * y + c], send_sem=vsend_sems.at[r - 1],
                recv_sem=vrecv_sems.at[r - 1], device_id=peer, device_id_type=MESH)
            cp.start()
            sends.append(cp)
        for r, peer in enumerate(peers):
            pltpu.make_async_remote_copy(
                src_ref=vec_ref, dst_ref=vall_ref.at[4 * peer[0] + 2 * peer[1] + peer[2]],
                send_sem=vsend_sems.at[r], recv_sem=vrecv_sems.at[r],
                device_id=peer, device_id_type=MESH).wait_recv()
        total = vall_ref[0]
        for d in range(1, N_DEV):
            total = total + vall_ref[d]
        vout_ref[...] = total
        for cp in sends:
            cp.wait_send()

    outs = pl.pallas_call(
        body, name=name,
        in_specs=[_VMEM_SPEC], out_specs=[_VMEM_SPEC, _VMEM_SPEC],
        out_shape=[jax.ShapeDtypeStruct((N_DEV, VR, W), F32), jax.ShapeDtypeStruct((VR, W), F32)],
        scratch_shapes=[pltpu.SemaphoreType.DMA((N_DEV - 1,)), pltpu.SemaphoreType.DMA((N_DEV - 1,))],
    )(vec)
    return outs[1]


class _Pack:
    def __init__(self, group, C):
        self.group, self.C = group, C
        self.rows, self.offs, off = {}, {}, 0
        for nm, (K, N), _ in group:
            assert N <= C, nm
            self.rows[nm] = K if 2 * N > C else -(-(K * N) // C)
            self.offs[nm] = off
            off += -(-self.rows[nm] // 16) * 16
        self.used = off
        self.R = -(-off // PACK_ROWS) * PACK_ROWS

    def _rows_of(self, a):
        K, N = a.shape
        if 2 * N > self.C:
            a = jnp.pad(a, ((0, 0), (0, self.C - N)))
        else:
            a = jnp.pad(a.reshape(-1), (0, -(K * N) % self.C)).reshape(-1, self.C)
        return jnp.pad(a, ((0, -a.shape[0] % 16), (0, 0)))

    def pack(self, shards):
        parts = [self._rows_of(shards[nm].astype(BF16)) for nm, _, _ in self.group]
        return jnp.concatenate(parts + [jnp.zeros((self.R - self.used, self.C), BF16)], axis=0)

    def _shard_of(self, rows, shape):
        K, N = shape
        return rows[:, :N] if 2 * N > self.C else rows.reshape(-1)[:K * N].reshape(K, N)

    def part(self, flat, nm, shape):
        return self._shard_of(flat[self.offs[nm]:self.offs[nm] + self.rows[nm]], shape)

    def slab_rows(self, nm, g):
        (K, N), axis = next((shape, axis) for n, shape, axis in self.group if n == nm)
        cuts = [g[:, k * N:(k + 1) * N] if axis == 1 else g[k * K:(k + 1) * K, :] for k in range(N_CHIPS)]
        return jnp.stack([self._rows_of(c.astype(BF16)) for c in cuts])

    def slabs(self, grads):
        parts = [self.slab_rows(nm, grads[nm]) for nm, _, _ in self.group]
        return jnp.concatenate(parts + [jnp.zeros((N_CHIPS, self.R - self.used, self.C), BF16)], axis=1)

    def full(self, gathered, names=None):
        res = {}
        for nm, (K, N), axis in self.group:
            if names is None or nm in names:
                rows = gathered[:, self.offs[nm]:self.offs[nm] + self.rows[nm]]
                res[nm] = jnp.concatenate([self._shard_of(rows[k], (K, N)) for k in range(N_CHIPS)], axis=axis)
        return res


def _rope_tables(S):
    pos = jnp.arange(S, dtype=F32)
    inv = 1.0 / (ROPE_THETA ** (jnp.arange(0, MLA_ROPE, 2, dtype=F32) / MLA_ROPE))
    ang = pos[:, None] * inv[None, :]
    cos, sin = jnp.cos(ang), jnp.sin(ang)
    half = MLA_ROPE // 2
    z = jnp.zeros((S, half), F32)
    one = jnp.ones((S, LANE - MLA_ROPE), F32)
    zero = jnp.zeros((S, LANE - MLA_ROPE), F32)
    kc = jnp.concatenate([cos, cos, one], axis=1)
    ksa = jnp.concatenate([-sin, z, zero], axis=1)
    ksb = jnp.concatenate([z, sin, zero], axis=1)
    qc = jnp.concatenate([jnp.ones((S, MLA_NOPE), F32), kc], axis=1)
    qsa = jnp.concatenate([jnp.zeros((S, MLA_NOPE), F32), ksa], axis=1)
    qsb = jnp.concatenate([jnp.zeros((S, MLA_NOPE), F32), ksb], axis=1)
    return (kc, ksa, ksb), (qc, qsa, qsb)


def _pad_cols(a, width):
    return jnp.pad(a, ((0, 0), (0, width - a.shape[1])))


def kernel(x, attn_norm, w_in, fox_f_bias, q_norm, w_uq, kv_norm, w_ukv, w_mla_branch, w_fox_branch, w_out, mlp_norm, w_up, w_down, final_norm, loss_target, m_attn_norm, m_w_in, m_fox_f_bias, m_q_norm, m_w_uq, m_kv_norm, m_w_ukv, m_w_mla_branch, m_w_fox_branch, m_w_out, m_mlp_norm, m_w_up, m_w_down, m_final_norm, v_attn_norm, v_w_in, v_fox_f_bias, v_q_norm, v_w_uq, v_kv_norm, v_w_ukv, v_w_mla_branch, v_w_fox_branch, v_w_out, v_mlp_norm, v_w_up, v_w_down, v_final_norm):
    _, S, D = x.shape
    H, HF = MLA_HEADS, FOX_HEADS
    QL, KVL = MLA_Q_LORA, MLA_KV_LORA
    assert H == HF and H <= 8
    xs = x[0]
    target = loss_target[0]
    C = D
    chip = 2 * lax.axis_index("x") + lax.axis_index("y")

    def flip(a):
        return jnp.transpose(a, (0, 2, 1))

    w_in, m_w_in, v_w_in = flip(w_in), flip(m_w_in), flip(v_w_in)
    weights = {"attn_norm": attn_norm, "w_in": w_in, "fox_f_bias": fox_f_bias, "q_norm": q_norm, "w_uq": w_uq,
               "kv_norm": kv_norm, "w_ukv": w_ukv, "w_mla_branch": w_mla_branch, "w_fox_branch": w_fox_branch,
               "w_out": w_out, "mlp_norm": mlp_norm, "w_up": w_up, "w_down": w_down, "final_norm": final_norm}
    moments = {"attn_norm": (m_attn_norm, v_attn_norm), "w_in": (m_w_in, v_w_in), "fox_f_bias": (m_fox_f_bias, v_fox_f_bias),
               "q_norm": (m_q_norm, v_q_norm), "w_uq": (m_w_uq, v_w_uq), "kv_norm": (m_kv_norm, v_kv_norm),
               "w_ukv": (m_w_ukv, v_w_ukv), "w_mla_branch": (m_w_mla_branch, v_w_mla_branch),
               "w_fox_branch": (m_w_fox_branch, v_w_fox_branch), "w_out": (m_w_out, v_w_out),
               "mlp_norm": (m_mlp_norm, v_mlp_norm), "w_up": (m_w_up, v_w_up), "w_down": (m_w_down, v_w_down),
               "final_norm": (m_final_norm, v_final_norm)}

    def group(names_axes):
        return [(nm, weights[nm].shape[1:], axis) for nm, axis in names_axes]

    pack_a = _Pack(group([("w_in", 0), ("w_uq", 1), ("w_ukv", 1)]), C)
    pack_b = _Pack(group([("w_down", 0), ("w_up", 1), ("w_out", 0), ("w_mla_branch", 1), ("w_fox_branch", 1)]), C)
    RA, RB = pack_a.R, pack_b.R
    wp_a = pack_a.pack({nm: weights[nm][0] for nm, _, _ in pack_a.group})
    wp_b = pack_b.pack({nm: weights[nm][0] for nm, _, _ in pack_b.group})
    n_in = w_in.shape[1]
    rows_in = -(-n_in // 16) * 16
    assert pack_a.offs["w_in"] == 0 and all((k * n_in) % 16 + n_in <= rows_in for k in range(N_CHIPS))
    shifted = lax.dynamic_update_slice(jnp.zeros((rows_in, C), BF16), wp_a[:n_in], ((chip * n_in) % 16, 0))
    wp_a = jnp.concatenate([shifted, wp_a[rows_in:]], axis=0)
    ag_a = _xchg_start(wp_a, lax.empty((N_CHIPS, RA, C), BF16), "half", jnp.zeros((8, LANE), F32), "all_gather_start_a")
    xn = _norm_fwd(xs, attn_norm, "attn_norm_fwd", order=ag_a[3])
    own_a, land_a = _xchg_wait(ag_a, "half", (xn, wp_b), "all_gather_wait_a")
    land_a = _forward_halves(land_a, "all_gather_forward_a")
    gathered_a = lax.dynamic_update_slice(land_a, own_a[None], (chip, 0, 0))
    ag_b = _xchg_start(wp_b, lax.empty((N_CHIPS, RB, C), BF16), True, gathered_a, "all_gather_start_b")
    full = pack_a.full(gathered_a, ("w_uq", "w_ukv"))
    tile0 = [(k * n_in) // 16 * 16 for k in range(N_CHIPS)]
    total = tile0[-1] + rows_in
    full["w_in"] = sum(jnp.pad(gathered_a[k, :rows_in], ((tile0[k], total - tile0[k] - rows_in), (0, 0)))
                       for k in range(N_CHIPS))

    o_ckv = QL
    o_kr = o_ckv + KVL
    o_fq = o_kr + MLA_ROPE
    o_ff = o_fq + 3 * HF * FOX_HEAD_DIM
    o_g = o_ff + HF
    wi = full["w_in"]
    assert N_CHIPS * n_in == o_g + 2 * D and wi.shape[0] >= o_g + 2 * D
    WS = QL + KVL + 2 * LANE
    NQKV = 3 * HF * FOX_HEAD_DIM

    def pad_rows(a, rows):
        return jnp.pad(a, ((0, rows - a.shape[0]), (0, 0)))

    w_small = jnp.concatenate([wi[:o_kr], pad_rows(wi[o_kr:o_fq], LANE), pad_rows(wi[o_ff:o_g], LANE)], axis=0)
    w_qkv = wi[o_fq:o_ff]
    w_g = wi[o_g:o_g + 2 * D]
    w_pack = jnp.concatenate([w_small, w_qkv, w_g], axis=0)
    dqk = MLA_NOPE + MLA_ROPE
    w_uq_p = jnp.pad(full["w_uq"].reshape(QL, H, dqk), ((0, 0), (0, 0), (0, QPAD - dqk))).reshape(QL, H * QPAD)
    ukv = full["w_ukv"].reshape(KVL, H, MLA_NOPE + MLA_V)
    w_ukv_p = jnp.concatenate([ukv[:, :, :MLA_NOPE].reshape(KVL, H * MLA_NOPE),
                               ukv[:, :, MLA_NOPE:].reshape(KVL, H * MLA_V)], axis=1)

    (kc, ksa, ksb), (qc, qsa, qsb) = _rope_tables(S)
    bias_pad = _pad_cols(fox_f_bias, LANE)

    small = _matmul(xn, w_small, "nt", [F32], "proj_small")
    n_fq = HF * FOX_HEAD_DIM
    q_scale = jnp.concatenate([jnp.full((1, n_fq), LOG2E / math.sqrt(FOX_HEAD_DIM), F32),
                               jnp.ones((1, NQKV - n_fq), F32)], axis=1)
    qkv = _matmul(xn, w_qkv, "nt", [BF16], "proj_qkv", col_extras=(q_scale,), epilogue=lambda acc, cs: (acc * cs,))
    gpre = _matmul(xn, w_g, "nt", [F32], "proj_gates")
    cqn, ckvn, kr, cum = _prep_fwd(small, q_norm, kv_norm, bias_pad, kc, ksa, ksb, HF, "prep_fwd")
    c2_mla = LOG2E / math.sqrt(dqk)
    q_rot = _matmul(cqn, w_uq_p, "nn", [BF16], "mla_q_up", tn=QPAD, row_extras=(qc * c2_mla, qsa * c2_mla, qsb * c2_mla),
                    epilogue=lambda acc, c, sa, sb: (_rope(acc, c, sa, sb, 1),))
    kv2 = _matmul(ckvn, w_ukv_p, "nn", [BF16], "mla_kv_up")

    def mla_att(qsub):
        return _AttT(S, H, (q_rot, QPAD, 0, True), [(kv2, MLA_NOPE, 0, True), (kr, LANE, 0, False)],
                     (kv2, MLA_V, H, True), 1.0 / math.sqrt(dqk), True, qsub=qsub)

    mla = mla_att(QSUB)
    o_mla, lse_mla = _att_fwd_t(mla_att(2 * QSUB), "mla_att_fwd")

    cum_t = jnp.transpose(cum[:, :HF]) * LOG2E
    cum_rep = jnp.broadcast_to(cum_t[:, :, None], (HF, S, min(QSUB, _tile(S, ATT_T))))
    fox = _AttT(S, HF, (qkv, FOX_HEAD_DIM, 0, True), [(qkv, FOX_HEAD_DIM, HF, True)],
                (qkv, FOX_HEAD_DIM, 2 * HF, True), 1.0 / math.sqrt(FOX_HEAD_DIM), False, cum_rep)
    o_fox, ox_fox, lse_fox = _att_fwd_t(fox, "fox_att_fwd", exact=True)

    own_b, land_b = _xchg_wait(ag_b, True, (lse_fox, lse_mla, gpre), "all_gather_wait_b")
    gathered_b = lax.dynamic_update_slice(land_b, own_b[None], (chip, 0, 0))
    full.update(pack_b.full(gathered_b, ("w_mla_branch", "w_fox_branch", "w_out")))
    w_mb, w_fb, w_o = (full[n] for n in ("w_mla_branch", "w_fox_branch", "w_out"))

    def b_of(nm, mode, tn, tk):
        (K, N), axis = next((shape, axis) for n, shape, axis in pack_b.group if n == nm)
        off = pack_b.offs[nm]
        shape = (N_CHIPS * K, N) if axis == 0 else (K, N_CHIPS * N)
        t_r, t_c = (tk, tn) if mode == "nn" else (tn, tk)
        t_r, t_c = _tile(shape[0], t_r), _tile(shape[1], t_c)
        if not (N == C and K % t_r == 0 and N % t_c == 0 and off % t_r == 0):
            return pack_b.full(gathered_b, (nm,))[nm], None
        base = off // t_r
        if axis == 0:
            per = K // t_r
            place = lambda rb, cb: (rb // per, base + rb % per, cb)
        else:
            per = N // t_c
            place = lambda rb, cb: (cb // per, base + rb, cb % per)
        return gathered_b, (shape, (lambda j, k: place(k, j)) if mode == "nn" else (lambda j, k: place(j, k)))

    y_mla = _matmul(o_mla, w_mb, "nn", [F32], "mla_branch")

    def gate_merge(acc, ga, gb, ya):
        return acc, _sigmoid(ga) * ya + _sigmoid(gb) * acc

    y_fox, merged = _matmul(o_fox, w_fb, "nn", [F32, BF16], "fox_branch_gates", tn=512,
                            extras=((gpre, 0), (gpre, 1), y_mla), epilogue=gate_merge)
    h1 = _matmul(merged, w_o, "nn", [F32], "out_proj", extras=(xs,), epilogue=lambda acc, r: (acc + r,))
    hn = _norm_fwd(h1, mlp_norm, "mlp_norm_fwd")

    def relu2(acc):
        a = jnp.maximum(acc, 0.0)
        return a * a, a

    w_u, w_u_in = b_of("w_up", "nn", 1024, 2048)
    u, a_pos = _matmul(hn, w_u, "nn", [BF16, BF16], "mlp_up", epilogue=relu2, b_in=w_u_in)
    w_d, w_d_in = b_of("w_down", "nn", 1024, 2048)
    h2 = _matmul(u, w_d, "nn", [F32], "mlp_down", tn=1024, extras=(h1,), epilogue=lambda acc, r: (acc + r,),
                 b_in=w_d_in)
    dh2, dh2_b, g_final, loss_part = _final(h2, final_norm.reshape(1, D), target, "final_norm_loss")

    gp_b = lax.empty((N_CHIPS, RB, C), BF16)
    by_glue = {}

    def grad_b(nm, a, b, name):
        nonlocal gp_b
        (K, N), axis = next((shape, axis) for n, shape, axis in pack_b.group if n == nm)
        off = pack_b.offs[nm]
        tm = min(1024, K) if axis == 0 else min(1024, a.shape[1])
        tn = min(1024, N) if axis == 1 else min(1024, b.shape[1])
        if not (N == C and tm % LANE == 0 and tn % LANE == 0 and K % tm == 0 and N % tn == 0 and off % tm == 0):
            by_glue[nm] = _mm_tn(a, b, name)
            return
        base = off // tm
        if axis == 0:
            per = K // tm
            place = lambda i, j: (i // per, base + i % per, j)
        else:
            per = N // tn
            place = lambda i, j: (j // per, base + i, j % per)
        gp_b = _mm_tn(a, b, name, tm=tm, tn=tn, into=(gp_b, place))

    w_d, w_d_in = b_of("w_down", "nt", 1024, 2048)
    da = _matmul(dh2_b, w_d, "nt", [BF16], "mlp_down_dx", extras=(a_pos,),
                 epilogue=lambda acc, a: (acc * (2.0 * a.astype(F32)),), b_in=w_d_in)
    grad_b("w_down", u, dh2_b, "mlp_down_dw")
    w_u, w_u_in = b_of("w_up", "nt", 1024, 2048)
    dhn = _matmul(da, w_u, "nt", [F32], "mlp_up_dx", tn=1024, b_in=w_u_in)
    grad_b("w_up", hn, da, "mlp_up_dw")
    dh1, dh1_b, g_mlp_norm = _norm_bwd(h1, dhn, mlp_norm, dh2, "mlp_norm_bwd")

    def gate_bwd(acc, ga, gb, ya, yb):
        ga, gb = _sigmoid(ga), _sigmoid(gb)
        return acc * ga, acc * gb, acc * ya * (ga * (1.0 - ga)), acc * yb * (gb * (1.0 - gb))

    dy_mla, dy_fox, dg_mla, dg_fox = _matmul(dh1_b, w_o, "nt", [BF16] * 4, "out_proj_dx_gates", tn=512,
                                             extras=((gpre, 0), (gpre, 1), y_mla, y_fox), epilogue=gate_bwd)
    grad_b("w_out", merged, dh1_b, "out_proj_dw")
    do_mla = _matmul(dy_mla, w_mb, "nt", [BF16], "mla_branch_dx")
    grad_b("w_mla_branch", o_mla, dy_mla, "mla_branch_dw")
    do_fox = _matmul(dy_fox, w_fb, "nt", [BF16], "fox_branch_dx")
    grad_b("w_fox_branch", o_fox, dy_fox, "fox_branch_dw")
    for nm, g in by_glue.items():
        gp_b = lax.dynamic_update_slice(gp_b, pack_b.slab_rows(nm, g), (0, pack_b.offs[nm], 0))
    if RB > pack_b.used:
        gp_b = lax.dynamic_update_slice(gp_b, jnp.zeros((N_CHIPS, RB - pack_b.used, C), BF16), (0, pack_b.used, 0))

    rs_b = _xchg_start(gp_b, lax.empty((3, RB, C), BF16), False, do_fox, "grad_scatter_start_b")

    dq_rot, dk_nope, dkr_heads, dv_mla = _att_bwd_t(mla, do_mla, lse_mla, o_mla, BF16, [BF16, F32],
                                                    "mla_att_bwd", dq_rope=(qc, qsa, qsb), order=rs_b[3])
    dfq, dfk, dfv, dcum = _att_bwd_t(fox, do_fox, lse_fox, ox_fox, BF16, [BF16], "fox_att_bwd")

    gp_b_sent, recv_b = _xchg_wait(rs_b, False, (dfq, dq_rot), "grad_scatter_wait_b")
    swap_b = _sib_start(_sum_slabs(gp_b_sent, recv_b, chip, "grad_sum_b"), "grad_swap_start_b")

    dcqn = _matmul(dq_rot, w_uq_p, "nt", [F32], "mla_q_up_dx", order=swap_b[4])
    g_w_uq_p = _mm_tn(cqn, dq_rot, "mla_q_up_dw")
    dkv2 = jnp.concatenate([dk_nope, dv_mla], axis=1)
    dckvn = _matmul(dkv2, w_ukv_p, "nt", [F32], "mla_kv_up_dx")
    g_w_ukv_p = _mm_tn(ckvn, dkv2, "mla_kv_up_dw")

    dcum_rows = jnp.pad(dcum[:, :, 0], ((0, 8 - HF), (0, 0)))
    dlogf_rows = _suffix_sum_rows(dcum_rows, "fox_forget_suffix_sum")
    dlogf = _pad_cols(jnp.transpose(dlogf_rows[:HF]), LANE)
    d_small, g_q_norm, g_kv_norm, g_bias = _prep_bwd(
        small, dcqn, dckvn, dkr_heads, dlogf, q_norm, kv_norm, bias_pad, kc, ksa, ksb, H, "prep_bwd")
    dproj = [d_small, dfq, dfk, dfv, dg_mla, dg_fox]
    gs, gfq, gfk, gfv, gg_mla, gg_fox = [
        _matmul(part, xn, "tn", [BF16], "proj_dw_" + tag, tm=1024, tn=1024, tk=2048)
        for part, tag in zip(dproj, ("small", "fq", "fk", "fv", "g_mla", "g_fox"))]

    g_w_in = jnp.concatenate([gs[:o_kr], gs[o_kr:o_kr + MLA_ROPE], gfq, gfk, gfv,
                              gs[o_kr + LANE:o_kr + LANE + HF], gg_mla, gg_fox], axis=0)
    g_w_uq = g_w_uq_p.reshape(QL, H, QPAD)[:, :, :dqk].reshape(QL, H * dqk)
    g_w_ukv = jnp.concatenate([g_w_ukv_p[:, :H * MLA_NOPE].reshape(KVL, H, MLA_NOPE),
                               g_w_ukv_p[:, H * MLA_NOPE:].reshape(KVL, H, MLA_V)], axis=2).reshape(KVL, -1)

    gp_a = pack_a.slabs({"w_in": g_w_in, "w_uq": g_w_uq, "w_ukv": g_w_ukv})
    rs_a = _xchg_start(gp_a, lax.empty((3, RA, C), BF16), False, gg_fox, "grad_scatter_start_a")
    dxn = _matmul_row_parts(dproj, w_pack, F32, "proj_dx", order=rs_a[3])
    grad_x, g_attn_norm = _norm_bwd(xs, dxn, attn_norm, dh1, "attn_norm_bwd", with_bf16=False)
    gp_a_sent, recv_a = _xchg_wait(rs_a, False, grad_x, "grad_scatter_wait_a")
    swap_a = _sib_start(_sum_slabs(gp_a_sent, recv_a, chip, "grad_sum_a"), "grad_swap_start_a")
    vec_w = max(D, LANE)
    vec_rows = [g_attn_norm, g_mlp_norm, g_final, g_q_norm, g_kv_norm, g_bias, loss_part]
    vec = jnp.concatenate([_pad_cols(v, vec_w) for v in vec_rows] + [jnp.zeros((1, vec_w), F32)], axis=0)
    vsum = _all_reduce_vec(vec, "all_reduce_vectors")
    part_b, sib_b = _sib_wait(swap_b, vsum, "grad_swap_wait_b")

    grads, deltas, new_m, new_v = {}, {}, {}, {}

    def update(pack, mine, theirs):
        for nm, shape, _ in pack.group:
            K, N = shape
            if N == pack.C and K % 8 == 0 and pack.offs[nm] % _tile(K, 256, 8) == 0:
                g, d, nm_, nv_ = _adamw(weights[nm], mine, theirs, moments[nm][0], moments[nm][1], "adamw_" + nm,
                                        g_row=pack.offs[nm])
            else:
                g, d, nm_, nv_ = _adamw(weights[nm], pack.part(mine, nm, shape), pack.part(theirs, nm, shape),
                                        moments[nm][0], moments[nm][1], "adamw_" + nm)
            grads[nm], deltas[nm], new_m[nm], new_v[nm] = g, d, nm_, nv_
        return g

    last_b = update(pack_b, part_b, sib_b)
    part_a, sib_a = _sib_wait(swap_a, last_b, "grad_swap_wait_a")
    update(pack_a, part_a, sib_a)

    vec_names = ["attn_norm", "mlp_norm", "final_norm", "q_norm", "kv_norm", "fox_f_bias"]

    def vec_pack(arrs):
        return jnp.concatenate([_pad_cols(a.reshape(1, -1), vec_w) for a in arrs]
                               + [jnp.zeros((2, vec_w), F32)], axis=0)[None]

    vg, vd, vm, vv = _adamw(vec_pack([weights[n] for n in vec_names]), vsum, jnp.zeros_like(vsum),
                            vec_pack([moments[n][0] for n in vec_names]), vec_pack([moments[n][1] for n in vec_names]),
                            "adamw_vectors")
    for r, nm in enumerate(vec_names):
        shp = weights[nm].shape
        n = weights[nm].size
        grads[nm] = vsum[r, :n].reshape(shp)
        deltas[nm], new_m[nm], new_v[nm] = (vd[0, r, :n].reshape(shp), vm[0, r, :n].reshape(shp),
                                            vv[0, r, :n].reshape(shp))
    loss = vsum[6, 0]

    for res in (grads, deltas, new_m, new_v):
        res["w_in"] = flip(res["w_in"])
    order = ["attn_norm", "w_in", "fox_f_bias", "q_norm", "w_uq", "kv_norm", "w_ukv", "w_mla_branch", "w_fox_branch",
             "w_out", "mlp_norm", "w_up", "w_down", "final_norm"]
    return (loss, grad_x[None], *[grads[n] for n in order], *[deltas[n] for n in order],
            *[new_m[n] for n in order], *[new_v[n] for n in order])
```

```python
import math

import jax
import jax.numpy as jnp
from jax import lax
from jax.experimental import pallas as pl
from jax.experimental.pallas import tpu as pltpu

CHUNK = 64
MLA_HEADS = 8
MLA_Q_LORA = 512
MLA_KV_LORA = 256
MLA_NOPE = 128
MLA_ROPE = 64
MLA_V = 128
ROPE_THETA = 10000.0
FOX_HEADS = 8
FOX_HEAD_DIM = 128
EPS = 1e-6

ADAM_LR = 0.001
ADAM_B1 = 0.9
ADAM_B2 = 0.999
ADAM_EPS = 1e-08
ADAM_WD = 0.01
ADAM_STEP = 10

LANE = 128
QPAD = 2 * LANE
N_CHIPS = 4
N_DEV = 8
VMEM_LIMIT = 48 * 1024 * 1024
ATT_T = 2048
QSUB = 256
ROW_T = 256
PACK_ROWS = 256
LOG2E = 1.4426950408889634

BF16 = jnp.bfloat16
F32 = jnp.float32
MESH = pl.DeviceIdType.MESH

_NT = (((1,), (1,)), ((), ()))
_TN = (((0,), (0,)), ((), ()))
_NN = (((1,), (0,)), ((), ()))


def _tile(dim, pref, align=LANE):
    if dim <= pref:
        return dim
    t = (pref // align) * align
    while t >= align:
        if dim % t == 0:
            return t
        t -= align
    return dim


def _params(sem=None):
    return pltpu.CompilerParams(dimension_semantics=sem, vmem_limit_bytes=VMEM_LIMIT)


_ANY_SPEC = pl.BlockSpec(memory_space=pl.ANY)


def _matmul(a, b, mode, out_dtypes, name, *, tm=1024, tn=1024, tk=2048, extras=(), row_extras=(), col_extras=(),
            epilogue=None, order=None, into=None, b_in=None):
    b_shape = b.shape if b_in is None else b_in[0]
    if mode == "nn":
        (M, K), (K2, N) = a.shape, b_shape
    elif mode == "nt":
        (M, K), (N, K2) = a.shape, b_shape
    else:
        (K, M), (K2, N) = a.shape, b_shape
    assert K == K2, (name, a.shape, b_shape)
    tm, tn, tk = _tile(M, tm), _tile(N, tn), _tile(K, tk)
    nk = K // tk
    extras = [e if isinstance(e, tuple) else (e, 0) for e in extras]
    n_out = len(out_dtypes)
    n_ex = len(extras) + len(row_extras) + len(col_extras)
    n_ord = 0 if order is None else 1
    assert all(r.shape == (M, tn) for r in row_extras), name
    dims = {"nn": _NN, "nt": _NT, "tn": _TN}[mode]

    def body(*refs):
        a_ref, b_ref = refs[0], refs[1]
        ex_refs = refs[2:2 + n_ex]
        o_refs = refs[2 + n_ex + n_ord:2 + n_ex + n_ord + n_out]
        acc_ref = refs[2 + n_ex + n_ord + n_out]
        k = pl.program_id(2)
        part = lax.dot_general(a_ref[...], b_ref[...], dims, preferred_element_type=F32)

        @pl.when(k == 0)
        def _():
            acc_ref[...] = part

        @pl.when(k > 0)
        def _():
            acc_ref[...] += part

        @pl.when(k == nk - 1)
        def _():
            acc = acc_ref[...]
            if epilogue is None:
                outs = (acc,)
            else:
                outs = epilogue(acc, *[r[...] for r in ex_refs])
            for o_ref, o in zip(o_refs, outs):
                o_ref[...] = o.astype(o_ref.dtype)

    if mode == "nn":
        a_spec = pl.BlockSpec((tm, tk), lambda i, j, k: (i, k))
        b_spec = pl.BlockSpec((tk, tn), lambda i, j, k: (k, j))
    elif mode == "nt":
        a_spec = pl.BlockSpec((tm, tk), lambda i, j, k: (i, k))
        b_spec = pl.BlockSpec((tn, tk), lambda i, j, k: (j, k))
    else:
        a_spec = pl.BlockSpec((tk, tm), lambda i, j, k: (k, i))
        b_spec = pl.BlockSpec((tk, tn), lambda i, j, k: (k, j))
    if b_in is not None:
        b_block = (None, tn, tk) if mode == "nt" else (None, tk, tn)
        b_spec = pl.BlockSpec(b_block, lambda i, j, k: b_in[1](j, k))
    mn_spec = pl.BlockSpec((tm, tn), lambda i, j, k: (i, j))
    row_spec = pl.BlockSpec((tm, tn), lambda i, j, k: (i, 0))
    col_spec = pl.BlockSpec((1, tn), lambda i, j, k: (0, j))
    out_specs = [mn_spec] * n_out
    out_shape = [jax.ShapeDtypeStruct((M, N), dt) for dt in out_dtypes]
    aliases = {}
    if into is not None:
        buf, place = into
        assert n_out == 1 and n_ord == 1 and order is buf, name
        out_specs = [pl.BlockSpec((None, tm, tn), lambda i, j, k: place(i, j))]
        out_shape = [jax.ShapeDtypeStruct(buf.shape, buf.dtype)]
        aliases = {2 + n_ex: 0}
    outs = pl.pallas_call(
        body,
        name=name,
        grid=(M // tm, N // tn, nk),
        in_specs=([a_spec, b_spec]
                  + [pl.BlockSpec((tm, tn), lambda i, j, k, g=g: (i, j + g * (N // tn))) for _, g in extras]
                  + [row_spec] * len(row_extras) + [col_spec] * len(col_extras) + [_ANY_SPEC] * n_ord),
        out_specs=out_specs,
        out_shape=out_shape,
        scratch_shapes=[pltpu.VMEM((tm, tn), F32)],
        input_output_aliases=aliases,
        compiler_params=_params(("parallel", "parallel", "arbitrary")),
    )(a, b, *[e for e, _ in extras], *row_extras, *col_extras, *([] if order is None else [order]))
    return outs[0] if n_out == 1 else outs


def _matmul_row_parts(parts, b, out_dtype, name, *, tm=512, tk=2048, order=None):
    M, (K, N) = parts[0].shape[0], b.shape
    widths = [p.shape[1] for p in parts]
    assert sum(widths) == K, name
    tm, tk = _tile(M, tm), _tile(K, tk)
    nk = K // tk
    steps, at = [], 0
    for p, w in enumerate(widths):
        off = 0
        while off < w:
            k, room = divmod(at, tk)
            take = min(w - off, tk - room)
            if room == 0:
                steps.append([])
            steps[k].append((p, off, take, room))
            off += take
            at += take
    assert len(steps) == nk and all(t % LANE == 0 and o % LANE == 0 for s in steps for _, o, t, _ in s), name
    n_parts = len(parts)
    n_ord = 0 if order is None else 1

    def body(*refs):
        a_refs = refs[0:n_parts]
        b_ref = refs[n_parts]
        o_ref, acc_ref = refs[n_parts + 1 + n_ord], refs[n_parts + 2 + n_ord]
        k = pl.program_id(1)
        for kk, pieces in enumerate(steps):
            @pl.when(k == kk)
            def _(kk=kk, pieces=pieces):
                part = None
                for p, off, take, room in pieces:
                    d = jnp.dot(a_refs[p][:, off:off + take], b_ref[room:room + take, :], preferred_element_type=F32)
                    part = d if part is None else part + d
                if kk == 0:
                    acc_ref[...] = part
                else:
                    acc_ref[...] += part

        @pl.when(k == nk - 1)
        def _():
            o_ref[...] = acc_ref[...].astype(o_ref.dtype)

    return pl.pallas_call(
        body, name=name, grid=(M // tm, nk),
        in_specs=[pl.BlockSpec((tm, w), lambda i, k: (i, 0)) for w in widths]
        + [pl.BlockSpec((tk, N), lambda i, k: (k, 0))] + [_ANY_SPEC] * n_ord,
        out_specs=pl.BlockSpec((tm, N), lambda i, k: (i, 0)),
        out_shape=jax.ShapeDtypeStruct((M, N), out_dtype),
        scratch_shapes=[pltpu.VMEM((tm, N), F32)],
        compiler_params=_params(("parallel", "arbitrary")),
    )(*parts, b, *([] if order is None else [order]))


def _mm_tn(a, b, name, tm=1024, tn=1024, into=None):
    return _matmul(a, b, "tn", [F32], name, tm=tm, tn=tn, tk=2048, into=into,
                   order=None if into is None else into[0])


def _row_spec(ts, width, col=0):
    return pl.BlockSpec((ts, width), lambda i: (i, col))


def _full_spec(shape):
    return pl.BlockSpec(shape, lambda i: tuple(0 for _ in shape))


def _rms(x):
    return lax.rsqrt(jnp.mean(x * x, axis=-1, keepdims=True) + EPS)


def _rms_bwd(x, dy, g):
    r = _rms(x)
    xh = x * r
    gy = dy * g
    dx = r * (gy - xh * jnp.mean(xh * gy, axis=-1, keepdims=True))
    return dx, dy * xh


def _norm_fwd(x, g, name, order=None):
    S, D = x.shape
    ts = _tile(S, ROW_T, 8)

    def body(x_ref, g_ref, *rest):
        o_ref = rest[-1]
        xv = x_ref[...]
        o_ref[...] = ((xv * _rms(xv)) * g_ref[...]).astype(BF16)

    extra = [] if order is None else [order]
    return pl.pallas_call(
        body, name=name, grid=(S // ts,),
        in_specs=[_row_spec(ts, D), _full_spec((1, D))] + [_ANY_SPEC] * len(extra),
        out_specs=_row_spec(ts, D),
        out_shape=jax.ShapeDtypeStruct((S, D), BF16),
        compiler_params=_params(("parallel",)),
    )(x, g, *extra)


def _norm_bwd(x, dy, g, dres, name, with_bf16=True):
    S, D = x.shape
    ts = _tile(S, ROW_T, 8)

    def body(x_ref, dy_ref, g_ref, dres_ref, dx_ref, *rest):
        dg_ref = rest[-1]
        dx, dg_rows = _rms_bwd(x_ref[...], dy_ref[...], g_ref[...])
        dx = dres_ref[...] + dx
        dx_ref[...] = dx
        if with_bf16:
            rest[0][...] = dx.astype(BF16)

        @pl.when(pl.program_id(0) == 0)
        def _():
            dg_ref[...] = jnp.zeros_like(dg_ref)

        dg_ref[...] += jnp.sum(dg_rows, axis=0, keepdims=True)

    return pl.pallas_call(
        body, name=name, grid=(S // ts,),
        in_specs=[_row_spec(ts, D), _row_spec(ts, D), _full_spec((1, D)), _row_spec(ts, D)],
        out_specs=[_row_spec(ts, D)] * (2 if with_bf16 else 1) + [_full_spec((1, D))],
        out_shape=([jax.ShapeDtypeStruct((S, D), F32)] + [jax.ShapeDtypeStruct((S, D), BF16)] * with_bf16
                   + [jax.ShapeDtypeStruct((1, D), F32)]),
        compiler_params=_params(("arbitrary",)),
    )(x, dy, g, dres)


def _rope(x, c, sa, sb, sign):
    w = x.shape[-1]
    half = MLA_ROPE // 2
    fwd = pltpu.roll(x, w - half, 1)
    back = pltpu.roll(x, half, 1)
    if sign < 0:
        return x * c - fwd * sa - back * sb
    return x * c + fwd * sa + back * sb


def _split3(x):
    hi = x.astype(BF16)
    r1 = x - hi.astype(F32)
    mid = r1.astype(BF16)
    lo = (r1 - mid.astype(F32)).astype(BF16)
    return hi, mid, lo


def _prep_fwd(small, q_norm, kv_norm, bias_pad, kc, ksa, ksb, n_heads, name):
    S, W = small.shape
    QL, KVL = q_norm.shape[1], kv_norm.shape[1]
    assert W == QL + KVL + 2 * LANE
    ts = _tile(S, ROW_T, 8)
    tri = (lax.broadcasted_iota(jnp.int32, (ts, ts), 0) >= lax.broadcasted_iota(jnp.int32, (ts, ts), 1)).astype(BF16)

    def body(s_ref, qn_ref, kvn_ref, b_ref, kc_ref, ksa_ref, ksb_ref, tri_ref,
             cqn_ref, ckvn_ref, kr_ref, cum_ref, carry_ref):
        cq = s_ref[:, 0:QL]
        cqn_ref[...] = ((cq * _rms(cq)) * qn_ref[...]).astype(BF16)
        ckv = s_ref[:, QL:QL + KVL]
        ckvn_ref[...] = ((ckv * _rms(ckv)) * kvn_ref[...]).astype(BF16)
        kr = s_ref[:, QL + KVL:QL + KVL + LANE]
        kr_ref[...] = _rope(kr, kc_ref[...], ksa_ref[...], ksb_ref[...], 1).astype(BF16)
        z = s_ref[:, QL + KVL + LANE:W] + b_ref[...]
        logf = jnp.minimum(z, 0.0) - jnp.log1p(jnp.exp(-jnp.abs(z)))
        lane = lax.broadcasted_iota(jnp.int32, logf.shape, 1)
        logf = jnp.where(lane < n_heads, logf, 0.0)

        @pl.when(pl.program_id(0) == 0)
        def _():
            carry_ref[...] = jnp.zeros_like(carry_ref)

        t = tri_ref[...]
        cum = carry_ref[...]
        for part in _split3(logf):
            cum = cum + jnp.dot(t, part, preferred_element_type=F32)
        cum_ref[...] = cum
        carry_ref[...] = cum[ts - 1:ts, :]

    return pl.pallas_call(
        body, name=name, grid=(S // ts,),
        in_specs=[_row_spec(ts, W), _full_spec((1, QL)), _full_spec((1, KVL)), _full_spec((1, LANE)),
                  _row_spec(ts, LANE), _row_spec(ts, LANE), _row_spec(ts, LANE), _full_spec((ts, ts))],
        out_specs=[_row_spec(ts, QL), _row_spec(ts, KVL), _row_spec(ts, LANE), _row_spec(ts, LANE)],
        out_shape=[jax.ShapeDtypeStruct((S, QL), BF16), jax.ShapeDtypeStruct((S, KVL), BF16),
                   jax.ShapeDtypeStruct((S, LANE), BF16), jax.ShapeDtypeStruct((S, LANE), F32)],
        scratch_shapes=[pltpu.VMEM((1, LANE), F32)],
        compiler_params=_params(("arbitrary",)),
    )(small, q_norm, kv_norm, bias_pad, kc, ksa, ksb, tri)


def _prep_bwd(small, dcqn, dckvn, dkr_heads, dlogf, q_norm, kv_norm, bias_pad, kc, ksa, ksb, n_heads, name):
    S, W = small.shape
    QL, KVL = q_norm.shape[1], kv_norm.shape[1]
    ts = _tile(S, ROW_T, 8)

    def body(s_ref, dcq_ref, dckv_ref, dkr_ref, dlf_ref, qn_ref, kvn_ref, b_ref, kc_ref, ksa_ref, ksb_ref,
             ds_ref, gq_ref, gkv_ref, gb_ref):
        dcq, gq_rows = _rms_bwd(s_ref[:, 0:QL], dcq_ref[...], qn_ref[...])
        ds_ref[:, 0:QL] = dcq.astype(BF16)
        dckv, gkv_rows = _rms_bwd(s_ref[:, QL:QL + KVL], dckv_ref[...], kvn_ref[...])
        ds_ref[:, QL:QL + KVL] = dckv.astype(BF16)
        dkr = dkr_ref[:, 0:LANE]
        for h in range(1, n_heads):
            dkr = dkr + dkr_ref[:, h * LANE:(h + 1) * LANE]
        ds_ref[:, QL + KVL:QL + KVL + LANE] = _rope(dkr, kc_ref[...], ksa_ref[...], ksb_ref[...], -1).astype(BF16)
        z = s_ref[:, QL + KVL + LANE:W] + b_ref[...]
        dff = dlf_ref[...] * (1.0 / (1.0 + jnp.exp(z)))
        ds_ref[:, QL + KVL + LANE:W] = dff.astype(BF16)

        @pl.when(pl.program_id(0) == 0)
        def _():
            gq_ref[...] = jnp.zeros_like(gq_ref)
            gkv_ref[...] = jnp.zeros_like(gkv_ref)
            gb_ref[...] = jnp.zeros_like(gb_ref)

        gq_ref[...] += jnp.sum(gq_rows, axis=0, keepdims=True)
        gkv_ref[...] += jnp.sum(gkv_rows, axis=0, keepdims=True)
        gb_ref[...] += jnp.sum(dff, axis=0, keepdims=True)

    return pl.pallas_call(
        body, name=name, grid=(S // ts,),
        in_specs=[_row_spec(ts, W), _row_spec(ts, QL), _row_spec(ts, KVL), _row_spec(ts, n_heads * LANE),
                  _row_spec(ts, LANE), _full_spec((1, QL)), _full_spec((1, KVL)), _full_spec((1, LANE)),
                  _row_spec(ts, LANE), _row_spec(ts, LANE), _row_spec(ts, LANE)],
        out_specs=[_row_spec(ts, W), _full_spec((1, QL)), _full_spec((1, KVL)), _full_spec((1, LANE))],
        out_shape=[jax.ShapeDtypeStruct((S, W), BF16), jax.ShapeDtypeStruct((1, QL), F32),
                   jax.ShapeDtypeStruct((1, KVL), F32), jax.ShapeDtypeStruct((1, LANE), F32)],
        compiler_params=_params(("arbitrary",)),
    )(small, dcqn, dckvn, dkr_heads, dlogf, q_norm, kv_norm, bias_pad, kc, ksa, ksb)


def _sigmoid(z):
    return 1.0 / (1.0 + jnp.exp(-z))


def _final(h, g, target, name):
    S, D = h.shape
    ts = _tile(S, ROW_T, 8)

    def body(h_ref, g_ref, t_ref, dh_ref, dhb_ref, dg_ref, loss_ref):
        hv = h_ref[...]
        gv = g_ref[...]
        err = (hv * _rms(hv)) * gv - t_ref[...]
        dh, dg_rows = _rms_bwd(hv, err / D, gv)
        dh_ref[...] = dh
        dhb_ref[...] = dh.astype(BF16)

        @pl.when(pl.program_id(0) == 0)
        def _():
            dg_ref[...] = jnp.zeros_like(dg_ref)
            loss_ref[...] = jnp.zeros_like(loss_ref)

        dg_ref[...] += jnp.sum(dg_rows, axis=0, keepdims=True)
        row_loss = jnp.mean(err * err, axis=-1, keepdims=True)
        loss_ref[...] += 0.5 * jnp.sum(row_loss, axis=0, keepdims=True)

    return pl.pallas_call(
        body, name=name, grid=(S // ts,),
        in_specs=[_row_spec(ts, D), _full_spec((1, D)), _row_spec(ts, D)],
        out_specs=[_row_spec(ts, D), _row_spec(ts, D), _full_spec((1, D)), _full_spec((1, LANE))],
        out_shape=[jax.ShapeDtypeStruct((S, D), F32), jax.ShapeDtypeStruct((S, D), BF16),
                   jax.ShapeDtypeStruct((1, D), F32), jax.ShapeDtypeStruct((1, LANE), F32)],
        compiler_params=_params(("arbitrary",)),
    )(h, g, target)


def _suffix_sum_rows(x, name):
    R, S = x.shape
    tb = _tile(S, 512)
    nb = S // tb
    tri = (lax.broadcasted_iota(jnp.int32, (tb, tb), 0) >= lax.broadcasted_iota(jnp.int32, (tb, tb), 1)).astype(BF16)

    def body(x_ref, tri_ref, o_ref, carry_ref):
        @pl.when(pl.program_id(0) == 0)
        def _():
            carry_ref[...] = jnp.zeros_like(carry_ref)

        xv = x_ref[...]
        t = tri_ref[...]
        acc = jnp.broadcast_to(carry_ref[:, 0:1], xv.shape)
        for part in _split3(xv):
            acc = acc + jnp.dot(part, t, preferred_element_type=F32)
        o_ref[...] = acc
        carry_ref[...] = jnp.broadcast_to(acc[:, 0:1], carry_ref.shape)

    rev = pl.BlockSpec((R, tb), lambda i: (0, nb - 1 - i))
    return pl.pallas_call(
        body, name=name, grid=(nb,),
        in_specs=[rev, _full_spec((tb, tb))], out_specs=rev,
        out_shape=jax.ShapeDtypeStruct((R, S), F32),
        scratch_shapes=[pltpu.VMEM((R, LANE), F32)],
        compiler_params=_params(("arbitrary",)),
    )(x, tri)


def _pairs(nb, by_key):
    if by_key:
        pr = [(i, j) for j in range(nb) for i in range(j, nb)]
    else:
        pr = [(i, j) for i in range(nb) for j in range(i + 1)]
    return (jnp.asarray([p[0] for p in pr], jnp.int32), jnp.asarray([p[1] for p in pr], jnp.int32), len(pr))


class _AttT:
    def __init__(self, S, n_heads, q, ks, v, scale, chunk_causal, cum_rep=None, qsub=None):
        self.S, self.H, self.q, self.ks, self.v = S, n_heads, q, ks, v
        self.scale, self.chunk_causal, self.cum_rep = scale, chunk_causal, cum_rep
        self.T = _tile(S, ATT_T)
        self.qs = min(qsub or QSUB, self.T)
        self.nb = S // self.T
        self.dq, self.dv = q[1], v[1]
        self.has_bias = cum_rep is not None

    def q_spec(self, op):
        _, w, off, per_head = op
        return pl.BlockSpec((self.T, w), lambda h, p, it, jt: (it[p], off + (h if per_head else 0)))

    def k_spec(self, op):
        _, w, off, per_head = op
        return pl.BlockSpec((self.T, w), lambda h, p, it, jt: (jt[p], off + (h if per_head else 0)))

    def row_q(self):
        return pl.BlockSpec((None, 1, self.T), lambda h, p, it, jt: (h, 0, it[p]))

    def cum_k(self):
        return pl.BlockSpec((None, self.T, self.qs), lambda h, p, it, jt: (h, jt[p], 0))

    def sub_blocks(self, masked):
        return [(q0, min(self.T, q0 + self.qs) if masked else self.T) for q0 in range(0, self.T, self.qs)]

    def scores(self, k, q_sub, cum, q0, masked):
        s = lax.dot_general(k, q_sub, _NT, preferred_element_type=F32)
        if self.has_bias:
            s = s - cum
        mask = None
        if masked:
            r = lax.broadcasted_iota(jnp.int32, s.shape, 0)
            c = lax.broadcasted_iota(jnp.int32, s.shape, 1) + q0
            mask = (r // CHUNK <= c // CHUNK) if self.chunk_causal else (r <= c)
        return s, mask


def _join(k_refs):
    return k_refs[0][...] if len(k_refs) == 1 else jnp.concatenate([r[...] for r in k_refs], axis=-1)


def _att_fwd_t(att, name, exact=False):
    S, H, T, qs = att.S, att.H, att.T, att.qs
    it, jt, npairs = _pairs(att.nb, by_key=False)
    nk = len(att.ks)

    def body(it_ref, jt_ref, *refs):
        q_ref = refs[0]
        k_refs = refs[1:1 + nk]
        v_ref = refs[1 + nk]
        n = 2 + nk
        cum_ref = None
        if att.has_bias:
            cum_ref = refs[n]
            n += 1
        o_ref = refs[n]
        n += 1
        ox_ref = None
        if exact:
            ox_ref = refs[n]
            n += 1
        lse_ref, m_ref, l_ref, acc_ref = refs[n:n + 4]
        lo_ref = refs[n + 4] if exact else None
        p = pl.program_id(1)
        i, j = it_ref[p], jt_ref[p]

        @pl.when(j == 0)
        def _():
            m_ref[...] = jnp.full_like(m_ref, -jnp.inf)
            l_ref[...] = jnp.zeros_like(l_ref)
            acc_ref[...] = jnp.zeros_like(acc_ref)
            if exact:
                lo_ref[...] = jnp.zeros_like(lo_ref)

        def step(masked):
            k = _join(k_refs)
            v = v_ref[...]
            subs = att.sub_blocks(masked)

            def logits(idx):
                q0, nkeys = subs[idx]
                cum = cum_ref[0:nkeys, :] if att.has_bias else None
                return att.scores(k[0:nkeys], q_ref[q0:q0 + qs, :], cum, q0, masked)

            ahead = logits(0)
            for idx, (q0, nkeys) in enumerate(subs):
                qsl = slice(q0, q0 + qs)
                s, mask = ahead
                if idx + 1 < len(subs):
                    ahead = logits(idx + 1)
                if masked:
                    s = jnp.where(mask, s, -jnp.inf)
                m_prev = m_ref[:, qsl]
                m_new = jnp.maximum(m_prev, jnp.max(s, axis=0, keepdims=True))
                alpha = jnp.exp2(m_prev - m_new)
                pr = jnp.exp2(s - m_new)
                l_ref[:, qsl] = alpha * l_ref[:, qsl] + jnp.sum(pr, axis=0, keepdims=True)
                p_hi = pr.astype(BF16)
                acc_ref[:, qsl] = alpha * acc_ref[:, qsl] + lax.dot_general(
                    v[0:nkeys], p_hi, _TN, preferred_element_type=F32)
                if exact:
                    p_lo = (pr - p_hi.astype(F32)).astype(BF16)
                    lo_ref[:, qsl] = alpha * lo_ref[:, qsl] + lax.dot_general(
                        v[0:nkeys], p_lo, _TN, preferred_element_type=F32)
                m_ref[:, qsl] = m_new

        @pl.when(j < i)
        def _():
            step(False)

        @pl.when(j == i)
        def _():
            step(True)
            l = l_ref[...]
            inv = 1.0 / l
            o_ref[...] = jnp.transpose(acc_ref[...] * inv).astype(o_ref.dtype)
            if exact:
                ox_ref[...] = jnp.transpose((acc_ref[...] + lo_ref[...]) * inv)
            lse_ref[...] = m_ref[...] + jnp.log2(l)

    in_specs = [att.q_spec(att.q)] + [att.k_spec(k) for k in att.ks] + [att.k_spec(att.v)]
    args = [att.q[0]] + [k[0] for k in att.ks] + [att.v[0]]
    if att.has_bias:
        in_specs.append(att.cum_k())
        args.append(att.cum_rep)
    o_spec = pl.BlockSpec((T, att.dv), lambda h, p, it, jt: (it[p], h))
    out_specs = [o_spec]
    out_shape = [jax.ShapeDtypeStruct((S, H * att.dv), BF16)]
    scratch = [pltpu.VMEM((1, T), F32), pltpu.VMEM((1, T), F32), pltpu.VMEM((att.dv, T), F32)]
    if exact:
        out_specs.append(o_spec)
        out_shape.append(jax.ShapeDtypeStruct((S, H * att.dv), F32))
        scratch.append(pltpu.VMEM((att.dv, T), F32))
    out_specs.append(att.row_q())
    out_shape.append(jax.ShapeDtypeStruct((H, 1, S), F32))
    return pl.pallas_call(
        body, name=name,
        grid_spec=pltpu.PrefetchScalarGridSpec(
            num_scalar_prefetch=2, grid=(H, npairs), in_specs=in_specs, out_specs=out_specs,
            scratch_shapes=scratch),
        out_shape=out_shape,
        compiler_params=_params(("parallel", "arbitrary")),
    )(it, jt, *args)


def _att_bwd_t(att, do, lse, o, dq_dtype, dk_dtypes, name, dq_rope=None, order=None):
    S, H, T, qs = att.S, att.H, att.T, att.qs
    it, jt, npairs = _pairs(att.nb, by_key=True)
    nk = len(att.ks)
    last = att.nb - 1
    widths = [k[1] for k in att.ks]

    def body(it_ref, jt_ref, *refs):
        q_ref = refs[0]
        k_refs = refs[1:1 + nk]
        v_ref, do_ref, lse_ref, o_ref = refs[1 + nk:5 + nk]
        n = 5 + nk
        cum_ref = None
        if att.has_bias:
            cum_ref = refs[n]
            n += 1
        rope_refs = None
        if dq_rope is not None:
            rope_refs = refs[n:n + 3]
            n += 3
        if order is not None:
            n += 1
        dl_acc = refs[-1]
        dq_ref = refs[n]
        dk_refs = refs[n + 1:n + 1 + nk]
        dv_ref = refs[n + 1 + nk]
        n += nk + 2
        dc_ref = None
        if att.has_bias:
            dc_ref = refs[n]
            n += 1
        dq_acc, dk_acc, dv_acc = refs[n:n + 3]
        dc_acc = refs[n + 3] if att.has_bias else None
        p = pl.program_id(1)
        i, j = it_ref[p], jt_ref[p]

        @pl.when(p == 0)
        def _():
            dq_acc[...] = jnp.zeros_like(dq_acc)

        @pl.when(i == j)
        def _():
            dk_acc[...] = jnp.zeros_like(dk_acc)
            dv_acc[...] = jnp.zeros_like(dv_acc)
            if att.has_bias:
                dc_acc[...] = jnp.zeros_like(dc_acc)

        @pl.when(j == 0)
        def _():
            prod = do_ref[...].astype(F32) * o_ref[...].astype(F32)
            ones = jnp.ones((8, att.dv), BF16)
            rows = jnp.zeros((8, T), F32)
            for part in _split3(prod):
                rows = rows + lax.dot_general(ones, part, _NT, preferred_element_type=F32)
            dl_acc[i] = rows[0:1, :]

        def step(masked):
            k = _join(k_refs)
            v = v_ref[...]
            dl = dl_acc[i]
            subs = att.sub_blocks(masked)

            def logits(idx):
                q0, nkeys = subs[idx]
                cum = cum_ref[0:nkeys, :] if att.has_bias else None
                return att.scores(k[0:nkeys], q_ref[q0:q0 + qs, :], cum, q0, masked)

            ahead = logits(0)
            for idx, (q0, nkeys) in enumerate(subs):
                qsl = slice(q0, q0 + qs)
                ksl = slice(0, nkeys)
                q_sub = q_ref[qsl, :]
                do_sub = do_ref[qsl, :]
                s, mask = ahead
                if idx + 1 < len(subs):
                    ahead = logits(idx + 1)
                pr = jnp.exp2(s - lse_ref[:, qsl])
                if masked:
                    pr = jnp.where(mask, pr, 0.0)
                dp = lax.dot_general(v[ksl], do_sub, _NT, preferred_element_type=F32)
                ds = pr * (dp - dl[:, qsl])
                ds_b = ds.astype(BF16)
                dv_acc[ksl, :] += jnp.dot(pr.astype(BF16), do_sub, preferred_element_type=F32)
                dk_acc[ksl, :] += jnp.dot(ds_b, q_sub, preferred_element_type=F32)
                dq_acc[i, :, qsl] += lax.dot_general(k[ksl], ds_b, _TN, preferred_element_type=F32)
                if att.has_bias:
                    part = ds[:, 0:LANE] if qs >= LANE else ds
                    for c0 in range(LANE, qs, LANE):
                        part = part + ds[:, c0:c0 + LANE]
                    dc_acc[ksl, :] += part

        @pl.when(i > j)
        def _():
            step(False)

        @pl.when(i == j)
        def _():
            step(True)
            dq = jnp.transpose(dq_acc[i] * att.scale)
            if dq_rope is not None:
                dq = _rope(dq, rope_refs[0][...], rope_refs[1][...], rope_refs[2][...], -1)
            dq_ref[...] = dq.astype(dq_ref.dtype)

        @pl.when(i == last)
        def _():
            dk = dk_acc[...] * (1.0 / LOG2E)
            off = 0
            for r, w in zip(dk_refs, widths):
                r[...] = dk[:, off:off + w].astype(r.dtype)
                off += w
            dv_ref[...] = dv_acc[...].astype(dv_ref.dtype)
            if att.has_bias:
                dc_ref[...] = -jnp.sum(dc_acc[...], axis=-1, keepdims=True)

    do_op = (do, att.dv, 0, True)
    o_spec = pl.BlockSpec((T, att.dv), lambda h, p, it, jt: (jnp.where(jt[p] == 0, it[p], last), h))
    in_specs = ([att.q_spec(att.q)] + [att.k_spec(k) for k in att.ks]
                + [att.k_spec(att.v), att.q_spec(do_op), att.row_q(), o_spec])
    args = [att.q[0]] + [k[0] for k in att.ks] + [att.v[0], do, lse, o]
    if att.has_bias:
        in_specs.append(att.cum_k())
        args.append(att.cum_rep)
    if dq_rope is not None:
        in_specs += [pl.BlockSpec((T, att.dq), lambda h, p, it, jt: (jt[p], 0))] * 3
        args += list(dq_rope)
    if order is not None:
        in_specs.append(_ANY_SPEC)
        args.append(order)
    out_specs = [pl.BlockSpec((T, att.dq), lambda h, p, it, jt: (jt[p], h))]
    out_shape = [jax.ShapeDtypeStruct((S, H * att.dq), dq_dtype)]
    out_specs += [pl.BlockSpec((T, w), lambda h, p, it, jt: (jt[p], h)) for w in widths]
    out_shape += [jax.ShapeDtypeStruct((S, H * w), dt) for w, dt in zip(widths, dk_dtypes)]
    out_specs.append(pl.BlockSpec((T, att.dv), lambda h, p, it, jt: (jt[p], h)))
    out_shape.append(jax.ShapeDtypeStruct((S, H * att.dv), BF16))
    scratch = [pltpu.VMEM((att.nb, att.dq, T), F32), pltpu.VMEM((T, att.dq), F32), pltpu.VMEM((T, att.dv), F32)]
    if att.has_bias:
        out_specs.append(pl.BlockSpec((None, T, 1), lambda h, p, it, jt: (h, jt[p], 0)))
        out_shape.append(jax.ShapeDtypeStruct((H, S, 1), F32))
        scratch.append(pltpu.VMEM((T, min(qs, LANE)), F32))
    scratch.append(pltpu.VMEM((att.nb, 1, T), F32))
    return pl.pallas_call(
        body, name=name,
        grid_spec=pltpu.PrefetchScalarGridSpec(
            num_scalar_prefetch=2, grid=(H, npairs), in_specs=in_specs, out_specs=out_specs,
            scratch_shapes=scratch),
        out_shape=out_shape,
        compiler_params=_params(("parallel", "arbitrary")),
    )(it, jt, *args)


def _adamw(w, g1, g2, m, v, name, g_row=None):
    _, K, N = w.shape
    by_rows = K % 8 == 0
    tr = _tile(K, 256, 8) if by_rows else K
    if g_row is None:
        assert g1.shape == (K, N) and g2.shape == (K, N), name
        g_row = 0
    assert by_rows and g_row % tr == 0 or g_row == 0, name
    g_blk = g_row // tr
    tc = N if by_rows else _tile(N, LANE)
    c1 = 1.0 - ADAM_B1 ** ADAM_STEP
    c2 = 1.0 - ADAM_B2 ** ADAM_STEP

    def body(w_ref, g1_ref, g2_ref, m_ref, v_ref, g_ref, d_ref, nm_ref, nv_ref):
        gv = g1_ref[...] + g2_ref[...]
        nm = ADAM_B1 * m_ref[...] + (1.0 - ADAM_B1) * gv
        nv = ADAM_B2 * v_ref[...] + (1.0 - ADAM_B2) * (gv * gv)
        g_ref[...] = gv
        d_ref[...] = -ADAM_LR * ((nm / c1) / (jnp.sqrt(nv / c2) + ADAM_EPS) + ADAM_WD * w_ref[...])
        nm_ref[...] = nm
        nv_ref[...] = nv

    if by_rows:
        blk = pl.BlockSpec((None, tr, N), lambda i: (0, i, 0))
        gblk = pl.BlockSpec((tr, N), lambda i: (g_blk + i, 0))
    else:
        blk = pl.BlockSpec((None, K, tc), lambda i: (0, 0, i))
        gblk = pl.BlockSpec((K, tc), lambda i: (0, i))
    return pl.pallas_call(
        body, name=name, grid=(K // tr if by_rows else N // tc,),
        in_specs=[blk, gblk, gblk, blk, blk], out_specs=[blk] * 4,
        out_shape=[jax.ShapeDtypeStruct((1, K, N), F32)] * 4,
        compiler_params=_params(("parallel",)),
    )(w, g1, g2, m, v)


_HBM_SPEC = pl.BlockSpec(memory_space=pltpu.HBM)
_SEM_SPEC = pl.BlockSpec(memory_space=pltpu.SEMAPHORE)
_VMEM_SPEC = pl.BlockSpec(memory_space=pltpu.VMEM)
_EFFECT = pltpu.SideEffectType.DATAFLOW_SIDE_EFFECTING


def _place():
    return lax.axis_index("x"), lax.axis_index("y"), lax.axis_index("c")


def _other_chips(x, y):
    return [(1 - x, y), (x, 1 - y), (1 - x, 1 - y)]


def _chip_copies(src_ref, land_ref, sems, gather):
    x, y, c = _place()
    me = 2 * x + y
    out, back = [], []
    if gather == "half":
        half = src_ref.shape[0] // 2
        mine = pl.ds(pl.multiple_of(c * half, 16), half)
    for n, (px, py) in enumerate(_other_chips(x, y)):
        if gather == "half":
            src, there, here = src_ref.at[mine], land_ref.at[me, mine], land_ref.at[2 * px + py, mine]
        elif gather:
            src, there, here = src_ref, land_ref.at[me], land_ref.at[2 * px + py]
        else:
            src, there, here = src_ref.at[2 * px + py], land_ref.at[n], land_ref.at[n]
        out.append(pltpu.make_async_remote_copy(
            src_ref=src, dst_ref=there, send_sem=sems[n], recv_sem=sems[3 + n],
            device_id=(px, py, c), device_id_type=MESH))
        back.append(pltpu.make_async_remote_copy(
            src_ref=src, dst_ref=here, send_sem=sems[n], recv_sem=sems[3 + n],
            device_id=(px, py, c), device_id_type=MESH))
    return out, back


def _xchg_start(src, land, gather, order, name):
    def body(src_ref, land_ref, order_ref, *outs):
        sems = outs[0:6]
        token = outs[8]
        out, _ = _chip_copies(src_ref, land_ref, sems, gather)
        for cp in out:
            cp.start()
        token[...] = jnp.zeros_like(token)

    outs = pl.pallas_call(
        body, name=name,
        out_shape=(pltpu.SemaphoreType.DMA(()),) * 6 + (
            pltpu.HBM(src.shape, src.dtype), pltpu.HBM(land.shape, land.dtype),
            jax.ShapeDtypeStruct((8, LANE), F32)),
        in_specs=(_HBM_SPEC, _HBM_SPEC, _ANY_SPEC),
        out_specs=(_SEM_SPEC,) * 6 + (_HBM_SPEC, _HBM_SPEC, _VMEM_SPEC),
        input_output_aliases={0: 6, 1: 7},
        compiler_params=pltpu.CompilerParams(has_side_effects=_EFFECT),
    )(pltpu.with_memory_space_constraint(src, pltpu.HBM), pltpu.with_memory_space_constraint(land, pltpu.HBM), order)
    return outs[0:6], outs[6], outs[7], outs[8]


def _xchg_wait(started, gather, after, name):
    sems, src, land, _ = started
    after = after if isinstance(after, tuple) else (after,)

    def body(src_ref, land_ref, *rest):
        _, back = _chip_copies(src_ref, land_ref, rest[0:6], gather)
        for cp in back:
            cp.wait_send()
            cp.wait_recv()

    return pl.pallas_call(
        body, name=name,
        out_shape=(pltpu.HBM(src.shape, src.dtype), pltpu.HBM(land.shape, land.dtype)),
        in_specs=(_HBM_SPEC, _HBM_SPEC) + (_SEM_SPEC,) * 6 + (_ANY_SPEC,) * len(after),
        out_specs=(_HBM_SPEC, _HBM_SPEC),
        input_output_aliases={0: 0, 1: 1},
        compiler_params=pltpu.CompilerParams(has_side_effects=_EFFECT),
    )(src, land, *sems, *after)


def _forward_halves(land, name):
    _, R, C = land.shape
    half = R // 2
    assert half % 16 == 0

    def body(land_ref, out_ref, send_sems, recv_sems):
        x, y, c = _place()
        mine = pl.ds(pl.multiple_of(c * half, 16), half)
        theirs = pl.ds(pl.multiple_of((1 - c) * half, 16), half)
        sends = []
        for n, (px, py) in enumerate(_other_chips(x, y)):
            cp = pltpu.make_async_remote_copy(
                src_ref=land_ref.at[2 * px + py, mine], dst_ref=out_ref.at[2 * px + py, mine],
                send_sem=send_sems.at[n], recv_sem=recv_sems.at[n], device_id=(x, y, 1 - c), device_id_type=MESH)
            cp.start()
            sends.append(cp)
        for n, (px, py) in enumerate(_other_chips(x, y)):
            pltpu.make_async_remote_copy(
                src_ref=land_ref.at[2 * px + py, theirs], dst_ref=out_ref.at[2 * px + py, theirs],
                send_sem=send_sems.at[n], recv_sem=recv_sems.at[n], device_id=(x, y, 1 - c),
                device_id_type=MESH).wait_recv()
        for cp in sends:
            cp.wait_send()

    return pl.pallas_call(
        body, name=name,
        in_specs=[_ANY_SPEC], out_specs=_ANY_SPEC,
        out_shape=jax.ShapeDtypeStruct(land.shape, land.dtype),
        input_output_aliases={0: 0},
        scratch_shapes=[pltpu.SemaphoreType.DMA((3,)), pltpu.SemaphoreType.DMA((3,))],
    )(land)


def _sib_copy(src_ref, land_ref, send_sem, recv_sem):
    x, y, c = _place()
    return pltpu.make_async_remote_copy(src_ref=src_ref, dst_ref=land_ref, send_sem=send_sem, recv_sem=recv_sem,
                                        device_id=(x, y, 1 - c), device_id_type=MESH)


def _sib_start(src, name):
    land = lax.empty(src.shape, src.dtype)

    def body(src_ref, land_ref, send_sem, recv_sem, src_thru, land_thru, token):
        _sib_copy(src_ref, land_ref, send_sem, recv_sem).start()
        token[...] = jnp.zeros_like(token)

    return pl.pallas_call(
        body, name=name,
        out_shape=(pltpu.SemaphoreType.DMA(()), pltpu.SemaphoreType.DMA(()),
                   pltpu.HBM(src.shape, src.dtype), pltpu.HBM(land.shape, land.dtype),
                   jax.ShapeDtypeStruct((8, LANE), F32)),
        in_specs=(_HBM_SPEC, _HBM_SPEC),
        out_specs=(_SEM_SPEC, _SEM_SPEC, _HBM_SPEC, _HBM_SPEC, _VMEM_SPEC),
        input_output_aliases={0: 2, 1: 3},
        compiler_params=pltpu.CompilerParams(has_side_effects=_EFFECT),
    )(pltpu.with_memory_space_constraint(src, pltpu.HBM), pltpu.with_memory_space_constraint(land, pltpu.HBM))


def _sib_wait(started, after, name):
    send_sem, recv_sem, src, land, _ = started

    def body(src_ref, land_ref, send_sem, recv_sem, after_ref, src_out, land_out):
        cp = _sib_copy(src_ref, land_ref, send_sem, recv_sem)
        cp.wait_send()
        cp.wait_recv()

    return pl.pallas_call(
        body, name=name,
        out_shape=(pltpu.HBM(src.shape, src.dtype), pltpu.HBM(land.shape, land.dtype)),
        in_specs=(_HBM_SPEC, _HBM_SPEC, _SEM_SPEC, _SEM_SPEC, _ANY_SPEC),
        out_specs=(_HBM_SPEC, _HBM_SPEC),
        input_output_aliases={0: 0, 1: 1},
        compiler_params=pltpu.CompilerParams(has_side_effects=_EFFECT),
    )(src, land, send_sem, recv_sem, after)


def _sum_slabs(gp, recv, chip, name):
    _, R, C = gp.shape
    tr = _tile(R, PACK_ROWS, 16)

    def body(chip_ref, own_ref, r0_ref, r1_ref, r2_ref, o_ref):
        acc = own_ref[...].astype(F32) + r0_ref[...].astype(F32)
        o_ref[...] = (acc + r1_ref[...].astype(F32)) + r2_ref[...].astype(F32)

    def got(n):
        return pl.BlockSpec((None, tr, C), lambda i, chip_ref: (n, i, 0))

    return pl.pallas_call(
        body, name=name,
        grid_spec=pltpu.PrefetchScalarGridSpec(
            num_scalar_prefetch=1, grid=(R // tr,),
            in_specs=[pl.BlockSpec((None, tr, C), lambda i, chip_ref: (chip_ref[0], i, 0)), got(0), got(1), got(2)],
            out_specs=pl.BlockSpec((tr, C), lambda i, chip_ref: (i, 0))),
        out_shape=jax.ShapeDtypeStruct((R, C), F32),
        compiler_params=_params(("parallel",)),
    )(jnp.reshape(chip, (1,)).astype(jnp.int32), gp, recv, recv, recv)


def _all_reduce_vec(vec, name):
    VR, W = vec.shape

    def body(vec_ref, vall_ref, vout_ref, vsend_sems, vrecv_sems):
        x, y, c = _place()
        vall_ref[4 * x + 2 * y + c] = vec_ref[...]
        sends = []
        peers = []
        for r in range(1, N_DEV):
            dx, dy, dc = (r >> 2) & 1, (r >> 1) & 1, r & 1
            peer = (x ^ dx, y ^ dy, c ^ dc)
            peers.append(peer)
            cp = pltpu.make_async_remote_copy(
                src_ref=vec_ref, dst_ref=vall_ref.at[4 * x + 2 * y + c], send_sem=vsend_sems.at[r - 1],
                recv_sem=vrecv_sems.at[r - 1], device_id=peer, device_id_type=MESH)
            cp.start()
            sends.append(cp)
        for r, peer in enumerate(peers):
            pltpu.make_async_remote_copy(
                src_ref=vec_ref, dst_ref=vall_ref.at[4 * peer[0] + 2 * peer[1] + peer[2]],
                send_sem=vsend_sems.at[r], recv_sem=vrecv_sems.at[r],
                device_id=peer, device_id_type=MESH).wait_recv()
        total = vall_ref[0]
        for d in range(1, N_DEV):
            total = total + vall_ref[d]
        vout_ref[...] = total
        for cp in sends:
            cp.wait_send()

    outs = pl.pallas_call(
        body, name=name,
        in_specs=[_VMEM_SPEC], out_specs=[_VMEM_SPEC, _VMEM_SPEC],
        out_shape=[jax.ShapeDtypeStruct((N_DEV, VR, W), F32), jax.ShapeDtypeStruct((VR, W), F32)],
        scratch_shapes=[pltpu.SemaphoreType.DMA((N_DEV - 1,)), pltpu.SemaphoreType.DMA((N_DEV - 1,))],
    )(vec)
    return outs[1]


class _Pack:
    def __init__(self, group, C):
        self.group, self.C = group, C
        self.rows, self.offs, off = {}, {}, 0
        for nm, (K, N), _ in group:
            assert N <= C, nm
            self.rows[nm] = K if 2 * N > C else -(-(K * N) // C)
            self.offs[nm] = off
            off += -(-self.rows[nm] // 16) * 16
        self.used = off
        self.R = -(-off // PACK_ROWS) * PACK_ROWS

    def _rows_of(self, a):
        K, N = a.shape
        if 2 * N > self.C:
            a = jnp.pad(a, ((0, 0), (0, self.C - N)))
        else:
            a = jnp.pad(a.reshape(-1), (0, -(K * N) % self.C)).reshape(-1, self.C)
        return jnp.pad(a, ((0, -a.shape[0] % 16), (0, 0)))

    def pack(self, shards):
        parts = [self._rows_of(shards[nm].astype(BF16)) for nm, _, _ in self.group]
        return jnp.concatenate(parts + [jnp.zeros((self.R - self.used, self.C), BF16)], axis=0)

    def _shard_of(self, rows, shape):
        K, N = shape
        return rows[:, :N] if 2 * N > self.C else rows.reshape(-1)[:K * N].reshape(K, N)

    def part(self, flat, nm, shape):
        return self._shard_of(flat[self.offs[nm]:self.offs[nm] + self.rows[nm]], shape)

    def slab_rows(self, nm, g):
        (K, N), axis = next((shape, axis) for n, shape, axis in self.group if n == nm)
        cuts = [g[:, k * N:(k + 1) * N] if axis == 1 else g[k * K:(k + 1) * K, :] for k in range(N_CHIPS)]
        return jnp.stack([self._rows_of(c.astype(BF16)) for c in cuts])

    def slabs(self, grads):
        parts = [self.slab_rows(nm, grads[nm]) for nm, _, _ in self.group]
        return jnp.concatenate(parts + [jnp.zeros((N_CHIPS, self.R - self.used, self.C), BF16)], axis=1)

    def full(self, gathered, names=None):
        res = {}
        for nm, (K, N), axis in self.group:
            if names is None or nm in names:
                rows = gathered[:, self.offs[nm]:self.offs[nm] + self.rows[nm]]
                res[nm] = jnp.concatenate([self._shard_of(rows[k], (K, N)) for k in range(N_CHIPS)], axis=axis)
        return res


def _rope_tables(S):
    pos = jnp.arange(S, dtype=F32)
    inv = 1.0 / (ROPE_THETA ** (jnp.arange(0, MLA_ROPE, 2, dtype=F32) / MLA_ROPE))
    ang = pos[:, None] * inv[None, :]
    cos, sin = jnp.cos(ang), jnp.sin(ang)
    half = MLA_ROPE // 2
    z = jnp.zeros((S, half), F32)
    one = jnp.ones((S, LANE - MLA_ROPE), F32)
    zero = jnp.zeros((S, LANE - MLA_ROPE), F32)
    kc = jnp.concatenate([cos, cos, one], axis=1)
    ksa = jnp.concatenate([-sin, z, zero], axis=1)
    ksb = jnp.concatenate([z, sin, zero], axis=1)
    qc = jnp.concatenate([jnp.ones((S, MLA_NOPE), F32), kc], axis=1)
    qsa = jnp.concatenate([jnp.zeros((S, MLA_NOPE), F32), ksa], axis=1)
    qsb = jnp.concatenate([jnp.zeros((S, MLA_NOPE), F32), ksb], axis=1)
    return (kc, ksa, ksb), (qc, qsa, qsb)


def _pad_cols(a, width):
    return jnp.pad(a, ((0, 0), (0, width - a.shape[1])))


def kernel(x, attn_norm, w_in, fox_f_bias, q_norm, w_uq, kv_norm, w_ukv, w_mla_branch, w_fox_branch, w_out, mlp_norm, w_up, w_down, final_norm, loss_target, m_attn_norm, m_w_in, m_fox_f_bias, m_q_norm, m_w_uq, m_kv_norm, m_w_ukv, m_w_mla_branch, m_w_fox_branch, m_w_out, m_mlp_norm, m_w_up, m_w_down, m_final_norm, v_attn_norm, v_w_in, v_fox_f_bias, v_q_norm, v_w_uq, v_kv_norm, v_w_ukv, v_w_mla_branch, v_w_fox_branch, v_w_out, v_mlp_norm, v_w_up, v_w_down, v_final_norm):
    _, S, D = x.shape
    H, HF = MLA_HEADS, FOX_HEADS
    QL, KVL = MLA_Q_LORA, MLA_KV_LORA
    assert H == HF and H <= 8
    xs = x[0]
    target = loss_target[0]
    C = D
    chip = 2 * lax.axis_index("x") + lax.axis_index("y")

    def flip(a):
        return jnp.transpose(a, (0, 2, 1))

    w_in, m_w_in, v_w_in = flip(w_in), flip(m_w_in), flip(v_w_in)
    weights = {"attn_norm": attn_norm, "w_in": w_in, "fox_f_bias": fox_f_bias, "q_norm": q_norm, "w_uq": w_uq,
               "kv_norm": kv_norm, "w_ukv": w_ukv, "w_mla_branch": w_mla_branch, "w_fox_branch": w_fox_branch,
               "w_out": w_out, "mlp_norm": mlp_norm, "w_up": w_up, "w_down": w_down, "final_norm": final_norm}
    moments = {"attn_norm": (m_attn_norm, v_attn_norm), "w_in": (m_w_in, v_w_in), "fox_f_bias": (m_fox_f_bias, v_fox_f_bias),
               "q_norm": (m_q_norm, v_q_norm), "w_uq": (m_w_uq, v_w_uq), "kv_norm": (m_kv_norm, v_kv_norm),
               "w_ukv": (m_w_ukv, v_w_ukv), "w_mla_branch": (m_w_mla_branch, v_w_mla_branch),
               "w_fox_branch": (m_w_fox_branch, v_w_fox_branch), "w_out": (m_w_out, v_w_out),
               "mlp_norm": (m_mlp_norm, v_mlp_norm), "w_up": (m_w_up, v_w_up), "w_down": (m_w_down, v_w_down),
               "final_norm": (m_final_norm, v_final_norm)}

    def group(names_axes):
        return [(nm, weights[nm].shape[1:], axis) for nm, axis in names_axes]

    pack_a = _Pack(group([("w_in", 0), ("w_uq", 1), ("w_ukv", 1)]), C)
    pack_b = _Pack(group([("w_down", 0), ("w_up", 1), ("w_out", 0), ("w_mla_branch", 1), ("w_fox_branch", 1)]), C)
    RA, RB = pack_a.R, pack_b.R
    wp_b = pack_b.pack({nm: weights[nm][0] for nm, _, _ in pack_b.group})
    n_in = w_in.shape[1]
    rows_in = -(-n_in // 16) * 16
    assert pack_a.offs["w_in"] == 0 and w_in.shape[2] == C
    assert all((k * n_in) % 16 + n_in <= rows_in for k in range(N_CHIPS))
    shifted = lax.dynamic_update_slice(jnp.zeros((rows_in, C), BF16), w_in[0].astype(BF16), ((chip * n_in) % 16, 0))
    wp_a = jnp.concatenate([shifted] + [pack_a._rows_of(weights[nm][0].astype(BF16)) for nm, _, _ in pack_a.group[1:]]
                           + [jnp.zeros((RA - pack_a.used, C), BF16)], axis=0)
    ag_a = _xchg_start(wp_a, lax.empty((N_CHIPS, RA, C), BF16), "half", jnp.zeros((8, LANE), F32), "all_gather_start_a")
    xn = _norm_fwd(xs, attn_norm, "attn_norm_fwd", order=ag_a[3])
    own_a, land_a = _xchg_wait(ag_a, "half", (xn, wp_b), "all_gather_wait_a")
    land_a = _forward_halves(land_a, "all_gather_forward_a")
    gathered_a = lax.dynamic_update_slice(land_a, own_a[None], (chip, 0, 0))
    ag_b = _xchg_start(wp_b, lax.empty((N_CHIPS, RB, C), BF16), True, gathered_a, "all_gather_start_b")
    full = pack_a.full(gathered_a, ("w_uq", "w_ukv"))
    tile0 = [(k * n_in) // 16 * 16 for k in range(N_CHIPS)]
    total = tile0[-1] + rows_in
    full["w_in"] = sum(jnp.pad(gathered_a[k, :rows_in], ((tile0[k], total - tile0[k] - rows_in), (0, 0)))
                       for k in range(N_CHIPS))

    o_ckv = QL
    o_kr = o_ckv + KVL
    o_fq = o_kr + MLA_ROPE
    o_ff = o_fq + 3 * HF * FOX_HEAD_DIM
    o_g = o_ff + HF
    wi = full["w_in"]
    assert N_CHIPS * n_in == o_g + 2 * D and wi.shape[0] >= o_g + 2 * D
    WS = QL + KVL + 2 * LANE
    NQKV = 3 * HF * FOX_HEAD_DIM

    def pad_rows(a, rows):
        return jnp.pad(a, ((0, rows - a.shape[0]), (0, 0)))

    w_small = jnp.concatenate([wi[:o_kr], pad_rows(wi[o_kr:o_fq], LANE), pad_rows(wi[o_ff:o_g], LANE)], axis=0)
    w_qkv = wi[o_fq:o_ff]
    w_g = wi[o_g:o_g + 2 * D]
    w_pack = jnp.concatenate([w_small, w_qkv, w_g], axis=0)
    dqk = MLA_NOPE + MLA_ROPE
    w_uq_p = jnp.pad(full["w_uq"].reshape(QL, H, dqk), ((0, 0), (0, 0), (0, QPAD - dqk))).reshape(QL, H * QPAD)
    ukv = full["w_ukv"].reshape(KVL, H, MLA_NOPE + MLA_V)
    w_ukv_p = jnp.concatenate([ukv[:, :, :MLA_NOPE].reshape(KVL, H * MLA_NOPE),
                               ukv[:, :, MLA_NOPE:].reshape(KVL, H * MLA_V)], axis=1)

    (kc, ksa, ksb), (qc, qsa, qsb) = _rope_tables(S)
    bias_pad = _pad_cols(fox_f_bias, LANE)

    small = _matmul(xn, w_small, "nt", [F32], "proj_small")
    n_fq = HF * FOX_HEAD_DIM
    q_scale = jnp.concatenate([jnp.full((1, n_fq), LOG2E / math.sqrt(FOX_HEAD_DIM), F32),
                               jnp.ones((1, NQKV - n_fq), F32)], axis=1)
    qkv = _matmul(xn, w_qkv, "nt", [BF16], "proj_qkv", col_extras=(q_scale,), epilogue=lambda acc, cs: (acc * cs,))
    gpre = _matmul(xn, w_g, "nt", [F32], "proj_gates")
    cqn, ckvn, kr, cum = _prep_fwd(small, q_norm, kv_norm, bias_pad, kc, ksa, ksb, HF, "prep_fwd")
    c2_mla = LOG2E / math.sqrt(dqk)
    q_rot = _matmul(cqn, w_uq_p, "nn", [BF16], "mla_q_up", tn=QPAD, row_extras=(qc * c2_mla, qsa * c2_mla, qsb * c2_mla),
                    epilogue=lambda acc, c, sa, sb: (_rope(acc, c, sa, sb, 1),))
    kv2 = _matmul(ckvn, w_ukv_p, "nn", [BF16], "mla_kv_up")

    def mla_att(qsub):
        return _AttT(S, H, (q_rot, QPAD, 0, True), [(kv2, MLA_NOPE, 0, True), (kr, LANE, 0, False)],
                     (kv2, MLA_V, H, True), 1.0 / math.sqrt(dqk), True, qsub=qsub)

    mla = mla_att(QSUB)
    o_mla, lse_mla = _att_fwd_t(mla_att(2 * QSUB), "mla_att_fwd")

    cum_t = jnp.transpose(cum[:, :HF]) * LOG2E
    cum_rep = jnp.broadcast_to(cum_t[:, :, None], (HF, S, min(QSUB, _tile(S, ATT_T))))
    fox = _AttT(S, HF, (qkv, FOX_HEAD_DIM, 0, True), [(qkv, FOX_HEAD_DIM, HF, True)],
                (qkv, FOX_HEAD_DIM, 2 * HF, True), 1.0 / math.sqrt(FOX_HEAD_DIM), False, cum_rep)
    o_fox, ox_fox, lse_fox = _att_fwd_t(fox, "fox_att_fwd", exact=True)

    own_b, land_b = _xchg_wait(ag_b, True, (lse_fox, lse_mla, gpre), "all_gather_wait_b")
    gathered_b = lax.dynamic_update_slice(land_b, own_b[None], (chip, 0, 0))
    full.update(pack_b.full(gathered_b, ("w_mla_branch", "w_fox_branch", "w_out")))
    w_mb, w_fb, w_o = (full[n] for n in ("w_mla_branch", "w_fox_branch", "w_out"))

    def b_of(nm, mode, tn, tk):
        (K, N), axis = next((shape, axis) for n, shape, axis in pack_b.group if n == nm)
        off = pack_b.offs[nm]
        shape = (N_CHIPS * K, N) if axis == 0 else (K, N_CHIPS * N)
        t_r, t_c = (tk, tn) if mode == "nn" else (tn, tk)
        t_r, t_c = _tile(shape[0], t_r), _tile(shape[1], t_c)
        if not (N == C and K % t_r == 0 and N % t_c == 0 and off % t_r == 0):
            return pack_b.full(gathered_b, (nm,))[nm], None
        base = off // t_r
        if axis == 0:
            per = K // t_r
            place = lambda rb, cb: (rb // per, base + rb % per, cb)
        else:
            per = N // t_c
            place = lambda rb, cb: (cb // per, base + rb, cb % per)
        return gathered_b, (shape, (lambda j, k: place(k, j)) if mode == "nn" else (lambda j, k: place(j, k)))

    y_mla = _matmul(o_mla, w_mb, "nn", [F32], "mla_branch")

    def gate_merge(acc, ga, gb, ya):
        return acc, _sigmoid(ga) * ya + _sigmoid(gb) * acc

    y_fox, merged = _matmul(o_fox, w_fb, "nn", [F32, BF16], "fox_branch_gates", tn=512,
                            extras=((gpre, 0), (gpre, 1), y_mla), epilogue=gate_merge)
    h1 = _matmul(merged, w_o, "nn", [F32], "out_proj", extras=(xs,), epilogue=lambda acc, r: (acc + r,))
    hn = _norm_fwd(h1, mlp_norm, "mlp_norm_fwd")

    def relu2(acc):
        a = jnp.maximum(acc, 0.0)
        return a * a, a

    w_u, w_u_in = b_of("w_up", "nn", 1024, 2048)
    u, a_pos = _matmul(hn, w_u, "nn", [BF16, BF16], "mlp_up", epilogue=relu2, b_in=w_u_in)
    w_d, w_d_in = b_of("w_down", "nn", 1024, 2048)
    h2 = _matmul(u, w_d, "nn", [F32], "mlp_down", tn=1024, extras=(h1,), epilogue=lambda acc, r: (acc + r,),
                 b_in=w_d_in)
    dh2, dh2_b, g_final, loss_part = _final(h2, final_norm.reshape(1, D), target, "final_norm_loss")

    gp_b = lax.empty((N_CHIPS, RB, C), BF16)
    by_glue = {}

    def grad_b(nm, a, b, name):
        nonlocal gp_b
        (K, N), axis = next((shape, axis) for n, shape, axis in pack_b.group if n == nm)
        off = pack_b.offs[nm]
        tm = min(1024, K) if axis == 0 else min(1024, a.shape[1])
        tn = min(1024, N) if axis == 1 else min(1024, b.shape[1])
        if not (N == C and tm % LANE == 0 and tn % LANE == 0 and K % tm == 0 and N % tn == 0 and off % tm == 0):
            by_glue[nm] = _mm_tn(a, b, name)
            return
        base = off // tm
        if axis == 0:
            per = K // tm
            place = lambda i, j: (i // per, base + i % per, j)
        else:
            per = N // tn
            place = lambda i, j: (j // per, base + i, j % per)
        gp_b = _mm_tn(a, b, name, tm=tm, tn=tn, into=(gp_b, place))

    w_d, w_d_in = b_of("w_down", "nt", 1024, 2048)
    da = _matmul(dh2_b, w_d, "nt", [BF16], "mlp_down_dx", extras=(a_pos,),
                 epilogue=lambda acc, a: (acc * (2.0 * a.astype(F32)),), b_in=w_d_in)
    grad_b("w_down", u, dh2_b, "mlp_down_dw")
    w_u, w_u_in = b_of("w_up", "nt", 1024, 2048)
    dhn = _matmul(da, w_u, "nt", [F32], "mlp_up_dx", tn=1024, b_in=w_u_in)
    grad_b("w_up", hn, da, "mlp_up_dw")
    dh1, dh1_b, g_mlp_norm = _norm_bwd(h1, dhn, mlp_norm, dh2, "mlp_norm_bwd")

    def gate_bwd(acc, ga, gb, ya, yb):
        ga, gb = _sigmoid(ga), _sigmoid(gb)
        return acc * ga, acc * gb, acc * ya * (ga * (1.0 - ga)), acc * yb * (gb * (1.0 - gb))

    dy_mla, dy_fox, dg_mla, dg_fox = _matmul(dh1_b, w_o, "nt", [BF16] * 4, "out_proj_dx_gates", tn=512,
                                             extras=((gpre, 0), (gpre, 1), y_mla, y_fox), epilogue=gate_bwd)
    grad_b("w_out", merged, dh1_b, "out_proj_dw")
    do_mla = _matmul(dy_mla, w_mb, "nt", [BF16], "mla_branch_dx")
    grad_b("w_mla_branch", o_mla, dy_mla, "mla_branch_dw")
    do_fox = _matmul(dy_fox, w_fb, "nt", [BF16], "fox_branch_dx")
    grad_b("w_fox_branch", o_fox, dy_fox, "fox_branch_dw")
    for nm, g in by_glue.items():
        gp_b = lax.dynamic_update_slice(gp_b, pack_b.slab_rows(nm, g), (0, pack_b.offs[nm], 0))
    if RB > pack_b.used:
        gp_b = lax.dynamic_update_slice(gp_b, jnp.zeros((N_CHIPS, RB - pack_b.used, C), BF16), (0, pack_b.used, 0))

    rs_b = _xchg_start(gp_b, lax.empty((3, RB, C), BF16), False, do_fox, "grad_scatter_start_b")

    dq_rot, dk_nope, dkr_heads, dv_mla = _att_bwd_t(mla, do_mla, lse_mla, o_mla, BF16, [BF16, F32],
                                                    "mla_att_bwd", dq_rope=(qc, qsa, qsb), order=rs_b[3])
    dfq, dfk, dfv, dcum = _att_bwd_t(fox, do_fox, lse_fox, ox_fox, BF16, [BF16], "fox_att_bwd")

    gp_b_sent, recv_b = _xchg_wait(rs_b, False, (dfq, dq_rot), "grad_scatter_wait_b")
    swap_b = _sib_start(_sum_slabs(gp_b_sent, recv_b, chip, "grad_sum_b"), "grad_swap_start_b")

    dcqn = _matmul(dq_rot, w_uq_p, "nt", [F32], "mla_q_up_dx", order=swap_b[4])
    g_w_uq_p = _mm_tn(cqn, dq_rot, "mla_q_up_dw")
    dkv2 = jnp.concatenate([dk_nope, dv_mla], axis=1)
    dckvn = _matmul(dkv2, w_ukv_p, "nt", [F32], "mla_kv_up_dx")
    g_w_ukv_p = _mm_tn(ckvn, dkv2, "mla_kv_up_dw")

    dcum_rows = jnp.pad(dcum[:, :, 0], ((0, 8 - HF), (0, 0)))
    dlogf_rows = _suffix_sum_rows(dcum_rows, "fox_forget_suffix_sum")
    dlogf = _pad_cols(jnp.transpose(dlogf_rows[:HF]), LANE)
    d_small, g_q_norm, g_kv_norm, g_bias = _prep_bwd(
        small, dcqn, dckvn, dkr_heads, dlogf, q_norm, kv_norm, bias_pad, kc, ksa, ksb, H, "prep_bwd")
    dproj = [d_small, dfq, dfk, dfv, dg_mla, dg_fox]
    gs, gfq, gfk, gfv, gg_mla, gg_fox = [
        _matmul(part, xn, "tn", [BF16], "proj_dw_" + tag, tm=1024, tn=1024, tk=2048)
        for part, tag in zip(dproj, ("small", "fq", "fk", "fv", "g_mla", "g_fox"))]

    g_w_in = jnp.concatenate([gs[:o_kr], gs[o_kr:o_kr + MLA_ROPE], gfq, gfk, gfv,
                              gs[o_kr + LANE:o_kr + LANE + HF], gg_mla, gg_fox], axis=0)
    g_w_uq = g_w_uq_p.reshape(QL, H, QPAD)[:, :, :dqk].reshape(QL, H * dqk)
    g_w_ukv = jnp.concatenate([g_w_ukv_p[:, :H * MLA_NOPE].reshape(KVL, H, MLA_NOPE),
                               g_w_ukv_p[:, H * MLA_NOPE:].reshape(KVL, H, MLA_V)], axis=2).reshape(KVL, -1)

    gp_a = pack_a.slabs({"w_in": g_w_in, "w_uq": g_w_uq, "w_ukv": g_w_ukv})
    rs_a = _xchg_start(gp_a, lax.empty((3, RA, C), BF16), False, gg_fox, "grad_scatter_start_a")
    dxn = _matmul_row_parts(dproj, w_pack, F32, "proj_dx", order=rs_a[3])
    grad_x, g_attn_norm = _norm_bwd(xs, dxn, attn_norm, dh1, "attn_norm_bwd", with_bf16=False)
    gp_a_sent, recv_a = _xchg_wait(rs_a, False, grad_x, "grad_scatter_wait_a")
    swap_a = _sib_start(_sum_slabs(gp_a_sent, recv_a, chip, "grad_sum_a"), "grad_swap_start_a")
    vec_w = max(D, LANE)
    vec_rows = [g_attn_norm, g_mlp_norm, g_final, g_q_norm, g_kv_norm, g_bias, loss_part]
    vec = jnp.concatenate([_pad_cols(v, vec_w) for v in vec_rows] + [jnp.zeros((1, vec_w), F32)], axis=0)
    vsum = _all_reduce_vec(vec, "all_reduce_vectors")
    part_b, sib_b = _sib_wait(swap_b, vsum, "grad_swap_wait_b")

    grads, deltas, new_m, new_v = {}, {}, {}, {}

    def update(pack, mine, theirs):
        for nm, shape, _ in pack.group:
            K, N = shape
            if N == pack.C and K % 8 == 0 and pack.offs[nm] % _tile(K, 256, 8) == 0:
                g, d, nm_, nv_ = _adamw(weights[nm], mine, theirs, moments[nm][0], moments[nm][1], "adamw_" + nm,
                                        g_row=pack.offs[nm])
            else:
                g, d, nm_, nv_ = _adamw(weights[nm], pack.part(mine, nm, shape), pack.part(theirs, nm, shape),
                                        moments[nm][0], moments[nm][1], "adamw_" + nm)
            grads[nm], deltas[nm], new_m[nm], new_v[nm] = g, d, nm_, nv_
        return g

    last_b = update(pack_b, part_b, sib_b)
    part_a, sib_a = _sib_wait(swap_a, last_b, "grad_swap_wait_a")
    update(pack_a, part_a, sib_a)

    vec_names = ["attn_norm", "mlp_norm", "final_norm", "q_norm", "kv_norm", "fox_f_bias"]

    def vec_pack(arrs):
        return jnp.concatenate([_pad_cols(a.reshape(1, -1), vec_w) for a in arrs]
                               + [jnp.zeros((2, vec_w), F32)], axis=0)[None]

    vg, vd, vm, vv = _adamw(vec_pack([weights[n] for n in vec_names]), vsum, jnp.zeros_like(vsum),
                            vec_pack([moments[n][0] for n in vec_names]), vec_pack([moments[n][1] for n in vec_names]),
                            "adamw_vectors")
    for r, nm in enumerate(vec_names):
        shp = weights[nm].shape
        n = weights[nm].size
        grads[nm] = vsum[r, :n].reshape(shp)
        deltas[nm], new_m[nm], new_v[nm] = (vd[0, r, :n].reshape(shp), vm[0, r, :n].reshape(shp),
                                            vv[0, r, :n].reshape(shp))
    loss = vsum[6, 0]

    for res in (grads, deltas, new_m, new_v):
        res["w_in"] = flip(res["w_in"])
    order = ["attn_norm", "w_in", "fox_f_bias", "q_norm", "w_uq", "kv_norm", "w_ukv", "w_mla_branch", "w_fox_branch",
             "w_out", "mlp_norm", "w_up", "w_down", "final_norm"]
    return (loss, grad_x[None], *[grads[n] for n in order], *[deltas[n] for n in order],
            *[new_m[n] for n in order], *[new_v[n] for n in order])
```

```python
import math

import jax
import jax.numpy as jnp
from jax import lax
from jax.experimental import pallas as pl
from jax.experimental.pallas import tpu as pltpu

CHUNK = 64
MLA_HEADS = 8
MLA_Q_LORA = 512
MLA_KV_LORA = 256
MLA_NOPE = 128
MLA_ROPE = 64
MLA_V = 128
ROPE_THETA = 10000.0
FOX_HEADS = 8
FOX_HEAD_DIM = 128
EPS = 1e-6

ADAM_LR = 0.001
ADAM_B1 = 0.9
ADAM_B2 = 0.999
ADAM_EPS = 1e-08
ADAM_WD = 0.01
ADAM_STEP = 10

LANE = 128
QPAD = 2 * LANE
N_CHIPS = 4
N_DEV = 8
VMEM_LIMIT = 48 * 1024 * 1024
ATT_T = 2048
QSUB = 256
ROW_T = 256
PACK_ROWS = 256
LOG2E = 1.4426950408889634

BF16 = jnp.bfloat16
F32 = jnp.float32
MESH = pl.DeviceIdType.MESH

_NT = (((1,), (1,)), ((), ()))
_TN = (((0,), (0,)), ((), ()))
_NN = (((1,), (0,)), ((), ()))


def _tile(dim, pref, align=LANE):
    if dim <= pref:
        return dim
    t = (pref // align) * align
    while t >= align:
        if dim % t == 0:
            return t
        t -= align
    return dim


def _params(sem=None):
    return pltpu.CompilerParams(dimension_semantics=sem, vmem_limit_bytes=VMEM_LIMIT)


_ANY_SPEC = pl.BlockSpec(memory_space=pl.ANY)


def _matmul(a, b, mode, out_dtypes, name, *, tm=1024, tn=1024, tk=2048, extras=(), row_extras=(), col_extras=(),
            epilogue=None, order=None, into=None, b_in=None):
    b_shape = b.shape if b_in is None else b_in[0]
    if mode == "nn":
        (M, K), (K2, N) = a.shape, b_shape
    elif mode == "nt":
        (M, K), (N, K2) = a.shape, b_shape
    else:
        (K, M), (K2, N) = a.shape, b_shape
    assert K == K2, (name, a.shape, b_shape)
    tm, tn, tk = _tile(M, tm), _tile(N, tn), _tile(K, tk)
    nk = K // tk
    extras = [e if isinstance(e, tuple) else (e, 0) for e in extras]
    n_out = len(out_dtypes)
    n_ex = len(extras) + len(row_extras) + len(col_extras)
    n_ord = 0 if order is None else 1
    assert all(r.shape == (M, tn) for r in row_extras), name
    dims = {"nn": _NN, "nt": _NT, "tn": _TN}[mode]

    def body(*refs):
        a_ref, b_ref = refs[0], refs[1]
        ex_refs = refs[2:2 + n_ex]
        o_refs = refs[2 + n_ex + n_ord:2 + n_ex + n_ord + n_out]
        acc_ref = refs[2 + n_ex + n_ord + n_out]
        k = pl.program_id(2)
        part = lax.dot_general(a_ref[...], b_ref[...], dims, preferred_element_type=F32)

        @pl.when(k == 0)
        def _():
            acc_ref[...] = part

        @pl.when(k > 0)
        def _():
            acc_ref[...] += part

        @pl.when(k == nk - 1)
        def _():
            acc = acc_ref[...]
            if epilogue is None:
                outs = (acc,)
            else:
                outs = epilogue(acc, *[r[...] for r in ex_refs])
            for o_ref, o in zip(o_refs, outs):
                o_ref[...] = o.astype(o_ref.dtype)

    if mode == "nn":
        a_spec = pl.BlockSpec((tm, tk), lambda i, j, k: (i, k))
        b_spec = pl.BlockSpec((tk, tn), lambda i, j, k: (k, j))
    elif mode == "nt":
        a_spec = pl.BlockSpec((tm, tk), lambda i, j, k: (i, k))
        b_spec = pl.BlockSpec((tn, tk), lambda i, j, k: (j, k))
    else:
        a_spec = pl.BlockSpec((tk, tm), lambda i, j, k: (k, i))
        b_spec = pl.BlockSpec((tk, tn), lambda i, j, k: (k, j))
    if b_in is not None:
        b_block = (None, tn, tk) if mode == "nt" else (None, tk, tn)
        b_spec = pl.BlockSpec(b_block, lambda i, j, k: b_in[1](j, k))
    mn_spec = pl.BlockSpec((tm, tn), lambda i, j, k: (i, j))
    row_spec = pl.BlockSpec((tm, tn), lambda i, j, k: (i, 0))
    col_spec = pl.BlockSpec((1, tn), lambda i, j, k: (0, j))
    out_specs = [mn_spec] * n_out
    out_shape = [jax.ShapeDtypeStruct((M, N), dt) for dt in out_dtypes]
    aliases = {}
    if into is not None:
        buf, place = into
        assert n_out == 1 and n_ord == 1 and order is buf, name
        out_specs = [pl.BlockSpec((None, tm, tn), lambda i, j, k: place(i, j))]
        out_shape = [jax.ShapeDtypeStruct(buf.shape, buf.dtype)]
        aliases = {2 + n_ex: 0}
    outs = pl.pallas_call(
        body,
        name=name,
        grid=(M // tm, N // tn, nk),
        in_specs=([a_spec, b_spec]
                  + [pl.BlockSpec((tm, tn), lambda i, j, k, g=g: (i, j + g * (N // tn))) for _, g in extras]
                  + [row_spec] * len(row_extras) + [col_spec] * len(col_extras) + [_ANY_SPEC] * n_ord),
        out_specs=out_specs,
        out_shape=out_shape,
        scratch_shapes=[pltpu.VMEM((tm, tn), F32)],
        input_output_aliases=aliases,
        compiler_params=_params(("parallel", "parallel", "arbitrary")),
    )(a, b, *[e for e, _ in extras], *row_extras, *col_extras, *([] if order is None else [order]))
    return outs[0] if n_out == 1 else outs


def _matmul_row_parts(parts, b, out_dtype, name, *, tm=512, tk=2048, order=None):
    M, (K, N) = parts[0].shape[0], b.shape
    widths = [p.shape[1] for p in parts]
    assert sum(widths) == K, name
    tm, tk = _tile(M, tm), _tile(K, tk)
    nk = K // tk
    steps, at = [], 0
    for p, w in enumerate(widths):
        off = 0
        while off < w:
            k, room = divmod(at, tk)
            take = min(w - off, tk - room)
            if room == 0:
                steps.append([])
            steps[k].append((p, off, take, room))
            off += take
            at += take
    assert len(steps) == nk and all(t % LANE == 0 and o % LANE == 0 for s in steps for _, o, t, _ in s), name
    n_parts = len(parts)
    n_ord = 0 if order is None else 1

    def body(*refs):
        a_refs = refs[0:n_parts]
        b_ref = refs[n_parts]
        o_ref, acc_ref = refs[n_parts + 1 + n_ord], refs[n_parts + 2 + n_ord]
        k = pl.program_id(1)
        for kk, pieces in enumerate(steps):
            @pl.when(k == kk)
            def _(kk=kk, pieces=pieces):
                part = None
                for p, off, take, room in pieces:
                    d = jnp.dot(a_refs[p][:, off:off + take], b_ref[room:room + take, :], preferred_element_type=F32)
                    part = d if part is None else part + d
                if kk == 0:
                    acc_ref[...] = part
                else:
                    acc_ref[...] += part

        @pl.when(k == nk - 1)
        def _():
            o_ref[...] = acc_ref[...].astype(o_ref.dtype)

    return pl.pallas_call(
        body, name=name, grid=(M // tm, nk),
        in_specs=[pl.BlockSpec((tm, w), lambda i, k: (i, 0)) for w in widths]
        + [pl.BlockSpec((tk, N), lambda i, k: (k, 0))] + [_ANY_SPEC] * n_ord,
        out_specs=pl.BlockSpec((tm, N), lambda i, k: (i, 0)),
        out_shape=jax.ShapeDtypeStruct((M, N), out_dtype),
        scratch_shapes=[pltpu.VMEM((tm, N), F32)],
        compiler_params=_params(("parallel", "arbitrary")),
    )(*parts, b, *([] if order is None else [order]))


def _mm_tn(a, b, name, tm=1024, tn=1024, into=None):
    return _matmul(a, b, "tn", [F32], name, tm=tm, tn=tn, tk=2048, into=into,
                   order=None if into is None else into[0])


def _row_spec(ts, width, col=0):
    return pl.BlockSpec((ts, width), lambda i: (i, col))


def _full_spec(shape):
    return pl.BlockSpec(shape, lambda i: tuple(0 for _ in shape))


def _rms(x):
    return lax.rsqrt(jnp.mean(x * x, axis=-1, keepdims=True) + EPS)


def _rms_bwd(x, dy, g):
    r = _rms(x)
    xh = x * r
    gy = dy * g
    dx = r * (gy - xh * jnp.mean(xh * gy, axis=-1, keepdims=True))
    return dx, dy * xh


def _norm_fwd(x, g, name, order=None):
    S, D = x.shape
    ts = _tile(S, ROW_T, 8)

    def body(x_ref, g_ref, *rest):
        o_ref = rest[-1]
        xv = x_ref[...]
        o_ref[...] = ((xv * _rms(xv)) * g_ref[...]).astype(BF16)

    extra = [] if order is None else [order]
    return pl.pallas_call(
        body, name=name, grid=(S // ts,),
        in_specs=[_row_spec(ts, D), _full_spec((1, D))] + [_ANY_SPEC] * len(extra),
        out_specs=_row_spec(ts, D),
        out_shape=jax.ShapeDtypeStruct((S, D), BF16),
        compiler_params=_params(("parallel",)),
    )(x, g, *extra)


def _norm_bwd(x, dy, g, dres, name, with_bf16=True):
    S, D = x.shape
    ts = _tile(S, ROW_T, 8)

    def body(x_ref, dy_ref, g_ref, dres_ref, dx_ref, *rest):
        dg_ref = rest[-1]
        dx, dg_rows = _rms_bwd(x_ref[...], dy_ref[...], g_ref[...])
        dx = dres_ref[...] + dx
        dx_ref[...] = dx
        if with_bf16:
            rest[0][...] = dx.astype(BF16)

        @pl.when(pl.program_id(0) == 0)
        def _():
            dg_ref[...] = jnp.zeros_like(dg_ref)

        dg_ref[...] += jnp.sum(dg_rows, axis=0, keepdims=True)

    return pl.pallas_call(
        body, name=name, grid=(S // ts,),
        in_specs=[_row_spec(ts, D), _row_spec(ts, D), _full_spec((1, D)), _row_spec(ts, D)],
        out_specs=[_row_spec(ts, D)] * (2 if with_bf16 else 1) + [_full_spec((1, D))],
        out_shape=([jax.ShapeDtypeStruct((S, D), F32)] + [jax.ShapeDtypeStruct((S, D), BF16)] * with_bf16
                   + [jax.ShapeDtypeStruct((1, D), F32)]),
        compiler_params=_params(("arbitrary",)),
    )(x, dy, g, dres)


def _rope(x, c, sa, sb, sign):
    w = x.shape[-1]
    half = MLA_ROPE // 2
    fwd = pltpu.roll(x, w - half, 1)
    back = pltpu.roll(x, half, 1)
    if sign < 0:
        return x * c - fwd * sa - back * sb
    return x * c + fwd * sa + back * sb


def _split3(x):
    hi = x.astype(BF16)
    r1 = x - hi.astype(F32)
    mid = r1.astype(BF16)
    lo = (r1 - mid.astype(F32)).astype(BF16)
    return hi, mid, lo


def _prep_fwd(small, q_norm, kv_norm, bias_pad, kc, ksa, ksb, n_heads, name):
    S, W = small.shape
    QL, KVL = q_norm.shape[1], kv_norm.shape[1]
    assert W == QL + KVL + 2 * LANE
    ts = _tile(S, ROW_T, 8)
    tri = (lax.broadcasted_iota(jnp.int32, (ts, ts), 0) >= lax.broadcasted_iota(jnp.int32, (ts, ts), 1)).astype(BF16)

    def body(s_ref, qn_ref, kvn_ref, b_ref, kc_ref, ksa_ref, ksb_ref, tri_ref,
             cqn_ref, ckvn_ref, kr_ref, cum_ref, carry_ref):
        cq = s_ref[:, 0:QL]
        cqn_ref[...] = ((cq * _rms(cq)) * qn_ref[...]).astype(BF16)
        ckv = s_ref[:, QL:QL + KVL]
        ckvn_ref[...] = ((ckv * _rms(ckv)) * kvn_ref[...]).astype(BF16)
        kr = s_ref[:, QL + KVL:QL + KVL + LANE]
        kr_ref[...] = _rope(kr, kc_ref[...], ksa_ref[...], ksb_ref[...], 1).astype(BF16)
        z = s_ref[:, QL + KVL + LANE:W] + b_ref[...]
        logf = jnp.minimum(z, 0.0) - jnp.log1p(jnp.exp(-jnp.abs(z)))
        lane = lax.broadcasted_iota(jnp.int32, logf.shape, 1)
        logf = jnp.where(lane < n_heads, logf, 0.0)

        @pl.when(pl.program_id(0) == 0)
        def _():
            carry_ref[...] = jnp.zeros_like(carry_ref)

        t = tri_ref[...]
        cum = carry_ref[...]
        for part in _split3(logf):
            cum = cum + jnp.dot(t, part, preferred_element_type=F32)
        cum_ref[...] = cum
        carry_ref[...] = cum[ts - 1:ts, :]

    return pl.pallas_call(
        body, name=name, grid=(S // ts,),
        in_specs=[_row_spec(ts, W), _full_spec((1, QL)), _full_spec((1, KVL)), _full_spec((1, LANE)),
                  _row_spec(ts, LANE), _row_spec(ts, LANE), _row_spec(ts, LANE), _full_spec((ts, ts))],
        out_specs=[_row_spec(ts, QL), _row_spec(ts, KVL), _row_spec(ts, LANE), _row_spec(ts, LANE)],
        out_shape=[jax.ShapeDtypeStruct((S, QL), BF16), jax.ShapeDtypeStruct((S, KVL), BF16),
                   jax.ShapeDtypeStruct((S, LANE), BF16), jax.ShapeDtypeStruct((S, LANE), F32)],
        scratch_shapes=[pltpu.VMEM((1, LANE), F32)],
        compiler_params=_params(("arbitrary",)),
    )(small, q_norm, kv_norm, bias_pad, kc, ksa, ksb, tri)


def _prep_bwd(small, dcqn, dckvn, dkr_heads, dlogf, q_norm, kv_norm, bias_pad, kc, ksa, ksb, n_heads, name):
    S, W = small.shape
    QL, KVL = q_norm.shape[1], kv_norm.shape[1]
    ts = _tile(S, ROW_T, 8)

    def body(s_ref, dcq_ref, dckv_ref, dkr_ref, dlf_ref, qn_ref, kvn_ref, b_ref, kc_ref, ksa_ref, ksb_ref,
             ds_ref, gq_ref, gkv_ref, gb_ref):
        dcq, gq_rows = _rms_bwd(s_ref[:, 0:QL], dcq_ref[...], qn_ref[...])
        ds_ref[:, 0:QL] = dcq.astype(BF16)
        dckv, gkv_rows = _rms_bwd(s_ref[:, QL:QL + KVL], dckv_ref[...], kvn_ref[...])
        ds_ref[:, QL:QL + KVL] = dckv.astype(BF16)
        dkr = dkr_ref[:, 0:LANE]
        for h in range(1, n_heads):
            dkr = dkr + dkr_ref[:, h * LANE:(h + 1) * LANE]
        ds_ref[:, QL + KVL:QL + KVL + LANE] = _rope(dkr, kc_ref[...], ksa_ref[...], ksb_ref[...], -1).astype(BF16)
        z = s_ref[:, QL + KVL + LANE:W] + b_ref[...]
        dff = dlf_ref[...] * (1.0 / (1.0 + jnp.exp(z)))
        ds_ref[:, QL + KVL + LANE:W] = dff.astype(BF16)

        @pl.when(pl.program_id(0) == 0)
        def _():
            gq_ref[...] = jnp.zeros_like(gq_ref)
            gkv_ref[...] = jnp.zeros_like(gkv_ref)
            gb_ref[...] = jnp.zeros_like(gb_ref)

        gq_ref[...] += jnp.sum(gq_rows, axis=0, keepdims=True)
        gkv_ref[...] += jnp.sum(gkv_rows, axis=0, keepdims=True)
        gb_ref[...] += jnp.sum(dff, axis=0, keepdims=True)

    return pl.pallas_call(
        body, name=name, grid=(S // ts,),
        in_specs=[_row_spec(ts, W), _row_spec(ts, QL), _row_spec(ts, KVL), _row_spec(ts, n_heads * LANE),
                  _row_spec(ts, LANE), _full_spec((1, QL)), _full_spec((1, KVL)), _full_spec((1, LANE)),
                  _row_spec(ts, LANE), _row_spec(ts, LANE), _row_spec(ts, LANE)],
        out_specs=[_row_spec(ts, W), _full_spec((1, QL)), _full_spec((1, KVL)), _full_spec((1, LANE))],
        out_shape=[jax.ShapeDtypeStruct((S, W), BF16), jax.ShapeDtypeStruct((1, QL), F32),
                   jax.ShapeDtypeStruct((1, KVL), F32), jax.ShapeDtypeStruct((1, LANE), F32)],
        compiler_params=_params(("arbitrary",)),
    )(small, dcqn, dckvn, dkr_heads, dlogf, q_norm, kv_norm, bias_pad, kc, ksa, ksb)


def _sigmoid(z):
    return 1.0 / (1.0 + jnp.exp(-z))


def _final(h, g, target, name):
    S, D = h.shape
    ts = _tile(S, ROW_T, 8)

    def body(h_ref, g_ref, t_ref, dh_ref, dhb_ref, dg_ref, loss_ref):
        hv = h_ref[...]
        gv = g_ref[...]
        err = (hv * _rms(hv)) * gv - t_ref[...]
        dh, dg_rows = _rms_bwd(hv, err / D, gv)
        dh_ref[...] = dh
        dhb_ref[...] = dh.astype(BF16)

        @pl.when(pl.program_id(0) == 0)
        def _():
            dg_ref[...] = jnp.zeros_like(dg_ref)
            loss_ref[...] = jnp.zeros_like(loss_ref)

        dg_ref[...] += jnp.sum(dg_rows, axis=0, keepdims=True)
        row_loss = jnp.mean(err * err, axis=-1, keepdims=True)
        loss_ref[...] += 0.5 * jnp.sum(row_loss, axis=0, keepdims=True)

    return pl.pallas_call(
        body, name=name, grid=(S // ts,),
        in_specs=[_row_spec(ts, D), _full_spec((1, D)), _row_spec(ts, D)],
        out_specs=[_row_spec(ts, D), _row_spec(ts, D), _full_spec((1, D)), _full_spec((1, LANE))],
        out_shape=[jax.ShapeDtypeStruct((S, D), F32), jax.ShapeDtypeStruct((S, D), BF16),
                   jax.ShapeDtypeStruct((1, D), F32), jax.ShapeDtypeStruct((1, LANE), F32)],
        compiler_params=_params(("arbitrary",)),
    )(h, g, target)


def _suffix_sum_rows(x, name):
    R, S = x.shape
    tb = _tile(S, 512)
    nb = S // tb
    tri = (lax.broadcasted_iota(jnp.int32, (tb, tb), 0) >= lax.broadcasted_iota(jnp.int32, (tb, tb), 1)).astype(BF16)

    def body(x_ref, tri_ref, o_ref, carry_ref):
        @pl.when(pl.program_id(0) == 0)
        def _():
            carry_ref[...] = jnp.zeros_like(carry_ref)

        xv = x_ref[...]
        t = tri_ref[...]
        acc = jnp.broadcast_to(carry_ref[:, 0:1], xv.shape)
        for part in _split3(xv):
            acc = acc + jnp.dot(part, t, preferred_element_type=F32)
        o_ref[...] = acc
        carry_ref[...] = jnp.broadcast_to(acc[:, 0:1], carry_ref.shape)

    rev = pl.BlockSpec((R, tb), lambda i: (0, nb - 1 - i))
    return pl.pallas_call(
        body, name=name, grid=(nb,),
        in_specs=[rev, _full_spec((tb, tb))], out_specs=rev,
        out_shape=jax.ShapeDtypeStruct((R, S), F32),
        scratch_shapes=[pltpu.VMEM((R, LANE), F32)],
        compiler_params=_params(("arbitrary",)),
    )(x, tri)


def _pairs(nb, by_key):
    if by_key:
        pr = [(i, j) for j in range(nb) for i in range(j, nb)]
    else:
        pr = [(i, j) for i in range(nb) for j in range(i + 1)]
    return (jnp.asarray([p[0] for p in pr], jnp.int32), jnp.asarray([p[1] for p in pr], jnp.int32), len(pr))


class _AttT:
    def __init__(self, S, n_heads, q, ks, v, scale, chunk_causal, cum_rep=None, qsub=None):
        self.S, self.H, self.q, self.ks, self.v = S, n_heads, q, ks, v
        self.scale, self.chunk_causal, self.cum_rep = scale, chunk_causal, cum_rep
        self.T = _tile(S, ATT_T)
        self.qs = min(qsub or QSUB, self.T)
        self.nb = S // self.T
        self.dq, self.dv = q[1], v[1]
        self.has_bias = cum_rep is not None

    def q_spec(self, op):
        _, w, off, per_head = op
        return pl.BlockSpec((self.T, w), lambda h, p, it, jt: (it[p], off + (h if per_head else 0)))

    def k_spec(self, op):
        _, w, off, per_head = op
        return pl.BlockSpec((self.T, w), lambda h, p, it, jt: (jt[p], off + (h if per_head else 0)))

    def row_q(self):
        return pl.BlockSpec((None, 1, self.T), lambda h, p, it, jt: (h, 0, it[p]))

    def cum_k(self):
        return pl.BlockSpec((None, self.T, self.qs), lambda h, p, it, jt: (h, jt[p], 0))

    def sub_blocks(self, masked):
        return [(q0, min(self.T, q0 + self.qs) if masked else self.T) for q0 in range(0, self.T, self.qs)]

    def scores(self, k, q_sub, cum, q0, masked):
        s = lax.dot_general(k, q_sub, _NT, preferred_element_type=F32)
        if self.has_bias:
            s = s - cum
        mask = None
        if masked:
            r = lax.broadcasted_iota(jnp.int32, s.shape, 0)
            c = lax.broadcasted_iota(jnp.int32, s.shape, 1) + q0
            mask = (r // CHUNK <= c // CHUNK) if self.chunk_causal else (r <= c)
        return s, mask


def _join(k_refs):
    return k_refs[0][...] if len(k_refs) == 1 else jnp.concatenate([r[...] for r in k_refs], axis=-1)


def _att_fwd_t(att, name, exact=False):
    S, H, T, qs = att.S, att.H, att.T, att.qs
    it, jt, npairs = _pairs(att.nb, by_key=False)
    nk = len(att.ks)

    def body(it_ref, jt_ref, *refs):
        q_ref = refs[0]
        k_refs = refs[1:1 + nk]
        v_ref = refs[1 + nk]
        n = 2 + nk
        cum_ref = None
        if att.has_bias:
            cum_ref = refs[n]
            n += 1
        o_ref = refs[n]
        n += 1
        ox_ref = None
        if exact:
            ox_ref = refs[n]
            n += 1
        lse_ref, m_ref, l_ref, acc_ref = refs[n:n + 4]
        lo_ref = refs[n + 4] if exact else None
        p = pl.program_id(1)
        i, j = it_ref[p], jt_ref[p]

        @pl.when(j == 0)
        def _():
            m_ref[...] = jnp.full_like(m_ref, -jnp.inf)
            l_ref[...] = jnp.zeros_like(l_ref)
            acc_ref[...] = jnp.zeros_like(acc_ref)
            if exact:
                lo_ref[...] = jnp.zeros_like(lo_ref)

        def step(masked):
            k = _join(k_refs)
            v = v_ref[...]
            subs = att.sub_blocks(masked)

            def logits(idx):
                q0, nkeys = subs[idx]
                cum = cum_ref[0:nkeys, :] if att.has_bias else None
                return att.scores(k[0:nkeys], q_ref[q0:q0 + qs, :], cum, q0, masked)

            ahead = logits(0)
            for idx, (q0, nkeys) in enumerate(subs):
                qsl = slice(q0, q0 + qs)
                s, mask = ahead
                if idx + 1 < len(subs):
                    ahead = logits(idx + 1)
                if masked:
                    s = jnp.where(mask, s, -jnp.inf)
                m_prev = m_ref[:, qsl]
                m_new = jnp.maximum(m_prev, jnp.max(s, axis=0, keepdims=True))
                alpha = jnp.exp2(m_prev - m_new)
                pr = jnp.exp2(s - m_new)
                l_ref[:, qsl] = alpha * l_ref[:, qsl] + jnp.sum(pr, axis=0, keepdims=True)
                p_hi = pr.astype(BF16)
                acc_ref[:, qsl] = alpha * acc_ref[:, qsl] + lax.dot_general(
                    v[0:nkeys], p_hi, _TN, preferred_element_type=F32)
                if exact:
                    p_lo = (pr - p_hi.astype(F32)).astype(BF16)
                    lo_ref[:, qsl] = alpha * lo_ref[:, qsl] + lax.dot_general(
                        v[0:nkeys], p_lo, _TN, preferred_element_type=F32)
                m_ref[:, qsl] = m_new

        @pl.when(j < i)
        def _():
            step(False)

        @pl.when(j == i)
        def _():
            step(True)
            l = l_ref[...]
            inv = 1.0 / l
            o_ref[...] = jnp.transpose(acc_ref[...] * inv).astype(o_ref.dtype)
            if exact:
                ox_ref[...] = jnp.transpose((acc_ref[...] + lo_ref[...]) * inv)
            lse_ref[...] = m_ref[...] + jnp.log2(l)

    in_specs = [att.q_spec(att.q)] + [att.k_spec(k) for k in att.ks] + [att.k_spec(att.v)]
    args = [att.q[0]] + [k[0] for k in att.ks] + [att.v[0]]
    if att.has_bias:
        in_specs.append(att.cum_k())
        args.append(att.cum_rep)
    o_spec = pl.BlockSpec((T, att.dv), lambda h, p, it, jt: (it[p], h))
    out_specs = [o_spec]
    out_shape = [jax.ShapeDtypeStruct((S, H * att.dv), BF16)]
    scratch = [pltpu.VMEM((1, T), F32), pltpu.VMEM((1, T), F32), pltpu.VMEM((att.dv, T), F32)]
    if exact:
        out_specs.append(o_spec)
        out_shape.append(jax.ShapeDtypeStruct((S, H * att.dv), F32))
        scratch.append(pltpu.VMEM((att.dv, T), F32))
    out_specs.append(att.row_q())
    out_shape.append(jax.ShapeDtypeStruct((H, 1, S), F32))
    return pl.pallas_call(
        body, name=name,
        grid_spec=pltpu.PrefetchScalarGridSpec(
            num_scalar_prefetch=2, grid=(H, npairs), in_specs=in_specs, out_specs=out_specs,
            scratch_shapes=scratch),
        out_shape=out_shape,
        compiler_params=_params(("parallel", "arbitrary")),
    )(it, jt, *args)


def _att_bwd_t(att, do, lse, o, dq_dtype, dk_dtypes, name, dq_rope=None, order=None):
    S, H, T, qs = att.S, att.H, att.T, att.qs
    it, jt, npairs = _pairs(att.nb, by_key=True)
    nk = len(att.ks)
    last = att.nb - 1
    widths = [k[1] for k in att.ks]

    def body(it_ref, jt_ref, *refs):
        q_ref = refs[0]
        k_refs = refs[1:1 + nk]
        v_ref, do_ref, lse_ref, o_ref = refs[1 + nk:5 + nk]
        n = 5 + nk
        cum_ref = None
        if att.has_bias:
            cum_ref = refs[n]
            n += 1
        rope_refs = None
        if dq_rope is not None:
            rope_refs = refs[n:n + 3]
            n += 3
        if order is not None:
            n += 1
        dl_acc = refs[-1]
        dq_ref = refs[n]
        dk_refs = refs[n + 1:n + 1 + nk]
        dv_ref = refs[n + 1 + nk]
        n += nk + 2
        dc_ref = None
        if att.has_bias:
            dc_ref = refs[n]
            n += 1
        dq_acc, dk_acc, dv_acc = refs[n:n + 3]
        dc_acc = refs[n + 3] if att.has_bias else None
        p = pl.program_id(1)
        i, j = it_ref[p], jt_ref[p]

        @pl.when(p == 0)
        def _():
            dq_acc[...] = jnp.zeros_like(dq_acc)

        @pl.when(i == j)
        def _():
            dk_acc[...] = jnp.zeros_like(dk_acc)
            dv_acc[...] = jnp.zeros_like(dv_acc)
            if att.has_bias:
                dc_acc[...] = jnp.zeros_like(dc_acc)

        @pl.when(j == 0)
        def _():
            prod = do_ref[...].astype(F32) * o_ref[...].astype(F32)
            ones = jnp.ones((8, att.dv), BF16)
            rows = jnp.zeros((8, T), F32)
            for part in _split3(prod):
                rows = rows + lax.dot_general(ones, part, _NT, preferred_element_type=F32)
            dl_acc[i] = rows[0:1, :]

        def step(masked):
            k = _join(k_refs)
            v = v_ref[...]
            dl = dl_acc[i]
            subs = att.sub_blocks(masked)

            def logits(idx):
                q0, nkeys = subs[idx]
                cum = cum_ref[0:nkeys, :] if att.has_bias else None
                return att.scores(k[0:nkeys], q_ref[q0:q0 + qs, :], cum, q0, masked)

            ahead = logits(0)
            for idx, (q0, nkeys) in enumerate(subs):
                qsl = slice(q0, q0 + qs)
                ksl = slice(0, nkeys)
                q_sub = q_ref[qsl, :]
                do_sub = do_ref[qsl, :]
                s, mask = ahead
                if idx + 1 < len(subs):
                    ahead = logits(idx + 1)
                pr = jnp.exp2(s - lse_ref[:, qsl])
                if masked:
                    pr = jnp.where(mask, pr, 0.0)
                dp = lax.dot_general(v[ksl], do_sub, _NT, preferred_element_type=F32)
                ds = pr * (dp - dl[:, qsl])
                ds_b = ds.astype(BF16)
                dv_acc[ksl, :] += jnp.dot(pr.astype(BF16), do_sub, preferred_element_type=F32)
                dk_acc[ksl, :] += jnp.dot(ds_b, q_sub, preferred_element_type=F32)
                dq_acc[i, :, qsl] += lax.dot_general(k[ksl], ds_b, _TN, preferred_element_type=F32)
                if att.has_bias:
                    part = ds[:, 0:LANE] if qs >= LANE else ds
                    for c0 in range(LANE, qs, LANE):
                        part = part + ds[:, c0:c0 + LANE]
                    dc_acc[ksl, :] += part

        @pl.when(i > j)
        def _():
            step(False)

        @pl.when(i == j)
        def _():
            step(True)
            dq = jnp.transpose(dq_acc[i] * att.scale)
            if dq_rope is not None:
                dq = _rope(dq, rope_refs[0][...], rope_refs[1][...], rope_refs[2][...], -1)
            dq_ref[...] = dq.astype(dq_ref.dtype)

        @pl.when(i == last)
        def _():
            dk = dk_acc[...] * (1.0 / LOG2E)
            off = 0
            for r, w in zip(dk_refs, widths):
                r[...] = dk[:, off:off + w].astype(r.dtype)
                off += w
            dv_ref[...] = dv_acc[...].astype(dv_ref.dtype)
            if att.has_bias:
                dc_ref[...] = -jnp.sum(dc_acc[...], axis=-1, keepdims=True)

    do_op = (do, att.dv, 0, True)
    o_spec = pl.BlockSpec((T, att.dv), lambda h, p, it, jt: (jnp.where(jt[p] == 0, it[p], last), h))
    in_specs = ([att.q_spec(att.q)] + [att.k_spec(k) for k in att.ks]
                + [att.k_spec(att.v), att.q_spec(do_op), att.row_q(), o_spec])
    args = [att.q[0]] + [k[0] for k in att.ks] + [att.v[0], do, lse, o]
    if att.has_bias:
        in_specs.append(att.cum_k())
        args.append(att.cum_rep)
    if dq_rope is not None:
        in_specs += [pl.BlockSpec((T, att.dq), lambda h, p, it, jt: (jt[p], 0))] * 3
        args += list(dq_rope)
    if order is not None:
        in_specs.append(_ANY_SPEC)
        args.append(order)
    out_specs = [pl.BlockSpec((T, att.dq), lambda h, p, it, jt: (jt[p], h))]
    out_shape = [jax.ShapeDtypeStruct((S, H * att.dq), dq_dtype)]
    out_specs += [pl.BlockSpec((T, w), lambda h, p, it, jt: (jt[p], h)) for w in widths]
    out_shape += [jax.ShapeDtypeStruct((S, H * w), dt) for w, dt in zip(widths, dk_dtypes)]
    out_specs.append(pl.BlockSpec((T, att.dv), lambda h, p, it, jt: (jt[p], h)))
    out_shape.append(jax.ShapeDtypeStruct((S, H * att.dv), BF16))
    scratch = [pltpu.VMEM((att.nb, att.dq, T), F32), pltpu.VMEM((T, att.dq), F32), pltpu.VMEM((T, att.dv), F32)]
    if att.has_bias:
        out_specs.append(pl.BlockSpec((None, T, 1), lambda h, p, it, jt: (h, jt[p], 0)))
        out_shape.append(jax.ShapeDtypeStruct((H, S, 1), F32))
        scratch.append(pltpu.VMEM((T, min(qs, LANE)), F32))
    scratch.append(pltpu.VMEM((att.nb, 1, T), F32))
    return pl.pallas_call(
        body, name=name,
        grid_spec=pltpu.PrefetchScalarGridSpec(
            num_scalar_prefetch=2, grid=(H, npairs), in_specs=in_specs, out_specs=out_specs,
            scratch_shapes=scratch),
        out_shape=out_shape,
        compiler_params=_params(("parallel", "arbitrary")),
    )(it, jt, *args)


def _adamw(w, g1, g2, m, v, name, g_row=None):
    _, K, N = w.shape
    by_rows = K % 8 == 0
    tr = _tile(K, 256, 8) if by_rows else K
    if g_row is None:
        assert g1.shape == (K, N) and g2.shape == (K, N), name
        g_row = 0
    assert by_rows and g_row % tr == 0 or g_row == 0, name
    g_blk = g_row // tr
    tc = N if by_rows else _tile(N, LANE)
    c1 = 1.0 - ADAM_B1 ** ADAM_STEP
    c2 = 1.0 - ADAM_B2 ** ADAM_STEP

    def body(w_ref, g1_ref, g2_ref, m_ref, v_ref, g_ref, d_ref, nm_ref, nv_ref):
        gv = g1_ref[...] + g2_ref[...]
        nm = ADAM_B1 * m_ref[...] + (1.0 - ADAM_B1) * gv
        nv = ADAM_B2 * v_ref[...] + (1.0 - ADAM_B2) * (gv * gv)
        g_ref[...] = gv
        d_ref[...] = -ADAM_LR * ((nm / c1) / (jnp.sqrt(nv / c2) + ADAM_EPS) + ADAM_WD * w_ref[...])
        nm_ref[...] = nm
        nv_ref[...] = nv

    if by_rows:
        blk = pl.BlockSpec((None, tr, N), lambda i: (0, i, 0))
        gblk = pl.BlockSpec((tr, N), lambda i: (g_blk + i, 0))
    else:
        blk = pl.BlockSpec((None, K, tc), lambda i: (0, 0, i))
        gblk = pl.BlockSpec((K, tc), lambda i: (0, i))
    return pl.pallas_call(
        body, name=name, grid=(K // tr if by_rows else N // tc,),
        in_specs=[blk, gblk, gblk, blk, blk], out_specs=[blk] * 4,
        out_shape=[jax.ShapeDtypeStruct((1, K, N), F32)] * 4,
        compiler_params=_params(("parallel",)),
    )(w, g1, g2, m, v)


_HBM_SPEC = pl.BlockSpec(memory_space=pltpu.HBM)
_SEM_SPEC = pl.BlockSpec(memory_space=pltpu.SEMAPHORE)
_VMEM_SPEC = pl.BlockSpec(memory_space=pltpu.VMEM)
_EFFECT = pltpu.SideEffectType.DATAFLOW_SIDE_EFFECTING


def _place():
    return lax.axis_index("x"), lax.axis_index("y"), lax.axis_index("c")


def _other_chips(x, y):
    return [(1 - x, y), (x, 1 - y), (1 - x, 1 - y)]


def _chip_copies(src_ref, land_ref, sems, gather):
    x, y, c = _place()
    me = 2 * x + y
    out, back = [], []
    if gather == "half":
        half = src_ref.shape[0] // 2
        mine = pl.ds(pl.multiple_of(c * half, 16), half)
    for n, (px, py) in enumerate(_other_chips(x, y)):
        if gather == "half":
            src, there, here = src_ref.at[mine], land_ref.at[me, mine], land_ref.at[2 * px + py, mine]
        elif gather:
            src, there, here = src_ref, land_ref.at[me], land_ref.at[2 * px + py]
        else:
            src, there, here = src_ref.at[2 * px + py], land_ref.at[n], land_ref.at[n]
        out.append(pltpu.make_async_remote_copy(
            src_ref=src, dst_ref=there, send_sem=sems[n], recv_sem=sems[3 + n],
            device_id=(px, py, c), device_id_type=MESH))
        back.append(pltpu.make_async_remote_copy(
            src_ref=src, dst_ref=here, send_sem=sems[n], recv_sem=sems[3 + n],
            device_id=(px, py, c), device_id_type=MESH))
    return out, back


def _xchg_start(src, land, gather, order, name):
    def body(src_ref, land_ref, order_ref, *outs):
        sems = outs[0:6]
        token = outs[8]
        out, _ = _chip_copies(src_ref, land_ref, sems, gather)
        for cp in out:
            cp.start()
        token[...] = jnp.zeros_like(token)

    outs = pl.pallas_call(
        body, name=name,
        out_shape=(pltpu.SemaphoreType.DMA(()),) * 6 + (
            pltpu.HBM(src.shape, src.dtype), pltpu.HBM(land.shape, land.dtype),
            jax.ShapeDtypeStruct((8, LANE), F32)),
        in_specs=(_HBM_SPEC, _HBM_SPEC, _ANY_SPEC),
        out_specs=(_SEM_SPEC,) * 6 + (_HBM_SPEC, _HBM_SPEC, _VMEM_SPEC),
        input_output_aliases={0: 6, 1: 7},
        compiler_params=pltpu.CompilerParams(has_side_effects=_EFFECT),
    )(pltpu.with_memory_space_constraint(src, pltpu.HBM), pltpu.with_memory_space_constraint(land, pltpu.HBM), order)
    return outs[0:6], outs[6], outs[7], outs[8]


def _xchg_wait(started, gather, after, name):
    sems, src, land, _ = started
    after = after if isinstance(after, tuple) else (after,)

    def body(src_ref, land_ref, *rest):
        _, back = _chip_copies(src_ref, land_ref, rest[0:6], gather)
        for cp in back:
            cp.wait_send()
            cp.wait_recv()

    return pl.pallas_call(
        body, name=name,
        out_shape=(pltpu.HBM(src.shape, src.dtype), pltpu.HBM(land.shape, land.dtype)),
        in_specs=(_HBM_SPEC, _HBM_SPEC) + (_SEM_SPEC,) * 6 + (_ANY_SPEC,) * len(after),
        out_specs=(_HBM_SPEC, _HBM_SPEC),
        input_output_aliases={0: 0, 1: 1},
        compiler_params=pltpu.CompilerParams(has_side_effects=_EFFECT),
    )(src, land, *sems, *after)


def _forward_halves(land, name):
    _, R, C = land.shape
    half = R // 2
    assert half % 16 == 0

    def body(land_ref, out_ref, send_sems, recv_sems):
        x, y, c = _place()
        mine = pl.ds(pl.multiple_of(c * half, 16), half)
        theirs = pl.ds(pl.multiple_of((1 - c) * half, 16), half)
        sends = []
        for n, (px, py) in enumerate(_other_chips(x, y)):
            cp = pltpu.make_async_remote_copy(
                src_ref=land_ref.at[2 * px + py, mine], dst_ref=out_ref.at[2 * px + py, mine],
                send_sem=send_sems.at[n], recv_sem=recv_sems.at[n], device_id=(x, y, 1 - c), device_id_type=MESH)
            cp.start()
            sends.append(cp)
        for n, (px, py) in enumerate(_other_chips(x, y)):
            pltpu.make_async_remote_copy(
                src_ref=land_ref.at[2 * px + py, theirs], dst_ref=out_ref.at[2 * px + py, theirs],
                send_sem=send_sems.at[n], recv_sem=recv_sems.at[n], device_id=(x, y, 1 - c),
                device_id_type=MESH).wait_recv()
        for cp in sends:
            cp.wait_send()

    return pl.pallas_call(
        body, name=name,
        in_specs=[_ANY_SPEC], out_specs=_ANY_SPEC,
        out_shape=jax.ShapeDtypeStruct(land.shape, land.dtype),
        input_output_aliases={0: 0},
        scratch_shapes=[pltpu.SemaphoreType.DMA((3,)), pltpu.SemaphoreType.DMA((3,))],
    )(land)


def _sib_copy(src_ref, land_ref, send_sem, recv_sem):
    x, y, c = _place()
    return pltpu.make_async_remote_copy(src_ref=src_ref, dst_ref=land_ref, send_sem=send_sem, recv_sem=recv_sem,
                                        device_id=(x, y, 1 - c), device_id_type=MESH)


def _sib_start(src, name):
    land = lax.empty(src.shape, src.dtype)

    def body(src_ref, land_ref, send_sem, recv_sem, src_thru, land_thru, token):
        _sib_copy(src_ref, land_ref, send_sem, recv_sem).start()
        token[...] = jnp.zeros_like(token)

    return pl.pallas_call(
        body, name=name,
        out_shape=(pltpu.SemaphoreType.DMA(()), pltpu.SemaphoreType.DMA(()),
                   pltpu.HBM(src.shape, src.dtype), pltpu.HBM(land.shape, land.dtype),
                   jax.ShapeDtypeStruct((8, LANE), F32)),
        in_specs=(_HBM_SPEC, _HBM_SPEC),
        out_specs=(_SEM_SPEC, _SEM_SPEC, _HBM_SPEC, _HBM_SPEC, _VMEM_SPEC),
        input_output_aliases={0: 2, 1: 3},
        compiler_params=pltpu.CompilerParams(has_side_effects=_EFFECT),
    )(pltpu.with_memory_space_constraint(src, pltpu.HBM), pltpu.with_memory_space_constraint(land, pltpu.HBM))


def _sib_wait(started, after, name):
    send_sem, recv_sem, src, land, _ = started

    def body(src_ref, land_ref, send_sem, recv_sem, after_ref, src_out, land_out):
        cp = _sib_copy(src_ref, land_ref, send_sem, recv_sem)
        cp.wait_send()
        cp.wait_recv()

    return pl.pallas_call(
        body, name=name,
        out_shape=(pltpu.HBM(src.shape, src.dtype), pltpu.HBM(land.shape, land.dtype)),
        in_specs=(_HBM_SPEC, _HBM_SPEC, _SEM_SPEC, _SEM_SPEC, _ANY_SPEC),
        out_specs=(_HBM_SPEC, _HBM_SPEC),
        input_output_aliases={0: 0, 1: 1},
        compiler_params=pltpu.CompilerParams(has_side_effects=_EFFECT),
    )(src, land, send_sem, recv_sem, after)


def _sum_slabs(gp, recv, chip, name):
    _, R, C = gp.shape
    tr = _tile(R, PACK_ROWS, 16)

    def body(chip_ref, own_ref, r0_ref, r1_ref, r2_ref, o_ref):
        acc = own_ref[...].astype(F32) + r0_ref[...].astype(F32)
        o_ref[...] = (acc + r1_ref[...].astype(F32)) + r2_ref[...].astype(F32)

    def got(n):
        return pl.BlockSpec((None, tr, C), lambda i, chip_ref: (n, i, 0))

    return pl.pallas_call(
        body, name=name,
        grid_spec=pltpu.PrefetchScalarGridSpec(
            num_scalar_prefetch=1, grid=(R // tr,),
            in_specs=[pl.BlockSpec((None, tr, C), lambda i, chip_ref: (chip_ref[0], i, 0)), got(0), got(1), got(2)],
            out_specs=pl.BlockSpec((tr, C), lambda i, chip_ref: (i, 0))),
        out_shape=jax.ShapeDtypeStruct((R, C), F32),
        compiler_params=_params(("parallel",)),
    )(jnp.reshape(chip, (1,)).astype(jnp.int32), gp, recv, recv, recv)


def _all_reduce_vec(vec, name):
    VR, W = vec.shape

    def body(vec_ref, vall_ref, vout_ref, vsend_sems, vrecv_sems):
        x, y, c = _place()
        vall_ref[4 * x + 2 * y + c] = vec_ref[...]
        sends = []
        peers = []
        for r in range(1, N_DEV):
            dx, dy, dc = (r >> 2) & 1, (r >> 1) & 1, r & 1
            peer = (x ^ dx, y ^ dy, c ^ dc)
            peers.append(peer)
            cp = pltpu.make_async_remote_copy(
                src_ref=vec_ref, dst_ref=vall_ref.at[4 * x + 2 * y + c], send_sem=vsend_sems.at[r - 1],
                recv_sem=vrecv_sems.at[r - 1], device_id=peer, device_id_type=MESH)
            cp.start()
            sends.append(cp)
        for r, peer in enumerate(peers):
            pltpu.make_async_remote_copy(
                src_ref=vec_ref, dst_ref=vall_ref.at[4 * peer[0] + 2 * peer[1] + peer[2]],
                send_sem=vsend_sems.at[r], recv_sem=vrecv_sems.at[r],
                device_id=peer, device_id_type=MESH).wait_recv()
        total = vall_ref[0]
        for d in range(1, N_DEV):
            total = total + vall_ref[d]
        vout_ref[...] = total
        for cp in sends:
            cp.wait_send()

    outs = pl.pallas_call(
        body, name=name,
        in_specs=[_VMEM_SPEC], out_specs=[_VMEM_SPEC, _VMEM_SPEC],
        out_shape=[jax.ShapeDtypeStruct((N_DEV, VR, W), F32), jax.ShapeDtypeStruct((VR, W), F32)],
        scratch_shapes=[pltpu.SemaphoreType.DMA((N_DEV - 1,)), pltpu.SemaphoreType.DMA((N_DEV - 1,))],
    )(vec)
    return outs[1]


class _Pack:
    def __init__(self, group, C):
        self.group, self.C = group, C
        self.rows, self.offs, off = {}, {}, 0
        for nm, (K, N), _ in group:
            assert N <= C, nm
            self.rows[nm] = K if 2 * N > C else -(-(K * N) // C)
            self.offs[nm] = off
            off += -(-self.rows[nm] // 16) * 16
        self.used = off
        self.R = -(-off // PACK_ROWS) * PACK_ROWS

    def _rows_of(self, a):
        K, N = a.shape
        if 2 * N > self.C:
            a = jnp.pad(a, ((0, 0), (0, self.C - N)))
        else:
            a = jnp.pad(a.reshape(-1), (0, -(K * N) % self.C)).reshape(-1, self.C)
        return jnp.pad(a, ((0, -a.shape[0] % 16), (0, 0)))

    def pack(self, shards):
        parts = [self._rows_of(shards[nm].astype(BF16)) for nm, _, _ in self.group]
        return jnp.concatenate(parts + [jnp.zeros((self.R - self.used, self.C), BF16)], axis=0)

    def _shard_of(self, rows, shape):
        K, N = shape
        return rows[:, :N] if 2 * N > self.C else rows.reshape(-1)[:K * N].reshape(K, N)

    def part(self, flat, nm, shape):
        return self._shard_of(flat[self.offs[nm]:self.offs[nm] + self.rows[nm]], shape)

    def slab_rows(self, nm, g):
        (K, N), axis = next((shape, axis) for n, shape, axis in self.group if n == nm)
        cuts = [g[:, k * N:(k + 1) * N] if axis == 1 else g[k * K:(k + 1) * K, :] for k in range(N_CHIPS)]
        return jnp.stack([self._rows_of(c.astype(BF16)) for c in cuts])

    def slabs(self, grads):
        parts = [self.slab_rows(nm, grads[nm]) for nm, _, _ in self.group]
        return jnp.concatenate(parts + [jnp.zeros((N_CHIPS, self.R - self.used, self.C), BF16)], axis=1)

    def full(self, gathered, names=None):
        res = {}
        for nm, (K, N), axis in self.group:
            if names is None or nm in names:
                rows = gathered[:, self.offs[nm]:self.offs[nm] + self.rows[nm]]
                res[nm] = jnp.concatenate([self._shard_of(rows[k], (K, N)) for k in range(N_CHIPS)], axis=axis)
        return res


def _rope_tables(S):
    pos = jnp.arange(S, dtype=F32)
    inv = 1.0 / (ROPE_THETA ** (jnp.arange(0, MLA_ROPE, 2, dtype=F32) / MLA_ROPE))
    ang = pos[:, None] * inv[None, :]
    cos, sin = jnp.cos(ang), jnp.sin(ang)
    half = MLA_ROPE // 2
    z = jnp.zeros((S, half), F32)
    one = jnp.ones((S, LANE - MLA_ROPE), F32)
    zero = jnp.zeros((S, LANE - MLA_ROPE), F32)
    kc = jnp.concatenate([cos, cos, one], axis=1)
    ksa = jnp.concatenate([-sin, z, zero], axis=1)
    ksb = jnp.concatenate([z, sin, zero], axis=1)
    qc = jnp.concatenate([jnp.ones((S, MLA_NOPE), F32), kc], axis=1)
    qsa = jnp.concatenate([jnp.zeros((S, MLA_NOPE), F32), ksa], axis=1)
    qsb = jnp.concatenate([jnp.zeros((S, MLA_NOPE), F32), ksb], axis=1)
    return (kc, ksa, ksb), (qc, qsa, qsb)


def _pad_cols(a, width):
    return jnp.pad(a, ((0, 0), (0, width - a.shape[1])))


def kernel(x, attn_norm, w_in, fox_f_bias, q_norm, w_uq, kv_norm, w_ukv, w_mla_branch, w_fox_branch, w_out, mlp_norm, w_up, w_down, final_norm, loss_target, m_attn_norm, m_w_in, m_fox_f_bias, m_q_norm, m_w_uq, m_kv_norm, m_w_ukv, m_w_mla_branch, m_w_fox_branch, m_w_out, m_mlp_norm, m_w_up, m_w_down, m_final_norm, v_attn_norm, v_w_in, v_fox_f_bias, v_q_norm, v_w_uq, v_kv_norm, v_w_ukv, v_w_mla_branch, v_w_fox_branch, v_w_out, v_mlp_norm, v_w_up, v_w_down, v_final_norm):
    _, S, D = x.shape
    H, HF = MLA_HEADS, FOX_HEADS
    QL, KVL = MLA_Q_LORA, MLA_KV_LORA
    assert H == HF and H <= 8
    xs = x[0]
    target = loss_target[0]
    C = D
    chip = 2 * lax.axis_index("x") + lax.axis_index("y")

    def flip(a):
        return jnp.transpose(a, (0, 2, 1))

    w_in, m_w_in, v_w_in = flip(w_in), flip(m_w_in), flip(v_w_in)
    weights = {"attn_norm": attn_norm, "w_in": w_in, "fox_f_bias": fox_f_bias, "q_norm": q_norm, "w_uq": w_uq,
               "kv_norm": kv_norm, "w_ukv": w_ukv, "w_mla_branch": w_mla_branch, "w_fox_branch": w_fox_branch,
               "w_out": w_out, "mlp_norm": mlp_norm, "w_up": w_up, "w_down": w_down, "final_norm": final_norm}
    moments = {"attn_norm": (m_attn_norm, v_attn_norm), "w_in": (m_w_in, v_w_in), "fox_f_bias": (m_fox_f_bias, v_fox_f_bias),
               "q_norm": (m_q_norm, v_q_norm), "w_uq": (m_w_uq, v_w_uq), "kv_norm": (m_kv_norm, v_kv_norm),
               "w_ukv": (m_w_ukv, v_w_ukv), "w_mla_branch": (m_w_mla_branch, v_w_mla_branch),
               "w_fox_branch": (m_w_fox_branch, v_w_fox_branch), "w_out": (m_w_out, v_w_out),
               "mlp_norm": (m_mlp_norm, v_mlp_norm), "w_up": (m_w_up, v_w_up), "w_down": (m_w_down, v_w_down),
               "final_norm": (m_final_norm, v_final_norm)}

    def group(names_axes):
        return [(nm, weights[nm].shape[1:], axis) for nm, axis in names_axes]

    pack_a = _Pack(group([("w_in", 0), ("w_uq", 1), ("w_ukv", 1)]), C)
    pack_b = _Pack(group([("w_down", 0), ("w_up", 1), ("w_out", 0), ("w_mla_branch", 1), ("w_fox_branch", 1)]), C)
    RA, RB = pack_a.R, pack_b.R
    wp_b = pack_b.pack({nm: weights[nm][0] for nm, _, _ in pack_b.group})
    n_in = w_in.shape[1]
    rows_in = -(-n_in // 16) * 16
    assert pack_a.offs["w_in"] == 0 and w_in.shape[2] == C
    assert all((k * n_in) % 16 + n_in <= rows_in for k in range(N_CHIPS))
    shifted = lax.dynamic_update_slice(jnp.zeros((rows_in, C), BF16), w_in[0].astype(BF16), ((chip * n_in) % 16, 0))
    wp_a = jnp.concatenate([shifted] + [pack_a._rows_of(weights[nm][0].astype(BF16)) for nm, _, _ in pack_a.group[1:]]
                           + [jnp.zeros((RA - pack_a.used, C), BF16)], axis=0)
    ag_a = _xchg_start(wp_a, lax.empty((N_CHIPS, RA, C), BF16), "half", jnp.zeros((8, LANE), F32), "all_gather_start_a")
    xn = _norm_fwd(xs, attn_norm, "attn_norm_fwd", order=ag_a[3])
    own_a, land_a = _xchg_wait(ag_a, "half", (xn, wp_b), "all_gather_wait_a")
    land_a = _forward_halves(land_a, "all_gather_forward_a")
    gathered_a = lax.dynamic_update_slice(land_a, own_a[None], (chip, 0, 0))
    ag_b = _xchg_start(wp_b, lax.empty((N_CHIPS, RB, C), BF16), True, gathered_a, "all_gather_start_b")
    full = pack_a.full(gathered_a, ("w_uq", "w_ukv"))
    tile0 = [(k * n_in) // 16 * 16 for k in range(N_CHIPS)]
    total = tile0[-1] + rows_in
    full["w_in"] = sum(jnp.pad(gathered_a[k, :rows_in], ((tile0[k], total - tile0[k] - rows_in), (0, 0)))
                       for k in range(N_CHIPS))

    o_ckv = QL
    o_kr = o_ckv + KVL
    o_fq = o_kr + MLA_ROPE
    o_ff = o_fq + 3 * HF * FOX_HEAD_DIM
    o_g = o_ff + HF
    wi = full["w_in"]
    assert N_CHIPS * n_in == o_g + 2 * D and wi.shape[0] >= o_g + 2 * D
    WS = QL + KVL + 2 * LANE
    NQKV = 3 * HF * FOX_HEAD_DIM

    def pad_rows(a, rows):
        return jnp.pad(a, ((0, rows - a.shape[0]), (0, 0)))

    w_small = jnp.concatenate([wi[:o_kr], pad_rows(wi[o_kr:o_fq], LANE), pad_rows(wi[o_ff:o_g], LANE)], axis=0)
    w_qkv = wi[o_fq:o_ff]
    w_g = wi[o_g:o_g + 2 * D]
    w_pack = jnp.concatenate([w_small, w_qkv, w_g], axis=0)
    dqk = MLA_NOPE + MLA_ROPE
    w_uq_p = jnp.pad(full["w_uq"].reshape(QL, H, dqk), ((0, 0), (0, 0), (0, QPAD - dqk))).reshape(QL, H * QPAD)
    ukv = full["w_ukv"].reshape(KVL, H, MLA_NOPE + MLA_V)
    w_ukv_p = jnp.concatenate([ukv[:, :, :MLA_NOPE].reshape(KVL, H * MLA_NOPE),
                               ukv[:, :, MLA_NOPE:].reshape(KVL, H * MLA_V)], axis=1)

    (kc, ksa, ksb), (qc, qsa, qsb) = _rope_tables(S)
    bias_pad = _pad_cols(fox_f_bias, LANE)

    small = _matmul(xn, w_small, "nt", [F32], "proj_small")
    n_fq = HF * FOX_HEAD_DIM
    q_scale = jnp.concatenate([jnp.full((1, n_fq), LOG2E / math.sqrt(FOX_HEAD_DIM), F32),
                               jnp.ones((1, NQKV - n_fq), F32)], axis=1)
    qkv = _matmul(xn, w_qkv, "nt", [BF16], "proj_qkv", col_extras=(q_scale,), epilogue=lambda acc, cs: (acc * cs,))
    gpre = _matmul(xn, w_g, "nt", [BF16], "proj_gates")
    cqn, ckvn, kr, cum = _prep_fwd(small, q_norm, kv_norm, bias_pad, kc, ksa, ksb, HF, "prep_fwd")
    c2_mla = LOG2E / math.sqrt(dqk)
    q_rot = _matmul(cqn, w_uq_p, "nn", [BF16], "mla_q_up", tn=QPAD, row_extras=(qc * c2_mla, qsa * c2_mla, qsb * c2_mla),
                    epilogue=lambda acc, c, sa, sb: (_rope(acc, c, sa, sb, 1),))
    kv2 = _matmul(ckvn, w_ukv_p, "nn", [BF16], "mla_kv_up")

    def mla_att(qsub):
        return _AttT(S, H, (q_rot, QPAD, 0, True), [(kv2, MLA_NOPE, 0, True), (kr, LANE, 0, False)],
                     (kv2, MLA_V, H, True), 1.0 / math.sqrt(dqk), True, qsub=qsub)

    mla = mla_att(QSUB)
    o_mla, lse_mla = _att_fwd_t(mla_att(2 * QSUB), "mla_att_fwd")

    cum_t = jnp.transpose(cum[:, :HF]) * LOG2E
    cum_rep = jnp.broadcast_to(cum_t[:, :, None], (HF, S, min(QSUB, _tile(S, ATT_T))))
    fox = _AttT(S, HF, (qkv, FOX_HEAD_DIM, 0, True), [(qkv, FOX_HEAD_DIM, HF, True)],
                (qkv, FOX_HEAD_DIM, 2 * HF, True), 1.0 / math.sqrt(FOX_HEAD_DIM), False, cum_rep)
    o_fox, ox_fox, lse_fox = _att_fwd_t(fox, "fox_att_fwd", exact=True)

    own_b, land_b = _xchg_wait(ag_b, True, (lse_fox, lse_mla, gpre), "all_gather_wait_b")
    gathered_b = lax.dynamic_update_slice(land_b, own_b[None], (chip, 0, 0))
    full.update(pack_b.full(gathered_b, ("w_mla_branch", "w_fox_branch", "w_out")))
    w_mb, w_fb, w_o = (full[n] for n in ("w_mla_branch", "w_fox_branch", "w_out"))

    def b_of(nm, mode, tn, tk):
        (K, N), axis = next((shape, axis) for n, shape, axis in pack_b.group if n == nm)
        off = pack_b.offs[nm]
        shape = (N_CHIPS * K, N) if axis == 0 else (K, N_CHIPS * N)
        t_r, t_c = (tk, tn) if mode == "nn" else (tn, tk)
        t_r, t_c = _tile(shape[0], t_r), _tile(shape[1], t_c)
        if not (N == C and K % t_r == 0 and N % t_c == 0 and off % t_r == 0):
            return pack_b.full(gathered_b, (nm,))[nm], None
        base = off // t_r
        if axis == 0:
            per = K // t_r
            place = lambda rb, cb: (rb // per, base + rb % per, cb)
        else:
            per = N // t_c
            place = lambda rb, cb: (cb // per, base + rb, cb % per)
        return gathered_b, (shape, (lambda j, k: place(k, j)) if mode == "nn" else (lambda j, k: place(j, k)))

    y_mla = _matmul(o_mla, w_mb, "nn", [BF16], "mla_branch")

    def gate_merge(acc, ga, gb, ya):
        return acc, _sigmoid(ga.astype(F32)) * ya.astype(F32) + _sigmoid(gb.astype(F32)) * acc

    y_fox, merged = _matmul(o_fox, w_fb, "nn", [BF16, BF16], "fox_branch_gates", tn=512,
                            extras=((gpre, 0), (gpre, 1), y_mla), epilogue=gate_merge)
    h1 = _matmul(merged, w_o, "nn", [F32], "out_proj", extras=(xs,), epilogue=lambda acc, r: (acc + r,))
    hn = _norm_fwd(h1, mlp_norm, "mlp_norm_fwd")

    def relu2(acc):
        a = jnp.maximum(acc, 0.0)
        return a * a, a

    w_u, w_u_in = b_of("w_up", "nn", 1024, 2048)
    u, a_pos = _matmul(hn, w_u, "nn", [BF16, BF16], "mlp_up", epilogue=relu2, b_in=w_u_in)
    w_d, w_d_in = b_of("w_down", "nn", 1024, 2048)
    h2 = _matmul(u, w_d, "nn", [F32], "mlp_down", tn=1024, extras=(h1,), epilogue=lambda acc, r: (acc + r,),
                 b_in=w_d_in)
    dh2, dh2_b, g_final, loss_part = _final(h2, final_norm.reshape(1, D), target, "final_norm_loss")

    gp_b = lax.empty((N_CHIPS, RB, C), BF16)
    by_glue = {}

    def grad_b(nm, a, b, name):
        nonlocal gp_b
        (K, N), axis = next((shape, axis) for n, shape, axis in pack_b.group if n == nm)
        off = pack_b.offs[nm]
        tm = min(1024, K) if axis == 0 else min(1024, a.shape[1])
        tn = min(1024, N) if axis == 1 else min(1024, b.shape[1])
        if not (N == C and tm % LANE == 0 and tn % LANE == 0 and K % tm == 0 and N % tn == 0 and off % tm == 0):
            by_glue[nm] = _mm_tn(a, b, name)
            return
        base = off // tm
        if axis == 0:
            per = K // tm
            place = lambda i, j: (i // per, base + i % per, j)
        else:
            per = N // tn
            place = lambda i, j: (j // per, base + i, j % per)
        gp_b = _mm_tn(a, b, name, tm=tm, tn=tn, into=(gp_b, place))

    w_d, w_d_in = b_of("w_down", "nt", 1024, 2048)
    da = _matmul(dh2_b, w_d, "nt", [BF16], "mlp_down_dx", extras=(a_pos,),
                 epilogue=lambda acc, a: (acc * (2.0 * a.astype(F32)),), b_in=w_d_in)
    grad_b("w_down", u, dh2_b, "mlp_down_dw")
    w_u, w_u_in = b_of("w_up", "nt", 1024, 2048)
    dhn = _matmul(da, w_u, "nt", [F32], "mlp_up_dx", tn=1024, b_in=w_u_in)
    grad_b("w_up", hn, da, "mlp_up_dw")
    dh1, dh1_b, g_mlp_norm = _norm_bwd(h1, dhn, mlp_norm, dh2, "mlp_norm_bwd")

    def gate_bwd(acc, ga, gb, ya, yb):
        ga, gb = _sigmoid(ga.astype(F32)), _sigmoid(gb.astype(F32))
        ya, yb = ya.astype(F32), yb.astype(F32)
        return acc * ga, acc * gb, acc * ya * (ga * (1.0 - ga)), acc * yb * (gb * (1.0 - gb))

    dy_mla, dy_fox, dg_mla, dg_fox = _matmul(dh1_b, w_o, "nt", [BF16] * 4, "out_proj_dx_gates", tn=512,
                                             extras=((gpre, 0), (gpre, 1), y_mla, y_fox), epilogue=gate_bwd)
    grad_b("w_out", merged, dh1_b, "out_proj_dw")
    do_mla = _matmul(dy_mla, w_mb, "nt", [BF16], "mla_branch_dx")
    grad_b("w_mla_branch", o_mla, dy_mla, "mla_branch_dw")
    do_fox = _matmul(dy_fox, w_fb, "nt", [BF16], "fox_branch_dx")
    grad_b("w_fox_branch", o_fox, dy_fox, "fox_branch_dw")
    for nm, g in by_glue.items():
        gp_b = lax.dynamic_update_slice(gp_b, pack_b.slab_rows(nm, g), (0, pack_b.offs[nm], 0))
    if RB > pack_b.used:
        gp_b = lax.dynamic_update_slice(gp_b, jnp.zeros((N_CHIPS, RB - pack_b.used, C), BF16), (0, pack_b.used, 0))

    rs_b = _xchg_start(gp_b, lax.empty((3, RB, C), BF16), False, do_fox, "grad_scatter_start_b")

    dq_rot, dk_nope, dkr_heads, dv_mla = _att_bwd_t(mla, do_mla, lse_mla, o_mla, BF16, [BF16, F32],
                                                    "mla_att_bwd", dq_rope=(qc, qsa, qsb), order=rs_b[3])
    dfq, dfk, dfv, dcum = _att_bwd_t(fox, do_fox, lse_fox, ox_fox, BF16, [BF16], "fox_att_bwd")

    gp_b_sent, recv_b = _xchg_wait(rs_b, False, (dfq, dq_rot), "grad_scatter_wait_b")
    swap_b = _sib_start(_sum_slabs(gp_b_sent, recv_b, chip, "grad_sum_b"), "grad_swap_start_b")

    dcqn = _matmul(dq_rot, w_uq_p, "nt", [F32], "mla_q_up_dx", order=swap_b[4])
    g_w_uq_p = _mm_tn(cqn, dq_rot, "mla_q_up_dw")
    dkv2 = jnp.concatenate([dk_nope, dv_mla], axis=1)
    dckvn = _matmul(dkv2, w_ukv_p, "nt", [F32], "mla_kv_up_dx")
    g_w_ukv_p = _mm_tn(ckvn, dkv2, "mla_kv_up_dw")

    dcum_rows = jnp.pad(dcum[:, :, 0], ((0, 8 - HF), (0, 0)))
    dlogf_rows = _suffix_sum_rows(dcum_rows, "fox_forget_suffix_sum")
    dlogf = _pad_cols(jnp.transpose(dlogf_rows[:HF]), LANE)
    d_small, g_q_norm, g_kv_norm, g_bias = _prep_bwd(
        small, dcqn, dckvn, dkr_heads, dlogf, q_norm, kv_norm, bias_pad, kc, ksa, ksb, H, "prep_bwd")
    dproj = [d_small, dfq, dfk, dfv, dg_mla, dg_fox]
    gs, gfq, gfk, gfv, gg_mla, gg_fox = [
        _matmul(part, xn, "tn", [BF16], "proj_dw_" + tag, tm=1024, tn=1024, tk=2048)
        for part, tag in zip(dproj, ("small", "fq", "fk", "fv", "g_mla", "g_fox"))]

    g_w_in = jnp.concatenate([gs[:o_kr], gs[o_kr:o_kr + MLA_ROPE], gfq, gfk, gfv,
                              gs[o_kr + LANE:o_kr + LANE + HF], gg_mla, gg_fox], axis=0)
    g_w_uq = g_w_uq_p.reshape(QL, H, QPAD)[:, :, :dqk].reshape(QL, H * dqk)
    g_w_ukv = jnp.concatenate([g_w_ukv_p[:, :H * MLA_NOPE].reshape(KVL, H, MLA_NOPE),
                               g_w_ukv_p[:, H * MLA_NOPE:].reshape(KVL, H, MLA_V)], axis=2).reshape(KVL, -1)

    gp_a = pack_a.slabs({"w_in": g_w_in, "w_uq": g_w_uq, "w_ukv": g_w_ukv})
    rs_a = _xchg_start(gp_a, lax.empty((3, RA, C), BF16), False, gg_fox, "grad_scatter_start_a")
    dxn = _matmul_row_parts(dproj, w_pack, F32, "proj_dx", order=rs_a[3])
    grad_x, g_attn_norm = _norm_bwd(xs, dxn, attn_norm, dh1, "attn_norm_bwd", with_bf16=False)
    gp_a_sent, recv_a = _xchg_wait(rs_a, False, grad_x, "grad_scatter_wait_a")
    swap_a = _sib_start(_sum_slabs(gp_a_sent, recv_a, chip, "grad_sum_a"), "grad_swap_start_a")
    vec_w = max(D, LANE)
    vec_rows = [g_attn_norm, g_mlp_norm, g_final, g_q_norm, g_kv_norm, g_bias, loss_part]
    vec = jnp.concatenate([_pad_cols(v, vec_w) for v in vec_rows] + [jnp.zeros((1, vec_w), F32)], axis=0)
    vsum = _all_reduce_vec(vec, "all_reduce_vectors")
    part_b, sib_b = _sib_wait(swap_b, vsum, "grad_swap_wait_b")

    grads, deltas, new_m, new_v = {}, {}, {}, {}

    def update(pack, mine, theirs):
        for nm, shape, _ in pack.group:
            K, N = shape
            if N == pack.C and K % 8 == 0 and pack.offs[nm] % _tile(K, 256, 8) == 0:
                g, d, nm_, nv_ = _adamw(weights[nm], mine, theirs, moments[nm][0], moments[nm][1], "adamw_" + nm,
                                        g_row=pack.offs[nm])
            else:
                g, d, nm_, nv_ = _adamw(weights[nm], pack.part(mine, nm, shape), pack.part(theirs, nm, shape),
                                        moments[nm][0], moments[nm][1], "adamw_" + nm)
            grads[nm], deltas[nm], new_m[nm], new_v[nm] = g, d, nm_, nv_
        return g

    last_b = update(pack_b, part_b, sib_b)
    part_a, sib_a = _sib_wait(swap_a, last_b, "grad_swap_wait_a")
    update(pack_a, part_a, sib_a)

    vec_names = ["attn_norm", "mlp_norm", "final_norm", "q_norm", "kv_norm", "fox_f_bias"]

    def vec_pack(arrs):
        return jnp.concatenate([_pad_cols(a.reshape(1, -1), vec_w) for a in arrs]
                               + [jnp.zeros((2, vec_w), F32)], axis=0)[None]

    vg, vd, vm, vv = _adamw(vec_pack([weights[n] for n in vec_names]), vsum, jnp.zeros_like(vsum),
                            vec_pack([moments[n][0] for n in vec_names]), vec_pack([moments[n][1] for n in vec_names]),
                            "adamw_vectors")
    for r, nm in enumerate(vec_names):
        shp = weights[nm].shape
        n = weights[nm].size
        grads[nm] = vsum[r, :n].reshape(shp)
        deltas[nm], new_m[nm], new_v[nm] = (vd[0, r, :n].reshape(shp), vm[0, r, :n].reshape(shp),
                                            vv[0, r, :n].reshape(shp))
    loss = vsum[6, 0]

    for res in (grads, deltas, new_m, new_v):
        res["w_in"] = flip(res["w_in"])
    order = ["attn_norm", "w_in", "fox_f_bias", "q_norm", "w_uq", "kv_norm", "w_ukv", "w_mla_branch", "w_fox_branch",
             "w_out", "mlp_norm", "w_up", "w_down", "final_norm"]
    return (loss, grad_x[None], *[grads[n] for n in order], *[deltas[n] for n in order],
            *[new_m[n] for n in order], *[new_v[n] for n in order])
```

```python
import math

import jax
import jax.numpy as jnp
from jax import lax
from jax.experimental import pallas as pl
from jax.experimental.pallas import tpu as pltpu

CHUNK = 64
MLA_HEADS = 8
MLA_Q_LORA = 512
MLA_KV_LORA = 256
MLA_NOPE = 128
MLA_ROPE = 64
MLA_V = 128
ROPE_THETA = 10000.0
FOX_HEADS = 8
FOX_HEAD_DIM = 128
EPS = 1e-6

ADAM_LR = 0.001
ADAM_B1 = 0.9
ADAM_B2 = 0.999
ADAM_EPS = 1e-08
ADAM_WD = 0.01
ADAM_STEP = 10

LANE = 128
QPAD = 2 * LANE
N_CHIPS = 4
N_DEV = 8
VMEM_LIMIT = 48 * 1024 * 1024
ATT_T = 2048
QSUB = 256
ROW_T = 256
PACK_ROWS = 256
LOG2E = 1.4426950408889634

BF16 = jnp.bfloat16
F32 = jnp.float32
MESH = pl.DeviceIdType.MESH

_NT = (((1,), (1,)), ((), ()))
_TN = (((0,), (0,)), ((), ()))
_NN = (((1,), (0,)), ((), ()))


def _tile(dim, pref, align=LANE):
    if dim <= pref:
        return dim
    t = (pref // align) * align
    while t >= align:
        if dim % t == 0:
            return t
        t -= align
    return dim


def _params(sem=None):
    return pltpu.CompilerParams(dimension_semantics=sem, vmem_limit_bytes=VMEM_LIMIT)


_ANY_SPEC = pl.BlockSpec(memory_space=pl.ANY)


def _matmul(a, b, mode, out_dtypes, name, *, tm=1024, tn=1024, tk=2048, extras=(), row_extras=(), col_extras=(),
            epilogue=None, order=None, into=None, b_in=None):
    b_shape = b.shape if b_in is None else b_in[0]
    if mode == "nn":
        (M, K), (K2, N) = a.shape, b_shape
    elif mode == "nt":
        (M, K), (N, K2) = a.shape, b_shape
    else:
        (K, M), (K2, N) = a.shape, b_shape
    assert K == K2, (name, a.shape, b_shape)
    tm, tn, tk = _tile(M, tm), _tile(N, tn), _tile(K, tk)
    nk = K // tk
    extras = [e if isinstance(e, tuple) else (e, 0) for e in extras]
    n_out = len(out_dtypes)
    n_ex = len(extras) + len(row_extras) + len(col_extras)
    n_ord = 0 if order is None else 1
    assert all(r.shape == (M, tn) for r in row_extras), name
    dims = {"nn": _NN, "nt": _NT, "tn": _TN}[mode]

    def body(*refs):
        a_ref, b_ref = refs[0], refs[1]
        ex_refs = refs[2:2 + n_ex]
        o_refs = refs[2 + n_ex + n_ord:2 + n_ex + n_ord + n_out]
        acc_ref = refs[2 + n_ex + n_ord + n_out]
        k = pl.program_id(2)
        part = lax.dot_general(a_ref[...], b_ref[...], dims, preferred_element_type=F32)

        @pl.when(k == 0)
        def _():
            acc_ref[...] = part

        @pl.when(k > 0)
        def _():
            acc_ref[...] += part

        @pl.when(k == nk - 1)
        def _():
            acc = acc_ref[...]
            if epilogue is None:
                outs = (acc,)
            else:
                outs = epilogue(acc, *[r[...] for r in ex_refs])
            for o_ref, o in zip(o_refs, outs):
                o_ref[...] = o.astype(o_ref.dtype)

    if mode == "nn":
        a_spec = pl.BlockSpec((tm, tk), lambda i, j, k: (i, k))
        b_spec = pl.BlockSpec((tk, tn), lambda i, j, k: (k, j))
    elif mode == "nt":
        a_spec = pl.BlockSpec((tm, tk), lambda i, j, k: (i, k))
        b_spec = pl.BlockSpec((tn, tk), lambda i, j, k: (j, k))
    else:
        a_spec = pl.BlockSpec((tk, tm), lambda i, j, k: (k, i))
        b_spec = pl.BlockSpec((tk, tn), lambda i, j, k: (k, j))
    if b_in is not None:
        b_block = (None, tn, tk) if mode == "nt" else (None, tk, tn)
        b_spec = pl.BlockSpec(b_block, lambda i, j, k: b_in[1](j, k))
    mn_spec = pl.BlockSpec((tm, tn), lambda i, j, k: (i, j))
    row_spec = pl.BlockSpec((tm, tn), lambda i, j, k: (i, 0))
    col_spec = pl.BlockSpec((1, tn), lambda i, j, k: (0, j))
    out_specs = [mn_spec] * n_out
    out_shape = [jax.ShapeDtypeStruct((M, N), dt) for dt in out_dtypes]
    aliases = {}
    if into is not None:
        buf, place = into
        assert n_out == 1 and n_ord == 1 and order is buf, name
        out_specs = [pl.BlockSpec((None, tm, tn), lambda i, j, k: place(i, j))]
        out_shape = [jax.ShapeDtypeStruct(buf.shape, buf.dtype)]
        aliases = {2 + n_ex: 0}
    outs = pl.pallas_call(
        body,
        name=name,
        grid=(M // tm, N // tn, nk),
        in_specs=([a_spec, b_spec]
                  + [pl.BlockSpec((tm, tn), lambda i, j, k, g=g: (i, j + g * (N // tn))) for _, g in extras]
                  + [row_spec] * len(row_extras) + [col_spec] * len(col_extras) + [_ANY_SPEC] * n_ord),
        out_specs=out_specs,
        out_shape=out_shape,
        scratch_shapes=[pltpu.VMEM((tm, tn), F32)],
        input_output_aliases=aliases,
        compiler_params=_params(("parallel", "parallel", "arbitrary")),
    )(a, b, *[e for e, _ in extras], *row_extras, *col_extras, *([] if order is None else [order]))
    return outs[0] if n_out == 1 else outs


def _matmul_row_parts(parts, b, out_dtype, name, *, tm=512, tk=2048, order=None):
    M, (K, N) = parts[0].shape[0], b.shape
    widths = [p.shape[1] for p in parts]
    assert sum(widths) == K, name
    tm, tk = _tile(M, tm), _tile(K, tk)
    nk = K // tk
    steps, at = [], 0
    for p, w in enumerate(widths):
        off = 0
        while off < w:
            k, room = divmod(at, tk)
            take = min(w - off, tk - room)
            if room == 0:
                steps.append([])
            steps[k].append((p, off, take, room))
            off += take
            at += take
    assert len(steps) == nk and all(t % LANE == 0 and o % LANE == 0 for s in steps for _, o, t, _ in s), name
    n_parts = len(parts)
    n_ord = 0 if order is None else 1

    def body(*refs):
        a_refs = refs[0:n_parts]
        b_ref = refs[n_parts]
        o_ref, acc_ref = refs[n_parts + 1 + n_ord], refs[n_parts + 2 + n_ord]
        k = pl.program_id(1)
        for kk, pieces in enumerate(steps):
            @pl.when(k == kk)
            def _(kk=kk, pieces=pieces):
                part = None
                for p, off, take, room in pieces:
                    d = jnp.dot(a_refs[p][:, off:off + take], b_ref[room:room + take, :], preferred_element_type=F32)
                    part = d if part is None else part + d
                if kk == 0:
                    acc_ref[...] = part
                else:
                    acc_ref[...] += part

        @pl.when(k == nk - 1)
        def _():
            o_ref[...] = acc_ref[...].astype(o_ref.dtype)

    return pl.pallas_call(
        body, name=name, grid=(M // tm, nk),
        in_specs=[pl.BlockSpec((tm, w), lambda i, k: (i, 0)) for w in widths]
        + [pl.BlockSpec((tk, N), lambda i, k: (k, 0))] + [_ANY_SPEC] * n_ord,
        out_specs=pl.BlockSpec((tm, N), lambda i, k: (i, 0)),
        out_shape=jax.ShapeDtypeStruct((M, N), out_dtype),
        scratch_shapes=[pltpu.VMEM((tm, N), F32)],
        compiler_params=_params(("parallel", "arbitrary")),
    )(*parts, b, *([] if order is None else [order]))


def _mm_tn(a, b, name, tm=1024, tn=1024, into=None):
    return _matmul(a, b, "tn", [F32], name, tm=tm, tn=tn, tk=2048, into=into,
                   order=None if into is None else into[0])


def _row_spec(ts, width, col=0):
    return pl.BlockSpec((ts, width), lambda i: (i, col))


def _full_spec(shape):
    return pl.BlockSpec(shape, lambda i: tuple(0 for _ in shape))


def _rms(x):
    return lax.rsqrt(jnp.mean(x * x, axis=-1, keepdims=True) + EPS)


def _rms_bwd(x, dy, g):
    r = _rms(x)
    xh = x * r
    gy = dy * g
    dx = r * (gy - xh * jnp.mean(xh * gy, axis=-1, keepdims=True))
    return dx, dy * xh


def _norm_fwd(x, g, name, order=None):
    S, D = x.shape
    ts = _tile(S, ROW_T, 8)

    def body(x_ref, g_ref, *rest):
        o_ref = rest[-1]
        xv = x_ref[...]
        o_ref[...] = ((xv * _rms(xv)) * g_ref[...]).astype(BF16)

    extra = [] if order is None else [order]
    return pl.pallas_call(
        body, name=name, grid=(S // ts,),
        in_specs=[_row_spec(ts, D), _full_spec((1, D))] + [_ANY_SPEC] * len(extra),
        out_specs=_row_spec(ts, D),
        out_shape=jax.ShapeDtypeStruct((S, D), BF16),
        compiler_params=_params(("parallel",)),
    )(x, g, *extra)


def _norm_bwd(x, dy, g, dres, name, with_bf16=True):
    S, D = x.shape
    ts = _tile(S, ROW_T, 8)

    def body(x_ref, dy_ref, g_ref, dres_ref, dx_ref, *rest):
        dg_ref = rest[-1]
        dx, dg_rows = _rms_bwd(x_ref[...], dy_ref[...], g_ref[...])
        dx = dres_ref[...] + dx
        dx_ref[...] = dx
        if with_bf16:
            rest[0][...] = dx.astype(BF16)

        @pl.when(pl.program_id(0) == 0)
        def _():
            dg_ref[...] = jnp.zeros_like(dg_ref)

        dg_ref[...] += jnp.sum(dg_rows, axis=0, keepdims=True)

    return pl.pallas_call(
        body, name=name, grid=(S // ts,),
        in_specs=[_row_spec(ts, D), _row_spec(ts, D), _full_spec((1, D)), _row_spec(ts, D)],
        out_specs=[_row_spec(ts, D)] * (2 if with_bf16 else 1) + [_full_spec((1, D))],
        out_shape=([jax.ShapeDtypeStruct((S, D), F32)] + [jax.ShapeDtypeStruct((S, D), BF16)] * with_bf16
                   + [jax.ShapeDtypeStruct((1, D), F32)]),
        compiler_params=_params(("arbitrary",)),
    )(x, dy, g, dres)


def _rope(x, c, sa, sb, sign):
    w = x.shape[-1]
    half = MLA_ROPE // 2
    fwd = pltpu.roll(x, w - half, 1)
    back = pltpu.roll(x, half, 1)
    if sign < 0:
        return x * c - fwd * sa - back * sb
    return x * c + fwd * sa + back * sb


def _split3(x):
    hi = x.astype(BF16)
    r1 = x - hi.astype(F32)
    mid = r1.astype(BF16)
    lo = (r1 - mid.astype(F32)).astype(BF16)
    return hi, mid, lo


def _prep_fwd(small, q_norm, kv_norm, bias_pad, kc, ksa, ksb, n_heads, name):
    S, W = small.shape
    QL, KVL = q_norm.shape[1], kv_norm.shape[1]
    assert W == QL + KVL + 2 * LANE
    ts = _tile(S, ROW_T, 8)
    tri = (lax.broadcasted_iota(jnp.int32, (ts, ts), 0) >= lax.broadcasted_iota(jnp.int32, (ts, ts), 1)).astype(BF16)

    def body(s_ref, qn_ref, kvn_ref, b_ref, kc_ref, ksa_ref, ksb_ref, tri_ref,
             cqn_ref, ckvn_ref, kr_ref, cum_ref, carry_ref):
        cq = s_ref[:, 0:QL]
        cqn_ref[...] = ((cq * _rms(cq)) * qn_ref[...]).astype(BF16)
        ckv = s_ref[:, QL:QL + KVL]
        ckvn_ref[...] = ((ckv * _rms(ckv)) * kvn_ref[...]).astype(BF16)
        kr = s_ref[:, QL + KVL:QL + KVL + LANE]
        kr_ref[...] = _rope(kr, kc_ref[...], ksa_ref[...], ksb_ref[...], 1).astype(BF16)
        z = s_ref[:, QL + KVL + LANE:W] + b_ref[...]
        logf = jnp.minimum(z, 0.0) - jnp.log1p(jnp.exp(-jnp.abs(z)))
        lane = lax.broadcasted_iota(jnp.int32, logf.shape, 1)
        logf = jnp.where(lane < n_heads, logf, 0.0)

        @pl.when(pl.program_id(0) == 0)
        def _():
            carry_ref[...] = jnp.zeros_like(carry_ref)

        t = tri_ref[...]
        cum = carry_ref[...]
        for part in _split3(logf):
            cum = cum + jnp.dot(t, part, preferred_element_type=F32)
        cum_ref[...] = cum
        carry_ref[...] = cum[ts - 1:ts, :]

    return pl.pallas_call(
        body, name=name, grid=(S // ts,),
        in_specs=[_row_spec(ts, W), _full_spec((1, QL)), _full_spec((1, KVL)), _full_spec((1, LANE)),
                  _row_spec(ts, LANE), _row_spec(ts, LANE), _row_spec(ts, LANE), _full_spec((ts, ts))],
        out_specs=[_row_spec(ts, QL), _row_spec(ts, KVL), _row_spec(ts, LANE), _row_spec(ts, LANE)],
        out_shape=[jax.ShapeDtypeStruct((S, QL), BF16), jax.ShapeDtypeStruct((S, KVL), BF16),
                   jax.ShapeDtypeStruct((S, LANE), BF16), jax.ShapeDtypeStruct((S, LANE), F32)],
        scratch_shapes=[pltpu.VMEM((1, LANE), F32)],
        compiler_params=_params(("arbitrary",)),
    )(small, q_norm, kv_norm, bias_pad, kc, ksa, ksb, tri)


def _prep_bwd(small, dcqn, dckvn, dkr_heads, dlogf, q_norm, kv_norm, bias_pad, kc, ksa, ksb, n_heads, name):
    S, W = small.shape
    QL, KVL = q_norm.shape[1], kv_norm.shape[1]
    ts = _tile(S, ROW_T, 8)

    def body(s_ref, dcq_ref, dckv_ref, dkr_ref, dlf_ref, qn_ref, kvn_ref, b_ref, kc_ref, ksa_ref, ksb_ref,
             ds_ref, gq_ref, gkv_ref, gb_ref):
        dcq, gq_rows = _rms_bwd(s_ref[:, 0:QL], dcq_ref[...], qn_ref[...])
        ds_ref[:, 0:QL] = dcq.astype(BF16)
        dckv, gkv_rows = _rms_bwd(s_ref[:, QL:QL + KVL], dckv_ref[...], kvn_ref[...])
        ds_ref[:, QL:QL + KVL] = dckv.astype(BF16)
        dkr = dkr_ref[:, 0:LANE]
        for h in range(1, n_heads):
            dkr = dkr + dkr_ref[:, h * LANE:(h + 1) * LANE]
        ds_ref[:, QL + KVL:QL + KVL + LANE] = _rope(dkr, kc_ref[...], ksa_ref[...], ksb_ref[...], -1).astype(BF16)
        z = s_ref[:, QL + KVL + LANE:W] + b_ref[...]
        dff = dlf_ref[...] * (1.0 / (1.0 + jnp.exp(z)))
        ds_ref[:, QL + KVL + LANE:W] = dff.astype(BF16)

        @pl.when(pl.program_id(0) == 0)
        def _():
            gq_ref[...] = jnp.zeros_like(gq_ref)
            gkv_ref[...] = jnp.zeros_like(gkv_ref)
            gb_ref[...] = jnp.zeros_like(gb_ref)

        gq_ref[...] += jnp.sum(gq_rows, axis=0, keepdims=True)
        gkv_ref[...] += jnp.sum(gkv_rows, axis=0, keepdims=True)
        gb_ref[...] += jnp.sum(dff, axis=0, keepdims=True)

    return pl.pallas_call(
        body, name=name, grid=(S // ts,),
        in_specs=[_row_spec(ts, W), _row_spec(ts, QL), _row_spec(ts, KVL), _row_spec(ts, n_heads * LANE),
                  _row_spec(ts, LANE), _full_spec((1, QL)), _full_spec((1, KVL)), _full_spec((1, LANE)),
                  _row_spec(ts, LANE), _row_spec(ts, LANE), _row_spec(ts, LANE)],
        out_specs=[_row_spec(ts, W), _full_spec((1, QL)), _full_spec((1, KVL)), _full_spec((1, LANE))],
        out_shape=[jax.ShapeDtypeStruct((S, W), BF16), jax.ShapeDtypeStruct((1, QL), F32),
                   jax.ShapeDtypeStruct((1, KVL), F32), jax.ShapeDtypeStruct((1, LANE), F32)],
        compiler_params=_params(("arbitrary",)),
    )(small, dcqn, dckvn, dkr_heads, dlogf, q_norm, kv_norm, bias_pad, kc, ksa, ksb)


def _sigmoid(z):
    return 1.0 / (1.0 + jnp.exp(-z))


def _final(h, g, target, name):
    S, D = h.shape
    ts = _tile(S, ROW_T, 8)

    def body(h_ref, g_ref, t_ref, dh_ref, dhb_ref, dg_ref, loss_ref):
        hv = h_ref[...]
        gv = g_ref[...]
        err = (hv * _rms(hv)) * gv - t_ref[...]
        dh, dg_rows = _rms_bwd(hv, err / D, gv)
        dh_ref[...] = dh
        dhb_ref[...] = dh.astype(BF16)

        @pl.when(pl.program_id(0) == 0)
        def _():
            dg_ref[...] = jnp.zeros_like(dg_ref)
            loss_ref[...] = jnp.zeros_like(loss_ref)

        dg_ref[...] += jnp.sum(dg_rows, axis=0, keepdims=True)
        row_loss = jnp.mean(err * err, axis=-1, keepdims=True)
        loss_ref[...] += 0.5 * jnp.sum(row_loss, axis=0, keepdims=True)

    return pl.pallas_call(
        body, name=name, grid=(S // ts,),
        in_specs=[_row_spec(ts, D), _full_spec((1, D)), _row_spec(ts, D)],
        out_specs=[_row_spec(ts, D), _row_spec(ts, D), _full_spec((1, D)), _full_spec((1, LANE))],
        out_shape=[jax.ShapeDtypeStruct((S, D), F32), jax.ShapeDtypeStruct((S, D), BF16),
                   jax.ShapeDtypeStruct((1, D), F32), jax.ShapeDtypeStruct((1, LANE), F32)],
        compiler_params=_params(("arbitrary",)),
    )(h, g, target)


def _suffix_sum_rows(x, name):
    R, S = x.shape
    tb = _tile(S, 512)
    nb = S // tb
    tri = (lax.broadcasted_iota(jnp.int32, (tb, tb), 0) >= lax.broadcasted_iota(jnp.int32, (tb, tb), 1)).astype(BF16)

    def body(x_ref, tri_ref, o_ref, carry_ref):
        @pl.when(pl.program_id(0) == 0)
        def _():
            carry_ref[...] = jnp.zeros_like(carry_ref)

        xv = x_ref[...]
        t = tri_ref[...]
        acc = jnp.broadcast_to(carry_ref[:, 0:1], xv.shape)
        for part in _split3(xv):
            acc = acc + jnp.dot(part, t, preferred_element_type=F32)
        o_ref[...] = acc
        carry_ref[...] = jnp.broadcast_to(acc[:, 0:1], carry_ref.shape)

    rev = pl.BlockSpec((R, tb), lambda i: (0, nb - 1 - i))
    return pl.pallas_call(
        body, name=name, grid=(nb,),
        in_specs=[rev, _full_spec((tb, tb))], out_specs=rev,
        out_shape=jax.ShapeDtypeStruct((R, S), F32),
        scratch_shapes=[pltpu.VMEM((R, LANE), F32)],
        compiler_params=_params(("arbitrary",)),
    )(x, tri)


def _pairs(nb, by_key):
    if by_key:
        pr = [(i, j) for j in range(nb) for i in range(j, nb)]
    else:
        pr = [(i, j) for i in range(nb) for j in range(i + 1)]
    return (jnp.asarray([p[0] for p in pr], jnp.int32), jnp.asarray([p[1] for p in pr], jnp.int32), len(pr))


class _AttT:
    def __init__(self, S, n_heads, q, ks, v, scale, chunk_causal, cum_rep=None, qsub=None):
        self.S, self.H, self.q, self.ks, self.v = S, n_heads, q, ks, v
        self.scale, self.chunk_causal, self.cum_rep = scale, chunk_causal, cum_rep
        self.T = _tile(S, ATT_T)
        self.qs = min(qsub or QSUB, self.T)
        self.nb = S // self.T
        self.dq, self.dv = q[1], v[1]
        self.has_bias = cum_rep is not None

    def q_spec(self, op):
        _, w, off, per_head = op
        return pl.BlockSpec((self.T, w), lambda h, p, it, jt: (it[p], off + (h if per_head else 0)))

    def k_spec(self, op):
        _, w, off, per_head = op
        return pl.BlockSpec((self.T, w), lambda h, p, it, jt: (jt[p], off + (h if per_head else 0)))

    def row_q(self):
        return pl.BlockSpec((None, 1, self.T), lambda h, p, it, jt: (h, 0, it[p]))

    def cum_k(self):
        return pl.BlockSpec((None, self.T, self.qs), lambda h, p, it, jt: (h, jt[p], 0))

    def sub_blocks(self, masked):
        return [(q0, min(self.T, q0 + self.qs) if masked else self.T) for q0 in range(0, self.T, self.qs)]

    def scores(self, k, q_sub, cum, q0, masked):
        s = lax.dot_general(k, q_sub, _NT, preferred_element_type=F32)
        if self.has_bias:
            s = s - cum
        mask = None
        if masked:
            r = lax.broadcasted_iota(jnp.int32, s.shape, 0)
            c = lax.broadcasted_iota(jnp.int32, s.shape, 1) + q0
            mask = (r // CHUNK <= c // CHUNK) if self.chunk_causal else (r <= c)
        return s, mask


def _join(k_refs):
    return k_refs[0][...] if len(k_refs) == 1 else jnp.concatenate([r[...] for r in k_refs], axis=-1)


def _att_fwd_t(att, name, exact=False):
    S, H, T, qs = att.S, att.H, att.T, att.qs
    it, jt, npairs = _pairs(att.nb, by_key=False)
    nk = len(att.ks)

    def body(it_ref, jt_ref, *refs):
        q_ref = refs[0]
        k_refs = refs[1:1 + nk]
        v_ref = refs[1 + nk]
        n = 2 + nk
        cum_ref = None
        if att.has_bias:
            cum_ref = refs[n]
            n += 1
        o_ref = refs[n]
        n += 1
        ox_ref = None
        if exact:
            ox_ref = refs[n]
            n += 1
        lse_ref, m_ref, l_ref, acc_ref = refs[n:n + 4]
        lo_ref = refs[n + 4] if exact else None
        p = pl.program_id(1)
        i, j = it_ref[p], jt_ref[p]

        @pl.when(j == 0)
        def _():
            m_ref[...] = jnp.full_like(m_ref, -jnp.inf)
            l_ref[...] = jnp.zeros_like(l_ref)
            acc_ref[...] = jnp.zeros_like(acc_ref)
            if exact:
                lo_ref[...] = jnp.zeros_like(lo_ref)

        def step(masked):
            k = _join(k_refs)
            v = v_ref[...]
            subs = att.sub_blocks(masked)

            def logits(idx):
                q0, nkeys = subs[idx]
                cum = cum_ref[0:nkeys, :] if att.has_bias else None
                return att.scores(k[0:nkeys], q_ref[q0:q0 + qs, :], cum, q0, masked)

            ahead = logits(0)
            for idx, (q0, nkeys) in enumerate(subs):
                qsl = slice(q0, q0 + qs)
                s, mask = ahead
                if idx + 1 < len(subs):
                    ahead = logits(idx + 1)
                if masked:
                    s = jnp.where(mask, s, -jnp.inf)
                m_prev = m_ref[:, qsl]
                m_new = jnp.maximum(m_prev, jnp.max(s, axis=0, keepdims=True))
                alpha = jnp.exp2(m_prev - m_new)
                pr = jnp.exp2(s - m_new)
                l_ref[:, qsl] = alpha * l_ref[:, qsl] + jnp.sum(pr, axis=0, keepdims=True)
                p_hi = pr.astype(BF16)
                acc_ref[:, qsl] = alpha * acc_ref[:, qsl] + lax.dot_general(
                    v[0:nkeys], p_hi, _TN, preferred_element_type=F32)
                if exact:
                    p_lo = (pr - p_hi.astype(F32)).astype(BF16)
                    lo_ref[:, qsl] = alpha * lo_ref[:, qsl] + lax.dot_general(
                        v[0:nkeys], p_lo, _TN, preferred_element_type=F32)
                m_ref[:, qsl] = m_new

        @pl.when(j < i)
        def _():
            step(False)

        @pl.when(j == i)
        def _():
            step(True)
            l = l_ref[...]
            inv = 1.0 / l
            o_ref[...] = jnp.transpose(acc_ref[...] * inv).astype(o_ref.dtype)
            if exact:
                ox_ref[...] = jnp.transpose((acc_ref[...] + lo_ref[...]) * inv)
            lse_ref[...] = m_ref[...] + jnp.log2(l)

    in_specs = [att.q_spec(att.q)] + [att.k_spec(k) for k in att.ks] + [att.k_spec(att.v)]
    args = [att.q[0]] + [k[0] for k in att.ks] + [att.v[0]]
    if att.has_bias:
        in_specs.append(att.cum_k())
        args.append(att.cum_rep)
    o_spec = pl.BlockSpec((T, att.dv), lambda h, p, it, jt: (it[p], h))
    out_specs = [o_spec]
    out_shape = [jax.ShapeDtypeStruct((S, H * att.dv), BF16)]
    scratch = [pltpu.VMEM((1, T), F32), pltpu.VMEM((1, T), F32), pltpu.VMEM((att.dv, T), F32)]
    if exact:
        out_specs.append(o_spec)
        out_shape.append(jax.ShapeDtypeStruct((S, H * att.dv), F32))
        scratch.append(pltpu.VMEM((att.dv, T), F32))
    out_specs.append(att.row_q())
    out_shape.append(jax.ShapeDtypeStruct((H, 1, S), F32))
    return pl.pallas_call(
        body, name=name,
        grid_spec=pltpu.PrefetchScalarGridSpec(
            num_scalar_prefetch=2, grid=(H, npairs), in_specs=in_specs, out_specs=out_specs,
            scratch_shapes=scratch),
        out_shape=out_shape,
        compiler_params=_params(("parallel", "arbitrary")),
    )(it, jt, *args)


def _att_bwd_t(att, do, lse, o, dq_dtype, dk_dtypes, name, dq_rope=None, order=None):
    S, H, T, qs = att.S, att.H, att.T, att.qs
    it, jt, npairs = _pairs(att.nb, by_key=True)
    nk = len(att.ks)
    last = att.nb - 1
    widths = [k[1] for k in att.ks]

    def body(it_ref, jt_ref, *refs):
        q_ref = refs[0]
        k_refs = refs[1:1 + nk]
        v_ref, do_ref, lse_ref, o_ref = refs[1 + nk:5 + nk]
        n = 5 + nk
        cum_ref = None
        if att.has_bias:
            cum_ref = refs[n]
            n += 1
        rope_refs = None
        if dq_rope is not None:
            rope_refs = refs[n:n + 3]
            n += 3
        if order is not None:
            n += 1
        dl_acc = refs[-1]
        dq_ref = refs[n]
        dk_refs = refs[n + 1:n + 1 + nk]
        dv_ref = refs[n + 1 + nk]
        n += nk + 2
        dc_ref = None
        if att.has_bias:
            dc_ref = refs[n]
            n += 1
        dq_acc, dk_acc, dv_acc = refs[n:n + 3]
        dc_acc = refs[n + 3] if att.has_bias else None
        p = pl.program_id(1)
        i, j = it_ref[p], jt_ref[p]

        @pl.when(p == 0)
        def _():
            dq_acc[...] = jnp.zeros_like(dq_acc)

        @pl.when(i == j)
        def _():
            dk_acc[...] = jnp.zeros_like(dk_acc)
            dv_acc[...] = jnp.zeros_like(dv_acc)
            if att.has_bias:
                dc_acc[...] = jnp.zeros_like(dc_acc)

        @pl.when(j == 0)
        def _():
            prod = do_ref[...].astype(F32) * o_ref[...].astype(F32)
            ones = jnp.ones((8, att.dv), BF16)
            rows = jnp.zeros((8, T), F32)
            for part in _split3(prod):
                rows = rows + lax.dot_general(ones, part, _NT, preferred_element_type=F32)
            dl_acc[i] = rows[0:1, :]

        def step(masked):
            k = _join(k_refs)
            v = v_ref[...]
            dl = dl_acc[i]
            subs = att.sub_blocks(masked)

            def logits(idx):
                q0, nkeys = subs[idx]
                cum = cum_ref[0:nkeys, :] if att.has_bias else None
                return att.scores(k[0:nkeys], q_ref[q0:q0 + qs, :], cum, q0, masked)

            ahead = logits(0)
            for idx, (q0, nkeys) in enumerate(subs):
                qsl = slice(q0, q0 + qs)
                ksl = slice(0, nkeys)
                q_sub = q_ref[qsl, :]
                do_sub = do_ref[qsl, :]
                s, mask = ahead
                if idx + 1 < len(subs):
                    ahead = logits(idx + 1)
                pr = jnp.exp2(s - lse_ref[:, qsl])
                if masked:
                    pr = jnp.where(mask, pr, 0.0)
                dp = lax.dot_general(v[ksl], do_sub, _NT, preferred_element_type=F32)
                ds = pr * (dp - dl[:, qsl])
                ds_b = ds.astype(BF16)
                dv_acc[ksl, :] += jnp.dot(pr.astype(BF16), do_sub, preferred_element_type=F32)
                dk_acc[ksl, :] += jnp.dot(ds_b, q_sub, preferred_element_type=F32)
                dq_acc[i, :, qsl] += lax.dot_general(k[ksl], ds_b, _TN, preferred_element_type=F32)
                if att.has_bias:
                    part = ds[:, 0:LANE] if qs >= LANE else ds
                    for c0 in range(LANE, qs, LANE):
                        part = part + ds[:, c0:c0 + LANE]
                    dc_acc[ksl, :] += part

        @pl.when(i > j)
        def _():
            step(False)

        @pl.when(i == j)
        def _():
            step(True)
            dq = jnp.transpose(dq_acc[i] * att.scale)
            if dq_rope is not None:
                dq = _rope(dq, rope_refs[0][...], rope_refs[1][...], rope_refs[2][...], -1)
            dq_ref[...] = dq.astype(dq_ref.dtype)

        @pl.when(i == last)
        def _():
            dk = dk_acc[...] * (1.0 / LOG2E)
            off = 0
            for r, w in zip(dk_refs, widths):
                r[...] = dk[:, off:off + w].astype(r.dtype)
                off += w
            dv_ref[...] = dv_acc[...].astype(dv_ref.dtype)
            if att.has_bias:
                dc_ref[...] = -jnp.sum(dc_acc[...], axis=-1, keepdims=True)

    do_op = (do, att.dv, 0, True)
    o_spec = pl.BlockSpec((T, att.dv), lambda h, p, it, jt: (jnp.where(jt[p] == 0, it[p], last), h))
    in_specs = ([att.q_spec(att.q)] + [att.k_spec(k) for k in att.ks]
                + [att.k_spec(att.v), att.q_spec(do_op), att.row_q(), o_spec])
    args = [att.q[0]] + [k[0] for k in att.ks] + [att.v[0], do, lse, o]
    if att.has_bias:
        in_specs.append(att.cum_k())
        args.append(att.cum_rep)
    if dq_rope is not None:
        in_specs += [pl.BlockSpec((T, att.dq), lambda h, p, it, jt: (jt[p], 0))] * 3
        args += list(dq_rope)
    if order is not None:
        in_specs.append(_ANY_SPEC)
        args.append(order)
    out_specs = [pl.BlockSpec((T, att.dq), lambda h, p, it, jt: (jt[p], h))]
    out_shape = [jax.ShapeDtypeStruct((S, H * att.dq), dq_dtype)]
    out_specs += [pl.BlockSpec((T, w), lambda h, p, it, jt: (jt[p], h)) for w in widths]
    out_shape += [jax.ShapeDtypeStruct((S, H * w), dt) for w, dt in zip(widths, dk_dtypes)]
    out_specs.append(pl.BlockSpec((T, att.dv), lambda h, p, it, jt: (jt[p], h)))
    out_shape.append(jax.ShapeDtypeStruct((S, H * att.dv), BF16))
    scratch = [pltpu.VMEM((att.nb, att.dq, T), F32), pltpu.VMEM((T, att.dq), F32), pltpu.VMEM((T, att.dv), F32)]
    if att.has_bias:
        out_specs.append(pl.BlockSpec((None, T, 1), lambda h, p, it, jt: (h, jt[p], 0)))
        out_shape.append(jax.ShapeDtypeStruct((H, S, 1), F32))
        scratch.append(pltpu.VMEM((T, min(qs, LANE)), F32))
    scratch.append(pltpu.VMEM((att.nb, 1, T), F32))
    return pl.pallas_call(
        body, name=name,
        grid_spec=pltpu.PrefetchScalarGridSpec(
            num_scalar_prefetch=2, grid=(H, npairs), in_specs=in_specs, out_specs=out_specs,
            scratch_shapes=scratch),
        out_shape=out_shape,
        compiler_params=_params(("parallel", "arbitrary")),
    )(it, jt, *args)


def _adamw(w, g1, g2, m, v, name, g_row=None):
    _, K, N = w.shape
    by_rows = K % 8 == 0
    tr = _tile(K, 256, 8) if by_rows else K
    if g_row is None:
        assert g1.shape == (K, N) and g2.shape == (K, N), name
        g_row = 0
    assert by_rows and g_row % tr == 0 or g_row == 0, name
    g_blk = g_row // tr
    tc = N if by_rows else _tile(N, LANE)
    c1 = 1.0 - ADAM_B1 ** ADAM_STEP
    c2 = 1.0 - ADAM_B2 ** ADAM_STEP

    def body(w_ref, g1_ref, g2_ref, m_ref, v_ref, g_ref, d_ref, nm_ref, nv_ref):
        gv = g1_ref[...] + g2_ref[...]
        nm = ADAM_B1 * m_ref[...] + (1.0 - ADAM_B1) * gv
        nv = ADAM_B2 * v_ref[...] + (1.0 - ADAM_B2) * (gv * gv)
        g_ref[...] = gv
        d_ref[...] = -ADAM_LR * ((nm / c1) / (jnp.sqrt(nv / c2) + ADAM_EPS) + ADAM_WD * w_ref[...])
        nm_ref[...] = nm
        nv_ref[...] = nv

    if by_rows:
        blk = pl.BlockSpec((None, tr, N), lambda i: (0, i, 0))
        gblk = pl.BlockSpec((tr, N), lambda i: (g_blk + i, 0))
    else:
        blk = pl.BlockSpec((None, K, tc), lambda i: (0, 0, i))
        gblk = pl.BlockSpec((K, tc), lambda i: (0, i))
    return pl.pallas_call(
        body, name=name, grid=(K // tr if by_rows else N // tc,),
        in_specs=[blk, gblk, gblk, blk, blk], out_specs=[blk] * 4,
        out_shape=[jax.ShapeDtypeStruct((1, K, N), F32)] * 4,
        compiler_params=_params(("parallel",)),
    )(w, g1, g2, m, v)


_HBM_SPEC = pl.BlockSpec(memory_space=pltpu.HBM)
_SEM_SPEC = pl.BlockSpec(memory_space=pltpu.SEMAPHORE)
_VMEM_SPEC = pl.BlockSpec(memory_space=pltpu.VMEM)
_EFFECT = pltpu.SideEffectType.DATAFLOW_SIDE_EFFECTING


def _place():
    return lax.axis_index("x"), lax.axis_index("y"), lax.axis_index("c")


def _other_chips(x, y):
    return [(1 - x, y), (x, 1 - y), (1 - x, 1 - y)]


def _chip_copies(src_ref, land_ref, sems, gather):
    x, y, c = _place()
    me = 2 * x + y
    out, back = [], []
    if gather == "half":
        half = src_ref.shape[0] // 2
        mine = pl.ds(pl.multiple_of(c * half, 16), half)
    for n, (px, py) in enumerate(_other_chips(x, y)):
        if gather == "half":
            src, there, here = src_ref.at[mine], land_ref.at[me, mine], land_ref.at[2 * px + py, mine]
        elif gather:
            src, there, here = src_ref, land_ref.at[me], land_ref.at[2 * px + py]
        else:
            src, there, here = src_ref.at[2 * px + py], land_ref.at[n], land_ref.at[n]
        out.append(pltpu.make_async_remote_copy(
            src_ref=src, dst_ref=there, send_sem=sems[n], recv_sem=sems[3 + n],
            device_id=(px, py, c), device_id_type=MESH))
        back.append(pltpu.make_async_remote_copy(
            src_ref=src, dst_ref=here, send_sem=sems[n], recv_sem=sems[3 + n],
            device_id=(px, py, c), device_id_type=MESH))
    return out, back


def _xchg_start(src, land, gather, order, name):
    def body(src_ref, land_ref, order_ref, *outs):
        sems = outs[0:6]
        token = outs[8]
        out, _ = _chip_copies(src_ref, land_ref, sems, gather)
        for cp in out:
            cp.start()
        token[...] = jnp.zeros_like(token)

    outs = pl.pallas_call(
        body, name=name,
        out_shape=(pltpu.SemaphoreType.DMA(()),) * 6 + (
            pltpu.HBM(src.shape, src.dtype), pltpu.HBM(land.shape, land.dtype),
            jax.ShapeDtypeStruct((8, LANE), F32)),
        in_specs=(_HBM_SPEC, _HBM_SPEC, _ANY_SPEC),
        out_specs=(_SEM_SPEC,) * 6 + (_HBM_SPEC, _HBM_SPEC, _VMEM_SPEC),
        input_output_aliases={0: 6, 1: 7},
        compiler_params=pltpu.CompilerParams(has_side_effects=_EFFECT),
    )(pltpu.with_memory_space_constraint(src, pltpu.HBM), pltpu.with_memory_space_constraint(land, pltpu.HBM), order)
    return outs[0:6], outs[6], outs[7], outs[8]


def _xchg_wait(started, gather, after, name):
    sems, src, land, _ = started
    after = after if isinstance(after, tuple) else (after,)

    def body(src_ref, land_ref, *rest):
        _, back = _chip_copies(src_ref, land_ref, rest[0:6], gather)
        for cp in back:
            cp.wait_send()
            cp.wait_recv()

    return pl.pallas_call(
        body, name=name,
        out_shape=(pltpu.HBM(src.shape, src.dtype), pltpu.HBM(land.shape, land.dtype)),
        in_specs=(_HBM_SPEC, _HBM_SPEC) + (_SEM_SPEC,) * 6 + (_ANY_SPEC,) * len(after),
        out_specs=(_HBM_SPEC, _HBM_SPEC),
        input_output_aliases={0: 0, 1: 1},
        compiler_params=pltpu.CompilerParams(has_side_effects=_EFFECT),
    )(src, land, *sems, *after)


def _forward_halves(land, name):
    _, R, C = land.shape
    half = R // 2
    assert half % 16 == 0

    def body(land_ref, out_ref, send_sems, recv_sems):
        x, y, c = _place()
        mine = pl.ds(pl.multiple_of(c * half, 16), half)
        theirs = pl.ds(pl.multiple_of((1 - c) * half, 16), half)
        sends = []
        for n, (px, py) in enumerate(_other_chips(x, y)):
            cp = pltpu.make_async_remote_copy(
                src_ref=land_ref.at[2 * px + py, mine], dst_ref=out_ref.at[2 * px + py, mine],
                send_sem=send_sems.at[n], recv_sem=recv_sems.at[n], device_id=(x, y, 1 - c), device_id_type=MESH)
            cp.start()
            sends.append(cp)
        for n, (px, py) in enumerate(_other_chips(x, y)):
            pltpu.make_async_remote_copy(
                src_ref=land_ref.at[2 * px + py, theirs], dst_ref=out_ref.at[2 * px + py, theirs],
                send_sem=send_sems.at[n], recv_sem=recv_sems.at[n], device_id=(x, y, 1 - c),
                device_id_type=MESH).wait_recv()
        for cp in sends:
            cp.wait_send()

    return pl.pallas_call(
        body, name=name,
        in_specs=[_ANY_SPEC], out_specs=_ANY_SPEC,
        out_shape=jax.ShapeDtypeStruct(land.shape, land.dtype),
        input_output_aliases={0: 0},
        scratch_shapes=[pltpu.SemaphoreType.DMA((3,)), pltpu.SemaphoreType.DMA((3,))],
    )(land)


def _sib_copy(src_ref, land_ref, send_sem, recv_sem):
    x, y, c = _place()
    return pltpu.make_async_remote_copy(src_ref=src_ref, dst_ref=land_ref, send_sem=send_sem, recv_sem=recv_sem,
                                        device_id=(x, y, 1 - c), device_id_type=MESH)


def _sib_start(src, name):
    land = lax.empty(src.shape, src.dtype)

    def body(src_ref, land_ref, send_sem, recv_sem, src_thru, land_thru, token):
        _sib_copy(src_ref, land_ref, send_sem, recv_sem).start()
        token[...] = jnp.zeros_like(token)

    return pl.pallas_call(
        body, name=name,
        out_shape=(pltpu.SemaphoreType.DMA(()), pltpu.SemaphoreType.DMA(()),
                   pltpu.HBM(src.shape, src.dtype), pltpu.HBM(land.shape, land.dtype),
                   jax.ShapeDtypeStruct((8, LANE), F32)),
        in_specs=(_HBM_SPEC, _HBM_SPEC),
        out_specs=(_SEM_SPEC, _SEM_SPEC, _HBM_SPEC, _HBM_SPEC, _VMEM_SPEC),
        input_output_aliases={0: 2, 1: 3},
        compiler_params=pltpu.CompilerParams(has_side_effects=_EFFECT),
    )(pltpu.with_memory_space_constraint(src, pltpu.HBM), pltpu.with_memory_space_constraint(land, pltpu.HBM))


def _sib_wait(started, after, name):
    send_sem, recv_sem, src, land, _ = started

    def body(src_ref, land_ref, send_sem, recv_sem, after_ref, src_out, land_out):
        cp = _sib_copy(src_ref, land_ref, send_sem, recv_sem)
        cp.wait_send()
        cp.wait_recv()

    return pl.pallas_call(
        body, name=name,
        out_shape=(pltpu.HBM(src.shape, src.dtype), pltpu.HBM(land.shape, land.dtype)),
        in_specs=(_HBM_SPEC, _HBM_SPEC, _SEM_SPEC, _SEM_SPEC, _ANY_SPEC),
        out_specs=(_HBM_SPEC, _HBM_SPEC),
        input_output_aliases={0: 0, 1: 1},
        compiler_params=pltpu.CompilerParams(has_side_effects=_EFFECT),
    )(src, land, send_sem, recv_sem, after)


def _sum_slabs(gp, recv, chip, name):
    _, R, C = gp.shape
    tr = _tile(R, PACK_ROWS, 16)

    def body(chip_ref, own_ref, r0_ref, r1_ref, r2_ref, o_ref):
        acc = own_ref[...].astype(F32) + r0_ref[...].astype(F32)
        o_ref[...] = (acc + r1_ref[...].astype(F32)) + r2_ref[...].astype(F32)

    def got(n):
        return pl.BlockSpec((None, tr, C), lambda i, chip_ref: (n, i, 0))

    return pl.pallas_call(
        body, name=name,
        grid_spec=pltpu.PrefetchScalarGridSpec(
            num_scalar_prefetch=1, grid=(R // tr,),
            in_specs=[pl.BlockSpec((None, tr, C), lambda i, chip_ref: (chip_ref[0], i, 0)), got(0), got(1), got(2)],
            out_specs=pl.BlockSpec((tr, C), lambda i, chip_ref: (i, 0))),
        out_shape=jax.ShapeDtypeStruct((R, C), F32),
        compiler_params=_params(("parallel",)),
    )(jnp.reshape(chip, (1,)).astype(jnp.int32), gp, recv, recv, recv)


def _all_reduce_vec(vec, name):
    VR, W = vec.shape

    def body(vec_ref, vall_ref, vout_ref, vsend_sems, vrecv_sems):
        x, y, c = _place()
        vall_ref[4 * x + 2 * y + c] = vec_ref[...]
        sends = []
        peers = []
        for r in range(1, N_DEV):
            dx, dy, dc = (r >> 2) & 1, (r >> 1) & 1, r & 1
            peer = (x ^ dx, y ^ dy, c ^ dc)
            peers.append(peer)
            cp = pltpu.make_async_remote_copy(
                src_ref=vec_ref, dst_ref=vall_ref.at[4 * x + 2 * y + c], send_sem=vsend_sems.at[r - 1],
                recv_sem=vrecv_sems.at[r - 1], device_id=peer, device_id_type=MESH)
            cp.start()
            sends.append(cp)
        for r, peer in enumerate(peers):
            pltpu.make_async_remote_copy(
                src_ref=vec_ref, dst_ref=vall_ref.at[4 * peer[0] + 2 * peer[1] + peer[2]],
                send_sem=vsend_sems.at[r], recv_sem=vrecv_sems.at[r],
                device_id=peer, device_id_type=MESH).wait_recv()
        total = vall_ref[0]
        for d in range(1, N_DEV):
            total = total + vall_ref[d]
        vout_ref[...] = total
        for cp in sends:
            cp.wait_send()

    outs = pl.pallas_call(
        body, name=name,
        in_specs=[_VMEM_SPEC], out_specs=[_VMEM_SPEC, _VMEM_SPEC],
        out_shape=[jax.ShapeDtypeStruct((N_DEV, VR, W), F32), jax.ShapeDtypeStruct((VR, W), F32)],
        scratch_shapes=[pltpu.SemaphoreType.DMA((N_DEV - 1,)), pltpu.SemaphoreType.DMA((N_DEV - 1,))],
    )(vec)
    return outs[1]


class _Pack:
    def __init__(self, group, C):
        self.group, self.C = group, C
        self.rows, self.offs, off = {}, {}, 0
        for nm, (K, N), _ in group:
            assert N <= C, nm
            self.rows[nm] = K if 2 * N > C else -(-(K * N) // C)
            self.offs[nm] = off
            off += -(-self.rows[nm] // 16) * 16
        self.used = off
        self.R = -(-off // PACK_ROWS) * PACK_ROWS

    def _rows_of(self, a):
        K, N = a.shape
        if 2 * N > self.C:
            a = jnp.pad(a, ((0, 0), (0, self.C - N)))
        else:
            a = jnp.pad(a.reshape(-1), (0, -(K * N) % self.C)).reshape(-1, self.C)
        return jnp.pad(a, ((0, -a.shape[0] % 16), (0, 0)))

    def pack(self, shards):
        parts = [self._rows_of(shards[nm].astype(BF16)) for nm, _, _ in self.group]
        return jnp.concatenate(parts + [jnp.zeros((self.R - self.used, self.C), BF16)], axis=0)

    def _shard_of(self, rows, shape):
        K, N = shape
        return rows[:, :N] if 2 * N > self.C else rows.reshape(-1)[:K * N].reshape(K, N)

    def part(self, flat, nm, shape):
        return self._shard_of(flat[self.offs[nm]:self.offs[nm] + self.rows[nm]], shape)

    def slab_rows(self, nm, g):
        (K, N), axis = next((shape, axis) for n, shape, axis in self.group if n == nm)
        cuts = [g[:, k * N:(k + 1) * N] if axis == 1 else g[k * K:(k + 1) * K, :] for k in range(N_CHIPS)]
        return jnp.stack([self._rows_of(c.astype(BF16)) for c in cuts])

    def slabs(self, grads):
        parts = [self.slab_rows(nm, grads[nm]) for nm, _, _ in self.group]
        return jnp.concatenate(parts + [jnp.zeros((N_CHIPS, self.R - self.used, self.C), BF16)], axis=1)

    def full(self, gathered, names=None):
        res = {}
        for nm, (K, N), axis in self.group:
            if names is None or nm in names:
                rows = gathered[:, self.offs[nm]:self.offs[nm] + self.rows[nm]]
                res[nm] = jnp.concatenate([self._shard_of(rows[k], (K, N)) for k in range(N_CHIPS)], axis=axis)
        return res


def _rope_tables(S):
    pos = jnp.arange(S, dtype=F32)
    inv = 1.0 / (ROPE_THETA ** (jnp.arange(0, MLA_ROPE, 2, dtype=F32) / MLA_ROPE))
    ang = pos[:, None] * inv[None, :]
    cos, sin = jnp.cos(ang), jnp.sin(ang)
    half = MLA_ROPE // 2
    z = jnp.zeros((S, half), F32)
    one = jnp.ones((S, LANE - MLA_ROPE), F32)
    zero = jnp.zeros((S, LANE - MLA_ROPE), F32)
    kc = jnp.concatenate([cos, cos, one], axis=1)
    ksa = jnp.concatenate([-sin, z, zero], axis=1)
    ksb = jnp.concatenate([z, sin, zero], axis=1)
    qc = jnp.concatenate([jnp.ones((S, MLA_NOPE), F32), kc], axis=1)
    qsa = jnp.concatenate([jnp.zeros((S, MLA_NOPE), F32), ksa], axis=1)
    qsb = jnp.concatenate([jnp.zeros((S, MLA_NOPE), F32), ksb], axis=1)
    return (kc, ksa, ksb), (qc, qsa, qsb)


def _pad_cols(a, width):
    return jnp.pad(a, ((0, 0), (0, width - a.shape[1])))


def kernel(x, attn_norm, w_in, fox_f_bias, q_norm, w_uq, kv_norm, w_ukv, w_mla_branch, w_fox_branch, w_out, mlp_norm, w_up, w_down, final_norm, loss_target, m_attn_norm, m_w_in, m_fox_f_bias, m_q_norm, m_w_uq, m_kv_norm, m_w_ukv, m_w_mla_branch, m_w_fox_branch, m_w_out, m_mlp_norm, m_w_up, m_w_down, m_final_norm, v_attn_norm, v_w_in, v_fox_f_bias, v_q_norm, v_w_uq, v_kv_norm, v_w_ukv, v_w_mla_branch, v_w_fox_branch, v_w_out, v_mlp_norm, v_w_up, v_w_down, v_final_norm):
    _, S, D = x.shape
    H, HF = MLA_HEADS, FOX_HEADS
    QL, KVL = MLA_Q_LORA, MLA_KV_LORA
    assert H == HF and H <= 8
    xs = x[0]
    target = loss_target[0]
    C = D
    chip = 2 * lax.axis_index("x") + lax.axis_index("y")

    def flip(a):
        return jnp.transpose(a, (0, 2, 1))

    w_in, m_w_in, v_w_in = flip(w_in), flip(m_w_in), flip(v_w_in)
    weights = {"attn_norm": attn_norm, "w_in": w_in, "fox_f_bias": fox_f_bias, "q_norm": q_norm, "w_uq": w_uq,
               "kv_norm": kv_norm, "w_ukv": w_ukv, "w_mla_branch": w_mla_branch, "w_fox_branch": w_fox_branch,
               "w_out": w_out, "mlp_norm": mlp_norm, "w_up": w_up, "w_down": w_down, "final_norm": final_norm}
    moments = {"attn_norm": (m_attn_norm, v_attn_norm), "w_in": (m_w_in, v_w_in), "fox_f_bias": (m_fox_f_bias, v_fox_f_bias),
               "q_norm": (m_q_norm, v_q_norm), "w_uq": (m_w_uq, v_w_uq), "kv_norm": (m_kv_norm, v_kv_norm),
               "w_ukv": (m_w_ukv, v_w_ukv), "w_mla_branch": (m_w_mla_branch, v_w_mla_branch),
               "w_fox_branch": (m_w_fox_branch, v_w_fox_branch), "w_out": (m_w_out, v_w_out),
               "mlp_norm": (m_mlp_norm, v_mlp_norm), "w_up": (m_w_up, v_w_up), "w_down": (m_w_down, v_w_down),
               "final_norm": (m_final_norm, v_final_norm)}

    def group(names_axes):
        return [(nm, weights[nm].shape[1:], axis) for nm, axis in names_axes]

    pack_a = _Pack(group([("w_in", 0), ("w_uq", 1), ("w_ukv", 1)]), C)
    pack_b = _Pack(group([("w_down", 0), ("w_up", 1), ("w_out", 0), ("w_mla_branch", 1), ("w_fox_branch", 1)]), C)
    RA, RB = pack_a.R, pack_b.R
    wp_b = pack_b.pack({nm: weights[nm][0] for nm, _, _ in pack_b.group})
    n_in = w_in.shape[1]
    rows_in = -(-n_in // 16) * 16
    assert pack_a.offs["w_in"] == 0 and w_in.shape[2] == C
    assert all((k * n_in) % 16 + n_in <= rows_in for k in range(N_CHIPS))
    shifted = lax.dynamic_update_slice(jnp.zeros((rows_in, C), BF16), w_in[0].astype(BF16), ((chip * n_in) % 16, 0))
    wp_a = jnp.concatenate([shifted] + [pack_a._rows_of(weights[nm][0].astype(BF16)) for nm, _, _ in pack_a.group[1:]]
                           + [jnp.zeros((RA - pack_a.used, C), BF16)], axis=0)
    ag_a = _xchg_start(wp_a, lax.empty((N_CHIPS, RA, C), BF16), "half", jnp.zeros((8, LANE), F32), "all_gather_start_a")
    xn = _norm_fwd(xs, attn_norm, "attn_norm_fwd", order=ag_a[3])
    own_a, land_a = _xchg_wait(ag_a, "half", (xn, wp_b), "all_gather_wait_a")
    land_a = _forward_halves(land_a, "all_gather_forward_a")
    gathered_a = lax.dynamic_update_slice(land_a, own_a[None], (chip, 0, 0))
    ag_b = _xchg_start(wp_b, lax.empty((N_CHIPS, RB, C), BF16), True, gathered_a, "all_gather_start_b")
    full = pack_a.full(gathered_a, ("w_uq", "w_ukv"))
    tile0 = [(k * n_in) // 16 * 16 for k in range(N_CHIPS)]
    total = tile0[-1] + rows_in
    full["w_in"] = sum(jnp.pad(gathered_a[k, :rows_in], ((tile0[k], total - tile0[k] - rows_in), (0, 0)))
                       for k in range(N_CHIPS))

    o_ckv = QL
    o_kr = o_ckv + KVL
    o_fq = o_kr + MLA_ROPE
    o_ff = o_fq + 3 * HF * FOX_HEAD_DIM
    o_g = o_ff + HF
    wi = full["w_in"]
    assert N_CHIPS * n_in == o_g + 2 * D and wi.shape[0] >= o_g + 2 * D
    WS = QL + KVL + 2 * LANE
    NQKV = 3 * HF * FOX_HEAD_DIM

    def pad_rows(a, rows):
        return jnp.pad(a, ((0, rows - a.shape[0]), (0, 0)))

    w_small = jnp.concatenate([wi[:o_kr], pad_rows(wi[o_kr:o_fq], LANE), pad_rows(wi[o_ff:o_g], LANE)], axis=0)
    w_qkv = wi[o_fq:o_ff]
    w_g = wi[o_g:o_g + 2 * D]
    w_pack = jnp.concatenate([w_small, w_qkv, w_g], axis=0)
    dqk = MLA_NOPE + MLA_ROPE
    w_uq_p = jnp.pad(full["w_uq"].reshape(QL, H, dqk), ((0, 0), (0, 0), (0, QPAD - dqk))).reshape(QL, H * QPAD)
    ukv = full["w_ukv"].reshape(KVL, H, MLA_NOPE + MLA_V)
    w_ukv_p = jnp.concatenate([ukv[:, :, :MLA_NOPE].reshape(KVL, H * MLA_NOPE),
                               ukv[:, :, MLA_NOPE:].reshape(KVL, H * MLA_V)], axis=1)

    (kc, ksa, ksb), (qc, qsa, qsb) = _rope_tables(S)
    bias_pad = _pad_cols(fox_f_bias, LANE)

    small = _matmul(xn, w_small, "nt", [F32], "proj_small")
    n_fq = HF * FOX_HEAD_DIM
    q_scale = jnp.concatenate([jnp.full((1, n_fq), LOG2E / math.sqrt(FOX_HEAD_DIM), F32),
                               jnp.ones((1, NQKV - n_fq), F32)], axis=1)
    qkv = _matmul(xn, w_qkv, "nt", [BF16], "proj_qkv", col_extras=(q_scale,), epilogue=lambda acc, cs: (acc * cs,))
    gpre = _matmul(xn, w_g, "nt", [BF16], "proj_gates")
    cqn, ckvn, kr, cum = _prep_fwd(small, q_norm, kv_norm, bias_pad, kc, ksa, ksb, HF, "prep_fwd")
    c2_mla = LOG2E / math.sqrt(dqk)
    q_rot = _matmul(cqn, w_uq_p, "nn", [BF16], "mla_q_up", tm=2048, tn=QPAD, row_extras=(qc * c2_mla, qsa * c2_mla, qsb * c2_mla),
                    epilogue=lambda acc, c, sa, sb: (_rope(acc, c, sa, sb, 1),))
    kv2 = _matmul(ckvn, w_ukv_p, "nn", [BF16], "mla_kv_up")

    def mla_att(qsub):
        return _AttT(S, H, (q_rot, QPAD, 0, True), [(kv2, MLA_NOPE, 0, True), (kr, LANE, 0, False)],
                     (kv2, MLA_V, H, True), 1.0 / math.sqrt(dqk), True, qsub=qsub)

    mla = mla_att(QSUB)
    o_mla, lse_mla = _att_fwd_t(mla_att(2 * QSUB), "mla_att_fwd")

    cum_t = jnp.transpose(cum[:, :HF]) * LOG2E
    cum_rep = jnp.broadcast_to(cum_t[:, :, None], (HF, S, min(QSUB, _tile(S, ATT_T))))
    fox = _AttT(S, HF, (qkv, FOX_HEAD_DIM, 0, True), [(qkv, FOX_HEAD_DIM, HF, True)],
                (qkv, FOX_HEAD_DIM, 2 * HF, True), 1.0 / math.sqrt(FOX_HEAD_DIM), False, cum_rep)
    o_fox, ox_fox, lse_fox = _att_fwd_t(fox, "fox_att_fwd", exact=True)

    own_b, land_b = _xchg_wait(ag_b, True, (lse_fox, lse_mla, gpre), "all_gather_wait_b")
    gathered_b = lax.dynamic_update_slice(land_b, own_b[None], (chip, 0, 0))
    full.update(pack_b.full(gathered_b, ("w_mla_branch", "w_fox_branch", "w_out")))
    w_mb, w_fb, w_o = (full[n] for n in ("w_mla_branch", "w_fox_branch", "w_out"))

    def b_of(nm, mode, tn, tk):
        (K, N), axis = next((shape, axis) for n, shape, axis in pack_b.group if n == nm)
        off = pack_b.offs[nm]
        shape = (N_CHIPS * K, N) if axis == 0 else (K, N_CHIPS * N)
        t_r, t_c = (tk, tn) if mode == "nn" else (tn, tk)
        t_r, t_c = _tile(shape[0], t_r), _tile(shape[1], t_c)
        if not (N == C and K % t_r == 0 and N % t_c == 0 and off % t_r == 0):
            return pack_b.full(gathered_b, (nm,))[nm], None
        base = off // t_r
        if axis == 0:
            per = K // t_r
            place = lambda rb, cb: (rb // per, base + rb % per, cb)
        else:
            per = N // t_c
            place = lambda rb, cb: (cb // per, base + rb, cb % per)
        return gathered_b, (shape, (lambda j, k: place(k, j)) if mode == "nn" else (lambda j, k: place(j, k)))

    y_mla = _matmul(o_mla, w_mb, "nn", [BF16], "mla_branch")

    def gate_merge(acc, ga, gb, ya):
        return acc, _sigmoid(ga.astype(F32)) * ya.astype(F32) + _sigmoid(gb.astype(F32)) * acc

    y_fox, merged = _matmul(o_fox, w_fb, "nn", [BF16, BF16], "fox_branch_gates",
                            extras=((gpre, 0), (gpre, 1), y_mla), epilogue=gate_merge)
    h1 = _matmul(merged, w_o, "nn", [F32], "out_proj", extras=(xs,), epilogue=lambda acc, r: (acc + r,))
    hn = _norm_fwd(h1, mlp_norm, "mlp_norm_fwd")

    def relu2(acc):
        a = jnp.maximum(acc, 0.0)
        return a * a, a

    w_u, w_u_in = b_of("w_up", "nn", 1024, 2048)
    u, a_pos = _matmul(hn, w_u, "nn", [BF16, BF16], "mlp_up", epilogue=relu2, b_in=w_u_in)
    w_d, w_d_in = b_of("w_down", "nn", 1024, 2048)
    h2 = _matmul(u, w_d, "nn", [F32], "mlp_down", tn=1024, extras=(h1,), epilogue=lambda acc, r: (acc + r,),
                 b_in=w_d_in)
    dh2, dh2_b, g_final, loss_part = _final(h2, final_norm.reshape(1, D), target, "final_norm_loss")

    gp_b = lax.empty((N_CHIPS, RB, C), BF16)
    by_glue = {}

    def grad_b(nm, a, b, name):
        nonlocal gp_b
        (K, N), axis = next((shape, axis) for n, shape, axis in pack_b.group if n == nm)
        off = pack_b.offs[nm]
        tm = min(1024, K) if axis == 0 else min(1024, a.shape[1])
        tn = min(1024, N) if axis == 1 else min(1024, b.shape[1])
        if not (N == C and tm % LANE == 0 and tn % LANE == 0 and K % tm == 0 and N % tn == 0 and off % tm == 0):
            by_glue[nm] = _mm_tn(a, b, name)
            return
        base = off // tm
        if axis == 0:
            per = K // tm
            place = lambda i, j: (i // per, base + i % per, j)
        else:
            per = N // tn
            place = lambda i, j: (j // per, base + i, j % per)
        gp_b = _mm_tn(a, b, name, tm=tm, tn=tn, into=(gp_b, place))

    w_d, w_d_in = b_of("w_down", "nt", 1024, 2048)
    da = _matmul(dh2_b, w_d, "nt", [BF16], "mlp_down_dx", extras=(a_pos,),
                 epilogue=lambda acc, a: (acc * (2.0 * a.astype(F32)),), b_in=w_d_in)
    grad_b("w_down", u, dh2_b, "mlp_down_dw")
    w_u, w_u_in = b_of("w_up", "nt", 1024, 2048)
    dhn = _matmul(da, w_u, "nt", [F32], "mlp_up_dx", tn=1024, b_in=w_u_in)
    grad_b("w_up", hn, da, "mlp_up_dw")
    dh1, dh1_b, g_mlp_norm = _norm_bwd(h1, dhn, mlp_norm, dh2, "mlp_norm_bwd")

    def gate_bwd(acc, ga, gb, ya, yb):
        ga, gb = _sigmoid(ga.astype(F32)), _sigmoid(gb.astype(F32))
        ya, yb = ya.astype(F32), yb.astype(F32)
        return acc * ga, acc * gb, acc * ya * (ga * (1.0 - ga)), acc * yb * (gb * (1.0 - gb))

    dy_mla, dy_fox, dg_mla, dg_fox = _matmul(dh1_b, w_o, "nt", [BF16] * 4, "out_proj_dx_gates", tn=512,
                                             extras=((gpre, 0), (gpre, 1), y_mla, y_fox), epilogue=gate_bwd)
    grad_b("w_out", merged, dh1_b, "out_proj_dw")
    do_mla = _matmul(dy_mla, w_mb, "nt", [BF16], "mla_branch_dx")
    grad_b("w_mla_branch", o_mla, dy_mla, "mla_branch_dw")
    do_fox = _matmul(dy_fox, w_fb, "nt", [BF16], "fox_branch_dx")
    grad_b("w_fox_branch", o_fox, dy_fox, "fox_branch_dw")
    for nm, g in by_glue.items():
        gp_b = lax.dynamic_update_slice(gp_b, pack_b.slab_rows(nm, g), (0, pack_b.offs[nm], 0))
    if RB > pack_b.used:
        gp_b = lax.dynamic_update_slice(gp_b, jnp.zeros((N_CHIPS, RB - pack_b.used, C), BF16), (0, pack_b.used, 0))

    rs_b = _xchg_start(gp_b, lax.empty((3, RB, C), BF16), False, do_fox, "grad_scatter_start_b")

    dq_rot, dk_nope, dkr_heads, dv_mla = _att_bwd_t(mla, do_mla, lse_mla, o_mla, BF16, [BF16, F32],
                                                    "mla_att_bwd", dq_rope=(qc, qsa, qsb), order=rs_b[3])
    dfq, dfk, dfv, dcum = _att_bwd_t(fox, do_fox, lse_fox, ox_fox, BF16, [BF16], "fox_att_bwd")

    gp_b_sent, recv_b = _xchg_wait(rs_b, False, (dfq, dq_rot), "grad_scatter_wait_b")
    swap_b = _sib_start(_sum_slabs(gp_b_sent, recv_b, chip, "grad_sum_b"), "grad_swap_start_b")

    dcqn = _matmul(dq_rot, w_uq_p, "nt", [F32], "mla_q_up_dx", order=swap_b[4])
    g_w_uq_p = _mm_tn(cqn, dq_rot, "mla_q_up_dw")
    dkv2 = jnp.concatenate([dk_nope, dv_mla], axis=1)
    dckvn = _matmul(dkv2, w_ukv_p, "nt", [F32], "mla_kv_up_dx")
    g_w_ukv_p = _mm_tn(ckvn, dkv2, "mla_kv_up_dw")

    dcum_rows = jnp.pad(dcum[:, :, 0], ((0, 8 - HF), (0, 0)))
    dlogf_rows = _suffix_sum_rows(dcum_rows, "fox_forget_suffix_sum")
    dlogf = _pad_cols(jnp.transpose(dlogf_rows[:HF]), LANE)
    d_small, g_q_norm, g_kv_norm, g_bias = _prep_bwd(
        small, dcqn, dckvn, dkr_heads, dlogf, q_norm, kv_norm, bias_pad, kc, ksa, ksb, H, "prep_bwd")
    dproj = [d_small, dfq, dfk, dfv, dg_mla, dg_fox]
    gs, gfq, gfk, gfv, gg_mla, gg_fox = [
        _matmul(part, xn, "tn", [BF16], "proj_dw_" + tag, tm=1024, tn=1024, tk=2048)
        for part, tag in zip(dproj, ("small", "fq", "fk", "fv", "g_mla", "g_fox"))]

    g_w_in = jnp.concatenate([gs[:o_kr], gs[o_kr:o_kr + MLA_ROPE], gfq, gfk, gfv,
                              gs[o_kr + LANE:o_kr + LANE + HF], gg_mla, gg_fox], axis=0)
    g_w_uq = g_w_uq_p.reshape(QL, H, QPAD)[:, :, :dqk].reshape(QL, H * dqk)
    g_w_ukv = jnp.concatenate([g_w_ukv_p[:, :H * MLA_NOPE].reshape(KVL, H, MLA_NOPE),
                               g_w_ukv_p[:, H * MLA_NOPE:].reshape(KVL, H, MLA_V)], axis=2).reshape(KVL, -1)

    gp_a = pack_a.slabs({"w_in": g_w_in, "w_uq": g_w_uq, "w_ukv": g_w_ukv})
    rs_a = _xchg_start(gp_a, lax.empty((3, RA, C), BF16), False, gg_fox, "grad_scatter_start_a")
    dxn = _matmul_row_parts(dproj, w_pack, F32, "proj_dx", order=rs_a[3])
    grad_x, g_attn_norm = _norm_bwd(xs, dxn, attn_norm, dh1, "attn_norm_bwd", with_bf16=False)
    gp_a_sent, recv_a = _xchg_wait(rs_a, False, grad_x, "grad_scatter_wait_a")
    swap_a = _sib_start(_sum_slabs(gp_a_sent, recv_a, chip, "grad_sum_a"), "grad_swap_start_a")
    vec_w = max(D, LANE)
    vec_rows = [g_attn_norm, g_mlp_norm, g_final, g_q_norm, g_kv_norm, g_bias, loss_part]
    vec = jnp.concatenate([_pad_cols(v, vec_w) for v in vec_rows] + [jnp.zeros((1, vec_w), F32)], axis=0)
    vsum = _all_reduce_vec(vec, "all_reduce_vectors")
    part_b, sib_b = _sib_wait(swap_b, vsum, "grad_swap_wait_b")

    grads, deltas, new_m, new_v = {}, {}, {}, {}

    def update(pack, mine, theirs):
        for nm, shape, _ in pack.group:
            K, N = shape
            if N == pack.C and K % 8 == 0 and pack.offs[nm] % _tile(K, 256, 8) == 0:
                g, d, nm_, nv_ = _adamw(weights[nm], mine, theirs, moments[nm][0], moments[nm][1], "adamw_" + nm,
                                        g_row=pack.offs[nm])
            else:
                g, d, nm_, nv_ = _adamw(weights[nm], pack.part(mine, nm, shape), pack.part(theirs, nm, shape),
                                        moments[nm][0], moments[nm][1], "adamw_" + nm)
            grads[nm], deltas[nm], new_m[nm], new_v[nm] = g, d, nm_, nv_
        return g

    last_b = update(pack_b, part_b, sib_b)
    part_a, sib_a = _sib_wait(swap_a, last_b, "grad_swap_wait_a")
    update(pack_a, part_a, sib_a)

    vec_names = ["attn_norm", "mlp_norm", "final_norm", "q_norm", "kv_norm", "fox_f_bias"]

    def vec_pack(arrs):
        return jnp.concatenate([_pad_cols(a.reshape(1, -1), vec_w) for a in arrs]
                               + [jnp.zeros((2, vec_w), F32)], axis=0)[None]

    vg, vd, vm, vv = _adamw(vec_pack([weights[n] for n in vec_names]), vsum, jnp.zeros_like(vsum),
                            vec_pack([moments[n][0] for n in vec_names]), vec_pack([moments[n][1] for n in vec_names]),
                            "adamw_vectors")
    for r, nm in enumerate(vec_names):
        shp = weights[nm].shape
        n = weights[nm].size
        grads[nm] = vsum[r, :n].reshape(shp)
        deltas[nm], new_m[nm], new_v[nm] = (vd[0, r, :n].reshape(shp), vm[0, r, :n].reshape(shp),
                                            vv[0, r, :n].reshape(shp))
    loss = vsum[6, 0]

    for res in (grads, deltas, new_m, new_v):
        res["w_in"] = flip(res["w_in"])
    order = ["attn_norm", "w_in", "fox_f_bias", "q_norm", "w_uq", "kv_norm", "w_ukv", "w_mla_branch", "w_fox_branch",
             "w_out", "mlp_norm", "w_up", "w_down", "final_norm"]
    return (loss, grad_x[None], *[grads[n] for n in order], *[deltas[n] for n in order],
            *[new_m[n] for n in order], *[new_v[n] for n in order])
```

```python
import math

import jax
import jax.numpy as jnp
from jax import lax
from jax.experimental import pallas as pl
from jax.experimental.pallas import tpu as pltpu

CHUNK = 64
MLA_HEADS = 8
MLA_Q_LORA = 512
MLA_KV_LORA = 256
MLA_NOPE = 128
MLA_ROPE = 64
MLA_V = 128
ROPE_THETA = 10000.0
FOX_HEADS = 8
FOX_HEAD_DIM = 128
EPS = 1e-6

ADAM_LR = 0.001
ADAM_B1 = 0.9
ADAM_B2 = 0.999
ADAM_EPS = 1e-08
ADAM_WD = 0.01
ADAM_STEP = 10

LANE = 128
QPAD = 2 * LANE
N_CHIPS = 4
N_DEV = 8
VMEM_LIMIT = 48 * 1024 * 1024
ATT_T = 2048
QSUB = 256
ROW_T = 256
PACK_ROWS = 256
LOG2E = 1.4426950408889634

BF16 = jnp.bfloat16
F32 = jnp.float32
MESH = pl.DeviceIdType.MESH

_NT = (((1,), (1,)), ((), ()))
_TN = (((0,), (0,)), ((), ()))
_NN = (((1,), (0,)), ((), ()))


def _tile(dim, pref, align=LANE):
    if dim <= pref:
        return dim
    t = (pref // align) * align
    while t >= align:
        if dim % t == 0:
            return t
        t -= align
    return dim


def _params(sem=None):
    return pltpu.CompilerParams(dimension_semantics=sem, vmem_limit_bytes=VMEM_LIMIT)


_ANY_SPEC = pl.BlockSpec(memory_space=pl.ANY)


def _matmul(a, b, mode, out_dtypes, name, *, tm=1024, tn=1024, tk=2048, extras=(), row_extras=(), col_extras=(),
            epilogue=None, order=None, into=None, b_in=None):
    b_shape = b.shape if b_in is None else b_in[0]
    if mode == "nn":
        (M, K), (K2, N) = a.shape, b_shape
    elif mode == "nt":
        (M, K), (N, K2) = a.shape, b_shape
    else:
        (K, M), (K2, N) = a.shape, b_shape
    assert K == K2, (name, a.shape, b_shape)
    tm, tn, tk = _tile(M, tm), _tile(N, tn), _tile(K, tk)
    nk = K // tk
    extras = [e if isinstance(e, tuple) else (e, 0) for e in extras]
    n_out = len(out_dtypes)
    n_ex = len(extras) + len(row_extras) + len(col_extras)
    n_ord = 0 if order is None else 1
    assert all(r.shape == (M, tn) for r in row_extras), name
    dims = {"nn": _NN, "nt": _NT, "tn": _TN}[mode]

    def body(*refs):
        a_ref, b_ref = refs[0], refs[1]
        ex_refs = refs[2:2 + n_ex]
        o_refs = refs[2 + n_ex + n_ord:2 + n_ex + n_ord + n_out]
        acc_ref = refs[2 + n_ex + n_ord + n_out]
        k = pl.program_id(2)
        part = lax.dot_general(a_ref[...], b_ref[...], dims, preferred_element_type=F32)

        @pl.when(k == 0)
        def _():
            acc_ref[...] = part

        @pl.when(k > 0)
        def _():
            acc_ref[...] += part

        @pl.when(k == nk - 1)
        def _():
            acc = acc_ref[...]
            if epilogue is None:
                outs = (acc,)
            else:
                outs = epilogue(acc, *[r[...] for r in ex_refs])
            for o_ref, o in zip(o_refs, outs):
                o_ref[...] = o.astype(o_ref.dtype)

    if mode == "nn":
        a_spec = pl.BlockSpec((tm, tk), lambda i, j, k: (i, k))
        b_spec = pl.BlockSpec((tk, tn), lambda i, j, k: (k, j))
    elif mode == "nt":
        a_spec = pl.BlockSpec((tm, tk), lambda i, j, k: (i, k))
        b_spec = pl.BlockSpec((tn, tk), lambda i, j, k: (j, k))
    else:
        a_spec = pl.BlockSpec((tk, tm), lambda i, j, k: (k, i))
        b_spec = pl.BlockSpec((tk, tn), lambda i, j, k: (k, j))
    if b_in is not None:
        b_block = (None, tn, tk) if mode == "nt" else (None, tk, tn)
        b_spec = pl.BlockSpec(b_block, lambda i, j, k: b_in[1](j, k))
    mn_spec = pl.BlockSpec((tm, tn), lambda i, j, k: (i, j))
    row_spec = pl.BlockSpec((tm, tn), lambda i, j, k: (i, 0))
    col_spec = pl.BlockSpec((1, tn), lambda i, j, k: (0, j))
    out_specs = [mn_spec] * n_out
    out_shape = [jax.ShapeDtypeStruct((M, N), dt) for dt in out_dtypes]
    aliases = {}
    if into is not None:
        buf, place = into
        assert n_out == 1 and n_ord == 1 and order is buf, name
        out_specs = [pl.BlockSpec((None, tm, tn), lambda i, j, k: place(i, j))]
        out_shape = [jax.ShapeDtypeStruct(buf.shape, buf.dtype)]
        aliases = {2 + n_ex: 0}
    outs = pl.pallas_call(
        body,
        name=name,
        grid=(M // tm, N // tn, nk),
        in_specs=([a_spec, b_spec]
                  + [pl.BlockSpec((tm, tn), lambda i, j, k, g=g: (i, j + g * (N // tn))) for _, g in extras]
                  + [row_spec] * len(row_extras) + [col_spec] * len(col_extras) + [_ANY_SPEC] * n_ord),
        out_specs=out_specs,
        out_shape=out_shape,
        scratch_shapes=[pltpu.VMEM((tm, tn), F32)],
        input_output_aliases=aliases,
        compiler_params=_params(("parallel", "parallel", "arbitrary")),
    )(a, b, *[e for e, _ in extras], *row_extras, *col_extras, *([] if order is None else [order]))
    return outs[0] if n_out == 1 else outs


def _matmul_row_parts(parts, b, out_dtype, name, *, tm=512, tk=2048, order=None):
    M, (K, N) = parts[0].shape[0], b.shape
    widths = [p.shape[1] for p in parts]
    assert sum(widths) == K, name
    tm, tk = _tile(M, tm), _tile(K, tk)
    nk = K // tk
    steps, at = [], 0
    for p, w in enumerate(widths):
        off = 0
        while off < w:
            k, room = divmod(at, tk)
            take = min(w - off, tk - room)
            if room == 0:
                steps.append([])
            steps[k].append((p, off, take, room))
            off += take
            at += take
    assert len(steps) == nk and all(t % LANE == 0 and o % LANE == 0 for s in steps for _, o, t, _ in s), name
    n_parts = len(parts)
    n_ord = 0 if order is None else 1

    def body(*refs):
        a_refs = refs[0:n_parts]
        b_ref = refs[n_parts]
        o_ref, acc_ref = refs[n_parts + 1 + n_ord], refs[n_parts + 2 + n_ord]
        k = pl.program_id(1)
        for kk, pieces in enumerate(steps):
            @pl.when(k == kk)
            def _(kk=kk, pieces=pieces):
                part = None
                for p, off, take, room in pieces:
                    d = jnp.dot(a_refs[p][:, off:off + take], b_ref[room:room + take, :], preferred_element_type=F32)
                    part = d if part is None else part + d
                if kk == 0:
                    acc_ref[...] = part
                else:
                    acc_ref[...] += part

        @pl.when(k == nk - 1)
        def _():
            o_ref[...] = acc_ref[...].astype(o_ref.dtype)

    return pl.pallas_call(
        body, name=name, grid=(M // tm, nk),
        in_specs=[pl.BlockSpec((tm, w), lambda i, k: (i, 0)) for w in widths]
        + [pl.BlockSpec((tk, N), lambda i, k: (k, 0))] + [_ANY_SPEC] * n_ord,
        out_specs=pl.BlockSpec((tm, N), lambda i, k: (i, 0)),
        out_shape=jax.ShapeDtypeStruct((M, N), out_dtype),
        scratch_shapes=[pltpu.VMEM((tm, N), F32)],
        compiler_params=_params(("parallel", "arbitrary")),
    )(*parts, b, *([] if order is None else [order]))


def _mm_tn(a, b, name, tm=1024, tn=1024, into=None):
    return _matmul(a, b, "tn", [F32], name, tm=tm, tn=tn, tk=2048, into=into,
                   order=None if into is None else into[0])


def _row_spec(ts, width, col=0):
    return pl.BlockSpec((ts, width), lambda i: (i, col))


def _full_spec(shape):
    return pl.BlockSpec(shape, lambda i: tuple(0 for _ in shape))


def _rms(x):
    return lax.rsqrt(jnp.mean(x * x, axis=-1, keepdims=True) + EPS)


def _rms_bwd(x, dy, g):
    r = _rms(x)
    xh = x * r
    gy = dy * g
    dx = r * (gy - xh * jnp.mean(xh * gy, axis=-1, keepdims=True))
    return dx, dy * xh


def _norm_fwd(x, g, name, order=None):
    S, D = x.shape
    ts = _tile(S, ROW_T, 8)

    def body(x_ref, g_ref, *rest):
        o_ref = rest[-1]
        xv = x_ref[...]
        o_ref[...] = ((xv * _rms(xv)) * g_ref[...]).astype(BF16)

    extra = [] if order is None else [order]
    return pl.pallas_call(
        body, name=name, grid=(S // ts,),
        in_specs=[_row_spec(ts, D), _full_spec((1, D))] + [_ANY_SPEC] * len(extra),
        out_specs=_row_spec(ts, D),
        out_shape=jax.ShapeDtypeStruct((S, D), BF16),
        compiler_params=_params(("parallel",)),
    )(x, g, *extra)


def _norm_bwd(x, dy, g, dres, name, with_bf16=True):
    S, D = x.shape
    ts = _tile(S, ROW_T, 8)

    def body(x_ref, dy_ref, g_ref, dres_ref, dx_ref, *rest):
        dg_ref = rest[-1]
        dx, dg_rows = _rms_bwd(x_ref[...], dy_ref[...], g_ref[...])
        dx = dres_ref[...] + dx
        dx_ref[...] = dx
        if with_bf16:
            rest[0][...] = dx.astype(BF16)

        @pl.when(pl.program_id(0) == 0)
        def _():
            dg_ref[...] = jnp.zeros_like(dg_ref)

        dg_ref[...] += jnp.sum(dg_rows, axis=0, keepdims=True)

    return pl.pallas_call(
        body, name=name, grid=(S // ts,),
        in_specs=[_row_spec(ts, D), _row_spec(ts, D), _full_spec((1, D)), _row_spec(ts, D)],
        out_specs=[_row_spec(ts, D)] * (2 if with_bf16 else 1) + [_full_spec((1, D))],
        out_shape=([jax.ShapeDtypeStruct((S, D), F32)] + [jax.ShapeDtypeStruct((S, D), BF16)] * with_bf16
                   + [jax.ShapeDtypeStruct((1, D), F32)]),
        compiler_params=_params(("arbitrary",)),
    )(x, dy, g, dres)


def _rope(x, c, sa, sb, sign):
    w = x.shape[-1]
    half = MLA_ROPE // 2
    fwd = pltpu.roll(x, w - half, 1)
    back = pltpu.roll(x, half, 1)
    if sign < 0:
        return x * c - fwd * sa - back * sb
    return x * c + fwd * sa + back * sb


def _split3(x):
    hi = x.astype(BF16)
    r1 = x - hi.astype(F32)
    mid = r1.astype(BF16)
    lo = (r1 - mid.astype(F32)).astype(BF16)
    return hi, mid, lo


def _prep_fwd(small, q_norm, kv_norm, bias_pad, kc, ksa, ksb, n_heads, name):
    S, W = small.shape
    QL, KVL = q_norm.shape[1], kv_norm.shape[1]
    assert W == QL + KVL + 2 * LANE
    ts = _tile(S, ROW_T, 8)
    tri = (lax.broadcasted_iota(jnp.int32, (ts, ts), 0) >= lax.broadcasted_iota(jnp.int32, (ts, ts), 1)).astype(BF16)

    def body(s_ref, qn_ref, kvn_ref, b_ref, kc_ref, ksa_ref, ksb_ref, tri_ref,
             cqn_ref, ckvn_ref, kr_ref, cum_ref, carry_ref):
        cq = s_ref[:, 0:QL]
        cqn_ref[...] = ((cq * _rms(cq)) * qn_ref[...]).astype(BF16)
        ckv = s_ref[:, QL:QL + KVL]
        ckvn_ref[...] = ((ckv * _rms(ckv)) * kvn_ref[...]).astype(BF16)
        kr = s_ref[:, QL + KVL:QL + KVL + LANE]
        kr_ref[...] = _rope(kr, kc_ref[...], ksa_ref[...], ksb_ref[...], 1).astype(BF16)
        z = s_ref[:, QL + KVL + LANE:W] + b_ref[...]
        logf = jnp.minimum(z, 0.0) - jnp.log1p(jnp.exp(-jnp.abs(z)))
        lane = lax.broadcasted_iota(jnp.int32, logf.shape, 1)
        logf = jnp.where(lane < n_heads, logf, 0.0)

        @pl.when(pl.program_id(0) == 0)
        def _():
            carry_ref[...] = jnp.zeros_like(carry_ref)

        t = tri_ref[...]
        cum = carry_ref[...]
        for part in _split3(logf):
            cum = cum + jnp.dot(t, part, preferred_element_type=F32)
        cum_ref[...] = cum
        carry_ref[...] = cum[ts - 1:ts, :]

    return pl.pallas_call(
        body, name=name, grid=(S // ts,),
        in_specs=[_row_spec(ts, W), _full_spec((1, QL)), _full_spec((1, KVL)), _full_spec((1, LANE)),
                  _row_spec(ts, LANE), _row_spec(ts, LANE), _row_spec(ts, LANE), _full_spec((ts, ts))],
        out_specs=[_row_spec(ts, QL), _row_spec(ts, KVL), _row_spec(ts, LANE), _row_spec(ts, LANE)],
        out_shape=[jax.ShapeDtypeStruct((S, QL), BF16), jax.ShapeDtypeStruct((S, KVL), BF16),
                   jax.ShapeDtypeStruct((S, LANE), BF16), jax.ShapeDtypeStruct((S, LANE), F32)],
        scratch_shapes=[pltpu.VMEM((1, LANE), F32)],
        compiler_params=_params(("arbitrary",)),
    )(small, q_norm, kv_norm, bias_pad, kc, ksa, ksb, tri)


def _prep_bwd(small, dcqn, dckvn, dkr_heads, dlogf, q_norm, kv_norm, bias_pad, kc, ksa, ksb, n_heads, name):
    S, W = small.shape
    QL, KVL = q_norm.shape[1], kv_norm.shape[1]
    ts = _tile(S, ROW_T, 8)

    def body(s_ref, dcq_ref, dckv_ref, dkr_ref, dlf_ref, qn_ref, kvn_ref, b_ref, kc_ref, ksa_ref, ksb_ref,
             ds_ref, gq_ref, gkv_ref, gb_ref):
        dcq, gq_rows = _rms_bwd(s_ref[:, 0:QL], dcq_ref[...], qn_ref[...])
        ds_ref[:, 0:QL] = dcq.astype(BF16)
        dckv, gkv_rows = _rms_bwd(s_ref[:, QL:QL + KVL], dckv_ref[...], kvn_ref[...])
        ds_ref[:, QL:QL + KVL] = dckv.astype(BF16)
        dkr = dkr_ref[:, 0:LANE]
        for h in range(1, n_heads):
            dkr = dkr + dkr_ref[:, h * LANE:(h + 1) * LANE]
        ds_ref[:, QL + KVL:QL + KVL + LANE] = _rope(dkr, kc_ref[...], ksa_ref[...], ksb_ref[...], -1).astype(BF16)
        z = s_ref[:, QL + KVL + LANE:W] + b_ref[...]
        dff = dlf_ref[...] * (1.0 / (1.0 + jnp.exp(z)))
        ds_ref[:, QL + KVL + LANE:W] = dff.astype(BF16)

        @pl.when(pl.program_id(0) == 0)
        def _():
            gq_ref[...] = jnp.zeros_like(gq_ref)
            gkv_ref[...] = jnp.zeros_like(gkv_ref)
            gb_ref[...] = jnp.zeros_like(gb_ref)

        gq_ref[...] += jnp.sum(gq_rows, axis=0, keepdims=True)
        gkv_ref[...] += jnp.sum(gkv_rows, axis=0, keepdims=True)
        gb_ref[...] += jnp.sum(dff, axis=0, keepdims=True)

    return pl.pallas_call(
        body, name=name, grid=(S // ts,),
        in_specs=[_row_spec(ts, W), _row_spec(ts, QL), _row_spec(ts, KVL), _row_spec(ts, n_heads * LANE),
                  _row_spec(ts, LANE), _full_spec((1, QL)), _full_spec((1, KVL)), _full_spec((1, LANE)),
                  _row_spec(ts, LANE), _row_spec(ts, LANE), _row_spec(ts, LANE)],
        out_specs=[_row_spec(ts, W), _full_spec((1, QL)), _full_spec((1, KVL)), _full_spec((1, LANE))],
        out_shape=[jax.ShapeDtypeStruct((S, W), BF16), jax.ShapeDtypeStruct((1, QL), F32),
                   jax.ShapeDtypeStruct((1, KVL), F32), jax.ShapeDtypeStruct((1, LANE), F32)],
        compiler_params=_params(("arbitrary",)),
    )(small, dcqn, dckvn, dkr_heads, dlogf, q_norm, kv_norm, bias_pad, kc, ksa, ksb)


def _sigmoid(z):
    return 1.0 / (1.0 + jnp.exp(-z))


def _final(h, g, target, name):
    S, D = h.shape
    ts = _tile(S, ROW_T, 8)

    def body(h_ref, g_ref, t_ref, dh_ref, dhb_ref, dg_ref, loss_ref):
        hv = h_ref[...]
        gv = g_ref[...]
        err = (hv * _rms(hv)) * gv - t_ref[...]
        dh, dg_rows = _rms_bwd(hv, err / D, gv)
        dh_ref[...] = dh
        dhb_ref[...] = dh.astype(BF16)

        @pl.when(pl.program_id(0) == 0)
        def _():
            dg_ref[...] = jnp.zeros_like(dg_ref)
            loss_ref[...] = jnp.zeros_like(loss_ref)

        dg_ref[...] += jnp.sum(dg_rows, axis=0, keepdims=True)
        row_loss = jnp.mean(err * err, axis=-1, keepdims=True)
        loss_ref[...] += 0.5 * jnp.sum(row_loss, axis=0, keepdims=True)

    return pl.pallas_call(
        body, name=name, grid=(S // ts,),
        in_specs=[_row_spec(ts, D), _full_spec((1, D)), _row_spec(ts, D)],
        out_specs=[_row_spec(ts, D), _row_spec(ts, D), _full_spec((1, D)), _full_spec((1, LANE))],
        out_shape=[jax.ShapeDtypeStruct((S, D), F32), jax.ShapeDtypeStruct((S, D), BF16),
                   jax.ShapeDtypeStruct((1, D), F32), jax.ShapeDtypeStruct((1, LANE), F32)],
        compiler_params=_params(("arbitrary",)),
    )(h, g, target)


def _suffix_sum_rows(x, name):
    R, S = x.shape
    tb = _tile(S, 512)
    nb = S // tb
    tri = (lax.broadcasted_iota(jnp.int32, (tb, tb), 0) >= lax.broadcasted_iota(jnp.int32, (tb, tb), 1)).astype(BF16)

    def body(x_ref, tri_ref, o_ref, carry_ref):
        @pl.when(pl.program_id(0) == 0)
        def _():
            carry_ref[...] = jnp.zeros_like(carry_ref)

        xv = x_ref[...]
        t = tri_ref[...]
        acc = jnp.broadcast_to(carry_ref[:, 0:1], xv.shape)
        for part in _split3(xv):
            acc = acc + jnp.dot(part, t, preferred_element_type=F32)
        o_ref[...] = acc
        carry_ref[...] = jnp.broadcast_to(acc[:, 0:1], carry_ref.shape)

    rev = pl.BlockSpec((R, tb), lambda i: (0, nb - 1 - i))
    return pl.pallas_call(
        body, name=name, grid=(nb,),
        in_specs=[rev, _full_spec((tb, tb))], out_specs=rev,
        out_shape=jax.ShapeDtypeStruct((R, S), F32),
        scratch_shapes=[pltpu.VMEM((R, LANE), F32)],
        compiler_params=_params(("arbitrary",)),
    )(x, tri)


def _pairs(nb, by_key):
    if by_key:
        pr = [(i, j) for j in range(nb) for i in range(j, nb)]
    else:
        pr = [(i, j) for i in range(nb) for j in range(i + 1)]
    return (jnp.asarray([p[0] for p in pr], jnp.int32), jnp.asarray([p[1] for p in pr], jnp.int32), len(pr))


class _AttT:
    def __init__(self, S, n_heads, q, ks, v, scale, chunk_causal, cum_rep=None, qsub=None):
        self.S, self.H, self.q, self.ks, self.v = S, n_heads, q, ks, v
        self.scale, self.chunk_causal, self.cum_rep = scale, chunk_causal, cum_rep
        self.T = _tile(S, ATT_T)
        self.qs = min(qsub or QSUB, self.T)
        self.nb = S // self.T
        self.dq, self.dv = q[1], v[1]
        self.has_bias = cum_rep is not None

    def q_spec(self, op):
        _, w, off, per_head = op
        return pl.BlockSpec((self.T, w), lambda h, p, it, jt: (it[p], off + (h if per_head else 0)))

    def k_spec(self, op):
        _, w, off, per_head = op
        return pl.BlockSpec((self.T, w), lambda h, p, it, jt: (jt[p], off + (h if per_head else 0)))

    def row_q(self):
        return pl.BlockSpec((None, 1, self.T), lambda h, p, it, jt: (h, 0, it[p]))

    def cum_k(self):
        return pl.BlockSpec((None, self.T, self.qs), lambda h, p, it, jt: (h, jt[p], 0))

    def sub_blocks(self, masked):
        return [(q0, min(self.T, q0 + self.qs) if masked else self.T) for q0 in range(0, self.T, self.qs)]

    def scores(self, k, q_sub, cum, q0, masked):
        s = lax.dot_general(k, q_sub, _NT, preferred_element_type=F32)
        if self.has_bias:
            s = s - cum
        mask = None
        if masked:
            r = lax.broadcasted_iota(jnp.int32, s.shape, 0)
            c = lax.broadcasted_iota(jnp.int32, s.shape, 1) + q0
            mask = (r // CHUNK <= c // CHUNK) if self.chunk_causal else (r <= c)
        return s, mask


def _join(k_refs):
    return k_refs[0][...] if len(k_refs) == 1 else jnp.concatenate([r[...] for r in k_refs], axis=-1)


def _att_fwd_t(att, name, exact=False):
    S, H, T, qs = att.S, att.H, att.T, att.qs
    it, jt, npairs = _pairs(att.nb, by_key=False)
    nk = len(att.ks)

    def body(it_ref, jt_ref, *refs):
        q_ref = refs[0]
        k_refs = refs[1:1 + nk]
        v_ref = refs[1 + nk]
        n = 2 + nk
        cum_ref = None
        if att.has_bias:
            cum_ref = refs[n]
            n += 1
        o_ref = refs[n]
        n += 1
        ox_ref = None
        if exact:
            ox_ref = refs[n]
            n += 1
        lse_ref, m_ref, l_ref, acc_ref = refs[n:n + 4]
        lo_ref = refs[n + 4] if exact else None
        p = pl.program_id(1)
        i, j = it_ref[p], jt_ref[p]

        @pl.when(j == 0)
        def _():
            m_ref[...] = jnp.full_like(m_ref, -jnp.inf)
            l_ref[...] = jnp.zeros_like(l_ref)
            acc_ref[...] = jnp.zeros_like(acc_ref)
            if exact:
                lo_ref[...] = jnp.zeros_like(lo_ref)

        def step(masked):
            k = _join(k_refs)
            v = v_ref[...]
            subs = att.sub_blocks(masked)

            def logits(idx):
                q0, nkeys = subs[idx]
                cum = cum_ref[0:nkeys, :] if att.has_bias else None
                return att.scores(k[0:nkeys], q_ref[q0:q0 + qs, :], cum, q0, masked)

            ahead = logits(0)
            for idx, (q0, nkeys) in enumerate(subs):
                qsl = slice(q0, q0 + qs)
                s, mask = ahead
                if idx + 1 < len(subs):
                    ahead = logits(idx + 1)
                if masked:
                    s = jnp.where(mask, s, -jnp.inf)
                m_prev = m_ref[:, qsl]
                m_new = jnp.maximum(m_prev, jnp.max(s, axis=0, keepdims=True))
                alpha = jnp.exp2(m_prev - m_new)
                pr = jnp.exp2(s - m_new)
                l_ref[:, qsl] = alpha * l_ref[:, qsl] + jnp.sum(pr, axis=0, keepdims=True)
                p_hi = pr.astype(BF16)
                acc_ref[:, qsl] = alpha * acc_ref[:, qsl] + lax.dot_general(
                    v[0:nkeys], p_hi, _TN, preferred_element_type=F32)
                if exact:
                    p_lo = (pr - p_hi.astype(F32)).astype(BF16)
                    lo_ref[:, qsl] = alpha * lo_ref[:, qsl] + lax.dot_general(
                        v[0:nkeys], p_lo, _TN, preferred_element_type=F32)
                m_ref[:, qsl] = m_new

        @pl.when(j < i)
        def _():
            step(False)

        @pl.when(j == i)
        def _():
            step(True)
            l = l_ref[...]
            inv = 1.0 / l
            o_ref[...] = jnp.transpose(acc_ref[...] * inv).astype(o_ref.dtype)
            if exact:
                ox_ref[...] = jnp.transpose((acc_ref[...] + lo_ref[...]) * inv)
            lse_ref[...] = m_ref[...] + jnp.log2(l)

    in_specs = [att.q_spec(att.q)] + [att.k_spec(k) for k in att.ks] + [att.k_spec(att.v)]
    args = [att.q[0]] + [k[0] for k in att.ks] + [att.v[0]]
    if att.has_bias:
        in_specs.append(att.cum_k())
        args.append(att.cum_rep)
    o_spec = pl.BlockSpec((T, att.dv), lambda h, p, it, jt: (it[p], h))
    out_specs = [o_spec]
    out_shape = [jax.ShapeDtypeStruct((S, H * att.dv), BF16)]
    scratch = [pltpu.VMEM((1, T), F32), pltpu.VMEM((1, T), F32), pltpu.VMEM((att.dv, T), F32)]
    if exact:
        out_specs.append(o_spec)
        out_shape.append(jax.ShapeDtypeStruct((S, H * att.dv), F32))
        scratch.append(pltpu.VMEM((att.dv, T), F32))
    out_specs.append(att.row_q())
    out_shape.append(jax.ShapeDtypeStruct((H, 1, S), F32))
    return pl.pallas_call(
        body, name=name,
        grid_spec=pltpu.PrefetchScalarGridSpec(
            num_scalar_prefetch=2, grid=(H, npairs), in_specs=in_specs, out_specs=out_specs,
            scratch_shapes=scratch),
        out_shape=out_shape,
        compiler_params=_params(("parallel", "arbitrary")),
    )(it, jt, *args)


def _att_bwd_t(att, do, lse, o, dq_dtype, dk_dtypes, name, dq_rope=None, order=None):
    S, H, T, qs = att.S, att.H, att.T, att.qs
    it, jt, npairs = _pairs(att.nb, by_key=True)
    nk = len(att.ks)
    last = att.nb - 1
    widths = [k[1] for k in att.ks]

    def body(it_ref, jt_ref, *refs):
        q_ref = refs[0]
        k_refs = refs[1:1 + nk]
        v_ref, do_ref, lse_ref, o_ref = refs[1 + nk:5 + nk]
        n = 5 + nk
        cum_ref = None
        if att.has_bias:
            cum_ref = refs[n]
            n += 1
        rope_refs = None
        if dq_rope is not None:
            rope_refs = refs[n:n + 3]
            n += 3
        if order is not None:
            n += 1
        dl_acc = refs[-1]
        dq_ref = refs[n]
        dk_refs = refs[n + 1:n + 1 + nk]
        dv_ref = refs[n + 1 + nk]
        n += nk + 2
        dc_ref = None
        if att.has_bias:
            dc_ref = refs[n]
            n += 1
        dq_acc, dk_acc, dv_acc = refs[n:n + 3]
        dc_acc = refs[n + 3] if att.has_bias else None
        p = pl.program_id(1)
        i, j = it_ref[p], jt_ref[p]

        @pl.when(p == 0)
        def _():
            dq_acc[...] = jnp.zeros_like(dq_acc)

        @pl.when(i == j)
        def _():
            dk_acc[...] = jnp.zeros_like(dk_acc)
            dv_acc[...] = jnp.zeros_like(dv_acc)
            if att.has_bias:
                dc_acc[...] = jnp.zeros_like(dc_acc)

        @pl.when(j == 0)
        def _():
            prod = do_ref[...].astype(F32) * o_ref[...].astype(F32)
            ones = jnp.ones((8, att.dv), BF16)
            rows = jnp.zeros((8, T), F32)
            for part in _split3(prod):
                rows = rows + lax.dot_general(ones, part, _NT, preferred_element_type=F32)
            dl_acc[i] = rows[0:1, :]

        def step(masked):
            k = _join(k_refs)
            v = v_ref[...]
            dl = dl_acc[i]
            subs = att.sub_blocks(masked)

            def logits(idx):
                q0, nkeys = subs[idx]
                cum = cum_ref[0:nkeys, :] if att.has_bias else None
                return att.scores(k[0:nkeys], q_ref[q0:q0 + qs, :], cum, q0, masked)

            ahead = logits(0)
            for idx, (q0, nkeys) in enumerate(subs):
                qsl = slice(q0, q0 + qs)
                ksl = slice(0, nkeys)
                q_sub = q_ref[qsl, :]
                do_sub = do_ref[qsl, :]
                s, mask = ahead
                if idx + 1 < len(subs):
                    ahead = logits(idx + 1)
                pr = jnp.exp2(s - lse_ref[:, qsl])
                if masked:
                    pr = jnp.where(mask, pr, 0.0)
                dp = lax.dot_general(v[ksl], do_sub, _NT, preferred_element_type=F32)
                ds = pr * (dp - dl[:, qsl])
                ds_b = ds.astype(BF16)
                dv_acc[ksl, :] += jnp.dot(pr.astype(BF16), do_sub, preferred_element_type=F32)
                dk_acc[ksl, :] += jnp.dot(ds_b, q_sub, preferred_element_type=F32)
                dq_acc[i, :, qsl] += lax.dot_general(k[ksl], ds_b, _TN, preferred_element_type=F32)
                if att.has_bias:
                    part = ds[:, 0:LANE] if qs >= LANE else ds
                    for c0 in range(LANE, qs, LANE):
                        part = part + ds[:, c0:c0 + LANE]
                    dc_acc[ksl, :] += part

        @pl.when(i > j)
        def _():
            step(False)

        @pl.when(i == j)
        def _():
            step(True)
            dq = jnp.transpose(dq_acc[i] * att.scale)
            if dq_rope is not None:
                dq = _rope(dq, rope_refs[0][...], rope_refs[1][...], rope_refs[2][...], -1)
            dq_ref[...] = dq.astype(dq_ref.dtype)

        @pl.when(i == last)
        def _():
            dk = dk_acc[...] * (1.0 / LOG2E)
            off = 0
            for r, w in zip(dk_refs, widths):
                r[...] = dk[:, off:off + w].astype(r.dtype)
                off += w
            dv_ref[...] = dv_acc[...].astype(dv_ref.dtype)
            if att.has_bias:
                dc_ref[...] = -jnp.sum(dc_acc[...], axis=-1, keepdims=True)

    do_op = (do, att.dv, 0, True)
    o_spec = pl.BlockSpec((T, att.dv), lambda h, p, it, jt: (jnp.where(jt[p] == 0, it[p], last), h))
    in_specs = ([att.q_spec(att.q)] + [att.k_spec(k) for k in att.ks]
                + [att.k_spec(att.v), att.q_spec(do_op), att.row_q(), o_spec])
    args = [att.q[0]] + [k[0] for k in att.ks] + [att.v[0], do, lse, o]
    if att.has_bias:
        in_specs.append(att.cum_k())
        args.append(att.cum_rep)
    if dq_rope is not None:
        in_specs += [pl.BlockSpec((T, att.dq), lambda h, p, it, jt: (jt[p], 0))] * 3
        args += list(dq_rope)
    if order is not None:
        in_specs.append(_ANY_SPEC)
        args.append(order)
    out_specs = [pl.BlockSpec((T, att.dq), lambda h, p, it, jt: (jt[p], h))]
    out_shape = [jax.ShapeDtypeStruct((S, H * att.dq), dq_dtype)]
    out_specs += [pl.BlockSpec((T, w), lambda h, p, it, jt: (jt[p], h)) for w in widths]
    out_shape += [jax.ShapeDtypeStruct((S, H * w), dt) for w, dt in zip(widths, dk_dtypes)]
    out_specs.append(pl.BlockSpec((T, att.dv), lambda h, p, it, jt: (jt[p], h)))
    out_shape.append(jax.ShapeDtypeStruct((S, H * att.dv), BF16))
    scratch = [pltpu.VMEM((att.nb, att.dq, T), F32), pltpu.VMEM((T, att.dq), F32), pltpu.VMEM((T, att.dv), F32)]
    if att.has_bias:
        out_specs.append(pl.BlockSpec((None, T, 1), lambda h, p, it, jt: (h, jt[p], 0)))
        out_shape.append(jax.ShapeDtypeStruct((H, S, 1), F32))
        scratch.append(pltpu.VMEM((T, min(qs, LANE)), F32))
    scratch.append(pltpu.VMEM((att.nb, 1, T), F32))
    return pl.pallas_call(
        body, name=name,
        grid_spec=pltpu.PrefetchScalarGridSpec(
            num_scalar_prefetch=2, grid=(H, npairs), in_specs=in_specs, out_specs=out_specs,
            scratch_shapes=scratch),
        out_shape=out_shape,
        compiler_params=_params(("parallel", "arbitrary")),
    )(it, jt, *args)


def _adamw(w, g1, g2, m, v, name, g_row=None):
    _, K, N = w.shape
    by_rows = K % 8 == 0
    tr = _tile(K, 256, 8) if by_rows else K
    if g_row is None:
        assert g1.shape == (K, N) and g2.shape == (K, N), name
        g_row = 0
    assert by_rows and g_row % tr == 0 or g_row == 0, name
    g_blk = g_row // tr
    tc = N if by_rows else _tile(N, LANE)
    c1 = 1.0 - ADAM_B1 ** ADAM_STEP
    c2 = 1.0 - ADAM_B2 ** ADAM_STEP

    def body(w_ref, g1_ref, g2_ref, m_ref, v_ref, g_ref, d_ref, nm_ref, nv_ref):
        gv = g1_ref[...] + g2_ref[...]
        nm = ADAM_B1 * m_ref[...] + (1.0 - ADAM_B1) * gv
        nv = ADAM_B2 * v_ref[...] + (1.0 - ADAM_B2) * (gv * gv)
        g_ref[...] = gv
        d_ref[...] = -ADAM_LR * ((nm / c1) / (jnp.sqrt(nv / c2) + ADAM_EPS) + ADAM_WD * w_ref[...])
        nm_ref[...] = nm
        nv_ref[...] = nv

    if by_rows:
        blk = pl.BlockSpec((None, tr, N), lambda i: (0, i, 0))
        gblk = pl.BlockSpec((tr, N), lambda i: (g_blk + i, 0))
    else:
        blk = pl.BlockSpec((None, K, tc), lambda i: (0, 0, i))
        gblk = pl.BlockSpec((K, tc), lambda i: (0, i))
    return pl.pallas_call(
        body, name=name, grid=(K // tr if by_rows else N // tc,),
        in_specs=[blk, gblk, gblk, blk, blk], out_specs=[blk] * 4,
        out_shape=[jax.ShapeDtypeStruct((1, K, N), F32)] * 4,
        compiler_params=_params(("parallel",)),
    )(w, g1, g2, m, v)


_HBM_SPEC = pl.BlockSpec(memory_space=pltpu.HBM)
_SEM_SPEC = pl.BlockSpec(memory_space=pltpu.SEMAPHORE)
_VMEM_SPEC = pl.BlockSpec(memory_space=pltpu.VMEM)
_EFFECT = pltpu.SideEffectType.DATAFLOW_SIDE_EFFECTING


def _place():
    return lax.axis_index("x"), lax.axis_index("y"), lax.axis_index("c")


def _other_chips(x, y):
    return [(1 - x, y), (x, 1 - y), (1 - x, 1 - y)]


def _chip_copies(src_ref, land_ref, sems, gather):
    x, y, c = _place()
    me = 2 * x + y
    out, back = [], []
    if gather == "half":
        half = src_ref.shape[0] // 2
        mine = pl.ds(pl.multiple_of(c * half, 16), half)
    for n, (px, py) in enumerate(_other_chips(x, y)):
        if gather == "half":
            src, there, here = src_ref.at[mine], land_ref.at[me, mine], land_ref.at[2 * px + py, mine]
        elif gather:
            src, there, here = src_ref, land_ref.at[me], land_ref.at[2 * px + py]
        else:
            src, there, here = src_ref.at[2 * px + py], land_ref.at[n], land_ref.at[n]
        out.append(pltpu.make_async_remote_copy(
            src_ref=src, dst_ref=there, send_sem=sems[n], recv_sem=sems[3 + n],
            device_id=(px, py, c), device_id_type=MESH))
        back.append(pltpu.make_async_remote_copy(
            src_ref=src, dst_ref=here, send_sem=sems[n], recv_sem=sems[3 + n],
            device_id=(px, py, c), device_id_type=MESH))
    return out, back


def _xchg_start(src, land, gather, order, name):
    def body(src_ref, land_ref, order_ref, *outs):
        sems = outs[0:6]
        token = outs[8]
        out, _ = _chip_copies(src_ref, land_ref, sems, gather)
        for cp in out:
            cp.start()
        token[...] = jnp.zeros_like(token)

    outs = pl.pallas_call(
        body, name=name,
        out_shape=(pltpu.SemaphoreType.DMA(()),) * 6 + (
            pltpu.HBM(src.shape, src.dtype), pltpu.HBM(land.shape, land.dtype),
            jax.ShapeDtypeStruct((8, LANE), F32)),
        in_specs=(_HBM_SPEC, _HBM_SPEC, _ANY_SPEC),
        out_specs=(_SEM_SPEC,) * 6 + (_HBM_SPEC, _HBM_SPEC, _VMEM_SPEC),
        input_output_aliases={0: 6, 1: 7},
        compiler_params=pltpu.CompilerParams(has_side_effects=_EFFECT),
    )(pltpu.with_memory_space_constraint(src, pltpu.HBM), pltpu.with_memory_space_constraint(land, pltpu.HBM), order)
    return outs[0:6], outs[6], outs[7], outs[8]


def _xchg_wait(started, gather, after, name):
    sems, src, land, _ = started
    after = after if isinstance(after, tuple) else (after,)

    def body(src_ref, land_ref, *rest):
        _, back = _chip_copies(src_ref, land_ref, rest[0:6], gather)
        for cp in back:
            cp.wait_send()
            cp.wait_recv()

    return pl.pallas_call(
        body, name=name,
        out_shape=(pltpu.HBM(src.shape, src.dtype), pltpu.HBM(land.shape, land.dtype)),
        in_specs=(_HBM_SPEC, _HBM_SPEC) + (_SEM_SPEC,) * 6 + (_ANY_SPEC,) * len(after),
        out_specs=(_HBM_SPEC, _HBM_SPEC),
        input_output_aliases={0: 0, 1: 1},
        compiler_params=pltpu.CompilerParams(has_side_effects=_EFFECT),
    )(src, land, *sems, *after)


def _forward_halves(land, name):
    _, R, C = land.shape
    half = R // 2
    assert half % 16 == 0

    def body(land_ref, out_ref, send_sems, recv_sems):
        x, y, c = _place()
        mine = pl.ds(pl.multiple_of(c * half, 16), half)
        theirs = pl.ds(pl.multiple_of((1 - c) * half, 16), half)
        sends = []
        for n, (px, py) in enumerate(_other_chips(x, y)):
            cp = pltpu.make_async_remote_copy(
                src_ref=land_ref.at[2 * px + py, mine], dst_ref=out_ref.at[2 * px + py, mine],
                send_sem=send_sems.at[n], recv_sem=recv_sems.at[n], device_id=(x, y, 1 - c), device_id_type=MESH)
            cp.start()
            sends.append(cp)
        for n, (px, py) in enumerate(_other_chips(x, y)):
            pltpu.make_async_remote_copy(
                src_ref=land_ref.at[2 * px + py, theirs], dst_ref=out_ref.at[2 * px + py, theirs],
                send_sem=send_sems.at[n], recv_sem=recv_sems.at[n], device_id=(x, y, 1 - c),
                device_id_type=MESH).wait_recv()
        for cp in sends:
            cp.wait_send()

    return pl.pallas_call(
        body, name=name,
        in_specs=[_ANY_SPEC], out_specs=_ANY_SPEC,
        out_shape=jax.ShapeDtypeStruct(land.shape, land.dtype),
        input_output_aliases={0: 0},
        scratch_shapes=[pltpu.SemaphoreType.DMA((3,)), pltpu.SemaphoreType.DMA((3,))],
    )(land)


def _sib_copy(src_ref, land_ref, send_sem, recv_sem):
    x, y, c = _place()
    return pltpu.make_async_remote_copy(src_ref=src_ref, dst_ref=land_ref, send_sem=send_sem, recv_sem=recv_sem,
                                        device_id=(x, y, 1 - c), device_id_type=MESH)


def _sib_start(src, name):
    land = lax.empty(src.shape, src.dtype)

    def body(src_ref, land_ref, send_sem, recv_sem, src_thru, land_thru, token):
        _sib_copy(src_ref, land_ref, send_sem, recv_sem).start()
        token[...] = jnp.zeros_like(token)

    return pl.pallas_call(
        body, name=name,
        out_shape=(pltpu.SemaphoreType.DMA(()), pltpu.SemaphoreType.DMA(()),
                   pltpu.HBM(src.shape, src.dtype), pltpu.HBM(land.shape, land.dtype),
                   jax.ShapeDtypeStruct((8, LANE), F32)),
        in_specs=(_HBM_SPEC, _HBM_SPEC),
        out_specs=(_SEM_SPEC, _SEM_SPEC, _HBM_SPEC, _HBM_SPEC, _VMEM_SPEC),
        input_output_aliases={0: 2, 1: 3},
        compiler_params=pltpu.CompilerParams(has_side_effects=_EFFECT),
    )(pltpu.with_memory_space_constraint(src, pltpu.HBM), pltpu.with_memory_space_constraint(land, pltpu.HBM))


def _sib_wait(started, after, name):
    send_sem, recv_sem, src, land, _ = started

    def body(src_ref, land_ref, send_sem, recv_sem, after_ref, src_out, land_out):
        cp = _sib_copy(src_ref, land_ref, send_sem, recv_sem)
        cp.wait_send()
        cp.wait_recv()

    return pl.pallas_call(
        body, name=name,
        out_shape=(pltpu.HBM(src.shape, src.dtype), pltpu.HBM(land.shape, land.dtype)),
        in_specs=(_HBM_SPEC, _HBM_SPEC, _SEM_SPEC, _SEM_SPEC, _ANY_SPEC),
        out_specs=(_HBM_SPEC, _HBM_SPEC),
        input_output_aliases={0: 0, 1: 1},
        compiler_params=pltpu.CompilerParams(has_side_effects=_EFFECT),
    )(src, land, send_sem, recv_sem, after)


def _sum_slabs(gp, recv, chip, name):
    _, R, C = gp.shape
    tr = _tile(R, PACK_ROWS, 16)

    def body(chip_ref, own_ref, r0_ref, r1_ref, r2_ref, o_ref):
        acc = own_ref[...].astype(F32) + r0_ref[...].astype(F32)
        o_ref[...] = (acc + r1_ref[...].astype(F32)) + r2_ref[...].astype(F32)

    def got(n):
        return pl.BlockSpec((None, tr, C), lambda i, chip_ref: (n, i, 0))

    return pl.pallas_call(
        body, name=name,
        grid_spec=pltpu.PrefetchScalarGridSpec(
            num_scalar_prefetch=1, grid=(R // tr,),
            in_specs=[pl.BlockSpec((None, tr, C), lambda i, chip_ref: (chip_ref[0], i, 0)), got(0), got(1), got(2)],
            out_specs=pl.BlockSpec((tr, C), lambda i, chip_ref: (i, 0))),
        out_shape=jax.ShapeDtypeStruct((R, C), F32),
        compiler_params=_params(("parallel",)),
    )(jnp.reshape(chip, (1,)).astype(jnp.int32), gp, recv, recv, recv)


def _all_reduce_vec(vec, name):
    VR, W = vec.shape

    def body(vec_ref, vall_ref, vout_ref, vsend_sems, vrecv_sems):
        x, y, c = _place()
        vall_ref[4 * x + 2 * y + c] = vec_ref[...]
        sends = []
        peers = []
        for r in range(1, N_DEV):
            dx, dy, dc = (r >> 2) & 1, (r >> 1) & 1, r & 1
            peer = (x ^ dx, y ^ dy, c ^ dc)
            peers.append(peer)
            cp = pltpu.make_async_remote_copy(
                src_ref=vec_ref, dst_ref=vall_ref.at[4 * x + 2 * y + c], send_sem=vsend_sems.at[r - 1],
                recv_sem=vrecv_sems.at[r - 1], device_id=peer, device_id_type=MESH)
            cp.start()
            sends.append(cp)
        for r, peer in enumerate(peers):
            pltpu.make_async_remote_copy(
                src_ref=vec_ref, dst_ref=vall_ref.at[4 * peer[0] + 2 * peer[1] + peer[2]],
                send_sem=vsend_sems.at[r], recv_sem=vrecv_sems.at[r],
                device_id=peer, device_id_type=MESH).wait_recv()
        total = vall_ref[0]
        for d in range(1, N_DEV):
            total = total + vall_ref[d]
        vout_ref[...] = total
        for cp in sends:
            cp.wait_send()

    outs = pl.pallas_call(
        body, name=name,
        in_specs=[_VMEM_SPEC], out_specs=[_VMEM_SPEC, _VMEM_SPEC],
        out_shape=[jax.ShapeDtypeStruct((N_DEV, VR, W), F32), jax.ShapeDtypeStruct((VR, W), F32)],
        scratch_shapes=[pltpu.SemaphoreType.DMA((N_DEV - 1,)), pltpu.SemaphoreType.DMA((N_DEV - 1,))],
    )(vec)
    return outs[1]


class _Pack:
    def __init__(self, group, C):
        self.group, self.C = group, C
        self.rows, self.offs, off = {}, {}, 0
        for nm, (K, N), _ in group:
            assert N <= C, nm
            self.rows[nm] = K if 2 * N > C else -(-(K * N) // C)
            self.offs[nm] = off
            off += -(-self.rows[nm] // 16) * 16
        self.used = off
        self.R = -(-off // PACK_ROWS) * PACK_ROWS

    def _rows_of(self, a):
        K, N = a.shape
        if 2 * N > self.C:
            a = jnp.pad(a, ((0, 0), (0, self.C - N)))
        else:
            a = jnp.pad(a.reshape(-1), (0, -(K * N) % self.C)).reshape(-1, self.C)
        return jnp.pad(a, ((0, -a.shape[0] % 16), (0, 0)))

    def pack(self, shards):
        parts = [self._rows_of(shards[nm].astype(BF16)) for nm, _, _ in self.group]
        return jnp.concatenate(parts + [jnp.zeros((self.R - self.used, self.C), BF16)], axis=0)

    def _shard_of(self, rows, shape):
        K, N = shape
        return rows[:, :N] if 2 * N > self.C else rows.reshape(-1)[:K * N].reshape(K, N)

    def part(self, flat, nm, shape):
        return self._shard_of(flat[self.offs[nm]:self.offs[nm] + self.rows[nm]], shape)

    def slab_rows(self, nm, g):
        (K, N), axis = next((shape, axis) for n, shape, axis in self.group if n == nm)
        cuts = [g[:, k * N:(k + 1) * N] if axis == 1 else g[k * K:(k + 1) * K, :] for k in range(N_CHIPS)]
        return jnp.stack([self._rows_of(c.astype(BF16)) for c in cuts])

    def slabs(self, grads):
        parts = [self.slab_rows(nm, grads[nm]) for nm, _, _ in self.group]
        return jnp.concatenate(parts + [jnp.zeros((N_CHIPS, self.R - self.used, self.C), BF16)], axis=1)

    def full(self, gathered, names=None):
        res = {}
        for nm, (K, N), axis in self.group:
            if names is None or nm in names:
                rows = gathered[:, self.offs[nm]:self.offs[nm] + self.rows[nm]]
                res[nm] = jnp.concatenate([self._shard_of(rows[k], (K, N)) for k in range(N_CHIPS)], axis=axis)
        return res


def _rope_tables(S):
    pos = jnp.arange(S, dtype=F32)
    inv = 1.0 / (ROPE_THETA ** (jnp.arange(0, MLA_ROPE, 2, dtype=F32) / MLA_ROPE))
    ang = pos[:, None] * inv[None, :]
    cos, sin = jnp.cos(ang), jnp.sin(ang)
    half = MLA_ROPE // 2
    z = jnp.zeros((S, half), F32)
    one = jnp.ones((S, LANE - MLA_ROPE), F32)
    zero = jnp.zeros((S, LANE - MLA_ROPE), F32)
    kc = jnp.concatenate([cos, cos, one], axis=1)
    ksa = jnp.concatenate([-sin, z, zero], axis=1)
    ksb = jnp.concatenate([z, sin, zero], axis=1)
    qc = jnp.concatenate([jnp.ones((S, MLA_NOPE), F32), kc], axis=1)
    qsa = jnp.concatenate([jnp.zeros((S, MLA_NOPE), F32), ksa], axis=1)
    qsb = jnp.concatenate([jnp.zeros((S, MLA_NOPE), F32), ksb], axis=1)
    return (kc, ksa, ksb), (qc, qsa, qsb)


def _pad_cols(a, width):
    return jnp.pad(a, ((0, 0), (0, width - a.shape[1])))


def kernel(x, attn_norm, w_in, fox_f_bias, q_norm, w_uq, kv_norm, w_ukv, w_mla_branch, w_fox_branch, w_out, mlp_norm, w_up, w_down, final_norm, loss_target, m_attn_norm, m_w_in, m_fox_f_bias, m_q_norm, m_w_uq, m_kv_norm, m_w_ukv, m_w_mla_branch, m_w_fox_branch, m_w_out, m_mlp_norm, m_w_up, m_w_down, m_final_norm, v_attn_norm, v_w_in, v_fox_f_bias, v_q_norm, v_w_uq, v_kv_norm, v_w_ukv, v_w_mla_branch, v_w_fox_branch, v_w_out, v_mlp_norm, v_w_up, v_w_down, v_final_norm):
    _, S, D = x.shape
    H, HF = MLA_HEADS, FOX_HEADS
    QL, KVL = MLA_Q_LORA, MLA_KV_LORA
    assert H == HF and H <= 8
    xs = x[0]
    target = loss_target[0]
    C = D
    chip = 2 * lax.axis_index("x") + lax.axis_index("y")

    def flip(a):
        return jnp.transpose(a, (0, 2, 1))

    w_in, m_w_in, v_w_in = flip(w_in), flip(m_w_in), flip(v_w_in)
    weights = {"attn_norm": attn_norm, "w_in": w_in, "fox_f_bias": fox_f_bias, "q_norm": q_norm, "w_uq": w_uq,
               "kv_norm": kv_norm, "w_ukv": w_ukv, "w_mla_branch": w_mla_branch, "w_fox_branch": w_fox_branch,
               "w_out": w_out, "mlp_norm": mlp_norm, "w_up": w_up, "w_down": w_down, "final_norm": final_norm}
    moments = {"attn_norm": (m_attn_norm, v_attn_norm), "w_in": (m_w_in, v_w_in), "fox_f_bias": (m_fox_f_bias, v_fox_f_bias),
               "q_norm": (m_q_norm, v_q_norm), "w_uq": (m_w_uq, v_w_uq), "kv_norm": (m_kv_norm, v_kv_norm),
               "w_ukv": (m_w_ukv, v_w_ukv), "w_mla_branch": (m_w_mla_branch, v_w_mla_branch),
               "w_fox_branch": (m_w_fox_branch, v_w_fox_branch), "w_out": (m_w_out, v_w_out),
               "mlp_norm": (m_mlp_norm, v_mlp_norm), "w_up": (m_w_up, v_w_up), "w_down": (m_w_down, v_w_down),
               "final_norm": (m_final_norm, v_final_norm)}

    def group(names_axes):
        return [(nm, weights[nm].shape[1:], axis) for nm, axis in names_axes]

    pack_a = _Pack(group([("w_in", 0), ("w_uq", 1), ("w_ukv", 1)]), C)
    pack_b = _Pack(group([("w_down", 0), ("w_up", 1), ("w_out", 0), ("w_mla_branch", 1), ("w_fox_branch", 1)]), C)
    RA, RB = pack_a.R, pack_b.R
    wp_b = pack_b.pack({nm: weights[nm][0] for nm, _, _ in pack_b.group})
    n_in = w_in.shape[1]
    rows_in = -(-n_in // 16) * 16
    assert pack_a.offs["w_in"] == 0 and w_in.shape[2] == C
    assert all((k * n_in) % 16 + n_in <= rows_in for k in range(N_CHIPS))
    shifted = lax.dynamic_update_slice(jnp.zeros((rows_in, C), BF16), w_in[0].astype(BF16), ((chip * n_in) % 16, 0))
    wp_a = jnp.concatenate([shifted] + [pack_a._rows_of(weights[nm][0].astype(BF16)) for nm, _, _ in pack_a.group[1:]]
                           + [jnp.zeros((RA - pack_a.used, C), BF16)], axis=0)
    ag_a = _xchg_start(wp_a, lax.empty((N_CHIPS, RA, C), BF16), "half", jnp.zeros((8, LANE), F32), "all_gather_start_a")
    xn = _norm_fwd(xs, attn_norm, "attn_norm_fwd", order=ag_a[3])
    own_a, land_a = _xchg_wait(ag_a, "half", (xn, wp_b), "all_gather_wait_a")
    land_a = _forward_halves(land_a, "all_gather_forward_a")
    gathered_a = lax.dynamic_update_slice(land_a, own_a[None], (chip, 0, 0))
    ag_b = _xchg_start(wp_b, lax.empty((N_CHIPS, RB, C), BF16), True, gathered_a, "all_gather_start_b")
    full = pack_a.full(gathered_a, ("w_uq", "w_ukv"))
    tile0 = [(k * n_in) // 16 * 16 for k in range(N_CHIPS)]
    total = tile0[-1] + rows_in
    full["w_in"] = sum(jnp.pad(gathered_a[k, :rows_in], ((tile0[k], total - tile0[k] - rows_in), (0, 0)))
                       for k in range(N_CHIPS))

    o_ckv = QL
    o_kr = o_ckv + KVL
    o_fq = o_kr + MLA_ROPE
    o_ff = o_fq + 3 * HF * FOX_HEAD_DIM
    o_g = o_ff + HF
    wi = full["w_in"]
    assert N_CHIPS * n_in == o_g + 2 * D and wi.shape[0] >= o_g + 2 * D
    WS = QL + KVL + 2 * LANE
    NQKV = 3 * HF * FOX_HEAD_DIM

    def pad_rows(a, rows):
        return jnp.pad(a, ((0, rows - a.shape[0]), (0, 0)))

    w_small = jnp.concatenate([wi[:o_kr], pad_rows(wi[o_kr:o_fq], LANE), pad_rows(wi[o_ff:o_g], LANE)], axis=0)
    w_qkv = wi[o_fq:o_ff]
    w_g = wi[o_g:o_g + 2 * D]
    w_pack = jnp.concatenate([w_small, w_qkv, w_g], axis=0)
    dqk = MLA_NOPE + MLA_ROPE
    w_uq_p = jnp.pad(full["w_uq"].reshape(QL, H, dqk), ((0, 0), (0, 0), (0, QPAD - dqk))).reshape(QL, H * QPAD)
    ukv = full["w_ukv"].reshape(KVL, H, MLA_NOPE + MLA_V)
    w_ukv_p = jnp.concatenate([ukv[:, :, :MLA_NOPE].reshape(KVL, H * MLA_NOPE),
                               ukv[:, :, MLA_NOPE:].reshape(KVL, H * MLA_V)], axis=1)

    (kc, ksa, ksb), (qc, qsa, qsb) = _rope_tables(S)
    bias_pad = _pad_cols(fox_f_bias, LANE)

    small = _matmul(xn, w_small, "nt", [F32], "proj_small")
    n_fq = HF * FOX_HEAD_DIM
    q_scale = jnp.concatenate([jnp.full((1, n_fq), LOG2E / math.sqrt(FOX_HEAD_DIM), F32),
                               jnp.ones((1, NQKV - n_fq), F32)], axis=1)
    qkv = _matmul(xn, w_qkv, "nt", [BF16], "proj_qkv", col_extras=(q_scale,), epilogue=lambda acc, cs: (acc * cs,))
    gpre = _matmul(xn, w_g, "nt", [BF16], "proj_gates")
    cqn, ckvn, kr, cum = _prep_fwd(small, q_norm, kv_norm, bias_pad, kc, ksa, ksb, HF, "prep_fwd")
    c2_mla = LOG2E / math.sqrt(dqk)
    q_rot = _matmul(cqn, w_uq_p, "nn", [BF16], "mla_q_up", tm=2048, tn=QPAD, row_extras=(qc * c2_mla, qsa * c2_mla, qsb * c2_mla),
                    epilogue=lambda acc, c, sa, sb: (_rope(acc, c, sa, sb, 1),))
    kv2 = _matmul(ckvn, w_ukv_p, "nn", [BF16], "mla_kv_up", tm=2048)

    def mla_att(qsub):
        return _AttT(S, H, (q_rot, QPAD, 0, True), [(kv2, MLA_NOPE, 0, True), (kr, LANE, 0, False)],
                     (kv2, MLA_V, H, True), 1.0 / math.sqrt(dqk), True, qsub=qsub)

    mla = mla_att(QSUB)
    o_mla, lse_mla = _att_fwd_t(mla_att(2 * QSUB), "mla_att_fwd")

    cum_t = jnp.transpose(cum[:, :HF]) * LOG2E
    cum_rep = jnp.broadcast_to(cum_t[:, :, None], (HF, S, min(QSUB, _tile(S, ATT_T))))
    fox = _AttT(S, HF, (qkv, FOX_HEAD_DIM, 0, True), [(qkv, FOX_HEAD_DIM, HF, True)],
                (qkv, FOX_HEAD_DIM, 2 * HF, True), 1.0 / math.sqrt(FOX_HEAD_DIM), False, cum_rep)
    o_fox, ox_fox, lse_fox = _att_fwd_t(fox, "fox_att_fwd", exact=True)

    own_b, land_b = _xchg_wait(ag_b, True, (lse_fox, lse_mla, gpre), "all_gather_wait_b")
    gathered_b = lax.dynamic_update_slice(land_b, own_b[None], (chip, 0, 0))
    full.update(pack_b.full(gathered_b, ("w_mla_branch", "w_fox_branch", "w_out")))
    w_mb, w_fb, w_o = (full[n] for n in ("w_mla_branch", "w_fox_branch", "w_out"))

    def b_of(nm, mode, tn, tk):
        (K, N), axis = next((shape, axis) for n, shape, axis in pack_b.group if n == nm)
        off = pack_b.offs[nm]
        shape = (N_CHIPS * K, N) if axis == 0 else (K, N_CHIPS * N)
        t_r, t_c = (tk, tn) if mode == "nn" else (tn, tk)
        t_r, t_c = _tile(shape[0], t_r), _tile(shape[1], t_c)
        if not (N == C and K % t_r == 0 and N % t_c == 0 and off % t_r == 0):
            return pack_b.full(gathered_b, (nm,))[nm], None
        base = off // t_r
        if axis == 0:
            per = K // t_r
            place = lambda rb, cb: (rb // per, base + rb % per, cb)
        else:
            per = N // t_c
            place = lambda rb, cb: (cb // per, base + rb, cb % per)
        return gathered_b, (shape, (lambda j, k: place(k, j)) if mode == "nn" else (lambda j, k: place(j, k)))

    y_mla = _matmul(o_mla, w_mb, "nn", [BF16], "mla_branch", tm=2048)

    def gate_merge(acc, ga, gb, ya):
        return acc, _sigmoid(ga.astype(F32)) * ya.astype(F32) + _sigmoid(gb.astype(F32)) * acc

    y_fox, merged = _matmul(o_fox, w_fb, "nn", [BF16, BF16], "fox_branch_gates",
                            extras=((gpre, 0), (gpre, 1), y_mla), epilogue=gate_merge)
    h1 = _matmul(merged, w_o, "nn", [F32], "out_proj", extras=(xs,), epilogue=lambda acc, r: (acc + r,))
    hn = _norm_fwd(h1, mlp_norm, "mlp_norm_fwd")

    def relu2(acc):
        a = jnp.maximum(acc, 0.0)
        return a * a, a

    w_u, w_u_in = b_of("w_up", "nn", 1024, 2048)
    u, a_pos = _matmul(hn, w_u, "nn", [BF16, BF16], "mlp_up", epilogue=relu2, b_in=w_u_in)
    w_d, w_d_in = b_of("w_down", "nn", 1024, 2048)
    h2 = _matmul(u, w_d, "nn", [F32], "mlp_down", tn=1024, extras=(h1,), epilogue=lambda acc, r: (acc + r,),
                 b_in=w_d_in)
    dh2, dh2_b, g_final, loss_part = _final(h2, final_norm.reshape(1, D), target, "final_norm_loss")

    gp_b = lax.empty((N_CHIPS, RB, C), BF16)
    by_glue = {}

    def grad_b(nm, a, b, name):
        nonlocal gp_b
        (K, N), axis = next((shape, axis) for n, shape, axis in pack_b.group if n == nm)
        off = pack_b.offs[nm]
        tm = min(1024, K) if axis == 0 else min(1024, a.shape[1])
        tn = min(1024, N) if axis == 1 else min(1024, b.shape[1])
        if not (N == C and tm % LANE == 0 and tn % LANE == 0 and K % tm == 0 and N % tn == 0 and off % tm == 0):
            by_glue[nm] = _mm_tn(a, b, name)
            return
        base = off // tm
        if axis == 0:
            per = K // tm
            place = lambda i, j: (i // per, base + i % per, j)
        else:
            per = N // tn
            place = lambda i, j: (j // per, base + i, j % per)
        gp_b = _mm_tn(a, b, name, tm=tm, tn=tn, into=(gp_b, place))

    w_d, w_d_in = b_of("w_down", "nt", 1024, 2048)
    da = _matmul(dh2_b, w_d, "nt", [BF16], "mlp_down_dx", extras=(a_pos,),
                 epilogue=lambda acc, a: (acc * (2.0 * a.astype(F32)),), b_in=w_d_in)
    grad_b("w_down", u, dh2_b, "mlp_down_dw")
    w_u, w_u_in = b_of("w_up", "nt", 1024, 2048)
    dhn = _matmul(da, w_u, "nt", [F32], "mlp_up_dx", tn=1024, b_in=w_u_in)
    grad_b("w_up", hn, da, "mlp_up_dw")
    dh1, dh1_b, g_mlp_norm = _norm_bwd(h1, dhn, mlp_norm, dh2, "mlp_norm_bwd")

    def gate_bwd(acc, ga, gb, ya, yb):
        ga, gb = _sigmoid(ga.astype(F32)), _sigmoid(gb.astype(F32))
        ya, yb = ya.astype(F32), yb.astype(F32)
        return acc * ga, acc * gb, acc * ya * (ga * (1.0 - ga)), acc * yb * (gb * (1.0 - gb))

    dy_mla, dy_fox, dg_mla, dg_fox = _matmul(dh1_b, w_o, "nt", [BF16] * 4, "out_proj_dx_gates", tn=512,
                                             extras=((gpre, 0), (gpre, 1), y_mla, y_fox), epilogue=gate_bwd)
    grad_b("w_out", merged, dh1_b, "out_proj_dw")
    do_mla = _matmul(dy_mla, w_mb, "nt", [BF16], "mla_branch_dx")
    grad_b("w_mla_branch", o_mla, dy_mla, "mla_branch_dw")
    do_fox = _matmul(dy_fox, w_fb, "nt", [BF16], "fox_branch_dx")
    grad_b("w_fox_branch", o_fox, dy_fox, "fox_branch_dw")
    for nm, g in by_glue.items():
        gp_b = lax.dynamic_update_slice(gp_b, pack_b.slab_rows(nm, g), (0, pack_b.offs[nm], 0))
    if RB > pack_b.used:
        gp_b = lax.dynamic_update_slice(gp_b, jnp.zeros((N_CHIPS, RB - pack_b.used, C), BF16), (0, pack_b.used, 0))

    rs_b = _xchg_start(gp_b, lax.empty((3, RB, C), BF16), False, do_fox, "grad_scatter_start_b")

    dq_rot, dk_nope, dkr_heads, dv_mla = _att_bwd_t(mla, do_mla, lse_mla, o_mla, BF16, [BF16, F32],
                                                    "mla_att_bwd", dq_rope=(qc, qsa, qsb), order=rs_b[3])
    dfq, dfk, dfv, dcum = _att_bwd_t(fox, do_fox, lse_fox, ox_fox, BF16, [BF16], "fox_att_bwd")

    gp_b_sent, recv_b = _xchg_wait(rs_b, False, (dfq, dq_rot), "grad_scatter_wait_b")
    swap_b = _sib_start(_sum_slabs(gp_b_sent, recv_b, chip, "grad_sum_b"), "grad_swap_start_b")

    dcqn = _matmul(dq_rot, w_uq_p, "nt", [F32], "mla_q_up_dx", tm=2048, order=swap_b[4])
    g_w_uq_p = _mm_tn(cqn, dq_rot, "mla_q_up_dw")
    dkv2 = jnp.concatenate([dk_nope, dv_mla], axis=1)
    dckvn = _matmul(dkv2, w_ukv_p, "nt", [F32], "mla_kv_up_dx", tm=2048)
    g_w_ukv_p = _mm_tn(ckvn, dkv2, "mla_kv_up_dw")

    dcum_rows = jnp.pad(dcum[:, :, 0], ((0, 8 - HF), (0, 0)))
    dlogf_rows = _suffix_sum_rows(dcum_rows, "fox_forget_suffix_sum")
    dlogf = _pad_cols(jnp.transpose(dlogf_rows[:HF]), LANE)
    d_small, g_q_norm, g_kv_norm, g_bias = _prep_bwd(
        small, dcqn, dckvn, dkr_heads, dlogf, q_norm, kv_norm, bias_pad, kc, ksa, ksb, H, "prep_bwd")
    dproj = [d_small, dfq, dfk, dfv, dg_mla, dg_fox]
    gs, gfq, gfk, gfv, gg_mla, gg_fox = [
        _matmul(part, xn, "tn", [BF16], "proj_dw_" + tag, tm=1024, tn=1024, tk=2048)
        for part, tag in zip(dproj, ("small", "fq", "fk", "fv", "g_mla", "g_fox"))]

    g_w_in = jnp.concatenate([gs[:o_kr], gs[o_kr:o_kr + MLA_ROPE], gfq, gfk, gfv,
                              gs[o_kr + LANE:o_kr + LANE + HF], gg_mla, gg_fox], axis=0)
    g_w_uq = g_w_uq_p.reshape(QL, H, QPAD)[:, :, :dqk].reshape(QL, H * dqk)
    g_w_ukv = jnp.concatenate([g_w_ukv_p[:, :H * MLA_NOPE].reshape(KVL, H, MLA_NOPE),
                               g_w_ukv_p[:, H * MLA_NOPE:].reshape(KVL, H, MLA_V)], axis=2).reshape(KVL, -1)

    gp_a = pack_a.slabs({"w_in": g_w_in, "w_uq": g_w_uq, "w_ukv": g_w_ukv})
    rs_a = _xchg_start(gp_a, lax.empty((3, RA, C), BF16), False, gg_fox, "grad_scatter_start_a")
    dxn = _matmul_row_parts(dproj, w_pack, F32, "proj_dx", order=rs_a[3])
    grad_x, g_attn_norm = _norm_bwd(xs, dxn, attn_norm, dh1, "attn_norm_bwd", with_bf16=False)
    gp_a_sent, recv_a = _xchg_wait(rs_a, False, grad_x, "grad_scatter_wait_a")
    swap_a = _sib_start(_sum_slabs(gp_a_sent, recv_a, chip, "grad_sum_a"), "grad_swap_start_a")
    vec_w = max(D, LANE)
    vec_rows = [g_attn_norm, g_mlp_norm, g_final, g_q_norm, g_kv_norm, g_bias, loss_part]
    vec = jnp.concatenate([_pad_cols(v, vec_w) for v in vec_rows] + [jnp.zeros((1, vec_w), F32)], axis=0)
    vsum = _all_reduce_vec(vec, "all_reduce_vectors")
    part_b, sib_b = _sib_wait(swap_b, vsum, "grad_swap_wait_b")

    grads, deltas, new_m, new_v = {}, {}, {}, {}

    def update(pack, mine, theirs):
        for nm, shape, _ in pack.group:
            K, N = shape
            if N == pack.C and K % 8 == 0 and pack.offs[nm] % _tile(K, 256, 8) == 0:
                g, d, nm_, nv_ = _adamw(weights[nm], mine, theirs, moments[nm][0], moments[nm][1], "adamw_" + nm,
                                        g_row=pack.offs[nm])
            else:
                g, d, nm_, nv_ = _adamw(weights[nm], pack.part(mine, nm, shape), pack.part(theirs, nm, shape),
                                        moments[nm][0], moments[nm][1], "adamw_" + nm)
            grads[nm], deltas[nm], new_m[nm], new_v[nm] = g, d, nm_, nv_
        return g

    last_b = update(pack_b, part_b, sib_b)
    part_a, sib_a = _sib_wait(swap_a, last_b, "grad_swap_wait_a")
    update(pack_a, part_a, sib_a)

    vec_names = ["attn_norm", "mlp_norm", "final_norm", "q_norm", "kv_norm", "fox_f_bias"]

    def vec_pack(arrs):
        return jnp.concatenate([_pad_cols(a.reshape(1, -1), vec_w) for a in arrs]
                               + [jnp.zeros((2, vec_w), F32)], axis=0)[None]

    vg, vd, vm, vv = _adamw(vec_pack([weights[n] for n in vec_names]), vsum, jnp.zeros_like(vsum),
                            vec_pack([moments[n][0] for n in vec_names]), vec_pack([moments[n][1] for n in vec_names]),
                            "adamw_vectors")
    for r, nm in enumerate(vec_names):
        shp = weights[nm].shape
        n = weights[nm].size
        grads[nm] = vsum[r, :n].reshape(shp)
        deltas[nm], new_m[nm], new_v[nm] = (vd[0, r, :n].reshape(shp), vm[0, r, :n].reshape(shp),
                                            vv[0, r, :n].reshape(shp))
    loss = vsum[6, 0]

    for res in (grads, deltas, new_m, new_v):
        res["w_in"] = flip(res["w_in"])
    order = ["attn_norm", "w_in", "fox_f_bias", "q_norm", "w_uq", "kv_norm", "w_ukv", "w_mla_branch", "w_fox_branch",
             "w_out", "mlp_norm", "w_up", "w_down", "final_norm"]
    return (loss, grad_x[None], *[grads[n] for n in order], *[deltas[n] for n in order],
            *[new_m[n] for n in order], *[new_v[n] for n in order])
```

```python
import math

import jax
import jax.numpy as jnp
from jax import lax
from jax.experimental import pallas as pl
from jax.experimental.pallas import tpu as pltpu

CHUNK = 64
MLA_HEADS = 8
MLA_Q_LORA = 512
MLA_KV_LORA = 256
MLA_NOPE = 128
MLA_ROPE = 64
MLA_V = 128
ROPE_THETA = 10000.0
FOX_HEADS = 8
FOX_HEAD_DIM = 128
EPS = 1e-6

ADAM_LR = 0.001
ADAM_B1 = 0.9
ADAM_B2 = 0.999
ADAM_EPS = 1e-08
ADAM_WD = 0.01
ADAM_STEP = 10

LANE = 128
QPAD = 2 * LANE
N_CHIPS = 4
N_DEV = 8
VMEM_LIMIT = 48 * 1024 * 1024
ATT_T = 2048
QSUB = 256
ROW_T = 256
PACK_ROWS = 256
LOG2E = 1.4426950408889634

BF16 = jnp.bfloat16
F32 = jnp.float32
MESH = pl.DeviceIdType.MESH

_NT = (((1,), (1,)), ((), ()))
_TN = (((0,), (0,)), ((), ()))
_NN = (((1,), (0,)), ((), ()))


def _tile(dim, pref, align=LANE):
    if dim <= pref:
        return dim
    t = (pref // align) * align
    while t >= align:
        if dim % t == 0:
            return t
        t -= align
    return dim


def _params(sem=None):
    return pltpu.CompilerParams(dimension_semantics=sem, vmem_limit_bytes=VMEM_LIMIT)


_ANY_SPEC = pl.BlockSpec(memory_space=pl.ANY)


def _matmul(a, b, mode, out_dtypes, name, *, tm=1024, tn=1024, tk=2048, extras=(), row_extras=(), col_extras=(),
            epilogue=None, order=None, into=None, b_in=None):
    b_shape = b.shape if b_in is None else b_in[0]
    if mode == "nn":
        (M, K), (K2, N) = a.shape, b_shape
    elif mode == "nt":
        (M, K), (N, K2) = a.shape, b_shape
    else:
        (K, M), (K2, N) = a.shape, b_shape
    assert K == K2, (name, a.shape, b_shape)
    tm, tn, tk = _tile(M, tm), _tile(N, tn), _tile(K, tk)
    nk = K // tk
    extras = [e if isinstance(e, tuple) else (e, 0) for e in extras]
    n_out = len(out_dtypes)
    n_ex = len(extras) + len(row_extras) + len(col_extras)
    n_ord = 0 if order is None else 1
    assert all(r.shape == (M, tn) for r in row_extras), name
    dims = {"nn": _NN, "nt": _NT, "tn": _TN}[mode]

    def body(*refs):
        a_ref, b_ref = refs[0], refs[1]
        ex_refs = refs[2:2 + n_ex]
        o_refs = refs[2 + n_ex + n_ord:2 + n_ex + n_ord + n_out]
        acc_ref = refs[2 + n_ex + n_ord + n_out]
        k = pl.program_id(2)
        part = lax.dot_general(a_ref[...], b_ref[...], dims, preferred_element_type=F32)

        @pl.when(k == 0)
        def _():
            acc_ref[...] = part

        @pl.when(k > 0)
        def _():
            acc_ref[...] += part

        @pl.when(k == nk - 1)
        def _():
            acc = acc_ref[...]
            if epilogue is None:
                outs = (acc,)
            else:
                outs = epilogue(acc, *[r[...] for r in ex_refs])
            for o_ref, o in zip(o_refs, outs):
                o_ref[...] = o.astype(o_ref.dtype)

    if mode == "nn":
        a_spec = pl.BlockSpec((tm, tk), lambda i, j, k: (i, k))
        b_spec = pl.BlockSpec((tk, tn), lambda i, j, k: (k, j))
    elif mode == "nt":
        a_spec = pl.BlockSpec((tm, tk), lambda i, j, k: (i, k))
        b_spec = pl.BlockSpec((tn, tk), lambda i, j, k: (j, k))
    else:
        a_spec = pl.BlockSpec((tk, tm), lambda i, j, k: (k, i))
        b_spec = pl.BlockSpec((tk, tn), lambda i, j, k: (k, j))
    if b_in is not None:
        b_block = (None, tn, tk) if mode == "nt" else (None, tk, tn)
        b_spec = pl.BlockSpec(b_block, lambda i, j, k: b_in[1](j, k))
    mn_spec = pl.BlockSpec((tm, tn), lambda i, j, k: (i, j))
    row_spec = pl.BlockSpec((tm, tn), lambda i, j, k: (i, 0))
    col_spec = pl.BlockSpec((1, tn), lambda i, j, k: (0, j))
    out_specs = [mn_spec] * n_out
    out_shape = [jax.ShapeDtypeStruct((M, N), dt) for dt in out_dtypes]
    aliases = {}
    if into is not None:
        buf, place = into
        assert n_out == 1 and n_ord == 1 and order is buf, name
        out_specs = [pl.BlockSpec((None, tm, tn), lambda i, j, k: place(i, j))]
        out_shape = [jax.ShapeDtypeStruct(buf.shape, buf.dtype)]
        aliases = {2 + n_ex: 0}
    outs = pl.pallas_call(
        body,
        name=name,
        grid=(M // tm, N // tn, nk),
        in_specs=([a_spec, b_spec]
                  + [pl.BlockSpec((tm, tn), lambda i, j, k, g=g: (i, j + g * (N // tn))) for _, g in extras]
                  + [row_spec] * len(row_extras) + [col_spec] * len(col_extras) + [_ANY_SPEC] * n_ord),
        out_specs=out_specs,
        out_shape=out_shape,
        scratch_shapes=[pltpu.VMEM((tm, tn), F32)],
        input_output_aliases=aliases,
        compiler_params=_params(("parallel", "parallel", "arbitrary")),
    )(a, b, *[e for e, _ in extras], *row_extras, *col_extras, *([] if order is None else [order]))
    return outs[0] if n_out == 1 else outs


def _matmul_row_parts(parts, b, out_dtype, name, *, tm=512, tk=2048, order=None):
    M, (K, N) = parts[0].shape[0], b.shape
    widths = [p.shape[1] for p in parts]
    assert sum(widths) == K, name
    tm, tk = _tile(M, tm), _tile(K, tk)
    nk = K // tk
    steps, at = [], 0
    for p, w in enumerate(widths):
        off = 0
        while off < w:
            k, room = divmod(at, tk)
            take = min(w - off, tk - room)
            if room == 0:
                steps.append([])
            steps[k].append((p, off, take, room))
            off += take
            at += take
    assert len(steps) == nk and all(t % LANE == 0 and o % LANE == 0 for s in steps for _, o, t, _ in s), name
    n_parts = len(parts)
    n_ord = 0 if order is None else 1

    def body(*refs):
        a_refs = refs[0:n_parts]
        b_ref = refs[n_parts]
        o_ref, acc_ref = refs[n_parts + 1 + n_ord], refs[n_parts + 2 + n_ord]
        k = pl.program_id(1)
        for kk, pieces in enumerate(steps):
            @pl.when(k == kk)
            def _(kk=kk, pieces=pieces):
                part = None
                for p, off, take, room in pieces:
                    d = jnp.dot(a_refs[p][:, off:off + take], b_ref[room:room + take, :], preferred_element_type=F32)
                    part = d if part is None else part + d
                if kk == 0:
                    acc_ref[...] = part
                else:
                    acc_ref[...] += part

        @pl.when(k == nk - 1)
        def _():
            o_ref[...] = acc_ref[...].astype(o_ref.dtype)

    return pl.pallas_call(
        body, name=name, grid=(M // tm, nk),
        in_specs=[pl.BlockSpec((tm, w), lambda i, k: (i, 0)) for w in widths]
        + [pl.BlockSpec((tk, N), lambda i, k: (k, 0))] + [_ANY_SPEC] * n_ord,
        out_specs=pl.BlockSpec((tm, N), lambda i, k: (i, 0)),
        out_shape=jax.ShapeDtypeStruct((M, N), out_dtype),
        scratch_shapes=[pltpu.VMEM((tm, N), F32)],
        compiler_params=_params(("parallel", "arbitrary")),
    )(*parts, b, *([] if order is None else [order]))


def _mm_tn(a, b, name, tm=1024, tn=1024, into=None):
    return _matmul(a, b, "tn", [F32], name, tm=tm, tn=tn, tk=2048, into=into,
                   order=None if into is None else into[0])


def _row_spec(ts, width, col=0):
    return pl.BlockSpec((ts, width), lambda i: (i, col))


def _full_spec(shape):
    return pl.BlockSpec(shape, lambda i: tuple(0 for _ in shape))


def _rms(x):
    return lax.rsqrt(jnp.mean(x * x, axis=-1, keepdims=True) + EPS)


def _rms_bwd(x, dy, g):
    r = _rms(x)
    xh = x * r
    gy = dy * g
    dx = r * (gy - xh * jnp.mean(xh * gy, axis=-1, keepdims=True))
    return dx, dy * xh


def _norm_fwd(x, g, name, order=None):
    S, D = x.shape
    ts = _tile(S, ROW_T, 8)

    def body(x_ref, g_ref, *rest):
        o_ref = rest[-1]
        xv = x_ref[...]
        o_ref[...] = ((xv * _rms(xv)) * g_ref[...]).astype(BF16)

    extra = [] if order is None else [order]
    return pl.pallas_call(
        body, name=name, grid=(S // ts,),
        in_specs=[_row_spec(ts, D), _full_spec((1, D))] + [_ANY_SPEC] * len(extra),
        out_specs=_row_spec(ts, D),
        out_shape=jax.ShapeDtypeStruct((S, D), BF16),
        compiler_params=_params(("parallel",)),
    )(x, g, *extra)


def _norm_bwd(x, dy, g, dres, name, with_bf16=True):
    S, D = x.shape
    ts = _tile(S, ROW_T, 8)

    def body(x_ref, dy_ref, g_ref, dres_ref, dx_ref, *rest):
        dg_ref = rest[-1]
        dx, dg_rows = _rms_bwd(x_ref[...], dy_ref[...], g_ref[...])
        dx = dres_ref[...] + dx
        dx_ref[...] = dx
        if with_bf16:
            rest[0][...] = dx.astype(BF16)

        @pl.when(pl.program_id(0) == 0)
        def _():
            dg_ref[...] = jnp.zeros_like(dg_ref)

        dg_ref[...] += jnp.sum(dg_rows, axis=0, keepdims=True)

    return pl.pallas_call(
        body, name=name, grid=(S // ts,),
        in_specs=[_row_spec(ts, D), _row_spec(ts, D), _full_spec((1, D)), _row_spec(ts, D)],
        out_specs=[_row_spec(ts, D)] * (2 if with_bf16 else 1) + [_full_spec((1, D))],
        out_shape=([jax.ShapeDtypeStruct((S, D), F32)] + [jax.ShapeDtypeStruct((S, D), BF16)] * with_bf16
                   + [jax.ShapeDtypeStruct((1, D), F32)]),
        compiler_params=_params(("arbitrary",)),
    )(x, dy, g, dres)


def _rope(x, c, sa, sb, sign):
    w = x.shape[-1]
    half = MLA_ROPE // 2
    fwd = pltpu.roll(x, w - half, 1)
    back = pltpu.roll(x, half, 1)
    if sign < 0:
        return x * c - fwd * sa - back * sb
    return x * c + fwd * sa + back * sb


def _split3(x):
    hi = x.astype(BF16)
    r1 = x - hi.astype(F32)
    mid = r1.astype(BF16)
    lo = (r1 - mid.astype(F32)).astype(BF16)
    return hi, mid, lo


def _prep_fwd(small, q_norm, kv_norm, bias_pad, kc, ksa, ksb, n_heads, name):
    S, W = small.shape
    QL, KVL = q_norm.shape[1], kv_norm.shape[1]
    assert W == QL + KVL + 2 * LANE
    ts = _tile(S, ROW_T, 8)
    tri = (lax.broadcasted_iota(jnp.int32, (ts, ts), 0) >= lax.broadcasted_iota(jnp.int32, (ts, ts), 1)).astype(BF16)

    def body(s_ref, qn_ref, kvn_ref, b_ref, kc_ref, ksa_ref, ksb_ref, tri_ref,
             cqn_ref, ckvn_ref, kr_ref, cum_ref, carry_ref):
        cq = s_ref[:, 0:QL]
        cqn_ref[...] = ((cq * _rms(cq)) * qn_ref[...]).astype(BF16)
        ckv = s_ref[:, QL:QL + KVL]
        ckvn_ref[...] = ((ckv * _rms(ckv)) * kvn_ref[...]).astype(BF16)
        kr = s_ref[:, QL + KVL:QL + KVL + LANE]
        kr_ref[...] = _rope(kr, kc_ref[...], ksa_ref[...], ksb_ref[...], 1).astype(BF16)
        z = s_ref[:, QL + KVL + LANE:W] + b_ref[...]
        logf = jnp.minimum(z, 0.0) - jnp.log1p(jnp.exp(-jnp.abs(z)))
        lane = lax.broadcasted_iota(jnp.int32, logf.shape, 1)
        logf = jnp.where(lane < n_heads, logf, 0.0)

        @pl.when(pl.program_id(0) == 0)
        def _():
            carry_ref[...] = jnp.zeros_like(carry_ref)

        t = tri_ref[...]
        cum = carry_ref[...]
        for part in _split3(logf):
            cum = cum + jnp.dot(t, part, preferred_element_type=F32)
        cum_ref[...] = cum
        carry_ref[...] = cum[ts - 1:ts, :]

    return pl.pallas_call(
        body, name=name, grid=(S // ts,),
        in_specs=[_row_spec(ts, W), _full_spec((1, QL)), _full_spec((1, KVL)), _full_spec((1, LANE)),
                  _row_spec(ts, LANE), _row_spec(ts, LANE), _row_spec(ts, LANE), _full_spec((ts, ts))],
        out_specs=[_row_spec(ts, QL), _row_spec(ts, KVL), _row_spec(ts, LANE), _row_spec(ts, LANE)],
        out_shape=[jax.ShapeDtypeStruct((S, QL), BF16), jax.ShapeDtypeStruct((S, KVL), BF16),
                   jax.ShapeDtypeStruct((S, LANE), BF16), jax.ShapeDtypeStruct((S, LANE), F32)],
        scratch_shapes=[pltpu.VMEM((1, LANE), F32)],
        compiler_params=_params(("arbitrary",)),
    )(small, q_norm, kv_norm, bias_pad, kc, ksa, ksb, tri)


def _prep_bwd(small, dcqn, dckvn, dkr_heads, dlogf, q_norm, kv_norm, bias_pad, kc, ksa, ksb, n_heads, name):
    S, W = small.shape
    QL, KVL = q_norm.shape[1], kv_norm.shape[1]
    ts = _tile(S, ROW_T, 8)

    def body(s_ref, dcq_ref, dckv_ref, dkr_ref, dlf_ref, qn_ref, kvn_ref, b_ref, kc_ref, ksa_ref, ksb_ref,
             ds_ref, gq_ref, gkv_ref, gb_ref):
        dcq, gq_rows = _rms_bwd(s_ref[:, 0:QL], dcq_ref[...], qn_ref[...])
        ds_ref[:, 0:QL] = dcq.astype(BF16)
        dckv, gkv_rows = _rms_bwd(s_ref[:, QL:QL + KVL], dckv_ref[...], kvn_ref[...])
        ds_ref[:, QL:QL + KVL] = dckv.astype(BF16)
        dkr = dkr_ref[:, 0:LANE]
        for h in range(1, n_heads):
            dkr = dkr + dkr_ref[:, h * LANE:(h + 1) * LANE]
        ds_ref[:, QL + KVL:QL + KVL + LANE] = _rope(dkr, kc_ref[...], ksa_ref[...], ksb_ref[...], -1).astype(BF16)
        z = s_ref[:, QL + KVL + LANE:W] + b_ref[...]
        dff = dlf_ref[...] * (1.0 / (1.0 + jnp.exp(z)))
        ds_ref[:, QL + KVL + LANE:W] = dff.astype(BF16)

        @pl.when(pl.program_id(0) == 0)
        def _():
            gq_ref[...] = jnp.zeros_like(gq_ref)
            gkv_ref[...] = jnp.zeros_like(gkv_ref)
            gb_ref[...] = jnp.zeros_like(gb_ref)

        gq_ref[...] += jnp.sum(gq_rows, axis=0, keepdims=True)
        gkv_ref[...] += jnp.sum(gkv_rows, axis=0, keepdims=True)
        gb_ref[...] += jnp.sum(dff, axis=0, keepdims=True)

    return pl.pallas_call(
        body, name=name, grid=(S // ts,),
        in_specs=[_row_spec(ts, W), _row_spec(ts, QL), _row_spec(ts, KVL), _row_spec(ts, n_heads * LANE),
                  _row_spec(ts, LANE), _full_spec((1, QL)), _full_spec((1, KVL)), _full_spec((1, LANE)),
                  _row_spec(ts, LANE), _row_spec(ts, LANE), _row_spec(ts, LANE)],
        out_specs=[_row_spec(ts, W), _full_spec((1, QL)), _full_spec((1, KVL)), _full_spec((1, LANE))],
        out_shape=[jax.ShapeDtypeStruct((S, W), BF16), jax.ShapeDtypeStruct((1, QL), F32),
                   jax.ShapeDtypeStruct((1, KVL), F32), jax.ShapeDtypeStruct((1, LANE), F32)],
        compiler_params=_params(("arbitrary",)),
    )(small, dcqn, dckvn, dkr_heads, dlogf, q_norm, kv_norm, bias_pad, kc, ksa, ksb)


def _sigmoid(z):
    return 1.0 / (1.0 + jnp.exp(-z))


def _final(h, g, target, name):
    S, D = h.shape
    ts = _tile(S, ROW_T, 8)

    def body(h_ref, g_ref, t_ref, dh_ref, dhb_ref, dg_ref, loss_ref):
        hv = h_ref[...]
        gv = g_ref[...]
        err = (hv * _rms(hv)) * gv - t_ref[...]
        dh, dg_rows = _rms_bwd(hv, err / D, gv)
        dh_ref[...] = dh
        dhb_ref[...] = dh.astype(BF16)

        @pl.when(pl.program_id(0) == 0)
        def _():
            dg_ref[...] = jnp.zeros_like(dg_ref)
            loss_ref[...] = jnp.zeros_like(loss_ref)

        dg_ref[...] += jnp.sum(dg_rows, axis=0, keepdims=True)
        row_loss = jnp.mean(err * err, axis=-1, keepdims=True)
        loss_ref[...] += 0.5 * jnp.sum(row_loss, axis=0, keepdims=True)

    return pl.pallas_call(
        body, name=name, grid=(S // ts,),
        in_specs=[_row_spec(ts, D), _full_spec((1, D)), _row_spec(ts, D)],
        out_specs=[_row_spec(ts, D), _row_spec(ts, D), _full_spec((1, D)), _full_spec((1, LANE))],
        out_shape=[jax.ShapeDtypeStruct((S, D), F32), jax.ShapeDtypeStruct((S, D), BF16),
                   jax.ShapeDtypeStruct((1, D), F32), jax.ShapeDtypeStruct((1, LANE), F32)],
        compiler_params=_params(("arbitrary",)),
    )(h, g, target)


def _suffix_sum_rows(x, name):
    R, S = x.shape
    tb = _tile(S, 512)
    nb = S // tb
    tri = (lax.broadcasted_iota(jnp.int32, (tb, tb), 0) >= lax.broadcasted_iota(jnp.int32, (tb, tb), 1)).astype(BF16)

    def body(x_ref, tri_ref, o_ref, carry_ref):
        @pl.when(pl.program_id(0) == 0)
        def _():
            carry_ref[...] = jnp.zeros_like(carry_ref)

        xv = x_ref[...]
        t = tri_ref[...]
        acc = jnp.broadcast_to(carry_ref[:, 0:1], xv.shape)
        for part in _split3(xv):
            acc = acc + jnp.dot(part, t, preferred_element_type=F32)
        o_ref[...] = acc
        carry_ref[...] = jnp.broadcast_to(acc[:, 0:1], carry_ref.shape)

    rev = pl.BlockSpec((R, tb), lambda i: (0, nb - 1 - i))
    return pl.pallas_call(
        body, name=name, grid=(nb,),
        in_specs=[rev, _full_spec((tb, tb))], out_specs=rev,
        out_shape=jax.ShapeDtypeStruct((R, S), F32),
        scratch_shapes=[pltpu.VMEM((R, LANE), F32)],
        compiler_params=_params(("arbitrary",)),
    )(x, tri)


def _pairs(nb, by_key):
    if by_key:
        pr = [(i, j) for j in range(nb) for i in range(j, nb)]
    else:
        pr = [(i, j) for i in range(nb) for j in range(i + 1)]
    return (jnp.asarray([p[0] for p in pr], jnp.int32), jnp.asarray([p[1] for p in pr], jnp.int32), len(pr))


class _AttT:
    def __init__(self, S, n_heads, q, ks, v, scale, chunk_causal, cum_rep=None, qsub=None):
        self.S, self.H, self.q, self.ks, self.v = S, n_heads, q, ks, v
        self.scale, self.chunk_causal, self.cum_rep = scale, chunk_causal, cum_rep
        self.T = _tile(S, ATT_T)
        self.qs = min(qsub or QSUB, self.T)
        self.nb = S // self.T
        self.dq, self.dv = q[1], v[1]
        self.has_bias = cum_rep is not None

    def q_spec(self, op):
        _, w, off, per_head = op
        return pl.BlockSpec((self.T, w), lambda h, p, it, jt: (it[p], off + (h if per_head else 0)))

    def k_spec(self, op):
        _, w, off, per_head = op
        return pl.BlockSpec((self.T, w), lambda h, p, it, jt: (jt[p], off + (h if per_head else 0)))

    def row_q(self):
        return pl.BlockSpec((None, 1, self.T), lambda h, p, it, jt: (h, 0, it[p]))

    def cum_k(self):
        return pl.BlockSpec((None, self.T, self.qs), lambda h, p, it, jt: (h, jt[p], 0))

    def sub_blocks(self, masked):
        return [(q0, min(self.T, q0 + self.qs) if masked else self.T) for q0 in range(0, self.T, self.qs)]

    def scores(self, k, q_sub, cum, q0, masked):
        s = lax.dot_general(k, q_sub, _NT, preferred_element_type=F32)
        if self.has_bias:
            s = s - cum
        mask = None
        if masked:
            r = lax.broadcasted_iota(jnp.int32, s.shape, 0)
            c = lax.broadcasted_iota(jnp.int32, s.shape, 1) + q0
            mask = (r // CHUNK <= c // CHUNK) if self.chunk_causal else (r <= c)
        return s, mask


def _join(k_refs):
    return k_refs[0][...] if len(k_refs) == 1 else jnp.concatenate([r[...] for r in k_refs], axis=-1)


def _att_fwd_t(att, name, exact=False):
    S, H, T, qs = att.S, att.H, att.T, att.qs
    it, jt, npairs = _pairs(att.nb, by_key=False)
    nk = len(att.ks)

    def body(it_ref, jt_ref, *refs):
        q_ref = refs[0]
        k_refs = refs[1:1 + nk]
        v_ref = refs[1 + nk]
        n = 2 + nk
        cum_ref = None
        if att.has_bias:
            cum_ref = refs[n]
            n += 1
        o_ref = refs[n]
        n += 1
        ox_ref = None
        if exact:
            ox_ref = refs[n]
            n += 1
        lse_ref, m_ref, l_ref, acc_ref = refs[n:n + 4]
        lo_ref = refs[n + 4] if exact else None
        p = pl.program_id(1)
        i, j = it_ref[p], jt_ref[p]

        @pl.when(j == 0)
        def _():
            m_ref[...] = jnp.full_like(m_ref, -jnp.inf)
            l_ref[...] = jnp.zeros_like(l_ref)
            acc_ref[...] = jnp.zeros_like(acc_ref)
            if exact:
                lo_ref[...] = jnp.zeros_like(lo_ref)

        def step(masked):
            k = _join(k_refs)
            v = v_ref[...]
            subs = att.sub_blocks(masked)

            def logits(idx):
                q0, nkeys = subs[idx]
                cum = cum_ref[0:nkeys, :] if att.has_bias else None
                return att.scores(k[0:nkeys], q_ref[q0:q0 + qs, :], cum, q0, masked)

            ahead = logits(0)
            for idx, (q0, nkeys) in enumerate(subs):
                qsl = slice(q0, q0 + qs)
                s, mask = ahead
                if idx + 1 < len(subs):
                    ahead = logits(idx + 1)
                if masked:
                    s = jnp.where(mask, s, -jnp.inf)
                m_prev = m_ref[:, qsl]
                m_new = jnp.maximum(m_prev, jnp.max(s, axis=0, keepdims=True))
                alpha = jnp.exp2(m_prev - m_new)
                pr = jnp.exp2(s - m_new)
                l_ref[:, qsl] = alpha * l_ref[:, qsl] + jnp.sum(pr, axis=0, keepdims=True)
                p_hi = pr.astype(BF16)
                acc_ref[:, qsl] = alpha * acc_ref[:, qsl] + lax.dot_general(
                    v[0:nkeys], p_hi, _TN, preferred_element_type=F32)
                if exact:
                    p_lo = (pr - p_hi.astype(F32)).astype(BF16)
                    lo_ref[:, qsl] = alpha * lo_ref[:, qsl] + lax.dot_general(
                        v[0:nkeys], p_lo, _TN, preferred_element_type=F32)
                m_ref[:, qsl] = m_new

        @pl.when(j < i)
        def _():
            step(False)

        @pl.when(j == i)
        def _():
            step(True)
            l = l_ref[...]
            inv = 1.0 / l
            o_ref[...] = jnp.transpose(acc_ref[...] * inv).astype(o_ref.dtype)
            if exact:
                ox_ref[...] = jnp.transpose((acc_ref[...] + lo_ref[...]) * inv)
            lse_ref[...] = m_ref[...] + jnp.log2(l)

    in_specs = [att.q_spec(att.q)] + [att.k_spec(k) for k in att.ks] + [att.k_spec(att.v)]
    args = [att.q[0]] + [k[0] for k in att.ks] + [att.v[0]]
    if att.has_bias:
        in_specs.append(att.cum_k())
        args.append(att.cum_rep)
    o_spec = pl.BlockSpec((T, att.dv), lambda h, p, it, jt: (it[p], h))
    out_specs = [o_spec]
    out_shape = [jax.ShapeDtypeStruct((S, H * att.dv), BF16)]
    scratch = [pltpu.VMEM((1, T), F32), pltpu.VMEM((1, T), F32), pltpu.VMEM((att.dv, T), F32)]
    if exact:
        out_specs.append(o_spec)
        out_shape.append(jax.ShapeDtypeStruct((S, H * att.dv), F32))
        scratch.append(pltpu.VMEM((att.dv, T), F32))
    out_specs.append(att.row_q())
    out_shape.append(jax.ShapeDtypeStruct((H, 1, S), F32))
    return pl.pallas_call(
        body, name=name,
        grid_spec=pltpu.PrefetchScalarGridSpec(
            num_scalar_prefetch=2, grid=(H, npairs), in_specs=in_specs, out_specs=out_specs,
            scratch_shapes=scratch),
        out_shape=out_shape,
        compiler_params=_params(("parallel", "arbitrary")),
    )(it, jt, *args)


def _att_bwd_t(att, do, lse, o, dq_dtype, dk_dtypes, name, dq_rope=None, order=None):
    S, H, T, qs = att.S, att.H, att.T, att.qs
    it, jt, npairs = _pairs(att.nb, by_key=True)
    nk = len(att.ks)
    last = att.nb - 1
    widths = [k[1] for k in att.ks]

    def body(it_ref, jt_ref, *refs):
        q_ref = refs[0]
        k_refs = refs[1:1 + nk]
        v_ref, do_ref, lse_ref, o_ref = refs[1 + nk:5 + nk]
        n = 5 + nk
        cum_ref = None
        if att.has_bias:
            cum_ref = refs[n]
            n += 1
        rope_refs = None
        if dq_rope is not None:
            rope_refs = refs[n:n + 3]
            n += 3
        if order is not None:
            n += 1
        dl_acc = refs[-1]
        dq_ref = refs[n]
        dk_refs = refs[n + 1:n + 1 + nk]
        dv_ref = refs[n + 1 + nk]
        n += nk + 2
        dc_ref = None
        if att.has_bias:
            dc_ref = refs[n]
            n += 1
        dq_acc, dk_acc, dv_acc = refs[n:n + 3]
        dc_acc = refs[n + 3] if att.has_bias else None
        p = pl.program_id(1)
        i, j = it_ref[p], jt_ref[p]

        @pl.when(p == 0)
        def _():
            dq_acc[...] = jnp.zeros_like(dq_acc)

        @pl.when(i == j)
        def _():
            dk_acc[...] = jnp.zeros_like(dk_acc)
            dv_acc[...] = jnp.zeros_like(dv_acc)
            if att.has_bias:
                dc_acc[...] = jnp.zeros_like(dc_acc)

        @pl.when(j == 0)
        def _():
            prod = do_ref[...].astype(F32) * o_ref[...].astype(F32)
            ones = jnp.ones((8, att.dv), BF16)
            rows = jnp.zeros((8, T), F32)
            for part in _split3(prod):
                rows = rows + lax.dot_general(ones, part, _NT, preferred_element_type=F32)
            dl_acc[i] = rows[0:1, :]

        def step(masked):
            k = _join(k_refs)
            v = v_ref[...]
            dl = dl_acc[i]
            subs = att.sub_blocks(masked)

            def logits(idx):
                q0, nkeys = subs[idx]
                cum = cum_ref[0:nkeys, :] if att.has_bias else None
                return att.scores(k[0:nkeys], q_ref[q0:q0 + qs, :], cum, q0, masked)

            ahead = logits(0)
            for idx, (q0, nkeys) in enumerate(subs):
                qsl = slice(q0, q0 + qs)
                ksl = slice(0, nkeys)
                q_sub = q_ref[qsl, :]
                do_sub = do_ref[qsl, :]
                s, mask = ahead
                if idx + 1 < len(subs):
                    ahead = logits(idx + 1)
                pr = jnp.exp2(s - lse_ref[:, qsl])
                if masked:
                    pr = jnp.where(mask, pr, 0.0)
                dp = lax.dot_general(v[ksl], do_sub, _NT, preferred_element_type=F32)
                ds = pr * (dp - dl[:, qsl])
                ds_b = ds.astype(BF16)
                dv_acc[ksl, :] += jnp.dot(pr.astype(BF16), do_sub, preferred_element_type=F32)
                dk_acc[ksl, :] += jnp.dot(ds_b, q_sub, preferred_element_type=F32)
                dq_acc[i, :, qsl] += lax.dot_general(k[ksl], ds_b, _TN, preferred_element_type=F32)
                if att.has_bias:
                    part = ds[:, 0:LANE] if qs >= LANE else ds
                    for c0 in range(LANE, qs, LANE):
                        part = part + ds[:, c0:c0 + LANE]
                    dc_acc[ksl, :] += part

        @pl.when(i > j)
        def _():
            step(False)

        @pl.when(i == j)
        def _():
            step(True)
            dq = jnp.transpose(dq_acc[i] * att.scale)
            if dq_rope is not None:
                dq = _rope(dq, rope_refs[0][...], rope_refs[1][...], rope_refs[2][...], -1)
            dq_ref[...] = dq.astype(dq_ref.dtype)

        @pl.when(i == last)
        def _():
            dk = dk_acc[...] * (1.0 / LOG2E)
            off = 0
            for r, w in zip(dk_refs, widths):
                r[...] = dk[:, off:off + w].astype(r.dtype)
                off += w
            dv_ref[...] = dv_acc[...].astype(dv_ref.dtype)
            if att.has_bias:
                dc_ref[...] = -jnp.sum(dc_acc[...], axis=-1, keepdims=True)

    do_op = (do, att.dv, 0, True)
    o_spec = pl.BlockSpec((T, att.dv), lambda h, p, it, jt: (jnp.where(jt[p] == 0, it[p], last), h))
    in_specs = ([att.q_spec(att.q)] + [att.k_spec(k) for k in att.ks]
                + [att.k_spec(att.v), att.q_spec(do_op), att.row_q(), o_spec])
    args = [att.q[0]] + [k[0] for k in att.ks] + [att.v[0], do, lse, o]
    if att.has_bias:
        in_specs.append(att.cum_k())
        args.append(att.cum_rep)
    if dq_rope is not None:
        in_specs += [pl.BlockSpec((T, att.dq), lambda h, p, it, jt: (jt[p], 0))] * 3
        args += list(dq_rope)
    if order is not None:
        in_specs.append(_ANY_SPEC)
        args.append(order)
    out_specs = [pl.BlockSpec((T, att.dq), lambda h, p, it, jt: (jt[p], h))]
    out_shape = [jax.ShapeDtypeStruct((S, H * att.dq), dq_dtype)]
    out_specs += [pl.BlockSpec((T, w), lambda h, p, it, jt: (jt[p], h)) for w in widths]
    out_shape += [jax.ShapeDtypeStruct((S, H * w), dt) for w, dt in zip(widths, dk_dtypes)]
    out_specs.append(pl.BlockSpec((T, att.dv), lambda h, p, it, jt: (jt[p], h)))
    out_shape.append(jax.ShapeDtypeStruct((S, H * att.dv), BF16))
    scratch = [pltpu.VMEM((att.nb, att.dq, T), F32), pltpu.VMEM((T, att.dq), F32), pltpu.VMEM((T, att.dv), F32)]
    if att.has_bias:
        out_specs.append(pl.BlockSpec((None, T, 1), lambda h, p, it, jt: (h, jt[p], 0)))
        out_shape.append(jax.ShapeDtypeStruct((H, S, 1), F32))
        scratch.append(pltpu.VMEM((T, min(qs, LANE)), F32))
    scratch.append(pltpu.VMEM((att.nb, 1, T), F32))
    return pl.pallas_call(
        body, name=name,
        grid_spec=pltpu.PrefetchScalarGridSpec(
            num_scalar_prefetch=2, grid=(H, npairs), in_specs=in_specs, out_specs=out_specs,
            scratch_shapes=scratch),
        out_shape=out_shape,
        compiler_params=_params(("parallel", "arbitrary")),
    )(it, jt, *args)


def _adamw(w, g1, g2, m, v, name, g_row=None):
    _, K, N = w.shape
    by_rows = K % 8 == 0
    tr = _tile(K, 256, 8) if by_rows else K
    if g_row is None:
        assert g1.shape == (K, N) and g2.shape == (K, N), name
        g_row = 0
    assert by_rows and g_row % tr == 0 or g_row == 0, name
    g_blk = g_row // tr
    tc = N if by_rows else _tile(N, LANE)
    c1 = 1.0 - ADAM_B1 ** ADAM_STEP
    c2 = 1.0 - ADAM_B2 ** ADAM_STEP

    def body(w_ref, g1_ref, g2_ref, m_ref, v_ref, g_ref, d_ref, nm_ref, nv_ref):
        gv = g1_ref[...] + g2_ref[...]
        nm = ADAM_B1 * m_ref[...] + (1.0 - ADAM_B1) * gv
        nv = ADAM_B2 * v_ref[...] + (1.0 - ADAM_B2) * (gv * gv)
        g_ref[...] = gv
        d_ref[...] = -ADAM_LR * ((nm / c1) / (jnp.sqrt(nv / c2) + ADAM_EPS) + ADAM_WD * w_ref[...])
        nm_ref[...] = nm
        nv_ref[...] = nv

    if by_rows:
        blk = pl.BlockSpec((None, tr, N), lambda i: (0, i, 0))
        gblk = pl.BlockSpec((tr, N), lambda i: (g_blk + i, 0))
    else:
        blk = pl.BlockSpec((None, K, tc), lambda i: (0, 0, i))
        gblk = pl.BlockSpec((K, tc), lambda i: (0, i))
    return pl.pallas_call(
        body, name=name, grid=(K // tr if by_rows else N // tc,),
        in_specs=[blk, gblk, gblk, blk, blk], out_specs=[blk] * 4,
        out_shape=[jax.ShapeDtypeStruct((1, K, N), F32)] * 4,
        compiler_params=_params(("parallel",)),
    )(w, g1, g2, m, v)


_HBM_SPEC = pl.BlockSpec(memory_space=pltpu.HBM)
_SEM_SPEC = pl.BlockSpec(memory_space=pltpu.SEMAPHORE)
_VMEM_SPEC = pl.BlockSpec(memory_space=pltpu.VMEM)
_EFFECT = pltpu.SideEffectType.DATAFLOW_SIDE_EFFECTING


def _place():
    return lax.axis_index("x"), lax.axis_index("y"), lax.axis_index("c")


def _other_chips(x, y):
    return [(1 - x, y), (x, 1 - y), (1 - x, 1 - y)]


def _chip_copies(src_ref, land_ref, sems, gather):
    x, y, c = _place()
    me = 2 * x + y
    out, back = [], []
    if gather == "half":
        half = src_ref.shape[0] // 2
        mine = pl.ds(pl.multiple_of(c * half, 16), half)
    for n, (px, py) in enumerate(_other_chips(x, y)):
        if gather == "half":
            src, there, here = src_ref.at[mine], land_ref.at[me, mine], land_ref.at[2 * px + py, mine]
        elif gather:
            src, there, here = src_ref, land_ref.at[me], land_ref.at[2 * px + py]
        else:
            src, there, here = src_ref.at[2 * px + py], land_ref.at[n], land_ref.at[n]
        out.append(pltpu.make_async_remote_copy(
            src_ref=src, dst_ref=there, send_sem=sems[n], recv_sem=sems[3 + n],
            device_id=(px, py, c), device_id_type=MESH))
        back.append(pltpu.make_async_remote_copy(
            src_ref=src, dst_ref=here, send_sem=sems[n], recv_sem=sems[3 + n],
            device_id=(px, py, c), device_id_type=MESH))
    return out, back


def _xchg_start(src, land, gather, order, name):
    def body(src_ref, land_ref, order_ref, *outs):
        sems = outs[0:6]
        token = outs[8]
        out, _ = _chip_copies(src_ref, land_ref, sems, gather)
        for cp in out:
            cp.start()
        token[...] = jnp.zeros_like(token)

    outs = pl.pallas_call(
        body, name=name,
        out_shape=(pltpu.SemaphoreType.DMA(()),) * 6 + (
            pltpu.HBM(src.shape, src.dtype), pltpu.HBM(land.shape, land.dtype),
            jax.ShapeDtypeStruct((8, LANE), F32)),
        in_specs=(_HBM_SPEC, _HBM_SPEC, _ANY_SPEC),
        out_specs=(_SEM_SPEC,) * 6 + (_HBM_SPEC, _HBM_SPEC, _VMEM_SPEC),
        input_output_aliases={0: 6, 1: 7},
        compiler_params=pltpu.CompilerParams(has_side_effects=_EFFECT),
    )(pltpu.with_memory_space_constraint(src, pltpu.HBM), pltpu.with_memory_space_constraint(land, pltpu.HBM), order)
    return outs[0:6], outs[6], outs[7], outs[8]


def _xchg_wait(started, gather, after, name):
    sems, src, land, _ = started
    after = after if isinstance(after, tuple) else (after,)

    def body(src_ref, land_ref, *rest):
        _, back = _chip_copies(src_ref, land_ref, rest[0:6], gather)
        for cp in back:
            cp.wait_send()
            cp.wait_recv()

    return pl.pallas_call(
        body, name=name,
        out_shape=(pltpu.HBM(src.shape, src.dtype), pltpu.HBM(land.shape, land.dtype)),
        in_specs=(_HBM_SPEC, _HBM_SPEC) + (_SEM_SPEC,) * 6 + (_ANY_SPEC,) * len(after),
        out_specs=(_HBM_SPEC, _HBM_SPEC),
        input_output_aliases={0: 0, 1: 1},
        compiler_params=pltpu.CompilerParams(has_side_effects=_EFFECT),
    )(src, land, *sems, *after)


def _forward_halves(land, name):
    _, R, C = land.shape
    half = R // 2
    assert half % 16 == 0

    def body(land_ref, out_ref, send_sems, recv_sems):
        x, y, c = _place()
        mine = pl.ds(pl.multiple_of(c * half, 16), half)
        theirs = pl.ds(pl.multiple_of((1 - c) * half, 16), half)
        sends = []
        for n, (px, py) in enumerate(_other_chips(x, y)):
            cp = pltpu.make_async_remote_copy(
                src_ref=land_ref.at[2 * px + py, mine], dst_ref=out_ref.at[2 * px + py, mine],
                send_sem=send_sems.at[n], recv_sem=recv_sems.at[n], device_id=(x, y, 1 - c), device_id_type=MESH)
            cp.start()
            sends.append(cp)
        for n, (px, py) in enumerate(_other_chips(x, y)):
            pltpu.make_async_remote_copy(
                src_ref=land_ref.at[2 * px + py, theirs], dst_ref=out_ref.at[2 * px + py, theirs],
                send_sem=send_sems.at[n], recv_sem=recv_sems.at[n], device_id=(x, y, 1 - c),
                device_id_type=MESH).wait_recv()
        for cp in sends:
            cp.wait_send()

    return pl.pallas_call(
        body, name=name,
        in_specs=[_ANY_SPEC], out_specs=_ANY_SPEC,
        out_shape=jax.ShapeDtypeStruct(land.shape, land.dtype),
        input_output_aliases={0: 0},
        scratch_shapes=[pltpu.SemaphoreType.DMA((3,)), pltpu.SemaphoreType.DMA((3,))],
    )(land)


def _sib_copy(src_ref, land_ref, send_sem, recv_sem):
    x, y, c = _place()
    return pltpu.make_async_remote_copy(src_ref=src_ref, dst_ref=land_ref, send_sem=send_sem, recv_sem=recv_sem,
                                        device_id=(x, y, 1 - c), device_id_type=MESH)


def _sib_start(src, name):
    land = lax.empty(src.shape, src.dtype)

    def body(src_ref, land_ref, send_sem, recv_sem, src_thru, land_thru, token):
        _sib_copy(src_ref, land_ref, send_sem, recv_sem).start()
        token[...] = jnp.zeros_like(token)

    return pl.pallas_call(
        body, name=name,
        out_shape=(pltpu.SemaphoreType.DMA(()), pltpu.SemaphoreType.DMA(()),
                   pltpu.HBM(src.shape, src.dtype), pltpu.HBM(land.shape, land.dtype),
                   jax.ShapeDtypeStruct((8, LANE), F32)),
        in_specs=(_HBM_SPEC, _HBM_SPEC),
        out_specs=(_SEM_SPEC, _SEM_SPEC, _HBM_SPEC, _HBM_SPEC, _VMEM_SPEC),
        input_output_aliases={0: 2, 1: 3},
        compiler_params=pltpu.CompilerParams(has_side_effects=_EFFECT),
    )(pltpu.with_memory_space_constraint(src, pltpu.HBM), pltpu.with_memory_space_constraint(land, pltpu.HBM))


def _sib_wait(started, after, name):
    send_sem, recv_sem, src, land, _ = started

    def body(src_ref, land_ref, send_sem, recv_sem, after_ref, src_out, land_out):
        cp = _sib_copy(src_ref, land_ref, send_sem, recv_sem)
        cp.wait_send()
        cp.wait_recv()

    return pl.pallas_call(
        body, name=name,
        out_shape=(pltpu.HBM(src.shape, src.dtype), pltpu.HBM(land.shape, land.dtype)),
        in_specs=(_HBM_SPEC, _HBM_SPEC, _SEM_SPEC, _SEM_SPEC, _ANY_SPEC),
        out_specs=(_HBM_SPEC, _HBM_SPEC),
        input_output_aliases={0: 0, 1: 1},
        compiler_params=pltpu.CompilerParams(has_side_effects=_EFFECT),
    )(src, land, send_sem, recv_sem, after)


def _sum_slabs(gp, recv, chip, name):
    _, R, C = gp.shape
    tr = _tile(R, PACK_ROWS, 16)

    def body(chip_ref, own_ref, r0_ref, r1_ref, r2_ref, o_ref):
        acc = own_ref[...].astype(F32) + r0_ref[...].astype(F32)
        o_ref[...] = (acc + r1_ref[...].astype(F32)) + r2_ref[...].astype(F32)

    def got(n):
        return pl.BlockSpec((None, tr, C), lambda i, chip_ref: (n, i, 0))

    return pl.pallas_call(
        body, name=name,
        grid_spec=pltpu.PrefetchScalarGridSpec(
            num_scalar_prefetch=1, grid=(R // tr,),
            in_specs=[pl.BlockSpec((None, tr, C), lambda i, chip_ref: (chip_ref[0], i, 0)), got(0), got(1), got(2)],
            out_specs=pl.BlockSpec((tr, C), lambda i, chip_ref: (i, 0))),
        out_shape=jax.ShapeDtypeStruct((R, C), F32),
        compiler_params=_params(("parallel",)),
    )(jnp.reshape(chip, (1,)).astype(jnp.int32), gp, recv, recv, recv)


def _all_reduce_vec(vec, name):
    VR, W = vec.shape

    def body(vec_ref, vall_ref, vout_ref, vsend_sems, vrecv_sems):
        x, y, c = _place()
        vall_ref[4 * x + 2 * y + c] = vec_ref[...]
        sends = []
        peers = []
        for r in range(1, N_DEV):
            dx, dy, dc = (r >> 2) & 1, (r >> 1) & 1, r & 1
            peer = (x ^ dx, y ^ dy, c ^ dc)
            peers.append(peer)
            cp = pltpu.make_async_remote_copy(
                src_ref=vec_ref, dst_ref=vall_ref.at[4 * x + 2 * y + c], send_sem=vsend_sems.at[r - 1],
                recv_sem=vrecv_sems.at[r - 1], device_id=peer, device_id_type=MESH)
            cp.start()
            sends.append(cp)
        for r, peer in enumerate(peers):
            pltpu.make_async_remote_copy(
                src_ref=vec_ref, dst_ref=vall_ref.at[4 * peer[0] + 2 * peer[1] + peer[2]],
                send_sem=vsend_sems.at[r], recv_sem=vrecv_sems.at[r],
                device_id=peer, device_id_type=MESH).wait_recv()
        total = vall_ref[0]
        for d in range(1, N_DEV):
            total = total + vall_ref[d]
        vout_ref[...] = total
        for cp in sends:
            cp.wait_send()

    outs = pl.pallas_call(
        body, name=name,
        in_specs=[_VMEM_SPEC], out_specs=[_VMEM_SPEC, _VMEM_SPEC],
        out_shape=[jax.ShapeDtypeStruct((N_DEV, VR, W), F32), jax.ShapeDtypeStruct((VR, W), F32)],
        scratch_shapes=[pltpu.SemaphoreType.DMA((N_DEV - 1,)), pltpu.SemaphoreType.DMA((N_DEV - 1,))],
    )(vec)
    return outs[1]


class _Pack:
    def __init__(self, group, C):
        self.group, self.C = group, C
        self.rows, self.offs, off = {}, {}, 0
        for nm, (K, N), _ in group:
            assert N <= C, nm
            self.rows[nm] = K if 2 * N > C else -(-(K * N) // C)
            self.offs[nm] = off
            off += -(-self.rows[nm] // 16) * 16
        self.used = off
        self.R = -(-off // PACK_ROWS) * PACK_ROWS

    def _rows_of(self, a):
        K, N = a.shape
        if 2 * N > self.C:
            a = jnp.pad(a, ((0, 0), (0, self.C - N)))
        else:
            a = jnp.pad(a.reshape(-1), (0, -(K * N) % self.C)).reshape(-1, self.C)
        return jnp.pad(a, ((0, -a.shape[0] % 16), (0, 0)))

    def pack(self, shards):
        parts = [self._rows_of(shards[nm].astype(BF16)) for nm, _, _ in self.group]
        return jnp.concatenate(parts + [jnp.zeros((self.R - self.used, self.C), BF16)], axis=0)

    def _shard_of(self, rows, shape):
        K, N = shape
        return rows[:, :N] if 2 * N > self.C else rows.reshape(-1)[:K * N].reshape(K, N)

    def part(self, flat, nm, shape):
        return self._shard_of(flat[self.offs[nm]:self.offs[nm] + self.rows[nm]], shape)

    def slab_rows(self, nm, g):
        (K, N), axis = next((shape, axis) for n, shape, axis in self.group if n == nm)
        cuts = [g[:, k * N:(k + 1) * N] if axis == 1 else g[k * K:(k + 1) * K, :] for k in range(N_CHIPS)]
        return jnp.stack([self._rows_of(c.astype(BF16)) for c in cuts])

    def slabs(self, grads):
        parts = [self.slab_rows(nm, grads[nm]) for nm, _, _ in self.group]
        return jnp.concatenate(parts + [jnp.zeros((N_CHIPS, self.R - self.used, self.C), BF16)], axis=1)

    def full(self, gathered, names=None):
        res = {}
        for nm, (K, N), axis in self.group:
            if names is None or nm in names:
                rows = gathered[:, self.offs[nm]:self.offs[nm] + self.rows[nm]]
                res[nm] = jnp.concatenate([self._shard_of(rows[k], (K, N)) for k in range(N_CHIPS)], axis=axis)
        return res


def _rope_tables(S):
    pos = jnp.arange(S, dtype=F32)
    inv = 1.0 / (ROPE_THETA ** (jnp.arange(0, MLA_ROPE, 2, dtype=F32) / MLA_ROPE))
    ang = pos[:, None] * inv[None, :]
    cos, sin = jnp.cos(ang), jnp.sin(ang)
    half = MLA_ROPE // 2
    z = jnp.zeros((S, half), F32)
    one = jnp.ones((S, LANE - MLA_ROPE), F32)
    zero = jnp.zeros((S, LANE - MLA_ROPE), F32)
    kc = jnp.concatenate([cos, cos, one], axis=1)
    ksa = jnp.concatenate([-sin, z, zero], axis=1)
    ksb = jnp.concatenate([z, sin, zero], axis=1)
    qc = jnp.concatenate([jnp.ones((S, MLA_NOPE), F32), kc], axis=1)
    qsa = jnp.concatenate([jnp.zeros((S, MLA_NOPE), F32), ksa], axis=1)
    qsb = jnp.concatenate([jnp.zeros((S, MLA_NOPE), F32), ksb], axis=1)
    return (kc, ksa, ksb), (qc, qsa, qsb)


def _pad_cols(a, width):
    return jnp.pad(a, ((0, 0), (0, width - a.shape[1])))


def kernel(x, attn_norm, w_in, fox_f_bias, q_norm, w_uq, kv_norm, w_ukv, w_mla_branch, w_fox_branch, w_out, mlp_norm, w_up, w_down, final_norm, loss_target, m_attn_norm, m_w_in, m_fox_f_bias, m_q_norm, m_w_uq, m_kv_norm, m_w_ukv, m_w_mla_branch, m_w_fox_branch, m_w_out, m_mlp_norm, m_w_up, m_w_down, m_final_norm, v_attn_norm, v_w_in, v_fox_f_bias, v_q_norm, v_w_uq, v_kv_norm, v_w_ukv, v_w_mla_branch, v_w_fox_branch, v_w_out, v_mlp_norm, v_w_up, v_w_down, v_final_norm):
    _, S, D = x.shape
    H, HF = MLA_HEADS, FOX_HEADS
    QL, KVL = MLA_Q_LORA, MLA_KV_LORA
    assert H == HF and H <= 8
    xs = x[0]
    target = loss_target[0]
    C = D
    chip = 2 * lax.axis_index("x") + lax.axis_index("y")

    def flip(a):
        return jnp.transpose(a, (0, 2, 1))

    w_in, m_w_in, v_w_in = flip(w_in), flip(m_w_in), flip(v_w_in)
    weights = {"attn_norm": attn_norm, "w_in": w_in, "fox_f_bias": fox_f_bias, "q_norm": q_norm, "w_uq": w_uq,
               "kv_norm": kv_norm, "w_ukv": w_ukv, "w_mla_branch": w_mla_branch, "w_fox_branch": w_fox_branch,
               "w_out": w_out, "mlp_norm": mlp_norm, "w_up": w_up, "w_down": w_down, "final_norm": final_norm}
    moments = {"attn_norm": (m_attn_norm, v_attn_norm), "w_in": (m_w_in, v_w_in), "fox_f_bias": (m_fox_f_bias, v_fox_f_bias),
               "q_norm": (m_q_norm, v_q_norm), "w_uq": (m_w_uq, v_w_uq), "kv_norm": (m_kv_norm, v_kv_norm),
               "w_ukv": (m_w_ukv, v_w_ukv), "w_mla_branch": (m_w_mla_branch, v_w_mla_branch),
               "w_fox_branch": (m_w_fox_branch, v_w_fox_branch), "w_out": (m_w_out, v_w_out),
               "mlp_norm": (m_mlp_norm, v_mlp_norm), "w_up": (m_w_up, v_w_up), "w_down": (m_w_down, v_w_down),
               "final_norm": (m_final_norm, v_final_norm)}

    def group(names_axes):
        return [(nm, weights[nm].shape[1:], axis) for nm, axis in names_axes]

    pack_a = _Pack(group([("w_in", 0), ("w_uq", 1), ("w_ukv", 1)]), C)
    pack_b = _Pack(group([("w_down", 0), ("w_up", 1), ("w_out", 0), ("w_mla_branch", 1), ("w_fox_branch", 1)]), C)
    RA, RB = pack_a.R, pack_b.R
    wp_b = pack_b.pack({nm: weights[nm][0] for nm, _, _ in pack_b.group})
    n_in = w_in.shape[1]
    rows_in = -(-n_in // 16) * 16
    assert pack_a.offs["w_in"] == 0 and w_in.shape[2] == C
    assert all((k * n_in) % 16 + n_in <= rows_in for k in range(N_CHIPS))
    shifted = lax.dynamic_update_slice(jnp.zeros((rows_in, C), BF16), w_in[0].astype(BF16), ((chip * n_in) % 16, 0))
    wp_a = jnp.concatenate([shifted] + [pack_a._rows_of(weights[nm][0].astype(BF16)) for nm, _, _ in pack_a.group[1:]]
                           + [jnp.zeros((RA - pack_a.used, C), BF16)], axis=0)
    ag_a = _xchg_start(wp_a, lax.empty((N_CHIPS, RA, C), BF16), "half", jnp.zeros((8, LANE), F32), "all_gather_start_a")
    xn = _norm_fwd(xs, attn_norm, "attn_norm_fwd", order=ag_a[3])
    own_a, land_a = _xchg_wait(ag_a, "half", (xn, wp_b), "all_gather_wait_a")
    land_a = _forward_halves(land_a, "all_gather_forward_a")
    gathered_a = lax.dynamic_update_slice(land_a, own_a[None], (chip, 0, 0))
    ag_b = _xchg_start(wp_b, lax.empty((N_CHIPS, RB, C), BF16), True, gathered_a, "all_gather_start_b")
    full = pack_a.full(gathered_a, ("w_uq", "w_ukv"))
    tile0 = [(k * n_in) // 16 * 16 for k in range(N_CHIPS)]
    total = tile0[-1] + rows_in
    full["w_in"] = sum(jnp.pad(gathered_a[k, :rows_in], ((tile0[k], total - tile0[k] - rows_in), (0, 0)))
                       for k in range(N_CHIPS))

    o_ckv = QL
    o_kr = o_ckv + KVL
    o_fq = o_kr + MLA_ROPE
    o_ff = o_fq + 3 * HF * FOX_HEAD_DIM
    o_g = o_ff + HF
    wi = full["w_in"]
    assert N_CHIPS * n_in == o_g + 2 * D and wi.shape[0] >= o_g + 2 * D
    WS = QL + KVL + 2 * LANE
    NQKV = 3 * HF * FOX_HEAD_DIM

    def pad_rows(a, rows):
        return jnp.pad(a, ((0, rows - a.shape[0]), (0, 0)))

    w_small = jnp.concatenate([wi[:o_kr], pad_rows(wi[o_kr:o_fq], LANE), pad_rows(wi[o_ff:o_g], LANE)], axis=0)
    w_qkv = wi[o_fq:o_ff]
    w_g = wi[o_g:o_g + 2 * D]
    w_pack = jnp.concatenate([w_small, w_qkv, w_g], axis=0)
    dqk = MLA_NOPE + MLA_ROPE
    w_uq_p = jnp.pad(full["w_uq"].reshape(QL, H, dqk), ((0, 0), (0, 0), (0, QPAD - dqk))).reshape(QL, H * QPAD)
    ukv = full["w_ukv"].reshape(KVL, H, MLA_NOPE + MLA_V)
    w_ukv_p = jnp.concatenate([ukv[:, :, :MLA_NOPE].reshape(KVL, H * MLA_NOPE),
                               ukv[:, :, MLA_NOPE:].reshape(KVL, H * MLA_V)], axis=1)

    (kc, ksa, ksb), (qc, qsa, qsb) = _rope_tables(S)
    bias_pad = _pad_cols(fox_f_bias, LANE)

    small = _matmul(xn, w_small, "nt", [F32], "proj_small")
    n_fq = HF * FOX_HEAD_DIM
    q_scale = jnp.concatenate([jnp.full((1, n_fq), LOG2E / math.sqrt(FOX_HEAD_DIM), F32),
                               jnp.ones((1, NQKV - n_fq), F32)], axis=1)
    qkv = _matmul(xn, w_qkv, "nt", [BF16], "proj_qkv", col_extras=(q_scale,), epilogue=lambda acc, cs: (acc * cs,))
    gpre = _matmul(xn, w_g, "nt", [BF16], "proj_gates", tn=2048)
    cqn, ckvn, kr, cum = _prep_fwd(small, q_norm, kv_norm, bias_pad, kc, ksa, ksb, HF, "prep_fwd")
    c2_mla = LOG2E / math.sqrt(dqk)
    q_rot = _matmul(cqn, w_uq_p, "nn", [BF16], "mla_q_up", tm=2048, tn=QPAD, row_extras=(qc * c2_mla, qsa * c2_mla, qsb * c2_mla),
                    epilogue=lambda acc, c, sa, sb: (_rope(acc, c, sa, sb, 1),))
    kv2 = _matmul(ckvn, w_ukv_p, "nn", [BF16], "mla_kv_up", tm=2048)

    def mla_att(qsub):
        return _AttT(S, H, (q_rot, QPAD, 0, True), [(kv2, MLA_NOPE, 0, True), (kr, LANE, 0, False)],
                     (kv2, MLA_V, H, True), 1.0 / math.sqrt(dqk), True, qsub=qsub)

    mla = mla_att(QSUB)
    o_mla, lse_mla = _att_fwd_t(mla_att(2 * QSUB), "mla_att_fwd")

    cum_t = jnp.transpose(cum[:, :HF]) * LOG2E
    cum_rep = jnp.broadcast_to(cum_t[:, :, None], (HF, S, min(QSUB, _tile(S, ATT_T))))
    fox = _AttT(S, HF, (qkv, FOX_HEAD_DIM, 0, True), [(qkv, FOX_HEAD_DIM, HF, True)],
                (qkv, FOX_HEAD_DIM, 2 * HF, True), 1.0 / math.sqrt(FOX_HEAD_DIM), False, cum_rep)
    o_fox, ox_fox, lse_fox = _att_fwd_t(fox, "fox_att_fwd", exact=True)

    own_b, land_b = _xchg_wait(ag_b, True, (lse_fox, lse_mla, gpre), "all_gather_wait_b")
    gathered_b = lax.dynamic_update_slice(land_b, own_b[None], (chip, 0, 0))
    full.update(pack_b.full(gathered_b, ("w_mla_branch", "w_fox_branch", "w_out")))
    w_mb, w_fb, w_o = (full[n] for n in ("w_mla_branch", "w_fox_branch", "w_out"))

    def b_of(nm, mode, tn, tk):
        (K, N), axis = next((shape, axis) for n, shape, axis in pack_b.group if n == nm)
        off = pack_b.offs[nm]
        shape = (N_CHIPS * K, N) if axis == 0 else (K, N_CHIPS * N)
        t_r, t_c = (tk, tn) if mode == "nn" else (tn, tk)
        t_r, t_c = _tile(shape[0], t_r), _tile(shape[1], t_c)
        if not (N == C and K % t_r == 0 and N % t_c == 0 and off % t_r == 0):
            return pack_b.full(gathered_b, (nm,))[nm], None
        base = off // t_r
        if axis == 0:
            per = K // t_r
            place = lambda rb, cb: (rb // per, base + rb % per, cb)
        else:
            per = N // t_c
            place = lambda rb, cb: (cb // per, base + rb, cb % per)
        return gathered_b, (shape, (lambda j, k: place(k, j)) if mode == "nn" else (lambda j, k: place(j, k)))

    y_mla = _matmul(o_mla, w_mb, "nn", [BF16], "mla_branch", tm=2048)

    def gate_merge(acc, ga, gb, ya):
        return acc, _sigmoid(ga.astype(F32)) * ya.astype(F32) + _sigmoid(gb.astype(F32)) * acc

    y_fox, merged = _matmul(o_fox, w_fb, "nn", [BF16, BF16], "fox_branch_gates",
                            extras=((gpre, 0), (gpre, 1), y_mla), epilogue=gate_merge)
    h1 = _matmul(merged, w_o, "nn", [F32], "out_proj", extras=(xs,), epilogue=lambda acc, r: (acc + r,))
    hn = _norm_fwd(h1, mlp_norm, "mlp_norm_fwd")

    def relu2(acc):
        a = jnp.maximum(acc, 0.0)
        return a * a, a

    w_u, w_u_in = b_of("w_up", "nn", 1024, 2048)
    u, a_pos = _matmul(hn, w_u, "nn", [BF16, BF16], "mlp_up", epilogue=relu2, b_in=w_u_in)
    w_d, w_d_in = b_of("w_down", "nn", 1024, 2048)
    h2 = _matmul(u, w_d, "nn", [F32], "mlp_down", tn=1024, extras=(h1,), epilogue=lambda acc, r: (acc + r,),
                 b_in=w_d_in)
    dh2, dh2_b, g_final, loss_part = _final(h2, final_norm.reshape(1, D), target, "final_norm_loss")

    gp_b = lax.empty((N_CHIPS, RB, C), BF16)
    by_glue = {}

    def grad_b(nm, a, b, name):
        nonlocal gp_b
        (K, N), axis = next((shape, axis) for n, shape, axis in pack_b.group if n == nm)
        off = pack_b.offs[nm]
        tm = min(1024, K) if axis == 0 else min(1024, a.shape[1])
        tn = min(1024, N) if axis == 1 else min(1024, b.shape[1])
        if not (N == C and tm % LANE == 0 and tn % LANE == 0 and K % tm == 0 and N % tn == 0 and off % tm == 0):
            by_glue[nm] = _mm_tn(a, b, name)
            return
        base = off // tm
        if axis == 0:
            per = K // tm
            place = lambda i, j: (i // per, base + i % per, j)
        else:
            per = N // tn
            place = lambda i, j: (j // per, base + i, j % per)
        gp_b = _mm_tn(a, b, name, tm=tm, tn=tn, into=(gp_b, place))

    w_d, w_d_in = b_of("w_down", "nt", 1024, 2048)
    da = _matmul(dh2_b, w_d, "nt", [BF16], "mlp_down_dx", extras=(a_pos,),
                 epilogue=lambda acc, a: (acc * (2.0 * a.astype(F32)),), b_in=w_d_in)
    grad_b("w_down", u, dh2_b, "mlp_down_dw")
    w_u, w_u_in = b_of("w_up", "nt", 1024, 2048)
    dhn = _matmul(da, w_u, "nt", [F32], "mlp_up_dx", tn=1024, b_in=w_u_in)
    grad_b("w_up", hn, da, "mlp_up_dw")
    dh1, dh1_b, g_mlp_norm = _norm_bwd(h1, dhn, mlp_norm, dh2, "mlp_norm_bwd")

    def gate_bwd(acc, ga, gb, ya, yb):
        ga, gb = _sigmoid(ga.astype(F32)), _sigmoid(gb.astype(F32))
        ya, yb = ya.astype(F32), yb.astype(F32)
        return acc * ga, acc * gb, acc * ya * (ga * (1.0 - ga)), acc * yb * (gb * (1.0 - gb))

    dy_mla, dy_fox, dg_mla, dg_fox = _matmul(dh1_b, w_o, "nt", [BF16] * 4, "out_proj_dx_gates", tn=512,
                                             extras=((gpre, 0), (gpre, 1), y_mla, y_fox), epilogue=gate_bwd)
    grad_b("w_out", merged, dh1_b, "out_proj_dw")
    do_mla = _matmul(dy_mla, w_mb, "nt", [BF16], "mla_branch_dx")
    grad_b("w_mla_branch", o_mla, dy_mla, "mla_branch_dw")
    do_fox = _matmul(dy_fox, w_fb, "nt", [BF16], "fox_branch_dx")
    grad_b("w_fox_branch", o_fox, dy_fox, "fox_branch_dw")
    for nm, g in by_glue.items():
        gp_b = lax.dynamic_update_slice(gp_b, pack_b.slab_rows(nm, g), (0, pack_b.offs[nm], 0))
    if RB > pack_b.used:
        gp_b = lax.dynamic_update_slice(gp_b, jnp.zeros((N_CHIPS, RB - pack_b.used, C), BF16), (0, pack_b.used, 0))

    rs_b = _xchg_start(gp_b, lax.empty((3, RB, C), BF16), False, do_fox, "grad_scatter_start_b")

    dq_rot, dk_nope, dkr_heads, dv_mla = _att_bwd_t(mla, do_mla, lse_mla, o_mla, BF16, [BF16, F32],
                                                    "mla_att_bwd", dq_rope=(qc, qsa, qsb), order=rs_b[3])
    dfq, dfk, dfv, dcum = _att_bwd_t(fox, do_fox, lse_fox, ox_fox, BF16, [BF16], "fox_att_bwd")

    gp_b_sent, recv_b = _xchg_wait(rs_b, False, (dfq, dq_rot), "grad_scatter_wait_b")
    swap_b = _sib_start(_sum_slabs(gp_b_sent, recv_b, chip, "grad_sum_b"), "grad_swap_start_b")

    dcqn = _matmul(dq_rot, w_uq_p, "nt", [F32], "mla_q_up_dx", tm=2048, order=swap_b[4])
    g_w_uq_p = _mm_tn(cqn, dq_rot, "mla_q_up_dw")
    dkv2 = jnp.concatenate([dk_nope, dv_mla], axis=1)
    dckvn = _matmul(dkv2, w_ukv_p, "nt", [F32], "mla_kv_up_dx", tm=2048)
    g_w_ukv_p = _mm_tn(ckvn, dkv2, "mla_kv_up_dw")

    dcum_rows = jnp.pad(dcum[:, :, 0], ((0, 8 - HF), (0, 0)))
    dlogf_rows = _suffix_sum_rows(dcum_rows, "fox_forget_suffix_sum")
    dlogf = _pad_cols(jnp.transpose(dlogf_rows[:HF]), LANE)
    d_small, g_q_norm, g_kv_norm, g_bias = _prep_bwd(
        small, dcqn, dckvn, dkr_heads, dlogf, q_norm, kv_norm, bias_pad, kc, ksa, ksb, H, "prep_bwd")
    dproj = [d_small, dfq, dfk, dfv, dg_mla, dg_fox]
    gs, gfq, gfk, gfv, gg_mla, gg_fox = [
        _matmul(part, xn, "tn", [BF16], "proj_dw_" + tag, tm=1024, tn=1024, tk=2048)
        for part, tag in zip(dproj, ("small", "fq", "fk", "fv", "g_mla", "g_fox"))]

    g_w_in = jnp.concatenate([gs[:o_kr], gs[o_kr:o_kr + MLA_ROPE], gfq, gfk, gfv,
                              gs[o_kr + LANE:o_kr + LANE + HF], gg_mla, gg_fox], axis=0)
    g_w_uq = g_w_uq_p.reshape(QL, H, QPAD)[:, :, :dqk].reshape(QL, H * dqk)
    g_w_ukv = jnp.concatenate([g_w_ukv_p[:, :H * MLA_NOPE].reshape(KVL, H, MLA_NOPE),
                               g_w_ukv_p[:, H * MLA_NOPE:].reshape(KVL, H, MLA_V)], axis=2).reshape(KVL, -1)

    gp_a = pack_a.slabs({"w_in": g_w_in, "w_uq": g_w_uq, "w_ukv": g_w_ukv})
    rs_a = _xchg_start(gp_a, lax.empty((3, RA, C), BF16), False, gg_fox, "grad_scatter_start_a")
    dxn = _matmul_row_parts(dproj, w_pack, F32, "proj_dx", order=rs_a[3])
    grad_x, g_attn_norm = _norm_bwd(xs, dxn, attn_norm, dh1, "attn_norm_bwd", with_bf16=False)
    gp_a_sent, recv_a = _xchg_wait(rs_a, False, grad_x, "grad_scatter_wait_a")
    swap_a = _sib_start(_sum_slabs(gp_a_sent, recv_a, chip, "grad_sum_a"), "grad_swap_start_a")
    vec_w = max(D, LANE)
    vec_rows = [g_attn_norm, g_mlp_norm, g_final, g_q_norm, g_kv_norm, g_bias, loss_part]
    vec = jnp.concatenate([_pad_cols(v, vec_w) for v in vec_rows] + [jnp.zeros((1, vec_w), F32)], axis=0)
    vsum = _all_reduce_vec(vec, "all_reduce_vectors")
    part_b, sib_b = _sib_wait(swap_b, vsum, "grad_swap_wait_b")

    grads, deltas, new_m, new_v = {}, {}, {}, {}

    def update(pack, mine, theirs):
        for nm, shape, _ in pack.group:
            K, N = shape
            if N == pack.C and K % 8 == 0 and pack.offs[nm] % _tile(K, 256, 8) == 0:
                g, d, nm_, nv_ = _adamw(weights[nm], mine, theirs, moments[nm][0], moments[nm][1], "adamw_" + nm,
                                        g_row=pack.offs[nm])
            else:
                g, d, nm_, nv_ = _adamw(weights[nm], pack.part(mine, nm, shape), pack.part(theirs, nm, shape),
                                        moments[nm][0], moments[nm][1], "adamw_" + nm)
            grads[nm], deltas[nm], new_m[nm], new_v[nm] = g, d, nm_, nv_
        return g

    last_b = update(pack_b, part_b, sib_b)
    part_a, sib_a = _sib_wait(swap_a, last_b, "grad_swap_wait_a")
    update(pack_a, part_a, sib_a)

    vec_names = ["attn_norm", "mlp_norm", "final_norm", "q_norm", "kv_norm", "fox_f_bias"]

    def vec_pack(arrs):
        return jnp.concatenate([_pad_cols(a.reshape(1, -1), vec_w) for a in arrs]
                               + [jnp.zeros((2, vec_w), F32)], axis=0)[None]

    vg, vd, vm, vv = _adamw(vec_pack([weights[n] for n in vec_names]), vsum, jnp.zeros_like(vsum),
                            vec_pack([moments[n][0] for n in vec_names]), vec_pack([moments[n][1] for n in vec_names]),
                            "adamw_vectors")
    for r, nm in enumerate(vec_names):
        shp = weights[nm].shape
        n = weights[nm].size
        grads[nm] = vsum[r, :n].reshape(shp)
        deltas[nm], new_m[nm], new_v[nm] = (vd[0, r, :n].reshape(shp), vm[0, r, :n].reshape(shp),
                                            vv[0, r, :n].reshape(shp))
    loss = vsum[6, 0]

    for res in (grads, deltas, new_m, new_v):
        res["w_in"] = flip(res["w_in"])
    order = ["attn_norm", "w_in", "fox_f_bias", "q_norm", "w_uq", "kv_norm", "w_ukv", "w_mla_branch", "w_fox_branch",
             "w_out", "mlp_norm", "w_up", "w_down", "final_norm"]
    return (loss, grad_x[None], *[grads[n] for n in order], *[deltas[n] for n in order],
            *[new_m[n] for n in order], *[new_v[n] for n in order])
```
